```python
import math
import jax, jax.numpy as jnp
from jax import lax
import numpy as np

D_MODEL = 1024
BATCH = 8
SEQ = 4096
DEPTH = 1

D_MIX = D_MODEL
ATTN_HEADS = 8
ATTN_HEAD_DIM = 64
ATTN_WIDTH = ATTN_HEADS * ATTN_HEAD_DIM
MLSTM_HEADS = 4
MLSTM_HEAD_DIM = 128
MLSTM_WIDTH = MLSTM_HEADS * MLSTM_HEAD_DIM
DILATED_PATTERNS = ((128, 1), (512, 4), (2048, 16))
ATTN_BLOCK = 128
ROPE_THETA = 500000.0
ROPE_DIM = ATTN_HEAD_DIM // 4
CONV_WIDTH = 4
MLSTM_CHUNK = 64
D_FF = 4 * D_MODEL
PLE_DIM = 256
RMS_EPS = 1e-6
IN_PROJ_WIDTH = 3 * ATTN_WIDTH + 4 * MLSTM_WIDTH + 2 * MLSTM_HEADS

kernel_name = "hymba_dilated_attn_mlstm_block"


def rms_norm(x, g):
    xf = x.astype(jnp.float32)
    y = xf * lax.rsqrt(jnp.mean(xf * xf, axis=-1, keepdims=True) + RMS_EPS)
    return (y * g.astype(jnp.float32)).astype(x.dtype)


def partial_rope(x, pos):
    half = ROPE_DIM // 2
    inv_freq = jnp.power(ROPE_THETA, -jnp.arange(half, dtype=jnp.float32) / half)
    ang = pos.astype(jnp.float32)[:, None] * inv_freq[None, :]
    cos, sin = jnp.cos(ang), jnp.sin(ang)
    xr = x[..., :ROPE_DIM].astype(jnp.float32)
    x1, x2 = xr[..., :half], xr[..., half:]
    rot = jnp.concatenate([x1 * cos - x2 * sin, x2 * cos + x1 * sin], axis=-1)
    return jnp.concatenate([rot.astype(x.dtype), x[..., ROPE_DIM:]], axis=-1)


def dilated_window_partial(q, k, v, window, dilation):
    B, H, S, hd = q.shape
    n_back = window // dilation
    L = S // dilation
    nb = -(-L // ATTN_BLOCK)
    Lp = nb * ATTN_BLOCK
    blk = ATTN_BLOCK

    def to_sub(t):
        t = t.reshape(B, H, L, dilation, hd).transpose(0, 1, 3, 2, 4)
        t = jnp.pad(t, ((0, 0), (0, 0), (0, 0), (0, Lp - L), (0, 0)))
        return t.reshape(B, H, dilation, nb, blk, hd)

    def with_prev(t):
        prev = jnp.pad(t, ((0, 0), (0, 0), (0, 0), (1, 0), (0, 0), (0, 0)))[:, :, :, :-1]
        return jnp.concatenate([prev, t], axis=4)

    qs = to_sub(q)
    kb = with_prev(to_sub(k))
    vb = with_prev(to_sub(v))
    s = jnp.einsum('bhrnqd,bhrnkd->bhrnqk', qs, kb) * (1.0 / math.sqrt(hd))
    qi = jnp.arange(blk)[:, None]
    ki = jnp.arange(2 * blk)[None, :]
    dist = qi + blk - ki
    key_pos = jnp.arange(nb)[:, None, None] * blk + ki[None] - blk
    valid = (dist >= 0) & (dist <= n_back) & (key_pos >= 0)
    s = jnp.where(valid, s, -jnp.inf)
    m = jnp.max(s, axis=-1)
    pexp = jnp.exp(s - m[..., None])
    den = jnp.sum(pexp, axis=-1)
    num = jnp.einsum('bhrnqk,bhrnkd->bhrnqd', pexp, vb)

    def back_vec(t):
        t = t.reshape(B, H, dilation, Lp, hd)[:, :, :, :L]
        return t.transpose(0, 1, 3, 2, 4).reshape(B, H, S, hd)

    def back_scalar(t):
        t = t.reshape(B, H, dilation, Lp)[:, :, :, :L]
        return t.transpose(0, 1, 3, 2).reshape(B, H, S)

    return back_vec(num), back_scalar(m), back_scalar(den)


def dilated_attention(q, k, v):
    parts = [dilated_window_partial(q, k, v, w, d) for (w, d) in DILATED_PATTERNS]
    m_all = jnp.max(jnp.stack([pm for (_, pm, _) in parts]), axis=0)
    num = sum(pn * jnp.exp(pm - m_all)[..., None] for (pn, pm, _) in parts)
    den = sum(pd * jnp.exp(pm - m_all) for (_, pm, pd) in parts)
    return num / den[..., None]


def mlstm_chunkwise(q, k, v, i_pre, f_pre):
    B, H, S, dh = q.shape
    Lc = MLSTM_CHUNK
    nc = S // Lc
    k = k * (1.0 / math.sqrt(dh))
    logf = jax.nn.log_sigmoid(f_pre)

    def chunks(t):
        t = t.reshape(B, H, nc, Lc, *t.shape[3:])
        return jnp.moveaxis(t, 2, 0)

    xs = (chunks(q), chunks(k), chunks(v), chunks(i_pre), chunks(logf))
    causal = jnp.tril(jnp.ones((Lc, Lc), dtype=bool))

    def step(carry, inp):
        C, n, m = carry
        qc, kc, vc, ic, fc = inp
        b = jnp.cumsum(fc, axis=-1)
        log_d = b[..., :, None] - b[..., None, :] + ic[..., None, :]
        log_d = jnp.where(causal, log_d, -jnp.inf)
        log_inter = b + m[..., None]
        m_t = jnp.maximum(log_inter, jnp.max(log_d, axis=-1))
        w_intra = jnp.exp(log_d - m_t[..., None])
        w_inter = jnp.exp(log_inter - m_t)
        qk = jnp.einsum('bhtd,bhsd->bhts', qc, kc) * w_intra
        num = (w_inter[..., None] * jnp.einsum('bhtd,bhde->bhte', qc, C)
               + jnp.einsum('bhts,bhse->bhte', qk, vc))
        den = w_inter * jnp.einsum('bhtd,bhd->bht', qc, n) + jnp.sum(qk, axis=-1)
        h = num / jnp.maximum(jnp.abs(den), jnp.exp(-m_t))[..., None]
        b_last = b[..., -1]
        log_s = b_last[..., None] - b + ic
        m_new = jnp.maximum(b_last + m, jnp.max(log_s, axis=-1))
        decay = jnp.exp(b_last + m - m_new)
        ws = jnp.exp(log_s - m_new[..., None])
        C_new = decay[..., None, None] * C + jnp.einsum('bhs,bhsd,bhse->bhde', ws, kc, vc)
        n_new = decay[..., None] * n + jnp.einsum('bhs,bhsd->bhd', ws, kc)
        return (C_new, n_new, m_new), h

    init = (jnp.zeros((B, H, dh, dh), jnp.float32),
            jnp.zeros((B, H, dh), jnp.float32),
            jnp.zeros((B, H), jnp.float32))
    _, hs = lax.scan(step, init, xs)
    return jnp.moveaxis(hs, 0, 2).reshape(B, H, S, dh)


def causal_short_conv(x, w, b):
    S = x.shape[1]
    xp = jnp.pad(x, ((0, 0), (CONV_WIDTH - 1, 0), (0, 0)))
    return sum(w[j] * xp[:, j:j + S] for j in range(CONV_WIDTH)) + b


def split_heads(t, n_heads, hd):
    B, S, _ = t.shape
    return t.reshape(B, S, n_heads, hd).transpose(0, 2, 1, 3)


def merge_heads(t):
    B, H, S, hd = t.shape
    return t.transpose(0, 2, 1, 3).reshape(B, S, H * hd)


def _fwd_setup_inputs(seed: int = 0) -> dict:
    key = jax.random.key(seed)
    ks = jax.random.split(key, 20)
    f32 = jnp.float32

    def nrm(k, shape, scale):
        return jax.random.normal(k, shape, f32) * scale

    x = jax.random.normal(ks[0], (BATCH, SEQ, D_MODEL), f32)
    p = jax.random.normal(ks[1], (DEPTH, BATCH, SEQ, PLE_DIM), f32)
    norm_mix_g = 1.0 + nrm(ks[2], (DEPTH, D_MODEL), 0.02)
    w_in = nrm(ks[3], (DEPTH, D_MODEL, IN_PROJ_WIDTH), D_MODEL ** -0.5)
    conv_w = nrm(ks[4], (DEPTH, CONV_WIDTH, 2 * MLSTM_WIDTH), CONV_WIDTH ** -0.5)
    conv_b = nrm(ks[5], (DEPTH, 2 * MLSTM_WIDTH), 0.01)
    i_bias = nrm(ks[6], (DEPTH, MLSTM_HEADS), 0.1)
    f_bias = (jnp.linspace(3.0, 6.0, MLSTM_HEADS, dtype=f32)[None, :]
              + nrm(ks[7], (DEPTH, MLSTM_HEADS), 0.1))
    gate_b = jnp.concatenate([i_bias, f_bias], axis=-1)
    mlstm_norm_g = 1.0 + nrm(ks[8], (DEPTH, MLSTM_WIDTH), 0.02)
    w_out = nrm(ks[9], (DEPTH, D_MIX, D_MODEL), D_MIX ** -0.5)
    norm_mlp_g = 1.0 + nrm(ks[10], (DEPTH, D_MODEL), 0.02)
    w_up = nrm(ks[11], (DEPTH, D_MODEL, D_FF), D_MODEL ** -0.5)
    w_down = nrm(ks[12], (DEPTH, D_FF, D_MODEL), D_FF ** -0.5)
    norm_ple_g = 1.0 + nrm(ks[13], (DEPTH, D_MODEL), 0.02)
    w_ple_gate = nrm(ks[14], (DEPTH, D_MODEL, D_MODEL), D_MODEL ** -0.5)
    w_ple = nrm(ks[15], (DEPTH, PLE_DIM, D_MODEL), PLE_DIM ** -0.5)
    final_norm_g = 1.0 + nrm(ks[16], (D_MODEL,), 0.02)
    return {"x": x, "p": p, "norm_mix_g": norm_mix_g, "w_in": w_in,
            "conv_w": conv_w, "conv_b": conv_b, "gate_b": gate_b,
            "mlstm_norm_g": mlstm_norm_g, "w_out": w_out, "norm_mlp_g": norm_mlp_g,
            "w_up": w_up, "w_down": w_down, "norm_ple_g": norm_ple_g,
            "w_ple_gate": w_ple_gate, "w_ple": w_ple, "final_norm_g": final_norm_g}


def _fwd_reference(x, p, norm_mix_g, w_in, conv_w, conv_b, gate_b, mlstm_norm_g, w_out,
              norm_mlp_g, w_up, w_down, norm_ple_g, w_ple_gate, w_ple, final_norm_g):
    B, S, _ = x.shape
    pos = jnp.arange(S, dtype=jnp.int32)
    A, M = ATTN_WIDTH, MLSTM_WIDTH
    h = x
    for layer in range(DEPTH):
        u = rms_norm(h, norm_mix_g[layer])
        proj = u @ w_in[layer]
        aq = proj[..., 0:A]
        ak = proj[..., A:2 * A]
        av = proj[..., 2 * A:3 * A]
        o0 = 3 * A
        mqk = proj[..., o0:o0 + 2 * M]
        mv = proj[..., o0 + 2 * M:o0 + 3 * M]
        mo = proj[..., o0 + 3 * M:o0 + 4 * M]
        gates = (proj[..., o0 + 4 * M:] + gate_b[layer]).astype(jnp.float32)

        qa = partial_rope(split_heads(aq, ATTN_HEADS, ATTN_HEAD_DIM), pos).astype(jnp.float32)
        ka = partial_rope(split_heads(ak, ATTN_HEADS, ATTN_HEAD_DIM), pos).astype(jnp.float32)
        va = split_heads(av, ATTN_HEADS, ATTN_HEAD_DIM).astype(jnp.float32)
        attn_out = merge_heads(dilated_attention(qa, ka, va)).astype(h.dtype)

        qk_c = jax.nn.silu(causal_short_conv(mqk, conv_w[layer], conv_b[layer]))
        qm = split_heads(qk_c[..., :M], MLSTM_HEADS, MLSTM_HEAD_DIM).astype(jnp.float32)
        km = split_heads(qk_c[..., M:], MLSTM_HEADS, MLSTM_HEAD_DIM).astype(jnp.float32)
        vm = split_heads(mv, MLSTM_HEADS, MLSTM_HEAD_DIM).astype(jnp.float32)
        i_pre = gates[..., :MLSTM_HEADS].transpose(0, 2, 1)
        f_pre = gates[..., MLSTM_HEADS:].transpose(0, 2, 1)
        hm = mlstm_chunkwise(qm, km, vm, i_pre, f_pre)
        hm = hm * lax.rsqrt(jnp.mean(hm * hm, axis=-1, keepdims=True) + RMS_EPS)
        hm = hm * mlstm_norm_g[layer].astype(jnp.float32).reshape(MLSTM_HEADS, 1, MLSTM_HEAD_DIM)
        mlstm_out = (jax.nn.sigmoid(mo.astype(jnp.float32)) * merge_heads(hm)).astype(h.dtype)

        mix = jnp.concatenate([attn_out, mlstm_out], axis=-1)
        h = h + mix @ w_out[layer]

        u = rms_norm(h, norm_mlp_g[layer])
        h = h + jnp.square(jax.nn.relu(u @ w_up[layer])) @ w_down[layer]

        gate = jax.nn.sigmoid(rms_norm(h, norm_ple_g[layer]) @ w_ple_gate[layer])
        h = h + gate * (p[layer] @ w_ple[layer])
    return rms_norm(h, final_norm_g)


import jax as _jax
import jax.numpy as _jnp

TWIN_FORMAT = 'train_step'
FWD_PARAMS = ['x', 'p', 'norm_mix_g', 'w_in', 'conv_w', 'conv_b', 'gate_b', 'mlstm_norm_g', 'w_out', 'norm_mlp_g', 'w_up', 'w_down', 'norm_ple_g', 'w_ple_gate', 'w_ple', 'final_norm_g']
TWIN_WEIGHTS = ['norm_mix_g', 'w_in', 'conv_w', 'conv_b', 'gate_b', 'mlstm_norm_g', 'w_out', 'norm_mlp_g', 'w_up', 'w_down', 'norm_ple_g', 'w_ple_gate', 'w_ple', 'final_norm_g']
TWIN_DIFF_INPUT = 'x'
TWIN_INPUTS = ['x', 'p', 'norm_mix_g', 'w_in', 'conv_w', 'conv_b', 'gate_b', 'mlstm_norm_g', 'w_out', 'norm_mlp_g', 'w_up', 'w_down', 'norm_ple_g', 'w_ple_gate', 'w_ple', 'final_norm_g', 'loss_target', 'm_norm_mix_g', 'm_w_in', 'm_conv_w', 'm_conv_b', 'm_gate_b', 'm_mlstm_norm_g', 'm_w_out', 'm_norm_mlp_g', 'm_w_up', 'm_w_down', 'm_norm_ple_g', 'm_w_ple_gate', 'm_w_ple', 'm_final_norm_g', 'v_norm_mix_g', 'v_w_in', 'v_conv_w', 'v_conv_b', 'v_gate_b', 'v_mlstm_norm_g', 'v_w_out', 'v_norm_mlp_g', 'v_w_up', 'v_w_down', 'v_norm_ple_g', 'v_w_ple_gate', 'v_w_ple', 'v_final_norm_g']
TWIN_OUTPUTS = ['loss', 'grad_x', 'grad_norm_mix_g', 'grad_w_in', 'grad_conv_w', 'grad_conv_b', 'grad_gate_b', 'grad_mlstm_norm_g', 'grad_w_out', 'grad_norm_mlp_g', 'grad_w_up', 'grad_w_down', 'grad_norm_ple_g', 'grad_w_ple_gate', 'grad_w_ple', 'grad_final_norm_g', 'delta_norm_mix_g', 'delta_w_in', 'delta_conv_w', 'delta_conv_b', 'delta_gate_b', 'delta_mlstm_norm_g', 'delta_w_out', 'delta_norm_mlp_g', 'delta_w_up', 'delta_w_down', 'delta_norm_ple_g', 'delta_w_ple_gate', 'delta_w_ple', 'delta_final_norm_g', 'new_m_norm_mix_g', 'new_m_w_in', 'new_m_conv_w', 'new_m_conv_b', 'new_m_gate_b', 'new_m_mlstm_norm_g', 'new_m_w_out', 'new_m_norm_mlp_g', 'new_m_w_up', 'new_m_w_down', 'new_m_norm_ple_g', 'new_m_w_ple_gate', 'new_m_w_ple', 'new_m_final_norm_g', 'new_v_norm_mix_g', 'new_v_w_in', 'new_v_conv_w', 'new_v_conv_b', 'new_v_gate_b', 'new_v_mlstm_norm_g', 'new_v_w_out', 'new_v_norm_mlp_g', 'new_v_w_up', 'new_v_w_down', 'new_v_norm_ple_g', 'new_v_w_ple_gate', 'new_v_w_ple', 'new_v_final_norm_g']
TWIN_LEAF_KINDS = {'loss': 'loss', 'grad_x': 'grad_x', 'grad_norm_mix_g': 'grad_w', 'grad_w_in': 'grad_w', 'grad_conv_w': 'grad_w', 'grad_conv_b': 'grad_w', 'grad_gate_b': 'grad_w', 'grad_mlstm_norm_g': 'grad_w', 'grad_w_out': 'grad_w', 'grad_norm_mlp_g': 'grad_w', 'grad_w_up': 'grad_w', 'grad_w_down': 'grad_w', 'grad_norm_ple_g': 'grad_w', 'grad_w_ple_gate': 'grad_w', 'grad_w_ple': 'grad_w', 'grad_final_norm_g': 'grad_w', 'delta_norm_mix_g': 'delta_w', 'delta_w_in': 'delta_w', 'delta_conv_w': 'delta_w', 'delta_conv_b': 'delta_w', 'delta_gate_b': 'delta_w', 'delta_mlstm_norm_g': 'delta_w', 'delta_w_out': 'delta_w', 'delta_norm_mlp_g': 'delta_w', 'delta_w_up': 'delta_w', 'delta_w_down': 'delta_w', 'delta_norm_ple_g': 'delta_w', 'delta_w_ple_gate': 'delta_w', 'delta_w_ple': 'delta_w', 'delta_final_norm_g': 'delta_w', 'new_m_norm_mix_g': 'new_m', 'new_m_w_in': 'new_m', 'new_m_conv_w': 'new_m', 'new_m_conv_b': 'new_m', 'new_m_gate_b': 'new_m', 'new_m_mlstm_norm_g': 'new_m', 'new_m_w_out': 'new_m', 'new_m_norm_mlp_g': 'new_m', 'new_m_w_up': 'new_m', 'new_m_w_down': 'new_m', 'new_m_norm_ple_g': 'new_m', 'new_m_w_ple_gate': 'new_m', 'new_m_w_ple': 'new_m', 'new_m_final_norm_g': 'new_m', 'new_v_norm_mix_g': 'new_v', 'new_v_w_in': 'new_v', 'new_v_conv_w': 'new_v', 'new_v_conv_b': 'new_v', 'new_v_gate_b': 'new_v', 'new_v_mlstm_norm_g': 'new_v', 'new_v_w_out': 'new_v', 'new_v_norm_mlp_g': 'new_v', 'new_v_w_up': 'new_v', 'new_v_w_down': 'new_v', 'new_v_norm_ple_g': 'new_v', 'new_v_w_ple_gate': 'new_v', 'new_v_w_ple': 'new_v', 'new_v_final_norm_g': 'new_v'}


def _forward(args):
    return _fwd_reference(*[args[k] for k in FWD_PARAMS])


def _output_shape():
    out = _jax.eval_shape(lambda: _forward(_fwd_setup_inputs(0)))
    return out.shape, out.dtype

N_MICROBATCH = 1
ADAM_LR = 0.001
ADAM_B1 = 0.9
ADAM_B2 = 0.999
ADAM_EPS = 1e-08
ADAM_WD = 0.01
ADAM_STEP = 10
PER_EXAMPLE_BATCH_AXIS = {'x': 0, 'p': 1, 'loss_target': 0}
SHARED_INPUTS = []
_WEIGHT_DTYPES = {'norm_mix_g': _jnp.float32, 'w_in': _jnp.float32, 'conv_w': _jnp.float32, 'conv_b': _jnp.float32, 'gate_b': _jnp.float32, 'mlstm_norm_g': _jnp.float32, 'w_out': _jnp.float32, 'norm_mlp_g': _jnp.float32, 'w_up': _jnp.float32, 'w_down': _jnp.float32, 'norm_ple_g': _jnp.float32, 'w_ple_gate': _jnp.float32, 'w_ple': _jnp.float32, 'final_norm_g': _jnp.float32}
MOMENT_SCALE = {'norm_mix_g': 1.181256e-01, 'w_in': 6.295131e-02, 'conv_w': 6.545186e-02, 'conv_b': 6.021625e-02, 'gate_b': 1.017340e+00, 'mlstm_norm_g': 9.514367e-02, 'w_out': 7.095229e-02, 'norm_mlp_g': 1.782559e-01, 'w_up': 7.389483e-02, 'w_down': 1.405469e-01, 'norm_ple_g': 2.467292e-02, 'w_ple_gate': 2.314575e-02, 'w_ple': 5.617664e-02, 'final_norm_g': 3.217988e+01}


def _to_microbatches(a, axis):
    t = _jnp.moveaxis(a, axis, 0)
    t = t.reshape((N_MICROBATCH, t.shape[0] // N_MICROBATCH) + t.shape[1:])
    return _jnp.moveaxis(t, 1, axis + 1)


def setup_inputs(seed: int = 0) -> dict:
    inp = _fwd_setup_inputs(seed)
    key = _jax.random.fold_in(_jax.random.key(seed), 7919)
    shape, _ = _output_shape()
    out = dict(inp)
    out["loss_target"] = _jax.random.normal(_jax.random.fold_in(key, 0), shape, _jnp.float32)
    for i, name in enumerate(TWIN_WEIGHTS):
        w = inp[name].astype(_jnp.float32)
        if MOMENT_SCALE is None:
            s = _jnp.sqrt(_jnp.mean(_jnp.square(w)) + 1e-30)
        else:
            s = MOMENT_SCALE[name]
        km, kv = _jax.random.split(_jax.random.fold_in(key, i + 1))
        out[name] = w
        out["m_" + name] = s * _jax.random.normal(km, w.shape, _jnp.float32)
        out["v_" + name] = (s * s) * _jax.random.uniform(kv, w.shape, _jnp.float32, 0.5, 1.5)
    if N_MICROBATCH > 1:
        for name, axis in PER_EXAMPLE_BATCH_AXIS.items():
            out[name] = _to_microbatches(out[name], axis)
    return {'x': out['x'], 'p': out['p'], 'norm_mix_g': out['norm_mix_g'], 'w_in': out['w_in'], 'conv_w': out['conv_w'], 'conv_b': out['conv_b'], 'gate_b': out['gate_b'], 'mlstm_norm_g': out['mlstm_norm_g'], 'w_out': out['w_out'], 'norm_mlp_g': out['norm_mlp_g'], 'w_up': out['w_up'], 'w_down': out['w_down'], 'norm_ple_g': out['norm_ple_g'], 'w_ple_gate': out['w_ple_gate'], 'w_ple': out['w_ple'], 'final_norm_g': out['final_norm_g'], 'loss_target': out['loss_target'], 'm_norm_mix_g': out['m_norm_mix_g'], 'm_w_in': out['m_w_in'], 'm_conv_w': out['m_conv_w'], 'm_conv_b': out['m_conv_b'], 'm_gate_b': out['m_gate_b'], 'm_mlstm_norm_g': out['m_mlstm_norm_g'], 'm_w_out': out['m_w_out'], 'm_norm_mlp_g': out['m_norm_mlp_g'], 'm_w_up': out['m_w_up'], 'm_w_down': out['m_w_down'], 'm_norm_ple_g': out['m_norm_ple_g'], 'm_w_ple_gate': out['m_w_ple_gate'], 'm_w_ple': out['m_w_ple'], 'm_final_norm_g': out['m_final_norm_g'], 'v_norm_mix_g': out['v_norm_mix_g'], 'v_w_in': out['v_w_in'], 'v_conv_w': out['v_conv_w'], 'v_conv_b': out['v_conv_b'], 'v_gate_b': out['v_gate_b'], 'v_mlstm_norm_g': out['v_mlstm_norm_g'], 'v_w_out': out['v_w_out'], 'v_norm_mlp_g': out['v_norm_mlp_g'], 'v_w_up': out['v_w_up'], 'v_w_down': out['v_w_down'], 'v_norm_ple_g': out['v_norm_ple_g'], 'v_w_ple_gate': out['v_w_ple_gate'], 'v_w_ple': out['v_w_ple'], 'v_final_norm_g': out['v_final_norm_g']}


def _loss(weights, diff, rest, loss_target):
    with _jax.named_scope("forward"):
        args = {**rest, TWIN_DIFF_INPUT: diff, **{k: w.astype(_WEIGHT_DTYPES[k]) for k, w in weights.items()}}
        y = _forward(args)
    with _jax.named_scope("loss_head"):
        err = _jnp.square(y.astype(_jnp.float32) - loss_target)
        return 0.5 * _jnp.sum(_jnp.mean(err, axis=-1)) if err.ndim else 0.5 * err


def _adamw(w, g, m, v):
    m = ADAM_B1 * m + (1.0 - ADAM_B1) * g
    v = ADAM_B2 * v + (1.0 - ADAM_B2) * _jnp.square(g)
    m_hat = m / (1.0 - ADAM_B1 ** ADAM_STEP)
    v_hat = v / (1.0 - ADAM_B2 ** ADAM_STEP)
    delta = -ADAM_LR * (m_hat / (_jnp.sqrt(v_hat) + ADAM_EPS) + ADAM_WD * w)
    return delta, m, v


def reference(x, p, norm_mix_g, w_in, conv_w, conv_b, gate_b, mlstm_norm_g, w_out, norm_mlp_g, w_up, w_down, norm_ple_g, w_ple_gate, w_ple, final_norm_g, loss_target, m_norm_mix_g, m_w_in, m_conv_w, m_conv_b, m_gate_b, m_mlstm_norm_g, m_w_out, m_norm_mlp_g, m_w_up, m_w_down, m_norm_ple_g, m_w_ple_gate, m_w_ple, m_final_norm_g, v_norm_mix_g, v_w_in, v_conv_w, v_conv_b, v_gate_b, v_mlstm_norm_g, v_w_out, v_norm_mlp_g, v_w_up, v_w_down, v_norm_ple_g, v_w_ple_gate, v_w_ple, v_final_norm_g):
    given = dict(x=x, p=p, norm_mix_g=norm_mix_g, w_in=w_in, conv_w=conv_w, conv_b=conv_b, gate_b=gate_b, mlstm_norm_g=mlstm_norm_g, w_out=w_out, norm_mlp_g=norm_mlp_g, w_up=w_up, w_down=w_down, norm_ple_g=norm_ple_g, w_ple_gate=w_ple_gate, w_ple=w_ple, final_norm_g=final_norm_g, loss_target=loss_target, m_norm_mix_g=m_norm_mix_g, m_w_in=m_w_in, m_conv_w=m_conv_w, m_conv_b=m_conv_b, m_gate_b=m_gate_b, m_mlstm_norm_g=m_mlstm_norm_g, m_w_out=m_w_out, m_norm_mlp_g=m_norm_mlp_g, m_w_up=m_w_up, m_w_down=m_w_down, m_norm_ple_g=m_norm_ple_g, m_w_ple_gate=m_w_ple_gate, m_w_ple=m_w_ple, m_final_norm_g=m_final_norm_g, v_norm_mix_g=v_norm_mix_g, v_w_in=v_w_in, v_conv_w=v_conv_w, v_conv_b=v_conv_b, v_gate_b=v_gate_b, v_mlstm_norm_g=v_mlstm_norm_g, v_w_out=v_w_out, v_norm_mlp_g=v_norm_mlp_g, v_w_up=v_w_up, v_w_down=v_w_down, v_norm_ple_g=v_norm_ple_g, v_w_ple_gate=v_w_ple_gate, v_w_ple=v_w_ple, v_final_norm_g=v_final_norm_g)
    weights = {n: given[n] for n in TWIN_WEIGHTS}
    shared = {n: given[n] for n in SHARED_INPUTS}
    per_example = {n: given[n] for n in ['x', 'p']}
    grad_fn = _jax.value_and_grad(_loss, argnums=(0, 1))

    def one_microbatch(ex, loss_target):
        ex = dict(ex)
        diff = ex.pop(TWIN_DIFF_INPUT)
        return grad_fn(weights, diff, {**shared, **ex}, loss_target)

    if N_MICROBATCH == 1:
        loss, (grad_w, grad_x) = one_microbatch(per_example, given["loss_target"])
    else:
        def body(carry, xs):
            loss_sum, grad_sum = carry
            l_k, (gw_k, gx_k) = one_microbatch(xs[0], xs[1])
            with _jax.named_scope("update"):
                return (loss_sum + l_k, _jax.tree.map(_jnp.add, grad_sum, gw_k)), gx_k

        init = (_jnp.zeros((), _jnp.float32), _jax.tree.map(_jnp.zeros_like, weights))
        (loss, grad_w), grad_x = _jax.lax.scan(body, init, (per_example, given["loss_target"]))
    with _jax.named_scope("update"):
        delta_w, new_m, new_v = {}, {}, {}
        for n in TWIN_WEIGHTS:
            delta_w[n], new_m[n], new_v[n] = _adamw(weights[n], grad_w[n], given["m_" + n], given["v_" + n])
    return (loss, grad_x, *[grad_w[n] for n in TWIN_WEIGHTS], *[delta_w[n] for n in TWIN_WEIGHTS],
            *[new_m[n] for n in TWIN_WEIGHTS], *[new_v[n] for n in TWIN_WEIGHTS])
```

```python
import functools
import math

import jax
import jax.numpy as jnp
from jax import lax
from jax.experimental import pallas as pl
from jax.experimental.pallas import tpu as pltpu

F32, BF16 = jnp.float32, jnp.bfloat16
S = 4096
D = 1024
AW = 512
MW = 512
DFF = 4096
PLE = 256
IN_W = 3592
PW = 3840
NDEV = 8
EPS = 1e-6
NEG = -1e30
LC = 64
TB = 256
ROPE_THETA = 500000.0
VMEM_LIMIT = 56 * 1024 * 1024
HI = lax.Precision.HIGHEST

ADAM_LR, ADAM_B1, ADAM_B2, ADAM_EPS, ADAM_WD, ADAM_STEP = 0.001, 0.9, 0.999, 1e-08, 0.01, 10


def _params(n_grid=0, **kw):
    sem = dict(dimension_semantics=("arbitrary",) * n_grid) if n_grid else {}
    return pltpu.CompilerParams(vmem_limit_bytes=VMEM_LIMIT, **sem, **kw)


def _cspec(shape):
    nd = len(shape)
    return pl.BlockSpec(shape, lambda *_: (0,) * nd, pipeline_mode=pl.Buffered(1))


def _dot(a, b):
    return jnp.dot(a, b, preferred_element_type=F32)


def _dot_nt(a, b):
    return lax.dot_general(a, b, (((1,), (1,)), ((), ())), preferred_element_type=F32)


def _dot_tn(a, b):
    return lax.dot_general(a, b, (((0,), (0,)), ((), ())), preferred_element_type=F32)


def _bf(x):
    return x.astype(BF16)


def _rms(x):
    rs = lax.rsqrt(jnp.mean(x * x, axis=-1, keepdims=True) + EPS)
    return x * rs, rs


def _rms_bwd(du, n, rs, g):
    dn = du * g
    return rs * (dn - n * jnp.mean(dn * n, axis=-1, keepdims=True))


def _sigmoid(x):
    return 1.0 / (1.0 + jnp.exp(-x))


def _rope_tables():
    half = 8
    inv_freq = jnp.power(ROPE_THETA, -jnp.arange(half, dtype=F32) / half)
    ang = jnp.arange(S, dtype=jnp.int32).astype(F32)[:, None] * inv_freq[None, :]
    cos, sin = jnp.cos(ang), jnp.sin(ang)
    z = jnp.zeros((S, 64 - 16), F32)
    c64 = jnp.concatenate([cos, cos, jnp.ones((S, 48), F32)], axis=1)
    a64 = jnp.concatenate([-sin, jnp.zeros((S, 8), F32), z], axis=1)
    b64 = jnp.concatenate([jnp.zeros((S, 8), F32), sin, z], axis=1)
    return tuple(jnp.concatenate([t, t], axis=1) for t in (c64, a64, b64))


def _rope(blk, c, a, b):
    return blk * c + pltpu.roll(blk, 120, 1) * a + pltpu.roll(blk, 8, 1) * b


def _rope_bwd(d, c, a, b):
    return d * c + pltpu.roll(d * a, 8, 1) + pltpu.roll(d * b, 120, 1)


def _in_proj(x, g1, w, rc, ra, rb):
    tm = 256

    def body(x_ref, g_ref, w_ref, rc_ref, ra_ref, rb_ref, qkv_ref, mqk_ref, mv_ref, mo_ref, gt_ref, u_ref):
        n, _ = _rms(x_ref[...])
        u = _bf(n * g_ref[...])
        u_ref[...] = u
        c, a, b = rc_ref[...], ra_ref[...], rb_ref[...]
        for half in range(2):
            blk = _dot(u, w_ref[:, half * 512:(half + 1) * 512])
            for t in range(4):
                lo = half * 512 + t * 128
                qkv_ref[:, lo:lo + 128] = _rope(blk[:, t * 128:(t + 1) * 128], c, a, b)
        qkv_ref[:, 1024:1536] = _dot(u, w_ref[:, 1024:1536])
        mqk_ref[:, 0:512] = _dot(u, w_ref[:, 1536:2048])
        mqk_ref[:, 512:1024] = _dot(u, w_ref[:, 2048:2560])
        mv_ref[...] = _dot(u, w_ref[:, 2560:3072])
        mo_ref[...] = _dot(u, w_ref[:, 3072:3584])
        gt_ref[...] = _dot(u, w_ref[:, 3584:3712])

    row = lambda wd: pl.BlockSpec((tm, wd), lambda i: (i, 0))
    return pl.pallas_call(
        body, name="in_proj", grid=(S // tm,),
        in_specs=[row(D), _cspec((1, D)), _cspec((D, PW)), row(128), row(128), row(128)],
        out_specs=[row(1536), row(1024), row(512), row(512), row(128), row(D)],
        out_shape=[jax.ShapeDtypeStruct((S, 1536), F32), jax.ShapeDtypeStruct((S, 1024), F32),
                   jax.ShapeDtypeStruct((S, 512), F32), jax.ShapeDtypeStruct((S, 512), F32),
                   jax.ShapeDtypeStruct((S, 128), F32), jax.ShapeDtypeStruct((S, D), BF16)],
        compiler_params=_params(1),
    )(x, g1, w, rc, ra, rb)


DILATIONS = (1, 4, 16)


def _attn_rows(d, idx):
    nb = S // (128 * d)
    r = idx // nb
    n = idx % nb
    if d == 1:
        q0 = pl.multiple_of(n * 128, 128)
        k0 = pl.multiple_of(jnp.maximum(n - 1, 0) * 128, 128)
        return pl.ds(q0, 128), pl.ds(k0, 256), n
    q0 = r + n * 128 * d
    k0 = r + jnp.maximum(n - 1, 0) * 128 * d
    return pl.ds(q0, 128, stride=d), pl.ds(k0, 256, stride=d), n


def _attn_masks():
    qi = lax.broadcasted_iota(jnp.int32, (128, 256), 0)
    ki = lax.broadcasted_iota(jnp.int32, (128, 256), 1)
    band = (ki - qi >= 0) & (ki - qi <= 128)
    q1 = lax.broadcasted_iota(jnp.int32, (128, 128), 0)
    k1 = lax.broadcasted_iota(jnp.int32, (128, 128), 1)
    causal = k1 <= q1
    head0 = k1 < 64
    return band, causal, head0


def _attn_fwd(qkv):
    def body(q_ref, k_ref, v_ref, o_ref, lse_ref, m0, m1, l0, l1, acc):
        band, causal, head0 = _attn_masks()
        for ref in (m0, m1):
            ref[...] = jnp.full((S, 128), NEG, F32)
        for ref in (l0, l1, acc):
            ref[...] = jnp.zeros((S, 128), F32)

        def update(rows_q, rows_k, valid):
            q = q_ref[rows_q, :]
            kb = _bf(k_ref[rows_k, :])
            vb = _bf(v_ref[rows_k, :])
            rep = kb.shape[0] // 128
            acc_prev = acc[rows_q, :]
            new = []
            for qa, m_ref, l_ref in ((_bf(jnp.where(head0, q, 0.0)), m0, l0), (_bf(jnp.where(head0, 0.0, q)), m1, l1)):
                s = jnp.where(valid, _dot_nt(qa, kb) * 0.125, NEG)
                m_prev = m_ref[rows_q, :]
                m_new = jnp.maximum(m_prev, jnp.max(s, axis=-1, keepdims=True))
                p = jnp.exp(s - jnp.tile(m_new, (1, rep)))
                alpha = jnp.exp(m_prev - m_new)
                l_ref[rows_q, :] = alpha * l_ref[rows_q, :] + jnp.sum(p, axis=-1, keepdims=True)
                m_ref[rows_q, :] = m_new
                new.append(alpha * acc_prev + _dot(_bf(p), vb))
            acc[rows_q, :] = jnp.where(head0, new[0], new[1])

        for d in DILATIONS:
            def step(idx, carry, d=d):
                rows_q, rows_k, n = _attn_rows(d, idx)

                @pl.when(n == 0)
                def _():
                    update(rows_q, rows_q, causal)

                @pl.when(n > 0)
                def _():
                    update(rows_q, rows_k, band)
                return carry

            lax.fori_loop(0, 32, step, 0)

        def fin(t, carry):
            rows = pl.ds(pl.multiple_of(t * 256, 256), 256)
            h0 = lax.broadcasted_iota(jnp.int32, (256, 128), 1) < 64
            l = jnp.where(h0, l0[rows, :], l1[rows, :])
            o_ref[rows, :] = acc[rows, :] / l
            lse_ref[rows, :] = jnp.where(h0, m0[rows, :], m1[rows, :]) + jnp.log(l)
            return carry

        lax.fori_loop(0, S // 256, fin, 0)

    col = lambda off: pl.BlockSpec((S, 128), lambda h, off=off: (0, off + h))
    return pl.pallas_call(
        body, name="attn_fwd", grid=(4,),
        in_specs=[col(0), col(4), col(8)],
        out_specs=[col(0), col(0)],
        out_shape=[jax.ShapeDtypeStruct((S, AW), F32), jax.ShapeDtypeStruct((S, AW), F32)],
        scratch_shapes=[pltpu.VMEM((S, 128), F32)] * 5,
        compiler_params=_params(1),
    )(qkv, qkv, qkv)


def _attn_bwd(qkv, o, lse, do):
    def body(q_ref, k_ref, v_ref, o_ref, lse_ref, do_ref, dq_ref, dk_ref, dv_ref, L0, L1, D0, D1):
        band, causal, head0 = _attn_masks()

        def pre(t, carry):
            rows = pl.ds(pl.multiple_of(t * 256, 256), 256)
            h0 = lax.broadcasted_iota(jnp.int32, (256, 128), 1) < 64
            ls = lse_ref[rows, :]
            dd = do_ref[rows, :] * o_ref[rows, :]
            shp = (256, 128)
            L0[rows, :] = jnp.broadcast_to(jnp.max(jnp.where(h0, ls, NEG), axis=-1, keepdims=True), shp)
            L1[rows, :] = jnp.broadcast_to(jnp.max(jnp.where(h0, NEG, ls), axis=-1, keepdims=True), shp)
            D0[rows, :] = jnp.broadcast_to(jnp.sum(jnp.where(h0, dd, 0.0), axis=-1, keepdims=True), shp)
            D1[rows, :] = jnp.broadcast_to(jnp.sum(jnp.where(h0, 0.0, dd), axis=-1, keepdims=True), shp)
            return carry

        lax.fori_loop(0, S // 256, pre, 0)
        for ref in (dq_ref, dk_ref, dv_ref):
            ref[...] = jnp.zeros((S, 128), F32)

        def update(rows_q, rows_k, valid):
            q = q_ref[rows_q, :]
            k = k_ref[rows_k, :]
            dout = do_ref[rows_q, :]
            kb = _bf(k)
            vb = _bf(v_ref[rows_k, :])
            rep = kb.shape[0] // 128
            dq = dk = dv = None
            for a, (L_ref, D_ref) in enumerate(((L0, D0), (L1, D1))):
                lo = lambda t: lax.broadcasted_iota(jnp.int32, t.shape, 1) < 64
                pick = (lambda t: jnp.where(lo(t), t, 0.0)) if a == 0 else (lambda t: jnp.where(lo(t), 0.0, t))
                qa, ka, da = _bf(pick(q)), _bf(pick(k)), _bf(pick(dout))
                s = jnp.where(valid, _dot_nt(qa, kb) * 0.125, NEG)
                p = jnp.exp(s - jnp.tile(L_ref[rows_q, :], (1, rep)))
                dp = _dot_nt(da, vb)
                ds = _bf(p * (dp - jnp.tile(D_ref[rows_q, :], (1, rep))) * 0.125)
                pb = _bf(p)
                tq, tk, tv = _dot(ds, ka), _dot_tn(ds, qa), _dot_tn(pb, da)
                dq, dk, dv = (tq, tk, tv) if a == 0 else (dq + tq, dk + tk, dv + tv)
            dq_ref[rows_q, :] = dq_ref[rows_q, :] + dq
            dk_ref[rows_k, :] = dk_ref[rows_k, :] + dk
            dv_ref[rows_k, :] = dv_ref[rows_k, :] + dv

        for d in DILATIONS:
            def step(idx, carry, d=d):
                rows_q, rows_k, n = _attn_rows(d, idx)

                @pl.when(n == 0)
                def _():
                    update(rows_q, rows_q, causal)

                @pl.when(n > 0)
                def _():
                    update(rows_q, rows_k, band)
                return carry

            lax.fori_loop(0, 32, step, 0)

    col = lambda off: pl.BlockSpec((S, 128), lambda h, off=off: (0, off + h))
    return pl.pallas_call(
        body, name="attn_bwd", grid=(4,),
        in_specs=[col(0), col(4), col(8), col(0), col(0), col(0)],
        out_specs=[col(0), col(0), col(0)],
        out_shape=[jax.ShapeDtypeStruct((S, AW), F32)] * 3,
        scratch_shapes=[pltpu.VMEM((S, 128), F32)] * 4,
        compiler_params=_params(1),
    )(qkv, qkv, qkv, o, lse, do)


def _logsig(x):
    return jnp.minimum(x, 0.0) - jnp.log1p(jnp.exp(-jnp.abs(x)))


def _conv_taps(xp, n):
    return [xp[8:] if j == 3 else pltpu.roll(xp, 3 - j, 0)[8:] for j in range(4)]


def _conv_silu(xp, w_ref, b_ref, n):
    taps = _conv_taps(xp, n)
    c = b_ref[...] + sum(w_ref[j:j + 1, :] * taps[j] for j in range(4))
    sg = _sigmoid(c)
    return c, sg, taps


def _chunk_gates(G):
    r = lax.broadcasted_iota(jnp.int32, (LC, LC), 0)
    c = lax.broadcasted_iota(jnp.int32, (LC, LC), 1)
    tril = (c <= r).astype(F32)
    triu = (c >= r).astype(F32)
    eye = (c == r).astype(F32)
    logf = _logsig(G)
    b_col = jnp.dot(tril, logf, preferred_element_type=F32, precision=HI)
    b_row = lax.dot_general(logf, triu, (((0,), (0,)), ((), ())), preferred_element_type=F32, precision=HI)
    g_row = lax.dot_general(G, eye, (((0,), (0,)), ((), ())), preferred_element_type=F32, precision=HI)
    return b_col, b_row, g_row, tril, triu


def _colpick(X, lane):
    li = lax.broadcasted_iota(jnp.int32, X.shape, 1)
    return jnp.sum(jnp.where(li == lane, X, 0.0), axis=1, keepdims=True)


def _rowpick(XT, row):
    ri = lax.broadcasted_iota(jnp.int32, XT.shape, 0)
    return jnp.sum(jnp.where(ri == row, XT, 0.0), axis=0, keepdims=True)


def _mlstm_head(qh, kh, vh, G, b_col, b_row, g_row, h, Ch, nh, m_prev):
    bt = _colpick(b_col, 4 + h)
    i_col = _colpick(G, h)
    bs = _rowpick(b_row, 4 + h)
    i_row = _rowpick(g_row, h)
    r = lax.broadcasted_iota(jnp.int32, (LC, LC), 0)
    c = lax.broadcasted_iota(jnp.int32, (LC, LC), 1)
    log_d = jnp.where(c <= r, bt - bs + i_row, NEG)
    log_inter = bt + m_prev
    m_t = jnp.maximum(log_inter, jnp.max(log_d, axis=1, keepdims=True))
    Dm = jnp.exp(log_d - m_t)
    g = jnp.exp(log_inter - m_t)
    qb, kb, vb = _bf(qh), _bf(kh), _bf(vh)
    Am = _dot_nt(qb, kb) * Dm
    qC = _dot(qb, _bf(Ch))
    num = g * qC + _dot(_bf(Am), vb)
    qn = jnp.sum(qh * nh, axis=1, keepdims=True)
    den = g * qn + jnp.sum(Am, axis=1, keepdims=True)
    floor = jnp.exp(-m_t)
    dd = jnp.maximum(jnp.abs(den), floor)
    hh = num / dd
    lane = lax.broadcasted_iota(jnp.int32, (1, LC), 1)
    blast = jnp.sum(jnp.where(lane == LC - 1, bs, 0.0), axis=1, keepdims=True)
    log_s = blast - bt + i_col
    m_new = jnp.maximum(blast + m_prev, jnp.max(log_s, axis=0, keepdims=True))
    decay = jnp.exp(blast + m_prev - m_new)
    ws = jnp.exp(log_s - m_new)
    kw = kh * ws
    C_new = decay * Ch + _dot_tn(_bf(kw), vb)
    n_new = decay * nh + jnp.sum(kw, axis=0, keepdims=True)
    return dict(Dm=Dm, g=g, Am=Am, qC=qC, qn=qn, den=den, floor=floor, dd=dd, h=hh, decay=decay, ws=ws, kw=kw,
                C_new=C_new, n_new=n_new, m_new=m_new, qb=qb, kb=kb, vb=vb)


def _head_out(hh, mo_h, gn_h):
    r = lax.rsqrt(jnp.mean(hh * hh, axis=-1, keepdims=True) + EPS)
    hn = hh * r
    sg = _sigmoid(mo_h)
    return sg * (hn * gn_h), hn, r, sg


def _mlstm_fwd(mqk, mv, mo, gates, conv_w, conv_b, gate_b, gn):
    nblk = S // TB
    ncb = TB // LC

    def body(x_ref, v_ref, o_ref, g_ref, w_ref, b_ref, gb_ref, gn_ref, out_ref, cs_ref, ns_ref, ms_ref,
             tail, Cst, nst, mst, qs, ks):
        i = pl.program_id(0)

        @pl.when(i == 0)
        def _():
            tail[...] = jnp.zeros_like(tail)
            Cst[...] = jnp.zeros_like(Cst)
            nst[...] = jnp.zeros_like(nst)
            mst[...] = jnp.zeros_like(mst)

        x = x_ref[...]
        xp = jnp.concatenate([tail[...], x], axis=0)
        tail[...] = x[TB - 8:TB, :]
        c, sg, _ = _conv_silu(xp, w_ref, b_ref, TB)
        y = c * sg
        qs[...] = y[:, 0:MW]
        ks[...] = y[:, MW:2 * MW] * (1.0 / math.sqrt(128.0))

        for cc in range(ncb):
            rows = slice(cc * LC, (cc + 1) * LC)
            G = g_ref[rows, :] + gb_ref[...]
            b_col, b_row, g_row, _, _ = _chunk_gates(G)
            cs_ref[cc] = Cst[...]
            ns_ref[cc] = nst[...]
            ms_ref[cc] = mst[...]
            for h in range(4):
                ln = slice(h * 128, (h + 1) * 128)
                m_prev = jnp.max(mst[0:1, ln], axis=1, keepdims=True)
                f = _mlstm_head(qs[rows, ln], ks[rows, ln], v_ref[rows, ln], G, b_col, b_row, g_row, h,
                                Cst[:, ln], nst[0:1, ln], m_prev)
                out, _, _, _ = _head_out(f["h"], o_ref[rows, ln], gn_ref[:, ln])
                out_ref[rows, ln] = out
                Cst[:, ln] = f["C_new"]
                nst[0:1, ln] = f["n_new"]
                mst[0:1, ln] = jnp.broadcast_to(f["m_new"], (1, 128))

    row = lambda wd: pl.BlockSpec((TB, wd), lambda i: (i, 0))
    return pl.pallas_call(
        body, name="mlstm_fwd", grid=(nblk,),
        in_specs=[row(1024), row(MW), row(MW), row(128), _cspec((4, 1024)), _cspec((1, 1024)), _cspec((1, 128)),
                  _cspec((1, MW))],
        out_specs=[row(MW), pl.BlockSpec((ncb, 128, MW), lambda i: (i, 0, 0)),
                   pl.BlockSpec((ncb, 8, MW), lambda i: (i, 0, 0)), pl.BlockSpec((ncb, 8, MW), lambda i: (i, 0, 0))],
        out_shape=[jax.ShapeDtypeStruct((S, MW), F32), jax.ShapeDtypeStruct((S // LC, 128, MW), F32),
                   jax.ShapeDtypeStruct((S // LC, 8, MW), F32), jax.ShapeDtypeStruct((S // LC, 8, MW), F32)],
        scratch_shapes=[pltpu.VMEM((8, 1024), F32), pltpu.VMEM((128, MW), F32), pltpu.VMEM((8, MW), F32),
                        pltpu.VMEM((8, MW), F32), pltpu.VMEM((TB, MW), F32), pltpu.VMEM((TB, MW), F32)],
        compiler_params=_params(1),
    )(mqk, mv, mo, gates, conv_w, conv_b, gate_b, gn)


def _mlstm_bwd(mqk, mv, mo, gates, conv_w, conv_b, gate_b, gn, cs, ns, ms, dout):
    nblk = S // TB
    ncb = TB // LC
    kscale = 1.0 / math.sqrt(128.0)

    def body(x_ref, xprev_ref, v_ref, o_ref, g_ref, w_ref, b_ref, gb_ref, gn_ref, cs_ref, ns_ref, ms_ref, do_ref,
             dx_ref, dv_ref, dmo_ref, dg_ref, dw_ref, db_ref, dgn_ref, dgb_ref,
             dCst, dnst, dyhead, qs, ks, dqk):
        i = pl.program_id(0)
        blk = nblk - 1 - i

        @pl.when(i == 0)
        def _():
            dCst[...] = jnp.zeros_like(dCst)
            dnst[...] = jnp.zeros_like(dnst)
            dyhead[...] = jnp.zeros_like(dyhead)
            dw_ref[...] = jnp.zeros_like(dw_ref)
            db_ref[...] = jnp.zeros_like(db_ref)
            dgn_ref[...] = jnp.zeros_like(dgn_ref)
            dgb_ref[...] = jnp.zeros_like(dgb_ref)

        x = x_ref[...]
        xprev = jnp.where(blk == 0, 0.0, xprev_ref[...])
        xp = jnp.concatenate([xprev, x], axis=0)
        c, sg, taps = _conv_silu(xp, w_ref, b_ref, TB)
        y = c * sg
        qs[...] = y[:, 0:MW]
        ks[...] = y[:, MW:2 * MW] * kscale
        lane128 = lax.broadcasted_iota(jnp.int32, (LC, 128), 1)
        rowi = lax.broadcasted_iota(jnp.int32, (LC, 1), 0)
        ones = jnp.ones((LC, 128), F32)

        for cc in reversed(range(ncb)):
            rows = slice(cc * LC, (cc + 1) * LC)
            G = g_ref[rows, :] + gb_ref[...]
            b_col, b_row, g_row, _, triu = _chunk_gates(G)
            dB = jnp.zeros((LC, 128), F32)
            dI = jnp.zeros((LC, 128), F32)
            for h in range(4):
                ln = slice(h * 128, (h + 1) * 128)
                Ch = cs_ref[cc, :, ln]
                nh = ns_ref[cc, 0:1, ln]
                m_prev = jnp.max(ms_ref[cc, 0:1, ln], axis=1, keepdims=True)
                qh, kh, vh = qs[rows, ln], ks[rows, ln], v_ref[rows, ln]
                f = _mlstm_head(qh, kh, vh, G, b_col, b_row, g_row, h, Ch, nh, m_prev)
                hh, dd, den, g, Am, Dm = f["h"], f["dd"], f["den"], f["g"], f["Am"], f["Dm"]
                qb, kb, vb = f["qb"], f["kb"], f["vb"]
                gn_h = gn_ref[:, ln]
                _, hn, r, sgo = _head_out(hh, o_ref[rows, ln], gn_h)
                do = do_ref[rows, ln]
                hm = hn * gn_h
                dmo_ref[rows, ln] = do * hm * sgo * (1.0 - sgo)
                dhm = do * sgo
                dgn_ref[:, ln] = dgn_ref[:, ln] + jnp.sum(dhm * hn, axis=0, keepdims=True)
                dhn = dhm * gn_h
                dh = r * (dhn - hn * jnp.mean(dhn * hn, axis=-1, keepdims=True))
                dnum = dh / dd
                ddd = -jnp.sum(dh * hh, axis=1, keepdims=True) / dd
                dden = jnp.where(jnp.abs(den) >= f["floor"], ddd * jnp.sign(den), 0.0)
                dnb = _bf(dnum)
                dA = _dot_nt(dnb, vb) + dden
                dv = _dot_tn(_bf(Am), dnb)
                gd = _bf(g * dnum)
                gq = g * dden
                dq = _dot_nt(gd, _bf(Ch)) + gq * nh
                dCn = dCst[:, ln]
                dnn = dnst[0:1, ln]
                dC = f["decay"] * dCn + _dot_tn(qb, gd)
                dn = f["decay"] * dnn + jnp.sum(gq * qh, axis=0, keepdims=True)
                dg = jnp.sum(dnum * f["qC"], axis=1, keepdims=True) + dden * f["qn"]
                dS = _bf(dA * Dm)
                dq = dq + _dot(dS, kb)
                dk = _dot_tn(dS, qb)
                Gm = dA * Am
                gam = dg * g
                dCb = _bf(dCn)
                E = _dot_nt(vb, dCb) + dnn
                ws = f["ws"]
                dk = dk + ws * E
                om = jnp.sum(E * kh, axis=1, keepdims=True) * ws
                dv = dv + _dot(_bf(f["kw"]), dCb)
                ddecay = (jnp.sum(jnp.sum(dCn * Ch, axis=1, keepdims=True), axis=0, keepdims=True)
                          + jnp.sum(dnn * nh, axis=1, keepdims=True))
                delta = ddecay * f["decay"]
                rows_g = jnp.sum(Gm, axis=1, keepdims=True)
                cols_g = lax.dot_general(Gm, ones, (((0,), (0,)), ((), ())), preferred_element_type=F32, precision=HI)
                last = jnp.where(rowi == LC - 1, jnp.sum(om, axis=0, keepdims=True) + delta, 0.0)
                db = rows_g + gam - om + last - cols_g
                di = cols_g + om
                dB = dB + jnp.where(lane128 == 4 + h, db, 0.0)
                dI = dI + jnp.where(lane128 == h, di, 0.0)
                dCst[:, ln] = dC
                dnst[0:1, ln] = dn
                dqk[rows, ln] = dq
                dqk[rows, MW + h * 128:MW + (h + 1) * 128] = dk * kscale
                dv_ref[rows, ln] = dv
            dlogf = jnp.dot(triu, dB, preferred_element_type=F32, precision=HI)
            dG = dI + dlogf * _sigmoid(-G)
            dG = jnp.where(lane128 < 8, dG, 0.0)
            dg_ref[rows, :] = dG
            dgb_ref[...] = dgb_ref[...] + jnp.sum(dG, axis=0, keepdims=True)

        dy = dqk[...] * (sg * (1.0 + c * (1.0 - sg)))
        db_ref[...] = db_ref[...] + jnp.sum(dy, axis=0, keepdims=True)
        for j in range(4):
            dw_ref[j:j + 1, :] = dw_ref[j:j + 1, :] + jnp.sum(dy * taps[j], axis=0, keepdims=True)
        dyp = jnp.concatenate([dy, dyhead[...]], axis=0)
        dx = w_ref[3:4, :] * dy
        for j in range(3):
            dx = dx + w_ref[j:j + 1, :] * pltpu.roll(dyp, TB + 8 - (3 - j), 0)[0:TB]
        dx_ref[...] = dx
        dyhead[...] = dy[0:8, :]

    rrow = lambda wd: pl.BlockSpec((TB, wd), lambda i: (nblk - 1 - i, 0))
    st = lambda r: pl.BlockSpec((ncb, r, MW), lambda i: (nblk - 1 - i, 0, 0))
    prev8 = pl.BlockSpec((8, 1024), lambda i: (jnp.maximum((nblk - 1 - i) * (TB // 8) - 1, 0), 0))
    return pl.pallas_call(
        body, name="mlstm_bwd", grid=(nblk,),
        in_specs=[rrow(1024), prev8, rrow(MW), rrow(MW), rrow(128), _cspec((4, 1024)), _cspec((1, 1024)),
                  _cspec((1, 128)), _cspec((1, MW)), st(128), st(8), st(8), rrow(MW)],
        out_specs=[rrow(1024), rrow(MW), rrow(MW), rrow(128),
                   pl.BlockSpec((4, 1024), lambda i: (0, 0)), pl.BlockSpec((1, 1024), lambda i: (0, 0)),
                   pl.BlockSpec((1, MW), lambda i: (0, 0)), pl.BlockSpec((1, 128), lambda i: (0, 0))],
        out_shape=[jax.ShapeDtypeStruct((S, 1024), F32), jax.ShapeDtypeStruct((S, MW), F32),
                   jax.ShapeDtypeStruct((S, MW), F32), jax.ShapeDtypeStruct((S, 128), F32),
                   jax.ShapeDtypeStruct((4, 1024), F32), jax.ShapeDtypeStruct((1, 1024), F32),
                   jax.ShapeDtypeStruct((1, MW), F32), jax.ShapeDtypeStruct((1, 128), F32)],
        scratch_shapes=[pltpu.VMEM((128, MW), F32), pltpu.VMEM((8, MW), F32), pltpu.VMEM((8, 1024), F32),
                        pltpu.VMEM((TB, MW), F32), pltpu.VMEM((TB, MW), F32), pltpu.VMEM((TB, 1024), F32)],
        compiler_params=_params(1),
    )(mqk, mqk, mv, mo, gates, conv_w, conv_b, gate_b, gn, cs, ns, ms, dout)


def _out_proj(x, attn, ml, w, g):
    tm = 256

    def body(x_ref, a_ref, m_ref, w_ref, g_ref, h_ref, u_ref):
        h1 = x_ref[...] + _dot(_bf(a_ref[...]), w_ref[0:AW, :]) + _dot(_bf(m_ref[...]), w_ref[AW:D, :])
        h_ref[...] = h1
        n, _ = _rms(h1)
        u_ref[...] = _bf(n * g_ref[...])

    row = lambda wd: pl.BlockSpec((tm, wd), lambda i: (i, 0))
    return pl.pallas_call(
        body, name="out_proj", grid=(S // tm,),
        in_specs=[row(D), row(AW), row(MW), _cspec((D, D)), _cspec((1, D))],
        out_specs=[row(D), row(D)],
        out_shape=[jax.ShapeDtypeStruct((S, D), F32), jax.ShapeDtypeStruct((S, D), BF16)],
        compiler_params=_params(1),
    )(x, attn, ml, w, g)


def _mlp_fwd(h1, u2, w_up, w_down):
    tm = 256

    def body(h_ref, u_ref, wu_ref, wd_ref, a_ref, o_ref):
        u = u_ref[...]
        acc = h_ref[...]
        for c in range(4):
            cols = slice(c * 1024, (c + 1) * 1024)
            a = _dot(u, wu_ref[:, cols])
            a_ref[:, cols] = a
            r = jnp.maximum(a, 0.0)
            acc = acc + _dot(_bf(r * r), wd_ref[cols, :])
        o_ref[...] = acc

    row = lambda wd: pl.BlockSpec((tm, wd), lambda i: (i, 0))
    return pl.pallas_call(
        body, name="mlp_fwd", grid=(S // tm,),
        in_specs=[row(D), row(D), _cspec((D, DFF)), _cspec((DFF, D))],
        out_specs=[row(DFF), row(D)],
        out_shape=[jax.ShapeDtypeStruct((S, DFF), F32), jax.ShapeDtypeStruct((S, D), F32)],
        compiler_params=_params(1),
    )(h1, u2, w_up, w_down)


def _ple_loss(h2, p, target, w_pg, w_ple, g_ple, g_fin):
    tm = 256

    def body(h_ref, p_ref, t_ref, wg_ref, wp_ref, gp_ref, gf_ref,
             dh_ref, dwg_ref, dwp_ref, dgp_ref, dgf_ref, loss_ref, acc_g, acc_p):
        i = pl.program_id(0)

        @pl.when(i == 0)
        def _():
            acc_g[...] = jnp.zeros_like(acc_g)
            acc_p[...] = jnp.zeros_like(acc_p)
            dgp_ref[...] = jnp.zeros_like(dgp_ref)
            dgf_ref[...] = jnp.zeros_like(dgf_ref)
            loss_ref[...] = jnp.zeros_like(loss_ref)

        h2v = h_ref[...]
        n2, rs2 = _rms(h2v)
        u3 = _bf(n2 * gp_ref[...])
        gt = _sigmoid(_dot(u3, wg_ref[...]))
        pb = _bf(p_ref[...])
        e = _dot(pb, wp_ref[...])
        h3 = h2v + gt * e
        n3, rs3 = _rms(h3)
        err = n3 * gf_ref[...] - t_ref[...]
        loss_ref[...] = loss_ref[...] + 0.5 / D * jnp.sum(jnp.sum(err * err, axis=1, keepdims=True), axis=0, keepdims=True)
        dy = err * (1.0 / D)
        dgf_ref[...] = dgf_ref[...] + jnp.sum(dy * n3, axis=0, keepdims=True)
        dh3 = _rms_bwd(dy, n3, rs3, gf_ref[...])
        de = _bf(dh3 * gt)
        dz = _bf(dh3 * e * gt * (1.0 - gt))
        acc_p[...] = acc_p[...] + _dot_tn(pb, de)
        acc_g[...] = acc_g[...] + _dot_tn(u3, dz)
        du3 = _dot_nt(dz, wg_ref[...])
        dgp_ref[...] = dgp_ref[...] + jnp.sum(du3 * n2, axis=0, keepdims=True)
        dh_ref[...] = dh3 + _rms_bwd(du3, n2, rs2, gp_ref[...])

        @pl.when(i == S // tm - 1)
        def _():
            dwg_ref[...] = _bf(acc_g[...])
            for j in range(NDEV):
                dwp_ref[j] = _bf(acc_p[:, j * 128:(j + 1) * 128])

    row = lambda wd: pl.BlockSpec((tm, wd), lambda i: (i, 0))
    whole = lambda shp: pl.BlockSpec(shp, lambda i: (0,) * len(shp))
    return pl.pallas_call(
        body, name="ple_loss", grid=(S // tm,),
        in_specs=[row(D), row(PLE), row(D), _cspec((D, D)), _cspec((PLE, D)), _cspec((1, D)), _cspec((1, D))],
        out_specs=[row(D), whole((D, D)), whole((NDEV, PLE, 128)), whole((1, D)), whole((1, D)), whole((1, 1))],
        out_shape=[jax.ShapeDtypeStruct((S, D), F32), jax.ShapeDtypeStruct((D, D), BF16),
                   jax.ShapeDtypeStruct((NDEV, PLE, 128), BF16), jax.ShapeDtypeStruct((1, D), F32),
                   jax.ShapeDtypeStruct((1, D), F32), jax.ShapeDtypeStruct((1, 1), F32)],
        scratch_shapes=[pltpu.VMEM((D, D), F32), pltpu.VMEM((PLE, D), F32)],
        compiler_params=_params(1),
    )(h2, p, target, w_pg, w_ple, g_ple, g_fin)


def _mlp_bwd(dh2, a, h1, g, w_up, w_down):
    tm = 256

    def body(d_ref, a_ref, h_ref, g_ref, wu_ref, wd_ref, da_ref, dh1_ref, dg_ref):
        @pl.when(pl.program_id(0) == 0)
        def _():
            dg_ref[...] = jnp.zeros_like(dg_ref)

        dh2v = d_ref[...]
        db = _bf(dh2v)
        du = jnp.zeros((tm, D), F32)
        for c in range(4):
            cols = slice(c * 1024, (c + 1) * 1024)
            dr = _dot_nt(db, wd_ref[cols, :])
            da = _bf(dr * (2.0 * jnp.maximum(a_ref[:, cols], 0.0)))
            da_ref[:, cols] = da
            du = du + _dot_nt(da, wu_ref[:, cols])
        n, rs = _rms(h_ref[...])
        dg_ref[...] = dg_ref[...] + jnp.sum(du * n, axis=0, keepdims=True)
        dh1_ref[...] = dh2v + _rms_bwd(du, n, rs, g_ref[...])

    row = lambda wd: pl.BlockSpec((tm, wd), lambda i: (i, 0))
    return pl.pallas_call(
        body, name="mlp_bwd", grid=(S // tm,),
        in_specs=[row(D), row(DFF), row(D), _cspec((1, D)), _cspec((D, DFF)), _cspec((DFF, D))],
        out_specs=[row(DFF), row(D), pl.BlockSpec((1, D), lambda i: (0, 0))],
        out_shape=[jax.ShapeDtypeStruct((S, DFF), BF16), jax.ShapeDtypeStruct((S, D), F32),
                   jax.ShapeDtypeStruct((1, D), F32)],
        compiler_params=_params(1),
    )(dh2, a, h1, g, w_up, w_down)


def _out_proj_bwd(dh1, attn, ml, w):
    tm = 256

    def body(d_ref, a_ref, m_ref, w_ref, da_ref, dm_ref, dw_ref, acc):
        i = pl.program_id(0)

        @pl.when(i == 0)
        def _():
            acc[...] = jnp.zeros_like(acc)

        db = _bf(d_ref[...])
        dmix = _dot_nt(db, w_ref[...])
        da_ref[...] = dmix[:, 0:AW]
        dm_ref[...] = dmix[:, AW:D]
        acc[0:AW, :] = acc[0:AW, :] + _dot_tn(_bf(a_ref[...]), db)
        acc[AW:D, :] = acc[AW:D, :] + _dot_tn(_bf(m_ref[...]), db)

        @pl.when(i == S // tm - 1)
        def _():
            dw_ref[...] = _bf(acc[...])

    row = lambda wd: pl.BlockSpec((tm, wd), lambda i: (i, 0))
    return pl.pallas_call(
        body, name="out_proj_bwd", grid=(S // tm,),
        in_specs=[row(D), row(AW), row(MW), _cspec((D, D))],
        out_specs=[row(AW), row(MW), pl.BlockSpec((D, D), lambda i: (0, 0))],
        out_shape=[jax.ShapeDtypeStruct((S, AW), F32), jax.ShapeDtypeStruct((S, MW), F32),
                   jax.ShapeDtypeStruct((D, D), BF16)],
        scratch_shapes=[pltpu.VMEM((D, D), F32)],
        compiler_params=_params(1),
    )(dh1, attn, ml, w)


def _in_proj_bwd(dq, dk, dv, dmqk, dmv, dmo, dgt, dh1, x, g1, w, rc, ra, rb):
    tm = 256

    def body(dq_ref, dk_ref, dv_ref, dmqk_ref, dmv_ref, dmo_ref, dgt_ref, dh_ref, x_ref, g_ref, w_ref,
             rc_ref, ra_ref, rb_ref, dp_ref, dx_ref, dg_ref):
        @pl.when(pl.program_id(0) == 0)
        def _():
            dg_ref[...] = jnp.zeros_like(dg_ref)

        c, a, b = rc_ref[...], ra_ref[...], rb_ref[...]
        for half, ref in enumerate((dq_ref, dk_ref)):
            for t in range(4):
                lo = half * 512 + t * 128
                dp_ref[:, lo:lo + 128] = _bf(_rope_bwd(ref[:, t * 128:(t + 1) * 128], c, a, b))
        dp_ref[:, 1024:1536] = _bf(dv_ref[...])
        dp_ref[:, 1536:2560] = _bf(dmqk_ref[...])
        dp_ref[:, 2560:3072] = _bf(dmv_ref[...])
        dp_ref[:, 3072:3584] = _bf(dmo_ref[...])
        dp_ref[:, 3584:3712] = _bf(dgt_ref[...])
        dp_ref[:, 3712:PW] = jnp.zeros((tm, PW - 3712), BF16)
        du = jnp.zeros((tm, D), F32)
        for s in range(PW // 768):
            cols = slice(s * 768, (s + 1) * 768)
            du = du + _dot_nt(dp_ref[:, cols], w_ref[:, cols])
        n, rs = _rms(x_ref[...])
        dg_ref[...] = dg_ref[...] + jnp.sum(du * n, axis=0, keepdims=True)
        dx_ref[...] = dh_ref[...] + _rms_bwd(du, n, rs, g_ref[...])

    row = lambda wd: pl.BlockSpec((tm, wd), lambda i: (i, 0))
    return pl.pallas_call(
        body, name="in_proj_bwd", grid=(S // tm,),
        in_specs=[row(AW), row(AW), row(AW), row(1024), row(MW), row(MW), row(128), row(D), row(D), _cspec((1, D)),
                  _cspec((D, PW)), row(128), row(128), row(128)],
        out_specs=[row(PW), row(D), pl.BlockSpec((1, D), lambda i: (0, 0))],
        out_shape=[jax.ShapeDtypeStruct((S, PW), BF16), jax.ShapeDtypeStruct((S, D), F32),
                   jax.ShapeDtypeStruct((1, D), F32)],
        compiler_params=_params(1),
    )(dq, dk, dv, dmqk, dmv, dmo, dgt, dh1, x, g1, w, rc, ra, rb)


def _wgrad(name, A, B, a_fn, b_fn, tk, tn, out_shape, out_spec, ts=512):
    K, N = A.shape[1], B.shape[1]
    nrt = S // ts

    def body(a_ref, b_ref, o_ref, acc):
        r = pl.program_id(2)

        @pl.when(r == 0)
        def _():
            acc[...] = jnp.zeros_like(acc)

        acc[...] = acc[...] + _dot_tn(a_fn(a_ref[...]), b_fn(b_ref[...]))

        @pl.when(r == nrt - 1)
        def _():
            o_ref[...] = _bf(acc[...]).reshape(o_ref.shape)

    return pl.pallas_call(
        body, name=name, grid=(N // tn, K // tk, nrt),
        in_specs=[pl.BlockSpec((ts, tk), lambda n, k, r: (r, k)), pl.BlockSpec((ts, tn), lambda n, k, r: (r, n))],
        out_specs=out_spec,
        out_shape=jax.ShapeDtypeStruct(out_shape, BF16),
        scratch_shapes=[pltpu.VMEM((tk, tn), F32)],
        compiler_params=_params(3),
    )(A, B)


def _relu2_bf(a):
    r = jnp.maximum(a, 0.0)
    return _bf(r * r)


def _ident(a):
    return a


def _local_step(x, p, target, g1, w_in_p, conv_w, conv_b, gate_b, gn, w_out, g_mlp, w_up, w_down, g_ple, w_pg,
                w_ple, g_fin):
    rc, ra, rb = _rope_tables()
    qkv, mqk, mv, mo, gates, u1 = _in_proj(x, g1, w_in_p, rc, ra, rb)
    attn, lse = _attn_fwd(qkv)
    ml, cs, ns, ms = _mlstm_fwd(mqk, mv, mo, gates, conv_w, conv_b, gate_b, gn)
    h1, u2 = _out_proj(x, attn, ml, w_out, g_mlp)
    a, h2 = _mlp_fwd(h1, u2, w_up, w_down)
    dh2, dw_pg, dw_ple8, dg_ple, dg_fin, loss = _ple_loss(h2, p, target, w_pg, w_ple, g_ple, g_fin)
    da, dh1, dg_mlp = _mlp_bwd(dh2, a, h1, g_mlp, w_up, w_down)
    dw_up8 = _wgrad("wgrad_up", u2, da, _ident, _ident, D, 512, (NDEV, D, 512),
                    pl.BlockSpec((None, D, 512), lambda n, k, r: (n, 0, 0)))
    dw_down = _wgrad("wgrad_down", a, dh2, _relu2_bf, _bf, 1024, 1024, (DFF, D),
                     pl.BlockSpec((1024, 1024), lambda n, k, r: (k, n)))
    d_attn, d_ml, dw_out = _out_proj_bwd(dh1, attn, ml, w_out)
    dmqk, dmv, dmo, dgt, dconv_w, dconv_b, dgn, dgate_b = _mlstm_bwd(
        mqk, mv, mo, gates, conv_w, conv_b, gate_b, gn, cs, ns, ms, d_ml)
    dq, dk, dv = _attn_bwd(qkv, attn, lse, d_attn)
    dproj, dx, dg1 = _in_proj_bwd(dq, dk, dv, dmqk, dmv, dmo, dgt, dh1, x, g1, w_in_p, rc, ra, rb)
    dw_in_p = _wgrad("wgrad_in", u1, dproj, _ident, _ident, D, 1280, (D, PW),
                     pl.BlockSpec((D, 1280), lambda n, k, r: (0, n)))
    big = dict(w_in=dw_in_p, w_out=dw_out, w_up=dw_up8, w_down=dw_down, w_ple_gate=dw_pg, w_ple=dw_ple8)
    small = dict(norm_mix_g=dg1, conv_b=dconv_b, gate_b=dgate_b, mlstm_norm_g=dgn, norm_mlp_g=dg_mlp,
                 norm_ple_g=dg_ple, final_norm_g=dg_fin, conv_w=dconv_w)
    return loss, dx, big, small


MESH = pl.DeviceIdType.MESH
ANY = pl.BlockSpec(memory_space=pl.ANY)
VM = pl.BlockSpec(memory_space=pltpu.VMEM)
FLIPS = [(dx, dy, dc) for dx in (0, 1) for dy in (0, 1) for dc in (0, 1)][1:]


def _place():
    return lax.axis_index("x"), lax.axis_index("y"), lax.axis_index("c")


def _dev_index(px, py, pc):
    return 4 * px + 2 * py + pc


def _gather_weights(shards, dtypes):
    nw = len(shards)

    def body(*refs):
        ins, outs, bufs = refs[:nw], refs[nw:2 * nw], refs[2 * nw:3 * nw]
        send_sems, recv_sems, local_sems = refs[3 * nw:]
        x, y, c = _place()
        me, sib = (x, y, c), (x, y, 1 - c)
        chips = [(1 - x, y), (x, 1 - y), (1 - x, 1 - y)]
        for w in range(nw):
            bufs[w][...] = ins[w][...].astype(bufs[w].dtype)

        def copy(w, k, block, to, from_buf=False):
            dst = outs[w].at[_dev_index(*block)]
            return pltpu.make_async_remote_copy(
                src_ref=bufs[w] if from_buf else dst, dst_ref=dst, send_sem=send_sems.at[w, k],
                recv_sem=recv_sems.at[w, k], device_id=to, device_id_type=MESH)

        mine = [pltpu.make_async_copy(bufs[w], outs[w].at[_dev_index(*me)], local_sems.at[w]) for w in range(nw)]
        for cp in mine:
            cp.start()
        first = []
        for w in range(nw):
            first.append(copy(w, 0, me, sib, True))
            first += [copy(w, 1 + j, me, (*chip, c), True) for j, chip in enumerate(chips)]
        for cp in first:
            cp.start()
        passed = []
        for j, chip in enumerate(chips):
            for w in range(nw):
                copy(w, 1 + j, (*chip, c), me).wait_recv()
                fwd = copy(w, 4 + j, (*chip, c), sib)
                fwd.start()
                passed.append(fwd)
        for w in range(nw):
            copy(w, 0, sib, me).wait_recv()
        for j, chip in enumerate(chips):
            for w in range(nw):
                copy(w, 4 + j, (*chip, 1 - c), me).wait_recv()
        for cp in first + passed:
            cp.wait_send()
        for cp in mine:
            cp.wait()

    return pl.pallas_call(
        body, name="gather_weights",
        in_specs=[VM] * nw, out_specs=[ANY] * nw,
        out_shape=[jax.ShapeDtypeStruct((NDEV, *s.shape), dt) for s, dt in zip(shards, dtypes)],
        scratch_shapes=[pltpu.VMEM(s.shape, dt) for s, dt in zip(shards, dtypes)]
        + [pltpu.SemaphoreType.DMA((nw, 7)), pltpu.SemaphoreType.DMA((nw, 7)), pltpu.SemaphoreType.DMA((nw,))],
        compiler_params=_params(),
    )(*shards)


def _scatter_grads(parts):
    nw = len(parts)

    def body(*refs):
        ins, outs = refs[:nw], refs[nw:2 * nw]
        send_sems, recv_sems, local_sems = refs[2 * nw:]
        x, y, c = _place()
        me = _dev_index(x, y, c)
        copies = []
        for w in range(nw):
            loc = pltpu.make_async_copy(ins[w].at[me], outs[w].at[me], local_sems.at[w])
            loc.start()
            copies.append(loc)
            for k, (dx, dy, dc) in enumerate(FLIPS):
                peer = ((x + dx) % 2, (y + dy) % 2, (c + dc) % 2)
                cp = pltpu.make_async_remote_copy(
                    src_ref=ins[w].at[_dev_index(*peer)], dst_ref=outs[w].at[me], send_sem=send_sems.at[w, k],
                    recv_sem=recv_sems.at[w, k], device_id=peer, device_id_type=MESH)
                cp.start()
                copies.append(cp)
        for cp in copies:
            cp.wait()

    return pl.pallas_call(
        body, name="scatter_grads",
        in_specs=[ANY] * nw, out_specs=[ANY] * nw,
        out_shape=[jax.ShapeDtypeStruct(a.shape, a.dtype) for a in parts],
        scratch_shapes=[pltpu.SemaphoreType.DMA((nw, 7)), pltpu.SemaphoreType.DMA((nw, 7)),
                        pltpu.SemaphoreType.DMA((nw,))],
        compiler_params=_params(),
    )(*parts)


SMALL_ROWS = 64


def _allreduce_small(vals):
    nv = len(vals)

    def body(*refs):
        ins, out_ref = refs[:nv], refs[nv]
        pack, rbuf, send_sems, recv_sems = refs[nv + 1:]
        x, y, c = _place()
        me = _dev_index(x, y, c)
        pack[...] = jnp.zeros_like(pack)
        for i in range(nv):
            pack[8 * i:8 * i + 1, 0:ins[i].shape[1]] = ins[i][...]
        rbuf[me] = pack[...]
        copies = []
        for k, (dx, dy, dc) in enumerate(FLIPS):
            peer = ((x + dx) % 2, (y + dy) % 2, (c + dc) % 2)
            cp = pltpu.make_async_remote_copy(
                src_ref=pack, dst_ref=rbuf.at[me], send_sem=send_sems.at[k], recv_sem=recv_sems.at[k],
                device_id=peer, device_id_type=MESH)
            cp.start()
            copies.append(cp)
        for cp in copies:
            cp.wait()
        tot = rbuf[0]
        for j in range(1, NDEV):
            tot = tot + rbuf[j]
        out_ref[...] = tot

    return pl.pallas_call(
        body, name="allreduce_small",
        in_specs=[VM] * nv, out_specs=VM,
        out_shape=jax.ShapeDtypeStruct((SMALL_ROWS, 1024), F32),
        scratch_shapes=[pltpu.VMEM((SMALL_ROWS, 1024), F32), pltpu.VMEM((NDEV, SMALL_ROWS, 1024), F32),
                        pltpu.SemaphoreType.DMA((7,)), pltpu.SemaphoreType.DMA((7,))],
        compiler_params=_params(),
    )(*vals)


def _adamw(name, gparts, w, m, v, tr):
    P, R, C = gparts.shape
    c1 = 1.0 - ADAM_B1 ** ADAM_STEP
    c2 = 1.0 - ADAM_B2 ** ADAM_STEP

    def body(g_ref, w_ref, m_ref, v_ref, go_ref, d_ref, mo_ref, vo_ref):
        g = g_ref[0].astype(F32)
        for j in range(1, P):
            g = g + g_ref[j].astype(F32)
        m2 = ADAM_B1 * m_ref[...] + (1.0 - ADAM_B1) * g
        v2 = ADAM_B2 * v_ref[...] + (1.0 - ADAM_B2) * (g * g)
        go_ref[...] = g
        mo_ref[...] = m2
        vo_ref[...] = v2
        d_ref[...] = -ADAM_LR * ((m2 / c1) / (jnp.sqrt(v2 / c2) + ADAM_EPS) + ADAM_WD * w_ref[...])

    row = pl.BlockSpec((tr, C), lambda i: (i, 0))
    return pl.pallas_call(
        body, name=name, grid=(R // tr,),
        in_specs=[pl.BlockSpec((P, tr, C), lambda i: (0, i, 0)), row, row, row],
        out_specs=[row] * 4,
        out_shape=[jax.ShapeDtypeStruct((R, C), F32)] * 4,
        compiler_params=_params(1),
    )(gparts, w, m, v)


SMALL = ("norm_mix_g", "conv_b", "gate_b", "mlstm_norm_g", "norm_mlp_g", "norm_ple_g", "final_norm_g")


def _pack_small(vals):
    return jnp.concatenate([jnp.pad(a, ((0, 7), (0, 1024 - a.shape[1]))) for a in vals], axis=0)


def kernel(x, p, norm_mix_g, w_in, conv_w, conv_b, gate_b, mlstm_norm_g, w_out, norm_mlp_g, w_up, w_down, norm_ple_g, w_ple_gate, w_ple, final_norm_g, loss_target, m_norm_mix_g, m_w_in, m_conv_w, m_conv_b, m_gate_b, m_mlstm_norm_g, m_w_out, m_norm_mlp_g, m_w_up, m_w_down, m_norm_ple_g, m_w_ple_gate, m_w_ple, m_final_norm_g, v_norm_mix_g, v_w_in, v_conv_w, v_conv_b, v_gate_b, v_mlstm_norm_g, v_w_out, v_norm_mlp_g, v_w_up, v_w_down, v_norm_ple_g, v_w_ple_gate, v_w_ple, v_final_norm_g):
    big_names = ("w_in", "conv_w", "w_out", "w_up", "w_down", "w_ple_gate", "w_ple")
    wts = dict(w_in=w_in, conv_w=conv_w, w_out=w_out, w_up=w_up, w_down=w_down, w_ple_gate=w_ple_gate, w_ple=w_ple)
    mom = dict(w_in=m_w_in, conv_w=m_conv_w, w_out=m_w_out, w_up=m_w_up, w_down=m_w_down, w_ple_gate=m_w_ple_gate,
               w_ple=m_w_ple)
    var = dict(w_in=v_w_in, conv_w=v_conv_w, w_out=v_w_out, w_up=v_w_up, w_down=v_w_down, w_ple_gate=v_w_ple_gate,
               w_ple=v_w_ple)
    sq = lambda a: a.reshape(a.shape[1:])
    shards = [sq(wts[n]) for n in big_names]
    dts = [F32 if n == "conv_w" else BF16 for n in big_names]
    g_in, g_conv, g_out, g_up, g_down, g_pg, g_ple = _gather_weights(shards, dts)
    cols = lambda g: g.transpose(1, 0, 2).reshape(g.shape[1], NDEV * g.shape[2])
    w_in_full = jnp.pad(cols(g_in), ((0, 0), (0, PW - IN_W)))
    fin = final_norm_g.reshape(1, D)
    loss, dx, big, small = _local_step(
        x[0], p[0, 0], loss_target[0], norm_mix_g, w_in_full, cols(g_conv), conv_b, jnp.pad(gate_b, ((0, 0), (0, 120))),
        mlstm_norm_g, g_out.reshape(D, D), norm_mlp_g, cols(g_up), g_down.reshape(DFF, D), norm_ple_g,
        g_pg.reshape(D, D), cols(g_ple), fin)

    parts = [big["w_in"][:, :IN_W].reshape(D, NDEV, IN_W // NDEV).transpose(1, 0, 2),
             small["conv_w"].reshape(4, NDEV, 128).transpose(1, 0, 2),
             big["w_out"].reshape(NDEV, D // NDEV, D), big["w_up"], big["w_down"].reshape(NDEV, DFF // NDEV, D),
             big["w_ple_gate"].reshape(NDEV, D // NDEV, D), big["w_ple"]]
    recv = _scatter_grads(parts)
    total = _allreduce_small([small[n] for n in SMALL] + [loss])

    out = {}
    for n, r, tr in zip(big_names, recv, (256, 4, 128, 256, 256, 128, 256)):
        res = _adamw("adamw_" + n, r, sq(wts[n]), sq(mom[n]), sq(var[n]), tr)
        out[n] = [t.reshape(wts[n].shape) for t in res]
    sw = dict(norm_mix_g=norm_mix_g, conv_b=conv_b, gate_b=gate_b, mlstm_norm_g=mlstm_norm_g, norm_mlp_g=norm_mlp_g,
              norm_ple_g=norm_ple_g, final_norm_g=fin)
    sm = dict(norm_mix_g=m_norm_mix_g, conv_b=m_conv_b, gate_b=m_gate_b, mlstm_norm_g=m_mlstm_norm_g,
              norm_mlp_g=m_norm_mlp_g, norm_ple_g=m_norm_ple_g, final_norm_g=m_final_norm_g.reshape(1, D))
    sv = dict(norm_mix_g=v_norm_mix_g, conv_b=v_conv_b, gate_b=v_gate_b, mlstm_norm_g=v_mlstm_norm_g,
              norm_mlp_g=v_norm_mlp_g, norm_ple_g=v_norm_ple_g, final_norm_g=v_final_norm_g.reshape(1, D))
    nrow = 8 * len(SMALL)
    res = _adamw("adamw_small", total[0:nrow].reshape(1, nrow, 1024), _pack_small([sw[n] for n in SMALL]),
                 _pack_small([sm[n] for n in SMALL]), _pack_small([sv[n] for n in SMALL]), nrow)
    for i, n in enumerate(SMALL):
        shp = final_norm_g.shape if n == "final_norm_g" else sw[n].shape
        out[n] = [t[8 * i, 0:sw[n].shape[1]].reshape(shp) for t in res]
    order = ("norm_mix_g", "w_in", "conv_w", "conv_b", "gate_b", "mlstm_norm_g", "w_out", "norm_mlp_g", "w_up", "w_down",
             "norm_ple_g", "w_ple_gate", "w_ple", "final_norm_g")
    loss_all = total[nrow, 0]
    return (loss_all, dx[None], *[out[n][0] for n in order], *[out[n][1] for n in order],
            *[out[n][2] for n in order], *[out[n][3] for n in order])
```

```python
import functools
import math

import jax
import jax.numpy as jnp
from jax import lax
from jax.experimental import pallas as pl
from jax.experimental.pallas import tpu as pltpu

F32, BF16 = jnp.float32, jnp.bfloat16
S = 4096
D = 1024
AW = 512
MW = 512
DFF = 4096
PLE = 256
IN_W = 3592
PW = 3840
NDEV = 8
EPS = 1e-6
NEG = -1e30
LC = 64
TB = 256
ROPE_THETA = 500000.0
VMEM_LIMIT = 56 * 1024 * 1024
HI = lax.Precision.HIGHEST

ADAM_LR, ADAM_B1, ADAM_B2, ADAM_EPS, ADAM_WD, ADAM_STEP = 0.001, 0.9, 0.999, 1e-08, 0.01, 10


def _params(n_grid=0, **kw):
    sem = dict(dimension_semantics=("arbitrary",) * n_grid) if n_grid else {}
    return pltpu.CompilerParams(vmem_limit_bytes=VMEM_LIMIT, **sem, **kw)


def _cspec(shape):
    nd = len(shape)
    return pl.BlockSpec(shape, lambda *_: (0,) * nd, pipeline_mode=pl.Buffered(1))


def _dot(a, b):
    return jnp.dot(a, b, preferred_element_type=F32)


def _dot_nt(a, b):
    return lax.dot_general(a, b, (((1,), (1,)), ((), ())), preferred_element_type=F32)


def _dot_tn(a, b):
    return lax.dot_general(a, b, (((0,), (0,)), ((), ())), preferred_element_type=F32)


def _bf(x):
    return x.astype(BF16)


def _rms(x):
    rs = lax.rsqrt(jnp.mean(x * x, axis=-1, keepdims=True) + EPS)
    return x * rs, rs


def _rms_bwd(du, n, rs, g):
    dn = du * g
    return rs * (dn - n * jnp.mean(dn * n, axis=-1, keepdims=True))


def _sigmoid(x):
    return 1.0 / (1.0 + jnp.exp(-x))


def _rope_tables():
    j = lax.broadcasted_iota(jnp.int32, (S, 128), 1) % 64
    pos = lax.broadcasted_iota(jnp.int32, (S, 128), 0).astype(F32)
    inv_freq = jnp.power(ROPE_THETA, -(j % 8).astype(F32) / 8.0)
    ang = pos * inv_freq
    cos, sin = jnp.cos(ang), jnp.sin(ang)
    c = jnp.where(j < 16, cos, 1.0)
    a = jnp.where(j < 8, -sin, 0.0)
    b = jnp.where((j >= 8) & (j < 16), sin, 0.0)
    return c, a, b


def _rope(blk, c, a, b):
    return blk * c + pltpu.roll(blk, 120, 1) * a + pltpu.roll(blk, 8, 1) * b


def _rope_bwd(d, c, a, b):
    return d * c + pltpu.roll(d * a, 8, 1) + pltpu.roll(d * b, 120, 1)


MESH = pl.DeviceIdType.MESH
ANY = pl.BlockSpec(memory_space=pl.ANY)
VM = pl.BlockSpec(memory_space=pltpu.VMEM)
FLIPS = [(dx, dy, dc) for dx in (0, 1) for dy in (0, 1) for dc in (0, 1)][1:]


def _place():
    return lax.axis_index("x"), lax.axis_index("y"), lax.axis_index("c")


def _dev_index(px, py, pc):
    return 4 * px + 2 * py + pc


def _gather_phases(ins, outs, bufs, send_sems, recv_sems, local_sems):
    nw = len(ins)
    x, y, c = _place()
    me, sib = (x, y, c), (x, y, 1 - c)
    chips = [(1 - x, y), (x, 1 - y), (1 - x, 1 - y)]

    def copy(w, k, block, to, from_buf=False):
        dst = outs[w].at[_dev_index(*block)]
        return pltpu.make_async_remote_copy(
            src_ref=bufs[w] if from_buf else dst, dst_ref=dst, send_sem=send_sems.at[w, k],
            recv_sem=recv_sems.at[w, k], device_id=to, device_id_type=MESH)

    def mine(w):
        return pltpu.make_async_copy(bufs[w], outs[w].at[_dev_index(*me)], local_sems.at[w])

    def first(w):
        return [copy(w, 0, me, sib, True)] + [copy(w, 1 + j, me, (*chip, c), True) for j, chip in enumerate(chips)]

    def passed(w):
        return [copy(w, 4 + j, (*chip, c), sib) for j, chip in enumerate(chips)]

    def start():
        for w in range(nw):
            bufs[w][...] = ins[w][...].astype(bufs[w].dtype)
        for w in range(nw):
            mine(w).start()
            for cp in first(w):
                cp.start()

    def forward():
        for j, chip in enumerate(chips):
            for w in range(nw):
                copy(w, 1 + j, (*chip, c), me).wait_recv()
                passed(w)[j].start()

    def finish():
        for w in range(nw):
            copy(w, 0, sib, me).wait_recv()
        for j, chip in enumerate(chips):
            for w in range(nw):
                copy(w, 4 + j, (*chip, 1 - c), me).wait_recv()
        for w in range(nw):
            for cp in first(w) + passed(w):
                cp.wait_send()
            mine(w).wait()

    return start, forward, finish


def _gather_scratch(shards, dtypes):
    nw = len(shards)
    return ([pltpu.VMEM(s.shape, dt) for s, dt in zip(shards, dtypes)]
            + [pltpu.SemaphoreType.DMA((nw, 7)), pltpu.SemaphoreType.DMA((nw, 7)), pltpu.SemaphoreType.DMA((nw,))])


def _gather_shapes(shards, dtypes):
    return [jax.ShapeDtypeStruct((NDEV, *s.shape), dt) for s, dt in zip(shards, dtypes)]


def _scatter_phases(ins, outs, send_sems, recv_sems, local_sems):
    nw = len(ins)
    x, y, c = _place()
    me = _dev_index(x, y, c)

    def copies():
        out = []
        for w in range(nw):
            out.append(pltpu.make_async_copy(ins[w].at[me], outs[w].at[me], local_sems.at[w]))
            for k, (dx, dy, dc) in enumerate(FLIPS):
                peer = ((x + dx) % 2, (y + dy) % 2, (c + dc) % 2)
                out.append(pltpu.make_async_remote_copy(
                    src_ref=ins[w].at[_dev_index(*peer)], dst_ref=outs[w].at[me], send_sem=send_sems.at[w, k],
                    recv_sem=recv_sems.at[w, k], device_id=peer, device_id_type=MESH))
        return out

    def start():
        for cp in copies():
            cp.start()

    def finish():
        for cp in copies():
            cp.wait()

    return start, finish


def _scatter_scratch(nw):
    return [pltpu.SemaphoreType.DMA((nw, 7)), pltpu.SemaphoreType.DMA((nw, 7)), pltpu.SemaphoreType.DMA((nw,))]


def _in_proj(x, g1, wg, rc, ra, rb):
    tm = 256
    sw = IN_W // NDEV

    def body(x_ref, g_ref, wg_ref, rc_ref, ra_ref, rb_ref, qkv_ref, mqk_ref, mv_ref, mo_ref, gt_ref, u_ref, w_ref):
        @pl.when(pl.program_id(0) == 0)
        def _():
            for j in range(NDEV):
                w_ref[:, sw * j:sw * (j + 1)] = wg_ref[j]
            w_ref[:, IN_W:PW] = jnp.zeros((D, PW - IN_W), BF16)

        n, _ = _rms(x_ref[...])
        u = _bf(n * g_ref[...])
        u_ref[...] = u
        c, a, b = rc_ref[...], ra_ref[...], rb_ref[...]
        for half in range(2):
            blk = _dot(u, w_ref[:, half * 512:(half + 1) * 512])
            for t in range(4):
                lo = half * 512 + t * 128
                qkv_ref[:, lo:lo + 128] = _rope(blk[:, t * 128:(t + 1) * 128], c, a, b)
        qkv_ref[:, 1024:1536] = _dot(u, w_ref[:, 1024:1536])
        mqk_ref[:, 0:512] = _dot(u, w_ref[:, 1536:2048])
        mqk_ref[:, 512:1024] = _dot(u, w_ref[:, 2048:2560])
        mv_ref[...] = _dot(u, w_ref[:, 2560:3072])
        mo_ref[...] = _dot(u, w_ref[:, 3072:3584])
        gt_ref[...] = _dot(u, w_ref[:, 3584:3712])

    row = lambda wd: pl.BlockSpec((tm, wd), lambda i: (i, 0))
    return pl.pallas_call(
        body, name="in_proj", grid=(S // tm,),
        in_specs=[row(D), _cspec((1, D)), _cspec((NDEV, D, sw)), row(128), row(128), row(128)],
        out_specs=[row(1536), row(1024), row(512), row(512), row(128), row(D), pl.BlockSpec((D, PW), lambda i: (0, 0))],
        out_shape=[jax.ShapeDtypeStruct((S, 1536), F32), jax.ShapeDtypeStruct((S, 1024), F32),
                   jax.ShapeDtypeStruct((S, 512), F32), jax.ShapeDtypeStruct((S, 512), F32),
                   jax.ShapeDtypeStruct((S, 128), F32), jax.ShapeDtypeStruct((S, D), BF16),
                   jax.ShapeDtypeStruct((D, PW), BF16)],
        compiler_params=_params(1),
    )(x, g1, wg, rc, ra, rb)


DILATIONS = (1, 4, 16)


def _attn_rows(d, idx):
    nb = S // (128 * d)
    r = idx // nb
    n = idx % nb
    if d == 1:
        q0 = pl.multiple_of(n * 128, 128)
        k0 = pl.multiple_of(jnp.maximum(n - 1, 0) * 128, 128)
        return pl.ds(q0, 128), pl.ds(k0, 256), n
    q0 = r + n * 128 * d
    k0 = r + jnp.maximum(n - 1, 0) * 128 * d
    return pl.ds(q0, 128, stride=d), pl.ds(k0, 256, stride=d), n


def _attn_masks():
    qi = lax.broadcasted_iota(jnp.int32, (128, 256), 0)
    ki = lax.broadcasted_iota(jnp.int32, (128, 256), 1)
    band = (ki - qi >= 0) & (ki - qi <= 128)
    q1 = lax.broadcasted_iota(jnp.int32, (128, 128), 0)
    k1 = lax.broadcasted_iota(jnp.int32, (128, 128), 1)
    causal = k1 <= q1
    head0 = k1 < 64
    return band, causal, head0


def _attn_fwd(qkv, shards, dtypes):
    nw = len(shards)

    def body(*refs):
        q_ref, k_ref, v_ref = refs[:3]
        ins = refs[3:3 + nw]
        o_ref, lse_ref = refs[3 + nw:5 + nw]
        outs = refs[5 + nw:5 + 2 * nw]
        m0, m1, l0, l1, acc = refs[5 + 2 * nw:10 + 2 * nw]
        bufs = refs[10 + 2 * nw:10 + 3 * nw]
        ag_start, ag_forward, ag_finish = _gather_phases(ins, outs, bufs, *refs[10 + 3 * nw:])
        hp = pl.program_id(0)
        pl.when(hp == 0)(ag_start)
        pl.when(hp == 2)(ag_forward)
        band, causal, head0 = _attn_masks()
        for ref in (m0, m1):
            ref[...] = jnp.full((S, 128), NEG, F32)
        for ref in (l0, l1, acc):
            ref[...] = jnp.zeros((S, 128), F32)

        def update(rows_q, rows_k, valid):
            q = q_ref[rows_q, :]
            kb = _bf(k_ref[rows_k, :])
            vb = _bf(v_ref[rows_k, :])
            rep = kb.shape[0] // 128
            acc_prev = acc[rows_q, :]
            new = []
            for qa, m_ref, l_ref in ((_bf(jnp.where(head0, q, 0.0)), m0, l0), (_bf(jnp.where(head0, 0.0, q)), m1, l1)):
                s = jnp.where(valid, _dot_nt(qa, kb) * 0.125, NEG)
                m_prev = m_ref[rows_q, :]
                m_new = jnp.maximum(m_prev, jnp.max(s, axis=-1, keepdims=True))
                p = jnp.exp(s - jnp.tile(m_new, (1, rep)))
                alpha = jnp.exp(m_prev - m_new)
                l_ref[rows_q, :] = alpha * l_ref[rows_q, :] + jnp.sum(p, axis=-1, keepdims=True)
                m_ref[rows_q, :] = m_new
                new.append(alpha * acc_prev + _dot(_bf(p), vb))
            acc[rows_q, :] = jnp.where(head0, new[0], new[1])

        for d in DILATIONS:
            def step(idx, carry, d=d):
                rows_q, rows_k, n = _attn_rows(d, idx)

                @pl.when(n == 0)
                def _():
                    update(rows_q, rows_q, causal)

                @pl.when(n > 0)
                def _():
                    update(rows_q, rows_k, band)
                return carry

            lax.fori_loop(0, 32, step, 0)

        def fin(t, carry):
            rows = pl.ds(pl.multiple_of(t * 256, 256), 256)
            h0 = lax.broadcasted_iota(jnp.int32, (256, 128), 1) < 64
            l = jnp.where(h0, l0[rows, :], l1[rows, :])
            o_ref[rows, :] = acc[rows, :] / l
            lse_ref[rows, :] = jnp.where(h0, m0[rows, :], m1[rows, :]) + jnp.log(l)
            return carry

        lax.fori_loop(0, S // 256, fin, 0)
        pl.when(hp == 3)(ag_finish)

    col = lambda off: pl.BlockSpec((S, 128), lambda h, off=off: (0, off + h))
    res = pl.pallas_call(
        body, name="attn_fwd", grid=(4,),
        in_specs=[col(0), col(4), col(8)] + [VM] * nw,
        out_specs=[col(0), col(0)] + [ANY] * nw,
        out_shape=[jax.ShapeDtypeStruct((S, AW), F32), jax.ShapeDtypeStruct((S, AW), F32)]
        + _gather_shapes(shards, dtypes),
        scratch_shapes=[pltpu.VMEM((S, 128), F32)] * 5 + _gather_scratch(shards, dtypes),
        compiler_params=_params(1),
    )(qkv, qkv, qkv, *shards)
    return res[0], res[1], res[2:]


def _attn_bwd(qkv, o, lse, do, parts):
    nw = len(parts)

    def body(*refs):
        q_ref, k_ref, v_ref, o_ref, lse_ref, do_ref = refs[:6]
        ins = refs[6:6 + nw]
        dq_ref, dk_ref, dv_ref = refs[6 + nw:9 + nw]
        outs = refs[9 + nw:9 + 2 * nw]
        L0, L1, D0, D1 = refs[9 + 2 * nw:13 + 2 * nw]
        rs_start, rs_finish = _scatter_phases(ins, outs, *refs[13 + 2 * nw:])
        hp = pl.program_id(0)
        pl.when(hp == 0)(rs_start)
        band, causal, head0 = _attn_masks()

        def pre(t, carry):
            rows = pl.ds(pl.multiple_of(t * 256, 256), 256)
            h0 = lax.broadcasted_iota(jnp.int32, (256, 128), 1) < 64
            ls = lse_ref[rows, :]
            dd = do_ref[rows, :] * o_ref[rows, :]
            shp = (256, 128)
            L0[rows, :] = jnp.broadcast_to(jnp.max(jnp.where(h0, ls, NEG), axis=-1, keepdims=True), shp)
            L1[rows, :] = jnp.broadcast_to(jnp.max(jnp.where(h0, NEG, ls), axis=-1, keepdims=True), shp)
            D0[rows, :] = jnp.broadcast_to(jnp.sum(jnp.where(h0, dd, 0.0), axis=-1, keepdims=True), shp)
            D1[rows, :] = jnp.broadcast_to(jnp.sum(jnp.where(h0, 0.0, dd), axis=-1, keepdims=True), shp)
            return carry

        lax.fori_loop(0, S // 256, pre, 0)
        for ref in (dq_ref, dk_ref, dv_ref):
            ref[...] = jnp.zeros((S, 128), F32)

        def update(rows_q, rows_k, valid):
            q = q_ref[rows_q, :]
            k = k_ref[rows_k, :]
            dout = do_ref[rows_q, :]
            kb = _bf(k)
            vb = _bf(v_ref[rows_k, :])
            rep = kb.shape[0] // 128
            dq = dk = dv = None
            for a, (L_ref, D_ref) in enumerate(((L0, D0), (L1, D1))):
                lo = lambda t: lax.broadcasted_iota(jnp.int32, t.shape, 1) < 64
                pick = (lambda t: jnp.where(lo(t), t, 0.0)) if a == 0 else (lambda t: jnp.where(lo(t), 0.0, t))
                qa, ka, da = _bf(pick(q)), _bf(pick(k)), _bf(pick(dout))
                s = jnp.where(valid, _dot_nt(qa, kb) * 0.125, NEG)
                p = jnp.exp(s - jnp.tile(L_ref[rows_q, :], (1, rep)))
                dp = _dot_nt(da, vb)
                ds = _bf(p * (dp - jnp.tile(D_ref[rows_q, :], (1, rep))) * 0.125)
                pb = _bf(p)
                tq, tk, tv = _dot(ds, ka), _dot_tn(ds, qa), _dot_tn(pb, da)
                dq, dk, dv = (tq, tk, tv) if a == 0 else (dq + tq, dk + tk, dv + tv)
            dq_ref[rows_q, :] = dq_ref[rows_q, :] + dq
            dk_ref[rows_k, :] = dk_ref[rows_k, :] + dk
            dv_ref[rows_k, :] = dv_ref[rows_k, :] + dv

        for d in DILATIONS:
            def step(idx, carry, d=d):
                rows_q, rows_k, n = _attn_rows(d, idx)

                @pl.when(n == 0)
                def _():
                    update(rows_q, rows_q, causal)

                @pl.when(n > 0)
                def _():
                    update(rows_q, rows_k, band)
                return carry

            lax.fori_loop(0, 32, step, 0)
        pl.when(hp == 3)(rs_finish)

    col = lambda off: pl.BlockSpec((S, 128), lambda h, off=off: (0, off + h))
    res = pl.pallas_call(
        body, name="attn_bwd", grid=(4,),
        in_specs=[col(0), col(4), col(8), col(0), col(0), col(0)] + [ANY] * nw,
        out_specs=[col(0), col(0), col(0)] + [ANY] * nw,
        out_shape=[jax.ShapeDtypeStruct((S, AW), F32)] * 3 + [jax.ShapeDtypeStruct(a.shape, a.dtype) for a in parts],
        scratch_shapes=[pltpu.VMEM((S, 128), F32)] * 4 + _scatter_scratch(nw),
        compiler_params=_params(1),
    )(qkv, qkv, qkv, o, lse, do, *parts)
    return res[0], res[1], res[2], res[3:]


def _logsig(x):
    return jnp.minimum(x, 0.0) - jnp.log1p(jnp.exp(-jnp.abs(x)))


def _conv_taps(xp, n):
    return [xp[8:] if j == 3 else pltpu.roll(xp, 3 - j, 0)[8:] for j in range(4)]


def _conv_silu(xp, w_ref, b_ref, n):
    taps = _conv_taps(xp, n)
    c = b_ref[...] + sum(w_ref[j:j + 1, :] * taps[j] for j in range(4))
    sg = _sigmoid(c)
    return c, sg, taps


def _chunk_gates(G):
    r = lax.broadcasted_iota(jnp.int32, (LC, LC), 0)
    c = lax.broadcasted_iota(jnp.int32, (LC, LC), 1)
    tril = (c <= r).astype(F32)
    triu = (c >= r).astype(F32)
    eye = (c == r).astype(F32)
    logf = _logsig(G)
    b_col = jnp.dot(tril, logf, preferred_element_type=F32, precision=HI)
    b_row = lax.dot_general(logf, triu, (((0,), (0,)), ((), ())), preferred_element_type=F32, precision=HI)
    g_row = lax.dot_general(G, eye, (((0,), (0,)), ((), ())), preferred_element_type=F32, precision=HI)
    return b_col, b_row, g_row, tril, triu


def _colpick(X, lane):
    li = lax.broadcasted_iota(jnp.int32, X.shape, 1)
    return jnp.sum(jnp.where(li == lane, X, 0.0), axis=1, keepdims=True)


def _rowpick(XT, row):
    ri = lax.broadcasted_iota(jnp.int32, XT.shape, 0)
    return jnp.sum(jnp.where(ri == row, XT, 0.0), axis=0, keepdims=True)


def _mlstm_head(qh, kh, vh, G, b_col, b_row, g_row, h, Ch, nh, m_prev):
    bt = _colpick(b_col, 4 + h)
    i_col = _colpick(G, h)
    bs = _rowpick(b_row, 4 + h)
    i_row = _rowpick(g_row, h)
    r = lax.broadcasted_iota(jnp.int32, (LC, LC), 0)
    c = lax.broadcasted_iota(jnp.int32, (LC, LC), 1)
    log_d = jnp.where(c <= r, bt - bs + i_row, NEG)
    log_inter = bt + m_prev
    m_t = jnp.maximum(log_inter, jnp.max(log_d, axis=1, keepdims=True))
    Dm = jnp.exp(log_d - m_t)
    g = jnp.exp(log_inter - m_t)
    qb, kb, vb = _bf(qh), _bf(kh), _bf(vh)
    Am = _dot_nt(qb, kb) * Dm
    qC = _dot(qb, _bf(Ch))
    num = g * qC + _dot(_bf(Am), vb)
    qn = jnp.sum(qh * nh, axis=1, keepdims=True)
    den = g * qn + jnp.sum(Am, axis=1, keepdims=True)
    floor = jnp.exp(-m_t)
    dd = jnp.maximum(jnp.abs(den), floor)
    hh = num / dd
    lane = lax.broadcasted_iota(jnp.int32, (1, LC), 1)
    blast = jnp.sum(jnp.where(lane == LC - 1, bs, 0.0), axis=1, keepdims=True)
    log_s = blast - bt + i_col
    m_new = jnp.maximum(blast + m_prev, jnp.max(log_s, axis=0, keepdims=True))
    decay = jnp.exp(blast + m_prev - m_new)
    ws = jnp.exp(log_s - m_new)
    kw = kh * ws
    C_new = decay * Ch + _dot_tn(_bf(kw), vb)
    n_new = decay * nh + jnp.sum(kw, axis=0, keepdims=True)
    return dict(Dm=Dm, g=g, Am=Am, qC=qC, qn=qn, den=den, floor=floor, dd=dd, h=hh, decay=decay, ws=ws, kw=kw,
                C_new=C_new, n_new=n_new, m_new=m_new, qb=qb, kb=kb, vb=vb)


def _head_out(hh, mo_h, gn_h):
    r = lax.rsqrt(jnp.mean(hh * hh, axis=-1, keepdims=True) + EPS)
    hn = hh * r
    sg = _sigmoid(mo_h)
    return sg * (hn * gn_h), hn, r, sg


def _mlstm_fwd(mqk, mv, mo, gates, conv_w, conv_b, gate_b, gn):
    nblk = S // TB
    ncb = TB // LC

    def body(x_ref, v_ref, o_ref, g_ref, w_ref, b_ref, gb_ref, gn_ref, out_ref, cs_ref, ns_ref, ms_ref,
             tail, Cst, nst, mst, qs, ks):
        i = pl.program_id(0)

        @pl.when(i == 0)
        def _():
            tail[...] = jnp.zeros_like(tail)
            Cst[...] = jnp.zeros_like(Cst)
            nst[...] = jnp.zeros_like(nst)
            mst[...] = jnp.zeros_like(mst)

        x = x_ref[...]
        xp = jnp.concatenate([tail[...], x], axis=0)
        tail[...] = x[TB - 8:TB, :]
        c, sg, _ = _conv_silu(xp, w_ref, b_ref, TB)
        y = c * sg
        qs[...] = y[:, 0:MW]
        ks[...] = y[:, MW:2 * MW] * (1.0 / math.sqrt(128.0))

        for cc in range(ncb):
            rows = slice(cc * LC, (cc + 1) * LC)
            G = g_ref[rows, :] + gb_ref[...]
            b_col, b_row, g_row, _, _ = _chunk_gates(G)
            cs_ref[cc] = Cst[...]
            ns_ref[cc] = nst[...]
            ms_ref[cc] = mst[...]
            for h in range(4):
                ln = slice(h * 128, (h + 1) * 128)
                m_prev = jnp.max(mst[0:1, ln], axis=1, keepdims=True)
                f = _mlstm_head(qs[rows, ln], ks[rows, ln], v_ref[rows, ln], G, b_col, b_row, g_row, h,
                                Cst[:, ln], nst[0:1, ln], m_prev)
                out, _, _, _ = _head_out(f["h"], o_ref[rows, ln], gn_ref[:, ln])
                out_ref[rows, ln] = out
                Cst[:, ln] = f["C_new"]
                nst[0:1, ln] = f["n_new"]
                mst[0:1, ln] = jnp.broadcast_to(f["m_new"], (1, 128))

    row = lambda wd: pl.BlockSpec((TB, wd), lambda i: (i, 0))
    return pl.pallas_call(
        body, name="mlstm_fwd", grid=(nblk,),
        in_specs=[row(1024), row(MW), row(MW), row(128), _cspec((4, 1024)), _cspec((1, 1024)), _cspec((1, 128)),
                  _cspec((1, MW))],
        out_specs=[row(MW), pl.BlockSpec((ncb, 128, MW), lambda i: (i, 0, 0)),
                   pl.BlockSpec((ncb, 8, MW), lambda i: (i, 0, 0)), pl.BlockSpec((ncb, 8, MW), lambda i: (i, 0, 0))],
        out_shape=[jax.ShapeDtypeStruct((S, MW), F32), jax.ShapeDtypeStruct((S // LC, 128, MW), F32),
                   jax.ShapeDtypeStruct((S // LC, 8, MW), F32), jax.ShapeDtypeStruct((S // LC, 8, MW), F32)],
        scratch_shapes=[pltpu.VMEM((8, 1024), F32), pltpu.VMEM((128, MW), F32), pltpu.VMEM((8, MW), F32),
                        pltpu.VMEM((8, MW), F32), pltpu.VMEM((TB, MW), F32), pltpu.VMEM((TB, MW), F32)],
        compiler_params=_params(1),
    )(mqk, mv, mo, gates, conv_w, conv_b, gate_b, gn)


def _mlstm_bwd(mqk, mv, mo, gates, conv_w, conv_b, gate_b, gn, cs, ns, ms, dout):
    nblk = S // TB
    ncb = TB // LC
    kscale = 1.0 / math.sqrt(128.0)

    def body(x_ref, xprev_ref, v_ref, o_ref, g_ref, w_ref, b_ref, gb_ref, gn_ref, cs_ref, ns_ref, ms_ref, do_ref,
             dx_ref, dv_ref, dmo_ref, dg_ref, dw_ref, db_ref, dgn_ref, dgb_ref,
             dCst, dnst, dyhead, qs, ks, dqk):
        i = pl.program_id(0)
        blk = nblk - 1 - i

        @pl.when(i == 0)
        def _():
            dCst[...] = jnp.zeros_like(dCst)
            dnst[...] = jnp.zeros_like(dnst)
            dyhead[...] = jnp.zeros_like(dyhead)
            dw_ref[...] = jnp.zeros_like(dw_ref)
            db_ref[...] = jnp.zeros_like(db_ref)
            dgn_ref[...] = jnp.zeros_like(dgn_ref)
            dgb_ref[...] = jnp.zeros_like(dgb_ref)

        x = x_ref[...]
        xprev = jnp.where(blk == 0, 0.0, xprev_ref[...])
        xp = jnp.concatenate([xprev, x], axis=0)
        c, sg, taps = _conv_silu(xp, w_ref, b_ref, TB)
        y = c * sg
        qs[...] = y[:, 0:MW]
        ks[...] = y[:, MW:2 * MW] * kscale
        lane128 = lax.broadcasted_iota(jnp.int32, (LC, 128), 1)
        rowi = lax.broadcasted_iota(jnp.int32, (LC, 1), 0)
        ones = jnp.ones((LC, 128), F32)

        for cc in reversed(range(ncb)):
            rows = slice(cc * LC, (cc + 1) * LC)
            G = g_ref[rows, :] + gb_ref[...]
            b_col, b_row, g_row, _, triu = _chunk_gates(G)
            dB = jnp.zeros((LC, 128), F32)
            dI = jnp.zeros((LC, 128), F32)
            for h in range(4):
                ln = slice(h * 128, (h + 1) * 128)
                Ch = cs_ref[cc, :, ln]
                nh = ns_ref[cc, 0:1, ln]
                m_prev = jnp.max(ms_ref[cc, 0:1, ln], axis=1, keepdims=True)
                qh, kh, vh = qs[rows, ln], ks[rows, ln], v_ref[rows, ln]
                f = _mlstm_head(qh, kh, vh, G, b_col, b_row, g_row, h, Ch, nh, m_prev)
                hh, dd, den, g, Am, Dm = f["h"], f["dd"], f["den"], f["g"], f["Am"], f["Dm"]
                qb, kb, vb = f["qb"], f["kb"], f["vb"]
                gn_h = gn_ref[:, ln]
                _, hn, r, sgo = _head_out(hh, o_ref[rows, ln], gn_h)
                do = do_ref[rows, ln]
                hm = hn * gn_h
                dmo_ref[rows, ln] = do * hm * sgo * (1.0 - sgo)
                dhm = do * sgo
                dgn_ref[:, ln] = dgn_ref[:, ln] + jnp.sum(dhm * hn, axis=0, keepdims=True)
                dhn = dhm * gn_h
                dh = r * (dhn - hn * jnp.mean(dhn * hn, axis=-1, keepdims=True))
                dnum = dh / dd
                ddd = -jnp.sum(dh * hh, axis=1, keepdims=True) / dd
                dden = jnp.where(jnp.abs(den) >= f["floor"], ddd * jnp.sign(den), 0.0)
                dnb = _bf(dnum)
                dA = _dot_nt(dnb, vb) + dden
                dv = _dot_tn(_bf(Am), dnb)
                gd = _bf(g * dnum)
                gq = g * dden
                dq = _dot_nt(gd, _bf(Ch)) + gq * nh
                dCn = dCst[:, ln]
                dnn = dnst[0:1, ln]
                dC = f["decay"] * dCn + _dot_tn(qb, gd)
                dn = f["decay"] * dnn + jnp.sum(gq * qh, axis=0, keepdims=True)
                dg = jnp.sum(dnum * f["qC"], axis=1, keepdims=True) + dden * f["qn"]
                dS = _bf(dA * Dm)
                dq = dq + _dot(dS, kb)
                dk = _dot_tn(dS, qb)
                Gm = dA * Am
                gam = dg * g
                dCb = _bf(dCn)
                E = _dot_nt(vb, dCb) + dnn
                ws = f["ws"]
                dk = dk + ws * E
                om = jnp.sum(E * kh, axis=1, keepdims=True) * ws
                dv = dv + _dot(_bf(f["kw"]), dCb)
                ddecay = (jnp.sum(jnp.sum(dCn * Ch, axis=1, keepdims=True), axis=0, keepdims=True)
                          + jnp.sum(dnn * nh, axis=1, keepdims=True))
                delta = ddecay * f["decay"]
                rows_g = jnp.sum(Gm, axis=1, keepdims=True)
                cols_g = lax.dot_general(Gm, ones, (((0,), (0,)), ((), ())), preferred_element_type=F32, precision=HI)
                last = jnp.where(rowi == LC - 1, jnp.sum(om, axis=0, keepdims=True) + delta, 0.0)
                db = rows_g + gam - om + last - cols_g
                di = cols_g + om
                dB = dB + jnp.where(lane128 == 4 + h, db, 0.0)
                dI = dI + jnp.where(lane128 == h, di, 0.0)
                dCst[:, ln] = dC
                dnst[0:1, ln] = dn
                dqk[rows, ln] = dq
                dqk[rows, MW + h * 128:MW + (h + 1) * 128] = dk * kscale
                dv_ref[rows, ln] = dv
            dlogf = jnp.dot(triu, dB, preferred_element_type=F32, precision=HI)
            dG = dI + dlogf * _sigmoid(-G)
            dG = jnp.where(lane128 < 8, dG, 0.0)
            dg_ref[rows, :] = dG
            dgb_ref[...] = dgb_ref[...] + jnp.sum(dG, axis=0, keepdims=True)

        dy = dqk[...] * (sg * (1.0 + c * (1.0 - sg)))
        db_ref[...] = db_ref[...] + jnp.sum(dy, axis=0, keepdims=True)
        for j in range(4):
            dw_ref[j:j + 1, :] = dw_ref[j:j + 1, :] + jnp.sum(dy * taps[j], axis=0, keepdims=True)
        dyp = jnp.concatenate([dy, dyhead[...]], axis=0)
        dx = w_ref[3:4, :] * dy
        for j in range(3):
            dx = dx + w_ref[j:j + 1, :] * pltpu.roll(dyp, TB + 8 - (3 - j), 0)[0:TB]
        dx_ref[...] = dx
        dyhead[...] = dy[0:8, :]

    rrow = lambda wd: pl.BlockSpec((TB, wd), lambda i: (nblk - 1 - i, 0))
    st = lambda r: pl.BlockSpec((ncb, r, MW), lambda i: (nblk - 1 - i, 0, 0))
    prev8 = pl.BlockSpec((8, 1024), lambda i: (jnp.maximum((nblk - 1 - i) * (TB // 8) - 1, 0), 0))
    return pl.pallas_call(
        body, name="mlstm_bwd", grid=(nblk,),
        in_specs=[rrow(1024), prev8, rrow(MW), rrow(MW), rrow(128), _cspec((4, 1024)), _cspec((1, 1024)),
                  _cspec((1, 128)), _cspec((1, MW)), st(128), st(8), st(8), rrow(MW)],
        out_specs=[rrow(1024), rrow(MW), rrow(MW), rrow(128),
                   pl.BlockSpec((4, 1024), lambda i: (0, 0)), pl.BlockSpec((1, 1024), lambda i: (0, 0)),
                   pl.BlockSpec((1, MW), lambda i: (0, 0)), pl.BlockSpec((1, 128), lambda i: (0, 0))],
        out_shape=[jax.ShapeDtypeStruct((S, 1024), F32), jax.ShapeDtypeStruct((S, MW), F32),
                   jax.ShapeDtypeStruct((S, MW), F32), jax.ShapeDtypeStruct((S, 128), F32),
                   jax.ShapeDtypeStruct((4, 1024), F32), jax.ShapeDtypeStruct((1, 1024), F32),
                   jax.ShapeDtypeStruct((1, MW), F32), jax.ShapeDtypeStruct((1, 128), F32)],
        scratch_shapes=[pltpu.VMEM((128, MW), F32), pltpu.VMEM((8, MW), F32), pltpu.VMEM((8, 1024), F32),
                        pltpu.VMEM((TB, MW), F32), pltpu.VMEM((TB, MW), F32), pltpu.VMEM((TB, 1024), F32)],
        compiler_params=_params(1),
    )(mqk, mqk, mv, mo, gates, conv_w, conv_b, gate_b, gn, cs, ns, ms, dout)


def _out_proj(x, attn, ml, w, g):
    tm = 256

    def body(x_ref, a_ref, m_ref, w_ref, g_ref, h_ref, u_ref):
        h1 = x_ref[...] + _dot(_bf(a_ref[...]), w_ref[0:AW, :]) + _dot(_bf(m_ref[...]), w_ref[AW:D, :])
        h_ref[...] = h1
        n, _ = _rms(h1)
        u_ref[...] = _bf(n * g_ref[...])

    row = lambda wd: pl.BlockSpec((tm, wd), lambda i: (i, 0))
    return pl.pallas_call(
        body, name="out_proj", grid=(S // tm,),
        in_specs=[row(D), row(AW), row(MW), _cspec((D, D)), _cspec((1, D))],
        out_specs=[row(D), row(D)],
        out_shape=[jax.ShapeDtypeStruct((S, D), F32), jax.ShapeDtypeStruct((S, D), BF16)],
        compiler_params=_params(1),
    )(x, attn, ml, w, g)


def _mlp_fwd(h1, u2, w_up, w_down):
    tm = 256

    def body(h_ref, u_ref, wu_ref, wd_ref, a_ref, o_ref):
        u = u_ref[...]
        acc = h_ref[...]
        for c in range(NDEV):
            cols = slice(c * 512, (c + 1) * 512)
            a = _dot(u, wu_ref[c])
            a_ref[:, cols] = a
            r = jnp.maximum(a, 0.0)
            acc = acc + _dot(_bf(r * r), wd_ref[cols, :])
        o_ref[...] = acc

    row = lambda wd: pl.BlockSpec((tm, wd), lambda i: (i, 0))
    return pl.pallas_call(
        body, name="mlp_fwd", grid=(S // tm,),
        in_specs=[row(D), row(D), _cspec((NDEV, D, DFF // NDEV)), _cspec((DFF, D))],
        out_specs=[row(DFF), row(D)],
        out_shape=[jax.ShapeDtypeStruct((S, DFF), F32), jax.ShapeDtypeStruct((S, D), F32)],
        compiler_params=_params(1),
    )(h1, u2, w_up, w_down)


def _ple_loss(h2, p, target, w_pg, w_ple, g_ple, g_fin):
    tm = 256

    def body(h_ref, p_ref, t_ref, wg_ref, wp_ref, gp_ref, gf_ref,
             dh_ref, dwg_ref, dwp_ref, dgp_ref, dgf_ref, loss_ref, acc_g, acc_p):
        i = pl.program_id(0)

        @pl.when(i == 0)
        def _():
            acc_g[...] = jnp.zeros_like(acc_g)
            acc_p[...] = jnp.zeros_like(acc_p)
            dgp_ref[...] = jnp.zeros_like(dgp_ref)
            dgf_ref[...] = jnp.zeros_like(dgf_ref)
            loss_ref[...] = jnp.zeros_like(loss_ref)

        h2v = h_ref[...]
        n2, rs2 = _rms(h2v)
        u3 = _bf(n2 * gp_ref[...])
        gt = _sigmoid(_dot(u3, wg_ref[...]))
        pb = _bf(p_ref[...])
        e = jnp.concatenate([_dot(pb, wp_ref[j]) for j in range(NDEV)], axis=1)
        h3 = h2v + gt * e
        n3, rs3 = _rms(h3)
        err = n3 * gf_ref[...] - t_ref[...]
        loss_ref[...] = loss_ref[...] + 0.5 / D * jnp.sum(jnp.sum(err * err, axis=1, keepdims=True), axis=0, keepdims=True)
        dy = err * (1.0 / D)
        dgf_ref[...] = dgf_ref[...] + jnp.sum(dy * n3, axis=0, keepdims=True)
        dh3 = _rms_bwd(dy, n3, rs3, gf_ref[...])
        de = _bf(dh3 * gt)
        dz = _bf(dh3 * e * gt * (1.0 - gt))
        acc_p[...] = acc_p[...] + _dot_tn(pb, de)
        acc_g[...] = acc_g[...] + _dot_tn(u3, dz)
        du3 = _dot_nt(dz, wg_ref[...])
        dgp_ref[...] = dgp_ref[...] + jnp.sum(du3 * n2, axis=0, keepdims=True)
        dh_ref[...] = dh3 + _rms_bwd(du3, n2, rs2, gp_ref[...])

        @pl.when(i == S // tm - 1)
        def _():
            dwg_ref[...] = _bf(acc_g[...])
            for j in range(NDEV):
                dwp_ref[j] = _bf(acc_p[:, j * 128:(j + 1) * 128])

    row = lambda wd: pl.BlockSpec((tm, wd), lambda i: (i, 0))
    whole = lambda shp: pl.BlockSpec(shp, lambda i: (0,) * len(shp))
    return pl.pallas_call(
        body, name="ple_loss", grid=(S // tm,),
        in_specs=[row(D), row(PLE), row(D), _cspec((D, D)), _cspec((NDEV, PLE, 128)), _cspec((1, D)), _cspec((1, D))],
        out_specs=[row(D), whole((D, D)), whole((NDEV, PLE, 128)), whole((1, D)), whole((1, D)), whole((1, 1))],
        out_shape=[jax.ShapeDtypeStruct((S, D), F32), jax.ShapeDtypeStruct((D, D), BF16),
                   jax.ShapeDtypeStruct((NDEV, PLE, 128), BF16), jax.ShapeDtypeStruct((1, D), F32),
                   jax.ShapeDtypeStruct((1, D), F32), jax.ShapeDtypeStruct((1, 1), F32)],
        scratch_shapes=[pltpu.VMEM((D, D), F32), pltpu.VMEM((PLE, D), F32)],
        compiler_params=_params(1),
    )(h2, p, target, w_pg, w_ple, g_ple, g_fin)


def _mlp_bwd(dh2, a, h1, g, w_up, w_down):
    tm = 256

    def body(d_ref, a_ref, h_ref, g_ref, wu_ref, wd_ref, da_ref, dh1_ref, dg_ref):
        @pl.when(pl.program_id(0) == 0)
        def _():
            dg_ref[...] = jnp.zeros_like(dg_ref)

        dh2v = d_ref[...]
        db = _bf(dh2v)
        du = jnp.zeros((tm, D), F32)
        for c in range(NDEV):
            cols = slice(c * 512, (c + 1) * 512)
            dr = _dot_nt(db, wd_ref[cols, :])
            da = _bf(dr * (2.0 * jnp.maximum(a_ref[:, cols], 0.0)))
            da_ref[:, cols] = da
            du = du + _dot_nt(da, wu_ref[c])
        n, rs = _rms(h_ref[...])
        dg_ref[...] = dg_ref[...] + jnp.sum(du * n, axis=0, keepdims=True)
        dh1_ref[...] = dh2v + _rms_bwd(du, n, rs, g_ref[...])

    row = lambda wd: pl.BlockSpec((tm, wd), lambda i: (i, 0))
    return pl.pallas_call(
        body, name="mlp_bwd", grid=(S // tm,),
        in_specs=[row(D), row(DFF), row(D), _cspec((1, D)), _cspec((NDEV, D, DFF // NDEV)), _cspec((DFF, D))],
        out_specs=[row(DFF), row(D), pl.BlockSpec((1, D), lambda i: (0, 0))],
        out_shape=[jax.ShapeDtypeStruct((S, DFF), BF16), jax.ShapeDtypeStruct((S, D), F32),
                   jax.ShapeDtypeStruct((1, D), F32)],
        compiler_params=_params(1),
    )(dh2, a, h1, g, w_up, w_down)


def _out_proj_bwd(dh1, attn, ml, w):
    tm = 256

    def body(d_ref, a_ref, m_ref, w_ref, da_ref, dm_ref, dw_ref, acc):
        i = pl.program_id(0)

        @pl.when(i == 0)
        def _():
            acc[...] = jnp.zeros_like(acc)

        db = _bf(d_ref[...])
        dmix = _dot_nt(db, w_ref[...])
        da_ref[...] = dmix[:, 0:AW]
        dm_ref[...] = dmix[:, AW:D]
        acc[0:AW, :] = acc[0:AW, :] + _dot_tn(_bf(a_ref[...]), db)
        acc[AW:D, :] = acc[AW:D, :] + _dot_tn(_bf(m_ref[...]), db)

        @pl.when(i == S // tm - 1)
        def _():
            dw_ref[...] = _bf(acc[...])

    row = lambda wd: pl.BlockSpec((tm, wd), lambda i: (i, 0))
    return pl.pallas_call(
        body, name="out_proj_bwd", grid=(S // tm,),
        in_specs=[row(D), row(AW), row(MW), _cspec((D, D))],
        out_specs=[row(AW), row(MW), pl.BlockSpec((D, D), lambda i: (0, 0))],
        out_shape=[jax.ShapeDtypeStruct((S, AW), F32), jax.ShapeDtypeStruct((S, MW), F32),
                   jax.ShapeDtypeStruct((D, D), BF16)],
        scratch_shapes=[pltpu.VMEM((D, D), F32)],
        compiler_params=_params(1),
    )(dh1, attn, ml, w)


def _in_proj_bwd(dq, dk, dv, dmqk, dmv, dmo, dgt, dh1, x, g1, w, rc, ra, rb):
    tm = 256

    def body(dq_ref, dk_ref, dv_ref, dmqk_ref, dmv_ref, dmo_ref, dgt_ref, dh_ref, x_ref, g_ref, w_ref,
             rc_ref, ra_ref, rb_ref, dp_ref, dx_ref, dg_ref):
        @pl.when(pl.program_id(0) == 0)
        def _():
            dg_ref[...] = jnp.zeros_like(dg_ref)

        c, a, b = rc_ref[...], ra_ref[...], rb_ref[...]
        for half, ref in enumerate((dq_ref, dk_ref)):
            for t in range(4):
                lo = half * 512 + t * 128
                dp_ref[:, lo:lo + 128] = _bf(_rope_bwd(ref[:, t * 128:(t + 1) * 128], c, a, b))
        dp_ref[:, 1024:1536] = _bf(dv_ref[...])
        dp_ref[:, 1536:2560] = _bf(dmqk_ref[...])
        dp_ref[:, 2560:3072] = _bf(dmv_ref[...])
        dp_ref[:, 3072:3584] = _bf(dmo_ref[...])
        dp_ref[:, 3584:3712] = _bf(dgt_ref[...])
        dp_ref[:, 3712:PW] = jnp.zeros((tm, PW - 3712), BF16)
        du = jnp.zeros((tm, D), F32)
        for s in range(PW // 768):
            cols = slice(s * 768, (s + 1) * 768)
            du = du + _dot_nt(dp_ref[:, cols], w_ref[:, cols])
        n, rs = _rms(x_ref[...])
        dg_ref[...] = dg_ref[...] + jnp.sum(du * n, axis=0, keepdims=True)
        dx_ref[...] = dh_ref[...] + _rms_bwd(du, n, rs, g_ref[...])

    row = lambda wd: pl.BlockSpec((tm, wd), lambda i: (i, 0))
    return pl.pallas_call(
        body, name="in_proj_bwd", grid=(S // tm,),
        in_specs=[row(AW), row(AW), row(AW), row(1024), row(MW), row(MW), row(128), row(D), row(D), _cspec((1, D)),
                  _cspec((D, PW)), row(128), row(128), row(128)],
        out_specs=[row(PW), row(D), pl.BlockSpec((1, D), lambda i: (0, 0))],
        out_shape=[jax.ShapeDtypeStruct((S, PW), BF16), jax.ShapeDtypeStruct((S, D), F32),
                   jax.ShapeDtypeStruct((1, D), F32)],
        compiler_params=_params(1),
    )(dq, dk, dv, dmqk, dmv, dmo, dgt, dh1, x, g1, w, rc, ra, rb)


def _wgrad(name, A, B, a_fn, b_fn, tk, tn, out_shape, out_spec, ts=512, split=None):
    K, N = A.shape[1], B.shape[1]
    nrt = S // ts

    def body(a_ref, b_ref, o_ref, acc):
        r = pl.program_id(2)

        @pl.when(r == 0)
        def _():
            acc[...] = jnp.zeros_like(acc)

        acc[...] = acc[...] + _dot_tn(a_fn(a_ref[...]), b_fn(b_ref[...]))

        @pl.when(r == nrt - 1)
        def _():
            if split is None:
                o_ref[...] = _bf(acc[...])
            else:
                for j in range(NDEV):
                    o_ref[j] = _bf(acc[:, split * j:split * (j + 1)])

    return pl.pallas_call(
        body, name=name, grid=(N // tn, K // tk, nrt),
        in_specs=[pl.BlockSpec((ts, tk), lambda n, k, r: (r, k)), pl.BlockSpec((ts, tn), lambda n, k, r: (r, n))],
        out_specs=out_spec,
        out_shape=jax.ShapeDtypeStruct(out_shape, BF16),
        scratch_shapes=[pltpu.VMEM((tk, tn), F32)],
        compiler_params=_params(3),
    )(A, B)


def _relu2_bf(a):
    r = jnp.maximum(a, 0.0)
    return _bf(r * r)


def _ident(a):
    return a


def _step(x, p, target, g1, conv_b, gate_b, gn, g_mlp, g_ple, g_fin, sh):
    late = ("w_out", "w_up", "w_down", "w_ple_gate", "w_ple")
    g_in, g_conv = _gather_weights([sh["w_in"], sh["conv_w"]], [BF16, F32])
    conv_w = g_conv.transpose(1, 0, 2).reshape(4, 1024)
    rc, ra, rb = _rope_tables()
    qkv, mqk, mv, mo, gates, u1, w_in_p = _in_proj(x, g1, g_in, rc, ra, rb)
    attn, lse, (w_out8, w_up8, w_down8, w_pg8, w_ple8) = _attn_fwd(qkv, [sh[n] for n in late], [BF16] * 5)
    w_out, w_down, w_pg = w_out8.reshape(D, D), w_down8.reshape(DFF, D), w_pg8.reshape(D, D)
    ml, cs, ns, ms = _mlstm_fwd(mqk, mv, mo, gates, conv_w, conv_b, gate_b, gn)
    h1, u2 = _out_proj(x, attn, ml, w_out, g_mlp)
    a, h2 = _mlp_fwd(h1, u2, w_up8, w_down)
    dh2, dw_pg, dw_ple8, dg_ple, dg_fin, loss = _ple_loss(h2, p, target, w_pg, w_ple8, g_ple, g_fin)
    da, dh1, dg_mlp = _mlp_bwd(dh2, a, h1, g_mlp, w_up8, w_down)
    dw_up8 = _wgrad("wgrad_up", u2, da, _ident, _ident, D, 512, (NDEV, D, 512),
                    pl.BlockSpec((None, D, 512), lambda n, k, r: (n, 0, 0)))
    dw_down = _wgrad("wgrad_down", a, dh2, _relu2_bf, _bf, 1024, 1024, (DFF, D),
                     pl.BlockSpec((1024, 1024), lambda n, k, r: (k, n)))
    d_attn, d_ml, dw_out = _out_proj_bwd(dh1, attn, ml, w_out)
    dmqk, dmv, dmo, dgt, dconv_w, dconv_b, dgn, dgate_b = _mlstm_bwd(
        mqk, mv, mo, gates, conv_w, conv_b, gate_b, gn, cs, ns, ms, d_ml)
    parts = [dw_out.reshape(NDEV, D // NDEV, D), dw_up8, dw_down.reshape(NDEV, DFF // NDEV, D),
             dw_pg.reshape(NDEV, D // NDEV, D), dw_ple8]
    dq, dk, dv, recv_late = _attn_bwd(qkv, attn, lse, d_attn, parts)
    dproj, dx, dg1 = _in_proj_bwd(dq, dk, dv, dmqk, dmv, dmo, dgt, dh1, x, g1, w_in_p, rc, ra, rb)
    dw_in8 = _wgrad("wgrad_in", u1, dproj, _ident, _ident, D, PW, (NDEV, D, IN_W // NDEV),
                    pl.BlockSpec((NDEV, D, IN_W // NDEV), lambda n, k, r: (0, 0, 0)), split=IN_W // NDEV)
    recv_in, recv_conv = _scatter_grads([dw_in8, dconv_w.reshape(4, NDEV, 128).transpose(1, 0, 2)])
    recv = dict(zip(late, recv_late), w_in=recv_in, conv_w=recv_conv)
    small = dict(norm_mix_g=dg1, conv_b=dconv_b, gate_b=dgate_b, mlstm_norm_g=dgn, norm_mlp_g=dg_mlp,
                 norm_ple_g=dg_ple, final_norm_g=dg_fin)
    return loss, dx, recv, small


def _gather_weights(shards, dtypes):
    nw = len(shards)

    def body(*refs):
        start, forward, finish = _gather_phases(refs[:nw], refs[nw:2 * nw], refs[2 * nw:3 * nw], *refs[3 * nw:])
        start()
        forward()
        finish()

    return pl.pallas_call(
        body, name="gather_weights",
        in_specs=[VM] * nw, out_specs=[ANY] * nw,
        out_shape=_gather_shapes(shards, dtypes),
        scratch_shapes=_gather_scratch(shards, dtypes),
        compiler_params=_params(),
    )(*shards)


def _scatter_grads(parts):
    nw = len(parts)

    def body(*refs):
        start, finish = _scatter_phases(refs[:nw], refs[nw:2 * nw], *refs[2 * nw:])
        start()
        finish()

    return pl.pallas_call(
        body, name="scatter_grads",
        in_specs=[ANY] * nw, out_specs=[ANY] * nw,
        out_shape=[jax.ShapeDtypeStruct(a.shape, a.dtype) for a in parts],
        scratch_shapes=_scatter_scratch(nw),
        compiler_params=_params(),
    )(*parts)


SMALL_ROWS = 64


def _allreduce_small(vals):
    nv = len(vals)

    def body(*refs):
        ins, out_ref = refs[:nv], refs[nv]
        pack, rbuf, send_sems, recv_sems = refs[nv + 1:]
        x, y, c = _place()
        me = _dev_index(x, y, c)
        pack[...] = jnp.zeros_like(pack)
        for i in range(nv):
            pack[8 * i:8 * i + 1, 0:ins[i].shape[1]] = ins[i][...]
        rbuf[me] = pack[...]
        copies = []
        for k, (dx, dy, dc) in enumerate(FLIPS):
            peer = ((x + dx) % 2, (y + dy) % 2, (c + dc) % 2)
            cp = pltpu.make_async_remote_copy(
                src_ref=pack, dst_ref=rbuf.at[me], send_sem=send_sems.at[k], recv_sem=recv_sems.at[k],
                device_id=peer, device_id_type=MESH)
            cp.start()
            copies.append(cp)
        for cp in copies:
            cp.wait()
        tot = rbuf[0]
        for j in range(1, NDEV):
            tot = tot + rbuf[j]
        out_ref[...] = tot

    return pl.pallas_call(
        body, name="allreduce_small",
        in_specs=[VM] * nv, out_specs=VM,
        out_shape=jax.ShapeDtypeStruct((SMALL_ROWS, 1024), F32),
        scratch_shapes=[pltpu.VMEM((SMALL_ROWS, 1024), F32), pltpu.VMEM((NDEV, SMALL_ROWS, 1024), F32),
                        pltpu.SemaphoreType.DMA((7,)), pltpu.SemaphoreType.DMA((7,))],
        compiler_params=_params(),
    )(*vals)


def _adamw(name, gparts, w, m, v, tr):
    P, R, C = gparts.shape
    c1 = 1.0 - ADAM_B1 ** ADAM_STEP
    c2 = 1.0 - ADAM_B2 ** ADAM_STEP

    def body(g_ref, w_ref, m_ref, v_ref, go_ref, d_ref, mo_ref, vo_ref):
        g = g_ref[0].astype(F32)
        for j in range(1, P):
            g = g + g_ref[j].astype(F32)
        m2 = ADAM_B1 * m_ref[...] + (1.0 - ADAM_B1) * g
        v2 = ADAM_B2 * v_ref[...] + (1.0 - ADAM_B2) * (g * g)
        go_ref[...] = g
        mo_ref[...] = m2
        vo_ref[...] = v2
        d_ref[...] = -ADAM_LR * ((m2 / c1) / (jnp.sqrt(v2 / c2) + ADAM_EPS) + ADAM_WD * w_ref[...])

    row = pl.BlockSpec((tr, C), lambda i: (i, 0))
    return pl.pallas_call(
        body, name=name, grid=(R // tr,),
        in_specs=[pl.BlockSpec((P, tr, C), lambda i: (0, i, 0)), row, row, row],
        out_specs=[row] * 4,
        out_shape=[jax.ShapeDtypeStruct((R, C), F32)] * 4,
        compiler_params=_params(1),
    )(gparts, w, m, v)


SMALL = ("norm_mix_g", "conv_b", "gate_b", "mlstm_norm_g", "norm_mlp_g", "norm_ple_g", "final_norm_g")


def _pack_small(vals):
    return jnp.concatenate([jnp.pad(a, ((0, 7), (0, 1024 - a.shape[1]))) for a in vals], axis=0)


def kernel(x, p, norm_mix_g, w_in, conv_w, conv_b, gate_b, mlstm_norm_g, w_out, norm_mlp_g, w_up, w_down, norm_ple_g, w_ple_gate, w_ple, final_norm_g, loss_target, m_norm_mix_g, m_w_in, m_conv_w, m_conv_b, m_gate_b, m_mlstm_norm_g, m_w_out, m_norm_mlp_g, m_w_up, m_w_down, m_norm_ple_g, m_w_ple_gate, m_w_ple, m_final_norm_g, v_norm_mix_g, v_w_in, v_conv_w, v_conv_b, v_gate_b, v_mlstm_norm_g, v_w_out, v_norm_mlp_g, v_w_up, v_w_down, v_norm_ple_g, v_w_ple_gate, v_w_ple, v_final_norm_g):
    big_names = ("w_in", "conv_w", "w_out", "w_up", "w_down", "w_ple_gate", "w_ple")
    wts = dict(w_in=w_in, conv_w=conv_w, w_out=w_out, w_up=w_up, w_down=w_down, w_ple_gate=w_ple_gate, w_ple=w_ple)
    mom = dict(w_in=m_w_in, conv_w=m_conv_w, w_out=m_w_out, w_up=m_w_up, w_down=m_w_down, w_ple_gate=m_w_ple_gate,
               w_ple=m_w_ple)
    var = dict(w_in=v_w_in, conv_w=v_conv_w, w_out=v_w_out, w_up=v_w_up, w_down=v_w_down, w_ple_gate=v_w_ple_gate,
               w_ple=v_w_ple)
    sq = lambda a: a.reshape(a.shape[1:])
    fin = final_norm_g.reshape(1, D)
    loss, dx, recv, small = _step(
        x[0], p[0, 0], loss_target[0], norm_mix_g, conv_b, jnp.pad(gate_b, ((0, 0), (0, 120))), mlstm_norm_g,
        norm_mlp_g, norm_ple_g, fin, {n: sq(wts[n]) for n in big_names})
    total = _allreduce_small([small[n] for n in SMALL] + [loss])

    out = {}
    for n, tr in zip(big_names, (256, 4, 128, 256, 256, 128, 256)):
        res = _adamw("adamw_" + n, recv[n], sq(wts[n]), sq(mom[n]), sq(var[n]), tr)
        out[n] = [t.reshape(wts[n].shape) for t in res]
    sw = dict(norm_mix_g=norm_mix_g, conv_b=conv_b, gate_b=gate_b, mlstm_norm_g=mlstm_norm_g, norm_mlp_g=norm_mlp_g,
              norm_ple_g=norm_ple_g, final_norm_g=fin)
    sm = dict(norm_mix_g=m_norm_mix_g, conv_b=m_conv_b, gate_b=m_gate_b, mlstm_norm_g=m_mlstm_norm_g,
              norm_mlp_g=m_norm_mlp_g, norm_ple_g=m_norm_ple_g, final_norm_g=m_final_norm_g.reshape(1, D))
    sv = dict(norm_mix_g=v_norm_mix_g, conv_b=v_conv_b, gate_b=v_gate_b, mlstm_norm_g=v_mlstm_norm_g,
              norm_mlp_g=v_norm_mlp_g, norm_ple_g=v_norm_ple_g, final_norm_g=v_final_norm_g.reshape(1, D))
    nrow = 8 * len(SMALL)
    res = _adamw("adamw_small", total[0:nrow].reshape(1, nrow, 1024), _pack_small([sw[n] for n in SMALL]),
                 _pack_small([sm[n] for n in SMALL]), _pack_small([sv[n] for n in SMALL]), nrow)
    for i, n in enumerate(SMALL):
        shp = final_norm_g.shape if n == "final_norm_g" else sw[n].shape
        out[n] = [t[8 * i, 0:sw[n].shape[1]].reshape(shp) for t in res]
    order = ("norm_mix_g", "w_in", "conv_w", "conv_b", "gate_b", "mlstm_norm_g", "w_out", "norm_mlp_g", "w_up", "w_down",
             "norm_ple_g", "w_ple_gate", "w_ple", "final_norm_g")
    loss_all = total[nrow, 0]
    return (loss_all, dx[None], *[out[n][0] for n in order], *[out[n][1] for n in order],
            *[out[n][2] for n in order], *[out[n][3] for n in order])
```

```python
import functools
import math

import jax
import jax.numpy as jnp
from jax import lax
from jax.experimental import pallas as pl
from jax.experimental.pallas import tpu as pltpu

F32, BF16 = jnp.float32, jnp.bfloat16
S = 4096
D = 1024
AW = 512
MW = 512
DFF = 4096
PLE = 256
IN_W = 3592
PW = 3840
NDEV = 8
EPS = 1e-6
NEG = -1e30
LC = 64
TB = 256
ROPE_THETA = 500000.0
VMEM_LIMIT = 56 * 1024 * 1024
HI = lax.Precision.HIGHEST

ADAM_LR, ADAM_B1, ADAM_B2, ADAM_EPS, ADAM_WD, ADAM_STEP = 0.001, 0.9, 0.999, 1e-08, 0.01, 10


def _params(n_grid=0, **kw):
    sem = dict(dimension_semantics=("arbitrary",) * n_grid) if n_grid else {}
    return pltpu.CompilerParams(vmem_limit_bytes=VMEM_LIMIT, **sem, **kw)


def _cspec(shape):
    nd = len(shape)
    return pl.BlockSpec(shape, lambda *_: (0,) * nd, pipeline_mode=pl.Buffered(1))


def _dot(a, b):
    return jnp.dot(a, b, preferred_element_type=F32)


def _dot_nt(a, b):
    return lax.dot_general(a, b, (((1,), (1,)), ((), ())), preferred_element_type=F32)


def _dot_tn(a, b):
    return lax.dot_general(a, b, (((0,), (0,)), ((), ())), preferred_element_type=F32)


def _bf(x):
    return x.astype(BF16)


def _rms(x):
    rs = lax.rsqrt(jnp.mean(x * x, axis=-1, keepdims=True) + EPS)
    return x * rs, rs


def _rms_bwd(du, n, rs, g):
    dn = du * g
    return rs * (dn - n * jnp.mean(dn * n, axis=-1, keepdims=True))


def _sigmoid(x):
    return 1.0 / (1.0 + jnp.exp(-x))


def _rope_tables():
    j = lax.broadcasted_iota(jnp.int32, (S, 128), 1) % 64
    pos = lax.broadcasted_iota(jnp.int32, (S, 128), 0).astype(F32)
    inv_freq = jnp.power(ROPE_THETA, -(j % 8).astype(F32) / 8.0)
    ang = pos * inv_freq
    cos, sin = jnp.cos(ang), jnp.sin(ang)
    c = jnp.where(j < 16, cos, 1.0)
    a = jnp.where(j < 8, -sin, 0.0)
    b = jnp.where((j >= 8) & (j < 16), sin, 0.0)
    return c, a, b


def _rope(blk, c, a, b):
    return blk * c + pltpu.roll(blk, 120, 1) * a + pltpu.roll(blk, 8, 1) * b


def _rope_bwd(d, c, a, b):
    return d * c + pltpu.roll(d * a, 8, 1) + pltpu.roll(d * b, 120, 1)


MESH = pl.DeviceIdType.MESH
ANY = pl.BlockSpec(memory_space=pl.ANY)
VM = pl.BlockSpec(memory_space=pltpu.VMEM)
FLIPS = [(dx, dy, dc) for dx in (0, 1) for dy in (0, 1) for dc in (0, 1)][1:]


def _place():
    return lax.axis_index("x"), lax.axis_index("y"), lax.axis_index("c")


def _dev_index(px, py, pc):
    return 4 * px + 2 * py + pc


def _gather_phases(ins, outs, bufs, send_sems, recv_sems, local_sems):
    nw = len(ins)
    x, y, c = _place()
    me, sib = (x, y, c), (x, y, 1 - c)
    chips = [(1 - x, y), (x, 1 - y), (1 - x, 1 - y)]

    def copy(w, k, block, to, from_buf=False):
        dst = outs[w].at[_dev_index(*block)]
        return pltpu.make_async_remote_copy(
            src_ref=bufs[w] if from_buf else dst, dst_ref=dst, send_sem=send_sems.at[w, k],
            recv_sem=recv_sems.at[w, k], device_id=to, device_id_type=MESH)

    def mine(w):
        return pltpu.make_async_copy(bufs[w], outs[w].at[_dev_index(*me)], local_sems.at[w])

    def first(w):
        return [copy(w, 0, me, sib, True)] + [copy(w, 1 + j, me, (*chip, c), True) for j, chip in enumerate(chips)]

    def passed(w):
        return [copy(w, 4 + j, (*chip, c), sib) for j, chip in enumerate(chips)]

    def start():
        for w in range(nw):
            bufs[w][...] = ins[w][...].astype(bufs[w].dtype)
        for w in range(nw):
            mine(w).start()
            for cp in first(w):
                cp.start()

    def forward():
        for j, chip in enumerate(chips):
            for w in range(nw):
                copy(w, 1 + j, (*chip, c), me).wait_recv()
                passed(w)[j].start()

    def finish():
        for w in range(nw):
            copy(w, 0, sib, me).wait_recv()
        for j, chip in enumerate(chips):
            for w in range(nw):
                copy(w, 4 + j, (*chip, 1 - c), me).wait_recv()
        for w in range(nw):
            for cp in first(w) + passed(w):
                cp.wait_send()
            mine(w).wait()

    return start, forward, finish


def _gather_scratch(shards, dtypes):
    nw = len(shards)
    return ([pltpu.VMEM(s.shape, dt) for s, dt in zip(shards, dtypes)]
            + [pltpu.SemaphoreType.DMA((nw, 7)), pltpu.SemaphoreType.DMA((nw, 7)), pltpu.SemaphoreType.DMA((nw,))])


def _gather_shapes(shards, dtypes):
    return [jax.ShapeDtypeStruct((NDEV, *s.shape), dt) for s, dt in zip(shards, dtypes)]


def _scatter_phases(ins, outs, send_sems, recv_sems, local_sems):
    nw = len(ins)
    x, y, c = _place()
    me = _dev_index(x, y, c)

    def copies():
        out = []
        for w in range(nw):
            out.append(pltpu.make_async_copy(ins[w].at[me], outs[w].at[me], local_sems.at[w]))
            for k, (dx, dy, dc) in enumerate(FLIPS):
                peer = ((x + dx) % 2, (y + dy) % 2, (c + dc) % 2)
                out.append(pltpu.make_async_remote_copy(
                    src_ref=ins[w].at[_dev_index(*peer)], dst_ref=outs[w].at[me], send_sem=send_sems.at[w, k],
                    recv_sem=recv_sems.at[w, k], device_id=peer, device_id_type=MESH))
        return out

    def start():
        for cp in copies():
            cp.start()

    def finish():
        for cp in copies():
            cp.wait()

    return start, finish


def _scatter_scratch(nw):
    return [pltpu.SemaphoreType.DMA((nw, 7)), pltpu.SemaphoreType.DMA((nw, 7)), pltpu.SemaphoreType.DMA((nw,))]


def _in_proj(x, g1, wg, rc, ra, rb):
    tm = 256
    sw = IN_W // NDEV

    def body(x_ref, g_ref, wg_ref, rc_ref, ra_ref, rb_ref, qkv_ref, mqk_ref, mv_ref, mo_ref, gt_ref, u_ref, w_ref):
        @pl.when(pl.program_id(0) == 0)
        def _():
            for j in range(NDEV):
                w_ref[:, sw * j:sw * (j + 1)] = wg_ref[j]
            w_ref[:, IN_W:PW] = jnp.zeros((D, PW - IN_W), BF16)

        n, _ = _rms(x_ref[...])
        u = _bf(n * g_ref[...])
        u_ref[...] = u
        c, a, b = rc_ref[...], ra_ref[...], rb_ref[...]
        for half in range(2):
            blk = _dot(u, w_ref[:, half * 512:(half + 1) * 512])
            for t in range(4):
                lo = half * 512 + t * 128
                qkv_ref[:, lo:lo + 128] = _rope(blk[:, t * 128:(t + 1) * 128], c, a, b)
        qkv_ref[:, 1024:1536] = _dot(u, w_ref[:, 1024:1536])
        mqk_ref[:, 0:512] = _dot(u, w_ref[:, 1536:2048])
        mqk_ref[:, 512:1024] = _dot(u, w_ref[:, 2048:2560])
        mv_ref[...] = _dot(u, w_ref[:, 2560:3072])
        mo_ref[...] = _dot(u, w_ref[:, 3072:3584])
        gt_ref[...] = _dot(u, w_ref[:, 3584:3712])

    row = lambda wd: pl.BlockSpec((tm, wd), lambda i: (i, 0))
    return pl.pallas_call(
        body, name="in_proj", grid=(S // tm,),
        in_specs=[row(D), _cspec((1, D)), _cspec((NDEV, D, sw)), row(128), row(128), row(128)],
        out_specs=[row(1536), row(1024), row(512), row(512), row(128), row(D), pl.BlockSpec((D, PW), lambda i: (0, 0))],
        out_shape=[jax.ShapeDtypeStruct((S, 1536), F32), jax.ShapeDtypeStruct((S, 1024), F32),
                   jax.ShapeDtypeStruct((S, 512), F32), jax.ShapeDtypeStruct((S, 512), F32),
                   jax.ShapeDtypeStruct((S, 128), F32), jax.ShapeDtypeStruct((S, D), BF16),
                   jax.ShapeDtypeStruct((D, PW), BF16)],
        compiler_params=_params(1),
    )(x, g1, wg, rc, ra, rb)


DILATIONS = (1, 4, 16)


def _attn_rows(d, r, n):
    if d == 1:
        q0 = pl.multiple_of(n * 128, 128)
        k0 = pl.multiple_of(jnp.maximum(n - 1, 0) * 128, 128)
        return pl.ds(q0, 128), pl.ds(k0, 256), n
    q0 = r + n * 128 * d
    k0 = r + jnp.maximum(n - 1, 0) * 128 * d
    return pl.ds(q0, 128, stride=d), pl.ds(k0, 256, stride=d), n


ATTN_GROUP = 4
ATTN_ITERS = S // 128 // ATTN_GROUP


def _attn_group(d, i):
    nb = S // (128 * d)
    if d == 1:
        return [_attn_rows(1, 0, i + ATTN_ITERS * u) for u in range(ATTN_GROUP)]
    return [_attn_rows(d, (i // nb) * ATTN_GROUP + u, i % nb) for u in range(ATTN_GROUP)]


def _attn_valid(n):
    kd = lax.broadcasted_iota(jnp.int32, (128, 256), 1) - lax.broadcasted_iota(jnp.int32, (128, 256), 0)
    off = jnp.where(n == 0, 0, 128)
    return (kd <= off) & (kd >= off - 128)


def _head0(shape):
    return lax.broadcasted_iota(jnp.int32, shape, 1) < 64


def _attn_fwd(qkv, shards, dtypes):
    nw = len(shards)

    def body(*refs):
        q_ref, k_ref, v_ref = refs[:3]
        ins = refs[3:3 + nw]
        o_ref, lse_ref = refs[3 + nw:5 + nw]
        outs = refs[5 + nw:5 + 2 * nw]
        m0, m1, l0, l1, acc = refs[5 + 2 * nw:10 + 2 * nw]
        bufs = refs[10 + 2 * nw:10 + 3 * nw]
        ag_start, ag_forward, ag_finish = _gather_phases(ins, outs, bufs, *refs[10 + 3 * nw:])
        hp = pl.program_id(0)
        pl.when(hp == 0)(ag_start)
        pl.when(hp == 2)(ag_forward)
        head0 = _head0((128, 128))
        for ref in (m0, m1):
            ref[...] = jnp.full((S, 128), NEG, F32)
        for ref in (l0, l1, acc):
            ref[...] = jnp.zeros((S, 128), F32)
        stats = (m0, m1, l0, l1, acc)

        def update(blocks):
            loaded = [([q_ref[rq, :], k_ref[rk, :], v_ref[rk, :]], [ref[rq, :] for ref in stats])
                      for rq, rk, _ in blocks]
            results = []
            for ((q, k, v), (mp0, mp1, lp0, lp1, acc_prev)), (_, _, n) in zip(loaded, blocks):
                valid = _attn_valid(n)
                kb, vb = _bf(k), _bf(v)
                q = q * 0.125
                new = []
                for qa, m_prev, l_prev in ((_bf(jnp.where(head0, q, 0.0)), mp0, lp0),
                                           (_bf(jnp.where(head0, 0.0, q)), mp1, lp1)):
                    s = jnp.where(valid, _dot_nt(qa, kb), NEG)
                    m_new = jnp.maximum(m_prev, jnp.max(s, axis=-1, keepdims=True))
                    p = jnp.exp(s - jnp.tile(m_new, (1, 2)))
                    alpha = jnp.exp(m_prev - m_new)
                    new += [m_new, alpha * l_prev + jnp.sum(p, axis=-1, keepdims=True),
                            alpha * acc_prev + _dot(_bf(p), vb)]
                results.append((new[0], new[3], new[1], new[4], jnp.where(head0, new[2], new[5])))
            for (rq, _, _), res in zip(blocks, results):
                for ref, val in zip(stats, res):
                    ref[rq, :] = val

        for d in DILATIONS:
            def step(i, carry, d=d):
                update(_attn_group(d, i))
                return carry

            lax.fori_loop(0, ATTN_ITERS, step, 0)

        def fin(t, carry):
            rows = pl.ds(pl.multiple_of(t * 256, 256), 256)
            h0 = lax.broadcasted_iota(jnp.int32, (256, 128), 1) < 64
            l = jnp.where(h0, l0[rows, :], l1[rows, :])
            o_ref[rows, :] = acc[rows, :] / l
            lse_ref[rows, :] = jnp.where(h0, m0[rows, :], m1[rows, :]) + jnp.log(l)
            return carry

        lax.fori_loop(0, S // 256, fin, 0)
        pl.when(hp == 3)(ag_finish)

    col = lambda off: pl.BlockSpec((S, 128), lambda h, off=off: (0, off + h))
    res = pl.pallas_call(
        body, name="attn_fwd", grid=(4,),
        in_specs=[col(0), col(4), col(8)] + [VM] * nw,
        out_specs=[col(0), col(0)] + [ANY] * nw,
        out_shape=[jax.ShapeDtypeStruct((S, AW), F32), jax.ShapeDtypeStruct((S, AW), F32)]
        + _gather_shapes(shards, dtypes),
        scratch_shapes=[pltpu.VMEM((S, 128), F32)] * 5 + _gather_scratch(shards, dtypes),
        compiler_params=_params(1),
    )(qkv, qkv, qkv, *shards)
    return res[0], res[1], res[2:]


def _attn_bwd(qkv, o, lse, do, parts):
    nw = len(parts)

    def body(*refs):
        q_ref, k_ref, v_ref, o_ref, lse_ref, do_ref = refs[:6]
        ins = refs[6:6 + nw]
        dq_ref, dk_ref, dv_ref = refs[6 + nw:9 + nw]
        outs = refs[9 + nw:9 + 2 * nw]
        L0, L1, D0, D1 = refs[9 + 2 * nw:13 + 2 * nw]
        rs_start, rs_finish = _scatter_phases(ins, outs, *refs[13 + 2 * nw:])
        hp = pl.program_id(0)
        pl.when(hp == 0)(rs_start)
        def pre(t, carry):
            rows = pl.ds(pl.multiple_of(t * 256, 256), 256)
            h0 = lax.broadcasted_iota(jnp.int32, (256, 128), 1) < 64
            ls = lse_ref[rows, :]
            dd = do_ref[rows, :] * o_ref[rows, :]
            shp = (256, 128)
            L0[rows, :] = jnp.broadcast_to(jnp.max(jnp.where(h0, ls, NEG), axis=-1, keepdims=True), shp)
            L1[rows, :] = jnp.broadcast_to(jnp.max(jnp.where(h0, NEG, ls), axis=-1, keepdims=True), shp)
            D0[rows, :] = jnp.broadcast_to(jnp.sum(jnp.where(h0, dd, 0.0), axis=-1, keepdims=True), shp)
            D1[rows, :] = jnp.broadcast_to(jnp.sum(jnp.where(h0, 0.0, dd), axis=-1, keepdims=True), shp)
            return carry

        lax.fori_loop(0, S // 256, pre, 0)
        for ref in (dq_ref, dk_ref, dv_ref):
            ref[...] = jnp.zeros((S, 128), F32)

        def update(blocks):
            loaded = [([q_ref[rq, :], k_ref[rk, :], v_ref[rk, :], do_ref[rq, :]],
                       [L0[rq, :], L1[rq, :], D0[rq, :], D1[rq, :]],
                       [dq_ref[rq, :], dk_ref[rk, :], dv_ref[rk, :]]) for rq, rk, _ in blocks]
            results = []
            for ((q, k, v, dout), (l0v, l1v, d0v, d1v), (dq, dk, dv)), (_, _, n) in zip(loaded, blocks):
                valid = _attn_valid(n)
                kb, vb = _bf(k), _bf(v)
                for a, (lse_a, delta_a) in enumerate(((l0v, d0v), (l1v, d1v))):
                    pick = ((lambda t: jnp.where(_head0(t.shape), t, 0.0)) if a == 0
                            else (lambda t: jnp.where(_head0(t.shape), 0.0, t)))
                    qa, ka, da = _bf(pick(q)), _bf(pick(k)), _bf(pick(dout))
                    s = jnp.where(valid, _dot_nt(qa, kb) * 0.125, NEG)
                    p = jnp.exp(s - jnp.tile(lse_a, (1, 2)))
                    dp = _dot_nt(da, vb)
                    ds = _bf(p * (dp - jnp.tile(delta_a, (1, 2))) * 0.125)
                    dq, dk, dv = dq + _dot(ds, ka), dk + _dot_tn(ds, qa), dv + _dot_tn(_bf(p), da)
                results.append((dq, dk, dv))
            for (rq, rk, _), (dq, dk, dv) in zip(blocks, results):
                dq_ref[rq, :] = dq
                dk_ref[rk, :] = dk
                dv_ref[rk, :] = dv

        for d in DILATIONS:
            def step(i, carry, d=d):
                update(_attn_group(d, i))
                return carry

            lax.fori_loop(0, ATTN_ITERS, step, 0)
        pl.when(hp == 3)(rs_finish)

    col = lambda off: pl.BlockSpec((S, 128), lambda h, off=off: (0, off + h))
    res = pl.pallas_call(
        body, name="attn_bwd", grid=(4,),
        in_specs=[col(0), col(4), col(8), col(0), col(0), col(0)] + [ANY] * nw,
        out_specs=[col(0), col(0), col(0)] + [ANY] * nw,
        out_shape=[jax.ShapeDtypeStruct((S, AW), F32)] * 3 + [jax.ShapeDtypeStruct(a.shape, a.dtype) for a in parts],
        scratch_shapes=[pltpu.VMEM((S, 128), F32)] * 4 + _scatter_scratch(nw),
        compiler_params=_params(1),
    )(qkv, qkv, qkv, o, lse, do, *parts)
    return res[0], res[1], res[2], res[3:]


def _logsig(x):
    return jnp.minimum(x, 0.0) - jnp.log1p(jnp.exp(-jnp.abs(x)))


def _conv_taps(xp, n):
    return [xp[8:] if j == 3 else pltpu.roll(xp, 3 - j, 0)[8:] for j in range(4)]


def _conv_silu(xp, w_ref, b_ref, n):
    taps = _conv_taps(xp, n)
    c = b_ref[...] + sum(w_ref[j:j + 1, :] * taps[j] for j in range(4))
    sg = _sigmoid(c)
    return c, sg, taps


def _chunk_gates(G):
    r = lax.broadcasted_iota(jnp.int32, (LC, LC), 0)
    c = lax.broadcasted_iota(jnp.int32, (LC, LC), 1)
    tril = (c <= r).astype(F32)
    triu = (c >= r).astype(F32)
    eye = (c == r).astype(F32)
    logf = _logsig(G)
    b_col = jnp.dot(tril, logf, preferred_element_type=F32, precision=HI)
    b_row = lax.dot_general(logf, triu, (((0,), (0,)), ((), ())), preferred_element_type=F32, precision=HI)
    g_row = lax.dot_general(G, eye, (((0,), (0,)), ((), ())), preferred_element_type=F32, precision=HI)
    return b_col, b_row, g_row, tril, triu


def _colpick(X, lane):
    li = lax.broadcasted_iota(jnp.int32, X.shape, 1)
    return jnp.sum(jnp.where(li == lane, X, 0.0), axis=1, keepdims=True)


def _rowpick(XT, row):
    ri = lax.broadcasted_iota(jnp.int32, XT.shape, 0)
    return jnp.sum(jnp.where(ri == row, XT, 0.0), axis=0, keepdims=True)


def _mlstm_head(qh, kh, vh, G, b_col, b_row, g_row, h, Ch, nh, m_prev):
    bt = _colpick(b_col, 4 + h)
    i_col = _colpick(G, h)
    bs = _rowpick(b_row, 4 + h)
    i_row = _rowpick(g_row, h)
    r = lax.broadcasted_iota(jnp.int32, (LC, LC), 0)
    c = lax.broadcasted_iota(jnp.int32, (LC, LC), 1)
    log_d = jnp.where(c <= r, bt - bs + i_row, NEG)
    log_inter = bt + m_prev
    m_t = jnp.maximum(log_inter, jnp.max(log_d, axis=1, keepdims=True))
    Dm = jnp.exp(log_d - m_t)
    g = jnp.exp(log_inter - m_t)
    qb, kb, vb = _bf(qh), _bf(kh), _bf(vh)
    Am = _dot_nt(qb, kb) * Dm
    qC = _dot(qb, _bf(Ch))
    num = g * qC + _dot(_bf(Am), vb)
    qn = jnp.sum(qh * nh, axis=1, keepdims=True)
    den = g * qn + jnp.sum(Am, axis=1, keepdims=True)
    floor = jnp.exp(-m_t)
    dd = jnp.maximum(jnp.abs(den), floor)
    hh = num / dd
    lane = lax.broadcasted_iota(jnp.int32, (1, LC), 1)
    blast = jnp.sum(jnp.where(lane == LC - 1, bs, 0.0), axis=1, keepdims=True)
    log_s = blast - bt + i_col
    m_new = jnp.maximum(blast + m_prev, jnp.max(log_s, axis=0, keepdims=True))
    decay = jnp.exp(blast + m_prev - m_new)
    ws = jnp.exp(log_s - m_new)
    kw = kh * ws
    C_new = decay * Ch + _dot_tn(_bf(kw), vb)
    n_new = decay * nh + jnp.sum(kw, axis=0, keepdims=True)
    return dict(Dm=Dm, g=g, Am=Am, qC=qC, qn=qn, den=den, floor=floor, dd=dd, h=hh, decay=decay, ws=ws, kw=kw,
                C_new=C_new, n_new=n_new, m_new=m_new, qb=qb, kb=kb, vb=vb)


def _head_out(hh, mo_h, gn_h):
    r = lax.rsqrt(jnp.mean(hh * hh, axis=-1, keepdims=True) + EPS)
    hn = hh * r
    sg = _sigmoid(mo_h)
    return sg * (hn * gn_h), hn, r, sg


def _mlstm_fwd(mqk, mv, mo, gates, conv_w, conv_b, gate_b, gn):
    nblk = S // TB
    ncb = TB // LC

    def body(x_ref, v_ref, o_ref, g_ref, w_ref, b_ref, gb_ref, gn_ref, out_ref, cs_ref, ns_ref, ms_ref,
             tail, Cst, nst, mst, qs, ks):
        i = pl.program_id(0)

        @pl.when(i == 0)
        def _():
            tail[...] = jnp.zeros_like(tail)
            Cst[...] = jnp.zeros_like(Cst)
            nst[...] = jnp.zeros_like(nst)
            mst[...] = jnp.zeros_like(mst)

        x = x_ref[...]
        xp = jnp.concatenate([tail[...], x], axis=0)
        tail[...] = x[TB - 8:TB, :]
        c, sg, _ = _conv_silu(xp, w_ref, b_ref, TB)
        y = c * sg
        qs[...] = y[:, 0:MW]
        ks[...] = y[:, MW:2 * MW] * (1.0 / math.sqrt(128.0))

        for cc in range(ncb):
            rows = slice(cc * LC, (cc + 1) * LC)
            G = g_ref[rows, :] + gb_ref[...]
            b_col, b_row, g_row, _, _ = _chunk_gates(G)
            cs_ref[cc] = Cst[...]
            ns_ref[cc] = nst[...]
            ms_ref[cc] = mst[...]
            for h in range(4):
                ln = slice(h * 128, (h + 1) * 128)
                m_prev = jnp.max(mst[0:1, ln], axis=1, keepdims=True)
                f = _mlstm_head(qs[rows, ln], ks[rows, ln], v_ref[rows, ln], G, b_col, b_row, g_row, h,
                                Cst[:, ln], nst[0:1, ln], m_prev)
                out, _, _, _ = _head_out(f["h"], o_ref[rows, ln], gn_ref[:, ln])
                out_ref[rows, ln] = out
                Cst[:, ln] = f["C_new"]
                nst[0:1, ln] = f["n_new"]
                mst[0:1, ln] = jnp.broadcast_to(f["m_new"], (1, 128))

    row = lambda wd: pl.BlockSpec((TB, wd), lambda i: (i, 0))
    return pl.pallas_call(
        body, name="mlstm_fwd", grid=(nblk,),
        in_specs=[row(1024), row(MW), row(MW), row(128), _cspec((4, 1024)), _cspec((1, 1024)), _cspec((1, 128)),
                  _cspec((1, MW))],
        out_specs=[row(MW), pl.BlockSpec((ncb, 128, MW), lambda i: (i, 0, 0)),
                   pl.BlockSpec((ncb, 8, MW), lambda i: (i, 0, 0)), pl.BlockSpec((ncb, 8, MW), lambda i: (i, 0, 0))],
        out_shape=[jax.ShapeDtypeStruct((S, MW), F32), jax.ShapeDtypeStruct((S // LC, 128, MW), F32),
                   jax.ShapeDtypeStruct((S // LC, 8, MW), F32), jax.ShapeDtypeStruct((S // LC, 8, MW), F32)],
        scratch_shapes=[pltpu.VMEM((8, 1024), F32), pltpu.VMEM((128, MW), F32), pltpu.VMEM((8, MW), F32),
                        pltpu.VMEM((8, MW), F32), pltpu.VMEM((TB, MW), F32), pltpu.VMEM((TB, MW), F32)],
        compiler_params=_params(1),
    )(mqk, mv, mo, gates, conv_w, conv_b, gate_b, gn)


def _mlstm_bwd(mqk, mv, mo, gates, conv_w, conv_b, gate_b, gn, cs, ns, ms, dout):
    nblk = S // TB
    ncb = TB // LC
    kscale = 1.0 / math.sqrt(128.0)

    def body(x_ref, xprev_ref, v_ref, o_ref, g_ref, w_ref, b_ref, gb_ref, gn_ref, cs_ref, ns_ref, ms_ref, do_ref,
             dx_ref, dv_ref, dmo_ref, dg_ref, dw_ref, db_ref, dgn_ref, dgb_ref,
             dCst, dnst, dyhead, qs, ks, dqk):
        i = pl.program_id(0)
        blk = nblk - 1 - i

        @pl.when(i == 0)
        def _():
            dCst[...] = jnp.zeros_like(dCst)
            dnst[...] = jnp.zeros_like(dnst)
            dyhead[...] = jnp.zeros_like(dyhead)
            dw_ref[...] = jnp.zeros_like(dw_ref)
            db_ref[...] = jnp.zeros_like(db_ref)
            dgn_ref[...] = jnp.zeros_like(dgn_ref)
            dgb_ref[...] = jnp.zeros_like(dgb_ref)

        x = x_ref[...]
        xprev = jnp.where(blk == 0, 0.0, xprev_ref[...])
        xp = jnp.concatenate([xprev, x], axis=0)
        c, sg, taps = _conv_silu(xp, w_ref, b_ref, TB)
        y = c * sg
        qs[...] = y[:, 0:MW]
        ks[...] = y[:, MW:2 * MW] * kscale
        lane128 = lax.broadcasted_iota(jnp.int32, (LC, 128), 1)
        rowi = lax.broadcasted_iota(jnp.int32, (LC, 1), 0)
        ones = jnp.ones((LC, 128), F32)

        for cc in reversed(range(ncb)):
            rows = slice(cc * LC, (cc + 1) * LC)
            G = g_ref[rows, :] + gb_ref[...]
            b_col, b_row, g_row, _, triu = _chunk_gates(G)
            dB = jnp.zeros((LC, 128), F32)
            dI = jnp.zeros((LC, 128), F32)
            for h in range(4):
                ln = slice(h * 128, (h + 1) * 128)
                Ch = cs_ref[cc, :, ln]
                nh = ns_ref[cc, 0:1, ln]
                m_prev = jnp.max(ms_ref[cc, 0:1, ln], axis=1, keepdims=True)
                qh, kh, vh = qs[rows, ln], ks[rows, ln], v_ref[rows, ln]
                f = _mlstm_head(qh, kh, vh, G, b_col, b_row, g_row, h, Ch, nh, m_prev)
                hh, dd, den, g, Am, Dm = f["h"], f["dd"], f["den"], f["g"], f["Am"], f["Dm"]
                qb, kb, vb = f["qb"], f["kb"], f["vb"]
                gn_h = gn_ref[:, ln]
                _, hn, r, sgo = _head_out(hh, o_ref[rows, ln], gn_h)
                do = do_ref[rows, ln]
                hm = hn * gn_h
                dmo_ref[rows, ln] = do * hm * sgo * (1.0 - sgo)
                dhm = do * sgo
                dgn_ref[:, ln] = dgn_ref[:, ln] + jnp.sum(dhm * hn, axis=0, keepdims=True)
                dhn = dhm * gn_h
                dh = r * (dhn - hn * jnp.mean(dhn * hn, axis=-1, keepdims=True))
                dnum = dh / dd
                ddd = -jnp.sum(dh * hh, axis=1, keepdims=True) / dd
                dden = jnp.where(jnp.abs(den) >= f["floor"], ddd * jnp.sign(den), 0.0)
                dnb = _bf(dnum)
                dA = _dot_nt(dnb, vb) + dden
                dv = _dot_tn(_bf(Am), dnb)
                gd = _bf(g * dnum)
                gq = g * dden
                dq = _dot_nt(gd, _bf(Ch)) + gq * nh
                dCn = dCst[:, ln]
                dnn = dnst[0:1, ln]
                dC = f["decay"] * dCn + _dot_tn(qb, gd)
                dn = f["decay"] * dnn + jnp.sum(gq * qh, axis=0, keepdims=True)
                dg = jnp.sum(dnum * f["qC"], axis=1, keepdims=True) + dden * f["qn"]
                dS = _bf(dA * Dm)
                dq = dq + _dot(dS, kb)
                dk = _dot_tn(dS, qb)
                Gm = dA * Am
                gam = dg * g
                dCb = _bf(dCn)
                E = _dot_nt(vb, dCb) + dnn
                ws = f["ws"]
                dk = dk + ws * E
                om = jnp.sum(E * kh, axis=1, keepdims=True) * ws
                dv = dv + _dot(_bf(f["kw"]), dCb)
                ddecay = (jnp.sum(jnp.sum(dCn * Ch, axis=1, keepdims=True), axis=0, keepdims=True)
                          + jnp.sum(dnn * nh, axis=1, keepdims=True))
                delta = ddecay * f["decay"]
                rows_g = jnp.sum(Gm, axis=1, keepdims=True)
                cols_g = lax.dot_general(Gm, ones, (((0,), (0,)), ((), ())), preferred_element_type=F32, precision=HI)
                last = jnp.where(rowi == LC - 1, jnp.sum(om, axis=0, keepdims=True) + delta, 0.0)
                db = rows_g + gam - om + last - cols_g
                di = cols_g + om
                dB = dB + jnp.where(lane128 == 4 + h, db, 0.0)
                dI = dI + jnp.where(lane128 == h, di, 0.0)
                dCst[:, ln] = dC
                dnst[0:1, ln] = dn
                dqk[rows, ln] = dq
                dqk[rows, MW + h * 128:MW + (h + 1) * 128] = dk * kscale
                dv_ref[rows, ln] = dv
            dlogf = jnp.dot(triu, dB, preferred_element_type=F32, precision=HI)
            dG = dI + dlogf * _sigmoid(-G)
            dG = jnp.where(lane128 < 8, dG, 0.0)
            dg_ref[rows, :] = dG
            dgb_ref[...] = dgb_ref[...] + jnp.sum(dG, axis=0, keepdims=True)

        dy = dqk[...] * (sg * (1.0 + c * (1.0 - sg)))
        db_ref[...] = db_ref[...] + jnp.sum(dy, axis=0, keepdims=True)
        for j in range(4):
            dw_ref[j:j + 1, :] = dw_ref[j:j + 1, :] + jnp.sum(dy * taps[j], axis=0, keepdims=True)
        dyp = jnp.concatenate([dy, dyhead[...]], axis=0)
        dx = w_ref[3:4, :] * dy
        for j in range(3):
            dx = dx + w_ref[j:j + 1, :] * pltpu.roll(dyp, TB + 8 - (3 - j), 0)[0:TB]
        dx_ref[...] = dx
        dyhead[...] = dy[0:8, :]

    rrow = lambda wd: pl.BlockSpec((TB, wd), lambda i: (nblk - 1 - i, 0))
    st = lambda r: pl.BlockSpec((ncb, r, MW), lambda i: (nblk - 1 - i, 0, 0))
    prev8 = pl.BlockSpec((8, 1024), lambda i: (jnp.maximum((nblk - 1 - i) * (TB // 8) - 1, 0), 0))
    return pl.pallas_call(
        body, name="mlstm_bwd", grid=(nblk,),
        in_specs=[rrow(1024), prev8, rrow(MW), rrow(MW), rrow(128), _cspec((4, 1024)), _cspec((1, 1024)),
                  _cspec((1, 128)), _cspec((1, MW)), st(128), st(8), st(8), rrow(MW)],
        out_specs=[rrow(1024), rrow(MW), rrow(MW), rrow(128),
                   pl.BlockSpec((4, 1024), lambda i: (0, 0)), pl.BlockSpec((1, 1024), lambda i: (0, 0)),
                   pl.BlockSpec((1, MW), lambda i: (0, 0)), pl.BlockSpec((1, 128), lambda i: (0, 0))],
        out_shape=[jax.ShapeDtypeStruct((S, 1024), F32), jax.ShapeDtypeStruct((S, MW), F32),
                   jax.ShapeDtypeStruct((S, MW), F32), jax.ShapeDtypeStruct((S, 128), F32),
                   jax.ShapeDtypeStruct((4, 1024), F32), jax.ShapeDtypeStruct((1, 1024), F32),
                   jax.ShapeDtypeStruct((1, MW), F32), jax.ShapeDtypeStruct((1, 128), F32)],
        scratch_shapes=[pltpu.VMEM((128, MW), F32), pltpu.VMEM((8, MW), F32), pltpu.VMEM((8, 1024), F32),
                        pltpu.VMEM((TB, MW), F32), pltpu.VMEM((TB, MW), F32), pltpu.VMEM((TB, 1024), F32)],
        compiler_params=_params(1),
    )(mqk, mqk, mv, mo, gates, conv_w, conv_b, gate_b, gn, cs, ns, ms, dout)


def _out_proj(x, attn, ml, w, g):
    tm = 256

    def body(x_ref, a_ref, m_ref, w_ref, g_ref, h_ref, u_ref):
        h1 = x_ref[...] + _dot(_bf(a_ref[...]), w_ref[0:AW, :]) + _dot(_bf(m_ref[...]), w_ref[AW:D, :])
        h_ref[...] = h1
        n, _ = _rms(h1)
        u_ref[...] = _bf(n * g_ref[...])

    row = lambda wd: pl.BlockSpec((tm, wd), lambda i: (i, 0))
    return pl.pallas_call(
        body, name="out_proj", grid=(S // tm,),
        in_specs=[row(D), row(AW), row(MW), _cspec((D, D)), _cspec((1, D))],
        out_specs=[row(D), row(D)],
        out_shape=[jax.ShapeDtypeStruct((S, D), F32), jax.ShapeDtypeStruct((S, D), BF16)],
        compiler_params=_params(1),
    )(x, attn, ml, w, g)


def _mlp_fwd(h1, u2, w_up, w_down):
    tm = 256

    def body(h_ref, u_ref, wu_ref, wd_ref, a_ref, o_ref):
        u = u_ref[...]
        acc = h_ref[...]
        for c in range(NDEV):
            cols = slice(c * 512, (c + 1) * 512)
            a = _dot(u, wu_ref[c])
            a_ref[:, cols] = a
            r = jnp.maximum(a, 0.0)
            acc = acc + _dot(_bf(r * r), wd_ref[cols, :])
        o_ref[...] = acc

    row = lambda wd: pl.BlockSpec((tm, wd), lambda i: (i, 0))
    return pl.pallas_call(
        body, name="mlp_fwd", grid=(S // tm,),
        in_specs=[row(D), row(D), _cspec((NDEV, D, DFF // NDEV)), _cspec((DFF, D))],
        out_specs=[row(DFF), row(D)],
        out_shape=[jax.ShapeDtypeStruct((S, DFF), F32), jax.ShapeDtypeStruct((S, D), F32)],
        compiler_params=_params(1),
    )(h1, u2, w_up, w_down)


def _ple_loss(h2, p, target, w_pg, w_ple, g_ple, g_fin):
    tm = 256

    def body(h_ref, p_ref, t_ref, wg_ref, wp_ref, gp_ref, gf_ref,
             dh_ref, dwg_ref, dwp_ref, dgp_ref, dgf_ref, loss_ref, acc_g, acc_p):
        i = pl.program_id(0)

        @pl.when(i == 0)
        def _():
            acc_g[...] = jnp.zeros_like(acc_g)
            acc_p[...] = jnp.zeros_like(acc_p)
            dgp_ref[...] = jnp.zeros_like(dgp_ref)
            dgf_ref[...] = jnp.zeros_like(dgf_ref)
            loss_ref[...] = jnp.zeros_like(loss_ref)

        h2v = h_ref[...]
        n2, rs2 = _rms(h2v)
        u3 = _bf(n2 * gp_ref[...])
        gt = _sigmoid(_dot(u3, wg_ref[...]))
        pb = _bf(p_ref[...])
        e = jnp.concatenate([_dot(pb, wp_ref[j]) for j in range(NDEV)], axis=1)
        h3 = h2v + gt * e
        n3, rs3 = _rms(h3)
        err = n3 * gf_ref[...] - t_ref[...]
        loss_ref[...] = loss_ref[...] + 0.5 / D * jnp.sum(jnp.sum(err * err, axis=1, keepdims=True), axis=0, keepdims=True)
        dy = err * (1.0 / D)
        dgf_ref[...] = dgf_ref[...] + jnp.sum(dy * n3, axis=0, keepdims=True)
        dh3 = _rms_bwd(dy, n3, rs3, gf_ref[...])
        de = _bf(dh3 * gt)
        dz = _bf(dh3 * e * gt * (1.0 - gt))
        acc_p[...] = acc_p[...] + _dot_tn(pb, de)
        acc_g[...] = acc_g[...] + _dot_tn(u3, dz)
        du3 = _dot_nt(dz, wg_ref[...])
        dgp_ref[...] = dgp_ref[...] + jnp.sum(du3 * n2, axis=0, keepdims=True)
        dh_ref[...] = dh3 + _rms_bwd(du3, n2, rs2, gp_ref[...])

        @pl.when(i == S // tm - 1)
        def _():
            dwg_ref[...] = _bf(acc_g[...])
            for j in range(NDEV):
                dwp_ref[j] = _bf(acc_p[:, j * 128:(j + 1) * 128])

    row = lambda wd: pl.BlockSpec((tm, wd), lambda i: (i, 0))
    whole = lambda shp: pl.BlockSpec(shp, lambda i: (0,) * len(shp))
    return pl.pallas_call(
        body, name="ple_loss", grid=(S // tm,),
        in_specs=[row(D), row(PLE), row(D), _cspec((D, D)), _cspec((NDEV, PLE, 128)), _cspec((1, D)), _cspec((1, D))],
        out_specs=[row(D), whole((D, D)), whole((NDEV, PLE, 128)), whole((1, D)), whole((1, D)), whole((1, 1))],
        out_shape=[jax.ShapeDtypeStruct((S, D), F32), jax.ShapeDtypeStruct((D, D), BF16),
                   jax.ShapeDtypeStruct((NDEV, PLE, 128), BF16), jax.ShapeDtypeStruct((1, D), F32),
                   jax.ShapeDtypeStruct((1, D), F32), jax.ShapeDtypeStruct((1, 1), F32)],
        scratch_shapes=[pltpu.VMEM((D, D), F32), pltpu.VMEM((PLE, D), F32)],
        compiler_params=_params(1),
    )(h2, p, target, w_pg, w_ple, g_ple, g_fin)


def _mlp_bwd(dh2, a, h1, g, w_up, w_down):
    tm = 256

    def body(d_ref, a_ref, h_ref, g_ref, wu_ref, wd_ref, da_ref, dh1_ref, dg_ref):
        @pl.when(pl.program_id(0) == 0)
        def _():
            dg_ref[...] = jnp.zeros_like(dg_ref)

        dh2v = d_ref[...]
        db = _bf(dh2v)
        du = jnp.zeros((tm, D), F32)
        for c in range(NDEV):
            cols = slice(c * 512, (c + 1) * 512)
            dr = _dot_nt(db, wd_ref[cols, :])
            da = _bf(dr * (2.0 * jnp.maximum(a_ref[:, cols], 0.0)))
            da_ref[:, cols] = da
            du = du + _dot_nt(da, wu_ref[c])
        n, rs = _rms(h_ref[...])
        dg_ref[...] = dg_ref[...] + jnp.sum(du * n, axis=0, keepdims=True)
        dh1_ref[...] = dh2v + _rms_bwd(du, n, rs, g_ref[...])

    row = lambda wd: pl.BlockSpec((tm, wd), lambda i: (i, 0))
    return pl.pallas_call(
        body, name="mlp_bwd", grid=(S // tm,),
        in_specs=[row(D), row(DFF), row(D), _cspec((1, D)), _cspec((NDEV, D, DFF // NDEV)), _cspec((DFF, D))],
        out_specs=[row(DFF), row(D), pl.BlockSpec((1, D), lambda i: (0, 0))],
        out_shape=[jax.ShapeDtypeStruct((S, DFF), BF16), jax.ShapeDtypeStruct((S, D), F32),
                   jax.ShapeDtypeStruct((1, D), F32)],
        compiler_params=_params(1),
    )(dh2, a, h1, g, w_up, w_down)


def _out_proj_bwd(dh1, attn, ml, w):
    tm = 256

    def body(d_ref, a_ref, m_ref, w_ref, da_ref, dm_ref, dw_ref, acc):
        i = pl.program_id(0)

        @pl.when(i == 0)
        def _():
            acc[...] = jnp.zeros_like(acc)

        db = _bf(d_ref[...])
        dmix = _dot_nt(db, w_ref[...])
        da_ref[...] = dmix[:, 0:AW]
        dm_ref[...] = dmix[:, AW:D]
        acc[0:AW, :] = acc[0:AW, :] + _dot_tn(_bf(a_ref[...]), db)
        acc[AW:D, :] = acc[AW:D, :] + _dot_tn(_bf(m_ref[...]), db)

        @pl.when(i == S // tm - 1)
        def _():
            dw_ref[...] = _bf(acc[...])

    row = lambda wd: pl.BlockSpec((tm, wd), lambda i: (i, 0))
    return pl.pallas_call(
        body, name="out_proj_bwd", grid=(S // tm,),
        in_specs=[row(D), row(AW), row(MW), _cspec((D, D))],
        out_specs=[row(AW), row(MW), pl.BlockSpec((D, D), lambda i: (0, 0))],
        out_shape=[jax.ShapeDtypeStruct((S, AW), F32), jax.ShapeDtypeStruct((S, MW), F32),
                   jax.ShapeDtypeStruct((D, D), BF16)],
        scratch_shapes=[pltpu.VMEM((D, D), F32)],
        compiler_params=_params(1),
    )(dh1, attn, ml, w)


def _in_proj_bwd(dq, dk, dv, dmqk, dmv, dmo, dgt, dh1, x, g1, w, rc, ra, rb):
    tm = 256

    def body(dq_ref, dk_ref, dv_ref, dmqk_ref, dmv_ref, dmo_ref, dgt_ref, dh_ref, x_ref, g_ref, w_ref,
             rc_ref, ra_ref, rb_ref, dp_ref, dx_ref, dg_ref):
        @pl.when(pl.program_id(0) == 0)
        def _():
            dg_ref[...] = jnp.zeros_like(dg_ref)

        c, a, b = rc_ref[...], ra_ref[...], rb_ref[...]
        for half, ref in enumerate((dq_ref, dk_ref)):
            for t in range(4):
                lo = half * 512 + t * 128
                dp_ref[:, lo:lo + 128] = _bf(_rope_bwd(ref[:, t * 128:(t + 1) * 128], c, a, b))
        dp_ref[:, 1024:1536] = _bf(dv_ref[...])
        dp_ref[:, 1536:2560] = _bf(dmqk_ref[...])
        dp_ref[:, 2560:3072] = _bf(dmv_ref[...])
        dp_ref[:, 3072:3584] = _bf(dmo_ref[...])
        dp_ref[:, 3584:3712] = _bf(dgt_ref[...])
        dp_ref[:, 3712:PW] = jnp.zeros((tm, PW - 3712), BF16)
        du = jnp.zeros((tm, D), F32)
        for s in range(PW // 768):
            cols = slice(s * 768, (s + 1) * 768)
            du = du + _dot_nt(dp_ref[:, cols], w_ref[:, cols])
        n, rs = _rms(x_ref[...])
        dg_ref[...] = dg_ref[...] + jnp.sum(du * n, axis=0, keepdims=True)
        dx_ref[...] = dh_ref[...] + _rms_bwd(du, n, rs, g_ref[...])

    row = lambda wd: pl.BlockSpec((tm, wd), lambda i: (i, 0))
    return pl.pallas_call(
        body, name="in_proj_bwd", grid=(S // tm,),
        in_specs=[row(AW), row(AW), row(AW), row(1024), row(MW), row(MW), row(128), row(D), row(D), _cspec((1, D)),
                  _cspec((D, PW)), row(128), row(128), row(128)],
        out_specs=[row(PW), row(D), pl.BlockSpec((1, D), lambda i: (0, 0))],
        out_shape=[jax.ShapeDtypeStruct((S, PW), BF16), jax.ShapeDtypeStruct((S, D), F32),
                   jax.ShapeDtypeStruct((1, D), F32)],
        compiler_params=_params(1),
    )(dq, dk, dv, dmqk, dmv, dmo, dgt, dh1, x, g1, w, rc, ra, rb)


def _wgrad(name, A, B, a_fn, b_fn, tk, tn, out_shape, out_spec, ts=512, split=None):
    K, N = A.shape[1], B.shape[1]
    nrt = S // ts

    def body(a_ref, b_ref, o_ref, acc):
        r = pl.program_id(2)

        @pl.when(r == 0)
        def _():
            acc[...] = jnp.zeros_like(acc)

        acc[...] = acc[...] + _dot_tn(a_fn(a_ref[...]), b_fn(b_ref[...]))

        @pl.when(r == nrt - 1)
        def _():
            if split is None:
                o_ref[...] = _bf(acc[...])
            else:
                for j in range(NDEV):
                    o_ref[j] = _bf(acc[:, split * j:split * (j + 1)])

    return pl.pallas_call(
        body, name=name, grid=(N // tn, K // tk, nrt),
        in_specs=[pl.BlockSpec((ts, tk), lambda n, k, r: (r, k)), pl.BlockSpec((ts, tn), lambda n, k, r: (r, n))],
        out_specs=out_spec,
        out_shape=jax.ShapeDtypeStruct(out_shape, BF16),
        scratch_shapes=[pltpu.VMEM((tk, tn), F32)],
        compiler_params=_params(3),
    )(A, B)


def _relu2_bf(a):
    r = jnp.maximum(a, 0.0)
    return _bf(r * r)


def _ident(a):
    return a


def _step(x, p, target, g1, conv_b, gate_b, gn, g_mlp, g_ple, g_fin, sh):
    late = ("w_out", "w_up", "w_down", "w_ple_gate", "w_ple")
    g_in, g_conv = _gather_weights([sh["w_in"], sh["conv_w"]], [BF16, F32])
    conv_w = g_conv.transpose(1, 0, 2).reshape(4, 1024)
    rc, ra, rb = _rope_tables()
    qkv, mqk, mv, mo, gates, u1, w_in_p = _in_proj(x, g1, g_in, rc, ra, rb)
    attn, lse, (w_out8, w_up8, w_down8, w_pg8, w_ple8) = _attn_fwd(qkv, [sh[n] for n in late], [BF16] * 5)
    w_out, w_down, w_pg = w_out8.reshape(D, D), w_down8.reshape(DFF, D), w_pg8.reshape(D, D)
    ml, cs, ns, ms = _mlstm_fwd(mqk, mv, mo, gates, conv_w, conv_b, gate_b, gn)
    h1, u2 = _out_proj(x, attn, ml, w_out, g_mlp)
    a, h2 = _mlp_fwd(h1, u2, w_up8, w_down)
    dh2, dw_pg, dw_ple8, dg_ple, dg_fin, loss = _ple_loss(h2, p, target, w_pg, w_ple8, g_ple, g_fin)
    da, dh1, dg_mlp = _mlp_bwd(dh2, a, h1, g_mlp, w_up8, w_down)
    dw_up8 = _wgrad("wgrad_up", u2, da, _ident, _ident, D, 512, (NDEV, D, 512),
                    pl.BlockSpec((None, D, 512), lambda n, k, r: (n, 0, 0)))
    dw_down = _wgrad("wgrad_down", a, dh2, _relu2_bf, _bf, 1024, 1024, (DFF, D),
                     pl.BlockSpec((1024, 1024), lambda n, k, r: (k, n)))
    d_attn, d_ml, dw_out = _out_proj_bwd(dh1, attn, ml, w_out)
    dmqk, dmv, dmo, dgt, dconv_w, dconv_b, dgn, dgate_b = _mlstm_bwd(
        mqk, mv, mo, gates, conv_w, conv_b, gate_b, gn, cs, ns, ms, d_ml)
    parts = [dw_out.reshape(NDEV, D // NDEV, D), dw_up8, dw_down.reshape(NDEV, DFF // NDEV, D),
             dw_pg.reshape(NDEV, D // NDEV, D), dw_ple8]
    dq, dk, dv, recv_late = _attn_bwd(qkv, attn, lse, d_attn, parts)
    dproj, dx, dg1 = _in_proj_bwd(dq, dk, dv, dmqk, dmv, dmo, dgt, dh1, x, g1, w_in_p, rc, ra, rb)
    dw_in8 = _wgrad("wgrad_in", u1, dproj, _ident, _ident, D, PW, (NDEV, D, IN_W // NDEV),
                    pl.BlockSpec((NDEV, D, IN_W // NDEV), lambda n, k, r: (0, 0, 0)), split=IN_W // NDEV)
    recv_in, recv_conv = _scatter_grads([dw_in8, dconv_w.reshape(4, NDEV, 128).transpose(1, 0, 2)])
    recv = dict(zip(late, recv_late), w_in=recv_in, conv_w=recv_conv)
    small = dict(norm_mix_g=dg1, conv_b=dconv_b, gate_b=dgate_b, mlstm_norm_g=dgn, norm_mlp_g=dg_mlp,
                 norm_ple_g=dg_ple, final_norm_g=dg_fin)
    return loss, dx, recv, small


def _gather_weights(shards, dtypes):
    nw = len(shards)

    def body(*refs):
        start, forward, finish = _gather_phases(refs[:nw], refs[nw:2 * nw], refs[2 * nw:3 * nw], *refs[3 * nw:])
        start()
        forward()
        finish()

    return pl.pallas_call(
        body, name="gather_weights",
        in_specs=[VM] * nw, out_specs=[ANY] * nw,
        out_shape=_gather_shapes(shards, dtypes),
        scratch_shapes=_gather_scratch(shards, dtypes),
        compiler_params=_params(),
    )(*shards)


def _scatter_grads(parts):
    nw = len(parts)

    def body(*refs):
        start, finish = _scatter_phases(refs[:nw], refs[nw:2 * nw], *refs[2 * nw:])
        start()
        finish()

    return pl.pallas_call(
        body, name="scatter_grads",
        in_specs=[ANY] * nw, out_specs=[ANY] * nw,
        out_shape=[jax.ShapeDtypeStruct(a.shape, a.dtype) for a in parts],
        scratch_shapes=_scatter_scratch(nw),
        compiler_params=_params(),
    )(*parts)


SMALL_ROWS = 64


def _allreduce_small(vals):
    nv = len(vals)

    def body(*refs):
        ins, out_ref = refs[:nv], refs[nv]
        pack, rbuf, send_sems, recv_sems = refs[nv + 1:]
        x, y, c = _place()
        me = _dev_index(x, y, c)
        pack[...] = jnp.zeros_like(pack)
        for i in range(nv):
            pack[8 * i:8 * i + 1, 0:ins[i].shape[1]] = ins[i][...]
        rbuf[me] = pack[...]
        copies = []
        for k, (dx, dy, dc) in enumerate(FLIPS):
            peer = ((x + dx) % 2, (y + dy) % 2, (c + dc) % 2)
            cp = pltpu.make_async_remote_copy(
                src_ref=pack, dst_ref=rbuf.at[me], send_sem=send_sems.at[k], recv_sem=recv_sems.at[k],
                device_id=peer, device_id_type=MESH)
            cp.start()
            copies.append(cp)
        for cp in copies:
            cp.wait()
        tot = rbuf[0]
        for j in range(1, NDEV):
            tot = tot + rbuf[j]
        out_ref[...] = tot

    return pl.pallas_call(
        body, name="allreduce_small",
        in_specs=[VM] * nv, out_specs=VM,
        out_shape=jax.ShapeDtypeStruct((SMALL_ROWS, 1024), F32),
        scratch_shapes=[pltpu.VMEM((SMALL_ROWS, 1024), F32), pltpu.VMEM((NDEV, SMALL_ROWS, 1024), F32),
                        pltpu.SemaphoreType.DMA((7,)), pltpu.SemaphoreType.DMA((7,))],
        compiler_params=_params(),
    )(*vals)


def _adamw(name, gparts, w, m, v, tr):
    P, R, C = gparts.shape
    c1 = 1.0 - ADAM_B1 ** ADAM_STEP
    c2 = 1.0 - ADAM_B2 ** ADAM_STEP

    def body(g_ref, w_ref, m_ref, v_ref, go_ref, d_ref, mo_ref, vo_ref):
        g = g_ref[0].astype(F32)
        for j in range(1, P):
            g = g + g_ref[j].astype(F32)
        m2 = ADAM_B1 * m_ref[...] + (1.0 - ADAM_B1) * g
        v2 = ADAM_B2 * v_ref[...] + (1.0 - ADAM_B2) * (g * g)
        go_ref[...] = g
        mo_ref[...] = m2
        vo_ref[...] = v2
        d_ref[...] = -ADAM_LR * ((m2 / c1) / (jnp.sqrt(v2 / c2) + ADAM_EPS) + ADAM_WD * w_ref[...])

    row = pl.BlockSpec((tr, C), lambda i: (i, 0))
    return pl.pallas_call(
        body, name=name, grid=(R // tr,),
        in_specs=[pl.BlockSpec((P, tr, C), lambda i: (0, i, 0)), row, row, row],
        out_specs=[row] * 4,
        out_shape=[jax.ShapeDtypeStruct((R, C), F32)] * 4,
        compiler_params=_params(1),
    )(gparts, w, m, v)


SMALL = ("norm_mix_g", "conv_b", "gate_b", "mlstm_norm_g", "norm_mlp_g", "norm_ple_g", "final_norm_g")


def _pack_small(vals):
    return jnp.concatenate([jnp.pad(a, ((0, 7), (0, 1024 - a.shape[1]))) for a in vals], axis=0)


def kernel(x, p, norm_mix_g, w_in, conv_w, conv_b, gate_b, mlstm_norm_g, w_out, norm_mlp_g, w_up, w_down, norm_ple_g, w_ple_gate, w_ple, final_norm_g, loss_target, m_norm_mix_g, m_w_in, m_conv_w, m_conv_b, m_gate_b, m_mlstm_norm_g, m_w_out, m_norm_mlp_g, m_w_up, m_w_down, m_norm_ple_g, m_w_ple_gate, m_w_ple, m_final_norm_g, v_norm_mix_g, v_w_in, v_conv_w, v_conv_b, v_gate_b, v_mlstm_norm_g, v_w_out, v_norm_mlp_g, v_w_up, v_w_down, v_norm_ple_g, v_w_ple_gate, v_w_ple, v_final_norm_g):
    big_names = ("w_in", "conv_w", "w_out", "w_up", "w_down", "w_ple_gate", "w_ple")
    wts = dict(w_in=w_in, conv_w=conv_w, w_out=w_out, w_up=w_up, w_down=w_down, w_ple_gate=w_ple_gate, w_ple=w_ple)
    mom = dict(w_in=m_w_in, conv_w=m_conv_w, w_out=m_w_out, w_up=m_w_up, w_down=m_w_down, w_ple_gate=m_w_ple_gate,
               w_ple=m_w_ple)
    var = dict(w_in=v_w_in, conv_w=v_conv_w, w_out=v_w_out, w_up=v_w_up, w_down=v_w_down, w_ple_gate=v_w_ple_gate,
               w_ple=v_w_ple)
    sq = lambda a: a.reshape(a.shape[1:])
    fin = final_norm_g.reshape(1, D)
    loss, dx, recv, small = _step(
        x[0], p[0, 0], loss_target[0], norm_mix_g, conv_b, jnp.pad(gate_b, ((0, 0), (0, 120))), mlstm_norm_g,
        norm_mlp_g, norm_ple_g, fin, {n: sq(wts[n]) for n in big_names})
    total = _allreduce_small([small[n] for n in SMALL] + [loss])

    out = {}
    for n, tr in zip(big_names, (256, 4, 128, 256, 256, 128, 256)):
        res = _adamw("adamw_" + n, recv[n], sq(wts[n]), sq(mom[n]), sq(var[n]), tr)
        out[n] = [t.reshape(wts[n].shape) for t in res]
    sw = dict(norm_mix_g=norm_mix_g, conv_b=conv_b, gate_b=gate_b, mlstm_norm_g=mlstm_norm_g, norm_mlp_g=norm_mlp_g,
              norm_ple_g=norm_ple_g, final_norm_g=fin)
    sm = dict(norm_mix_g=m_norm_mix_g, conv_b=m_conv_b, gate_b=m_gate_b, mlstm_norm_g=m_mlstm_norm_g,
              norm_mlp_g=m_norm_mlp_g, norm_ple_g=m_norm_ple_g, final_norm_g=m_final_norm_g.reshape(1, D))
    sv = dict(norm_mix_g=v_norm_mix_g, conv_b=v_conv_b, gate_b=v_gate_b, mlstm_norm_g=v_mlstm_norm_g,
              norm_mlp_g=v_norm_mlp_g, norm_ple_g=v_norm_ple_g, final_norm_g=v_final_norm_g.reshape(1, D))
    nrow = 8 * len(SMALL)
    res = _adamw("adamw_small", total[0:nrow].reshape(1, nrow, 1024), _pack_small([sw[n] for n in SMALL]),
                 _pack_small([sm[n] for n in SMALL]), _pack_small([sv[n] for n in SMALL]), nrow)
    for i, n in enumerate(SMALL):
        shp = final_norm_g.shape if n == "final_norm_g" else sw[n].shape
        out[n] = [t[8 * i, 0:sw[n].shape[1]].reshape(shp) for t in res]
    order = ("norm_mix_g", "w_in", "conv_w", "conv_b", "gate_b", "mlstm_norm_g", "w_out", "norm_mlp_g", "w_up", "w_down",
             "norm_ple_g", "w_ple_gate", "w_ple", "final_norm_g")
    loss_all = total[nrow, 0]
    return (loss_all, dx[None], *[out[n][0] for n in order], *[out[n][1] for n in order],
            *[out[n][2] for n in order], *[out[n][3] for n in order])
```

```python
import functools
import math

import jax
import jax.numpy as jnp
from jax import lax
from jax.experimental import pallas as pl
from jax.experimental.pallas import tpu as pltpu

F32, BF16 = jnp.float32, jnp.bfloat16
S = 4096
D = 1024
AW = 512
MW = 512
DFF = 4096
PLE = 256
IN_W = 3592
PW = 3840
NDEV = 8
EPS = 1e-6
NEG = -1e30
LC = 64
TB = 256
ROPE_THETA = 500000.0
VMEM_LIMIT = 56 * 1024 * 1024
HI = lax.Precision.HIGHEST

ADAM_LR, ADAM_B1, ADAM_B2, ADAM_EPS, ADAM_WD, ADAM_STEP = 0.001, 0.9, 0.999, 1e-08, 0.01, 10


def _params(n_grid=0, **kw):
    sem = dict(dimension_semantics=("arbitrary",) * n_grid) if n_grid else {}
    return pltpu.CompilerParams(vmem_limit_bytes=VMEM_LIMIT, **sem, **kw)


def _cspec(shape):
    nd = len(shape)
    return pl.BlockSpec(shape, lambda *_: (0,) * nd, pipeline_mode=pl.Buffered(1))


def _dot(a, b):
    return jnp.dot(a, b, preferred_element_type=F32)


def _dot_nt(a, b):
    return lax.dot_general(a, b, (((1,), (1,)), ((), ())), preferred_element_type=F32)


def _dot_tn(a, b):
    return lax.dot_general(a, b, (((0,), (0,)), ((), ())), preferred_element_type=F32)


def _bf(x):
    return x.astype(BF16)


def _rms(x):
    rs = lax.rsqrt(jnp.mean(x * x, axis=-1, keepdims=True) + EPS)
    return x * rs, rs


def _rms_bwd(du, n, rs, g):
    dn = du * g
    return rs * (dn - n * jnp.mean(dn * n, axis=-1, keepdims=True))


def _sigmoid(x):
    return 1.0 / (1.0 + jnp.exp(-x))


def _rope_tables():
    j = lax.broadcasted_iota(jnp.int32, (S, 128), 1) % 64
    pos = lax.broadcasted_iota(jnp.int32, (S, 128), 0).astype(F32)
    inv_freq = jnp.power(ROPE_THETA, -(j % 8).astype(F32) / 8.0)
    ang = pos * inv_freq
    cos, sin = jnp.cos(ang), jnp.sin(ang)
    c = jnp.where(j < 16, cos, 1.0)
    a = jnp.where(j < 8, -sin, 0.0)
    b = jnp.where((j >= 8) & (j < 16), sin, 0.0)
    return c, a, b


def _rope(blk, c, a, b):
    return blk * c + pltpu.roll(blk, 120, 1) * a + pltpu.roll(blk, 8, 1) * b


def _rope_bwd(d, c, a, b):
    return d * c + pltpu.roll(d * a, 8, 1) + pltpu.roll(d * b, 120, 1)


MESH = pl.DeviceIdType.MESH
ANY = pl.BlockSpec(memory_space=pl.ANY)
VM = pl.BlockSpec(memory_space=pltpu.VMEM)
FLIPS = [(dx, dy, dc) for dx in (0, 1) for dy in (0, 1) for dc in (0, 1)][1:]


def _place():
    return lax.axis_index("x"), lax.axis_index("y"), lax.axis_index("c")


def _dev_index(px, py, pc):
    return 4 * px + 2 * py + pc


def _gather_phases(ins, outs, bufs, send_sems, recv_sems, local_sems):
    nw = len(ins)
    x, y, c = _place()
    me, sib = (x, y, c), (x, y, 1 - c)
    chips = [(1 - x, y), (x, 1 - y), (1 - x, 1 - y)]

    def copy(w, k, block, to, from_buf=False):
        dst = outs[w].at[_dev_index(*block)]
        return pltpu.make_async_remote_copy(
            src_ref=bufs[w] if from_buf else dst, dst_ref=dst, send_sem=send_sems.at[w, k],
            recv_sem=recv_sems.at[w, k], device_id=to, device_id_type=MESH)

    def mine(w):
        return pltpu.make_async_copy(bufs[w], outs[w].at[_dev_index(*me)], local_sems.at[w])

    def first(w):
        return [copy(w, 0, me, sib, True)] + [copy(w, 1 + j, me, (*chip, c), True) for j, chip in enumerate(chips)]

    def passed(w):
        return [copy(w, 4 + j, (*chip, c), sib) for j, chip in enumerate(chips)]

    def start():
        for w in range(nw):
            bufs[w][...] = ins[w][...].astype(bufs[w].dtype)
        for w in range(nw):
            mine(w).start()
            for cp in first(w):
                cp.start()

    def forward():
        for j, chip in enumerate(chips):
            for w in range(nw):
                copy(w, 1 + j, (*chip, c), me).wait_recv()
                passed(w)[j].start()

    def finish():
        for w in range(nw):
            copy(w, 0, sib, me).wait_recv()
        for j, chip in enumerate(chips):
            for w in range(nw):
                copy(w, 4 + j, (*chip, 1 - c), me).wait_recv()
        for w in range(nw):
            for cp in first(w) + passed(w):
                cp.wait_send()
            mine(w).wait()

    return start, forward, finish


def _gather_scratch(shards, dtypes):
    nw = len(shards)
    return ([pltpu.VMEM(s.shape, dt) for s, dt in zip(shards, dtypes)]
            + [pltpu.SemaphoreType.DMA((nw, 7)), pltpu.SemaphoreType.DMA((nw, 7)), pltpu.SemaphoreType.DMA((nw,))])


def _gather_shapes(shards, dtypes):
    return [jax.ShapeDtypeStruct((NDEV, *s.shape), dt) for s, dt in zip(shards, dtypes)]


def _scatter_phases(ins, outs, send_sems, recv_sems, local_sems):
    nw = len(ins)
    x, y, c = _place()
    me = _dev_index(x, y, c)

    def copies():
        out = []
        for w in range(nw):
            out.append(pltpu.make_async_copy(ins[w].at[me], outs[w].at[me], local_sems.at[w]))
            for k, (dx, dy, dc) in enumerate(FLIPS):
                peer = ((x + dx) % 2, (y + dy) % 2, (c + dc) % 2)
                out.append(pltpu.make_async_remote_copy(
                    src_ref=ins[w].at[_dev_index(*peer)], dst_ref=outs[w].at[me], send_sem=send_sems.at[w, k],
                    recv_sem=recv_sems.at[w, k], device_id=peer, device_id_type=MESH))
        return out

    def start():
        for cp in copies():
            cp.start()

    def finish():
        for cp in copies():
            cp.wait()

    return start, finish


def _scatter_scratch(nw):
    return [pltpu.SemaphoreType.DMA((nw, 7)), pltpu.SemaphoreType.DMA((nw, 7)), pltpu.SemaphoreType.DMA((nw,))]


def _in_proj(x, g1, wg, rc, ra, rb):
    tm = 256
    sw = IN_W // NDEV

    def body(x_ref, g_ref, wg_ref, rc_ref, ra_ref, rb_ref, qkv_ref, mqk_ref, mv_ref, mo_ref, gt_ref, u_ref, w_ref):
        @pl.when(pl.program_id(0) == 0)
        def _():
            for j in range(NDEV):
                w_ref[:, sw * j:sw * (j + 1)] = wg_ref[j]
            w_ref[:, IN_W:PW] = jnp.zeros((D, PW - IN_W), BF16)

        n, _ = _rms(x_ref[...])
        u = _bf(n * g_ref[...])
        u_ref[...] = u
        c, a, b = rc_ref[...], ra_ref[...], rb_ref[...]
        for half in range(2):
            blk = _dot(u, w_ref[:, half * 512:(half + 1) * 512])
            for t in range(4):
                lo = half * 512 + t * 128
                qkv_ref[:, lo:lo + 128] = _rope(blk[:, t * 128:(t + 1) * 128], c, a, b)
        qkv_ref[:, 1024:1536] = _dot(u, w_ref[:, 1024:1536])
        mqk_ref[:, 0:512] = _dot(u, w_ref[:, 1536:2048])
        mqk_ref[:, 512:1024] = _dot(u, w_ref[:, 2048:2560])
        mv_ref[...] = _dot(u, w_ref[:, 2560:3072])
        mo_ref[...] = _dot(u, w_ref[:, 3072:3584])
        gt_ref[...] = _dot(u, w_ref[:, 3584:3712])

    row = lambda wd: pl.BlockSpec((tm, wd), lambda i: (i, 0))
    return pl.pallas_call(
        body, name="in_proj", grid=(S // tm,),
        in_specs=[row(D), _cspec((1, D)), _cspec((NDEV, D, sw)), row(128), row(128), row(128)],
        out_specs=[row(1536), row(1024), row(512), row(512), row(128), row(D), pl.BlockSpec((D, PW), lambda i: (0, 0))],
        out_shape=[jax.ShapeDtypeStruct((S, 1536), F32), jax.ShapeDtypeStruct((S, 1024), F32),
                   jax.ShapeDtypeStruct((S, 512), F32), jax.ShapeDtypeStruct((S, 512), F32),
                   jax.ShapeDtypeStruct((S, 128), F32), jax.ShapeDtypeStruct((S, D), BF16),
                   jax.ShapeDtypeStruct((D, PW), BF16)],
        compiler_params=_params(1),
    )(x, g1, wg, rc, ra, rb)


DILATIONS = (1, 4, 16)


def _attn_rows(d, r, n):
    if d == 1:
        q0 = pl.multiple_of(n * 128, 128)
        k0 = pl.multiple_of(jnp.maximum(n - 1, 0) * 128, 128)
        return pl.ds(q0, 128), pl.ds(k0, 256), n
    q0 = r + n * 128 * d
    k0 = r + jnp.maximum(n - 1, 0) * 128 * d
    return pl.ds(q0, 128, stride=d), pl.ds(k0, 256, stride=d), n


ATTN_GROUP = 4
ATTN_ITERS = S // 128 // ATTN_GROUP


def _attn_group(d, i):
    nb = S // (128 * d)
    if d == 1:
        return [_attn_rows(1, 0, i + ATTN_ITERS * u) for u in range(ATTN_GROUP)]
    return [_attn_rows(d, (i // nb) * ATTN_GROUP + u, i % nb) for u in range(ATTN_GROUP)]


def _attn_valid(n):
    kd = lax.broadcasted_iota(jnp.int32, (128, 256), 1) - lax.broadcasted_iota(jnp.int32, (128, 256), 0)
    off = jnp.where(n == 0, 0, 128)
    return (kd <= off) & (kd >= off - 128)


def _head0(shape):
    return lax.broadcasted_iota(jnp.int32, shape, 1) < 64


def _attn_fwd(qkv, shards, dtypes):
    nw = len(shards)

    def body(*refs):
        q_ref, k_ref, v_ref = refs[:3]
        ins = refs[3:3 + nw]
        o_ref, lse_ref = refs[3 + nw:5 + nw]
        outs = refs[5 + nw:5 + 2 * nw]
        m0, m1, l0, l1, acc = refs[5 + 2 * nw:10 + 2 * nw]
        bufs = refs[10 + 2 * nw:10 + 3 * nw]
        ag_start, ag_forward, ag_finish = _gather_phases(ins, outs, bufs, *refs[10 + 3 * nw:])
        hp = pl.program_id(0)
        pl.when(hp == 0)(ag_start)
        pl.when(hp == 2)(ag_forward)
        head0 = _head0((128, 128))
        for ref in (m0, m1):
            ref[...] = jnp.full((S, 128), NEG, F32)
        for ref in (l0, l1, acc):
            ref[...] = jnp.zeros((S, 128), F32)
        stats = (m0, m1, l0, l1, acc)

        def update(blocks):
            loaded = [([q_ref[rq, :], k_ref[rk, :], v_ref[rk, :]], [ref[rq, :] for ref in stats])
                      for rq, rk, _ in blocks]
            results = []
            for ((q, k, v), (mp0, mp1, lp0, lp1, acc_prev)), (_, _, n) in zip(loaded, blocks):
                valid = _attn_valid(n)
                kb, vb = _bf(k), _bf(v)
                q = q * 0.125
                new = []
                for qa, m_prev, l_prev in ((_bf(jnp.where(head0, q, 0.0)), mp0, lp0),
                                           (_bf(jnp.where(head0, 0.0, q)), mp1, lp1)):
                    s = jnp.where(valid, _dot_nt(qa, kb), NEG)
                    m_new = jnp.maximum(m_prev, jnp.max(s, axis=-1, keepdims=True))
                    p = jnp.exp(s - jnp.tile(m_new, (1, 2)))
                    alpha = jnp.exp(m_prev - m_new)
                    new += [m_new, alpha * l_prev + jnp.sum(p, axis=-1, keepdims=True),
                            alpha * acc_prev + _dot(_bf(p), vb)]
                results.append((new[0], new[3], new[1], new[4], jnp.where(head0, new[2], new[5])))
            for (rq, _, _), res in zip(blocks, results):
                for ref, val in zip(stats, res):
                    ref[rq, :] = val

        for d in DILATIONS:
            def step(i, carry, d=d):
                update(_attn_group(d, i))
                return carry

            lax.fori_loop(0, ATTN_ITERS, step, 0)

        def fin(t, carry):
            rows = pl.ds(pl.multiple_of(t * 256, 256), 256)
            h0 = lax.broadcasted_iota(jnp.int32, (256, 128), 1) < 64
            l = jnp.where(h0, l0[rows, :], l1[rows, :])
            o_ref[rows, :] = acc[rows, :] / l
            lse_ref[rows, :] = jnp.where(h0, m0[rows, :], m1[rows, :]) + jnp.log(l)
            return carry

        lax.fori_loop(0, S // 256, fin, 0)
        pl.when(hp == 3)(ag_finish)

    col = lambda off: pl.BlockSpec((S, 128), lambda h, off=off: (0, off + h))
    res = pl.pallas_call(
        body, name="attn_fwd", grid=(4,),
        in_specs=[col(0), col(4), col(8)] + [VM] * nw,
        out_specs=[col(0), col(0)] + [ANY] * nw,
        out_shape=[jax.ShapeDtypeStruct((S, AW), F32), jax.ShapeDtypeStruct((S, AW), F32)]
        + _gather_shapes(shards, dtypes),
        scratch_shapes=[pltpu.VMEM((S, 128), F32)] * 5 + _gather_scratch(shards, dtypes),
        compiler_params=_params(1),
    )(qkv, qkv, qkv, *shards)
    return res[0], res[1], res[2:]


def _attn_bwd(qkv, o, lse, do, parts):
    nw = len(parts)

    def body(*refs):
        q_ref, k_ref, v_ref, o_ref, lse_ref, do_ref = refs[:6]
        ins = refs[6:6 + nw]
        dq_ref, dk_ref, dv_ref = refs[6 + nw:9 + nw]
        outs = refs[9 + nw:9 + 2 * nw]
        L0, L1, D0, D1 = refs[9 + 2 * nw:13 + 2 * nw]
        rs_start, rs_finish = _scatter_phases(ins, outs, *refs[13 + 2 * nw:])
        hp = pl.program_id(0)
        pl.when(hp == 0)(rs_start)
        def pre(t, carry):
            rows = pl.ds(pl.multiple_of(t * 256, 256), 256)
            h0 = lax.broadcasted_iota(jnp.int32, (256, 128), 1) < 64
            ls = lse_ref[rows, :]
            dd = do_ref[rows, :] * o_ref[rows, :]
            shp = (256, 128)
            L0[rows, :] = jnp.broadcast_to(jnp.max(jnp.where(h0, ls, NEG), axis=-1, keepdims=True), shp)
            L1[rows, :] = jnp.broadcast_to(jnp.max(jnp.where(h0, NEG, ls), axis=-1, keepdims=True), shp)
            D0[rows, :] = jnp.broadcast_to(jnp.sum(jnp.where(h0, dd, 0.0), axis=-1, keepdims=True), shp)
            D1[rows, :] = jnp.broadcast_to(jnp.sum(jnp.where(h0, 0.0, dd), axis=-1, keepdims=True), shp)
            return carry

        lax.fori_loop(0, S // 256, pre, 0)
        for ref in (dq_ref, dk_ref, dv_ref):
            ref[...] = jnp.zeros((S, 128), F32)

        def update(blocks):
            loaded = [([q_ref[rq, :], k_ref[rk, :], v_ref[rk, :], do_ref[rq, :]],
                       [L0[rq, :], L1[rq, :], D0[rq, :], D1[rq, :]],
                       [dq_ref[rq, :], dk_ref[rk, :], dv_ref[rk, :]]) for rq, rk, _ in blocks]
            results = []
            for ((q, k, v, dout), (l0v, l1v, d0v, d1v), (dq, dk, dv)), (_, _, n) in zip(loaded, blocks):
                valid = _attn_valid(n)
                kb, vb = _bf(k), _bf(v)
                for a, (lse_a, delta_a) in enumerate(((l0v, d0v), (l1v, d1v))):
                    pick = ((lambda t: jnp.where(_head0(t.shape), t, 0.0)) if a == 0
                            else (lambda t: jnp.where(_head0(t.shape), 0.0, t)))
                    qa, ka, da = _bf(pick(q)), _bf(pick(k)), _bf(pick(dout))
                    s = jnp.where(valid, _dot_nt(qa, kb) * 0.125, NEG)
                    p = jnp.exp(s - jnp.tile(lse_a, (1, 2)))
                    dp = _dot_nt(da, vb)
                    ds = _bf(p * (dp - jnp.tile(delta_a, (1, 2))) * 0.125)
                    dq, dk, dv = dq + _dot(ds, ka), dk + _dot_tn(ds, qa), dv + _dot_tn(_bf(p), da)
                results.append((dq, dk, dv))
            for (rq, rk, _), (dq, dk, dv) in zip(blocks, results):
                dq_ref[rq, :] = dq
                dk_ref[rk, :] = dk
                dv_ref[rk, :] = dv

        for d in DILATIONS:
            def step(i, carry, d=d):
                update(_attn_group(d, i))
                return carry

            lax.fori_loop(0, ATTN_ITERS, step, 0)
        pl.when(hp == 3)(rs_finish)

    col = lambda off: pl.BlockSpec((S, 128), lambda h, off=off: (0, off + h))
    res = pl.pallas_call(
        body, name="attn_bwd", grid=(4,),
        in_specs=[col(0), col(4), col(8), col(0), col(0), col(0)] + [ANY] * nw,
        out_specs=[col(0), col(0), col(0)] + [ANY] * nw,
        out_shape=[jax.ShapeDtypeStruct((S, AW), F32)] * 3 + [jax.ShapeDtypeStruct(a.shape, a.dtype) for a in parts],
        scratch_shapes=[pltpu.VMEM((S, 128), F32)] * 4 + _scatter_scratch(nw),
        compiler_params=_params(1),
    )(qkv, qkv, qkv, o, lse, do, *parts)
    return res[0], res[1], res[2], res[3:]


def _logsig(x):
    return jnp.minimum(x, 0.0) - jnp.log1p(jnp.exp(-jnp.abs(x)))


def _conv_taps(xp, n):
    return [xp[8:] if j == 3 else pltpu.roll(xp, 3 - j, 0)[8:] for j in range(4)]


def _conv_silu(xp, w_ref, b_ref, n):
    taps = _conv_taps(xp, n)
    c = b_ref[...] + sum(w_ref[j:j + 1, :] * taps[j] for j in range(4))
    sg = _sigmoid(c)
    return c, sg, taps


def _chunk_gates(G):
    r = lax.broadcasted_iota(jnp.int32, (LC, LC), 0)
    c = lax.broadcasted_iota(jnp.int32, (LC, LC), 1)
    tril = (c <= r).astype(F32)
    triu = (c >= r).astype(F32)
    eye = (c == r).astype(F32)
    logf = _logsig(G)
    b_col = jnp.dot(tril, logf, preferred_element_type=F32, precision=HI)
    b_row = lax.dot_general(logf, triu, (((0,), (0,)), ((), ())), preferred_element_type=F32, precision=HI)
    g_row = lax.dot_general(G, eye, (((0,), (0,)), ((), ())), preferred_element_type=F32, precision=HI)
    return b_col, b_row, g_row, tril, triu


def _colpick(X, lane):
    li = lax.broadcasted_iota(jnp.int32, X.shape, 1)
    return jnp.sum(jnp.where(li == lane, X, 0.0), axis=1, keepdims=True)


def _rowpick(XT, row):
    ri = lax.broadcasted_iota(jnp.int32, XT.shape, 0)
    return jnp.sum(jnp.where(ri == row, XT, 0.0), axis=0, keepdims=True)


def _mlstm_head(qh, kh, vh, G, b_col, b_row, g_row, h, Ch, nh, m_prev):
    bt = _colpick(b_col, 4 + h)
    i_col = _colpick(G, h)
    bs = _rowpick(b_row, 4 + h)
    i_row = _rowpick(g_row, h)
    r = lax.broadcasted_iota(jnp.int32, (LC, LC), 0)
    c = lax.broadcasted_iota(jnp.int32, (LC, LC), 1)
    log_d = jnp.where(c <= r, bt - bs + i_row, NEG)
    log_inter = bt + m_prev
    m_t = jnp.maximum(log_inter, jnp.max(log_d, axis=1, keepdims=True))
    Dm = jnp.exp(log_d - m_t)
    g = jnp.exp(log_inter - m_t)
    qb, kb, vb = _bf(qh), _bf(kh), _bf(vh)
    Am = _dot_nt(qb, kb) * Dm
    qC = _dot(qb, _bf(Ch))
    num = g * qC + _dot(_bf(Am), vb)
    qn = jnp.sum(qh * nh, axis=1, keepdims=True)
    den = g * qn + jnp.sum(Am, axis=1, keepdims=True)
    floor = jnp.exp(-m_t)
    dd = jnp.maximum(jnp.abs(den), floor)
    hh = num / dd
    lane = lax.broadcasted_iota(jnp.int32, (1, LC), 1)
    blast = jnp.sum(jnp.where(lane == LC - 1, bs, 0.0), axis=1, keepdims=True)
    log_s = blast - bt + i_col
    m_new = jnp.maximum(blast + m_prev, jnp.max(log_s, axis=0, keepdims=True))
    decay = jnp.exp(blast + m_prev - m_new)
    ws = jnp.exp(log_s - m_new)
    kw = kh * ws
    C_new = decay * Ch + _dot_tn(_bf(kw), vb)
    n_new = decay * nh + jnp.sum(kw, axis=0, keepdims=True)
    return dict(Dm=Dm, g=g, Am=Am, qC=qC, qn=qn, den=den, floor=floor, dd=dd, h=hh, decay=decay, ws=ws, kw=kw,
                C_new=C_new, n_new=n_new, m_new=m_new, qb=qb, kb=kb, vb=vb)


def _head_out(hh, mo_h, gn_h):
    r = lax.rsqrt(jnp.mean(hh * hh, axis=-1, keepdims=True) + EPS)
    hn = hh * r
    sg = _sigmoid(mo_h)
    return sg * (hn * gn_h), hn, r, sg


def _mlstm_fwd(mqk, mv, mo, gates, conv_w, conv_b, gate_b, gn, shards, dtypes):
    nblk = S // TB
    ncb = TB // LC
    nw = len(shards)

    def body(*refs):
        x_ref, v_ref, o_ref, g_ref, w_ref, b_ref, gb_ref, gn_ref = refs[:8]
        ins = refs[8:8 + nw]
        out_ref, cs_ref, ns_ref, ms_ref = refs[8 + nw:12 + nw]
        outs = refs[12 + nw:12 + 2 * nw]
        tail, Cst, nst, mst, qs, ks = refs[12 + 2 * nw:18 + 2 * nw]
        bufs = refs[18 + 2 * nw:18 + 3 * nw]
        ag_start, ag_forward, ag_finish = _gather_phases(ins, outs, bufs, *refs[18 + 3 * nw:])
        i = pl.program_id(0)
        pl.when(i == 0)(ag_start)
        pl.when(i == nblk // 2)(ag_forward)

        @pl.when(i == 0)
        def _():
            tail[...] = jnp.zeros_like(tail)
            Cst[...] = jnp.zeros_like(Cst)
            nst[...] = jnp.zeros_like(nst)
            mst[...] = jnp.zeros_like(mst)

        x = x_ref[...]
        xp = jnp.concatenate([tail[...], x], axis=0)
        tail[...] = x[TB - 8:TB, :]
        c, sg, _ = _conv_silu(xp, w_ref, b_ref, TB)
        y = c * sg
        qs[...] = y[:, 0:MW]
        ks[...] = y[:, MW:2 * MW] * (1.0 / math.sqrt(128.0))

        for cc in range(ncb):
            rows = slice(cc * LC, (cc + 1) * LC)
            G = g_ref[rows, :] + gb_ref[...]
            b_col, b_row, g_row, _, _ = _chunk_gates(G)
            cs_ref[cc] = Cst[...]
            ns_ref[cc] = nst[...]
            ms_ref[cc] = mst[...]
            for h in range(4):
                ln = slice(h * 128, (h + 1) * 128)
                m_prev = jnp.max(mst[0:1, ln], axis=1, keepdims=True)
                f = _mlstm_head(qs[rows, ln], ks[rows, ln], v_ref[rows, ln], G, b_col, b_row, g_row, h,
                                Cst[:, ln], nst[0:1, ln], m_prev)
                out, _, _, _ = _head_out(f["h"], o_ref[rows, ln], gn_ref[:, ln])
                out_ref[rows, ln] = out
                Cst[:, ln] = f["C_new"]
                nst[0:1, ln] = f["n_new"]
                mst[0:1, ln] = jnp.broadcast_to(f["m_new"], (1, 128))
        pl.when(i == nblk - 1)(ag_finish)

    row = lambda wd: pl.BlockSpec((TB, wd), lambda i: (i, 0))
    res = pl.pallas_call(
        body, name="mlstm_fwd", grid=(nblk,),
        in_specs=[row(1024), row(MW), row(MW), row(128), _cspec((4, 1024)), _cspec((1, 1024)), _cspec((1, 128)),
                  _cspec((1, MW))] + [VM] * nw,
        out_specs=[row(MW), pl.BlockSpec((ncb, 128, MW), lambda i: (i, 0, 0)),
                   pl.BlockSpec((ncb, 8, MW), lambda i: (i, 0, 0)), pl.BlockSpec((ncb, 8, MW), lambda i: (i, 0, 0))]
        + [ANY] * nw,
        out_shape=[jax.ShapeDtypeStruct((S, MW), F32), jax.ShapeDtypeStruct((S // LC, 128, MW), F32),
                   jax.ShapeDtypeStruct((S // LC, 8, MW), F32), jax.ShapeDtypeStruct((S // LC, 8, MW), F32)]
        + _gather_shapes(shards, dtypes),
        scratch_shapes=[pltpu.VMEM((8, 1024), F32), pltpu.VMEM((128, MW), F32), pltpu.VMEM((8, MW), F32),
                        pltpu.VMEM((8, MW), F32), pltpu.VMEM((TB, MW), F32), pltpu.VMEM((TB, MW), F32)]
        + _gather_scratch(shards, dtypes),
        compiler_params=_params(1),
    )(mqk, mv, mo, gates, conv_w, conv_b, gate_b, gn, *shards)
    return res[0], res[1], res[2], res[3], res[4:]


def _mlstm_bwd(mqk, mv, mo, gates, conv_w, conv_b, gate_b, gn, cs, ns, ms, dout, parts):
    nblk = S // TB
    ncb = TB // LC
    kscale = 1.0 / math.sqrt(128.0)
    nw = len(parts)

    def body(*refs):
        x_ref, xprev_ref, v_ref, o_ref, g_ref, w_ref, b_ref, gb_ref, gn_ref, cs_ref, ns_ref, ms_ref, do_ref = refs[:13]
        ins = refs[13:13 + nw]
        dx_ref, dv_ref, dmo_ref, dg_ref, dw_ref, db_ref, dgn_ref, dgb_ref = refs[13 + nw:21 + nw]
        outs = refs[21 + nw:21 + 2 * nw]
        dCst, dnst, dyhead, qs, ks, dqk = refs[21 + 2 * nw:27 + 2 * nw]
        rs_start, rs_finish = _scatter_phases(ins, outs, *refs[27 + 2 * nw:])
        i = pl.program_id(0)
        blk = nblk - 1 - i
        pl.when(i == 0)(rs_start)

        @pl.when(i == 0)
        def _():
            dCst[...] = jnp.zeros_like(dCst)
            dnst[...] = jnp.zeros_like(dnst)
            dyhead[...] = jnp.zeros_like(dyhead)
            dw_ref[...] = jnp.zeros_like(dw_ref)
            db_ref[...] = jnp.zeros_like(db_ref)
            dgn_ref[...] = jnp.zeros_like(dgn_ref)
            dgb_ref[...] = jnp.zeros_like(dgb_ref)

        x = x_ref[...]
        xprev = jnp.where(blk == 0, 0.0, xprev_ref[...])
        xp = jnp.concatenate([xprev, x], axis=0)
        c, sg, taps = _conv_silu(xp, w_ref, b_ref, TB)
        y = c * sg
        qs[...] = y[:, 0:MW]
        ks[...] = y[:, MW:2 * MW] * kscale
        lane128 = lax.broadcasted_iota(jnp.int32, (LC, 128), 1)
        rowi = lax.broadcasted_iota(jnp.int32, (LC, 1), 0)
        ones = jnp.ones((LC, 128), F32)

        for cc in reversed(range(ncb)):
            rows = slice(cc * LC, (cc + 1) * LC)
            G = g_ref[rows, :] + gb_ref[...]
            b_col, b_row, g_row, _, triu = _chunk_gates(G)
            dB = jnp.zeros((LC, 128), F32)
            dI = jnp.zeros((LC, 128), F32)
            for h in range(4):
                ln = slice(h * 128, (h + 1) * 128)
                Ch = cs_ref[cc, :, ln]
                nh = ns_ref[cc, 0:1, ln]
                m_prev = jnp.max(ms_ref[cc, 0:1, ln], axis=1, keepdims=True)
                qh, kh, vh = qs[rows, ln], ks[rows, ln], v_ref[rows, ln]
                f = _mlstm_head(qh, kh, vh, G, b_col, b_row, g_row, h, Ch, nh, m_prev)
                hh, dd, den, g, Am, Dm = f["h"], f["dd"], f["den"], f["g"], f["Am"], f["Dm"]
                qb, kb, vb = f["qb"], f["kb"], f["vb"]
                gn_h = gn_ref[:, ln]
                _, hn, r, sgo = _head_out(hh, o_ref[rows, ln], gn_h)
                do = do_ref[rows, ln]
                hm = hn * gn_h
                dmo_ref[rows, ln] = do * hm * sgo * (1.0 - sgo)
                dhm = do * sgo
                dgn_ref[:, ln] = dgn_ref[:, ln] + jnp.sum(dhm * hn, axis=0, keepdims=True)
                dhn = dhm * gn_h
                dh = r * (dhn - hn * jnp.mean(dhn * hn, axis=-1, keepdims=True))
                dnum = dh / dd
                ddd = -jnp.sum(dh * hh, axis=1, keepdims=True) / dd
                dden = jnp.where(jnp.abs(den) >= f["floor"], ddd * jnp.sign(den), 0.0)
                dnb = _bf(dnum)
                dA = _dot_nt(dnb, vb) + dden
                dv = _dot_tn(_bf(Am), dnb)
                gd = _bf(g * dnum)
                gq = g * dden
                dq = _dot_nt(gd, _bf(Ch)) + gq * nh
                dCn = dCst[:, ln]
                dnn = dnst[0:1, ln]
                dC = f["decay"] * dCn + _dot_tn(qb, gd)
                dn = f["decay"] * dnn + jnp.sum(gq * qh, axis=0, keepdims=True)
                dg = jnp.sum(dnum * f["qC"], axis=1, keepdims=True) + dden * f["qn"]
                dS = _bf(dA * Dm)
                dq = dq + _dot(dS, kb)
                dk = _dot_tn(dS, qb)
                Gm = dA * Am
                gam = dg * g
                dCb = _bf(dCn)
                E = _dot_nt(vb, dCb) + dnn
                ws = f["ws"]
                dk = dk + ws * E
                om = jnp.sum(E * kh, axis=1, keepdims=True) * ws
                dv = dv + _dot(_bf(f["kw"]), dCb)
                ddecay = (jnp.sum(jnp.sum(dCn * Ch, axis=1, keepdims=True), axis=0, keepdims=True)
                          + jnp.sum(dnn * nh, axis=1, keepdims=True))
                delta = ddecay * f["decay"]
                rows_g = jnp.sum(Gm, axis=1, keepdims=True)
                cols_g = lax.dot_general(Gm, ones, (((0,), (0,)), ((), ())), preferred_element_type=F32, precision=HI)
                last = jnp.where(rowi == LC - 1, jnp.sum(om, axis=0, keepdims=True) + delta, 0.0)
                db = rows_g + gam - om + last - cols_g
                di = cols_g + om
                dB = dB + jnp.where(lane128 == 4 + h, db, 0.0)
                dI = dI + jnp.where(lane128 == h, di, 0.0)
                dCst[:, ln] = dC
                dnst[0:1, ln] = dn
                dqk[rows, ln] = dq
                dqk[rows, MW + h * 128:MW + (h + 1) * 128] = dk * kscale
                dv_ref[rows, ln] = dv
            dlogf = jnp.dot(triu, dB, preferred_element_type=F32, precision=HI)
            dG = dI + dlogf * _sigmoid(-G)
            dG = jnp.where(lane128 < 8, dG, 0.0)
            dg_ref[rows, :] = dG
            dgb_ref[...] = dgb_ref[...] + jnp.sum(dG, axis=0, keepdims=True)

        dy = dqk[...] * (sg * (1.0 + c * (1.0 - sg)))
        db_ref[...] = db_ref[...] + jnp.sum(dy, axis=0, keepdims=True)
        for j in range(4):
            dw_ref[j:j + 1, :] = dw_ref[j:j + 1, :] + jnp.sum(dy * taps[j], axis=0, keepdims=True)
        dyp = jnp.concatenate([dy, dyhead[...]], axis=0)
        dx = w_ref[3:4, :] * dy
        for j in range(3):
            dx = dx + w_ref[j:j + 1, :] * pltpu.roll(dyp, TB + 8 - (3 - j), 0)[0:TB]
        dx_ref[...] = dx
        dyhead[...] = dy[0:8, :]
        pl.when(i == nblk - 1)(rs_finish)

    rrow = lambda wd: pl.BlockSpec((TB, wd), lambda i: (nblk - 1 - i, 0))
    st = lambda r: pl.BlockSpec((ncb, r, MW), lambda i: (nblk - 1 - i, 0, 0))
    prev8 = pl.BlockSpec((8, 1024), lambda i: (jnp.maximum((nblk - 1 - i) * (TB // 8) - 1, 0), 0))
    res = pl.pallas_call(
        body, name="mlstm_bwd", grid=(nblk,),
        in_specs=[rrow(1024), prev8, rrow(MW), rrow(MW), rrow(128), _cspec((4, 1024)), _cspec((1, 1024)),
                  _cspec((1, 128)), _cspec((1, MW)), st(128), st(8), st(8), rrow(MW)] + [ANY] * nw,
        out_specs=[rrow(1024), rrow(MW), rrow(MW), rrow(128),
                   pl.BlockSpec((4, 1024), lambda i: (0, 0)), pl.BlockSpec((1, 1024), lambda i: (0, 0)),
                   pl.BlockSpec((1, MW), lambda i: (0, 0)), pl.BlockSpec((1, 128), lambda i: (0, 0))] + [ANY] * nw,
        out_shape=[jax.ShapeDtypeStruct((S, 1024), F32), jax.ShapeDtypeStruct((S, MW), F32),
                   jax.ShapeDtypeStruct((S, MW), F32), jax.ShapeDtypeStruct((S, 128), F32),
                   jax.ShapeDtypeStruct((4, 1024), F32), jax.ShapeDtypeStruct((1, 1024), F32),
                   jax.ShapeDtypeStruct((1, MW), F32), jax.ShapeDtypeStruct((1, 128), F32)]
        + [jax.ShapeDtypeStruct(a.shape, a.dtype) for a in parts],
        scratch_shapes=[pltpu.VMEM((128, MW), F32), pltpu.VMEM((8, MW), F32), pltpu.VMEM((8, 1024), F32),
                        pltpu.VMEM((TB, MW), F32), pltpu.VMEM((TB, MW), F32), pltpu.VMEM((TB, 1024), F32)]
        + _scatter_scratch(nw),
        compiler_params=_params(1),
    )(mqk, mqk, mv, mo, gates, conv_w, conv_b, gate_b, gn, cs, ns, ms, dout, *parts)
    return res[:8], res[8:]


def _out_proj(x, attn, ml, w, g):
    tm = 256

    def body(x_ref, a_ref, m_ref, w_ref, g_ref, h_ref, u_ref):
        h1 = x_ref[...] + _dot(_bf(a_ref[...]), w_ref[0:AW, :]) + _dot(_bf(m_ref[...]), w_ref[AW:D, :])
        h_ref[...] = h1
        n, _ = _rms(h1)
        u_ref[...] = _bf(n * g_ref[...])

    row = lambda wd: pl.BlockSpec((tm, wd), lambda i: (i, 0))
    return pl.pallas_call(
        body, name="out_proj", grid=(S // tm,),
        in_specs=[row(D), row(AW), row(MW), _cspec((D, D)), _cspec((1, D))],
        out_specs=[row(D), row(D)],
        out_shape=[jax.ShapeDtypeStruct((S, D), F32), jax.ShapeDtypeStruct((S, D), BF16)],
        compiler_params=_params(1),
    )(x, attn, ml, w, g)


def _mlp_fwd(h1, u2, w_up, w_down):
    tm = 256

    def body(h_ref, u_ref, wu_ref, wd_ref, a_ref, o_ref):
        u = u_ref[...]
        acc = h_ref[...]
        for c in range(NDEV):
            cols = slice(c * 512, (c + 1) * 512)
            a = _dot(u, wu_ref[c])
            a_ref[:, cols] = a
            r = jnp.maximum(a, 0.0)
            acc = acc + _dot(_bf(r * r), wd_ref[cols, :])
        o_ref[...] = acc

    row = lambda wd: pl.BlockSpec((tm, wd), lambda i: (i, 0))
    return pl.pallas_call(
        body, name="mlp_fwd", grid=(S // tm,),
        in_specs=[row(D), row(D), _cspec((NDEV, D, DFF // NDEV)), _cspec((DFF, D))],
        out_specs=[row(DFF), row(D)],
        out_shape=[jax.ShapeDtypeStruct((S, DFF), F32), jax.ShapeDtypeStruct((S, D), F32)],
        compiler_params=_params(1),
    )(h1, u2, w_up, w_down)


def _ple_loss(h2, p, target, w_pg, w_ple, g_ple, g_fin):
    tm = 256

    def body(h_ref, p_ref, t_ref, wg_ref, wp_ref, gp_ref, gf_ref,
             dh_ref, dwg_ref, dwp_ref, dgp_ref, dgf_ref, loss_ref, acc_g, acc_p):
        i = pl.program_id(0)

        @pl.when(i == 0)
        def _():
            acc_g[...] = jnp.zeros_like(acc_g)
            acc_p[...] = jnp.zeros_like(acc_p)
            dgp_ref[...] = jnp.zeros_like(dgp_ref)
            dgf_ref[...] = jnp.zeros_like(dgf_ref)
            loss_ref[...] = jnp.zeros_like(loss_ref)

        h2v = h_ref[...]
        n2, rs2 = _rms(h2v)
        u3 = _bf(n2 * gp_ref[...])
        gt = _sigmoid(_dot(u3, wg_ref[...]))
        pb = _bf(p_ref[...])
        e = jnp.concatenate([_dot(pb, wp_ref[j]) for j in range(NDEV)], axis=1)
        h3 = h2v + gt * e
        n3, rs3 = _rms(h3)
        err = n3 * gf_ref[...] - t_ref[...]
        loss_ref[...] = loss_ref[...] + 0.5 / D * jnp.sum(jnp.sum(err * err, axis=1, keepdims=True), axis=0, keepdims=True)
        dy = err * (1.0 / D)
        dgf_ref[...] = dgf_ref[...] + jnp.sum(dy * n3, axis=0, keepdims=True)
        dh3 = _rms_bwd(dy, n3, rs3, gf_ref[...])
        de = _bf(dh3 * gt)
        dz = _bf(dh3 * e * gt * (1.0 - gt))
        acc_p[...] = acc_p[...] + _dot_tn(pb, de)
        acc_g[...] = acc_g[...] + _dot_tn(u3, dz)
        du3 = _dot_nt(dz, wg_ref[...])
        dgp_ref[...] = dgp_ref[...] + jnp.sum(du3 * n2, axis=0, keepdims=True)
        dh_ref[...] = dh3 + _rms_bwd(du3, n2, rs2, gp_ref[...])

        @pl.when(i == S // tm - 1)
        def _():
            dwg_ref[...] = _bf(acc_g[...])
            for j in range(NDEV):
                dwp_ref[j] = _bf(acc_p[:, j * 128:(j + 1) * 128])

    row = lambda wd: pl.BlockSpec((tm, wd), lambda i: (i, 0))
    whole = lambda shp: pl.BlockSpec(shp, lambda i: (0,) * len(shp))
    return pl.pallas_call(
        body, name="ple_loss", grid=(S // tm,),
        in_specs=[row(D), row(PLE), row(D), _cspec((D, D)), _cspec((NDEV, PLE, 128)), _cspec((1, D)), _cspec((1, D))],
        out_specs=[row(D), whole((D, D)), whole((NDEV, PLE, 128)), whole((1, D)), whole((1, D)), whole((1, 1))],
        out_shape=[jax.ShapeDtypeStruct((S, D), F32), jax.ShapeDtypeStruct((D, D), BF16),
                   jax.ShapeDtypeStruct((NDEV, PLE, 128), BF16), jax.ShapeDtypeStruct((1, D), F32),
                   jax.ShapeDtypeStruct((1, D), F32), jax.ShapeDtypeStruct((1, 1), F32)],
        scratch_shapes=[pltpu.VMEM((D, D), F32), pltpu.VMEM((PLE, D), F32)],
        compiler_params=_params(1),
    )(h2, p, target, w_pg, w_ple, g_ple, g_fin)


def _mlp_bwd(dh2, a, h1, g, w_up, w_down):
    tm = 256

    def body(d_ref, a_ref, h_ref, g_ref, wu_ref, wd_ref, da_ref, dh1_ref, dg_ref):
        @pl.when(pl.program_id(0) == 0)
        def _():
            dg_ref[...] = jnp.zeros_like(dg_ref)

        dh2v = d_ref[...]
        db = _bf(dh2v)
        du = jnp.zeros((tm, D), F32)
        for c in range(NDEV):
            cols = slice(c * 512, (c + 1) * 512)
            dr = _dot_nt(db, wd_ref[cols, :])
            da = _bf(dr * (2.0 * jnp.maximum(a_ref[:, cols], 0.0)))
            da_ref[:, cols] = da
            du = du + _dot_nt(da, wu_ref[c])
        n, rs = _rms(h_ref[...])
        dg_ref[...] = dg_ref[...] + jnp.sum(du * n, axis=0, keepdims=True)
        dh1_ref[...] = dh2v + _rms_bwd(du, n, rs, g_ref[...])

    row = lambda wd: pl.BlockSpec((tm, wd), lambda i: (i, 0))
    return pl.pallas_call(
        body, name="mlp_bwd", grid=(S // tm,),
        in_specs=[row(D), row(DFF), row(D), _cspec((1, D)), _cspec((NDEV, D, DFF // NDEV)), _cspec((DFF, D))],
        out_specs=[row(DFF), row(D), pl.BlockSpec((1, D), lambda i: (0, 0))],
        out_shape=[jax.ShapeDtypeStruct((S, DFF), BF16), jax.ShapeDtypeStruct((S, D), F32),
                   jax.ShapeDtypeStruct((1, D), F32)],
        compiler_params=_params(1),
    )(dh2, a, h1, g, w_up, w_down)


def _out_proj_bwd(dh1, attn, ml, w):
    tm = 256

    def body(d_ref, a_ref, m_ref, w_ref, da_ref, dm_ref, dw_ref, acc):
        i = pl.program_id(0)

        @pl.when(i == 0)
        def _():
            acc[...] = jnp.zeros_like(acc)

        db = _bf(d_ref[...])
        dmix = _dot_nt(db, w_ref[...])
        da_ref[...] = dmix[:, 0:AW]
        dm_ref[...] = dmix[:, AW:D]
        acc[0:AW, :] = acc[0:AW, :] + _dot_tn(_bf(a_ref[...]), db)
        acc[AW:D, :] = acc[AW:D, :] + _dot_tn(_bf(m_ref[...]), db)

        @pl.when(i == S // tm - 1)
        def _():
            dw_ref[...] = _bf(acc[...])

    row = lambda wd: pl.BlockSpec((tm, wd), lambda i: (i, 0))
    return pl.pallas_call(
        body, name="out_proj_bwd", grid=(S // tm,),
        in_specs=[row(D), row(AW), row(MW), _cspec((D, D))],
        out_specs=[row(AW), row(MW), pl.BlockSpec((D, D), lambda i: (0, 0))],
        out_shape=[jax.ShapeDtypeStruct((S, AW), F32), jax.ShapeDtypeStruct((S, MW), F32),
                   jax.ShapeDtypeStruct((D, D), BF16)],
        scratch_shapes=[pltpu.VMEM((D, D), F32)],
        compiler_params=_params(1),
    )(dh1, attn, ml, w)


def _in_proj_bwd(dq, dk, dv, dmqk, dmv, dmo, dgt, dh1, x, g1, w, rc, ra, rb):
    tm = 256

    def body(dq_ref, dk_ref, dv_ref, dmqk_ref, dmv_ref, dmo_ref, dgt_ref, dh_ref, x_ref, g_ref, w_ref,
             rc_ref, ra_ref, rb_ref, dp_ref, dx_ref, dg_ref):
        @pl.when(pl.program_id(0) == 0)
        def _():
            dg_ref[...] = jnp.zeros_like(dg_ref)

        c, a, b = rc_ref[...], ra_ref[...], rb_ref[...]
        for half, ref in enumerate((dq_ref, dk_ref)):
            for t in range(4):
                lo = half * 512 + t * 128
                dp_ref[:, lo:lo + 128] = _bf(_rope_bwd(ref[:, t * 128:(t + 1) * 128], c, a, b))
        dp_ref[:, 1024:1536] = _bf(dv_ref[...])
        dp_ref[:, 1536:2560] = _bf(dmqk_ref[...])
        dp_ref[:, 2560:3072] = _bf(dmv_ref[...])
        dp_ref[:, 3072:3584] = _bf(dmo_ref[...])
        dp_ref[:, 3584:3712] = _bf(dgt_ref[...])
        dp_ref[:, 3712:PW] = jnp.zeros((tm, PW - 3712), BF16)
        du = jnp.zeros((tm, D), F32)
        for s in range(PW // 768):
            cols = slice(s * 768, (s + 1) * 768)
            du = du + _dot_nt(dp_ref[:, cols], w_ref[:, cols])
        n, rs = _rms(x_ref[...])
        dg_ref[...] = dg_ref[...] + jnp.sum(du * n, axis=0, keepdims=True)
        dx_ref[...] = dh_ref[...] + _rms_bwd(du, n, rs, g_ref[...])

    row = lambda wd: pl.BlockSpec((tm, wd), lambda i: (i, 0))
    return pl.pallas_call(
        body, name="in_proj_bwd", grid=(S // tm,),
        in_specs=[row(AW), row(AW), row(AW), row(1024), row(MW), row(MW), row(128), row(D), row(D), _cspec((1, D)),
                  _cspec((D, PW)), row(128), row(128), row(128)],
        out_specs=[row(PW), row(D), pl.BlockSpec((1, D), lambda i: (0, 0))],
        out_shape=[jax.ShapeDtypeStruct((S, PW), BF16), jax.ShapeDtypeStruct((S, D), F32),
                   jax.ShapeDtypeStruct((1, D), F32)],
        compiler_params=_params(1),
    )(dq, dk, dv, dmqk, dmv, dmo, dgt, dh1, x, g1, w, rc, ra, rb)


def _wgrad(name, A, B, a_fn, b_fn, tk, tn, out_shape, out_spec, ts=512, split=None):
    K, N = A.shape[1], B.shape[1]
    nrt = S // ts

    def body(a_ref, b_ref, o_ref, acc):
        r = pl.program_id(2)

        @pl.when(r == 0)
        def _():
            acc[...] = jnp.zeros_like(acc)

        acc[...] = acc[...] + _dot_tn(a_fn(a_ref[...]), b_fn(b_ref[...]))

        @pl.when(r == nrt - 1)
        def _():
            if split is None:
                o_ref[...] = _bf(acc[...])
            else:
                for j in range(NDEV):
                    o_ref[j] = _bf(acc[:, split * j:split * (j + 1)])

    return pl.pallas_call(
        body, name=name, grid=(N // tn, K // tk, nrt),
        in_specs=[pl.BlockSpec((ts, tk), lambda n, k, r: (r, k)), pl.BlockSpec((ts, tn), lambda n, k, r: (r, n))],
        out_specs=out_spec,
        out_shape=jax.ShapeDtypeStruct(out_shape, BF16),
        scratch_shapes=[pltpu.VMEM((tk, tn), F32)],
        compiler_params=_params(3),
    )(A, B)


def _relu2_bf(a):
    r = jnp.maximum(a, 0.0)
    return _bf(r * r)


def _ident(a):
    return a


def _step(x, p, target, g1, conv_b, gate_b, gn, g_mlp, g_ple, g_fin, sh):
    g_in, g_conv = _gather_weights([sh["w_in"], sh["conv_w"]], [BF16, F32])
    conv_w = g_conv.transpose(1, 0, 2).reshape(4, 1024)
    rc, ra, rb = _rope_tables()
    qkv, mqk, mv, mo, gates, u1, w_in_p = _in_proj(x, g1, g_in, rc, ra, rb)
    attn, lse, (w_up8, w_down8) = _attn_fwd(qkv, [sh["w_up"], sh["w_down"]], [BF16] * 2)
    ml, cs, ns, ms, (w_out8, w_pg8, w_ple8) = _mlstm_fwd(
        mqk, mv, mo, gates, conv_w, conv_b, gate_b, gn, [sh["w_out"], sh["w_ple_gate"], sh["w_ple"]], [BF16] * 3)
    w_out, w_down, w_pg = w_out8.reshape(D, D), w_down8.reshape(DFF, D), w_pg8.reshape(D, D)
    h1, u2 = _out_proj(x, attn, ml, w_out, g_mlp)
    a, h2 = _mlp_fwd(h1, u2, w_up8, w_down)
    dh2, dw_pg, dw_ple8, dg_ple, dg_fin, loss = _ple_loss(h2, p, target, w_pg, w_ple8, g_ple, g_fin)
    da, dh1, dg_mlp = _mlp_bwd(dh2, a, h1, g_mlp, w_up8, w_down)
    dw_up8 = _wgrad("wgrad_up", u2, da, _ident, _ident, D, 512, (NDEV, D, 512),
                    pl.BlockSpec((None, D, 512), lambda n, k, r: (n, 0, 0)))
    dw_down = _wgrad("wgrad_down", a, dh2, _relu2_bf, _bf, 1024, 1024, (DFF, D),
                     pl.BlockSpec((1024, 1024), lambda n, k, r: (k, n)))
    d_attn, d_ml, dw_out = _out_proj_bwd(dh1, attn, ml, w_out)
    (dmqk, dmv, dmo, dgt, dconv_w, dconv_b, dgn, dgate_b), (r_out, r_up, r_pg, r_ple) = _mlstm_bwd(
        mqk, mv, mo, gates, conv_w, conv_b, gate_b, gn, cs, ns, ms, d_ml,
        [dw_out.reshape(NDEV, D // NDEV, D), dw_up8, dw_pg.reshape(NDEV, D // NDEV, D), dw_ple8])
    dq, dk, dv, (r_down,) = _attn_bwd(qkv, attn, lse, d_attn, [dw_down.reshape(NDEV, DFF // NDEV, D)])
    dproj, dx, dg1 = _in_proj_bwd(dq, dk, dv, dmqk, dmv, dmo, dgt, dh1, x, g1, w_in_p, rc, ra, rb)
    dw_in8 = _wgrad("wgrad_in", u1, dproj, _ident, _ident, D, PW, (NDEV, D, IN_W // NDEV),
                    pl.BlockSpec((NDEV, D, IN_W // NDEV), lambda n, k, r: (0, 0, 0)), split=IN_W // NDEV)
    recv_in, recv_conv = _scatter_grads([dw_in8, dconv_w.reshape(4, NDEV, 128).transpose(1, 0, 2)])
    recv = dict(w_in=recv_in, conv_w=recv_conv, w_out=r_out, w_up=r_up, w_down=r_down, w_ple_gate=r_pg, w_ple=r_ple)
    small = dict(norm_mix_g=dg1, conv_b=dconv_b, gate_b=dgate_b, mlstm_norm_g=dgn, norm_mlp_g=dg_mlp,
                 norm_ple_g=dg_ple, final_norm_g=dg_fin)
    return loss, dx, recv, small


def _gather_weights(shards, dtypes):
    nw = len(shards)

    def body(*refs):
        start, forward, finish = _gather_phases(refs[:nw], refs[nw:2 * nw], refs[2 * nw:3 * nw], *refs[3 * nw:])
        start()
        forward()
        finish()

    return pl.pallas_call(
        body, name="gather_weights",
        in_specs=[VM] * nw, out_specs=[ANY] * nw,
        out_shape=_gather_shapes(shards, dtypes),
        scratch_shapes=_gather_scratch(shards, dtypes),
        compiler_params=_params(),
    )(*shards)


def _scatter_grads(parts):
    nw = len(parts)

    def body(*refs):
        start, finish = _scatter_phases(refs[:nw], refs[nw:2 * nw], *refs[2 * nw:])
        start()
        finish()

    return pl.pallas_call(
        body, name="scatter_grads",
        in_specs=[ANY] * nw, out_specs=[ANY] * nw,
        out_shape=[jax.ShapeDtypeStruct(a.shape, a.dtype) for a in parts],
        scratch_shapes=_scatter_scratch(nw),
        compiler_params=_params(),
    )(*parts)


SMALL_ROWS = 64


def _allreduce_small(vals):
    nv = len(vals)

    def body(*refs):
        ins, out_ref = refs[:nv], refs[nv]
        pack, rbuf, send_sems, recv_sems = refs[nv + 1:]
        x, y, c = _place()
        me = _dev_index(x, y, c)
        pack[...] = jnp.zeros_like(pack)
        for i in range(nv):
            pack[8 * i:8 * i + 1, 0:ins[i].shape[1]] = ins[i][...]
        rbuf[me] = pack[...]
        copies = []
        for k, (dx, dy, dc) in enumerate(FLIPS):
            peer = ((x + dx) % 2, (y + dy) % 2, (c + dc) % 2)
            cp = pltpu.make_async_remote_copy(
                src_ref=pack, dst_ref=rbuf.at[me], send_sem=send_sems.at[k], recv_sem=recv_sems.at[k],
                device_id=peer, device_id_type=MESH)
            cp.start()
            copies.append(cp)
        for cp in copies:
            cp.wait()
        tot = rbuf[0]
        for j in range(1, NDEV):
            tot = tot + rbuf[j]
        out_ref[...] = tot

    return pl.pallas_call(
        body, name="allreduce_small",
        in_specs=[VM] * nv, out_specs=VM,
        out_shape=jax.ShapeDtypeStruct((SMALL_ROWS, 1024), F32),
        scratch_shapes=[pltpu.VMEM((SMALL_ROWS, 1024), F32), pltpu.VMEM((NDEV, SMALL_ROWS, 1024), F32),
                        pltpu.SemaphoreType.DMA((7,)), pltpu.SemaphoreType.DMA((7,))],
        compiler_params=_params(),
    )(*vals)


def _adamw(name, gparts, w, m, v, tr):
    P, R, C = gparts.shape
    c1 = 1.0 - ADAM_B1 ** ADAM_STEP
    c2 = 1.0 - ADAM_B2 ** ADAM_STEP

    def body(g_ref, w_ref, m_ref, v_ref, go_ref, d_ref, mo_ref, vo_ref):
        g = g_ref[0].astype(F32)
        for j in range(1, P):
            g = g + g_ref[j].astype(F32)
        m2 = ADAM_B1 * m_ref[...] + (1.0 - ADAM_B1) * g
        v2 = ADAM_B2 * v_ref[...] + (1.0 - ADAM_B2) * (g * g)
        go_ref[...] = g
        mo_ref[...] = m2
        vo_ref[...] = v2
        d_ref[...] = -ADAM_LR * ((m2 / c1) / (jnp.sqrt(v2 / c2) + ADAM_EPS) + ADAM_WD * w_ref[...])

    row = pl.BlockSpec((tr, C), lambda i: (i, 0))
    return pl.pallas_call(
        body, name=name, grid=(R // tr,),
        in_specs=[pl.BlockSpec((P, tr, C), lambda i: (0, i, 0)), row, row, row],
        out_specs=[row] * 4,
        out_shape=[jax.ShapeDtypeStruct((R, C), F32)] * 4,
        compiler_params=_params(1),
    )(gparts, w, m, v)


SMALL = ("norm_mix_g", "conv_b", "gate_b", "mlstm_norm_g", "norm_mlp_g", "norm_ple_g", "final_norm_g")


def _pack_small(vals):
    return jnp.concatenate([jnp.pad(a, ((0, 7), (0, 1024 - a.shape[1]))) for a in vals], axis=0)


def kernel(x, p, norm_mix_g, w_in, conv_w, conv_b, gate_b, mlstm_norm_g, w_out, norm_mlp_g, w_up, w_down, norm_ple_g, w_ple_gate, w_ple, final_norm_g, loss_target, m_norm_mix_g, m_w_in, m_conv_w, m_conv_b, m_gate_b, m_mlstm_norm_g, m_w_out, m_norm_mlp_g, m_w_up, m_w_down, m_norm_ple_g, m_w_ple_gate, m_w_ple, m_final_norm_g, v_norm_mix_g, v_w_in, v_conv_w, v_conv_b, v_gate_b, v_mlstm_norm_g, v_w_out, v_norm_mlp_g, v_w_up, v_w_down, v_norm_ple_g, v_w_ple_gate, v_w_ple, v_final_norm_g):
    big_names = ("w_in", "conv_w", "w_out", "w_up", "w_down", "w_ple_gate", "w_ple")
    wts = dict(w_in=w_in, conv_w=conv_w, w_out=w_out, w_up=w_up, w_down=w_down, w_ple_gate=w_ple_gate, w_ple=w_ple)
    mom = dict(w_in=m_w_in, conv_w=m_conv_w, w_out=m_w_out, w_up=m_w_up, w_down=m_w_down, w_ple_gate=m_w_ple_gate,
               w_ple=m_w_ple)
    var = dict(w_in=v_w_in, conv_w=v_conv_w, w_out=v_w_out, w_up=v_w_up, w_down=v_w_down, w_ple_gate=v_w_ple_gate,
               w_ple=v_w_ple)
    sq = lambda a: a.reshape(a.shape[1:])
    fin = final_norm_g.reshape(1, D)
    loss, dx, recv, small = _step(
        x[0], p[0, 0], loss_target[0], norm_mix_g, conv_b, jnp.pad(gate_b, ((0, 0), (0, 120))), mlstm_norm_g,
        norm_mlp_g, norm_ple_g, fin, {n: sq(wts[n]) for n in big_names})
    total = _allreduce_small([small[n] for n in SMALL] + [loss])

    out = {}
    for n, tr in zip(big_names, (256, 4, 128, 256, 256, 128, 256)):
        res = _adamw("adamw_" + n, recv[n], sq(wts[n]), sq(mom[n]), sq(var[n]), tr)
        out[n] = [t.reshape(wts[n].shape) for t in res]
    sw = dict(norm_mix_g=norm_mix_g, conv_b=conv_b, gate_b=gate_b, mlstm_norm_g=mlstm_norm_g, norm_mlp_g=norm_mlp_g,
              norm_ple_g=norm_ple_g, final_norm_g=fin)
    sm = dict(norm_mix_g=m_norm_mix_g, conv_b=m_conv_b, gate_b=m_gate_b, mlstm_norm_g=m_mlstm_norm_g,
              norm_mlp_g=m_norm_mlp_g, norm_ple_g=m_norm_ple_g, final_norm_g=m_final_norm_g.reshape(1, D))
    sv = dict(norm_mix_g=v_norm_mix_g, conv_b=v_conv_b, gate_b=v_gate_b, mlstm_norm_g=v_mlstm_norm_g,
              norm_mlp_g=v_norm_mlp_g, norm_ple_g=v_norm_ple_g, final_norm_g=v_final_norm_g.reshape(1, D))
    nrow = 8 * len(SMALL)
    res = _adamw("adamw_small", total[0:nrow].reshape(1, nrow, 1024), _pack_small([sw[n] for n in SMALL]),
                 _pack_small([sm[n] for n in SMALL]), _pack_small([sv[n] for n in SMALL]), nrow)
    for i, n in enumerate(SMALL):
        shp = final_norm_g.shape if n == "final_norm_g" else sw[n].shape
        out[n] = [t[8 * i, 0:sw[n].shape[1]].reshape(shp) for t in res]
    order = ("norm_mix_g", "w_in", "conv_w", "conv_b", "gate_b", "mlstm_norm_g", "w_out", "norm_mlp_g", "w_up", "w_down",
             "norm_ple_g", "w_ple_gate", "w_ple", "final_norm_g")
    loss_all = total[nrow, 0]
    return (loss_all, dx[None], *[out[n][0] for n in order], *[out[n][1] for n in order],
            *[out[n][2] for n in order], *[out[n][3] for n in order])
```

```python
import functools
import math

import jax
import jax.numpy as jnp
from jax import lax
from jax.experimental import pallas as pl
from jax.experimental.pallas import tpu as pltpu

F32, BF16 = jnp.float32, jnp.bfloat16
S = 4096
D = 1024
AW = 512
MW = 512
DFF = 4096
PLE = 256
IN_W = 3592
PW = 3840
NDEV = 8
EPS = 1e-6
NEG = -1e30
LC = 64
TB = 256
ROPE_THETA = 500000.0
VMEM_LIMIT = 56 * 1024 * 1024
HI = lax.Precision.HIGHEST

ADAM_LR, ADAM_B1, ADAM_B2, ADAM_EPS, ADAM_WD, ADAM_STEP = 0.001, 0.9, 0.999, 1e-08, 0.01, 10


def _params(n_grid=0, **kw):
    sem = dict(dimension_semantics=("arbitrary",) * n_grid) if n_grid else {}
    return pltpu.CompilerParams(vmem_limit_bytes=VMEM_LIMIT, **sem, **kw)


def _cspec(shape):
    nd = len(shape)
    return pl.BlockSpec(shape, lambda *_: (0,) * nd, pipeline_mode=pl.Buffered(1))


def _dot(a, b):
    return jnp.dot(a, b, preferred_element_type=F32)


def _dot_nt(a, b):
    return lax.dot_general(a, b, (((1,), (1,)), ((), ())), preferred_element_type=F32)


def _dot_tn(a, b):
    return lax.dot_general(a, b, (((0,), (0,)), ((), ())), preferred_element_type=F32)


def _bf(x):
    return x.astype(BF16)


def _rms(x):
    rs = lax.rsqrt(jnp.mean(x * x, axis=-1, keepdims=True) + EPS)
    return x * rs, rs


def _rms_bwd(du, n, rs, g):
    dn = du * g
    return rs * (dn - n * jnp.mean(dn * n, axis=-1, keepdims=True))


def _sigmoid(x):
    return 1.0 / (1.0 + jnp.exp(-x))


def _rope_tables():
    j = lax.broadcasted_iota(jnp.int32, (S, 128), 1) % 64
    pos = lax.broadcasted_iota(jnp.int32, (S, 128), 0).astype(F32)
    inv_freq = jnp.power(ROPE_THETA, -(j % 8).astype(F32) / 8.0)
    ang = pos * inv_freq
    cos, sin = jnp.cos(ang), jnp.sin(ang)
    c = jnp.where(j < 16, cos, 1.0)
    a = jnp.where(j < 8, -sin, 0.0)
    b = jnp.where((j >= 8) & (j < 16), sin, 0.0)
    return c, a, b


def _rope(blk, c, a, b):
    return blk * c + pltpu.roll(blk, 120, 1) * a + pltpu.roll(blk, 8, 1) * b


def _rope_bwd(d, c, a, b):
    return d * c + pltpu.roll(d * a, 8, 1) + pltpu.roll(d * b, 120, 1)


MESH = pl.DeviceIdType.MESH
ANY = pl.BlockSpec(memory_space=pl.ANY)
VM = pl.BlockSpec(memory_space=pltpu.VMEM)
FLIPS = [(dx, dy, dc) for dx in (0, 1) for dy in (0, 1) for dc in (0, 1)][1:]


def _place():
    return lax.axis_index("x"), lax.axis_index("y"), lax.axis_index("c")


def _dev_index(px, py, pc):
    return 4 * px + 2 * py + pc


def _gather_phases(ins, outs, bufs, send_sems=None, recv_sems=None, local_sems=None):
    nw = len(ins)
    if nw == 0:
        return (lambda: None,) * 3
    x, y, c = _place()
    me, sib = (x, y, c), (x, y, 1 - c)
    chips = [(1 - x, y), (x, 1 - y), (1 - x, 1 - y)]

    def copy(w, k, block, to, from_buf=False):
        dst = outs[w].at[_dev_index(*block)]
        return pltpu.make_async_remote_copy(
            src_ref=bufs[w] if from_buf else dst, dst_ref=dst, send_sem=send_sems.at[w, k],
            recv_sem=recv_sems.at[w, k], device_id=to, device_id_type=MESH)

    def mine(w):
        return pltpu.make_async_copy(bufs[w], outs[w].at[_dev_index(*me)], local_sems.at[w])

    def first(w):
        return [copy(w, 0, me, sib, True)] + [copy(w, 1 + j, me, (*chip, c), True) for j, chip in enumerate(chips)]

    def passed(w):
        return [copy(w, 4 + j, (*chip, c), sib) for j, chip in enumerate(chips)]

    def start():
        for w in range(nw):
            bufs[w][...] = ins[w][...].astype(bufs[w].dtype)
        for w in range(nw):
            mine(w).start()
            for cp in first(w):
                cp.start()

    def forward():
        for j, chip in enumerate(chips):
            for w in range(nw):
                copy(w, 1 + j, (*chip, c), me).wait_recv()
                passed(w)[j].start()

    def finish():
        for w in range(nw):
            copy(w, 0, sib, me).wait_recv()
        for j, chip in enumerate(chips):
            for w in range(nw):
                copy(w, 4 + j, (*chip, 1 - c), me).wait_recv()
        for w in range(nw):
            for cp in first(w) + passed(w):
                cp.wait_send()
            mine(w).wait()

    return start, forward, finish


def _gather_scratch(shards, dtypes):
    nw = len(shards)
    if nw == 0:
        return []
    return ([pltpu.VMEM(s.shape, dt) for s, dt in zip(shards, dtypes)]
            + [pltpu.SemaphoreType.DMA((nw, 7)), pltpu.SemaphoreType.DMA((nw, 7)), pltpu.SemaphoreType.DMA((nw,))])


def _gather_shapes(shards, dtypes):
    return [jax.ShapeDtypeStruct((NDEV, *s.shape), dt) for s, dt in zip(shards, dtypes)]


def _scatter_phases(ins, outs, send_sems=None, recv_sems=None, local_sems=None):
    nw = len(ins)
    if nw == 0:
        return (lambda: None,) * 2
    x, y, c = _place()
    me = _dev_index(x, y, c)

    def copies():
        out = []
        for w in range(nw):
            out.append(pltpu.make_async_copy(ins[w].at[me], outs[w].at[me], local_sems.at[w]))
            for k, (dx, dy, dc) in enumerate(FLIPS):
                peer = ((x + dx) % 2, (y + dy) % 2, (c + dc) % 2)
                out.append(pltpu.make_async_remote_copy(
                    src_ref=ins[w].at[_dev_index(*peer)], dst_ref=outs[w].at[me], send_sem=send_sems.at[w, k],
                    recv_sem=recv_sems.at[w, k], device_id=peer, device_id_type=MESH))
        return out

    def start():
        for cp in copies():
            cp.start()

    def finish():
        for cp in copies():
            cp.wait()

    return start, finish


def _scatter_scratch(nw):
    if nw == 0:
        return []
    return [pltpu.SemaphoreType.DMA((nw, 7)), pltpu.SemaphoreType.DMA((nw, 7)), pltpu.SemaphoreType.DMA((nw,))]


def _in_proj(x, g1, wg, rc, ra, rb):
    tm = 256
    sw = IN_W // NDEV

    def body(x_ref, g_ref, wg_ref, rc_ref, ra_ref, rb_ref, qkv_ref, mqk_ref, mv_ref, mo_ref, gt_ref, u_ref, w_ref):
        @pl.when(pl.program_id(0) == 0)
        def _():
            for j in range(NDEV):
                w_ref[:, sw * j:sw * (j + 1)] = wg_ref[j]
            w_ref[:, IN_W:PW] = jnp.zeros((D, PW - IN_W), BF16)

        n, _ = _rms(x_ref[...])
        u = _bf(n * g_ref[...])
        u_ref[...] = u
        c, a, b = rc_ref[...], ra_ref[...], rb_ref[...]
        for half in range(2):
            blk = _dot(u, w_ref[:, half * 512:(half + 1) * 512])
            for t in range(4):
                lo = half * 512 + t * 128
                qkv_ref[:, lo:lo + 128] = _rope(blk[:, t * 128:(t + 1) * 128], c, a, b)
        qkv_ref[:, 1024:1536] = _dot(u, w_ref[:, 1024:1536])
        mqk_ref[:, 0:512] = _dot(u, w_ref[:, 1536:2048])
        mqk_ref[:, 512:1024] = _dot(u, w_ref[:, 2048:2560])
        mv_ref[...] = _dot(u, w_ref[:, 2560:3072])
        mo_ref[...] = _dot(u, w_ref[:, 3072:3584])
        gt_ref[...] = _dot(u, w_ref[:, 3584:3712])

    row = lambda wd: pl.BlockSpec((tm, wd), lambda i: (i, 0))
    return pl.pallas_call(
        body, name="in_proj", grid=(S // tm,),
        in_specs=[row(D), _cspec((1, D)), _cspec((NDEV, D, sw)), row(128), row(128), row(128)],
        out_specs=[row(1536), row(1024), row(512), row(512), row(128), row(D), pl.BlockSpec((D, PW), lambda i: (0, 0))],
        out_shape=[jax.ShapeDtypeStruct((S, 1536), F32), jax.ShapeDtypeStruct((S, 1024), F32),
                   jax.ShapeDtypeStruct((S, 512), F32), jax.ShapeDtypeStruct((S, 512), F32),
                   jax.ShapeDtypeStruct((S, 128), F32), jax.ShapeDtypeStruct((S, D), BF16),
                   jax.ShapeDtypeStruct((D, PW), BF16)],
        compiler_params=_params(1),
    )(x, g1, wg, rc, ra, rb)


DILATIONS = (16, 4, 1)


def _attn_valid(n):
    kd = lax.broadcasted_iota(jnp.int32, (128, 256), 1) - lax.broadcasted_iota(jnp.int32, (128, 256), 0)
    off = jnp.where(n == 0, 0, 128)
    return (kd <= off) & (kd >= off - 128)


def _attn_rows(d, r, n):
    if d == 1:
        q0 = pl.multiple_of(n * 128, 128)
        k0 = pl.multiple_of(jnp.maximum(n - 1, 0) * 128, 128)
        return pl.ds(q0, 128), pl.ds(k0, 256), _attn_valid(n)
    q0 = r + n * 128 * d
    k0 = r + jnp.maximum(n - 1, 0) * 128 * d
    return pl.ds(q0, 128, stride=d), pl.ds(k0, 256, stride=d), _attn_valid(n)


ATTN_GROUP = 4
ATTN_ITERS = S // 128 // ATTN_GROUP


def _attn_group(d, i):
    nb = S // (128 * d)
    if nb == 2:
        qi = lax.broadcasted_iota(jnp.int32, (256, 256), 0) - lax.broadcasted_iota(jnp.int32, (256, 256), 1)
        whole = [pl.ds((ATTN_GROUP // 2) * i + u, 256, stride=d) for u in range(ATTN_GROUP // 2)]
        return [(rows, rows, (qi >= 0) & (qi <= 128)) for rows in whole]
    if d == 1:
        return [_attn_rows(1, 0, i + ATTN_ITERS * u) for u in range(ATTN_GROUP)]
    return [_attn_rows(d, (i // nb) * ATTN_GROUP + u, i % nb) for u in range(ATTN_GROUP)]


def _head0(shape):
    return lax.broadcasted_iota(jnp.int32, shape, 1) < 64


def _stack_heads(t):
    h0 = _head0(t.shape)
    tb = _bf(t)
    zero = jnp.zeros_like(tb)
    return jnp.concatenate([jnp.where(h0, tb, zero), jnp.where(h0, zero, tb)], axis=0)


def _attn_fwd(qkv, shards, dtypes):
    nw = len(shards)

    def body(*refs):
        q_ref, k_ref, v_ref = refs[:3]
        ins = refs[3:3 + nw]
        o_ref, lse_ref = refs[3 + nw:5 + nw]
        outs = refs[5 + nw:5 + 2 * nw]
        m0, m1, l0, l1, acc = refs[5 + 2 * nw:10 + 2 * nw]
        bufs = refs[10 + 2 * nw:10 + 3 * nw]
        ag_start, ag_forward, ag_finish = _gather_phases(ins, outs, bufs, *refs[10 + 3 * nw:])
        hp = pl.program_id(0)
        pl.when(hp == 0)(ag_start)
        pl.when(hp == 2)(ag_forward)
        stats = (m0, m1, l0, l1, acc)

        def update(blocks, first):
            loaded = [([q_ref[rq, :], k_ref[rk, :], v_ref[rk, :]], None if first else [ref[rq, :] for ref in stats])
                      for rq, rk, _ in blocks]
            results = []
            for ((q, k, v), prev), (_, _, valid) in zip(loaded, blocks):
                head0 = _head0(q.shape)
                kb, vb = _bf(k), _bf(v)
                q = q * 0.125
                m_new, l_new, acc_new = [], [], []
                for a, qa in enumerate((_bf(jnp.where(head0, q, 0.0)), _bf(jnp.where(head0, 0.0, q)))):
                    s = jnp.where(valid, _dot_nt(qa, kb), NEG)
                    mc = jnp.max(s, axis=-1, keepdims=True)
                    m_a = jnp.broadcast_to(mc, q.shape) if first else jnp.maximum(prev[a], mc)
                    p = jnp.exp(s - jnp.tile(m_a, (1, 2)))
                    l_add = jnp.sum(p, axis=-1, keepdims=True)
                    pv = _dot(_bf(p), vb)
                    if first:
                        l_a = jnp.broadcast_to(l_add, q.shape)
                    else:
                        alpha = jnp.exp(prev[a] - m_a)
                        l_a, pv = alpha * prev[2 + a] + l_add, alpha * prev[4] + pv
                    m_new.append(m_a), l_new.append(l_a), acc_new.append(pv)
                results.append((m_new[0], m_new[1], l_new[0], l_new[1], jnp.where(head0, acc_new[0], acc_new[1])))
            for (rq, _, _), res in zip(blocks, results):
                for ref, val in zip(stats, res):
                    ref[rq, :] = val

        for d in DILATIONS:
            def step(i, carry, d=d):
                update(_attn_group(d, i), d == DILATIONS[0])
                return carry

            lax.fori_loop(0, ATTN_ITERS, step, 0)

        def fin(t, carry):
            rows = pl.ds(pl.multiple_of(t * 256, 256), 256)
            h0 = lax.broadcasted_iota(jnp.int32, (256, 128), 1) < 64
            l = jnp.where(h0, l0[rows, :], l1[rows, :])
            o_ref[rows, :] = acc[rows, :] / l
            lse_ref[rows, :] = jnp.where(h0, m0[rows, :], m1[rows, :]) + jnp.log(l)
            return carry

        lax.fori_loop(0, S // 256, fin, 0)
        pl.when(hp == 3)(ag_finish)

    col = lambda off: pl.BlockSpec((S, 128), lambda h, off=off: (0, off + h))
    res = pl.pallas_call(
        body, name="attn_fwd", grid=(4,),
        in_specs=[col(0), col(4), col(8)] + [VM] * nw,
        out_specs=[col(0), col(0)] + [ANY] * nw,
        out_shape=[jax.ShapeDtypeStruct((S, AW), F32), jax.ShapeDtypeStruct((S, AW), F32)]
        + _gather_shapes(shards, dtypes),
        scratch_shapes=[pltpu.VMEM((S, 128), F32)] * 5 + _gather_scratch(shards, dtypes),
        compiler_params=_params(1),
    )(qkv, qkv, qkv, *shards)
    return res[0], res[1], res[2:]


def _attn_bwd(qkv, o, lse, do, parts):
    nw = len(parts)

    def body(*refs):
        q_ref, k_ref, v_ref, o_ref, lse_ref, do_ref = refs[:6]
        ins = refs[6:6 + nw]
        dq_ref, dk_ref, dv_ref = refs[6 + nw:9 + nw]
        outs = refs[9 + nw:9 + 2 * nw]
        L0, L1, D0, D1 = refs[9 + 2 * nw:13 + 2 * nw]
        rs_start, rs_finish = _scatter_phases(ins, outs, *refs[13 + 2 * nw:])
        hp = pl.program_id(0)
        pl.when(hp == 0)(rs_start)
        def pre(t, carry):
            rows = pl.ds(pl.multiple_of(t * 256, 256), 256)
            h0 = lax.broadcasted_iota(jnp.int32, (256, 128), 1) < 64
            ls = lse_ref[rows, :]
            dd = do_ref[rows, :] * o_ref[rows, :]
            shp = (256, 128)
            L0[rows, :] = jnp.broadcast_to(jnp.max(jnp.where(h0, ls, NEG), axis=-1, keepdims=True), shp)
            L1[rows, :] = jnp.broadcast_to(jnp.max(jnp.where(h0, NEG, ls), axis=-1, keepdims=True), shp)
            D0[rows, :] = jnp.broadcast_to(jnp.sum(jnp.where(h0, dd, 0.0), axis=-1, keepdims=True), shp)
            D1[rows, :] = jnp.broadcast_to(jnp.sum(jnp.where(h0, 0.0, dd), axis=-1, keepdims=True), shp)
            return carry

        lax.fori_loop(0, S // 256, pre, 0)

        def update(blocks, first):
            loaded = [([q_ref[rq, :], k_ref[rk, :], v_ref[rk, :], do_ref[rq, :]],
                       [L0[rq, :], L1[rq, :], D0[rq, :], D1[rq, :]],
                       [0.0] * 3 if first else [dq_ref[rq, :], dk_ref[rk, :], dv_ref[rk, :]]) for rq, rk, _ in blocks]
            results = []
            for ((q, k, v, dout), (l0v, l1v, d0v, d1v), (dq, dk, dv)), (_, _, valid) in zip(loaded, blocks):
                valid = jnp.tile(valid, (1, 2))
                kst, vst = _stack_heads(k), _stack_heads(v)
                hk = _head0((256, 128))
                dob = _bf(dout)
                cat = lambda a, b: jnp.concatenate([jnp.tile(a, (1, 2)), jnp.tile(b, (1, 2))], axis=1)
                s = jnp.where(valid, _dot_nt(_bf(q * 0.125), kst), NEG)
                p = jnp.exp(s - cat(l0v, l1v))
                ds = _bf(p * (_dot_nt(dob, vst) - cat(d0v, d1v)) * 0.125)
                dk2 = _dot_tn(ds, _bf(q))
                dv2 = _dot_tn(_bf(p), dob)
                results.append((dq + _dot(ds, kst), dk + jnp.where(hk, dk2[0:256], dk2[256:512]),
                                dv + jnp.where(hk, dv2[0:256], dv2[256:512])))
            for (rq, rk, _), (dq, dk, dv) in zip(blocks, results):
                dq_ref[rq, :] = dq
                dk_ref[rk, :] = dk
                dv_ref[rk, :] = dv

        assert S // (128 * DILATIONS[0]) == 2
        for d in DILATIONS:
            def step(i, carry, d=d):
                update(_attn_group(d, i), d == DILATIONS[0])
                return carry

            lax.fori_loop(0, ATTN_ITERS, step, 0)
        pl.when(hp == 3)(rs_finish)

    col = lambda off: pl.BlockSpec((S, 128), lambda h, off=off: (0, off + h))
    res = pl.pallas_call(
        body, name="attn_bwd", grid=(4,),
        in_specs=[col(0), col(4), col(8), col(0), col(0), col(0)] + [ANY] * nw,
        out_specs=[col(0), col(0), col(0)] + [ANY] * nw,
        out_shape=[jax.ShapeDtypeStruct((S, AW), F32)] * 3 + [jax.ShapeDtypeStruct(a.shape, a.dtype) for a in parts],
        scratch_shapes=[pltpu.VMEM((S, 128), F32)] * 4 + _scatter_scratch(nw),
        compiler_params=_params(1),
    )(qkv, qkv, qkv, o, lse, do, *parts)
    return res[0], res[1], res[2], res[3:]


def _logsig(x):
    return jnp.minimum(x, 0.0) - jnp.log1p(jnp.exp(-jnp.abs(x)))


def _conv_taps(xp, n):
    return [xp[8:] if j == 3 else pltpu.roll(xp, 3 - j, 0)[8:] for j in range(4)]


def _conv_silu(xp, w_ref, b_ref, n):
    taps = _conv_taps(xp, n)
    c = b_ref[...] + sum(w_ref[j:j + 1, :] * taps[j] for j in range(4))
    sg = _sigmoid(c)
    return c, sg, taps


def _chunk_gates(G):
    r = lax.broadcasted_iota(jnp.int32, (LC, LC), 0)
    c = lax.broadcasted_iota(jnp.int32, (LC, LC), 1)
    tril = (c <= r).astype(F32)
    triu = (c >= r).astype(F32)
    eye = (c == r).astype(F32)
    logf = _logsig(G)
    b_col = jnp.dot(tril, logf, preferred_element_type=F32, precision=HI)
    b_row = lax.dot_general(logf, triu, (((0,), (0,)), ((), ())), preferred_element_type=F32, precision=HI)
    g_row = lax.dot_general(G, eye, (((0,), (0,)), ((), ())), preferred_element_type=F32, precision=HI)
    return b_col, b_row, g_row, tril, triu


def _colpick(X, lane):
    li = lax.broadcasted_iota(jnp.int32, X.shape, 1)
    return jnp.sum(jnp.where(li == lane, X, 0.0), axis=1, keepdims=True)


def _rowpick(XT, row):
    ri = lax.broadcasted_iota(jnp.int32, XT.shape, 0)
    return jnp.sum(jnp.where(ri == row, XT, 0.0), axis=0, keepdims=True)


def _mlstm_head(qh, kh, vh, G, b_col, b_row, g_row, h, Ch, nh, m_prev):
    bt = _colpick(b_col, 4 + h)
    i_col = _colpick(G, h)
    bs = _rowpick(b_row, 4 + h)
    i_row = _rowpick(g_row, h)
    r = lax.broadcasted_iota(jnp.int32, (LC, LC), 0)
    c = lax.broadcasted_iota(jnp.int32, (LC, LC), 1)
    log_d = jnp.where(c <= r, bt - bs + i_row, NEG)
    log_inter = bt + m_prev
    m_t = jnp.maximum(log_inter, jnp.max(log_d, axis=1, keepdims=True))
    Dm = jnp.exp(log_d - m_t)
    g = jnp.exp(log_inter - m_t)
    qb, kb, vb = _bf(qh), _bf(kh), _bf(vh)
    Am = _dot_nt(qb, kb) * Dm
    qC = _dot(qb, _bf(Ch))
    num = g * qC + _dot(_bf(Am), vb)
    qn = jnp.sum(qh * nh, axis=1, keepdims=True)
    den = g * qn + jnp.sum(Am, axis=1, keepdims=True)
    floor = jnp.exp(-m_t)
    dd = jnp.maximum(jnp.abs(den), floor)
    hh = num / dd
    lane = lax.broadcasted_iota(jnp.int32, (1, LC), 1)
    blast = jnp.sum(jnp.where(lane == LC - 1, bs, 0.0), axis=1, keepdims=True)
    log_s = blast - bt + i_col
    m_new = jnp.maximum(blast + m_prev, jnp.max(log_s, axis=0, keepdims=True))
    decay = jnp.exp(blast + m_prev - m_new)
    ws = jnp.exp(log_s - m_new)
    kw = kh * ws
    C_new = decay * Ch + _dot_tn(_bf(kw), vb)
    n_new = decay * nh + jnp.sum(kw, axis=0, keepdims=True)
    return dict(Dm=Dm, g=g, Am=Am, qC=qC, qn=qn, den=den, floor=floor, dd=dd, h=hh, decay=decay, ws=ws, kw=kw,
                C_new=C_new, n_new=n_new, m_new=m_new, qb=qb, kb=kb, vb=vb)


def _head_out(hh, mo_h, gn_h):
    r = lax.rsqrt(jnp.mean(hh * hh, axis=-1, keepdims=True) + EPS)
    hn = hh * r
    sg = _sigmoid(mo_h)
    return sg * (hn * gn_h), hn, r, sg


def _mlstm_fwd(mqk, mv, mo, gates, conv_w, conv_b, gate_b, gn, shards, dtypes):
    nblk = S // TB
    ncb = TB // LC
    nw = len(shards)

    def body(*refs):
        x_ref, v_ref, o_ref, g_ref, w_ref, b_ref, gb_ref, gn_ref = refs[:8]
        ins = refs[8:8 + nw]
        out_ref, cs_ref, ns_ref, ms_ref = refs[8 + nw:12 + nw]
        outs = refs[12 + nw:12 + 2 * nw]
        tail, Cst, nst, mst, qs, ks = refs[12 + 2 * nw:18 + 2 * nw]
        bufs = refs[18 + 2 * nw:18 + 3 * nw]
        ag_start, ag_forward, ag_finish = _gather_phases(ins, outs, bufs, *refs[18 + 3 * nw:])
        i = pl.program_id(0)
        pl.when(i == 0)(ag_start)
        pl.when(i == nblk // 2)(ag_forward)

        @pl.when(i == 0)
        def _():
            tail[...] = jnp.zeros_like(tail)
            Cst[...] = jnp.zeros_like(Cst)
            nst[...] = jnp.zeros_like(nst)
            mst[...] = jnp.zeros_like(mst)

        x = x_ref[...]
        xp = jnp.concatenate([tail[...], x], axis=0)
        tail[...] = x[TB - 8:TB, :]
        c, sg, _ = _conv_silu(xp, w_ref, b_ref, TB)
        y = c * sg
        qs[...] = y[:, 0:MW]
        ks[...] = y[:, MW:2 * MW] * (1.0 / math.sqrt(128.0))

        for cc in range(ncb):
            rows = slice(cc * LC, (cc + 1) * LC)
            G = g_ref[rows, :] + gb_ref[...]
            b_col, b_row, g_row, _, _ = _chunk_gates(G)
            cs_ref[cc] = Cst[...]
            ns_ref[cc] = nst[...]
            ms_ref[cc] = mst[...]
            for h in range(4):
                ln = slice(h * 128, (h + 1) * 128)
                m_prev = jnp.max(mst[0:1, ln], axis=1, keepdims=True)
                f = _mlstm_head(qs[rows, ln], ks[rows, ln], v_ref[rows, ln], G, b_col, b_row, g_row, h,
                                Cst[:, ln], nst[0:1, ln], m_prev)
                out, _, _, _ = _head_out(f["h"], o_ref[rows, ln], gn_ref[:, ln])
                out_ref[rows, ln] = out
                Cst[:, ln] = f["C_new"]
                nst[0:1, ln] = f["n_new"]
                mst[0:1, ln] = jnp.broadcast_to(f["m_new"], (1, 128))
        pl.when(i == nblk - 1)(ag_finish)

    row = lambda wd: pl.BlockSpec((TB, wd), lambda i: (i, 0))
    res = pl.pallas_call(
        body, name="mlstm_fwd", grid=(nblk,),
        in_specs=[row(1024), row(MW), row(MW), row(128), _cspec((4, 1024)), _cspec((1, 1024)), _cspec((1, 128)),
                  _cspec((1, MW))] + [VM] * nw,
        out_specs=[row(MW), pl.BlockSpec((ncb, 128, MW), lambda i: (i, 0, 0)),
                   pl.BlockSpec((ncb, 8, MW), lambda i: (i, 0, 0)), pl.BlockSpec((ncb, 8, MW), lambda i: (i, 0, 0))]
        + [ANY] * nw,
        out_shape=[jax.ShapeDtypeStruct((S, MW), F32), jax.ShapeDtypeStruct((S // LC, 128, MW), F32),
                   jax.ShapeDtypeStruct((S // LC, 8, MW), F32), jax.ShapeDtypeStruct((S // LC, 8, MW), F32)]
        + _gather_shapes(shards, dtypes),
        scratch_shapes=[pltpu.VMEM((8, 1024), F32), pltpu.VMEM((128, MW), F32), pltpu.VMEM((8, MW), F32),
                        pltpu.VMEM((8, MW), F32), pltpu.VMEM((TB, MW), F32), pltpu.VMEM((TB, MW), F32)]
        + _gather_scratch(shards, dtypes),
        compiler_params=_params(1),
    )(mqk, mv, mo, gates, conv_w, conv_b, gate_b, gn, *shards)
    return res[0], res[1], res[2], res[3], res[4:]


def _mlstm_bwd(mqk, mv, mo, gates, conv_w, conv_b, gate_b, gn, cs, ns, ms, dout, parts):
    nblk = S // TB
    ncb = TB // LC
    kscale = 1.0 / math.sqrt(128.0)
    nw = len(parts)

    def body(*refs):
        x_ref, xprev_ref, v_ref, o_ref, g_ref, w_ref, b_ref, gb_ref, gn_ref, cs_ref, ns_ref, ms_ref, do_ref = refs[:13]
        ins = refs[13:13 + nw]
        dx_ref, dv_ref, dmo_ref, dg_ref, dw_ref, db_ref, dgn_ref, dgb_ref = refs[13 + nw:21 + nw]
        outs = refs[21 + nw:21 + 2 * nw]
        dCst, dnst, dyhead, qs, ks, dqk = refs[21 + 2 * nw:27 + 2 * nw]
        rs_start, rs_finish = _scatter_phases(ins, outs, *refs[27 + 2 * nw:])
        i = pl.program_id(0)
        blk = nblk - 1 - i
        pl.when(i == 0)(rs_start)

        @pl.when(i == 0)
        def _():
            dCst[...] = jnp.zeros_like(dCst)
            dnst[...] = jnp.zeros_like(dnst)
            dyhead[...] = jnp.zeros_like(dyhead)
            dw_ref[...] = jnp.zeros_like(dw_ref)
            db_ref[...] = jnp.zeros_like(db_ref)
            dgn_ref[...] = jnp.zeros_like(dgn_ref)
            dgb_ref[...] = jnp.zeros_like(dgb_ref)

        x = x_ref[...]
        xprev = jnp.where(blk == 0, 0.0, xprev_ref[...])
        xp = jnp.concatenate([xprev, x], axis=0)
        c, sg, taps = _conv_silu(xp, w_ref, b_ref, TB)
        y = c * sg
        qs[...] = y[:, 0:MW]
        ks[...] = y[:, MW:2 * MW] * kscale
        lane128 = lax.broadcasted_iota(jnp.int32, (LC, 128), 1)
        rowi = lax.broadcasted_iota(jnp.int32, (LC, 1), 0)
        ones = jnp.ones((LC, 128), F32)

        for cc in reversed(range(ncb)):
            rows = slice(cc * LC, (cc + 1) * LC)
            G = g_ref[rows, :] + gb_ref[...]
            b_col, b_row, g_row, _, triu = _chunk_gates(G)
            dB = jnp.zeros((LC, 128), F32)
            dI = jnp.zeros((LC, 128), F32)
            for h in range(4):
                ln = slice(h * 128, (h + 1) * 128)
                Ch = cs_ref[cc, :, ln]
                nh = ns_ref[cc, 0:1, ln]
                m_prev = jnp.max(ms_ref[cc, 0:1, ln], axis=1, keepdims=True)
                qh, kh, vh = qs[rows, ln], ks[rows, ln], v_ref[rows, ln]
                f = _mlstm_head(qh, kh, vh, G, b_col, b_row, g_row, h, Ch, nh, m_prev)
                hh, dd, den, g, Am, Dm = f["h"], f["dd"], f["den"], f["g"], f["Am"], f["Dm"]
                qb, kb, vb = f["qb"], f["kb"], f["vb"]
                gn_h = gn_ref[:, ln]
                _, hn, r, sgo = _head_out(hh, o_ref[rows, ln], gn_h)
                do = do_ref[rows, ln]
                hm = hn * gn_h
                dmo_ref[rows, ln] = do * hm * sgo * (1.0 - sgo)
                dhm = do * sgo
                dgn_ref[:, ln] = dgn_ref[:, ln] + jnp.sum(dhm * hn, axis=0, keepdims=True)
                dhn = dhm * gn_h
                dh = r * (dhn - hn * jnp.mean(dhn * hn, axis=-1, keepdims=True))
                dnum = dh / dd
                ddd = -jnp.sum(dh * hh, axis=1, keepdims=True) / dd
                dden = jnp.where(jnp.abs(den) >= f["floor"], ddd * jnp.sign(den), 0.0)
                dnb = _bf(dnum)
                dA = _dot_nt(dnb, vb) + dden
                dv = _dot_tn(_bf(Am), dnb)
                gd = _bf(g * dnum)
                gq = g * dden
                dq = _dot_nt(gd, _bf(Ch)) + gq * nh
                dCn = dCst[:, ln]
                dnn = dnst[0:1, ln]
                dC = f["decay"] * dCn + _dot_tn(qb, gd)
                dn = f["decay"] * dnn + jnp.sum(gq * qh, axis=0, keepdims=True)
                dg = jnp.sum(dnum * f["qC"], axis=1, keepdims=True) + dden * f["qn"]
                dS = _bf(dA * Dm)
                dq = dq + _dot(dS, kb)
                dk = _dot_tn(dS, qb)
                Gm = dA * Am
                gam = dg * g
                dCb = _bf(dCn)
                E = _dot_nt(vb, dCb) + dnn
                ws = f["ws"]
                dk = dk + ws * E
                om = jnp.sum(E * kh, axis=1, keepdims=True) * ws
                dv = dv + _dot(_bf(f["kw"]), dCb)
                ddecay = (jnp.sum(jnp.sum(dCn * Ch, axis=1, keepdims=True), axis=0, keepdims=True)
                          + jnp.sum(dnn * nh, axis=1, keepdims=True))
                delta = ddecay * f["decay"]
                rows_g = jnp.sum(Gm, axis=1, keepdims=True)
                cols_g = lax.dot_general(Gm, ones, (((0,), (0,)), ((), ())), preferred_element_type=F32, precision=HI)
                last = jnp.where(rowi == LC - 1, jnp.sum(om, axis=0, keepdims=True) + delta, 0.0)
                db = rows_g + gam - om + last - cols_g
                di = cols_g + om
                dB = dB + jnp.where(lane128 == 4 + h, db, 0.0)
                dI = dI + jnp.where(lane128 == h, di, 0.0)
                dCst[:, ln] = dC
                dnst[0:1, ln] = dn
                dqk[rows, ln] = dq
                dqk[rows, MW + h * 128:MW + (h + 1) * 128] = dk * kscale
                dv_ref[rows, ln] = dv
            dlogf = jnp.dot(triu, dB, preferred_element_type=F32, precision=HI)
            dG = dI + dlogf * _sigmoid(-G)
            dG = jnp.where(lane128 < 8, dG, 0.0)
            dg_ref[rows, :] = dG
            dgb_ref[...] = dgb_ref[...] + jnp.sum(dG, axis=0, keepdims=True)

        dy = dqk[...] * (sg * (1.0 + c * (1.0 - sg)))
        db_ref[...] = db_ref[...] + jnp.sum(dy, axis=0, keepdims=True)
        for j in range(4):
            dw_ref[j:j + 1, :] = dw_ref[j:j + 1, :] + jnp.sum(dy * taps[j], axis=0, keepdims=True)
        dyp = jnp.concatenate([dy, dyhead[...]], axis=0)
        dx = w_ref[3:4, :] * dy
        for j in range(3):
            dx = dx + w_ref[j:j + 1, :] * pltpu.roll(dyp, TB + 8 - (3 - j), 0)[0:TB]
        dx_ref[...] = dx
        dyhead[...] = dy[0:8, :]
        pl.when(i == nblk - 1)(rs_finish)

    rrow = lambda wd: pl.BlockSpec((TB, wd), lambda i: (nblk - 1 - i, 0))
    st = lambda r: pl.BlockSpec((ncb, r, MW), lambda i: (nblk - 1 - i, 0, 0))
    prev8 = pl.BlockSpec((8, 1024), lambda i: (jnp.maximum((nblk - 1 - i) * (TB // 8) - 1, 0), 0))
    res = pl.pallas_call(
        body, name="mlstm_bwd", grid=(nblk,),
        in_specs=[rrow(1024), prev8, rrow(MW), rrow(MW), rrow(128), _cspec((4, 1024)), _cspec((1, 1024)),
                  _cspec((1, 128)), _cspec((1, MW)), st(128), st(8), st(8), rrow(MW)] + [ANY] * nw,
        out_specs=[rrow(1024), rrow(MW), rrow(MW), rrow(128),
                   pl.BlockSpec((4, 1024), lambda i: (0, 0)), pl.BlockSpec((1, 1024), lambda i: (0, 0)),
                   pl.BlockSpec((1, MW), lambda i: (0, 0)), pl.BlockSpec((1, 128), lambda i: (0, 0))] + [ANY] * nw,
        out_shape=[jax.ShapeDtypeStruct((S, 1024), F32), jax.ShapeDtypeStruct((S, MW), F32),
                   jax.ShapeDtypeStruct((S, MW), F32), jax.ShapeDtypeStruct((S, 128), F32),
                   jax.ShapeDtypeStruct((4, 1024), F32), jax.ShapeDtypeStruct((1, 1024), F32),
                   jax.ShapeDtypeStruct((1, MW), F32), jax.ShapeDtypeStruct((1, 128), F32)]
        + [jax.ShapeDtypeStruct(a.shape, a.dtype) for a in parts],
        scratch_shapes=[pltpu.VMEM((128, MW), F32), pltpu.VMEM((8, MW), F32), pltpu.VMEM((8, 1024), F32),
                        pltpu.VMEM((TB, MW), F32), pltpu.VMEM((TB, MW), F32), pltpu.VMEM((TB, 1024), F32)]
        + _scatter_scratch(nw),
        compiler_params=_params(1),
    )(mqk, mqk, mv, mo, gates, conv_w, conv_b, gate_b, gn, cs, ns, ms, dout, *parts)
    return res[:8], res[8:]


def _out_proj(x, attn, ml, w, g):
    tm = 256

    def body(x_ref, a_ref, m_ref, w_ref, g_ref, h_ref, u_ref):
        h1 = x_ref[...] + _dot(_bf(a_ref[...]), w_ref[0:AW, :]) + _dot(_bf(m_ref[...]), w_ref[AW:D, :])
        h_ref[...] = h1
        n, _ = _rms(h1)
        u_ref[...] = _bf(n * g_ref[...])

    row = lambda wd: pl.BlockSpec((tm, wd), lambda i: (i, 0))
    return pl.pallas_call(
        body, name="out_proj", grid=(S // tm,),
        in_specs=[row(D), row(AW), row(MW), _cspec((D, D)), _cspec((1, D))],
        out_specs=[row(D), row(D)],
        out_shape=[jax.ShapeDtypeStruct((S, D), F32), jax.ShapeDtypeStruct((S, D), BF16)],
        compiler_params=_params(1),
    )(x, attn, ml, w, g)


def _mlp_fwd(h1, u2, w_up, w_down):
    tm = 256

    def body(h_ref, u_ref, wu_ref, wd_ref, a_ref, o_ref):
        u = u_ref[...]
        acc = h_ref[...]
        for c in range(NDEV):
            cols = slice(c * 512, (c + 1) * 512)
            a = _dot(u, wu_ref[c])
            a_ref[:, cols] = a
            r = jnp.maximum(a, 0.0)
            acc = acc + _dot(_bf(r * r), wd_ref[cols, :])
        o_ref[...] = acc

    row = lambda wd: pl.BlockSpec((tm, wd), lambda i: (i, 0))
    return pl.pallas_call(
        body, name="mlp_fwd", grid=(S // tm,),
        in_specs=[row(D), row(D), _cspec((NDEV, D, DFF // NDEV)), _cspec((DFF, D))],
        out_specs=[row(DFF), row(D)],
        out_shape=[jax.ShapeDtypeStruct((S, DFF), F32), jax.ShapeDtypeStruct((S, D), F32)],
        compiler_params=_params(1),
    )(h1, u2, w_up, w_down)


def _ple_loss(h2, p, target, w_pg, w_ple, g_ple, g_fin):
    tm = 256

    def body(h_ref, p_ref, t_ref, wg_ref, wp_ref, gp_ref, gf_ref,
             dh_ref, dwg_ref, dwp_ref, dgp_ref, dgf_ref, loss_ref, acc_g, acc_p):
        i = pl.program_id(0)

        @pl.when(i == 0)
        def _():
            acc_g[...] = jnp.zeros_like(acc_g)
            acc_p[...] = jnp.zeros_like(acc_p)
            dgp_ref[...] = jnp.zeros_like(dgp_ref)
            dgf_ref[...] = jnp.zeros_like(dgf_ref)
            loss_ref[...] = jnp.zeros_like(loss_ref)

        h2v = h_ref[...]
        n2, rs2 = _rms(h2v)
        u3 = _bf(n2 * gp_ref[...])
        gt = _sigmoid(_dot(u3, wg_ref[...]))
        pb = _bf(p_ref[...])
        e = jnp.concatenate([_dot(pb, wp_ref[j]) for j in range(NDEV)], axis=1)
        h3 = h2v + gt * e
        n3, rs3 = _rms(h3)
        err = n3 * gf_ref[...] - t_ref[...]
        loss_ref[...] = loss_ref[...] + 0.5 / D * jnp.sum(jnp.sum(err * err, axis=1, keepdims=True), axis=0, keepdims=True)
        dy = err * (1.0 / D)
        dgf_ref[...] = dgf_ref[...] + jnp.sum(dy * n3, axis=0, keepdims=True)
        dh3 = _rms_bwd(dy, n3, rs3, gf_ref[...])
        de = _bf(dh3 * gt)
        dz = _bf(dh3 * e * gt * (1.0 - gt))
        acc_p[...] = acc_p[...] + _dot_tn(pb, de)
        acc_g[...] = acc_g[...] + _dot_tn(u3, dz)
        du3 = _dot_nt(dz, wg_ref[...])
        dgp_ref[...] = dgp_ref[...] + jnp.sum(du3 * n2, axis=0, keepdims=True)
        dh_ref[...] = dh3 + _rms_bwd(du3, n2, rs2, gp_ref[...])

        @pl.when(i == S // tm - 1)
        def _():
            dwg_ref[...] = _bf(acc_g[...])
            for j in range(NDEV):
                dwp_ref[j] = _bf(acc_p[:, j * 128:(j + 1) * 128])

    row = lambda wd: pl.BlockSpec((tm, wd), lambda i: (i, 0))
    whole = lambda shp: pl.BlockSpec(shp, lambda i: (0,) * len(shp))
    return pl.pallas_call(
        body, name="ple_loss", grid=(S // tm,),
        in_specs=[row(D), row(PLE), row(D), _cspec((D, D)), _cspec((NDEV, PLE, 128)), _cspec((1, D)), _cspec((1, D))],
        out_specs=[row(D), whole((D, D)), whole((NDEV, PLE, 128)), whole((1, D)), whole((1, D)), whole((1, 1))],
        out_shape=[jax.ShapeDtypeStruct((S, D), F32), jax.ShapeDtypeStruct((D, D), BF16),
                   jax.ShapeDtypeStruct((NDEV, PLE, 128), BF16), jax.ShapeDtypeStruct((1, D), F32),
                   jax.ShapeDtypeStruct((1, D), F32), jax.ShapeDtypeStruct((1, 1), F32)],
        scratch_shapes=[pltpu.VMEM((D, D), F32), pltpu.VMEM((PLE, D), F32)],
        compiler_params=_params(1),
    )(h2, p, target, w_pg, w_ple, g_ple, g_fin)


def _mlp_bwd(dh2, a, h1, g, w_up, w_down):
    tm = 256

    def body(d_ref, a_ref, h_ref, g_ref, wu_ref, wd_ref, da_ref, dh1_ref, dg_ref):
        @pl.when(pl.program_id(0) == 0)
        def _():
            dg_ref[...] = jnp.zeros_like(dg_ref)

        dh2v = d_ref[...]
        db = _bf(dh2v)
        du = jnp.zeros((tm, D), F32)
        for c in range(NDEV):
            cols = slice(c * 512, (c + 1) * 512)
            dr = _dot_nt(db, wd_ref[cols, :])
            da = _bf(dr * (2.0 * jnp.maximum(a_ref[:, cols], 0.0)))
            da_ref[:, cols] = da
            du = du + _dot_nt(da, wu_ref[c])
        n, rs = _rms(h_ref[...])
        dg_ref[...] = dg_ref[...] + jnp.sum(du * n, axis=0, keepdims=True)
        dh1_ref[...] = dh2v + _rms_bwd(du, n, rs, g_ref[...])

    row = lambda wd: pl.BlockSpec((tm, wd), lambda i: (i, 0))
    return pl.pallas_call(
        body, name="mlp_bwd", grid=(S // tm,),
        in_specs=[row(D), row(DFF), row(D), _cspec((1, D)), _cspec((NDEV, D, DFF // NDEV)), _cspec((DFF, D))],
        out_specs=[row(DFF), row(D), pl.BlockSpec((1, D), lambda i: (0, 0))],
        out_shape=[jax.ShapeDtypeStruct((S, DFF), BF16), jax.ShapeDtypeStruct((S, D), F32),
                   jax.ShapeDtypeStruct((1, D), F32)],
        compiler_params=_params(1),
    )(dh2, a, h1, g, w_up, w_down)


def _out_proj_bwd(dh1, attn, ml, w):
    tm = 256

    def body(d_ref, a_ref, m_ref, w_ref, da_ref, dm_ref, dw_ref, acc):
        i = pl.program_id(0)

        @pl.when(i == 0)
        def _():
            acc[...] = jnp.zeros_like(acc)

        db = _bf(d_ref[...])
        dmix = _dot_nt(db, w_ref[...])
        da_ref[...] = dmix[:, 0:AW]
        dm_ref[...] = dmix[:, AW:D]
        acc[0:AW, :] = acc[0:AW, :] + _dot_tn(_bf(a_ref[...]), db)
        acc[AW:D, :] = acc[AW:D, :] + _dot_tn(_bf(m_ref[...]), db)

        @pl.when(i == S // tm - 1)
        def _():
            dw_ref[...] = _bf(acc[...])

    row = lambda wd: pl.BlockSpec((tm, wd), lambda i: (i, 0))
    return pl.pallas_call(
        body, name="out_proj_bwd", grid=(S // tm,),
        in_specs=[row(D), row(AW), row(MW), _cspec((D, D))],
        out_specs=[row(AW), row(MW), pl.BlockSpec((D, D), lambda i: (0, 0))],
        out_shape=[jax.ShapeDtypeStruct((S, AW), F32), jax.ShapeDtypeStruct((S, MW), F32),
                   jax.ShapeDtypeStruct((D, D), BF16)],
        scratch_shapes=[pltpu.VMEM((D, D), F32)],
        compiler_params=_params(1),
    )(dh1, attn, ml, w)


def _in_proj_bwd(dq, dk, dv, dmqk, dmv, dmo, dgt, dh1, x, g1, w, rc, ra, rb):
    tm = 256

    def body(dq_ref, dk_ref, dv_ref, dmqk_ref, dmv_ref, dmo_ref, dgt_ref, dh_ref, x_ref, g_ref, w_ref,
             rc_ref, ra_ref, rb_ref, dp_ref, dx_ref, dg_ref):
        @pl.when(pl.program_id(0) == 0)
        def _():
            dg_ref[...] = jnp.zeros_like(dg_ref)

        c, a, b = rc_ref[...], ra_ref[...], rb_ref[...]
        for half, ref in enumerate((dq_ref, dk_ref)):
            for t in range(4):
                lo = half * 512 + t * 128
                dp_ref[:, lo:lo + 128] = _bf(_rope_bwd(ref[:, t * 128:(t + 1) * 128], c, a, b))
        dp_ref[:, 1024:1536] = _bf(dv_ref[...])
        dp_ref[:, 1536:2560] = _bf(dmqk_ref[...])
        dp_ref[:, 2560:3072] = _bf(dmv_ref[...])
        dp_ref[:, 3072:3584] = _bf(dmo_ref[...])
        dp_ref[:, 3584:3712] = _bf(dgt_ref[...])
        dp_ref[:, 3712:PW] = jnp.zeros((tm, PW - 3712), BF16)
        du = jnp.zeros((tm, D), F32)
        for s in range(PW // 768):
            cols = slice(s * 768, (s + 1) * 768)
            du = du + _dot_nt(dp_ref[:, cols], w_ref[:, cols])
        n, rs = _rms(x_ref[...])
        dg_ref[...] = dg_ref[...] + jnp.sum(du * n, axis=0, keepdims=True)
        dx_ref[...] = dh_ref[...] + _rms_bwd(du, n, rs, g_ref[...])

    row = lambda wd: pl.BlockSpec((tm, wd), lambda i: (i, 0))
    return pl.pallas_call(
        body, name="in_proj_bwd", grid=(S // tm,),
        in_specs=[row(AW), row(AW), row(AW), row(1024), row(MW), row(MW), row(128), row(D), row(D), _cspec((1, D)),
                  _cspec((D, PW)), row(128), row(128), row(128)],
        out_specs=[row(PW), row(D), pl.BlockSpec((1, D), lambda i: (0, 0))],
        out_shape=[jax.ShapeDtypeStruct((S, PW), BF16), jax.ShapeDtypeStruct((S, D), F32),
                   jax.ShapeDtypeStruct((1, D), F32)],
        compiler_params=_params(1),
    )(dq, dk, dv, dmqk, dmv, dmo, dgt, dh1, x, g1, w, rc, ra, rb)


def _wgrad(name, A, B, a_fn, b_fn, tk, tn, out_shape, out_spec, ts=512, split=None):
    K, N = A.shape[1], B.shape[1]
    nrt = S // ts

    def body(a_ref, b_ref, o_ref, acc):
        r = pl.program_id(2)

        @pl.when(r == 0)
        def _():
            acc[...] = jnp.zeros_like(acc)

        acc[...] = acc[...] + _dot_tn(a_fn(a_ref[...]), b_fn(b_ref[...]))

        @pl.when(r == nrt - 1)
        def _():
            if split is None:
                o_ref[...] = _bf(acc[...])
            else:
                for j in range(NDEV):
                    o_ref[j] = _bf(acc[:, split * j:split * (j + 1)])

    return pl.pallas_call(
        body, name=name, grid=(N // tn, K // tk, nrt),
        in_specs=[pl.BlockSpec((ts, tk), lambda n, k, r: (r, k)), pl.BlockSpec((ts, tn), lambda n, k, r: (r, n))],
        out_specs=out_spec,
        out_shape=jax.ShapeDtypeStruct(out_shape, BF16),
        scratch_shapes=[pltpu.VMEM((tk, tn), F32)],
        compiler_params=_params(3),
    )(A, B)


def _relu2_bf(a):
    r = jnp.maximum(a, 0.0)
    return _bf(r * r)


def _ident(a):
    return a


def _step(x, p, target, g1, conv_b, gate_b, gn, g_mlp, g_ple, g_fin, sh):
    g_in, g_conv = _gather_weights([sh["w_in"], sh["conv_w"]], [BF16, F32])
    conv_w = g_conv.transpose(1, 0, 2).reshape(4, 1024)
    rc, ra, rb = _rope_tables()
    qkv, mqk, mv, mo, gates, u1, w_in_p = _in_proj(x, g1, g_in, rc, ra, rb)
    attn, lse, (w_up8, w_down8) = _attn_fwd(qkv, [sh["w_up"], sh["w_down"]], [BF16] * 2)
    ml, cs, ns, ms, (w_out8, w_pg8, w_ple8) = _mlstm_fwd(
        mqk, mv, mo, gates, conv_w, conv_b, gate_b, gn, [sh["w_out"], sh["w_ple_gate"], sh["w_ple"]], [BF16] * 3)
    w_out, w_down, w_pg = w_out8.reshape(D, D), w_down8.reshape(DFF, D), w_pg8.reshape(D, D)
    h1, u2 = _out_proj(x, attn, ml, w_out, g_mlp)
    a, h2 = _mlp_fwd(h1, u2, w_up8, w_down)
    dh2, dw_pg, dw_ple8, dg_ple, dg_fin, loss = _ple_loss(h2, p, target, w_pg, w_ple8, g_ple, g_fin)
    da, dh1, dg_mlp = _mlp_bwd(dh2, a, h1, g_mlp, w_up8, w_down)
    dw_up8 = _wgrad("wgrad_up", u2, da, _ident, _ident, D, 512, (NDEV, D, 512),
                    pl.BlockSpec((None, D, 512), lambda n, k, r: (n, 0, 0)))
    dw_down = _wgrad("wgrad_down", a, dh2, _relu2_bf, _bf, 1024, 1024, (DFF, D),
                     pl.BlockSpec((1024, 1024), lambda n, k, r: (k, n)))
    d_attn, d_ml, dw_out = _out_proj_bwd(dh1, attn, ml, w_out)
    (dmqk, dmv, dmo, dgt, dconv_w, dconv_b, dgn, dgate_b), (r_out, r_up, r_pg, r_ple) = _mlstm_bwd(
        mqk, mv, mo, gates, conv_w, conv_b, gate_b, gn, cs, ns, ms, d_ml,
        [dw_out.reshape(NDEV, D // NDEV, D), dw_up8, dw_pg.reshape(NDEV, D // NDEV, D), dw_ple8])
    dq, dk, dv, (r_down,) = _attn_bwd(qkv, attn, lse, d_attn, [dw_down.reshape(NDEV, DFF // NDEV, D)])
    dproj, dx, dg1 = _in_proj_bwd(dq, dk, dv, dmqk, dmv, dmo, dgt, dh1, x, g1, w_in_p, rc, ra, rb)
    dw_in8 = _wgrad("wgrad_in", u1, dproj, _ident, _ident, D, PW, (NDEV, D, IN_W // NDEV),
                    pl.BlockSpec((NDEV, D, IN_W // NDEV), lambda n, k, r: (0, 0, 0)), split=IN_W // NDEV)
    recv_in, recv_conv = _scatter_grads([dw_in8, dconv_w.reshape(4, NDEV, 128).transpose(1, 0, 2)])
    recv = dict(w_in=recv_in, conv_w=recv_conv, w_out=r_out, w_up=r_up, w_down=r_down, w_ple_gate=r_pg, w_ple=r_ple)
    small = dict(norm_mix_g=dg1, conv_b=dconv_b, gate_b=dgate_b, mlstm_norm_g=dgn, norm_mlp_g=dg_mlp,
                 norm_ple_g=dg_ple, final_norm_g=dg_fin)
    return loss, dx, recv, small


def _gather_weights(shards, dtypes):
    nw = len(shards)

    def body(*refs):
        start, forward, finish = _gather_phases(refs[:nw], refs[nw:2 * nw], refs[2 * nw:3 * nw], *refs[3 * nw:])
        start()
        forward()
        finish()

    return pl.pallas_call(
        body, name="gather_weights",
        in_specs=[VM] * nw, out_specs=[ANY] * nw,
        out_shape=_gather_shapes(shards, dtypes),
        scratch_shapes=_gather_scratch(shards, dtypes),
        compiler_params=_params(),
    )(*shards)


def _scatter_grads(parts):
    nw = len(parts)

    def body(*refs):
        start, finish = _scatter_phases(refs[:nw], refs[nw:2 * nw], *refs[2 * nw:])
        start()
        finish()

    return pl.pallas_call(
        body, name="scatter_grads",
        in_specs=[ANY] * nw, out_specs=[ANY] * nw,
        out_shape=[jax.ShapeDtypeStruct(a.shape, a.dtype) for a in parts],
        scratch_shapes=_scatter_scratch(nw),
        compiler_params=_params(),
    )(*parts)


SMALL_ROWS = 64


def _allreduce_small(vals):
    nv = len(vals)

    def body(*refs):
        ins, out_ref = refs[:nv], refs[nv]
        pack, rbuf, send_sems, recv_sems = refs[nv + 1:]
        x, y, c = _place()
        me = _dev_index(x, y, c)
        pack[...] = jnp.zeros_like(pack)
        for i in range(nv):
            pack[8 * i:8 * i + 1, 0:ins[i].shape[1]] = ins[i][...]
        rbuf[me] = pack[...]
        copies = []
        for k, (dx, dy, dc) in enumerate(FLIPS):
            peer = ((x + dx) % 2, (y + dy) % 2, (c + dc) % 2)
            cp = pltpu.make_async_remote_copy(
                src_ref=pack, dst_ref=rbuf.at[me], send_sem=send_sems.at[k], recv_sem=recv_sems.at[k],
                device_id=peer, device_id_type=MESH)
            cp.start()
            copies.append(cp)
        for cp in copies:
            cp.wait()
        tot = rbuf[0]
        for j in range(1, NDEV):
            tot = tot + rbuf[j]
        out_ref[...] = tot

    return pl.pallas_call(
        body, name="allreduce_small",
        in_specs=[VM] * nv, out_specs=VM,
        out_shape=jax.ShapeDtypeStruct((SMALL_ROWS, 1024), F32),
        scratch_shapes=[pltpu.VMEM((SMALL_ROWS, 1024), F32), pltpu.VMEM((NDEV, SMALL_ROWS, 1024), F32),
                        pltpu.SemaphoreType.DMA((7,)), pltpu.SemaphoreType.DMA((7,))],
        compiler_params=_params(),
    )(*vals)


def _adamw(name, gparts, w, m, v, tr):
    P, R, C = gparts.shape
    c1 = 1.0 - ADAM_B1 ** ADAM_STEP
    c2 = 1.0 - ADAM_B2 ** ADAM_STEP

    def body(g_ref, w_ref, m_ref, v_ref, go_ref, d_ref, mo_ref, vo_ref):
        g = g_ref[0].astype(F32)
        for j in range(1, P):
            g = g + g_ref[j].astype(F32)
        m2 = ADAM_B1 * m_ref[...] + (1.0 - ADAM_B1) * g
        v2 = ADAM_B2 * v_ref[...] + (1.0 - ADAM_B2) * (g * g)
        go_ref[...] = g
        mo_ref[...] = m2
        vo_ref[...] = v2
        d_ref[...] = -ADAM_LR * ((m2 / c1) / (jnp.sqrt(v2 / c2) + ADAM_EPS) + ADAM_WD * w_ref[...])

    row = pl.BlockSpec((tr, C), lambda i: (i, 0))
    return pl.pallas_call(
        body, name=name, grid=(R // tr,),
        in_specs=[pl.BlockSpec((P, tr, C), lambda i: (0, i, 0)), row, row, row],
        out_specs=[row] * 4,
        out_shape=[jax.ShapeDtypeStruct((R, C), F32)] * 4,
        compiler_params=_params(1),
    )(gparts, w, m, v)


SMALL = ("norm_mix_g", "conv_b", "gate_b", "mlstm_norm_g", "norm_mlp_g", "norm_ple_g", "final_norm_g")


def _pack_small(vals):
    return jnp.concatenate([jnp.pad(a, ((0, 7), (0, 1024 - a.shape[1]))) for a in vals], axis=0)


def kernel(x, p, norm_mix_g, w_in, conv_w, conv_b, gate_b, mlstm_norm_g, w_out, norm_mlp_g, w_up, w_down, norm_ple_g, w_ple_gate, w_ple, final_norm_g, loss_target, m_norm_mix_g, m_w_in, m_conv_w, m_conv_b, m_gate_b, m_mlstm_norm_g, m_w_out, m_norm_mlp_g, m_w_up, m_w_down, m_norm_ple_g, m_w_ple_gate, m_w_ple, m_final_norm_g, v_norm_mix_g, v_w_in, v_conv_w, v_conv_b, v_gate_b, v_mlstm_norm_g, v_w_out, v_norm_mlp_g, v_w_up, v_w_down, v_norm_ple_g, v_w_ple_gate, v_w_ple, v_final_norm_g):
    big_names = ("w_in", "conv_w", "w_out", "w_up", "w_down", "w_ple_gate", "w_ple")
    wts = dict(w_in=w_in, conv_w=conv_w, w_out=w_out, w_up=w_up, w_down=w_down, w_ple_gate=w_ple_gate, w_ple=w_ple)
    mom = dict(w_in=m_w_in, conv_w=m_conv_w, w_out=m_w_out, w_up=m_w_up, w_down=m_w_down, w_ple_gate=m_w_ple_gate,
               w_ple=m_w_ple)
    var = dict(w_in=v_w_in, conv_w=v_conv_w, w_out=v_w_out, w_up=v_w_up, w_down=v_w_down, w_ple_gate=v_w_ple_gate,
               w_ple=v_w_ple)
    sq = lambda a: a.reshape(a.shape[1:])
    fin = final_norm_g.reshape(1, D)
    loss, dx, recv, small = _step(
        x[0], p[0, 0], loss_target[0], norm_mix_g, conv_b, jnp.pad(gate_b, ((0, 0), (0, 120))), mlstm_norm_g,
        norm_mlp_g, norm_ple_g, fin, {n: sq(wts[n]) for n in big_names})
    total = _allreduce_small([small[n] for n in SMALL] + [loss])

    out = {}
    for n, tr in zip(big_names, (256, 4, 128, 256, 256, 128, 256)):
        res = _adamw("adamw_" + n, recv[n], sq(wts[n]), sq(mom[n]), sq(var[n]), tr)
        out[n] = [t.reshape(wts[n].shape) for t in res]
    sw = dict(norm_mix_g=norm_mix_g, conv_b=conv_b, gate_b=gate_b, mlstm_norm_g=mlstm_norm_g, norm_mlp_g=norm_mlp_g,
              norm_ple_g=norm_ple_g, final_norm_g=fin)
    sm = dict(norm_mix_g=m_norm_mix_g, conv_b=m_conv_b, gate_b=m_gate_b, mlstm_norm_g=m_mlstm_norm_g,
              norm_mlp_g=m_norm_mlp_g, norm_ple_g=m_norm_ple_g, final_norm_g=m_final_norm_g.reshape(1, D))
    sv = dict(norm_mix_g=v_norm_mix_g, conv_b=v_conv_b, gate_b=v_gate_b, mlstm_norm_g=v_mlstm_norm_g,
              norm_mlp_g=v_norm_mlp_g, norm_ple_g=v_norm_ple_g, final_norm_g=v_final_norm_g.reshape(1, D))
    nrow = 8 * len(SMALL)
    res = _adamw("adamw_small", total[0:nrow].reshape(1, nrow, 1024), _pack_small([sw[n] for n in SMALL]),
                 _pack_small([sm[n] for n in SMALL]), _pack_small([sv[n] for n in SMALL]), nrow)
    for i, n in enumerate(SMALL):
        shp = final_norm_g.shape if n == "final_norm_g" else sw[n].shape
        out[n] = [t[8 * i, 0:sw[n].shape[1]].reshape(shp) for t in res]
    order = ("norm_mix_g", "w_in", "conv_w", "conv_b", "gate_b", "mlstm_norm_g", "w_out", "norm_mlp_g", "w_up", "w_down",
             "norm_ple_g", "w_ple_gate", "w_ple", "final_norm_g")
    loss_all = total[nrow, 0]
    return (loss_all, dx[None], *[out[n][0] for n in order], *[out[n][1] for n in order],
            *[out[n][2] for n in order], *[out[n][3] for n in order])
```

```python
import functools
import math

import jax
import jax.numpy as jnp
from jax import lax
from jax.experimental import pallas as pl
from jax.experimental.pallas import tpu as pltpu

F32, BF16 = jnp.float32, jnp.bfloat16
S = 4096
D = 1024
AW = 512
MW = 512
DFF = 4096
PLE = 256
IN_W = 3592
PW = 3840
NDEV = 8
EPS = 1e-6
NEG = -1e30
LC = 128
TB = 256
ROPE_THETA = 500000.0
VMEM_LIMIT = 56 * 1024 * 1024
HI = lax.Precision.HIGHEST

ADAM_LR, ADAM_B1, ADAM_B2, ADAM_EPS, ADAM_WD, ADAM_STEP = 0.001, 0.9, 0.999, 1e-08, 0.01, 10


def _params(n_grid=0, **kw):
    sem = dict(dimension_semantics=("arbitrary",) * n_grid) if n_grid else {}
    return pltpu.CompilerParams(vmem_limit_bytes=VMEM_LIMIT, **sem, **kw)


def _cspec(shape):
    nd = len(shape)
    return pl.BlockSpec(shape, lambda *_: (0,) * nd, pipeline_mode=pl.Buffered(1))


def _dot(a, b):
    return jnp.dot(a, b, preferred_element_type=F32)


def _dot_nt(a, b):
    return lax.dot_general(a, b, (((1,), (1,)), ((), ())), preferred_element_type=F32)


def _dot_tn(a, b):
    return lax.dot_general(a, b, (((0,), (0,)), ((), ())), preferred_element_type=F32)


def _bf(x):
    return x.astype(BF16)


def _rms(x):
    rs = lax.rsqrt(jnp.mean(x * x, axis=-1, keepdims=True) + EPS)
    return x * rs, rs


def _rms_bwd(du, n, rs, g):
    dn = du * g
    return rs * (dn - n * jnp.mean(dn * n, axis=-1, keepdims=True))


def _sigmoid(x):
    return 1.0 / (1.0 + jnp.exp(-x))


def _rope_tables():
    j = lax.broadcasted_iota(jnp.int32, (S, 128), 1) % 64
    pos = lax.broadcasted_iota(jnp.int32, (S, 128), 0).astype(F32)
    inv_freq = jnp.power(ROPE_THETA, -(j % 8).astype(F32) / 8.0)
    ang = pos * inv_freq
    cos, sin = jnp.cos(ang), jnp.sin(ang)
    c = jnp.where(j < 16, cos, 1.0)
    a = jnp.where(j < 8, -sin, 0.0)
    b = jnp.where((j >= 8) & (j < 16), sin, 0.0)
    return c, a, b


def _rope(blk, c, a, b):
    return blk * c + pltpu.roll(blk, 120, 1) * a + pltpu.roll(blk, 8, 1) * b


def _rope_bwd(d, c, a, b):
    return d * c + pltpu.roll(d * a, 8, 1) + pltpu.roll(d * b, 120, 1)


MESH = pl.DeviceIdType.MESH
ANY = pl.BlockSpec(memory_space=pl.ANY)
VM = pl.BlockSpec(memory_space=pltpu.VMEM)
FLIPS = [(dx, dy, dc) for dx in (0, 1) for dy in (0, 1) for dc in (0, 1)][1:]


def _place():
    return lax.axis_index("x"), lax.axis_index("y"), lax.axis_index("c")


def _dev_index(px, py, pc):
    return 4 * px + 2 * py + pc


def _gather_phases(ins, outs, bufs, send_sems=None, recv_sems=None, local_sems=None):
    nw = len(ins)
    if nw == 0:
        return (lambda: None,) * 3
    x, y, c = _place()
    me, sib = (x, y, c), (x, y, 1 - c)
    chips = [(1 - x, y), (x, 1 - y), (1 - x, 1 - y)]

    def copy(w, k, block, to, from_buf=False):
        dst = outs[w].at[_dev_index(*block)]
        return pltpu.make_async_remote_copy(
            src_ref=bufs[w] if from_buf else dst, dst_ref=dst, send_sem=send_sems.at[w, k],
            recv_sem=recv_sems.at[w, k], device_id=to, device_id_type=MESH)

    def mine(w):
        return pltpu.make_async_copy(bufs[w], outs[w].at[_dev_index(*me)], local_sems.at[w])

    def first(w):
        return [copy(w, 0, me, sib, True)] + [copy(w, 1 + j, me, (*chip, c), True) for j, chip in enumerate(chips)]

    def passed(w):
        return [copy(w, 4 + j, (*chip, c), sib) for j, chip in enumerate(chips)]

    def start():
        for w in range(nw):
            bufs[w][...] = ins[w][...].astype(bufs[w].dtype)
        for w in range(nw):
            mine(w).start()
            for cp in first(w):
                cp.start()

    def forward():
        for j, chip in enumerate(chips):
            for w in range(nw):
                copy(w, 1 + j, (*chip, c), me).wait_recv()
                passed(w)[j].start()

    def finish():
        for w in range(nw):
            copy(w, 0, sib, me).wait_recv()
        for j, chip in enumerate(chips):
            for w in range(nw):
                copy(w, 4 + j, (*chip, 1 - c), me).wait_recv()
        for w in range(nw):
            for cp in first(w) + passed(w):
                cp.wait_send()
            mine(w).wait()

    return start, forward, finish


def _gather_scratch(shards, dtypes):
    nw = len(shards)
    if nw == 0:
        return []
    return ([pltpu.VMEM(s.shape, dt) for s, dt in zip(shards, dtypes)]
            + [pltpu.SemaphoreType.DMA((nw, 7)), pltpu.SemaphoreType.DMA((nw, 7)), pltpu.SemaphoreType.DMA((nw,))])


def _gather_shapes(shards, dtypes):
    return [jax.ShapeDtypeStruct((NDEV, *s.shape), dt) for s, dt in zip(shards, dtypes)]


def _scatter_phases(ins, outs, send_sems=None, recv_sems=None, local_sems=None):
    nw = len(ins)
    if nw == 0:
        return (lambda: None,) * 2
    x, y, c = _place()
    me = _dev_index(x, y, c)

    def copies():
        out = []
        for w in range(nw):
            out.append(pltpu.make_async_copy(ins[w].at[me], outs[w].at[me], local_sems.at[w]))
            for k, (dx, dy, dc) in enumerate(FLIPS):
                peer = ((x + dx) % 2, (y + dy) % 2, (c + dc) % 2)
                out.append(pltpu.make_async_remote_copy(
                    src_ref=ins[w].at[_dev_index(*peer)], dst_ref=outs[w].at[me], send_sem=send_sems.at[w, k],
                    recv_sem=recv_sems.at[w, k], device_id=peer, device_id_type=MESH))
        return out

    def start():
        for cp in copies():
            cp.start()

    def finish():
        for cp in copies():
            cp.wait()

    return start, finish


def _scatter_scratch(nw):
    if nw == 0:
        return []
    return [pltpu.SemaphoreType.DMA((nw, 7)), pltpu.SemaphoreType.DMA((nw, 7)), pltpu.SemaphoreType.DMA((nw,))]


def _in_proj(x, g1, wg, rc, ra, rb):
    tm = 256
    sw = IN_W // NDEV

    def body(x_ref, g_ref, wg_ref, rc_ref, ra_ref, rb_ref, qkv_ref, mqk_ref, mv_ref, mo_ref, gt_ref, u_ref, w_ref):
        @pl.when(pl.program_id(0) == 0)
        def _():
            for j in range(NDEV):
                w_ref[:, sw * j:sw * (j + 1)] = wg_ref[j]
            w_ref[:, IN_W:PW] = jnp.zeros((D, PW - IN_W), BF16)

        n, _ = _rms(x_ref[...])
        u = _bf(n * g_ref[...])
        u_ref[...] = u
        c, a, b = rc_ref[...], ra_ref[...], rb_ref[...]
        for half in range(2):
            blk = _dot(u, w_ref[:, half * 512:(half + 1) * 512])
            for t in range(4):
                lo = half * 512 + t * 128
                qkv_ref[:, lo:lo + 128] = _rope(blk[:, t * 128:(t + 1) * 128], c, a, b)
        qkv_ref[:, 1024:1536] = _dot(u, w_ref[:, 1024:1536])
        mqk_ref[:, 0:512] = _dot(u, w_ref[:, 1536:2048])
        mqk_ref[:, 512:1024] = _dot(u, w_ref[:, 2048:2560])
        mv_ref[...] = _dot(u, w_ref[:, 2560:3072])
        mo_ref[...] = _dot(u, w_ref[:, 3072:3584])
        gt_ref[...] = _dot(u, w_ref[:, 3584:3712])

    row = lambda wd: pl.BlockSpec((tm, wd), lambda i: (i, 0))
    return pl.pallas_call(
        body, name="in_proj", grid=(S // tm,),
        in_specs=[row(D), _cspec((1, D)), _cspec((NDEV, D, sw)), row(128), row(128), row(128)],
        out_specs=[row(1536), row(1024), row(512), row(512), row(128), row(D), pl.BlockSpec((D, PW), lambda i: (0, 0))],
        out_shape=[jax.ShapeDtypeStruct((S, 1536), F32), jax.ShapeDtypeStruct((S, 1024), F32),
                   jax.ShapeDtypeStruct((S, 512), F32), jax.ShapeDtypeStruct((S, 512), F32),
                   jax.ShapeDtypeStruct((S, 128), F32), jax.ShapeDtypeStruct((S, D), BF16),
                   jax.ShapeDtypeStruct((D, PW), BF16)],
        compiler_params=_params(1),
    )(x, g1, wg, rc, ra, rb)


DILATIONS = (16, 4, 1)


def _attn_valid(n):
    kd = lax.broadcasted_iota(jnp.int32, (128, 256), 1) - lax.broadcasted_iota(jnp.int32, (128, 256), 0)
    off = jnp.where(n == 0, 0, 128)
    return (kd <= off) & (kd >= off - 128)


def _attn_rows(d, r, n):
    if d == 1:
        q0 = pl.multiple_of(n * 128, 128)
        k0 = pl.multiple_of(jnp.maximum(n - 1, 0) * 128, 128)
        return pl.ds(q0, 128), pl.ds(k0, 256), _attn_valid(n)
    q0 = r + n * 128 * d
    k0 = r + jnp.maximum(n - 1, 0) * 128 * d
    return pl.ds(q0, 128, stride=d), pl.ds(k0, 256, stride=d), _attn_valid(n)


ATTN_GROUP = 4
ATTN_ITERS = S // 128 // ATTN_GROUP


def _attn_group(d, i):
    nb = S // (128 * d)
    if nb == 2:
        qi = lax.broadcasted_iota(jnp.int32, (256, 256), 0) - lax.broadcasted_iota(jnp.int32, (256, 256), 1)
        whole = [pl.ds((ATTN_GROUP // 2) * i + u, 256, stride=d) for u in range(ATTN_GROUP // 2)]
        return [(rows, rows, (qi >= 0) & (qi <= 128)) for rows in whole]
    if d == 1:
        return [_attn_rows(1, 0, i + ATTN_ITERS * u) for u in range(ATTN_GROUP)]
    return [_attn_rows(d, (i // nb) * ATTN_GROUP + u, i % nb) for u in range(ATTN_GROUP)]


def _head0(shape):
    return lax.broadcasted_iota(jnp.int32, shape, 1) < 64


def _stack_heads(t):
    h0 = _head0(t.shape)
    tb = _bf(t)
    zero = jnp.zeros_like(tb)
    return jnp.concatenate([jnp.where(h0, tb, zero), jnp.where(h0, zero, tb)], axis=0)


def _attn_fwd(qkv, shards, dtypes):
    nw = len(shards)

    def body(*refs):
        q_ref, k_ref, v_ref = refs[:3]
        ins = refs[3:3 + nw]
        o_ref, lse_ref = refs[3 + nw:5 + nw]
        outs = refs[5 + nw:5 + 2 * nw]
        m0, m1, l0, l1, acc = refs[5 + 2 * nw:10 + 2 * nw]
        bufs = refs[10 + 2 * nw:10 + 3 * nw]
        ag_start, ag_forward, ag_finish = _gather_phases(ins, outs, bufs, *refs[10 + 3 * nw:])
        hp = pl.program_id(0)
        pl.when(hp == 0)(ag_start)
        pl.when(hp == 3)(ag_forward)
        stats = (m0, m1, l0, l1, acc)

        def update(blocks, first):
            loaded = [([q_ref[rq, :], k_ref[rk, :], v_ref[rk, :]], None if first else [ref[rq, :] for ref in stats])
                      for rq, rk, _ in blocks]
            results = []
            for ((q, k, v), prev), (_, _, valid) in zip(loaded, blocks):
                head0 = _head0(q.shape)
                kb, vb = _bf(k), _bf(v)
                q = q * 0.125
                m_new, l_new, acc_new = [], [], []
                for a, qa in enumerate((_bf(jnp.where(head0, q, 0.0)), _bf(jnp.where(head0, 0.0, q)))):
                    s = jnp.where(valid, _dot_nt(qa, kb), NEG)
                    mc = jnp.max(s, axis=-1, keepdims=True)
                    m_a = jnp.broadcast_to(mc, q.shape) if first else jnp.maximum(prev[a], mc)
                    p = jnp.exp(s - jnp.tile(m_a, (1, 2)))
                    l_add = jnp.sum(p, axis=-1, keepdims=True)
                    pv = _dot(_bf(p), vb)
                    if first:
                        l_a = jnp.broadcast_to(l_add, q.shape)
                    else:
                        alpha = jnp.exp(prev[a] - m_a)
                        l_a, pv = alpha * prev[2 + a] + l_add, alpha * prev[4] + pv
                    m_new.append(m_a), l_new.append(l_a), acc_new.append(pv)
                results.append((m_new[0], m_new[1], l_new[0], l_new[1], jnp.where(head0, acc_new[0], acc_new[1])))
            for (rq, _, _), res in zip(blocks, results):
                for ref, val in zip(stats, res):
                    ref[rq, :] = val

        for d in DILATIONS:
            def step(i, carry, d=d):
                update(_attn_group(d, i), d == DILATIONS[0])
                return carry

            lax.fori_loop(0, ATTN_ITERS, step, 0)

        def fin(t, carry):
            rows = pl.ds(pl.multiple_of(t * 256, 256), 256)
            h0 = lax.broadcasted_iota(jnp.int32, (256, 128), 1) < 64
            l = jnp.where(h0, l0[rows, :], l1[rows, :])
            o_ref[rows, :] = acc[rows, :] / l
            lse_ref[rows, :] = jnp.where(h0, m0[rows, :], m1[rows, :]) + jnp.log(l)
            return carry

        lax.fori_loop(0, S // 256, fin, 0)
        pl.when(hp == 3)(ag_finish)

    col = lambda off: pl.BlockSpec((S, 128), lambda h, off=off: (0, off + h))
    res = pl.pallas_call(
        body, name="attn_fwd", grid=(4,),
        in_specs=[col(0), col(4), col(8)] + [VM] * nw,
        out_specs=[col(0), col(0)] + [ANY] * nw,
        out_shape=[jax.ShapeDtypeStruct((S, AW), F32), jax.ShapeDtypeStruct((S, AW), F32)]
        + _gather_shapes(shards, dtypes),
        scratch_shapes=[pltpu.VMEM((S, 128), F32)] * 5 + _gather_scratch(shards, dtypes),
        compiler_params=_params(1),
    )(qkv, qkv, qkv, *shards)
    return res[0], res[1], res[2:]


def _attn_bwd(qkv, o, lse, do, parts):
    nw = len(parts)

    def body(*refs):
        q_ref, k_ref, v_ref, o_ref, lse_ref, do_ref = refs[:6]
        ins = refs[6:6 + nw]
        dq_ref, dk_ref, dv_ref = refs[6 + nw:9 + nw]
        outs = refs[9 + nw:9 + 2 * nw]
        L0, L1, D0, D1 = refs[9 + 2 * nw:13 + 2 * nw]
        rs_start, rs_finish = _scatter_phases(ins, outs, *refs[13 + 2 * nw:])
        hp = pl.program_id(0)
        pl.when(hp == 0)(rs_start)
        def pre(t, carry):
            rows = pl.ds(pl.multiple_of(t * 256, 256), 256)
            h0 = lax.broadcasted_iota(jnp.int32, (256, 128), 1) < 64
            ls = lse_ref[rows, :]
            dd = do_ref[rows, :] * o_ref[rows, :]
            shp = (256, 128)
            L0[rows, :] = jnp.broadcast_to(jnp.max(jnp.where(h0, ls, NEG), axis=-1, keepdims=True), shp)
            L1[rows, :] = jnp.broadcast_to(jnp.max(jnp.where(h0, NEG, ls), axis=-1, keepdims=True), shp)
            D0[rows, :] = jnp.broadcast_to(jnp.sum(jnp.where(h0, dd, 0.0), axis=-1, keepdims=True), shp)
            D1[rows, :] = jnp.broadcast_to(jnp.sum(jnp.where(h0, 0.0, dd), axis=-1, keepdims=True), shp)
            return carry

        lax.fori_loop(0, S // 256, pre, 0)

        def update(blocks, first):
            loaded = [([q_ref[rq, :], k_ref[rk, :], v_ref[rk, :], do_ref[rq, :]],
                       [L0[rq, :], L1[rq, :], D0[rq, :], D1[rq, :]],
                       [0.0] * 3 if first else [dq_ref[rq, :], dk_ref[rk, :], dv_ref[rk, :]]) for rq, rk, _ in blocks]
            results = []
            for ((q, k, v, dout), (l0v, l1v, d0v, d1v), (dq, dk, dv)), (_, _, valid) in zip(loaded, blocks):
                valid = jnp.tile(valid, (1, 2))
                kst, vst = _stack_heads(k), _stack_heads(v)
                hk = _head0((256, 128))
                dob = _bf(dout)
                cat = lambda a, b: jnp.concatenate([jnp.tile(a, (1, 2)), jnp.tile(b, (1, 2))], axis=1)
                s = jnp.where(valid, _dot_nt(_bf(q * 0.125), kst), NEG)
                p = jnp.exp(s - cat(l0v, l1v))
                ds = _bf(p * (_dot_nt(dob, vst) - cat(d0v, d1v)) * 0.125)
                dk2 = _dot_tn(ds, _bf(q))
                dv2 = _dot_tn(_bf(p), dob)
                results.append((dq + _dot(ds, kst), dk + jnp.where(hk, dk2[0:256], dk2[256:512]),
                                dv + jnp.where(hk, dv2[0:256], dv2[256:512])))
            for (rq, rk, _), (dq, dk, dv) in zip(blocks, results):
                dq_ref[rq, :] = dq
                dk_ref[rk, :] = dk
                dv_ref[rk, :] = dv

        assert S // (128 * DILATIONS[0]) == 2
        for d in DILATIONS:
            def step(i, carry, d=d):
                update(_attn_group(d, i), d == DILATIONS[0])
                return carry

            lax.fori_loop(0, ATTN_ITERS, step, 0)
        pl.when(hp == 3)(rs_finish)

    col = lambda off: pl.BlockSpec((S, 128), lambda h, off=off: (0, off + h))
    res = pl.pallas_call(
        body, name="attn_bwd", grid=(4,),
        in_specs=[col(0), col(4), col(8), col(0), col(0), col(0)] + [ANY] * nw,
        out_specs=[col(0), col(0), col(0)] + [ANY] * nw,
        out_shape=[jax.ShapeDtypeStruct((S, AW), F32)] * 3 + [jax.ShapeDtypeStruct(a.shape, a.dtype) for a in parts],
        scratch_shapes=[pltpu.VMEM((S, 128), F32)] * 4 + _scatter_scratch(nw),
        compiler_params=_params(1),
    )(qkv, qkv, qkv, o, lse, do, *parts)
    return res[0], res[1], res[2], res[3:]


def _logsig(x):
    return jnp.minimum(x, 0.0) - jnp.log1p(jnp.exp(-jnp.abs(x)))


def _conv_taps(xp, n):
    return [xp[8:] if j == 3 else pltpu.roll(xp, 3 - j, 0)[8:] for j in range(4)]


def _conv_silu(xp, w_ref, b_ref, n):
    taps = _conv_taps(xp, n)
    c = b_ref[...] + sum(w_ref[j:j + 1, :] * taps[j] for j in range(4))
    sg = _sigmoid(c)
    return c, sg, taps


def _chunk_gates(G):
    r = lax.broadcasted_iota(jnp.int32, (LC, LC), 0)
    c = lax.broadcasted_iota(jnp.int32, (LC, LC), 1)
    tril = (c <= r).astype(F32)
    triu = (c >= r).astype(F32)
    eye = (c == r).astype(F32)
    logf = _logsig(G)
    b_col = jnp.dot(tril, logf, preferred_element_type=F32, precision=HI)
    b_row = lax.dot_general(logf, triu, (((0,), (0,)), ((), ())), preferred_element_type=F32, precision=HI)
    g_row = lax.dot_general(G, eye, (((0,), (0,)), ((), ())), preferred_element_type=F32, precision=HI)
    return b_col, b_row, g_row, tril, triu


def _colpick(X, lane):
    li = lax.broadcasted_iota(jnp.int32, X.shape, 1)
    return jnp.sum(jnp.where(li == lane, X, 0.0), axis=1, keepdims=True)


def _rowpick(XT, row):
    ri = lax.broadcasted_iota(jnp.int32, XT.shape, 0)
    return jnp.sum(jnp.where(ri == row, XT, 0.0), axis=0, keepdims=True)


def _mlstm_head(qh, kh, vh, G, b_col, b_row, g_row, h, Ch, nh, m_prev):
    bt = _colpick(b_col, 4 + h)
    i_col = _colpick(G, h)
    bs = _rowpick(b_row, 4 + h)
    i_row = _rowpick(g_row, h)
    r = lax.broadcasted_iota(jnp.int32, (LC, LC), 0)
    c = lax.broadcasted_iota(jnp.int32, (LC, LC), 1)
    log_d = jnp.where(c <= r, bt - bs + i_row, NEG)
    log_inter = bt + m_prev
    m_t = jnp.maximum(log_inter, jnp.max(log_d, axis=1, keepdims=True))
    Dm = jnp.exp(log_d - m_t)
    g = jnp.exp(log_inter - m_t)
    qb, kb, vb = _bf(qh), _bf(kh), _bf(vh)
    Am = _dot_nt(qb, kb) * Dm
    qC = _dot(qb, _bf(Ch))
    num = g * qC + _dot(_bf(Am), vb)
    qn = jnp.sum(qh * nh, axis=1, keepdims=True)
    den = g * qn + jnp.sum(Am, axis=1, keepdims=True)
    floor = jnp.exp(-m_t)
    dd = jnp.maximum(jnp.abs(den), floor)
    hh = num / dd
    lane = lax.broadcasted_iota(jnp.int32, (1, LC), 1)
    blast = jnp.sum(jnp.where(lane == LC - 1, bs, 0.0), axis=1, keepdims=True)
    log_s = blast - bt + i_col
    m_new = jnp.maximum(blast + m_prev, jnp.max(log_s, axis=0, keepdims=True))
    decay = jnp.exp(blast + m_prev - m_new)
    ws = jnp.exp(log_s - m_new)
    kw = kh * ws
    C_new = decay * Ch + _dot_tn(_bf(kw), vb)
    n_new = decay * nh + jnp.sum(kw, axis=0, keepdims=True)
    return dict(Dm=Dm, g=g, Am=Am, qC=qC, qn=qn, den=den, floor=floor, dd=dd, h=hh, decay=decay, ws=ws, kw=kw,
                C_new=C_new, n_new=n_new, m_new=m_new, qb=qb, kb=kb, vb=vb)


def _head_out(hh, mo_h, gn_h):
    r = lax.rsqrt(jnp.mean(hh * hh, axis=-1, keepdims=True) + EPS)
    hn = hh * r
    sg = _sigmoid(mo_h)
    return sg * (hn * gn_h), hn, r, sg


def _mlstm_fwd(mqk, mv, mo, gates, conv_w, conv_b, gate_b, gn, shards, dtypes):
    nblk = S // TB
    ncb = TB // LC
    nw = len(shards)

    def body(*refs):
        x_ref, v_ref, o_ref, g_ref, w_ref, b_ref, gb_ref, gn_ref = refs[:8]
        ins = refs[8:8 + nw]
        out_ref, cs_ref, ns_ref, ms_ref = refs[8 + nw:12 + nw]
        outs = refs[12 + nw:12 + 2 * nw]
        tail, Cst, nst, mst, qs, ks = refs[12 + 2 * nw:18 + 2 * nw]
        bufs = refs[18 + 2 * nw:18 + 3 * nw]
        ag_start, ag_forward, ag_finish = _gather_phases(ins, outs, bufs, *refs[18 + 3 * nw:])
        i = pl.program_id(0)
        pl.when(i == 0)(ag_start)
        pl.when(i == nblk // 2)(ag_forward)

        @pl.when(i == 0)
        def _():
            tail[...] = jnp.zeros_like(tail)
            Cst[...] = jnp.zeros_like(Cst)
            nst[...] = jnp.zeros_like(nst)
            mst[...] = jnp.zeros_like(mst)

        x = x_ref[...]
        xp = jnp.concatenate([tail[...], x], axis=0)
        tail[...] = x[TB - 8:TB, :]
        c, sg, _ = _conv_silu(xp, w_ref, b_ref, TB)
        y = c * sg
        qs[...] = y[:, 0:MW]
        ks[...] = y[:, MW:2 * MW] * (1.0 / math.sqrt(128.0))

        for cc in range(ncb):
            rows = slice(cc * LC, (cc + 1) * LC)
            G = g_ref[rows, :] + gb_ref[...]
            b_col, b_row, g_row, _, _ = _chunk_gates(G)
            cs_ref[cc] = Cst[...]
            ns_ref[cc] = nst[...]
            ms_ref[cc] = mst[...]
            for h in range(4):
                ln = slice(h * 128, (h + 1) * 128)
                m_prev = jnp.max(mst[0:1, ln], axis=1, keepdims=True)
                f = _mlstm_head(qs[rows, ln], ks[rows, ln], v_ref[rows, ln], G, b_col, b_row, g_row, h,
                                Cst[:, ln], nst[0:1, ln], m_prev)
                out, _, _, _ = _head_out(f["h"], o_ref[rows, ln], gn_ref[:, ln])
                out_ref[rows, ln] = out
                Cst[:, ln] = f["C_new"]
                nst[0:1, ln] = f["n_new"]
                mst[0:1, ln] = jnp.broadcast_to(f["m_new"], (1, 128))
        pl.when(i == nblk - 1)(ag_finish)

    row = lambda wd: pl.BlockSpec((TB, wd), lambda i: (i, 0))
    res = pl.pallas_call(
        body, name="mlstm_fwd", grid=(nblk,),
        in_specs=[row(1024), row(MW), row(MW), row(128), _cspec((4, 1024)), _cspec((1, 1024)), _cspec((1, 128)),
                  _cspec((1, MW))] + [VM] * nw,
        out_specs=[row(MW), pl.BlockSpec((ncb, 128, MW), lambda i: (i, 0, 0)),
                   pl.BlockSpec((ncb, 8, MW), lambda i: (i, 0, 0)), pl.BlockSpec((ncb, 8, MW), lambda i: (i, 0, 0))]
        + [ANY] * nw,
        out_shape=[jax.ShapeDtypeStruct((S, MW), F32), jax.ShapeDtypeStruct((S // LC, 128, MW), F32),
                   jax.ShapeDtypeStruct((S // LC, 8, MW), F32), jax.ShapeDtypeStruct((S // LC, 8, MW), F32)]
        + _gather_shapes(shards, dtypes),
        scratch_shapes=[pltpu.VMEM((8, 1024), F32), pltpu.VMEM((128, MW), F32), pltpu.VMEM((8, MW), F32),
                        pltpu.VMEM((8, MW), F32), pltpu.VMEM((TB, MW), F32), pltpu.VMEM((TB, MW), F32)]
        + _gather_scratch(shards, dtypes),
        compiler_params=_params(1),
    )(mqk, mv, mo, gates, conv_w, conv_b, gate_b, gn, *shards)
    return res[0], res[1], res[2], res[3], res[4:]


def _mlstm_bwd(mqk, mv, mo, gates, conv_w, conv_b, gate_b, gn, cs, ns, ms, dout, parts):
    nblk = S // TB
    ncb = TB // LC
    kscale = 1.0 / math.sqrt(128.0)
    nw = len(parts)

    def body(*refs):
        x_ref, xprev_ref, v_ref, o_ref, g_ref, w_ref, b_ref, gb_ref, gn_ref, cs_ref, ns_ref, ms_ref, do_ref = refs[:13]
        ins = refs[13:13 + nw]
        dx_ref, dv_ref, dmo_ref, dg_ref, dw_ref, db_ref, dgn_ref, dgb_ref = refs[13 + nw:21 + nw]
        outs = refs[21 + nw:21 + 2 * nw]
        dCst, dnst, dyhead, qs, ks, dqk = refs[21 + 2 * nw:27 + 2 * nw]
        rs_start, rs_finish = _scatter_phases(ins, outs, *refs[27 + 2 * nw:])
        i = pl.program_id(0)
        blk = nblk - 1 - i
        pl.when(i == 0)(rs_start)

        @pl.when(i == 0)
        def _():
            dCst[...] = jnp.zeros_like(dCst)
            dnst[...] = jnp.zeros_like(dnst)
            dyhead[...] = jnp.zeros_like(dyhead)
            dw_ref[...] = jnp.zeros_like(dw_ref)
            db_ref[...] = jnp.zeros_like(db_ref)
            dgn_ref[...] = jnp.zeros_like(dgn_ref)
            dgb_ref[...] = jnp.zeros_like(dgb_ref)

        x = x_ref[...]
        xprev = jnp.where(blk == 0, 0.0, xprev_ref[...])
        xp = jnp.concatenate([xprev, x], axis=0)
        c, sg, taps = _conv_silu(xp, w_ref, b_ref, TB)
        y = c * sg
        qs[...] = y[:, 0:MW]
        ks[...] = y[:, MW:2 * MW] * kscale
        lane128 = lax.broadcasted_iota(jnp.int32, (LC, 128), 1)
        rowi = lax.broadcasted_iota(jnp.int32, (LC, 1), 0)
        ones = jnp.ones((LC, 128), F32)

        for cc in reversed(range(ncb)):
            rows = slice(cc * LC, (cc + 1) * LC)
            G = g_ref[rows, :] + gb_ref[...]
            b_col, b_row, g_row, _, triu = _chunk_gates(G)
            dB = jnp.zeros((LC, 128), F32)
            dI = jnp.zeros((LC, 128), F32)
            for h in range(4):
                ln = slice(h * 128, (h + 1) * 128)
                Ch = cs_ref[cc, :, ln]
                nh = ns_ref[cc, 0:1, ln]
                m_prev = jnp.max(ms_ref[cc, 0:1, ln], axis=1, keepdims=True)
                qh, kh, vh = qs[rows, ln], ks[rows, ln], v_ref[rows, ln]
                f = _mlstm_head(qh, kh, vh, G, b_col, b_row, g_row, h, Ch, nh, m_prev)
                hh, dd, den, g, Am, Dm = f["h"], f["dd"], f["den"], f["g"], f["Am"], f["Dm"]
                qb, kb, vb = f["qb"], f["kb"], f["vb"]
                gn_h = gn_ref[:, ln]
                _, hn, r, sgo = _head_out(hh, o_ref[rows, ln], gn_h)
                do = do_ref[rows, ln]
                hm = hn * gn_h
                dmo_ref[rows, ln] = do * hm * sgo * (1.0 - sgo)
                dhm = do * sgo
                dgn_ref[:, ln] = dgn_ref[:, ln] + jnp.sum(dhm * hn, axis=0, keepdims=True)
                dhn = dhm * gn_h
                dh = r * (dhn - hn * jnp.mean(dhn * hn, axis=-1, keepdims=True))
                dnum = dh / dd
                ddd = -jnp.sum(dh * hh, axis=1, keepdims=True) / dd
                dden = jnp.where(jnp.abs(den) >= f["floor"], ddd * jnp.sign(den), 0.0)
                dnb = _bf(dnum)
                dA = _dot_nt(dnb, vb) + dden
                dv = _dot_tn(_bf(Am), dnb)
                gd = _bf(g * dnum)
                gq = g * dden
                dq = _dot_nt(gd, _bf(Ch)) + gq * nh
                dCn = dCst[:, ln]
                dnn = dnst[0:1, ln]
                dC = f["decay"] * dCn + _dot_tn(qb, gd)
                dn = f["decay"] * dnn + jnp.sum(gq * qh, axis=0, keepdims=True)
                dg = jnp.sum(dnum * f["qC"], axis=1, keepdims=True) + dden * f["qn"]
                dS = _bf(dA * Dm)
                dq = dq + _dot(dS, kb)
                dk = _dot_tn(dS, qb)
                Gm = dA * Am
                gam = dg * g
                dCb = _bf(dCn)
                E = _dot_nt(vb, dCb) + dnn
                ws = f["ws"]
                dk = dk + ws * E
                om = jnp.sum(E * kh, axis=1, keepdims=True) * ws
                dv = dv + _dot(_bf(f["kw"]), dCb)
                ddecay = (jnp.sum(jnp.sum(dCn * Ch, axis=1, keepdims=True), axis=0, keepdims=True)
                          + jnp.sum(dnn * nh, axis=1, keepdims=True))
                delta = ddecay * f["decay"]
                rows_g = jnp.sum(Gm, axis=1, keepdims=True)
                cols_g = lax.dot_general(Gm, ones, (((0,), (0,)), ((), ())), preferred_element_type=F32, precision=HI)
                last = jnp.where(rowi == LC - 1, jnp.sum(om, axis=0, keepdims=True) + delta, 0.0)
                db = rows_g + gam - om + last - cols_g
                di = cols_g + om
                dB = dB + jnp.where(lane128 == 4 + h, db, 0.0)
                dI = dI + jnp.where(lane128 == h, di, 0.0)
                dCst[:, ln] = dC
                dnst[0:1, ln] = dn
                dqk[rows, ln] = dq
                dqk[rows, MW + h * 128:MW + (h + 1) * 128] = dk * kscale
                dv_ref[rows, ln] = dv
            dlogf = jnp.dot(triu, dB, preferred_element_type=F32, precision=HI)
            dG = dI + dlogf * _sigmoid(-G)
            dG = jnp.where(lane128 < 8, dG, 0.0)
            dg_ref[rows, :] = dG
            dgb_ref[...] = dgb_ref[...] + jnp.sum(dG, axis=0, keepdims=True)

        dy = dqk[...] * (sg * (1.0 + c * (1.0 - sg)))
        db_ref[...] = db_ref[...] + jnp.sum(dy, axis=0, keepdims=True)
        for j in range(4):
            dw_ref[j:j + 1, :] = dw_ref[j:j + 1, :] + jnp.sum(dy * taps[j], axis=0, keepdims=True)
        dyp = jnp.concatenate([dy, dyhead[...]], axis=0)
        dx = w_ref[3:4, :] * dy
        for j in range(3):
            dx = dx + w_ref[j:j + 1, :] * pltpu.roll(dyp, TB + 8 - (3 - j), 0)[0:TB]
        dx_ref[...] = dx
        dyhead[...] = dy[0:8, :]
        pl.when(i == nblk - 1)(rs_finish)

    rrow = lambda wd: pl.BlockSpec((TB, wd), lambda i: (nblk - 1 - i, 0))
    st = lambda r: pl.BlockSpec((ncb, r, MW), lambda i: (nblk - 1 - i, 0, 0))
    prev8 = pl.BlockSpec((8, 1024), lambda i: (jnp.maximum((nblk - 1 - i) * (TB // 8) - 1, 0), 0))
    res = pl.pallas_call(
        body, name="mlstm_bwd", grid=(nblk,),
        in_specs=[rrow(1024), prev8, rrow(MW), rrow(MW), rrow(128), _cspec((4, 1024)), _cspec((1, 1024)),
                  _cspec((1, 128)), _cspec((1, MW)), st(128), st(8), st(8), rrow(MW)] + [ANY] * nw,
        out_specs=[rrow(1024), rrow(MW), rrow(MW), rrow(128),
                   pl.BlockSpec((4, 1024), lambda i: (0, 0)), pl.BlockSpec((1, 1024), lambda i: (0, 0)),
                   pl.BlockSpec((1, MW), lambda i: (0, 0)), pl.BlockSpec((1, 128), lambda i: (0, 0))] + [ANY] * nw,
        out_shape=[jax.ShapeDtypeStruct((S, 1024), F32), jax.ShapeDtypeStruct((S, MW), F32),
                   jax.ShapeDtypeStruct((S, MW), F32), jax.ShapeDtypeStruct((S, 128), F32),
                   jax.ShapeDtypeStruct((4, 1024), F32), jax.ShapeDtypeStruct((1, 1024), F32),
                   jax.ShapeDtypeStruct((1, MW), F32), jax.ShapeDtypeStruct((1, 128), F32)]
        + [jax.ShapeDtypeStruct(a.shape, a.dtype) for a in parts],
        scratch_shapes=[pltpu.VMEM((128, MW), F32), pltpu.VMEM((8, MW), F32), pltpu.VMEM((8, 1024), F32),
                        pltpu.VMEM((TB, MW), F32), pltpu.VMEM((TB, MW), F32), pltpu.VMEM((TB, 1024), F32)]
        + _scatter_scratch(nw),
        compiler_params=_params(1),
    )(mqk, mqk, mv, mo, gates, conv_w, conv_b, gate_b, gn, cs, ns, ms, dout, *parts)
    return res[:8], res[8:]


def _out_proj(x, attn, ml, w, g):
    tm = 256

    def body(x_ref, a_ref, m_ref, w_ref, g_ref, h_ref, u_ref):
        h1 = x_ref[...] + _dot(_bf(a_ref[...]), w_ref[0:AW, :]) + _dot(_bf(m_ref[...]), w_ref[AW:D, :])
        h_ref[...] = h1
        n, _ = _rms(h1)
        u_ref[...] = _bf(n * g_ref[...])

    row = lambda wd: pl.BlockSpec((tm, wd), lambda i: (i, 0))
    return pl.pallas_call(
        body, name="out_proj", grid=(S // tm,),
        in_specs=[row(D), row(AW), row(MW), _cspec((D, D)), _cspec((1, D))],
        out_specs=[row(D), row(D)],
        out_shape=[jax.ShapeDtypeStruct((S, D), F32), jax.ShapeDtypeStruct((S, D), BF16)],
        compiler_params=_params(1),
    )(x, attn, ml, w, g)


def _mlp_fwd(h1, u2, w_up, w_down):
    tm = 256

    def body(h_ref, u_ref, wu_ref, wd_ref, a_ref, o_ref):
        u = u_ref[...]
        acc = h_ref[...]
        for c in range(NDEV):
            cols = slice(c * 512, (c + 1) * 512)
            a = _dot(u, wu_ref[c])
            a_ref[:, cols] = a
            r = jnp.maximum(a, 0.0)
            acc = acc + _dot(_bf(r * r), wd_ref[cols, :])
        o_ref[...] = acc

    row = lambda wd: pl.BlockSpec((tm, wd), lambda i: (i, 0))
    return pl.pallas_call(
        body, name="mlp_fwd", grid=(S // tm,),
        in_specs=[row(D), row(D), _cspec((NDEV, D, DFF // NDEV)), _cspec((DFF, D))],
        out_specs=[row(DFF), row(D)],
        out_shape=[jax.ShapeDtypeStruct((S, DFF), F32), jax.ShapeDtypeStruct((S, D), F32)],
        compiler_params=_params(1),
    )(h1, u2, w_up, w_down)


def _ple_loss(h2, p, target, w_pg, w_ple, g_ple, g_fin):
    tm = 256

    def body(h_ref, p_ref, t_ref, wg_ref, wp_ref, gp_ref, gf_ref,
             dh_ref, dwg_ref, dwp_ref, dgp_ref, dgf_ref, loss_ref, acc_g, acc_p):
        i = pl.program_id(0)

        @pl.when(i == 0)
        def _():
            acc_g[...] = jnp.zeros_like(acc_g)
            acc_p[...] = jnp.zeros_like(acc_p)
            dgp_ref[...] = jnp.zeros_like(dgp_ref)
            dgf_ref[...] = jnp.zeros_like(dgf_ref)
            loss_ref[...] = jnp.zeros_like(loss_ref)

        h2v = h_ref[...]
        n2, rs2 = _rms(h2v)
        u3 = _bf(n2 * gp_ref[...])
        gt = _sigmoid(_dot(u3, wg_ref[...]))
        pb = _bf(p_ref[...])
        e = jnp.concatenate([_dot(pb, wp_ref[j]) for j in range(NDEV)], axis=1)
        h3 = h2v + gt * e
        n3, rs3 = _rms(h3)
        err = n3 * gf_ref[...] - t_ref[...]
        loss_ref[...] = loss_ref[...] + 0.5 / D * jnp.sum(jnp.sum(err * err, axis=1, keepdims=True), axis=0, keepdims=True)
        dy = err * (1.0 / D)
        dgf_ref[...] = dgf_ref[...] + jnp.sum(dy * n3, axis=0, keepdims=True)
        dh3 = _rms_bwd(dy, n3, rs3, gf_ref[...])
        de = _bf(dh3 * gt)
        dz = _bf(dh3 * e * gt * (1.0 - gt))
        acc_p[...] = acc_p[...] + _dot_tn(pb, de)
        acc_g[...] = acc_g[...] + _dot_tn(u3, dz)
        du3 = _dot_nt(dz, wg_ref[...])
        dgp_ref[...] = dgp_ref[...] + jnp.sum(du3 * n2, axis=0, keepdims=True)
        dh_ref[...] = dh3 + _rms_bwd(du3, n2, rs2, gp_ref[...])

        @pl.when(i == S // tm - 1)
        def _():
            dwg_ref[...] = _bf(acc_g[...])
            for j in range(NDEV):
                dwp_ref[j] = _bf(acc_p[:, j * 128:(j + 1) * 128])

    row = lambda wd: pl.BlockSpec((tm, wd), lambda i: (i, 0))
    whole = lambda shp: pl.BlockSpec(shp, lambda i: (0,) * len(shp))
    return pl.pallas_call(
        body, name="ple_loss", grid=(S // tm,),
        in_specs=[row(D), row(PLE), row(D), _cspec((D, D)), _cspec((NDEV, PLE, 128)), _cspec((1, D)), _cspec((1, D))],
        out_specs=[row(D), whole((D, D)), whole((NDEV, PLE, 128)), whole((1, D)), whole((1, D)), whole((1, 1))],
        out_shape=[jax.ShapeDtypeStruct((S, D), F32), jax.ShapeDtypeStruct((D, D), BF16),
                   jax.ShapeDtypeStruct((NDEV, PLE, 128), BF16), jax.ShapeDtypeStruct((1, D), F32),
                   jax.ShapeDtypeStruct((1, D), F32), jax.ShapeDtypeStruct((1, 1), F32)],
        scratch_shapes=[pltpu.VMEM((D, D), F32), pltpu.VMEM((PLE, D), F32)],
        compiler_params=_params(1),
    )(h2, p, target, w_pg, w_ple, g_ple, g_fin)


def _mlp_bwd(dh2, a, h1, g, w_up, w_down):
    tm = 256

    def body(d_ref, a_ref, h_ref, g_ref, wu_ref, wd_ref, da_ref, dh1_ref, dg_ref):
        @pl.when(pl.program_id(0) == 0)
        def _():
            dg_ref[...] = jnp.zeros_like(dg_ref)

        dh2v = d_ref[...]
        db = _bf(dh2v)
        du = jnp.zeros((tm, D), F32)
        for c in range(NDEV):
            cols = slice(c * 512, (c + 1) * 512)
            dr = _dot_nt(db, wd_ref[cols, :])
            da = _bf(dr * (2.0 * jnp.maximum(a_ref[:, cols], 0.0)))
            da_ref[:, cols] = da
            du = du + _dot_nt(da, wu_ref[c])
        n, rs = _rms(h_ref[...])
        dg_ref[...] = dg_ref[...] + jnp.sum(du * n, axis=0, keepdims=True)
        dh1_ref[...] = dh2v + _rms_bwd(du, n, rs, g_ref[...])

    row = lambda wd: pl.BlockSpec((tm, wd), lambda i: (i, 0))
    return pl.pallas_call(
        body, name="mlp_bwd", grid=(S // tm,),
        in_specs=[row(D), row(DFF), row(D), _cspec((1, D)), _cspec((NDEV, D, DFF // NDEV)), _cspec((DFF, D))],
        out_specs=[row(DFF), row(D), pl.BlockSpec((1, D), lambda i: (0, 0))],
        out_shape=[jax.ShapeDtypeStruct((S, DFF), BF16), jax.ShapeDtypeStruct((S, D), F32),
                   jax.ShapeDtypeStruct((1, D), F32)],
        compiler_params=_params(1),
    )(dh2, a, h1, g, w_up, w_down)


def _out_proj_bwd(dh1, attn, ml, w):
    tm = 256

    def body(d_ref, a_ref, m_ref, w_ref, da_ref, dm_ref, dw_ref, acc):
        i = pl.program_id(0)

        @pl.when(i == 0)
        def _():
            acc[...] = jnp.zeros_like(acc)

        db = _bf(d_ref[...])
        dmix = _dot_nt(db, w_ref[...])
        da_ref[...] = dmix[:, 0:AW]
        dm_ref[...] = dmix[:, AW:D]
        acc[0:AW, :] = acc[0:AW, :] + _dot_tn(_bf(a_ref[...]), db)
        acc[AW:D, :] = acc[AW:D, :] + _dot_tn(_bf(m_ref[...]), db)

        @pl.when(i == S // tm - 1)
        def _():
            dw_ref[...] = _bf(acc[...])

    row = lambda wd: pl.BlockSpec((tm, wd), lambda i: (i, 0))
    return pl.pallas_call(
        body, name="out_proj_bwd", grid=(S // tm,),
        in_specs=[row(D), row(AW), row(MW), _cspec((D, D))],
        out_specs=[row(AW), row(MW), pl.BlockSpec((D, D), lambda i: (0, 0))],
        out_shape=[jax.ShapeDtypeStruct((S, AW), F32), jax.ShapeDtypeStruct((S, MW), F32),
                   jax.ShapeDtypeStruct((D, D), BF16)],
        scratch_shapes=[pltpu.VMEM((D, D), F32)],
        compiler_params=_params(1),
    )(dh1, attn, ml, w)


def _in_proj_bwd(dq, dk, dv, dmqk, dmv, dmo, dgt, dh1, x, g1, w, rc, ra, rb):
    tm = 256

    def body(dq_ref, dk_ref, dv_ref, dmqk_ref, dmv_ref, dmo_ref, dgt_ref, dh_ref, x_ref, g_ref, w_ref,
             rc_ref, ra_ref, rb_ref, dp_ref, dx_ref, dg_ref):
        @pl.when(pl.program_id(0) == 0)
        def _():
            dg_ref[...] = jnp.zeros_like(dg_ref)

        c, a, b = rc_ref[...], ra_ref[...], rb_ref[...]
        for half, ref in enumerate((dq_ref, dk_ref)):
            for t in range(4):
                lo = half * 512 + t * 128
                dp_ref[:, lo:lo + 128] = _bf(_rope_bwd(ref[:, t * 128:(t + 1) * 128], c, a, b))
        dp_ref[:, 1024:1536] = _bf(dv_ref[...])
        dp_ref[:, 1536:2560] = _bf(dmqk_ref[...])
        dp_ref[:, 2560:3072] = _bf(dmv_ref[...])
        dp_ref[:, 3072:3584] = _bf(dmo_ref[...])
        dp_ref[:, 3584:3712] = _bf(dgt_ref[...])
        dp_ref[:, 3712:PW] = jnp.zeros((tm, PW - 3712), BF16)
        du = jnp.zeros((tm, D), F32)
        for s in range(PW // 768):
            cols = slice(s * 768, (s + 1) * 768)
            du = du + _dot_nt(dp_ref[:, cols], w_ref[:, cols])
        n, rs = _rms(x_ref[...])
        dg_ref[...] = dg_ref[...] + jnp.sum(du * n, axis=0, keepdims=True)
        dx_ref[...] = dh_ref[...] + _rms_bwd(du, n, rs, g_ref[...])

    row = lambda wd: pl.BlockSpec((tm, wd), lambda i: (i, 0))
    return pl.pallas_call(
        body, name="in_proj_bwd", grid=(S // tm,),
        in_specs=[row(AW), row(AW), row(AW), row(1024), row(MW), row(MW), row(128), row(D), row(D), _cspec((1, D)),
                  _cspec((D, PW)), row(128), row(128), row(128)],
        out_specs=[row(PW), row(D), pl.BlockSpec((1, D), lambda i: (0, 0))],
        out_shape=[jax.ShapeDtypeStruct((S, PW), BF16), jax.ShapeDtypeStruct((S, D), F32),
                   jax.ShapeDtypeStruct((1, D), F32)],
        compiler_params=_params(1),
    )(dq, dk, dv, dmqk, dmv, dmo, dgt, dh1, x, g1, w, rc, ra, rb)


def _wgrad(name, A, B, a_fn, b_fn, tk, tn, out_shape, out_spec, ts=512, split=None):
    K, N = A.shape[1], B.shape[1]
    nrt = S // ts

    def body(a_ref, b_ref, o_ref, acc):
        r = pl.program_id(2)

        @pl.when(r == 0)
        def _():
            acc[...] = jnp.zeros_like(acc)

        acc[...] = acc[...] + _dot_tn(a_fn(a_ref[...]), b_fn(b_ref[...]))

        @pl.when(r == nrt - 1)
        def _():
            if split is None:
                o_ref[...] = _bf(acc[...])
            else:
                for j in range(NDEV):
                    o_ref[j] = _bf(acc[:, split * j:split * (j + 1)])

    return pl.pallas_call(
        body, name=name, grid=(N // tn, K // tk, nrt),
        in_specs=[pl.BlockSpec((ts, tk), lambda n, k, r: (r, k)), pl.BlockSpec((ts, tn), lambda n, k, r: (r, n))],
        out_specs=out_spec,
        out_shape=jax.ShapeDtypeStruct(out_shape, BF16),
        scratch_shapes=[pltpu.VMEM((tk, tn), F32)],
        compiler_params=_params(3),
    )(A, B)


def _relu2_bf(a):
    r = jnp.maximum(a, 0.0)
    return _bf(r * r)


def _ident(a):
    return a


def _step(x, p, target, g1, conv_b, gate_b, gn, g_mlp, g_ple, g_fin, sh):
    g_in, g_conv = _gather_weights([sh["w_in"], sh["conv_w"]], [BF16, F32])
    conv_w = g_conv.transpose(1, 0, 2).reshape(4, 1024)
    rc, ra, rb = _rope_tables()
    qkv, mqk, mv, mo, gates, u1, w_in_p = _in_proj(x, g1, g_in, rc, ra, rb)
    attn, lse, (w_up8, w_down8) = _attn_fwd(qkv, [sh["w_up"], sh["w_down"]], [BF16] * 2)
    ml, cs, ns, ms, (w_out8, w_pg8, w_ple8) = _mlstm_fwd(
        mqk, mv, mo, gates, conv_w, conv_b, gate_b, gn, [sh["w_out"], sh["w_ple_gate"], sh["w_ple"]], [BF16] * 3)
    w_out, w_down, w_pg = w_out8.reshape(D, D), w_down8.reshape(DFF, D), w_pg8.reshape(D, D)
    h1, u2 = _out_proj(x, attn, ml, w_out, g_mlp)
    a, h2 = _mlp_fwd(h1, u2, w_up8, w_down)
    dh2, dw_pg, dw_ple8, dg_ple, dg_fin, loss = _ple_loss(h2, p, target, w_pg, w_ple8, g_ple, g_fin)
    da, dh1, dg_mlp = _mlp_bwd(dh2, a, h1, g_mlp, w_up8, w_down)
    dw_up8 = _wgrad("wgrad_up", u2, da, _ident, _ident, D, 512, (NDEV, D, 512),
                    pl.BlockSpec((None, D, 512), lambda n, k, r: (n, 0, 0)))
    dw_down = _wgrad("wgrad_down", a, dh2, _relu2_bf, _bf, 1024, 1024, (DFF, D),
                     pl.BlockSpec((1024, 1024), lambda n, k, r: (k, n)))
    d_attn, d_ml, dw_out = _out_proj_bwd(dh1, attn, ml, w_out)
    (dmqk, dmv, dmo, dgt, dconv_w, dconv_b, dgn, dgate_b), (r_out, r_up, r_pg, r_ple) = _mlstm_bwd(
        mqk, mv, mo, gates, conv_w, conv_b, gate_b, gn, cs, ns, ms, d_ml,
        [dw_out.reshape(NDEV, D // NDEV, D), dw_up8, dw_pg.reshape(NDEV, D // NDEV, D), dw_ple8])
    dq, dk, dv, (r_down,) = _attn_bwd(qkv, attn, lse, d_attn, [dw_down.reshape(NDEV, DFF // NDEV, D)])
    dproj, dx, dg1 = _in_proj_bwd(dq, dk, dv, dmqk, dmv, dmo, dgt, dh1, x, g1, w_in_p, rc, ra, rb)
    dw_in8 = _wgrad("wgrad_in", u1, dproj, _ident, _ident, D, PW, (NDEV, D, IN_W // NDEV),
                    pl.BlockSpec((NDEV, D, IN_W // NDEV), lambda n, k, r: (0, 0, 0)), split=IN_W // NDEV)
    recv_in, recv_conv = _scatter_grads([dw_in8, dconv_w.reshape(4, NDEV, 128).transpose(1, 0, 2)])
    recv = dict(w_in=recv_in, conv_w=recv_conv, w_out=r_out, w_up=r_up, w_down=r_down, w_ple_gate=r_pg, w_ple=r_ple)
    small = dict(norm_mix_g=dg1, conv_b=dconv_b, gate_b=dgate_b, mlstm_norm_g=dgn, norm_mlp_g=dg_mlp,
                 norm_ple_g=dg_ple, final_norm_g=dg_fin)
    return loss, dx, recv, small


def _gather_weights(shards, dtypes):
    nw = len(shards)

    def body(*refs):
        start, forward, finish = _gather_phases(refs[:nw], refs[nw:2 * nw], refs[2 * nw:3 * nw], *refs[3 * nw:])
        start()
        forward()
        finish()

    return pl.pallas_call(
        body, name="gather_weights",
        in_specs=[VM] * nw, out_specs=[ANY] * nw,
        out_shape=_gather_shapes(shards, dtypes),
        scratch_shapes=_gather_scratch(shards, dtypes),
        compiler_params=_params(),
    )(*shards)


def _scatter_grads(parts):
    nw = len(parts)

    def body(*refs):
        start, finish = _scatter_phases(refs[:nw], refs[nw:2 * nw], *refs[2 * nw:])
        start()
        finish()

    return pl.pallas_call(
        body, name="scatter_grads",
        in_specs=[ANY] * nw, out_specs=[ANY] * nw,
        out_shape=[jax.ShapeDtypeStruct(a.shape, a.dtype) for a in parts],
        scratch_shapes=_scatter_scratch(nw),
        compiler_params=_params(),
    )(*parts)


SMALL_ROWS = 64


def _allreduce_small(vals):
    nv = len(vals)

    def body(*refs):
        ins, out_ref = refs[:nv], refs[nv]
        pack, rbuf, send_sems, recv_sems = refs[nv + 1:]
        x, y, c = _place()
        me = _dev_index(x, y, c)
        pack[...] = jnp.zeros_like(pack)
        for i in range(nv):
            pack[8 * i:8 * i + 1, 0:ins[i].shape[1]] = ins[i][...]
        rbuf[me] = pack[...]
        copies = []
        for k, (dx, dy, dc) in enumerate(FLIPS):
            peer = ((x + dx) % 2, (y + dy) % 2, (c + dc) % 2)
            cp = pltpu.make_async_remote_copy(
                src_ref=pack, dst_ref=rbuf.at[me], send_sem=send_sems.at[k], recv_sem=recv_sems.at[k],
                device_id=peer, device_id_type=MESH)
            cp.start()
            copies.append(cp)
        for cp in copies:
            cp.wait()
        tot = rbuf[0]
        for j in range(1, NDEV):
            tot = tot + rbuf[j]
        out_ref[...] = tot

    return pl.pallas_call(
        body, name="allreduce_small",
        in_specs=[VM] * nv, out_specs=VM,
        out_shape=jax.ShapeDtypeStruct((SMALL_ROWS, 1024), F32),
        scratch_shapes=[pltpu.VMEM((SMALL_ROWS, 1024), F32), pltpu.VMEM((NDEV, SMALL_ROWS, 1024), F32),
                        pltpu.SemaphoreType.DMA((7,)), pltpu.SemaphoreType.DMA((7,))],
        compiler_params=_params(),
    )(*vals)


def _adamw(name, gparts, w, m, v, tr):
    P, R, C = gparts.shape
    c1 = 1.0 - ADAM_B1 ** ADAM_STEP
    c2 = 1.0 - ADAM_B2 ** ADAM_STEP

    def body(g_ref, w_ref, m_ref, v_ref, go_ref, d_ref, mo_ref, vo_ref):
        g = g_ref[0].astype(F32)
        for j in range(1, P):
            g = g + g_ref[j].astype(F32)
        m2 = ADAM_B1 * m_ref[...] + (1.0 - ADAM_B1) * g
        v2 = ADAM_B2 * v_ref[...] + (1.0 - ADAM_B2) * (g * g)
        go_ref[...] = g
        mo_ref[...] = m2
        vo_ref[...] = v2
        d_ref[...] = -ADAM_LR * ((m2 / c1) / (jnp.sqrt(v2 / c2) + ADAM_EPS) + ADAM_WD * w_ref[...])

    row = pl.BlockSpec((tr, C), lambda i: (i, 0))
    return pl.pallas_call(
        body, name=name, grid=(R // tr,),
        in_specs=[pl.BlockSpec((P, tr, C), lambda i: (0, i, 0)), row, row, row],
        out_specs=[row] * 4,
        out_shape=[jax.ShapeDtypeStruct((R, C), F32)] * 4,
        compiler_params=_params(1),
    )(gparts, w, m, v)


SMALL = ("norm_mix_g", "conv_b", "gate_b", "mlstm_norm_g", "norm_mlp_g", "norm_ple_g", "final_norm_g")


def _pack_small(vals):
    return jnp.concatenate([jnp.pad(a, ((0, 7), (0, 1024 - a.shape[1]))) for a in vals], axis=0)


def kernel(x, p, norm_mix_g, w_in, conv_w, conv_b, gate_b, mlstm_norm_g, w_out, norm_mlp_g, w_up, w_down, norm_ple_g, w_ple_gate, w_ple, final_norm_g, loss_target, m_norm_mix_g, m_w_in, m_conv_w, m_conv_b, m_gate_b, m_mlstm_norm_g, m_w_out, m_norm_mlp_g, m_w_up, m_w_down, m_norm_ple_g, m_w_ple_gate, m_w_ple, m_final_norm_g, v_norm_mix_g, v_w_in, v_conv_w, v_conv_b, v_gate_b, v_mlstm_norm_g, v_w_out, v_norm_mlp_g, v_w_up, v_w_down, v_norm_ple_g, v_w_ple_gate, v_w_ple, v_final_norm_g):
    big_names = ("w_in", "conv_w", "w_out", "w_up", "w_down", "w_ple_gate", "w_ple")
    wts = dict(w_in=w_in, conv_w=conv_w, w_out=w_out, w_up=w_up, w_down=w_down, w_ple_gate=w_ple_gate, w_ple=w_ple)
    mom = dict(w_in=m_w_in, conv_w=m_conv_w, w_out=m_w_out, w_up=m_w_up, w_down=m_w_down, w_ple_gate=m_w_ple_gate,
               w_ple=m_w_ple)
    var = dict(w_in=v_w_in, conv_w=v_conv_w, w_out=v_w_out, w_up=v_w_up, w_down=v_w_down, w_ple_gate=v_w_ple_gate,
               w_ple=v_w_ple)
    sq = lambda a: a.reshape(a.shape[1:])
    fin = final_norm_g.reshape(1, D)
    loss, dx, recv, small = _step(
        x[0], p[0, 0], loss_target[0], norm_mix_g, conv_b, jnp.pad(gate_b, ((0, 0), (0, 120))), mlstm_norm_g,
        norm_mlp_g, norm_ple_g, fin, {n: sq(wts[n]) for n in big_names})
    total = _allreduce_small([small[n] for n in SMALL] + [loss])

    out = {}
    for n, tr in zip(big_names, (256, 4, 128, 256, 256, 128, 256)):
        res = _adamw("adamw_" + n, recv[n], sq(wts[n]), sq(mom[n]), sq(var[n]), tr)
        out[n] = [t.reshape(wts[n].shape) for t in res]
    sw = dict(norm_mix_g=norm_mix_g, conv_b=conv_b, gate_b=gate_b, mlstm_norm_g=mlstm_norm_g, norm_mlp_g=norm_mlp_g,
              norm_ple_g=norm_ple_g, final_norm_g=fin)
    sm = dict(norm_mix_g=m_norm_mix_g, conv_b=m_conv_b, gate_b=m_gate_b, mlstm_norm_g=m_mlstm_norm_g,
              norm_mlp_g=m_norm_mlp_g, norm_ple_g=m_norm_ple_g, final_norm_g=m_final_norm_g.reshape(1, D))
    sv = dict(norm_mix_g=v_norm_mix_g, conv_b=v_conv_b, gate_b=v_gate_b, mlstm_norm_g=v_mlstm_norm_g,
              norm_mlp_g=v_norm_mlp_g, norm_ple_g=v_norm_ple_g, final_norm_g=v_final_norm_g.reshape(1, D))
    nrow = 8 * len(SMALL)
    res = _adamw("adamw_small", total[0:nrow].reshape(1, nrow, 1024), _pack_small([sw[n] for n in SMALL]),
                 _pack_small([sm[n] for n in SMALL]), _pack_small([sv[n] for n in SMALL]), nrow)
    for i, n in enumerate(SMALL):
        shp = final_norm_g.shape if n == "final_norm_g" else sw[n].shape
        out[n] = [t[8 * i, 0:sw[n].shape[1]].reshape(shp) for t in res]
    order = ("norm_mix_g", "w_in", "conv_w", "conv_b", "gate_b", "mlstm_norm_g", "w_out", "norm_mlp_g", "w_up", "w_down",
             "norm_ple_g", "w_ple_gate", "w_ple", "final_norm_g")
    loss_all = total[nrow, 0]
    return (loss_all, dx[None], *[out[n][0] for n in order], *[out[n][1] for n in order],
            *[out[n][2] for n in order], *[out[n][3] for n in order])
```

```python
import functools
import math

import jax
import jax.numpy as jnp
from jax import lax
from jax.experimental import pallas as pl
from jax.experimental.pallas import tpu as pltpu

F32, BF16 = jnp.float32, jnp.bfloat16
S = 4096
D = 1024
AW = 512
MW = 512
DFF = 4096
PLE = 256
IN_W = 3592
PW = 3840
NDEV = 8
EPS = 1e-6
NEG = -1e30
LC = 128
TB = 256
ROPE_THETA = 500000.0
VMEM_LIMIT = 56 * 1024 * 1024
HI = lax.Precision.HIGHEST

ADAM_LR, ADAM_B1, ADAM_B2, ADAM_EPS, ADAM_WD, ADAM_STEP = 0.001, 0.9, 0.999, 1e-08, 0.01, 10


def _params(n_grid=0, **kw):
    sem = dict(dimension_semantics=("arbitrary",) * n_grid) if n_grid else {}
    return pltpu.CompilerParams(vmem_limit_bytes=VMEM_LIMIT, **sem, **kw)


def _cspec(shape):
    nd = len(shape)
    return pl.BlockSpec(shape, lambda *_: (0,) * nd, pipeline_mode=pl.Buffered(1))


def _dot(a, b):
    return jnp.dot(a, b, preferred_element_type=F32)


def _dot_nt(a, b):
    return lax.dot_general(a, b, (((1,), (1,)), ((), ())), preferred_element_type=F32)


def _dot_tn(a, b):
    return lax.dot_general(a, b, (((0,), (0,)), ((), ())), preferred_element_type=F32)


def _bf(x):
    return x.astype(BF16)


def _rms(x):
    rs = lax.rsqrt(jnp.mean(x * x, axis=-1, keepdims=True) + EPS)
    return x * rs, rs


def _rms_bwd(du, n, rs, g):
    dn = du * g
    return rs * (dn - n * jnp.mean(dn * n, axis=-1, keepdims=True))


def _sigmoid(x):
    return 1.0 / (1.0 + jnp.exp(-x))


def _rope_tables():
    j = lax.broadcasted_iota(jnp.int32, (S, 128), 1) % 64
    pos = lax.broadcasted_iota(jnp.int32, (S, 128), 0).astype(F32)
    inv_freq = jnp.power(ROPE_THETA, -(j % 8).astype(F32) / 8.0)
    ang = pos * inv_freq
    cos, sin = jnp.cos(ang), jnp.sin(ang)
    c = jnp.where(j < 16, cos, 1.0)
    a = jnp.where(j < 8, -sin, 0.0)
    b = jnp.where((j >= 8) & (j < 16), sin, 0.0)
    return c, a, b


def _rope(blk, c, a, b):
    return blk * c + pltpu.roll(blk, 120, 1) * a + pltpu.roll(blk, 8, 1) * b


def _rope_bwd(d, c, a, b):
    return d * c + pltpu.roll(d * a, 8, 1) + pltpu.roll(d * b, 120, 1)


MESH = pl.DeviceIdType.MESH
ANY = pl.BlockSpec(memory_space=pl.ANY)
VM = pl.BlockSpec(memory_space=pltpu.VMEM)
FLIPS = [(dx, dy, dc) for dx in (0, 1) for dy in (0, 1) for dc in (0, 1)][1:]


def _place():
    return lax.axis_index("x"), lax.axis_index("y"), lax.axis_index("c")


def _dev_index(px, py, pc):
    return 4 * px + 2 * py + pc


def _gather_phases(ins, outs, bufs, send_sems=None, recv_sems=None, local_sems=None):
    nw = len(ins)
    if nw == 0:
        return (lambda: None,) * 3
    x, y, c = _place()
    me, sib = (x, y, c), (x, y, 1 - c)
    chips = [(1 - x, y), (x, 1 - y), (1 - x, 1 - y)]

    def copy(w, k, block, to, from_buf=False):
        dst = outs[w].at[_dev_index(*block)]
        return pltpu.make_async_remote_copy(
            src_ref=bufs[w] if from_buf else dst, dst_ref=dst, send_sem=send_sems.at[w, k],
            recv_sem=recv_sems.at[w, k], device_id=to, device_id_type=MESH)

    def mine(w):
        return pltpu.make_async_copy(bufs[w], outs[w].at[_dev_index(*me)], local_sems.at[w])

    def first(w):
        return [copy(w, 0, me, sib, True)] + [copy(w, 1 + j, me, (*chip, c), True) for j, chip in enumerate(chips)]

    def passed(w):
        return [copy(w, 4 + j, (*chip, c), sib) for j, chip in enumerate(chips)]

    def start():
        for w in range(nw):
            bufs[w][...] = ins[w][...].astype(bufs[w].dtype)
        for w in range(nw):
            mine(w).start()
            for cp in first(w):
                cp.start()

    def forward():
        for j, chip in enumerate(chips):
            for w in range(nw):
                copy(w, 1 + j, (*chip, c), me).wait_recv()
                passed(w)[j].start()

    def finish():
        for w in range(nw):
            copy(w, 0, sib, me).wait_recv()
        for j, chip in enumerate(chips):
            for w in range(nw):
                copy(w, 4 + j, (*chip, 1 - c), me).wait_recv()
        for w in range(nw):
            for cp in first(w) + passed(w):
                cp.wait_send()
            mine(w).wait()

    return start, forward, finish


def _gather_scratch(shards, dtypes):
    nw = len(shards)
    if nw == 0:
        return []
    return ([pltpu.VMEM(s.shape, dt) for s, dt in zip(shards, dtypes)]
            + [pltpu.SemaphoreType.DMA((nw, 7)), pltpu.SemaphoreType.DMA((nw, 7)), pltpu.SemaphoreType.DMA((nw,))])


def _gather_shapes(shards, dtypes):
    return [jax.ShapeDtypeStruct((NDEV, *s.shape), dt) for s, dt in zip(shards, dtypes)]


def _scatter_phases(ins, outs, send_sems=None, recv_sems=None, local_sems=None):
    nw = len(ins)
    if nw == 0:
        return (lambda: None,) * 2
    x, y, c = _place()
    me = _dev_index(x, y, c)

    def copies():
        out = []
        for w in range(nw):
            out.append(pltpu.make_async_copy(ins[w].at[me], outs[w].at[me], local_sems.at[w]))
            for k, (dx, dy, dc) in enumerate(FLIPS):
                peer = ((x + dx) % 2, (y + dy) % 2, (c + dc) % 2)
                out.append(pltpu.make_async_remote_copy(
                    src_ref=ins[w].at[_dev_index(*peer)], dst_ref=outs[w].at[me], send_sem=send_sems.at[w, k],
                    recv_sem=recv_sems.at[w, k], device_id=peer, device_id_type=MESH))
        return out

    def start():
        for cp in copies():
            cp.start()

    def finish():
        for cp in copies():
            cp.wait()

    return start, finish


def _scatter_scratch(nw):
    if nw == 0:
        return []
    return [pltpu.SemaphoreType.DMA((nw, 7)), pltpu.SemaphoreType.DMA((nw, 7)), pltpu.SemaphoreType.DMA((nw,))]


TM = 512


def _join_w_in(wg):
    sw = IN_W // NDEV

    def body(wg_ref, w_ref):
        for j in range(NDEV):
            w_ref[:, sw * j:sw * (j + 1)] = wg_ref[j]
        w_ref[:, IN_W:PW] = jnp.zeros((D, PW - IN_W), BF16)

    return pl.pallas_call(body, name="join_w_in", out_shape=jax.ShapeDtypeStruct((D, PW), BF16),
                          compiler_params=_params())(wg)


def _in_proj(x, g1, w, rc, ra, rb):
    tm = TM

    def body(x_ref, g_ref, w_ref, rc_ref, ra_ref, rb_ref, qkv_ref, mqk_ref, mv_ref, mo_ref, gt_ref, u_ref):
        n, _ = _rms(x_ref[...])
        u = _bf(n * g_ref[...])
        u_ref[...] = u
        c, a, b = rc_ref[...], ra_ref[...], rb_ref[...]
        for half in range(2):
            blk = _dot(u, w_ref[:, half * 512:(half + 1) * 512])
            for t in range(4):
                lo = half * 512 + t * 128
                qkv_ref[:, lo:lo + 128] = _rope(blk[:, t * 128:(t + 1) * 128], c, a, b)
        qkv_ref[:, 1024:1536] = _dot(u, w_ref[:, 1024:1536])
        mqk_ref[:, 0:512] = _dot(u, w_ref[:, 1536:2048])
        mqk_ref[:, 512:1024] = _dot(u, w_ref[:, 2048:2560])
        mv_ref[...] = _dot(u, w_ref[:, 2560:3072])
        mo_ref[...] = _dot(u, w_ref[:, 3072:3584])
        gt_ref[...] = _dot(u, w_ref[:, 3584:3712])

    row = lambda wd: pl.BlockSpec((tm, wd), lambda i: (i, 0))
    return pl.pallas_call(
        body, name="in_proj", grid=(S // tm,),
        in_specs=[row(D), _cspec((1, D)), _cspec((D, PW)), row(128), row(128), row(128)],
        out_specs=[row(1536), row(1024), row(512), row(512), row(128), row(D)],
        out_shape=[jax.ShapeDtypeStruct((S, 1536), F32), jax.ShapeDtypeStruct((S, 1024), F32),
                   jax.ShapeDtypeStruct((S, 512), F32), jax.ShapeDtypeStruct((S, 512), F32),
                   jax.ShapeDtypeStruct((S, 128), F32), jax.ShapeDtypeStruct((S, D), BF16)],
        compiler_params=_params(1),
    )(x, g1, w, rc, ra, rb)


DILATIONS = (16, 4, 1)


def _attn_valid(n):
    kd = lax.broadcasted_iota(jnp.int32, (128, 256), 1) - lax.broadcasted_iota(jnp.int32, (128, 256), 0)
    off = jnp.where(n == 0, 0, 128)
    return (kd <= off) & (kd >= off - 128)


def _attn_rows(d, r, n):
    if d == 1:
        q0 = pl.multiple_of(n * 128, 128)
        k0 = pl.multiple_of(jnp.maximum(n - 1, 0) * 128, 128)
        return pl.ds(q0, 128), pl.ds(k0, 256), _attn_valid(n)
    q0 = r + n * 128 * d
    k0 = r + jnp.maximum(n - 1, 0) * 128 * d
    return pl.ds(q0, 128, stride=d), pl.ds(k0, 256, stride=d), _attn_valid(n)


ATTN_GROUP = 4
ATTN_ITERS = S // 128 // ATTN_GROUP


def _attn_group(d, i):
    nb = S // (128 * d)
    if nb == 2:
        qi = lax.broadcasted_iota(jnp.int32, (256, 256), 0) - lax.broadcasted_iota(jnp.int32, (256, 256), 1)
        whole = [pl.ds((ATTN_GROUP // 2) * i + u, 256, stride=d) for u in range(ATTN_GROUP // 2)]
        return [(rows, rows, (qi >= 0) & (qi <= 128)) for rows in whole]
    if d == 1:
        return [_attn_rows(1, 0, i + ATTN_ITERS * u) for u in range(ATTN_GROUP)]
    return [_attn_rows(d, (i // nb) * ATTN_GROUP + u, i % nb) for u in range(ATTN_GROUP)]


def _head0(shape):
    return lax.broadcasted_iota(jnp.int32, shape, 1) < 64


def _stack_heads(t):
    h0 = _head0(t.shape)
    tb = _bf(t)
    zero = jnp.zeros_like(tb)
    return jnp.concatenate([jnp.where(h0, tb, zero), jnp.where(h0, zero, tb)], axis=0)


def _attn_fwd(qkv, shards, dtypes):
    nw = len(shards)

    def body(*refs):
        q_ref, k_ref, v_ref = refs[:3]
        ins = refs[3:3 + nw]
        o_ref, lse_ref = refs[3 + nw:5 + nw]
        outs = refs[5 + nw:5 + 2 * nw]
        m0, m1, l0, l1, acc = refs[5 + 2 * nw:10 + 2 * nw]
        bufs = refs[10 + 2 * nw:10 + 3 * nw]
        ag_start, ag_forward, ag_finish = _gather_phases(ins, outs, bufs, *refs[10 + 3 * nw:])
        hp = pl.program_id(0)
        pl.when(hp == 0)(ag_start)
        pl.when(hp == 3)(ag_forward)
        stats = (m0, m1, l0, l1, acc)

        def update(blocks, first):
            loaded = [([q_ref[rq, :], k_ref[rk, :], v_ref[rk, :]], None if first else [ref[rq, :] for ref in stats])
                      for rq, rk, _ in blocks]
            results = []
            for ((q, k, v), prev), (_, _, valid) in zip(loaded, blocks):
                head0 = _head0(q.shape)
                kb, vb = _bf(k), _bf(v)
                q = q * 0.125
                m_new, l_new, acc_new = [], [], []
                for a, qa in enumerate((_bf(jnp.where(head0, q, 0.0)), _bf(jnp.where(head0, 0.0, q)))):
                    s = jnp.where(valid, _dot_nt(qa, kb), NEG)
                    mc = jnp.max(s, axis=-1, keepdims=True)
                    m_a = jnp.broadcast_to(mc, q.shape) if first else jnp.maximum(prev[a], mc)
                    p = jnp.exp(s - jnp.tile(m_a, (1, 2)))
                    l_add = jnp.sum(p, axis=-1, keepdims=True)
                    pv = _dot(_bf(p), vb)
                    if first:
                        l_a = jnp.broadcast_to(l_add, q.shape)
                    else:
                        alpha = jnp.exp(prev[a] - m_a)
                        l_a, pv = alpha * prev[2 + a] + l_add, alpha * prev[4] + pv
                    m_new.append(m_a), l_new.append(l_a), acc_new.append(pv)
                results.append((m_new[0], m_new[1], l_new[0], l_new[1], jnp.where(head0, acc_new[0], acc_new[1])))
            for (rq, _, _), res in zip(blocks, results):
                for ref, val in zip(stats, res):
                    ref[rq, :] = val

        for d in DILATIONS:
            def step(i, carry, d=d):
                update(_attn_group(d, i), d == DILATIONS[0])
                return carry

            lax.fori_loop(0, ATTN_ITERS, step, 0)

        def fin(t, carry):
            rows = pl.ds(pl.multiple_of(t * 256, 256), 256)
            h0 = lax.broadcasted_iota(jnp.int32, (256, 128), 1) < 64
            l = jnp.where(h0, l0[rows, :], l1[rows, :])
            o_ref[rows, :] = acc[rows, :] / l
            lse_ref[rows, :] = jnp.where(h0, m0[rows, :], m1[rows, :]) + jnp.log(l)
            return carry

        lax.fori_loop(0, S // 256, fin, 0)
        pl.when(hp == 3)(ag_finish)

    col = lambda off: pl.BlockSpec((S, 128), lambda h, off=off: (0, off + h))
    res = pl.pallas_call(
        body, name="attn_fwd", grid=(4,),
        in_specs=[col(0), col(4), col(8)] + [VM] * nw,
        out_specs=[col(0), col(0)] + [ANY] * nw,
        out_shape=[jax.ShapeDtypeStruct((S, AW), F32), jax.ShapeDtypeStruct((S, AW), F32)]
        + _gather_shapes(shards, dtypes),
        scratch_shapes=[pltpu.VMEM((S, 128), F32)] * 5 + _gather_scratch(shards, dtypes),
        compiler_params=_params(1),
    )(qkv, qkv, qkv, *shards)
    return res[0], res[1], res[2:]


def _attn_bwd(qkv, o, lse, do, parts):
    nw = len(parts)

    def body(*refs):
        q_ref, k_ref, v_ref, o_ref, lse_ref, do_ref = refs[:6]
        ins = refs[6:6 + nw]
        dq_ref, dk_ref, dv_ref = refs[6 + nw:9 + nw]
        outs = refs[9 + nw:9 + 2 * nw]
        L0, L1, D0, D1 = refs[9 + 2 * nw:13 + 2 * nw]
        rs_start, rs_finish = _scatter_phases(ins, outs, *refs[13 + 2 * nw:])
        hp = pl.program_id(0)
        pl.when(hp == 0)(rs_start)
        def pre(t, carry):
            rows = pl.ds(pl.multiple_of(t * 256, 256), 256)
            h0 = lax.broadcasted_iota(jnp.int32, (256, 128), 1) < 64
            ls = lse_ref[rows, :]
            dd = do_ref[rows, :] * o_ref[rows, :]
            shp = (256, 128)
            L0[rows, :] = jnp.broadcast_to(jnp.max(jnp.where(h0, ls, NEG), axis=-1, keepdims=True), shp)
            L1[rows, :] = jnp.broadcast_to(jnp.max(jnp.where(h0, NEG, ls), axis=-1, keepdims=True), shp)
            D0[rows, :] = jnp.broadcast_to(jnp.sum(jnp.where(h0, dd, 0.0), axis=-1, keepdims=True), shp)
            D1[rows, :] = jnp.broadcast_to(jnp.sum(jnp.where(h0, 0.0, dd), axis=-1, keepdims=True), shp)
            return carry

        lax.fori_loop(0, S // 256, pre, 0)

        def update(blocks, first):
            loaded = [([q_ref[rq, :], k_ref[rk, :], v_ref[rk, :], do_ref[rq, :]],
                       [L0[rq, :], L1[rq, :], D0[rq, :], D1[rq, :]],
                       [0.0] * 3 if first else [dq_ref[rq, :], dk_ref[rk, :], dv_ref[rk, :]]) for rq, rk, _ in blocks]
            results = []
            for ((q, k, v, dout), (l0v, l1v, d0v, d1v), (dq, dk, dv)), (_, _, valid) in zip(loaded, blocks):
                valid = jnp.tile(valid, (1, 2))
                kst, vst = _stack_heads(k), _stack_heads(v)
                hk = _head0((256, 128))
                dob = _bf(dout)
                cat = lambda a, b: jnp.concatenate([jnp.tile(a, (1, 2)), jnp.tile(b, (1, 2))], axis=1)
                s = jnp.where(valid, _dot_nt(_bf(q * 0.125), kst), NEG)
                p = jnp.exp(s - cat(l0v, l1v))
                ds = _bf(p * (_dot_nt(dob, vst) - cat(d0v, d1v)) * 0.125)
                dk2 = _dot_tn(ds, _bf(q))
                dv2 = _dot_tn(_bf(p), dob)
                results.append((dq + _dot(ds, kst), dk + jnp.where(hk, dk2[0:256], dk2[256:512]),
                                dv + jnp.where(hk, dv2[0:256], dv2[256:512])))
            for (rq, rk, _), (dq, dk, dv) in zip(blocks, results):
                dq_ref[rq, :] = dq
                dk_ref[rk, :] = dk
                dv_ref[rk, :] = dv

        assert S // (128 * DILATIONS[0]) == 2
        for d in DILATIONS:
            def step(i, carry, d=d):
                update(_attn_group(d, i), d == DILATIONS[0])
                return carry

            lax.fori_loop(0, ATTN_ITERS, step, 0)
        pl.when(hp == 3)(rs_finish)

    col = lambda off: pl.BlockSpec((S, 128), lambda h, off=off: (0, off + h))
    res = pl.pallas_call(
        body, name="attn_bwd", grid=(4,),
        in_specs=[col(0), col(4), col(8), col(0), col(0), col(0)] + [ANY] * nw,
        out_specs=[col(0), col(0), col(0)] + [ANY] * nw,
        out_shape=[jax.ShapeDtypeStruct((S, AW), F32)] * 3 + [jax.ShapeDtypeStruct(a.shape, a.dtype) for a in parts],
        scratch_shapes=[pltpu.VMEM((S, 128), F32)] * 4 + _scatter_scratch(nw),
        compiler_params=_params(1),
    )(qkv, qkv, qkv, o, lse, do, *parts)
    return res[0], res[1], res[2], res[3:]


def _logsig(x):
    return jnp.minimum(x, 0.0) - jnp.log1p(jnp.exp(-jnp.abs(x)))


def _conv_taps(xp, n):
    return [xp[8:] if j == 3 else pltpu.roll(xp, 3 - j, 0)[8:] for j in range(4)]


def _conv_silu(xp, w_ref, b_ref, n):
    taps = _conv_taps(xp, n)
    c = b_ref[...] + sum(w_ref[j:j + 1, :] * taps[j] for j in range(4))
    sg = _sigmoid(c)
    return c, sg, taps


def _chunk_gates(G):
    r = lax.broadcasted_iota(jnp.int32, (LC, LC), 0)
    c = lax.broadcasted_iota(jnp.int32, (LC, LC), 1)
    tril = (c <= r).astype(F32)
    triu = (c >= r).astype(F32)
    eye = (c == r).astype(F32)
    logf = _logsig(G)
    b_col = jnp.dot(tril, logf, preferred_element_type=F32, precision=HI)
    b_row = lax.dot_general(logf, triu, (((0,), (0,)), ((), ())), preferred_element_type=F32, precision=HI)
    g_row = lax.dot_general(G, eye, (((0,), (0,)), ((), ())), preferred_element_type=F32, precision=HI)
    return b_col, b_row, g_row, tril, triu


def _colpick(X, lane):
    li = lax.broadcasted_iota(jnp.int32, X.shape, 1)
    return jnp.sum(jnp.where(li == lane, X, 0.0), axis=1, keepdims=True)


def _rowpick(XT, row):
    ri = lax.broadcasted_iota(jnp.int32, XT.shape, 0)
    return jnp.sum(jnp.where(ri == row, XT, 0.0), axis=0, keepdims=True)


def _mlstm_head(qh, kh, vh, G, b_col, b_row, g_row, h, Ch, nh, m_prev):
    bt = _colpick(b_col, 4 + h)
    i_col = _colpick(G, h)
    bs = _rowpick(b_row, 4 + h)
    i_row = _rowpick(g_row, h)
    r = lax.broadcasted_iota(jnp.int32, (LC, LC), 0)
    c = lax.broadcasted_iota(jnp.int32, (LC, LC), 1)
    log_d = jnp.where(c <= r, bt - bs + i_row, NEG)
    log_inter = bt + m_prev
    m_t = jnp.maximum(log_inter, jnp.max(log_d, axis=1, keepdims=True))
    Dm = jnp.exp(log_d - m_t)
    g = jnp.exp(log_inter - m_t)
    qb, kb, vb = _bf(qh), _bf(kh), _bf(vh)
    Am = _dot_nt(qb, kb) * Dm
    qC = _dot(qb, _bf(Ch))
    num = g * qC + _dot(_bf(Am), vb)
    qn = jnp.sum(qh * nh, axis=1, keepdims=True)
    den = g * qn + jnp.sum(Am, axis=1, keepdims=True)
    floor = jnp.exp(-m_t)
    dd = jnp.maximum(jnp.abs(den), floor)
    hh = num / dd
    lane = lax.broadcasted_iota(jnp.int32, (1, LC), 1)
    blast = jnp.sum(jnp.where(lane == LC - 1, bs, 0.0), axis=1, keepdims=True)
    log_s = blast - bt + i_col
    m_new = jnp.maximum(blast + m_prev, jnp.max(log_s, axis=0, keepdims=True))
    decay = jnp.exp(blast + m_prev - m_new)
    ws = jnp.exp(log_s - m_new)
    kw = kh * ws
    C_new = decay * Ch + _dot_tn(_bf(kw), vb)
    n_new = decay * nh + jnp.sum(kw, axis=0, keepdims=True)
    return dict(Dm=Dm, g=g, Am=Am, qC=qC, qn=qn, den=den, floor=floor, dd=dd, h=hh, decay=decay, ws=ws, kw=kw,
                C_new=C_new, n_new=n_new, m_new=m_new, qb=qb, kb=kb, vb=vb)


def _head_out(hh, mo_h, gn_h):
    r = lax.rsqrt(jnp.mean(hh * hh, axis=-1, keepdims=True) + EPS)
    hn = hh * r
    sg = _sigmoid(mo_h)
    return sg * (hn * gn_h), hn, r, sg


def _mlstm_fwd(mqk, mv, mo, gates, conv_w, conv_b, gate_b, gn, shards, dtypes):
    nblk = S // TB
    ncb = TB // LC
    nw = len(shards)

    def body(*refs):
        x_ref, v_ref, o_ref, g_ref, w_ref, b_ref, gb_ref, gn_ref = refs[:8]
        ins = refs[8:8 + nw]
        out_ref, cs_ref, ns_ref, ms_ref = refs[8 + nw:12 + nw]
        outs = refs[12 + nw:12 + 2 * nw]
        tail, Cst, nst, mst, qs, ks = refs[12 + 2 * nw:18 + 2 * nw]
        bufs = refs[18 + 2 * nw:18 + 3 * nw]
        ag_start, ag_forward, ag_finish = _gather_phases(ins, outs, bufs, *refs[18 + 3 * nw:])
        i = pl.program_id(0)
        pl.when(i == 0)(ag_start)
        pl.when(i == nblk // 2)(ag_forward)

        @pl.when(i == 0)
        def _():
            tail[...] = jnp.zeros_like(tail)
            Cst[...] = jnp.zeros_like(Cst)
            nst[...] = jnp.zeros_like(nst)
            mst[...] = jnp.zeros_like(mst)

        x = x_ref[...]
        xp = jnp.concatenate([tail[...], x], axis=0)
        tail[...] = x[TB - 8:TB, :]
        c, sg, _ = _conv_silu(xp, w_ref, b_ref, TB)
        y = c * sg
        qs[...] = y[:, 0:MW]
        ks[...] = y[:, MW:2 * MW] * (1.0 / math.sqrt(128.0))

        for cc in range(ncb):
            rows = slice(cc * LC, (cc + 1) * LC)
            G = g_ref[rows, :] + gb_ref[...]
            b_col, b_row, g_row, _, _ = _chunk_gates(G)
            cs_ref[cc] = Cst[...]
            ns_ref[cc] = nst[...]
            ms_ref[cc] = mst[...]
            for h in range(4):
                ln = slice(h * 128, (h + 1) * 128)
                m_prev = jnp.max(mst[0:1, ln], axis=1, keepdims=True)
                f = _mlstm_head(qs[rows, ln], ks[rows, ln], v_ref[rows, ln], G, b_col, b_row, g_row, h,
                                Cst[:, ln], nst[0:1, ln], m_prev)
                out, _, _, _ = _head_out(f["h"], o_ref[rows, ln], gn_ref[:, ln])
                out_ref[rows, ln] = out
                Cst[:, ln] = f["C_new"]
                nst[0:1, ln] = f["n_new"]
                mst[0:1, ln] = jnp.broadcast_to(f["m_new"], (1, 128))
        pl.when(i == nblk - 1)(ag_finish)

    row = lambda wd: pl.BlockSpec((TB, wd), lambda i: (i, 0))
    res = pl.pallas_call(
        body, name="mlstm_fwd", grid=(nblk,),
        in_specs=[row(1024), row(MW), row(MW), row(128), _cspec((4, 1024)), _cspec((1, 1024)), _cspec((1, 128)),
                  _cspec((1, MW))] + [VM] * nw,
        out_specs=[row(MW), pl.BlockSpec((ncb, 128, MW), lambda i: (i, 0, 0)),
                   pl.BlockSpec((ncb, 8, MW), lambda i: (i, 0, 0)), pl.BlockSpec((ncb, 8, MW), lambda i: (i, 0, 0))]
        + [ANY] * nw,
        out_shape=[jax.ShapeDtypeStruct((S, MW), F32), jax.ShapeDtypeStruct((S // LC, 128, MW), F32),
                   jax.ShapeDtypeStruct((S // LC, 8, MW), F32), jax.ShapeDtypeStruct((S // LC, 8, MW), F32)]
        + _gather_shapes(shards, dtypes),
        scratch_shapes=[pltpu.VMEM((8, 1024), F32), pltpu.VMEM((128, MW), F32), pltpu.VMEM((8, MW), F32),
                        pltpu.VMEM((8, MW), F32), pltpu.VMEM((TB, MW), F32), pltpu.VMEM((TB, MW), F32)]
        + _gather_scratch(shards, dtypes),
        compiler_params=_params(1),
    )(mqk, mv, mo, gates, conv_w, conv_b, gate_b, gn, *shards)
    return res[0], res[1], res[2], res[3], res[4:]


def _mlstm_bwd(mqk, mv, mo, gates, conv_w, conv_b, gate_b, gn, cs, ns, ms, dout, parts):
    nblk = S // TB
    ncb = TB // LC
    kscale = 1.0 / math.sqrt(128.0)
    nw = len(parts)

    def body(*refs):
        x_ref, xprev_ref, v_ref, o_ref, g_ref, w_ref, b_ref, gb_ref, gn_ref, cs_ref, ns_ref, ms_ref, do_ref = refs[:13]
        ins = refs[13:13 + nw]
        dx_ref, dv_ref, dmo_ref, dg_ref, dw_ref, db_ref, dgn_ref, dgb_ref = refs[13 + nw:21 + nw]
        outs = refs[21 + nw:21 + 2 * nw]
        dCst, dnst, dyhead, qs, ks, dqk = refs[21 + 2 * nw:27 + 2 * nw]
        rs_start, rs_finish = _scatter_phases(ins, outs, *refs[27 + 2 * nw:])
        i = pl.program_id(0)
        blk = nblk - 1 - i
        pl.when(i == 0)(rs_start)

        @pl.when(i == 0)
        def _():
            dCst[...] = jnp.zeros_like(dCst)
            dnst[...] = jnp.zeros_like(dnst)
            dyhead[...] = jnp.zeros_like(dyhead)
            dw_ref[...] = jnp.zeros_like(dw_ref)
            db_ref[...] = jnp.zeros_like(db_ref)
            dgn_ref[...] = jnp.zeros_like(dgn_ref)
            dgb_ref[...] = jnp.zeros_like(dgb_ref)

        x = x_ref[...]
        xprev = jnp.where(blk == 0, 0.0, xprev_ref[...])
        xp = jnp.concatenate([xprev, x], axis=0)
        c, sg, taps = _conv_silu(xp, w_ref, b_ref, TB)
        y = c * sg
        qs[...] = y[:, 0:MW]
        ks[...] = y[:, MW:2 * MW] * kscale
        lane128 = lax.broadcasted_iota(jnp.int32, (LC, 128), 1)
        rowi = lax.broadcasted_iota(jnp.int32, (LC, 1), 0)
        ones = jnp.ones((LC, 128), F32)

        for cc in reversed(range(ncb)):
            rows = slice(cc * LC, (cc + 1) * LC)
            G = g_ref[rows, :] + gb_ref[...]
            b_col, b_row, g_row, _, triu = _chunk_gates(G)
            dB = jnp.zeros((LC, 128), F32)
            dI = jnp.zeros((LC, 128), F32)
            for h in range(4):
                ln = slice(h * 128, (h + 1) * 128)
                Ch = cs_ref[cc, :, ln]
                nh = ns_ref[cc, 0:1, ln]
                m_prev = jnp.max(ms_ref[cc, 0:1, ln], axis=1, keepdims=True)
                qh, kh, vh = qs[rows, ln], ks[rows, ln], v_ref[rows, ln]
                f = _mlstm_head(qh, kh, vh, G, b_col, b_row, g_row, h, Ch, nh, m_prev)
                hh, dd, den, g, Am, Dm = f["h"], f["dd"], f["den"], f["g"], f["Am"], f["Dm"]
                qb, kb, vb = f["qb"], f["kb"], f["vb"]
                gn_h = gn_ref[:, ln]
                _, hn, r, sgo = _head_out(hh, o_ref[rows, ln], gn_h)
                do = do_ref[rows, ln]
                hm = hn * gn_h
                dmo_ref[rows, ln] = do * hm * sgo * (1.0 - sgo)
                dhm = do * sgo
                dgn_ref[:, ln] = dgn_ref[:, ln] + jnp.sum(dhm * hn, axis=0, keepdims=True)
                dhn = dhm * gn_h
                dh = r * (dhn - hn * jnp.mean(dhn * hn, axis=-1, keepdims=True))
                dnum = dh / dd
                ddd = -jnp.sum(dh * hh, axis=1, keepdims=True) / dd
                dden = jnp.where(jnp.abs(den) >= f["floor"], ddd * jnp.sign(den), 0.0)
                dnb = _bf(dnum)
                dA = _dot_nt(dnb, vb) + dden
                dv = _dot_tn(_bf(Am), dnb)
                gd = _bf(g * dnum)
                gq = g * dden
                dq = _dot_nt(gd, _bf(Ch)) + gq * nh
                dCn = dCst[:, ln]
                dnn = dnst[0:1, ln]
                dC = f["decay"] * dCn + _dot_tn(qb, gd)
                dn = f["decay"] * dnn + jnp.sum(gq * qh, axis=0, keepdims=True)
                dg = jnp.sum(dnum * f["qC"], axis=1, keepdims=True) + dden * f["qn"]
                dS = _bf(dA * Dm)
                dq = dq + _dot(dS, kb)
                dk = _dot_tn(dS, qb)
                Gm = dA * Am
                gam = dg * g
                dCb = _bf(dCn)
                E = _dot_nt(vb, dCb) + dnn
                ws = f["ws"]
                dk = dk + ws * E
                om = jnp.sum(E * kh, axis=1, keepdims=True) * ws
                dv = dv + _dot(_bf(f["kw"]), dCb)
                ddecay = (jnp.sum(jnp.sum(dCn * Ch, axis=1, keepdims=True), axis=0, keepdims=True)
                          + jnp.sum(dnn * nh, axis=1, keepdims=True))
                delta = ddecay * f["decay"]
                rows_g = jnp.sum(Gm, axis=1, keepdims=True)
                cols_g = lax.dot_general(Gm, ones, (((0,), (0,)), ((), ())), preferred_element_type=F32, precision=HI)
                last = jnp.where(rowi == LC - 1, jnp.sum(om, axis=0, keepdims=True) + delta, 0.0)
                db = rows_g + gam - om + last - cols_g
                di = cols_g + om
                dB = dB + jnp.where(lane128 == 4 + h, db, 0.0)
                dI = dI + jnp.where(lane128 == h, di, 0.0)
                dCst[:, ln] = dC
                dnst[0:1, ln] = dn
                dqk[rows, ln] = dq
                dqk[rows, MW + h * 128:MW + (h + 1) * 128] = dk * kscale
                dv_ref[rows, ln] = dv
            dlogf = jnp.dot(triu, dB, preferred_element_type=F32, precision=HI)
            dG = dI + dlogf * _sigmoid(-G)
            dG = jnp.where(lane128 < 8, dG, 0.0)
            dg_ref[rows, :] = dG
            dgb_ref[...] = dgb_ref[...] + jnp.sum(dG, axis=0, keepdims=True)

        dy = dqk[...] * (sg * (1.0 + c * (1.0 - sg)))
        db_ref[...] = db_ref[...] + jnp.sum(dy, axis=0, keepdims=True)
        for j in range(4):
            dw_ref[j:j + 1, :] = dw_ref[j:j + 1, :] + jnp.sum(dy * taps[j], axis=0, keepdims=True)
        dyp = jnp.concatenate([dy, dyhead[...]], axis=0)
        dx = w_ref[3:4, :] * dy
        for j in range(3):
            dx = dx + w_ref[j:j + 1, :] * pltpu.roll(dyp, TB + 8 - (3 - j), 0)[0:TB]
        dx_ref[...] = dx
        dyhead[...] = dy[0:8, :]
        pl.when(i == nblk - 1)(rs_finish)

    rrow = lambda wd: pl.BlockSpec((TB, wd), lambda i: (nblk - 1 - i, 0))
    st = lambda r: pl.BlockSpec((ncb, r, MW), lambda i: (nblk - 1 - i, 0, 0))
    prev8 = pl.BlockSpec((8, 1024), lambda i: (jnp.maximum((nblk - 1 - i) * (TB // 8) - 1, 0), 0))
    res = pl.pallas_call(
        body, name="mlstm_bwd", grid=(nblk,),
        in_specs=[rrow(1024), prev8, rrow(MW), rrow(MW), rrow(128), _cspec((4, 1024)), _cspec((1, 1024)),
                  _cspec((1, 128)), _cspec((1, MW)), st(128), st(8), st(8), rrow(MW)] + [ANY] * nw,
        out_specs=[rrow(1024), rrow(MW), rrow(MW), rrow(128),
                   pl.BlockSpec((4, 1024), lambda i: (0, 0)), pl.BlockSpec((1, 1024), lambda i: (0, 0)),
                   pl.BlockSpec((1, MW), lambda i: (0, 0)), pl.BlockSpec((1, 128), lambda i: (0, 0))] + [ANY] * nw,
        out_shape=[jax.ShapeDtypeStruct((S, 1024), F32), jax.ShapeDtypeStruct((S, MW), F32),
                   jax.ShapeDtypeStruct((S, MW), F32), jax.ShapeDtypeStruct((S, 128), F32),
                   jax.ShapeDtypeStruct((4, 1024), F32), jax.ShapeDtypeStruct((1, 1024), F32),
                   jax.ShapeDtypeStruct((1, MW), F32), jax.ShapeDtypeStruct((1, 128), F32)]
        + [jax.ShapeDtypeStruct(a.shape, a.dtype) for a in parts],
        scratch_shapes=[pltpu.VMEM((128, MW), F32), pltpu.VMEM((8, MW), F32), pltpu.VMEM((8, 1024), F32),
                        pltpu.VMEM((TB, MW), F32), pltpu.VMEM((TB, MW), F32), pltpu.VMEM((TB, 1024), F32)]
        + _scatter_scratch(nw),
        compiler_params=_params(1),
    )(mqk, mqk, mv, mo, gates, conv_w, conv_b, gate_b, gn, cs, ns, ms, dout, *parts)
    return res[:8], res[8:]


def _out_proj(x, attn, ml, w, g):
    tm = TM

    def body(x_ref, a_ref, m_ref, w_ref, g_ref, h_ref, u_ref):
        h1 = x_ref[...] + _dot(_bf(a_ref[...]), w_ref[0:AW, :]) + _dot(_bf(m_ref[...]), w_ref[AW:D, :])
        h_ref[...] = h1
        n, _ = _rms(h1)
        u_ref[...] = _bf(n * g_ref[...])

    row = lambda wd: pl.BlockSpec((tm, wd), lambda i: (i, 0))
    return pl.pallas_call(
        body, name="out_proj", grid=(S // tm,),
        in_specs=[row(D), row(AW), row(MW), _cspec((D, D)), _cspec((1, D))],
        out_specs=[row(D), row(D)],
        out_shape=[jax.ShapeDtypeStruct((S, D), F32), jax.ShapeDtypeStruct((S, D), BF16)],
        compiler_params=_params(1),
    )(x, attn, ml, w, g)


def _mlp_fwd(h1, u2, w_up, w_down):
    tm = TM

    def body(h_ref, u_ref, wu_ref, wd_ref, a_ref, o_ref):
        u = u_ref[...]
        acc = h_ref[...]
        for c in range(NDEV):
            cols = slice(c * 512, (c + 1) * 512)
            a = _dot(u, wu_ref[c])
            a_ref[:, cols] = _bf(a)
            r = jnp.maximum(a, 0.0)
            acc = acc + _dot(_bf(r * r), wd_ref[cols, :])
        o_ref[...] = acc

    row = lambda wd: pl.BlockSpec((tm, wd), lambda i: (i, 0))
    return pl.pallas_call(
        body, name="mlp_fwd", grid=(S // tm,),
        in_specs=[row(D), row(D), _cspec((NDEV, D, DFF // NDEV)), _cspec((DFF, D))],
        out_specs=[row(DFF), row(D)],
        out_shape=[jax.ShapeDtypeStruct((S, DFF), BF16), jax.ShapeDtypeStruct((S, D), F32)],
        compiler_params=_params(1),
    )(h1, u2, w_up, w_down)


def _ple_loss(h2, p, target, w_pg, w_ple, g_ple, g_fin):
    tm = TM

    def body(h_ref, p_ref, t_ref, wg_ref, wp_ref, gp_ref, gf_ref,
             dh_ref, dwg_ref, dwp_ref, dgp_ref, dgf_ref, loss_ref, acc_g, acc_p):
        i = pl.program_id(0)

        @pl.when(i == 0)
        def _():
            acc_g[...] = jnp.zeros_like(acc_g)
            acc_p[...] = jnp.zeros_like(acc_p)
            dgp_ref[...] = jnp.zeros_like(dgp_ref)
            dgf_ref[...] = jnp.zeros_like(dgf_ref)
            loss_ref[...] = jnp.zeros_like(loss_ref)

        h2v = h_ref[...]
        n2, rs2 = _rms(h2v)
        u3 = _bf(n2 * gp_ref[...])
        gt = _sigmoid(_dot(u3, wg_ref[...]))
        pb = _bf(p_ref[...])
        e = jnp.concatenate([_dot(pb, wp_ref[j]) for j in range(NDEV)], axis=1)
        h3 = h2v + gt * e
        n3, rs3 = _rms(h3)
        err = n3 * gf_ref[...] - t_ref[...]
        loss_ref[...] = loss_ref[...] + 0.5 / D * jnp.sum(jnp.sum(err * err, axis=1, keepdims=True), axis=0, keepdims=True)
        dy = err * (1.0 / D)
        dgf_ref[...] = dgf_ref[...] + jnp.sum(dy * n3, axis=0, keepdims=True)
        dh3 = _rms_bwd(dy, n3, rs3, gf_ref[...])
        de = _bf(dh3 * gt)
        dz = _bf(dh3 * e * gt * (1.0 - gt))
        acc_p[...] = acc_p[...] + _dot_tn(pb, de)
        acc_g[...] = acc_g[...] + _dot_tn(u3, dz)
        du3 = _dot_nt(dz, wg_ref[...])
        dgp_ref[...] = dgp_ref[...] + jnp.sum(du3 * n2, axis=0, keepdims=True)
        dh_ref[...] = dh3 + _rms_bwd(du3, n2, rs2, gp_ref[...])

        @pl.when(i == S // tm - 1)
        def _():
            dwg_ref[...] = _bf(acc_g[...])
            for j in range(NDEV):
                dwp_ref[j] = _bf(acc_p[:, j * 128:(j + 1) * 128])

    row = lambda wd: pl.BlockSpec((tm, wd), lambda i: (i, 0))
    whole = lambda shp: pl.BlockSpec(shp, lambda i: (0,) * len(shp))
    return pl.pallas_call(
        body, name="ple_loss", grid=(S // tm,),
        in_specs=[row(D), row(PLE), row(D), _cspec((D, D)), _cspec((NDEV, PLE, 128)), _cspec((1, D)), _cspec((1, D))],
        out_specs=[row(D), whole((D, D)), whole((NDEV, PLE, 128)), whole((1, D)), whole((1, D)), whole((1, 1))],
        out_shape=[jax.ShapeDtypeStruct((S, D), F32), jax.ShapeDtypeStruct((D, D), BF16),
                   jax.ShapeDtypeStruct((NDEV, PLE, 128), BF16), jax.ShapeDtypeStruct((1, D), F32),
                   jax.ShapeDtypeStruct((1, D), F32), jax.ShapeDtypeStruct((1, 1), F32)],
        scratch_shapes=[pltpu.VMEM((D, D), F32), pltpu.VMEM((PLE, D), F32)],
        compiler_params=_params(1),
    )(h2, p, target, w_pg, w_ple, g_ple, g_fin)


def _mlp_bwd(dh2, a, h1, g, w_up, w_down):
    tm = TM

    def body(d_ref, a_ref, h_ref, g_ref, wu_ref, wd_ref, da_ref, dh1_ref, dg_ref):
        @pl.when(pl.program_id(0) == 0)
        def _():
            dg_ref[...] = jnp.zeros_like(dg_ref)

        dh2v = d_ref[...]
        db = _bf(dh2v)
        du = jnp.zeros((tm, D), F32)
        for c in range(NDEV):
            cols = slice(c * 512, (c + 1) * 512)
            dr = _dot_nt(db, wd_ref[cols, :])
            da = _bf(dr * (2.0 * jnp.maximum(a_ref[:, cols], 0.0)))
            da_ref[:, cols] = da
            du = du + _dot_nt(da, wu_ref[c])
        n, rs = _rms(h_ref[...])
        dg_ref[...] = dg_ref[...] + jnp.sum(du * n, axis=0, keepdims=True)
        dh1_ref[...] = dh2v + _rms_bwd(du, n, rs, g_ref[...])

    row = lambda wd: pl.BlockSpec((tm, wd), lambda i: (i, 0))
    return pl.pallas_call(
        body, name="mlp_bwd", grid=(S // tm,),
        in_specs=[row(D), row(DFF), row(D), _cspec((1, D)), _cspec((NDEV, D, DFF // NDEV)), _cspec((DFF, D))],
        out_specs=[row(DFF), row(D), pl.BlockSpec((1, D), lambda i: (0, 0))],
        out_shape=[jax.ShapeDtypeStruct((S, DFF), BF16), jax.ShapeDtypeStruct((S, D), F32),
                   jax.ShapeDtypeStruct((1, D), F32)],
        compiler_params=_params(1),
    )(dh2, a, h1, g, w_up, w_down)


def _out_proj_bwd(dh1, attn, ml, w):
    tm = TM

    def body(d_ref, a_ref, m_ref, w_ref, da_ref, dm_ref, dw_ref, acc):
        i = pl.program_id(0)

        @pl.when(i == 0)
        def _():
            acc[...] = jnp.zeros_like(acc)

        db = _bf(d_ref[...])
        dmix = _dot_nt(db, w_ref[...])
        da_ref[...] = dmix[:, 0:AW]
        dm_ref[...] = dmix[:, AW:D]
        acc[0:AW, :] = acc[0:AW, :] + _dot_tn(_bf(a_ref[...]), db)
        acc[AW:D, :] = acc[AW:D, :] + _dot_tn(_bf(m_ref[...]), db)

        @pl.when(i == S // tm - 1)
        def _():
            dw_ref[...] = _bf(acc[...])

    row = lambda wd: pl.BlockSpec((tm, wd), lambda i: (i, 0))
    return pl.pallas_call(
        body, name="out_proj_bwd", grid=(S // tm,),
        in_specs=[row(D), row(AW), row(MW), _cspec((D, D))],
        out_specs=[row(AW), row(MW), pl.BlockSpec((D, D), lambda i: (0, 0))],
        out_shape=[jax.ShapeDtypeStruct((S, AW), F32), jax.ShapeDtypeStruct((S, MW), F32),
                   jax.ShapeDtypeStruct((D, D), BF16)],
        scratch_shapes=[pltpu.VMEM((D, D), F32)],
        compiler_params=_params(1),
    )(dh1, attn, ml, w)


def _in_proj_bwd(dq, dk, dv, dmqk, dmv, dmo, dgt, dh1, x, g1, w, rc, ra, rb):
    tm = TM

    def body(dq_ref, dk_ref, dv_ref, dmqk_ref, dmv_ref, dmo_ref, dgt_ref, dh_ref, x_ref, g_ref, w_ref,
             rc_ref, ra_ref, rb_ref, dp_ref, dx_ref, dg_ref):
        @pl.when(pl.program_id(0) == 0)
        def _():
            dg_ref[...] = jnp.zeros_like(dg_ref)

        c, a, b = rc_ref[...], ra_ref[...], rb_ref[...]
        for half, ref in enumerate((dq_ref, dk_ref)):
            for t in range(4):
                lo = half * 512 + t * 128
                dp_ref[:, lo:lo + 128] = _bf(_rope_bwd(ref[:, t * 128:(t + 1) * 128], c, a, b))
        dp_ref[:, 1024:1536] = _bf(dv_ref[...])
        dp_ref[:, 1536:2560] = _bf(dmqk_ref[...])
        dp_ref[:, 2560:3072] = _bf(dmv_ref[...])
        dp_ref[:, 3072:3584] = _bf(dmo_ref[...])
        dp_ref[:, 3584:3712] = _bf(dgt_ref[...])
        dp_ref[:, 3712:PW] = jnp.zeros((tm, PW - 3712), BF16)
        du = jnp.zeros((tm, D), F32)
        for s in range(PW // 768):
            cols = slice(s * 768, (s + 1) * 768)
            du = du + _dot_nt(dp_ref[:, cols], w_ref[:, cols])
        n, rs = _rms(x_ref[...])
        dg_ref[...] = dg_ref[...] + jnp.sum(du * n, axis=0, keepdims=True)
        dx_ref[...] = dh_ref[...] + _rms_bwd(du, n, rs, g_ref[...])

    row = lambda wd: pl.BlockSpec((tm, wd), lambda i: (i, 0))
    return pl.pallas_call(
        body, name="in_proj_bwd", grid=(S // tm,),
        in_specs=[row(AW), row(AW), row(AW), row(1024), row(MW), row(MW), row(128), row(D), row(D), _cspec((1, D)),
                  _cspec((D, PW)), row(128), row(128), row(128)],
        out_specs=[row(PW), row(D), pl.BlockSpec((1, D), lambda i: (0, 0))],
        out_shape=[jax.ShapeDtypeStruct((S, PW), BF16), jax.ShapeDtypeStruct((S, D), F32),
                   jax.ShapeDtypeStruct((1, D), F32)],
        compiler_params=_params(1),
    )(dq, dk, dv, dmqk, dmv, dmo, dgt, dh1, x, g1, w, rc, ra, rb)


def _wgrad(name, A, B, a_fn, b_fn, tk, tn, out_shape, out_spec, ts=512, split=None):
    K, N = A.shape[1], B.shape[1]
    nrt = S // ts
    nc = next(c for c in (1024, 1280, tn) if tn % c == 0)

    def body(a_ref, b_ref, o_ref, acc):
        r = pl.program_id(2)

        @pl.when(r == 0)
        def _():
            acc[...] = jnp.zeros_like(acc)

        at = a_fn(a_ref[...]).T
        for c in range(tn // nc):
            cols = slice(c * nc, (c + 1) * nc)
            acc[:, cols] = acc[:, cols] + _dot(at, b_fn(b_ref[:, cols]))

        @pl.when(r == nrt - 1)
        def _():
            if split is None:
                o_ref[...] = _bf(acc[...])
            else:
                for j in range(NDEV):
                    o_ref[j] = _bf(acc[:, split * j:split * (j + 1)])

    return pl.pallas_call(
        body, name=name, grid=(N // tn, K // tk, nrt),
        in_specs=[pl.BlockSpec((ts, tk), lambda n, k, r: (r, k)), pl.BlockSpec((ts, tn), lambda n, k, r: (r, n))],
        out_specs=out_spec,
        out_shape=jax.ShapeDtypeStruct(out_shape, BF16),
        scratch_shapes=[pltpu.VMEM((tk, tn), F32)],
        compiler_params=_params(3),
    )(A, B)


def _relu2_bf(a):
    r = jnp.maximum(a.astype(F32), 0.0)
    return _bf(r * r)


def _ident(a):
    return a


def _step(x, p, target, g1, conv_b, gate_b, gn, g_mlp, g_ple, g_fin, sh):
    g_in, g_conv = _gather_weights([sh["w_in"], sh["conv_w"]], [BF16, F32])
    conv_w = g_conv.transpose(1, 0, 2).reshape(4, 1024)
    rc, ra, rb = _rope_tables()
    w_in_p = _join_w_in(g_in)
    qkv, mqk, mv, mo, gates, u1 = _in_proj(x, g1, w_in_p, rc, ra, rb)
    attn, lse, (w_up8, w_down8) = _attn_fwd(qkv, [sh["w_up"], sh["w_down"]], [BF16] * 2)
    ml, cs, ns, ms, (w_out8, w_pg8, w_ple8) = _mlstm_fwd(
        mqk, mv, mo, gates, conv_w, conv_b, gate_b, gn, [sh["w_out"], sh["w_ple_gate"], sh["w_ple"]], [BF16] * 3)
    w_out, w_down, w_pg = w_out8.reshape(D, D), w_down8.reshape(DFF, D), w_pg8.reshape(D, D)
    h1, u2 = _out_proj(x, attn, ml, w_out, g_mlp)
    a, h2 = _mlp_fwd(h1, u2, w_up8, w_down)
    dh2, dw_pg, dw_ple8, dg_ple, dg_fin, loss = _ple_loss(h2, p, target, w_pg, w_ple8, g_ple, g_fin)
    da, dh1, dg_mlp = _mlp_bwd(dh2, a, h1, g_mlp, w_up8, w_down)
    dw_up8 = _wgrad("wgrad_up", u2, da, _ident, _ident, D, DFF, (NDEV, D, DFF // NDEV),
                    pl.BlockSpec((NDEV, D, DFF // NDEV), lambda n, k, r: (0, 0, 0)), split=DFF // NDEV)
    dw_down = _wgrad("wgrad_down", a, dh2, _relu2_bf, _bf, 1024, 1024, (DFF, D),
                     pl.BlockSpec((1024, 1024), lambda n, k, r: (k, n)))
    d_attn, d_ml, dw_out = _out_proj_bwd(dh1, attn, ml, w_out)
    (dmqk, dmv, dmo, dgt, dconv_w, dconv_b, dgn, dgate_b), (r_out, r_up, r_pg, r_ple) = _mlstm_bwd(
        mqk, mv, mo, gates, conv_w, conv_b, gate_b, gn, cs, ns, ms, d_ml,
        [dw_out.reshape(NDEV, D // NDEV, D), dw_up8, dw_pg.reshape(NDEV, D // NDEV, D), dw_ple8])
    dq, dk, dv, (r_down,) = _attn_bwd(qkv, attn, lse, d_attn, [dw_down.reshape(NDEV, DFF // NDEV, D)])
    dproj, dx, dg1 = _in_proj_bwd(dq, dk, dv, dmqk, dmv, dmo, dgt, dh1, x, g1, w_in_p, rc, ra, rb)
    dw_in8 = _wgrad("wgrad_in", u1, dproj, _ident, _ident, D, PW, (NDEV, D, IN_W // NDEV),
                    pl.BlockSpec((NDEV, D, IN_W // NDEV), lambda n, k, r: (0, 0, 0)), split=IN_W // NDEV)
    recv_in, recv_conv = _scatter_grads([dw_in8, dconv_w.reshape(4, NDEV, 128).transpose(1, 0, 2)])
    recv = dict(w_in=recv_in, conv_w=recv_conv, w_out=r_out, w_up=r_up, w_down=r_down, w_ple_gate=r_pg, w_ple=r_ple)
    small = dict(norm_mix_g=dg1, conv_b=dconv_b, gate_b=dgate_b, mlstm_norm_g=dgn, norm_mlp_g=dg_mlp,
                 norm_ple_g=dg_ple, final_norm_g=dg_fin)
    return loss, dx, recv, small


def _gather_weights(shards, dtypes):
    nw = len(shards)

    def body(*refs):
        start, forward, finish = _gather_phases(refs[:nw], refs[nw:2 * nw], refs[2 * nw:3 * nw], *refs[3 * nw:])
        start()
        forward()
        finish()

    return pl.pallas_call(
        body, name="gather_weights",
        in_specs=[VM] * nw, out_specs=[ANY] * nw,
        out_shape=_gather_shapes(shards, dtypes),
        scratch_shapes=_gather_scratch(shards, dtypes),
        compiler_params=_params(),
    )(*shards)


def _scatter_grads(parts):
    nw = len(parts)

    def body(*refs):
        start, finish = _scatter_phases(refs[:nw], refs[nw:2 * nw], *refs[2 * nw:])
        start()
        finish()

    return pl.pallas_call(
        body, name="scatter_grads",
        in_specs=[ANY] * nw, out_specs=[ANY] * nw,
        out_shape=[jax.ShapeDtypeStruct(a.shape, a.dtype) for a in parts],
        scratch_shapes=_scatter_scratch(nw),
        compiler_params=_params(),
    )(*parts)


SMALL_ROWS = 64


def _allreduce_small(vals):
    nv = len(vals)

    def body(*refs):
        ins, out_ref = refs[:nv], refs[nv]
        pack, rbuf, send_sems, recv_sems = refs[nv + 1:]
        x, y, c = _place()
        me = _dev_index(x, y, c)
        pack[...] = jnp.zeros_like(pack)
        for i in range(nv):
            pack[8 * i:8 * i + 1, 0:ins[i].shape[1]] = ins[i][...]
        rbuf[me] = pack[...]
        copies = []
        for k, (dx, dy, dc) in enumerate(FLIPS):
            peer = ((x + dx) % 2, (y + dy) % 2, (c + dc) % 2)
            cp = pltpu.make_async_remote_copy(
                src_ref=pack, dst_ref=rbuf.at[me], send_sem=send_sems.at[k], recv_sem=recv_sems.at[k],
                device_id=peer, device_id_type=MESH)
            cp.start()
            copies.append(cp)
        for cp in copies:
            cp.wait()
        tot = rbuf[0]
        for j in range(1, NDEV):
            tot = tot + rbuf[j]
        out_ref[...] = tot

    return pl.pallas_call(
        body, name="allreduce_small",
        in_specs=[VM] * nv, out_specs=VM,
        out_shape=jax.ShapeDtypeStruct((SMALL_ROWS, 1024), F32),
        scratch_shapes=[pltpu.VMEM((SMALL_ROWS, 1024), F32), pltpu.VMEM((NDEV, SMALL_ROWS, 1024), F32),
                        pltpu.SemaphoreType.DMA((7,)), pltpu.SemaphoreType.DMA((7,))],
        compiler_params=_params(),
    )(*vals)


def _adamw(name, gparts, w, m, v, tr):
    P, R, C = gparts.shape
    c1 = 1.0 - ADAM_B1 ** ADAM_STEP
    c2 = 1.0 - ADAM_B2 ** ADAM_STEP

    def body(g_ref, w_ref, m_ref, v_ref, go_ref, d_ref, mo_ref, vo_ref):
        g = g_ref[0].astype(F32)
        for j in range(1, P):
            g = g + g_ref[j].astype(F32)
        m2 = ADAM_B1 * m_ref[...] + (1.0 - ADAM_B1) * g
        v2 = ADAM_B2 * v_ref[...] + (1.0 - ADAM_B2) * (g * g)
        go_ref[...] = g
        mo_ref[...] = m2
        vo_ref[...] = v2
        d_ref[...] = -ADAM_LR * ((m2 / c1) / (jnp.sqrt(v2 / c2) + ADAM_EPS) + ADAM_WD * w_ref[...])

    row = pl.BlockSpec((tr, C), lambda i: (i, 0))
    return pl.pallas_call(
        body, name=name, grid=(R // tr,),
        in_specs=[pl.BlockSpec((P, tr, C), lambda i: (0, i, 0)), row, row, row],
        out_specs=[row] * 4,
        out_shape=[jax.ShapeDtypeStruct((R, C), F32)] * 4,
        compiler_params=_params(1),
    )(gparts, w, m, v)


SMALL = ("norm_mix_g", "conv_b", "gate_b", "mlstm_norm_g", "norm_mlp_g", "norm_ple_g", "final_norm_g")


def _pack_small(vals):
    return jnp.concatenate([jnp.pad(a, ((0, 7), (0, 1024 - a.shape[1]))) for a in vals], axis=0)


def kernel(x, p, norm_mix_g, w_in, conv_w, conv_b, gate_b, mlstm_norm_g, w_out, norm_mlp_g, w_up, w_down, norm_ple_g, w_ple_gate, w_ple, final_norm_g, loss_target, m_norm_mix_g, m_w_in, m_conv_w, m_conv_b, m_gate_b, m_mlstm_norm_g, m_w_out, m_norm_mlp_g, m_w_up, m_w_down, m_norm_ple_g, m_w_ple_gate, m_w_ple, m_final_norm_g, v_norm_mix_g, v_w_in, v_conv_w, v_conv_b, v_gate_b, v_mlstm_norm_g, v_w_out, v_norm_mlp_g, v_w_up, v_w_down, v_norm_ple_g, v_w_ple_gate, v_w_ple, v_final_norm_g):
    big_names = ("w_in", "conv_w", "w_out", "w_up", "w_down", "w_ple_gate", "w_ple")
    wts = dict(w_in=w_in, conv_w=conv_w, w_out=w_out, w_up=w_up, w_down=w_down, w_ple_gate=w_ple_gate, w_ple=w_ple)
    mom = dict(w_in=m_w_in, conv_w=m_conv_w, w_out=m_w_out, w_up=m_w_up, w_down=m_w_down, w_ple_gate=m_w_ple_gate,
               w_ple=m_w_ple)
    var = dict(w_in=v_w_in, conv_w=v_conv_w, w_out=v_w_out, w_up=v_w_up, w_down=v_w_down, w_ple_gate=v_w_ple_gate,
               w_ple=v_w_ple)
    sq = lambda a: a.reshape(a.shape[1:])
    fin = final_norm_g.reshape(1, D)
    loss, dx, recv, small = _step(
        x[0], p[0, 0], loss_target[0], norm_mix_g, conv_b, jnp.pad(gate_b, ((0, 0), (0, 120))), mlstm_norm_g,
        norm_mlp_g, norm_ple_g, fin, {n: sq(wts[n]) for n in big_names})
    total = _allreduce_small([small[n] for n in SMALL] + [loss])

    out = {}
    for n, tr in zip(big_names, (256, 4, 128, 256, 256, 128, 256)):
        res = _adamw("adamw_" + n, recv[n], sq(wts[n]), sq(mom[n]), sq(var[n]), tr)
        out[n] = [t.reshape(wts[n].shape) for t in res]
    sw = dict(norm_mix_g=norm_mix_g, conv_b=conv_b, gate_b=gate_b, mlstm_norm_g=mlstm_norm_g, norm_mlp_g=norm_mlp_g,
              norm_ple_g=norm_ple_g, final_norm_g=fin)
    sm = dict(norm_mix_g=m_norm_mix_g, conv_b=m_conv_b, gate_b=m_gate_b, mlstm_norm_g=m_mlstm_norm_g,
              norm_mlp_g=m_norm_mlp_g, norm_ple_g=m_norm_ple_g, final_norm_g=m_final_norm_g.reshape(1, D))
    sv = dict(norm_mix_g=v_norm_mix_g, conv_b=v_conv_b, gate_b=v_gate_b, mlstm_norm_g=v_mlstm_norm_g,
              norm_mlp_g=v_norm_mlp_g, norm_ple_g=v_norm_ple_g, final_norm_g=v_final_norm_g.reshape(1, D))
    nrow = 8 * len(SMALL)
    res = _adamw("adamw_small", total[0:nrow].reshape(1, nrow, 1024), _pack_small([sw[n] for n in SMALL]),
                 _pack_small([sm[n] for n in SMALL]), _pack_small([sv[n] for n in SMALL]), nrow)
    for i, n in enumerate(SMALL):
        shp = final_norm_g.shape if n == "final_norm_g" else sw[n].shape
        out[n] = [t[8 * i, 0:sw[n].shape[1]].reshape(shp) for t in res]
    order = ("norm_mix_g", "w_in", "conv_w", "conv_b", "gate_b", "mlstm_norm_g", "w_out", "norm_mlp_g", "w_up", "w_down",
             "norm_ple_g", "w_ple_gate", "w_ple", "final_norm_g")
    loss_all = total[nrow, 0]
    return (loss_all, dx[None], *[out[n][0] for n in order], *[out[n][1] for n in order],
            *[out[n][2] for n in order], *[out[n][3] for n in order])
```

```python
import functools
import math

import jax
import jax.numpy as jnp
from jax import lax
from jax.experimental import pallas as pl
from jax.experimental.pallas import tpu as pltpu

F32, BF16 = jnp.float32, jnp.bfloat16
S = 4096
D = 1024
AW = 512
MW = 512
DFF = 4096
PLE = 256
IN_W = 3592
PW = 3840
NDEV = 8
EPS = 1e-6
NEG = -1e30
LC = 128
TB = 256
ROPE_THETA = 500000.0
VMEM_LIMIT = 56 * 1024 * 1024
HI = lax.Precision.HIGHEST

ADAM_LR, ADAM_B1, ADAM_B2, ADAM_EPS, ADAM_WD, ADAM_STEP = 0.001, 0.9, 0.999, 1e-08, 0.01, 10


def _params(n_grid=0, **kw):
    sem = dict(dimension_semantics=("arbitrary",) * n_grid) if n_grid else {}
    return pltpu.CompilerParams(vmem_limit_bytes=VMEM_LIMIT, **sem, **kw)


def _cspec(shape):
    nd = len(shape)
    return pl.BlockSpec(shape, lambda *_: (0,) * nd, pipeline_mode=pl.Buffered(1))


def _dot(a, b):
    return jnp.dot(a, b, preferred_element_type=F32)


def _dot_nt(a, b):
    return lax.dot_general(a, b, (((1,), (1,)), ((), ())), preferred_element_type=F32)


def _dot_tn(a, b):
    return lax.dot_general(a, b, (((0,), (0,)), ((), ())), preferred_element_type=F32)


def _bf(x):
    return x.astype(BF16)


def _rms(x):
    rs = lax.rsqrt(jnp.mean(x * x, axis=-1, keepdims=True) + EPS)
    return x * rs, rs


def _rms_bwd(du, n, rs, g):
    dn = du * g
    return rs * (dn - n * jnp.mean(dn * n, axis=-1, keepdims=True))


def _sigmoid(x):
    return 1.0 / (1.0 + jnp.exp(-x))


def _rope_tables():
    j = lax.broadcasted_iota(jnp.int32, (S, 128), 1) % 64
    pos = lax.broadcasted_iota(jnp.int32, (S, 128), 0).astype(F32)
    inv_freq = jnp.power(ROPE_THETA, -(j % 8).astype(F32) / 8.0)
    ang = pos * inv_freq
    cos, sin = jnp.cos(ang), jnp.sin(ang)
    c = jnp.where(j < 16, cos, 1.0)
    a = jnp.where(j < 8, -sin, 0.0)
    b = jnp.where((j >= 8) & (j < 16), sin, 0.0)
    return c, a, b


def _rope(blk, c, a, b):
    return blk * c + pltpu.roll(blk, 120, 1) * a + pltpu.roll(blk, 8, 1) * b


def _rope_bwd(d, c, a, b):
    return d * c + pltpu.roll(d * a, 8, 1) + pltpu.roll(d * b, 120, 1)


MESH = pl.DeviceIdType.MESH
ANY = pl.BlockSpec(memory_space=pl.ANY)
VM = pl.BlockSpec(memory_space=pltpu.VMEM)
FLIPS = [(dx, dy, dc) for dx in (0, 1) for dy in (0, 1) for dc in (0, 1)][1:]


def _place():
    return lax.axis_index("x"), lax.axis_index("y"), lax.axis_index("c")


def _dev_index(px, py, pc):
    return 4 * px + 2 * py + pc


def _gather_phases(ins, outs, bufs, send_sems=None, recv_sems=None, local_sems=None):
    nw = len(ins)
    if nw == 0:
        return (lambda: None,) * 3
    x, y, c = _place()
    me, sib = (x, y, c), (x, y, 1 - c)
    chips = [(1 - x, y), (x, 1 - y), (1 - x, 1 - y)]

    def copy(w, k, block, to, from_buf=False):
        dst = outs[w].at[_dev_index(*block)]
        return pltpu.make_async_remote_copy(
            src_ref=bufs[w] if from_buf else dst, dst_ref=dst, send_sem=send_sems.at[w, k],
            recv_sem=recv_sems.at[w, k], device_id=to, device_id_type=MESH)

    def mine(w):
        return pltpu.make_async_copy(bufs[w], outs[w].at[_dev_index(*me)], local_sems.at[w])

    def first(w):
        return [copy(w, 0, me, sib, True)] + [copy(w, 1 + j, me, (*chip, c), True) for j, chip in enumerate(chips)]

    def passed(w):
        return [copy(w, 4 + j, (*chip, c), sib) for j, chip in enumerate(chips)]

    def start():
        for w in range(nw):
            bufs[w][...] = ins[w][...].astype(bufs[w].dtype)
        for w in range(nw):
            mine(w).start()
            for cp in first(w):
                cp.start()

    def forward():
        for j, chip in enumerate(chips):
            for w in range(nw):
                copy(w, 1 + j, (*chip, c), me).wait_recv()
                passed(w)[j].start()

    def finish():
        for w in range(nw):
            copy(w, 0, sib, me).wait_recv()
        for j, chip in enumerate(chips):
            for w in range(nw):
                copy(w, 4 + j, (*chip, 1 - c), me).wait_recv()
        for w in range(nw):
            for cp in first(w) + passed(w):
                cp.wait_send()
            mine(w).wait()

    return start, forward, finish


def _gather_scratch(shards, dtypes):
    nw = len(shards)
    if nw == 0:
        return []
    return ([pltpu.VMEM(s.shape, dt) for s, dt in zip(shards, dtypes)]
            + [pltpu.SemaphoreType.DMA((nw, 7)), pltpu.SemaphoreType.DMA((nw, 7)), pltpu.SemaphoreType.DMA((nw,))])


def _gather_shapes(shards, dtypes):
    return [jax.ShapeDtypeStruct((NDEV, *s.shape), dt) for s, dt in zip(shards, dtypes)]


def _scatter_phases(ins, outs, send_sems=None, recv_sems=None, local_sems=None):
    nw = len(ins)
    if nw == 0:
        return (lambda: None,) * 2
    x, y, c = _place()
    me = _dev_index(x, y, c)

    def copies():
        out = []
        for w in range(nw):
            out.append(pltpu.make_async_copy(ins[w].at[me], outs[w].at[me], local_sems.at[w]))
            for k, (dx, dy, dc) in enumerate(FLIPS):
                peer = ((x + dx) % 2, (y + dy) % 2, (c + dc) % 2)
                out.append(pltpu.make_async_remote_copy(
                    src_ref=ins[w].at[_dev_index(*peer)], dst_ref=outs[w].at[me], send_sem=send_sems.at[w, k],
                    recv_sem=recv_sems.at[w, k], device_id=peer, device_id_type=MESH))
        return out

    def start():
        for cp in copies():
            cp.start()

    def finish():
        for cp in copies():
            cp.wait()

    return start, finish


def _scatter_scratch(nw):
    if nw == 0:
        return []
    return [pltpu.SemaphoreType.DMA((nw, 7)), pltpu.SemaphoreType.DMA((nw, 7)), pltpu.SemaphoreType.DMA((nw,))]


TM = 512


def _join_w_in(wg):
    sw = IN_W // NDEV

    def body(wg_ref, w_ref):
        for j in range(NDEV):
            w_ref[:, sw * j:sw * (j + 1)] = wg_ref[j]
        w_ref[:, IN_W:PW] = jnp.zeros((D, PW - IN_W), BF16)

    return pl.pallas_call(body, name="join_w_in", out_shape=jax.ShapeDtypeStruct((D, PW), BF16),
                          compiler_params=_params())(wg)


def _in_proj(x, g1, w, rc, ra, rb):
    tm = TM

    def body(x_ref, g_ref, w_ref, rc_ref, ra_ref, rb_ref, qkv_ref, mqk_ref, mv_ref, mo_ref, gt_ref, u_ref):
        n, _ = _rms(x_ref[...])
        u = _bf(n * g_ref[...])
        u_ref[...] = u
        c, a, b = rc_ref[...], ra_ref[...], rb_ref[...]
        for half in range(2):
            blk = _dot(u, w_ref[:, half * 512:(half + 1) * 512])
            for t in range(4):
                lo = half * 512 + t * 128
                qkv_ref[:, lo:lo + 128] = _rope(blk[:, t * 128:(t + 1) * 128], c, a, b)
        qkv_ref[:, 1024:1536] = _dot(u, w_ref[:, 1024:1536])
        mqk_ref[:, 0:512] = _dot(u, w_ref[:, 1536:2048])
        mqk_ref[:, 512:1024] = _dot(u, w_ref[:, 2048:2560])
        mv_ref[...] = _dot(u, w_ref[:, 2560:3072])
        mo_ref[...] = _dot(u, w_ref[:, 3072:3584])
        gt_ref[...] = _dot(u, w_ref[:, 3584:3712])

    row = lambda wd: pl.BlockSpec((tm, wd), lambda i: (i, 0))
    return pl.pallas_call(
        body, name="in_proj", grid=(S // tm,),
        in_specs=[row(D), _cspec((1, D)), _cspec((D, PW)), row(128), row(128), row(128)],
        out_specs=[row(1536), row(1024), row(512), row(512), row(128), row(D)],
        out_shape=[jax.ShapeDtypeStruct((S, 1536), F32), jax.ShapeDtypeStruct((S, 1024), F32),
                   jax.ShapeDtypeStruct((S, 512), F32), jax.ShapeDtypeStruct((S, 512), F32),
                   jax.ShapeDtypeStruct((S, 128), F32), jax.ShapeDtypeStruct((S, D), BF16)],
        compiler_params=_params(1),
    )(x, g1, w, rc, ra, rb)


DILATIONS = (16, 4, 1)


def _attn_valid(n):
    kd = lax.broadcasted_iota(jnp.int32, (128, 256), 1) - lax.broadcasted_iota(jnp.int32, (128, 256), 0)
    off = jnp.where(n == 0, 0, 128)
    return (kd <= off) & (kd >= off - 128)


def _attn_rows(d, r, n):
    if d == 1:
        q0 = pl.multiple_of(n * 128, 128)
        k0 = pl.multiple_of(jnp.maximum(n - 1, 0) * 128, 128)
        return pl.ds(q0, 128), pl.ds(k0, 256), _attn_valid(n)
    q0 = r + n * 128 * d
    k0 = r + jnp.maximum(n - 1, 0) * 128 * d
    return pl.ds(q0, 128, stride=d), pl.ds(k0, 256, stride=d), _attn_valid(n)


ATTN_GROUP = 4
ATTN_ITERS = S // 128 // ATTN_GROUP


def _attn_group(d, i):
    nb = S // (128 * d)
    if nb == 2:
        qi = lax.broadcasted_iota(jnp.int32, (256, 256), 0) - lax.broadcasted_iota(jnp.int32, (256, 256), 1)
        whole = [pl.ds((ATTN_GROUP // 2) * i + u, 256, stride=d) for u in range(ATTN_GROUP // 2)]
        return [(rows, rows, (qi >= 0) & (qi <= 128)) for rows in whole]
    if d == 1:
        return [_attn_rows(1, 0, i + ATTN_ITERS * u) for u in range(ATTN_GROUP)]
    return [_attn_rows(d, (i // nb) * ATTN_GROUP + u, i % nb) for u in range(ATTN_GROUP)]


def _head0(shape):
    return lax.broadcasted_iota(jnp.int32, shape, 1) < 64


def _stack_heads(t):
    h0 = _head0(t.shape)
    tb = _bf(t)
    zero = jnp.zeros_like(tb)
    return jnp.concatenate([jnp.where(h0, tb, zero), jnp.where(h0, zero, tb)], axis=0)


def _attn_fwd(qkv, shards, dtypes):
    nw = len(shards)

    def body(*refs):
        q_ref, k_ref, v_ref = refs[:3]
        ins = refs[3:3 + nw]
        o_ref, lse_ref = refs[3 + nw:5 + nw]
        outs = refs[5 + nw:5 + 2 * nw]
        m0, m1, l0, l1, acc = refs[5 + 2 * nw:10 + 2 * nw]
        bufs = refs[10 + 2 * nw:10 + 3 * nw]
        ag_start, ag_forward, ag_finish = _gather_phases(ins, outs, bufs, *refs[10 + 3 * nw:])
        hp = pl.program_id(0)
        pl.when(hp == 0)(ag_start)
        pl.when(hp == 3)(ag_forward)
        stats = (m0, m1, l0, l1, acc)

        def update(blocks, first):
            loaded = [([q_ref[rq, :], k_ref[rk, :], v_ref[rk, :]], None if first else [ref[rq, :] for ref in stats])
                      for rq, rk, _ in blocks]
            results = []
            for ((q, k, v), prev), (_, _, valid) in zip(loaded, blocks):
                head0 = _head0(q.shape)
                kb, vb = _bf(k), _bf(v)
                q = q * 0.125
                m_new, l_new, acc_new = [], [], []
                for a, qa in enumerate((_bf(jnp.where(head0, q, 0.0)), _bf(jnp.where(head0, 0.0, q)))):
                    s = jnp.where(valid, _dot_nt(qa, kb), NEG)
                    mc = jnp.max(s, axis=-1, keepdims=True)
                    m_a = jnp.broadcast_to(mc, q.shape) if first else jnp.maximum(prev[a], mc)
                    p = jnp.exp(s - jnp.tile(m_a, (1, 2)))
                    l_add = jnp.sum(p, axis=-1, keepdims=True)
                    pv = _dot(_bf(p), vb)
                    if first:
                        l_a = jnp.broadcast_to(l_add, q.shape)
                    else:
                        alpha = jnp.exp(prev[a] - m_a)
                        l_a, pv = alpha * prev[2 + a] + l_add, alpha * prev[4] + pv
                    m_new.append(m_a), l_new.append(l_a), acc_new.append(pv)
                results.append((m_new[0], m_new[1], l_new[0], l_new[1], jnp.where(head0, acc_new[0], acc_new[1])))
            for (rq, _, _), res in zip(blocks, results):
                for ref, val in zip(stats, res):
                    ref[rq, :] = val

        for d in DILATIONS:
            def step(i, carry, d=d):
                update(_attn_group(d, i), d == DILATIONS[0])
                return carry

            lax.fori_loop(0, ATTN_ITERS, step, 0)

        def fin(t, carry):
            rows = pl.ds(pl.multiple_of(t * 256, 256), 256)
            h0 = lax.broadcasted_iota(jnp.int32, (256, 128), 1) < 64
            l = jnp.where(h0, l0[rows, :], l1[rows, :])
            o_ref[rows, :] = acc[rows, :] / l
            lse_ref[rows, :] = jnp.where(h0, m0[rows, :], m1[rows, :]) + jnp.log(l)
            return carry

        lax.fori_loop(0, S // 256, fin, 0)
        pl.when(hp == 3)(ag_finish)

    col = lambda off: pl.BlockSpec((S, 128), lambda h, off=off: (0, off + h))
    res = pl.pallas_call(
        body, name="attn_fwd", grid=(4,),
        in_specs=[col(0), col(4), col(8)] + [VM] * nw,
        out_specs=[col(0), col(0)] + [ANY] * nw,
        out_shape=[jax.ShapeDtypeStruct((S, AW), F32), jax.ShapeDtypeStruct((S, AW), F32)]
        + _gather_shapes(shards, dtypes),
        scratch_shapes=[pltpu.VMEM((S, 128), F32)] * 5 + _gather_scratch(shards, dtypes),
        compiler_params=_params(1),
    )(qkv, qkv, qkv, *shards)
    return res[0], res[1], res[2:]


def _attn_bwd(qkv, o, lse, do, parts):
    nw = len(parts)

    def body(*refs):
        q_ref, k_ref, v_ref, o_ref, lse_ref, do_ref = refs[:6]
        ins = refs[6:6 + nw]
        dq_ref, dk_ref, dv_ref = refs[6 + nw:9 + nw]
        outs = refs[9 + nw:9 + 2 * nw]
        L0, L1, D0, D1 = refs[9 + 2 * nw:13 + 2 * nw]
        rs_start, rs_finish = _scatter_phases(ins, outs, *refs[13 + 2 * nw:])
        hp = pl.program_id(0)
        pl.when(hp == 0)(rs_start)
        def pre(t, carry):
            rows = pl.ds(pl.multiple_of(t * 256, 256), 256)
            h0 = lax.broadcasted_iota(jnp.int32, (256, 128), 1) < 64
            ls = lse_ref[rows, :]
            dd = do_ref[rows, :] * o_ref[rows, :]
            shp = (256, 128)
            L0[rows, :] = jnp.broadcast_to(jnp.max(jnp.where(h0, ls, NEG), axis=-1, keepdims=True), shp)
            L1[rows, :] = jnp.broadcast_to(jnp.max(jnp.where(h0, NEG, ls), axis=-1, keepdims=True), shp)
            D0[rows, :] = jnp.broadcast_to(jnp.sum(jnp.where(h0, dd, 0.0), axis=-1, keepdims=True), shp)
            D1[rows, :] = jnp.broadcast_to(jnp.sum(jnp.where(h0, 0.0, dd), axis=-1, keepdims=True), shp)
            return carry

        lax.fori_loop(0, S // 256, pre, 0)

        def update(blocks, first):
            loaded = [([q_ref[rq, :], k_ref[rk, :], v_ref[rk, :], do_ref[rq, :]],
                       [L0[rq, :], L1[rq, :], D0[rq, :], D1[rq, :]],
                       [0.0] * 3 if first else [dq_ref[rq, :], dk_ref[rk, :], dv_ref[rk, :]]) for rq, rk, _ in blocks]
            results = []
            for ((q, k, v, dout), (l0v, l1v, d0v, d1v), (dq, dk, dv)), (_, _, valid) in zip(loaded, blocks):
                valid = jnp.tile(valid, (1, 2))
                kst, vst = _stack_heads(k), _stack_heads(v)
                hk = _head0((256, 128))
                dob = _bf(dout)
                cat = lambda a, b: jnp.concatenate([jnp.tile(a, (1, 2)), jnp.tile(b, (1, 2))], axis=1)
                s = jnp.where(valid, _dot_nt(_bf(q * 0.125), kst), NEG)
                p = jnp.exp(s - cat(l0v, l1v))
                ds = _bf(p * (_dot_nt(dob, vst) - cat(d0v, d1v)) * 0.125)
                dk2 = _dot_tn(ds, _bf(q))
                dv2 = _dot_tn(_bf(p), dob)
                results.append((dq + _dot(ds, kst), dk + jnp.where(hk, dk2[0:256], dk2[256:512]),
                                dv + jnp.where(hk, dv2[0:256], dv2[256:512])))
            for (rq, rk, _), (dq, dk, dv) in zip(blocks, results):
                dq_ref[rq, :] = dq
                dk_ref[rk, :] = dk
                dv_ref[rk, :] = dv

        assert S // (128 * DILATIONS[0]) == 2
        for d in DILATIONS:
            def step(i, carry, d=d):
                update(_attn_group(d, i), d == DILATIONS[0])
                return carry

            lax.fori_loop(0, ATTN_ITERS, step, 0)
        pl.when(hp == 3)(rs_finish)

    col = lambda off: pl.BlockSpec((S, 128), lambda h, off=off: (0, off + h))
    res = pl.pallas_call(
        body, name="attn_bwd", grid=(4,),
        in_specs=[col(0), col(4), col(8), col(0), col(0), col(0)] + [ANY] * nw,
        out_specs=[col(0), col(0), col(0)] + [ANY] * nw,
        out_shape=[jax.ShapeDtypeStruct((S, AW), F32)] * 3 + [jax.ShapeDtypeStruct(a.shape, a.dtype) for a in parts],
        scratch_shapes=[pltpu.VMEM((S, 128), F32)] * 4 + _scatter_scratch(nw),
        compiler_params=_params(1),
    )(qkv, qkv, qkv, o, lse, do, *parts)
    return res[0], res[1], res[2], res[3:]


def _logsig(x):
    return jnp.minimum(x, 0.0) - jnp.log1p(jnp.exp(-jnp.abs(x)))


def _conv_taps(xp, n):
    return [xp[8:] if j == 3 else pltpu.roll(xp, 3 - j, 0)[8:] for j in range(4)]


def _conv_silu(xp, w_ref, b_ref, n):
    taps = _conv_taps(xp, n)
    c = b_ref[...] + sum(w_ref[j:j + 1, :] * taps[j] for j in range(4))
    sg = _sigmoid(c)
    return c, sg, taps


def _chunk_gates(G):
    r = lax.broadcasted_iota(jnp.int32, (LC, LC), 0)
    c = lax.broadcasted_iota(jnp.int32, (LC, LC), 1)
    tril = (c <= r).astype(F32)
    triu = (c >= r).astype(F32)
    eye = (c == r).astype(F32)
    logf = _logsig(G)
    b_col = jnp.dot(tril, logf, preferred_element_type=F32, precision=HI)
    b_row = lax.dot_general(logf, triu, (((0,), (0,)), ((), ())), preferred_element_type=F32, precision=HI)
    g_row = lax.dot_general(G, eye, (((0,), (0,)), ((), ())), preferred_element_type=F32, precision=HI)
    return b_col, b_row, g_row, tril, triu


def _colpick(X, lane):
    li = lax.broadcasted_iota(jnp.int32, X.shape, 1)
    return jnp.sum(jnp.where(li == lane, X, 0.0), axis=1, keepdims=True)


def _rowpick(XT, row):
    ri = lax.broadcasted_iota(jnp.int32, XT.shape, 0)
    return jnp.sum(jnp.where(ri == row, XT, 0.0), axis=0, keepdims=True)


def _mlstm_head(qh, kh, vh, G, b_col, b_row, g_row, h, Ch, nh, m_prev):
    bt = _colpick(b_col, 4 + h)
    i_col = _colpick(G, h)
    bs = _rowpick(b_row, 4 + h)
    i_row = _rowpick(g_row, h)
    r = lax.broadcasted_iota(jnp.int32, (LC, LC), 0)
    c = lax.broadcasted_iota(jnp.int32, (LC, LC), 1)
    log_d = jnp.where(c <= r, bt - bs + i_row, NEG)
    log_inter = bt + m_prev
    m_t = jnp.maximum(log_inter, jnp.max(log_d, axis=1, keepdims=True))
    Dm = jnp.exp(log_d - m_t)
    g = jnp.exp(log_inter - m_t)
    qb, kb, vb = _bf(qh), _bf(kh), _bf(vh)
    Am = _dot_nt(qb, kb) * Dm
    qC = _dot(qb, _bf(Ch))
    num = g * qC + _dot(_bf(Am), vb)
    qn = jnp.sum(qh * nh, axis=1, keepdims=True)
    den = g * qn + jnp.sum(Am, axis=1, keepdims=True)
    floor = jnp.exp(-m_t)
    dd = jnp.maximum(jnp.abs(den), floor)
    hh = num / dd
    lane = lax.broadcasted_iota(jnp.int32, (1, LC), 1)
    blast = jnp.sum(jnp.where(lane == LC - 1, bs, 0.0), axis=1, keepdims=True)
    log_s = blast - bt + i_col
    m_new = jnp.maximum(blast + m_prev, jnp.max(log_s, axis=0, keepdims=True))
    decay = jnp.exp(blast + m_prev - m_new)
    ws = jnp.exp(log_s - m_new)
    kw = kh * ws
    C_new = decay * Ch + _dot_tn(_bf(kw), vb)
    n_new = decay * nh + jnp.sum(kw, axis=0, keepdims=True)
    return dict(Dm=Dm, g=g, Am=Am, qC=qC, qn=qn, den=den, floor=floor, dd=dd, h=hh, decay=decay, ws=ws, kw=kw,
                C_new=C_new, n_new=n_new, m_new=m_new, qb=qb, kb=kb, vb=vb)


def _head_out(hh, mo_h, gn_h):
    r = lax.rsqrt(jnp.mean(hh * hh, axis=-1, keepdims=True) + EPS)
    hn = hh * r
    sg = _sigmoid(mo_h)
    return sg * (hn * gn_h), hn, r, sg


def _mlstm_fwd(mqk, mv, mo, gates, conv_w, conv_b, gate_b, gn, shards, dtypes):
    nblk = S // TB
    ncb = TB // LC
    nw = len(shards)

    def body(*refs):
        x_ref, v_ref, o_ref, g_ref, w_ref, b_ref, gb_ref, gn_ref = refs[:8]
        ins = refs[8:8 + nw]
        out_ref, cs_ref, ns_ref, ms_ref = refs[8 + nw:12 + nw]
        outs = refs[12 + nw:12 + 2 * nw]
        tail, Cst, nst, mst, qs, ks = refs[12 + 2 * nw:18 + 2 * nw]
        bufs = refs[18 + 2 * nw:18 + 3 * nw]
        ag_start, ag_forward, ag_finish = _gather_phases(ins, outs, bufs, *refs[18 + 3 * nw:])
        i = pl.program_id(0)
        pl.when(i == 0)(ag_start)
        pl.when(i == nblk // 2)(ag_forward)

        @pl.when(i == 0)
        def _():
            tail[...] = jnp.zeros_like(tail)
            Cst[...] = jnp.zeros_like(Cst)
            nst[...] = jnp.zeros_like(nst)
            mst[...] = jnp.zeros_like(mst)

        x = x_ref[...]
        xp = jnp.concatenate([tail[...], x], axis=0)
        tail[...] = x[TB - 8:TB, :]
        c, sg, _ = _conv_silu(xp, w_ref, b_ref, TB)
        y = c * sg
        qs[...] = y[:, 0:MW]
        ks[...] = y[:, MW:2 * MW] * (1.0 / math.sqrt(128.0))

        for cc in range(ncb):
            rows = slice(cc * LC, (cc + 1) * LC)
            G = g_ref[rows, :] + gb_ref[...]
            b_col, b_row, g_row, _, _ = _chunk_gates(G)
            cs_ref[cc] = Cst[...]
            ns_ref[cc] = nst[...]
            ms_ref[cc] = mst[...]
            for h in range(4):
                ln = slice(h * 128, (h + 1) * 128)
                m_prev = jnp.max(mst[0:1, ln], axis=1, keepdims=True)
                f = _mlstm_head(qs[rows, ln], ks[rows, ln], v_ref[rows, ln], G, b_col, b_row, g_row, h,
                                Cst[:, ln], nst[0:1, ln], m_prev)
                out, _, _, _ = _head_out(f["h"], o_ref[rows, ln], gn_ref[:, ln])
                out_ref[rows, ln] = out
                Cst[:, ln] = f["C_new"]
                nst[0:1, ln] = f["n_new"]
                mst[0:1, ln] = jnp.broadcast_to(f["m_new"], (1, 128))
        pl.when(i == nblk - 1)(ag_finish)

    row = lambda wd: pl.BlockSpec((TB, wd), lambda i: (i, 0))
    res = pl.pallas_call(
        body, name="mlstm_fwd", grid=(nblk,),
        in_specs=[row(1024), row(MW), row(MW), row(128), _cspec((4, 1024)), _cspec((1, 1024)), _cspec((1, 128)),
                  _cspec((1, MW))] + [VM] * nw,
        out_specs=[row(MW), pl.BlockSpec((ncb, 128, MW), lambda i: (i, 0, 0)),
                   pl.BlockSpec((ncb, 8, MW), lambda i: (i, 0, 0)), pl.BlockSpec((ncb, 8, MW), lambda i: (i, 0, 0))]
        + [ANY] * nw,
        out_shape=[jax.ShapeDtypeStruct((S, MW), F32), jax.ShapeDtypeStruct((S // LC, 128, MW), F32),
                   jax.ShapeDtypeStruct((S // LC, 8, MW), F32), jax.ShapeDtypeStruct((S // LC, 8, MW), F32)]
        + _gather_shapes(shards, dtypes),
        scratch_shapes=[pltpu.VMEM((8, 1024), F32), pltpu.VMEM((128, MW), F32), pltpu.VMEM((8, MW), F32),
                        pltpu.VMEM((8, MW), F32), pltpu.VMEM((TB, MW), F32), pltpu.VMEM((TB, MW), F32)]
        + _gather_scratch(shards, dtypes),
        compiler_params=_params(1),
    )(mqk, mv, mo, gates, conv_w, conv_b, gate_b, gn, *shards)
    return res[0], res[1], res[2], res[3], res[4:]


def _mlstm_bwd(mqk, mv, mo, gates, conv_w, conv_b, gate_b, gn, cs, ns, ms, dout, parts):
    nblk = S // TB
    ncb = TB // LC
    kscale = 1.0 / math.sqrt(128.0)
    nw = len(parts)

    def body(*refs):
        x_ref, xprev_ref, v_ref, o_ref, g_ref, w_ref, b_ref, gb_ref, gn_ref, cs_ref, ns_ref, ms_ref, do_ref = refs[:13]
        ins = refs[13:13 + nw]
        dx_ref, dv_ref, dmo_ref, dg_ref, dw_ref, db_ref, dgn_ref, dgb_ref = refs[13 + nw:21 + nw]
        outs = refs[21 + nw:21 + 2 * nw]
        dCst, dnst, dyhead, qs, ks, dqk = refs[21 + 2 * nw:27 + 2 * nw]
        rs_start, rs_finish = _scatter_phases(ins, outs, *refs[27 + 2 * nw:])
        i = pl.program_id(0)
        blk = nblk - 1 - i
        pl.when(i == 0)(rs_start)

        @pl.when(i == 0)
        def _():
            dCst[...] = jnp.zeros_like(dCst)
            dnst[...] = jnp.zeros_like(dnst)
            dyhead[...] = jnp.zeros_like(dyhead)
            dw_ref[...] = jnp.zeros_like(dw_ref)
            db_ref[...] = jnp.zeros_like(db_ref)
            dgn_ref[...] = jnp.zeros_like(dgn_ref)
            dgb_ref[...] = jnp.zeros_like(dgb_ref)

        x = x_ref[...]
        xprev = jnp.where(blk == 0, 0.0, xprev_ref[...])
        xp = jnp.concatenate([xprev, x], axis=0)
        c, sg, taps = _conv_silu(xp, w_ref, b_ref, TB)
        y = c * sg
        qs[...] = y[:, 0:MW]
        ks[...] = y[:, MW:2 * MW] * kscale
        lane128 = lax.broadcasted_iota(jnp.int32, (LC, 128), 1)
        rowi = lax.broadcasted_iota(jnp.int32, (LC, 1), 0)
        ones = jnp.ones((LC, 128), F32)

        for cc in reversed(range(ncb)):
            rows = slice(cc * LC, (cc + 1) * LC)
            G = g_ref[rows, :] + gb_ref[...]
            b_col, b_row, g_row, _, triu = _chunk_gates(G)
            dB = jnp.zeros((LC, 128), F32)
            dI = jnp.zeros((LC, 128), F32)
            for h in range(4):
                ln = slice(h * 128, (h + 1) * 128)
                Ch = cs_ref[cc, :, ln]
                nh = ns_ref[cc, 0:1, ln]
                m_prev = jnp.max(ms_ref[cc, 0:1, ln], axis=1, keepdims=True)
                qh, kh, vh = qs[rows, ln], ks[rows, ln], v_ref[rows, ln]
                f = _mlstm_head(qh, kh, vh, G, b_col, b_row, g_row, h, Ch, nh, m_prev)
                hh, dd, den, g, Am, Dm = f["h"], f["dd"], f["den"], f["g"], f["Am"], f["Dm"]
                qb, kb, vb = f["qb"], f["kb"], f["vb"]
                gn_h = gn_ref[:, ln]
                _, hn, r, sgo = _head_out(hh, o_ref[rows, ln], gn_h)
                do = do_ref[rows, ln]
                hm = hn * gn_h
                dmo_ref[rows, ln] = do * hm * sgo * (1.0 - sgo)
                dhm = do * sgo
                dgn_ref[:, ln] = dgn_ref[:, ln] + jnp.sum(dhm * hn, axis=0, keepdims=True)
                dhn = dhm * gn_h
                dh = r * (dhn - hn * jnp.mean(dhn * hn, axis=-1, keepdims=True))
                dnum = dh / dd
                ddd = -jnp.sum(dh * hh, axis=1, keepdims=True) / dd
                dden = jnp.where(jnp.abs(den) >= f["floor"], ddd * jnp.sign(den), 0.0)
                dnb = _bf(dnum)
                dA = _dot_nt(dnb, vb) + dden
                dv = _dot_tn(_bf(Am), dnb)
                gd = _bf(g * dnum)
                gq = g * dden
                dq = _dot_nt(gd, _bf(Ch)) + gq * nh
                dCn = dCst[:, ln]
                dnn = dnst[0:1, ln]
                dC = f["decay"] * dCn + _dot_tn(qb, gd)
                dn = f["decay"] * dnn + jnp.sum(gq * qh, axis=0, keepdims=True)
                dg = jnp.sum(dnum * f["qC"], axis=1, keepdims=True) + dden * f["qn"]
                dS = _bf(dA * Dm)
                dq = dq + _dot(dS, kb)
                dk = _dot_tn(dS, qb)
                Gm = dA * Am
                gam = dg * g
                dCb = _bf(dCn)
                E = _dot_nt(vb, dCb) + dnn
                ws = f["ws"]
                dk = dk + ws * E
                om = jnp.sum(E * kh, axis=1, keepdims=True) * ws
                dv = dv + _dot(_bf(f["kw"]), dCb)
                ddecay = (jnp.sum(jnp.sum(dCn * Ch, axis=1, keepdims=True), axis=0, keepdims=True)
                          + jnp.sum(dnn * nh, axis=1, keepdims=True))
                delta = ddecay * f["decay"]
                rows_g = jnp.sum(Gm, axis=1, keepdims=True)
                cols_g = lax.dot_general(Gm, ones, (((0,), (0,)), ((), ())), preferred_element_type=F32, precision=HI)
                last = jnp.where(rowi == LC - 1, jnp.sum(om, axis=0, keepdims=True) + delta, 0.0)
                db = rows_g + gam - om + last - cols_g
                di = cols_g + om
                dB = dB + jnp.where(lane128 == 4 + h, db, 0.0)
                dI = dI + jnp.where(lane128 == h, di, 0.0)
                dCst[:, ln] = dC
                dnst[0:1, ln] = dn
                dqk[rows, ln] = dq
                dqk[rows, MW + h * 128:MW + (h + 1) * 128] = dk * kscale
                dv_ref[rows, ln] = dv
            dlogf = jnp.dot(triu, dB, preferred_element_type=F32, precision=HI)
            dG = dI + dlogf * _sigmoid(-G)
            dG = jnp.where(lane128 < 8, dG, 0.0)
            dg_ref[rows, :] = dG
            dgb_ref[...] = dgb_ref[...] + jnp.sum(dG, axis=0, keepdims=True)

        dy = dqk[...] * (sg * (1.0 + c * (1.0 - sg)))
        db_ref[...] = db_ref[...] + jnp.sum(dy, axis=0, keepdims=True)
        for j in range(4):
            dw_ref[j:j + 1, :] = dw_ref[j:j + 1, :] + jnp.sum(dy * taps[j], axis=0, keepdims=True)
        dyp = jnp.concatenate([dy, dyhead[...]], axis=0)
        dx = w_ref[3:4, :] * dy
        for j in range(3):
            dx = dx + w_ref[j:j + 1, :] * pltpu.roll(dyp, TB + 8 - (3 - j), 0)[0:TB]
        dx_ref[...] = dx
        dyhead[...] = dy[0:8, :]
        pl.when(i == nblk - 1)(rs_finish)

    rrow = lambda wd: pl.BlockSpec((TB, wd), lambda i: (nblk - 1 - i, 0))
    st = lambda r: pl.BlockSpec((ncb, r, MW), lambda i: (nblk - 1 - i, 0, 0))
    prev8 = pl.BlockSpec((8, 1024), lambda i: (jnp.maximum((nblk - 1 - i) * (TB // 8) - 1, 0), 0))
    res = pl.pallas_call(
        body, name="mlstm_bwd", grid=(nblk,),
        in_specs=[rrow(1024), prev8, rrow(MW), rrow(MW), rrow(128), _cspec((4, 1024)), _cspec((1, 1024)),
                  _cspec((1, 128)), _cspec((1, MW)), st(128), st(8), st(8), rrow(MW)] + [ANY] * nw,
        out_specs=[rrow(1024), rrow(MW), rrow(MW), rrow(128),
                   pl.BlockSpec((4, 1024), lambda i: (0, 0)), pl.BlockSpec((1, 1024), lambda i: (0, 0)),
                   pl.BlockSpec((1, MW), lambda i: (0, 0)), pl.BlockSpec((1, 128), lambda i: (0, 0))] + [ANY] * nw,
        out_shape=[jax.ShapeDtypeStruct((S, 1024), F32), jax.ShapeDtypeStruct((S, MW), F32),
                   jax.ShapeDtypeStruct((S, MW), F32), jax.ShapeDtypeStruct((S, 128), F32),
                   jax.ShapeDtypeStruct((4, 1024), F32), jax.ShapeDtypeStruct((1, 1024), F32),
                   jax.ShapeDtypeStruct((1, MW), F32), jax.ShapeDtypeStruct((1, 128), F32)]
        + [jax.ShapeDtypeStruct(a.shape, a.dtype) for a in parts],
        scratch_shapes=[pltpu.VMEM((128, MW), F32), pltpu.VMEM((8, MW), F32), pltpu.VMEM((8, 1024), F32),
                        pltpu.VMEM((TB, MW), F32), pltpu.VMEM((TB, MW), F32), pltpu.VMEM((TB, 1024), F32)]
        + _scatter_scratch(nw),
        compiler_params=_params(1),
    )(mqk, mqk, mv, mo, gates, conv_w, conv_b, gate_b, gn, cs, ns, ms, dout, *parts)
    return res[:8], res[8:]


def _out_proj(x, attn, ml, w, g):
    tm = TM

    def body(x_ref, a_ref, m_ref, w_ref, g_ref, h_ref, u_ref):
        h1 = x_ref[...] + _dot(_bf(a_ref[...]), w_ref[0:AW, :]) + _dot(_bf(m_ref[...]), w_ref[AW:D, :])
        h_ref[...] = h1
        n, _ = _rms(h1)
        u_ref[...] = _bf(n * g_ref[...])

    row = lambda wd: pl.BlockSpec((tm, wd), lambda i: (i, 0))
    return pl.pallas_call(
        body, name="out_proj", grid=(S // tm,),
        in_specs=[row(D), row(AW), row(MW), _cspec((D, D)), _cspec((1, D))],
        out_specs=[row(D), row(D)],
        out_shape=[jax.ShapeDtypeStruct((S, D), F32), jax.ShapeDtypeStruct((S, D), BF16)],
        compiler_params=_params(1),
    )(x, attn, ml, w, g)


def _mlp_fwd(h1, u2, w_up, w_down):
    tm = TM

    def body(h_ref, u_ref, wu_ref, wd_ref, a_ref, o_ref):
        u = u_ref[...]
        acc = h_ref[...]
        for c in range(NDEV):
            cols = slice(c * 512, (c + 1) * 512)
            a = _dot(u, wu_ref[c])
            a_ref[:, cols] = _bf(a)
            r = jnp.maximum(a, 0.0)
            acc = acc + _dot(_bf(r * r), wd_ref[cols, :])
        o_ref[...] = acc

    row = lambda wd: pl.BlockSpec((tm, wd), lambda i: (i, 0))
    return pl.pallas_call(
        body, name="mlp_fwd", grid=(S // tm,),
        in_specs=[row(D), row(D), _cspec((NDEV, D, DFF // NDEV)), _cspec((DFF, D))],
        out_specs=[row(DFF), row(D)],
        out_shape=[jax.ShapeDtypeStruct((S, DFF), BF16), jax.ShapeDtypeStruct((S, D), F32)],
        compiler_params=_params(1),
    )(h1, u2, w_up, w_down)


def _ple_loss(h2, p, target, w_pg, w_ple, g_ple, g_fin):
    tm = TM

    def body(h_ref, p_ref, t_ref, wg_ref, wp_ref, gp_ref, gf_ref,
             dh_ref, dwg_ref, dwp_ref, dgp_ref, dgf_ref, loss_ref, acc_g, acc_p):
        i = pl.program_id(0)

        @pl.when(i == 0)
        def _():
            acc_g[...] = jnp.zeros_like(acc_g)
            acc_p[...] = jnp.zeros_like(acc_p)
            dgp_ref[...] = jnp.zeros_like(dgp_ref)
            dgf_ref[...] = jnp.zeros_like(dgf_ref)
            loss_ref[...] = jnp.zeros_like(loss_ref)

        h2v = h_ref[...]
        n2, rs2 = _rms(h2v)
        u3 = _bf(n2 * gp_ref[...])
        gt = _sigmoid(_dot(u3, wg_ref[...]))
        pb = _bf(p_ref[...])
        e = jnp.concatenate([_dot(pb, wp_ref[j]) for j in range(NDEV)], axis=1)
        h3 = h2v + gt * e
        n3, rs3 = _rms(h3)
        err = n3 * gf_ref[...] - t_ref[...]
        loss_ref[...] = loss_ref[...] + 0.5 / D * jnp.sum(jnp.sum(err * err, axis=1, keepdims=True), axis=0, keepdims=True)
        dy = err * (1.0 / D)
        dgf_ref[...] = dgf_ref[...] + jnp.sum(dy * n3, axis=0, keepdims=True)
        dh3 = _rms_bwd(dy, n3, rs3, gf_ref[...])
        de = _bf(dh3 * gt)
        dz = _bf(dh3 * e * gt * (1.0 - gt))
        acc_p[...] = acc_p[...] + _dot_tn(pb, de)
        acc_g[...] = acc_g[...] + _dot_tn(u3, dz)
        du3 = _dot_nt(dz, wg_ref[...])
        dgp_ref[...] = dgp_ref[...] + jnp.sum(du3 * n2, axis=0, keepdims=True)
        dh_ref[...] = dh3 + _rms_bwd(du3, n2, rs2, gp_ref[...])

        @pl.when(i == S // tm - 1)
        def _():
            dwg_ref[...] = _bf(acc_g[...])
            for j in range(NDEV):
                dwp_ref[j] = _bf(acc_p[:, j * 128:(j + 1) * 128])

    row = lambda wd: pl.BlockSpec((tm, wd), lambda i: (i, 0))
    whole = lambda shp: pl.BlockSpec(shp, lambda i: (0,) * len(shp))
    return pl.pallas_call(
        body, name="ple_loss", grid=(S // tm,),
        in_specs=[row(D), row(PLE), row(D), _cspec((D, D)), _cspec((NDEV, PLE, 128)), _cspec((1, D)), _cspec((1, D))],
        out_specs=[row(D), whole((D, D)), whole((NDEV, PLE, 128)), whole((1, D)), whole((1, D)), whole((1, 1))],
        out_shape=[jax.ShapeDtypeStruct((S, D), F32), jax.ShapeDtypeStruct((D, D), BF16),
                   jax.ShapeDtypeStruct((NDEV, PLE, 128), BF16), jax.ShapeDtypeStruct((1, D), F32),
                   jax.ShapeDtypeStruct((1, D), F32), jax.ShapeDtypeStruct((1, 1), F32)],
        scratch_shapes=[pltpu.VMEM((D, D), F32), pltpu.VMEM((PLE, D), F32)],
        compiler_params=_params(1),
    )(h2, p, target, w_pg, w_ple, g_ple, g_fin)


def _mlp_bwd(dh2, a, h1, g, w_up, w_down):
    tm = TM

    def body(d_ref, a_ref, h_ref, g_ref, wu_ref, wd_ref, da_ref, dh1_ref, dg_ref):
        @pl.when(pl.program_id(0) == 0)
        def _():
            dg_ref[...] = jnp.zeros_like(dg_ref)

        dh2v = d_ref[...]
        db = _bf(dh2v)
        du = jnp.zeros((tm, D), F32)
        for c in range(NDEV):
            cols = slice(c * 512, (c + 1) * 512)
            dr = _dot_nt(db, wd_ref[cols, :])
            da = _bf(dr * (2.0 * jnp.maximum(a_ref[:, cols], 0.0)))
            da_ref[:, cols] = da
            du = du + _dot_nt(da, wu_ref[c])
        n, rs = _rms(h_ref[...])
        dg_ref[...] = dg_ref[...] + jnp.sum(du * n, axis=0, keepdims=True)
        dh1_ref[...] = dh2v + _rms_bwd(du, n, rs, g_ref[...])

    row = lambda wd: pl.BlockSpec((tm, wd), lambda i: (i, 0))
    return pl.pallas_call(
        body, name="mlp_bwd", grid=(S // tm,),
        in_specs=[row(D), row(DFF), row(D), _cspec((1, D)), _cspec((NDEV, D, DFF // NDEV)), _cspec((DFF, D))],
        out_specs=[row(DFF), row(D), pl.BlockSpec((1, D), lambda i: (0, 0))],
        out_shape=[jax.ShapeDtypeStruct((S, DFF), BF16), jax.ShapeDtypeStruct((S, D), F32),
                   jax.ShapeDtypeStruct((1, D), F32)],
        compiler_params=_params(1),
    )(dh2, a, h1, g, w_up, w_down)


def _out_proj_bwd(dh1, attn, ml, w):
    tm = TM

    def body(d_ref, a_ref, m_ref, w_ref, da_ref, dm_ref, dw_ref, acc):
        i = pl.program_id(0)

        @pl.when(i == 0)
        def _():
            acc[...] = jnp.zeros_like(acc)

        db = _bf(d_ref[...])
        dmix = _dot_nt(db, w_ref[...])
        da_ref[...] = dmix[:, 0:AW]
        dm_ref[...] = dmix[:, AW:D]
        acc[0:AW, :] = acc[0:AW, :] + _dot_tn(_bf(a_ref[...]), db)
        acc[AW:D, :] = acc[AW:D, :] + _dot_tn(_bf(m_ref[...]), db)

        @pl.when(i == S // tm - 1)
        def _():
            dw_ref[...] = _bf(acc[...])

    row = lambda wd: pl.BlockSpec((tm, wd), lambda i: (i, 0))
    return pl.pallas_call(
        body, name="out_proj_bwd", grid=(S // tm,),
        in_specs=[row(D), row(AW), row(MW), _cspec((D, D))],
        out_specs=[row(AW), row(MW), pl.BlockSpec((D, D), lambda i: (0, 0))],
        out_shape=[jax.ShapeDtypeStruct((S, AW), F32), jax.ShapeDtypeStruct((S, MW), F32),
                   jax.ShapeDtypeStruct((D, D), BF16)],
        scratch_shapes=[pltpu.VMEM((D, D), F32)],
        compiler_params=_params(1),
    )(dh1, attn, ml, w)


def _in_proj_bwd(dq, dk, dv, dmqk, dmv, dmo, dgt, dh1, x, g1, w, rc, ra, rb):
    tm = TM

    def body(dq_ref, dk_ref, dv_ref, dmqk_ref, dmv_ref, dmo_ref, dgt_ref, dh_ref, x_ref, g_ref, w_ref,
             rc_ref, ra_ref, rb_ref, dp_ref, dx_ref, dg_ref):
        @pl.when(pl.program_id(0) == 0)
        def _():
            dg_ref[...] = jnp.zeros_like(dg_ref)

        c, a, b = rc_ref[...], ra_ref[...], rb_ref[...]
        for half, ref in enumerate((dq_ref, dk_ref)):
            for t in range(4):
                lo = half * 512 + t * 128
                dp_ref[:, lo:lo + 128] = _bf(_rope_bwd(ref[:, t * 128:(t + 1) * 128], c, a, b))
        dp_ref[:, 1024:1536] = _bf(dv_ref[...])
        dp_ref[:, 1536:2560] = _bf(dmqk_ref[...])
        dp_ref[:, 2560:3072] = _bf(dmv_ref[...])
        dp_ref[:, 3072:3584] = _bf(dmo_ref[...])
        dp_ref[:, 3584:3712] = _bf(dgt_ref[...])
        dp_ref[:, 3712:PW] = jnp.zeros((tm, PW - 3712), BF16)
        du = jnp.zeros((tm, D), F32)
        for s in range(PW // 768):
            cols = slice(s * 768, (s + 1) * 768)
            du = du + _dot_nt(dp_ref[:, cols], w_ref[:, cols])
        n, rs = _rms(x_ref[...])
        dg_ref[...] = dg_ref[...] + jnp.sum(du * n, axis=0, keepdims=True)
        dx_ref[...] = dh_ref[...] + _rms_bwd(du, n, rs, g_ref[...])

    row = lambda wd: pl.BlockSpec((tm, wd), lambda i: (i, 0))
    return pl.pallas_call(
        body, name="in_proj_bwd", grid=(S // tm,),
        in_specs=[row(AW), row(AW), row(AW), row(1024), row(MW), row(MW), row(128), row(D), row(D), _cspec((1, D)),
                  _cspec((D, PW)), row(128), row(128), row(128)],
        out_specs=[row(PW), row(D), pl.BlockSpec((1, D), lambda i: (0, 0))],
        out_shape=[jax.ShapeDtypeStruct((S, PW), BF16), jax.ShapeDtypeStruct((S, D), F32),
                   jax.ShapeDtypeStruct((1, D), F32)],
        compiler_params=_params(1),
    )(dq, dk, dv, dmqk, dmv, dmo, dgt, dh1, x, g1, w, rc, ra, rb)


SMALL_ROWS = 64


def _small_phases(ins, out_ref, pack, rbuf, send_sems, recv_sems):
    x, y, c = _place()
    me = _dev_index(x, y, c)

    def copies():
        out = []
        for k, (dx, dy, dc) in enumerate(FLIPS):
            peer = ((x + dx) % 2, (y + dy) % 2, (c + dc) % 2)
            out.append(pltpu.make_async_remote_copy(
                src_ref=pack, dst_ref=rbuf.at[me], send_sem=send_sems.at[k], recv_sem=recv_sems.at[k],
                device_id=peer, device_id_type=MESH))
        return out

    def start():
        pack[...] = jnp.zeros_like(pack)
        for i, ref in enumerate(ins):
            pack[8 * i:8 * i + 1, 0:ref.shape[1]] = ref[...]
        rbuf[me] = pack[...]
        for cp in copies():
            cp.start()

    def finish():
        for cp in copies():
            cp.wait()
        tot = rbuf[0]
        for j in range(1, NDEV):
            tot = tot + rbuf[j]
        out_ref[...] = tot

    return start, finish


def _wgrad(name, A, B, a_fn, b_fn, tk, tn, out_shape, out_spec, ts=512, split=None, small=()):
    K, N = A.shape[1], B.shape[1]
    nrt = S // ts
    nc = next(c for c in (1024, 1280, tn) if tn % c == 0)
    ns = len(small)
    grid = (N // tn, K // tk, nrt)

    def body(*refs):
        a_ref, b_ref = refs[:2]
        o_ref = refs[2 + ns]
        acc = refs[3 + ns + bool(ns)]
        r = pl.program_id(2)
        if ns:
            step = (pl.program_id(0) * grid[1] + pl.program_id(1)) * nrt + r
            sm_start, sm_finish = _small_phases(refs[2:2 + ns], refs[3 + ns], *refs[4 + ns + 1:])
            pl.when(step == 0)(sm_start)

        @pl.when(r == 0)
        def _():
            acc[...] = jnp.zeros_like(acc)

        at = a_fn(a_ref[...]).T
        for c in range(tn // nc):
            cols = slice(c * nc, (c + 1) * nc)
            acc[:, cols] = acc[:, cols] + _dot(at, b_fn(b_ref[:, cols]))

        @pl.when(r == nrt - 1)
        def _():
            if split is None:
                o_ref[...] = _bf(acc[...])
            else:
                for j in range(NDEV):
                    o_ref[j] = _bf(acc[:, split * j:split * (j + 1)])

        if ns:
            pl.when(step == grid[0] * grid[1] * nrt - 1)(sm_finish)

    in_specs = [pl.BlockSpec((ts, tk), lambda n, k, r: (r, k)), pl.BlockSpec((ts, tn), lambda n, k, r: (r, n))]
    scratch = [pltpu.VMEM((tk, tn), F32)]
    if not ns:
        return pl.pallas_call(
            body, name=name, grid=grid, in_specs=in_specs, out_specs=out_spec,
            out_shape=jax.ShapeDtypeStruct(out_shape, BF16), scratch_shapes=scratch, compiler_params=_params(3),
        )(A, B)
    return pl.pallas_call(
        body, name=name, grid=grid, in_specs=in_specs + [VM] * ns, out_specs=[out_spec, VM],
        out_shape=[jax.ShapeDtypeStruct(out_shape, BF16), jax.ShapeDtypeStruct((SMALL_ROWS, 1024), F32)],
        scratch_shapes=scratch + [pltpu.VMEM((SMALL_ROWS, 1024), F32), pltpu.VMEM((NDEV, SMALL_ROWS, 1024), F32),
                                  pltpu.SemaphoreType.DMA((7,)), pltpu.SemaphoreType.DMA((7,))],
        compiler_params=_params(3),
    )(A, B, *small)


def _relu2_bf(a):
    r = jnp.maximum(a.astype(F32), 0.0)
    return _bf(r * r)


def _ident(a):
    return a


def _step(x, p, target, g1, conv_b, gate_b, gn, g_mlp, g_ple, g_fin, sh):
    g_in, g_conv = _gather_weights([sh["w_in"], sh["conv_w"]], [BF16, F32])
    conv_w = g_conv.transpose(1, 0, 2).reshape(4, 1024)
    rc, ra, rb = _rope_tables()
    w_in_p = _join_w_in(g_in)
    qkv, mqk, mv, mo, gates, u1 = _in_proj(x, g1, w_in_p, rc, ra, rb)
    attn, lse, (w_up8, w_down8) = _attn_fwd(qkv, [sh["w_up"], sh["w_down"]], [BF16] * 2)
    ml, cs, ns, ms, (w_out8, w_pg8, w_ple8) = _mlstm_fwd(
        mqk, mv, mo, gates, conv_w, conv_b, gate_b, gn, [sh["w_out"], sh["w_ple_gate"], sh["w_ple"]], [BF16] * 3)
    w_out, w_down, w_pg = w_out8.reshape(D, D), w_down8.reshape(DFF, D), w_pg8.reshape(D, D)
    h1, u2 = _out_proj(x, attn, ml, w_out, g_mlp)
    a, h2 = _mlp_fwd(h1, u2, w_up8, w_down)
    dh2, dw_pg, dw_ple8, dg_ple, dg_fin, loss = _ple_loss(h2, p, target, w_pg, w_ple8, g_ple, g_fin)
    da, dh1, dg_mlp = _mlp_bwd(dh2, a, h1, g_mlp, w_up8, w_down)
    dw_up8 = _wgrad("wgrad_up", u2, da, _ident, _ident, D, DFF, (NDEV, D, DFF // NDEV),
                    pl.BlockSpec((NDEV, D, DFF // NDEV), lambda n, k, r: (0, 0, 0)), split=DFF // NDEV)
    dw_down = _wgrad("wgrad_down", a, dh2, _relu2_bf, _bf, 1024, 1024, (DFF, D),
                     pl.BlockSpec((1024, 1024), lambda n, k, r: (k, n)))
    d_attn, d_ml, dw_out = _out_proj_bwd(dh1, attn, ml, w_out)
    (dmqk, dmv, dmo, dgt, dconv_w, dconv_b, dgn, dgate_b), (r_out, r_up, r_pg, r_ple) = _mlstm_bwd(
        mqk, mv, mo, gates, conv_w, conv_b, gate_b, gn, cs, ns, ms, d_ml,
        [dw_out.reshape(NDEV, D // NDEV, D), dw_up8, dw_pg.reshape(NDEV, D // NDEV, D), dw_ple8])
    dq, dk, dv, (r_down,) = _attn_bwd(qkv, attn, lse, d_attn, [dw_down.reshape(NDEV, DFF // NDEV, D)])
    dproj, dx, dg1 = _in_proj_bwd(dq, dk, dv, dmqk, dmv, dmo, dgt, dh1, x, g1, w_in_p, rc, ra, rb)
    small = dict(norm_mix_g=dg1, conv_b=dconv_b, gate_b=dgate_b, mlstm_norm_g=dgn, norm_mlp_g=dg_mlp,
                 norm_ple_g=dg_ple, final_norm_g=dg_fin)
    dw_in8, total = _wgrad("wgrad_in", u1, dproj, _ident, _ident, D, PW, (NDEV, D, IN_W // NDEV),
                           pl.BlockSpec((NDEV, D, IN_W // NDEV), lambda n, k, r: (0, 0, 0)), split=IN_W // NDEV,
                           small=[small[n] for n in SMALL] + [loss])
    recv_in, recv_conv = _scatter_grads([dw_in8, dconv_w.reshape(4, NDEV, 128).transpose(1, 0, 2)])
    recv = dict(w_in=recv_in, conv_w=recv_conv, w_out=r_out, w_up=r_up, w_down=r_down, w_ple_gate=r_pg, w_ple=r_ple)
    return dx, recv, total


def _gather_weights(shards, dtypes):
    nw = len(shards)

    def body(*refs):
        start, forward, finish = _gather_phases(refs[:nw], refs[nw:2 * nw], refs[2 * nw:3 * nw], *refs[3 * nw:])
        start()
        forward()
        finish()

    return pl.pallas_call(
        body, name="gather_weights",
        in_specs=[VM] * nw, out_specs=[ANY] * nw,
        out_shape=_gather_shapes(shards, dtypes),
        scratch_shapes=_gather_scratch(shards, dtypes),
        compiler_params=_params(),
    )(*shards)


def _scatter_grads(parts):
    nw = len(parts)

    def body(*refs):
        start, finish = _scatter_phases(refs[:nw], refs[nw:2 * nw], *refs[2 * nw:])
        start()
        finish()

    return pl.pallas_call(
        body, name="scatter_grads",
        in_specs=[ANY] * nw, out_specs=[ANY] * nw,
        out_shape=[jax.ShapeDtypeStruct(a.shape, a.dtype) for a in parts],
        scratch_shapes=_scatter_scratch(nw),
        compiler_params=_params(),
    )(*parts)


def _adamw(name, gparts, w, m, v, tr):
    P, R, C = gparts.shape
    c1 = 1.0 - ADAM_B1 ** ADAM_STEP
    c2 = 1.0 - ADAM_B2 ** ADAM_STEP

    def body(g_ref, w_ref, m_ref, v_ref, go_ref, d_ref, mo_ref, vo_ref):
        g = g_ref[0].astype(F32)
        for j in range(1, P):
            g = g + g_ref[j].astype(F32)
        m2 = ADAM_B1 * m_ref[...] + (1.0 - ADAM_B1) * g
        v2 = ADAM_B2 * v_ref[...] + (1.0 - ADAM_B2) * (g * g)
        go_ref[...] = g
        mo_ref[...] = m2
        vo_ref[...] = v2
        d_ref[...] = -ADAM_LR * ((m2 / c1) / (jnp.sqrt(v2 / c2) + ADAM_EPS) + ADAM_WD * w_ref[...])

    row = pl.BlockSpec((tr, C), lambda i: (i, 0))
    return pl.pallas_call(
        body, name=name, grid=(R // tr,),
        in_specs=[pl.BlockSpec((P, tr, C), lambda i: (0, i, 0)), row, row, row],
        out_specs=[row] * 4,
        out_shape=[jax.ShapeDtypeStruct((R, C), F32)] * 4,
        compiler_params=_params(1),
    )(gparts, w, m, v)


SMALL = ("norm_mix_g", "conv_b", "gate_b", "mlstm_norm_g", "norm_mlp_g", "norm_ple_g", "final_norm_g")


def _pack_small(vals):
    return jnp.concatenate([jnp.pad(a, ((0, 7), (0, 1024 - a.shape[1]))) for a in vals], axis=0)


def kernel(x, p, norm_mix_g, w_in, conv_w, conv_b, gate_b, mlstm_norm_g, w_out, norm_mlp_g, w_up, w_down, norm_ple_g, w_ple_gate, w_ple, final_norm_g, loss_target, m_norm_mix_g, m_w_in, m_conv_w, m_conv_b, m_gate_b, m_mlstm_norm_g, m_w_out, m_norm_mlp_g, m_w_up, m_w_down, m_norm_ple_g, m_w_ple_gate, m_w_ple, m_final_norm_g, v_norm_mix_g, v_w_in, v_conv_w, v_conv_b, v_gate_b, v_mlstm_norm_g, v_w_out, v_norm_mlp_g, v_w_up, v_w_down, v_norm_ple_g, v_w_ple_gate, v_w_ple, v_final_norm_g):
    big_names = ("w_in", "conv_w", "w_out", "w_up", "w_down", "w_ple_gate", "w_ple")
    wts = dict(w_in=w_in, conv_w=conv_w, w_out=w_out, w_up=w_up, w_down=w_down, w_ple_gate=w_ple_gate, w_ple=w_ple)
    mom = dict(w_in=m_w_in, conv_w=m_conv_w, w_out=m_w_out, w_up=m_w_up, w_down=m_w_down, w_ple_gate=m_w_ple_gate,
               w_ple=m_w_ple)
    var = dict(w_in=v_w_in, conv_w=v_conv_w, w_out=v_w_out, w_up=v_w_up, w_down=v_w_down, w_ple_gate=v_w_ple_gate,
               w_ple=v_w_ple)
    sq = lambda a: a.reshape(a.shape[1:])
    fin = final_norm_g.reshape(1, D)
    dx, recv, total = _step(
        x[0], p[0, 0], loss_target[0], norm_mix_g, conv_b, jnp.pad(gate_b, ((0, 0), (0, 120))), mlstm_norm_g,
        norm_mlp_g, norm_ple_g, fin, {n: sq(wts[n]) for n in big_names})

    out = {}
    for n, tr in zip(big_names, (256, 4, 128, 256, 256, 128, 256)):
        res = _adamw("adamw_" + n, recv[n], sq(wts[n]), sq(mom[n]), sq(var[n]), tr)
        out[n] = [t.reshape(wts[n].shape) for t in res]
    sw = dict(norm_mix_g=norm_mix_g, conv_b=conv_b, gate_b=gate_b, mlstm_norm_g=mlstm_norm_g, norm_mlp_g=norm_mlp_g,
              norm_ple_g=norm_ple_g, final_norm_g=fin)
    sm = dict(norm_mix_g=m_norm_mix_g, conv_b=m_conv_b, gate_b=m_gate_b, mlstm_norm_g=m_mlstm_norm_g,
              norm_mlp_g=m_norm_mlp_g, norm_ple_g=m_norm_ple_g, final_norm_g=m_final_norm_g.reshape(1, D))
    sv = dict(norm_mix_g=v_norm_mix_g, conv_b=v_conv_b, gate_b=v_gate_b, mlstm_norm_g=v_mlstm_norm_g,
              norm_mlp_g=v_norm_mlp_g, norm_ple_g=v_norm_ple_g, final_norm_g=v_final_norm_g.reshape(1, D))
    nrow = 8 * len(SMALL)
    res = _adamw("adamw_small", total[0:nrow].reshape(1, nrow, 1024), _pack_small([sw[n] for n in SMALL]),
                 _pack_small([sm[n] for n in SMALL]), _pack_small([sv[n] for n in SMALL]), nrow)
    for i, n in enumerate(SMALL):
        shp = final_norm_g.shape if n == "final_norm_g" else sw[n].shape
        out[n] = [t[8 * i, 0:sw[n].shape[1]].reshape(shp) for t in res]
    order = ("norm_mix_g", "w_in", "conv_w", "conv_b", "gate_b", "mlstm_norm_g", "w_out", "norm_mlp_g", "w_up", "w_down",
             "norm_ple_g", "w_ple_gate", "w_ple", "final_norm_g")
    loss_all = total[nrow, 0]
    return (loss_all, dx[None], *[out[n][0] for n in order], *[out[n][1] for n in order],
            *[out[n][2] for n in order], *[out[n][3] for n in order])
```

```python
import functools
import math

import jax
import jax.numpy as jnp
from jax import lax
from jax.experimental import pallas as pl
from jax.experimental.pallas import tpu as pltpu

F32, BF16 = jnp.float32, jnp.bfloat16
S = 4096
D = 1024
AW = 512
MW = 512
DFF = 4096
PLE = 256
IN_W = 3592
PW = 3840
NDEV = 8
EPS = 1e-6
NEG = -1e30
LC = 128
TB = 256
ROPE_THETA = 500000.0
VMEM_LIMIT = 56 * 1024 * 1024
HI = lax.Precision.HIGHEST

ADAM_LR, ADAM_B1, ADAM_B2, ADAM_EPS, ADAM_WD, ADAM_STEP = 0.001, 0.9, 0.999, 1e-08, 0.01, 10


def _params(n_grid=0, **kw):
    sem = dict(dimension_semantics=("arbitrary",) * n_grid) if n_grid else {}
    return pltpu.CompilerParams(vmem_limit_bytes=VMEM_LIMIT, **sem, **kw)


def _cspec(shape):
    nd = len(shape)
    return pl.BlockSpec(shape, lambda *_: (0,) * nd, pipeline_mode=pl.Buffered(1))


def _dot(a, b):
    return jnp.dot(a, b, preferred_element_type=F32)


def _dot_nt(a, b):
    return lax.dot_general(a, b, (((1,), (1,)), ((), ())), preferred_element_type=F32)


def _dot_tn(a, b):
    return lax.dot_general(a, b, (((0,), (0,)), ((), ())), preferred_element_type=F32)


def _bf(x):
    return x.astype(BF16)


def _rms(x):
    rs = lax.rsqrt(jnp.mean(x * x, axis=-1, keepdims=True) + EPS)
    return x * rs, rs


def _rms_bwd(du, n, rs, g):
    dn = du * g
    return rs * (dn - n * jnp.mean(dn * n, axis=-1, keepdims=True))


def _sigmoid(x):
    return 1.0 / (1.0 + jnp.exp(-x))


def _rope_tables():
    j = lax.broadcasted_iota(jnp.int32, (S, 128), 1) % 64
    pos = lax.broadcasted_iota(jnp.int32, (S, 128), 0).astype(F32)
    inv_freq = jnp.power(ROPE_THETA, -(j % 8).astype(F32) / 8.0)
    ang = pos * inv_freq
    cos, sin = jnp.cos(ang), jnp.sin(ang)
    c = jnp.where(j < 16, cos, 1.0)
    a = jnp.where(j < 8, -sin, 0.0)
    b = jnp.where((j >= 8) & (j < 16), sin, 0.0)
    return c, a, b


def _rope(blk, c, a, b):
    return blk * c + pltpu.roll(blk, 120, 1) * a + pltpu.roll(blk, 8, 1) * b


def _rope_bwd(d, c, a, b):
    return d * c + pltpu.roll(d * a, 8, 1) + pltpu.roll(d * b, 120, 1)


MESH = pl.DeviceIdType.MESH
ANY = pl.BlockSpec(memory_space=pl.ANY)
VM = pl.BlockSpec(memory_space=pltpu.VMEM)
FLIPS = [(dx, dy, dc) for dx in (0, 1) for dy in (0, 1) for dc in (0, 1)][1:]


def _place():
    return lax.axis_index("x"), lax.axis_index("y"), lax.axis_index("c")


def _dev_index(px, py, pc):
    return 4 * px + 2 * py + pc


def _gather_phases(ins, outs, bufs, send_sems=None, recv_sems=None, local_sems=None):
    nw = len(ins)
    if nw == 0:
        return (lambda: None,) * 3
    x, y, c = _place()
    me, sib = (x, y, c), (x, y, 1 - c)
    chips = [(1 - x, y), (x, 1 - y), (1 - x, 1 - y)]

    def copy(w, k, block, to, from_buf=False):
        dst = outs[w].at[_dev_index(*block)]
        return pltpu.make_async_remote_copy(
            src_ref=bufs[w] if from_buf else dst, dst_ref=dst, send_sem=send_sems.at[w, k],
            recv_sem=recv_sems.at[w, k], device_id=to, device_id_type=MESH)

    def mine(w):
        return pltpu.make_async_copy(bufs[w], outs[w].at[_dev_index(*me)], local_sems.at[w])

    def first(w):
        return [copy(w, 0, me, sib, True)] + [copy(w, 1 + j, me, (*chip, c), True) for j, chip in enumerate(chips)]

    def passed(w):
        return [copy(w, 4 + j, (*chip, c), sib) for j, chip in enumerate(chips)]

    def start():
        for w in range(nw):
            bufs[w][...] = ins[w][...].astype(bufs[w].dtype)
        for w in range(nw):
            mine(w).start()
            for cp in first(w):
                cp.start()

    def forward():
        for j, chip in enumerate(chips):
            for w in range(nw):
                copy(w, 1 + j, (*chip, c), me).wait_recv()
                passed(w)[j].start()

    def finish():
        for w in range(nw):
            copy(w, 0, sib, me).wait_recv()
        for j, chip in enumerate(chips):
            for w in range(nw):
                copy(w, 4 + j, (*chip, 1 - c), me).wait_recv()
        for w in range(nw):
            for cp in first(w) + passed(w):
                cp.wait_send()
            mine(w).wait()

    return start, forward, finish


def _gather_scratch(shards, dtypes):
    nw = len(shards)
    if nw == 0:
        return []
    return ([pltpu.VMEM(s.shape, dt) for s, dt in zip(shards, dtypes)]
            + [pltpu.SemaphoreType.DMA((nw, 7)), pltpu.SemaphoreType.DMA((nw, 7)), pltpu.SemaphoreType.DMA((nw,))])


def _gather_shapes(shards, dtypes):
    return [jax.ShapeDtypeStruct((NDEV, *s.shape), dt) for s, dt in zip(shards, dtypes)]


def _scatter_phases(ins, outs, send_sems=None, recv_sems=None, local_sems=None):
    nw = len(ins)
    if nw == 0:
        return (lambda: None,) * 2
    x, y, c = _place()
    me = _dev_index(x, y, c)

    def copies():
        out = []
        for w in range(nw):
            out.append(pltpu.make_async_copy(ins[w].at[me], outs[w].at[me], local_sems.at[w]))
            for k, (dx, dy, dc) in enumerate(FLIPS):
                peer = ((x + dx) % 2, (y + dy) % 2, (c + dc) % 2)
                out.append(pltpu.make_async_remote_copy(
                    src_ref=ins[w].at[_dev_index(*peer)], dst_ref=outs[w].at[me], send_sem=send_sems.at[w, k],
                    recv_sem=recv_sems.at[w, k], device_id=peer, device_id_type=MESH))
        return out

    def start():
        for cp in copies():
            cp.start()

    def finish():
        for cp in copies():
            cp.wait()

    return start, finish


def _scatter_scratch(nw):
    if nw == 0:
        return []
    return [pltpu.SemaphoreType.DMA((nw, 7)), pltpu.SemaphoreType.DMA((nw, 7)), pltpu.SemaphoreType.DMA((nw,))]


TM = 512


def _join_w_in(wg):
    sw = IN_W // NDEV

    def body(wg_ref, w_ref):
        for j in range(NDEV):
            w_ref[:, sw * j:sw * (j + 1)] = wg_ref[j]
        w_ref[:, IN_W:PW] = jnp.zeros((D, PW - IN_W), BF16)

    return pl.pallas_call(body, name="join_w_in", out_shape=jax.ShapeDtypeStruct((D, PW), BF16),
                          compiler_params=_params())(wg)


def _in_proj(x, g1, w, rc, ra, rb):
    tm = TM

    def body(x_ref, g_ref, w_ref, rc_ref, ra_ref, rb_ref, qkv_ref, mqk_ref, mv_ref, mo_ref, gt_ref, u_ref):
        n, _ = _rms(x_ref[...])
        u = _bf(n * g_ref[...])
        u_ref[...] = u
        c, a, b = rc_ref[...], ra_ref[...], rb_ref[...]
        for half in range(2):
            blk = _dot(u, w_ref[:, half * 512:(half + 1) * 512])
            for t in range(4):
                lo = half * 512 + t * 128
                qkv_ref[:, lo:lo + 128] = _rope(blk[:, t * 128:(t + 1) * 128], c, a, b)
        qkv_ref[:, 1024:1536] = _dot(u, w_ref[:, 1024:1536])
        mqk_ref[:, 0:512] = _dot(u, w_ref[:, 1536:2048])
        mqk_ref[:, 512:1024] = _dot(u, w_ref[:, 2048:2560])
        mv_ref[...] = _dot(u, w_ref[:, 2560:3072])
        mo_ref[...] = _dot(u, w_ref[:, 3072:3584])
        gt_ref[...] = _dot(u, w_ref[:, 3584:3712])

    row = lambda wd: pl.BlockSpec((tm, wd), lambda i: (i, 0))
    return pl.pallas_call(
        body, name="in_proj", grid=(S // tm,),
        in_specs=[row(D), _cspec((1, D)), _cspec((D, PW)), row(128), row(128), row(128)],
        out_specs=[row(1536), row(1024), row(512), row(512), row(128), row(D)],
        out_shape=[jax.ShapeDtypeStruct((S, 1536), F32), jax.ShapeDtypeStruct((S, 1024), F32),
                   jax.ShapeDtypeStruct((S, 512), F32), jax.ShapeDtypeStruct((S, 512), F32),
                   jax.ShapeDtypeStruct((S, 128), F32), jax.ShapeDtypeStruct((S, D), BF16)],
        compiler_params=_params(1),
    )(x, g1, w, rc, ra, rb)


DILATIONS = (16, 4, 1)


def _attn_valid(n):
    kd = lax.broadcasted_iota(jnp.int32, (128, 256), 1) - lax.broadcasted_iota(jnp.int32, (128, 256), 0)
    off = jnp.where(n == 0, 0, 128)
    return (kd <= off) & (kd >= off - 128)


def _attn_rows(d, r, n):
    if d == 1:
        q0 = pl.multiple_of(n * 128, 128)
        k0 = pl.multiple_of(jnp.maximum(n - 1, 0) * 128, 128)
        return pl.ds(q0, 128), pl.ds(k0, 256), _attn_valid(n)
    q0 = r + n * 128 * d
    k0 = r + jnp.maximum(n - 1, 0) * 128 * d
    return pl.ds(q0, 128, stride=d), pl.ds(k0, 256, stride=d), _attn_valid(n)


ATTN_GROUP = 4
ATTN_ITERS = S // 128 // ATTN_GROUP


def _attn_group(d, i):
    nb = S // (128 * d)
    if nb == 2:
        qi = lax.broadcasted_iota(jnp.int32, (256, 256), 0) - lax.broadcasted_iota(jnp.int32, (256, 256), 1)
        whole = [pl.ds((ATTN_GROUP // 2) * i + u, 256, stride=d) for u in range(ATTN_GROUP // 2)]
        return [(rows, rows, (qi >= 0) & (qi <= 128)) for rows in whole]
    if d == 1:
        return [_attn_rows(1, 0, i + ATTN_ITERS * u) for u in range(ATTN_GROUP)]
    return [_attn_rows(d, (i // nb) * ATTN_GROUP + u, i % nb) for u in range(ATTN_GROUP)]


def _head0(shape):
    return lax.broadcasted_iota(jnp.int32, shape, 1) < 64


def _stack_heads(t):
    h0 = _head0(t.shape)
    tb = _bf(t)
    zero = jnp.zeros_like(tb)
    return jnp.concatenate([jnp.where(h0, tb, zero), jnp.where(h0, zero, tb)], axis=0)


def _attn_fwd(qkv, shards, dtypes):
    nw = len(shards)

    def body(*refs):
        q_ref, k_ref, v_ref = refs[:3]
        ins = refs[3:3 + nw]
        o_ref, lse_ref = refs[3 + nw:5 + nw]
        outs = refs[5 + nw:5 + 2 * nw]
        m0, m1, l0, l1, acc = refs[5 + 2 * nw:10 + 2 * nw]
        bufs = refs[10 + 2 * nw:10 + 3 * nw]
        ag_start, ag_forward, ag_finish = _gather_phases(ins, outs, bufs, *refs[10 + 3 * nw:])
        hp = pl.program_id(0)
        pl.when(hp == 0)(ag_start)
        pl.when(hp == 3)(ag_forward)
        stats = (m0, m1, l0, l1, acc)

        def update(blocks, first):
            loaded = [([q_ref[rq, :], k_ref[rk, :], v_ref[rk, :]], None if first else [ref[rq, :] for ref in stats])
                      for rq, rk, _ in blocks]
            results = []
            for ((q, k, v), prev), (_, _, valid) in zip(loaded, blocks):
                head0 = _head0(q.shape)
                kb, vb = _bf(k), _bf(v)
                q = q * 0.125
                m_new, l_new, acc_new = [], [], []
                for a, qa in enumerate((_bf(jnp.where(head0, q, 0.0)), _bf(jnp.where(head0, 0.0, q)))):
                    s = jnp.where(valid, _dot_nt(qa, kb), NEG)
                    mc = jnp.max(s, axis=-1, keepdims=True)
                    m_a = jnp.broadcast_to(mc, q.shape) if first else jnp.maximum(prev[a], mc)
                    p = jnp.exp(s - jnp.tile(m_a, (1, 2)))
                    l_add = jnp.sum(p, axis=-1, keepdims=True)
                    pv = _dot(_bf(p), vb)
                    if first:
                        l_a = jnp.broadcast_to(l_add, q.shape)
                    else:
                        alpha = jnp.exp(prev[a] - m_a)
                        l_a, pv = alpha * prev[2 + a] + l_add, alpha * prev[4] + pv
                    m_new.append(m_a), l_new.append(l_a), acc_new.append(pv)
                results.append((m_new[0], m_new[1], l_new[0], l_new[1], jnp.where(head0, acc_new[0], acc_new[1])))
            for (rq, _, _), res in zip(blocks, results):
                for ref, val in zip(stats, res):
                    ref[rq, :] = val

        for d in DILATIONS:
            def step(i, carry, d=d):
                update(_attn_group(d, i), d == DILATIONS[0])
                return carry

            lax.fori_loop(0, ATTN_ITERS, step, 0)

        def fin(t, carry):
            rows = pl.ds(pl.multiple_of(t * 256, 256), 256)
            h0 = lax.broadcasted_iota(jnp.int32, (256, 128), 1) < 64
            l = jnp.where(h0, l0[rows, :], l1[rows, :])
            o_ref[rows, :] = acc[rows, :] / l
            lse_ref[rows, :] = jnp.where(h0, m0[rows, :], m1[rows, :]) + jnp.log(l)
            return carry

        lax.fori_loop(0, S // 256, fin, 0)
        pl.when(hp == 3)(ag_finish)

    col = lambda off: pl.BlockSpec((S, 128), lambda h, off=off: (0, off + h))
    res = pl.pallas_call(
        body, name="attn_fwd", grid=(4,),
        in_specs=[col(0), col(4), col(8)] + [VM] * nw,
        out_specs=[col(0), col(0)] + [ANY] * nw,
        out_shape=[jax.ShapeDtypeStruct((S, AW), F32), jax.ShapeDtypeStruct((S, AW), F32)]
        + _gather_shapes(shards, dtypes),
        scratch_shapes=[pltpu.VMEM((S, 128), F32)] * 5 + _gather_scratch(shards, dtypes),
        compiler_params=_params(1),
    )(qkv, qkv, qkv, *shards)
    return res[0], res[1], res[2:]


def _attn_bwd(qkv, o, lse, do, parts):
    nw = len(parts)

    def body(*refs):
        q_ref, k_ref, v_ref, o_ref, lse_ref, do_ref = refs[:6]
        ins = refs[6:6 + nw]
        dq_ref, dk_ref, dv_ref = refs[6 + nw:9 + nw]
        outs = refs[9 + nw:9 + 2 * nw]
        L0, L1, D0, D1 = refs[9 + 2 * nw:13 + 2 * nw]
        rs_start, rs_finish = _scatter_phases(ins, outs, *refs[13 + 2 * nw:])
        hp = pl.program_id(0)
        pl.when(hp == 0)(rs_start)
        def pre(t, carry):
            rows = pl.ds(pl.multiple_of(t * 256, 256), 256)
            h0 = lax.broadcasted_iota(jnp.int32, (256, 128), 1) < 64
            ls = lse_ref[rows, :]
            dd = do_ref[rows, :] * o_ref[rows, :]
            shp = (256, 128)
            L0[rows, :] = jnp.broadcast_to(jnp.max(jnp.where(h0, ls, NEG), axis=-1, keepdims=True), shp)
            L1[rows, :] = jnp.broadcast_to(jnp.max(jnp.where(h0, NEG, ls), axis=-1, keepdims=True), shp)
            D0[rows, :] = jnp.broadcast_to(jnp.sum(jnp.where(h0, dd, 0.0), axis=-1, keepdims=True), shp)
            D1[rows, :] = jnp.broadcast_to(jnp.sum(jnp.where(h0, 0.0, dd), axis=-1, keepdims=True), shp)
            return carry

        lax.fori_loop(0, S // 256, pre, 0)

        def update(blocks, first):
            loaded = [([q_ref[rq, :], k_ref[rk, :], v_ref[rk, :], do_ref[rq, :]],
                       [L0[rq, :], L1[rq, :], D0[rq, :], D1[rq, :]],
                       [0.0] * 3 if first else [dq_ref[rq, :], dk_ref[rk, :], dv_ref[rk, :]]) for rq, rk, _ in blocks]
            results = []
            for ((q, k, v, dout), (l0v, l1v, d0v, d1v), (dq, dk, dv)), (_, _, valid) in zip(loaded, blocks):
                valid = jnp.tile(valid, (1, 2))
                kst, vst = _stack_heads(k), _stack_heads(v)
                hk = _head0((256, 128))
                dob = _bf(dout)
                cat = lambda a, b: jnp.concatenate([jnp.tile(a, (1, 2)), jnp.tile(b, (1, 2))], axis=1)
                s = jnp.where(valid, _dot_nt(_bf(q * 0.125), kst), NEG)
                p = jnp.exp(s - cat(l0v, l1v))
                ds = _bf(p * (_dot_nt(dob, vst) - cat(d0v, d1v)) * 0.125)
                dk2 = _dot_tn(ds, _bf(q))
                dv2 = _dot_tn(_bf(p), dob)
                results.append((dq + _dot(ds, kst), dk + jnp.where(hk, dk2[0:256], dk2[256:512]),
                                dv + jnp.where(hk, dv2[0:256], dv2[256:512])))
            for (rq, rk, _), (dq, dk, dv) in zip(blocks, results):
                dq_ref[rq, :] = dq
                dk_ref[rk, :] = dk
                dv_ref[rk, :] = dv

        assert S // (128 * DILATIONS[0]) == 2
        for d in DILATIONS:
            def step(i, carry, d=d):
                update(_attn_group(d, i), d == DILATIONS[0])
                return carry

            lax.fori_loop(0, ATTN_ITERS, step, 0)
        pl.when(hp == 3)(rs_finish)

    col = lambda off: pl.BlockSpec((S, 128), lambda h, off=off: (0, off + h))
    res = pl.pallas_call(
        body, name="attn_bwd", grid=(4,),
        in_specs=[col(0), col(4), col(8), col(0), col(0), col(0)] + [ANY] * nw,
        out_specs=[col(0), col(0), col(0)] + [ANY] * nw,
        out_shape=[jax.ShapeDtypeStruct((S, AW), F32)] * 3 + [jax.ShapeDtypeStruct(a.shape, a.dtype) for a in parts],
        scratch_shapes=[pltpu.VMEM((S, 128), F32)] * 4 + _scatter_scratch(nw),
        compiler_params=_params(1),
    )(qkv, qkv, qkv, o, lse, do, *parts)
    return res[0], res[1], res[2], res[3:]


def _logsig(x):
    return jnp.minimum(x, 0.0) - jnp.log1p(jnp.exp(-jnp.abs(x)))


def _conv_taps(xp, n):
    return [xp[8:] if j == 3 else pltpu.roll(xp, 3 - j, 0)[8:] for j in range(4)]


def _conv_silu(xp, w_ref, b_ref, n):
    taps = _conv_taps(xp, n)
    c = b_ref[...] + sum(w_ref[j:j + 1, :] * taps[j] for j in range(4))
    sg = _sigmoid(c)
    return c, sg, taps


def _chunk_gates(G):
    r = lax.broadcasted_iota(jnp.int32, (LC, LC), 0)
    c = lax.broadcasted_iota(jnp.int32, (LC, LC), 1)
    tril = (c <= r).astype(F32)
    triu = (c >= r).astype(F32)
    eye = (c == r).astype(F32)
    logf = _logsig(G)
    b_col = jnp.dot(tril, logf, preferred_element_type=F32, precision=HI)
    b_row = lax.dot_general(logf, triu, (((0,), (0,)), ((), ())), preferred_element_type=F32, precision=HI)
    g_row = lax.dot_general(G, eye, (((0,), (0,)), ((), ())), preferred_element_type=F32, precision=HI)
    return b_col, b_row, g_row, tril, triu


def _colpick(X, lane):
    li = lax.broadcasted_iota(jnp.int32, X.shape, 1)
    return jnp.sum(jnp.where(li == lane, X, 0.0), axis=1, keepdims=True)


def _rowpick(XT, row):
    ri = lax.broadcasted_iota(jnp.int32, XT.shape, 0)
    return jnp.sum(jnp.where(ri == row, XT, 0.0), axis=0, keepdims=True)


def _mlstm_head(qh, kh, vh, G, b_col, b_row, g_row, h, Ch, nh, m_prev):
    bt = _colpick(b_col, 4 + h)
    i_col = _colpick(G, h)
    bs = _rowpick(b_row, 4 + h)
    i_row = _rowpick(g_row, h)
    r = lax.broadcasted_iota(jnp.int32, (LC, LC), 0)
    c = lax.broadcasted_iota(jnp.int32, (LC, LC), 1)
    log_d = jnp.where(c <= r, bt - bs + i_row, NEG)
    log_inter = bt + m_prev
    m_t = jnp.maximum(log_inter, jnp.max(log_d, axis=1, keepdims=True))
    Dm = jnp.exp(log_d - m_t)
    g = jnp.exp(log_inter - m_t)
    qb, kb, vb = _bf(qh), _bf(kh), _bf(vh)
    Am = _dot_nt(qb, kb) * Dm
    qC = _dot(qb, _bf(Ch))
    num = g * qC + _dot(_bf(Am), vb)
    qn = jnp.sum(qh * nh, axis=1, keepdims=True)
    den = g * qn + jnp.sum(Am, axis=1, keepdims=True)
    floor = jnp.exp(-m_t)
    dd = jnp.maximum(jnp.abs(den), floor)
    hh = num / dd
    lane = lax.broadcasted_iota(jnp.int32, (1, LC), 1)
    blast = jnp.sum(jnp.where(lane == LC - 1, bs, 0.0), axis=1, keepdims=True)
    log_s = blast - bt + i_col
    m_new = jnp.maximum(blast + m_prev, jnp.max(log_s, axis=0, keepdims=True))
    decay = jnp.exp(blast + m_prev - m_new)
    ws = jnp.exp(log_s - m_new)
    kw = kh * ws
    C_new = decay * Ch + _dot_tn(_bf(kw), vb)
    n_new = decay * nh + jnp.sum(kw, axis=0, keepdims=True)
    return dict(Dm=Dm, g=g, Am=Am, qC=qC, qn=qn, den=den, floor=floor, dd=dd, h=hh, decay=decay, ws=ws, kw=kw,
                C_new=C_new, n_new=n_new, m_new=m_new, qb=qb, kb=kb, vb=vb)


def _head_out(hh, mo_h, gn_h):
    r = lax.rsqrt(jnp.mean(hh * hh, axis=-1, keepdims=True) + EPS)
    hn = hh * r
    sg = _sigmoid(mo_h)
    return sg * (hn * gn_h), hn, r, sg


def _mlstm_fwd(mqk, mv, mo, gates, conv_w, conv_b, gate_b, gn, shards, dtypes):
    nblk = S // TB
    ncb = TB // LC
    nw = len(shards)

    def body(*refs):
        x_ref, v_ref, o_ref, g_ref, w_ref, b_ref, gb_ref, gn_ref = refs[:8]
        ins = refs[8:8 + nw]
        out_ref, cs_ref, ns_ref, ms_ref = refs[8 + nw:12 + nw]
        outs = refs[12 + nw:12 + 2 * nw]
        tail, Cst, nst, mst, qs, ks = refs[12 + 2 * nw:18 + 2 * nw]
        bufs = refs[18 + 2 * nw:18 + 3 * nw]
        ag_start, ag_forward, ag_finish = _gather_phases(ins, outs, bufs, *refs[18 + 3 * nw:])
        i = pl.program_id(0)
        pl.when(i == 0)(ag_start)
        pl.when(i == nblk // 2)(ag_forward)

        @pl.when(i == 0)
        def _():
            tail[...] = jnp.zeros_like(tail)
            Cst[...] = jnp.zeros_like(Cst)
            nst[...] = jnp.zeros_like(nst)
            mst[...] = jnp.zeros_like(mst)

        x = x_ref[...]
        xp = jnp.concatenate([tail[...], x], axis=0)
        tail[...] = x[TB - 8:TB, :]
        c, sg, _ = _conv_silu(xp, w_ref, b_ref, TB)
        y = c * sg
        qs[...] = y[:, 0:MW]
        ks[...] = y[:, MW:2 * MW] * (1.0 / math.sqrt(128.0))

        for cc in range(ncb):
            rows = slice(cc * LC, (cc + 1) * LC)
            G = g_ref[rows, :] + gb_ref[...]
            b_col, b_row, g_row, _, _ = _chunk_gates(G)
            cs_ref[cc] = Cst[...]
            ns_ref[cc] = nst[...]
            ms_ref[cc] = mst[...]
            for h in range(4):
                ln = slice(h * 128, (h + 1) * 128)
                m_prev = jnp.max(mst[0:1, ln], axis=1, keepdims=True)
                f = _mlstm_head(qs[rows, ln], ks[rows, ln], v_ref[rows, ln], G, b_col, b_row, g_row, h,
                                Cst[:, ln], nst[0:1, ln], m_prev)
                out, _, _, _ = _head_out(f["h"], o_ref[rows, ln], gn_ref[:, ln])
                out_ref[rows, ln] = out
                Cst[:, ln] = f["C_new"]
                nst[0:1, ln] = f["n_new"]
                mst[0:1, ln] = jnp.broadcast_to(f["m_new"], (1, 128))
        pl.when(i == nblk - 1)(ag_finish)

    row = lambda wd: pl.BlockSpec((TB, wd), lambda i: (i, 0))
    res = pl.pallas_call(
        body, name="mlstm_fwd", grid=(nblk,),
        in_specs=[row(1024), row(MW), row(MW), row(128), _cspec((4, 1024)), _cspec((1, 1024)), _cspec((1, 128)),
                  _cspec((1, MW))] + [VM] * nw,
        out_specs=[row(MW), pl.BlockSpec((ncb, 128, MW), lambda i: (i, 0, 0)),
                   pl.BlockSpec((ncb, 8, MW), lambda i: (i, 0, 0)), pl.BlockSpec((ncb, 8, MW), lambda i: (i, 0, 0))]
        + [ANY] * nw,
        out_shape=[jax.ShapeDtypeStruct((S, MW), F32), jax.ShapeDtypeStruct((S // LC, 128, MW), F32),
                   jax.ShapeDtypeStruct((S // LC, 8, MW), F32), jax.ShapeDtypeStruct((S // LC, 8, MW), F32)]
        + _gather_shapes(shards, dtypes),
        scratch_shapes=[pltpu.VMEM((8, 1024), F32), pltpu.VMEM((128, MW), F32), pltpu.VMEM((8, MW), F32),
                        pltpu.VMEM((8, MW), F32), pltpu.VMEM((TB, MW), F32), pltpu.VMEM((TB, MW), F32)]
        + _gather_scratch(shards, dtypes),
        compiler_params=_params(1),
    )(mqk, mv, mo, gates, conv_w, conv_b, gate_b, gn, *shards)
    return res[0], res[1], res[2], res[3], res[4:]


def _mlstm_bwd(mqk, mv, mo, gates, conv_w, conv_b, gate_b, gn, cs, ns, ms, dout, parts):
    nblk = S // TB
    ncb = TB // LC
    kscale = 1.0 / math.sqrt(128.0)
    nw = len(parts)

    def body(*refs):
        x_ref, xprev_ref, v_ref, o_ref, g_ref, w_ref, b_ref, gb_ref, gn_ref, cs_ref, ns_ref, ms_ref, do_ref = refs[:13]
        ins = refs[13:13 + nw]
        dx_ref, dv_ref, dmo_ref, dg_ref, dw_ref, db_ref, dgn_ref, dgb_ref = refs[13 + nw:21 + nw]
        outs = refs[21 + nw:21 + 2 * nw]
        dCst, dnst, dyhead, qs, ks, dqk = refs[21 + 2 * nw:27 + 2 * nw]
        rs_start, rs_finish = _scatter_phases(ins, outs, *refs[27 + 2 * nw:])
        i = pl.program_id(0)
        blk = nblk - 1 - i
        pl.when(i == 0)(rs_start)

        @pl.when(i == 0)
        def _():
            dCst[...] = jnp.zeros_like(dCst)
            dnst[...] = jnp.zeros_like(dnst)
            dyhead[...] = jnp.zeros_like(dyhead)
            dw_ref[...] = jnp.zeros_like(dw_ref)
            db_ref[...] = jnp.zeros_like(db_ref)
            dgn_ref[...] = jnp.zeros_like(dgn_ref)
            dgb_ref[...] = jnp.zeros_like(dgb_ref)

        x = x_ref[...]
        xprev = jnp.where(blk == 0, 0.0, xprev_ref[...])
        xp = jnp.concatenate([xprev, x], axis=0)
        c, sg, taps = _conv_silu(xp, w_ref, b_ref, TB)
        y = c * sg
        qs[...] = y[:, 0:MW]
        ks[...] = y[:, MW:2 * MW] * kscale
        lane128 = lax.broadcasted_iota(jnp.int32, (LC, 128), 1)
        rowi = lax.broadcasted_iota(jnp.int32, (LC, 1), 0)
        ones = jnp.ones((LC, 128), F32)

        for cc in reversed(range(ncb)):
            rows = slice(cc * LC, (cc + 1) * LC)
            G = g_ref[rows, :] + gb_ref[...]
            b_col, b_row, g_row, _, triu = _chunk_gates(G)
            dB = jnp.zeros((LC, 128), F32)
            dI = jnp.zeros((LC, 128), F32)
            for h in range(4):
                ln = slice(h * 128, (h + 1) * 128)
                Ch = cs_ref[cc, :, ln]
                nh = ns_ref[cc, 0:1, ln]
                m_prev = jnp.max(ms_ref[cc, 0:1, ln], axis=1, keepdims=True)
                qh, kh, vh = qs[rows, ln], ks[rows, ln], v_ref[rows, ln]
                f = _mlstm_head(qh, kh, vh, G, b_col, b_row, g_row, h, Ch, nh, m_prev)
                hh, dd, den, g, Am, Dm = f["h"], f["dd"], f["den"], f["g"], f["Am"], f["Dm"]
                qb, kb, vb = f["qb"], f["kb"], f["vb"]
                gn_h = gn_ref[:, ln]
                _, hn, r, sgo = _head_out(hh, o_ref[rows, ln], gn_h)
                do = do_ref[rows, ln]
                hm = hn * gn_h
                dmo_ref[rows, ln] = do * hm * sgo * (1.0 - sgo)
                dhm = do * sgo
                dgn_ref[:, ln] = dgn_ref[:, ln] + jnp.sum(dhm * hn, axis=0, keepdims=True)
                dhn = dhm * gn_h
                dh = r * (dhn - hn * jnp.mean(dhn * hn, axis=-1, keepdims=True))
                dnum = dh / dd
                ddd = -jnp.sum(dh * hh, axis=1, keepdims=True) / dd
                dden = jnp.where(jnp.abs(den) >= f["floor"], ddd * jnp.sign(den), 0.0)
                dnb = _bf(dnum)
                dA = _dot_nt(dnb, vb) + dden
                dv = _dot_tn(_bf(Am), dnb)
                gd = _bf(g * dnum)
                gq = g * dden
                dq = _dot_nt(gd, _bf(Ch)) + gq * nh
                dCn = dCst[:, ln]
                dnn = dnst[0:1, ln]
                dC = f["decay"] * dCn + _dot_tn(qb, gd)
                dn = f["decay"] * dnn + jnp.sum(gq * qh, axis=0, keepdims=True)
                dg = jnp.sum(dnum * f["qC"], axis=1, keepdims=True) + dden * f["qn"]
                dS = _bf(dA * Dm)
                dq = dq + _dot(dS, kb)
                dk = _dot_tn(dS, qb)
                Gm = dA * Am
                gam = dg * g
                dCb = _bf(dCn)
                E = _dot_nt(vb, dCb) + dnn
                ws = f["ws"]
                dk = dk + ws * E
                om = jnp.sum(E * kh, axis=1, keepdims=True) * ws
                dv = dv + _dot(_bf(f["kw"]), dCb)
                ddecay = (jnp.sum(jnp.sum(dCn * Ch, axis=1, keepdims=True), axis=0, keepdims=True)
                          + jnp.sum(dnn * nh, axis=1, keepdims=True))
                delta = ddecay * f["decay"]
                rows_g = jnp.sum(Gm, axis=1, keepdims=True)
                cols_g = lax.dot_general(Gm, ones, (((0,), (0,)), ((), ())), preferred_element_type=F32, precision=HI)
                last = jnp.where(rowi == LC - 1, jnp.sum(om, axis=0, keepdims=True) + delta, 0.0)
                db = rows_g + gam - om + last - cols_g
                di = cols_g + om
                dB = dB + jnp.where(lane128 == 4 + h, db, 0.0)
                dI = dI + jnp.where(lane128 == h, di, 0.0)
                dCst[:, ln] = dC
                dnst[0:1, ln] = dn
                dqk[rows, ln] = dq
                dqk[rows, MW + h * 128:MW + (h + 1) * 128] = dk * kscale
                dv_ref[rows, ln] = dv
            dlogf = jnp.dot(triu, dB, preferred_element_type=F32, precision=HI)
            dG = dI + dlogf * _sigmoid(-G)
            dG = jnp.where(lane128 < 8, dG, 0.0)
            dg_ref[rows, :] = dG
            dgb_ref[...] = dgb_ref[...] + jnp.sum(dG, axis=0, keepdims=True)

        dy = dqk[...] * (sg * (1.0 + c * (1.0 - sg)))
        db_ref[...] = db_ref[...] + jnp.sum(dy, axis=0, keepdims=True)
        for j in range(4):
            dw_ref[j:j + 1, :] = dw_ref[j:j + 1, :] + jnp.sum(dy * taps[j], axis=0, keepdims=True)
        dyp = jnp.concatenate([dy, dyhead[...]], axis=0)
        dx = w_ref[3:4, :] * dy
        for j in range(3):
            dx = dx + w_ref[j:j + 1, :] * pltpu.roll(dyp, TB + 8 - (3 - j), 0)[0:TB]
        dx_ref[...] = dx
        dyhead[...] = dy[0:8, :]
        pl.when(i == nblk - 1)(rs_finish)

    rrow = lambda wd: pl.BlockSpec((TB, wd), lambda i: (nblk - 1 - i, 0))
    st = lambda r: pl.BlockSpec((ncb, r, MW), lambda i: (nblk - 1 - i, 0, 0))
    prev8 = pl.BlockSpec((8, 1024), lambda i: (jnp.maximum((nblk - 1 - i) * (TB // 8) - 1, 0), 0))
    res = pl.pallas_call(
        body, name="mlstm_bwd", grid=(nblk,),
        in_specs=[rrow(1024), prev8, rrow(MW), rrow(MW), rrow(128), _cspec((4, 1024)), _cspec((1, 1024)),
                  _cspec((1, 128)), _cspec((1, MW)), st(128), st(8), st(8), rrow(MW)] + [ANY] * nw,
        out_specs=[rrow(1024), rrow(MW), rrow(MW), rrow(128),
                   pl.BlockSpec((4, 1024), lambda i: (0, 0)), pl.BlockSpec((1, 1024), lambda i: (0, 0)),
                   pl.BlockSpec((1, MW), lambda i: (0, 0)), pl.BlockSpec((1, 128), lambda i: (0, 0))] + [ANY] * nw,
        out_shape=[jax.ShapeDtypeStruct((S, 1024), F32), jax.ShapeDtypeStruct((S, MW), F32),
                   jax.ShapeDtypeStruct((S, MW), F32), jax.ShapeDtypeStruct((S, 128), F32),
                   jax.ShapeDtypeStruct((4, 1024), F32), jax.ShapeDtypeStruct((1, 1024), F32),
                   jax.ShapeDtypeStruct((1, MW), F32), jax.ShapeDtypeStruct((1, 128), F32)]
        + [jax.ShapeDtypeStruct(a.shape, a.dtype) for a in parts],
        scratch_shapes=[pltpu.VMEM((128, MW), F32), pltpu.VMEM((8, MW), F32), pltpu.VMEM((8, 1024), F32),
                        pltpu.VMEM((TB, MW), F32), pltpu.VMEM((TB, MW), F32), pltpu.VMEM((TB, 1024), F32)]
        + _scatter_scratch(nw),
        compiler_params=_params(1),
    )(mqk, mqk, mv, mo, gates, conv_w, conv_b, gate_b, gn, cs, ns, ms, dout, *parts)
    return res[:8], res[8:]


def _out_proj(x, attn, ml, w, g):
    tm = TM

    def body(x_ref, a_ref, m_ref, w_ref, g_ref, h_ref, u_ref):
        h1 = x_ref[...] + _dot(_bf(a_ref[...]), w_ref[0:AW, :]) + _dot(_bf(m_ref[...]), w_ref[AW:D, :])
        h_ref[...] = h1
        n, _ = _rms(h1)
        u_ref[...] = _bf(n * g_ref[...])

    row = lambda wd: pl.BlockSpec((tm, wd), lambda i: (i, 0))
    return pl.pallas_call(
        body, name="out_proj", grid=(S // tm,),
        in_specs=[row(D), row(AW), row(MW), _cspec((D, D)), _cspec((1, D))],
        out_specs=[row(D), row(D)],
        out_shape=[jax.ShapeDtypeStruct((S, D), F32), jax.ShapeDtypeStruct((S, D), BF16)],
        compiler_params=_params(1),
    )(x, attn, ml, w, g)


def _mlp_fwd(h1, u2, w_up, w_down):
    tm = TM

    def body(h_ref, u_ref, wu_ref, wd_ref, a_ref, o_ref):
        u = u_ref[...]
        acc = h_ref[...]
        for c in range(NDEV):
            cols = slice(c * 512, (c + 1) * 512)
            a = _dot(u, wu_ref[c])
            a_ref[:, cols] = _bf(a)
            r = jnp.maximum(a, 0.0)
            acc = acc + _dot(_bf(r * r), wd_ref[cols, :])
        o_ref[...] = acc

    row = lambda wd: pl.BlockSpec((tm, wd), lambda i: (i, 0))
    return pl.pallas_call(
        body, name="mlp_fwd", grid=(S // tm,),
        in_specs=[row(D), row(D), _cspec((NDEV, D, DFF // NDEV)), _cspec((DFF, D))],
        out_specs=[row(DFF), row(D)],
        out_shape=[jax.ShapeDtypeStruct((S, DFF), BF16), jax.ShapeDtypeStruct((S, D), F32)],
        compiler_params=_params(1),
    )(h1, u2, w_up, w_down)


def _ple_loss(h2, p, target, w_pg, w_ple, g_ple, g_fin):
    tm = TM

    def body(h_ref, p_ref, t_ref, wg_ref, wp_ref, gp_ref, gf_ref,
             dh_ref, dwg_ref, dwp_ref, dgp_ref, dgf_ref, loss_ref, acc_g, acc_p):
        i = pl.program_id(0)

        @pl.when(i == 0)
        def _():
            acc_g[...] = jnp.zeros_like(acc_g)
            acc_p[...] = jnp.zeros_like(acc_p)
            dgp_ref[...] = jnp.zeros_like(dgp_ref)
            dgf_ref[...] = jnp.zeros_like(dgf_ref)
            loss_ref[...] = jnp.zeros_like(loss_ref)

        h2v = h_ref[...]
        n2, rs2 = _rms(h2v)
        u3 = _bf(n2 * gp_ref[...])
        gt = _sigmoid(_dot(u3, wg_ref[...]))
        pb = _bf(p_ref[...])
        e = jnp.concatenate([_dot(pb, wp_ref[j]) for j in range(NDEV)], axis=1)
        h3 = h2v + gt * e
        n3, rs3 = _rms(h3)
        err = n3 * gf_ref[...] - t_ref[...]
        loss_ref[...] = loss_ref[...] + 0.5 / D * jnp.sum(jnp.sum(err * err, axis=1, keepdims=True), axis=0, keepdims=True)
        dy = err * (1.0 / D)
        dgf_ref[...] = dgf_ref[...] + jnp.sum(dy * n3, axis=0, keepdims=True)
        dh3 = _rms_bwd(dy, n3, rs3, gf_ref[...])
        de = _bf(dh3 * gt)
        dz = _bf(dh3 * e * gt * (1.0 - gt))
        acc_p[...] = acc_p[...] + _dot_tn(pb, de)
        acc_g[...] = acc_g[...] + _dot_tn(u3, dz)
        du3 = _dot_nt(dz, wg_ref[...])
        dgp_ref[...] = dgp_ref[...] + jnp.sum(du3 * n2, axis=0, keepdims=True)
        dh_ref[...] = dh3 + _rms_bwd(du3, n2, rs2, gp_ref[...])

        @pl.when(i == S // tm - 1)
        def _():
            dwg_ref[...] = _bf(acc_g[...])
            for j in range(NDEV):
                dwp_ref[j] = _bf(acc_p[:, j * 128:(j + 1) * 128])

    row = lambda wd: pl.BlockSpec((tm, wd), lambda i: (i, 0))
    whole = lambda shp: pl.BlockSpec(shp, lambda i: (0,) * len(shp))
    return pl.pallas_call(
        body, name="ple_loss", grid=(S // tm,),
        in_specs=[row(D), row(PLE), row(D), _cspec((D, D)), _cspec((NDEV, PLE, 128)), _cspec((1, D)), _cspec((1, D))],
        out_specs=[row(D), whole((D, D)), whole((NDEV, PLE, 128)), whole((1, D)), whole((1, D)), whole((1, 1))],
        out_shape=[jax.ShapeDtypeStruct((S, D), F32), jax.ShapeDtypeStruct((D, D), BF16),
                   jax.ShapeDtypeStruct((NDEV, PLE, 128), BF16), jax.ShapeDtypeStruct((1, D), F32),
                   jax.ShapeDtypeStruct((1, D), F32), jax.ShapeDtypeStruct((1, 1), F32)],
        scratch_shapes=[pltpu.VMEM((D, D), F32), pltpu.VMEM((PLE, D), F32)],
        compiler_params=_params(1),
    )(h2, p, target, w_pg, w_ple, g_ple, g_fin)


def _mlp_bwd(dh2, a, h1, g, w_up, w_down):
    tm = TM

    def body(d_ref, a_ref, h_ref, g_ref, wu_ref, wd_ref, da_ref, dh1_ref, dg_ref):
        @pl.when(pl.program_id(0) == 0)
        def _():
            dg_ref[...] = jnp.zeros_like(dg_ref)

        dh2v = d_ref[...]
        db = _bf(dh2v)
        du = jnp.zeros((tm, D), F32)
        for c in range(NDEV):
            cols = slice(c * 512, (c + 1) * 512)
            dr = _dot_nt(db, wd_ref[cols, :])
            da = _bf(dr * (2.0 * jnp.maximum(a_ref[:, cols], 0.0)))
            da_ref[:, cols] = da
            du = du + _dot_nt(da, wu_ref[c])
        n, rs = _rms(h_ref[...])
        dg_ref[...] = dg_ref[...] + jnp.sum(du * n, axis=0, keepdims=True)
        dh1_ref[...] = dh2v + _rms_bwd(du, n, rs, g_ref[...])

    row = lambda wd: pl.BlockSpec((tm, wd), lambda i: (i, 0))
    return pl.pallas_call(
        body, name="mlp_bwd", grid=(S // tm,),
        in_specs=[row(D), row(DFF), row(D), _cspec((1, D)), _cspec((NDEV, D, DFF // NDEV)), _cspec((DFF, D))],
        out_specs=[row(DFF), row(D), pl.BlockSpec((1, D), lambda i: (0, 0))],
        out_shape=[jax.ShapeDtypeStruct((S, DFF), BF16), jax.ShapeDtypeStruct((S, D), F32),
                   jax.ShapeDtypeStruct((1, D), F32)],
        compiler_params=_params(1),
    )(dh2, a, h1, g, w_up, w_down)


def _out_proj_bwd(dh1, attn, ml, w):
    tm = TM

    def body(d_ref, a_ref, m_ref, w_ref, da_ref, dm_ref, dw_ref, acc):
        i = pl.program_id(0)

        @pl.when(i == 0)
        def _():
            acc[...] = jnp.zeros_like(acc)

        db = _bf(d_ref[...])
        dmix = _dot_nt(db, w_ref[...])
        da_ref[...] = dmix[:, 0:AW]
        dm_ref[...] = dmix[:, AW:D]
        acc[0:AW, :] = acc[0:AW, :] + _dot_tn(_bf(a_ref[...]), db)
        acc[AW:D, :] = acc[AW:D, :] + _dot_tn(_bf(m_ref[...]), db)

        @pl.when(i == S // tm - 1)
        def _():
            dw_ref[...] = _bf(acc[...])

    row = lambda wd: pl.BlockSpec((tm, wd), lambda i: (i, 0))
    return pl.pallas_call(
        body, name="out_proj_bwd", grid=(S // tm,),
        in_specs=[row(D), row(AW), row(MW), _cspec((D, D))],
        out_specs=[row(AW), row(MW), pl.BlockSpec((D, D), lambda i: (0, 0))],
        out_shape=[jax.ShapeDtypeStruct((S, AW), F32), jax.ShapeDtypeStruct((S, MW), F32),
                   jax.ShapeDtypeStruct((D, D), BF16)],
        scratch_shapes=[pltpu.VMEM((D, D), F32)],
        compiler_params=_params(1),
    )(dh1, attn, ml, w)


def _in_proj_bwd(dq, dk, dv, dmqk, dmv, dmo, dgt, dh1, x, g1, w, rc, ra, rb):
    tm = TM

    def body(dq_ref, dk_ref, dv_ref, dmqk_ref, dmv_ref, dmo_ref, dgt_ref, dh_ref, x_ref, g_ref, w_ref,
             rc_ref, ra_ref, rb_ref, dp_ref, dx_ref, dg_ref):
        @pl.when(pl.program_id(0) == 0)
        def _():
            dg_ref[...] = jnp.zeros_like(dg_ref)

        c, a, b = rc_ref[...], ra_ref[...], rb_ref[...]
        for half, ref in enumerate((dq_ref, dk_ref)):
            for t in range(4):
                lo = half * 512 + t * 128
                dp_ref[:, lo:lo + 128] = _bf(_rope_bwd(ref[:, t * 128:(t + 1) * 128], c, a, b))
        dp_ref[:, 1024:1536] = _bf(dv_ref[...])
        dp_ref[:, 1536:2560] = _bf(dmqk_ref[...])
        dp_ref[:, 2560:3072] = _bf(dmv_ref[...])
        dp_ref[:, 3072:3584] = _bf(dmo_ref[...])
        dp_ref[:, 3584:3712] = _bf(dgt_ref[...])
        dp_ref[:, 3712:PW] = jnp.zeros((tm, PW - 3712), BF16)
        du = jnp.zeros((tm, D), F32)
        for s in range(PW // 768):
            cols = slice(s * 768, (s + 1) * 768)
            du = du + _dot_nt(dp_ref[:, cols], w_ref[:, cols])
        n, rs = _rms(x_ref[...])
        dg_ref[...] = dg_ref[...] + jnp.sum(du * n, axis=0, keepdims=True)
        dx_ref[...] = dh_ref[...] + _rms_bwd(du, n, rs, g_ref[...])

    row = lambda wd: pl.BlockSpec((tm, wd), lambda i: (i, 0))
    return pl.pallas_call(
        body, name="in_proj_bwd", grid=(S // tm,),
        in_specs=[row(AW), row(AW), row(AW), row(1024), row(MW), row(MW), row(128), row(D), row(D), _cspec((1, D)),
                  _cspec((D, PW)), row(128), row(128), row(128)],
        out_specs=[row(PW), row(D), pl.BlockSpec((1, D), lambda i: (0, 0))],
        out_shape=[jax.ShapeDtypeStruct((S, PW), BF16), jax.ShapeDtypeStruct((S, D), F32),
                   jax.ShapeDtypeStruct((1, D), F32)],
        compiler_params=_params(1),
    )(dq, dk, dv, dmqk, dmv, dmo, dgt, dh1, x, g1, w, rc, ra, rb)


SMALL_ROWS = 96


def _small_phases(ins, out_ref, pack, rbuf, send_sems, recv_sems):
    x, y, c = _place()
    me = _dev_index(x, y, c)

    def copies():
        out = []
        for k, (dx, dy, dc) in enumerate(FLIPS):
            peer = ((x + dx) % 2, (y + dy) % 2, (c + dc) % 2)
            out.append(pltpu.make_async_remote_copy(
                src_ref=pack, dst_ref=rbuf.at[me], send_sem=send_sems.at[k], recv_sem=recv_sems.at[k],
                device_id=peer, device_id_type=MESH))
        return out

    def start():
        pack[...] = jnp.zeros_like(pack)
        for i, ref in enumerate(ins):
            pack[8 * i:8 * i + 1, 0:ref.shape[1]] = ref[...]
        rbuf[me] = pack[...]
        for cp in copies():
            cp.start()

    def finish():
        for cp in copies():
            cp.wait()
        tot = rbuf[0]
        for j in range(1, NDEV):
            tot = tot + rbuf[j]
        out_ref[...] = tot

    return start, finish


def _wgrad(name, A, B, a_fn, b_fn, tk, tn, out_shape, out_spec, ts=512, split=None, small=()):
    K, N = A.shape[1], B.shape[1]
    nrt = S // ts
    nc = next(c for c in (1024, 1280, tn) if tn % c == 0)
    ns = len(small)
    grid = (N // tn, K // tk, nrt)

    def body(*refs):
        a_ref, b_ref = refs[:2]
        o_ref = refs[2 + ns]
        acc = refs[3 + ns + bool(ns)]
        r = pl.program_id(2)
        if ns:
            step = (pl.program_id(0) * grid[1] + pl.program_id(1)) * nrt + r
            sm_start, sm_finish = _small_phases(refs[2:2 + ns], refs[3 + ns], *refs[4 + ns + 1:])
            pl.when(step == 0)(sm_start)

        @pl.when(r == 0)
        def _():
            acc[...] = jnp.zeros_like(acc)

        at = a_fn(a_ref[...]).T
        for c in range(tn // nc):
            cols = slice(c * nc, (c + 1) * nc)
            acc[:, cols] = acc[:, cols] + _dot(at, b_fn(b_ref[:, cols]))

        @pl.when(r == nrt - 1)
        def _():
            if split is None:
                o_ref[...] = _bf(acc[...])
            else:
                for j in range(NDEV):
                    o_ref[j] = _bf(acc[:, split * j:split * (j + 1)])

        if ns:
            pl.when(step == grid[0] * grid[1] * nrt - 1)(sm_finish)

    in_specs = [pl.BlockSpec((ts, tk), lambda n, k, r: (r, k)), pl.BlockSpec((ts, tn), lambda n, k, r: (r, n))]
    scratch = [pltpu.VMEM((tk, tn), F32)]
    if not ns:
        return pl.pallas_call(
            body, name=name, grid=grid, in_specs=in_specs, out_specs=out_spec,
            out_shape=jax.ShapeDtypeStruct(out_shape, BF16), scratch_shapes=scratch, compiler_params=_params(3),
        )(A, B)
    return pl.pallas_call(
        body, name=name, grid=grid, in_specs=in_specs + [VM] * ns, out_specs=[out_spec, VM],
        out_shape=[jax.ShapeDtypeStruct(out_shape, BF16), jax.ShapeDtypeStruct((SMALL_ROWS, 1024), F32)],
        scratch_shapes=scratch + [pltpu.VMEM((SMALL_ROWS, 1024), F32), pltpu.VMEM((NDEV, SMALL_ROWS, 1024), F32),
                                  pltpu.SemaphoreType.DMA((7,)), pltpu.SemaphoreType.DMA((7,))],
        compiler_params=_params(3),
    )(A, B, *small)


def _relu2_bf(a):
    r = jnp.maximum(a.astype(F32), 0.0)
    return _bf(r * r)


def _ident(a):
    return a


def _step(x, p, target, g1, conv_b, gate_b, gn, g_mlp, g_ple, g_fin, sh):
    g_in, g_conv = _gather_weights([sh["w_in"], sh["conv_w"]], [BF16, F32])
    conv_w = g_conv.transpose(1, 0, 2).reshape(4, 1024)
    rc, ra, rb = _rope_tables()
    w_in_p = _join_w_in(g_in)
    qkv, mqk, mv, mo, gates, u1 = _in_proj(x, g1, w_in_p, rc, ra, rb)
    attn, lse, (w_up8, w_down8) = _attn_fwd(qkv, [sh["w_up"], sh["w_down"]], [BF16] * 2)
    ml, cs, ns, ms, (w_out8, w_pg8, w_ple8) = _mlstm_fwd(
        mqk, mv, mo, gates, conv_w, conv_b, gate_b, gn, [sh["w_out"], sh["w_ple_gate"], sh["w_ple"]], [BF16] * 3)
    w_out, w_down, w_pg = w_out8.reshape(D, D), w_down8.reshape(DFF, D), w_pg8.reshape(D, D)
    h1, u2 = _out_proj(x, attn, ml, w_out, g_mlp)
    a, h2 = _mlp_fwd(h1, u2, w_up8, w_down)
    dh2, dw_pg, dw_ple8, dg_ple, dg_fin, loss = _ple_loss(h2, p, target, w_pg, w_ple8, g_ple, g_fin)
    da, dh1, dg_mlp = _mlp_bwd(dh2, a, h1, g_mlp, w_up8, w_down)
    dw_up8 = _wgrad("wgrad_up", u2, da, _ident, _ident, D, DFF, (NDEV, D, DFF // NDEV),
                    pl.BlockSpec((NDEV, D, DFF // NDEV), lambda n, k, r: (0, 0, 0)), split=DFF // NDEV)
    dw_down = _wgrad("wgrad_down", a, dh2, _relu2_bf, _bf, 1024, 1024, (DFF, D),
                     pl.BlockSpec((1024, 1024), lambda n, k, r: (k, n)))
    d_attn, d_ml, dw_out = _out_proj_bwd(dh1, attn, ml, w_out)
    (dmqk, dmv, dmo, dgt, dconv_w, dconv_b, dgn, dgate_b), (r_out, r_up, r_pg, r_ple) = _mlstm_bwd(
        mqk, mv, mo, gates, conv_w, conv_b, gate_b, gn, cs, ns, ms, d_ml,
        [dw_out.reshape(NDEV, D // NDEV, D), dw_up8, dw_pg.reshape(NDEV, D // NDEV, D), dw_ple8])
    dq, dk, dv, (r_down,) = _attn_bwd(qkv, attn, lse, d_attn, [dw_down.reshape(NDEV, DFF // NDEV, D)])
    dproj, dx, dg1 = _in_proj_bwd(dq, dk, dv, dmqk, dmv, dmo, dgt, dh1, x, g1, w_in_p, rc, ra, rb)
    small = dict(norm_mix_g=dg1, conv_b=dconv_b, gate_b=dgate_b, mlstm_norm_g=dgn, norm_mlp_g=dg_mlp,
                 norm_ple_g=dg_ple, final_norm_g=dg_fin)
    dw_in8, total = _wgrad("wgrad_in", u1, dproj, _ident, _ident, D, PW, (NDEV, D, IN_W // NDEV),
                           pl.BlockSpec((NDEV, D, IN_W // NDEV), lambda n, k, r: (0, 0, 0)), split=IN_W // NDEV,
                           small=[small[n] for n in SMALL] + [loss] + [dconv_w[j:j + 1] for j in range(4)])
    recv = dict(w_in=_scatter_two_level(dw_in8), w_out=r_out, w_up=r_up, w_down=r_down, w_ple_gate=r_pg, w_ple=r_ple)
    return dx, recv, total


def _gather_weights(shards, dtypes):
    nw = len(shards)

    def body(*refs):
        start, forward, finish = _gather_phases(refs[:nw], refs[nw:2 * nw], refs[2 * nw:3 * nw], *refs[3 * nw:])
        start()
        forward()
        finish()

    return pl.pallas_call(
        body, name="gather_weights",
        in_specs=[VM] * nw, out_specs=[ANY] * nw,
        out_shape=_gather_shapes(shards, dtypes),
        scratch_shapes=_gather_scratch(shards, dtypes),
        compiler_params=_params(),
    )(*shards)


CHIP_FLIPS = [(0, 0), (0, 1), (1, 0), (1, 1)]


def _scatter_two_level(part):
    shard = part.shape[1:]
    nc = len(CHIP_FLIPS)

    def body(in_ref, out_ref, mine_v, sib_v, psum_v, loc_sems, d2d_send, d2d_recv, ici_send, ici_recv, own_sem):
        x, y, c = _place()
        chips = [((x + dx) % 2, (y + dy) % 2) for dx, dy in CHIP_FLIPS]
        local, to_sib = [], []
        for k, (px, py) in enumerate(chips):
            local.append(pltpu.make_async_copy(in_ref.at[_dev_index(px, py, c)], mine_v.at[k], loc_sems.at[k]))
            to_sib.append(pltpu.make_async_remote_copy(
                src_ref=in_ref.at[_dev_index(px, py, 1 - c)], dst_ref=sib_v.at[k], send_sem=d2d_send.at[k],
                recv_sem=d2d_recv.at[k], device_id=(x, y, 1 - c), device_id_type=MESH))
        for cp in to_sib + local:
            cp.start()

        def over_ici(k):
            return pltpu.make_async_remote_copy(
                src_ref=psum_v.at[k], dst_ref=out_ref.at[k], send_sem=ici_send.at[k - 1], recv_sem=ici_recv.at[k - 1],
                device_id=(*chips[k], c), device_id_type=MESH)

        own = pltpu.make_async_copy(psum_v.at[0], out_ref.at[0], own_sem)
        for k in (1, 2, 3, 0):
            local[k].wait()
            to_sib[k].wait_recv()
            psum_v[k] = _bf(mine_v[k].astype(F32) + sib_v[k].astype(F32))
            if k:
                over_ici(k).start()
            else:
                own.start()
        for k in range(1, nc):
            over_ici(k).wait()
        for cp in to_sib:
            cp.wait_send()
        own.wait()

    return pl.pallas_call(
        body, name="scatter_grads",
        in_specs=[ANY], out_specs=ANY,
        out_shape=jax.ShapeDtypeStruct((nc, *shard), part.dtype),
        scratch_shapes=[pltpu.VMEM((nc, *shard), part.dtype)] * 3
        + [pltpu.SemaphoreType.DMA((nc,))] * 3 + [pltpu.SemaphoreType.DMA((nc - 1,))] * 2 + [pltpu.SemaphoreType.DMA],
        compiler_params=_params(),
    )(part)


def _adamw(name, gparts, w, m, v, tr):
    P, R, C = gparts.shape
    c1 = 1.0 - ADAM_B1 ** ADAM_STEP
    c2 = 1.0 - ADAM_B2 ** ADAM_STEP

    def body(g_ref, w_ref, m_ref, v_ref, go_ref, d_ref, mo_ref, vo_ref):
        g = g_ref[0].astype(F32)
        for j in range(1, P):
            g = g + g_ref[j].astype(F32)
        m2 = ADAM_B1 * m_ref[...] + (1.0 - ADAM_B1) * g
        v2 = ADAM_B2 * v_ref[...] + (1.0 - ADAM_B2) * (g * g)
        go_ref[...] = g
        mo_ref[...] = m2
        vo_ref[...] = v2
        d_ref[...] = -ADAM_LR * ((m2 / c1) / (jnp.sqrt(v2 / c2) + ADAM_EPS) + ADAM_WD * w_ref[...])

    row = pl.BlockSpec((tr, C), lambda i: (i, 0))
    return pl.pallas_call(
        body, name=name, grid=(R // tr,),
        in_specs=[pl.BlockSpec((P, tr, C), lambda i: (0, i, 0)), row, row, row],
        out_specs=[row] * 4,
        out_shape=[jax.ShapeDtypeStruct((R, C), F32)] * 4,
        compiler_params=_params(1),
    )(gparts, w, m, v)


SMALL = ("norm_mix_g", "conv_b", "gate_b", "mlstm_norm_g", "norm_mlp_g", "norm_ple_g", "final_norm_g")


def _pack_small(vals):
    return jnp.concatenate([jnp.pad(a, ((0, 7), (0, 1024 - a.shape[1]))) for a in vals], axis=0)


def kernel(x, p, norm_mix_g, w_in, conv_w, conv_b, gate_b, mlstm_norm_g, w_out, norm_mlp_g, w_up, w_down, norm_ple_g, w_ple_gate, w_ple, final_norm_g, loss_target, m_norm_mix_g, m_w_in, m_conv_w, m_conv_b, m_gate_b, m_mlstm_norm_g, m_w_out, m_norm_mlp_g, m_w_up, m_w_down, m_norm_ple_g, m_w_ple_gate, m_w_ple, m_final_norm_g, v_norm_mix_g, v_w_in, v_conv_w, v_conv_b, v_gate_b, v_mlstm_norm_g, v_w_out, v_norm_mlp_g, v_w_up, v_w_down, v_norm_ple_g, v_w_ple_gate, v_w_ple, v_final_norm_g):
    big_names = ("w_in", "conv_w", "w_out", "w_up", "w_down", "w_ple_gate", "w_ple")
    wts = dict(w_in=w_in, conv_w=conv_w, w_out=w_out, w_up=w_up, w_down=w_down, w_ple_gate=w_ple_gate, w_ple=w_ple)
    mom = dict(w_in=m_w_in, conv_w=m_conv_w, w_out=m_w_out, w_up=m_w_up, w_down=m_w_down, w_ple_gate=m_w_ple_gate,
               w_ple=m_w_ple)
    var = dict(w_in=v_w_in, conv_w=v_conv_w, w_out=v_w_out, w_up=v_w_up, w_down=v_w_down, w_ple_gate=v_w_ple_gate,
               w_ple=v_w_ple)
    sq = lambda a: a.reshape(a.shape[1:])
    fin = final_norm_g.reshape(1, D)
    dx, recv, total = _step(
        x[0], p[0, 0], loss_target[0], norm_mix_g, conv_b, jnp.pad(gate_b, ((0, 0), (0, 120))), mlstm_norm_g,
        norm_mlp_g, norm_ple_g, fin, {n: sq(wts[n]) for n in big_names})

    nrow = 8 * len(SMALL)
    me = _dev_index(*_place())
    conv_rows = total[nrow + 8:nrow + 40:8]
    recv["conv_w"] = lax.dynamic_slice_in_dim(conv_rows, me * 128, 128, axis=1).reshape(1, 4, 128)
    out = {}
    for n, tr in zip(big_names, (256, 4, 128, 256, 256, 128, 256)):
        res = _adamw("adamw_" + n, recv[n], sq(wts[n]), sq(mom[n]), sq(var[n]), tr)
        out[n] = [t.reshape(wts[n].shape) for t in res]
    sw = dict(norm_mix_g=norm_mix_g, conv_b=conv_b, gate_b=gate_b, mlstm_norm_g=mlstm_norm_g, norm_mlp_g=norm_mlp_g,
              norm_ple_g=norm_ple_g, final_norm_g=fin)
    sm = dict(norm_mix_g=m_norm_mix_g, conv_b=m_conv_b, gate_b=m_gate_b, mlstm_norm_g=m_mlstm_norm_g,
              norm_mlp_g=m_norm_mlp_g, norm_ple_g=m_norm_ple_g, final_norm_g=m_final_norm_g.reshape(1, D))
    sv = dict(norm_mix_g=v_norm_mix_g, conv_b=v_conv_b, gate_b=v_gate_b, mlstm_norm_g=v_mlstm_norm_g,
              norm_mlp_g=v_norm_mlp_g, norm_ple_g=v_norm_ple_g, final_norm_g=v_final_norm_g.reshape(1, D))
    res = _adamw("adamw_small", total[0:nrow].reshape(1, nrow, 1024), _pack_small([sw[n] for n in SMALL]),
                 _pack_small([sm[n] for n in SMALL]), _pack_small([sv[n] for n in SMALL]), nrow)
    for i, n in enumerate(SMALL):
        shp = final_norm_g.shape if n == "final_norm_g" else sw[n].shape
        out[n] = [t[8 * i, 0:sw[n].shape[1]].reshape(shp) for t in res]
    order = ("norm_mix_g", "w_in", "conv_w", "conv_b", "gate_b", "mlstm_norm_g", "w_out", "norm_mlp_g", "w_up", "w_down",
             "norm_ple_g", "w_ple_gate", "w_ple", "final_norm_g")
    loss_all = total[nrow, 0]
    return (loss_all, dx[None], *[out[n][0] for n in order], *[out[n][1] for n in order],
            *[out[n][2] for n in order], *[out[n][3] for n in order])
```

```python
import functools
import math

import jax
import jax.numpy as jnp
from jax import lax
from jax.experimental import pallas as pl
from jax.experimental.pallas import tpu as pltpu

F32, BF16 = jnp.float32, jnp.bfloat16
S = 4096
D = 1024
AW = 512
MW = 512
DFF = 4096
PLE = 256
IN_W = 3592
PW = 3840
NDEV = 8
EPS = 1e-6
NEG = -1e30
LC = 128
TB = 256
ROPE_THETA = 500000.0
VMEM_LIMIT = 56 * 1024 * 1024
HI = lax.Precision.HIGHEST

ADAM_LR, ADAM_B1, ADAM_B2, ADAM_EPS, ADAM_WD, ADAM_STEP = 0.001, 0.9, 0.999, 1e-08, 0.01, 10


def _params(n_grid=0, **kw):
    sem = dict(dimension_semantics=("arbitrary",) * n_grid) if n_grid else {}
    return pltpu.CompilerParams(vmem_limit_bytes=VMEM_LIMIT, **sem, **kw)


def _cspec(shape):
    nd = len(shape)
    return pl.BlockSpec(shape, lambda *_: (0,) * nd, pipeline_mode=pl.Buffered(1))


def _dot(a, b):
    return jnp.dot(a, b, preferred_element_type=F32)


def _dot_nt(a, b):
    return lax.dot_general(a, b, (((1,), (1,)), ((), ())), preferred_element_type=F32)


def _dot_tn(a, b):
    return lax.dot_general(a, b, (((0,), (0,)), ((), ())), preferred_element_type=F32)


def _bf(x):
    return x.astype(BF16)


def _rms(x):
    rs = lax.rsqrt(jnp.mean(x * x, axis=-1, keepdims=True) + EPS)
    return x * rs, rs


def _rms_bwd(du, n, rs, g):
    dn = du * g
    return rs * (dn - n * jnp.mean(dn * n, axis=-1, keepdims=True))


def _sigmoid(x):
    return 1.0 / (1.0 + jnp.exp(-x))


def _rope_tables():
    j = lax.broadcasted_iota(jnp.int32, (S, 128), 1) % 64
    pos = lax.broadcasted_iota(jnp.int32, (S, 128), 0).astype(F32)
    inv_freq = jnp.power(ROPE_THETA, -(j % 8).astype(F32) / 8.0)
    ang = pos * inv_freq
    cos, sin = jnp.cos(ang), jnp.sin(ang)
    c = jnp.where(j < 16, cos, 1.0)
    a = jnp.where(j < 8, -sin, 0.0)
    b = jnp.where((j >= 8) & (j < 16), sin, 0.0)
    return c, a, b


def _rope(blk, c, a, b):
    return blk * c + pltpu.roll(blk, 120, 1) * a + pltpu.roll(blk, 8, 1) * b


def _rope_bwd(d, c, a, b):
    return d * c + pltpu.roll(d * a, 8, 1) + pltpu.roll(d * b, 120, 1)


MESH = pl.DeviceIdType.MESH
ANY = pl.BlockSpec(memory_space=pl.ANY)
VM = pl.BlockSpec(memory_space=pltpu.VMEM)
FLIPS = [(dx, dy, dc) for dx in (0, 1) for dy in (0, 1) for dc in (0, 1)][1:]


def _place():
    return lax.axis_index("x"), lax.axis_index("y"), lax.axis_index("c")


def _dev_index(px, py, pc):
    return 4 * px + 2 * py + pc


def _gather_phases(ins, outs, bufs, send_sems=None, recv_sems=None, local_sems=None):
    nw = len(ins)
    if nw == 0:
        return (lambda: None,) * 3
    x, y, c = _place()
    me, sib = (x, y, c), (x, y, 1 - c)
    chips = [(1 - x, y), (x, 1 - y), (1 - x, 1 - y)]

    def copy(w, k, block, to, from_buf=False):
        dst = outs[w].at[_dev_index(*block)]
        return pltpu.make_async_remote_copy(
            src_ref=bufs[w] if from_buf else dst, dst_ref=dst, send_sem=send_sems.at[w, k],
            recv_sem=recv_sems.at[w, k], device_id=to, device_id_type=MESH)

    def mine(w):
        return pltpu.make_async_copy(bufs[w], outs[w].at[_dev_index(*me)], local_sems.at[w])

    def first(w):
        return [copy(w, 0, me, sib, True)] + [copy(w, 1 + j, me, (*chip, c), True) for j, chip in enumerate(chips)]

    def passed(w):
        return [copy(w, 4 + j, (*chip, c), sib) for j, chip in enumerate(chips)]

    def start():
        for w in range(nw):
            bufs[w][...] = ins[w][...].astype(bufs[w].dtype)
        for w in range(nw):
            mine(w).start()
            for cp in first(w):
                cp.start()

    def forward():
        for j, chip in enumerate(chips):
            for w in range(nw):
                copy(w, 1 + j, (*chip, c), me).wait_recv()
                passed(w)[j].start()

    def finish():
        for w in range(nw):
            copy(w, 0, sib, me).wait_recv()
        for j, chip in enumerate(chips):
            for w in range(nw):
                copy(w, 4 + j, (*chip, 1 - c), me).wait_recv()
        for w in range(nw):
            for cp in first(w) + passed(w):
                cp.wait_send()
            mine(w).wait()

    return start, forward, finish


def _gather_scratch(shards, dtypes):
    nw = len(shards)
    if nw == 0:
        return []
    return ([pltpu.VMEM(s.shape, dt) for s, dt in zip(shards, dtypes)]
            + [pltpu.SemaphoreType.DMA((nw, 7)), pltpu.SemaphoreType.DMA((nw, 7)), pltpu.SemaphoreType.DMA((nw,))])


def _gather_shapes(shards, dtypes):
    return [jax.ShapeDtypeStruct((NDEV, *s.shape), dt) for s, dt in zip(shards, dtypes)]


def _scatter_phases(ins, outs, send_sems=None, recv_sems=None, local_sems=None):
    nw = len(ins)
    if nw == 0:
        return (lambda: None,) * 2
    x, y, c = _place()
    me = _dev_index(x, y, c)

    def copies():
        out = []
        for w in range(nw):
            out.append(pltpu.make_async_copy(ins[w].at[me], outs[w].at[me], local_sems.at[w]))
            for k, (dx, dy, dc) in enumerate(FLIPS):
                peer = ((x + dx) % 2, (y + dy) % 2, (c + dc) % 2)
                out.append(pltpu.make_async_remote_copy(
                    src_ref=ins[w].at[_dev_index(*peer)], dst_ref=outs[w].at[me], send_sem=send_sems.at[w, k],
                    recv_sem=recv_sems.at[w, k], device_id=peer, device_id_type=MESH))
        return out

    def start():
        for cp in copies():
            cp.start()

    def finish():
        for cp in copies():
            cp.wait()

    return start, finish


def _scatter_scratch(nw):
    if nw == 0:
        return []
    return [pltpu.SemaphoreType.DMA((nw, 7)), pltpu.SemaphoreType.DMA((nw, 7)), pltpu.SemaphoreType.DMA((nw,))]


TM = 512


def _join_w_in(wg):
    sw = IN_W // NDEV

    def body(wg_ref, w_ref):
        for j in range(NDEV):
            w_ref[:, sw * j:sw * (j + 1)] = wg_ref[j]
        w_ref[:, IN_W:PW] = jnp.zeros((D, PW - IN_W), BF16)

    return pl.pallas_call(body, name="join_w_in", out_shape=jax.ShapeDtypeStruct((D, PW), BF16),
                          compiler_params=_params())(wg)


def _in_proj(x, g1, w, rc, ra, rb):
    tm = TM

    def body(x_ref, g_ref, w_ref, rc_ref, ra_ref, rb_ref, qkv_ref, mqk_ref, mv_ref, mo_ref, gt_ref, u_ref):
        n, _ = _rms(x_ref[...])
        u = _bf(n * g_ref[...])
        u_ref[...] = u
        c, a, b = rc_ref[...], ra_ref[...], rb_ref[...]
        for half in range(2):
            blk = _dot(u, w_ref[:, half * 512:(half + 1) * 512])
            for t in range(4):
                lo = half * 512 + t * 128
                qkv_ref[:, lo:lo + 128] = _rope(blk[:, t * 128:(t + 1) * 128], c, a, b)
        qkv_ref[:, 1024:1536] = _dot(u, w_ref[:, 1024:1536])
        mqk_ref[:, 0:512] = _dot(u, w_ref[:, 1536:2048])
        mqk_ref[:, 512:1024] = _dot(u, w_ref[:, 2048:2560])
        mv_ref[...] = _dot(u, w_ref[:, 2560:3072])
        mo_ref[...] = _dot(u, w_ref[:, 3072:3584])
        gt_ref[...] = _dot(u, w_ref[:, 3584:3712])

    row = lambda wd: pl.BlockSpec((tm, wd), lambda i: (i, 0))
    return pl.pallas_call(
        body, name="in_proj", grid=(S // tm,),
        in_specs=[row(D), _cspec((1, D)), _cspec((D, PW)), row(128), row(128), row(128)],
        out_specs=[row(1536), row(1024), row(512), row(512), row(128), row(D)],
        out_shape=[jax.ShapeDtypeStruct((S, 1536), F32), jax.ShapeDtypeStruct((S, 1024), F32),
                   jax.ShapeDtypeStruct((S, 512), F32), jax.ShapeDtypeStruct((S, 512), F32),
                   jax.ShapeDtypeStruct((S, 128), F32), jax.ShapeDtypeStruct((S, D), BF16)],
        compiler_params=_params(1),
    )(x, g1, w, rc, ra, rb)


DILATIONS = (16, 4, 1)


def _attn_valid(n):
    kd = lax.broadcasted_iota(jnp.int32, (128, 256), 1) - lax.broadcasted_iota(jnp.int32, (128, 256), 0)
    off = jnp.where(n == 0, 0, 128)
    return (kd <= off) & (kd >= off - 128)


def _attn_rows(d, r, n):
    if d == 1:
        q0 = pl.multiple_of(n * 128, 128)
        k0 = pl.multiple_of(jnp.maximum(n - 1, 0) * 128, 128)
        return pl.ds(q0, 128), pl.ds(k0, 256), _attn_valid(n)
    q0 = r + n * 128 * d
    k0 = r + jnp.maximum(n - 1, 0) * 128 * d
    return pl.ds(q0, 128, stride=d), pl.ds(k0, 256, stride=d), _attn_valid(n)


ATTN_GROUP = 4
ATTN_ITERS = S // 128 // ATTN_GROUP


def _attn_group(d, i):
    nb = S // (128 * d)
    if nb == 2:
        qi = lax.broadcasted_iota(jnp.int32, (256, 256), 0) - lax.broadcasted_iota(jnp.int32, (256, 256), 1)
        whole = [pl.ds((ATTN_GROUP // 2) * i + u, 256, stride=d) for u in range(ATTN_GROUP // 2)]
        return [(rows, rows, (qi >= 0) & (qi <= 128)) for rows in whole]
    if d == 1:
        return [_attn_rows(1, 0, i + ATTN_ITERS * u) for u in range(ATTN_GROUP)]
    return [_attn_rows(d, (i // nb) * ATTN_GROUP + u, i % nb) for u in range(ATTN_GROUP)]


def _head0(shape):
    return lax.broadcasted_iota(jnp.int32, shape, 1) < 64


def _stack_heads(t):
    h0 = _head0(t.shape)
    tb = _bf(t)
    zero = jnp.zeros_like(tb)
    return jnp.concatenate([jnp.where(h0, tb, zero), jnp.where(h0, zero, tb)], axis=0)


def _attn_fwd(qkv, shards, dtypes):
    nw = len(shards)

    def body(*refs):
        q_ref, k_ref, v_ref = refs[:3]
        ins = refs[3:3 + nw]
        o_ref, lse_ref = refs[3 + nw:5 + nw]
        outs = refs[5 + nw:5 + 2 * nw]
        m0, m1, l0, l1, acc = refs[5 + 2 * nw:10 + 2 * nw]
        bufs = refs[10 + 2 * nw:10 + 3 * nw]
        ag_start, ag_forward, ag_finish = _gather_phases(ins, outs, bufs, *refs[10 + 3 * nw:])
        hp = pl.program_id(0)
        pl.when(hp == 0)(ag_start)
        pl.when(hp == 3)(ag_forward)
        stats = (m0, m1, l0, l1, acc)

        def update(blocks, first):
            loaded = [([q_ref[rq, :], k_ref[rk, :], v_ref[rk, :]], None if first else [ref[rq, :] for ref in stats])
                      for rq, rk, _ in blocks]
            results = []
            for ((q, k, v), prev), (_, _, valid) in zip(loaded, blocks):
                head0 = _head0(q.shape)
                kb, vb = _bf(k), _bf(v)
                q = q * 0.125
                m_new, l_new, acc_new = [], [], []
                for a, qa in enumerate((_bf(jnp.where(head0, q, 0.0)), _bf(jnp.where(head0, 0.0, q)))):
                    s = jnp.where(valid, _dot_nt(qa, kb), NEG)
                    mc = jnp.max(s, axis=-1, keepdims=True)
                    m_a = jnp.broadcast_to(mc, q.shape) if first else jnp.maximum(prev[a], mc)
                    p = jnp.exp(s - jnp.tile(m_a, (1, 2)))
                    l_add = jnp.sum(p, axis=-1, keepdims=True)
                    pv = _dot(_bf(p), vb)
                    if first:
                        l_a = jnp.broadcast_to(l_add, q.shape)
                    else:
                        alpha = jnp.exp(prev[a] - m_a)
                        l_a, pv = alpha * prev[2 + a] + l_add, alpha * prev[4] + pv
                    m_new.append(m_a), l_new.append(l_a), acc_new.append(pv)
                results.append((m_new[0], m_new[1], l_new[0], l_new[1], jnp.where(head0, acc_new[0], acc_new[1])))
            for (rq, _, _), res in zip(blocks, results):
                for ref, val in zip(stats, res):
                    ref[rq, :] = val

        for d in DILATIONS:
            def step(i, carry, d=d):
                update(_attn_group(d, i), d == DILATIONS[0])
                return carry

            lax.fori_loop(0, ATTN_ITERS, step, 0)

        def fin(t, carry):
            rows = pl.ds(pl.multiple_of(t * 256, 256), 256)
            h0 = lax.broadcasted_iota(jnp.int32, (256, 128), 1) < 64
            l = jnp.where(h0, l0[rows, :], l1[rows, :])
            o_ref[rows, :] = acc[rows, :] / l
            lse_ref[rows, :] = jnp.where(h0, m0[rows, :], m1[rows, :]) + jnp.log(l)
            return carry

        lax.fori_loop(0, S // 256, fin, 0)
        pl.when(hp == 3)(ag_finish)

    col = lambda off: pl.BlockSpec((S, 128), lambda h, off=off: (0, off + h))
    res = pl.pallas_call(
        body, name="attn_fwd", grid=(4,),
        in_specs=[col(0), col(4), col(8)] + [VM] * nw,
        out_specs=[col(0), col(0)] + [ANY] * nw,
        out_shape=[jax.ShapeDtypeStruct((S, AW), F32), jax.ShapeDtypeStruct((S, AW), F32)]
        + _gather_shapes(shards, dtypes),
        scratch_shapes=[pltpu.VMEM((S, 128), F32)] * 5 + _gather_scratch(shards, dtypes),
        compiler_params=_params(1),
    )(qkv, qkv, qkv, *shards)
    return res[0], res[1], res[2:]


def _attn_bwd(qkv, o, lse, do, parts):
    nw = len(parts)

    def body(*refs):
        q_ref, k_ref, v_ref, o_ref, lse_ref, do_ref = refs[:6]
        ins = refs[6:6 + nw]
        dq_ref, dk_ref, dv_ref = refs[6 + nw:9 + nw]
        outs = refs[9 + nw:9 + 2 * nw]
        L0, L1, D0, D1 = refs[9 + 2 * nw:13 + 2 * nw]
        rs_start, rs_finish = _scatter_phases(ins, outs, *refs[13 + 2 * nw:])
        hp = pl.program_id(0)
        pl.when(hp == 0)(rs_start)
        def pre(t, carry):
            rows = pl.ds(pl.multiple_of(t * 256, 256), 256)
            h0 = lax.broadcasted_iota(jnp.int32, (256, 128), 1) < 64
            ls = lse_ref[rows, :]
            dd = do_ref[rows, :] * o_ref[rows, :]
            shp = (256, 128)
            L0[rows, :] = jnp.broadcast_to(jnp.max(jnp.where(h0, ls, NEG), axis=-1, keepdims=True), shp)
            L1[rows, :] = jnp.broadcast_to(jnp.max(jnp.where(h0, NEG, ls), axis=-1, keepdims=True), shp)
            D0[rows, :] = jnp.broadcast_to(jnp.sum(jnp.where(h0, dd, 0.0), axis=-1, keepdims=True), shp)
            D1[rows, :] = jnp.broadcast_to(jnp.sum(jnp.where(h0, 0.0, dd), axis=-1, keepdims=True), shp)
            return carry

        lax.fori_loop(0, S // 256, pre, 0)

        def update(blocks, first):
            loaded = [([q_ref[rq, :], k_ref[rk, :], v_ref[rk, :], do_ref[rq, :]],
                       [L0[rq, :], L1[rq, :], D0[rq, :], D1[rq, :]],
                       [0.0] * 3 if first else [dq_ref[rq, :], dk_ref[rk, :], dv_ref[rk, :]]) for rq, rk, _ in blocks]
            results = []
            for ((q, k, v, dout), (l0v, l1v, d0v, d1v), (dq, dk, dv)), (_, _, valid) in zip(loaded, blocks):
                valid = jnp.tile(valid, (1, 2))
                kst, vst = _stack_heads(k), _stack_heads(v)
                hk = _head0((256, 128))
                dob = _bf(dout)
                cat = lambda a, b: jnp.concatenate([jnp.tile(a, (1, 2)), jnp.tile(b, (1, 2))], axis=1)
                s = jnp.where(valid, _dot_nt(_bf(q * 0.125), kst), NEG)
                p = jnp.exp(s - cat(l0v, l1v))
                ds = _bf(p * (_dot_nt(dob, vst) - cat(d0v, d1v)) * 0.125)
                dk2 = _dot_tn(ds, _bf(q))
                dv2 = _dot_tn(_bf(p), dob)
                results.append((dq + _dot(ds, kst), dk + jnp.where(hk, dk2[0:256], dk2[256:512]),
                                dv + jnp.where(hk, dv2[0:256], dv2[256:512])))
            for (rq, rk, _), (dq, dk, dv) in zip(blocks, results):
                dq_ref[rq, :] = dq
                dk_ref[rk, :] = dk
                dv_ref[rk, :] = dv

        assert S // (128 * DILATIONS[0]) == 2
        for d in DILATIONS:
            def step(i, carry, d=d):
                update(_attn_group(d, i), d == DILATIONS[0])
                return carry

            lax.fori_loop(0, ATTN_ITERS, step, 0)
        pl.when(hp == 3)(rs_finish)

    col = lambda off: pl.BlockSpec((S, 128), lambda h, off=off: (0, off + h))
    res = pl.pallas_call(
        body, name="attn_bwd", grid=(4,),
        in_specs=[col(0), col(4), col(8), col(0), col(0), col(0)] + [ANY] * nw,
        out_specs=[col(0), col(0), col(0)] + [ANY] * nw,
        out_shape=[jax.ShapeDtypeStruct((S, AW), F32)] * 3 + [jax.ShapeDtypeStruct(a.shape, a.dtype) for a in parts],
        scratch_shapes=[pltpu.VMEM((S, 128), F32)] * 4 + _scatter_scratch(nw),
        compiler_params=_params(1),
    )(qkv, qkv, qkv, o, lse, do, *parts)
    return res[0], res[1], res[2], res[3:]


def _logsig(x):
    return jnp.minimum(x, 0.0) - jnp.log1p(jnp.exp(-jnp.abs(x)))


def _conv_taps(xp, n):
    return [xp[8:] if j == 3 else pltpu.roll(xp, 3 - j, 0)[8:] for j in range(4)]


def _conv_silu(xp, w_ref, b_ref, n):
    taps = _conv_taps(xp, n)
    c = b_ref[...] + sum(w_ref[j:j + 1, :] * taps[j] for j in range(4))
    sg = _sigmoid(c)
    return c, sg, taps


def _chunk_gates(G):
    assert LC == 128
    r = lax.broadcasted_iota(jnp.int32, (LC, LC), 0)
    c = lax.broadcasted_iota(jnp.int32, (LC, LC), 1)
    tril = (c <= r).astype(F32)
    triu = (c >= r).astype(F32)
    b_col = jnp.dot(tril, _logsig(G), preferred_element_type=F32, precision=HI)
    return b_col, b_col.T, G.T, tril, triu


def _colpick(X, lane):
    li = lax.broadcasted_iota(jnp.int32, X.shape, 1)
    return jnp.sum(jnp.where(li == lane, X, 0.0), axis=1, keepdims=True)


def _rowpick(XT, row):
    ri = lax.broadcasted_iota(jnp.int32, XT.shape, 0)
    return jnp.sum(jnp.where(ri == row, XT, 0.0), axis=0, keepdims=True)


def _mlstm_head(qh, kh, vh, G, b_col, b_row, g_row, h, Ch, nh, m_prev):
    bt = _colpick(b_col, 4 + h)
    i_col = _colpick(G, h)
    bs = _rowpick(b_row, 4 + h)
    i_row = _rowpick(g_row, h)
    r = lax.broadcasted_iota(jnp.int32, (LC, LC), 0)
    c = lax.broadcasted_iota(jnp.int32, (LC, LC), 1)
    log_d = jnp.where(c <= r, bt - bs + i_row, NEG)
    log_inter = bt + m_prev
    m_t = jnp.maximum(log_inter, jnp.max(log_d, axis=1, keepdims=True))
    Dm = jnp.exp(log_d - m_t)
    g = jnp.exp(log_inter - m_t)
    qb, kb, vb = _bf(qh), _bf(kh), _bf(vh)
    Am = _dot_nt(qb, kb) * Dm
    qC = _dot(qb, _bf(Ch))
    num = g * qC + _dot(_bf(Am), vb)
    qn = jnp.sum(qh * nh, axis=1, keepdims=True)
    den = g * qn + jnp.sum(Am, axis=1, keepdims=True)
    floor = jnp.exp(-m_t)
    dd = jnp.maximum(jnp.abs(den), floor)
    inv_dd = 1.0 / dd
    hh = num * inv_dd
    lane = lax.broadcasted_iota(jnp.int32, (1, LC), 1)
    blast = jnp.sum(jnp.where(lane == LC - 1, bs, 0.0), axis=1, keepdims=True)
    log_s = blast - bt + i_col
    m_new = jnp.maximum(blast + m_prev, jnp.max(log_s, axis=0, keepdims=True))
    decay = jnp.exp(blast + m_prev - m_new)
    ws = jnp.exp(log_s - m_new)
    kw = kh * ws
    C_new = decay * Ch + _dot_tn(_bf(kw), vb)
    n_new = decay * nh + jnp.sum(kw, axis=0, keepdims=True)
    return dict(Dm=Dm, g=g, Am=Am, qC=qC, qn=qn, den=den, floor=floor, inv_dd=inv_dd, h=hh, decay=decay, ws=ws, kw=kw,
                C_new=C_new, n_new=n_new, m_new=m_new, qb=qb, kb=kb, vb=vb)


def _head_out(hh, mo_h, gn_h):
    r = lax.rsqrt(jnp.mean(hh * hh, axis=-1, keepdims=True) + EPS)
    hn = hh * r
    sg = _sigmoid(mo_h)
    return sg * (hn * gn_h), hn, r, sg


def _mlstm_fwd(mqk, mv, mo, gates, conv_w, conv_b, gate_b, gn, shards, dtypes):
    nblk = S // TB
    ncb = TB // LC
    nw = len(shards)

    def body(*refs):
        x_ref, v_ref, o_ref, g_ref, w_ref, b_ref, gb_ref, gn_ref = refs[:8]
        ins = refs[8:8 + nw]
        out_ref, cs_ref, ns_ref, ms_ref = refs[8 + nw:12 + nw]
        outs = refs[12 + nw:12 + 2 * nw]
        tail, Cst, nst, mst, qs, ks = refs[12 + 2 * nw:18 + 2 * nw]
        bufs = refs[18 + 2 * nw:18 + 3 * nw]
        ag_start, ag_forward, ag_finish = _gather_phases(ins, outs, bufs, *refs[18 + 3 * nw:])
        i = pl.program_id(0)
        pl.when(i == 0)(ag_start)
        pl.when(i == nblk // 2)(ag_forward)

        @pl.when(i == 0)
        def _():
            tail[...] = jnp.zeros_like(tail)
            Cst[...] = jnp.zeros_like(Cst)
            nst[...] = jnp.zeros_like(nst)
            mst[...] = jnp.zeros_like(mst)

        x = x_ref[...]
        xp = jnp.concatenate([tail[...], x], axis=0)
        tail[...] = x[TB - 8:TB, :]
        c, sg, _ = _conv_silu(xp, w_ref, b_ref, TB)
        y = c * sg
        qs[...] = y[:, 0:MW]
        ks[...] = y[:, MW:2 * MW] * (1.0 / math.sqrt(128.0))

        for cc in range(ncb):
            rows = slice(cc * LC, (cc + 1) * LC)
            G = g_ref[rows, :] + gb_ref[...]
            b_col, b_row, g_row, _, _ = _chunk_gates(G)
            cs_ref[cc] = Cst[...]
            ns_ref[cc] = nst[...]
            ms_ref[cc] = mst[...]
            for h in range(4):
                ln = slice(h * 128, (h + 1) * 128)
                m_prev = jnp.max(mst[0:1, ln], axis=1, keepdims=True)
                f = _mlstm_head(qs[rows, ln], ks[rows, ln], v_ref[rows, ln], G, b_col, b_row, g_row, h,
                                Cst[:, ln], nst[0:1, ln], m_prev)
                out, _, _, _ = _head_out(f["h"], o_ref[rows, ln], gn_ref[:, ln])
                out_ref[rows, ln] = out
                Cst[:, ln] = f["C_new"]
                nst[0:1, ln] = f["n_new"]
                mst[0:1, ln] = jnp.broadcast_to(f["m_new"], (1, 128))
        pl.when(i == nblk - 1)(ag_finish)

    row = lambda wd: pl.BlockSpec((TB, wd), lambda i: (i, 0))
    res = pl.pallas_call(
        body, name="mlstm_fwd", grid=(nblk,),
        in_specs=[row(1024), row(MW), row(MW), row(128), _cspec((4, 1024)), _cspec((1, 1024)), _cspec((1, 128)),
                  _cspec((1, MW))] + [VM] * nw,
        out_specs=[row(MW), pl.BlockSpec((ncb, 128, MW), lambda i: (i, 0, 0)),
                   pl.BlockSpec((ncb, 8, MW), lambda i: (i, 0, 0)), pl.BlockSpec((ncb, 8, MW), lambda i: (i, 0, 0))]
        + [ANY] * nw,
        out_shape=[jax.ShapeDtypeStruct((S, MW), F32), jax.ShapeDtypeStruct((S // LC, 128, MW), F32),
                   jax.ShapeDtypeStruct((S // LC, 8, MW), F32), jax.ShapeDtypeStruct((S // LC, 8, MW), F32)]
        + _gather_shapes(shards, dtypes),
        scratch_shapes=[pltpu.VMEM((8, 1024), F32), pltpu.VMEM((128, MW), F32), pltpu.VMEM((8, MW), F32),
                        pltpu.VMEM((8, MW), F32), pltpu.VMEM((TB, MW), F32), pltpu.VMEM((TB, MW), F32)]
        + _gather_scratch(shards, dtypes),
        compiler_params=_params(1),
    )(mqk, mv, mo, gates, conv_w, conv_b, gate_b, gn, *shards)
    return res[0], res[1], res[2], res[3], res[4:]


def _mlstm_bwd(mqk, mv, mo, gates, conv_w, conv_b, gate_b, gn, cs, ns, ms, dout, parts):
    nblk = S // TB
    ncb = TB // LC
    kscale = 1.0 / math.sqrt(128.0)
    nw = len(parts)

    def body(*refs):
        x_ref, xprev_ref, v_ref, o_ref, g_ref, w_ref, b_ref, gb_ref, gn_ref, cs_ref, ns_ref, ms_ref, do_ref = refs[:13]
        ins = refs[13:13 + nw]
        dx_ref, dv_ref, dmo_ref, dg_ref, dw_ref, db_ref, dgn_ref, dgb_ref = refs[13 + nw:21 + nw]
        outs = refs[21 + nw:21 + 2 * nw]
        dCst, dnst, dyhead, qs, ks, dqk = refs[21 + 2 * nw:27 + 2 * nw]
        rs_start, rs_finish = _scatter_phases(ins, outs, *refs[27 + 2 * nw:])
        i = pl.program_id(0)
        blk = nblk - 1 - i
        pl.when(i == 0)(rs_start)

        @pl.when(i == 0)
        def _():
            dCst[...] = jnp.zeros_like(dCst)
            dnst[...] = jnp.zeros_like(dnst)
            dyhead[...] = jnp.zeros_like(dyhead)
            dw_ref[...] = jnp.zeros_like(dw_ref)
            db_ref[...] = jnp.zeros_like(db_ref)
            dgn_ref[...] = jnp.zeros_like(dgn_ref)
            dgb_ref[...] = jnp.zeros_like(dgb_ref)

        x = x_ref[...]
        xprev = jnp.where(blk == 0, 0.0, xprev_ref[...])
        xp = jnp.concatenate([xprev, x], axis=0)
        c, sg, taps = _conv_silu(xp, w_ref, b_ref, TB)
        y = c * sg
        qs[...] = y[:, 0:MW]
        ks[...] = y[:, MW:2 * MW] * kscale
        lane128 = lax.broadcasted_iota(jnp.int32, (LC, 128), 1)
        rowi = lax.broadcasted_iota(jnp.int32, (LC, 1), 0)
        ones = jnp.ones((LC, 128), F32)

        for cc in reversed(range(ncb)):
            rows = slice(cc * LC, (cc + 1) * LC)
            G = g_ref[rows, :] + gb_ref[...]
            b_col, b_row, g_row, _, triu = _chunk_gates(G)
            dB = jnp.zeros((LC, 128), F32)
            dI = jnp.zeros((LC, 128), F32)
            for h in range(4):
                ln = slice(h * 128, (h + 1) * 128)
                Ch = cs_ref[cc, :, ln]
                nh = ns_ref[cc, 0:1, ln]
                m_prev = jnp.max(ms_ref[cc, 0:1, ln], axis=1, keepdims=True)
                qh, kh, vh = qs[rows, ln], ks[rows, ln], v_ref[rows, ln]
                f = _mlstm_head(qh, kh, vh, G, b_col, b_row, g_row, h, Ch, nh, m_prev)
                hh, inv_dd, den, g, Am, Dm = f["h"], f["inv_dd"], f["den"], f["g"], f["Am"], f["Dm"]
                qb, kb, vb = f["qb"], f["kb"], f["vb"]
                gn_h = gn_ref[:, ln]
                _, hn, r, sgo = _head_out(hh, o_ref[rows, ln], gn_h)
                do = do_ref[rows, ln]
                hm = hn * gn_h
                dmo_ref[rows, ln] = do * hm * sgo * (1.0 - sgo)
                dhm = do * sgo
                dgn_ref[:, ln] = dgn_ref[:, ln] + jnp.sum(dhm * hn, axis=0, keepdims=True)
                dhn = dhm * gn_h
                dh = r * (dhn - hn * jnp.mean(dhn * hn, axis=-1, keepdims=True))
                dnum = dh * inv_dd
                ddd = -jnp.sum(dh * hh, axis=1, keepdims=True) * inv_dd
                dden = jnp.where(jnp.abs(den) >= f["floor"], ddd * jnp.sign(den), 0.0)
                dnb = _bf(dnum)
                dA = _dot_nt(dnb, vb) + dden
                dv = _dot_tn(_bf(Am), dnb)
                gd = _bf(g * dnum)
                gq = g * dden
                dq = _dot_nt(gd, _bf(Ch)) + gq * nh
                dCn = dCst[:, ln]
                dnn = dnst[0:1, ln]
                dC = f["decay"] * dCn + _dot_tn(qb, gd)
                dn = f["decay"] * dnn + jnp.sum(gq * qh, axis=0, keepdims=True)
                dg = jnp.sum(dnum * f["qC"], axis=1, keepdims=True) + dden * f["qn"]
                dS = _bf(dA * Dm)
                dq = dq + _dot(dS, kb)
                dk = _dot_tn(dS, qb)
                Gm = dA * Am
                gam = dg * g
                dCb = _bf(dCn)
                E = _dot_nt(vb, dCb) + dnn
                ws = f["ws"]
                dk = dk + ws * E
                om = jnp.sum(E * kh, axis=1, keepdims=True) * ws
                dv = dv + _dot(_bf(f["kw"]), dCb)
                ddecay = (jnp.sum(jnp.sum(dCn * Ch, axis=1, keepdims=True), axis=0, keepdims=True)
                          + jnp.sum(dnn * nh, axis=1, keepdims=True))
                delta = ddecay * f["decay"]
                rows_g = jnp.sum(Gm, axis=1, keepdims=True)
                cols_g = jnp.broadcast_to(jnp.sum(Gm, axis=0, keepdims=True), (LC, 128)).T
                last = jnp.where(rowi == LC - 1, jnp.sum(om, axis=0, keepdims=True) + delta, 0.0)
                db = rows_g + gam - om + last - cols_g
                di = cols_g + om
                dB = jnp.where(lane128 == 4 + h, db, dB)
                dI = jnp.where(lane128 == h, di, dI)
                dCst[:, ln] = dC
                dnst[0:1, ln] = dn
                dqk[rows, ln] = dq
                dqk[rows, MW + h * 128:MW + (h + 1) * 128] = dk * kscale
                dv_ref[rows, ln] = dv
            dlogf = jnp.dot(triu, dB, preferred_element_type=F32, precision=HI)
            dG = dI + dlogf * _sigmoid(-G)
            dG = jnp.where(lane128 < 8, dG, 0.0)
            dg_ref[rows, :] = dG
            dgb_ref[...] = dgb_ref[...] + jnp.sum(dG, axis=0, keepdims=True)

        dy = dqk[...] * (sg * (1.0 + c * (1.0 - sg)))
        db_ref[...] = db_ref[...] + jnp.sum(dy, axis=0, keepdims=True)
        for j in range(4):
            dw_ref[j:j + 1, :] = dw_ref[j:j + 1, :] + jnp.sum(dy * taps[j], axis=0, keepdims=True)
        dyp = jnp.concatenate([dy, dyhead[...]], axis=0)
        dx = w_ref[3:4, :] * dy
        for j in range(3):
            dx = dx + w_ref[j:j + 1, :] * pltpu.roll(dyp, TB + 8 - (3 - j), 0)[0:TB]
        dx_ref[...] = dx
        dyhead[...] = dy[0:8, :]
        pl.when(i == nblk - 1)(rs_finish)

    rrow = lambda wd: pl.BlockSpec((TB, wd), lambda i: (nblk - 1 - i, 0))
    st = lambda r: pl.BlockSpec((ncb, r, MW), lambda i: (nblk - 1 - i, 0, 0))
    prev8 = pl.BlockSpec((8, 1024), lambda i: (jnp.maximum((nblk - 1 - i) * (TB // 8) - 1, 0), 0))
    res = pl.pallas_call(
        body, name="mlstm_bwd", grid=(nblk,),
        in_specs=[rrow(1024), prev8, rrow(MW), rrow(MW), rrow(128), _cspec((4, 1024)), _cspec((1, 1024)),
                  _cspec((1, 128)), _cspec((1, MW)), st(128), st(8), st(8), rrow(MW)] + [ANY] * nw,
        out_specs=[rrow(1024), rrow(MW), rrow(MW), rrow(128),
                   pl.BlockSpec((4, 1024), lambda i: (0, 0)), pl.BlockSpec((1, 1024), lambda i: (0, 0)),
                   pl.BlockSpec((1, MW), lambda i: (0, 0)), pl.BlockSpec((1, 128), lambda i: (0, 0))] + [ANY] * nw,
        out_shape=[jax.ShapeDtypeStruct((S, 1024), F32), jax.ShapeDtypeStruct((S, MW), F32),
                   jax.ShapeDtypeStruct((S, MW), F32), jax.ShapeDtypeStruct((S, 128), F32),
                   jax.ShapeDtypeStruct((4, 1024), F32), jax.ShapeDtypeStruct((1, 1024), F32),
                   jax.ShapeDtypeStruct((1, MW), F32), jax.ShapeDtypeStruct((1, 128), F32)]
        + [jax.ShapeDtypeStruct(a.shape, a.dtype) for a in parts],
        scratch_shapes=[pltpu.VMEM((128, MW), F32), pltpu.VMEM((8, MW), F32), pltpu.VMEM((8, 1024), F32),
                        pltpu.VMEM((TB, MW), F32), pltpu.VMEM((TB, MW), F32), pltpu.VMEM((TB, 1024), F32)]
        + _scatter_scratch(nw),
        compiler_params=_params(1),
    )(mqk, mqk, mv, mo, gates, conv_w, conv_b, gate_b, gn, cs, ns, ms, dout, *parts)
    return res[:8], res[8:]


def _out_proj(x, attn, ml, w, g):
    tm = TM

    def body(x_ref, a_ref, m_ref, w_ref, g_ref, h_ref, u_ref):
        h1 = x_ref[...] + _dot(_bf(a_ref[...]), w_ref[0:AW, :]) + _dot(_bf(m_ref[...]), w_ref[AW:D, :])
        h_ref[...] = h1
        n, _ = _rms(h1)
        u_ref[...] = _bf(n * g_ref[...])

    row = lambda wd: pl.BlockSpec((tm, wd), lambda i: (i, 0))
    return pl.pallas_call(
        body, name="out_proj", grid=(S // tm,),
        in_specs=[row(D), row(AW), row(MW), _cspec((D, D)), _cspec((1, D))],
        out_specs=[row(D), row(D)],
        out_shape=[jax.ShapeDtypeStruct((S, D), F32), jax.ShapeDtypeStruct((S, D), BF16)],
        compiler_params=_params(1),
    )(x, attn, ml, w, g)


def _mlp_fwd(h1, u2, w_up, w_down):
    tm = TM

    def body(h_ref, u_ref, wu_ref, wd_ref, a_ref, o_ref):
        u = u_ref[...]
        acc = h_ref[...]
        for c in range(NDEV):
            cols = slice(c * 512, (c + 1) * 512)
            a = _dot(u, wu_ref[c])
            a_ref[:, cols] = _bf(a)
            r = jnp.maximum(a, 0.0)
            acc = acc + _dot(_bf(r * r), wd_ref[cols, :])
        o_ref[...] = acc

    row = lambda wd: pl.BlockSpec((tm, wd), lambda i: (i, 0))
    return pl.pallas_call(
        body, name="mlp_fwd", grid=(S // tm,),
        in_specs=[row(D), row(D), _cspec((NDEV, D, DFF // NDEV)), _cspec((DFF, D))],
        out_specs=[row(DFF), row(D)],
        out_shape=[jax.ShapeDtypeStruct((S, DFF), BF16), jax.ShapeDtypeStruct((S, D), F32)],
        compiler_params=_params(1),
    )(h1, u2, w_up, w_down)


def _ple_loss(h2, p, target, w_pg, w_ple, g_ple, g_fin):
    tm = TM

    def body(h_ref, p_ref, t_ref, wg_ref, wp_ref, gp_ref, gf_ref,
             dh_ref, dwg_ref, dwp_ref, dgp_ref, dgf_ref, loss_ref, acc_g, acc_p):
        i = pl.program_id(0)

        @pl.when(i == 0)
        def _():
            acc_g[...] = jnp.zeros_like(acc_g)
            acc_p[...] = jnp.zeros_like(acc_p)
            dgp_ref[...] = jnp.zeros_like(dgp_ref)
            dgf_ref[...] = jnp.zeros_like(dgf_ref)
            loss_ref[...] = jnp.zeros_like(loss_ref)

        h2v = h_ref[...]
        n2, rs2 = _rms(h2v)
        u3 = _bf(n2 * gp_ref[...])
        gt = _sigmoid(_dot(u3, wg_ref[...]))
        pb = _bf(p_ref[...])
        e = jnp.concatenate([_dot(pb, wp_ref[j]) for j in range(NDEV)], axis=1)
        h3 = h2v + gt * e
        n3, rs3 = _rms(h3)
        err = n3 * gf_ref[...] - t_ref[...]
        loss_ref[...] = loss_ref[...] + 0.5 / D * jnp.sum(jnp.sum(err * err, axis=1, keepdims=True), axis=0, keepdims=True)
        dy = err * (1.0 / D)
        dgf_ref[...] = dgf_ref[...] + jnp.sum(dy * n3, axis=0, keepdims=True)
        dh3 = _rms_bwd(dy, n3, rs3, gf_ref[...])
        de = _bf(dh3 * gt)
        dz = _bf(dh3 * e * gt * (1.0 - gt))
        acc_p[...] = acc_p[...] + _dot_tn(pb, de)
        acc_g[...] = acc_g[...] + _dot_tn(u3, dz)
        du3 = _dot_nt(dz, wg_ref[...])
        dgp_ref[...] = dgp_ref[...] + jnp.sum(du3 * n2, axis=0, keepdims=True)
        dh_ref[...] = dh3 + _rms_bwd(du3, n2, rs2, gp_ref[...])

        @pl.when(i == S // tm - 1)
        def _():
            dwg_ref[...] = _bf(acc_g[...])
            for j in range(NDEV):
                dwp_ref[j] = _bf(acc_p[:, j * 128:(j + 1) * 128])

    row = lambda wd: pl.BlockSpec((tm, wd), lambda i: (i, 0))
    whole = lambda shp: pl.BlockSpec(shp, lambda i: (0,) * len(shp))
    return pl.pallas_call(
        body, name="ple_loss", grid=(S // tm,),
        in_specs=[row(D), row(PLE), row(D), _cspec((D, D)), _cspec((NDEV, PLE, 128)), _cspec((1, D)), _cspec((1, D))],
        out_specs=[row(D), whole((D, D)), whole((NDEV, PLE, 128)), whole((1, D)), whole((1, D)), whole((1, 1))],
        out_shape=[jax.ShapeDtypeStruct((S, D), F32), jax.ShapeDtypeStruct((D, D), BF16),
                   jax.ShapeDtypeStruct((NDEV, PLE, 128), BF16), jax.ShapeDtypeStruct((1, D), F32),
                   jax.ShapeDtypeStruct((1, D), F32), jax.ShapeDtypeStruct((1, 1), F32)],
        scratch_shapes=[pltpu.VMEM((D, D), F32), pltpu.VMEM((PLE, D), F32)],
        compiler_params=_params(1),
    )(h2, p, target, w_pg, w_ple, g_ple, g_fin)


def _mlp_bwd(dh2, a, h1, g, w_up, w_down):
    tm = TM

    def body(d_ref, a_ref, h_ref, g_ref, wu_ref, wd_ref, da_ref, dh1_ref, dg_ref):
        @pl.when(pl.program_id(0) == 0)
        def _():
            dg_ref[...] = jnp.zeros_like(dg_ref)

        dh2v = d_ref[...]
        db = _bf(dh2v)
        du = jnp.zeros((tm, D), F32)
        for c in range(NDEV):
            cols = slice(c * 512, (c + 1) * 512)
            dr = _dot_nt(db, wd_ref[cols, :])
            da = _bf(dr * (2.0 * jnp.maximum(a_ref[:, cols], 0.0)))
            da_ref[:, cols] = da
            du = du + _dot_nt(da, wu_ref[c])
        n, rs = _rms(h_ref[...])
        dg_ref[...] = dg_ref[...] + jnp.sum(du * n, axis=0, keepdims=True)
        dh1_ref[...] = dh2v + _rms_bwd(du, n, rs, g_ref[...])

    row = lambda wd: pl.BlockSpec((tm, wd), lambda i: (i, 0))
    return pl.pallas_call(
        body, name="mlp_bwd", grid=(S // tm,),
        in_specs=[row(D), row(DFF), row(D), _cspec((1, D)), _cspec((NDEV, D, DFF // NDEV)), _cspec((DFF, D))],
        out_specs=[row(DFF), row(D), pl.BlockSpec((1, D), lambda i: (0, 0))],
        out_shape=[jax.ShapeDtypeStruct((S, DFF), BF16), jax.ShapeDtypeStruct((S, D), F32),
                   jax.ShapeDtypeStruct((1, D), F32)],
        compiler_params=_params(1),
    )(dh2, a, h1, g, w_up, w_down)


def _out_proj_bwd(dh1, attn, ml, w):
    tm = TM

    def body(d_ref, a_ref, m_ref, w_ref, da_ref, dm_ref, dw_ref, acc):
        i = pl.program_id(0)

        @pl.when(i == 0)
        def _():
            acc[...] = jnp.zeros_like(acc)

        db = _bf(d_ref[...])
        dmix = _dot_nt(db, w_ref[...])
        da_ref[...] = dmix[:, 0:AW]
        dm_ref[...] = dmix[:, AW:D]
        acc[0:AW, :] = acc[0:AW, :] + _dot_tn(_bf(a_ref[...]), db)
        acc[AW:D, :] = acc[AW:D, :] + _dot_tn(_bf(m_ref[...]), db)

        @pl.when(i == S // tm - 1)
        def _():
            dw_ref[...] = _bf(acc[...])

    row = lambda wd: pl.BlockSpec((tm, wd), lambda i: (i, 0))
    return pl.pallas_call(
        body, name="out_proj_bwd", grid=(S // tm,),
        in_specs=[row(D), row(AW), row(MW), _cspec((D, D))],
        out_specs=[row(AW), row(MW), pl.BlockSpec((D, D), lambda i: (0, 0))],
        out_shape=[jax.ShapeDtypeStruct((S, AW), F32), jax.ShapeDtypeStruct((S, MW), F32),
                   jax.ShapeDtypeStruct((D, D), BF16)],
        scratch_shapes=[pltpu.VMEM((D, D), F32)],
        compiler_params=_params(1),
    )(dh1, attn, ml, w)


def _in_proj_bwd(dq, dk, dv, dmqk, dmv, dmo, dgt, dh1, x, g1, w, rc, ra, rb):
    tm = TM

    def body(dq_ref, dk_ref, dv_ref, dmqk_ref, dmv_ref, dmo_ref, dgt_ref, dh_ref, x_ref, g_ref, w_ref,
             rc_ref, ra_ref, rb_ref, dp_ref, dx_ref, dg_ref):
        @pl.when(pl.program_id(0) == 0)
        def _():
            dg_ref[...] = jnp.zeros_like(dg_ref)

        c, a, b = rc_ref[...], ra_ref[...], rb_ref[...]
        for half, ref in enumerate((dq_ref, dk_ref)):
            for t in range(4):
                lo = half * 512 + t * 128
                dp_ref[:, lo:lo + 128] = _bf(_rope_bwd(ref[:, t * 128:(t + 1) * 128], c, a, b))
        dp_ref[:, 1024:1536] = _bf(dv_ref[...])
        dp_ref[:, 1536:2560] = _bf(dmqk_ref[...])
        dp_ref[:, 2560:3072] = _bf(dmv_ref[...])
        dp_ref[:, 3072:3584] = _bf(dmo_ref[...])
        dp_ref[:, 3584:3712] = _bf(dgt_ref[...])
        dp_ref[:, 3712:PW] = jnp.zeros((tm, PW - 3712), BF16)
        du = jnp.zeros((tm, D), F32)
        for s in range(PW // 768):
            cols = slice(s * 768, (s + 1) * 768)
            du = du + _dot_nt(dp_ref[:, cols], w_ref[:, cols])
        n, rs = _rms(x_ref[...])
        dg_ref[...] = dg_ref[...] + jnp.sum(du * n, axis=0, keepdims=True)
        dx_ref[...] = dh_ref[...] + _rms_bwd(du, n, rs, g_ref[...])

    row = lambda wd: pl.BlockSpec((tm, wd), lambda i: (i, 0))
    return pl.pallas_call(
        body, name="in_proj_bwd", grid=(S // tm,),
        in_specs=[row(AW), row(AW), row(AW), row(1024), row(MW), row(MW), row(128), row(D), row(D), _cspec((1, D)),
                  _cspec((D, PW)), row(128), row(128), row(128)],
        out_specs=[row(PW), row(D), pl.BlockSpec((1, D), lambda i: (0, 0))],
        out_shape=[jax.ShapeDtypeStruct((S, PW), BF16), jax.ShapeDtypeStruct((S, D), F32),
                   jax.ShapeDtypeStruct((1, D), F32)],
        compiler_params=_params(1),
    )(dq, dk, dv, dmqk, dmv, dmo, dgt, dh1, x, g1, w, rc, ra, rb)


SMALL_ROWS = 96


def _small_phases(ins, out_ref, pack, rbuf, send_sems, recv_sems):
    x, y, c = _place()
    me = _dev_index(x, y, c)

    def copies():
        out = []
        for k, (dx, dy, dc) in enumerate(FLIPS):
            peer = ((x + dx) % 2, (y + dy) % 2, (c + dc) % 2)
            out.append(pltpu.make_async_remote_copy(
                src_ref=pack, dst_ref=rbuf.at[me], send_sem=send_sems.at[k], recv_sem=recv_sems.at[k],
                device_id=peer, device_id_type=MESH))
        return out

    def start():
        pack[...] = jnp.zeros_like(pack)
        for i, ref in enumerate(ins):
            pack[8 * i:8 * i + 1, 0:ref.shape[1]] = ref[...]
        rbuf[me] = pack[...]
        for cp in copies():
            cp.start()

    def finish():
        for cp in copies():
            cp.wait()
        tot = rbuf[0]
        for j in range(1, NDEV):
            tot = tot + rbuf[j]
        out_ref[...] = tot

    return start, finish


def _wgrad(name, A, B, a_fn, b_fn, tk, tn, out_shape, out_spec, ts=512, split=None, small=()):
    K, N = A.shape[1], B.shape[1]
    nrt = S // ts
    nc = next(c for c in (1024, 1280, tn) if tn % c == 0)
    ns = len(small)
    grid = (N // tn, K // tk, nrt)

    def body(*refs):
        a_ref, b_ref = refs[:2]
        o_ref = refs[2 + ns]
        acc = refs[3 + ns + bool(ns)]
        r = pl.program_id(2)
        if ns:
            step = (pl.program_id(0) * grid[1] + pl.program_id(1)) * nrt + r
            sm_start, sm_finish = _small_phases(refs[2:2 + ns], refs[3 + ns], *refs[4 + ns + 1:])
            pl.when(step == 0)(sm_start)

        @pl.when(r == 0)
        def _():
            acc[...] = jnp.zeros_like(acc)

        at = a_fn(a_ref[...]).T
        for c in range(tn // nc):
            cols = slice(c * nc, (c + 1) * nc)
            acc[:, cols] = acc[:, cols] + _dot(at, b_fn(b_ref[:, cols]))

        @pl.when(r == nrt - 1)
        def _():
            if split is None:
                o_ref[...] = _bf(acc[...])
            else:
                for j in range(NDEV):
                    o_ref[j] = _bf(acc[:, split * j:split * (j + 1)])

        if ns:
            pl.when(step == grid[0] * grid[1] * nrt - 1)(sm_finish)

    in_specs = [pl.BlockSpec((ts, tk), lambda n, k, r: (r, k)), pl.BlockSpec((ts, tn), lambda n, k, r: (r, n))]
    scratch = [pltpu.VMEM((tk, tn), F32)]
    if not ns:
        return pl.pallas_call(
            body, name=name, grid=grid, in_specs=in_specs, out_specs=out_spec,
            out_shape=jax.ShapeDtypeStruct(out_shape, BF16), scratch_shapes=scratch, compiler_params=_params(3),
        )(A, B)
    return pl.pallas_call(
        body, name=name, grid=grid, in_specs=in_specs + [VM] * ns, out_specs=[out_spec, VM],
        out_shape=[jax.ShapeDtypeStruct(out_shape, BF16), jax.ShapeDtypeStruct((SMALL_ROWS, 1024), F32)],
        scratch_shapes=scratch + [pltpu.VMEM((SMALL_ROWS, 1024), F32), pltpu.VMEM((NDEV, SMALL_ROWS, 1024), F32),
                                  pltpu.SemaphoreType.DMA((7,)), pltpu.SemaphoreType.DMA((7,))],
        compiler_params=_params(3),
    )(A, B, *small)


def _relu2_bf(a):
    r = jnp.maximum(a.astype(F32), 0.0)
    return _bf(r * r)


def _ident(a):
    return a


def _step(x, p, target, g1, conv_b, gate_b, gn, g_mlp, g_ple, g_fin, sh):
    g_in, g_conv = _gather_weights([sh["w_in"], sh["conv_w"]], [BF16, F32])
    conv_w = g_conv.transpose(1, 0, 2).reshape(4, 1024)
    rc, ra, rb = _rope_tables()
    w_in_p = _join_w_in(g_in)
    qkv, mqk, mv, mo, gates, u1 = _in_proj(x, g1, w_in_p, rc, ra, rb)
    attn, lse, (w_up8, w_down8) = _attn_fwd(qkv, [sh["w_up"], sh["w_down"]], [BF16] * 2)
    ml, cs, ns, ms, (w_out8, w_pg8, w_ple8) = _mlstm_fwd(
        mqk, mv, mo, gates, conv_w, conv_b, gate_b, gn, [sh["w_out"], sh["w_ple_gate"], sh["w_ple"]], [BF16] * 3)
    w_out, w_down, w_pg = w_out8.reshape(D, D), w_down8.reshape(DFF, D), w_pg8.reshape(D, D)
    h1, u2 = _out_proj(x, attn, ml, w_out, g_mlp)
    a, h2 = _mlp_fwd(h1, u2, w_up8, w_down)
    dh2, dw_pg, dw_ple8, dg_ple, dg_fin, loss = _ple_loss(h2, p, target, w_pg, w_ple8, g_ple, g_fin)
    da, dh1, dg_mlp = _mlp_bwd(dh2, a, h1, g_mlp, w_up8, w_down)
    dw_up8 = _wgrad("wgrad_up", u2, da, _ident, _ident, D, DFF, (NDEV, D, DFF // NDEV),
                    pl.BlockSpec((NDEV, D, DFF // NDEV), lambda n, k, r: (0, 0, 0)), split=DFF // NDEV)
    dw_down = _wgrad("wgrad_down", a, dh2, _relu2_bf, _bf, 1024, 1024, (DFF, D),
                     pl.BlockSpec((1024, 1024), lambda n, k, r: (k, n)))
    d_attn, d_ml, dw_out = _out_proj_bwd(dh1, attn, ml, w_out)
    (dmqk, dmv, dmo, dgt, dconv_w, dconv_b, dgn, dgate_b), (r_out, r_up, r_pg, r_ple) = _mlstm_bwd(
        mqk, mv, mo, gates, conv_w, conv_b, gate_b, gn, cs, ns, ms, d_ml,
        [dw_out.reshape(NDEV, D // NDEV, D), dw_up8, dw_pg.reshape(NDEV, D // NDEV, D), dw_ple8])
    dq, dk, dv, (r_down,) = _attn_bwd(qkv, attn, lse, d_attn, [dw_down.reshape(NDEV, DFF // NDEV, D)])
    dproj, dx, dg1 = _in_proj_bwd(dq, dk, dv, dmqk, dmv, dmo, dgt, dh1, x, g1, w_in_p, rc, ra, rb)
    small = dict(norm_mix_g=dg1, conv_b=dconv_b, gate_b=dgate_b, mlstm_norm_g=dgn, norm_mlp_g=dg_mlp,
                 norm_ple_g=dg_ple, final_norm_g=dg_fin)
    dw_in8, total = _wgrad("wgrad_in", u1, dproj, _ident, _ident, D, PW, (NDEV, D, IN_W // NDEV),
                           pl.BlockSpec((NDEV, D, IN_W // NDEV), lambda n, k, r: (0, 0, 0)), split=IN_W // NDEV,
                           small=[small[n] for n in SMALL] + [loss] + [dconv_w[j:j + 1] for j in range(4)])
    recv = dict(w_in=_scatter_two_level(dw_in8), w_out=r_out, w_up=r_up, w_down=r_down, w_ple_gate=r_pg, w_ple=r_ple)
    return dx, recv, total


def _gather_weights(shards, dtypes):
    nw = len(shards)

    def body(*refs):
        start, forward, finish = _gather_phases(refs[:nw], refs[nw:2 * nw], refs[2 * nw:3 * nw], *refs[3 * nw:])
        start()
        forward()
        finish()

    return pl.pallas_call(
        body, name="gather_weights",
        in_specs=[VM] * nw, out_specs=[ANY] * nw,
        out_shape=_gather_shapes(shards, dtypes),
        scratch_shapes=_gather_scratch(shards, dtypes),
        compiler_params=_params(),
    )(*shards)


CHIP_FLIPS = [(0, 0), (0, 1), (1, 0), (1, 1)]


def _scatter_two_level(part):
    shard = part.shape[1:]
    nc = len(CHIP_FLIPS)

    def body(in_ref, out_ref, mine_v, sib_v, psum_v, loc_sems, d2d_send, d2d_recv, ici_send, ici_recv, own_sem):
        x, y, c = _place()
        chips = [((x + dx) % 2, (y + dy) % 2) for dx, dy in CHIP_FLIPS]
        local, to_sib = [], []
        for k, (px, py) in enumerate(chips):
            local.append(pltpu.make_async_copy(in_ref.at[_dev_index(px, py, c)], mine_v.at[k], loc_sems.at[k]))
            to_sib.append(pltpu.make_async_remote_copy(
                src_ref=in_ref.at[_dev_index(px, py, 1 - c)], dst_ref=sib_v.at[k], send_sem=d2d_send.at[k],
                recv_sem=d2d_recv.at[k], device_id=(x, y, 1 - c), device_id_type=MESH))
        for cp in to_sib + local:
            cp.start()

        def over_ici(k):
            return pltpu.make_async_remote_copy(
                src_ref=psum_v.at[k], dst_ref=out_ref.at[k], send_sem=ici_send.at[k - 1], recv_sem=ici_recv.at[k - 1],
                device_id=(*chips[k], c), device_id_type=MESH)

        own = pltpu.make_async_copy(psum_v.at[0], out_ref.at[0], own_sem)
        for k in (1, 2, 3, 0):
            local[k].wait()
            to_sib[k].wait_recv()
            psum_v[k] = _bf(mine_v[k].astype(F32) + sib_v[k].astype(F32))
            if k:
                over_ici(k).start()
            else:
                own.start()
        for k in range(1, nc):
            over_ici(k).wait()
        for cp in to_sib:
            cp.wait_send()
        own.wait()

    return pl.pallas_call(
        body, name="scatter_grads",
        in_specs=[ANY], out_specs=ANY,
        out_shape=jax.ShapeDtypeStruct((nc, *shard), part.dtype),
        scratch_shapes=[pltpu.VMEM((nc, *shard), part.dtype)] * 3
        + [pltpu.SemaphoreType.DMA((nc,))] * 3 + [pltpu.SemaphoreType.DMA((nc - 1,))] * 2 + [pltpu.SemaphoreType.DMA],
        compiler_params=_params(),
    )(part)


def _adamw(name, gparts, w, m, v, tr):
    P, R, C = gparts.shape
    c1 = 1.0 - ADAM_B1 ** ADAM_STEP
    c2 = 1.0 - ADAM_B2 ** ADAM_STEP

    def body(g_ref, w_ref, m_ref, v_ref, go_ref, d_ref, mo_ref, vo_ref):
        g = g_ref[0].astype(F32)
        for j in range(1, P):
            g = g + g_ref[j].astype(F32)
        m2 = ADAM_B1 * m_ref[...] + (1.0 - ADAM_B1) * g
        v2 = ADAM_B2 * v_ref[...] + (1.0 - ADAM_B2) * (g * g)
        go_ref[...] = g
        mo_ref[...] = m2
        vo_ref[...] = v2
        d_ref[...] = -ADAM_LR * ((m2 / c1) / (jnp.sqrt(v2 / c2) + ADAM_EPS) + ADAM_WD * w_ref[...])

    row = pl.BlockSpec((tr, C), lambda i: (i, 0))
    return pl.pallas_call(
        body, name=name, grid=(R // tr,),
        in_specs=[pl.BlockSpec((P, tr, C), lambda i: (0, i, 0)), row, row, row],
        out_specs=[row] * 4,
        out_shape=[jax.ShapeDtypeStruct((R, C), F32)] * 4,
        compiler_params=_params(1),
    )(gparts, w, m, v)


SMALL = ("norm_mix_g", "conv_b", "gate_b", "mlstm_norm_g", "norm_mlp_g", "norm_ple_g", "final_norm_g")


def _pack_small(vals):
    return jnp.concatenate([jnp.pad(a, ((0, 7), (0, 1024 - a.shape[1]))) for a in vals], axis=0)


def kernel(x, p, norm_mix_g, w_in, conv_w, conv_b, gate_b, mlstm_norm_g, w_out, norm_mlp_g, w_up, w_down, norm_ple_g, w_ple_gate, w_ple, final_norm_g, loss_target, m_norm_mix_g, m_w_in, m_conv_w, m_conv_b, m_gate_b, m_mlstm_norm_g, m_w_out, m_norm_mlp_g, m_w_up, m_w_down, m_norm_ple_g, m_w_ple_gate, m_w_ple, m_final_norm_g, v_norm_mix_g, v_w_in, v_conv_w, v_conv_b, v_gate_b, v_mlstm_norm_g, v_w_out, v_norm_mlp_g, v_w_up, v_w_down, v_norm_ple_g, v_w_ple_gate, v_w_ple, v_final_norm_g):
    big_names = ("w_in", "conv_w", "w_out", "w_up", "w_down", "w_ple_gate", "w_ple")
    wts = dict(w_in=w_in, conv_w=conv_w, w_out=w_out, w_up=w_up, w_down=w_down, w_ple_gate=w_ple_gate, w_ple=w_ple)
    mom = dict(w_in=m_w_in, conv_w=m_conv_w, w_out=m_w_out, w_up=m_w_up, w_down=m_w_down, w_ple_gate=m_w_ple_gate,
               w_ple=m_w_ple)
    var = dict(w_in=v_w_in, conv_w=v_conv_w, w_out=v_w_out, w_up=v_w_up, w_down=v_w_down, w_ple_gate=v_w_ple_gate,
               w_ple=v_w_ple)
    sq = lambda a: a.reshape(a.shape[1:])
    fin = final_norm_g.reshape(1, D)
    dx, recv, total = _step(
        x[0], p[0, 0], loss_target[0], norm_mix_g, conv_b, jnp.pad(gate_b, ((0, 0), (0, 120))), mlstm_norm_g,
        norm_mlp_g, norm_ple_g, fin, {n: sq(wts[n]) for n in big_names})

    nrow = 8 * len(SMALL)
    me = _dev_index(*_place())
    conv_rows = total[nrow + 8:nrow + 40:8]
    recv["conv_w"] = lax.dynamic_slice_in_dim(conv_rows, me * 128, 128, axis=1).reshape(1, 4, 128)
    out = {}
    for n, tr in zip(big_names, (256, 4, 128, 256, 256, 128, 256)):
        res = _adamw("adamw_" + n, recv[n], sq(wts[n]), sq(mom[n]), sq(var[n]), tr)
        out[n] = [t.reshape(wts[n].shape) for t in res]
    sw = dict(norm_mix_g=norm_mix_g, conv_b=conv_b, gate_b=gate_b, mlstm_norm_g=mlstm_norm_g, norm_mlp_g=norm_mlp_g,
              norm_ple_g=norm_ple_g, final_norm_g=fin)
    sm = dict(norm_mix_g=m_norm_mix_g, conv_b=m_conv_b, gate_b=m_gate_b, mlstm_norm_g=m_mlstm_norm_g,
              norm_mlp_g=m_norm_mlp_g, norm_ple_g=m_norm_ple_g, final_norm_g=m_final_norm_g.reshape(1, D))
    sv = dict(norm_mix_g=v_norm_mix_g, conv_b=v_conv_b, gate_b=v_gate_b, mlstm_norm_g=v_mlstm_norm_g,
              norm_mlp_g=v_norm_mlp_g, norm_ple_g=v_norm_ple_g, final_norm_g=v_final_norm_g.reshape(1, D))
    res = _adamw("adamw_small", total[0:nrow].reshape(1, nrow, 1024), _pack_small([sw[n] for n in SMALL]),
                 _pack_small([sm[n] for n in SMALL]), _pack_small([sv[n] for n in SMALL]), nrow)
    for i, n in enumerate(SMALL):
        shp = final_norm_g.shape if n == "final_norm_g" else sw[n].shape
        out[n] = [t[8 * i, 0:sw[n].shape[1]].reshape(shp) for t in res]
    order = ("norm_mix_g", "w_in", "conv_w", "conv_b", "gate_b", "mlstm_norm_g", "w_out", "norm_mlp_g", "w_up", "w_down",
             "norm_ple_g", "w_ple_gate", "w_ple", "final_norm_g")
    loss_all = total[nrow, 0]
    return (loss_all, dx[None], *[out[n][0] for n in order], *[out[n][1] for n in order],
            *[out[n][2] for n in order], *[out[n][3] for n in order])
```

```python
import functools
import math

import jax
import jax.numpy as jnp
from jax import lax
from jax.experimental import pallas as pl
from jax.experimental.pallas import tpu as pltpu

F32, BF16 = jnp.float32, jnp.bfloat16
S = 4096
D = 1024
AW = 512
MW = 512
DFF = 4096
PLE = 256
IN_W = 3592
PW = 3840
NDEV = 8
EPS = 1e-6
NEG = -1e30
LC = 128
TB = 256
ROPE_THETA = 500000.0
VMEM_LIMIT = 56 * 1024 * 1024
HI = lax.Precision.HIGHEST

ADAM_LR, ADAM_B1, ADAM_B2, ADAM_EPS, ADAM_WD, ADAM_STEP = 0.001, 0.9, 0.999, 1e-08, 0.01, 10


def _params(n_grid=0, **kw):
    sem = dict(dimension_semantics=("arbitrary",) * n_grid) if n_grid else {}
    return pltpu.CompilerParams(vmem_limit_bytes=VMEM_LIMIT, **sem, **kw)


def _cspec(shape):
    nd = len(shape)
    return pl.BlockSpec(shape, lambda *_: (0,) * nd, pipeline_mode=pl.Buffered(1))


def _dot(a, b):
    return jnp.dot(a, b, preferred_element_type=F32)


def _dot_nt(a, b):
    return lax.dot_general(a, b, (((1,), (1,)), ((), ())), preferred_element_type=F32)


def _dot_tn(a, b):
    return lax.dot_general(a, b, (((0,), (0,)), ((), ())), preferred_element_type=F32)


def _bf(x):
    return x.astype(BF16)


def _rms(x):
    rs = lax.rsqrt(jnp.mean(x * x, axis=-1, keepdims=True) + EPS)
    return x * rs, rs


def _rms_bwd(du, n, rs, g):
    dn = du * g
    return rs * (dn - n * jnp.mean(dn * n, axis=-1, keepdims=True))


def _sigmoid(x):
    return 1.0 / (1.0 + jnp.exp(-x))


def _rope_tables():
    j = lax.broadcasted_iota(jnp.int32, (S, 128), 1) % 64
    pos = lax.broadcasted_iota(jnp.int32, (S, 128), 0).astype(F32)
    inv_freq = jnp.power(ROPE_THETA, -(j % 8).astype(F32) / 8.0)
    ang = pos * inv_freq
    cos, sin = jnp.cos(ang), jnp.sin(ang)
    c = jnp.where(j < 16, cos, 1.0)
    a = jnp.where(j < 8, -sin, 0.0)
    b = jnp.where((j >= 8) & (j < 16), sin, 0.0)
    return c, a, b


def _rope(blk, c, a, b):
    return blk * c + pltpu.roll(blk, 120, 1) * a + pltpu.roll(blk, 8, 1) * b


def _rope_bwd(d, c, a, b):
    return d * c + pltpu.roll(d * a, 8, 1) + pltpu.roll(d * b, 120, 1)


MESH = pl.DeviceIdType.MESH
ANY = pl.BlockSpec(memory_space=pl.ANY)
VM = pl.BlockSpec(memory_space=pltpu.VMEM)
FLIPS = [(dx, dy, dc) for dx in (0, 1) for dy in (0, 1) for dc in (0, 1)][1:]


def _place():
    return lax.axis_index("x"), lax.axis_index("y"), lax.axis_index("c")


def _dev_index(px, py, pc):
    return 4 * px + 2 * py + pc


def _gather_phases(ins, outs, bufs, send_sems=None, recv_sems=None, local_sems=None):
    nw = len(ins)
    if nw == 0:
        return (lambda: None,) * 3
    x, y, c = _place()
    me, sib = (x, y, c), (x, y, 1 - c)
    chips = [(1 - x, y), (x, 1 - y), (1 - x, 1 - y)]

    def copy(w, k, block, to, from_buf=False):
        dst = outs[w].at[_dev_index(*block)]
        return pltpu.make_async_remote_copy(
            src_ref=bufs[w] if from_buf else dst, dst_ref=dst, send_sem=send_sems.at[w, k],
            recv_sem=recv_sems.at[w, k], device_id=to, device_id_type=MESH)

    def mine(w):
        return pltpu.make_async_copy(bufs[w], outs[w].at[_dev_index(*me)], local_sems.at[w])

    def first(w):
        return [copy(w, 0, me, sib, True)] + [copy(w, 1 + j, me, (*chip, c), True) for j, chip in enumerate(chips)]

    def passed(w):
        return [copy(w, 4 + j, (*chip, c), sib) for j, chip in enumerate(chips)]

    def start():
        for w in range(nw):
            bufs[w][...] = ins[w][...].astype(bufs[w].dtype)
        for w in range(nw):
            mine(w).start()
            for cp in first(w):
                cp.start()

    def forward():
        for j, chip in enumerate(chips):
            for w in range(nw):
                copy(w, 1 + j, (*chip, c), me).wait_recv()
                passed(w)[j].start()

    def finish():
        for w in range(nw):
            copy(w, 0, sib, me).wait_recv()
        for j, chip in enumerate(chips):
            for w in range(nw):
                copy(w, 4 + j, (*chip, 1 - c), me).wait_recv()
        for w in range(nw):
            for cp in first(w) + passed(w):
                cp.wait_send()
            mine(w).wait()

    return start, forward, finish


def _gather_scratch(shards, dtypes):
    nw = len(shards)
    if nw == 0:
        return []
    return ([pltpu.VMEM(s.shape, dt) for s, dt in zip(shards, dtypes)]
            + [pltpu.SemaphoreType.DMA((nw, 7)), pltpu.SemaphoreType.DMA((nw, 7)), pltpu.SemaphoreType.DMA((nw,))])


def _gather_shapes(shards, dtypes):
    return [jax.ShapeDtypeStruct((NDEV, *s.shape), dt) for s, dt in zip(shards, dtypes)]


def _scatter_phases(ins, outs, send_sems=None, recv_sems=None, local_sems=None):
    nw = len(ins)
    if nw == 0:
        return (lambda: None,) * 2
    x, y, c = _place()
    me = _dev_index(x, y, c)

    def copies():
        out = []
        for w in range(nw):
            out.append(pltpu.make_async_copy(ins[w].at[me], outs[w].at[me], local_sems.at[w]))
            for k, (dx, dy, dc) in enumerate(FLIPS):
                peer = ((x + dx) % 2, (y + dy) % 2, (c + dc) % 2)
                out.append(pltpu.make_async_remote_copy(
                    src_ref=ins[w].at[_dev_index(*peer)], dst_ref=outs[w].at[me], send_sem=send_sems.at[w, k],
                    recv_sem=recv_sems.at[w, k], device_id=peer, device_id_type=MESH))
        return out

    def start():
        for cp in copies():
            cp.start()

    def finish():
        for cp in copies():
            cp.wait()

    return start, finish


def _scatter_scratch(nw):
    if nw == 0:
        return []
    return [pltpu.SemaphoreType.DMA((nw, 7)), pltpu.SemaphoreType.DMA((nw, 7)), pltpu.SemaphoreType.DMA((nw,))]


TM = 512


def _join_w_in(wg):
    sw = IN_W // NDEV

    def body(wg_ref, w_ref):
        for j in range(NDEV):
            w_ref[:, sw * j:sw * (j + 1)] = wg_ref[j]
        w_ref[:, IN_W:PW] = jnp.zeros((D, PW - IN_W), BF16)

    return pl.pallas_call(body, name="join_w_in", out_shape=jax.ShapeDtypeStruct((D, PW), BF16),
                          compiler_params=_params())(wg)


def _in_proj(x, g1, w, rc, ra, rb):
    tm = TM

    def body(x_ref, g_ref, w_ref, rc_ref, ra_ref, rb_ref, qkv_ref, mqk_ref, mv_ref, mo_ref, gt_ref, u_ref):
        n, _ = _rms(x_ref[...])
        u = _bf(n * g_ref[...])
        u_ref[...] = u
        c, a, b = rc_ref[...], ra_ref[...], rb_ref[...]
        for half in range(2):
            blk = _dot(u, w_ref[:, half * 512:(half + 1) * 512])
            for t in range(4):
                lo = half * 512 + t * 128
                qkv_ref[:, lo:lo + 128] = _rope(blk[:, t * 128:(t + 1) * 128], c, a, b)
        qkv_ref[:, 1024:1536] = _dot(u, w_ref[:, 1024:1536])
        mqk_ref[:, 0:512] = _dot(u, w_ref[:, 1536:2048])
        mqk_ref[:, 512:1024] = _dot(u, w_ref[:, 2048:2560])
        mv_ref[...] = _dot(u, w_ref[:, 2560:3072])
        mo_ref[...] = _dot(u, w_ref[:, 3072:3584])
        gt_ref[...] = _dot(u, w_ref[:, 3584:3712])

    row = lambda wd: pl.BlockSpec((tm, wd), lambda i: (i, 0))
    return pl.pallas_call(
        body, name="in_proj", grid=(S // tm,),
        in_specs=[row(D), _cspec((1, D)), _cspec((D, PW)), row(128), row(128), row(128)],
        out_specs=[row(1536), row(1024), row(512), row(512), row(128), row(D)],
        out_shape=[jax.ShapeDtypeStruct((S, 1536), F32), jax.ShapeDtypeStruct((S, 1024), F32),
                   jax.ShapeDtypeStruct((S, 512), F32), jax.ShapeDtypeStruct((S, 512), F32),
                   jax.ShapeDtypeStruct((S, 128), F32), jax.ShapeDtypeStruct((S, D), BF16)],
        compiler_params=_params(1),
    )(x, g1, w, rc, ra, rb)


DILATIONS = (16, 4, 1)


def _attn_valid(n):
    kd = lax.broadcasted_iota(jnp.int32, (128, 256), 1) - lax.broadcasted_iota(jnp.int32, (128, 256), 0)
    off = jnp.where(n == 0, 0, 128)
    return (kd <= off) & (kd >= off - 128)


def _attn_rows(d, r, n):
    if d == 1:
        q0 = pl.multiple_of(n * 128, 128)
        k0 = pl.multiple_of(jnp.maximum(n - 1, 0) * 128, 128)
        return pl.ds(q0, 128), pl.ds(k0, 256), _attn_valid(n)
    q0 = r + n * 128 * d
    k0 = r + jnp.maximum(n - 1, 0) * 128 * d
    return pl.ds(q0, 128, stride=d), pl.ds(k0, 256, stride=d), _attn_valid(n)


ATTN_GROUP = 4
ATTN_ITERS = S // 128 // ATTN_GROUP


def _attn_group(d, i):
    nb = S // (128 * d)
    if nb == 2:
        qi = lax.broadcasted_iota(jnp.int32, (256, 256), 0) - lax.broadcasted_iota(jnp.int32, (256, 256), 1)
        whole = [pl.ds((ATTN_GROUP // 2) * i + u, 256, stride=d) for u in range(ATTN_GROUP // 2)]
        return [(rows, rows, (qi >= 0) & (qi <= 128)) for rows in whole]
    if d == 1:
        return [_attn_rows(1, 0, i + ATTN_ITERS * u) for u in range(ATTN_GROUP)]
    return [_attn_rows(d, (i // nb) * ATTN_GROUP + u, i % nb) for u in range(ATTN_GROUP)]


def _head0(shape):
    return lax.broadcasted_iota(jnp.int32, shape, 1) < 64


def _stack_heads(t):
    h0 = _head0(t.shape)
    tb = _bf(t)
    zero = jnp.zeros_like(tb)
    return jnp.concatenate([jnp.where(h0, tb, zero), jnp.where(h0, zero, tb)], axis=0)


def _attn_fwd(qkv, shards, dtypes):
    nw = len(shards)

    def body(*refs):
        q_ref, k_ref, v_ref = refs[:3]
        ins = refs[3:3 + nw]
        o_ref, lse_ref = refs[3 + nw:5 + nw]
        outs = refs[5 + nw:5 + 2 * nw]
        m0, m1, l0, l1, acc = refs[5 + 2 * nw:10 + 2 * nw]
        bufs = refs[10 + 2 * nw:10 + 3 * nw]
        ag_start, ag_forward, ag_finish = _gather_phases(ins, outs, bufs, *refs[10 + 3 * nw:])
        hp = pl.program_id(0)
        pl.when(hp == 0)(ag_start)
        pl.when(hp == 3)(ag_forward)
        stats = (m0, m1, l0, l1, acc)

        def update(blocks, first):
            loaded = [([q_ref[rq, :], k_ref[rk, :], v_ref[rk, :]], None if first else [ref[rq, :] for ref in stats])
                      for rq, rk, _ in blocks]
            results = []
            for ((q, k, v), prev), (_, _, valid) in zip(loaded, blocks):
                head0 = _head0(q.shape)
                kb, vb = _bf(k), _bf(v)
                q = q * 0.125
                m_new, l_new, acc_new = [], [], []
                for a, qa in enumerate((_bf(jnp.where(head0, q, 0.0)), _bf(jnp.where(head0, 0.0, q)))):
                    s = jnp.where(valid, _dot_nt(qa, kb), NEG)
                    mc = jnp.max(s, axis=-1, keepdims=True)
                    m_a = jnp.broadcast_to(mc, q.shape) if first else jnp.maximum(prev[a], mc)
                    p = jnp.exp(s - jnp.tile(m_a, (1, 2)))
                    l_add = jnp.sum(p, axis=-1, keepdims=True)
                    pv = _dot(_bf(p), vb)
                    if first:
                        l_a = jnp.broadcast_to(l_add, q.shape)
                    else:
                        alpha = jnp.exp(prev[a] - m_a)
                        l_a, pv = alpha * prev[2 + a] + l_add, alpha * prev[4] + pv
                    m_new.append(m_a), l_new.append(l_a), acc_new.append(pv)
                results.append((m_new[0], m_new[1], l_new[0], l_new[1], jnp.where(head0, acc_new[0], acc_new[1])))
            for (rq, _, _), res in zip(blocks, results):
                for ref, val in zip(stats, res):
                    ref[rq, :] = val

        for d in DILATIONS:
            def step(i, carry, d=d):
                update(_attn_group(d, i), d == DILATIONS[0])
                return carry

            lax.fori_loop(0, ATTN_ITERS, step, 0)

        def fin(t, carry):
            rows = pl.ds(pl.multiple_of(t * 256, 256), 256)
            h0 = lax.broadcasted_iota(jnp.int32, (256, 128), 1) < 64
            l = jnp.where(h0, l0[rows, :], l1[rows, :])
            o_ref[rows, :] = acc[rows, :] / l
            lse_ref[rows, :] = jnp.where(h0, m0[rows, :], m1[rows, :]) + jnp.log(l)
            return carry

        lax.fori_loop(0, S // 256, fin, 0)
        pl.when(hp == 3)(ag_finish)

    col = lambda off: pl.BlockSpec((S, 128), lambda h, off=off: (0, off + h))
    res = pl.pallas_call(
        body, name="attn_fwd", grid=(4,),
        in_specs=[col(0), col(4), col(8)] + [VM] * nw,
        out_specs=[col(0), col(0)] + [ANY] * nw,
        out_shape=[jax.ShapeDtypeStruct((S, AW), F32), jax.ShapeDtypeStruct((S, AW), F32)]
        + _gather_shapes(shards, dtypes),
        scratch_shapes=[pltpu.VMEM((S, 128), F32)] * 5 + _gather_scratch(shards, dtypes),
        compiler_params=_params(1),
    )(qkv, qkv, qkv, *shards)
    return res[0], res[1], res[2:]


def _attn_bwd(qkv, o, lse, do, parts):
    nw = len(parts)

    def body(*refs):
        q_ref, k_ref, v_ref, o_ref, lse_ref, do_ref = refs[:6]
        ins = refs[6:6 + nw]
        dq_ref, dk_ref, dv_ref = refs[6 + nw:9 + nw]
        outs = refs[9 + nw:9 + 2 * nw]
        L0, L1, D0, D1 = refs[9 + 2 * nw:13 + 2 * nw]
        rs_start, rs_finish = _scatter_phases(ins, outs, *refs[13 + 2 * nw:])
        hp = pl.program_id(0)
        pl.when(hp == 0)(rs_start)
        def pre(t, carry):
            rows = pl.ds(pl.multiple_of(t * 256, 256), 256)
            h0 = lax.broadcasted_iota(jnp.int32, (256, 128), 1) < 64
            ls = lse_ref[rows, :]
            dd = do_ref[rows, :] * o_ref[rows, :]
            shp = (256, 128)
            L0[rows, :] = jnp.broadcast_to(jnp.max(jnp.where(h0, ls, NEG), axis=-1, keepdims=True), shp)
            L1[rows, :] = jnp.broadcast_to(jnp.max(jnp.where(h0, NEG, ls), axis=-1, keepdims=True), shp)
            D0[rows, :] = jnp.broadcast_to(jnp.sum(jnp.where(h0, dd, 0.0), axis=-1, keepdims=True), shp)
            D1[rows, :] = jnp.broadcast_to(jnp.sum(jnp.where(h0, 0.0, dd), axis=-1, keepdims=True), shp)
            return carry

        lax.fori_loop(0, S // 256, pre, 0)

        def update(blocks, first):
            loaded = [([q_ref[rq, :], k_ref[rk, :], v_ref[rk, :], do_ref[rq, :]],
                       [L0[rq, :], L1[rq, :], D0[rq, :], D1[rq, :]],
                       [0.0] * 3 if first else [dq_ref[rq, :], dk_ref[rk, :], dv_ref[rk, :]]) for rq, rk, _ in blocks]
            results = []
            for ((q, k, v, dout), (l0v, l1v, d0v, d1v), (dq, dk, dv)), (_, _, valid) in zip(loaded, blocks):
                valid = jnp.tile(valid, (1, 2))
                kst, vst = _stack_heads(k), _stack_heads(v)
                hk = _head0((256, 128))
                dob = _bf(dout)
                cat = lambda a, b: jnp.concatenate([jnp.tile(a, (1, 2)), jnp.tile(b, (1, 2))], axis=1)
                s = jnp.where(valid, _dot_nt(_bf(q * 0.125), kst), NEG)
                p = jnp.exp(s - cat(l0v, l1v))
                ds = _bf(p * (_dot_nt(dob, vst) - cat(d0v, d1v)) * 0.125)
                dk2 = _dot_tn(ds, _bf(q))
                dv2 = _dot_tn(_bf(p), dob)
                results.append((dq + _dot(ds, kst), dk + jnp.where(hk, dk2[0:256], dk2[256:512]),
                                dv + jnp.where(hk, dv2[0:256], dv2[256:512])))
            for (rq, rk, _), (dq, dk, dv) in zip(blocks, results):
                dq_ref[rq, :] = dq
                dk_ref[rk, :] = dk
                dv_ref[rk, :] = dv

        assert S // (128 * DILATIONS[0]) == 2
        for d in DILATIONS:
            def step(i, carry, d=d):
                update(_attn_group(d, i), d == DILATIONS[0])
                return carry

            lax.fori_loop(0, ATTN_ITERS, step, 0)
        pl.when(hp == 3)(rs_finish)

    col = lambda off: pl.BlockSpec((S, 128), lambda h, off=off: (0, off + h))
    res = pl.pallas_call(
        body, name="attn_bwd", grid=(4,),
        in_specs=[col(0), col(4), col(8), col(0), col(0), col(0)] + [ANY] * nw,
        out_specs=[col(0), col(0), col(0)] + [ANY] * nw,
        out_shape=[jax.ShapeDtypeStruct((S, AW), F32)] * 3 + [jax.ShapeDtypeStruct(a.shape, a.dtype) for a in parts],
        scratch_shapes=[pltpu.VMEM((S, 128), F32)] * 4 + _scatter_scratch(nw),
        compiler_params=_params(1),
    )(qkv, qkv, qkv, o, lse, do, *parts)
    return res[0], res[1], res[2], res[3:]


def _logsig(x):
    return jnp.minimum(x, 0.0) - jnp.log1p(jnp.exp(-jnp.abs(x)))


def _conv_taps(xp, n):
    return [xp[8:] if j == 3 else pltpu.roll(xp, 3 - j, 0)[8:] for j in range(4)]


def _conv_silu(xp, w_ref, b_ref, n):
    taps = _conv_taps(xp, n)
    c = b_ref[...] + sum(w_ref[j:j + 1, :] * taps[j] for j in range(4))
    sg = _sigmoid(c)
    return c, sg, taps


def _chunk_gates(G):
    assert LC == 128
    r = lax.broadcasted_iota(jnp.int32, (LC, LC), 0)
    c = lax.broadcasted_iota(jnp.int32, (LC, LC), 1)
    tril = (c <= r).astype(F32)
    triu = (c >= r).astype(F32)
    b_col = jnp.dot(tril, _logsig(G), preferred_element_type=F32, precision=HI)
    return b_col, b_col.T, G.T, tril, triu


def _colpick(X, lane):
    li = lax.broadcasted_iota(jnp.int32, X.shape, 1)
    return jnp.sum(jnp.where(li == lane, X, 0.0), axis=1, keepdims=True)


def _rowpick(XT, row):
    ri = lax.broadcasted_iota(jnp.int32, XT.shape, 0)
    return jnp.sum(jnp.where(ri == row, XT, 0.0), axis=0, keepdims=True)


def _mlstm_head(qh, kh, vh, G, b_col, b_row, g_row, h, Ch, nh, m_prev):
    bt = _colpick(b_col, 4 + h)
    i_col = _colpick(G, h)
    bs = _rowpick(b_row, 4 + h)
    i_row = _rowpick(g_row, h)
    r = lax.broadcasted_iota(jnp.int32, (LC, LC), 0)
    c = lax.broadcasted_iota(jnp.int32, (LC, LC), 1)
    log_d = jnp.where(c <= r, bt - bs + i_row, NEG)
    log_inter = bt + m_prev
    m_t = jnp.maximum(log_inter, jnp.max(log_d, axis=1, keepdims=True))
    Dm = jnp.exp(log_d - m_t)
    g = jnp.exp(log_inter - m_t)
    qb, kb, vb = _bf(qh), _bf(kh), _bf(vh)
    Am = _dot_nt(qb, kb) * Dm
    qC = _dot(qb, _bf(Ch))
    num = g * qC + _dot(_bf(Am), vb)
    qn = jnp.sum(qh * nh, axis=1, keepdims=True)
    den = g * qn + jnp.sum(Am, axis=1, keepdims=True)
    floor = jnp.exp(-m_t)
    dd = jnp.maximum(jnp.abs(den), floor)
    inv_dd = 1.0 / dd
    hh = num * inv_dd
    lane = lax.broadcasted_iota(jnp.int32, (1, LC), 1)
    blast = jnp.sum(jnp.where(lane == LC - 1, bs, 0.0), axis=1, keepdims=True)
    log_s = blast - bt + i_col
    m_new = jnp.maximum(blast + m_prev, jnp.max(log_s, axis=0, keepdims=True))
    decay = jnp.exp(blast + m_prev - m_new)
    ws = jnp.exp(log_s - m_new)
    kw = kh * ws
    C_new = decay * Ch + _dot_tn(_bf(kw), vb)
    n_new = decay * nh + jnp.sum(kw, axis=0, keepdims=True)
    return dict(Dm=Dm, g=g, Am=Am, qC=qC, qn=qn, den=den, floor=floor, inv_dd=inv_dd, h=hh, decay=decay, ws=ws, kw=kw,
                C_new=C_new, n_new=n_new, m_new=m_new, qb=qb, kb=kb, vb=vb)


def _head_out(hh, mo_h, gn_h):
    r = lax.rsqrt(jnp.mean(hh * hh, axis=-1, keepdims=True) + EPS)
    hn = hh * r
    sg = _sigmoid(mo_h)
    return sg * (hn * gn_h), hn, r, sg


def _mlstm_fwd(mqk, mv, mo, gates, conv_w, conv_b, gate_b, gn, shards, dtypes):
    nblk = S // TB
    ncb = TB // LC
    nw = len(shards)

    def body(*refs):
        x_ref, v_ref, o_ref, g_ref, w_ref, b_ref, gb_ref, gn_ref = refs[:8]
        ins = refs[8:8 + nw]
        out_ref, cs_ref, ns_ref, ms_ref = refs[8 + nw:12 + nw]
        outs = refs[12 + nw:12 + 2 * nw]
        tail, Cst, nst, mst, qs, ks = refs[12 + 2 * nw:18 + 2 * nw]
        bufs = refs[18 + 2 * nw:18 + 3 * nw]
        ag_start, ag_forward, ag_finish = _gather_phases(ins, outs, bufs, *refs[18 + 3 * nw:])
        i = pl.program_id(0)
        pl.when(i == 0)(ag_start)
        pl.when(i == nblk // 2)(ag_forward)

        @pl.when(i == 0)
        def _():
            tail[...] = jnp.zeros_like(tail)
            Cst[...] = jnp.zeros_like(Cst)
            nst[...] = jnp.zeros_like(nst)
            mst[...] = jnp.zeros_like(mst)

        x = x_ref[...]
        xp = jnp.concatenate([tail[...], x], axis=0)
        tail[...] = x[TB - 8:TB, :]
        c, sg, _ = _conv_silu(xp, w_ref, b_ref, TB)
        y = c * sg
        qs[...] = y[:, 0:MW]
        ks[...] = y[:, MW:2 * MW] * (1.0 / math.sqrt(128.0))

        for cc in range(ncb):
            rows = slice(cc * LC, (cc + 1) * LC)
            G = g_ref[rows, :] + gb_ref[...]
            b_col, b_row, g_row, _, _ = _chunk_gates(G)
            cs_ref[cc] = Cst[...]
            ns_ref[cc] = nst[...]
            ms_ref[cc] = mst[...]
            for h in range(4):
                ln = slice(h * 128, (h + 1) * 128)
                m_prev = jnp.max(mst[0:1, ln], axis=1, keepdims=True)
                f = _mlstm_head(qs[rows, ln], ks[rows, ln], v_ref[rows, ln], G, b_col, b_row, g_row, h,
                                Cst[:, ln], nst[0:1, ln], m_prev)
                out, _, _, _ = _head_out(f["h"], o_ref[rows, ln], gn_ref[:, ln])
                out_ref[rows, ln] = out
                Cst[:, ln] = f["C_new"]
                nst[0:1, ln] = f["n_new"]
                mst[0:1, ln] = jnp.broadcast_to(f["m_new"], (1, 128))
        pl.when(i == nblk - 1)(ag_finish)

    row = lambda wd: pl.BlockSpec((TB, wd), lambda i: (i, 0))
    res = pl.pallas_call(
        body, name="mlstm_fwd", grid=(nblk,),
        in_specs=[row(1024), row(MW), row(MW), row(128), _cspec((4, 1024)), _cspec((1, 1024)), _cspec((1, 128)),
                  _cspec((1, MW))] + [VM] * nw,
        out_specs=[row(MW), pl.BlockSpec((ncb, 128, MW), lambda i: (i, 0, 0)),
                   pl.BlockSpec((ncb, 8, MW), lambda i: (i, 0, 0)), pl.BlockSpec((ncb, 8, MW), lambda i: (i, 0, 0))]
        + [ANY] * nw,
        out_shape=[jax.ShapeDtypeStruct((S, MW), F32), jax.ShapeDtypeStruct((S // LC, 128, MW), F32),
                   jax.ShapeDtypeStruct((S // LC, 8, MW), F32), jax.ShapeDtypeStruct((S // LC, 8, MW), F32)]
        + _gather_shapes(shards, dtypes),
        scratch_shapes=[pltpu.VMEM((8, 1024), F32), pltpu.VMEM((128, MW), F32), pltpu.VMEM((8, MW), F32),
                        pltpu.VMEM((8, MW), F32), pltpu.VMEM((TB, MW), F32), pltpu.VMEM((TB, MW), F32)]
        + _gather_scratch(shards, dtypes),
        compiler_params=_params(1),
    )(mqk, mv, mo, gates, conv_w, conv_b, gate_b, gn, *shards)
    return res[0], res[1], res[2], res[3], res[4:]


def _mlstm_bwd(mqk, mv, mo, gates, conv_w, conv_b, gate_b, gn, cs, ns, ms, dout, parts):
    nblk = S // TB
    ncb = TB // LC
    kscale = 1.0 / math.sqrt(128.0)
    nw = len(parts)

    def body(*refs):
        x_ref, xprev_ref, v_ref, o_ref, g_ref, w_ref, b_ref, gb_ref, gn_ref, cs_ref, ns_ref, ms_ref, do_ref = refs[:13]
        ins = refs[13:13 + nw]
        dx_ref, dv_ref, dmo_ref, dg_ref, dw_ref, db_ref, dgn_ref, dgb_ref = refs[13 + nw:21 + nw]
        outs = refs[21 + nw:21 + 2 * nw]
        dCst, dnst, dyhead, qs, ks, dqk = refs[21 + 2 * nw:27 + 2 * nw]
        rs_start, rs_finish = _scatter_phases(ins, outs, *refs[27 + 2 * nw:])
        i = pl.program_id(0)
        blk = nblk - 1 - i
        pl.when(i == 0)(rs_start)

        @pl.when(i == 0)
        def _():
            dCst[...] = jnp.zeros_like(dCst)
            dnst[...] = jnp.zeros_like(dnst)
            dyhead[...] = jnp.zeros_like(dyhead)
            dw_ref[...] = jnp.zeros_like(dw_ref)
            db_ref[...] = jnp.zeros_like(db_ref)
            dgn_ref[...] = jnp.zeros_like(dgn_ref)
            dgb_ref[...] = jnp.zeros_like(dgb_ref)

        x = x_ref[...]
        xprev = jnp.where(blk == 0, 0.0, xprev_ref[...])
        xp = jnp.concatenate([xprev, x], axis=0)
        c, sg, taps = _conv_silu(xp, w_ref, b_ref, TB)
        y = c * sg
        qs[...] = y[:, 0:MW]
        ks[...] = y[:, MW:2 * MW] * kscale
        lane128 = lax.broadcasted_iota(jnp.int32, (LC, 128), 1)
        rowi = lax.broadcasted_iota(jnp.int32, (LC, 1), 0)
        ones = jnp.ones((LC, 128), F32)

        for cc in reversed(range(ncb)):
            rows = slice(cc * LC, (cc + 1) * LC)
            G = g_ref[rows, :] + gb_ref[...]
            b_col, b_row, g_row, _, triu = _chunk_gates(G)
            dB = jnp.zeros((LC, 128), F32)
            dI = jnp.zeros((LC, 128), F32)
            for h in range(4):
                ln = slice(h * 128, (h + 1) * 128)
                Ch = cs_ref[cc, :, ln]
                nh = ns_ref[cc, 0:1, ln]
                m_prev = jnp.max(ms_ref[cc, 0:1, ln], axis=1, keepdims=True)
                qh, kh, vh = qs[rows, ln], ks[rows, ln], v_ref[rows, ln]
                f = _mlstm_head(qh, kh, vh, G, b_col, b_row, g_row, h, Ch, nh, m_prev)
                hh, inv_dd, den, g, Am, Dm = f["h"], f["inv_dd"], f["den"], f["g"], f["Am"], f["Dm"]
                qb, kb, vb = f["qb"], f["kb"], f["vb"]
                gn_h = gn_ref[:, ln]
                _, hn, r, sgo = _head_out(hh, o_ref[rows, ln], gn_h)
                do = do_ref[rows, ln]
                hm = hn * gn_h
                dmo_ref[rows, ln] = do * hm * sgo * (1.0 - sgo)
                dhm = do * sgo
                dgn_ref[:, ln] = dgn_ref[:, ln] + jnp.sum(dhm * hn, axis=0, keepdims=True)
                dhn = dhm * gn_h
                dh = r * (dhn - hn * jnp.mean(dhn * hn, axis=-1, keepdims=True))
                dnum = dh * inv_dd
                ddd = -jnp.sum(dh * hh, axis=1, keepdims=True) * inv_dd
                dden = jnp.where(jnp.abs(den) >= f["floor"], ddd * jnp.sign(den), 0.0)
                dnb = _bf(dnum)
                dA = _dot_nt(dnb, vb) + dden
                dv = _dot_tn(_bf(Am), dnb)
                gd = _bf(g * dnum)
                gq = g * dden
                dq = _dot_nt(gd, _bf(Ch)) + gq * nh
                dCn = dCst[:, ln]
                dnn = dnst[0:1, ln]
                dC = f["decay"] * dCn + _dot_tn(qb, gd)
                dn = f["decay"] * dnn + jnp.sum(gq * qh, axis=0, keepdims=True)
                dg = jnp.sum(dnum * f["qC"], axis=1, keepdims=True) + dden * f["qn"]
                dS = _bf(dA * Dm)
                dq = dq + _dot(dS, kb)
                dk = _dot_tn(dS, qb)
                Gm = dA * Am
                gam = dg * g
                dCb = _bf(dCn)
                E = _dot_nt(vb, dCb) + dnn
                ws = f["ws"]
                dk = dk + ws * E
                om = jnp.sum(E * kh, axis=1, keepdims=True) * ws
                dv = dv + _dot(_bf(f["kw"]), dCb)
                ddecay = (jnp.sum(jnp.sum(dCn * Ch, axis=1, keepdims=True), axis=0, keepdims=True)
                          + jnp.sum(dnn * nh, axis=1, keepdims=True))
                delta = ddecay * f["decay"]
                rows_g = jnp.sum(Gm, axis=1, keepdims=True)
                cols_g = jnp.broadcast_to(jnp.sum(Gm, axis=0, keepdims=True), (LC, 128)).T
                last = jnp.where(rowi == LC - 1, jnp.sum(om, axis=0, keepdims=True) + delta, 0.0)
                db = rows_g + gam - om + last - cols_g
                di = cols_g + om
                dB = jnp.where(lane128 == 4 + h, db, dB)
                dI = jnp.where(lane128 == h, di, dI)
                dCst[:, ln] = dC
                dnst[0:1, ln] = dn
                dqk[rows, ln] = dq
                dqk[rows, MW + h * 128:MW + (h + 1) * 128] = dk * kscale
                dv_ref[rows, ln] = dv
            dlogf = jnp.dot(triu, dB, preferred_element_type=F32, precision=HI)
            dG = dI + dlogf * _sigmoid(-G)
            dG = jnp.where(lane128 < 8, dG, 0.0)
            dg_ref[rows, :] = dG
            dgb_ref[...] = dgb_ref[...] + jnp.sum(dG, axis=0, keepdims=True)

        dy = dqk[...] * (sg * (1.0 + c * (1.0 - sg)))
        db_ref[...] = db_ref[...] + jnp.sum(dy, axis=0, keepdims=True)
        for j in range(4):
            dw_ref[j:j + 1, :] = dw_ref[j:j + 1, :] + jnp.sum(dy * taps[j], axis=0, keepdims=True)
        dyp = jnp.concatenate([dy, dyhead[...]], axis=0)
        dx = w_ref[3:4, :] * dy
        for j in range(3):
            dx = dx + w_ref[j:j + 1, :] * pltpu.roll(dyp, TB + 8 - (3 - j), 0)[0:TB]
        dx_ref[...] = dx
        dyhead[...] = dy[0:8, :]
        pl.when(i == nblk - 1)(rs_finish)

    rrow = lambda wd: pl.BlockSpec((TB, wd), lambda i: (nblk - 1 - i, 0))
    st = lambda r: pl.BlockSpec((ncb, r, MW), lambda i: (nblk - 1 - i, 0, 0))
    prev8 = pl.BlockSpec((8, 1024), lambda i: (jnp.maximum((nblk - 1 - i) * (TB // 8) - 1, 0), 0))
    res = pl.pallas_call(
        body, name="mlstm_bwd", grid=(nblk,),
        in_specs=[rrow(1024), prev8, rrow(MW), rrow(MW), rrow(128), _cspec((4, 1024)), _cspec((1, 1024)),
                  _cspec((1, 128)), _cspec((1, MW)), st(128), st(8), st(8), rrow(MW)] + [ANY] * nw,
        out_specs=[rrow(1024), rrow(MW), rrow(MW), rrow(128),
                   pl.BlockSpec((4, 1024), lambda i: (0, 0)), pl.BlockSpec((1, 1024), lambda i: (0, 0)),
                   pl.BlockSpec((1, MW), lambda i: (0, 0)), pl.BlockSpec((1, 128), lambda i: (0, 0))] + [ANY] * nw,
        out_shape=[jax.ShapeDtypeStruct((S, 1024), F32), jax.ShapeDtypeStruct((S, MW), F32),
                   jax.ShapeDtypeStruct((S, MW), F32), jax.ShapeDtypeStruct((S, 128), F32),
                   jax.ShapeDtypeStruct((4, 1024), F32), jax.ShapeDtypeStruct((1, 1024), F32),
                   jax.ShapeDtypeStruct((1, MW), F32), jax.ShapeDtypeStruct((1, 128), F32)]
        + [jax.ShapeDtypeStruct(a.shape, a.dtype) for a in parts],
        scratch_shapes=[pltpu.VMEM((128, MW), F32), pltpu.VMEM((8, MW), F32), pltpu.VMEM((8, 1024), F32),
                        pltpu.VMEM((TB, MW), F32), pltpu.VMEM((TB, MW), F32), pltpu.VMEM((TB, 1024), F32)]
        + _scatter_scratch(nw),
        compiler_params=_params(1),
    )(mqk, mqk, mv, mo, gates, conv_w, conv_b, gate_b, gn, cs, ns, ms, dout, *parts)
    return res[:8], res[8:]


def _out_proj(x, attn, ml, w, g):
    tm = TM

    def body(x_ref, a_ref, m_ref, w_ref, g_ref, h_ref, u_ref):
        h1 = x_ref[...] + _dot(_bf(a_ref[...]), w_ref[0:AW, :]) + _dot(_bf(m_ref[...]), w_ref[AW:D, :])
        h_ref[...] = h1
        n, _ = _rms(h1)
        u_ref[...] = _bf(n * g_ref[...])

    row = lambda wd: pl.BlockSpec((tm, wd), lambda i: (i, 0))
    return pl.pallas_call(
        body, name="out_proj", grid=(S // tm,),
        in_specs=[row(D), row(AW), row(MW), _cspec((D, D)), _cspec((1, D))],
        out_specs=[row(D), row(D)],
        out_shape=[jax.ShapeDtypeStruct((S, D), F32), jax.ShapeDtypeStruct((S, D), BF16)],
        compiler_params=_params(1),
    )(x, attn, ml, w, g)


def _mlp_fwd(h1, u2, w_up, w_down):
    tm = TM

    def body(h_ref, u_ref, wu_ref, wd_ref, a_ref, o_ref):
        u = u_ref[...]
        acc = h_ref[...]
        for c in range(NDEV):
            cols = slice(c * 512, (c + 1) * 512)
            a = _dot(u, wu_ref[c])
            a_ref[:, cols] = _bf(a)
            r = jnp.maximum(a, 0.0)
            acc = acc + _dot(_bf(r * r), wd_ref[cols, :])
        o_ref[...] = acc

    row = lambda wd: pl.BlockSpec((tm, wd), lambda i: (i, 0))
    return pl.pallas_call(
        body, name="mlp_fwd", grid=(S // tm,),
        in_specs=[row(D), row(D), _cspec((NDEV, D, DFF // NDEV)), _cspec((DFF, D))],
        out_specs=[row(DFF), row(D)],
        out_shape=[jax.ShapeDtypeStruct((S, DFF), BF16), jax.ShapeDtypeStruct((S, D), F32)],
        compiler_params=_params(1),
    )(h1, u2, w_up, w_down)


def _ple_loss(h2, p, target, w_pg, w_ple, g_ple, g_fin):
    tm = TM

    def body(h_ref, p_ref, t_ref, wg_ref, wp_ref, gp_ref, gf_ref,
             dh_ref, dwg_ref, dwp_ref, dgp_ref, dgf_ref, loss_ref, acc_g, acc_p):
        i = pl.program_id(0)

        @pl.when(i == 0)
        def _():
            acc_g[...] = jnp.zeros_like(acc_g)
            acc_p[...] = jnp.zeros_like(acc_p)
            dgp_ref[...] = jnp.zeros_like(dgp_ref)
            dgf_ref[...] = jnp.zeros_like(dgf_ref)
            loss_ref[...] = jnp.zeros_like(loss_ref)

        h2v = h_ref[...]
        n2, rs2 = _rms(h2v)
        u3 = _bf(n2 * gp_ref[...])
        gt = _sigmoid(_dot(u3, wg_ref[...]))
        pb = _bf(p_ref[...])
        e = jnp.concatenate([_dot(pb, wp_ref[j]) for j in range(NDEV)], axis=1)
        h3 = h2v + gt * e
        n3, rs3 = _rms(h3)
        err = n3 * gf_ref[...] - t_ref[...]
        loss_ref[...] = loss_ref[...] + 0.5 / D * jnp.sum(jnp.sum(err * err, axis=1, keepdims=True), axis=0, keepdims=True)
        dy = err * (1.0 / D)
        dgf_ref[...] = dgf_ref[...] + jnp.sum(dy * n3, axis=0, keepdims=True)
        dh3 = _rms_bwd(dy, n3, rs3, gf_ref[...])
        de = _bf(dh3 * gt)
        dz = _bf(dh3 * e * gt * (1.0 - gt))
        acc_p[...] = acc_p[...] + _dot_tn(pb, de)
        acc_g[...] = acc_g[...] + _dot_tn(u3, dz)
        du3 = _dot_nt(dz, wg_ref[...])
        dgp_ref[...] = dgp_ref[...] + jnp.sum(du3 * n2, axis=0, keepdims=True)
        dh_ref[...] = dh3 + _rms_bwd(du3, n2, rs2, gp_ref[...])

        @pl.when(i == S // tm - 1)
        def _():
            dwg_ref[...] = _bf(acc_g[...])
            for j in range(NDEV):
                dwp_ref[j] = _bf(acc_p[:, j * 128:(j + 1) * 128])

    row = lambda wd: pl.BlockSpec((tm, wd), lambda i: (i, 0))
    whole = lambda shp: pl.BlockSpec(shp, lambda i: (0,) * len(shp))
    return pl.pallas_call(
        body, name="ple_loss", grid=(S // tm,),
        in_specs=[row(D), row(PLE), row(D), _cspec((D, D)), _cspec((NDEV, PLE, 128)), _cspec((1, D)), _cspec((1, D))],
        out_specs=[row(D), whole((D, D)), whole((NDEV, PLE, 128)), whole((1, D)), whole((1, D)), whole((1, 1))],
        out_shape=[jax.ShapeDtypeStruct((S, D), F32), jax.ShapeDtypeStruct((D, D), BF16),
                   jax.ShapeDtypeStruct((NDEV, PLE, 128), BF16), jax.ShapeDtypeStruct((1, D), F32),
                   jax.ShapeDtypeStruct((1, D), F32), jax.ShapeDtypeStruct((1, 1), F32)],
        scratch_shapes=[pltpu.VMEM((D, D), F32), pltpu.VMEM((PLE, D), F32)],
        compiler_params=_params(1),
    )(h2, p, target, w_pg, w_ple, g_ple, g_fin)


def _mlp_bwd(dh2, a, h1, g, w_up, w_down):
    tm = TM

    def body(d_ref, a_ref, h_ref, g_ref, wu_ref, wd_ref, da_ref, dh1_ref, dg_ref):
        @pl.when(pl.program_id(0) == 0)
        def _():
            dg_ref[...] = jnp.zeros_like(dg_ref)

        dh2v = d_ref[...]
        db = _bf(dh2v)
        du = jnp.zeros((tm, D), F32)
        for c in range(NDEV):
            cols = slice(c * 512, (c + 1) * 512)
            dr = _dot_nt(db, wd_ref[cols, :])
            da = _bf(dr * (2.0 * jnp.maximum(a_ref[:, cols], 0.0)))
            da_ref[:, cols] = da
            du = du + _dot_nt(da, wu_ref[c])
        n, rs = _rms(h_ref[...])
        dg_ref[...] = dg_ref[...] + jnp.sum(du * n, axis=0, keepdims=True)
        dh1_ref[...] = dh2v + _rms_bwd(du, n, rs, g_ref[...])

    row = lambda wd: pl.BlockSpec((tm, wd), lambda i: (i, 0))
    return pl.pallas_call(
        body, name="mlp_bwd", grid=(S // tm,),
        in_specs=[row(D), row(DFF), row(D), _cspec((1, D)), _cspec((NDEV, D, DFF // NDEV)), _cspec((DFF, D))],
        out_specs=[row(DFF), row(D), pl.BlockSpec((1, D), lambda i: (0, 0))],
        out_shape=[jax.ShapeDtypeStruct((S, DFF), BF16), jax.ShapeDtypeStruct((S, D), F32),
                   jax.ShapeDtypeStruct((1, D), F32)],
        compiler_params=_params(1),
    )(dh2, a, h1, g, w_up, w_down)


def _out_proj_bwd(dh1, attn, ml, w):
    tm = TM

    def body(d_ref, a_ref, m_ref, w_ref, da_ref, dm_ref, dw_ref, acc):
        i = pl.program_id(0)

        @pl.when(i == 0)
        def _():
            acc[...] = jnp.zeros_like(acc)

        db = _bf(d_ref[...])
        dmix = _dot_nt(db, w_ref[...])
        da_ref[...] = dmix[:, 0:AW]
        dm_ref[...] = dmix[:, AW:D]
        acc[0:AW, :] = acc[0:AW, :] + _dot_tn(_bf(a_ref[...]), db)
        acc[AW:D, :] = acc[AW:D, :] + _dot_tn(_bf(m_ref[...]), db)

        @pl.when(i == S // tm - 1)
        def _():
            dw_ref[...] = _bf(acc[...])

    row = lambda wd: pl.BlockSpec((tm, wd), lambda i: (i, 0))
    return pl.pallas_call(
        body, name="out_proj_bwd", grid=(S // tm,),
        in_specs=[row(D), row(AW), row(MW), _cspec((D, D))],
        out_specs=[row(AW), row(MW), pl.BlockSpec((D, D), lambda i: (0, 0))],
        out_shape=[jax.ShapeDtypeStruct((S, AW), F32), jax.ShapeDtypeStruct((S, MW), F32),
                   jax.ShapeDtypeStruct((D, D), BF16)],
        scratch_shapes=[pltpu.VMEM((D, D), F32)],
        compiler_params=_params(1),
    )(dh1, attn, ml, w)


def _in_proj_bwd(dq, dk, dv, dmqk, dmv, dmo, dgt, dh1, x, g1, w, rc, ra, rb):
    tm = TM

    def body(dq_ref, dk_ref, dv_ref, dmqk_ref, dmv_ref, dmo_ref, dgt_ref, dh_ref, x_ref, g_ref, w_ref,
             rc_ref, ra_ref, rb_ref, dp_ref, dx_ref, dg_ref):
        @pl.when(pl.program_id(0) == 0)
        def _():
            dg_ref[...] = jnp.zeros_like(dg_ref)

        c, a, b = rc_ref[...], ra_ref[...], rb_ref[...]
        for half, ref in enumerate((dq_ref, dk_ref)):
            for t in range(4):
                lo = half * 512 + t * 128
                dp_ref[:, lo:lo + 128] = _bf(_rope_bwd(ref[:, t * 128:(t + 1) * 128], c, a, b))
        dp_ref[:, 1024:1536] = _bf(dv_ref[...])
        dp_ref[:, 1536:2560] = _bf(dmqk_ref[...])
        dp_ref[:, 2560:3072] = _bf(dmv_ref[...])
        dp_ref[:, 3072:3584] = _bf(dmo_ref[...])
        dp_ref[:, 3584:3712] = _bf(dgt_ref[...])
        dp_ref[:, 3712:PW] = jnp.zeros((tm, PW - 3712), BF16)
        du = jnp.zeros((tm, D), F32)
        for s in range(PW // 768):
            cols = slice(s * 768, (s + 1) * 768)
            du = du + _dot_nt(dp_ref[:, cols], w_ref[:, cols])
        n, rs = _rms(x_ref[...])
        dg_ref[...] = dg_ref[...] + jnp.sum(du * n, axis=0, keepdims=True)
        dx_ref[...] = dh_ref[...] + _rms_bwd(du, n, rs, g_ref[...])

    row = lambda wd: pl.BlockSpec((tm, wd), lambda i: (i, 0))
    return pl.pallas_call(
        body, name="in_proj_bwd", grid=(S // tm,),
        in_specs=[row(AW), row(AW), row(AW), row(1024), row(MW), row(MW), row(128), row(D), row(D), _cspec((1, D)),
                  _cspec((D, PW)), row(128), row(128), row(128)],
        out_specs=[row(PW), row(D), pl.BlockSpec((1, D), lambda i: (0, 0))],
        out_shape=[jax.ShapeDtypeStruct((S, PW), BF16), jax.ShapeDtypeStruct((S, D), F32),
                   jax.ShapeDtypeStruct((1, D), F32)],
        compiler_params=_params(1),
    )(dq, dk, dv, dmqk, dmv, dmo, dgt, dh1, x, g1, w, rc, ra, rb)


SMALL_ROWS = 96


def _small_phases(ins, out_ref, pack, rbuf, send_sems, recv_sems):
    x, y, c = _place()
    me = _dev_index(x, y, c)

    def copies():
        out = []
        for k, (dx, dy, dc) in enumerate(FLIPS):
            peer = ((x + dx) % 2, (y + dy) % 2, (c + dc) % 2)
            out.append(pltpu.make_async_remote_copy(
                src_ref=pack, dst_ref=rbuf.at[me], send_sem=send_sems.at[k], recv_sem=recv_sems.at[k],
                device_id=peer, device_id_type=MESH))
        return out

    def start():
        pack[...] = jnp.zeros_like(pack)
        for i, ref in enumerate(ins):
            pack[8 * i:8 * i + 1, 0:ref.shape[1]] = ref[...]
        rbuf[me] = pack[...]
        for cp in copies():
            cp.start()

    def finish():
        for cp in copies():
            cp.wait()
        tot = rbuf[0]
        for j in range(1, NDEV):
            tot = tot + rbuf[j]
        out_ref[...] = tot

    return start, finish


def _wgrad(name, A, B, a_fn, b_fn, tk, tn, out_shape, out_spec, ts=512, split=None, small=()):
    K, N = A.shape[1], B.shape[1]
    nrt = S // ts
    nc = next(c for c in (1024, 1280, tn) if tn % c == 0)
    ns = len(small)
    grid = (N // tn, K // tk, nrt)

    def body(*refs):
        a_ref, b_ref = refs[:2]
        o_ref = refs[2 + ns]
        acc = refs[3 + ns + bool(ns)]
        r = pl.program_id(2)
        if ns:
            step = (pl.program_id(0) * grid[1] + pl.program_id(1)) * nrt + r
            sm_start, sm_finish = _small_phases(refs[2:2 + ns], refs[3 + ns], *refs[4 + ns + 1:])
            pl.when(step == 0)(sm_start)

        @pl.when(r == 0)
        def _():
            acc[...] = jnp.zeros_like(acc)

        at = a_fn(a_ref[...]).T
        for c in range(tn // nc):
            cols = slice(c * nc, (c + 1) * nc)
            acc[:, cols] = acc[:, cols] + _dot(at, b_fn(b_ref[:, cols]))

        @pl.when(r == nrt - 1)
        def _():
            if split is None:
                o_ref[...] = _bf(acc[...])
            else:
                for j in range(NDEV):
                    o_ref[j] = _bf(acc[:, split * j:split * (j + 1)])

        if ns:
            pl.when(step == grid[0] * grid[1] * nrt - 1)(sm_finish)

    in_specs = [pl.BlockSpec((ts, tk), lambda n, k, r: (r, k)), pl.BlockSpec((ts, tn), lambda n, k, r: (r, n))]
    scratch = [pltpu.VMEM((tk, tn), F32)]
    if not ns:
        return pl.pallas_call(
            body, name=name, grid=grid, in_specs=in_specs, out_specs=out_spec,
            out_shape=jax.ShapeDtypeStruct(out_shape, BF16), scratch_shapes=scratch, compiler_params=_params(3),
        )(A, B)
    return pl.pallas_call(
        body, name=name, grid=grid, in_specs=in_specs + [VM] * ns, out_specs=[out_spec, VM],
        out_shape=[jax.ShapeDtypeStruct(out_shape, BF16), jax.ShapeDtypeStruct((SMALL_ROWS, 1024), F32)],
        scratch_shapes=scratch + [pltpu.VMEM((SMALL_ROWS, 1024), F32), pltpu.VMEM((NDEV, SMALL_ROWS, 1024), F32),
                                  pltpu.SemaphoreType.DMA((7,)), pltpu.SemaphoreType.DMA((7,))],
        compiler_params=_params(3),
    )(A, B, *small)


def _relu2_bf(a):
    r = jnp.maximum(a.astype(F32), 0.0)
    return _bf(r * r)


def _ident(a):
    return a


def _step(x, p, target, g1, conv_b, gate_b, gn, g_mlp, g_ple, g_fin, sh):
    g_in, g_conv = _gather_weights([sh["w_in"], sh["conv_w"]], [BF16, F32])
    conv_w = g_conv.transpose(1, 0, 2).reshape(4, 1024)
    rc, ra, rb = _rope_tables()
    w_in_p = _join_w_in(g_in)
    qkv, mqk, mv, mo, gates, u1 = _in_proj(x, g1, w_in_p, rc, ra, rb)
    attn, lse, (w_up8, w_down8) = _attn_fwd(qkv, [sh["w_up"], sh["w_down"]], [BF16] * 2)
    ml, cs, ns, ms, (w_out8, w_pg8, w_ple8) = _mlstm_fwd(
        mqk, mv, mo, gates, conv_w, conv_b, gate_b, gn, [sh["w_out"], sh["w_ple_gate"], sh["w_ple"]], [BF16] * 3)
    w_out, w_down, w_pg = w_out8.reshape(D, D), w_down8.reshape(DFF, D), w_pg8.reshape(D, D)
    h1, u2 = _out_proj(x, attn, ml, w_out, g_mlp)
    a, h2 = _mlp_fwd(h1, u2, w_up8, w_down)
    dh2, dw_pg, dw_ple8, dg_ple, dg_fin, loss = _ple_loss(h2, p, target, w_pg, w_ple8, g_ple, g_fin)
    da, dh1, dg_mlp = _mlp_bwd(dh2, a, h1, g_mlp, w_up8, w_down)
    dw_up8 = _wgrad("wgrad_up", u2, da, _ident, _ident, D, DFF, (NDEV, D, DFF // NDEV),
                    pl.BlockSpec((NDEV, D, DFF // NDEV), lambda n, k, r: (0, 0, 0)), split=DFF // NDEV)
    dw_down = _wgrad("wgrad_down", a, dh2, _relu2_bf, _bf, 1024, 1024, (DFF, D),
                     pl.BlockSpec((1024, 1024), lambda n, k, r: (k, n)))
    d_attn, d_ml, dw_out = _out_proj_bwd(dh1, attn, ml, w_out)
    (dmqk, dmv, dmo, dgt, dconv_w, dconv_b, dgn, dgate_b), (r_out, r_pg, r_ple) = _mlstm_bwd(
        mqk, mv, mo, gates, conv_w, conv_b, gate_b, gn, cs, ns, ms, d_ml,
        [dw_out.reshape(NDEV, D // NDEV, D), dw_pg.reshape(NDEV, D // NDEV, D), dw_ple8])
    dq, dk, dv, (r_up, r_down) = _attn_bwd(qkv, attn, lse, d_attn, [dw_up8, dw_down.reshape(NDEV, DFF // NDEV, D)])
    dproj, dx, dg1 = _in_proj_bwd(dq, dk, dv, dmqk, dmv, dmo, dgt, dh1, x, g1, w_in_p, rc, ra, rb)
    small = dict(norm_mix_g=dg1, conv_b=dconv_b, gate_b=dgate_b, mlstm_norm_g=dgn, norm_mlp_g=dg_mlp,
                 norm_ple_g=dg_ple, final_norm_g=dg_fin)
    dw_in8, total = _wgrad("wgrad_in", u1, dproj, _ident, _ident, D, PW, (NDEV, D, IN_W // NDEV),
                           pl.BlockSpec((NDEV, D, IN_W // NDEV), lambda n, k, r: (0, 0, 0)), split=IN_W // NDEV,
                           small=[small[n] for n in SMALL] + [loss] + [dconv_w[j:j + 1] for j in range(4)])
    recv = dict(w_in=_scatter_two_level(dw_in8), w_out=r_out, w_up=r_up, w_down=r_down, w_ple_gate=r_pg, w_ple=r_ple)
    return dx, recv, total


def _gather_weights(shards, dtypes):
    nw = len(shards)

    def body(*refs):
        start, forward, finish = _gather_phases(refs[:nw], refs[nw:2 * nw], refs[2 * nw:3 * nw], *refs[3 * nw:])
        start()
        forward()
        finish()

    return pl.pallas_call(
        body, name="gather_weights",
        in_specs=[VM] * nw, out_specs=[ANY] * nw,
        out_shape=_gather_shapes(shards, dtypes),
        scratch_shapes=_gather_scratch(shards, dtypes),
        compiler_params=_params(),
    )(*shards)


CHIP_FLIPS = [(0, 0), (0, 1), (1, 0), (1, 1)]


def _scatter_two_level(part):
    shard = part.shape[1:]
    nc = len(CHIP_FLIPS)

    def body(in_ref, out_ref, mine_v, sib_v, psum_v, loc_sems, d2d_send, d2d_recv, ici_send, ici_recv, own_sem):
        x, y, c = _place()
        chips = [((x + dx) % 2, (y + dy) % 2) for dx, dy in CHIP_FLIPS]
        local, to_sib = [], []
        for k, (px, py) in enumerate(chips):
            local.append(pltpu.make_async_copy(in_ref.at[_dev_index(px, py, c)], mine_v.at[k], loc_sems.at[k]))
            to_sib.append(pltpu.make_async_remote_copy(
                src_ref=in_ref.at[_dev_index(px, py, 1 - c)], dst_ref=sib_v.at[k], send_sem=d2d_send.at[k],
                recv_sem=d2d_recv.at[k], device_id=(x, y, 1 - c), device_id_type=MESH))
        for cp in to_sib + local:
            cp.start()

        def over_ici(k):
            return pltpu.make_async_remote_copy(
                src_ref=psum_v.at[k], dst_ref=out_ref.at[k], send_sem=ici_send.at[k - 1], recv_sem=ici_recv.at[k - 1],
                device_id=(*chips[k], c), device_id_type=MESH)

        own = pltpu.make_async_copy(psum_v.at[0], out_ref.at[0], own_sem)
        for k in (1, 2, 3, 0):
            local[k].wait()
            to_sib[k].wait_recv()
            psum_v[k] = _bf(mine_v[k].astype(F32) + sib_v[k].astype(F32))
            if k:
                over_ici(k).start()
            else:
                own.start()
        for k in range(1, nc):
            over_ici(k).wait()
        for cp in to_sib:
            cp.wait_send()
        own.wait()

    return pl.pallas_call(
        body, name="scatter_grads",
        in_specs=[ANY], out_specs=ANY,
        out_shape=jax.ShapeDtypeStruct((nc, *shard), part.dtype),
        scratch_shapes=[pltpu.VMEM((nc, *shard), part.dtype)] * 3
        + [pltpu.SemaphoreType.DMA((nc,))] * 3 + [pltpu.SemaphoreType.DMA((nc - 1,))] * 2 + [pltpu.SemaphoreType.DMA],
        compiler_params=_params(),
    )(part)


def _adamw(name, gparts, w, m, v, tr):
    P, R, C = gparts.shape
    c1 = 1.0 - ADAM_B1 ** ADAM_STEP
    c2 = 1.0 - ADAM_B2 ** ADAM_STEP

    def body(g_ref, w_ref, m_ref, v_ref, go_ref, d_ref, mo_ref, vo_ref):
        g = g_ref[0].astype(F32)
        for j in range(1, P):
            g = g + g_ref[j].astype(F32)
        m2 = ADAM_B1 * m_ref[...] + (1.0 - ADAM_B1) * g
        v2 = ADAM_B2 * v_ref[...] + (1.0 - ADAM_B2) * (g * g)
        go_ref[...] = g
        mo_ref[...] = m2
        vo_ref[...] = v2
        d_ref[...] = -ADAM_LR * ((m2 / c1) / (jnp.sqrt(v2 / c2) + ADAM_EPS) + ADAM_WD * w_ref[...])

    row = pl.BlockSpec((tr, C), lambda i: (i, 0))
    return pl.pallas_call(
        body, name=name, grid=(R // tr,),
        in_specs=[pl.BlockSpec((P, tr, C), lambda i: (0, i, 0)), row, row, row],
        out_specs=[row] * 4,
        out_shape=[jax.ShapeDtypeStruct((R, C), F32)] * 4,
        compiler_params=_params(1),
    )(gparts, w, m, v)


SMALL = ("norm_mix_g", "conv_b", "gate_b", "mlstm_norm_g", "norm_mlp_g", "norm_ple_g", "final_norm_g")


def _pack_small(vals):
    return jnp.concatenate([jnp.pad(a, ((0, 7), (0, 1024 - a.shape[1]))) for a in vals], axis=0)


def kernel(x, p, norm_mix_g, w_in, conv_w, conv_b, gate_b, mlstm_norm_g, w_out, norm_mlp_g, w_up, w_down, norm_ple_g, w_ple_gate, w_ple, final_norm_g, loss_target, m_norm_mix_g, m_w_in, m_conv_w, m_conv_b, m_gate_b, m_mlstm_norm_g, m_w_out, m_norm_mlp_g, m_w_up, m_w_down, m_norm_ple_g, m_w_ple_gate, m_w_ple, m_final_norm_g, v_norm_mix_g, v_w_in, v_conv_w, v_conv_b, v_gate_b, v_mlstm_norm_g, v_w_out, v_norm_mlp_g, v_w_up, v_w_down, v_norm_ple_g, v_w_ple_gate, v_w_ple, v_final_norm_g):
    big_names = ("w_in", "conv_w", "w_out", "w_up", "w_down", "w_ple_gate", "w_ple")
    wts = dict(w_in=w_in, conv_w=conv_w, w_out=w_out, w_up=w_up, w_down=w_down, w_ple_gate=w_ple_gate, w_ple=w_ple)
    mom = dict(w_in=m_w_in, conv_w=m_conv_w, w_out=m_w_out, w_up=m_w_up, w_down=m_w_down, w_ple_gate=m_w_ple_gate,
               w_ple=m_w_ple)
    var = dict(w_in=v_w_in, conv_w=v_conv_w, w_out=v_w_out, w_up=v_w_up, w_down=v_w_down, w_ple_gate=v_w_ple_gate,
               w_ple=v_w_ple)
    sq = lambda a: a.reshape(a.shape[1:])
    fin = final_norm_g.reshape(1, D)
    dx, recv, total = _step(
        x[0], p[0, 0], loss_target[0], norm_mix_g, conv_b, jnp.pad(gate_b, ((0, 0), (0, 120))), mlstm_norm_g,
        norm_mlp_g, norm_ple_g, fin, {n: sq(wts[n]) for n in big_names})

    nrow = 8 * len(SMALL)
    me = _dev_index(*_place())
    conv_rows = total[nrow + 8:nrow + 40:8]
    recv["conv_w"] = lax.dynamic_slice_in_dim(conv_rows, me * 128, 128, axis=1).reshape(1, 4, 128)
    out = {}
    for n, tr in zip(big_names, (256, 4, 128, 256, 256, 128, 256)):
        res = _adamw("adamw_" + n, recv[n], sq(wts[n]), sq(mom[n]), sq(var[n]), tr)
        out[n] = [t.reshape(wts[n].shape) for t in res]
    sw = dict(norm_mix_g=norm_mix_g, conv_b=conv_b, gate_b=gate_b, mlstm_norm_g=mlstm_norm_g, norm_mlp_g=norm_mlp_g,
              norm_ple_g=norm_ple_g, final_norm_g=fin)
    sm = dict(norm_mix_g=m_norm_mix_g, conv_b=m_conv_b, gate_b=m_gate_b, mlstm_norm_g=m_mlstm_norm_g,
              norm_mlp_g=m_norm_mlp_g, norm_ple_g=m_norm_ple_g, final_norm_g=m_final_norm_g.reshape(1, D))
    sv = dict(norm_mix_g=v_norm_mix_g, conv_b=v_conv_b, gate_b=v_gate_b, mlstm_norm_g=v_mlstm_norm_g,
              norm_mlp_g=v_norm_mlp_g, norm_ple_g=v_norm_ple_g, final_norm_g=v_final_norm_g.reshape(1, D))
    res = _adamw("adamw_small", total[0:nrow].reshape(1, nrow, 1024), _pack_small([sw[n] for n in SMALL]),
                 _pack_small([sm[n] for n in SMALL]), _pack_small([sv[n] for n in SMALL]), nrow)
    for i, n in enumerate(SMALL):
        shp = final_norm_g.shape if n == "final_norm_g" else sw[n].shape
        out[n] = [t[8 * i, 0:sw[n].shape[1]].reshape(shp) for t in res]
    order = ("norm_mix_g", "w_in", "conv_w", "conv_b", "gate_b", "mlstm_norm_g", "w_out", "norm_mlp_g", "w_up", "w_down",
             "norm_ple_g", "w_ple_gate", "w_ple", "final_norm_g")
    loss_all = total[nrow, 0]
    return (loss_all, dx[None], *[out[n][0] for n in order], *[out[n][1] for n in order],
            *[out[n][2] for n in order], *[out[n][3] for n in order])
```

```python
import functools
import math

import jax
import jax.numpy as jnp
from jax import lax
from jax.experimental import pallas as pl
from jax.experimental.pallas import tpu as pltpu

F32, BF16 = jnp.float32, jnp.bfloat16
S = 4096
D = 1024
AW = 512
MW = 512
DFF = 4096
PLE = 256
IN_W = 3592
PW = 3840
NDEV = 8
EPS = 1e-6
NEG = -1e30
LC = 128
TB = 256
ROPE_THETA = 500000.0
VMEM_LIMIT = 56 * 1024 * 1024
HI = lax.Precision.HIGHEST

ADAM_LR, ADAM_B1, ADAM_B2, ADAM_EPS, ADAM_WD, ADAM_STEP = 0.001, 0.9, 0.999, 1e-08, 0.01, 10


def _params(n_grid=0, **kw):
    sem = dict(dimension_semantics=("arbitrary",) * n_grid) if n_grid else {}
    return pltpu.CompilerParams(vmem_limit_bytes=VMEM_LIMIT, **sem, **kw)


def _cspec(shape):
    nd = len(shape)
    return pl.BlockSpec(shape, lambda *_: (0,) * nd, pipeline_mode=pl.Buffered(1))


def _dot(a, b):
    return jnp.dot(a, b, preferred_element_type=F32)


def _dot_nt(a, b):
    return lax.dot_general(a, b, (((1,), (1,)), ((), ())), preferred_element_type=F32)


def _dot_tn(a, b):
    return lax.dot_general(a, b, (((0,), (0,)), ((), ())), preferred_element_type=F32)


def _bf(x):
    return x.astype(BF16)


def _rms(x):
    rs = lax.rsqrt(jnp.mean(x * x, axis=-1, keepdims=True) + EPS)
    return x * rs, rs


def _rms_bwd(du, n, rs, g):
    dn = du * g
    return rs * (dn - n * jnp.mean(dn * n, axis=-1, keepdims=True))


def _sigmoid(x):
    return 1.0 / (1.0 + jnp.exp(-x))


def _rope_tables():
    j = lax.broadcasted_iota(jnp.int32, (S, 128), 1) % 64
    pos = lax.broadcasted_iota(jnp.int32, (S, 128), 0).astype(F32)
    inv_freq = jnp.power(ROPE_THETA, -(j % 8).astype(F32) / 8.0)
    ang = pos * inv_freq
    cos, sin = jnp.cos(ang), jnp.sin(ang)
    c = jnp.where(j < 16, cos, 1.0)
    a = jnp.where(j < 8, -sin, 0.0)
    b = jnp.where((j >= 8) & (j < 16), sin, 0.0)
    return c, a, b


def _rope(blk, c, a, b):
    return blk * c + pltpu.roll(blk, 120, 1) * a + pltpu.roll(blk, 8, 1) * b


def _rope_bwd(d, c, a, b):
    return d * c + pltpu.roll(d * a, 8, 1) + pltpu.roll(d * b, 120, 1)


MESH = pl.DeviceIdType.MESH
ANY = pl.BlockSpec(memory_space=pl.ANY)
VM = pl.BlockSpec(memory_space=pltpu.VMEM)
FLIPS = [(dx, dy, dc) for dx in (0, 1) for dy in (0, 1) for dc in (0, 1)][1:]


def _place():
    return lax.axis_index("x"), lax.axis_index("y"), lax.axis_index("c")


def _dev_index(px, py, pc):
    return 4 * px + 2 * py + pc


def _gather_phases(ins, outs, bufs, send_sems=None, recv_sems=None, local_sems=None):
    nw = len(ins)
    if nw == 0:
        return (lambda: None,) * 3
    x, y, c = _place()
    me, sib = (x, y, c), (x, y, 1 - c)
    chips = [(1 - x, y), (x, 1 - y), (1 - x, 1 - y)]

    def copy(w, k, block, to, from_buf=False):
        dst = outs[w].at[_dev_index(*block)]
        return pltpu.make_async_remote_copy(
            src_ref=bufs[w] if from_buf else dst, dst_ref=dst, send_sem=send_sems.at[w, k],
            recv_sem=recv_sems.at[w, k], device_id=to, device_id_type=MESH)

    def mine(w):
        return pltpu.make_async_copy(bufs[w], outs[w].at[_dev_index(*me)], local_sems.at[w])

    def first(w):
        return [copy(w, 0, me, sib, True)] + [copy(w, 1 + j, me, (*chip, c), True) for j, chip in enumerate(chips)]

    def passed(w):
        return [copy(w, 4 + j, (*chip, c), sib) for j, chip in enumerate(chips)]

    def start():
        for w in range(nw):
            bufs[w][...] = ins[w][...].astype(bufs[w].dtype)
        for w in range(nw):
            mine(w).start()
            for cp in first(w):
                cp.start()

    def forward():
        for j, chip in enumerate(chips):
            for w in range(nw):
                copy(w, 1 + j, (*chip, c), me).wait_recv()
                passed(w)[j].start()

    def finish():
        for w in range(nw):
            copy(w, 0, sib, me).wait_recv()
        for j, chip in enumerate(chips):
            for w in range(nw):
                copy(w, 4 + j, (*chip, 1 - c), me).wait_recv()
        for w in range(nw):
            for cp in first(w) + passed(w):
                cp.wait_send()
            mine(w).wait()

    return start, forward, finish


def _gather_scratch(shards, dtypes):
    nw = len(shards)
    if nw == 0:
        return []
    return ([pltpu.VMEM(s.shape, dt) for s, dt in zip(shards, dtypes)]
            + [pltpu.SemaphoreType.DMA((nw, 7)), pltpu.SemaphoreType.DMA((nw, 7)), pltpu.SemaphoreType.DMA((nw,))])


def _gather_shapes(shards, dtypes):
    return [jax.ShapeDtypeStruct((NDEV, *s.shape), dt) for s, dt in zip(shards, dtypes)]


def _scatter_phases(ins, outs, send_sems=None, recv_sems=None, local_sems=None):
    nw = len(ins)
    if nw == 0:
        return (lambda: None,) * 2
    x, y, c = _place()
    me = _dev_index(x, y, c)

    def copies():
        out = []
        for w in range(nw):
            out.append(pltpu.make_async_copy(ins[w].at[me], outs[w].at[me], local_sems.at[w]))
            for k, (dx, dy, dc) in enumerate(FLIPS):
                peer = ((x + dx) % 2, (y + dy) % 2, (c + dc) % 2)
                out.append(pltpu.make_async_remote_copy(
                    src_ref=ins[w].at[_dev_index(*peer)], dst_ref=outs[w].at[me], send_sem=send_sems.at[w, k],
                    recv_sem=recv_sems.at[w, k], device_id=peer, device_id_type=MESH))
        return out

    def start():
        for cp in copies():
            cp.start()

    def finish():
        for cp in copies():
            cp.wait()

    return start, finish


def _scatter_scratch(nw):
    if nw == 0:
        return []
    return [pltpu.SemaphoreType.DMA((nw, 7)), pltpu.SemaphoreType.DMA((nw, 7)), pltpu.SemaphoreType.DMA((nw,))]


TM = 512


def _join_w_in(wg):
    sw = IN_W // NDEV

    def body(wg_ref, w_ref):
        for j in range(NDEV):
            w_ref[:, sw * j:sw * (j + 1)] = wg_ref[j]
        w_ref[:, IN_W:PW] = jnp.zeros((D, PW - IN_W), BF16)

    return pl.pallas_call(body, name="join_w_in", out_shape=jax.ShapeDtypeStruct((D, PW), BF16),
                          compiler_params=_params())(wg)


def _in_proj(x, g1, w, rc, ra, rb, shards, dtypes):
    tm = TM
    nw = len(shards)
    nt = S // tm

    def body(*refs):
        x_ref, g_ref, w_ref, rc_ref, ra_ref, rb_ref = refs[:6]
        ins = refs[6:6 + nw]
        qkv_ref, mqk_ref, mv_ref, mo_ref, gt_ref, u_ref = refs[6 + nw:12 + nw]
        outs = refs[12 + nw:12 + 2 * nw]
        bufs = refs[12 + 2 * nw:12 + 3 * nw]
        ag_start, ag_forward, ag_finish = _gather_phases(ins, outs, bufs, *refs[12 + 3 * nw:])
        i = pl.program_id(0)
        pl.when(i == 0)(ag_start)
        pl.when(i == nt - 2)(ag_forward)
        n, _ = _rms(x_ref[...])
        u = _bf(n * g_ref[...])
        u_ref[...] = u
        c, a, b = rc_ref[...], ra_ref[...], rb_ref[...]
        for half in range(2):
            blk = _dot(u, w_ref[:, half * 512:(half + 1) * 512])
            for t in range(4):
                lo = half * 512 + t * 128
                qkv_ref[:, lo:lo + 128] = _rope(blk[:, t * 128:(t + 1) * 128], c, a, b)
        qkv_ref[:, 1024:1536] = _dot(u, w_ref[:, 1024:1536])
        mqk_ref[:, 0:512] = _dot(u, w_ref[:, 1536:2048])
        mqk_ref[:, 512:1024] = _dot(u, w_ref[:, 2048:2560])
        mv_ref[...] = _dot(u, w_ref[:, 2560:3072])
        mo_ref[...] = _dot(u, w_ref[:, 3072:3584])
        gt_ref[...] = _dot(u, w_ref[:, 3584:3712])
        pl.when(i == nt - 1)(ag_finish)

    row = lambda wd: pl.BlockSpec((tm, wd), lambda i: (i, 0))
    res = pl.pallas_call(
        body, name="in_proj", grid=(nt,),
        in_specs=[row(D), _cspec((1, D)), _cspec((D, PW)), row(128), row(128), row(128)] + [VM] * nw,
        out_specs=[row(1536), row(1024), row(512), row(512), row(128), row(D)] + [ANY] * nw,
        out_shape=[jax.ShapeDtypeStruct((S, 1536), F32), jax.ShapeDtypeStruct((S, 1024), F32),
                   jax.ShapeDtypeStruct((S, 512), F32), jax.ShapeDtypeStruct((S, 512), F32),
                   jax.ShapeDtypeStruct((S, 128), F32), jax.ShapeDtypeStruct((S, D), BF16)]
        + _gather_shapes(shards, dtypes),
        scratch_shapes=_gather_scratch(shards, dtypes),
        compiler_params=_params(1),
    )(x, g1, w, rc, ra, rb, *shards)
    return res[:6], res[6:]


DILATIONS = (16, 4, 1)


def _attn_valid(n):
    kd = lax.broadcasted_iota(jnp.int32, (128, 256), 1) - lax.broadcasted_iota(jnp.int32, (128, 256), 0)
    off = jnp.where(n == 0, 0, 128)
    return (kd <= off) & (kd >= off - 128)


def _attn_rows(d, r, n):
    if d == 1:
        q0 = pl.multiple_of(n * 128, 128)
        k0 = pl.multiple_of(jnp.maximum(n - 1, 0) * 128, 128)
        return pl.ds(q0, 128), pl.ds(k0, 256), _attn_valid(n)
    q0 = r + n * 128 * d
    k0 = r + jnp.maximum(n - 1, 0) * 128 * d
    return pl.ds(q0, 128, stride=d), pl.ds(k0, 256, stride=d), _attn_valid(n)


ATTN_GROUP = 4
ATTN_ITERS = S // 128 // ATTN_GROUP


def _attn_group(d, i):
    nb = S // (128 * d)
    if nb == 2:
        qi = lax.broadcasted_iota(jnp.int32, (256, 256), 0) - lax.broadcasted_iota(jnp.int32, (256, 256), 1)
        whole = [pl.ds((ATTN_GROUP // 2) * i + u, 256, stride=d) for u in range(ATTN_GROUP // 2)]
        return [(rows, rows, (qi >= 0) & (qi <= 128)) for rows in whole]
    if d == 1:
        return [_attn_rows(1, 0, i + ATTN_ITERS * u) for u in range(ATTN_GROUP)]
    return [_attn_rows(d, (i // nb) * ATTN_GROUP + u, i % nb) for u in range(ATTN_GROUP)]


def _head0(shape):
    return lax.broadcasted_iota(jnp.int32, shape, 1) < 64


def _stack_heads(t):
    h0 = _head0(t.shape)
    tb = _bf(t)
    zero = jnp.zeros_like(tb)
    return jnp.concatenate([jnp.where(h0, tb, zero), jnp.where(h0, zero, tb)], axis=0)


def _attn_fwd(qkv, shards, dtypes):
    nw = len(shards)

    def body(*refs):
        q_ref, k_ref, v_ref = refs[:3]
        ins = refs[3:3 + nw]
        o_ref, lse_ref = refs[3 + nw:5 + nw]
        outs = refs[5 + nw:5 + 2 * nw]
        m0, m1, l0, l1, acc = refs[5 + 2 * nw:10 + 2 * nw]
        bufs = refs[10 + 2 * nw:10 + 3 * nw]
        ag_start, ag_forward, ag_finish = _gather_phases(ins, outs, bufs, *refs[10 + 3 * nw:])
        hp = pl.program_id(0)
        pl.when(hp == 0)(ag_start)
        pl.when(hp == 3)(ag_forward)
        stats = (m0, m1, l0, l1, acc)

        def update(blocks, first):
            loaded = [([q_ref[rq, :], k_ref[rk, :], v_ref[rk, :]], None if first else [ref[rq, :] for ref in stats])
                      for rq, rk, _ in blocks]
            results = []
            for ((q, k, v), prev), (_, _, valid) in zip(loaded, blocks):
                head0 = _head0(q.shape)
                kb, vb = _bf(k), _bf(v)
                q = q * 0.125
                m_new, l_new, acc_new = [], [], []
                for a, qa in enumerate((_bf(jnp.where(head0, q, 0.0)), _bf(jnp.where(head0, 0.0, q)))):
                    s = jnp.where(valid, _dot_nt(qa, kb), NEG)
                    mc = jnp.max(s, axis=-1, keepdims=True)
                    m_a = jnp.broadcast_to(mc, q.shape) if first else jnp.maximum(prev[a], mc)
                    p = jnp.exp(s - jnp.tile(m_a, (1, 2)))
                    l_add = jnp.sum(p, axis=-1, keepdims=True)
                    pv = _dot(_bf(p), vb)
                    if first:
                        l_a = jnp.broadcast_to(l_add, q.shape)
                    else:
                        alpha = jnp.exp(prev[a] - m_a)
                        l_a, pv = alpha * prev[2 + a] + l_add, alpha * prev[4] + pv
                    m_new.append(m_a), l_new.append(l_a), acc_new.append(pv)
                results.append((m_new[0], m_new[1], l_new[0], l_new[1], jnp.where(head0, acc_new[0], acc_new[1])))
            for (rq, _, _), res in zip(blocks, results):
                for ref, val in zip(stats, res):
                    ref[rq, :] = val

        for d in DILATIONS:
            def step(i, carry, d=d):
                update(_attn_group(d, i), d == DILATIONS[0])
                return carry

            lax.fori_loop(0, ATTN_ITERS, step, 0)

        def fin(t, carry):
            rows = pl.ds(pl.multiple_of(t * 256, 256), 256)
            h0 = lax.broadcasted_iota(jnp.int32, (256, 128), 1) < 64
            l = jnp.where(h0, l0[rows, :], l1[rows, :])
            o_ref[rows, :] = acc[rows, :] / l
            lse_ref[rows, :] = jnp.where(h0, m0[rows, :], m1[rows, :]) + jnp.log(l)
            return carry

        lax.fori_loop(0, S // 256, fin, 0)
        pl.when(hp == 3)(ag_finish)

    col = lambda off: pl.BlockSpec((S, 128), lambda h, off=off: (0, off + h))
    res = pl.pallas_call(
        body, name="attn_fwd", grid=(4,),
        in_specs=[col(0), col(4), col(8)] + [VM] * nw,
        out_specs=[col(0), col(0)] + [ANY] * nw,
        out_shape=[jax.ShapeDtypeStruct((S, AW), F32), jax.ShapeDtypeStruct((S, AW), F32)]
        + _gather_shapes(shards, dtypes),
        scratch_shapes=[pltpu.VMEM((S, 128), F32)] * 5 + _gather_scratch(shards, dtypes),
        compiler_params=_params(1),
    )(qkv, qkv, qkv, *shards)
    return res[0], res[1], res[2:]


def _attn_bwd(qkv, o, lse, do, parts):
    nw = len(parts)

    def body(*refs):
        q_ref, k_ref, v_ref, o_ref, lse_ref, do_ref = refs[:6]
        ins = refs[6:6 + nw]
        dq_ref, dk_ref, dv_ref = refs[6 + nw:9 + nw]
        outs = refs[9 + nw:9 + 2 * nw]
        L0, L1, D0, D1 = refs[9 + 2 * nw:13 + 2 * nw]
        rs_start, rs_finish = _scatter_phases(ins, outs, *refs[13 + 2 * nw:])
        hp = pl.program_id(0)
        pl.when(hp == 0)(rs_start)
        def pre(t, carry):
            rows = pl.ds(pl.multiple_of(t * 256, 256), 256)
            h0 = lax.broadcasted_iota(jnp.int32, (256, 128), 1) < 64
            ls = lse_ref[rows, :]
            dd = do_ref[rows, :] * o_ref[rows, :]
            shp = (256, 128)
            L0[rows, :] = jnp.broadcast_to(jnp.max(jnp.where(h0, ls, NEG), axis=-1, keepdims=True), shp)
            L1[rows, :] = jnp.broadcast_to(jnp.max(jnp.where(h0, NEG, ls), axis=-1, keepdims=True), shp)
            D0[rows, :] = jnp.broadcast_to(jnp.sum(jnp.where(h0, dd, 0.0), axis=-1, keepdims=True), shp)
            D1[rows, :] = jnp.broadcast_to(jnp.sum(jnp.where(h0, 0.0, dd), axis=-1, keepdims=True), shp)
            return carry

        lax.fori_loop(0, S // 256, pre, 0)

        def update(blocks, first):
            loaded = [([q_ref[rq, :], k_ref[rk, :], v_ref[rk, :], do_ref[rq, :]],
                       [L0[rq, :], L1[rq, :], D0[rq, :], D1[rq, :]],
                       [0.0] * 3 if first else [dq_ref[rq, :], dk_ref[rk, :], dv_ref[rk, :]]) for rq, rk, _ in blocks]
            results = []
            for ((q, k, v, dout), (l0v, l1v, d0v, d1v), (dq, dk, dv)), (_, _, valid) in zip(loaded, blocks):
                valid = jnp.tile(valid, (1, 2))
                kst, vst = _stack_heads(k), _stack_heads(v)
                hk = _head0((256, 128))
                dob = _bf(dout)
                cat = lambda a, b: jnp.concatenate([jnp.tile(a, (1, 2)), jnp.tile(b, (1, 2))], axis=1)
                s = jnp.where(valid, _dot_nt(_bf(q * 0.125), kst), NEG)
                p = jnp.exp(s - cat(l0v, l1v))
                ds = _bf(p * (_dot_nt(dob, vst) - cat(d0v, d1v)) * 0.125)
                dk2 = _dot_tn(ds, _bf(q))
                dv2 = _dot_tn(_bf(p), dob)
                results.append((dq + _dot(ds, kst), dk + jnp.where(hk, dk2[0:256], dk2[256:512]),
                                dv + jnp.where(hk, dv2[0:256], dv2[256:512])))
            for (rq, rk, _), (dq, dk, dv) in zip(blocks, results):
                dq_ref[rq, :] = dq
                dk_ref[rk, :] = dk
                dv_ref[rk, :] = dv

        assert S // (128 * DILATIONS[0]) == 2
        for d in DILATIONS:
            def step(i, carry, d=d):
                update(_attn_group(d, i), d == DILATIONS[0])
                return carry

            lax.fori_loop(0, ATTN_ITERS, step, 0)
        pl.when(hp == 3)(rs_finish)

    col = lambda off: pl.BlockSpec((S, 128), lambda h, off=off: (0, off + h))
    res = pl.pallas_call(
        body, name="attn_bwd", grid=(4,),
        in_specs=[col(0), col(4), col(8), col(0), col(0), col(0)] + [ANY] * nw,
        out_specs=[col(0), col(0), col(0)] + [ANY] * nw,
        out_shape=[jax.ShapeDtypeStruct((S, AW), F32)] * 3 + [jax.ShapeDtypeStruct(a.shape, a.dtype) for a in parts],
        scratch_shapes=[pltpu.VMEM((S, 128), F32)] * 4 + _scatter_scratch(nw),
        compiler_params=_params(1),
    )(qkv, qkv, qkv, o, lse, do, *parts)
    return res[0], res[1], res[2], res[3:]


def _logsig(x):
    return jnp.minimum(x, 0.0) - jnp.log1p(jnp.exp(-jnp.abs(x)))


def _conv_taps(xp, n):
    return [xp[8:] if j == 3 else pltpu.roll(xp, 3 - j, 0)[8:] for j in range(4)]


def _conv_silu(xp, w_ref, b_ref, n):
    taps = _conv_taps(xp, n)
    c = b_ref[...] + sum(w_ref[j:j + 1, :] * taps[j] for j in range(4))
    sg = _sigmoid(c)
    return c, sg, taps


def _chunk_gates(G):
    assert LC == 128
    r = lax.broadcasted_iota(jnp.int32, (LC, LC), 0)
    c = lax.broadcasted_iota(jnp.int32, (LC, LC), 1)
    tril = (c <= r).astype(F32)
    triu = (c >= r).astype(F32)
    b_col = jnp.dot(tril, _logsig(G), preferred_element_type=F32, precision=HI)
    return b_col, b_col.T, G.T, tril, triu


def _colpick(X, lane):
    li = lax.broadcasted_iota(jnp.int32, X.shape, 1)
    return jnp.sum(jnp.where(li == lane, X, 0.0), axis=1, keepdims=True)


def _rowpick(XT, row):
    ri = lax.broadcasted_iota(jnp.int32, XT.shape, 0)
    return jnp.sum(jnp.where(ri == row, XT, 0.0), axis=0, keepdims=True)


def _mlstm_head(qh, kh, vh, G, b_col, b_row, g_row, h, Ch, nh, m_prev):
    bt = _colpick(b_col, 4 + h)
    i_col = _colpick(G, h)
    bs = _rowpick(b_row, 4 + h)
    i_row = _rowpick(g_row, h)
    r = lax.broadcasted_iota(jnp.int32, (LC, LC), 0)
    c = lax.broadcasted_iota(jnp.int32, (LC, LC), 1)
    log_d = jnp.where(c <= r, bt - bs + i_row, NEG)
    log_inter = bt + m_prev
    m_t = jnp.maximum(log_inter, jnp.max(log_d, axis=1, keepdims=True))
    Dm = jnp.exp(log_d - m_t)
    g = jnp.exp(log_inter - m_t)
    qb, kb, vb = _bf(qh), _bf(kh), _bf(vh)
    Am = _dot_nt(qb, kb) * Dm
    qC = _dot(qb, _bf(Ch))
    num = g * qC + _dot(_bf(Am), vb)
    qn = jnp.sum(qh * nh, axis=1, keepdims=True)
    den = g * qn + jnp.sum(Am, axis=1, keepdims=True)
    floor = jnp.exp(-m_t)
    dd = jnp.maximum(jnp.abs(den), floor)
    inv_dd = 1.0 / dd
    hh = num * inv_dd
    lane = lax.broadcasted_iota(jnp.int32, (1, LC), 1)
    blast = jnp.sum(jnp.where(lane == LC - 1, bs, 0.0), axis=1, keepdims=True)
    log_s = blast - bt + i_col
    m_new = jnp.maximum(blast + m_prev, jnp.max(log_s, axis=0, keepdims=True))
    decay = jnp.exp(blast + m_prev - m_new)
    ws = jnp.exp(log_s - m_new)
    kw = kh * ws
    C_new = decay * Ch + _dot_tn(_bf(kw), vb)
    n_new = decay * nh + jnp.sum(kw, axis=0, keepdims=True)
    return dict(Dm=Dm, g=g, Am=Am, qC=qC, qn=qn, den=den, floor=floor, inv_dd=inv_dd, h=hh, decay=decay, ws=ws, kw=kw,
                C_new=C_new, n_new=n_new, m_new=m_new, qb=qb, kb=kb, vb=vb)


def _head_out(hh, mo_h, gn_h):
    r = lax.rsqrt(jnp.mean(hh * hh, axis=-1, keepdims=True) + EPS)
    hn = hh * r
    sg = _sigmoid(mo_h)
    return sg * (hn * gn_h), hn, r, sg


def _mlstm_fwd(mqk, mv, mo, gates, conv_w, conv_b, gate_b, gn, shards, dtypes):
    nblk = S // TB
    ncb = TB // LC
    nw = len(shards)

    def body(*refs):
        x_ref, v_ref, o_ref, g_ref, w_ref, b_ref, gb_ref, gn_ref = refs[:8]
        ins = refs[8:8 + nw]
        out_ref, cs_ref, ns_ref, ms_ref = refs[8 + nw:12 + nw]
        outs = refs[12 + nw:12 + 2 * nw]
        tail, Cst, nst, mst, qs, ks = refs[12 + 2 * nw:18 + 2 * nw]
        bufs = refs[18 + 2 * nw:18 + 3 * nw]
        ag_start, ag_forward, ag_finish = _gather_phases(ins, outs, bufs, *refs[18 + 3 * nw:])
        i = pl.program_id(0)
        pl.when(i == 0)(ag_start)
        pl.when(i == nblk // 2)(ag_forward)

        @pl.when(i == 0)
        def _():
            tail[...] = jnp.zeros_like(tail)
            Cst[...] = jnp.zeros_like(Cst)
            nst[...] = jnp.zeros_like(nst)
            mst[...] = jnp.zeros_like(mst)

        x = x_ref[...]
        xp = jnp.concatenate([tail[...], x], axis=0)
        tail[...] = x[TB - 8:TB, :]
        c, sg, _ = _conv_silu(xp, w_ref, b_ref, TB)
        y = c * sg
        qs[...] = y[:, 0:MW]
        ks[...] = y[:, MW:2 * MW] * (1.0 / math.sqrt(128.0))

        for cc in range(ncb):
            rows = slice(cc * LC, (cc + 1) * LC)
            G = g_ref[rows, :] + gb_ref[...]
            b_col, b_row, g_row, _, _ = _chunk_gates(G)
            cs_ref[cc] = Cst[...]
            ns_ref[cc] = nst[...]
            ms_ref[cc] = mst[...]
            for h in range(4):
                ln = slice(h * 128, (h + 1) * 128)
                m_prev = jnp.max(mst[0:1, ln], axis=1, keepdims=True)
                f = _mlstm_head(qs[rows, ln], ks[rows, ln], v_ref[rows, ln], G, b_col, b_row, g_row, h,
                                Cst[:, ln], nst[0:1, ln], m_prev)
                out, _, _, _ = _head_out(f["h"], o_ref[rows, ln], gn_ref[:, ln])
                out_ref[rows, ln] = out
                Cst[:, ln] = f["C_new"]
                nst[0:1, ln] = f["n_new"]
                mst[0:1, ln] = jnp.broadcast_to(f["m_new"], (1, 128))
        pl.when(i == nblk - 1)(ag_finish)

    row = lambda wd: pl.BlockSpec((TB, wd), lambda i: (i, 0))
    res = pl.pallas_call(
        body, name="mlstm_fwd", grid=(nblk,),
        in_specs=[row(1024), row(MW), row(MW), row(128), _cspec((4, 1024)), _cspec((1, 1024)), _cspec((1, 128)),
                  _cspec((1, MW))] + [VM] * nw,
        out_specs=[row(MW), pl.BlockSpec((ncb, 128, MW), lambda i: (i, 0, 0)),
                   pl.BlockSpec((ncb, 8, MW), lambda i: (i, 0, 0)), pl.BlockSpec((ncb, 8, MW), lambda i: (i, 0, 0))]
        + [ANY] * nw,
        out_shape=[jax.ShapeDtypeStruct((S, MW), F32), jax.ShapeDtypeStruct((S // LC, 128, MW), F32),
                   jax.ShapeDtypeStruct((S // LC, 8, MW), F32), jax.ShapeDtypeStruct((S // LC, 8, MW), F32)]
        + _gather_shapes(shards, dtypes),
        scratch_shapes=[pltpu.VMEM((8, 1024), F32), pltpu.VMEM((128, MW), F32), pltpu.VMEM((8, MW), F32),
                        pltpu.VMEM((8, MW), F32), pltpu.VMEM((TB, MW), F32), pltpu.VMEM((TB, MW), F32)]
        + _gather_scratch(shards, dtypes),
        compiler_params=_params(1),
    )(mqk, mv, mo, gates, conv_w, conv_b, gate_b, gn, *shards)
    return res[0], res[1], res[2], res[3], res[4:]


def _mlstm_bwd(mqk, mv, mo, gates, conv_w, conv_b, gate_b, gn, cs, ns, ms, dout, parts):
    nblk = S // TB
    ncb = TB // LC
    kscale = 1.0 / math.sqrt(128.0)
    nw = len(parts)

    def body(*refs):
        x_ref, xprev_ref, v_ref, o_ref, g_ref, w_ref, b_ref, gb_ref, gn_ref, cs_ref, ns_ref, ms_ref, do_ref = refs[:13]
        ins = refs[13:13 + nw]
        dx_ref, dv_ref, dmo_ref, dg_ref, dw_ref, db_ref, dgn_ref, dgb_ref = refs[13 + nw:21 + nw]
        outs = refs[21 + nw:21 + 2 * nw]
        dCst, dnst, dyhead, qs, ks, dqk = refs[21 + 2 * nw:27 + 2 * nw]
        rs_start, rs_finish = _scatter_phases(ins, outs, *refs[27 + 2 * nw:])
        i = pl.program_id(0)
        blk = nblk - 1 - i
        pl.when(i == 0)(rs_start)

        @pl.when(i == 0)
        def _():
            dCst[...] = jnp.zeros_like(dCst)
            dnst[...] = jnp.zeros_like(dnst)
            dyhead[...] = jnp.zeros_like(dyhead)
            dw_ref[...] = jnp.zeros_like(dw_ref)
            db_ref[...] = jnp.zeros_like(db_ref)
            dgn_ref[...] = jnp.zeros_like(dgn_ref)
            dgb_ref[...] = jnp.zeros_like(dgb_ref)

        x = x_ref[...]
        xprev = jnp.where(blk == 0, 0.0, xprev_ref[...])
        xp = jnp.concatenate([xprev, x], axis=0)
        c, sg, taps = _conv_silu(xp, w_ref, b_ref, TB)
        y = c * sg
        qs[...] = y[:, 0:MW]
        ks[...] = y[:, MW:2 * MW] * kscale
        lane128 = lax.broadcasted_iota(jnp.int32, (LC, 128), 1)
        rowi = lax.broadcasted_iota(jnp.int32, (LC, 1), 0)
        ones = jnp.ones((LC, 128), F32)

        for cc in reversed(range(ncb)):
            rows = slice(cc * LC, (cc + 1) * LC)
            G = g_ref[rows, :] + gb_ref[...]
            b_col, b_row, g_row, _, triu = _chunk_gates(G)
            dB = jnp.zeros((LC, 128), F32)
            dI = jnp.zeros((LC, 128), F32)
            for h in range(4):
                ln = slice(h * 128, (h + 1) * 128)
                Ch = cs_ref[cc, :, ln]
                nh = ns_ref[cc, 0:1, ln]
                m_prev = jnp.max(ms_ref[cc, 0:1, ln], axis=1, keepdims=True)
                qh, kh, vh = qs[rows, ln], ks[rows, ln], v_ref[rows, ln]
                f = _mlstm_head(qh, kh, vh, G, b_col, b_row, g_row, h, Ch, nh, m_prev)
                hh, inv_dd, den, g, Am, Dm = f["h"], f["inv_dd"], f["den"], f["g"], f["Am"], f["Dm"]
                qb, kb, vb = f["qb"], f["kb"], f["vb"]
                gn_h = gn_ref[:, ln]
                _, hn, r, sgo = _head_out(hh, o_ref[rows, ln], gn_h)
                do = do_ref[rows, ln]
                hm = hn * gn_h
                dmo_ref[rows, ln] = do * hm * sgo * (1.0 - sgo)
                dhm = do * sgo
                dgn_ref[:, ln] = dgn_ref[:, ln] + jnp.sum(dhm * hn, axis=0, keepdims=True)
                dhn = dhm * gn_h
                dh = r * (dhn - hn * jnp.mean(dhn * hn, axis=-1, keepdims=True))
                dnum = dh * inv_dd
                ddd = -jnp.sum(dh * hh, axis=1, keepdims=True) * inv_dd
                dden = jnp.where(jnp.abs(den) >= f["floor"], ddd * jnp.sign(den), 0.0)
                dnb = _bf(dnum)
                dA = _dot_nt(dnb, vb) + dden
                dv = _dot_tn(_bf(Am), dnb)
                gd = _bf(g * dnum)
                gq = g * dden
                dq = _dot_nt(gd, _bf(Ch)) + gq * nh
                dCn = dCst[:, ln]
                dnn = dnst[0:1, ln]
                dC = f["decay"] * dCn + _dot_tn(qb, gd)
                dn = f["decay"] * dnn + jnp.sum(gq * qh, axis=0, keepdims=True)
                dg = jnp.sum(dnum * f["qC"], axis=1, keepdims=True) + dden * f["qn"]
                dS = _bf(dA * Dm)
                dq = dq + _dot(dS, kb)
                dk = _dot_tn(dS, qb)
                Gm = dA * Am
                gam = dg * g
                dCb = _bf(dCn)
                E = _dot_nt(vb, dCb) + dnn
                ws = f["ws"]
                dk = dk + ws * E
                om = jnp.sum(E * kh, axis=1, keepdims=True) * ws
                dv = dv + _dot(_bf(f["kw"]), dCb)
                ddecay = (jnp.sum(jnp.sum(dCn * Ch, axis=1, keepdims=True), axis=0, keepdims=True)
                          + jnp.sum(dnn * nh, axis=1, keepdims=True))
                delta = ddecay * f["decay"]
                rows_g = jnp.sum(Gm, axis=1, keepdims=True)
                cols_g = jnp.broadcast_to(jnp.sum(Gm, axis=0, keepdims=True), (LC, 128)).T
                last = jnp.where(rowi == LC - 1, jnp.sum(om, axis=0, keepdims=True) + delta, 0.0)
                db = rows_g + gam - om + last - cols_g
                di = cols_g + om
                dB = jnp.where(lane128 == 4 + h, db, dB)
                dI = jnp.where(lane128 == h, di, dI)
                dCst[:, ln] = dC
                dnst[0:1, ln] = dn
                dqk[rows, ln] = dq
                dqk[rows, MW + h * 128:MW + (h + 1) * 128] = dk * kscale
                dv_ref[rows, ln] = dv
            dlogf = jnp.dot(triu, dB, preferred_element_type=F32, precision=HI)
            dG = dI + dlogf * _sigmoid(-G)
            dG = jnp.where(lane128 < 8, dG, 0.0)
            dg_ref[rows, :] = dG
            dgb_ref[...] = dgb_ref[...] + jnp.sum(dG, axis=0, keepdims=True)

        dy = dqk[...] * (sg * (1.0 + c * (1.0 - sg)))
        db_ref[...] = db_ref[...] + jnp.sum(dy, axis=0, keepdims=True)
        for j in range(4):
            dw_ref[j:j + 1, :] = dw_ref[j:j + 1, :] + jnp.sum(dy * taps[j], axis=0, keepdims=True)
        dyp = jnp.concatenate([dy, dyhead[...]], axis=0)
        dx = w_ref[3:4, :] * dy
        for j in range(3):
            dx = dx + w_ref[j:j + 1, :] * pltpu.roll(dyp, TB + 8 - (3 - j), 0)[0:TB]
        dx_ref[...] = dx
        dyhead[...] = dy[0:8, :]
        pl.when(i == nblk - 1)(rs_finish)

    rrow = lambda wd: pl.BlockSpec((TB, wd), lambda i: (nblk - 1 - i, 0))
    st = lambda r: pl.BlockSpec((ncb, r, MW), lambda i: (nblk - 1 - i, 0, 0))
    prev8 = pl.BlockSpec((8, 1024), lambda i: (jnp.maximum((nblk - 1 - i) * (TB // 8) - 1, 0), 0))
    res = pl.pallas_call(
        body, name="mlstm_bwd", grid=(nblk,),
        in_specs=[rrow(1024), prev8, rrow(MW), rrow(MW), rrow(128), _cspec((4, 1024)), _cspec((1, 1024)),
                  _cspec((1, 128)), _cspec((1, MW)), st(128), st(8), st(8), rrow(MW)] + [ANY] * nw,
        out_specs=[rrow(1024), rrow(MW), rrow(MW), rrow(128),
                   pl.BlockSpec((4, 1024), lambda i: (0, 0)), pl.BlockSpec((1, 1024), lambda i: (0, 0)),
                   pl.BlockSpec((1, MW), lambda i: (0, 0)), pl.BlockSpec((1, 128), lambda i: (0, 0))] + [ANY] * nw,
        out_shape=[jax.ShapeDtypeStruct((S, 1024), F32), jax.ShapeDtypeStruct((S, MW), F32),
                   jax.ShapeDtypeStruct((S, MW), F32), jax.ShapeDtypeStruct((S, 128), F32),
                   jax.ShapeDtypeStruct((4, 1024), F32), jax.ShapeDtypeStruct((1, 1024), F32),
                   jax.ShapeDtypeStruct((1, MW), F32), jax.ShapeDtypeStruct((1, 128), F32)]
        + [jax.ShapeDtypeStruct(a.shape, a.dtype) for a in parts],
        scratch_shapes=[pltpu.VMEM((128, MW), F32), pltpu.VMEM((8, MW), F32), pltpu.VMEM((8, 1024), F32),
                        pltpu.VMEM((TB, MW), F32), pltpu.VMEM((TB, MW), F32), pltpu.VMEM((TB, 1024), F32)]
        + _scatter_scratch(nw),
        compiler_params=_params(1),
    )(mqk, mqk, mv, mo, gates, conv_w, conv_b, gate_b, gn, cs, ns, ms, dout, *parts)
    return res[:8], res[8:]


def _out_proj(x, attn, ml, w, g):
    tm = TM

    def body(x_ref, a_ref, m_ref, w_ref, g_ref, h_ref, u_ref):
        h1 = x_ref[...] + _dot(_bf(a_ref[...]), w_ref[0:AW, :]) + _dot(_bf(m_ref[...]), w_ref[AW:D, :])
        h_ref[...] = h1
        n, _ = _rms(h1)
        u_ref[...] = _bf(n * g_ref[...])

    row = lambda wd: pl.BlockSpec((tm, wd), lambda i: (i, 0))
    return pl.pallas_call(
        body, name="out_proj", grid=(S // tm,),
        in_specs=[row(D), row(AW), row(MW), _cspec((D, D)), _cspec((1, D))],
        out_specs=[row(D), row(D)],
        out_shape=[jax.ShapeDtypeStruct((S, D), F32), jax.ShapeDtypeStruct((S, D), BF16)],
        compiler_params=_params(1),
    )(x, attn, ml, w, g)


def _mlp_fwd(h1, u2, w_up, w_down):
    tm = TM

    def body(h_ref, u_ref, wu_ref, wd_ref, a_ref, o_ref):
        u = u_ref[...]
        acc = h_ref[...]
        for c in range(NDEV):
            cols = slice(c * 512, (c + 1) * 512)
            a = _dot(u, wu_ref[c])
            a_ref[:, cols] = _bf(a)
            r = jnp.maximum(a, 0.0)
            acc = acc + _dot(_bf(r * r), wd_ref[cols, :])
        o_ref[...] = acc

    row = lambda wd: pl.BlockSpec((tm, wd), lambda i: (i, 0))
    return pl.pallas_call(
        body, name="mlp_fwd", grid=(S // tm,),
        in_specs=[row(D), row(D), _cspec((NDEV, D, DFF // NDEV)), _cspec((DFF, D))],
        out_specs=[row(DFF), row(D)],
        out_shape=[jax.ShapeDtypeStruct((S, DFF), BF16), jax.ShapeDtypeStruct((S, D), F32)],
        compiler_params=_params(1),
    )(h1, u2, w_up, w_down)


def _ple_loss(h2, p, target, w_pg, w_ple, g_ple, g_fin):
    tm = TM

    def body(h_ref, p_ref, t_ref, wg_ref, wp_ref, gp_ref, gf_ref,
             dh_ref, dwg_ref, dwp_ref, dgp_ref, dgf_ref, loss_ref, acc_g, acc_p):
        i = pl.program_id(0)

        @pl.when(i == 0)
        def _():
            acc_g[...] = jnp.zeros_like(acc_g)
            acc_p[...] = jnp.zeros_like(acc_p)
            dgp_ref[...] = jnp.zeros_like(dgp_ref)
            dgf_ref[...] = jnp.zeros_like(dgf_ref)
            loss_ref[...] = jnp.zeros_like(loss_ref)

        h2v = h_ref[...]
        n2, rs2 = _rms(h2v)
        u3 = _bf(n2 * gp_ref[...])
        gt = _sigmoid(_dot(u3, wg_ref[...]))
        pb = _bf(p_ref[...])
        e = jnp.concatenate([_dot(pb, wp_ref[j]) for j in range(NDEV)], axis=1)
        h3 = h2v + gt * e
        n3, rs3 = _rms(h3)
        err = n3 * gf_ref[...] - t_ref[...]
        loss_ref[...] = loss_ref[...] + 0.5 / D * jnp.sum(jnp.sum(err * err, axis=1, keepdims=True), axis=0, keepdims=True)
        dy = err * (1.0 / D)
        dgf_ref[...] = dgf_ref[...] + jnp.sum(dy * n3, axis=0, keepdims=True)
        dh3 = _rms_bwd(dy, n3, rs3, gf_ref[...])
        de = _bf(dh3 * gt)
        dz = _bf(dh3 * e * gt * (1.0 - gt))
        acc_p[...] = acc_p[...] + _dot_tn(pb, de)
        acc_g[...] = acc_g[...] + _dot_tn(u3, dz)
        du3 = _dot_nt(dz, wg_ref[...])
        dgp_ref[...] = dgp_ref[...] + jnp.sum(du3 * n2, axis=0, keepdims=True)
        dh_ref[...] = dh3 + _rms_bwd(du3, n2, rs2, gp_ref[...])

        @pl.when(i == S // tm - 1)
        def _():
            dwg_ref[...] = _bf(acc_g[...])
            for j in range(NDEV):
                dwp_ref[j] = _bf(acc_p[:, j * 128:(j + 1) * 128])

    row = lambda wd: pl.BlockSpec((tm, wd), lambda i: (i, 0))
    whole = lambda shp: pl.BlockSpec(shp, lambda i: (0,) * len(shp))
    return pl.pallas_call(
        body, name="ple_loss", grid=(S // tm,),
        in_specs=[row(D), row(PLE), row(D), _cspec((D, D)), _cspec((NDEV, PLE, 128)), _cspec((1, D)), _cspec((1, D))],
        out_specs=[row(D), whole((D, D)), whole((NDEV, PLE, 128)), whole((1, D)), whole((1, D)), whole((1, 1))],
        out_shape=[jax.ShapeDtypeStruct((S, D), F32), jax.ShapeDtypeStruct((D, D), BF16),
                   jax.ShapeDtypeStruct((NDEV, PLE, 128), BF16), jax.ShapeDtypeStruct((1, D), F32),
                   jax.ShapeDtypeStruct((1, D), F32), jax.ShapeDtypeStruct((1, 1), F32)],
        scratch_shapes=[pltpu.VMEM((D, D), F32), pltpu.VMEM((PLE, D), F32)],
        compiler_params=_params(1),
    )(h2, p, target, w_pg, w_ple, g_ple, g_fin)


def _mlp_bwd(dh2, a, h1, g, w_up, w_down):
    tm = TM

    def body(d_ref, a_ref, h_ref, g_ref, wu_ref, wd_ref, da_ref, dh1_ref, dg_ref):
        @pl.when(pl.program_id(0) == 0)
        def _():
            dg_ref[...] = jnp.zeros_like(dg_ref)

        dh2v = d_ref[...]
        db = _bf(dh2v)
        du = jnp.zeros((tm, D), F32)
        for c in range(NDEV):
            cols = slice(c * 512, (c + 1) * 512)
            dr = _dot_nt(db, wd_ref[cols, :])
            da = _bf(dr * (2.0 * jnp.maximum(a_ref[:, cols], 0.0)))
            da_ref[:, cols] = da
            du = du + _dot_nt(da, wu_ref[c])
        n, rs = _rms(h_ref[...])
        dg_ref[...] = dg_ref[...] + jnp.sum(du * n, axis=0, keepdims=True)
        dh1_ref[...] = dh2v + _rms_bwd(du, n, rs, g_ref[...])

    row = lambda wd: pl.BlockSpec((tm, wd), lambda i: (i, 0))
    return pl.pallas_call(
        body, name="mlp_bwd", grid=(S // tm,),
        in_specs=[row(D), row(DFF), row(D), _cspec((1, D)), _cspec((NDEV, D, DFF // NDEV)), _cspec((DFF, D))],
        out_specs=[row(DFF), row(D), pl.BlockSpec((1, D), lambda i: (0, 0))],
        out_shape=[jax.ShapeDtypeStruct((S, DFF), BF16), jax.ShapeDtypeStruct((S, D), F32),
                   jax.ShapeDtypeStruct((1, D), F32)],
        compiler_params=_params(1),
    )(dh2, a, h1, g, w_up, w_down)


def _out_proj_bwd(dh1, attn, ml, w):
    tm = TM

    def body(d_ref, a_ref, m_ref, w_ref, da_ref, dm_ref, dw_ref, acc):
        i = pl.program_id(0)

        @pl.when(i == 0)
        def _():
            acc[...] = jnp.zeros_like(acc)

        db = _bf(d_ref[...])
        dmix = _dot_nt(db, w_ref[...])
        da_ref[...] = dmix[:, 0:AW]
        dm_ref[...] = dmix[:, AW:D]
        acc[0:AW, :] = acc[0:AW, :] + _dot_tn(_bf(a_ref[...]), db)
        acc[AW:D, :] = acc[AW:D, :] + _dot_tn(_bf(m_ref[...]), db)

        @pl.when(i == S // tm - 1)
        def _():
            dw_ref[...] = _bf(acc[...])

    row = lambda wd: pl.BlockSpec((tm, wd), lambda i: (i, 0))
    return pl.pallas_call(
        body, name="out_proj_bwd", grid=(S // tm,),
        in_specs=[row(D), row(AW), row(MW), _cspec((D, D))],
        out_specs=[row(AW), row(MW), pl.BlockSpec((D, D), lambda i: (0, 0))],
        out_shape=[jax.ShapeDtypeStruct((S, AW), F32), jax.ShapeDtypeStruct((S, MW), F32),
                   jax.ShapeDtypeStruct((D, D), BF16)],
        scratch_shapes=[pltpu.VMEM((D, D), F32)],
        compiler_params=_params(1),
    )(dh1, attn, ml, w)


def _in_proj_bwd(dq, dk, dv, dmqk, dmv, dmo, dgt, dh1, x, g1, w, rc, ra, rb):
    tm = TM

    def body(dq_ref, dk_ref, dv_ref, dmqk_ref, dmv_ref, dmo_ref, dgt_ref, dh_ref, x_ref, g_ref, w_ref,
             rc_ref, ra_ref, rb_ref, dp_ref, dx_ref, dg_ref):
        @pl.when(pl.program_id(0) == 0)
        def _():
            dg_ref[...] = jnp.zeros_like(dg_ref)

        c, a, b = rc_ref[...], ra_ref[...], rb_ref[...]
        for half, ref in enumerate((dq_ref, dk_ref)):
            for t in range(4):
                lo = half * 512 + t * 128
                dp_ref[:, lo:lo + 128] = _bf(_rope_bwd(ref[:, t * 128:(t + 1) * 128], c, a, b))
        dp_ref[:, 1024:1536] = _bf(dv_ref[...])
        dp_ref[:, 1536:2560] = _bf(dmqk_ref[...])
        dp_ref[:, 2560:3072] = _bf(dmv_ref[...])
        dp_ref[:, 3072:3584] = _bf(dmo_ref[...])
        dp_ref[:, 3584:3712] = _bf(dgt_ref[...])
        dp_ref[:, 3712:PW] = jnp.zeros((tm, PW - 3712), BF16)
        du = jnp.zeros((tm, D), F32)
        for s in range(PW // 768):
            cols = slice(s * 768, (s + 1) * 768)
            du = du + _dot_nt(dp_ref[:, cols], w_ref[:, cols])
        n, rs = _rms(x_ref[...])
        dg_ref[...] = dg_ref[...] + jnp.sum(du * n, axis=0, keepdims=True)
        dx_ref[...] = dh_ref[...] + _rms_bwd(du, n, rs, g_ref[...])

    row = lambda wd: pl.BlockSpec((tm, wd), lambda i: (i, 0))
    return pl.pallas_call(
        body, name="in_proj_bwd", grid=(S // tm,),
        in_specs=[row(AW), row(AW), row(AW), row(1024), row(MW), row(MW), row(128), row(D), row(D), _cspec((1, D)),
                  _cspec((D, PW)), row(128), row(128), row(128)],
        out_specs=[row(PW), row(D), pl.BlockSpec((1, D), lambda i: (0, 0))],
        out_shape=[jax.ShapeDtypeStruct((S, PW), BF16), jax.ShapeDtypeStruct((S, D), F32),
                   jax.ShapeDtypeStruct((1, D), F32)],
        compiler_params=_params(1),
    )(dq, dk, dv, dmqk, dmv, dmo, dgt, dh1, x, g1, w, rc, ra, rb)


SMALL_ROWS = 96


def _small_phases(ins, out_ref, pack, rbuf, send_sems, recv_sems):
    x, y, c = _place()
    me = _dev_index(x, y, c)

    def copies():
        out = []
        for k, (dx, dy, dc) in enumerate(FLIPS):
            peer = ((x + dx) % 2, (y + dy) % 2, (c + dc) % 2)
            out.append(pltpu.make_async_remote_copy(
                src_ref=pack, dst_ref=rbuf.at[me], send_sem=send_sems.at[k], recv_sem=recv_sems.at[k],
                device_id=peer, device_id_type=MESH))
        return out

    def start():
        pack[...] = jnp.zeros_like(pack)
        for i, ref in enumerate(ins):
            pack[8 * i:8 * i + 1, 0:ref.shape[1]] = ref[...]
        rbuf[me] = pack[...]
        for cp in copies():
            cp.start()

    def finish():
        for cp in copies():
            cp.wait()
        tot = rbuf[0]
        for j in range(1, NDEV):
            tot = tot + rbuf[j]
        out_ref[...] = tot

    return start, finish


def _wgrad(name, A, B, a_fn, b_fn, tk, tn, out_shape, out_spec, ts=512, split=None, small=()):
    K, N = A.shape[1], B.shape[1]
    nrt = S // ts
    nc = next(c for c in (1024, 1280, tn) if tn % c == 0)
    ns = len(small)
    grid = (N // tn, K // tk, nrt)

    def body(*refs):
        a_ref, b_ref = refs[:2]
        o_ref = refs[2 + ns]
        acc = refs[3 + ns + bool(ns)]
        r = pl.program_id(2)
        if ns:
            step = (pl.program_id(0) * grid[1] + pl.program_id(1)) * nrt + r
            sm_start, sm_finish = _small_phases(refs[2:2 + ns], refs[3 + ns], *refs[4 + ns + 1:])
            pl.when(step == 0)(sm_start)

        @pl.when(r == 0)
        def _():
            acc[...] = jnp.zeros_like(acc)

        kc = min(tk, 1024)
        bs = [b_fn(b_ref[:, c * nc:(c + 1) * nc]) for c in range(tn // nc)]
        for kk in range(tk // kc):
            rows = slice(kk * kc, (kk + 1) * kc)
            at = a_fn(a_ref[:, rows]).T
            for c, b in enumerate(bs):
                cols = slice(c * nc, (c + 1) * nc)
                acc[rows, cols] = acc[rows, cols] + _dot(at, b)

        @pl.when(r == nrt - 1)
        def _():
            if split is None:
                o_ref[...] = _bf(acc[...])
            else:
                for j in range(NDEV):
                    o_ref[j] = _bf(acc[:, split * j:split * (j + 1)])

        if ns:
            pl.when(step == grid[0] * grid[1] * nrt - 1)(sm_finish)

    in_specs = [pl.BlockSpec((ts, tk), lambda n, k, r: (r, k)), pl.BlockSpec((ts, tn), lambda n, k, r: (r, n))]
    scratch = [pltpu.VMEM((tk, tn), F32)]
    if not ns:
        return pl.pallas_call(
            body, name=name, grid=grid, in_specs=in_specs, out_specs=out_spec,
            out_shape=jax.ShapeDtypeStruct(out_shape, BF16), scratch_shapes=scratch, compiler_params=_params(3),
        )(A, B)
    return pl.pallas_call(
        body, name=name, grid=grid, in_specs=in_specs + [VM] * ns, out_specs=[out_spec, VM],
        out_shape=[jax.ShapeDtypeStruct(out_shape, BF16), jax.ShapeDtypeStruct((SMALL_ROWS, 1024), F32)],
        scratch_shapes=scratch + [pltpu.VMEM((SMALL_ROWS, 1024), F32), pltpu.VMEM((NDEV, SMALL_ROWS, 1024), F32),
                                  pltpu.SemaphoreType.DMA((7,)), pltpu.SemaphoreType.DMA((7,))],
        compiler_params=_params(3),
    )(A, B, *small)


def _relu2_bf(a):
    r = jnp.maximum(a.astype(F32), 0.0)
    return _bf(r * r)


def _ident(a):
    return a


def _step(x, p, target, g1, conv_b, gate_b, gn, g_mlp, g_ple, g_fin, sh):
    g_in, g_conv = _gather_weights([sh["w_in"], sh["conv_w"]], [BF16, F32])
    conv_w = g_conv.transpose(1, 0, 2).reshape(4, 1024)
    rc, ra, rb = _rope_tables()
    w_in_p = _join_w_in(g_in)
    (qkv, mqk, mv, mo, gates, u1), (w_out8, w_pg8, w_ple8) = _in_proj(
        x, g1, w_in_p, rc, ra, rb, [sh["w_out"], sh["w_ple_gate"], sh["w_ple"]], [BF16] * 3)
    attn, lse, (w_up8,) = _attn_fwd(qkv, [sh["w_up"]], [BF16])
    ml, cs, ns, ms, (w_down8,) = _mlstm_fwd(mqk, mv, mo, gates, conv_w, conv_b, gate_b, gn, [sh["w_down"]], [BF16])
    w_out, w_down, w_pg = w_out8.reshape(D, D), w_down8.reshape(DFF, D), w_pg8.reshape(D, D)
    h1, u2 = _out_proj(x, attn, ml, w_out, g_mlp)
    a, h2 = _mlp_fwd(h1, u2, w_up8, w_down)
    dh2, dw_pg, dw_ple8, dg_ple, dg_fin, loss = _ple_loss(h2, p, target, w_pg, w_ple8, g_ple, g_fin)
    da, dh1, dg_mlp = _mlp_bwd(dh2, a, h1, g_mlp, w_up8, w_down)
    dw_up8 = _wgrad("wgrad_up", u2, da, _ident, _ident, D, DFF, (NDEV, D, DFF // NDEV),
                    pl.BlockSpec((NDEV, D, DFF // NDEV), lambda n, k, r: (0, 0, 0)), split=DFF // NDEV)
    dw_down = _wgrad("wgrad_down", a, dh2, _relu2_bf, _bf, DFF, D, (DFF, D),
                     pl.BlockSpec((DFF, D), lambda n, k, r: (0, 0)))
    d_attn, d_ml, dw_out = _out_proj_bwd(dh1, attn, ml, w_out)
    (dmqk, dmv, dmo, dgt, dconv_w, dconv_b, dgn, dgate_b), (r_out, r_pg, r_ple) = _mlstm_bwd(
        mqk, mv, mo, gates, conv_w, conv_b, gate_b, gn, cs, ns, ms, d_ml,
        [dw_out.reshape(NDEV, D // NDEV, D), dw_pg.reshape(NDEV, D // NDEV, D), dw_ple8])
    dq, dk, dv, (r_up, r_down) = _attn_bwd(qkv, attn, lse, d_attn, [dw_up8, dw_down.reshape(NDEV, DFF // NDEV, D)])
    dproj, dx, dg1 = _in_proj_bwd(dq, dk, dv, dmqk, dmv, dmo, dgt, dh1, x, g1, w_in_p, rc, ra, rb)
    small = dict(norm_mix_g=dg1, conv_b=dconv_b, gate_b=dgate_b, mlstm_norm_g=dgn, norm_mlp_g=dg_mlp,
                 norm_ple_g=dg_ple, final_norm_g=dg_fin)
    dw_in8, total = _wgrad("wgrad_in", u1, dproj, _ident, _ident, D, PW, (NDEV, D, IN_W // NDEV),
                           pl.BlockSpec((NDEV, D, IN_W // NDEV), lambda n, k, r: (0, 0, 0)), split=IN_W // NDEV,
                           small=[small[n] for n in SMALL] + [loss] + [dconv_w[j:j + 1] for j in range(4)])
    recv = dict(w_in=_scatter_two_level(dw_in8), w_out=r_out, w_up=r_up, w_down=r_down, w_ple_gate=r_pg, w_ple=r_ple)
    return dx, recv, total


def _gather_weights(shards, dtypes):
    nw = len(shards)

    def body(*refs):
        start, forward, finish = _gather_phases(refs[:nw], refs[nw:2 * nw], refs[2 * nw:3 * nw], *refs[3 * nw:])
        start()
        forward()
        finish()

    return pl.pallas_call(
        body, name="gather_weights",
        in_specs=[VM] * nw, out_specs=[ANY] * nw,
        out_shape=_gather_shapes(shards, dtypes),
        scratch_shapes=_gather_scratch(shards, dtypes),
        compiler_params=_params(),
    )(*shards)


CHIP_FLIPS = [(0, 0), (0, 1), (1, 0), (1, 1)]


def _scatter_two_level(part):
    shard = part.shape[1:]
    nc = len(CHIP_FLIPS)

    def body(in_ref, out_ref, mine_v, sib_v, psum_v, loc_sems, d2d_send, d2d_recv, ici_send, ici_recv, own_sem):
        x, y, c = _place()
        chips = [((x + dx) % 2, (y + dy) % 2) for dx, dy in CHIP_FLIPS]
        local, to_sib = [], []
        for k, (px, py) in enumerate(chips):
            local.append(pltpu.make_async_copy(in_ref.at[_dev_index(px, py, c)], mine_v.at[k], loc_sems.at[k]))
            to_sib.append(pltpu.make_async_remote_copy(
                src_ref=in_ref.at[_dev_index(px, py, 1 - c)], dst_ref=sib_v.at[k], send_sem=d2d_send.at[k],
                recv_sem=d2d_recv.at[k], device_id=(x, y, 1 - c), device_id_type=MESH))
        for cp in to_sib + local:
            cp.start()

        def over_ici(k):
            return pltpu.make_async_remote_copy(
                src_ref=psum_v.at[k], dst_ref=out_ref.at[k], send_sem=ici_send.at[k - 1], recv_sem=ici_recv.at[k - 1],
                device_id=(*chips[k], c), device_id_type=MESH)

        own = pltpu.make_async_copy(psum_v.at[0], out_ref.at[0], own_sem)
        for k in (1, 2, 3, 0):
            local[k].wait()
            to_sib[k].wait_recv()
            psum_v[k] = _bf(mine_v[k].astype(F32) + sib_v[k].astype(F32))
            if k:
                over_ici(k).start()
            else:
                own.start()
        for k in range(1, nc):
            over_ici(k).wait()
        for cp in to_sib:
            cp.wait_send()
        own.wait()

    return pl.pallas_call(
        body, name="scatter_grads",
        in_specs=[ANY], out_specs=ANY,
        out_shape=jax.ShapeDtypeStruct((nc, *shard), part.dtype),
        scratch_shapes=[pltpu.VMEM((nc, *shard), part.dtype)] * 3
        + [pltpu.SemaphoreType.DMA((nc,))] * 3 + [pltpu.SemaphoreType.DMA((nc - 1,))] * 2 + [pltpu.SemaphoreType.DMA],
        compiler_params=_params(),
    )(part)


def _adamw(name, gparts, w, m, v, tr):
    P, R, C = gparts.shape
    c1 = 1.0 - ADAM_B1 ** ADAM_STEP
    c2 = 1.0 - ADAM_B2 ** ADAM_STEP

    def body(g_ref, w_ref, m_ref, v_ref, go_ref, d_ref, mo_ref, vo_ref):
        g = g_ref[0].astype(F32)
        for j in range(1, P):
            g = g + g_ref[j].astype(F32)
        m2 = ADAM_B1 * m_ref[...] + (1.0 - ADAM_B1) * g
        v2 = ADAM_B2 * v_ref[...] + (1.0 - ADAM_B2) * (g * g)
        go_ref[...] = g
        mo_ref[...] = m2
        vo_ref[...] = v2
        d_ref[...] = -ADAM_LR * ((m2 / c1) / (jnp.sqrt(v2 / c2) + ADAM_EPS) + ADAM_WD * w_ref[...])

    row = pl.BlockSpec((tr, C), lambda i: (i, 0))
    return pl.pallas_call(
        body, name=name, grid=(R // tr,),
        in_specs=[pl.BlockSpec((P, tr, C), lambda i: (0, i, 0)), row, row, row],
        out_specs=[row] * 4,
        out_shape=[jax.ShapeDtypeStruct((R, C), F32)] * 4,
        compiler_params=_params(1),
    )(gparts, w, m, v)


SMALL = ("norm_mix_g", "conv_b", "gate_b", "mlstm_norm_g", "norm_mlp_g", "norm_ple_g", "final_norm_g")


def _pack_small(vals):
    return jnp.concatenate([jnp.pad(a, ((0, 7), (0, 1024 - a.shape[1]))) for a in vals], axis=0)


def kernel(x, p, norm_mix_g, w_in, conv_w, conv_b, gate_b, mlstm_norm_g, w_out, norm_mlp_g, w_up, w_down, norm_ple_g, w_ple_gate, w_ple, final_norm_g, loss_target, m_norm_mix_g, m_w_in, m_conv_w, m_conv_b, m_gate_b, m_mlstm_norm_g, m_w_out, m_norm_mlp_g, m_w_up, m_w_down, m_norm_ple_g, m_w_ple_gate, m_w_ple, m_final_norm_g, v_norm_mix_g, v_w_in, v_conv_w, v_conv_b, v_gate_b, v_mlstm_norm_g, v_w_out, v_norm_mlp_g, v_w_up, v_w_down, v_norm_ple_g, v_w_ple_gate, v_w_ple, v_final_norm_g):
    big_names = ("w_in", "conv_w", "w_out", "w_up", "w_down", "w_ple_gate", "w_ple")
    wts = dict(w_in=w_in, conv_w=conv_w, w_out=w_out, w_up=w_up, w_down=w_down, w_ple_gate=w_ple_gate, w_ple=w_ple)
    mom = dict(w_in=m_w_in, conv_w=m_conv_w, w_out=m_w_out, w_up=m_w_up, w_down=m_w_down, w_ple_gate=m_w_ple_gate,
               w_ple=m_w_ple)
    var = dict(w_in=v_w_in, conv_w=v_conv_w, w_out=v_w_out, w_up=v_w_up, w_down=v_w_down, w_ple_gate=v_w_ple_gate,
               w_ple=v_w_ple)
    sq = lambda a: a.reshape(a.shape[1:])
    fin = final_norm_g.reshape(1, D)
    dx, recv, total = _step(
        x[0], p[0, 0], loss_target[0], norm_mix_g, conv_b, jnp.pad(gate_b, ((0, 0), (0, 120))), mlstm_norm_g,
        norm_mlp_g, norm_ple_g, fin, {n: sq(wts[n]) for n in big_names})

    nrow = 8 * len(SMALL)
    me = _dev_index(*_place())
    conv_rows = total[nrow + 8:nrow + 40:8]
    recv["conv_w"] = lax.dynamic_slice_in_dim(conv_rows, me * 128, 128, axis=1).reshape(1, 4, 128)
    out = {}
    for n, tr in zip(big_names, (256, 4, 128, 256, 256, 128, 256)):
        res = _adamw("adamw_" + n, recv[n], sq(wts[n]), sq(mom[n]), sq(var[n]), tr)
        out[n] = [t.reshape(wts[n].shape) for t in res]
    sw = dict(norm_mix_g=norm_mix_g, conv_b=conv_b, gate_b=gate_b, mlstm_norm_g=mlstm_norm_g, norm_mlp_g=norm_mlp_g,
              norm_ple_g=norm_ple_g, final_norm_g=fin)
    sm = dict(norm_mix_g=m_norm_mix_g, conv_b=m_conv_b, gate_b=m_gate_b, mlstm_norm_g=m_mlstm_norm_g,
              norm_mlp_g=m_norm_mlp_g, norm_ple_g=m_norm_ple_g, final_norm_g=m_final_norm_g.reshape(1, D))
    sv = dict(norm_mix_g=v_norm_mix_g, conv_b=v_conv_b, gate_b=v_gate_b, mlstm_norm_g=v_mlstm_norm_g,
              norm_mlp_g=v_norm_mlp_g, norm_ple_g=v_norm_ple_g, final_norm_g=v_final_norm_g.reshape(1, D))
    res = _adamw("adamw_small", total[0:nrow].reshape(1, nrow, 1024), _pack_small([sw[n] for n in SMALL]),
                 _pack_small([sm[n] for n in SMALL]), _pack_small([sv[n] for n in SMALL]), nrow)
    for i, n in enumerate(SMALL):
        shp = final_norm_g.shape if n == "final_norm_g" else sw[n].shape
        out[n] = [t[8 * i, 0:sw[n].shape[1]].reshape(shp) for t in res]
    order = ("norm_mix_g", "w_in", "conv_w", "conv_b", "gate_b", "mlstm_norm_g", "w_out", "norm_mlp_g", "w_up", "w_down",
             "norm_ple_g", "w_ple_gate", "w_ple", "final_norm_g")
    loss_all = total[nrow, 0]
    return (loss_all, dx[None], *[out[n][0] for n in order], *[out[n][1] for n in order],
            *[out[n][2] for n in order], *[out[n][3] for n in order])
```

```python
import functools
import math

import jax
import jax.numpy as jnp
from jax import lax
from jax.experimental import pallas as pl
from jax.experimental.pallas import tpu as pltpu

F32, BF16 = jnp.float32, jnp.bfloat16
S = 4096
D = 1024
AW = 512
MW = 512
DFF = 4096
PLE = 256
IN_W = 3592
PW = 3840
NDEV = 8
EPS = 1e-6
NEG = -1e30
LC = 128
TB = 256
ROPE_THETA = 500000.0
VMEM_LIMIT = 56 * 1024 * 1024
HI = lax.Precision.HIGHEST

ADAM_LR, ADAM_B1, ADAM_B2, ADAM_EPS, ADAM_WD, ADAM_STEP = 0.001, 0.9, 0.999, 1e-08, 0.01, 10


def _params(n_grid=0, **kw):
    sem = dict(dimension_semantics=("arbitrary",) * n_grid) if n_grid else {}
    return pltpu.CompilerParams(vmem_limit_bytes=VMEM_LIMIT, **sem, **kw)


def _cspec(shape):
    nd = len(shape)
    return pl.BlockSpec(shape, lambda *_: (0,) * nd, pipeline_mode=pl.Buffered(1))


def _dot(a, b):
    return jnp.dot(a, b, preferred_element_type=F32)


def _dot_nt(a, b):
    return lax.dot_general(a, b, (((1,), (1,)), ((), ())), preferred_element_type=F32)


def _dot_tn(a, b):
    return lax.dot_general(a, b, (((0,), (0,)), ((), ())), preferred_element_type=F32)


def _bf(x):
    return x.astype(BF16)


def _rms(x):
    rs = lax.rsqrt(jnp.mean(x * x, axis=-1, keepdims=True) + EPS)
    return x * rs, rs


def _rms_bwd(du, n, rs, g):
    dn = du * g
    return rs * (dn - n * jnp.mean(dn * n, axis=-1, keepdims=True))


def _sigmoid(x):
    return 1.0 / (1.0 + jnp.exp(-x))


def _rope_tables():
    j = lax.broadcasted_iota(jnp.int32, (S, 128), 1) % 64
    pos = lax.broadcasted_iota(jnp.int32, (S, 128), 0).astype(F32)
    inv_freq = jnp.power(ROPE_THETA, -(j % 8).astype(F32) / 8.0)
    ang = pos * inv_freq
    cos, sin = jnp.cos(ang), jnp.sin(ang)
    c = jnp.where(j < 16, cos, 1.0)
    a = jnp.where(j < 8, -sin, 0.0)
    b = jnp.where((j >= 8) & (j < 16), sin, 0.0)
    return c, a, b


def _rope(blk, c, a, b):
    return blk * c + pltpu.roll(blk, 120, 1) * a + pltpu.roll(blk, 8, 1) * b


def _rope_bwd(d, c, a, b):
    return d * c + pltpu.roll(d * a, 8, 1) + pltpu.roll(d * b, 120, 1)


MESH = pl.DeviceIdType.MESH
ANY = pl.BlockSpec(memory_space=pl.ANY)
VM = pl.BlockSpec(memory_space=pltpu.VMEM)
FLIPS = [(dx, dy, dc) for dx in (0, 1) for dy in (0, 1) for dc in (0, 1)][1:]


def _place():
    return lax.axis_index("x"), lax.axis_index("y"), lax.axis_index("c")


def _dev_index(px, py, pc):
    return 4 * px + 2 * py + pc


def _gather_phases(ins, outs, bufs, send_sems=None, recv_sems=None, local_sems=None):
    nw = len(ins)
    if nw == 0:
        return (lambda: None,) * 3
    x, y, c = _place()
    me, sib = (x, y, c), (x, y, 1 - c)
    chips = [(1 - x, y), (x, 1 - y), (1 - x, 1 - y)]

    def copy(w, k, block, to, from_buf=False):
        dst = outs[w].at[_dev_index(*block)]
        return pltpu.make_async_remote_copy(
            src_ref=bufs[w] if from_buf else dst, dst_ref=dst, send_sem=send_sems.at[w, k],
            recv_sem=recv_sems.at[w, k], device_id=to, device_id_type=MESH)

    def mine(w):
        return pltpu.make_async_copy(bufs[w], outs[w].at[_dev_index(*me)], local_sems.at[w])

    def first(w):
        return [copy(w, 0, me, sib, True)] + [copy(w, 1 + j, me, (*chip, c), True) for j, chip in enumerate(chips)]

    def passed(w):
        return [copy(w, 4 + j, (*chip, c), sib) for j, chip in enumerate(chips)]

    def start():
        for w in range(nw):
            bufs[w][...] = ins[w][...].astype(bufs[w].dtype)
        for w in range(nw):
            mine(w).start()
            for cp in first(w):
                cp.start()

    def forward():
        for j, chip in enumerate(chips):
            for w in range(nw):
                copy(w, 1 + j, (*chip, c), me).wait_recv()
                passed(w)[j].start()

    def finish():
        for w in range(nw):
            copy(w, 0, sib, me).wait_recv()
        for j, chip in enumerate(chips):
            for w in range(nw):
                copy(w, 4 + j, (*chip, 1 - c), me).wait_recv()
        for w in range(nw):
            for cp in first(w) + passed(w):
                cp.wait_send()
            mine(w).wait()

    return start, forward, finish


def _gather_scratch(shards, dtypes):
    nw = len(shards)
    if nw == 0:
        return []
    return ([pltpu.VMEM(s.shape, dt) for s, dt in zip(shards, dtypes)]
            + [pltpu.SemaphoreType.DMA((nw, 7)), pltpu.SemaphoreType.DMA((nw, 7)), pltpu.SemaphoreType.DMA((nw,))])


def _gather_shapes(shards, dtypes):
    return [jax.ShapeDtypeStruct((NDEV, *s.shape), dt) for s, dt in zip(shards, dtypes)]


def _scatter_phases(ins, outs, send_sems=None, recv_sems=None, local_sems=None):
    nw = len(ins)
    if nw == 0:
        return (lambda: None,) * 2
    x, y, c = _place()
    me = _dev_index(x, y, c)

    def copies():
        out = []
        for w in range(nw):
            out.append(pltpu.make_async_copy(ins[w].at[me], outs[w].at[me], local_sems.at[w]))
            for k, (dx, dy, dc) in enumerate(FLIPS):
                peer = ((x + dx) % 2, (y + dy) % 2, (c + dc) % 2)
                out.append(pltpu.make_async_remote_copy(
                    src_ref=ins[w].at[_dev_index(*peer)], dst_ref=outs[w].at[me], send_sem=send_sems.at[w, k],
                    recv_sem=recv_sems.at[w, k], device_id=peer, device_id_type=MESH))
        return out

    def start():
        for cp in copies():
            cp.start()

    def finish():
        for cp in copies():
            cp.wait()

    return start, finish


def _scatter_scratch(nw):
    if nw == 0:
        return []
    return [pltpu.SemaphoreType.DMA((nw, 7)), pltpu.SemaphoreType.DMA((nw, 7)), pltpu.SemaphoreType.DMA((nw,))]


TM = 512


def _join_w_in(wg):
    sw = IN_W // NDEV

    def body(wg_ref, w_ref):
        for j in range(NDEV):
            w_ref[:, sw * j:sw * (j + 1)] = wg_ref[j]
        w_ref[:, IN_W:PW] = jnp.zeros((D, PW - IN_W), BF16)

    return pl.pallas_call(body, name="join_w_in", out_shape=jax.ShapeDtypeStruct((D, PW), BF16),
                          compiler_params=_params())(wg)


def _in_proj(x, g1, w, rc, ra, rb, shards, dtypes):
    tm = TM
    nw = len(shards)
    nt = S // tm

    def body(*refs):
        x_ref, g_ref, w_ref, rc_ref, ra_ref, rb_ref = refs[:6]
        ins = refs[6:6 + nw]
        qkv_ref, mqk_ref, mv_ref, mo_ref, gt_ref, u_ref = refs[6 + nw:12 + nw]
        outs = refs[12 + nw:12 + 2 * nw]
        bufs = refs[12 + 2 * nw:12 + 3 * nw]
        ag_start, ag_forward, ag_finish = _gather_phases(ins, outs, bufs, *refs[12 + 3 * nw:])
        i = pl.program_id(0)
        pl.when(i == 0)(ag_start)
        pl.when(i == nt - 2)(ag_forward)
        n, _ = _rms(x_ref[...])
        u = _bf(n * g_ref[...])
        u_ref[...] = u
        c, a, b = rc_ref[...], ra_ref[...], rb_ref[...]
        for half in range(2):
            blk = _dot(u, w_ref[:, half * 512:(half + 1) * 512])
            for t in range(4):
                lo = half * 512 + t * 128
                qkv_ref[:, lo:lo + 128] = _rope(blk[:, t * 128:(t + 1) * 128], c, a, b)
        qkv_ref[:, 1024:1536] = _dot(u, w_ref[:, 1024:1536])
        mqk_ref[:, 0:512] = _dot(u, w_ref[:, 1536:2048])
        mqk_ref[:, 512:1024] = _dot(u, w_ref[:, 2048:2560])
        mv_ref[...] = _dot(u, w_ref[:, 2560:3072])
        mo_ref[...] = _dot(u, w_ref[:, 3072:3584])
        gt_ref[...] = _dot(u, w_ref[:, 3584:3712])
        pl.when(i == nt - 1)(ag_finish)

    row = lambda wd: pl.BlockSpec((tm, wd), lambda i: (i, 0))
    res = pl.pallas_call(
        body, name="in_proj", grid=(nt,),
        in_specs=[row(D), _cspec((1, D)), _cspec((D, PW)), row(128), row(128), row(128)] + [VM] * nw,
        out_specs=[row(1536), row(1024), row(512), row(512), row(128), row(D)] + [ANY] * nw,
        out_shape=[jax.ShapeDtypeStruct((S, 1536), F32), jax.ShapeDtypeStruct((S, 1024), F32),
                   jax.ShapeDtypeStruct((S, 512), F32), jax.ShapeDtypeStruct((S, 512), F32),
                   jax.ShapeDtypeStruct((S, 128), F32), jax.ShapeDtypeStruct((S, D), BF16)]
        + _gather_shapes(shards, dtypes),
        scratch_shapes=_gather_scratch(shards, dtypes),
        compiler_params=_params(1),
    )(x, g1, w, rc, ra, rb, *shards)
    return res[:6], res[6:]


DILATIONS = (16, 4, 1)


def _attn_valid(n):
    kd = lax.broadcasted_iota(jnp.int32, (128, 256), 1) - lax.broadcasted_iota(jnp.int32, (128, 256), 0)
    off = jnp.where(n == 0, 0, 128)
    return (kd <= off) & (kd >= off - 128)


def _attn_rows(d, r, n):
    if d == 1:
        q0 = pl.multiple_of(n * 128, 128)
        k0 = pl.multiple_of(jnp.maximum(n - 1, 0) * 128, 128)
        return pl.ds(q0, 128), pl.ds(k0, 256), _attn_valid(n)
    q0 = r + n * 128 * d
    k0 = r + jnp.maximum(n - 1, 0) * 128 * d
    return pl.ds(q0, 128, stride=d), pl.ds(k0, 256, stride=d), _attn_valid(n)


ATTN_GROUP = 4
ATTN_ITERS = S // 128 // ATTN_GROUP


def _attn_group(d, i):
    nb = S // (128 * d)
    if nb == 2:
        qi = lax.broadcasted_iota(jnp.int32, (256, 256), 0) - lax.broadcasted_iota(jnp.int32, (256, 256), 1)
        whole = [pl.ds((ATTN_GROUP // 2) * i + u, 256, stride=d) for u in range(ATTN_GROUP // 2)]
        return [(rows, rows, (qi >= 0) & (qi <= 128)) for rows in whole]
    if d == 1:
        return [_attn_rows(1, 0, i + ATTN_ITERS * u) for u in range(ATTN_GROUP)]
    return [_attn_rows(d, (i // nb) * ATTN_GROUP + u, i % nb) for u in range(ATTN_GROUP)]


def _head0(shape):
    return lax.broadcasted_iota(jnp.int32, shape, 1) < 64


def _stack_heads(t):
    h0 = _head0(t.shape)
    tb = _bf(t)
    zero = jnp.zeros_like(tb)
    return jnp.concatenate([jnp.where(h0, tb, zero), jnp.where(h0, zero, tb)], axis=0)


def _attn_fwd(qkv, shards, dtypes):
    nw = len(shards)

    def body(*refs):
        q_ref, k_ref, v_ref = refs[:3]
        ins = refs[3:3 + nw]
        o_ref, lse_ref = refs[3 + nw:5 + nw]
        outs = refs[5 + nw:5 + 2 * nw]
        m0, m1, l0, l1, acc = refs[5 + 2 * nw:10 + 2 * nw]
        bufs = refs[10 + 2 * nw:10 + 3 * nw]
        ag_start, ag_forward, ag_finish = _gather_phases(ins, outs, bufs, *refs[10 + 3 * nw:])
        hp = pl.program_id(0)
        pl.when(hp == 0)(ag_start)
        pl.when(hp == 3)(ag_forward)
        stats = (m0, m1, l0, l1, acc)

        def update(blocks, first):
            loaded = [([q_ref[rq, :], k_ref[rk, :], v_ref[rk, :]], None if first else [ref[rq, :] for ref in stats])
                      for rq, rk, _ in blocks]
            results = []
            for ((q, k, v), prev), (_, _, valid) in zip(loaded, blocks):
                head0 = _head0(q.shape)
                kb, vb = _bf(k), _bf(v)
                q = q * 0.125
                m_new, l_new, acc_new = [], [], []
                for a, qa in enumerate((_bf(jnp.where(head0, q, 0.0)), _bf(jnp.where(head0, 0.0, q)))):
                    s = jnp.where(valid, _dot_nt(qa, kb), NEG)
                    mc = jnp.max(s, axis=-1, keepdims=True)
                    m_a = jnp.broadcast_to(mc, q.shape) if first else jnp.maximum(prev[a], mc)
                    p = jnp.exp(s - jnp.tile(m_a, (1, 2)))
                    l_add = jnp.sum(p, axis=-1, keepdims=True)
                    pv = _dot(_bf(p), vb)
                    if first:
                        l_a = jnp.broadcast_to(l_add, q.shape)
                    else:
                        alpha = jnp.exp(prev[a] - m_a)
                        l_a, pv = alpha * prev[2 + a] + l_add, alpha * prev[4] + pv
                    m_new.append(m_a), l_new.append(l_a), acc_new.append(pv)
                results.append((m_new[0], m_new[1], l_new[0], l_new[1], jnp.where(head0, acc_new[0], acc_new[1])))
            for (rq, _, _), res in zip(blocks, results):
                for ref, val in zip(stats, res):
                    ref[rq, :] = val

        for d in DILATIONS:
            def step(i, carry, d=d):
                update(_attn_group(d, i), d == DILATIONS[0])
                return carry

            lax.fori_loop(0, ATTN_ITERS, step, 0)

        def fin(t, carry):
            rows = pl.ds(pl.multiple_of(t * 256, 256), 256)
            h0 = lax.broadcasted_iota(jnp.int32, (256, 128), 1) < 64
            l = jnp.where(h0, l0[rows, :], l1[rows, :])
            o_ref[rows, :] = acc[rows, :] / l
            lse_ref[rows, :] = jnp.where(h0, m0[rows, :], m1[rows, :]) + jnp.log(l)
            return carry

        lax.fori_loop(0, S // 256, fin, 0)
        pl.when(hp == 3)(ag_finish)

    col = lambda off: pl.BlockSpec((S, 128), lambda h, off=off: (0, off + h))
    res = pl.pallas_call(
        body, name="attn_fwd", grid=(4,),
        in_specs=[col(0), col(4), col(8)] + [VM] * nw,
        out_specs=[col(0), col(0)] + [ANY] * nw,
        out_shape=[jax.ShapeDtypeStruct((S, AW), F32), jax.ShapeDtypeStruct((S, AW), F32)]
        + _gather_shapes(shards, dtypes),
        scratch_shapes=[pltpu.VMEM((S, 128), F32)] * 5 + _gather_scratch(shards, dtypes),
        compiler_params=_params(1),
    )(qkv, qkv, qkv, *shards)
    return res[0], res[1], res[2:]


def _attn_bwd(qkv, o, lse, do, parts):
    nw = len(parts)

    def body(*refs):
        q_ref, k_ref, v_ref, o_ref, lse_ref, do_ref = refs[:6]
        ins = refs[6:6 + nw]
        dq_ref, dk_ref, dv_ref = refs[6 + nw:9 + nw]
        outs = refs[9 + nw:9 + 2 * nw]
        L0, L1, D0, D1 = refs[9 + 2 * nw:13 + 2 * nw]
        rs_start, rs_finish = _scatter_phases(ins, outs, *refs[13 + 2 * nw:])
        hp = pl.program_id(0)
        pl.when(hp == 0)(rs_start)
        def pre(t, carry):
            rows = pl.ds(pl.multiple_of(t * 256, 256), 256)
            h0 = lax.broadcasted_iota(jnp.int32, (256, 128), 1) < 64
            ls = lse_ref[rows, :]
            dd = do_ref[rows, :] * o_ref[rows, :]
            shp = (256, 128)
            L0[rows, :] = jnp.broadcast_to(jnp.max(jnp.where(h0, ls, NEG), axis=-1, keepdims=True), shp)
            L1[rows, :] = jnp.broadcast_to(jnp.max(jnp.where(h0, NEG, ls), axis=-1, keepdims=True), shp)
            D0[rows, :] = jnp.broadcast_to(jnp.sum(jnp.where(h0, dd, 0.0), axis=-1, keepdims=True), shp)
            D1[rows, :] = jnp.broadcast_to(jnp.sum(jnp.where(h0, 0.0, dd), axis=-1, keepdims=True), shp)
            return carry

        lax.fori_loop(0, S // 256, pre, 0)

        def update(blocks, first):
            loaded = [([q_ref[rq, :], k_ref[rk, :], v_ref[rk, :], do_ref[rq, :]],
                       [L0[rq, :], L1[rq, :], D0[rq, :], D1[rq, :]],
                       [0.0] * 3 if first else [dq_ref[rq, :], dk_ref[rk, :], dv_ref[rk, :]]) for rq, rk, _ in blocks]
            results = []
            for ((q, k, v, dout), (l0v, l1v, d0v, d1v), (dq, dk, dv)), (_, _, valid) in zip(loaded, blocks):
                valid = jnp.tile(valid, (1, 2))
                kst, vst = _stack_heads(k), _stack_heads(v)
                hk = _head0((256, 128))
                dob = _bf(dout)
                cat = lambda a, b: jnp.concatenate([jnp.tile(a, (1, 2)), jnp.tile(b, (1, 2))], axis=1)
                s = jnp.where(valid, _dot_nt(_bf(q * 0.125), kst), NEG)
                p = jnp.exp(s - cat(l0v, l1v))
                ds = _bf(p * (_dot_nt(dob, vst) - cat(d0v, d1v)) * 0.125)
                dk2 = _dot_tn(ds, _bf(q))
                dv2 = _dot_tn(_bf(p), dob)
                results.append((dq + _dot(ds, kst), dk + jnp.where(hk, dk2[0:256], dk2[256:512]),
                                dv + jnp.where(hk, dv2[0:256], dv2[256:512])))
            for (rq, rk, _), (dq, dk, dv) in zip(blocks, results):
                dq_ref[rq, :] = dq
                dk_ref[rk, :] = dk
                dv_ref[rk, :] = dv

        assert S // (128 * DILATIONS[0]) == 2
        for d in DILATIONS:
            def step(i, carry, d=d):
                update(_attn_group(d, i), d == DILATIONS[0])
                return carry

            lax.fori_loop(0, ATTN_ITERS, step, 0)
        pl.when(hp == 3)(rs_finish)

    col = lambda off: pl.BlockSpec((S, 128), lambda h, off=off: (0, off + h))
    res = pl.pallas_call(
        body, name="attn_bwd", grid=(4,),
        in_specs=[col(0), col(4), col(8), col(0), col(0), col(0)] + [ANY] * nw,
        out_specs=[col(0), col(0), col(0)] + [ANY] * nw,
        out_shape=[jax.ShapeDtypeStruct((S, AW), F32)] * 3 + [jax.ShapeDtypeStruct(a.shape, a.dtype) for a in parts],
        scratch_shapes=[pltpu.VMEM((S, 128), F32)] * 4 + _scatter_scratch(nw),
        compiler_params=_params(1),
    )(qkv, qkv, qkv, o, lse, do, *parts)
    return res[0], res[1], res[2], res[3:]


def _logsig(x):
    return jnp.minimum(x, 0.0) - jnp.log1p(jnp.exp(-jnp.abs(x)))


def _conv_taps(xp, n):
    return [xp[8:] if j == 3 else pltpu.roll(xp, 3 - j, 0)[8:] for j in range(4)]


def _conv_silu(xp, w_ref, b_ref, n):
    taps = _conv_taps(xp, n)
    c = b_ref[...] + sum(w_ref[j:j + 1, :] * taps[j] for j in range(4))
    sg = _sigmoid(c)
    return c, sg, taps


def _chunk_gates(G):
    assert LC == 128
    r = lax.broadcasted_iota(jnp.int32, (LC, LC), 0)
    c = lax.broadcasted_iota(jnp.int32, (LC, LC), 1)
    tril = (c <= r).astype(F32)
    triu = (c >= r).astype(F32)
    b_col = jnp.dot(tril, _logsig(G), preferred_element_type=F32, precision=HI)
    return b_col, b_col.T, G.T, tril, triu


def _colpick(X, lane):
    li = lax.broadcasted_iota(jnp.int32, X.shape, 1)
    return jnp.sum(jnp.where(li == lane, X, 0.0), axis=1, keepdims=True)


def _rowpick(XT, row):
    ri = lax.broadcasted_iota(jnp.int32, XT.shape, 0)
    return jnp.sum(jnp.where(ri == row, XT, 0.0), axis=0, keepdims=True)


def _mlstm_head(qh, kh, vh, G, b_col, b_row, g_row, h, Ch, nh, m_prev):
    bt = _colpick(b_col, 4 + h)
    i_col = _colpick(G, h)
    bs = _rowpick(b_row, 4 + h)
    i_row = _rowpick(g_row, h)
    r = lax.broadcasted_iota(jnp.int32, (LC, LC), 0)
    c = lax.broadcasted_iota(jnp.int32, (LC, LC), 1)
    log_d = jnp.where(c <= r, bt - bs + i_row, NEG)
    log_inter = bt + m_prev
    m_t = jnp.maximum(log_inter, jnp.max(log_d, axis=1, keepdims=True))
    Dm = jnp.exp(log_d - m_t)
    g = jnp.exp(log_inter - m_t)
    qb, kb, vb = _bf(qh), _bf(kh), _bf(vh)
    Am = _dot_nt(qb, kb) * Dm
    qC = _dot(qb, _bf(Ch))
    num = g * qC + _dot(_bf(Am), vb)
    qn = jnp.sum(qh * nh, axis=1, keepdims=True)
    den = g * qn + jnp.sum(Am, axis=1, keepdims=True)
    floor = jnp.exp(-m_t)
    dd = jnp.maximum(jnp.abs(den), floor)
    inv_dd = 1.0 / dd
    hh = num * inv_dd
    lane = lax.broadcasted_iota(jnp.int32, (1, LC), 1)
    blast = jnp.sum(jnp.where(lane == LC - 1, bs, 0.0), axis=1, keepdims=True)
    log_s = blast - bt + i_col
    m_new = jnp.maximum(blast + m_prev, jnp.max(log_s, axis=0, keepdims=True))
    decay = jnp.exp(blast + m_prev - m_new)
    ws = jnp.exp(log_s - m_new)
    kw = kh * ws
    C_new = decay * Ch + _dot_tn(_bf(kw), vb)
    n_new = decay * nh + jnp.sum(kw, axis=0, keepdims=True)
    return dict(Dm=Dm, g=g, Am=Am, qC=qC, qn=qn, den=den, floor=floor, inv_dd=inv_dd, h=hh, decay=decay, ws=ws, kw=kw,
                C_new=C_new, n_new=n_new, m_new=m_new, qb=qb, kb=kb, vb=vb)


def _head_out(hh, mo_h, gn_h):
    r = lax.rsqrt(jnp.mean(hh * hh, axis=-1, keepdims=True) + EPS)
    hn = hh * r
    sg = _sigmoid(mo_h)
    return sg * (hn * gn_h), hn, r, sg


def _mlstm_fwd(mqk, mv, mo, gates, conv_w, conv_b, gate_b, gn, shards, dtypes):
    nblk = S // TB
    ncb = TB // LC
    nw = len(shards)

    def body(*refs):
        x_ref, v_ref, o_ref, g_ref, w_ref, b_ref, gb_ref, gn_ref = refs[:8]
        ins = refs[8:8 + nw]
        out_ref, cs_ref, ns_ref, ms_ref = refs[8 + nw:12 + nw]
        outs = refs[12 + nw:12 + 2 * nw]
        tail, Cst, nst, mst, qs, ks = refs[12 + 2 * nw:18 + 2 * nw]
        bufs = refs[18 + 2 * nw:18 + 3 * nw]
        ag_start, ag_forward, ag_finish = _gather_phases(ins, outs, bufs, *refs[18 + 3 * nw:])
        i = pl.program_id(0)
        pl.when(i == 0)(ag_start)
        pl.when(i == nblk // 2)(ag_forward)

        @pl.when(i == 0)
        def _():
            tail[...] = jnp.zeros_like(tail)
            Cst[...] = jnp.zeros_like(Cst)
            nst[...] = jnp.zeros_like(nst)
            mst[...] = jnp.zeros_like(mst)

        x = x_ref[...]
        xp = jnp.concatenate([tail[...], x], axis=0)
        tail[...] = x[TB - 8:TB, :]
        c, sg, _ = _conv_silu(xp, w_ref, b_ref, TB)
        y = c * sg
        qs[...] = y[:, 0:MW]
        ks[...] = y[:, MW:2 * MW] * (1.0 / math.sqrt(128.0))

        for cc in range(ncb):
            rows = slice(cc * LC, (cc + 1) * LC)
            G = g_ref[rows, :] + gb_ref[...]
            b_col, b_row, g_row, _, _ = _chunk_gates(G)
            cs_ref[cc] = Cst[...]
            ns_ref[cc] = nst[...]
            ms_ref[cc] = mst[...]
            for h in range(4):
                ln = slice(h * 128, (h + 1) * 128)
                m_prev = jnp.max(mst[0:1, ln], axis=1, keepdims=True)
                f = _mlstm_head(qs[rows, ln], ks[rows, ln], v_ref[rows, ln], G, b_col, b_row, g_row, h,
                                Cst[:, ln], nst[0:1, ln], m_prev)
                out, _, _, _ = _head_out(f["h"], o_ref[rows, ln], gn_ref[:, ln])
                out_ref[rows, ln] = out
                Cst[:, ln] = f["C_new"]
                nst[0:1, ln] = f["n_new"]
                mst[0:1, ln] = jnp.broadcast_to(f["m_new"], (1, 128))
        pl.when(i == nblk - 1)(ag_finish)

    row = lambda wd: pl.BlockSpec((TB, wd), lambda i: (i, 0))
    res = pl.pallas_call(
        body, name="mlstm_fwd", grid=(nblk,),
        in_specs=[row(1024), row(MW), row(MW), row(128), _cspec((4, 1024)), _cspec((1, 1024)), _cspec((1, 128)),
                  _cspec((1, MW))] + [VM] * nw,
        out_specs=[row(MW), pl.BlockSpec((ncb, 128, MW), lambda i: (i, 0, 0)),
                   pl.BlockSpec((ncb, 8, MW), lambda i: (i, 0, 0)), pl.BlockSpec((ncb, 8, MW), lambda i: (i, 0, 0))]
        + [ANY] * nw,
        out_shape=[jax.ShapeDtypeStruct((S, MW), F32), jax.ShapeDtypeStruct((S // LC, 128, MW), F32),
                   jax.ShapeDtypeStruct((S // LC, 8, MW), F32), jax.ShapeDtypeStruct((S // LC, 8, MW), F32)]
        + _gather_shapes(shards, dtypes),
        scratch_shapes=[pltpu.VMEM((8, 1024), F32), pltpu.VMEM((128, MW), F32), pltpu.VMEM((8, MW), F32),
                        pltpu.VMEM((8, MW), F32), pltpu.VMEM((TB, MW), F32), pltpu.VMEM((TB, MW), F32)]
        + _gather_scratch(shards, dtypes),
        compiler_params=_params(1),
    )(mqk, mv, mo, gates, conv_w, conv_b, gate_b, gn, *shards)
    return res[0], res[1], res[2], res[3], res[4:]


def _mlstm_bwd(mqk, mv, mo, gates, conv_w, conv_b, gate_b, gn, cs, ns, ms, dout, parts):
    nblk = S // TB
    ncb = TB // LC
    kscale = 1.0 / math.sqrt(128.0)
    nw = len(parts)

    def body(*refs):
        x_ref, xprev_ref, v_ref, o_ref, g_ref, w_ref, b_ref, gb_ref, gn_ref, cs_ref, ns_ref, ms_ref, do_ref = refs[:13]
        ins = refs[13:13 + nw]
        dx_ref, dv_ref, dmo_ref, dg_ref, dw_ref, db_ref, dgn_ref, dgb_ref = refs[13 + nw:21 + nw]
        outs = refs[21 + nw:21 + 2 * nw]
        dCst, dnst, dyhead, qs, ks, dqk = refs[21 + 2 * nw:27 + 2 * nw]
        rs_start, rs_finish = _scatter_phases(ins, outs, *refs[27 + 2 * nw:])
        i = pl.program_id(0)
        blk = nblk - 1 - i
        pl.when(i == 0)(rs_start)

        @pl.when(i == 0)
        def _():
            dCst[...] = jnp.zeros_like(dCst)
            dnst[...] = jnp.zeros_like(dnst)
            dyhead[...] = jnp.zeros_like(dyhead)
            dw_ref[...] = jnp.zeros_like(dw_ref)
            db_ref[...] = jnp.zeros_like(db_ref)
            dgn_ref[...] = jnp.zeros_like(dgn_ref)
            dgb_ref[...] = jnp.zeros_like(dgb_ref)

        x = x_ref[...]
        xprev = jnp.where(blk == 0, 0.0, xprev_ref[...])
        xp = jnp.concatenate([xprev, x], axis=0)
        c, sg, taps = _conv_silu(xp, w_ref, b_ref, TB)
        y = c * sg
        qs[...] = y[:, 0:MW]
        ks[...] = y[:, MW:2 * MW] * kscale
        lane128 = lax.broadcasted_iota(jnp.int32, (LC, 128), 1)
        rowi = lax.broadcasted_iota(jnp.int32, (LC, 1), 0)
        ones = jnp.ones((LC, 128), F32)

        for cc in reversed(range(ncb)):
            rows = slice(cc * LC, (cc + 1) * LC)
            G = g_ref[rows, :] + gb_ref[...]
            b_col, b_row, g_row, _, triu = _chunk_gates(G)
            dB = jnp.zeros((LC, 128), F32)
            dI = jnp.zeros((LC, 128), F32)
            for h in range(4):
                ln = slice(h * 128, (h + 1) * 128)
                Ch = cs_ref[cc, :, ln]
                nh = ns_ref[cc, 0:1, ln]
                m_prev = jnp.max(ms_ref[cc, 0:1, ln], axis=1, keepdims=True)
                qh, kh, vh = qs[rows, ln], ks[rows, ln], v_ref[rows, ln]
                f = _mlstm_head(qh, kh, vh, G, b_col, b_row, g_row, h, Ch, nh, m_prev)
                hh, inv_dd, den, g, Am, Dm = f["h"], f["inv_dd"], f["den"], f["g"], f["Am"], f["Dm"]
                qb, kb, vb = f["qb"], f["kb"], f["vb"]
                gn_h = gn_ref[:, ln]
                _, hn, r, sgo = _head_out(hh, o_ref[rows, ln], gn_h)
                do = do_ref[rows, ln]
                hm = hn * gn_h
                dmo_ref[rows, ln] = do * hm * sgo * (1.0 - sgo)
                dhm = do * sgo
                dgn_ref[:, ln] = dgn_ref[:, ln] + jnp.sum(dhm * hn, axis=0, keepdims=True)
                dhn = dhm * gn_h
                dh = r * (dhn - hn * jnp.mean(dhn * hn, axis=-1, keepdims=True))
                dnum = dh * inv_dd
                ddd = -jnp.sum(dh * hh, axis=1, keepdims=True) * inv_dd
                dden = jnp.where(jnp.abs(den) >= f["floor"], ddd * jnp.sign(den), 0.0)
                dnb = _bf(dnum)
                dA = _dot_nt(dnb, vb) + dden
                dv = _dot_tn(_bf(Am), dnb)
                gd = _bf(g * dnum)
                gq = g * dden
                dq = _dot_nt(gd, _bf(Ch)) + gq * nh
                dCn = dCst[:, ln]
                dnn = dnst[0:1, ln]
                dC = f["decay"] * dCn + _dot_tn(qb, gd)
                dn = f["decay"] * dnn + jnp.sum(gq * qh, axis=0, keepdims=True)
                dg = jnp.sum(dnum * f["qC"], axis=1, keepdims=True) + dden * f["qn"]
                dS = _bf(dA * Dm)
                dq = dq + _dot(dS, kb)
                dk = _dot_tn(dS, qb)
                Gm = dA * Am
                gam = dg * g
                dCb = _bf(dCn)
                E = _dot_nt(vb, dCb) + dnn
                ws = f["ws"]
                dk = dk + ws * E
                om = jnp.sum(E * kh, axis=1, keepdims=True) * ws
                dv = dv + _dot(_bf(f["kw"]), dCb)
                ddecay = (jnp.sum(jnp.sum(dCn * Ch, axis=1, keepdims=True), axis=0, keepdims=True)
                          + jnp.sum(dnn * nh, axis=1, keepdims=True))
                delta = ddecay * f["decay"]
                rows_g = jnp.sum(Gm, axis=1, keepdims=True)
                cols_g = jnp.broadcast_to(jnp.sum(Gm, axis=0, keepdims=True), (LC, 128)).T
                last = jnp.where(rowi == LC - 1, jnp.sum(om, axis=0, keepdims=True) + delta, 0.0)
                db = rows_g + gam - om + last - cols_g
                di = cols_g + om
                dB = jnp.where(lane128 == 4 + h, db, dB)
                dI = jnp.where(lane128 == h, di, dI)
                dCst[:, ln] = dC
                dnst[0:1, ln] = dn
                dqk[rows, ln] = dq
                dqk[rows, MW + h * 128:MW + (h + 1) * 128] = dk * kscale
                dv_ref[rows, ln] = dv
            dlogf = jnp.dot(triu, dB, preferred_element_type=F32, precision=HI)
            dG = dI + dlogf * _sigmoid(-G)
            dG = jnp.where(lane128 < 8, dG, 0.0)
            dg_ref[rows, :] = dG
            dgb_ref[...] = dgb_ref[...] + jnp.sum(dG, axis=0, keepdims=True)

        dy = dqk[...] * (sg * (1.0 + c * (1.0 - sg)))
        db_ref[...] = db_ref[...] + jnp.sum(dy, axis=0, keepdims=True)
        for j in range(4):
            dw_ref[j:j + 1, :] = dw_ref[j:j + 1, :] + jnp.sum(dy * taps[j], axis=0, keepdims=True)
        dyp = jnp.concatenate([dy, dyhead[...]], axis=0)
        dx = w_ref[3:4, :] * dy
        for j in range(3):
            dx = dx + w_ref[j:j + 1, :] * pltpu.roll(dyp, TB + 8 - (3 - j), 0)[0:TB]
        dx_ref[...] = dx
        dyhead[...] = dy[0:8, :]
        pl.when(i == nblk - 1)(rs_finish)

    rrow = lambda wd: pl.BlockSpec((TB, wd), lambda i: (nblk - 1 - i, 0))
    st = lambda r: pl.BlockSpec((ncb, r, MW), lambda i: (nblk - 1 - i, 0, 0))
    prev8 = pl.BlockSpec((8, 1024), lambda i: (jnp.maximum((nblk - 1 - i) * (TB // 8) - 1, 0), 0))
    res = pl.pallas_call(
        body, name="mlstm_bwd", grid=(nblk,),
        in_specs=[rrow(1024), prev8, rrow(MW), rrow(MW), rrow(128), _cspec((4, 1024)), _cspec((1, 1024)),
                  _cspec((1, 128)), _cspec((1, MW)), st(128), st(8), st(8), rrow(MW)] + [ANY] * nw,
        out_specs=[rrow(1024), rrow(MW), rrow(MW), rrow(128),
                   pl.BlockSpec((4, 1024), lambda i: (0, 0)), pl.BlockSpec((1, 1024), lambda i: (0, 0)),
                   pl.BlockSpec((1, MW), lambda i: (0, 0)), pl.BlockSpec((1, 128), lambda i: (0, 0))] + [ANY] * nw,
        out_shape=[jax.ShapeDtypeStruct((S, 1024), F32), jax.ShapeDtypeStruct((S, MW), F32),
                   jax.ShapeDtypeStruct((S, MW), F32), jax.ShapeDtypeStruct((S, 128), F32),
                   jax.ShapeDtypeStruct((4, 1024), F32), jax.ShapeDtypeStruct((1, 1024), F32),
                   jax.ShapeDtypeStruct((1, MW), F32), jax.ShapeDtypeStruct((1, 128), F32)]
        + [jax.ShapeDtypeStruct(a.shape, a.dtype) for a in parts],
        scratch_shapes=[pltpu.VMEM((128, MW), F32), pltpu.VMEM((8, MW), F32), pltpu.VMEM((8, 1024), F32),
                        pltpu.VMEM((TB, MW), F32), pltpu.VMEM((TB, MW), F32), pltpu.VMEM((TB, 1024), F32)]
        + _scatter_scratch(nw),
        compiler_params=_params(1),
    )(mqk, mqk, mv, mo, gates, conv_w, conv_b, gate_b, gn, cs, ns, ms, dout, *parts)
    return res[:8], res[8:]


def _out_proj(x, attn, ml, w, g):
    tm = TM

    def body(x_ref, a_ref, m_ref, w_ref, g_ref, h_ref, u_ref):
        h1 = x_ref[...] + _dot(_bf(a_ref[...]), w_ref[0:AW, :]) + _dot(_bf(m_ref[...]), w_ref[AW:D, :])
        h_ref[...] = h1
        n, _ = _rms(h1)
        u_ref[...] = _bf(n * g_ref[...])

    row = lambda wd: pl.BlockSpec((tm, wd), lambda i: (i, 0))
    return pl.pallas_call(
        body, name="out_proj", grid=(S // tm,),
        in_specs=[row(D), row(AW), row(MW), _cspec((D, D)), _cspec((1, D))],
        out_specs=[row(D), row(D)],
        out_shape=[jax.ShapeDtypeStruct((S, D), F32), jax.ShapeDtypeStruct((S, D), BF16)],
        compiler_params=_params(1),
    )(x, attn, ml, w, g)


HALF = DFF // NDEV // 2


def _mlp_fwd(h1, u2, w_up, w_down_a, w_down_b):
    tm = TM

    def body(h_ref, u_ref, wu_ref, wa_ref, wb_ref, a_ref, o_ref):
        u = u_ref[...]
        acc = h_ref[...]
        for c in range(NDEV):
            cols = slice(c * 512, (c + 1) * 512)
            a = _dot(u, wu_ref[c])
            a_ref[:, cols] = _bf(a)
            r = jnp.maximum(a, 0.0)
            r = _bf(r * r)
            acc = acc + _dot(r[:, 0:HALF], wa_ref[c]) + _dot(r[:, HALF:2 * HALF], wb_ref[c])
        o_ref[...] = acc

    row = lambda wd: pl.BlockSpec((tm, wd), lambda i: (i, 0))
    return pl.pallas_call(
        body, name="mlp_fwd", grid=(S // tm,),
        in_specs=[row(D), row(D), _cspec((NDEV, D, DFF // NDEV)), _cspec((NDEV, HALF, D)), _cspec((NDEV, HALF, D))],
        out_specs=[row(DFF), row(D)],
        out_shape=[jax.ShapeDtypeStruct((S, DFF), BF16), jax.ShapeDtypeStruct((S, D), F32)],
        compiler_params=_params(1),
    )(h1, u2, w_up, w_down_a, w_down_b)


def _ple_loss(h2, p, target, w_pg, w_ple, g_ple, g_fin):
    tm = TM

    def body(h_ref, p_ref, t_ref, wg_ref, wp_ref, gp_ref, gf_ref,
             dh_ref, dwg_ref, dwp_ref, dgp_ref, dgf_ref, loss_ref, acc_g, acc_p):
        i = pl.program_id(0)

        @pl.when(i == 0)
        def _():
            acc_g[...] = jnp.zeros_like(acc_g)
            acc_p[...] = jnp.zeros_like(acc_p)
            dgp_ref[...] = jnp.zeros_like(dgp_ref)
            dgf_ref[...] = jnp.zeros_like(dgf_ref)
            loss_ref[...] = jnp.zeros_like(loss_ref)

        h2v = h_ref[...]
        n2, rs2 = _rms(h2v)
        u3 = _bf(n2 * gp_ref[...])
        gt = _sigmoid(_dot(u3, wg_ref[...]))
        pb = _bf(p_ref[...])
        e = jnp.concatenate([_dot(pb, wp_ref[j]) for j in range(NDEV)], axis=1)
        h3 = h2v + gt * e
        n3, rs3 = _rms(h3)
        err = n3 * gf_ref[...] - t_ref[...]
        loss_ref[...] = loss_ref[...] + 0.5 / D * jnp.sum(jnp.sum(err * err, axis=1, keepdims=True), axis=0, keepdims=True)
        dy = err * (1.0 / D)
        dgf_ref[...] = dgf_ref[...] + jnp.sum(dy * n3, axis=0, keepdims=True)
        dh3 = _rms_bwd(dy, n3, rs3, gf_ref[...])
        de = _bf(dh3 * gt)
        dz = _bf(dh3 * e * gt * (1.0 - gt))
        acc_p[...] = acc_p[...] + _dot_tn(pb, de)
        acc_g[...] = acc_g[...] + _dot_tn(u3, dz)
        du3 = _dot_nt(dz, wg_ref[...])
        dgp_ref[...] = dgp_ref[...] + jnp.sum(du3 * n2, axis=0, keepdims=True)
        dh_ref[...] = dh3 + _rms_bwd(du3, n2, rs2, gp_ref[...])

        @pl.when(i == S // tm - 1)
        def _():
            dwg_ref[...] = _bf(acc_g[...])
            for j in range(NDEV):
                dwp_ref[j] = _bf(acc_p[:, j * 128:(j + 1) * 128])

    row = lambda wd: pl.BlockSpec((tm, wd), lambda i: (i, 0))
    whole = lambda shp: pl.BlockSpec(shp, lambda i: (0,) * len(shp))
    return pl.pallas_call(
        body, name="ple_loss", grid=(S // tm,),
        in_specs=[row(D), row(PLE), row(D), _cspec((D, D)), _cspec((NDEV, PLE, 128)), _cspec((1, D)), _cspec((1, D))],
        out_specs=[row(D), whole((D, D)), whole((NDEV, PLE, 128)), whole((1, D)), whole((1, D)), whole((1, 1))],
        out_shape=[jax.ShapeDtypeStruct((S, D), F32), jax.ShapeDtypeStruct((D, D), BF16),
                   jax.ShapeDtypeStruct((NDEV, PLE, 128), BF16), jax.ShapeDtypeStruct((1, D), F32),
                   jax.ShapeDtypeStruct((1, D), F32), jax.ShapeDtypeStruct((1, 1), F32)],
        scratch_shapes=[pltpu.VMEM((D, D), F32), pltpu.VMEM((PLE, D), F32)],
        compiler_params=_params(1),
    )(h2, p, target, w_pg, w_ple, g_ple, g_fin)


def _mlp_bwd(dh2, a, h1, g, w_up, w_down_a, w_down_b):
    tm = TM

    def body(d_ref, a_ref, h_ref, g_ref, wu_ref, wa_ref, wb_ref, da_ref, dh1_ref, dg_ref):
        @pl.when(pl.program_id(0) == 0)
        def _():
            dg_ref[...] = jnp.zeros_like(dg_ref)

        dh2v = d_ref[...]
        db = _bf(dh2v)
        du = jnp.zeros((tm, D), F32)
        for c in range(NDEV):
            cols = slice(c * 512, (c + 1) * 512)
            dr = jnp.concatenate([_dot_nt(db, wa_ref[c]), _dot_nt(db, wb_ref[c])], axis=1)
            da = _bf(dr * (2.0 * jnp.maximum(a_ref[:, cols], 0.0)))
            da_ref[:, cols] = da
            du = du + _dot_nt(da, wu_ref[c])
        n, rs = _rms(h_ref[...])
        dg_ref[...] = dg_ref[...] + jnp.sum(du * n, axis=0, keepdims=True)
        dh1_ref[...] = dh2v + _rms_bwd(du, n, rs, g_ref[...])

    row = lambda wd: pl.BlockSpec((tm, wd), lambda i: (i, 0))
    return pl.pallas_call(
        body, name="mlp_bwd", grid=(S // tm,),
        in_specs=[row(D), row(DFF), row(D), _cspec((1, D)), _cspec((NDEV, D, DFF // NDEV)), _cspec((NDEV, HALF, D)),
                  _cspec((NDEV, HALF, D))],
        out_specs=[row(DFF), row(D), pl.BlockSpec((1, D), lambda i: (0, 0))],
        out_shape=[jax.ShapeDtypeStruct((S, DFF), BF16), jax.ShapeDtypeStruct((S, D), F32),
                   jax.ShapeDtypeStruct((1, D), F32)],
        compiler_params=_params(1),
    )(dh2, a, h1, g, w_up, w_down_a, w_down_b)


def _out_proj_bwd(dh1, attn, ml, w):
    tm = TM

    def body(d_ref, a_ref, m_ref, w_ref, da_ref, dm_ref, dw_ref, acc):
        i = pl.program_id(0)

        @pl.when(i == 0)
        def _():
            acc[...] = jnp.zeros_like(acc)

        db = _bf(d_ref[...])
        dmix = _dot_nt(db, w_ref[...])
        da_ref[...] = dmix[:, 0:AW]
        dm_ref[...] = dmix[:, AW:D]
        acc[0:AW, :] = acc[0:AW, :] + _dot_tn(_bf(a_ref[...]), db)
        acc[AW:D, :] = acc[AW:D, :] + _dot_tn(_bf(m_ref[...]), db)

        @pl.when(i == S // tm - 1)
        def _():
            dw_ref[...] = _bf(acc[...])

    row = lambda wd: pl.BlockSpec((tm, wd), lambda i: (i, 0))
    return pl.pallas_call(
        body, name="out_proj_bwd", grid=(S // tm,),
        in_specs=[row(D), row(AW), row(MW), _cspec((D, D))],
        out_specs=[row(AW), row(MW), pl.BlockSpec((D, D), lambda i: (0, 0))],
        out_shape=[jax.ShapeDtypeStruct((S, AW), F32), jax.ShapeDtypeStruct((S, MW), F32),
                   jax.ShapeDtypeStruct((D, D), BF16)],
        scratch_shapes=[pltpu.VMEM((D, D), F32)],
        compiler_params=_params(1),
    )(dh1, attn, ml, w)


def _in_proj_bwd(dq, dk, dv, dmqk, dmv, dmo, dgt, dh1, x, g1, w, rc, ra, rb):
    tm = TM

    def body(dq_ref, dk_ref, dv_ref, dmqk_ref, dmv_ref, dmo_ref, dgt_ref, dh_ref, x_ref, g_ref, w_ref,
             rc_ref, ra_ref, rb_ref, dp_ref, dx_ref, dg_ref):
        @pl.when(pl.program_id(0) == 0)
        def _():
            dg_ref[...] = jnp.zeros_like(dg_ref)

        c, a, b = rc_ref[...], ra_ref[...], rb_ref[...]
        for half, ref in enumerate((dq_ref, dk_ref)):
            for t in range(4):
                lo = half * 512 + t * 128
                dp_ref[:, lo:lo + 128] = _bf(_rope_bwd(ref[:, t * 128:(t + 1) * 128], c, a, b))
        dp_ref[:, 1024:1536] = _bf(dv_ref[...])
        dp_ref[:, 1536:2560] = _bf(dmqk_ref[...])
        dp_ref[:, 2560:3072] = _bf(dmv_ref[...])
        dp_ref[:, 3072:3584] = _bf(dmo_ref[...])
        dp_ref[:, 3584:3712] = _bf(dgt_ref[...])
        dp_ref[:, 3712:PW] = jnp.zeros((tm, PW - 3712), BF16)
        du = jnp.zeros((tm, D), F32)
        for s in range(PW // 768):
            cols = slice(s * 768, (s + 1) * 768)
            du = du + _dot_nt(dp_ref[:, cols], w_ref[:, cols])
        n, rs = _rms(x_ref[...])
        dg_ref[...] = dg_ref[...] + jnp.sum(du * n, axis=0, keepdims=True)
        dx_ref[...] = dh_ref[...] + _rms_bwd(du, n, rs, g_ref[...])

    row = lambda wd: pl.BlockSpec((tm, wd), lambda i: (i, 0))
    return pl.pallas_call(
        body, name="in_proj_bwd", grid=(S // tm,),
        in_specs=[row(AW), row(AW), row(AW), row(1024), row(MW), row(MW), row(128), row(D), row(D), _cspec((1, D)),
                  _cspec((D, PW)), row(128), row(128), row(128)],
        out_specs=[row(PW), row(D), pl.BlockSpec((1, D), lambda i: (0, 0))],
        out_shape=[jax.ShapeDtypeStruct((S, PW), BF16), jax.ShapeDtypeStruct((S, D), F32),
                   jax.ShapeDtypeStruct((1, D), F32)],
        compiler_params=_params(1),
    )(dq, dk, dv, dmqk, dmv, dmo, dgt, dh1, x, g1, w, rc, ra, rb)


SMALL_ROWS = 96


def _small_phases(ins, out_ref, pack, rbuf, send_sems, recv_sems):
    x, y, c = _place()
    me = _dev_index(x, y, c)

    def copies():
        out = []
        for k, (dx, dy, dc) in enumerate(FLIPS):
            peer = ((x + dx) % 2, (y + dy) % 2, (c + dc) % 2)
            out.append(pltpu.make_async_remote_copy(
                src_ref=pack, dst_ref=rbuf.at[me], send_sem=send_sems.at[k], recv_sem=recv_sems.at[k],
                device_id=peer, device_id_type=MESH))
        return out

    def start():
        pack[...] = jnp.zeros_like(pack)
        for i, ref in enumerate(ins):
            pack[8 * i:8 * i + 1, 0:ref.shape[1]] = ref[...]
        rbuf[me] = pack[...]
        for cp in copies():
            cp.start()

    def finish():
        for cp in copies():
            cp.wait()
        tot = rbuf[0]
        for j in range(1, NDEV):
            tot = tot + rbuf[j]
        out_ref[...] = tot

    return start, finish


def _wgrad(name, A, B, a_fn, b_fn, tk, tn, out_shape, out_spec, ts=512, split=None, small=()):
    K, N = A.shape[1], B.shape[1]
    nrt = S // ts
    nc = next(c for c in (1024, 1280, tn) if tn % c == 0)
    ns = len(small)
    grid = (N // tn, K // tk, nrt)

    def body(*refs):
        a_ref, b_ref = refs[:2]
        o_ref = refs[2 + ns]
        acc = refs[3 + ns + bool(ns)]
        r = pl.program_id(2)
        if ns:
            step = (pl.program_id(0) * grid[1] + pl.program_id(1)) * nrt + r
            sm_start, sm_finish = _small_phases(refs[2:2 + ns], refs[3 + ns], *refs[4 + ns + 1:])
            pl.when(step == 0)(sm_start)

        @pl.when(r == 0)
        def _():
            acc[...] = jnp.zeros_like(acc)

        kc = min(tk, 1024)
        bs = [b_fn(b_ref[:, c * nc:(c + 1) * nc]) for c in range(tn // nc)]
        for kk in range(tk // kc):
            rows = slice(kk * kc, (kk + 1) * kc)
            at = a_fn(a_ref[:, rows]).T
            for c, b in enumerate(bs):
                cols = slice(c * nc, (c + 1) * nc)
                acc[rows, cols] = acc[rows, cols] + _dot(at, b)

        @pl.when(r == nrt - 1)
        def _():
            if split is None:
                o_ref[...] = _bf(acc[...])
            else:
                for j in range(NDEV):
                    o_ref[j] = _bf(acc[:, split * j:split * (j + 1)])

        if ns:
            pl.when(step == grid[0] * grid[1] * nrt - 1)(sm_finish)

    in_specs = [pl.BlockSpec((ts, tk), lambda n, k, r: (r, k)), pl.BlockSpec((ts, tn), lambda n, k, r: (r, n))]
    scratch = [pltpu.VMEM((tk, tn), F32)]
    if not ns:
        return pl.pallas_call(
            body, name=name, grid=grid, in_specs=in_specs, out_specs=out_spec,
            out_shape=jax.ShapeDtypeStruct(out_shape, BF16), scratch_shapes=scratch, compiler_params=_params(3),
        )(A, B)
    return pl.pallas_call(
        body, name=name, grid=grid, in_specs=in_specs + [VM] * ns, out_specs=[out_spec, VM],
        out_shape=[jax.ShapeDtypeStruct(out_shape, BF16), jax.ShapeDtypeStruct((SMALL_ROWS, 1024), F32)],
        scratch_shapes=scratch + [pltpu.VMEM((SMALL_ROWS, 1024), F32), pltpu.VMEM((NDEV, SMALL_ROWS, 1024), F32),
                                  pltpu.SemaphoreType.DMA((7,)), pltpu.SemaphoreType.DMA((7,))],
        compiler_params=_params(3),
    )(A, B, *small)


def _relu2_bf(a):
    r = jnp.maximum(a.astype(F32), 0.0)
    return _bf(r * r)


def _ident(a):
    return a


def _step(x, p, target, g1, conv_b, gate_b, gn, g_mlp, g_ple, g_fin, sh):
    g_in, g_conv = _gather_weights([sh["w_in"], sh["conv_w"]], [BF16, F32])
    conv_w = g_conv.transpose(1, 0, 2).reshape(4, 1024)
    rc, ra, rb = _rope_tables()
    w_in_p = _join_w_in(g_in)
    (qkv, mqk, mv, mo, gates, u1), (w_out8, w_pg8, w_ple8) = _in_proj(
        x, g1, w_in_p, rc, ra, rb, [sh["w_out"], sh["w_ple_gate"], sh["w_ple"]], [BF16] * 3)
    attn, lse, (w_up8, w_down_a) = _attn_fwd(qkv, [sh["w_up"], sh["w_down"][0:HALF]], [BF16] * 2)
    ml, cs, ns, ms, (w_down_b,) = _mlstm_fwd(mqk, mv, mo, gates, conv_w, conv_b, gate_b, gn,
                                             [sh["w_down"][HALF:2 * HALF]], [BF16])
    w_out, w_pg = w_out8.reshape(D, D), w_pg8.reshape(D, D)
    h1, u2 = _out_proj(x, attn, ml, w_out, g_mlp)
    a, h2 = _mlp_fwd(h1, u2, w_up8, w_down_a, w_down_b)
    dh2, dw_pg, dw_ple8, dg_ple, dg_fin, loss = _ple_loss(h2, p, target, w_pg, w_ple8, g_ple, g_fin)
    da, dh1, dg_mlp = _mlp_bwd(dh2, a, h1, g_mlp, w_up8, w_down_a, w_down_b)
    dw_up8 = _wgrad("wgrad_up", u2, da, _ident, _ident, D, DFF, (NDEV, D, DFF // NDEV),
                    pl.BlockSpec((NDEV, D, DFF // NDEV), lambda n, k, r: (0, 0, 0)), split=DFF // NDEV)
    dw_down = _wgrad("wgrad_down", a, dh2, _relu2_bf, _bf, DFF, D, (DFF, D),
                     pl.BlockSpec((DFF, D), lambda n, k, r: (0, 0)))
    d_attn, d_ml, dw_out = _out_proj_bwd(dh1, attn, ml, w_out)
    (dmqk, dmv, dmo, dgt, dconv_w, dconv_b, dgn, dgate_b), (r_out, r_pg, r_ple) = _mlstm_bwd(
        mqk, mv, mo, gates, conv_w, conv_b, gate_b, gn, cs, ns, ms, d_ml,
        [dw_out.reshape(NDEV, D // NDEV, D), dw_pg.reshape(NDEV, D // NDEV, D), dw_ple8])
    dq, dk, dv, (r_up, r_down) = _attn_bwd(qkv, attn, lse, d_attn, [dw_up8, dw_down.reshape(NDEV, DFF // NDEV, D)])
    dproj, dx, dg1 = _in_proj_bwd(dq, dk, dv, dmqk, dmv, dmo, dgt, dh1, x, g1, w_in_p, rc, ra, rb)
    small = dict(norm_mix_g=dg1, conv_b=dconv_b, gate_b=dgate_b, mlstm_norm_g=dgn, norm_mlp_g=dg_mlp,
                 norm_ple_g=dg_ple, final_norm_g=dg_fin)
    dw_in8, total = _wgrad("wgrad_in", u1, dproj, _ident, _ident, D, PW, (NDEV, D, IN_W // NDEV),
                           pl.BlockSpec((NDEV, D, IN_W // NDEV), lambda n, k, r: (0, 0, 0)), split=IN_W // NDEV,
                           small=[small[n] for n in SMALL] + [loss] + [dconv_w[j:j + 1] for j in range(4)])
    recv = dict(w_in=_scatter_two_level(dw_in8), w_out=r_out, w_up=r_up, w_down=r_down, w_ple_gate=r_pg, w_ple=r_ple)
    return dx, recv, total


def _gather_weights(shards, dtypes):
    nw = len(shards)

    def body(*refs):
        start, forward, finish = _gather_phases(refs[:nw], refs[nw:2 * nw], refs[2 * nw:3 * nw], *refs[3 * nw:])
        start()
        forward()
        finish()

    return pl.pallas_call(
        body, name="gather_weights",
        in_specs=[VM] * nw, out_specs=[ANY] * nw,
        out_shape=_gather_shapes(shards, dtypes),
        scratch_shapes=_gather_scratch(shards, dtypes),
        compiler_params=_params(),
    )(*shards)


CHIP_FLIPS = [(0, 0), (0, 1), (1, 0), (1, 1)]


def _scatter_two_level(part):
    shard = part.shape[1:]
    nc = len(CHIP_FLIPS)

    def body(in_ref, out_ref, mine_v, sib_v, psum_v, loc_sems, d2d_send, d2d_recv, ici_send, ici_recv, own_sem):
        x, y, c = _place()
        chips = [((x + dx) % 2, (y + dy) % 2) for dx, dy in CHIP_FLIPS]
        local, to_sib = [], []
        for k, (px, py) in enumerate(chips):
            local.append(pltpu.make_async_copy(in_ref.at[_dev_index(px, py, c)], mine_v.at[k], loc_sems.at[k]))
            to_sib.append(pltpu.make_async_remote_copy(
                src_ref=in_ref.at[_dev_index(px, py, 1 - c)], dst_ref=sib_v.at[k], send_sem=d2d_send.at[k],
                recv_sem=d2d_recv.at[k], device_id=(x, y, 1 - c), device_id_type=MESH))
        for cp in to_sib + local:
            cp.start()

        def over_ici(k):
            return pltpu.make_async_remote_copy(
                src_ref=psum_v.at[k], dst_ref=out_ref.at[k], send_sem=ici_send.at[k - 1], recv_sem=ici_recv.at[k - 1],
                device_id=(*chips[k], c), device_id_type=MESH)

        own = pltpu.make_async_copy(psum_v.at[0], out_ref.at[0], own_sem)
        for k in (1, 2, 3, 0):
            local[k].wait()
            to_sib[k].wait_recv()
            psum_v[k] = _bf(mine_v[k].astype(F32) + sib_v[k].astype(F32))
            if k:
                over_ici(k).start()
            else:
                own.start()
        for k in range(1, nc):
            over_ici(k).wait()
        for cp in to_sib:
            cp.wait_send()
        own.wait()

    return pl.pallas_call(
        body, name="scatter_grads",
        in_specs=[ANY], out_specs=ANY,
        out_shape=jax.ShapeDtypeStruct((nc, *shard), part.dtype),
        scratch_shapes=[pltpu.VMEM((nc, *shard), part.dtype)] * 3
        + [pltpu.SemaphoreType.DMA((nc,))] * 3 + [pltpu.SemaphoreType.DMA((nc - 1,))] * 2 + [pltpu.SemaphoreType.DMA],
        compiler_params=_params(),
    )(part)


def _adamw(name, gparts, w, m, v, tr):
    P, R, C = gparts.shape
    c1 = 1.0 - ADAM_B1 ** ADAM_STEP
    c2 = 1.0 - ADAM_B2 ** ADAM_STEP

    def body(g_ref, w_ref, m_ref, v_ref, go_ref, d_ref, mo_ref, vo_ref):
        g = g_ref[0].astype(F32)
        for j in range(1, P):
            g = g + g_ref[j].astype(F32)
        m2 = ADAM_B1 * m_ref[...] + (1.0 - ADAM_B1) * g
        v2 = ADAM_B2 * v_ref[...] + (1.0 - ADAM_B2) * (g * g)
        go_ref[...] = g
        mo_ref[...] = m2
        vo_ref[...] = v2
        d_ref[...] = -ADAM_LR * ((m2 / c1) / (jnp.sqrt(v2 / c2) + ADAM_EPS) + ADAM_WD * w_ref[...])

    row = pl.BlockSpec((tr, C), lambda i: (i, 0))
    return pl.pallas_call(
        body, name=name, grid=(R // tr,),
        in_specs=[pl.BlockSpec((P, tr, C), lambda i: (0, i, 0)), row, row, row],
        out_specs=[row] * 4,
        out_shape=[jax.ShapeDtypeStruct((R, C), F32)] * 4,
        compiler_params=_params(1),
    )(gparts, w, m, v)


SMALL = ("norm_mix_g", "conv_b", "gate_b", "mlstm_norm_g", "norm_mlp_g", "norm_ple_g", "final_norm_g")


def _pack_small(vals):
    return jnp.concatenate([jnp.pad(a, ((0, 7), (0, 1024 - a.shape[1]))) for a in vals], axis=0)


def kernel(x, p, norm_mix_g, w_in, conv_w, conv_b, gate_b, mlstm_norm_g, w_out, norm_mlp_g, w_up, w_down, norm_ple_g, w_ple_gate, w_ple, final_norm_g, loss_target, m_norm_mix_g, m_w_in, m_conv_w, m_conv_b, m_gate_b, m_mlstm_norm_g, m_w_out, m_norm_mlp_g, m_w_up, m_w_down, m_norm_ple_g, m_w_ple_gate, m_w_ple, m_final_norm_g, v_norm_mix_g, v_w_in, v_conv_w, v_conv_b, v_gate_b, v_mlstm_norm_g, v_w_out, v_norm_mlp_g, v_w_up, v_w_down, v_norm_ple_g, v_w_ple_gate, v_w_ple, v_final_norm_g):
    big_names = ("w_in", "conv_w", "w_out", "w_up", "w_down", "w_ple_gate", "w_ple")
    wts = dict(w_in=w_in, conv_w=conv_w, w_out=w_out, w_up=w_up, w_down=w_down, w_ple_gate=w_ple_gate, w_ple=w_ple)
    mom = dict(w_in=m_w_in, conv_w=m_conv_w, w_out=m_w_out, w_up=m_w_up, w_down=m_w_down, w_ple_gate=m_w_ple_gate,
               w_ple=m_w_ple)
    var = dict(w_in=v_w_in, conv_w=v_conv_w, w_out=v_w_out, w_up=v_w_up, w_down=v_w_down, w_ple_gate=v_w_ple_gate,
               w_ple=v_w_ple)
    sq = lambda a: a.reshape(a.shape[1:])
    fin = final_norm_g.reshape(1, D)
    dx, recv, total = _step(
        x[0], p[0, 0], loss_target[0], norm_mix_g, conv_b, jnp.pad(gate_b, ((0, 0), (0, 120))), mlstm_norm_g,
        norm_mlp_g, norm_ple_g, fin, {n: sq(wts[n]) for n in big_names})

    nrow = 8 * len(SMALL)
    me = _dev_index(*_place())
    conv_rows = total[nrow + 8:nrow + 40:8]
    recv["conv_w"] = lax.dynamic_slice_in_dim(conv_rows, me * 128, 128, axis=1).reshape(1, 4, 128)
    out = {}
    for n, tr in zip(big_names, (256, 4, 128, 256, 256, 128, 256)):
        res = _adamw("adamw_" + n, recv[n], sq(wts[n]), sq(mom[n]), sq(var[n]), tr)
        out[n] = [t.reshape(wts[n].shape) for t in res]
    sw = dict(norm_mix_g=norm_mix_g, conv_b=conv_b, gate_b=gate_b, mlstm_norm_g=mlstm_norm_g, norm_mlp_g=norm_mlp_g,
              norm_ple_g=norm_ple_g, final_norm_g=fin)
    sm = dict(norm_mix_g=m_norm_mix_g, conv_b=m_conv_b, gate_b=m_gate_b, mlstm_norm_g=m_mlstm_norm_g,
              norm_mlp_g=m_norm_mlp_g, norm_ple_g=m_norm_ple_g, final_norm_g=m_final_norm_g.reshape(1, D))
    sv = dict(norm_mix_g=v_norm_mix_g, conv_b=v_conv_b, gate_b=v_gate_b, mlstm_norm_g=v_mlstm_norm_g,
              norm_mlp_g=v_norm_mlp_g, norm_ple_g=v_norm_ple_g, final_norm_g=v_final_norm_g.reshape(1, D))
    res = _adamw("adamw_small", total[0:nrow].reshape(1, nrow, 1024), _pack_small([sw[n] for n in SMALL]),
                 _pack_small([sm[n] for n in SMALL]), _pack_small([sv[n] for n in SMALL]), nrow)
    for i, n in enumerate(SMALL):
        shp = final_norm_g.shape if n == "final_norm_g" else sw[n].shape
        out[n] = [t[8 * i, 0:sw[n].shape[1]].reshape(shp) for t in res]
    order = ("norm_mix_g", "w_in", "conv_w", "conv_b", "gate_b", "mlstm_norm_g", "w_out", "norm_mlp_g", "w_up", "w_down",
             "norm_ple_g", "w_ple_gate", "w_ple", "final_norm_g")
    loss_all = total[nrow, 0]
    return (loss_all, dx[None], *[out[n][0] for n in order], *[out[n][1] for n in order],
            *[out[n][2] for n in order], *[out[n][3] for n in order])
```

```python
import functools
import math

import jax
import jax.numpy as jnp
from jax import lax
from jax.experimental import pallas as pl
from jax.experimental.pallas import tpu as pltpu

F32, BF16 = jnp.float32, jnp.bfloat16
S = 4096
D = 1024
AW = 512
MW = 512
DFF = 4096
PLE = 256
IN_W = 3592
PW = 3840
NDEV = 8
EPS = 1e-6
NEG = -1e30
LC = 128
TB = 256
ROPE_THETA = 500000.0
VMEM_LIMIT = 56 * 1024 * 1024
HI = lax.Precision.HIGHEST

ADAM_LR, ADAM_B1, ADAM_B2, ADAM_EPS, ADAM_WD, ADAM_STEP = 0.001, 0.9, 0.999, 1e-08, 0.01, 10


def _params(n_grid=0, **kw):
    sem = dict(dimension_semantics=("arbitrary",) * n_grid) if n_grid else {}
    return pltpu.CompilerParams(vmem_limit_bytes=VMEM_LIMIT, **sem, **kw)


def _cspec(shape):
    nd = len(shape)
    return pl.BlockSpec(shape, lambda *_: (0,) * nd, pipeline_mode=pl.Buffered(1))


def _dot(a, b):
    return jnp.dot(a, b, preferred_element_type=F32)


def _dot_nt(a, b):
    return lax.dot_general(a, b, (((1,), (1,)), ((), ())), preferred_element_type=F32)


def _dot_tn(a, b):
    return lax.dot_general(a, b, (((0,), (0,)), ((), ())), preferred_element_type=F32)


def _bf(x):
    return x.astype(BF16)


def _rms(x):
    rs = lax.rsqrt(jnp.mean(x * x, axis=-1, keepdims=True) + EPS)
    return x * rs, rs


def _rms_bwd(du, n, rs, g):
    dn = du * g
    return rs * (dn - n * jnp.mean(dn * n, axis=-1, keepdims=True))


def _sigmoid(x):
    return 1.0 / (1.0 + jnp.exp(-x))


ROPE_BLK = 512


def _rope_parts():
    def cs(n, step):
        j = lax.broadcasted_iota(jnp.int32, (n, 128), 1) % 64
        pos = (lax.broadcasted_iota(jnp.int32, (n, 128), 0) * step).astype(F32)
        ang = pos * jnp.power(ROPE_THETA, -(j % 8).astype(F32) / 8.0)
        return jnp.cos(ang), jnp.sin(ang)

    return (*cs(ROPE_BLK, 1), *cs(S // ROPE_BLK, ROPE_BLK))


def _rope_fill(co_ref, so_ref, cb_ref, sb_ref, rc_ref, ra_ref, rb_ref):
    j = lax.broadcasted_iota(jnp.int32, (ROPE_BLK, 128), 1) % 64
    co, so = co_ref[...], so_ref[...]
    for t in range(S // ROPE_BLK):
        cb, sb = cb_ref[t:t + 1, :], sb_ref[t:t + 1, :]
        cos, sin = cb * co - sb * so, sb * co + cb * so
        rows = slice(t * ROPE_BLK, (t + 1) * ROPE_BLK)
        rc_ref[rows, :] = jnp.where(j < 16, cos, 1.0)
        ra_ref[rows, :] = jnp.where(j < 8, -sin, 0.0)
        rb_ref[rows, :] = jnp.where((j >= 8) & (j < 16), sin, 0.0)


def _rope(blk, c, a, b):
    return blk * c + pltpu.roll(blk, 120, 1) * a + pltpu.roll(blk, 8, 1) * b


def _rope_bwd(d, c, a, b):
    return d * c + pltpu.roll(d * a, 8, 1) + pltpu.roll(d * b, 120, 1)


MESH = pl.DeviceIdType.MESH
ANY = pl.BlockSpec(memory_space=pl.ANY)
VM = pl.BlockSpec(memory_space=pltpu.VMEM)
FLIPS = [(dx, dy, dc) for dx in (0, 1) for dy in (0, 1) for dc in (0, 1)][1:]


def _place():
    return lax.axis_index("x"), lax.axis_index("y"), lax.axis_index("c")


def _dev_index(px, py, pc):
    return 4 * px + 2 * py + pc


def _gather_phases(ins, outs, bufs, send_sems=None, recv_sems=None, local_sems=None):
    nw = len(ins)
    if nw == 0:
        return (lambda: None,) * 3
    x, y, c = _place()
    me, sib = (x, y, c), (x, y, 1 - c)
    chips = [(1 - x, y), (x, 1 - y), (1 - x, 1 - y)]

    def copy(w, k, block, to, from_buf=False):
        dst = outs[w].at[_dev_index(*block)]
        return pltpu.make_async_remote_copy(
            src_ref=bufs[w] if from_buf else dst, dst_ref=dst, send_sem=send_sems.at[w, k],
            recv_sem=recv_sems.at[w, k], device_id=to, device_id_type=MESH)

    def mine(w):
        return pltpu.make_async_copy(bufs[w], outs[w].at[_dev_index(*me)], local_sems.at[w])

    def first(w):
        return [copy(w, 0, me, sib, True)] + [copy(w, 1 + j, me, (*chip, c), True) for j, chip in enumerate(chips)]

    def passed(w):
        return [copy(w, 4 + j, (*chip, c), sib) for j, chip in enumerate(chips)]

    def start():
        for w in range(nw):
            bufs[w][...] = ins[w][...].astype(bufs[w].dtype)
        for w in range(nw):
            mine(w).start()
            for cp in first(w):
                cp.start()

    def forward():
        for j, chip in enumerate(chips):
            for w in range(nw):
                copy(w, 1 + j, (*chip, c), me).wait_recv()
                passed(w)[j].start()

    def finish():
        for w in range(nw):
            copy(w, 0, sib, me).wait_recv()
        for j, chip in enumerate(chips):
            for w in range(nw):
                copy(w, 4 + j, (*chip, 1 - c), me).wait_recv()
        for w in range(nw):
            for cp in first(w) + passed(w):
                cp.wait_send()
            mine(w).wait()

    return start, forward, finish


def _gather_scratch(shards, dtypes):
    nw = len(shards)
    if nw == 0:
        return []
    return ([pltpu.VMEM(s.shape, dt) for s, dt in zip(shards, dtypes)]
            + [pltpu.SemaphoreType.DMA((nw, 7)), pltpu.SemaphoreType.DMA((nw, 7)), pltpu.SemaphoreType.DMA((nw,))])


def _gather_shapes(shards, dtypes):
    return [jax.ShapeDtypeStruct((NDEV, *s.shape), dt) for s, dt in zip(shards, dtypes)]


def _scatter_phases(ins, outs, send_sems=None, recv_sems=None, local_sems=None):
    nw = len(ins)
    if nw == 0:
        return (lambda: None,) * 2
    x, y, c = _place()
    me = _dev_index(x, y, c)

    def copies():
        out = []
        for w in range(nw):
            out.append(pltpu.make_async_copy(ins[w].at[me], outs[w].at[me], local_sems.at[w]))
            for k, (dx, dy, dc) in enumerate(FLIPS):
                peer = ((x + dx) % 2, (y + dy) % 2, (c + dc) % 2)
                out.append(pltpu.make_async_remote_copy(
                    src_ref=ins[w].at[_dev_index(*peer)], dst_ref=outs[w].at[me], send_sem=send_sems.at[w, k],
                    recv_sem=recv_sems.at[w, k], device_id=peer, device_id_type=MESH))
        return out

    def start():
        for cp in copies():
            cp.start()

    def finish():
        for cp in copies():
            cp.wait()

    return start, finish


def _scatter_scratch(nw):
    if nw == 0:
        return []
    return [pltpu.SemaphoreType.DMA((nw, 7)), pltpu.SemaphoreType.DMA((nw, 7)), pltpu.SemaphoreType.DMA((nw,))]


TM = 512


def _join_w_in(wg):
    sw = IN_W // NDEV

    def body(wg_ref, w_ref):
        for j in range(NDEV):
            w_ref[:, sw * j:sw * (j + 1)] = wg_ref[j]
        w_ref[:, IN_W:PW] = jnp.zeros((D, PW - IN_W), BF16)

    return pl.pallas_call(body, name="join_w_in", out_shape=jax.ShapeDtypeStruct((D, PW), BF16),
                          compiler_params=_params())(wg)


def _in_proj(x, g1, w, rc, ra, rb, shards, dtypes):
    tm = TM
    nw = len(shards)
    nt = S // tm

    def body(*refs):
        x_ref, g_ref, w_ref, rc_ref, ra_ref, rb_ref = refs[:6]
        ins = refs[6:6 + nw]
        qkv_ref, mqk_ref, mv_ref, mo_ref, gt_ref, u_ref = refs[6 + nw:12 + nw]
        outs = refs[12 + nw:12 + 2 * nw]
        bufs = refs[12 + 2 * nw:12 + 3 * nw]
        ag_start, ag_forward, ag_finish = _gather_phases(ins, outs, bufs, *refs[12 + 3 * nw:])
        i = pl.program_id(0)
        pl.when(i == 0)(ag_start)
        pl.when(i == nt - 2)(ag_forward)
        n, _ = _rms(x_ref[...])
        u = _bf(n * g_ref[...])
        u_ref[...] = u
        c, a, b = rc_ref[...], ra_ref[...], rb_ref[...]
        for half in range(2):
            blk = _dot(u, w_ref[:, half * 512:(half + 1) * 512])
            for t in range(4):
                lo = half * 512 + t * 128
                qkv_ref[:, lo:lo + 128] = _rope(blk[:, t * 128:(t + 1) * 128], c, a, b)
        qkv_ref[:, 1024:1536] = _dot(u, w_ref[:, 1024:1536])
        mqk_ref[:, 0:512] = _dot(u, w_ref[:, 1536:2048])
        mqk_ref[:, 512:1024] = _dot(u, w_ref[:, 2048:2560])
        mv_ref[...] = _dot(u, w_ref[:, 2560:3072])
        mo_ref[...] = _dot(u, w_ref[:, 3072:3584])
        gt_ref[...] = _dot(u, w_ref[:, 3584:3712])
        pl.when(i == nt - 1)(ag_finish)

    row = lambda wd: pl.BlockSpec((tm, wd), lambda i: (i, 0))
    res = pl.pallas_call(
        body, name="in_proj", grid=(nt,),
        in_specs=[row(D), _cspec((1, D)), _cspec((D, PW)), row(128), row(128), row(128)] + [VM] * nw,
        out_specs=[row(1536), row(1024), row(512), row(512), row(128), row(D)] + [ANY] * nw,
        out_shape=[jax.ShapeDtypeStruct((S, 1536), F32), jax.ShapeDtypeStruct((S, 1024), F32),
                   jax.ShapeDtypeStruct((S, 512), F32), jax.ShapeDtypeStruct((S, 512), F32),
                   jax.ShapeDtypeStruct((S, 128), F32), jax.ShapeDtypeStruct((S, D), BF16)]
        + _gather_shapes(shards, dtypes),
        scratch_shapes=_gather_scratch(shards, dtypes),
        compiler_params=_params(1),
    )(x, g1, w, rc, ra, rb, *shards)
    return res[:6], res[6:]


DILATIONS = (16, 4, 1)


def _attn_valid(n):
    kd = lax.broadcasted_iota(jnp.int32, (128, 256), 1) - lax.broadcasted_iota(jnp.int32, (128, 256), 0)
    off = jnp.where(n == 0, 0, 128)
    return (kd <= off) & (kd >= off - 128)


def _attn_rows(d, r, n):
    if d == 1:
        q0 = pl.multiple_of(n * 128, 128)
        k0 = pl.multiple_of(jnp.maximum(n - 1, 0) * 128, 128)
        return pl.ds(q0, 128), pl.ds(k0, 256), _attn_valid(n)
    q0 = r + n * 128 * d
    k0 = r + jnp.maximum(n - 1, 0) * 128 * d
    return pl.ds(q0, 128, stride=d), pl.ds(k0, 256, stride=d), _attn_valid(n)


ATTN_GROUP = 4
ATTN_ITERS = S // 128 // ATTN_GROUP


def _attn_group(d, i):
    nb = S // (128 * d)
    if nb == 2:
        qi = lax.broadcasted_iota(jnp.int32, (256, 256), 0) - lax.broadcasted_iota(jnp.int32, (256, 256), 1)
        whole = [pl.ds((ATTN_GROUP // 2) * i + u, 256, stride=d) for u in range(ATTN_GROUP // 2)]
        return [(rows, rows, (qi >= 0) & (qi <= 128)) for rows in whole]
    if d == 1:
        return [_attn_rows(1, 0, i + ATTN_ITERS * u) for u in range(ATTN_GROUP)]
    return [_attn_rows(d, (i // nb) * ATTN_GROUP + u, i % nb) for u in range(ATTN_GROUP)]


def _head0(shape):
    return lax.broadcasted_iota(jnp.int32, shape, 1) < 64


def _stack_heads(t):
    h0 = _head0(t.shape)
    tb = _bf(t)
    zero = jnp.zeros_like(tb)
    return jnp.concatenate([jnp.where(h0, tb, zero), jnp.where(h0, zero, tb)], axis=0)


def _attn_fwd(qkv, shards, dtypes):
    nw = len(shards)

    def body(*refs):
        q_ref, k_ref, v_ref = refs[:3]
        ins = refs[3:3 + nw]
        o_ref, lse_ref = refs[3 + nw:5 + nw]
        outs = refs[5 + nw:5 + 2 * nw]
        m0, m1, l0, l1, acc = refs[5 + 2 * nw:10 + 2 * nw]
        bufs = refs[10 + 2 * nw:10 + 3 * nw]
        ag_start, ag_forward, ag_finish = _gather_phases(ins, outs, bufs, *refs[10 + 3 * nw:])
        hp = pl.program_id(0)
        pl.when(hp == 0)(ag_start)
        pl.when(hp == 3)(ag_forward)
        stats = (m0, m1, l0, l1, acc)

        def update(blocks, first):
            loaded = [([q_ref[rq, :], k_ref[rk, :], v_ref[rk, :]], None if first else [ref[rq, :] for ref in stats])
                      for rq, rk, _ in blocks]
            results = []
            for ((q, k, v), prev), (_, _, valid) in zip(loaded, blocks):
                head0 = _head0(q.shape)
                kb, vb = _bf(k), _bf(v)
                q = q * 0.125
                m_new, l_new, acc_new = [], [], []
                for a, qa in enumerate((_bf(jnp.where(head0, q, 0.0)), _bf(jnp.where(head0, 0.0, q)))):
                    s = jnp.where(valid, _dot_nt(qa, kb), NEG)
                    mc = jnp.max(s, axis=-1, keepdims=True)
                    m_a = jnp.broadcast_to(mc, q.shape) if first else jnp.maximum(prev[a], mc)
                    p = jnp.exp(s - jnp.tile(m_a, (1, 2)))
                    l_add = jnp.sum(p, axis=-1, keepdims=True)
                    pv = _dot(_bf(p), vb)
                    if first:
                        l_a = jnp.broadcast_to(l_add, q.shape)
                    else:
                        alpha = jnp.exp(prev[a] - m_a)
                        l_a, pv = alpha * prev[2 + a] + l_add, alpha * prev[4] + pv
                    m_new.append(m_a), l_new.append(l_a), acc_new.append(pv)
                results.append((m_new[0], m_new[1], l_new[0], l_new[1], jnp.where(head0, acc_new[0], acc_new[1])))
            for (rq, _, _), res in zip(blocks, results):
                for ref, val in zip(stats, res):
                    ref[rq, :] = val

        for d in DILATIONS:
            def step(i, carry, d=d):
                update(_attn_group(d, i), d == DILATIONS[0])
                return carry

            lax.fori_loop(0, ATTN_ITERS, step, 0)

        def fin(t, carry):
            rows = pl.ds(pl.multiple_of(t * 256, 256), 256)
            h0 = lax.broadcasted_iota(jnp.int32, (256, 128), 1) < 64
            l = jnp.where(h0, l0[rows, :], l1[rows, :])
            o_ref[rows, :] = acc[rows, :] / l
            lse_ref[rows, :] = jnp.where(h0, m0[rows, :], m1[rows, :]) + jnp.log(l)
            return carry

        lax.fori_loop(0, S // 256, fin, 0)
        pl.when(hp == 3)(ag_finish)

    col = lambda off: pl.BlockSpec((S, 128), lambda h, off=off: (0, off + h))
    res = pl.pallas_call(
        body, name="attn_fwd", grid=(4,),
        in_specs=[col(0), col(4), col(8)] + [VM] * nw,
        out_specs=[col(0), col(0)] + [ANY] * nw,
        out_shape=[jax.ShapeDtypeStruct((S, AW), F32), jax.ShapeDtypeStruct((S, AW), F32)]
        + _gather_shapes(shards, dtypes),
        scratch_shapes=[pltpu.VMEM((S, 128), F32)] * 5 + _gather_scratch(shards, dtypes),
        compiler_params=_params(1),
    )(qkv, qkv, qkv, *shards)
    return res[0], res[1], res[2:]


def _attn_bwd(qkv, o, lse, do, parts):
    nw = len(parts)

    def body(*refs):
        q_ref, k_ref, v_ref, o_ref, lse_ref, do_ref = refs[:6]
        ins = refs[6:6 + nw]
        dq_ref, dk_ref, dv_ref = refs[6 + nw:9 + nw]
        outs = refs[9 + nw:9 + 2 * nw]
        L0, L1, D0, D1 = refs[9 + 2 * nw:13 + 2 * nw]
        rs_start, rs_finish = _scatter_phases(ins, outs, *refs[13 + 2 * nw:])
        hp = pl.program_id(0)
        pl.when(hp == 0)(rs_start)
        def pre(t, carry):
            rows = pl.ds(pl.multiple_of(t * 256, 256), 256)
            h0 = lax.broadcasted_iota(jnp.int32, (256, 128), 1) < 64
            ls = lse_ref[rows, :]
            dd = do_ref[rows, :] * o_ref[rows, :]
            shp = (256, 128)
            L0[rows, :] = jnp.broadcast_to(jnp.max(jnp.where(h0, ls, NEG), axis=-1, keepdims=True), shp)
            L1[rows, :] = jnp.broadcast_to(jnp.max(jnp.where(h0, NEG, ls), axis=-1, keepdims=True), shp)
            D0[rows, :] = jnp.broadcast_to(jnp.sum(jnp.where(h0, dd, 0.0), axis=-1, keepdims=True), shp)
            D1[rows, :] = jnp.broadcast_to(jnp.sum(jnp.where(h0, 0.0, dd), axis=-1, keepdims=True), shp)
            return carry

        lax.fori_loop(0, S // 256, pre, 0)

        def update(blocks, first):
            loaded = [([q_ref[rq, :], k_ref[rk, :], v_ref[rk, :], do_ref[rq, :]],
                       [L0[rq, :], L1[rq, :], D0[rq, :], D1[rq, :]],
                       [0.0] * 3 if first else [dq_ref[rq, :], dk_ref[rk, :], dv_ref[rk, :]]) for rq, rk, _ in blocks]
            results = []
            for ((q, k, v, dout), (l0v, l1v, d0v, d1v), (dq, dk, dv)), (_, _, valid) in zip(loaded, blocks):
                valid = jnp.tile(valid, (1, 2))
                kst, vst = _stack_heads(k), _stack_heads(v)
                hk = _head0((256, 128))
                dob = _bf(dout)
                cat = lambda a, b: jnp.concatenate([jnp.tile(a, (1, 2)), jnp.tile(b, (1, 2))], axis=1)
                s = jnp.where(valid, _dot_nt(_bf(q * 0.125), kst), NEG)
                p = jnp.exp(s - cat(l0v, l1v))
                ds = _bf(p * (_dot_nt(dob, vst) - cat(d0v, d1v)) * 0.125)
                dk2 = _dot_tn(ds, _bf(q))
                dv2 = _dot_tn(_bf(p), dob)
                results.append((dq + _dot(ds, kst), dk + jnp.where(hk, dk2[0:256], dk2[256:512]),
                                dv + jnp.where(hk, dv2[0:256], dv2[256:512])))
            for (rq, rk, _), (dq, dk, dv) in zip(blocks, results):
                dq_ref[rq, :] = dq
                dk_ref[rk, :] = dk
                dv_ref[rk, :] = dv

        assert S // (128 * DILATIONS[0]) == 2
        for d in DILATIONS:
            def step(i, carry, d=d):
                update(_attn_group(d, i), d == DILATIONS[0])
                return carry

            lax.fori_loop(0, ATTN_ITERS, step, 0)
        pl.when(hp == 3)(rs_finish)

    col = lambda off: pl.BlockSpec((S, 128), lambda h, off=off: (0, off + h))
    res = pl.pallas_call(
        body, name="attn_bwd", grid=(4,),
        in_specs=[col(0), col(4), col(8), col(0), col(0), col(0)] + [ANY] * nw,
        out_specs=[col(0), col(0), col(0)] + [ANY] * nw,
        out_shape=[jax.ShapeDtypeStruct((S, AW), F32)] * 3 + [jax.ShapeDtypeStruct(a.shape, a.dtype) for a in parts],
        scratch_shapes=[pltpu.VMEM((S, 128), F32)] * 4 + _scatter_scratch(nw),
        compiler_params=_params(1),
    )(qkv, qkv, qkv, o, lse, do, *parts)
    return res[0], res[1], res[2], res[3:]


def _logsig(x):
    return jnp.minimum(x, 0.0) - jnp.log1p(jnp.exp(-jnp.abs(x)))


def _conv_taps(xp, n):
    return [xp[8:] if j == 3 else pltpu.roll(xp, 3 - j, 0)[8:] for j in range(4)]


def _conv_silu(xp, w_ref, b_ref, n):
    taps = _conv_taps(xp, n)
    c = b_ref[...] + sum(w_ref[j:j + 1, :] * taps[j] for j in range(4))
    sg = _sigmoid(c)
    return c, sg, taps


def _chunk_gates(G):
    assert LC == 128
    r = lax.broadcasted_iota(jnp.int32, (LC, LC), 0)
    c = lax.broadcasted_iota(jnp.int32, (LC, LC), 1)
    tril = (c <= r).astype(F32)
    triu = (c >= r).astype(F32)
    b_col = jnp.dot(tril, _logsig(G), preferred_element_type=F32, precision=HI)
    return b_col, b_col.T, G.T, tril, triu


def _colpick(X, lane):
    li = lax.broadcasted_iota(jnp.int32, X.shape, 1)
    return jnp.sum(jnp.where(li == lane, X, 0.0), axis=1, keepdims=True)


def _rowpick(XT, row):
    ri = lax.broadcasted_iota(jnp.int32, XT.shape, 0)
    return jnp.sum(jnp.where(ri == row, XT, 0.0), axis=0, keepdims=True)


def _mlstm_head(qh, kh, vh, G, b_col, b_row, g_row, h, Ch, nh, m_prev):
    bt = _colpick(b_col, 4 + h)
    i_col = _colpick(G, h)
    bs = _rowpick(b_row, 4 + h)
    i_row = _rowpick(g_row, h)
    r = lax.broadcasted_iota(jnp.int32, (LC, LC), 0)
    c = lax.broadcasted_iota(jnp.int32, (LC, LC), 1)
    log_d = jnp.where(c <= r, bt - bs + i_row, NEG)
    log_inter = bt + m_prev
    m_t = jnp.maximum(log_inter, jnp.max(log_d, axis=1, keepdims=True))
    Dm = jnp.exp(log_d - m_t)
    g = jnp.exp(log_inter - m_t)
    qb, kb, vb = _bf(qh), _bf(kh), _bf(vh)
    Am = _dot_nt(qb, kb) * Dm
    qC = _dot(qb, _bf(Ch))
    num = g * qC + _dot(_bf(Am), vb)
    qn = jnp.sum(qh * nh, axis=1, keepdims=True)
    den = g * qn + jnp.sum(Am, axis=1, keepdims=True)
    floor = jnp.exp(-m_t)
    dd = jnp.maximum(jnp.abs(den), floor)
    inv_dd = 1.0 / dd
    hh = num * inv_dd
    lane = lax.broadcasted_iota(jnp.int32, (1, LC), 1)
    blast = jnp.sum(jnp.where(lane == LC - 1, bs, 0.0), axis=1, keepdims=True)
    log_s = blast - bt + i_col
    m_new = jnp.maximum(blast + m_prev, jnp.max(log_s, axis=0, keepdims=True))
    decay = jnp.exp(blast + m_prev - m_new)
    ws = jnp.exp(log_s - m_new)
    kw = kh * ws
    C_new = decay * Ch + _dot_tn(_bf(kw), vb)
    n_new = decay * nh + jnp.sum(kw, axis=0, keepdims=True)
    return dict(Dm=Dm, g=g, Am=Am, qC=qC, qn=qn, den=den, floor=floor, inv_dd=inv_dd, h=hh, decay=decay, ws=ws, kw=kw,
                C_new=C_new, n_new=n_new, m_new=m_new, qb=qb, kb=kb, vb=vb)


def _head_out(hh, mo_h, gn_h):
    r = lax.rsqrt(jnp.mean(hh * hh, axis=-1, keepdims=True) + EPS)
    hn = hh * r
    sg = _sigmoid(mo_h)
    return sg * (hn * gn_h), hn, r, sg


def _mlstm_fwd(mqk, mv, mo, gates, conv_w, conv_b, gate_b, gn, shards, dtypes):
    nblk = S // TB
    ncb = TB // LC
    nw = len(shards)

    def body(*refs):
        x_ref, v_ref, o_ref, g_ref, w_ref, b_ref, gb_ref, gn_ref = refs[:8]
        ins = refs[8:8 + nw]
        out_ref, cs_ref, ns_ref, ms_ref = refs[8 + nw:12 + nw]
        outs = refs[12 + nw:12 + 2 * nw]
        tail, Cst, nst, mst, qs, ks = refs[12 + 2 * nw:18 + 2 * nw]
        bufs = refs[18 + 2 * nw:18 + 3 * nw]
        ag_start, ag_forward, ag_finish = _gather_phases(ins, outs, bufs, *refs[18 + 3 * nw:])
        i = pl.program_id(0)
        pl.when(i == 0)(ag_start)
        pl.when(i == nblk // 2)(ag_forward)

        @pl.when(i == 0)
        def _():
            tail[...] = jnp.zeros_like(tail)
            Cst[...] = jnp.zeros_like(Cst)
            nst[...] = jnp.zeros_like(nst)
            mst[...] = jnp.zeros_like(mst)

        x = x_ref[...]
        xp = jnp.concatenate([tail[...], x], axis=0)
        tail[...] = x[TB - 8:TB, :]
        c, sg, _ = _conv_silu(xp, w_ref, b_ref, TB)
        y = c * sg
        qs[...] = y[:, 0:MW]
        ks[...] = y[:, MW:2 * MW] * (1.0 / math.sqrt(128.0))

        for cc in range(ncb):
            rows = slice(cc * LC, (cc + 1) * LC)
            G = g_ref[rows, :] + gb_ref[...]
            b_col, b_row, g_row, _, _ = _chunk_gates(G)
            cs_ref[cc] = Cst[...]
            ns_ref[cc] = nst[...]
            ms_ref[cc] = mst[...]
            for h in range(4):
                ln = slice(h * 128, (h + 1) * 128)
                m_prev = jnp.max(mst[0:1, ln], axis=1, keepdims=True)
                f = _mlstm_head(qs[rows, ln], ks[rows, ln], v_ref[rows, ln], G, b_col, b_row, g_row, h,
                                Cst[:, ln], nst[0:1, ln], m_prev)
                out, _, _, _ = _head_out(f["h"], o_ref[rows, ln], gn_ref[:, ln])
                out_ref[rows, ln] = out
                Cst[:, ln] = f["C_new"]
                nst[0:1, ln] = f["n_new"]
                mst[0:1, ln] = jnp.broadcast_to(f["m_new"], (1, 128))
        pl.when(i == nblk - 1)(ag_finish)

    row = lambda wd: pl.BlockSpec((TB, wd), lambda i: (i, 0))
    res = pl.pallas_call(
        body, name="mlstm_fwd", grid=(nblk,),
        in_specs=[row(1024), row(MW), row(MW), row(128), _cspec((4, 1024)), _cspec((1, 1024)), _cspec((1, 128)),
                  _cspec((1, MW))] + [VM] * nw,
        out_specs=[row(MW), pl.BlockSpec((ncb, 128, MW), lambda i: (i, 0, 0)),
                   pl.BlockSpec((ncb, 8, MW), lambda i: (i, 0, 0)), pl.BlockSpec((ncb, 8, MW), lambda i: (i, 0, 0))]
        + [ANY] * nw,
        out_shape=[jax.ShapeDtypeStruct((S, MW), F32), jax.ShapeDtypeStruct((S // LC, 128, MW), F32),
                   jax.ShapeDtypeStruct((S // LC, 8, MW), F32), jax.ShapeDtypeStruct((S // LC, 8, MW), F32)]
        + _gather_shapes(shards, dtypes),
        scratch_shapes=[pltpu.VMEM((8, 1024), F32), pltpu.VMEM((128, MW), F32), pltpu.VMEM((8, MW), F32),
                        pltpu.VMEM((8, MW), F32), pltpu.VMEM((TB, MW), F32), pltpu.VMEM((TB, MW), F32)]
        + _gather_scratch(shards, dtypes),
        compiler_params=_params(1),
    )(mqk, mv, mo, gates, conv_w, conv_b, gate_b, gn, *shards)
    return res[0], res[1], res[2], res[3], res[4:]


def _mlstm_bwd(mqk, mv, mo, gates, conv_w, conv_b, gate_b, gn, cs, ns, ms, dout, parts):
    nblk = S // TB
    ncb = TB // LC
    kscale = 1.0 / math.sqrt(128.0)
    nw = len(parts)

    def body(*refs):
        x_ref, xprev_ref, v_ref, o_ref, g_ref, w_ref, b_ref, gb_ref, gn_ref, cs_ref, ns_ref, ms_ref, do_ref = refs[:13]
        ins = refs[13:13 + nw]
        dx_ref, dv_ref, dmo_ref, dg_ref, dw_ref, db_ref, dgn_ref, dgb_ref = refs[13 + nw:21 + nw]
        outs = refs[21 + nw:21 + 2 * nw]
        dCst, dnst, dyhead, qs, ks, dqk = refs[21 + 2 * nw:27 + 2 * nw]
        rs_start, rs_finish = _scatter_phases(ins, outs, *refs[27 + 2 * nw:])
        i = pl.program_id(0)
        blk = nblk - 1 - i
        pl.when(i == 0)(rs_start)

        @pl.when(i == 0)
        def _():
            dCst[...] = jnp.zeros_like(dCst)
            dnst[...] = jnp.zeros_like(dnst)
            dyhead[...] = jnp.zeros_like(dyhead)
            dw_ref[...] = jnp.zeros_like(dw_ref)
            db_ref[...] = jnp.zeros_like(db_ref)
            dgn_ref[...] = jnp.zeros_like(dgn_ref)
            dgb_ref[...] = jnp.zeros_like(dgb_ref)

        x = x_ref[...]
        xprev = jnp.where(blk == 0, 0.0, xprev_ref[...])
        xp = jnp.concatenate([xprev, x], axis=0)
        c, sg, taps = _conv_silu(xp, w_ref, b_ref, TB)
        y = c * sg
        qs[...] = y[:, 0:MW]
        ks[...] = y[:, MW:2 * MW] * kscale
        lane128 = lax.broadcasted_iota(jnp.int32, (LC, 128), 1)
        rowi = lax.broadcasted_iota(jnp.int32, (LC, 1), 0)
        ones = jnp.ones((LC, 128), F32)

        for cc in reversed(range(ncb)):
            rows = slice(cc * LC, (cc + 1) * LC)
            G = g_ref[rows, :] + gb_ref[...]
            b_col, b_row, g_row, _, triu = _chunk_gates(G)
            dB = jnp.zeros((LC, 128), F32)
            dI = jnp.zeros((LC, 128), F32)
            for h in range(4):
                ln = slice(h * 128, (h + 1) * 128)
                Ch = cs_ref[cc, :, ln]
                nh = ns_ref[cc, 0:1, ln]
                m_prev = jnp.max(ms_ref[cc, 0:1, ln], axis=1, keepdims=True)
                qh, kh, vh = qs[rows, ln], ks[rows, ln], v_ref[rows, ln]
                f = _mlstm_head(qh, kh, vh, G, b_col, b_row, g_row, h, Ch, nh, m_prev)
                hh, inv_dd, den, g, Am, Dm = f["h"], f["inv_dd"], f["den"], f["g"], f["Am"], f["Dm"]
                qb, kb, vb = f["qb"], f["kb"], f["vb"]
                gn_h = gn_ref[:, ln]
                _, hn, r, sgo = _head_out(hh, o_ref[rows, ln], gn_h)
                do = do_ref[rows, ln]
                hm = hn * gn_h
                dmo_ref[rows, ln] = do * hm * sgo * (1.0 - sgo)
                dhm = do * sgo
                dgn_ref[:, ln] = dgn_ref[:, ln] + jnp.sum(dhm * hn, axis=0, keepdims=True)
                dhn = dhm * gn_h
                dh = r * (dhn - hn * jnp.mean(dhn * hn, axis=-1, keepdims=True))
                dnum = dh * inv_dd
                ddd = -jnp.sum(dh * hh, axis=1, keepdims=True) * inv_dd
                dden = jnp.where(jnp.abs(den) >= f["floor"], ddd * jnp.sign(den), 0.0)
                dnb = _bf(dnum)
                dA = _dot_nt(dnb, vb) + dden
                dv = _dot_tn(_bf(Am), dnb)
                gd = _bf(g * dnum)
                gq = g * dden
                dq = _dot_nt(gd, _bf(Ch)) + gq * nh
                dCn = dCst[:, ln]
                dnn = dnst[0:1, ln]
                dC = f["decay"] * dCn + _dot_tn(qb, gd)
                dn = f["decay"] * dnn + jnp.sum(gq * qh, axis=0, keepdims=True)
                dg = jnp.sum(dnum * f["qC"], axis=1, keepdims=True) + dden * f["qn"]
                dS = _bf(dA * Dm)
                dq = dq + _dot(dS, kb)
                dk = _dot_tn(dS, qb)
                Gm = dA * Am
                gam = dg * g
                dCb = _bf(dCn)
                E = _dot_nt(vb, dCb) + dnn
                ws = f["ws"]
                dk = dk + ws * E
                om = jnp.sum(E * kh, axis=1, keepdims=True) * ws
                dv = dv + _dot(_bf(f["kw"]), dCb)
                ddecay = (jnp.sum(jnp.sum(dCn * Ch, axis=1, keepdims=True), axis=0, keepdims=True)
                          + jnp.sum(dnn * nh, axis=1, keepdims=True))
                delta = ddecay * f["decay"]
                rows_g = jnp.sum(Gm, axis=1, keepdims=True)
                cols_g = jnp.broadcast_to(jnp.sum(Gm, axis=0, keepdims=True), (LC, 128)).T
                last = jnp.where(rowi == LC - 1, jnp.sum(om, axis=0, keepdims=True) + delta, 0.0)
                db = rows_g + gam - om + last - cols_g
                di = cols_g + om
                dB = jnp.where(lane128 == 4 + h, db, dB)
                dI = jnp.where(lane128 == h, di, dI)
                dCst[:, ln] = dC
                dnst[0:1, ln] = dn
                dqk[rows, ln] = dq
                dqk[rows, MW + h * 128:MW + (h + 1) * 128] = dk * kscale
                dv_ref[rows, ln] = dv
            dlogf = jnp.dot(triu, dB, preferred_element_type=F32, precision=HI)
            dG = dI + dlogf * _sigmoid(-G)
            dG = jnp.where(lane128 < 8, dG, 0.0)
            dg_ref[rows, :] = dG
            dgb_ref[...] = dgb_ref[...] + jnp.sum(dG, axis=0, keepdims=True)

        dy = dqk[...] * (sg * (1.0 + c * (1.0 - sg)))
        db_ref[...] = db_ref[...] + jnp.sum(dy, axis=0, keepdims=True)
        for j in range(4):
            dw_ref[j:j + 1, :] = dw_ref[j:j + 1, :] + jnp.sum(dy * taps[j], axis=0, keepdims=True)
        dyp = jnp.concatenate([dy, dyhead[...]], axis=0)
        dx = w_ref[3:4, :] * dy
        for j in range(3):
            dx = dx + w_ref[j:j + 1, :] * pltpu.roll(dyp, TB + 8 - (3 - j), 0)[0:TB]
        dx_ref[...] = dx
        dyhead[...] = dy[0:8, :]
        pl.when(i == nblk - 1)(rs_finish)

    rrow = lambda wd: pl.BlockSpec((TB, wd), lambda i: (nblk - 1 - i, 0))
    st = lambda r: pl.BlockSpec((ncb, r, MW), lambda i: (nblk - 1 - i, 0, 0))
    prev8 = pl.BlockSpec((8, 1024), lambda i: (jnp.maximum((nblk - 1 - i) * (TB // 8) - 1, 0), 0))
    res = pl.pallas_call(
        body, name="mlstm_bwd", grid=(nblk,),
        in_specs=[rrow(1024), prev8, rrow(MW), rrow(MW), rrow(128), _cspec((4, 1024)), _cspec((1, 1024)),
                  _cspec((1, 128)), _cspec((1, MW)), st(128), st(8), st(8), rrow(MW)] + [ANY] * nw,
        out_specs=[rrow(1024), rrow(MW), rrow(MW), rrow(128),
                   pl.BlockSpec((4, 1024), lambda i: (0, 0)), pl.BlockSpec((1, 1024), lambda i: (0, 0)),
                   pl.BlockSpec((1, MW), lambda i: (0, 0)), pl.BlockSpec((1, 128), lambda i: (0, 0))] + [ANY] * nw,
        out_shape=[jax.ShapeDtypeStruct((S, 1024), F32), jax.ShapeDtypeStruct((S, MW), F32),
                   jax.ShapeDtypeStruct((S, MW), F32), jax.ShapeDtypeStruct((S, 128), F32),
                   jax.ShapeDtypeStruct((4, 1024), F32), jax.ShapeDtypeStruct((1, 1024), F32),
                   jax.ShapeDtypeStruct((1, MW), F32), jax.ShapeDtypeStruct((1, 128), F32)]
        + [jax.ShapeDtypeStruct(a.shape, a.dtype) for a in parts],
        scratch_shapes=[pltpu.VMEM((128, MW), F32), pltpu.VMEM((8, MW), F32), pltpu.VMEM((8, 1024), F32),
                        pltpu.VMEM((TB, MW), F32), pltpu.VMEM((TB, MW), F32), pltpu.VMEM((TB, 1024), F32)]
        + _scatter_scratch(nw),
        compiler_params=_params(1),
    )(mqk, mqk, mv, mo, gates, conv_w, conv_b, gate_b, gn, cs, ns, ms, dout, *parts)
    return res[:8], res[8:]


def _out_proj(x, attn, ml, w, g):
    tm = TM

    def body(x_ref, a_ref, m_ref, w_ref, g_ref, h_ref, u_ref):
        h1 = x_ref[...] + _dot(_bf(a_ref[...]), w_ref[0:AW, :]) + _dot(_bf(m_ref[...]), w_ref[AW:D, :])
        h_ref[...] = h1
        n, _ = _rms(h1)
        u_ref[...] = _bf(n * g_ref[...])

    row = lambda wd: pl.BlockSpec((tm, wd), lambda i: (i, 0))
    return pl.pallas_call(
        body, name="out_proj", grid=(S // tm,),
        in_specs=[row(D), row(AW), row(MW), _cspec((D, D)), _cspec((1, D))],
        out_specs=[row(D), row(D)],
        out_shape=[jax.ShapeDtypeStruct((S, D), F32), jax.ShapeDtypeStruct((S, D), BF16)],
        compiler_params=_params(1),
    )(x, attn, ml, w, g)


HALF = DFF // NDEV // 2


def _mlp_fwd(h1, u2, w_up, w_down_a, w_down_b):
    tm = TM

    def body(h_ref, u_ref, wu_ref, wa_ref, wb_ref, a_ref, o_ref):
        u = u_ref[...]
        acc = h_ref[...]
        for c in range(NDEV):
            cols = slice(c * 512, (c + 1) * 512)
            a = _dot(u, wu_ref[c])
            a_ref[:, cols] = _bf(a)
            r = jnp.maximum(a, 0.0)
            r = _bf(r * r)
            acc = acc + _dot(r[:, 0:HALF], wa_ref[c]) + _dot(r[:, HALF:2 * HALF], wb_ref[c])
        o_ref[...] = acc

    row = lambda wd: pl.BlockSpec((tm, wd), lambda i: (i, 0))
    return pl.pallas_call(
        body, name="mlp_fwd", grid=(S // tm,),
        in_specs=[row(D), row(D), _cspec((NDEV, D, DFF // NDEV)), _cspec((NDEV, HALF, D)), _cspec((NDEV, HALF, D))],
        out_specs=[row(DFF), row(D)],
        out_shape=[jax.ShapeDtypeStruct((S, DFF), BF16), jax.ShapeDtypeStruct((S, D), F32)],
        compiler_params=_params(1),
    )(h1, u2, w_up, w_down_a, w_down_b)


def _ple_loss(h2, p, target, w_pg, w_ple, g_ple, g_fin):
    tm = TM

    def body(h_ref, p_ref, t_ref, wg_ref, wp_ref, gp_ref, gf_ref,
             dh_ref, dwg_ref, dwp_ref, dgp_ref, dgf_ref, loss_ref, acc_g, acc_p):
        i = pl.program_id(0)

        @pl.when(i == 0)
        def _():
            acc_g[...] = jnp.zeros_like(acc_g)
            acc_p[...] = jnp.zeros_like(acc_p)
            dgp_ref[...] = jnp.zeros_like(dgp_ref)
            dgf_ref[...] = jnp.zeros_like(dgf_ref)
            loss_ref[...] = jnp.zeros_like(loss_ref)

        h2v = h_ref[...]
        n2, rs2 = _rms(h2v)
        u3 = _bf(n2 * gp_ref[...])
        gt = _sigmoid(_dot(u3, wg_ref[...]))
        pb = _bf(p_ref[...])
        e = jnp.concatenate([_dot(pb, wp_ref[j]) for j in range(NDEV)], axis=1)
        h3 = h2v + gt * e
        n3, rs3 = _rms(h3)
        err = n3 * gf_ref[...] - t_ref[...]
        loss_ref[...] = loss_ref[...] + 0.5 / D * jnp.sum(jnp.sum(err * err, axis=1, keepdims=True), axis=0, keepdims=True)
        dy = err * (1.0 / D)
        dgf_ref[...] = dgf_ref[...] + jnp.sum(dy * n3, axis=0, keepdims=True)
        dh3 = _rms_bwd(dy, n3, rs3, gf_ref[...])
        de = _bf(dh3 * gt)
        dz = _bf(dh3 * e * gt * (1.0 - gt))
        acc_p[...] = acc_p[...] + _dot_tn(pb, de)
        acc_g[...] = acc_g[...] + _dot_tn(u3, dz)
        du3 = _dot_nt(dz, wg_ref[...])
        dgp_ref[...] = dgp_ref[...] + jnp.sum(du3 * n2, axis=0, keepdims=True)
        dh_ref[...] = dh3 + _rms_bwd(du3, n2, rs2, gp_ref[...])

        @pl.when(i == S // tm - 1)
        def _():
            dwg_ref[...] = _bf(acc_g[...])
            for j in range(NDEV):
                dwp_ref[j] = _bf(acc_p[:, j * 128:(j + 1) * 128])

    row = lambda wd: pl.BlockSpec((tm, wd), lambda i: (i, 0))
    whole = lambda shp: pl.BlockSpec(shp, lambda i: (0,) * len(shp))
    return pl.pallas_call(
        body, name="ple_loss", grid=(S // tm,),
        in_specs=[row(D), row(PLE), row(D), _cspec((D, D)), _cspec((NDEV, PLE, 128)), _cspec((1, D)), _cspec((1, D))],
        out_specs=[row(D), whole((D, D)), whole((NDEV, PLE, 128)), whole((1, D)), whole((1, D)), whole((1, 1))],
        out_shape=[jax.ShapeDtypeStruct((S, D), F32), jax.ShapeDtypeStruct((D, D), BF16),
                   jax.ShapeDtypeStruct((NDEV, PLE, 128), BF16), jax.ShapeDtypeStruct((1, D), F32),
                   jax.ShapeDtypeStruct((1, D), F32), jax.ShapeDtypeStruct((1, 1), F32)],
        scratch_shapes=[pltpu.VMEM((D, D), F32), pltpu.VMEM((PLE, D), F32)],
        compiler_params=_params(1),
    )(h2, p, target, w_pg, w_ple, g_ple, g_fin)


def _mlp_bwd(dh2, a, h1, g, w_up, w_down_a, w_down_b):
    tm = TM

    def body(d_ref, a_ref, h_ref, g_ref, wu_ref, wa_ref, wb_ref, da_ref, dh1_ref, dg_ref):
        @pl.when(pl.program_id(0) == 0)
        def _():
            dg_ref[...] = jnp.zeros_like(dg_ref)

        dh2v = d_ref[...]
        db = _bf(dh2v)
        du = jnp.zeros((tm, D), F32)
        for c in range(NDEV):
            cols = slice(c * 512, (c + 1) * 512)
            dr = jnp.concatenate([_dot_nt(db, wa_ref[c]), _dot_nt(db, wb_ref[c])], axis=1)
            da = _bf(dr * (2.0 * jnp.maximum(a_ref[:, cols], 0.0)))
            da_ref[:, cols] = da
            du = du + _dot_nt(da, wu_ref[c])
        n, rs = _rms(h_ref[...])
        dg_ref[...] = dg_ref[...] + jnp.sum(du * n, axis=0, keepdims=True)
        dh1_ref[...] = dh2v + _rms_bwd(du, n, rs, g_ref[...])

    row = lambda wd: pl.BlockSpec((tm, wd), lambda i: (i, 0))
    return pl.pallas_call(
        body, name="mlp_bwd", grid=(S // tm,),
        in_specs=[row(D), row(DFF), row(D), _cspec((1, D)), _cspec((NDEV, D, DFF // NDEV)), _cspec((NDEV, HALF, D)),
                  _cspec((NDEV, HALF, D))],
        out_specs=[row(DFF), row(D), pl.BlockSpec((1, D), lambda i: (0, 0))],
        out_shape=[jax.ShapeDtypeStruct((S, DFF), BF16), jax.ShapeDtypeStruct((S, D), F32),
                   jax.ShapeDtypeStruct((1, D), F32)],
        compiler_params=_params(1),
    )(dh2, a, h1, g, w_up, w_down_a, w_down_b)


def _out_proj_bwd(dh1, attn, ml, w):
    tm = TM

    def body(d_ref, a_ref, m_ref, w_ref, da_ref, dm_ref, dw_ref, acc):
        i = pl.program_id(0)

        @pl.when(i == 0)
        def _():
            acc[...] = jnp.zeros_like(acc)

        db = _bf(d_ref[...])
        dmix = _dot_nt(db, w_ref[...])
        da_ref[...] = dmix[:, 0:AW]
        dm_ref[...] = dmix[:, AW:D]
        acc[0:AW, :] = acc[0:AW, :] + _dot_tn(_bf(a_ref[...]), db)
        acc[AW:D, :] = acc[AW:D, :] + _dot_tn(_bf(m_ref[...]), db)

        @pl.when(i == S // tm - 1)
        def _():
            dw_ref[...] = _bf(acc[...])

    row = lambda wd: pl.BlockSpec((tm, wd), lambda i: (i, 0))
    return pl.pallas_call(
        body, name="out_proj_bwd", grid=(S // tm,),
        in_specs=[row(D), row(AW), row(MW), _cspec((D, D))],
        out_specs=[row(AW), row(MW), pl.BlockSpec((D, D), lambda i: (0, 0))],
        out_shape=[jax.ShapeDtypeStruct((S, AW), F32), jax.ShapeDtypeStruct((S, MW), F32),
                   jax.ShapeDtypeStruct((D, D), BF16)],
        scratch_shapes=[pltpu.VMEM((D, D), F32)],
        compiler_params=_params(1),
    )(dh1, attn, ml, w)


def _in_proj_bwd(dq, dk, dv, dmqk, dmv, dmo, dgt, dh1, x, g1, w, rc, ra, rb):
    tm = TM

    def body(dq_ref, dk_ref, dv_ref, dmqk_ref, dmv_ref, dmo_ref, dgt_ref, dh_ref, x_ref, g_ref, w_ref,
             rc_ref, ra_ref, rb_ref, dp_ref, dx_ref, dg_ref):
        @pl.when(pl.program_id(0) == 0)
        def _():
            dg_ref[...] = jnp.zeros_like(dg_ref)

        c, a, b = rc_ref[...], ra_ref[...], rb_ref[...]
        for half, ref in enumerate((dq_ref, dk_ref)):
            for t in range(4):
                lo = half * 512 + t * 128
                dp_ref[:, lo:lo + 128] = _bf(_rope_bwd(ref[:, t * 128:(t + 1) * 128], c, a, b))
        dp_ref[:, 1024:1536] = _bf(dv_ref[...])
        dp_ref[:, 1536:2560] = _bf(dmqk_ref[...])
        dp_ref[:, 2560:3072] = _bf(dmv_ref[...])
        dp_ref[:, 3072:3584] = _bf(dmo_ref[...])
        dp_ref[:, 3584:3712] = _bf(dgt_ref[...])
        dp_ref[:, 3712:PW] = jnp.zeros((tm, PW - 3712), BF16)
        du = jnp.zeros((tm, D), F32)
        for s in range(PW // 768):
            cols = slice(s * 768, (s + 1) * 768)
            du = du + _dot_nt(dp_ref[:, cols], w_ref[:, cols])
        n, rs = _rms(x_ref[...])
        dg_ref[...] = dg_ref[...] + jnp.sum(du * n, axis=0, keepdims=True)
        dx_ref[...] = dh_ref[...] + _rms_bwd(du, n, rs, g_ref[...])

    row = lambda wd: pl.BlockSpec((tm, wd), lambda i: (i, 0))
    return pl.pallas_call(
        body, name="in_proj_bwd", grid=(S // tm,),
        in_specs=[row(AW), row(AW), row(AW), row(1024), row(MW), row(MW), row(128), row(D), row(D), _cspec((1, D)),
                  _cspec((D, PW)), row(128), row(128), row(128)],
        out_specs=[row(PW), row(D), pl.BlockSpec((1, D), lambda i: (0, 0))],
        out_shape=[jax.ShapeDtypeStruct((S, PW), BF16), jax.ShapeDtypeStruct((S, D), F32),
                   jax.ShapeDtypeStruct((1, D), F32)],
        compiler_params=_params(1),
    )(dq, dk, dv, dmqk, dmv, dmo, dgt, dh1, x, g1, w, rc, ra, rb)


SMALL_ROWS = 96


def _small_phases(ins, out_ref, pack, rbuf, send_sems, recv_sems):
    x, y, c = _place()
    me = _dev_index(x, y, c)

    def copies():
        out = []
        for k, (dx, dy, dc) in enumerate(FLIPS):
            peer = ((x + dx) % 2, (y + dy) % 2, (c + dc) % 2)
            out.append(pltpu.make_async_remote_copy(
                src_ref=pack, dst_ref=rbuf.at[me], send_sem=send_sems.at[k], recv_sem=recv_sems.at[k],
                device_id=peer, device_id_type=MESH))
        return out

    def start():
        pack[...] = jnp.zeros_like(pack)
        for i, ref in enumerate(ins):
            pack[8 * i:8 * i + 1, 0:ref.shape[1]] = ref[...]
        rbuf[me] = pack[...]
        for cp in copies():
            cp.start()

    def finish():
        for cp in copies():
            cp.wait()
        tot = rbuf[0]
        for j in range(1, NDEV):
            tot = tot + rbuf[j]
        out_ref[...] = tot

    return start, finish


def _wgrad(name, A, B, a_fn, b_fn, tk, tn, out_shape, out_spec, ts=512, split=None, small=()):
    K, N = A.shape[1], B.shape[1]
    nrt = S // ts
    nc = next(c for c in (1024, 1280, tn) if tn % c == 0)
    ns = len(small)
    grid = (N // tn, K // tk, nrt)

    def body(*refs):
        a_ref, b_ref = refs[:2]
        o_ref = refs[2 + ns]
        acc = refs[3 + ns + bool(ns)]
        r = pl.program_id(2)
        if ns:
            step = (pl.program_id(0) * grid[1] + pl.program_id(1)) * nrt + r
            sm_start, sm_finish = _small_phases(refs[2:2 + ns], refs[3 + ns], *refs[4 + ns + 1:])
            pl.when(step == 0)(sm_start)

        @pl.when(r == 0)
        def _():
            acc[...] = jnp.zeros_like(acc)

        kc = min(tk, 1024)
        bs = [b_fn(b_ref[:, c * nc:(c + 1) * nc]) for c in range(tn // nc)]
        for kk in range(tk // kc):
            rows = slice(kk * kc, (kk + 1) * kc)
            at = a_fn(a_ref[:, rows]).T
            for c, b in enumerate(bs):
                cols = slice(c * nc, (c + 1) * nc)
                acc[rows, cols] = acc[rows, cols] + _dot(at, b)

        @pl.when(r == nrt - 1)
        def _():
            if split is None:
                o_ref[...] = _bf(acc[...])
            else:
                for j in range(NDEV):
                    o_ref[j] = _bf(acc[:, split * j:split * (j + 1)])

        if ns:
            pl.when(step == grid[0] * grid[1] * nrt - 1)(sm_finish)

    in_specs = [pl.BlockSpec((ts, tk), lambda n, k, r: (r, k)), pl.BlockSpec((ts, tn), lambda n, k, r: (r, n))]
    scratch = [pltpu.VMEM((tk, tn), F32)]
    if not ns:
        return pl.pallas_call(
            body, name=name, grid=grid, in_specs=in_specs, out_specs=out_spec,
            out_shape=jax.ShapeDtypeStruct(out_shape, BF16), scratch_shapes=scratch, compiler_params=_params(3),
        )(A, B)
    return pl.pallas_call(
        body, name=name, grid=grid, in_specs=in_specs + [VM] * ns, out_specs=[out_spec, VM],
        out_shape=[jax.ShapeDtypeStruct(out_shape, BF16), jax.ShapeDtypeStruct((SMALL_ROWS, 1024), F32)],
        scratch_shapes=scratch + [pltpu.VMEM((SMALL_ROWS, 1024), F32), pltpu.VMEM((NDEV, SMALL_ROWS, 1024), F32),
                                  pltpu.SemaphoreType.DMA((7,)), pltpu.SemaphoreType.DMA((7,))],
        compiler_params=_params(3),
    )(A, B, *small)


def _relu2_bf(a):
    r = jnp.maximum(a.astype(F32), 0.0)
    return _bf(r * r)


def _ident(a):
    return a


def _step(x, p, target, g1, conv_b, gate_b, gn, g_mlp, g_ple, g_fin, sh):
    (g_in, g_conv), (rc, ra, rb) = _gather_weights([sh["w_in"], sh["conv_w"]], [BF16, F32])
    conv_w = g_conv.transpose(1, 0, 2).reshape(4, 1024)
    w_in_p = _join_w_in(g_in)
    (qkv, mqk, mv, mo, gates, u1), (w_out8, w_pg8, w_ple8) = _in_proj(
        x, g1, w_in_p, rc, ra, rb, [sh["w_out"], sh["w_ple_gate"], sh["w_ple"]], [BF16] * 3)
    attn, lse, (w_up8, w_down_a) = _attn_fwd(qkv, [sh["w_up"], sh["w_down"][0:HALF]], [BF16] * 2)
    ml, cs, ns, ms, (w_down_b,) = _mlstm_fwd(mqk, mv, mo, gates, conv_w, conv_b, gate_b, gn,
                                             [sh["w_down"][HALF:2 * HALF]], [BF16])
    w_out, w_pg = w_out8.reshape(D, D), w_pg8.reshape(D, D)
    h1, u2 = _out_proj(x, attn, ml, w_out, g_mlp)
    a, h2 = _mlp_fwd(h1, u2, w_up8, w_down_a, w_down_b)
    dh2, dw_pg, dw_ple8, dg_ple, dg_fin, loss = _ple_loss(h2, p, target, w_pg, w_ple8, g_ple, g_fin)
    da, dh1, dg_mlp = _mlp_bwd(dh2, a, h1, g_mlp, w_up8, w_down_a, w_down_b)
    dw_up8 = _wgrad("wgrad_up", u2, da, _ident, _ident, D, DFF, (NDEV, D, DFF // NDEV),
                    pl.BlockSpec((NDEV, D, DFF // NDEV), lambda n, k, r: (0, 0, 0)), split=DFF // NDEV)
    dw_down = _wgrad("wgrad_down", a, dh2, _relu2_bf, _bf, DFF, D, (DFF, D),
                     pl.BlockSpec((DFF, D), lambda n, k, r: (0, 0)))
    d_attn, d_ml, dw_out = _out_proj_bwd(dh1, attn, ml, w_out)
    (dmqk, dmv, dmo, dgt, dconv_w, dconv_b, dgn, dgate_b), (r_out, r_pg, r_ple) = _mlstm_bwd(
        mqk, mv, mo, gates, conv_w, conv_b, gate_b, gn, cs, ns, ms, d_ml,
        [dw_out.reshape(NDEV, D // NDEV, D), dw_pg.reshape(NDEV, D // NDEV, D), dw_ple8])
    dq, dk, dv, (r_up, r_down) = _attn_bwd(qkv, attn, lse, d_attn, [dw_up8, dw_down.reshape(NDEV, DFF // NDEV, D)])
    dproj, dx, dg1 = _in_proj_bwd(dq, dk, dv, dmqk, dmv, dmo, dgt, dh1, x, g1, w_in_p, rc, ra, rb)
    small = dict(norm_mix_g=dg1, conv_b=dconv_b, gate_b=dgate_b, mlstm_norm_g=dgn, norm_mlp_g=dg_mlp,
                 norm_ple_g=dg_ple, final_norm_g=dg_fin)
    dw_in8, total = _wgrad("wgrad_in", u1, dproj, _ident, _ident, D, PW, (NDEV, D, IN_W // NDEV),
                           pl.BlockSpec((NDEV, D, IN_W // NDEV), lambda n, k, r: (0, 0, 0)), split=IN_W // NDEV,
                           small=[small[n] for n in SMALL] + [loss] + [dconv_w[j:j + 1] for j in range(4)])
    recv = dict(w_in=_scatter_two_level(dw_in8), w_out=r_out, w_up=r_up, w_down=r_down, w_ple_gate=r_pg, w_ple=r_ple)
    return dx, recv, total


def _gather_weights(shards, dtypes):
    nw = len(shards)

    def body(*refs):
        ins, parts = refs[:nw], refs[nw:nw + 4]
        outs, tables = refs[nw + 4:2 * nw + 4], refs[2 * nw + 4:2 * nw + 7]
        start, forward, finish = _gather_phases(ins, outs, refs[2 * nw + 7:3 * nw + 7], *refs[3 * nw + 7:])
        start()
        _rope_fill(*parts, *tables)
        forward()
        finish()

    res = pl.pallas_call(
        body, name="gather_weights",
        in_specs=[VM] * (nw + 4), out_specs=[ANY] * nw + [VM] * 3,
        out_shape=_gather_shapes(shards, dtypes) + [jax.ShapeDtypeStruct((S, 128), F32)] * 3,
        scratch_shapes=_gather_scratch(shards, dtypes),
        compiler_params=_params(),
    )(*shards, *_rope_parts())
    return res[:nw], res[nw:]


CHIP_FLIPS = [(0, 0), (0, 1), (1, 0), (1, 1)]


def _scatter_two_level(part):
    shard = part.shape[1:]
    nc = len(CHIP_FLIPS)

    def body(in_ref, out_ref, mine_v, sib_v, psum_v, loc_sems, d2d_send, d2d_recv, ici_send, ici_recv, own_sem):
        x, y, c = _place()
        chips = [((x + dx) % 2, (y + dy) % 2) for dx, dy in CHIP_FLIPS]
        local, to_sib = [], []
        for k, (px, py) in enumerate(chips):
            local.append(pltpu.make_async_copy(in_ref.at[_dev_index(px, py, c)], mine_v.at[k], loc_sems.at[k]))
            to_sib.append(pltpu.make_async_remote_copy(
                src_ref=in_ref.at[_dev_index(px, py, 1 - c)], dst_ref=sib_v.at[k], send_sem=d2d_send.at[k],
                recv_sem=d2d_recv.at[k], device_id=(x, y, 1 - c), device_id_type=MESH))
        for cp in to_sib + local:
            cp.start()

        def over_ici(k):
            return pltpu.make_async_remote_copy(
                src_ref=psum_v.at[k], dst_ref=out_ref.at[k], send_sem=ici_send.at[k - 1], recv_sem=ici_recv.at[k - 1],
                device_id=(*chips[k], c), device_id_type=MESH)

        own = pltpu.make_async_copy(psum_v.at[0], out_ref.at[0], own_sem)
        for k in (1, 2, 3, 0):
            local[k].wait()
            to_sib[k].wait_recv()
            psum_v[k] = _bf(mine_v[k].astype(F32) + sib_v[k].astype(F32))
            if k:
                over_ici(k).start()
            else:
                own.start()
        for k in range(1, nc):
            over_ici(k).wait()
        for cp in to_sib:
            cp.wait_send()
        own.wait()

    return pl.pallas_call(
        body, name="scatter_grads",
        in_specs=[ANY], out_specs=ANY,
        out_shape=jax.ShapeDtypeStruct((nc, *shard), part.dtype),
        scratch_shapes=[pltpu.VMEM((nc, *shard), part.dtype)] * 3
        + [pltpu.SemaphoreType.DMA((nc,))] * 3 + [pltpu.SemaphoreType.DMA((nc - 1,))] * 2 + [pltpu.SemaphoreType.DMA],
        compiler_params=_params(),
    )(part)


def _adamw(name, gparts, w, m, v, tr):
    P, R, C = gparts.shape

    def body(g_ref, w_ref, m_ref, v_ref, go_ref, d_ref, mo_ref, vo_ref):
        g = g_ref[0].astype(F32)
        for j in range(1, P):
            g = g + g_ref[j].astype(F32)
        go_ref[...] = g
        d_ref[...], mo_ref[...], vo_ref[...] = _adam_update(g, w_ref[...], m_ref[...], v_ref[...])

    row = pl.BlockSpec((tr, C), lambda i: (i, 0))
    return pl.pallas_call(
        body, name=name, grid=(R // tr,),
        in_specs=[pl.BlockSpec((P, tr, C), lambda i: (0, i, 0)), row, row, row],
        out_specs=[row] * 4,
        out_shape=[jax.ShapeDtypeStruct((R, C), F32)] * 4,
        compiler_params=_params(1),
    )(gparts, w, m, v)


SMALL = ("norm_mix_g", "conv_b", "gate_b", "mlstm_norm_g", "norm_mlp_g", "norm_ple_g", "final_norm_g")


def _adam_update(g, w, m, v):
    c1 = 1.0 - ADAM_B1 ** ADAM_STEP
    c2 = 1.0 - ADAM_B2 ** ADAM_STEP
    m2 = ADAM_B1 * m + (1.0 - ADAM_B1) * g
    v2 = ADAM_B2 * v + (1.0 - ADAM_B2) * (g * g)
    return -ADAM_LR * ((m2 / c1) / (jnp.sqrt(v2 / c2) + ADAM_EPS) + ADAM_WD * w), m2, v2


def _adamw_small(total, ws, ms, vs):
    n = len(ws)

    def body(*refs):
        t_ref = refs[0]
        outs = refs[1 + 3 * n:]
        for i in range(n):
            w_ref, m_ref, v_ref = refs[1 + i], refs[1 + n + i], refs[1 + 2 * n + i]
            g = t_ref[8 * i:8 * i + 1, 0:w_ref.shape[1]]
            delta, m2, v2 = _adam_update(g, w_ref[...], m_ref[...], v_ref[...])
            for ref, val in zip(outs[4 * i:4 * i + 4], (g, delta, m2, v2)):
                ref[...] = val

    res = pl.pallas_call(
        body, name="adamw_small",
        out_shape=[jax.ShapeDtypeStruct(w.shape, F32) for w in ws for _ in range(4)],
        compiler_params=_params(),
    )(total, *ws, *ms, *vs)
    return [res[4 * i:4 * i + 4] for i in range(n)]


def kernel(x, p, norm_mix_g, w_in, conv_w, conv_b, gate_b, mlstm_norm_g, w_out, norm_mlp_g, w_up, w_down, norm_ple_g, w_ple_gate, w_ple, final_norm_g, loss_target, m_norm_mix_g, m_w_in, m_conv_w, m_conv_b, m_gate_b, m_mlstm_norm_g, m_w_out, m_norm_mlp_g, m_w_up, m_w_down, m_norm_ple_g, m_w_ple_gate, m_w_ple, m_final_norm_g, v_norm_mix_g, v_w_in, v_conv_w, v_conv_b, v_gate_b, v_mlstm_norm_g, v_w_out, v_norm_mlp_g, v_w_up, v_w_down, v_norm_ple_g, v_w_ple_gate, v_w_ple, v_final_norm_g):
    big_names = ("w_in", "conv_w", "w_out", "w_up", "w_down", "w_ple_gate", "w_ple")
    wts = dict(w_in=w_in, conv_w=conv_w, w_out=w_out, w_up=w_up, w_down=w_down, w_ple_gate=w_ple_gate, w_ple=w_ple)
    mom = dict(w_in=m_w_in, conv_w=m_conv_w, w_out=m_w_out, w_up=m_w_up, w_down=m_w_down, w_ple_gate=m_w_ple_gate,
               w_ple=m_w_ple)
    var = dict(w_in=v_w_in, conv_w=v_conv_w, w_out=v_w_out, w_up=v_w_up, w_down=v_w_down, w_ple_gate=v_w_ple_gate,
               w_ple=v_w_ple)
    sq = lambda a: a.reshape(a.shape[1:])
    fin = final_norm_g.reshape(1, D)
    dx, recv, total = _step(
        x[0], p[0, 0], loss_target[0], norm_mix_g, conv_b, jnp.pad(gate_b, ((0, 0), (0, 120))), mlstm_norm_g,
        norm_mlp_g, norm_ple_g, fin, {n: sq(wts[n]) for n in big_names})

    nrow = 8 * len(SMALL)
    me = _dev_index(*_place())
    conv_rows = total[nrow + 8:nrow + 40:8]
    recv["conv_w"] = lax.dynamic_slice_in_dim(conv_rows, me * 128, 128, axis=1).reshape(1, 4, 128)
    out = {}
    for n, tr in zip(big_names, (256, 4, 128, 256, 256, 128, 256)):
        res = _adamw("adamw_" + n, recv[n], sq(wts[n]), sq(mom[n]), sq(var[n]), tr)
        out[n] = [t.reshape(wts[n].shape) for t in res]
    sw = dict(norm_mix_g=norm_mix_g, conv_b=conv_b, gate_b=gate_b, mlstm_norm_g=mlstm_norm_g, norm_mlp_g=norm_mlp_g,
              norm_ple_g=norm_ple_g, final_norm_g=fin)
    sm = dict(norm_mix_g=m_norm_mix_g, conv_b=m_conv_b, gate_b=m_gate_b, mlstm_norm_g=m_mlstm_norm_g,
              norm_mlp_g=m_norm_mlp_g, norm_ple_g=m_norm_ple_g, final_norm_g=m_final_norm_g.reshape(1, D))
    sv = dict(norm_mix_g=v_norm_mix_g, conv_b=v_conv_b, gate_b=v_gate_b, mlstm_norm_g=v_mlstm_norm_g,
              norm_mlp_g=v_norm_mlp_g, norm_ple_g=v_norm_ple_g, final_norm_g=v_final_norm_g.reshape(1, D))
    res = _adamw_small(total, [sw[n] for n in SMALL], [sm[n] for n in SMALL], [sv[n] for n in SMALL])
    for n, r in zip(SMALL, res):
        out[n] = [t.reshape(final_norm_g.shape) for t in r] if n == "final_norm_g" else list(r)
    order = ("norm_mix_g", "w_in", "conv_w", "conv_b", "gate_b", "mlstm_norm_g", "w_out", "norm_mlp_g", "w_up", "w_down",
             "norm_ple_g", "w_ple_gate", "w_ple", "final_norm_g")
    loss_all = total[nrow, 0]
    return (loss_all, dx[None], *[out[n][0] for n in order], *[out[n][1] for n in order],
            *[out[n][2] for n in order], *[out[n][3] for n in order])
```

```python
import functools
import math

import jax
import jax.numpy as jnp
from jax import lax
from jax.experimental import pallas as pl
from jax.experimental.pallas import tpu as pltpu

F32, BF16 = jnp.float32, jnp.bfloat16
S = 4096
D = 1024
AW = 512
MW = 512
DFF = 4096
PLE = 256
IN_W = 3592
PW = 3840
NDEV = 8
EPS = 1e-6
NEG = -1e30
LC = 128
TB = 256
ROPE_THETA = 500000.0
VMEM_LIMIT = 56 * 1024 * 1024
HI = lax.Precision.HIGHEST

ADAM_LR, ADAM_B1, ADAM_B2, ADAM_EPS, ADAM_WD, ADAM_STEP = 0.001, 0.9, 0.999, 1e-08, 0.01, 10


def _params(n_grid=0, **kw):
    sem = dict(dimension_semantics=("arbitrary",) * n_grid) if n_grid else {}
    return pltpu.CompilerParams(vmem_limit_bytes=VMEM_LIMIT, **sem, **kw)


def _cspec(shape):
    nd = len(shape)
    return pl.BlockSpec(shape, lambda *_: (0,) * nd, pipeline_mode=pl.Buffered(1))


def _dot(a, b):
    return jnp.dot(a, b, preferred_element_type=F32)


def _dot_nt(a, b):
    return lax.dot_general(a, b, (((1,), (1,)), ((), ())), preferred_element_type=F32)


def _dot_tn(a, b):
    return lax.dot_general(a, b, (((0,), (0,)), ((), ())), preferred_element_type=F32)


def _bf(x):
    return x.astype(BF16)


def _rms(x):
    rs = lax.rsqrt(jnp.mean(x * x, axis=-1, keepdims=True) + EPS)
    return x * rs, rs


def _rms_bwd(du, n, rs, g):
    dn = du * g
    return rs * (dn - n * jnp.mean(dn * n, axis=-1, keepdims=True))


def _sigmoid(x):
    return 1.0 / (1.0 + jnp.exp(-x))


ROPE_BLK = 512


def _rope_parts():
    def cs(n, step):
        j = lax.broadcasted_iota(jnp.int32, (n, 128), 1) % 64
        pos = (lax.broadcasted_iota(jnp.int32, (n, 128), 0) * step).astype(F32)
        ang = pos * jnp.power(ROPE_THETA, -(j % 8).astype(F32) / 8.0)
        return jnp.cos(ang), jnp.sin(ang)

    return (*cs(ROPE_BLK, 1), *cs(S // ROPE_BLK, ROPE_BLK))


def _rope_fill(co_ref, so_ref, cb_ref, sb_ref, rc_ref, ra_ref, rb_ref):
    j = lax.broadcasted_iota(jnp.int32, (ROPE_BLK, 128), 1) % 64
    co, so = co_ref[...], so_ref[...]
    for t in range(S // ROPE_BLK):
        cb, sb = cb_ref[t:t + 1, :], sb_ref[t:t + 1, :]
        cos, sin = cb * co - sb * so, sb * co + cb * so
        rows = slice(t * ROPE_BLK, (t + 1) * ROPE_BLK)
        rc_ref[rows, :] = jnp.where(j < 16, cos, 1.0)
        ra_ref[rows, :] = jnp.where(j < 8, -sin, 0.0)
        rb_ref[rows, :] = jnp.where((j >= 8) & (j < 16), sin, 0.0)


def _rope(blk, c, a, b):
    return blk * c + pltpu.roll(blk, 120, 1) * a + pltpu.roll(blk, 8, 1) * b


def _rope_bwd(d, c, a, b):
    return d * c + pltpu.roll(d * a, 8, 1) + pltpu.roll(d * b, 120, 1)


MESH = pl.DeviceIdType.MESH
ANY = pl.BlockSpec(memory_space=pl.ANY)
VM = pl.BlockSpec(memory_space=pltpu.VMEM)
FLIPS = [(dx, dy, dc) for dx in (0, 1) for dy in (0, 1) for dc in (0, 1)][1:]


def _place():
    return lax.axis_index("x"), lax.axis_index("y"), lax.axis_index("c")


def _dev_index(px, py, pc):
    return 4 * px + 2 * py + pc


def _gather_phases(ins, outs, bufs, send_sems=None, recv_sems=None, local_sems=None):
    nw = len(ins)
    if nw == 0:
        return (lambda: None,) * 3
    x, y, c = _place()
    me, sib = (x, y, c), (x, y, 1 - c)
    chips = [(1 - x, y), (x, 1 - y), (1 - x, 1 - y)]

    def copy(w, k, block, to, from_buf=False):
        dst = outs[w].at[_dev_index(*block)]
        return pltpu.make_async_remote_copy(
            src_ref=bufs[w] if from_buf else dst, dst_ref=dst, send_sem=send_sems.at[w, k],
            recv_sem=recv_sems.at[w, k], device_id=to, device_id_type=MESH)

    def mine(w):
        return pltpu.make_async_copy(bufs[w], outs[w].at[_dev_index(*me)], local_sems.at[w])

    def first(w):
        return [copy(w, 0, me, sib, True)] + [copy(w, 1 + j, me, (*chip, c), True) for j, chip in enumerate(chips)]

    def passed(w):
        return [copy(w, 4 + j, (*chip, c), sib) for j, chip in enumerate(chips)]

    def start():
        for w in range(nw):
            bufs[w][...] = ins[w][...].astype(bufs[w].dtype)
        for w in range(nw):
            mine(w).start()
            for cp in first(w):
                cp.start()

    def forward():
        for j, chip in enumerate(chips):
            for w in range(nw):
                copy(w, 1 + j, (*chip, c), me).wait_recv()
                passed(w)[j].start()

    def finish():
        for w in range(nw):
            copy(w, 0, sib, me).wait_recv()
        for j, chip in enumerate(chips):
            for w in range(nw):
                copy(w, 4 + j, (*chip, 1 - c), me).wait_recv()
        for w in range(nw):
            for cp in first(w) + passed(w):
                cp.wait_send()
            mine(w).wait()

    return start, forward, finish


def _gather_scratch(shards, dtypes):
    nw = len(shards)
    if nw == 0:
        return []
    return ([pltpu.VMEM(s.shape, dt) for s, dt in zip(shards, dtypes)]
            + [pltpu.SemaphoreType.DMA((nw, 7)), pltpu.SemaphoreType.DMA((nw, 7)), pltpu.SemaphoreType.DMA((nw,))])


def _gather_shapes(shards, dtypes):
    return [jax.ShapeDtypeStruct((NDEV, *s.shape), dt) for s, dt in zip(shards, dtypes)]


def _scatter_phases(ins, outs, send_sems=None, recv_sems=None, local_sems=None):
    nw = len(ins)
    if nw == 0:
        return (lambda: None,) * 2
    x, y, c = _place()
    me = _dev_index(x, y, c)

    def copies():
        out = []
        for w in range(nw):
            out.append(pltpu.make_async_copy(ins[w].at[me], outs[w].at[me], local_sems.at[w]))
            for k, (dx, dy, dc) in enumerate(FLIPS):
                peer = ((x + dx) % 2, (y + dy) % 2, (c + dc) % 2)
                out.append(pltpu.make_async_remote_copy(
                    src_ref=ins[w].at[_dev_index(*peer)], dst_ref=outs[w].at[me], send_sem=send_sems.at[w, k],
                    recv_sem=recv_sems.at[w, k], device_id=peer, device_id_type=MESH))
        return out

    def start():
        for cp in copies():
            cp.start()

    def finish():
        for cp in copies():
            cp.wait()

    return start, finish


def _scatter_scratch(nw):
    if nw == 0:
        return []
    return [pltpu.SemaphoreType.DMA((nw, 7)), pltpu.SemaphoreType.DMA((nw, 7)), pltpu.SemaphoreType.DMA((nw,))]


TM = 512


def _join_w_in(wg):
    sw = IN_W // NDEV

    def body(wg_ref, w_ref):
        for j in range(NDEV):
            w_ref[:, sw * j:sw * (j + 1)] = wg_ref[j]
        w_ref[:, IN_W:PW] = jnp.zeros((D, PW - IN_W), BF16)

    return pl.pallas_call(body, name="join_w_in", out_shape=jax.ShapeDtypeStruct((D, PW), BF16),
                          compiler_params=_params())(wg)


def _in_proj(x, g1, w, rc, ra, rb, shards, dtypes):
    tm = TM
    nw = len(shards)
    nt = S // tm

    def body(*refs):
        x_ref, g_ref, w_ref, rc_ref, ra_ref, rb_ref = refs[:6]
        ins = refs[6:6 + nw]
        qkv_ref, mqk_ref, mv_ref, mo_ref, gt_ref, u_ref = refs[6 + nw:12 + nw]
        outs = refs[12 + nw:12 + 2 * nw]
        bufs = refs[12 + 2 * nw:12 + 3 * nw]
        ag_start, ag_forward, ag_finish = _gather_phases(ins, outs, bufs, *refs[12 + 3 * nw:])
        i = pl.program_id(0)
        pl.when(i == 0)(ag_start)
        pl.when(i == nt - 2)(ag_forward)
        n, _ = _rms(x_ref[...])
        u = _bf(n * g_ref[...])
        u_ref[...] = u
        c, a, b = rc_ref[...], ra_ref[...], rb_ref[...]
        for half in range(2):
            blk = _dot(u, w_ref[:, half * 512:(half + 1) * 512])
            for t in range(4):
                lo = half * 512 + t * 128
                qkv_ref[:, lo:lo + 128] = _rope(blk[:, t * 128:(t + 1) * 128], c, a, b)
        qkv_ref[:, 1024:1536] = _dot(u, w_ref[:, 1024:1536])
        mqk_ref[:, 0:512] = _dot(u, w_ref[:, 1536:2048])
        mqk_ref[:, 512:1024] = _dot(u, w_ref[:, 2048:2560])
        mv_ref[...] = _dot(u, w_ref[:, 2560:3072])
        mo_ref[...] = _dot(u, w_ref[:, 3072:3584])
        gt_ref[...] = _dot(u, w_ref[:, 3584:3712])
        pl.when(i == nt - 1)(ag_finish)

    row = lambda wd: pl.BlockSpec((tm, wd), lambda i: (i, 0))
    res = pl.pallas_call(
        body, name="in_proj", grid=(nt,),
        in_specs=[row(D), _cspec((1, D)), _cspec((D, PW)), row(128), row(128), row(128)] + [VM] * nw,
        out_specs=[row(1536), row(1024), row(512), row(512), row(128), row(D)] + [ANY] * nw,
        out_shape=[jax.ShapeDtypeStruct((S, 1536), F32), jax.ShapeDtypeStruct((S, 1024), F32),
                   jax.ShapeDtypeStruct((S, 512), F32), jax.ShapeDtypeStruct((S, 512), F32),
                   jax.ShapeDtypeStruct((S, 128), F32), jax.ShapeDtypeStruct((S, D), BF16)]
        + _gather_shapes(shards, dtypes),
        scratch_shapes=_gather_scratch(shards, dtypes),
        compiler_params=_params(1),
    )(x, g1, w, rc, ra, rb, *shards)
    return res[:6], res[6:]


DILATIONS = (16, 4, 1)


def _attn_valid(n):
    kd = lax.broadcasted_iota(jnp.int32, (128, 256), 1) - lax.broadcasted_iota(jnp.int32, (128, 256), 0)
    off = jnp.where(n == 0, 0, 128)
    return (kd <= off) & (kd >= off - 128)


def _attn_rows(d, r, n):
    if d == 1:
        q0 = pl.multiple_of(n * 128, 128)
        k0 = pl.multiple_of(jnp.maximum(n - 1, 0) * 128, 128)
        return pl.ds(q0, 128), pl.ds(k0, 256), _attn_valid(n)
    q0 = r + n * 128 * d
    k0 = r + jnp.maximum(n - 1, 0) * 128 * d
    return pl.ds(q0, 128, stride=d), pl.ds(k0, 256, stride=d), _attn_valid(n)


ATTN_GROUP = 4
ATTN_ITERS = S // 128 // ATTN_GROUP


def _attn_group(d, i):
    nb = S // (128 * d)
    if nb == 2:
        qi = lax.broadcasted_iota(jnp.int32, (256, 256), 0) - lax.broadcasted_iota(jnp.int32, (256, 256), 1)
        whole = [pl.ds((ATTN_GROUP // 2) * i + u, 256, stride=d) for u in range(ATTN_GROUP // 2)]
        return [(rows, rows, (qi >= 0) & (qi <= 128)) for rows in whole]
    if d == 1:
        return [_attn_rows(1, 0, i + ATTN_ITERS * u) for u in range(ATTN_GROUP)]
    return [_attn_rows(d, (i // nb) * ATTN_GROUP + u, i % nb) for u in range(ATTN_GROUP)]


def _head0(shape):
    return lax.broadcasted_iota(jnp.int32, shape, 1) < 64


def _stack_heads(t):
    h0 = _head0(t.shape)
    tb = _bf(t)
    zero = jnp.zeros_like(tb)
    return jnp.concatenate([jnp.where(h0, tb, zero), jnp.where(h0, zero, tb)], axis=0)


def _attn_fwd(qkv, shards, dtypes):
    nw = len(shards)

    def body(*refs):
        q_ref, k_ref, v_ref = refs[:3]
        ins = refs[3:3 + nw]
        o_ref, lse_ref = refs[3 + nw:5 + nw]
        outs = refs[5 + nw:5 + 2 * nw]
        m0, m1, l0, l1, acc = refs[5 + 2 * nw:10 + 2 * nw]
        bufs = refs[10 + 2 * nw:10 + 3 * nw]
        ag_start, ag_forward, ag_finish = _gather_phases(ins, outs, bufs, *refs[10 + 3 * nw:])
        hp = pl.program_id(0)
        pl.when(hp == 0)(ag_start)
        pl.when(hp == 3)(ag_forward)
        stats = (m0, m1, l0, l1, acc)

        def update(blocks, first):
            loaded = [([q_ref[rq, :], k_ref[rk, :], v_ref[rk, :]], None if first else [ref[rq, :] for ref in stats])
                      for rq, rk, _ in blocks]
            results = []
            for ((q, k, v), prev), (_, _, valid) in zip(loaded, blocks):
                head0 = _head0(q.shape)
                kb, vb = _bf(k), _bf(v)
                q = q * 0.125
                m_new, l_new, acc_new = [], [], []
                for a, qa in enumerate((_bf(jnp.where(head0, q, 0.0)), _bf(jnp.where(head0, 0.0, q)))):
                    s = jnp.where(valid, _dot_nt(qa, kb), NEG)
                    mc = jnp.max(s, axis=-1, keepdims=True)
                    m_a = jnp.broadcast_to(mc, q.shape) if first else jnp.maximum(prev[a], mc)
                    p = jnp.exp(s - jnp.tile(m_a, (1, 2)))
                    l_add = jnp.sum(p, axis=-1, keepdims=True)
                    pv = _dot(_bf(p), vb)
                    if first:
                        l_a = jnp.broadcast_to(l_add, q.shape)
                    else:
                        alpha = jnp.exp(prev[a] - m_a)
                        l_a, pv = alpha * prev[2 + a] + l_add, alpha * prev[4] + pv
                    m_new.append(m_a), l_new.append(l_a), acc_new.append(pv)
                results.append((m_new[0], m_new[1], l_new[0], l_new[1], jnp.where(head0, acc_new[0], acc_new[1])))
            for (rq, _, _), res in zip(blocks, results):
                for ref, val in zip(stats, res):
                    ref[rq, :] = val

        for d in DILATIONS:
            def step(i, carry, d=d):
                update(_attn_group(d, i), d == DILATIONS[0])
                return carry

            lax.fori_loop(0, ATTN_ITERS, step, 0)

        def fin(t, carry):
            rows = pl.ds(pl.multiple_of(t * 256, 256), 256)
            h0 = lax.broadcasted_iota(jnp.int32, (256, 128), 1) < 64
            l = jnp.where(h0, l0[rows, :], l1[rows, :])
            o_ref[rows, :] = acc[rows, :] / l
            lse_ref[rows, :] = jnp.where(h0, m0[rows, :], m1[rows, :]) + jnp.log(l)
            return carry

        lax.fori_loop(0, S // 256, fin, 0)
        pl.when(hp == 3)(ag_finish)

    col = lambda off: pl.BlockSpec((S, 128), lambda h, off=off: (0, off + h))
    res = pl.pallas_call(
        body, name="attn_fwd", grid=(4,),
        in_specs=[col(0), col(4), col(8)] + [VM] * nw,
        out_specs=[col(0), col(0)] + [ANY] * nw,
        out_shape=[jax.ShapeDtypeStruct((S, AW), F32), jax.ShapeDtypeStruct((S, AW), F32)]
        + _gather_shapes(shards, dtypes),
        scratch_shapes=[pltpu.VMEM((S, 128), F32)] * 5 + _gather_scratch(shards, dtypes),
        compiler_params=_params(1),
    )(qkv, qkv, qkv, *shards)
    return res[0], res[1], res[2:]


def _attn_bwd(qkv, o, lse, do, rc, ra, rb, parts):
    nw = len(parts)

    def body(*refs):
        q_ref, k_ref, v_ref, o_ref, lse_ref, do_ref, rc_ref, ra_ref, rb_ref = refs[:9]
        ins = refs[9:9 + nw]
        dq_out, dk_out, dv_out = refs[9 + nw:12 + nw]
        outs = refs[12 + nw:12 + 2 * nw]
        L0, L1, D0, D1, dq_ref, dk_ref, dv_ref = refs[12 + 2 * nw:19 + 2 * nw]
        rs_start, rs_finish = _scatter_phases(ins, outs, *refs[19 + 2 * nw:])
        hp = pl.program_id(0)
        pl.when(hp == 0)(rs_start)

        def pre(t, carry):
            rows = pl.ds(pl.multiple_of(t * 256, 256), 256)
            h0 = lax.broadcasted_iota(jnp.int32, (256, 128), 1) < 64
            ls = lse_ref[rows, :]
            dd = do_ref[rows, :] * o_ref[rows, :]
            shp = (256, 128)
            L0[rows, :] = jnp.broadcast_to(jnp.max(jnp.where(h0, ls, NEG), axis=-1, keepdims=True), shp)
            L1[rows, :] = jnp.broadcast_to(jnp.max(jnp.where(h0, NEG, ls), axis=-1, keepdims=True), shp)
            D0[rows, :] = jnp.broadcast_to(jnp.sum(jnp.where(h0, dd, 0.0), axis=-1, keepdims=True), shp)
            D1[rows, :] = jnp.broadcast_to(jnp.sum(jnp.where(h0, 0.0, dd), axis=-1, keepdims=True), shp)
            return carry

        lax.fori_loop(0, S // 256, pre, 0)

        def update(blocks, first):
            loaded = [([q_ref[rq, :], k_ref[rk, :], v_ref[rk, :], do_ref[rq, :]],
                       [L0[rq, :], L1[rq, :], D0[rq, :], D1[rq, :]],
                       [0.0] * 3 if first else [dq_ref[rq, :], dk_ref[rk, :], dv_ref[rk, :]]) for rq, rk, _ in blocks]
            results = []
            for ((q, k, v, dout), (l0v, l1v, d0v, d1v), (dq, dk, dv)), (_, _, valid) in zip(loaded, blocks):
                valid = jnp.tile(valid, (1, 2))
                kst, vst = _stack_heads(k), _stack_heads(v)
                hk = _head0((256, 128))
                dob = _bf(dout)
                cat = lambda a, b: jnp.concatenate([jnp.tile(a, (1, 2)), jnp.tile(b, (1, 2))], axis=1)
                s = jnp.where(valid, _dot_nt(_bf(q * 0.125), kst), NEG)
                p = jnp.exp(s - cat(l0v, l1v))
                ds = _bf(p * (_dot_nt(dob, vst) - cat(d0v, d1v)) * 0.125)
                dk2 = _dot_tn(ds, _bf(q))
                dv2 = _dot_tn(_bf(p), dob)
                results.append((dq + _dot(ds, kst), dk + jnp.where(hk, dk2[0:256], dk2[256:512]),
                                dv + jnp.where(hk, dv2[0:256], dv2[256:512])))
            for (rq, rk, _), (dq, dk, dv) in zip(blocks, results):
                dq_ref[rq, :] = dq
                dk_ref[rk, :] = dk
                dv_ref[rk, :] = dv

        assert S // (128 * DILATIONS[0]) == 2
        for d in DILATIONS:
            def step(i, carry, d=d):
                update(_attn_group(d, i), d == DILATIONS[0])
                return carry

            lax.fori_loop(0, ATTN_ITERS, step, 0)

        def fin(t, carry):
            rows = pl.ds(pl.multiple_of(t * 256, 256), 256)
            c, a, b = rc_ref[rows, :], ra_ref[rows, :], rb_ref[rows, :]
            dq_out[rows, :] = _bf(_rope_bwd(dq_ref[rows, :], c, a, b))
            dk_out[rows, :] = _bf(_rope_bwd(dk_ref[rows, :], c, a, b))
            dv_out[rows, :] = _bf(dv_ref[rows, :])
            return carry

        lax.fori_loop(0, S // 256, fin, 0)
        pl.when(hp == 3)(rs_finish)

    col = lambda off: pl.BlockSpec((S, 128), lambda h, off=off: (0, off + h))
    res = pl.pallas_call(
        body, name="attn_bwd", grid=(4,),
        in_specs=[col(0), col(4), col(8), col(0), col(0), col(0)] + [_cspec((S, 128))] * 3 + [ANY] * nw,
        out_specs=[col(0), col(0), col(0)] + [ANY] * nw,
        out_shape=[jax.ShapeDtypeStruct((S, AW), BF16)] * 3 + [jax.ShapeDtypeStruct(a.shape, a.dtype) for a in parts],
        scratch_shapes=[pltpu.VMEM((S, 128), F32)] * 7 + _scatter_scratch(nw),
        compiler_params=_params(1),
    )(qkv, qkv, qkv, o, lse, do, rc, ra, rb, *parts)
    return res[0], res[1], res[2], res[3:]


def _logsig(x):
    return jnp.minimum(x, 0.0) - jnp.log1p(jnp.exp(-jnp.abs(x)))


def _conv_taps(xp, n):
    return [xp[8:] if j == 3 else pltpu.roll(xp, 3 - j, 0)[8:] for j in range(4)]


def _conv_silu(xp, w_ref, b_ref, n):
    taps = _conv_taps(xp, n)
    c = b_ref[...] + sum(w_ref[j:j + 1, :] * taps[j] for j in range(4))
    sg = _sigmoid(c)
    return c, sg, taps


def _chunk_gates(G):
    assert LC == 128
    r = lax.broadcasted_iota(jnp.int32, (LC, LC), 0)
    c = lax.broadcasted_iota(jnp.int32, (LC, LC), 1)
    tril = (c <= r).astype(F32)
    triu = (c >= r).astype(F32)
    b_col = jnp.dot(tril, _logsig(G), preferred_element_type=F32, precision=HI)
    return b_col, b_col.T, G.T, tril, triu


def _colpick(X, lane):
    li = lax.broadcasted_iota(jnp.int32, X.shape, 1)
    return jnp.sum(jnp.where(li == lane, X, 0.0), axis=1, keepdims=True)


def _rowpick(XT, row):
    ri = lax.broadcasted_iota(jnp.int32, XT.shape, 0)
    return jnp.sum(jnp.where(ri == row, XT, 0.0), axis=0, keepdims=True)


def _mlstm_head(qh, kh, vh, G, b_col, b_row, g_row, h, Ch, nh, m_prev):
    bt = _colpick(b_col, 4 + h)
    i_col = _colpick(G, h)
    bs = _rowpick(b_row, 4 + h)
    i_row = _rowpick(g_row, h)
    r = lax.broadcasted_iota(jnp.int32, (LC, LC), 0)
    c = lax.broadcasted_iota(jnp.int32, (LC, LC), 1)
    log_d = jnp.where(c <= r, bt - bs + i_row, NEG)
    log_inter = bt + m_prev
    m_t = jnp.maximum(log_inter, jnp.max(log_d, axis=1, keepdims=True))
    Dm = jnp.exp(log_d - m_t)
    g = jnp.exp(log_inter - m_t)
    qb, kb, vb = _bf(qh), _bf(kh), _bf(vh)
    Am = _dot_nt(qb, kb) * Dm
    qC = _dot(qb, _bf(Ch))
    num = g * qC + _dot(_bf(Am), vb)
    qn = jnp.sum(qh * nh, axis=1, keepdims=True)
    den = g * qn + jnp.sum(Am, axis=1, keepdims=True)
    floor = jnp.exp(-m_t)
    dd = jnp.maximum(jnp.abs(den), floor)
    inv_dd = 1.0 / dd
    hh = num * inv_dd
    lane = lax.broadcasted_iota(jnp.int32, (1, LC), 1)
    blast = jnp.sum(jnp.where(lane == LC - 1, bs, 0.0), axis=1, keepdims=True)
    log_s = blast - bt + i_col
    m_new = jnp.maximum(blast + m_prev, jnp.max(log_s, axis=0, keepdims=True))
    decay = jnp.exp(blast + m_prev - m_new)
    ws = jnp.exp(log_s - m_new)
    kw = kh * ws
    C_new = decay * Ch + _dot_tn(_bf(kw), vb)
    n_new = decay * nh + jnp.sum(kw, axis=0, keepdims=True)
    return dict(Dm=Dm, g=g, Am=Am, qC=qC, qn=qn, den=den, floor=floor, inv_dd=inv_dd, h=hh, decay=decay, ws=ws, kw=kw,
                C_new=C_new, n_new=n_new, m_new=m_new, qb=qb, kb=kb, vb=vb)


def _head_out(hh, mo_h, gn_h):
    r = lax.rsqrt(jnp.mean(hh * hh, axis=-1, keepdims=True) + EPS)
    hn = hh * r
    sg = _sigmoid(mo_h)
    return sg * (hn * gn_h), hn, r, sg


def _mlstm_fwd(mqk, mv, mo, gates, conv_w, conv_b, gate_b, gn, shards, dtypes):
    nblk = S // TB
    ncb = TB // LC
    nw = len(shards)

    def body(*refs):
        x_ref, v_ref, o_ref, g_ref, w_ref, b_ref, gb_ref, gn_ref = refs[:8]
        ins = refs[8:8 + nw]
        out_ref, cs_ref, ns_ref, ms_ref = refs[8 + nw:12 + nw]
        outs = refs[12 + nw:12 + 2 * nw]
        tail, Cst, nst, mst, qs, ks = refs[12 + 2 * nw:18 + 2 * nw]
        bufs = refs[18 + 2 * nw:18 + 3 * nw]
        ag_start, ag_forward, ag_finish = _gather_phases(ins, outs, bufs, *refs[18 + 3 * nw:])
        i = pl.program_id(0)
        pl.when(i == 0)(ag_start)
        pl.when(i == nblk // 2)(ag_forward)

        @pl.when(i == 0)
        def _():
            tail[...] = jnp.zeros_like(tail)
            Cst[...] = jnp.zeros_like(Cst)
            nst[...] = jnp.zeros_like(nst)
            mst[...] = jnp.zeros_like(mst)

        x = x_ref[...]
        xp = jnp.concatenate([tail[...], x], axis=0)
        tail[...] = x[TB - 8:TB, :]
        c, sg, _ = _conv_silu(xp, w_ref, b_ref, TB)
        y = c * sg
        qs[...] = y[:, 0:MW]
        ks[...] = y[:, MW:2 * MW] * (1.0 / math.sqrt(128.0))

        for cc in range(ncb):
            rows = slice(cc * LC, (cc + 1) * LC)
            G = g_ref[rows, :] + gb_ref[...]
            b_col, b_row, g_row, _, _ = _chunk_gates(G)
            cs_ref[cc] = Cst[...]
            ns_ref[cc] = nst[...]
            ms_ref[cc] = mst[...]
            for h in range(4):
                ln = slice(h * 128, (h + 1) * 128)
                m_prev = jnp.max(mst[0:1, ln], axis=1, keepdims=True)
                f = _mlstm_head(qs[rows, ln], ks[rows, ln], v_ref[rows, ln], G, b_col, b_row, g_row, h,
                                Cst[:, ln], nst[0:1, ln], m_prev)
                out, _, _, _ = _head_out(f["h"], o_ref[rows, ln], gn_ref[:, ln])
                out_ref[rows, ln] = out
                Cst[:, ln] = f["C_new"]
                nst[0:1, ln] = f["n_new"]
                mst[0:1, ln] = jnp.broadcast_to(f["m_new"], (1, 128))
        pl.when(i == nblk - 1)(ag_finish)

    row = lambda wd: pl.BlockSpec((TB, wd), lambda i: (i, 0))
    res = pl.pallas_call(
        body, name="mlstm_fwd", grid=(nblk,),
        in_specs=[row(1024), row(MW), row(MW), row(128), _cspec((4, 1024)), _cspec((1, 1024)), _cspec((1, 128)),
                  _cspec((1, MW))] + [VM] * nw,
        out_specs=[row(MW), pl.BlockSpec((ncb, 128, MW), lambda i: (i, 0, 0)),
                   pl.BlockSpec((ncb, 8, MW), lambda i: (i, 0, 0)), pl.BlockSpec((ncb, 8, MW), lambda i: (i, 0, 0))]
        + [ANY] * nw,
        out_shape=[jax.ShapeDtypeStruct((S, MW), F32), jax.ShapeDtypeStruct((S // LC, 128, MW), F32),
                   jax.ShapeDtypeStruct((S // LC, 8, MW), F32), jax.ShapeDtypeStruct((S // LC, 8, MW), F32)]
        + _gather_shapes(shards, dtypes),
        scratch_shapes=[pltpu.VMEM((8, 1024), F32), pltpu.VMEM((128, MW), F32), pltpu.VMEM((8, MW), F32),
                        pltpu.VMEM((8, MW), F32), pltpu.VMEM((TB, MW), F32), pltpu.VMEM((TB, MW), F32)]
        + _gather_scratch(shards, dtypes),
        compiler_params=_params(1),
    )(mqk, mv, mo, gates, conv_w, conv_b, gate_b, gn, *shards)
    return res[0], res[1], res[2], res[3], res[4:]


DM_V, DM_O, DM_G, DM_W = 1024, 1536, 2048, PW - 3 * AW


def _mlstm_bwd(mqk, mv, mo, gates, conv_w, conv_b, gate_b, gn, cs, ns, ms, dout, parts):
    nblk = S // TB
    ncb = TB // LC
    kscale = 1.0 / math.sqrt(128.0)
    nw = len(parts)

    def body(*refs):
        x_ref, xprev_ref, v_ref, o_ref, g_ref, w_ref, b_ref, gb_ref, gn_ref, cs_ref, ns_ref, ms_ref, do_ref = refs[:13]
        ins = refs[13:13 + nw]
        dm_ref, dw_ref, db_ref, dgn_ref, dgb_ref = refs[13 + nw:18 + nw]
        outs = refs[18 + nw:18 + 2 * nw]
        dCst, dnst, dyhead, qs, ks, dqk = refs[18 + 2 * nw:24 + 2 * nw]
        rs_start, rs_finish = _scatter_phases(ins, outs, *refs[24 + 2 * nw:])
        i = pl.program_id(0)
        blk = nblk - 1 - i
        pl.when(i == 0)(rs_start)

        @pl.when(i == 0)
        def _():
            dCst[...] = jnp.zeros_like(dCst)
            dnst[...] = jnp.zeros_like(dnst)
            dyhead[...] = jnp.zeros_like(dyhead)
            dw_ref[...] = jnp.zeros_like(dw_ref)
            db_ref[...] = jnp.zeros_like(db_ref)
            dgn_ref[...] = jnp.zeros_like(dgn_ref)
            dgb_ref[...] = jnp.zeros_like(dgb_ref)

        x = x_ref[...]
        xprev = jnp.where(blk == 0, 0.0, xprev_ref[...])
        xp = jnp.concatenate([xprev, x], axis=0)
        c, sg, taps = _conv_silu(xp, w_ref, b_ref, TB)
        y = c * sg
        qs[...] = y[:, 0:MW]
        ks[...] = y[:, MW:2 * MW] * kscale
        lane128 = lax.broadcasted_iota(jnp.int32, (LC, 128), 1)
        rowi = lax.broadcasted_iota(jnp.int32, (LC, 1), 0)
        ones = jnp.ones((LC, 128), F32)

        for cc in reversed(range(ncb)):
            rows = slice(cc * LC, (cc + 1) * LC)
            G = g_ref[rows, :] + gb_ref[...]
            b_col, b_row, g_row, _, triu = _chunk_gates(G)
            dB = jnp.zeros((LC, 128), F32)
            dI = jnp.zeros((LC, 128), F32)
            for h in range(4):
                ln = slice(h * 128, (h + 1) * 128)
                Ch = cs_ref[cc, :, ln]
                nh = ns_ref[cc, 0:1, ln]
                m_prev = jnp.max(ms_ref[cc, 0:1, ln], axis=1, keepdims=True)
                qh, kh, vh = qs[rows, ln], ks[rows, ln], v_ref[rows, ln]
                f = _mlstm_head(qh, kh, vh, G, b_col, b_row, g_row, h, Ch, nh, m_prev)
                hh, inv_dd, den, g, Am, Dm = f["h"], f["inv_dd"], f["den"], f["g"], f["Am"], f["Dm"]
                qb, kb, vb = f["qb"], f["kb"], f["vb"]
                gn_h = gn_ref[:, ln]
                _, hn, r, sgo = _head_out(hh, o_ref[rows, ln], gn_h)
                do = do_ref[rows, ln]
                hm = hn * gn_h
                dm_ref[rows, DM_O + h * 128:DM_O + (h + 1) * 128] = _bf(do * hm * sgo * (1.0 - sgo))
                dhm = do * sgo
                dgn_ref[:, ln] = dgn_ref[:, ln] + jnp.sum(dhm * hn, axis=0, keepdims=True)
                dhn = dhm * gn_h
                dh = r * (dhn - hn * jnp.mean(dhn * hn, axis=-1, keepdims=True))
                dnum = dh * inv_dd
                ddd = -jnp.sum(dh * hh, axis=1, keepdims=True) * inv_dd
                dden = jnp.where(jnp.abs(den) >= f["floor"], ddd * jnp.sign(den), 0.0)
                dnb = _bf(dnum)
                dA = _dot_nt(dnb, vb) + dden
                dv = _dot_tn(_bf(Am), dnb)
                gd = _bf(g * dnum)
                gq = g * dden
                dq = _dot_nt(gd, _bf(Ch)) + gq * nh
                dCn = dCst[:, ln]
                dnn = dnst[0:1, ln]
                dC = f["decay"] * dCn + _dot_tn(qb, gd)
                dn = f["decay"] * dnn + jnp.sum(gq * qh, axis=0, keepdims=True)
                dg = jnp.sum(dnum * f["qC"], axis=1, keepdims=True) + dden * f["qn"]
                dS = _bf(dA * Dm)
                dq = dq + _dot(dS, kb)
                dk = _dot_tn(dS, qb)
                Gm = dA * Am
                gam = dg * g
                dCb = _bf(dCn)
                E = _dot_nt(vb, dCb) + dnn
                ws = f["ws"]
                dk = dk + ws * E
                om = jnp.sum(E * kh, axis=1, keepdims=True) * ws
                dv = dv + _dot(_bf(f["kw"]), dCb)
                ddecay = (jnp.sum(jnp.sum(dCn * Ch, axis=1, keepdims=True), axis=0, keepdims=True)
                          + jnp.sum(dnn * nh, axis=1, keepdims=True))
                delta = ddecay * f["decay"]
                rows_g = jnp.sum(Gm, axis=1, keepdims=True)
                cols_g = jnp.broadcast_to(jnp.sum(Gm, axis=0, keepdims=True), (LC, 128)).T
                last = jnp.where(rowi == LC - 1, jnp.sum(om, axis=0, keepdims=True) + delta, 0.0)
                db = rows_g + gam - om + last - cols_g
                di = cols_g + om
                dB = jnp.where(lane128 == 4 + h, db, dB)
                dI = jnp.where(lane128 == h, di, dI)
                dCst[:, ln] = dC
                dnst[0:1, ln] = dn
                dqk[rows, ln] = dq
                dqk[rows, MW + h * 128:MW + (h + 1) * 128] = dk * kscale
                dm_ref[rows, DM_V + h * 128:DM_V + (h + 1) * 128] = _bf(dv)
            dlogf = jnp.dot(triu, dB, preferred_element_type=F32, precision=HI)
            dG = dI + dlogf * _sigmoid(-G)
            dG = jnp.where(lane128 < 8, dG, 0.0)
            dm_ref[rows, DM_G:DM_G + 128] = _bf(dG)
            dm_ref[rows, DM_G + 128:DM_W] = jnp.zeros((LC, DM_W - DM_G - 128), BF16)
            dgb_ref[...] = dgb_ref[...] + jnp.sum(dG, axis=0, keepdims=True)

        dy = dqk[...] * (sg * (1.0 + c * (1.0 - sg)))
        db_ref[...] = db_ref[...] + jnp.sum(dy, axis=0, keepdims=True)
        for j in range(4):
            dw_ref[j:j + 1, :] = dw_ref[j:j + 1, :] + jnp.sum(dy * taps[j], axis=0, keepdims=True)
        dyp = jnp.concatenate([dy, dyhead[...]], axis=0)
        dx = w_ref[3:4, :] * dy
        for j in range(3):
            dx = dx + w_ref[j:j + 1, :] * pltpu.roll(dyp, TB + 8 - (3 - j), 0)[0:TB]
        dm_ref[:, 0:DM_V] = _bf(dx)
        dyhead[...] = dy[0:8, :]
        pl.when(i == nblk - 1)(rs_finish)

    rrow = lambda wd: pl.BlockSpec((TB, wd), lambda i: (nblk - 1 - i, 0))
    st = lambda r: pl.BlockSpec((ncb, r, MW), lambda i: (nblk - 1 - i, 0, 0))
    prev8 = pl.BlockSpec((8, 1024), lambda i: (jnp.maximum((nblk - 1 - i) * (TB // 8) - 1, 0), 0))
    res = pl.pallas_call(
        body, name="mlstm_bwd", grid=(nblk,),
        in_specs=[rrow(1024), prev8, rrow(MW), rrow(MW), rrow(128), _cspec((4, 1024)), _cspec((1, 1024)),
                  _cspec((1, 128)), _cspec((1, MW)), st(128), st(8), st(8), rrow(MW)] + [ANY] * nw,
        out_specs=[rrow(DM_W),
                   pl.BlockSpec((4, 1024), lambda i: (0, 0)), pl.BlockSpec((1, 1024), lambda i: (0, 0)),
                   pl.BlockSpec((1, MW), lambda i: (0, 0)), pl.BlockSpec((1, 128), lambda i: (0, 0))] + [ANY] * nw,
        out_shape=[jax.ShapeDtypeStruct((S, DM_W), BF16),
                   jax.ShapeDtypeStruct((4, 1024), F32), jax.ShapeDtypeStruct((1, 1024), F32),
                   jax.ShapeDtypeStruct((1, MW), F32), jax.ShapeDtypeStruct((1, 128), F32)]
        + [jax.ShapeDtypeStruct(a.shape, a.dtype) for a in parts],
        scratch_shapes=[pltpu.VMEM((128, MW), F32), pltpu.VMEM((8, MW), F32), pltpu.VMEM((8, 1024), F32),
                        pltpu.VMEM((TB, MW), F32), pltpu.VMEM((TB, MW), F32), pltpu.VMEM((TB, 1024), F32)]
        + _scatter_scratch(nw),
        compiler_params=_params(1),
    )(mqk, mqk, mv, mo, gates, conv_w, conv_b, gate_b, gn, cs, ns, ms, dout, *parts)
    return res[:5], res[5:]


def _out_proj(x, attn, ml, w, g):
    tm = TM

    def body(x_ref, a_ref, m_ref, w_ref, g_ref, h_ref, u_ref):
        h1 = x_ref[...] + _dot(_bf(a_ref[...]), w_ref[0:AW, :]) + _dot(_bf(m_ref[...]), w_ref[AW:D, :])
        h_ref[...] = h1
        n, _ = _rms(h1)
        u_ref[...] = _bf(n * g_ref[...])

    row = lambda wd: pl.BlockSpec((tm, wd), lambda i: (i, 0))
    return pl.pallas_call(
        body, name="out_proj", grid=(S // tm,),
        in_specs=[row(D), row(AW), row(MW), _cspec((D, D)), _cspec((1, D))],
        out_specs=[row(D), row(D)],
        out_shape=[jax.ShapeDtypeStruct((S, D), F32), jax.ShapeDtypeStruct((S, D), BF16)],
        compiler_params=_params(1),
    )(x, attn, ml, w, g)


HALF = DFF // NDEV // 2


def _mlp_fwd(h1, u2, w_up, w_down_a, w_down_b):
    tm = TM

    def body(h_ref, u_ref, wu_ref, wa_ref, wb_ref, a_ref, o_ref):
        u = u_ref[...]
        acc = h_ref[...]
        for c in range(NDEV):
            cols = slice(c * 512, (c + 1) * 512)
            a = _dot(u, wu_ref[c])
            a_ref[:, cols] = _bf(a)
            r = jnp.maximum(a, 0.0)
            r = _bf(r * r)
            acc = acc + _dot(r[:, 0:HALF], wa_ref[c]) + _dot(r[:, HALF:2 * HALF], wb_ref[c])
        o_ref[...] = acc

    row = lambda wd: pl.BlockSpec((tm, wd), lambda i: (i, 0))
    return pl.pallas_call(
        body, name="mlp_fwd", grid=(S // tm,),
        in_specs=[row(D), row(D), _cspec((NDEV, D, DFF // NDEV)), _cspec((NDEV, HALF, D)), _cspec((NDEV, HALF, D))],
        out_specs=[row(DFF), row(D)],
        out_shape=[jax.ShapeDtypeStruct((S, DFF), BF16), jax.ShapeDtypeStruct((S, D), F32)],
        compiler_params=_params(1),
    )(h1, u2, w_up, w_down_a, w_down_b)


def _ple_loss(h2, p, target, w_pg, w_ple, g_ple, g_fin):
    tm = TM

    def body(h_ref, p_ref, t_ref, wg_ref, wp_ref, gp_ref, gf_ref,
             dh_ref, dwg_ref, dwp_ref, dgp_ref, dgf_ref, loss_ref, acc_g, acc_p):
        i = pl.program_id(0)

        @pl.when(i == 0)
        def _():
            acc_g[...] = jnp.zeros_like(acc_g)
            acc_p[...] = jnp.zeros_like(acc_p)
            dgp_ref[...] = jnp.zeros_like(dgp_ref)
            dgf_ref[...] = jnp.zeros_like(dgf_ref)
            loss_ref[...] = jnp.zeros_like(loss_ref)

        h2v = h_ref[...]
        n2, rs2 = _rms(h2v)
        u3 = _bf(n2 * gp_ref[...])
        gt = _sigmoid(_dot(u3, wg_ref[...]))
        pb = _bf(p_ref[...])
        e = jnp.concatenate([_dot(pb, wp_ref[j]) for j in range(NDEV)], axis=1)
        h3 = h2v + gt * e
        n3, rs3 = _rms(h3)
        err = n3 * gf_ref[...] - t_ref[...]
        loss_ref[...] = loss_ref[...] + 0.5 / D * jnp.sum(jnp.sum(err * err, axis=1, keepdims=True), axis=0, keepdims=True)
        dy = err * (1.0 / D)
        dgf_ref[...] = dgf_ref[...] + jnp.sum(dy * n3, axis=0, keepdims=True)
        dh3 = _rms_bwd(dy, n3, rs3, gf_ref[...])
        de = _bf(dh3 * gt)
        dz = _bf(dh3 * e * gt * (1.0 - gt))
        acc_p[...] = acc_p[...] + _dot_tn(pb, de)
        acc_g[...] = acc_g[...] + _dot_tn(u3, dz)
        du3 = _dot_nt(dz, wg_ref[...])
        dgp_ref[...] = dgp_ref[...] + jnp.sum(du3 * n2, axis=0, keepdims=True)
        dh_ref[...] = dh3 + _rms_bwd(du3, n2, rs2, gp_ref[...])

        @pl.when(i == S // tm - 1)
        def _():
            dwg_ref[...] = _bf(acc_g[...])
            for j in range(NDEV):
                dwp_ref[j] = _bf(acc_p[:, j * 128:(j + 1) * 128])

    row = lambda wd: pl.BlockSpec((tm, wd), lambda i: (i, 0))
    whole = lambda shp: pl.BlockSpec(shp, lambda i: (0,) * len(shp))
    return pl.pallas_call(
        body, name="ple_loss", grid=(S // tm,),
        in_specs=[row(D), row(PLE), row(D), _cspec((D, D)), _cspec((NDEV, PLE, 128)), _cspec((1, D)), _cspec((1, D))],
        out_specs=[row(D), whole((D, D)), whole((NDEV, PLE, 128)), whole((1, D)), whole((1, D)), whole((1, 1))],
        out_shape=[jax.ShapeDtypeStruct((S, D), F32), jax.ShapeDtypeStruct((D, D), BF16),
                   jax.ShapeDtypeStruct((NDEV, PLE, 128), BF16), jax.ShapeDtypeStruct((1, D), F32),
                   jax.ShapeDtypeStruct((1, D), F32), jax.ShapeDtypeStruct((1, 1), F32)],
        scratch_shapes=[pltpu.VMEM((D, D), F32), pltpu.VMEM((PLE, D), F32)],
        compiler_params=_params(1),
    )(h2, p, target, w_pg, w_ple, g_ple, g_fin)


def _mlp_bwd(dh2, a, h1, g, w_up, w_down_a, w_down_b):
    tm = TM

    def body(d_ref, a_ref, h_ref, g_ref, wu_ref, wa_ref, wb_ref, da_ref, dh1_ref, dg_ref):
        @pl.when(pl.program_id(0) == 0)
        def _():
            dg_ref[...] = jnp.zeros_like(dg_ref)

        dh2v = d_ref[...]
        db = _bf(dh2v)
        du = jnp.zeros((tm, D), F32)
        for c in range(NDEV):
            cols = slice(c * 512, (c + 1) * 512)
            dr = jnp.concatenate([_dot_nt(db, wa_ref[c]), _dot_nt(db, wb_ref[c])], axis=1)
            da = _bf(dr * (2.0 * jnp.maximum(a_ref[:, cols], 0.0)))
            da_ref[:, cols] = da
            du = du + _dot_nt(da, wu_ref[c])
        n, rs = _rms(h_ref[...])
        dg_ref[...] = dg_ref[...] + jnp.sum(du * n, axis=0, keepdims=True)
        dh1_ref[...] = dh2v + _rms_bwd(du, n, rs, g_ref[...])

    row = lambda wd: pl.BlockSpec((tm, wd), lambda i: (i, 0))
    return pl.pallas_call(
        body, name="mlp_bwd", grid=(S // tm,),
        in_specs=[row(D), row(DFF), row(D), _cspec((1, D)), _cspec((NDEV, D, DFF // NDEV)), _cspec((NDEV, HALF, D)),
                  _cspec((NDEV, HALF, D))],
        out_specs=[row(DFF), row(D), pl.BlockSpec((1, D), lambda i: (0, 0))],
        out_shape=[jax.ShapeDtypeStruct((S, DFF), BF16), jax.ShapeDtypeStruct((S, D), F32),
                   jax.ShapeDtypeStruct((1, D), F32)],
        compiler_params=_params(1),
    )(dh2, a, h1, g, w_up, w_down_a, w_down_b)


def _out_proj_bwd(dh1, attn, ml, w):
    tm = TM

    def body(d_ref, a_ref, m_ref, w_ref, da_ref, dm_ref, dw_ref, acc):
        i = pl.program_id(0)

        @pl.when(i == 0)
        def _():
            acc[...] = jnp.zeros_like(acc)

        db = _bf(d_ref[...])
        dmix = _dot_nt(db, w_ref[...])
        da_ref[...] = dmix[:, 0:AW]
        dm_ref[...] = dmix[:, AW:D]
        acc[0:AW, :] = acc[0:AW, :] + _dot_tn(_bf(a_ref[...]), db)
        acc[AW:D, :] = acc[AW:D, :] + _dot_tn(_bf(m_ref[...]), db)

        @pl.when(i == S // tm - 1)
        def _():
            dw_ref[...] = _bf(acc[...])

    row = lambda wd: pl.BlockSpec((tm, wd), lambda i: (i, 0))
    return pl.pallas_call(
        body, name="out_proj_bwd", grid=(S // tm,),
        in_specs=[row(D), row(AW), row(MW), _cspec((D, D))],
        out_specs=[row(AW), row(MW), pl.BlockSpec((D, D), lambda i: (0, 0))],
        out_shape=[jax.ShapeDtypeStruct((S, AW), F32), jax.ShapeDtypeStruct((S, MW), F32),
                   jax.ShapeDtypeStruct((D, D), BF16)],
        scratch_shapes=[pltpu.VMEM((D, D), F32)],
        compiler_params=_params(1),
    )(dh1, attn, ml, w)


CHIP_FLIPS = [(0, 0), (0, 1), (1, 0), (1, 1)]


def _scatter2_phases(in_ref, out_ref, mine_v, sib_v, psum_v, loc_sems, d2d_send, d2d_recv, ici_send, ici_recv, own_sem):
    x, y, c = _place()
    chips = [((x + dx) % 2, (y + dy) % 2) for dx, dy in CHIP_FLIPS]
    nc = len(chips)

    def local(k):
        return pltpu.make_async_copy(in_ref.at[_dev_index(*chips[k], c)], mine_v.at[k], loc_sems.at[k])

    def to_sib(k):
        return pltpu.make_async_remote_copy(
            src_ref=in_ref.at[_dev_index(*chips[k], 1 - c)], dst_ref=sib_v.at[k], send_sem=d2d_send.at[k],
            recv_sem=d2d_recv.at[k], device_id=(x, y, 1 - c), device_id_type=MESH)

    def over_ici(k):
        return pltpu.make_async_remote_copy(
            src_ref=psum_v.at[k], dst_ref=out_ref.at[k], send_sem=ici_send.at[k - 1], recv_sem=ici_recv.at[k - 1],
            device_id=(*chips[k], c), device_id_type=MESH)

    def own():
        return pltpu.make_async_copy(psum_v.at[0], out_ref.at[0], own_sem)

    def start():
        for k in range(nc):
            to_sib(k).start()
            local(k).start()

    def middle():
        for k in (1, 2, 3, 0):
            local(k).wait()
            to_sib(k).wait_recv()
            psum_v[k] = _bf(mine_v[k].astype(F32) + sib_v[k].astype(F32))
            (over_ici(k) if k else own()).start()

    def finish():
        for k in range(1, nc):
            over_ici(k).wait()
        for k in range(nc):
            to_sib(k).wait_send()
        own().wait()

    return start, middle, finish


def _scatter2_scratch(shard, dtype):
    nc = len(CHIP_FLIPS)
    return ([pltpu.VMEM((nc, *shard), dtype)] * 3
            + [pltpu.SemaphoreType.DMA((nc,))] * 3 + [pltpu.SemaphoreType.DMA((nc - 1,))] * 2 + [pltpu.SemaphoreType.DMA])


def _in_proj_bwd(dparts, dh1, x, g1, w, part):
    tm = TM
    nt = S // tm
    widths = [d.shape[1] for d in dparts]
    assert sum(widths) == PW
    npar = len(dparts)

    def body(*refs):
        d_refs = refs[:npar]
        dh_ref, x_ref, g_ref, w_ref, in_ref, dx_ref, dg_ref, out_ref = refs[npar:npar + 8]
        rs_start, rs_middle, rs_finish = _scatter2_phases(in_ref, out_ref, *refs[npar + 8:])
        i = pl.program_id(0)
        pl.when(i == 0)(rs_start)
        pl.when(i == 1)(rs_middle)

        @pl.when(i == 0)
        def _():
            dg_ref[...] = jnp.zeros_like(dg_ref)

        du = jnp.zeros((tm, D), F32)
        off = 0
        for d_ref, wd in zip(d_refs, widths):
            nc = next(c for c in (768, 512) if wd % c == 0)
            for s in range(wd // nc):
                du = du + _dot_nt(d_ref[:, s * nc:(s + 1) * nc], w_ref[:, off + s * nc:off + (s + 1) * nc])
            off += wd
        n, rs = _rms(x_ref[...])
        dg_ref[...] = dg_ref[...] + jnp.sum(du * n, axis=0, keepdims=True)
        dx_ref[...] = dh_ref[...] + _rms_bwd(du, n, rs, g_ref[...])
        pl.when(i == nt - 1)(rs_finish)

    row = lambda wd: pl.BlockSpec((tm, wd), lambda i: (i, 0))
    shard = part.shape[1:]
    return pl.pallas_call(
        body, name="in_proj_bwd", grid=(nt,),
        in_specs=[row(wd) for wd in widths] + [row(D), row(D), _cspec((1, D)), _cspec((D, PW)), ANY],
        out_specs=[row(D), pl.BlockSpec((1, D), lambda i: (0, 0)), ANY],
        out_shape=[jax.ShapeDtypeStruct((S, D), F32), jax.ShapeDtypeStruct((1, D), F32),
                   jax.ShapeDtypeStruct((len(CHIP_FLIPS), *shard), part.dtype)],
        scratch_shapes=_scatter2_scratch(shard, part.dtype),
        compiler_params=_params(1),
    )(*dparts, dh1, x, g1, w, part)


SMALL_ROWS = 96


def _small_phases(ins, out_ref, pack, rbuf, send_sems, recv_sems):
    x, y, c = _place()
    me = _dev_index(x, y, c)

    def copies():
        out = []
        for k, (dx, dy, dc) in enumerate(FLIPS):
            peer = ((x + dx) % 2, (y + dy) % 2, (c + dc) % 2)
            out.append(pltpu.make_async_remote_copy(
                src_ref=pack, dst_ref=rbuf.at[me], send_sem=send_sems.at[k], recv_sem=recv_sems.at[k],
                device_id=peer, device_id_type=MESH))
        return out

    def start():
        pack[...] = jnp.zeros_like(pack)
        for i, ref in enumerate(ins):
            pack[8 * i:8 * i + 1, 0:ref.shape[1]] = ref[...]
        rbuf[me] = pack[...]
        for cp in copies():
            cp.start()

    def finish():
        for cp in copies():
            cp.wait()
        tot = rbuf[0]
        for j in range(1, NDEV):
            tot = tot + rbuf[j]
        out_ref[...] = tot

    return start, finish


def _wgrad(name, A, Bs, a_fn, b_fn, out_shape, split=None, ts=512, small=()):
    K = A.shape[1]
    widths = [b.shape[1] for b in Bs]
    N = sum(widths)
    nb, ns, nrt = len(Bs), len(small), S // ts
    kc = min(K, 1024)

    def body(*refs):
        a_ref, b_refs = refs[0], refs[1:1 + nb]
        o_ref = refs[1 + nb + ns]
        acc = refs[2 + nb + ns + bool(ns)]
        r = pl.program_id(0)
        if ns:
            sm_start, sm_finish = _small_phases(refs[1 + nb:1 + nb + ns], refs[2 + nb + ns], *refs[4 + nb + ns:])
            pl.when(r == 0)(sm_start)

        @pl.when(r == 0)
        def _():
            acc[...] = jnp.zeros_like(acc)

        bs, off = [], 0
        for b_ref, w in zip(b_refs, widths):
            nc = next(c for c in (1024, 768, 512) if w % c == 0)
            bs += [(off + c * nc, nc, b_fn(b_ref[:, c * nc:(c + 1) * nc])) for c in range(w // nc)]
            off += w
        for kk in range(K // kc):
            rows = slice(kk * kc, (kk + 1) * kc)
            at = a_fn(a_ref[:, rows]).T
            for lo, nc, b in bs:
                acc[rows, lo:lo + nc] = acc[rows, lo:lo + nc] + _dot(at, b)

        @pl.when(r == nrt - 1)
        def _():
            if split is None:
                o_ref[...] = _bf(acc[...])
            else:
                for j in range(NDEV):
                    o_ref[j] = _bf(acc[:, split * j:split * (j + 1)])

        if ns:
            pl.when(r == nrt - 1)(sm_finish)

    in_specs = [pl.BlockSpec((ts, K), lambda r: (r, 0))] + [pl.BlockSpec((ts, w), lambda r: (r, 0)) for w in widths]
    out_spec = pl.BlockSpec(out_shape, lambda r: (0,) * len(out_shape))
    scratch = [pltpu.VMEM((K, N), F32)]
    if not ns:
        return pl.pallas_call(
            body, name=name, grid=(nrt,), in_specs=in_specs, out_specs=out_spec,
            out_shape=jax.ShapeDtypeStruct(out_shape, BF16), scratch_shapes=scratch, compiler_params=_params(1),
        )(A, *Bs)
    return pl.pallas_call(
        body, name=name, grid=(nrt,), in_specs=in_specs + [VM] * ns, out_specs=[out_spec, VM],
        out_shape=[jax.ShapeDtypeStruct(out_shape, BF16), jax.ShapeDtypeStruct((SMALL_ROWS, 1024), F32)],
        scratch_shapes=scratch + [pltpu.VMEM((SMALL_ROWS, 1024), F32), pltpu.VMEM((NDEV, SMALL_ROWS, 1024), F32),
                                  pltpu.SemaphoreType.DMA((7,)), pltpu.SemaphoreType.DMA((7,))],
        compiler_params=_params(1),
    )(A, *Bs, *small)


def _relu2_bf(a):
    r = jnp.maximum(a.astype(F32), 0.0)
    return _bf(r * r)


def _ident(a):
    return a


def _step(x, p, target, g1, conv_b, gate_b, gn, g_mlp, g_ple, g_fin, sh):
    (g_in, g_conv), (rc, ra, rb) = _gather_weights([sh["w_in"], sh["conv_w"]], [BF16, F32])
    conv_w = g_conv.transpose(1, 0, 2).reshape(4, 1024)
    w_in_p = _join_w_in(g_in)
    (qkv, mqk, mv, mo, gates, u1), (w_out8, w_pg8, w_ple8) = _in_proj(
        x, g1, w_in_p, rc, ra, rb, [sh["w_out"], sh["w_ple_gate"], sh["w_ple"]], [BF16] * 3)
    attn, lse, (w_up8, w_down_a) = _attn_fwd(qkv, [sh["w_up"], sh["w_down"][0:HALF]], [BF16] * 2)
    ml, cs, ns, ms, (w_down_b,) = _mlstm_fwd(mqk, mv, mo, gates, conv_w, conv_b, gate_b, gn,
                                             [sh["w_down"][HALF:2 * HALF]], [BF16])
    w_out, w_pg = w_out8.reshape(D, D), w_pg8.reshape(D, D)
    h1, u2 = _out_proj(x, attn, ml, w_out, g_mlp)
    a, h2 = _mlp_fwd(h1, u2, w_up8, w_down_a, w_down_b)
    dh2, dw_pg, dw_ple8, dg_ple, dg_fin, loss = _ple_loss(h2, p, target, w_pg, w_ple8, g_ple, g_fin)
    da, dh1, dg_mlp = _mlp_bwd(dh2, a, h1, g_mlp, w_up8, w_down_a, w_down_b)
    dw_up8 = _wgrad("wgrad_up", u2, [da], _ident, _ident, (NDEV, D, DFF // NDEV), split=DFF // NDEV)
    dw_down = _wgrad("wgrad_down", a, [dh2], _relu2_bf, _bf, (DFF, D))
    d_attn, d_ml, dw_out = _out_proj_bwd(dh1, attn, ml, w_out)
    (dm, dconv_w, dconv_b, dgn, dgate_b), (r_out, r_pg, r_ple) = _mlstm_bwd(
        mqk, mv, mo, gates, conv_w, conv_b, gate_b, gn, cs, ns, ms, d_ml,
        [dw_out.reshape(NDEV, D // NDEV, D), dw_pg.reshape(NDEV, D // NDEV, D), dw_ple8])
    dq, dk, dv, (r_up, r_down) = _attn_bwd(qkv, attn, lse, d_attn, rc, ra, rb,
                                           [dw_up8, dw_down.reshape(NDEV, DFF // NDEV, D)])
    dparts = [dq, dk, dv, dm]
    small = [jnp.zeros((1, D), F32), dconv_b, dgate_b, dgn, dg_mlp, dg_ple, dg_fin, loss]
    dw_in8, total = _wgrad("wgrad_in", u1, dparts, _ident, _ident, (NDEV, D, IN_W // NDEV), split=IN_W // NDEV,
                           small=small + [dconv_w[j:j + 1] for j in range(4)])
    dx, dg1, r_in = _in_proj_bwd(dparts, dh1, x, g1, w_in_p, dw_in8)
    recv = dict(w_in=r_in, w_out=r_out, w_up=r_up, w_down=r_down, w_ple_gate=r_pg, w_ple=r_ple)
    return dx, recv, total, _allreduce_vec(dg1)


def _gather_weights(shards, dtypes):
    nw = len(shards)

    def body(*refs):
        ins, parts = refs[:nw], refs[nw:nw + 4]
        outs, tables = refs[nw + 4:2 * nw + 4], refs[2 * nw + 4:2 * nw + 7]
        start, forward, finish = _gather_phases(ins, outs, refs[2 * nw + 7:3 * nw + 7], *refs[3 * nw + 7:])
        start()
        _rope_fill(*parts, *tables)
        forward()
        finish()

    res = pl.pallas_call(
        body, name="gather_weights",
        in_specs=[VM] * (nw + 4), out_specs=[ANY] * nw + [VM] * 3,
        out_shape=_gather_shapes(shards, dtypes) + [jax.ShapeDtypeStruct((S, 128), F32)] * 3,
        scratch_shapes=_gather_scratch(shards, dtypes),
        compiler_params=_params(),
    )(*shards, *_rope_parts())
    return res[:nw], res[nw:]


def _allreduce_vec(v):
    def body(v_ref, out_ref, pack, rbuf, send_sems, recv_sems):
        start, finish = _small_phases([v_ref], out_ref, pack, rbuf, send_sems, recv_sems)
        start()
        finish()

    return pl.pallas_call(
        body, name="allreduce_last", out_shape=jax.ShapeDtypeStruct((8, 1024), F32),
        scratch_shapes=[pltpu.VMEM((8, 1024), F32), pltpu.VMEM((NDEV, 8, 1024), F32),
                        pltpu.SemaphoreType.DMA((7,)), pltpu.SemaphoreType.DMA((7,))],
        compiler_params=_params(),
    )(v)


def _adamw(name, gparts, w, m, v, tr):
    P, R, C = gparts.shape

    def body(g_ref, w_ref, m_ref, v_ref, go_ref, d_ref, mo_ref, vo_ref):
        g = g_ref[0].astype(F32)
        for j in range(1, P):
            g = g + g_ref[j].astype(F32)
        go_ref[...] = g
        d_ref[...], mo_ref[...], vo_ref[...] = _adam_update(g, w_ref[...], m_ref[...], v_ref[...])

    row = pl.BlockSpec((tr, C), lambda i: (i, 0))
    return pl.pallas_call(
        body, name=name, grid=(R // tr,),
        in_specs=[pl.BlockSpec((P, tr, C), lambda i: (0, i, 0)), row, row, row],
        out_specs=[row] * 4,
        out_shape=[jax.ShapeDtypeStruct((R, C), F32)] * 4,
        compiler_params=_params(1),
    )(gparts, w, m, v)


SMALL = ("norm_mix_g", "conv_b", "gate_b", "mlstm_norm_g", "norm_mlp_g", "norm_ple_g", "final_norm_g")


def _adam_update(g, w, m, v):
    c1 = 1.0 - ADAM_B1 ** ADAM_STEP
    c2 = 1.0 - ADAM_B2 ** ADAM_STEP
    m2 = ADAM_B1 * m + (1.0 - ADAM_B1) * g
    v2 = ADAM_B2 * v + (1.0 - ADAM_B2) * (g * g)
    return -ADAM_LR * ((m2 / c1) / (jnp.sqrt(v2 / c2) + ADAM_EPS) + ADAM_WD * w), m2, v2


def _adamw_small(total, first, ws, ms, vs):
    n = len(ws)

    def body(*refs):
        t_ref, f_ref = refs[:2]
        refs = refs[1:]
        outs = refs[1 + 3 * n:]
        for i in range(n):
            w_ref, m_ref, v_ref = refs[1 + i], refs[1 + n + i], refs[1 + 2 * n + i]
            g = (t_ref if i else f_ref)[8 * i:8 * i + 1, 0:w_ref.shape[1]]
            delta, m2, v2 = _adam_update(g, w_ref[...], m_ref[...], v_ref[...])
            for ref, val in zip(outs[4 * i:4 * i + 4], (g, delta, m2, v2)):
                ref[...] = val

    res = pl.pallas_call(
        body, name="adamw_small",
        out_shape=[jax.ShapeDtypeStruct(w.shape, F32) for w in ws for _ in range(4)],
        compiler_params=_params(),
    )(total, first, *ws, *ms, *vs)
    return [res[4 * i:4 * i + 4] for i in range(n)]


def kernel(x, p, norm_mix_g, w_in, conv_w, conv_b, gate_b, mlstm_norm_g, w_out, norm_mlp_g, w_up, w_down, norm_ple_g, w_ple_gate, w_ple, final_norm_g, loss_target, m_norm_mix_g, m_w_in, m_conv_w, m_conv_b, m_gate_b, m_mlstm_norm_g, m_w_out, m_norm_mlp_g, m_w_up, m_w_down, m_norm_ple_g, m_w_ple_gate, m_w_ple, m_final_norm_g, v_norm_mix_g, v_w_in, v_conv_w, v_conv_b, v_gate_b, v_mlstm_norm_g, v_w_out, v_norm_mlp_g, v_w_up, v_w_down, v_norm_ple_g, v_w_ple_gate, v_w_ple, v_final_norm_g):
    big_names = ("w_in", "conv_w", "w_out", "w_up", "w_down", "w_ple_gate", "w_ple")
    wts = dict(w_in=w_in, conv_w=conv_w, w_out=w_out, w_up=w_up, w_down=w_down, w_ple_gate=w_ple_gate, w_ple=w_ple)
    mom = dict(w_in=m_w_in, conv_w=m_conv_w, w_out=m_w_out, w_up=m_w_up, w_down=m_w_down, w_ple_gate=m_w_ple_gate,
               w_ple=m_w_ple)
    var = dict(w_in=v_w_in, conv_w=v_conv_w, w_out=v_w_out, w_up=v_w_up, w_down=v_w_down, w_ple_gate=v_w_ple_gate,
               w_ple=v_w_ple)
    sq = lambda a: a.reshape(a.shape[1:])
    fin = final_norm_g.reshape(1, D)
    dx, recv, total, first = _step(
        x[0], p[0, 0], loss_target[0], norm_mix_g, conv_b, jnp.pad(gate_b, ((0, 0), (0, 120))), mlstm_norm_g,
        norm_mlp_g, norm_ple_g, fin, {n: sq(wts[n]) for n in big_names})

    nrow = 8 * len(SMALL)
    me = _dev_index(*_place())
    conv_rows = total[nrow + 8:nrow + 40:8]
    recv["conv_w"] = lax.dynamic_slice_in_dim(conv_rows, me * 128, 128, axis=1).reshape(1, 4, 128)
    out = {}
    for n, tr in zip(big_names, (256, 4, 128, 256, 256, 128, 256)):
        res = _adamw("adamw_" + n, recv[n], sq(wts[n]), sq(mom[n]), sq(var[n]), tr)
        out[n] = [t.reshape(wts[n].shape) for t in res]
    sw = dict(norm_mix_g=norm_mix_g, conv_b=conv_b, gate_b=gate_b, mlstm_norm_g=mlstm_norm_g, norm_mlp_g=norm_mlp_g,
              norm_ple_g=norm_ple_g, final_norm_g=fin)
    sm = dict(norm_mix_g=m_norm_mix_g, conv_b=m_conv_b, gate_b=m_gate_b, mlstm_norm_g=m_mlstm_norm_g,
              norm_mlp_g=m_norm_mlp_g, norm_ple_g=m_norm_ple_g, final_norm_g=m_final_norm_g.reshape(1, D))
    sv = dict(norm_mix_g=v_norm_mix_g, conv_b=v_conv_b, gate_b=v_gate_b, mlstm_norm_g=v_mlstm_norm_g,
              norm_mlp_g=v_norm_mlp_g, norm_ple_g=v_norm_ple_g, final_norm_g=v_final_norm_g.reshape(1, D))
    res = _adamw_small(total, first, [sw[n] for n in SMALL], [sm[n] for n in SMALL], [sv[n] for n in SMALL])
    for n, r in zip(SMALL, res):
        out[n] = [t.reshape(final_norm_g.shape) for t in r] if n == "final_norm_g" else list(r)
    order = ("norm_mix_g", "w_in", "conv_w", "conv_b", "gate_b", "mlstm_norm_g", "w_out", "norm_mlp_g", "w_up", "w_down",
             "norm_ple_g", "w_ple_gate", "w_ple", "final_norm_g")
    loss_all = total[nrow, 0]
    return (loss_all, dx[None], *[out[n][0] for n in order], *[out[n][1] for n in order],
            *[out[n][2] for n in order], *[out[n][3] for n in order])
```

```python
import functools
import math

import jax
import jax.numpy as jnp
from jax import lax
from jax.experimental import pallas as pl
from jax.experimental.pallas import tpu as pltpu

F32, BF16 = jnp.float32, jnp.bfloat16
S = 4096
D = 1024
AW = 512
MW = 512
DFF = 4096
PLE = 256
IN_W = 3592
PW = 3840
NDEV = 8
EPS = 1e-6
NEG = -1e30
LC = 128
TB = 256
ROPE_THETA = 500000.0
VMEM_LIMIT = 56 * 1024 * 1024
HI = lax.Precision.HIGHEST

ADAM_LR, ADAM_B1, ADAM_B2, ADAM_EPS, ADAM_WD, ADAM_STEP = 0.001, 0.9, 0.999, 1e-08, 0.01, 10


def _params(n_grid=0, **kw):
    sem = dict(dimension_semantics=("arbitrary",) * n_grid) if n_grid else {}
    return pltpu.CompilerParams(vmem_limit_bytes=VMEM_LIMIT, **sem, **kw)


def _cspec(shape):
    nd = len(shape)
    return pl.BlockSpec(shape, lambda *_: (0,) * nd, pipeline_mode=pl.Buffered(1))


def _dot(a, b):
    return jnp.dot(a, b, preferred_element_type=F32)


def _dot_nt(a, b):
    return lax.dot_general(a, b, (((1,), (1,)), ((), ())), preferred_element_type=F32)


def _dot_tn(a, b):
    return lax.dot_general(a, b, (((0,), (0,)), ((), ())), preferred_element_type=F32)


def _bf(x):
    return x.astype(BF16)


def _rms(x):
    rs = lax.rsqrt(jnp.mean(x * x, axis=-1, keepdims=True) + EPS)
    return x * rs, rs


def _rms_bwd(du, n, rs, g):
    dn = du * g
    return rs * (dn - n * jnp.mean(dn * n, axis=-1, keepdims=True))


def _sigmoid(x):
    return 1.0 / (1.0 + jnp.exp(-x))


ROPE_BLK = 512


def _rope_parts():
    def cs(n, step):
        j = lax.broadcasted_iota(jnp.int32, (n, 128), 1) % 64
        pos = (lax.broadcasted_iota(jnp.int32, (n, 128), 0) * step).astype(F32)
        ang = pos * jnp.power(ROPE_THETA, -(j % 8).astype(F32) / 8.0)
        return jnp.cos(ang), jnp.sin(ang)

    return (*cs(ROPE_BLK, 1), *cs(S // ROPE_BLK, ROPE_BLK))


def _rope_fill(co_ref, so_ref, cb_ref, sb_ref, rc_ref, ra_ref, rb_ref):
    j = lax.broadcasted_iota(jnp.int32, (ROPE_BLK, 128), 1) % 64
    co, so = co_ref[...], so_ref[...]
    for t in range(S // ROPE_BLK):
        cb, sb = cb_ref[t:t + 1, :], sb_ref[t:t + 1, :]
        cos, sin = cb * co - sb * so, sb * co + cb * so
        rows = slice(t * ROPE_BLK, (t + 1) * ROPE_BLK)
        rc_ref[rows, :] = jnp.where(j < 16, cos, 1.0)
        ra_ref[rows, :] = jnp.where(j < 8, -sin, 0.0)
        rb_ref[rows, :] = jnp.where((j >= 8) & (j < 16), sin, 0.0)


def _rope(blk, c, a, b):
    return blk * c + pltpu.roll(blk, 120, 1) * a + pltpu.roll(blk, 8, 1) * b


def _rope_bwd(d, c, a, b):
    return d * c + pltpu.roll(d * a, 8, 1) + pltpu.roll(d * b, 120, 1)


def _unrope(t, c, a, b):
    return jnp.concatenate([_bf(_rope_bwd(t[:, j * 128:(j + 1) * 128].astype(F32), c, a, b))
                            for j in range(t.shape[1] // 128)], axis=1)


MESH = pl.DeviceIdType.MESH
ANY = pl.BlockSpec(memory_space=pl.ANY)
VM = pl.BlockSpec(memory_space=pltpu.VMEM)
FLIPS = [(dx, dy, dc) for dx in (0, 1) for dy in (0, 1) for dc in (0, 1)][1:]


def _place():
    return lax.axis_index("x"), lax.axis_index("y"), lax.axis_index("c")


def _dev_index(px, py, pc):
    return 4 * px + 2 * py + pc


def _gather_phases(ins, outs, bufs, send_sems=None, recv_sems=None, local_sems=None):
    nw = len(ins)
    if nw == 0:
        return (lambda: None,) * 3
    x, y, c = _place()
    me, sib = (x, y, c), (x, y, 1 - c)
    chips = [(1 - x, y), (x, 1 - y), (1 - x, 1 - y)]

    def copy(w, k, block, to, from_buf=False):
        dst = outs[w].at[_dev_index(*block)]
        return pltpu.make_async_remote_copy(
            src_ref=bufs[w] if from_buf else dst, dst_ref=dst, send_sem=send_sems.at[w, k],
            recv_sem=recv_sems.at[w, k], device_id=to, device_id_type=MESH)

    def mine(w):
        return pltpu.make_async_copy(bufs[w], outs[w].at[_dev_index(*me)], local_sems.at[w])

    def first(w):
        return [copy(w, 0, me, sib, True)] + [copy(w, 1 + j, me, (*chip, c), True) for j, chip in enumerate(chips)]

    def passed(w):
        return [copy(w, 4 + j, (*chip, c), sib) for j, chip in enumerate(chips)]

    def start():
        for w in range(nw):
            bufs[w][...] = ins[w][...].astype(bufs[w].dtype)
        for w in range(nw):
            mine(w).start()
            for cp in first(w):
                cp.start()

    def forward():
        for j, chip in enumerate(chips):
            for w in range(nw):
                copy(w, 1 + j, (*chip, c), me).wait_recv()
                passed(w)[j].start()

    def finish():
        for w in range(nw):
            copy(w, 0, sib, me).wait_recv()
        for j, chip in enumerate(chips):
            for w in range(nw):
                copy(w, 4 + j, (*chip, 1 - c), me).wait_recv()
        for w in range(nw):
            for cp in first(w) + passed(w):
                cp.wait_send()
            mine(w).wait()

    return start, forward, finish


def _gather_scratch(shards, dtypes):
    nw = len(shards)
    if nw == 0:
        return []
    return ([pltpu.VMEM(s.shape, dt) for s, dt in zip(shards, dtypes)]
            + [pltpu.SemaphoreType.DMA((nw, 7)), pltpu.SemaphoreType.DMA((nw, 7)), pltpu.SemaphoreType.DMA((nw,))])


def _gather_shapes(shards, dtypes):
    return [jax.ShapeDtypeStruct((NDEV, *s.shape), dt) for s, dt in zip(shards, dtypes)]


def _scatter_phases(ins, outs, send_sems=None, recv_sems=None, local_sems=None):
    nw = len(ins)
    if nw == 0:
        return (lambda: None,) * 2
    x, y, c = _place()
    me = _dev_index(x, y, c)

    def copies():
        out = []
        for w in range(nw):
            out.append(pltpu.make_async_copy(ins[w].at[me], outs[w].at[me], local_sems.at[w]))
            for k, (dx, dy, dc) in enumerate(FLIPS):
                peer = ((x + dx) % 2, (y + dy) % 2, (c + dc) % 2)
                out.append(pltpu.make_async_remote_copy(
                    src_ref=ins[w].at[_dev_index(*peer)], dst_ref=outs[w].at[me], send_sem=send_sems.at[w, k],
                    recv_sem=recv_sems.at[w, k], device_id=peer, device_id_type=MESH))
        return out

    def start():
        for cp in copies():
            cp.start()

    def finish():
        for cp in copies():
            cp.wait()

    return start, finish


def _scatter_scratch(nw):
    if nw == 0:
        return []
    return [pltpu.SemaphoreType.DMA((nw, 7)), pltpu.SemaphoreType.DMA((nw, 7)), pltpu.SemaphoreType.DMA((nw,))]


TM = 512


def _join_w_in(wg):
    sw = IN_W // NDEV

    def body(wg_ref, w_ref):
        for j in range(NDEV):
            w_ref[:, sw * j:sw * (j + 1)] = wg_ref[j]
        w_ref[:, IN_W:PW] = jnp.zeros((D, PW - IN_W), BF16)

    return pl.pallas_call(body, name="join_w_in", out_shape=jax.ShapeDtypeStruct((D, PW), BF16),
                          compiler_params=_params())(wg)


def _in_proj(x, g1, w, rc, ra, rb, shards, dtypes):
    tm = TM
    nw = len(shards)
    nt = S // tm

    def body(*refs):
        x_ref, g_ref, w_ref, rc_ref, ra_ref, rb_ref = refs[:6]
        ins = refs[6:6 + nw]
        qkv_ref, mqk_ref, mv_ref, mo_ref, gt_ref, u_ref = refs[6 + nw:12 + nw]
        outs = refs[12 + nw:12 + 2 * nw]
        bufs = refs[12 + 2 * nw:12 + 3 * nw]
        ag_start, ag_forward, ag_finish = _gather_phases(ins, outs, bufs, *refs[12 + 3 * nw:])
        i = pl.program_id(0)
        pl.when(i == 0)(ag_start)
        pl.when(i == nt - 2)(ag_forward)
        n, _ = _rms(x_ref[...])
        u = _bf(n * g_ref[...])
        u_ref[...] = u
        c, a, b = rc_ref[...], ra_ref[...], rb_ref[...]
        for half in range(2):
            blk = _dot(u, w_ref[:, half * 512:(half + 1) * 512])
            for t in range(4):
                lo = half * 512 + t * 128
                qkv_ref[:, lo:lo + 128] = _rope(blk[:, t * 128:(t + 1) * 128], c, a, b)
        qkv_ref[:, 1024:1536] = _dot(u, w_ref[:, 1024:1536])
        mqk_ref[:, 0:512] = _dot(u, w_ref[:, 1536:2048])
        mqk_ref[:, 512:1024] = _dot(u, w_ref[:, 2048:2560])
        mv_ref[...] = _dot(u, w_ref[:, 2560:3072])
        mo_ref[...] = _dot(u, w_ref[:, 3072:3584])
        gt_ref[...] = _dot(u, w_ref[:, 3584:3712])
        pl.when(i == nt - 1)(ag_finish)

    row = lambda wd: pl.BlockSpec((tm, wd), lambda i: (i, 0))
    res = pl.pallas_call(
        body, name="in_proj", grid=(nt,),
        in_specs=[row(D), _cspec((1, D)), _cspec((D, PW)), row(128), row(128), row(128)] + [VM] * nw,
        out_specs=[row(1536), row(1024), row(512), row(512), row(128), row(D)] + [ANY] * nw,
        out_shape=[jax.ShapeDtypeStruct((S, 1536), F32), jax.ShapeDtypeStruct((S, 1024), F32),
                   jax.ShapeDtypeStruct((S, 512), F32), jax.ShapeDtypeStruct((S, 512), F32),
                   jax.ShapeDtypeStruct((S, 128), F32), jax.ShapeDtypeStruct((S, D), BF16)]
        + _gather_shapes(shards, dtypes),
        scratch_shapes=_gather_scratch(shards, dtypes),
        compiler_params=_params(1),
    )(x, g1, w, rc, ra, rb, *shards)
    return res[:6], res[6:]


DILATIONS = (16, 4, 1)


def _attn_valid(n):
    kd = lax.broadcasted_iota(jnp.int32, (128, 256), 1) - lax.broadcasted_iota(jnp.int32, (128, 256), 0)
    off = jnp.where(n == 0, 0, 128)
    return (kd <= off) & (kd >= off - 128)


def _attn_rows(d, r, n):
    if d == 1:
        q0 = pl.multiple_of(n * 128, 128)
        k0 = pl.multiple_of(jnp.maximum(n - 1, 0) * 128, 128)
        return pl.ds(q0, 128), pl.ds(k0, 256), _attn_valid(n)
    q0 = r + n * 128 * d
    k0 = r + jnp.maximum(n - 1, 0) * 128 * d
    return pl.ds(q0, 128, stride=d), pl.ds(k0, 256, stride=d), _attn_valid(n)


ATTN_GROUP = 4
ATTN_ITERS = S // 128 // ATTN_GROUP


def _attn_group(d, i):
    nb = S // (128 * d)
    if nb == 2:
        qi = lax.broadcasted_iota(jnp.int32, (256, 256), 0) - lax.broadcasted_iota(jnp.int32, (256, 256), 1)
        whole = [pl.ds((ATTN_GROUP // 2) * i + u, 256, stride=d) for u in range(ATTN_GROUP // 2)]
        return [(rows, rows, (qi >= 0) & (qi <= 128)) for rows in whole]
    if d == 1:
        return [_attn_rows(1, 0, i + ATTN_ITERS * u) for u in range(ATTN_GROUP)]
    return [_attn_rows(d, (i // nb) * ATTN_GROUP + u, i % nb) for u in range(ATTN_GROUP)]


def _head0(shape):
    return lax.broadcasted_iota(jnp.int32, shape, 1) < 64


def _stack_heads(t):
    h0 = _head0(t.shape)
    tb = _bf(t)
    zero = jnp.zeros_like(tb)
    return jnp.concatenate([jnp.where(h0, tb, zero), jnp.where(h0, zero, tb)], axis=0)


def _attn_fwd(qkv, shards, dtypes):
    nw = len(shards)

    def body(*refs):
        q_ref, k_ref, v_ref = refs[:3]
        ins = refs[3:3 + nw]
        o_ref, lse_ref = refs[3 + nw:5 + nw]
        outs = refs[5 + nw:5 + 2 * nw]
        m0, m1, l0, l1, acc = refs[5 + 2 * nw:10 + 2 * nw]
        bufs = refs[10 + 2 * nw:10 + 3 * nw]
        ag_start, ag_forward, ag_finish = _gather_phases(ins, outs, bufs, *refs[10 + 3 * nw:])
        hp = pl.program_id(0)
        pl.when(hp == 0)(ag_start)
        pl.when(hp == 3)(ag_forward)
        stats = (m0, m1, l0, l1, acc)

        def update(blocks, first):
            loaded = [([q_ref[rq, :], k_ref[rk, :], v_ref[rk, :]], None if first else [ref[rq, :] for ref in stats])
                      for rq, rk, _ in blocks]
            results = []
            for ((q, k, v), prev), (_, _, valid) in zip(loaded, blocks):
                head0 = _head0(q.shape)
                kb, vb = _bf(k), _bf(v)
                q = q * 0.125
                m_new, l_new, acc_new = [], [], []
                for a, qa in enumerate((_bf(jnp.where(head0, q, 0.0)), _bf(jnp.where(head0, 0.0, q)))):
                    s = jnp.where(valid, _dot_nt(qa, kb), NEG)
                    mc = jnp.max(s, axis=-1, keepdims=True)
                    m_a = jnp.broadcast_to(mc, q.shape) if first else jnp.maximum(prev[a], mc)
                    p = jnp.exp(s - jnp.tile(m_a, (1, 2)))
                    l_add = jnp.sum(p, axis=-1, keepdims=True)
                    pv = _dot(_bf(p), vb)
                    if first:
                        l_a = jnp.broadcast_to(l_add, q.shape)
                    else:
                        alpha = jnp.exp(prev[a] - m_a)
                        l_a, pv = alpha * prev[2 + a] + l_add, alpha * prev[4] + pv
                    m_new.append(m_a), l_new.append(l_a), acc_new.append(pv)
                results.append((m_new[0], m_new[1], l_new[0], l_new[1], jnp.where(head0, acc_new[0], acc_new[1])))
            for (rq, _, _), res in zip(blocks, results):
                for ref, val in zip(stats, res):
                    ref[rq, :] = val

        for d in DILATIONS:
            def step(i, carry, d=d):
                update(_attn_group(d, i), d == DILATIONS[0])
                return carry

            lax.fori_loop(0, ATTN_ITERS, step, 0)

        def fin(t, carry):
            rows = pl.ds(pl.multiple_of(t * 256, 256), 256)
            h0 = lax.broadcasted_iota(jnp.int32, (256, 128), 1) < 64
            l = jnp.where(h0, l0[rows, :], l1[rows, :])
            o_ref[rows, :] = acc[rows, :] / l
            lse_ref[rows, :] = jnp.where(h0, m0[rows, :], m1[rows, :]) + jnp.log(l)
            return carry

        lax.fori_loop(0, S // 256, fin, 0)
        pl.when(hp == 3)(ag_finish)

    col = lambda off: pl.BlockSpec((S, 128), lambda h, off=off: (0, off + h))
    res = pl.pallas_call(
        body, name="attn_fwd", grid=(4,),
        in_specs=[col(0), col(4), col(8)] + [VM] * nw,
        out_specs=[col(0), col(0)] + [ANY] * nw,
        out_shape=[jax.ShapeDtypeStruct((S, AW), F32), jax.ShapeDtypeStruct((S, AW), F32)]
        + _gather_shapes(shards, dtypes),
        scratch_shapes=[pltpu.VMEM((S, 128), F32)] * 5 + _gather_scratch(shards, dtypes),
        compiler_params=_params(1),
    )(qkv, qkv, qkv, *shards)
    return res[0], res[1], res[2:]


def _attn_bwd(qkv, o, lse, do, parts):
    nw = len(parts)

    def body(*refs):
        q_ref, k_ref, v_ref, o_ref, lse_ref, do_ref = refs[:6]
        ins = refs[6:6 + nw]
        dq_out, dk_out, dv_out = refs[6 + nw:9 + nw]
        outs = refs[9 + nw:9 + 2 * nw]
        L0, L1, D0, D1, dq_ref, dk_ref, dv_ref = refs[9 + 2 * nw:16 + 2 * nw]
        rs_start, rs_finish = _scatter_phases(ins, outs, *refs[16 + 2 * nw:])
        hp = pl.program_id(0)
        pl.when(hp == 0)(rs_start)

        def pre(t, carry):
            rows = pl.ds(pl.multiple_of(t * 256, 256), 256)
            h0 = lax.broadcasted_iota(jnp.int32, (256, 128), 1) < 64
            ls = lse_ref[rows, :]
            dd = do_ref[rows, :] * o_ref[rows, :]
            shp = (256, 128)
            L0[rows, :] = jnp.broadcast_to(jnp.max(jnp.where(h0, ls, NEG), axis=-1, keepdims=True), shp)
            L1[rows, :] = jnp.broadcast_to(jnp.max(jnp.where(h0, NEG, ls), axis=-1, keepdims=True), shp)
            D0[rows, :] = jnp.broadcast_to(jnp.sum(jnp.where(h0, dd, 0.0), axis=-1, keepdims=True), shp)
            D1[rows, :] = jnp.broadcast_to(jnp.sum(jnp.where(h0, 0.0, dd), axis=-1, keepdims=True), shp)
            return carry

        lax.fori_loop(0, S // 256, pre, 0)

        def update(blocks, first):
            loaded = [([q_ref[rq, :], k_ref[rk, :], v_ref[rk, :], do_ref[rq, :]],
                       [L0[rq, :], L1[rq, :], D0[rq, :], D1[rq, :]],
                       [0.0] * 3 if first else [dq_ref[rq, :], dk_ref[rk, :], dv_ref[rk, :]]) for rq, rk, _ in blocks]
            results = []
            for ((q, k, v, dout), (l0v, l1v, d0v, d1v), (dq, dk, dv)), (_, _, valid) in zip(loaded, blocks):
                valid = jnp.tile(valid, (1, 2))
                kst, vst = _stack_heads(k), _stack_heads(v)
                hk = _head0((256, 128))
                dob = _bf(dout)
                cat = lambda a, b: jnp.concatenate([jnp.tile(a, (1, 2)), jnp.tile(b, (1, 2))], axis=1)
                s = jnp.where(valid, _dot_nt(_bf(q * 0.125), kst), NEG)
                p = jnp.exp(s - cat(l0v, l1v))
                ds = _bf(p * (_dot_nt(dob, vst) - cat(d0v, d1v)) * 0.125)
                dk2 = _dot_tn(ds, _bf(q))
                dv2 = _dot_tn(_bf(p), dob)
                results.append((dq + _dot(ds, kst), dk + jnp.where(hk, dk2[0:256], dk2[256:512]),
                                dv + jnp.where(hk, dv2[0:256], dv2[256:512])))
            for (rq, rk, _), (dq, dk, dv) in zip(blocks, results):
                dq_ref[rq, :] = dq
                dk_ref[rk, :] = dk
                dv_ref[rk, :] = dv

        assert S // (128 * DILATIONS[0]) == 2
        for d in DILATIONS:
            def step(i, carry, d=d):
                update(_attn_group(d, i), d == DILATIONS[0])
                return carry

            lax.fori_loop(0, ATTN_ITERS, step, 0)

        def fin(t, carry):
            rows = pl.ds(pl.multiple_of(t * 256, 256), 256)
            for src, dst in ((dq_ref, dq_out), (dk_ref, dk_out), (dv_ref, dv_out)):
                dst[rows, :] = _bf(src[rows, :])
            return carry

        lax.fori_loop(0, S // 256, fin, 0)
        pl.when(hp == 3)(rs_finish)

    col = lambda off: pl.BlockSpec((S, 128), lambda h, off=off: (0, off + h))
    res = pl.pallas_call(
        body, name="attn_bwd", grid=(4,),
        in_specs=[col(0), col(4), col(8), col(0), col(0), col(0)] + [ANY] * nw,
        out_specs=[col(0), col(0), col(0)] + [ANY] * nw,
        out_shape=[jax.ShapeDtypeStruct((S, AW), BF16)] * 3 + [jax.ShapeDtypeStruct(a.shape, a.dtype) for a in parts],
        scratch_shapes=[pltpu.VMEM((S, 128), F32)] * 7 + _scatter_scratch(nw),
        compiler_params=_params(1),
    )(qkv, qkv, qkv, o, lse, do, *parts)
    return res[0], res[1], res[2], res[3:]


def _logsig(x):
    return jnp.minimum(x, 0.0) - jnp.log1p(jnp.exp(-jnp.abs(x)))


def _conv_taps(xp, n):
    return [xp[8:] if j == 3 else pltpu.roll(xp, 3 - j, 0)[8:] for j in range(4)]


def _conv_silu(xp, w_ref, b_ref, n):
    taps = _conv_taps(xp, n)
    c = b_ref[...] + sum(w_ref[j:j + 1, :] * taps[j] for j in range(4))
    sg = _sigmoid(c)
    return c, sg, taps


def _chunk_gates(G):
    assert LC == 128
    r = lax.broadcasted_iota(jnp.int32, (LC, LC), 0)
    c = lax.broadcasted_iota(jnp.int32, (LC, LC), 1)
    tril = (c <= r).astype(F32)
    triu = (c >= r).astype(F32)
    b_col = jnp.dot(tril, _logsig(G), preferred_element_type=F32, precision=HI)
    return b_col, b_col.T, G.T, tril, triu


def _colpick(X, lane):
    li = lax.broadcasted_iota(jnp.int32, X.shape, 1)
    return jnp.sum(jnp.where(li == lane, X, 0.0), axis=1, keepdims=True)


def _rowpick(XT, row):
    ri = lax.broadcasted_iota(jnp.int32, XT.shape, 0)
    return jnp.sum(jnp.where(ri == row, XT, 0.0), axis=0, keepdims=True)


def _mlstm_head(qh, kh, vh, G, b_col, b_row, g_row, h, Ch, nh, m_prev):
    bt = _colpick(b_col, 4 + h)
    i_col = _colpick(G, h)
    bs = _rowpick(b_row, 4 + h)
    i_row = _rowpick(g_row, h)
    r = lax.broadcasted_iota(jnp.int32, (LC, LC), 0)
    c = lax.broadcasted_iota(jnp.int32, (LC, LC), 1)
    log_d = jnp.where(c <= r, bt - bs + i_row, NEG)
    log_inter = bt + m_prev
    m_t = jnp.maximum(log_inter, jnp.max(log_d, axis=1, keepdims=True))
    Dm = jnp.exp(log_d - m_t)
    g = jnp.exp(log_inter - m_t)
    qb, kb, vb = _bf(qh), _bf(kh), _bf(vh)
    Am = _dot_nt(qb, kb) * Dm
    qC = _dot(qb, _bf(Ch))
    num = g * qC + _dot(_bf(Am), vb)
    qn = jnp.sum(qh * nh, axis=1, keepdims=True)
    den = g * qn + jnp.sum(Am, axis=1, keepdims=True)
    floor = jnp.exp(-m_t)
    dd = jnp.maximum(jnp.abs(den), floor)
    inv_dd = 1.0 / dd
    hh = num * inv_dd
    lane = lax.broadcasted_iota(jnp.int32, (1, LC), 1)
    blast = jnp.sum(jnp.where(lane == LC - 1, bs, 0.0), axis=1, keepdims=True)
    log_s = blast - bt + i_col
    m_new = jnp.maximum(blast + m_prev, jnp.max(log_s, axis=0, keepdims=True))
    decay = jnp.exp(blast + m_prev - m_new)
    ws = jnp.exp(log_s - m_new)
    kw = kh * ws
    C_new = decay * Ch + _dot_tn(_bf(kw), vb)
    n_new = decay * nh + jnp.sum(kw, axis=0, keepdims=True)
    return dict(Dm=Dm, g=g, Am=Am, qC=qC, qn=qn, den=den, floor=floor, inv_dd=inv_dd, h=hh, decay=decay, ws=ws, kw=kw,
                C_new=C_new, n_new=n_new, m_new=m_new, qb=qb, kb=kb, vb=vb)


def _head_out(hh, mo_h, gn_h):
    r = lax.rsqrt(jnp.mean(hh * hh, axis=-1, keepdims=True) + EPS)
    hn = hh * r
    sg = _sigmoid(mo_h)
    return sg * (hn * gn_h), hn, r, sg


def _mlstm_fwd(mqk, mv, mo, gates, conv_w, conv_b, gate_b, gn, shards, dtypes):
    nblk = S // TB
    ncb = TB // LC
    nw = len(shards)

    def body(*refs):
        x_ref, v_ref, o_ref, g_ref, w_ref, b_ref, gb_ref, gn_ref = refs[:8]
        ins = refs[8:8 + nw]
        out_ref, cs_ref, ns_ref, ms_ref = refs[8 + nw:12 + nw]
        outs = refs[12 + nw:12 + 2 * nw]
        tail, Cst, nst, mst, qs, ks = refs[12 + 2 * nw:18 + 2 * nw]
        bufs = refs[18 + 2 * nw:18 + 3 * nw]
        ag_start, ag_forward, ag_finish = _gather_phases(ins, outs, bufs, *refs[18 + 3 * nw:])
        i = pl.program_id(0)
        pl.when(i == 0)(ag_start)
        pl.when(i == nblk // 2)(ag_forward)

        @pl.when(i == 0)
        def _():
            tail[...] = jnp.zeros_like(tail)
            Cst[...] = jnp.zeros_like(Cst)
            nst[...] = jnp.zeros_like(nst)
            mst[...] = jnp.zeros_like(mst)

        x = x_ref[...]
        xp = jnp.concatenate([tail[...], x], axis=0)
        tail[...] = x[TB - 8:TB, :]
        c, sg, _ = _conv_silu(xp, w_ref, b_ref, TB)
        y = c * sg
        qs[...] = y[:, 0:MW]
        ks[...] = y[:, MW:2 * MW] * (1.0 / math.sqrt(128.0))

        for cc in range(ncb):
            rows = slice(cc * LC, (cc + 1) * LC)
            G = g_ref[rows, :] + gb_ref[...]
            b_col, b_row, g_row, _, _ = _chunk_gates(G)
            cs_ref[cc] = Cst[...]
            ns_ref[cc] = nst[...]
            ms_ref[cc] = mst[...]
            for h in range(4):
                ln = slice(h * 128, (h + 1) * 128)
                m_prev = jnp.max(mst[0:1, ln], axis=1, keepdims=True)
                f = _mlstm_head(qs[rows, ln], ks[rows, ln], v_ref[rows, ln], G, b_col, b_row, g_row, h,
                                Cst[:, ln], nst[0:1, ln], m_prev)
                out, _, _, _ = _head_out(f["h"], o_ref[rows, ln], gn_ref[:, ln])
                out_ref[rows, ln] = out
                Cst[:, ln] = f["C_new"]
                nst[0:1, ln] = f["n_new"]
                mst[0:1, ln] = jnp.broadcast_to(f["m_new"], (1, 128))
        pl.when(i == nblk - 1)(ag_finish)

    row = lambda wd: pl.BlockSpec((TB, wd), lambda i: (i, 0))
    res = pl.pallas_call(
        body, name="mlstm_fwd", grid=(nblk,),
        in_specs=[row(1024), row(MW), row(MW), row(128), _cspec((4, 1024)), _cspec((1, 1024)), _cspec((1, 128)),
                  _cspec((1, MW))] + [VM] * nw,
        out_specs=[row(MW), pl.BlockSpec((ncb, 128, MW), lambda i: (i, 0, 0)),
                   pl.BlockSpec((ncb, 8, MW), lambda i: (i, 0, 0)), pl.BlockSpec((ncb, 8, MW), lambda i: (i, 0, 0))]
        + [ANY] * nw,
        out_shape=[jax.ShapeDtypeStruct((S, MW), F32), jax.ShapeDtypeStruct((S // LC, 128, MW), F32),
                   jax.ShapeDtypeStruct((S // LC, 8, MW), F32), jax.ShapeDtypeStruct((S // LC, 8, MW), F32)]
        + _gather_shapes(shards, dtypes),
        scratch_shapes=[pltpu.VMEM((8, 1024), F32), pltpu.VMEM((128, MW), F32), pltpu.VMEM((8, MW), F32),
                        pltpu.VMEM((8, MW), F32), pltpu.VMEM((TB, MW), F32), pltpu.VMEM((TB, MW), F32)]
        + _gather_scratch(shards, dtypes),
        compiler_params=_params(1),
    )(mqk, mv, mo, gates, conv_w, conv_b, gate_b, gn, *shards)
    return res[0], res[1], res[2], res[3], res[4:]


DM_V, DM_O, DM_G, DM_W = 1024, 1536, 2048, PW - 3 * AW


def _mlstm_bwd(mqk, mv, mo, gates, conv_w, conv_b, gate_b, gn, cs, ns, ms, dout, parts):
    nblk = S // TB
    ncb = TB // LC
    kscale = 1.0 / math.sqrt(128.0)
    nw = len(parts)

    def body(*refs):
        x_ref, xprev_ref, v_ref, o_ref, g_ref, w_ref, b_ref, gb_ref, gn_ref, cs_ref, ns_ref, ms_ref, do_ref = refs[:13]
        ins = refs[13:13 + nw]
        dm_ref, dw_ref, db_ref, dgn_ref, dgb_ref = refs[13 + nw:18 + nw]
        outs = refs[18 + nw:18 + 2 * nw]
        dCst, dnst, dyhead, qs, ks, dqk = refs[18 + 2 * nw:24 + 2 * nw]
        rs_start, rs_finish = _scatter_phases(ins, outs, *refs[24 + 2 * nw:])
        i = pl.program_id(0)
        blk = nblk - 1 - i
        pl.when(i == 0)(rs_start)

        @pl.when(i == 0)
        def _():
            dCst[...] = jnp.zeros_like(dCst)
            dnst[...] = jnp.zeros_like(dnst)
            dyhead[...] = jnp.zeros_like(dyhead)
            dw_ref[...] = jnp.zeros_like(dw_ref)
            db_ref[...] = jnp.zeros_like(db_ref)
            dgn_ref[...] = jnp.zeros_like(dgn_ref)
            dgb_ref[...] = jnp.zeros_like(dgb_ref)

        x = x_ref[...]
        xprev = jnp.where(blk == 0, 0.0, xprev_ref[...])
        xp = jnp.concatenate([xprev, x], axis=0)
        c, sg, taps = _conv_silu(xp, w_ref, b_ref, TB)
        y = c * sg
        qs[...] = y[:, 0:MW]
        ks[...] = y[:, MW:2 * MW] * kscale
        lane128 = lax.broadcasted_iota(jnp.int32, (LC, 128), 1)
        rowi = lax.broadcasted_iota(jnp.int32, (LC, 1), 0)
        ones = jnp.ones((LC, 128), F32)

        for cc in reversed(range(ncb)):
            rows = slice(cc * LC, (cc + 1) * LC)
            G = g_ref[rows, :] + gb_ref[...]
            b_col, b_row, g_row, _, triu = _chunk_gates(G)
            dB = jnp.zeros((LC, 128), F32)
            dI = jnp.zeros((LC, 128), F32)
            for h in range(4):
                ln = slice(h * 128, (h + 1) * 128)
                Ch = cs_ref[cc, :, ln]
                nh = ns_ref[cc, 0:1, ln]
                m_prev = jnp.max(ms_ref[cc, 0:1, ln], axis=1, keepdims=True)
                qh, kh, vh = qs[rows, ln], ks[rows, ln], v_ref[rows, ln]
                f = _mlstm_head(qh, kh, vh, G, b_col, b_row, g_row, h, Ch, nh, m_prev)
                hh, inv_dd, den, g, Am, Dm = f["h"], f["inv_dd"], f["den"], f["g"], f["Am"], f["Dm"]
                qb, kb, vb = f["qb"], f["kb"], f["vb"]
                gn_h = gn_ref[:, ln]
                _, hn, r, sgo = _head_out(hh, o_ref[rows, ln], gn_h)
                do = do_ref[rows, ln]
                hm = hn * gn_h
                dm_ref[rows, DM_O + h * 128:DM_O + (h + 1) * 128] = _bf(do * hm * sgo * (1.0 - sgo))
                dhm = do * sgo
                dgn_ref[:, ln] = dgn_ref[:, ln] + jnp.sum(dhm * hn, axis=0, keepdims=True)
                dhn = dhm * gn_h
                dh = r * (dhn - hn * jnp.mean(dhn * hn, axis=-1, keepdims=True))
                dnum = dh * inv_dd
                ddd = -jnp.sum(dh * hh, axis=1, keepdims=True) * inv_dd
                dden = jnp.where(jnp.abs(den) >= f["floor"], ddd * jnp.sign(den), 0.0)
                dnb = _bf(dnum)
                dA = _dot_nt(dnb, vb) + dden
                dv = _dot_tn(_bf(Am), dnb)
                gd = _bf(g * dnum)
                gq = g * dden
                dq = _dot_nt(gd, _bf(Ch)) + gq * nh
                dCn = dCst[:, ln]
                dnn = dnst[0:1, ln]
                dC = f["decay"] * dCn + _dot_tn(qb, gd)
                dn = f["decay"] * dnn + jnp.sum(gq * qh, axis=0, keepdims=True)
                dg = jnp.sum(dnum * f["qC"], axis=1, keepdims=True) + dden * f["qn"]
                dS = _bf(dA * Dm)
                dq = dq + _dot(dS, kb)
                dk = _dot_tn(dS, qb)
                Gm = dA * Am
                gam = dg * g
                dCb = _bf(dCn)
                E = _dot_nt(vb, dCb) + dnn
                ws = f["ws"]
                dk = dk + ws * E
                om = jnp.sum(E * kh, axis=1, keepdims=True) * ws
                dv = dv + _dot(_bf(f["kw"]), dCb)
                ddecay = (jnp.sum(jnp.sum(dCn * Ch, axis=1, keepdims=True), axis=0, keepdims=True)
                          + jnp.sum(dnn * nh, axis=1, keepdims=True))
                delta = ddecay * f["decay"]
                rows_g = jnp.sum(Gm, axis=1, keepdims=True)
                cols_g = jnp.broadcast_to(jnp.sum(Gm, axis=0, keepdims=True), (LC, 128)).T
                last = jnp.where(rowi == LC - 1, jnp.sum(om, axis=0, keepdims=True) + delta, 0.0)
                db = rows_g + gam - om + last - cols_g
                di = cols_g + om
                dB = jnp.where(lane128 == 4 + h, db, dB)
                dI = jnp.where(lane128 == h, di, dI)
                dCst[:, ln] = dC
                dnst[0:1, ln] = dn
                dqk[rows, ln] = dq
                dqk[rows, MW + h * 128:MW + (h + 1) * 128] = dk * kscale
                dm_ref[rows, DM_V + h * 128:DM_V + (h + 1) * 128] = _bf(dv)
            dlogf = jnp.dot(triu, dB, preferred_element_type=F32, precision=HI)
            dG = dI + dlogf * _sigmoid(-G)
            dG = jnp.where(lane128 < 8, dG, 0.0)
            dm_ref[rows, DM_G:DM_G + 128] = _bf(dG)
            dm_ref[rows, DM_G + 128:DM_W] = jnp.zeros((LC, DM_W - DM_G - 128), BF16)
            dgb_ref[...] = dgb_ref[...] + jnp.sum(dG, axis=0, keepdims=True)

        dy = dqk[...] * (sg * (1.0 + c * (1.0 - sg)))
        db_ref[...] = db_ref[...] + jnp.sum(dy, axis=0, keepdims=True)
        for j in range(4):
            dw_ref[j:j + 1, :] = dw_ref[j:j + 1, :] + jnp.sum(dy * taps[j], axis=0, keepdims=True)
        dyp = jnp.concatenate([dy, dyhead[...]], axis=0)
        dx = w_ref[3:4, :] * dy
        for j in range(3):
            dx = dx + w_ref[j:j + 1, :] * pltpu.roll(dyp, TB + 8 - (3 - j), 0)[0:TB]
        dm_ref[:, 0:DM_V] = _bf(dx)
        dyhead[...] = dy[0:8, :]
        pl.when(i == nblk - 1)(rs_finish)

    rrow = lambda wd: pl.BlockSpec((TB, wd), lambda i: (nblk - 1 - i, 0))
    st = lambda r: pl.BlockSpec((ncb, r, MW), lambda i: (nblk - 1 - i, 0, 0))
    prev8 = pl.BlockSpec((8, 1024), lambda i: (jnp.maximum((nblk - 1 - i) * (TB // 8) - 1, 0), 0))
    res = pl.pallas_call(
        body, name="mlstm_bwd", grid=(nblk,),
        in_specs=[rrow(1024), prev8, rrow(MW), rrow(MW), rrow(128), _cspec((4, 1024)), _cspec((1, 1024)),
                  _cspec((1, 128)), _cspec((1, MW)), st(128), st(8), st(8), rrow(MW)] + [ANY] * nw,
        out_specs=[rrow(DM_W),
                   pl.BlockSpec((4, 1024), lambda i: (0, 0)), pl.BlockSpec((1, 1024), lambda i: (0, 0)),
                   pl.BlockSpec((1, MW), lambda i: (0, 0)), pl.BlockSpec((1, 128), lambda i: (0, 0))] + [ANY] * nw,
        out_shape=[jax.ShapeDtypeStruct((S, DM_W), BF16),
                   jax.ShapeDtypeStruct((4, 1024), F32), jax.ShapeDtypeStruct((1, 1024), F32),
                   jax.ShapeDtypeStruct((1, MW), F32), jax.ShapeDtypeStruct((1, 128), F32)]
        + [jax.ShapeDtypeStruct(a.shape, a.dtype) for a in parts],
        scratch_shapes=[pltpu.VMEM((128, MW), F32), pltpu.VMEM((8, MW), F32), pltpu.VMEM((8, 1024), F32),
                        pltpu.VMEM((TB, MW), F32), pltpu.VMEM((TB, MW), F32), pltpu.VMEM((TB, 1024), F32)]
        + _scatter_scratch(nw),
        compiler_params=_params(1),
    )(mqk, mqk, mv, mo, gates, conv_w, conv_b, gate_b, gn, cs, ns, ms, dout, *parts)
    return res[:5], res[5:]


def _out_proj(x, attn, ml, w, g):
    tm = TM

    def body(x_ref, a_ref, m_ref, w_ref, g_ref, h_ref, u_ref):
        h1 = x_ref[...] + _dot(_bf(a_ref[...]), w_ref[0:AW, :]) + _dot(_bf(m_ref[...]), w_ref[AW:D, :])
        h_ref[...] = h1
        n, _ = _rms(h1)
        u_ref[...] = _bf(n * g_ref[...])

    row = lambda wd: pl.BlockSpec((tm, wd), lambda i: (i, 0))
    return pl.pallas_call(
        body, name="out_proj", grid=(S // tm,),
        in_specs=[row(D), row(AW), row(MW), _cspec((D, D)), _cspec((1, D))],
        out_specs=[row(D), row(D)],
        out_shape=[jax.ShapeDtypeStruct((S, D), F32), jax.ShapeDtypeStruct((S, D), BF16)],
        compiler_params=_params(1),
    )(x, attn, ml, w, g)


HALF = DFF // NDEV // 2


def _mlp_fwd(h1, u2, w_up, w_down_a, w_down_b):
    tm = TM

    def body(h_ref, u_ref, wu_ref, wa_ref, wb_ref, a_ref, o_ref):
        u = u_ref[...]
        acc = h_ref[...]
        for c in range(NDEV):
            cols = slice(c * 512, (c + 1) * 512)
            a = _dot(u, wu_ref[c])
            a_ref[:, cols] = _bf(a)
            r = jnp.maximum(a, 0.0)
            r = _bf(r * r)
            acc = acc + _dot(r[:, 0:HALF], wa_ref[c]) + _dot(r[:, HALF:2 * HALF], wb_ref[c])
        o_ref[...] = acc

    row = lambda wd: pl.BlockSpec((tm, wd), lambda i: (i, 0))
    return pl.pallas_call(
        body, name="mlp_fwd", grid=(S // tm,),
        in_specs=[row(D), row(D), _cspec((NDEV, D, DFF // NDEV)), _cspec((NDEV, HALF, D)), _cspec((NDEV, HALF, D))],
        out_specs=[row(DFF), row(D)],
        out_shape=[jax.ShapeDtypeStruct((S, DFF), BF16), jax.ShapeDtypeStruct((S, D), F32)],
        compiler_params=_params(1),
    )(h1, u2, w_up, w_down_a, w_down_b)


def _ple_loss(h2, p, target, w_pg, w_ple, g_ple, g_fin):
    tm = TM

    def body(h_ref, p_ref, t_ref, wg_ref, wp_ref, gp_ref, gf_ref,
             dh_ref, dwg_ref, dwp_ref, dgp_ref, dgf_ref, loss_ref, acc_g, acc_p):
        i = pl.program_id(0)

        @pl.when(i == 0)
        def _():
            acc_g[...] = jnp.zeros_like(acc_g)
            acc_p[...] = jnp.zeros_like(acc_p)
            dgp_ref[...] = jnp.zeros_like(dgp_ref)
            dgf_ref[...] = jnp.zeros_like(dgf_ref)
            loss_ref[...] = jnp.zeros_like(loss_ref)

        h2v = h_ref[...]
        n2, rs2 = _rms(h2v)
        u3 = _bf(n2 * gp_ref[...])
        gt = _sigmoid(_dot(u3, wg_ref[...]))
        pb = _bf(p_ref[...])
        e = jnp.concatenate([_dot(pb, wp_ref[j]) for j in range(NDEV)], axis=1)
        h3 = h2v + gt * e
        n3, rs3 = _rms(h3)
        err = n3 * gf_ref[...] - t_ref[...]
        loss_ref[...] = loss_ref[...] + 0.5 / D * jnp.sum(jnp.sum(err * err, axis=1, keepdims=True), axis=0, keepdims=True)
        dy = err * (1.0 / D)
        dgf_ref[...] = dgf_ref[...] + jnp.sum(dy * n3, axis=0, keepdims=True)
        dh3 = _rms_bwd(dy, n3, rs3, gf_ref[...])
        de = _bf(dh3 * gt)
        dz = _bf(dh3 * e * gt * (1.0 - gt))
        acc_p[...] = acc_p[...] + _dot_tn(pb, de)
        acc_g[...] = acc_g[...] + _dot_tn(u3, dz)
        du3 = _dot_nt(dz, wg_ref[...])
        dgp_ref[...] = dgp_ref[...] + jnp.sum(du3 * n2, axis=0, keepdims=True)
        dh_ref[...] = dh3 + _rms_bwd(du3, n2, rs2, gp_ref[...])

        @pl.when(i == S // tm - 1)
        def _():
            dwg_ref[...] = _bf(acc_g[...])
            for j in range(NDEV):
                dwp_ref[j] = _bf(acc_p[:, j * 128:(j + 1) * 128])

    row = lambda wd: pl.BlockSpec((tm, wd), lambda i: (i, 0))
    whole = lambda shp: pl.BlockSpec(shp, lambda i: (0,) * len(shp))
    return pl.pallas_call(
        body, name="ple_loss", grid=(S // tm,),
        in_specs=[row(D), row(PLE), row(D), _cspec((D, D)), _cspec((NDEV, PLE, 128)), _cspec((1, D)), _cspec((1, D))],
        out_specs=[row(D), whole((D, D)), whole((NDEV, PLE, 128)), whole((1, D)), whole((1, D)), whole((1, 1))],
        out_shape=[jax.ShapeDtypeStruct((S, D), F32), jax.ShapeDtypeStruct((D, D), BF16),
                   jax.ShapeDtypeStruct((NDEV, PLE, 128), BF16), jax.ShapeDtypeStruct((1, D), F32),
                   jax.ShapeDtypeStruct((1, D), F32), jax.ShapeDtypeStruct((1, 1), F32)],
        scratch_shapes=[pltpu.VMEM((D, D), F32), pltpu.VMEM((PLE, D), F32)],
        compiler_params=_params(1),
    )(h2, p, target, w_pg, w_ple, g_ple, g_fin)


def _mlp_bwd(dh2, a, h1, g, w_up, w_down_a, w_down_b):
    tm = TM

    def body(d_ref, a_ref, h_ref, g_ref, wu_ref, wa_ref, wb_ref, da_ref, dh1_ref, dg_ref):
        @pl.when(pl.program_id(0) == 0)
        def _():
            dg_ref[...] = jnp.zeros_like(dg_ref)

        dh2v = d_ref[...]
        db = _bf(dh2v)
        du = jnp.zeros((tm, D), F32)
        for c in range(NDEV):
            cols = slice(c * 512, (c + 1) * 512)
            dr = jnp.concatenate([_dot_nt(db, wa_ref[c]), _dot_nt(db, wb_ref[c])], axis=1)
            da = _bf(dr * (2.0 * jnp.maximum(a_ref[:, cols], 0.0)))
            da_ref[:, cols] = da
            du = du + _dot_nt(da, wu_ref[c])
        n, rs = _rms(h_ref[...])
        dg_ref[...] = dg_ref[...] + jnp.sum(du * n, axis=0, keepdims=True)
        dh1_ref[...] = dh2v + _rms_bwd(du, n, rs, g_ref[...])

    row = lambda wd: pl.BlockSpec((tm, wd), lambda i: (i, 0))
    return pl.pallas_call(
        body, name="mlp_bwd", grid=(S // tm,),
        in_specs=[row(D), row(DFF), row(D), _cspec((1, D)), _cspec((NDEV, D, DFF // NDEV)), _cspec((NDEV, HALF, D)),
                  _cspec((NDEV, HALF, D))],
        out_specs=[row(DFF), row(D), pl.BlockSpec((1, D), lambda i: (0, 0))],
        out_shape=[jax.ShapeDtypeStruct((S, DFF), BF16), jax.ShapeDtypeStruct((S, D), F32),
                   jax.ShapeDtypeStruct((1, D), F32)],
        compiler_params=_params(1),
    )(dh2, a, h1, g, w_up, w_down_a, w_down_b)


def _out_proj_bwd(dh1, attn, ml, w):
    tm = TM

    def body(d_ref, a_ref, m_ref, w_ref, da_ref, dm_ref, dw_ref, acc):
        i = pl.program_id(0)

        @pl.when(i == 0)
        def _():
            acc[...] = jnp.zeros_like(acc)

        db = _bf(d_ref[...])
        dmix = _dot_nt(db, w_ref[...])
        da_ref[...] = dmix[:, 0:AW]
        dm_ref[...] = dmix[:, AW:D]
        acc[0:AW, :] = acc[0:AW, :] + _dot_tn(_bf(a_ref[...]), db)
        acc[AW:D, :] = acc[AW:D, :] + _dot_tn(_bf(m_ref[...]), db)

        @pl.when(i == S // tm - 1)
        def _():
            dw_ref[...] = _bf(acc[...])

    row = lambda wd: pl.BlockSpec((tm, wd), lambda i: (i, 0))
    return pl.pallas_call(
        body, name="out_proj_bwd", grid=(S // tm,),
        in_specs=[row(D), row(AW), row(MW), _cspec((D, D))],
        out_specs=[row(AW), row(MW), pl.BlockSpec((D, D), lambda i: (0, 0))],
        out_shape=[jax.ShapeDtypeStruct((S, AW), F32), jax.ShapeDtypeStruct((S, MW), F32),
                   jax.ShapeDtypeStruct((D, D), BF16)],
        scratch_shapes=[pltpu.VMEM((D, D), F32)],
        compiler_params=_params(1),
    )(dh1, attn, ml, w)


CHIP_FLIPS = [(0, 0), (0, 1), (1, 0), (1, 1)]


def _scatter2_phases(in_ref, out_ref, mine_v, sib_v, psum_v, loc_sems, d2d_send, d2d_recv, ici_send, ici_recv, own_sem):
    x, y, c = _place()
    chips = [((x + dx) % 2, (y + dy) % 2) for dx, dy in CHIP_FLIPS]
    nc = len(chips)

    def local(k):
        return pltpu.make_async_copy(in_ref.at[_dev_index(*chips[k], c)], mine_v.at[k], loc_sems.at[k])

    def to_sib(k):
        return pltpu.make_async_remote_copy(
            src_ref=in_ref.at[_dev_index(*chips[k], 1 - c)], dst_ref=sib_v.at[k], send_sem=d2d_send.at[k],
            recv_sem=d2d_recv.at[k], device_id=(x, y, 1 - c), device_id_type=MESH)

    def over_ici(k):
        return pltpu.make_async_remote_copy(
            src_ref=psum_v.at[k], dst_ref=out_ref.at[k], send_sem=ici_send.at[k - 1], recv_sem=ici_recv.at[k - 1],
            device_id=(*chips[k], c), device_id_type=MESH)

    def own():
        return pltpu.make_async_copy(psum_v.at[0], out_ref.at[0], own_sem)

    def start():
        for k in range(nc):
            to_sib(k).start()
            local(k).start()

    def middle():
        for k in (1, 2, 3, 0):
            local(k).wait()
            to_sib(k).wait_recv()
            psum_v[k] = _bf(mine_v[k].astype(F32) + sib_v[k].astype(F32))
            (over_ici(k) if k else own()).start()

    def finish():
        for k in range(1, nc):
            over_ici(k).wait()
        for k in range(nc):
            to_sib(k).wait_send()
        own().wait()

    return start, middle, finish


def _scatter2_scratch(shard, dtype):
    nc = len(CHIP_FLIPS)
    return ([pltpu.VMEM((nc, *shard), dtype)] * 3
            + [pltpu.SemaphoreType.DMA((nc,))] * 3 + [pltpu.SemaphoreType.DMA((nc - 1,))] * 2 + [pltpu.SemaphoreType.DMA])


def _in_proj_bwd(dparts, n_roped, rope, dh1, x, g1, w, part):
    tm = TM
    nt = S // tm
    widths = [d.shape[1] for d in dparts]
    assert sum(widths) == PW
    npar = len(dparts)

    def body(*refs):
        d_refs = refs[:npar]
        tabs = [t[...] for t in refs[npar:npar + 3]]
        dh_ref, x_ref, g_ref, w_ref, in_ref, dx_ref, dg_ref, out_ref = refs[npar + 3:npar + 11]
        rs_start, rs_middle, rs_finish = _scatter2_phases(in_ref, out_ref, *refs[npar + 11:])
        i = pl.program_id(0)
        pl.when(i == 0)(rs_start)
        pl.when(i == 1)(rs_middle)

        @pl.when(i == 0)
        def _():
            dg_ref[...] = jnp.zeros_like(dg_ref)

        du = jnp.zeros((tm, D), F32)
        off = 0
        for j, (d_ref, wd) in enumerate(zip(d_refs, widths)):
            nc = next(c for c in (768, 512) if wd % c == 0)
            for s in range(wd // nc):
                d = d_ref[:, s * nc:(s + 1) * nc]
                du = du + _dot_nt(_unrope(d, *tabs) if j < n_roped else d, w_ref[:, off + s * nc:off + (s + 1) * nc])
            off += wd
        n, rs = _rms(x_ref[...])
        dg_ref[...] = dg_ref[...] + jnp.sum(du * n, axis=0, keepdims=True)
        dx_ref[...] = dh_ref[...] + _rms_bwd(du, n, rs, g_ref[...])
        pl.when(i == nt - 1)(rs_finish)

    row = lambda wd: pl.BlockSpec((tm, wd), lambda i: (i, 0))
    shard = part.shape[1:]
    return pl.pallas_call(
        body, name="in_proj_bwd", grid=(nt,),
        in_specs=[row(wd) for wd in widths] + [row(128)] * 3 + [row(D), row(D), _cspec((1, D)), _cspec((D, PW)), ANY],
        out_specs=[row(D), pl.BlockSpec((1, D), lambda i: (0, 0)), ANY],
        out_shape=[jax.ShapeDtypeStruct((S, D), F32), jax.ShapeDtypeStruct((1, D), F32),
                   jax.ShapeDtypeStruct((len(CHIP_FLIPS), *shard), part.dtype)],
        scratch_shapes=_scatter2_scratch(shard, part.dtype),
        compiler_params=_params(1),
    )(*dparts, *rope, dh1, x, g1, w, part)


SMALL_ROWS = 96


def _small_phases(ins, out_ref, pack, rbuf, send_sems, recv_sems):
    x, y, c = _place()
    me = _dev_index(x, y, c)

    def copies():
        out = []
        for k, (dx, dy, dc) in enumerate(FLIPS):
            peer = ((x + dx) % 2, (y + dy) % 2, (c + dc) % 2)
            out.append(pltpu.make_async_remote_copy(
                src_ref=pack, dst_ref=rbuf.at[me], send_sem=send_sems.at[k], recv_sem=recv_sems.at[k],
                device_id=peer, device_id_type=MESH))
        return out

    def start():
        pack[...] = jnp.zeros_like(pack)
        for i, ref in enumerate(ins):
            pack[8 * i:8 * i + 1, 0:ref.shape[1]] = ref[...]
        rbuf[me] = pack[...]
        for cp in copies():
            cp.start()

    def finish():
        for cp in copies():
            cp.wait()
        tot = rbuf[0]
        for j in range(1, NDEV):
            tot = tot + rbuf[j]
        out_ref[...] = tot

    return start, finish


def _wgrad(name, A, Bs, a_fn, b_fn, out_shape, split=None, ts=512, small=(), rope=(), n_roped=0):
    K = A.shape[1]
    widths = [b.shape[1] for b in Bs]
    N = sum(widths)
    nb, ns, nrt = len(Bs) + len(rope), len(small), S // ts
    kc = min(K, 1024)

    def body(*refs):
        a_ref, b_refs = refs[0], refs[1:1 + len(Bs)]
        tabs = [t[...] for t in refs[1 + len(Bs):1 + nb]]
        o_ref = refs[1 + nb + ns]
        acc = refs[2 + nb + ns + bool(ns)]
        r = pl.program_id(0)
        if ns:
            sm_start, sm_finish = _small_phases(refs[1 + nb:1 + nb + ns], refs[2 + nb + ns], *refs[4 + nb + ns:])
            pl.when(r == 0)(sm_start)

        @pl.when(r == 0)
        def _():
            acc[...] = jnp.zeros_like(acc)

        bs, off = [], 0
        for i, (b_ref, w) in enumerate(zip(b_refs, widths)):
            nc = next(c for c in (1024, 768, 512) if w % c == 0)
            fn = (lambda t: _unrope(t, *tabs)) if i < n_roped else b_fn
            bs += [(off + c * nc, nc, fn(b_ref[:, c * nc:(c + 1) * nc])) for c in range(w // nc)]
            off += w
        for kk in range(K // kc):
            rows = slice(kk * kc, (kk + 1) * kc)
            at = a_fn(a_ref[:, rows]).T
            for lo, nc, b in bs:
                acc[rows, lo:lo + nc] = acc[rows, lo:lo + nc] + _dot(at, b)

        @pl.when(r == nrt - 1)
        def _():
            if split is None:
                o_ref[...] = _bf(acc[...])
            else:
                for j in range(NDEV):
                    o_ref[j] = _bf(acc[:, split * j:split * (j + 1)])

        if ns:
            pl.when(r == nrt - 1)(sm_finish)

    in_specs = ([pl.BlockSpec((ts, K), lambda r: (r, 0))] + [pl.BlockSpec((ts, w), lambda r: (r, 0)) for w in widths]
                + [pl.BlockSpec((ts, 128), lambda r: (r, 0))] * len(rope))
    out_spec = pl.BlockSpec(out_shape, lambda r: (0,) * len(out_shape))
    scratch = [pltpu.VMEM((K, N), F32)]
    if not ns:
        return pl.pallas_call(
            body, name=name, grid=(nrt,), in_specs=in_specs, out_specs=out_spec,
            out_shape=jax.ShapeDtypeStruct(out_shape, BF16), scratch_shapes=scratch, compiler_params=_params(1),
        )(A, *Bs, *rope)
    return pl.pallas_call(
        body, name=name, grid=(nrt,), in_specs=in_specs + [VM] * ns, out_specs=[out_spec, VM],
        out_shape=[jax.ShapeDtypeStruct(out_shape, BF16), jax.ShapeDtypeStruct((SMALL_ROWS, 1024), F32)],
        scratch_shapes=scratch + [pltpu.VMEM((SMALL_ROWS, 1024), F32), pltpu.VMEM((NDEV, SMALL_ROWS, 1024), F32),
                                  pltpu.SemaphoreType.DMA((7,)), pltpu.SemaphoreType.DMA((7,))],
        compiler_params=_params(1),
    )(A, *Bs, *rope, *small)


def _relu2_bf(a):
    r = jnp.maximum(a.astype(F32), 0.0)
    return _bf(r * r)


def _ident(a):
    return a


def _step(x, p, target, g1, conv_b, gate_b, gn, g_mlp, g_ple, g_fin, sh):
    (g_in, g_conv), (rc, ra, rb) = _gather_weights([sh["w_in"], sh["conv_w"]], [BF16, F32])
    conv_w = g_conv.transpose(1, 0, 2).reshape(4, 1024)
    w_in_p = _join_w_in(g_in)
    (qkv, mqk, mv, mo, gates, u1), (w_out8, w_pg8, w_ple8) = _in_proj(
        x, g1, w_in_p, rc, ra, rb, [sh["w_out"], sh["w_ple_gate"], sh["w_ple"]], [BF16] * 3)
    attn, lse, (w_up8, w_down_a) = _attn_fwd(qkv, [sh["w_up"], sh["w_down"][0:HALF]], [BF16] * 2)
    ml, cs, ns, ms, (w_down_b,) = _mlstm_fwd(mqk, mv, mo, gates, conv_w, conv_b, gate_b, gn,
                                             [sh["w_down"][HALF:2 * HALF]], [BF16])
    w_out, w_pg = w_out8.reshape(D, D), w_pg8.reshape(D, D)
    h1, u2 = _out_proj(x, attn, ml, w_out, g_mlp)
    a, h2 = _mlp_fwd(h1, u2, w_up8, w_down_a, w_down_b)
    dh2, dw_pg, dw_ple8, dg_ple, dg_fin, loss = _ple_loss(h2, p, target, w_pg, w_ple8, g_ple, g_fin)
    da, dh1, dg_mlp = _mlp_bwd(dh2, a, h1, g_mlp, w_up8, w_down_a, w_down_b)
    dw_up8 = _wgrad("wgrad_up", u2, [da], _ident, _ident, (NDEV, D, DFF // NDEV), split=DFF // NDEV)
    dw_down = _wgrad("wgrad_down", a, [dh2], _relu2_bf, _bf, (DFF, D))
    d_attn, d_ml, dw_out = _out_proj_bwd(dh1, attn, ml, w_out)
    (dm, dconv_w, dconv_b, dgn, dgate_b), (r_out, r_pg, r_ple) = _mlstm_bwd(
        mqk, mv, mo, gates, conv_w, conv_b, gate_b, gn, cs, ns, ms, d_ml,
        [dw_out.reshape(NDEV, D // NDEV, D), dw_pg.reshape(NDEV, D // NDEV, D), dw_ple8])
    dq, dk, dv, (r_up, r_down) = _attn_bwd(qkv, attn, lse, d_attn, [dw_up8, dw_down.reshape(NDEV, DFF // NDEV, D)])
    dparts = [dq, dk, dv, dm]
    small = [jnp.zeros((1, D), F32), dconv_b, dgate_b, dgn, dg_mlp, dg_ple, dg_fin, loss]
    dw_in8, total = _wgrad("wgrad_in", u1, dparts, _ident, _ident, (NDEV, D, IN_W // NDEV), split=IN_W // NDEV,
                           small=small + [dconv_w[j:j + 1] for j in range(4)], rope=(rc, ra, rb), n_roped=2)
    dx, dg1, r_in = _in_proj_bwd(dparts, 2, (rc, ra, rb), dh1, x, g1, w_in_p, dw_in8)
    recv = dict(w_in=r_in, w_out=r_out, w_up=r_up, w_down=r_down, w_ple_gate=r_pg, w_ple=r_ple)
    return dx, recv, total, _allreduce_vec(dg1)


def _gather_weights(shards, dtypes):
    nw = len(shards)

    def body(*refs):
        ins, parts = refs[:nw], refs[nw:nw + 4]
        outs, tables = refs[nw + 4:2 * nw + 4], refs[2 * nw + 4:2 * nw + 7]
        start, forward, finish = _gather_phases(ins, outs, refs[2 * nw + 7:3 * nw + 7], *refs[3 * nw + 7:])
        start()
        _rope_fill(*parts, *tables)
        forward()
        finish()

    res = pl.pallas_call(
        body, name="gather_weights",
        in_specs=[VM] * (nw + 4), out_specs=[ANY] * nw + [VM] * 3,
        out_shape=_gather_shapes(shards, dtypes) + [jax.ShapeDtypeStruct((S, 128), F32)] * 3,
        scratch_shapes=_gather_scratch(shards, dtypes),
        compiler_params=_params(),
    )(*shards, *_rope_parts())
    return res[:nw], res[nw:]


def _allreduce_vec(v):
    def body(v_ref, out_ref, pack, rbuf, send_sems, recv_sems):
        start, finish = _small_phases([v_ref], out_ref, pack, rbuf, send_sems, recv_sems)
        start()
        finish()

    return pl.pallas_call(
        body, name="allreduce_last", out_shape=jax.ShapeDtypeStruct((8, 1024), F32),
        scratch_shapes=[pltpu.VMEM((8, 1024), F32), pltpu.VMEM((NDEV, 8, 1024), F32),
                        pltpu.SemaphoreType.DMA((7,)), pltpu.SemaphoreType.DMA((7,))],
        compiler_params=_params(),
    )(v)


def _adamw(name, gparts, w, m, v, tr):
    P, R, C = gparts.shape

    def body(g_ref, w_ref, m_ref, v_ref, go_ref, d_ref, mo_ref, vo_ref):
        g = g_ref[0].astype(F32)
        for j in range(1, P):
            g = g + g_ref[j].astype(F32)
        go_ref[...] = g
        d_ref[...], mo_ref[...], vo_ref[...] = _adam_update(g, w_ref[...], m_ref[...], v_ref[...])

    row = pl.BlockSpec((tr, C), lambda i: (i, 0))
    return pl.pallas_call(
        body, name=name, grid=(R // tr,),
        in_specs=[pl.BlockSpec((P, tr, C), lambda i: (0, i, 0)), row, row, row],
        out_specs=[row] * 4,
        out_shape=[jax.ShapeDtypeStruct((R, C), F32)] * 4,
        compiler_params=_params(1),
    )(gparts, w, m, v)


SMALL = ("norm_mix_g", "conv_b", "gate_b", "mlstm_norm_g", "norm_mlp_g", "norm_ple_g", "final_norm_g")


def _adam_update(g, w, m, v):
    c1 = 1.0 - ADAM_B1 ** ADAM_STEP
    c2 = 1.0 - ADAM_B2 ** ADAM_STEP
    m2 = ADAM_B1 * m + (1.0 - ADAM_B1) * g
    v2 = ADAM_B2 * v + (1.0 - ADAM_B2) * (g * g)
    return -ADAM_LR * ((m2 / c1) / (jnp.sqrt(v2 / c2) + ADAM_EPS) + ADAM_WD * w), m2, v2


def _adamw_small(total, first, ws, ms, vs):
    n = len(ws)

    def body(*refs):
        t_ref, f_ref = refs[:2]
        refs = refs[1:]
        outs = refs[1 + 3 * n:]
        for i in range(n):
            w_ref, m_ref, v_ref = refs[1 + i], refs[1 + n + i], refs[1 + 2 * n + i]
            g = (t_ref if i else f_ref)[8 * i:8 * i + 1, 0:w_ref.shape[1]]
            delta, m2, v2 = _adam_update(g, w_ref[...], m_ref[...], v_ref[...])
            for ref, val in zip(outs[4 * i:4 * i + 4], (g, delta, m2, v2)):
                ref[...] = val

    res = pl.pallas_call(
        body, name="adamw_small",
        out_shape=[jax.ShapeDtypeStruct(w.shape, F32) for w in ws for _ in range(4)],
        compiler_params=_params(),
    )(total, first, *ws, *ms, *vs)
    return [res[4 * i:4 * i + 4] for i in range(n)]


def kernel(x, p, norm_mix_g, w_in, conv_w, conv_b, gate_b, mlstm_norm_g, w_out, norm_mlp_g, w_up, w_down, norm_ple_g, w_ple_gate, w_ple, final_norm_g, loss_target, m_norm_mix_g, m_w_in, m_conv_w, m_conv_b, m_gate_b, m_mlstm_norm_g, m_w_out, m_norm_mlp_g, m_w_up, m_w_down, m_norm_ple_g, m_w_ple_gate, m_w_ple, m_final_norm_g, v_norm_mix_g, v_w_in, v_conv_w, v_conv_b, v_gate_b, v_mlstm_norm_g, v_w_out, v_norm_mlp_g, v_w_up, v_w_down, v_norm_ple_g, v_w_ple_gate, v_w_ple, v_final_norm_g):
    big_names = ("w_in", "conv_w", "w_out", "w_up", "w_down", "w_ple_gate", "w_ple")
    wts = dict(w_in=w_in, conv_w=conv_w, w_out=w_out, w_up=w_up, w_down=w_down, w_ple_gate=w_ple_gate, w_ple=w_ple)
    mom = dict(w_in=m_w_in, conv_w=m_conv_w, w_out=m_w_out, w_up=m_w_up, w_down=m_w_down, w_ple_gate=m_w_ple_gate,
               w_ple=m_w_ple)
    var = dict(w_in=v_w_in, conv_w=v_conv_w, w_out=v_w_out, w_up=v_w_up, w_down=v_w_down, w_ple_gate=v_w_ple_gate,
               w_ple=v_w_ple)
    sq = lambda a: a.reshape(a.shape[1:])
    fin = final_norm_g.reshape(1, D)
    dx, recv, total, first = _step(
        x[0], p[0, 0], loss_target[0], norm_mix_g, conv_b, jnp.pad(gate_b, ((0, 0), (0, 120))), mlstm_norm_g,
        norm_mlp_g, norm_ple_g, fin, {n: sq(wts[n]) for n in big_names})

    nrow = 8 * len(SMALL)
    me = _dev_index(*_place())
    conv_rows = total[nrow + 8:nrow + 40:8]
    recv["conv_w"] = lax.dynamic_slice_in_dim(conv_rows, me * 128, 128, axis=1).reshape(1, 4, 128)
    out = {}
    for n, tr in zip(big_names, (256, 4, 128, 256, 256, 128, 256)):
        res = _adamw("adamw_" + n, recv[n], sq(wts[n]), sq(mom[n]), sq(var[n]), tr)
        out[n] = [t.reshape(wts[n].shape) for t in res]
    sw = dict(norm_mix_g=norm_mix_g, conv_b=conv_b, gate_b=gate_b, mlstm_norm_g=mlstm_norm_g, norm_mlp_g=norm_mlp_g,
              norm_ple_g=norm_ple_g, final_norm_g=fin)
    sm = dict(norm_mix_g=m_norm_mix_g, conv_b=m_conv_b, gate_b=m_gate_b, mlstm_norm_g=m_mlstm_norm_g,
              norm_mlp_g=m_norm_mlp_g, norm_ple_g=m_norm_ple_g, final_norm_g=m_final_norm_g.reshape(1, D))
    sv = dict(norm_mix_g=v_norm_mix_g, conv_b=v_conv_b, gate_b=v_gate_b, mlstm_norm_g=v_mlstm_norm_g,
              norm_mlp_g=v_norm_mlp_g, norm_ple_g=v_norm_ple_g, final_norm_g=v_final_norm_g.reshape(1, D))
    res = _adamw_small(total, first, [sw[n] for n in SMALL], [sm[n] for n in SMALL], [sv[n] for n in SMALL])
    for n, r in zip(SMALL, res):
        out[n] = [t.reshape(final_norm_g.shape) for t in r] if n == "final_norm_g" else list(r)
    order = ("norm_mix_g", "w_in", "conv_w", "conv_b", "gate_b", "mlstm_norm_g", "w_out", "norm_mlp_g", "w_up", "w_down",
             "norm_ple_g", "w_ple_gate", "w_ple", "final_norm_g")
    loss_all = total[nrow, 0]
    return (loss_all, dx[None], *[out[n][0] for n in order], *[out[n][1] for n in order],
            *[out[n][2] for n in order], *[out[n][3] for n in order])
```

```python
import functools
import math

import jax
import jax.numpy as jnp
from jax import lax
from jax.experimental import pallas as pl
from jax.experimental.pallas import tpu as pltpu

F32, BF16 = jnp.float32, jnp.bfloat16
S = 4096
D = 1024
AW = 512
MW = 512
DFF = 4096
PLE = 256
IN_W = 3592
PW = 3840
NDEV = 8
EPS = 1e-6
NEG = -1e30
LC = 128
TB = 256
ROPE_THETA = 500000.0
VMEM_LIMIT = 56 * 1024 * 1024
HI = lax.Precision.HIGHEST

ADAM_LR, ADAM_B1, ADAM_B2, ADAM_EPS, ADAM_WD, ADAM_STEP = 0.001, 0.9, 0.999, 1e-08, 0.01, 10


def _params(n_grid=0, **kw):
    sem = dict(dimension_semantics=("arbitrary",) * n_grid) if n_grid else {}
    return pltpu.CompilerParams(vmem_limit_bytes=VMEM_LIMIT, **sem, **kw)


def _cspec(shape):
    nd = len(shape)
    return pl.BlockSpec(shape, lambda *_: (0,) * nd, pipeline_mode=pl.Buffered(1))


def _dot(a, b):
    return jnp.dot(a, b, preferred_element_type=F32)


def _dot_nt(a, b):
    return lax.dot_general(a, b, (((1,), (1,)), ((), ())), preferred_element_type=F32)


def _dot_tn(a, b):
    return lax.dot_general(a, b, (((0,), (0,)), ((), ())), preferred_element_type=F32)


def _bf(x):
    return x.astype(BF16)


def _rms(x):
    rs = lax.rsqrt(jnp.mean(x * x, axis=-1, keepdims=True) + EPS)
    return x * rs, rs


def _rms_bwd(du, n, rs, g):
    dn = du * g
    return rs * (dn - n * jnp.mean(dn * n, axis=-1, keepdims=True))


def _sigmoid(x):
    return 1.0 / (1.0 + jnp.exp(-x))


ROPE_BLK = 512


def _rope_parts():
    def cs(n, step):
        j = lax.broadcasted_iota(jnp.int32, (n, 128), 1) % 64
        pos = (lax.broadcasted_iota(jnp.int32, (n, 128), 0) * step).astype(F32)
        ang = pos * jnp.power(ROPE_THETA, -(j % 8).astype(F32) / 8.0)
        return jnp.cos(ang), jnp.sin(ang)

    return (*cs(ROPE_BLK, 1), *cs(S // ROPE_BLK, ROPE_BLK))


def _rope_fill(co_ref, so_ref, cb_ref, sb_ref, rc_ref, ra_ref, rb_ref):
    j = lax.broadcasted_iota(jnp.int32, (ROPE_BLK, 128), 1) % 64
    co, so = co_ref[...], so_ref[...]
    for t in range(S // ROPE_BLK):
        cb, sb = cb_ref[t:t + 1, :], sb_ref[t:t + 1, :]
        cos, sin = cb * co - sb * so, sb * co + cb * so
        rows = slice(t * ROPE_BLK, (t + 1) * ROPE_BLK)
        rc_ref[rows, :] = jnp.where(j < 16, cos, 1.0)
        ra_ref[rows, :] = jnp.where(j < 8, -sin, 0.0)
        rb_ref[rows, :] = jnp.where((j >= 8) & (j < 16), sin, 0.0)


def _rope(blk, c, a, b):
    return blk * c + pltpu.roll(blk, 120, 1) * a + pltpu.roll(blk, 8, 1) * b


def _rope_bwd(d, c, a, b):
    return d * c + pltpu.roll(d * a, 8, 1) + pltpu.roll(d * b, 120, 1)


def _unrope(t, c, a, b):
    return jnp.concatenate([_bf(_rope_bwd(t[:, j * 128:(j + 1) * 128].astype(F32), c, a, b))
                            for j in range(t.shape[1] // 128)], axis=1)


MESH = pl.DeviceIdType.MESH
ANY = pl.BlockSpec(memory_space=pl.ANY)
VM = pl.BlockSpec(memory_space=pltpu.VMEM)
FLIPS = [(dx, dy, dc) for dx in (0, 1) for dy in (0, 1) for dc in (0, 1)][1:]


def _place():
    return lax.axis_index("x"), lax.axis_index("y"), lax.axis_index("c")


def _dev_index(px, py, pc):
    return 4 * px + 2 * py + pc


def _gather_phases(ins, outs, bufs, send_sems=None, recv_sems=None, local_sems=None):
    nw = len(ins)
    if nw == 0:
        return (lambda: None,) * 3
    x, y, c = _place()
    me, sib = (x, y, c), (x, y, 1 - c)
    chips = [(1 - x, y), (x, 1 - y), (1 - x, 1 - y)]

    def copy(w, k, block, to, from_buf=False):
        dst = outs[w].at[_dev_index(*block)]
        return pltpu.make_async_remote_copy(
            src_ref=bufs[w] if from_buf else dst, dst_ref=dst, send_sem=send_sems.at[w, k],
            recv_sem=recv_sems.at[w, k], device_id=to, device_id_type=MESH)

    def mine(w):
        return pltpu.make_async_copy(bufs[w], outs[w].at[_dev_index(*me)], local_sems.at[w])

    def first(w):
        return [copy(w, 0, me, sib, True)] + [copy(w, 1 + j, me, (*chip, c), True) for j, chip in enumerate(chips)]

    def passed(w):
        return [copy(w, 4 + j, (*chip, c), sib) for j, chip in enumerate(chips)]

    def start():
        for w in range(nw):
            bufs[w][...] = ins[w][...].astype(bufs[w].dtype)
        for w in range(nw):
            mine(w).start()
            for cp in first(w):
                cp.start()

    def forward():
        for j, chip in enumerate(chips):
            for w in range(nw):
                copy(w, 1 + j, (*chip, c), me).wait_recv()
                passed(w)[j].start()

    def finish():
        for w in range(nw):
            copy(w, 0, sib, me).wait_recv()
        for j, chip in enumerate(chips):
            for w in range(nw):
                copy(w, 4 + j, (*chip, 1 - c), me).wait_recv()
        for w in range(nw):
            for cp in first(w) + passed(w):
                cp.wait_send()
            mine(w).wait()

    return start, forward, finish


def _gather_scratch(shards, dtypes):
    nw = len(shards)
    if nw == 0:
        return []
    return ([pltpu.VMEM(s.shape, dt) for s, dt in zip(shards, dtypes)]
            + [pltpu.SemaphoreType.DMA((nw, 7)), pltpu.SemaphoreType.DMA((nw, 7)), pltpu.SemaphoreType.DMA((nw,))])


def _gather_shapes(shards, dtypes):
    return [jax.ShapeDtypeStruct((NDEV, *s.shape), dt) for s, dt in zip(shards, dtypes)]


def _scatter_phases(ins, outs, send_sems=None, recv_sems=None, local_sems=None):
    nw = len(ins)
    if nw == 0:
        return (lambda: None,) * 2
    x, y, c = _place()
    me = _dev_index(x, y, c)

    def copies():
        out = []
        for w in range(nw):
            out.append(pltpu.make_async_copy(ins[w].at[me], outs[w].at[me], local_sems.at[w]))
            for k, (dx, dy, dc) in enumerate(FLIPS):
                peer = ((x + dx) % 2, (y + dy) % 2, (c + dc) % 2)
                out.append(pltpu.make_async_remote_copy(
                    src_ref=ins[w].at[_dev_index(*peer)], dst_ref=outs[w].at[me], send_sem=send_sems.at[w, k],
                    recv_sem=recv_sems.at[w, k], device_id=peer, device_id_type=MESH))
        return out

    def start():
        for cp in copies():
            cp.start()

    def finish():
        for cp in copies():
            cp.wait()

    return start, finish


def _scatter_scratch(nw):
    if nw == 0:
        return []
    return [pltpu.SemaphoreType.DMA((nw, 7)), pltpu.SemaphoreType.DMA((nw, 7)), pltpu.SemaphoreType.DMA((nw,))]


TM = 512


def _join_w_in(wg):
    sw = IN_W // NDEV

    def body(wg_ref, w_ref):
        for j in range(NDEV):
            w_ref[:, sw * j:sw * (j + 1)] = wg_ref[j]
        w_ref[:, IN_W:PW] = jnp.zeros((D, PW - IN_W), BF16)

    return pl.pallas_call(body, name="join_w_in", out_shape=jax.ShapeDtypeStruct((D, PW), BF16),
                          compiler_params=_params())(wg)


def _in_proj(x, g1, w, rc, ra, rb, shards, dtypes):
    tm = TM
    nw = len(shards)
    nt = S // tm

    def body(*refs):
        x_ref, g_ref, w_ref, rc_ref, ra_ref, rb_ref = refs[:6]
        ins = refs[6:6 + nw]
        qkv_ref, mqk_ref, mv_ref, mo_ref, gt_ref, u_ref = refs[6 + nw:12 + nw]
        outs = refs[12 + nw:12 + 2 * nw]
        bufs = refs[12 + 2 * nw:12 + 3 * nw]
        ag_start, ag_forward, ag_finish = _gather_phases(ins, outs, bufs, *refs[12 + 3 * nw:])
        i = pl.program_id(0)
        pl.when(i == 0)(ag_start)
        pl.when(i == nt - 2)(ag_forward)
        n, _ = _rms(x_ref[...])
        u = _bf(n * g_ref[...])
        u_ref[...] = u
        c, a, b = rc_ref[...], ra_ref[...], rb_ref[...]
        for half in range(2):
            blk = _dot(u, w_ref[:, half * 512:(half + 1) * 512])
            for t in range(4):
                lo = half * 512 + t * 128
                qkv_ref[:, lo:lo + 128] = _rope(blk[:, t * 128:(t + 1) * 128], c, a, b)
        qkv_ref[:, 1024:1536] = _dot(u, w_ref[:, 1024:1536])
        mqk_ref[:, 0:512] = _dot(u, w_ref[:, 1536:2048])
        mqk_ref[:, 512:1024] = _dot(u, w_ref[:, 2048:2560])
        mv_ref[...] = _dot(u, w_ref[:, 2560:3072])
        mo_ref[...] = _dot(u, w_ref[:, 3072:3584])
        gt_ref[...] = _dot(u, w_ref[:, 3584:3712])
        pl.when(i == nt - 1)(ag_finish)

    row = lambda wd: pl.BlockSpec((tm, wd), lambda i: (i, 0))
    res = pl.pallas_call(
        body, name="in_proj", grid=(nt,),
        in_specs=[row(D), _cspec((1, D)), _cspec((D, PW)), row(128), row(128), row(128)] + [VM] * nw,
        out_specs=[row(1536), row(1024), row(512), row(512), row(128), row(D)] + [ANY] * nw,
        out_shape=[jax.ShapeDtypeStruct((S, 1536), F32), jax.ShapeDtypeStruct((S, 1024), F32),
                   jax.ShapeDtypeStruct((S, 512), F32), jax.ShapeDtypeStruct((S, 512), F32),
                   jax.ShapeDtypeStruct((S, 128), F32), jax.ShapeDtypeStruct((S, D), BF16)]
        + _gather_shapes(shards, dtypes),
        scratch_shapes=_gather_scratch(shards, dtypes),
        compiler_params=_params(1),
    )(x, g1, w, rc, ra, rb, *shards)
    return res[:6], res[6:]


DILATIONS = (16, 4, 1)


def _attn_valid(n):
    kd = lax.broadcasted_iota(jnp.int32, (128, 256), 1) - lax.broadcasted_iota(jnp.int32, (128, 256), 0)
    off = jnp.where(n == 0, 0, 128)
    return (kd <= off) & (kd >= off - 128)


def _attn_rows(d, r, n):
    if d == 1:
        q0 = pl.multiple_of(n * 128, 128)
        k0 = pl.multiple_of(jnp.maximum(n - 1, 0) * 128, 128)
        return pl.ds(q0, 128), pl.ds(k0, 256), _attn_valid(n)
    q0 = r + n * 128 * d
    k0 = r + jnp.maximum(n - 1, 0) * 128 * d
    return pl.ds(q0, 128, stride=d), pl.ds(k0, 256, stride=d), _attn_valid(n)


ATTN_GROUP = 4
ATTN_ITERS = S // 128 // ATTN_GROUP


def _attn_group(d, i):
    nb = S // (128 * d)
    if nb == 2:
        qi = lax.broadcasted_iota(jnp.int32, (256, 256), 0) - lax.broadcasted_iota(jnp.int32, (256, 256), 1)
        whole = [pl.ds((ATTN_GROUP // 2) * i + u, 256, stride=d) for u in range(ATTN_GROUP // 2)]
        return [(rows, rows, (qi >= 0) & (qi <= 128)) for rows in whole]
    if d == 1:
        return [_attn_rows(1, 0, i + ATTN_ITERS * u) for u in range(ATTN_GROUP)]
    return [_attn_rows(d, (i // nb) * ATTN_GROUP + u, i % nb) for u in range(ATTN_GROUP)]


def _head0(shape):
    return lax.broadcasted_iota(jnp.int32, shape, 1) < 64


def _stack_heads(t):
    h0 = _head0(t.shape)
    tb = _bf(t)
    zero = jnp.zeros_like(tb)
    return jnp.concatenate([jnp.where(h0, tb, zero), jnp.where(h0, zero, tb)], axis=0)


def _attn_fwd(qkv, shards, dtypes):
    nw = len(shards)

    def body(*refs):
        q_ref, k_ref, v_ref = refs[:3]
        ins = refs[3:3 + nw]
        o_ref, lse0_ref, lse1_ref = refs[3 + nw:6 + nw]
        outs = refs[6 + nw:6 + 2 * nw]
        m0, m1, l0, l1, acc = refs[6 + 2 * nw:11 + 2 * nw]
        bufs = refs[11 + 2 * nw:11 + 3 * nw]
        ag_start, ag_forward, ag_finish = _gather_phases(ins, outs, bufs, *refs[11 + 3 * nw:])
        hp = pl.program_id(0)
        pl.when(hp == 0)(ag_start)
        pl.when(hp == 3)(ag_forward)
        stats = (m0, m1, l0, l1, acc)

        def update(blocks, first):
            loaded = [([q_ref[rq, :], k_ref[rk, :], v_ref[rk, :]], None if first else [ref[rq, :] for ref in stats])
                      for rq, rk, _ in blocks]
            results = []
            for ((q, k, v), prev), (_, _, valid) in zip(loaded, blocks):
                head0 = _head0(q.shape)
                kb, vb = _bf(k), _bf(v)
                q = q * 0.125
                m_new, l_new, acc_new = [], [], []
                for a, qa in enumerate((_bf(jnp.where(head0, q, 0.0)), _bf(jnp.where(head0, 0.0, q)))):
                    s = jnp.where(valid, _dot_nt(qa, kb), NEG)
                    mc = jnp.max(s, axis=-1, keepdims=True)
                    m_a = jnp.broadcast_to(mc, q.shape) if first else jnp.maximum(prev[a], mc)
                    p = jnp.exp(s - jnp.tile(m_a, (1, 2)))
                    l_add = jnp.sum(p, axis=-1, keepdims=True)
                    pv = _dot(_bf(p), vb)
                    if first:
                        l_a = jnp.broadcast_to(l_add, q.shape)
                    else:
                        alpha = jnp.exp(prev[a] - m_a)
                        l_a, pv = alpha * prev[2 + a] + l_add, alpha * prev[4] + pv
                    m_new.append(m_a), l_new.append(l_a), acc_new.append(pv)
                results.append((m_new[0], m_new[1], l_new[0], l_new[1], jnp.where(head0, acc_new[0], acc_new[1])))
            for (rq, _, _), res in zip(blocks, results):
                for ref, val in zip(stats, res):
                    ref[rq, :] = val

        for d in DILATIONS:
            def step(i, carry, d=d):
                update(_attn_group(d, i), d == DILATIONS[0])
                return carry

            lax.fori_loop(0, ATTN_ITERS, step, 0)

        def fin(t, carry):
            rows = pl.ds(pl.multiple_of(t * 256, 256), 256)
            h0 = lax.broadcasted_iota(jnp.int32, (256, 128), 1) < 64
            la, lb = l0[rows, :], l1[rows, :]
            o_ref[rows, :] = acc[rows, :] / jnp.where(h0, la, lb)
            lse0_ref[rows, :] = m0[rows, :] + jnp.log(la)
            lse1_ref[rows, :] = m1[rows, :] + jnp.log(lb)
            return carry

        lax.fori_loop(0, S // 256, fin, 0)
        pl.when(hp == 3)(ag_finish)

    col = lambda off: pl.BlockSpec((S, 128), lambda h, off=off: (0, off + h))
    res = pl.pallas_call(
        body, name="attn_fwd", grid=(4,),
        in_specs=[col(0), col(4), col(8)] + [VM] * nw,
        out_specs=[col(0), col(0), col(0)] + [ANY] * nw,
        out_shape=[jax.ShapeDtypeStruct((S, AW), F32)] * 3 + _gather_shapes(shards, dtypes),
        scratch_shapes=[pltpu.VMEM((S, 128), F32)] * 5 + _gather_scratch(shards, dtypes),
        compiler_params=_params(1),
    )(qkv, qkv, qkv, *shards)
    return res[0], (res[1], res[2]), res[3:]


def _attn_bwd(qkv, o, lse, do, parts):
    nw = len(parts)

    def body(*refs):
        q_ref, k_ref, v_ref, o_ref, L0, L1, do_ref = refs[:7]
        ins = refs[7:7 + nw]
        dq_out, dk_out, dv_out = refs[7 + nw:10 + nw]
        outs = refs[10 + nw:10 + 2 * nw]
        D0, D1, dq_ref, dk_ref, dv_ref = refs[10 + 2 * nw:15 + 2 * nw]
        rs_start, rs_finish = _scatter_phases(ins, outs, *refs[15 + 2 * nw:])
        hp = pl.program_id(0)
        pl.when(hp == 0)(rs_start)

        def pre(t, carry):
            rows = pl.ds(pl.multiple_of(t * 256, 256), 256)
            h0 = lax.broadcasted_iota(jnp.int32, (256, 128), 1) < 64
            dd = do_ref[rows, :] * o_ref[rows, :]
            shp = (256, 128)
            D0[rows, :] = jnp.broadcast_to(jnp.sum(jnp.where(h0, dd, 0.0), axis=-1, keepdims=True), shp)
            D1[rows, :] = jnp.broadcast_to(jnp.sum(jnp.where(h0, 0.0, dd), axis=-1, keepdims=True), shp)
            return carry

        lax.fori_loop(0, S // 256, pre, 0)

        def update(blocks, first):
            loaded = [([q_ref[rq, :], k_ref[rk, :], v_ref[rk, :], do_ref[rq, :]],
                       [L0[rq, :], L1[rq, :], D0[rq, :], D1[rq, :]],
                       [0.0] * 3 if first else [dq_ref[rq, :], dk_ref[rk, :], dv_ref[rk, :]]) for rq, rk, _ in blocks]
            results = []
            for ((q, k, v, dout), (l0v, l1v, d0v, d1v), (dq, dk, dv)), (_, _, valid) in zip(loaded, blocks):
                nq = q.shape[0]
                valid = jnp.concatenate([valid, valid], axis=0)
                q2, do2, kb, vb = _stack_heads(q), _stack_heads(dout), _bf(k), _bf(v)
                cat = lambda a, b: jnp.tile(jnp.concatenate([a, b], axis=0), (1, 2))
                s = jnp.where(valid, _dot_nt(_stack_heads(q * 0.125), kb), NEG)
                p = jnp.exp(s - cat(l0v, l1v))
                ds = _bf(p * (_dot_nt(do2, vb) - cat(d0v, d1v)) * 0.125)
                dq2 = _dot(ds, kb)
                results.append((dq + jnp.where(_head0((nq, 128)), dq2[0:nq], dq2[nq:2 * nq]),
                                dk + _dot_tn(ds, q2), dv + _dot_tn(_bf(p), do2)))
            for (rq, rk, _), (dq, dk, dv) in zip(blocks, results):
                dq_ref[rq, :] = dq
                dk_ref[rk, :] = dk
                dv_ref[rk, :] = dv

        assert S // (128 * DILATIONS[0]) == 2
        for d in DILATIONS:
            def step(i, carry, d=d):
                update(_attn_group(d, i), d == DILATIONS[0])
                return carry

            lax.fori_loop(0, ATTN_ITERS, step, 0)

        def fin(t, carry):
            rows = pl.ds(pl.multiple_of(t * 256, 256), 256)
            for src, dst in ((dq_ref, dq_out), (dk_ref, dk_out), (dv_ref, dv_out)):
                dst[rows, :] = _bf(src[rows, :])
            return carry

        lax.fori_loop(0, S // 256, fin, 0)
        pl.when(hp == 3)(rs_finish)

    col = lambda off: pl.BlockSpec((S, 128), lambda h, off=off: (0, off + h))
    res = pl.pallas_call(
        body, name="attn_bwd", grid=(4,),
        in_specs=[col(0), col(4), col(8), col(0), col(0), col(0), col(0)] + [ANY] * nw,
        out_specs=[col(0), col(0), col(0)] + [ANY] * nw,
        out_shape=[jax.ShapeDtypeStruct((S, AW), BF16)] * 3 + [jax.ShapeDtypeStruct(a.shape, a.dtype) for a in parts],
        scratch_shapes=[pltpu.VMEM((S, 128), F32)] * 5 + _scatter_scratch(nw),
        compiler_params=_params(1),
    )(qkv, qkv, qkv, o, lse[0], lse[1], do, *parts)
    return res[0], res[1], res[2], res[3:]


def _logsig(x):
    return jnp.minimum(x, 0.0) - jnp.log1p(jnp.exp(-jnp.abs(x)))


def _conv_taps(xp, n):
    return [xp[8:] if j == 3 else pltpu.roll(xp, 3 - j, 0)[8:] for j in range(4)]


def _conv_silu(xp, w_ref, b_ref, n):
    taps = _conv_taps(xp, n)
    c = b_ref[...] + sum(w_ref[j:j + 1, :] * taps[j] for j in range(4))
    sg = _sigmoid(c)
    return c, sg, taps


def _chunk_gates(G):
    assert LC == 128
    r = lax.broadcasted_iota(jnp.int32, (LC, LC), 0)
    c = lax.broadcasted_iota(jnp.int32, (LC, LC), 1)
    tril = (c <= r).astype(F32)
    triu = (c >= r).astype(F32)
    b_col = jnp.dot(tril, _logsig(G), preferred_element_type=F32, precision=HI)
    return b_col, b_col.T, G.T, tril, triu


def _colpick(X, lane):
    li = lax.broadcasted_iota(jnp.int32, X.shape, 1)
    return jnp.sum(jnp.where(li == lane, X, 0.0), axis=1, keepdims=True)


def _rowpick(XT, row):
    ri = lax.broadcasted_iota(jnp.int32, XT.shape, 0)
    return jnp.sum(jnp.where(ri == row, XT, 0.0), axis=0, keepdims=True)


def _mlstm_head(qh, kh, vh, G, b_col, b_row, g_row, h, Ch, nh, m_prev):
    bt = _colpick(b_col, 4 + h)
    i_col = _colpick(G, h)
    bs = _rowpick(b_row, 4 + h)
    i_row = _rowpick(g_row, h)
    r = lax.broadcasted_iota(jnp.int32, (LC, LC), 0)
    c = lax.broadcasted_iota(jnp.int32, (LC, LC), 1)
    log_d = jnp.where(c <= r, bt - bs + i_row, NEG)
    log_inter = bt + m_prev
    m_t = jnp.maximum(log_inter, jnp.max(log_d, axis=1, keepdims=True))
    Dm = jnp.exp(log_d - m_t)
    g = jnp.exp(log_inter - m_t)
    qb, kb, vb = _bf(qh), _bf(kh), _bf(vh)
    Am = _dot_nt(qb, kb) * Dm
    qC = _dot(qb, _bf(Ch))
    num = g * qC + _dot(_bf(Am), vb)
    qn = jnp.sum(qh * nh, axis=1, keepdims=True)
    den = g * qn + jnp.sum(Am, axis=1, keepdims=True)
    floor = jnp.exp(-m_t)
    dd = jnp.maximum(jnp.abs(den), floor)
    inv_dd = 1.0 / dd
    hh = num * inv_dd
    lane = lax.broadcasted_iota(jnp.int32, (1, LC), 1)
    blast = jnp.sum(jnp.where(lane == LC - 1, bs, 0.0), axis=1, keepdims=True)
    log_s = blast - bt + i_col
    m_new = jnp.maximum(blast + m_prev, jnp.max(log_s, axis=0, keepdims=True))
    decay = jnp.exp(blast + m_prev - m_new)
    ws = jnp.exp(log_s - m_new)
    kw = kh * ws
    C_new = decay * Ch + _dot_tn(_bf(kw), vb)
    n_new = decay * nh + jnp.sum(kw, axis=0, keepdims=True)
    return dict(Dm=Dm, g=g, Am=Am, qC=qC, qn=qn, den=den, floor=floor, inv_dd=inv_dd, h=hh, decay=decay, ws=ws, kw=kw,
                C_new=C_new, n_new=n_new, m_new=m_new, qb=qb, kb=kb, vb=vb)


def _head_out(hh, mo_h, gn_h):
    r = lax.rsqrt(jnp.mean(hh * hh, axis=-1, keepdims=True) + EPS)
    hn = hh * r
    sg = _sigmoid(mo_h)
    return sg * (hn * gn_h), hn, r, sg


def _mlstm_fwd(mqk, mv, mo, gates, conv_w, conv_b, gate_b, gn, shards, dtypes):
    nblk = S // TB
    ncb = TB // LC
    nw = len(shards)

    def body(*refs):
        x_ref, v_ref, o_ref, g_ref, w_ref, b_ref, gb_ref, gn_ref = refs[:8]
        ins = refs[8:8 + nw]
        out_ref, cs_ref, ns_ref, ms_ref = refs[8 + nw:12 + nw]
        outs = refs[12 + nw:12 + 2 * nw]
        tail, Cst, nst, mst, qs, ks = refs[12 + 2 * nw:18 + 2 * nw]
        bufs = refs[18 + 2 * nw:18 + 3 * nw]
        ag_start, ag_forward, ag_finish = _gather_phases(ins, outs, bufs, *refs[18 + 3 * nw:])
        i = pl.program_id(0)
        pl.when(i == 0)(ag_start)
        pl.when(i == nblk // 2)(ag_forward)

        @pl.when(i == 0)
        def _():
            tail[...] = jnp.zeros_like(tail)
            Cst[...] = jnp.zeros_like(Cst)
            nst[...] = jnp.zeros_like(nst)
            mst[...] = jnp.zeros_like(mst)

        x = x_ref[...]
        xp = jnp.concatenate([tail[...], x], axis=0)
        tail[...] = x[TB - 8:TB, :]
        c, sg, _ = _conv_silu(xp, w_ref, b_ref, TB)
        y = c * sg
        qs[...] = y[:, 0:MW]
        ks[...] = y[:, MW:2 * MW] * (1.0 / math.sqrt(128.0))

        for cc in range(ncb):
            rows = slice(cc * LC, (cc + 1) * LC)
            G = g_ref[rows, :] + gb_ref[...]
            b_col, b_row, g_row, _, _ = _chunk_gates(G)
            cs_ref[cc] = Cst[...]
            ns_ref[cc] = nst[...]
            ms_ref[cc] = mst[...]
            for h in range(4):
                ln = slice(h * 128, (h + 1) * 128)
                m_prev = jnp.max(mst[0:1, ln], axis=1, keepdims=True)
                f = _mlstm_head(qs[rows, ln], ks[rows, ln], v_ref[rows, ln], G, b_col, b_row, g_row, h,
                                Cst[:, ln], nst[0:1, ln], m_prev)
                out, _, _, _ = _head_out(f["h"], o_ref[rows, ln], gn_ref[:, ln])
                out_ref[rows, ln] = out
                Cst[:, ln] = f["C_new"]
                nst[0:1, ln] = f["n_new"]
                mst[0:1, ln] = jnp.broadcast_to(f["m_new"], (1, 128))
        pl.when(i == nblk - 1)(ag_finish)

    row = lambda wd: pl.BlockSpec((TB, wd), lambda i: (i, 0))
    res = pl.pallas_call(
        body, name="mlstm_fwd", grid=(nblk,),
        in_specs=[row(1024), row(MW), row(MW), row(128), _cspec((4, 1024)), _cspec((1, 1024)), _cspec((1, 128)),
                  _cspec((1, MW))] + [VM] * nw,
        out_specs=[row(MW), pl.BlockSpec((ncb, 128, MW), lambda i: (i, 0, 0)),
                   pl.BlockSpec((ncb, 8, MW), lambda i: (i, 0, 0)), pl.BlockSpec((ncb, 8, MW), lambda i: (i, 0, 0))]
        + [ANY] * nw,
        out_shape=[jax.ShapeDtypeStruct((S, MW), F32), jax.ShapeDtypeStruct((S // LC, 128, MW), F32),
                   jax.ShapeDtypeStruct((S // LC, 8, MW), F32), jax.ShapeDtypeStruct((S // LC, 8, MW), F32)]
        + _gather_shapes(shards, dtypes),
        scratch_shapes=[pltpu.VMEM((8, 1024), F32), pltpu.VMEM((128, MW), F32), pltpu.VMEM((8, MW), F32),
                        pltpu.VMEM((8, MW), F32), pltpu.VMEM((TB, MW), F32), pltpu.VMEM((TB, MW), F32)]
        + _gather_scratch(shards, dtypes),
        compiler_params=_params(1),
    )(mqk, mv, mo, gates, conv_w, conv_b, gate_b, gn, *shards)
    return res[0], res[1], res[2], res[3], res[4:]


DM_V, DM_O, DM_G, DM_W = 1024, 1536, 2048, PW - 3 * AW


def _mlstm_bwd(mqk, mv, mo, gates, conv_w, conv_b, gate_b, gn, cs, ns, ms, dout, parts):
    nblk = S // TB
    ncb = TB // LC
    kscale = 1.0 / math.sqrt(128.0)
    nw = len(parts)

    def body(*refs):
        x_ref, xprev_ref, v_ref, o_ref, g_ref, w_ref, b_ref, gb_ref, gn_ref, cs_ref, ns_ref, ms_ref, do_ref = refs[:13]
        ins = refs[13:13 + nw]
        dm_ref, dw_ref, db_ref, dgn_ref, dgb_ref = refs[13 + nw:18 + nw]
        outs = refs[18 + nw:18 + 2 * nw]
        dCst, dnst, dyhead, qs, ks, dqk = refs[18 + 2 * nw:24 + 2 * nw]
        rs_start, rs_finish = _scatter_phases(ins, outs, *refs[24 + 2 * nw:])
        i = pl.program_id(0)
        blk = nblk - 1 - i
        pl.when(i == 0)(rs_start)

        @pl.when(i == 0)
        def _():
            dCst[...] = jnp.zeros_like(dCst)
            dnst[...] = jnp.zeros_like(dnst)
            dyhead[...] = jnp.zeros_like(dyhead)
            dw_ref[...] = jnp.zeros_like(dw_ref)
            db_ref[...] = jnp.zeros_like(db_ref)
            dgn_ref[...] = jnp.zeros_like(dgn_ref)
            dgb_ref[...] = jnp.zeros_like(dgb_ref)

        x = x_ref[...]
        xprev = jnp.where(blk == 0, 0.0, xprev_ref[...])
        xp = jnp.concatenate([xprev, x], axis=0)
        c, sg, taps = _conv_silu(xp, w_ref, b_ref, TB)
        y = c * sg
        qs[...] = y[:, 0:MW]
        ks[...] = y[:, MW:2 * MW] * kscale
        lane128 = lax.broadcasted_iota(jnp.int32, (LC, 128), 1)
        rowi = lax.broadcasted_iota(jnp.int32, (LC, 1), 0)
        ones = jnp.ones((LC, 128), F32)

        for cc in reversed(range(ncb)):
            rows = slice(cc * LC, (cc + 1) * LC)
            G = g_ref[rows, :] + gb_ref[...]
            b_col, b_row, g_row, _, triu = _chunk_gates(G)
            dB = jnp.zeros((LC, 128), F32)
            dI = jnp.zeros((LC, 128), F32)
            for h in range(4):
                ln = slice(h * 128, (h + 1) * 128)
                Ch = cs_ref[cc, :, ln]
                nh = ns_ref[cc, 0:1, ln]
                m_prev = jnp.max(ms_ref[cc, 0:1, ln], axis=1, keepdims=True)
                qh, kh, vh = qs[rows, ln], ks[rows, ln], v_ref[rows, ln]
                f = _mlstm_head(qh, kh, vh, G, b_col, b_row, g_row, h, Ch, nh, m_prev)
                hh, inv_dd, den, g, Am, Dm = f["h"], f["inv_dd"], f["den"], f["g"], f["Am"], f["Dm"]
                qb, kb, vb = f["qb"], f["kb"], f["vb"]
                gn_h = gn_ref[:, ln]
                _, hn, r, sgo = _head_out(hh, o_ref[rows, ln], gn_h)
                do = do_ref[rows, ln]
                hm = hn * gn_h
                dm_ref[rows, DM_O + h * 128:DM_O + (h + 1) * 128] = _bf(do * hm * sgo * (1.0 - sgo))
                dhm = do * sgo
                dgn_ref[:, ln] = dgn_ref[:, ln] + jnp.sum(dhm * hn, axis=0, keepdims=True)
                dhn = dhm * gn_h
                dh = r * (dhn - hn * jnp.mean(dhn * hn, axis=-1, keepdims=True))
                dnum = dh * inv_dd
                ddd = -jnp.sum(dh * hh, axis=1, keepdims=True) * inv_dd
                dden = jnp.where(jnp.abs(den) >= f["floor"], ddd * jnp.sign(den), 0.0)
                dnb = _bf(dnum)
                dA = _dot_nt(dnb, vb) + dden
                dv = _dot_tn(_bf(Am), dnb)
                gd = _bf(g * dnum)
                gq = g * dden
                dq = _dot_nt(gd, _bf(Ch)) + gq * nh
                dCn = dCst[:, ln]
                dnn = dnst[0:1, ln]
                dC = f["decay"] * dCn + _dot_tn(qb, gd)
                dn = f["decay"] * dnn + jnp.sum(gq * qh, axis=0, keepdims=True)
                dg = jnp.sum(dnum * f["qC"], axis=1, keepdims=True) + dden * f["qn"]
                dS = _bf(dA * Dm)
                dq = dq + _dot(dS, kb)
                dk = _dot_tn(dS, qb)
                Gm = dA * Am
                gam = dg * g
                dCb = _bf(dCn)
                E = _dot_nt(vb, dCb) + dnn
                ws = f["ws"]
                dk = dk + ws * E
                om = jnp.sum(E * kh, axis=1, keepdims=True) * ws
                dv = dv + _dot(_bf(f["kw"]), dCb)
                ddecay = (jnp.sum(jnp.sum(dCn * Ch, axis=1, keepdims=True), axis=0, keepdims=True)
                          + jnp.sum(dnn * nh, axis=1, keepdims=True))
                delta = ddecay * f["decay"]
                rows_g = jnp.sum(Gm, axis=1, keepdims=True)
                cols_g = jnp.broadcast_to(jnp.sum(Gm, axis=0, keepdims=True), (LC, 128)).T
                last = jnp.where(rowi == LC - 1, jnp.sum(om, axis=0, keepdims=True) + delta, 0.0)
                db = rows_g + gam - om + last - cols_g
                di = cols_g + om
                dB = jnp.where(lane128 == 4 + h, db, dB)
                dI = jnp.where(lane128 == h, di, dI)
                dCst[:, ln] = dC
                dnst[0:1, ln] = dn
                dqk[rows, ln] = dq
                dqk[rows, MW + h * 128:MW + (h + 1) * 128] = dk * kscale
                dm_ref[rows, DM_V + h * 128:DM_V + (h + 1) * 128] = _bf(dv)
            dlogf = jnp.dot(triu, dB, preferred_element_type=F32, precision=HI)
            dG = dI + dlogf * _sigmoid(-G)
            dG = jnp.where(lane128 < 8, dG, 0.0)
            dm_ref[rows, DM_G:DM_G + 128] = _bf(dG)
            dm_ref[rows, DM_G + 128:DM_W] = jnp.zeros((LC, DM_W - DM_G - 128), BF16)
            dgb_ref[...] = dgb_ref[...] + jnp.sum(dG, axis=0, keepdims=True)

        dy = dqk[...] * (sg * (1.0 + c * (1.0 - sg)))
        db_ref[...] = db_ref[...] + jnp.sum(dy, axis=0, keepdims=True)
        for j in range(4):
            dw_ref[j:j + 1, :] = dw_ref[j:j + 1, :] + jnp.sum(dy * taps[j], axis=0, keepdims=True)
        dyp = jnp.concatenate([dy, dyhead[...]], axis=0)
        dx = w_ref[3:4, :] * dy
        for j in range(3):
            dx = dx + w_ref[j:j + 1, :] * pltpu.roll(dyp, TB + 8 - (3 - j), 0)[0:TB]
        dm_ref[:, 0:DM_V] = _bf(dx)
        dyhead[...] = dy[0:8, :]
        pl.when(i == nblk - 1)(rs_finish)

    rrow = lambda wd: pl.BlockSpec((TB, wd), lambda i: (nblk - 1 - i, 0))
    st = lambda r: pl.BlockSpec((ncb, r, MW), lambda i: (nblk - 1 - i, 0, 0))
    prev8 = pl.BlockSpec((8, 1024), lambda i: (jnp.maximum((nblk - 1 - i) * (TB // 8) - 1, 0), 0))
    res = pl.pallas_call(
        body, name="mlstm_bwd", grid=(nblk,),
        in_specs=[rrow(1024), prev8, rrow(MW), rrow(MW), rrow(128), _cspec((4, 1024)), _cspec((1, 1024)),
                  _cspec((1, 128)), _cspec((1, MW)), st(128), st(8), st(8), rrow(MW)] + [ANY] * nw,
        out_specs=[rrow(DM_W),
                   pl.BlockSpec((4, 1024), lambda i: (0, 0)), pl.BlockSpec((1, 1024), lambda i: (0, 0)),
                   pl.BlockSpec((1, MW), lambda i: (0, 0)), pl.BlockSpec((1, 128), lambda i: (0, 0))] + [ANY] * nw,
        out_shape=[jax.ShapeDtypeStruct((S, DM_W), BF16),
                   jax.ShapeDtypeStruct((4, 1024), F32), jax.ShapeDtypeStruct((1, 1024), F32),
                   jax.ShapeDtypeStruct((1, MW), F32), jax.ShapeDtypeStruct((1, 128), F32)]
        + [jax.ShapeDtypeStruct(a.shape, a.dtype) for a in parts],
        scratch_shapes=[pltpu.VMEM((128, MW), F32), pltpu.VMEM((8, MW), F32), pltpu.VMEM((8, 1024), F32),
                        pltpu.VMEM((TB, MW), F32), pltpu.VMEM((TB, MW), F32), pltpu.VMEM((TB, 1024), F32)]
        + _scatter_scratch(nw),
        compiler_params=_params(1),
    )(mqk, mqk, mv, mo, gates, conv_w, conv_b, gate_b, gn, cs, ns, ms, dout, *parts)
    return res[:5], res[5:]


def _out_proj(x, attn, ml, w, g):
    tm = TM

    def body(x_ref, a_ref, m_ref, w_ref, g_ref, h_ref, u_ref):
        h1 = x_ref[...] + _dot(_bf(a_ref[...]), w_ref[0:AW, :]) + _dot(_bf(m_ref[...]), w_ref[AW:D, :])
        h_ref[...] = h1
        n, _ = _rms(h1)
        u_ref[...] = _bf(n * g_ref[...])

    row = lambda wd: pl.BlockSpec((tm, wd), lambda i: (i, 0))
    return pl.pallas_call(
        body, name="out_proj", grid=(S // tm,),
        in_specs=[row(D), row(AW), row(MW), _cspec((D, D)), _cspec((1, D))],
        out_specs=[row(D), row(D)],
        out_shape=[jax.ShapeDtypeStruct((S, D), F32), jax.ShapeDtypeStruct((S, D), BF16)],
        compiler_params=_params(1),
    )(x, attn, ml, w, g)


HALF = DFF // NDEV // 2


def _mlp_fwd(h1, u2, w_up, w_down_a, w_down_b):
    tm = TM

    def body(h_ref, u_ref, wu_ref, wa_ref, wb_ref, a_ref, o_ref):
        u = u_ref[...]
        acc = h_ref[...]
        for c in range(NDEV):
            cols = slice(c * 512, (c + 1) * 512)
            a = _dot(u, wu_ref[c])
            a_ref[:, cols] = _bf(a)
            r = jnp.maximum(a, 0.0)
            r = _bf(r * r)
            acc = acc + _dot(r[:, 0:HALF], wa_ref[c]) + _dot(r[:, HALF:2 * HALF], wb_ref[c])
        o_ref[...] = acc

    row = lambda wd: pl.BlockSpec((tm, wd), lambda i: (i, 0))
    return pl.pallas_call(
        body, name="mlp_fwd", grid=(S // tm,),
        in_specs=[row(D), row(D), _cspec((NDEV, D, DFF // NDEV)), _cspec((NDEV, HALF, D)), _cspec((NDEV, HALF, D))],
        out_specs=[row(DFF), row(D)],
        out_shape=[jax.ShapeDtypeStruct((S, DFF), BF16), jax.ShapeDtypeStruct((S, D), F32)],
        compiler_params=_params(1),
    )(h1, u2, w_up, w_down_a, w_down_b)


def _ple_loss(h2, p, target, w_pg, w_ple, g_ple, g_fin):
    tm = TM

    def body(h_ref, p_ref, t_ref, wg_ref, wp_ref, gp_ref, gf_ref,
             dh_ref, dwg_ref, dwp_ref, dgp_ref, dgf_ref, loss_ref, acc_g, acc_p):
        i = pl.program_id(0)

        @pl.when(i == 0)
        def _():
            acc_g[...] = jnp.zeros_like(acc_g)
            acc_p[...] = jnp.zeros_like(acc_p)
            dgp_ref[...] = jnp.zeros_like(dgp_ref)
            dgf_ref[...] = jnp.zeros_like(dgf_ref)
            loss_ref[...] = jnp.zeros_like(loss_ref)

        h2v = h_ref[...]
        n2, rs2 = _rms(h2v)
        u3 = _bf(n2 * gp_ref[...])
        gt = _sigmoid(_dot(u3, wg_ref[...]))
        pb = _bf(p_ref[...])
        e = jnp.concatenate([_dot(pb, wp_ref[j]) for j in range(NDEV)], axis=1)
        h3 = h2v + gt * e
        n3, rs3 = _rms(h3)
        err = n3 * gf_ref[...] - t_ref[...]
        loss_ref[...] = loss_ref[...] + 0.5 / D * jnp.sum(jnp.sum(err * err, axis=1, keepdims=True), axis=0, keepdims=True)
        dy = err * (1.0 / D)
        dgf_ref[...] = dgf_ref[...] + jnp.sum(dy * n3, axis=0, keepdims=True)
        dh3 = _rms_bwd(dy, n3, rs3, gf_ref[...])
        de = _bf(dh3 * gt)
        dz = _bf(dh3 * e * gt * (1.0 - gt))
        acc_p[...] = acc_p[...] + _dot_tn(pb, de)
        acc_g[...] = acc_g[...] + _dot_tn(u3, dz)
        du3 = _dot_nt(dz, wg_ref[...])
        dgp_ref[...] = dgp_ref[...] + jnp.sum(du3 * n2, axis=0, keepdims=True)
        dh_ref[...] = dh3 + _rms_bwd(du3, n2, rs2, gp_ref[...])

        @pl.when(i == S // tm - 1)
        def _():
            dwg_ref[...] = _bf(acc_g[...])
            for j in range(NDEV):
                dwp_ref[j] = _bf(acc_p[:, j * 128:(j + 1) * 128])

    row = lambda wd: pl.BlockSpec((tm, wd), lambda i: (i, 0))
    whole = lambda shp: pl.BlockSpec(shp, lambda i: (0,) * len(shp))
    return pl.pallas_call(
        body, name="ple_loss", grid=(S // tm,),
        in_specs=[row(D), row(PLE), row(D), _cspec((D, D)), _cspec((NDEV, PLE, 128)), _cspec((1, D)), _cspec((1, D))],
        out_specs=[row(D), whole((D, D)), whole((NDEV, PLE, 128)), whole((1, D)), whole((1, D)), whole((1, 1))],
        out_shape=[jax.ShapeDtypeStruct((S, D), F32), jax.ShapeDtypeStruct((D, D), BF16),
                   jax.ShapeDtypeStruct((NDEV, PLE, 128), BF16), jax.ShapeDtypeStruct((1, D), F32),
                   jax.ShapeDtypeStruct((1, D), F32), jax.ShapeDtypeStruct((1, 1), F32)],
        scratch_shapes=[pltpu.VMEM((D, D), F32), pltpu.VMEM((PLE, D), F32)],
        compiler_params=_params(1),
    )(h2, p, target, w_pg, w_ple, g_ple, g_fin)


def _mlp_bwd(dh2, a, h1, g, w_up, w_down_a, w_down_b):
    tm = TM

    def body(d_ref, a_ref, h_ref, g_ref, wu_ref, wa_ref, wb_ref, da_ref, dh1_ref, dg_ref):
        @pl.when(pl.program_id(0) == 0)
        def _():
            dg_ref[...] = jnp.zeros_like(dg_ref)

        dh2v = d_ref[...]
        db = _bf(dh2v)
        du = jnp.zeros((tm, D), F32)
        for c in range(NDEV):
            cols = slice(c * 512, (c + 1) * 512)
            dr = jnp.concatenate([_dot_nt(db, wa_ref[c]), _dot_nt(db, wb_ref[c])], axis=1)
            da = _bf(dr * (2.0 * jnp.maximum(a_ref[:, cols], 0.0)))
            da_ref[:, cols] = da
            du = du + _dot_nt(da, wu_ref[c])
        n, rs = _rms(h_ref[...])
        dg_ref[...] = dg_ref[...] + jnp.sum(du * n, axis=0, keepdims=True)
        dh1_ref[...] = dh2v + _rms_bwd(du, n, rs, g_ref[...])

    row = lambda wd: pl.BlockSpec((tm, wd), lambda i: (i, 0))
    return pl.pallas_call(
        body, name="mlp_bwd", grid=(S // tm,),
        in_specs=[row(D), row(DFF), row(D), _cspec((1, D)), _cspec((NDEV, D, DFF // NDEV)), _cspec((NDEV, HALF, D)),
                  _cspec((NDEV, HALF, D))],
        out_specs=[row(DFF), row(D), pl.BlockSpec((1, D), lambda i: (0, 0))],
        out_shape=[jax.ShapeDtypeStruct((S, DFF), BF16), jax.ShapeDtypeStruct((S, D), F32),
                   jax.ShapeDtypeStruct((1, D), F32)],
        compiler_params=_params(1),
    )(dh2, a, h1, g, w_up, w_down_a, w_down_b)


def _out_proj_bwd(dh1, attn, ml, w):
    tm = TM

    def body(d_ref, a_ref, m_ref, w_ref, da_ref, dm_ref, dw_ref, acc):
        i = pl.program_id(0)

        @pl.when(i == 0)
        def _():
            acc[...] = jnp.zeros_like(acc)

        db = _bf(d_ref[...])
        dmix = _dot_nt(db, w_ref[...])
        da_ref[...] = dmix[:, 0:AW]
        dm_ref[...] = dmix[:, AW:D]
        acc[0:AW, :] = acc[0:AW, :] + _dot_tn(_bf(a_ref[...]), db)
        acc[AW:D, :] = acc[AW:D, :] + _dot_tn(_bf(m_ref[...]), db)

        @pl.when(i == S // tm - 1)
        def _():
            dw_ref[...] = _bf(acc[...])

    row = lambda wd: pl.BlockSpec((tm, wd), lambda i: (i, 0))
    return pl.pallas_call(
        body, name="out_proj_bwd", grid=(S // tm,),
        in_specs=[row(D), row(AW), row(MW), _cspec((D, D))],
        out_specs=[row(AW), row(MW), pl.BlockSpec((D, D), lambda i: (0, 0))],
        out_shape=[jax.ShapeDtypeStruct((S, AW), F32), jax.ShapeDtypeStruct((S, MW), F32),
                   jax.ShapeDtypeStruct((D, D), BF16)],
        scratch_shapes=[pltpu.VMEM((D, D), F32)],
        compiler_params=_params(1),
    )(dh1, attn, ml, w)


CHIP_FLIPS = [(0, 0), (0, 1), (1, 0), (1, 1)]


def _scatter2_phases(in_ref, out_ref, mine_v, sib_v, psum_v, loc_sems, d2d_send, d2d_recv, ici_send, ici_recv, own_sem):
    x, y, c = _place()
    chips = [((x + dx) % 2, (y + dy) % 2) for dx, dy in CHIP_FLIPS]
    nc = len(chips)

    def local(k):
        return pltpu.make_async_copy(in_ref.at[_dev_index(*chips[k], c)], mine_v.at[k], loc_sems.at[k])

    def to_sib(k):
        return pltpu.make_async_remote_copy(
            src_ref=in_ref.at[_dev_index(*chips[k], 1 - c)], dst_ref=sib_v.at[k], send_sem=d2d_send.at[k],
            recv_sem=d2d_recv.at[k], device_id=(x, y, 1 - c), device_id_type=MESH)

    def over_ici(k):
        return pltpu.make_async_remote_copy(
            src_ref=psum_v.at[k], dst_ref=out_ref.at[k], send_sem=ici_send.at[k - 1], recv_sem=ici_recv.at[k - 1],
            device_id=(*chips[k], c), device_id_type=MESH)

    def own():
        return pltpu.make_async_copy(psum_v.at[0], out_ref.at[0], own_sem)

    def start():
        for k in range(nc):
            to_sib(k).start()
            local(k).start()

    def middle():
        for k in (1, 2, 3, 0):
            local(k).wait()
            to_sib(k).wait_recv()
            psum_v[k] = _bf(mine_v[k].astype(F32) + sib_v[k].astype(F32))
            (over_ici(k) if k else own()).start()

    def finish():
        for k in range(1, nc):
            over_ici(k).wait()
        for k in range(nc):
            to_sib(k).wait_send()
        own().wait()

    return start, middle, finish


def _scatter2_scratch(shard, dtype):
    nc = len(CHIP_FLIPS)
    return ([pltpu.VMEM((nc, *shard), dtype)] * 3
            + [pltpu.SemaphoreType.DMA((nc,))] * 3 + [pltpu.SemaphoreType.DMA((nc - 1,))] * 2 + [pltpu.SemaphoreType.DMA])


def _in_proj_bwd(dparts, n_roped, rope, dh1, x, g1, w, part):
    tm = TM
    nt = S // tm
    widths = [d.shape[1] for d in dparts]
    assert sum(widths) == PW
    npar = len(dparts)

    def body(*refs):
        d_refs = refs[:npar]
        tabs = [t[...] for t in refs[npar:npar + 3]]
        dh_ref, x_ref, g_ref, w_ref, in_ref, dx_ref, dg_ref, out_ref = refs[npar + 3:npar + 11]
        rs_start, rs_middle, rs_finish = _scatter2_phases(in_ref, out_ref, *refs[npar + 11:])
        i = pl.program_id(0)
        pl.when(i == 0)(rs_start)
        pl.when(i == 1)(rs_middle)

        @pl.when(i == 0)
        def _():
            dg_ref[...] = jnp.zeros_like(dg_ref)

        du = jnp.zeros((tm, D), F32)
        off = 0
        for j, (d_ref, wd) in enumerate(zip(d_refs, widths)):
            nc = next(c for c in (768, 512) if wd % c == 0)
            for s in range(wd // nc):
                d = d_ref[:, s * nc:(s + 1) * nc]
                du = du + _dot_nt(_unrope(d, *tabs) if j < n_roped else d, w_ref[:, off + s * nc:off + (s + 1) * nc])
            off += wd
        n, rs = _rms(x_ref[...])
        dg_ref[...] = dg_ref[...] + jnp.sum(du * n, axis=0, keepdims=True)
        dx_ref[...] = dh_ref[...] + _rms_bwd(du, n, rs, g_ref[...])
        pl.when(i == nt - 1)(rs_finish)

    row = lambda wd: pl.BlockSpec((tm, wd), lambda i: (i, 0))
    shard = part.shape[1:]
    return pl.pallas_call(
        body, name="in_proj_bwd", grid=(nt,),
        in_specs=[row(wd) for wd in widths] + [row(128)] * 3 + [row(D), row(D), _cspec((1, D)), _cspec((D, PW)), ANY],
        out_specs=[row(D), pl.BlockSpec((1, D), lambda i: (0, 0)), ANY],
        out_shape=[jax.ShapeDtypeStruct((S, D), F32), jax.ShapeDtypeStruct((1, D), F32),
                   jax.ShapeDtypeStruct((len(CHIP_FLIPS), *shard), part.dtype)],
        scratch_shapes=_scatter2_scratch(shard, part.dtype),
        compiler_params=_params(1),
    )(*dparts, *rope, dh1, x, g1, w, part)


SMALL_ROWS = 96


def _small_phases(ins, out_ref, pack, rbuf, send_sems, recv_sems):
    x, y, c = _place()
    me = _dev_index(x, y, c)

    def copies():
        out = []
        for k, (dx, dy, dc) in enumerate(FLIPS):
            peer = ((x + dx) % 2, (y + dy) % 2, (c + dc) % 2)
            out.append(pltpu.make_async_remote_copy(
                src_ref=pack, dst_ref=rbuf.at[me], send_sem=send_sems.at[k], recv_sem=recv_sems.at[k],
                device_id=peer, device_id_type=MESH))
        return out

    def start():
        pack[...] = jnp.zeros_like(pack)
        for i, ref in enumerate(ins):
            pack[8 * i:8 * i + 1, 0:ref.shape[1]] = ref[...]
        rbuf[me] = pack[...]
        for cp in copies():
            cp.start()

    def finish():
        for cp in copies():
            cp.wait()
        tot = rbuf[0]
        for j in range(1, NDEV):
            tot = tot + rbuf[j]
        out_ref[...] = tot

    return start, finish


def _wgrad(name, A, Bs, a_fn, b_fn, out_shape, split=None, ts=512, small=(), rope=(), n_roped=0):
    K = A.shape[1]
    widths = [b.shape[1] for b in Bs]
    N = sum(widths)
    nb, ns, nrt = len(Bs) + len(rope), len(small), S // ts
    kc = min(K, 1024)

    def body(*refs):
        a_ref, b_refs = refs[0], refs[1:1 + len(Bs)]
        tabs = [t[...] for t in refs[1 + len(Bs):1 + nb]]
        o_ref = refs[1 + nb + ns]
        acc = refs[2 + nb + ns + bool(ns)]
        r = pl.program_id(0)
        if ns:
            sm_start, sm_finish = _small_phases(refs[1 + nb:1 + nb + ns], refs[2 + nb + ns], *refs[4 + nb + ns:])
            pl.when(r == 0)(sm_start)

        @pl.when(r == 0)
        def _():
            acc[...] = jnp.zeros_like(acc)

        bs, off = [], 0
        for i, (b_ref, w) in enumerate(zip(b_refs, widths)):
            nc = next(c for c in (1024, 768, 512) if w % c == 0)
            fn = (lambda t: _unrope(t, *tabs)) if i < n_roped else b_fn
            bs += [(off + c * nc, nc, fn(b_ref[:, c * nc:(c + 1) * nc])) for c in range(w // nc)]
            off += w
        for kk in range(K // kc):
            rows = slice(kk * kc, (kk + 1) * kc)
            at = a_fn(a_ref[:, rows]).T
            for lo, nc, b in bs:
                acc[rows, lo:lo + nc] = acc[rows, lo:lo + nc] + _dot(at, b)

        @pl.when(r == nrt - 1)
        def _():
            if split is None:
                o_ref[...] = _bf(acc[...])
            else:
                for j in range(NDEV):
                    o_ref[j] = _bf(acc[:, split * j:split * (j + 1)])

        if ns:
            pl.when(r == nrt - 1)(sm_finish)

    in_specs = ([pl.BlockSpec((ts, K), lambda r: (r, 0))] + [pl.BlockSpec((ts, w), lambda r: (r, 0)) for w in widths]
                + [pl.BlockSpec((ts, 128), lambda r: (r, 0))] * len(rope))
    out_spec = pl.BlockSpec(out_shape, lambda r: (0,) * len(out_shape))
    scratch = [pltpu.VMEM((K, N), F32)]
    if not ns:
        return pl.pallas_call(
            body, name=name, grid=(nrt,), in_specs=in_specs, out_specs=out_spec,
            out_shape=jax.ShapeDtypeStruct(out_shape, BF16), scratch_shapes=scratch, compiler_params=_params(1),
        )(A, *Bs, *rope)
    return pl.pallas_call(
        body, name=name, grid=(nrt,), in_specs=in_specs + [VM] * ns, out_specs=[out_spec, VM],
        out_shape=[jax.ShapeDtypeStruct(out_shape, BF16), jax.ShapeDtypeStruct((SMALL_ROWS, 1024), F32)],
        scratch_shapes=scratch + [pltpu.VMEM((SMALL_ROWS, 1024), F32), pltpu.VMEM((NDEV, SMALL_ROWS, 1024), F32),
                                  pltpu.SemaphoreType.DMA((7,)), pltpu.SemaphoreType.DMA((7,))],
        compiler_params=_params(1),
    )(A, *Bs, *rope, *small)


def _relu2_bf(a):
    r = jnp.maximum(a.astype(F32), 0.0)
    return _bf(r * r)


def _ident(a):
    return a


def _step(x, p, target, g1, conv_b, gate_b, gn, g_mlp, g_ple, g_fin, sh):
    (g_in, g_conv), (rc, ra, rb) = _gather_weights([sh["w_in"], sh["conv_w"]], [BF16, F32])
    conv_w = g_conv.transpose(1, 0, 2).reshape(4, 1024)
    w_in_p = _join_w_in(g_in)
    (qkv, mqk, mv, mo, gates, u1), (w_out8, w_pg8, w_ple8) = _in_proj(
        x, g1, w_in_p, rc, ra, rb, [sh["w_out"], sh["w_ple_gate"], sh["w_ple"]], [BF16] * 3)
    attn, lse, (w_up8, w_down_a) = _attn_fwd(qkv, [sh["w_up"], sh["w_down"][0:HALF]], [BF16] * 2)
    ml, cs, ns, ms, (w_down_b,) = _mlstm_fwd(mqk, mv, mo, gates, conv_w, conv_b, gate_b, gn,
                                             [sh["w_down"][HALF:2 * HALF]], [BF16])
    w_out, w_pg = w_out8.reshape(D, D), w_pg8.reshape(D, D)
    h1, u2 = _out_proj(x, attn, ml, w_out, g_mlp)
    a, h2 = _mlp_fwd(h1, u2, w_up8, w_down_a, w_down_b)
    dh2, dw_pg, dw_ple8, dg_ple, dg_fin, loss = _ple_loss(h2, p, target, w_pg, w_ple8, g_ple, g_fin)
    da, dh1, dg_mlp = _mlp_bwd(dh2, a, h1, g_mlp, w_up8, w_down_a, w_down_b)
    dw_up8 = _wgrad("wgrad_up", u2, [da], _ident, _ident, (NDEV, D, DFF // NDEV), split=DFF // NDEV)
    dw_down = _wgrad("wgrad_down", a, [dh2], _relu2_bf, _bf, (DFF, D))
    d_attn, d_ml, dw_out = _out_proj_bwd(dh1, attn, ml, w_out)
    (dm, dconv_w, dconv_b, dgn, dgate_b), (r_out, r_pg, r_ple) = _mlstm_bwd(
        mqk, mv, mo, gates, conv_w, conv_b, gate_b, gn, cs, ns, ms, d_ml,
        [dw_out.reshape(NDEV, D // NDEV, D), dw_pg.reshape(NDEV, D // NDEV, D), dw_ple8])
    dq, dk, dv, (r_up, r_down) = _attn_bwd(qkv, attn, lse, d_attn, [dw_up8, dw_down.reshape(NDEV, DFF // NDEV, D)])
    dparts = [dq, dk, dv, dm]
    small = [jnp.zeros((1, D), F32), dconv_b, dgate_b, dgn, dg_mlp, dg_ple, dg_fin, loss]
    dw_in8, total = _wgrad("wgrad_in", u1, dparts, _ident, _ident, (NDEV, D, IN_W // NDEV), split=IN_W // NDEV,
                           small=small + [dconv_w[j:j + 1] for j in range(4)], rope=(rc, ra, rb), n_roped=2)
    dx, dg1, r_in = _in_proj_bwd(dparts, 2, (rc, ra, rb), dh1, x, g1, w_in_p, dw_in8)
    recv = dict(w_in=r_in, w_out=r_out, w_up=r_up, w_down=r_down, w_ple_gate=r_pg, w_ple=r_ple)
    return dx, recv, total, _allreduce_vec(dg1)


def _gather_weights(shards, dtypes):
    nw = len(shards)

    def body(*refs):
        ins, parts = refs[:nw], refs[nw:nw + 4]
        outs, tables = refs[nw + 4:2 * nw + 4], refs[2 * nw + 4:2 * nw + 7]
        start, forward, finish = _gather_phases(ins, outs, refs[2 * nw + 7:3 * nw + 7], *refs[3 * nw + 7:])
        start()
        _rope_fill(*parts, *tables)
        forward()
        finish()

    res = pl.pallas_call(
        body, name="gather_weights",
        in_specs=[VM] * (nw + 4), out_specs=[ANY] * nw + [VM] * 3,
        out_shape=_gather_shapes(shards, dtypes) + [jax.ShapeDtypeStruct((S, 128), F32)] * 3,
        scratch_shapes=_gather_scratch(shards, dtypes),
        compiler_params=_params(),
    )(*shards, *_rope_parts())
    return res[:nw], res[nw:]


def _allreduce_vec(v):
    def body(v_ref, out_ref, pack, rbuf, send_sems, recv_sems):
        start, finish = _small_phases([v_ref], out_ref, pack, rbuf, send_sems, recv_sems)
        start()
        finish()

    return pl.pallas_call(
        body, name="allreduce_last", out_shape=jax.ShapeDtypeStruct((8, 1024), F32),
        scratch_shapes=[pltpu.VMEM((8, 1024), F32), pltpu.VMEM((NDEV, 8, 1024), F32),
                        pltpu.SemaphoreType.DMA((7,)), pltpu.SemaphoreType.DMA((7,))],
        compiler_params=_params(),
    )(v)


def _adamw(name, gparts, w, m, v, tr):
    P, R, C = gparts.shape

    def body(g_ref, w_ref, m_ref, v_ref, go_ref, d_ref, mo_ref, vo_ref):
        g = g_ref[0].astype(F32)
        for j in range(1, P):
            g = g + g_ref[j].astype(F32)
        go_ref[...] = g
        d_ref[...], mo_ref[...], vo_ref[...] = _adam_update(g, w_ref[...], m_ref[...], v_ref[...])

    row = pl.BlockSpec((tr, C), lambda i: (i, 0))
    return pl.pallas_call(
        body, name=name, grid=(R // tr,),
        in_specs=[pl.BlockSpec((P, tr, C), lambda i: (0, i, 0)), row, row, row],
        out_specs=[row] * 4,
        out_shape=[jax.ShapeDtypeStruct((R, C), F32)] * 4,
        compiler_params=_params(1),
    )(gparts, w, m, v)


SMALL = ("norm_mix_g", "conv_b", "gate_b", "mlstm_norm_g", "norm_mlp_g", "norm_ple_g", "final_norm_g")


def _adam_update(g, w, m, v):
    c1 = 1.0 - ADAM_B1 ** ADAM_STEP
    c2 = 1.0 - ADAM_B2 ** ADAM_STEP
    m2 = ADAM_B1 * m + (1.0 - ADAM_B1) * g
    v2 = ADAM_B2 * v + (1.0 - ADAM_B2) * (g * g)
    return -ADAM_LR * ((m2 / c1) / (jnp.sqrt(v2 / c2) + ADAM_EPS) + ADAM_WD * w), m2, v2


def _adamw_small(total, first, ws, ms, vs):
    n = len(ws)

    def body(*refs):
        t_ref, f_ref = refs[:2]
        refs = refs[1:]
        outs = refs[1 + 3 * n:]
        for i in range(n):
            w_ref, m_ref, v_ref = refs[1 + i], refs[1 + n + i], refs[1 + 2 * n + i]
            g = (t_ref if i else f_ref)[8 * i:8 * i + 1, 0:w_ref.shape[1]]
            delta, m2, v2 = _adam_update(g, w_ref[...], m_ref[...], v_ref[...])
            for ref, val in zip(outs[4 * i:4 * i + 4], (g, delta, m2, v2)):
                ref[...] = val

    res = pl.pallas_call(
        body, name="adamw_small",
        out_shape=[jax.ShapeDtypeStruct(w.shape, F32) for w in ws for _ in range(4)],
        compiler_params=_params(),
    )(total, first, *ws, *ms, *vs)
    return [res[4 * i:4 * i + 4] for i in range(n)]


def kernel(x, p, norm_mix_g, w_in, conv_w, conv_b, gate_b, mlstm_norm_g, w_out, norm_mlp_g, w_up, w_down, norm_ple_g, w_ple_gate, w_ple, final_norm_g, loss_target, m_norm_mix_g, m_w_in, m_conv_w, m_conv_b, m_gate_b, m_mlstm_norm_g, m_w_out, m_norm_mlp_g, m_w_up, m_w_down, m_norm_ple_g, m_w_ple_gate, m_w_ple, m_final_norm_g, v_norm_mix_g, v_w_in, v_conv_w, v_conv_b, v_gate_b, v_mlstm_norm_g, v_w_out, v_norm_mlp_g, v_w_up, v_w_down, v_norm_ple_g, v_w_ple_gate, v_w_ple, v_final_norm_g):
    big_names = ("w_in", "conv_w", "w_out", "w_up", "w_down", "w_ple_gate", "w_ple")
    wts = dict(w_in=w_in, conv_w=conv_w, w_out=w_out, w_up=w_up, w_down=w_down, w_ple_gate=w_ple_gate, w_ple=w_ple)
    mom = dict(w_in=m_w_in, conv_w=m_conv_w, w_out=m_w_out, w_up=m_w_up, w_down=m_w_down, w_ple_gate=m_w_ple_gate,
               w_ple=m_w_ple)
    var = dict(w_in=v_w_in, conv_w=v_conv_w, w_out=v_w_out, w_up=v_w_up, w_down=v_w_down, w_ple_gate=v_w_ple_gate,
               w_ple=v_w_ple)
    sq = lambda a: a.reshape(a.shape[1:])
    fin = final_norm_g.reshape(1, D)
    dx, recv, total, first = _step(
        x[0], p[0, 0], loss_target[0], norm_mix_g, conv_b, jnp.pad(gate_b, ((0, 0), (0, 120))), mlstm_norm_g,
        norm_mlp_g, norm_ple_g, fin, {n: sq(wts[n]) for n in big_names})

    nrow = 8 * len(SMALL)
    me = _dev_index(*_place())
    conv_rows = total[nrow + 8:nrow + 40:8]
    recv["conv_w"] = lax.dynamic_slice_in_dim(conv_rows, me * 128, 128, axis=1).reshape(1, 4, 128)
    out = {}
    for n, tr in zip(big_names, (256, 4, 128, 256, 256, 128, 256)):
        res = _adamw("adamw_" + n, recv[n], sq(wts[n]), sq(mom[n]), sq(var[n]), tr)
        out[n] = [t.reshape(wts[n].shape) for t in res]
    sw = dict(norm_mix_g=norm_mix_g, conv_b=conv_b, gate_b=gate_b, mlstm_norm_g=mlstm_norm_g, norm_mlp_g=norm_mlp_g,
              norm_ple_g=norm_ple_g, final_norm_g=fin)
    sm = dict(norm_mix_g=m_norm_mix_g, conv_b=m_conv_b, gate_b=m_gate_b, mlstm_norm_g=m_mlstm_norm_g,
              norm_mlp_g=m_norm_mlp_g, norm_ple_g=m_norm_ple_g, final_norm_g=m_final_norm_g.reshape(1, D))
    sv = dict(norm_mix_g=v_norm_mix_g, conv_b=v_conv_b, gate_b=v_gate_b, mlstm_norm_g=v_mlstm_norm_g,
              norm_mlp_g=v_norm_mlp_g, norm_ple_g=v_norm_ple_g, final_norm_g=v_final_norm_g.reshape(1, D))
    res = _adamw_small(total, first, [sw[n] for n in SMALL], [sm[n] for n in SMALL], [sv[n] for n in SMALL])
    for n, r in zip(SMALL, res):
        out[n] = [t.reshape(final_norm_g.shape) for t in r] if n == "final_norm_g" else list(r)
    order = ("norm_mix_g", "w_in", "conv_w", "conv_b", "gate_b", "mlstm_norm_g", "w_out", "norm_mlp_g", "w_up", "w_down",
             "norm_ple_g", "w_ple_gate", "w_ple", "final_norm_g")
    loss_all = total[nrow, 0]
    return (loss_all, dx[None], *[out[n][0] for n in order], *[out[n][1] for n in order],
            *[out[n][2] for n in order], *[out[n][3] for n in order])
```

```python
import functools
import math

import jax
import jax.numpy as jnp
from jax import lax
from jax.experimental import pallas as pl
from jax.experimental.pallas import tpu as pltpu

F32, BF16 = jnp.float32, jnp.bfloat16
S = 4096
D = 1024
AW = 512
MW = 512
DFF = 4096
PLE = 256
IN_W = 3592
PW = 3840
NDEV = 8
EPS = 1e-6
NEG = -1e30
LC = 128
TB = 256
ROPE_THETA = 500000.0
VMEM_LIMIT = 56 * 1024 * 1024
HI = lax.Precision.HIGHEST

ADAM_LR, ADAM_B1, ADAM_B2, ADAM_EPS, ADAM_WD, ADAM_STEP = 0.001, 0.9, 0.999, 1e-08, 0.01, 10


def _params(n_grid=0, **kw):
    sem = dict(dimension_semantics=("arbitrary",) * n_grid) if n_grid else {}
    return pltpu.CompilerParams(vmem_limit_bytes=VMEM_LIMIT, **sem, **kw)


def _cspec(shape):
    nd = len(shape)
    return pl.BlockSpec(shape, lambda *_: (0,) * nd, pipeline_mode=pl.Buffered(1))


def _dot(a, b):
    return jnp.dot(a, b, preferred_element_type=F32)


def _dot_nt(a, b):
    return lax.dot_general(a, b, (((1,), (1,)), ((), ())), preferred_element_type=F32)


def _dot_tn(a, b):
    return lax.dot_general(a, b, (((0,), (0,)), ((), ())), preferred_element_type=F32)


def _bf(x):
    return x.astype(BF16)


def _rms(x):
    rs = lax.rsqrt(jnp.mean(x * x, axis=-1, keepdims=True) + EPS)
    return x * rs, rs


def _rms_bwd(du, n, rs, g):
    dn = du * g
    return rs * (dn - n * jnp.mean(dn * n, axis=-1, keepdims=True))


def _sigmoid(x):
    return 1.0 / (1.0 + jnp.exp(-x))


ROPE_BLK = 512


def _rope_parts():
    def cs(n, step):
        j = lax.broadcasted_iota(jnp.int32, (n, 128), 1) % 64
        pos = (lax.broadcasted_iota(jnp.int32, (n, 128), 0) * step).astype(F32)
        ang = pos * jnp.power(ROPE_THETA, -(j % 8).astype(F32) / 8.0)
        return jnp.cos(ang), jnp.sin(ang)

    return (*cs(ROPE_BLK, 1), *cs(S // ROPE_BLK, ROPE_BLK))


def _rope_fill(co_ref, so_ref, cb_ref, sb_ref, rc_ref, ra_ref, rb_ref):
    j = lax.broadcasted_iota(jnp.int32, (ROPE_BLK, 128), 1) % 64
    co, so = co_ref[...], so_ref[...]
    for t in range(S // ROPE_BLK):
        cb, sb = cb_ref[t:t + 1, :], sb_ref[t:t + 1, :]
        cos, sin = cb * co - sb * so, sb * co + cb * so
        rows = slice(t * ROPE_BLK, (t + 1) * ROPE_BLK)
        rc_ref[rows, :] = jnp.where(j < 16, cos, 1.0)
        ra_ref[rows, :] = jnp.where(j < 8, -sin, 0.0)
        rb_ref[rows, :] = jnp.where((j >= 8) & (j < 16), sin, 0.0)


def _rope(blk, c, a, b):
    return blk * c + pltpu.roll(blk, 120, 1) * a + pltpu.roll(blk, 8, 1) * b


def _rope_bwd(d, c, a, b):
    return d * c + pltpu.roll(d * a, 8, 1) + pltpu.roll(d * b, 120, 1)


def _unrope(t, c, a, b):
    return jnp.concatenate([_bf(_rope_bwd(t[:, j * 128:(j + 1) * 128].astype(F32), c, a, b))
                            for j in range(t.shape[1] // 128)], axis=1)


MESH = pl.DeviceIdType.MESH
ANY = pl.BlockSpec(memory_space=pl.ANY)
VM = pl.BlockSpec(memory_space=pltpu.VMEM)
FLIPS = [(dx, dy, dc) for dx in (0, 1) for dy in (0, 1) for dc in (0, 1)][1:]


def _place():
    return lax.axis_index("x"), lax.axis_index("y"), lax.axis_index("c")


def _dev_index(px, py, pc):
    return 4 * px + 2 * py + pc


def _gather_phases(ins, outs, bufs, send_sems=None, recv_sems=None, local_sems=None):
    nw = len(ins)
    if nw == 0:
        return (lambda: None,) * 3
    x, y, c = _place()
    me, sib = (x, y, c), (x, y, 1 - c)
    chips = [(1 - x, y), (x, 1 - y), (1 - x, 1 - y)]

    def copy(w, k, block, to, from_buf=False):
        dst = outs[w].at[_dev_index(*block)]
        return pltpu.make_async_remote_copy(
            src_ref=bufs[w] if from_buf else dst, dst_ref=dst, send_sem=send_sems.at[w, k],
            recv_sem=recv_sems.at[w, k], device_id=to, device_id_type=MESH)

    def mine(w):
        return pltpu.make_async_copy(bufs[w], outs[w].at[_dev_index(*me)], local_sems.at[w])

    def first(w):
        return [copy(w, 0, me, sib, True)] + [copy(w, 1 + j, me, (*chip, c), True) for j, chip in enumerate(chips)]

    def passed(w):
        return [copy(w, 4 + j, (*chip, c), sib) for j, chip in enumerate(chips)]

    def start():
        for w in range(nw):
            bufs[w][...] = ins[w][...].astype(bufs[w].dtype)
        for w in range(nw):
            mine(w).start()
            for cp in first(w):
                cp.start()

    def forward():
        for j, chip in enumerate(chips):
            for w in range(nw):
                copy(w, 1 + j, (*chip, c), me).wait_recv()
                passed(w)[j].start()

    def finish():
        for w in range(nw):
            copy(w, 0, sib, me).wait_recv()
        for j, chip in enumerate(chips):
            for w in range(nw):
                copy(w, 4 + j, (*chip, 1 - c), me).wait_recv()
        for w in range(nw):
            for cp in first(w) + passed(w):
                cp.wait_send()
            mine(w).wait()

    return start, forward, finish


def _gather_scratch(shards, dtypes):
    nw = len(shards)
    if nw == 0:
        return []
    return ([pltpu.VMEM(s.shape, dt) for s, dt in zip(shards, dtypes)]
            + [pltpu.SemaphoreType.DMA((nw, 7)), pltpu.SemaphoreType.DMA((nw, 7)), pltpu.SemaphoreType.DMA((nw,))])


def _gather_shapes(shards, dtypes):
    return [jax.ShapeDtypeStruct((NDEV, *s.shape), dt) for s, dt in zip(shards, dtypes)]


def _scatter_phases(ins, outs, send_sems=None, recv_sems=None, local_sems=None):
    nw = len(ins)
    if nw == 0:
        return (lambda: None,) * 2
    x, y, c = _place()
    me = _dev_index(x, y, c)

    def copies():
        out = []
        for w in range(nw):
            out.append(pltpu.make_async_copy(ins[w].at[me], outs[w].at[me], local_sems.at[w]))
            for k, (dx, dy, dc) in enumerate(FLIPS):
                peer = ((x + dx) % 2, (y + dy) % 2, (c + dc) % 2)
                out.append(pltpu.make_async_remote_copy(
                    src_ref=ins[w].at[_dev_index(*peer)], dst_ref=outs[w].at[me], send_sem=send_sems.at[w, k],
                    recv_sem=recv_sems.at[w, k], device_id=peer, device_id_type=MESH))
        return out

    def start():
        for cp in copies():
            cp.start()

    def finish():
        for cp in copies():
            cp.wait()

    return start, finish


def _scatter_scratch(nw):
    if nw == 0:
        return []
    return [pltpu.SemaphoreType.DMA((nw, 7)), pltpu.SemaphoreType.DMA((nw, 7)), pltpu.SemaphoreType.DMA((nw,))]


TM = 512


def _join_w_in(wg):
    sw = IN_W // NDEV

    def body(wg_ref, w_ref):
        for j in range(NDEV):
            w_ref[:, sw * j:sw * (j + 1)] = wg_ref[j]
        w_ref[:, IN_W:PW] = jnp.zeros((D, PW - IN_W), BF16)

    return pl.pallas_call(body, name="join_w_in", out_shape=jax.ShapeDtypeStruct((D, PW), BF16),
                          compiler_params=_params())(wg)


def _in_proj(x, g1, w, rc, ra, rb, shards, dtypes):
    tm = TM
    nw = len(shards)
    nt = S // tm

    def body(*refs):
        x_ref, g_ref, w_ref, rc_ref, ra_ref, rb_ref = refs[:6]
        ins = refs[6:6 + nw]
        qkv_ref, mqk_ref, mv_ref, mo_ref, gt_ref, u_ref = refs[6 + nw:12 + nw]
        outs = refs[12 + nw:12 + 2 * nw]
        bufs = refs[12 + 2 * nw:12 + 3 * nw]
        ag_start, ag_forward, ag_finish = _gather_phases(ins, outs, bufs, *refs[12 + 3 * nw:])
        i = pl.program_id(0)
        pl.when(i == 0)(ag_start)
        pl.when(i == nt - 2)(ag_forward)
        n, _ = _rms(x_ref[...])
        u = _bf(n * g_ref[...])
        u_ref[...] = u
        c, a, b = rc_ref[...], ra_ref[...], rb_ref[...]
        for half in range(2):
            blk = _dot(u, w_ref[:, half * 512:(half + 1) * 512])
            for t in range(4):
                lo = half * 512 + t * 128
                qkv_ref[:, lo:lo + 128] = _rope(blk[:, t * 128:(t + 1) * 128], c, a, b)
        qkv_ref[:, 1024:1536] = _dot(u, w_ref[:, 1024:1536])
        mqk_ref[:, 0:512] = _dot(u, w_ref[:, 1536:2048])
        mqk_ref[:, 512:1024] = _dot(u, w_ref[:, 2048:2560])
        mv_ref[...] = _dot(u, w_ref[:, 2560:3072])
        mo_ref[...] = _dot(u, w_ref[:, 3072:3584])
        gt_ref[...] = _dot(u, w_ref[:, 3584:3712])
        pl.when(i == nt - 1)(ag_finish)

    row = lambda wd: pl.BlockSpec((tm, wd), lambda i: (i, 0))
    res = pl.pallas_call(
        body, name="in_proj", grid=(nt,),
        in_specs=[row(D), _cspec((1, D)), _cspec((D, PW)), row(128), row(128), row(128)] + [VM] * nw,
        out_specs=[row(1536), row(1024), row(512), row(512), row(128), row(D)] + [ANY] * nw,
        out_shape=[jax.ShapeDtypeStruct((S, 1536), F32), jax.ShapeDtypeStruct((S, 1024), F32),
                   jax.ShapeDtypeStruct((S, 512), F32), jax.ShapeDtypeStruct((S, 512), F32),
                   jax.ShapeDtypeStruct((S, 128), F32), jax.ShapeDtypeStruct((S, D), BF16)]
        + _gather_shapes(shards, dtypes),
        scratch_shapes=_gather_scratch(shards, dtypes),
        compiler_params=_params(1),
    )(x, g1, w, rc, ra, rb, *shards)
    return res[:6], res[6:]


DILATIONS = (16, 4, 1)


def _attn_valid(n):
    kd = lax.broadcasted_iota(jnp.int32, (128, 256), 1) - lax.broadcasted_iota(jnp.int32, (128, 256), 0)
    off = jnp.where(n == 0, 0, 128)
    return (kd <= off) & (kd >= off - 128)


def _attn_rows(d, r, n):
    if d == 1:
        q0 = pl.multiple_of(n * 128, 128)
        k0 = pl.multiple_of(jnp.maximum(n - 1, 0) * 128, 128)
        return pl.ds(q0, 128), pl.ds(k0, 256), _attn_valid(n)
    q0 = r + n * 128 * d
    k0 = r + jnp.maximum(n - 1, 0) * 128 * d
    return pl.ds(q0, 128, stride=d), pl.ds(k0, 256, stride=d), _attn_valid(n)


ATTN_GROUP = 4
ATTN_ITERS = S // 128 // ATTN_GROUP


def _attn_group(d, i):
    nb = S // (128 * d)
    if nb == 2:
        qi = lax.broadcasted_iota(jnp.int32, (256, 256), 0) - lax.broadcasted_iota(jnp.int32, (256, 256), 1)
        whole = [pl.ds((ATTN_GROUP // 2) * i + u, 256, stride=d) for u in range(ATTN_GROUP // 2)]
        return [(rows, rows, (qi >= 0) & (qi <= 128)) for rows in whole]
    if d == 1:
        return [_attn_rows(1, 0, i + ATTN_ITERS * u) for u in range(ATTN_GROUP)]
    return [_attn_rows(d, (i // nb) * ATTN_GROUP + u, i % nb) for u in range(ATTN_GROUP)]


def _head0(shape):
    return lax.broadcasted_iota(jnp.int32, shape, 1) < 64


def _stack_heads(t):
    h0 = _head0(t.shape)
    tb = _bf(t)
    zero = jnp.zeros_like(tb)
    return jnp.concatenate([jnp.where(h0, tb, zero), jnp.where(h0, zero, tb)], axis=0)


def _attn_fwd(qkv, shards, dtypes):
    nw = len(shards)

    def body(*refs):
        q_ref, k_ref, v_ref = refs[:3]
        ins = refs[3:3 + nw]
        o_ref, lse0_ref, lse1_ref = refs[3 + nw:6 + nw]
        outs = refs[6 + nw:6 + 2 * nw]
        m0, m1, l0, l1, acc = refs[6 + 2 * nw:11 + 2 * nw]
        bufs = refs[11 + 2 * nw:11 + 3 * nw]
        ag_start, ag_forward, ag_finish = _gather_phases(ins, outs, bufs, *refs[11 + 3 * nw:])
        hp = pl.program_id(0)
        pl.when(hp == 0)(ag_start)
        pl.when(hp == 3)(ag_forward)
        stats = (m0, m1, l0, l1, acc)

        def update(blocks, first):
            loaded = [([q_ref[rq, :], k_ref[rk, :], v_ref[rk, :]], None if first else [ref[rq, :] for ref in stats])
                      for rq, rk, _ in blocks]
            results = []
            for ((q, k, v), prev), (_, _, valid) in zip(loaded, blocks):
                head0 = _head0(q.shape)
                kb, vb = _bf(k), _bf(v)
                q = q * 0.125
                m_new, l_new, acc_new = [], [], []
                for a, qa in enumerate((_bf(jnp.where(head0, q, 0.0)), _bf(jnp.where(head0, 0.0, q)))):
                    s = jnp.where(valid, _dot_nt(qa, kb), NEG)
                    mc = jnp.max(s, axis=-1, keepdims=True)
                    m_a = jnp.broadcast_to(mc, q.shape) if first else jnp.maximum(prev[a], mc)
                    p = jnp.exp(s - jnp.tile(m_a, (1, 2)))
                    l_add = jnp.sum(p, axis=-1, keepdims=True)
                    pv = _dot(_bf(p), vb)
                    if first:
                        l_a = jnp.broadcast_to(l_add, q.shape)
                    else:
                        alpha = jnp.exp(prev[a] - m_a)
                        l_a, pv = alpha * prev[2 + a] + l_add, alpha * prev[4] + pv
                    m_new.append(m_a), l_new.append(l_a), acc_new.append(pv)
                results.append((m_new[0], m_new[1], l_new[0], l_new[1], jnp.where(head0, acc_new[0], acc_new[1])))
            for (rq, _, _), res in zip(blocks, results):
                for ref, val in zip(stats, res):
                    ref[rq, :] = val

        for d in DILATIONS:
            def step(i, carry, d=d):
                update(_attn_group(d, i), d == DILATIONS[0])
                return carry

            lax.fori_loop(0, ATTN_ITERS, step, 0)

        def fin(t, carry):
            rows = pl.ds(pl.multiple_of(t * 256, 256), 256)
            h0 = lax.broadcasted_iota(jnp.int32, (256, 128), 1) < 64
            la, lb = l0[rows, :], l1[rows, :]
            o_ref[rows, :] = acc[rows, :] / jnp.where(h0, la, lb)
            lse0_ref[rows, :] = m0[rows, :] + jnp.log(la)
            lse1_ref[rows, :] = m1[rows, :] + jnp.log(lb)
            return carry

        lax.fori_loop(0, S // 256, fin, 0)
        pl.when(hp == 3)(ag_finish)

    col = lambda off: pl.BlockSpec((S, 128), lambda h, off=off: (0, off + h))
    res = pl.pallas_call(
        body, name="attn_fwd", grid=(4,),
        in_specs=[col(0), col(4), col(8)] + [VM] * nw,
        out_specs=[col(0), col(0), col(0)] + [ANY] * nw,
        out_shape=[jax.ShapeDtypeStruct((S, AW), F32)] * 3 + _gather_shapes(shards, dtypes),
        scratch_shapes=[pltpu.VMEM((S, 128), F32)] * 5 + _gather_scratch(shards, dtypes),
        compiler_params=_params(1),
    )(qkv, qkv, qkv, *shards)
    return res[0], (res[1], res[2]), res[3:]


def _attn_bwd(qkv, o, lse, do, parts):
    nw = len(parts)

    def body(*refs):
        q_ref, k_ref, v_ref, o_ref, L0, L1, do_ref = refs[:7]
        ins = refs[7:7 + nw]
        dq_out, dk_out, dv_out = refs[7 + nw:10 + nw]
        outs = refs[10 + nw:10 + 2 * nw]
        D0, D1, dq_ref, dk_ref, dv_ref = refs[10 + 2 * nw:15 + 2 * nw]
        rs_start, rs_finish = _scatter_phases(ins, outs, *refs[15 + 2 * nw:])
        hp = pl.program_id(0)
        pl.when(hp == 0)(rs_start)

        def pre(t, carry):
            rows = pl.ds(pl.multiple_of(t * 256, 256), 256)
            h0 = lax.broadcasted_iota(jnp.int32, (256, 128), 1) < 64
            dd = do_ref[rows, :] * o_ref[rows, :]
            shp = (256, 128)
            D0[rows, :] = jnp.broadcast_to(jnp.sum(jnp.where(h0, dd, 0.0), axis=-1, keepdims=True), shp)
            D1[rows, :] = jnp.broadcast_to(jnp.sum(jnp.where(h0, 0.0, dd), axis=-1, keepdims=True), shp)
            return carry

        lax.fori_loop(0, S // 256, pre, 0)

        def update(blocks, first):
            loaded = [([q_ref[rq, :], k_ref[rk, :], v_ref[rk, :], do_ref[rq, :]],
                       [L0[rq, :], L1[rq, :], D0[rq, :], D1[rq, :]],
                       [0.0] * 3 if first else [dq_ref[rq, :], dk_ref[rk, :], dv_ref[rk, :]]) for rq, rk, _ in blocks]
            results = []
            for ((q, k, v, dout), (l0v, l1v, d0v, d1v), (dq, dk, dv)), (_, _, valid) in zip(loaded, blocks):
                nq = q.shape[0]
                valid = jnp.concatenate([valid, valid], axis=0)
                q2, do2, kb, vb = _stack_heads(q), _stack_heads(dout), _bf(k), _bf(v)
                cat = lambda a, b: jnp.tile(jnp.concatenate([a, b], axis=0), (1, 2))
                s = jnp.where(valid, _dot_nt(_stack_heads(q * 0.125), kb), NEG)
                p = jnp.exp(s - cat(l0v, l1v))
                ds = _bf(p * (_dot_nt(do2, vb) - cat(d0v, d1v)) * 0.125)
                dq2 = _dot(ds, kb)
                results.append((dq + jnp.where(_head0((nq, 128)), dq2[0:nq], dq2[nq:2 * nq]),
                                dk + _dot_tn(ds, q2), dv + _dot_tn(_bf(p), do2)))
            for (rq, rk, _), (dq, dk, dv) in zip(blocks, results):
                dq_ref[rq, :] = dq
                dk_ref[rk, :] = dk
                dv_ref[rk, :] = dv

        assert S // (128 * DILATIONS[0]) == 2
        for d in DILATIONS:
            def step(i, carry, d=d):
                update(_attn_group(d, i), d == DILATIONS[0])
                return carry

            lax.fori_loop(0, ATTN_ITERS, step, 0)

        def fin(t, carry):
            rows = pl.ds(pl.multiple_of(t * 256, 256), 256)
            for src, dst in ((dq_ref, dq_out), (dk_ref, dk_out), (dv_ref, dv_out)):
                dst[rows, :] = _bf(src[rows, :])
            return carry

        lax.fori_loop(0, S // 256, fin, 0)
        pl.when(hp == 3)(rs_finish)

    col = lambda off: pl.BlockSpec((S, 128), lambda h, off=off: (0, off + h))
    res = pl.pallas_call(
        body, name="attn_bwd", grid=(4,),
        in_specs=[col(0), col(4), col(8), col(0), col(0), col(0), col(0)] + [ANY] * nw,
        out_specs=[col(0), col(0), col(0)] + [ANY] * nw,
        out_shape=[jax.ShapeDtypeStruct((S, AW), BF16)] * 3 + [jax.ShapeDtypeStruct(a.shape, a.dtype) for a in parts],
        scratch_shapes=[pltpu.VMEM((S, 128), F32)] * 5 + _scatter_scratch(nw),
        compiler_params=_params(1),
    )(qkv, qkv, qkv, o, lse[0], lse[1], do, *parts)
    return res[0], res[1], res[2], res[3:]


def _logsig(x):
    return jnp.minimum(x, 0.0) - jnp.log1p(jnp.exp(-jnp.abs(x)))


def _conv_taps(xp, n):
    return [xp[8:] if j == 3 else pltpu.roll(xp, 3 - j, 0)[8:] for j in range(4)]


def _conv_silu(xp, w_ref, b_ref, n):
    taps = _conv_taps(xp, n)
    c = b_ref[...] + sum(w_ref[j:j + 1, :] * taps[j] for j in range(4))
    sg = _sigmoid(c)
    return c, sg, taps


def _chunk_gates(G):
    assert LC == 128
    r = lax.broadcasted_iota(jnp.int32, (LC, LC), 0)
    c = lax.broadcasted_iota(jnp.int32, (LC, LC), 1)
    tril = (c <= r).astype(F32)
    triu = (c >= r).astype(F32)
    b_col = jnp.dot(tril, _logsig(G), preferred_element_type=F32, precision=HI)
    return b_col, b_col.T, G.T, tril, triu


def _colpick(X, lane):
    li = lax.broadcasted_iota(jnp.int32, X.shape, 1)
    return jnp.sum(jnp.where(li == lane, X, 0.0), axis=1, keepdims=True)


def _rowpick(XT, row):
    ri = lax.broadcasted_iota(jnp.int32, XT.shape, 0)
    return jnp.sum(jnp.where(ri == row, XT, 0.0), axis=0, keepdims=True)


def _mlstm_head(qh, kh, vh, G, b_col, b_row, g_row, h, Ch, nh, m_prev):
    bt = _colpick(b_col, 4 + h)
    i_col = _colpick(G, h)
    bs = _rowpick(b_row, 4 + h)
    i_row = _rowpick(g_row, h)
    r = lax.broadcasted_iota(jnp.int32, (LC, LC), 0)
    c = lax.broadcasted_iota(jnp.int32, (LC, LC), 1)
    log_d = jnp.where(c <= r, bt - bs + i_row, NEG)
    log_inter = bt + m_prev
    m_t = jnp.maximum(log_inter, jnp.max(log_d, axis=1, keepdims=True))
    Dm = jnp.exp(log_d - m_t)
    g = jnp.exp(log_inter - m_t)
    qb, kb, vb = _bf(qh), _bf(kh), _bf(vh)
    Am = _dot_nt(qb, kb) * Dm
    qC = _dot(qb, _bf(Ch))
    num = g * qC + _dot(_bf(Am), vb)
    qn = jnp.sum(qh * nh, axis=1, keepdims=True)
    den = g * qn + jnp.sum(Am, axis=1, keepdims=True)
    floor = jnp.exp(-m_t)
    dd = jnp.maximum(jnp.abs(den), floor)
    inv_dd = 1.0 / dd
    hh = num * inv_dd
    lane = lax.broadcasted_iota(jnp.int32, (1, LC), 1)
    blast = jnp.sum(jnp.where(lane == LC - 1, bs, 0.0), axis=1, keepdims=True)
    log_s = blast - bt + i_col
    m_new = jnp.maximum(blast + m_prev, jnp.max(log_s, axis=0, keepdims=True))
    decay = jnp.exp(blast + m_prev - m_new)
    ws = jnp.exp(log_s - m_new)
    kw = kh * ws
    C_new = decay * Ch + _dot_tn(_bf(kw), vb)
    n_new = decay * nh + jnp.sum(kw, axis=0, keepdims=True)
    return dict(Dm=Dm, g=g, Am=Am, qC=qC, qn=qn, den=den, floor=floor, inv_dd=inv_dd, h=hh, decay=decay, ws=ws, kw=kw,
                C_new=C_new, n_new=n_new, m_new=m_new, qb=qb, kb=kb, vb=vb)


def _head_out(hh, mo_h, gn_h):
    r = lax.rsqrt(jnp.mean(hh * hh, axis=-1, keepdims=True) + EPS)
    hn = hh * r
    sg = _sigmoid(mo_h)
    return sg * (hn * gn_h), hn, r, sg


def _mlstm_fwd(mqk, mv, mo, gates, conv_w, conv_b, gate_b, gn, shards, dtypes):
    nblk = S // TB
    ncb = TB // LC
    nw = len(shards)

    def body(*refs):
        x_ref, v_ref, o_ref, g_ref, w_ref, b_ref, gb_ref, gn_ref = refs[:8]
        ins = refs[8:8 + nw]
        out_ref, cs_ref, ns_ref, ms_ref = refs[8 + nw:12 + nw]
        outs = refs[12 + nw:12 + 2 * nw]
        tail, Cst, nst, mst, qs, ks = refs[12 + 2 * nw:18 + 2 * nw]
        bufs = refs[18 + 2 * nw:18 + 3 * nw]
        ag_start, ag_forward, ag_finish = _gather_phases(ins, outs, bufs, *refs[18 + 3 * nw:])
        i = pl.program_id(0)
        pl.when(i == 0)(ag_start)
        pl.when(i == nblk // 2)(ag_forward)

        @pl.when(i == 0)
        def _():
            tail[...] = jnp.zeros_like(tail)
            Cst[...] = jnp.zeros_like(Cst)
            nst[...] = jnp.zeros_like(nst)
            mst[...] = jnp.zeros_like(mst)

        x = x_ref[...]
        xp = jnp.concatenate([tail[...], x], axis=0)
        tail[...] = x[TB - 8:TB, :]
        c, sg, _ = _conv_silu(xp, w_ref, b_ref, TB)
        y = c * sg
        qs[...] = y[:, 0:MW]
        ks[...] = y[:, MW:2 * MW] * (1.0 / math.sqrt(128.0))

        for cc in range(ncb):
            rows = slice(cc * LC, (cc + 1) * LC)
            G = g_ref[rows, :] + gb_ref[...]
            b_col, b_row, g_row, _, _ = _chunk_gates(G)
            cs_ref[cc] = Cst[...]
            ns_ref[cc] = nst[...]
            ms_ref[cc] = mst[...]
            for h in range(4):
                ln = slice(h * 128, (h + 1) * 128)
                m_prev = jnp.max(mst[0:1, ln], axis=1, keepdims=True)
                f = _mlstm_head(qs[rows, ln], ks[rows, ln], v_ref[rows, ln], G, b_col, b_row, g_row, h,
                                Cst[:, ln], nst[0:1, ln], m_prev)
                out, _, _, _ = _head_out(f["h"], o_ref[rows, ln], gn_ref[:, ln])
                out_ref[rows, ln] = out
                Cst[:, ln] = f["C_new"]
                nst[0:1, ln] = f["n_new"]
                mst[0:1, ln] = jnp.broadcast_to(f["m_new"], (1, 128))
        pl.when(i == nblk - 1)(ag_finish)

    row = lambda wd: pl.BlockSpec((TB, wd), lambda i: (i, 0))
    res = pl.pallas_call(
        body, name="mlstm_fwd", grid=(nblk,),
        in_specs=[row(1024), row(MW), row(MW), row(128), _cspec((4, 1024)), _cspec((1, 1024)), _cspec((1, 128)),
                  _cspec((1, MW))] + [VM] * nw,
        out_specs=[row(MW), pl.BlockSpec((ncb, 128, MW), lambda i: (i, 0, 0)),
                   pl.BlockSpec((ncb, 8, MW), lambda i: (i, 0, 0)), pl.BlockSpec((ncb, 8, MW), lambda i: (i, 0, 0))]
        + [ANY] * nw,
        out_shape=[jax.ShapeDtypeStruct((S, MW), F32), jax.ShapeDtypeStruct((S // LC, 128, MW), F32),
                   jax.ShapeDtypeStruct((S // LC, 8, MW), F32), jax.ShapeDtypeStruct((S // LC, 8, MW), F32)]
        + _gather_shapes(shards, dtypes),
        scratch_shapes=[pltpu.VMEM((8, 1024), F32), pltpu.VMEM((128, MW), F32), pltpu.VMEM((8, MW), F32),
                        pltpu.VMEM((8, MW), F32), pltpu.VMEM((TB, MW), F32), pltpu.VMEM((TB, MW), F32)]
        + _gather_scratch(shards, dtypes),
        compiler_params=_params(1),
    )(mqk, mv, mo, gates, conv_w, conv_b, gate_b, gn, *shards)
    return res[0], res[1], res[2], res[3], res[4:]


DM_V, DM_O, DM_G, DM_W = 1024, 1536, 2048, PW - 3 * AW


def _mlstm_bwd(mqk, mv, mo, gates, conv_w, conv_b, gate_b, gn, cs, ns, ms, dout, parts):
    nblk = S // TB
    ncb = TB // LC
    kscale = 1.0 / math.sqrt(128.0)
    nw = len(parts)

    def body(*refs):
        x_ref, xprev_ref, v_ref, o_ref, g_ref, w_ref, b_ref, gb_ref, gn_ref, cs_ref, ns_ref, ms_ref, do_ref = refs[:13]
        ins = refs[13:13 + nw]
        dm_ref, dw_ref, db_ref, dgn_ref, dgb_ref = refs[13 + nw:18 + nw]
        outs = refs[18 + nw:18 + 2 * nw]
        dCst, dnst, dyhead, qs, ks, dqk = refs[18 + 2 * nw:24 + 2 * nw]
        rs_start, rs_finish = _scatter_phases(ins, outs, *refs[24 + 2 * nw:])
        i = pl.program_id(0)
        blk = nblk - 1 - i
        pl.when(i == 0)(rs_start)

        @pl.when(i == 0)
        def _():
            dCst[...] = jnp.zeros_like(dCst)
            dnst[...] = jnp.zeros_like(dnst)
            dyhead[...] = jnp.zeros_like(dyhead)
            dw_ref[...] = jnp.zeros_like(dw_ref)
            db_ref[...] = jnp.zeros_like(db_ref)
            dgn_ref[...] = jnp.zeros_like(dgn_ref)
            dgb_ref[...] = jnp.zeros_like(dgb_ref)

        x = x_ref[...]
        xprev = jnp.where(blk == 0, 0.0, xprev_ref[...])
        xp = jnp.concatenate([xprev, x], axis=0)
        c, sg, taps = _conv_silu(xp, w_ref, b_ref, TB)
        y = c * sg
        qs[...] = y[:, 0:MW]
        ks[...] = y[:, MW:2 * MW] * kscale
        lane128 = lax.broadcasted_iota(jnp.int32, (LC, 128), 1)
        rowi = lax.broadcasted_iota(jnp.int32, (LC, 1), 0)
        ones = jnp.ones((LC, 128), F32)

        for cc in reversed(range(ncb)):
            rows = slice(cc * LC, (cc + 1) * LC)
            G = g_ref[rows, :] + gb_ref[...]
            b_col, b_row, g_row, _, triu = _chunk_gates(G)
            dB = jnp.zeros((LC, 128), F32)
            dI = jnp.zeros((LC, 128), F32)
            for h in range(4):
                ln = slice(h * 128, (h + 1) * 128)
                Ch = cs_ref[cc, :, ln]
                nh = ns_ref[cc, 0:1, ln]
                m_prev = jnp.max(ms_ref[cc, 0:1, ln], axis=1, keepdims=True)
                qh, kh, vh = qs[rows, ln], ks[rows, ln], v_ref[rows, ln]
                f = _mlstm_head(qh, kh, vh, G, b_col, b_row, g_row, h, Ch, nh, m_prev)
                hh, inv_dd, den, g, Am, Dm = f["h"], f["inv_dd"], f["den"], f["g"], f["Am"], f["Dm"]
                qb, kb, vb = f["qb"], f["kb"], f["vb"]
                gn_h = gn_ref[:, ln]
                _, hn, r, sgo = _head_out(hh, o_ref[rows, ln], gn_h)
                do = do_ref[rows, ln]
                hm = hn * gn_h
                dm_ref[rows, DM_O + h * 128:DM_O + (h + 1) * 128] = _bf(do * hm * sgo * (1.0 - sgo))
                dhm = do * sgo
                dgn_ref[:, ln] = dgn_ref[:, ln] + jnp.sum(dhm * hn, axis=0, keepdims=True)
                dhn = dhm * gn_h
                dh = r * (dhn - hn * jnp.mean(dhn * hn, axis=-1, keepdims=True))
                dnum = dh * inv_dd
                ddd = -jnp.sum(dh * hh, axis=1, keepdims=True) * inv_dd
                dden = jnp.where(jnp.abs(den) >= f["floor"], ddd * jnp.sign(den), 0.0)
                dnb = _bf(dnum)
                dA = _dot_nt(dnb, vb) + dden
                dv = _dot_tn(_bf(Am), dnb)
                gd = _bf(g * dnum)
                gq = g * dden
                dq = _dot_nt(gd, _bf(Ch)) + gq * nh
                dCn = dCst[:, ln]
                dnn = dnst[0:1, ln]
                dC = f["decay"] * dCn + _dot_tn(qb, gd)
                dn = f["decay"] * dnn + jnp.sum(gq * qh, axis=0, keepdims=True)
                dg = jnp.sum(dnum * f["qC"], axis=1, keepdims=True) + dden * f["qn"]
                dS = _bf(dA * Dm)
                dq = dq + _dot(dS, kb)
                dk = _dot_tn(dS, qb)
                Gm = dA * Am
                gam = dg * g
                dCb = _bf(dCn)
                E = _dot_nt(vb, dCb) + dnn
                ws = f["ws"]
                dk = dk + ws * E
                om = jnp.sum(E * kh, axis=1, keepdims=True) * ws
                dv = dv + _dot(_bf(f["kw"]), dCb)
                ddecay = (jnp.sum(jnp.sum(dCn * Ch, axis=1, keepdims=True), axis=0, keepdims=True)
                          + jnp.sum(dnn * nh, axis=1, keepdims=True))
                delta = ddecay * f["decay"]
                rows_g = jnp.sum(Gm, axis=1, keepdims=True)
                cols_g = jnp.broadcast_to(jnp.sum(Gm, axis=0, keepdims=True), (LC, 128)).T
                last = jnp.where(rowi == LC - 1, jnp.sum(om, axis=0, keepdims=True) + delta, 0.0)
                db = rows_g + gam - om + last - cols_g
                di = cols_g + om
                dB = jnp.where(lane128 == 4 + h, db, dB)
                dI = jnp.where(lane128 == h, di, dI)
                dCst[:, ln] = dC
                dnst[0:1, ln] = dn
                dqk[rows, ln] = dq
                dqk[rows, MW + h * 128:MW + (h + 1) * 128] = dk * kscale
                dm_ref[rows, DM_V + h * 128:DM_V + (h + 1) * 128] = _bf(dv)
            dlogf = jnp.dot(triu, dB, preferred_element_type=F32, precision=HI)
            dG = dI + dlogf * _sigmoid(-G)
            dG = jnp.where(lane128 < 8, dG, 0.0)
            dm_ref[rows, DM_G:DM_G + 128] = _bf(dG)
            dm_ref[rows, DM_G + 128:DM_W] = jnp.zeros((LC, DM_W - DM_G - 128), BF16)
            dgb_ref[...] = dgb_ref[...] + jnp.sum(dG, axis=0, keepdims=True)

        dy = dqk[...] * (sg * (1.0 + c * (1.0 - sg)))
        db_ref[...] = db_ref[...] + jnp.sum(dy, axis=0, keepdims=True)
        for j in range(4):
            dw_ref[j:j + 1, :] = dw_ref[j:j + 1, :] + jnp.sum(dy * taps[j], axis=0, keepdims=True)
        dyp = jnp.concatenate([dy, dyhead[...]], axis=0)
        dx = w_ref[3:4, :] * dy
        for j in range(3):
            dx = dx + w_ref[j:j + 1, :] * pltpu.roll(dyp, TB + 8 - (3 - j), 0)[0:TB]
        dm_ref[:, 0:DM_V] = _bf(dx)
        dyhead[...] = dy[0:8, :]
        pl.when(i == nblk - 1)(rs_finish)

    rrow = lambda wd: pl.BlockSpec((TB, wd), lambda i: (nblk - 1 - i, 0))
    st = lambda r: pl.BlockSpec((ncb, r, MW), lambda i: (nblk - 1 - i, 0, 0))
    prev8 = pl.BlockSpec((8, 1024), lambda i: (jnp.maximum((nblk - 1 - i) * (TB // 8) - 1, 0), 0))
    res = pl.pallas_call(
        body, name="mlstm_bwd", grid=(nblk,),
        in_specs=[rrow(1024), prev8, rrow(MW), rrow(MW), rrow(128), _cspec((4, 1024)), _cspec((1, 1024)),
                  _cspec((1, 128)), _cspec((1, MW)), st(128), st(8), st(8), rrow(MW)] + [ANY] * nw,
        out_specs=[rrow(DM_W),
                   pl.BlockSpec((4, 1024), lambda i: (0, 0)), pl.BlockSpec((1, 1024), lambda i: (0, 0)),
                   pl.BlockSpec((1, MW), lambda i: (0, 0)), pl.BlockSpec((1, 128), lambda i: (0, 0))] + [ANY] * nw,
        out_shape=[jax.ShapeDtypeStruct((S, DM_W), BF16),
                   jax.ShapeDtypeStruct((4, 1024), F32), jax.ShapeDtypeStruct((1, 1024), F32),
                   jax.ShapeDtypeStruct((1, MW), F32), jax.ShapeDtypeStruct((1, 128), F32)]
        + [jax.ShapeDtypeStruct(a.shape, a.dtype) for a in parts],
        scratch_shapes=[pltpu.VMEM((128, MW), F32), pltpu.VMEM((8, MW), F32), pltpu.VMEM((8, 1024), F32),
                        pltpu.VMEM((TB, MW), F32), pltpu.VMEM((TB, MW), F32), pltpu.VMEM((TB, 1024), F32)]
        + _scatter_scratch(nw),
        compiler_params=_params(1),
    )(mqk, mqk, mv, mo, gates, conv_w, conv_b, gate_b, gn, cs, ns, ms, dout, *parts)
    return res[:5], res[5:]


def _out_proj(x, attn, ml, w, g):
    tm = TM

    def body(x_ref, a_ref, m_ref, w_ref, g_ref, h_ref, u_ref):
        h1 = x_ref[...] + _dot(_bf(a_ref[...]), w_ref[0:AW, :]) + _dot(_bf(m_ref[...]), w_ref[AW:D, :])
        h_ref[...] = h1
        n, _ = _rms(h1)
        u_ref[...] = _bf(n * g_ref[...])

    row = lambda wd: pl.BlockSpec((tm, wd), lambda i: (i, 0))
    return pl.pallas_call(
        body, name="out_proj", grid=(S // tm,),
        in_specs=[row(D), row(AW), row(MW), _cspec((D, D)), _cspec((1, D))],
        out_specs=[row(D), row(D)],
        out_shape=[jax.ShapeDtypeStruct((S, D), F32), jax.ShapeDtypeStruct((S, D), BF16)],
        compiler_params=_params(1),
    )(x, attn, ml, w, g)


HALF = DFF // NDEV // 2


def _mlp_fwd(h1, u2, w_up, w_down_a, w_down_b):
    tm = TM

    def body(h_ref, u_ref, wu_ref, wa_ref, wb_ref, a_ref, o_ref):
        u = u_ref[...]
        acc = h_ref[...]
        for c in range(NDEV):
            cols = slice(c * 512, (c + 1) * 512)
            a = _dot(u, wu_ref[c])
            a_ref[:, cols] = _bf(a)
            r = jnp.maximum(a, 0.0)
            r = _bf(r * r)
            acc = acc + _dot(r[:, 0:HALF], wa_ref[c]) + _dot(r[:, HALF:2 * HALF], wb_ref[c])
        o_ref[...] = acc

    row = lambda wd: pl.BlockSpec((tm, wd), lambda i: (i, 0))
    return pl.pallas_call(
        body, name="mlp_fwd", grid=(S // tm,),
        in_specs=[row(D), row(D), _cspec((NDEV, D, DFF // NDEV)), _cspec((NDEV, HALF, D)), _cspec((NDEV, HALF, D))],
        out_specs=[row(DFF), row(D)],
        out_shape=[jax.ShapeDtypeStruct((S, DFF), BF16), jax.ShapeDtypeStruct((S, D), F32)],
        compiler_params=_params(1),
    )(h1, u2, w_up, w_down_a, w_down_b)


def _ple_loss(h2, p, target, w_pg, w_ple, g_ple, g_fin):
    tm = TM

    def body(h_ref, p_ref, t_ref, wg_ref, wp_ref, gp_ref, gf_ref,
             dh_ref, dwg_ref, dwp_ref, dgp_ref, dgf_ref, loss_ref, acc_g, acc_p):
        i = pl.program_id(0)

        @pl.when(i == 0)
        def _():
            acc_g[...] = jnp.zeros_like(acc_g)
            acc_p[...] = jnp.zeros_like(acc_p)
            dgp_ref[...] = jnp.zeros_like(dgp_ref)
            dgf_ref[...] = jnp.zeros_like(dgf_ref)
            loss_ref[...] = jnp.zeros_like(loss_ref)

        h2v = h_ref[...]
        n2, rs2 = _rms(h2v)
        u3 = _bf(n2 * gp_ref[...])
        gt = _sigmoid(_dot(u3, wg_ref[...]))
        pb = _bf(p_ref[...])
        e = jnp.concatenate([_dot(pb, wp_ref[j]) for j in range(NDEV)], axis=1)
        h3 = h2v + gt * e
        n3, rs3 = _rms(h3)
        err = n3 * gf_ref[...] - t_ref[...]
        loss_ref[...] = loss_ref[...] + 0.5 / D * jnp.sum(jnp.sum(err * err, axis=1, keepdims=True), axis=0, keepdims=True)
        dy = err * (1.0 / D)
        dgf_ref[...] = dgf_ref[...] + jnp.sum(dy * n3, axis=0, keepdims=True)
        dh3 = _rms_bwd(dy, n3, rs3, gf_ref[...])
        de = _bf(dh3 * gt)
        dz = _bf(dh3 * e * gt * (1.0 - gt))
        acc_p[...] = acc_p[...] + _dot_tn(pb, de)
        acc_g[...] = acc_g[...] + _dot_tn(u3, dz)
        du3 = _dot_nt(dz, wg_ref[...])
        dgp_ref[...] = dgp_ref[...] + jnp.sum(du3 * n2, axis=0, keepdims=True)
        dh_ref[...] = dh3 + _rms_bwd(du3, n2, rs2, gp_ref[...])

        @pl.when(i == S // tm - 1)
        def _():
            dwg_ref[...] = _bf(acc_g[...])
            for j in range(NDEV):
                dwp_ref[j] = _bf(acc_p[:, j * 128:(j + 1) * 128])

    row = lambda wd: pl.BlockSpec((tm, wd), lambda i: (i, 0))
    whole = lambda shp: pl.BlockSpec(shp, lambda i: (0,) * len(shp))
    return pl.pallas_call(
        body, name="ple_loss", grid=(S // tm,),
        in_specs=[row(D), row(PLE), row(D), _cspec((D, D)), _cspec((NDEV, PLE, 128)), _cspec((1, D)), _cspec((1, D))],
        out_specs=[row(D), whole((D, D)), whole((NDEV, PLE, 128)), whole((1, D)), whole((1, D)), whole((1, 1))],
        out_shape=[jax.ShapeDtypeStruct((S, D), F32), jax.ShapeDtypeStruct((D, D), BF16),
                   jax.ShapeDtypeStruct((NDEV, PLE, 128), BF16), jax.ShapeDtypeStruct((1, D), F32),
                   jax.ShapeDtypeStruct((1, D), F32), jax.ShapeDtypeStruct((1, 1), F32)],
        scratch_shapes=[pltpu.VMEM((D, D), F32), pltpu.VMEM((PLE, D), F32)],
        compiler_params=_params(1),
    )(h2, p, target, w_pg, w_ple, g_ple, g_fin)


def _mlp_bwd(dh2, a, h1, g, w_up, w_down_a, w_down_b):
    tm = TM

    def body(d_ref, a_ref, h_ref, g_ref, wu_ref, wa_ref, wb_ref, da_ref, dh1_ref, dg_ref):
        @pl.when(pl.program_id(0) == 0)
        def _():
            dg_ref[...] = jnp.zeros_like(dg_ref)

        dh2v = d_ref[...]
        db = _bf(dh2v)
        du = jnp.zeros((tm, D), F32)
        for c in range(NDEV):
            cols = slice(c * 512, (c + 1) * 512)
            dr = jnp.concatenate([_dot_nt(db, wa_ref[c]), _dot_nt(db, wb_ref[c])], axis=1)
            da = _bf(dr * (2.0 * jnp.maximum(a_ref[:, cols], 0.0)))
            da_ref[:, cols] = da
            du = du + _dot_nt(da, wu_ref[c])
        n, rs = _rms(h_ref[...])
        dg_ref[...] = dg_ref[...] + jnp.sum(du * n, axis=0, keepdims=True)
        dh1_ref[...] = dh2v + _rms_bwd(du, n, rs, g_ref[...])

    row = lambda wd: pl.BlockSpec((tm, wd), lambda i: (i, 0))
    return pl.pallas_call(
        body, name="mlp_bwd", grid=(S // tm,),
        in_specs=[row(D), row(DFF), row(D), _cspec((1, D)), _cspec((NDEV, D, DFF // NDEV)), _cspec((NDEV, HALF, D)),
                  _cspec((NDEV, HALF, D))],
        out_specs=[row(DFF), row(D), pl.BlockSpec((1, D), lambda i: (0, 0))],
        out_shape=[jax.ShapeDtypeStruct((S, DFF), BF16), jax.ShapeDtypeStruct((S, D), F32),
                   jax.ShapeDtypeStruct((1, D), F32)],
        compiler_params=_params(1),
    )(dh2, a, h1, g, w_up, w_down_a, w_down_b)


def _out_proj_bwd(dh1, attn, ml, w):
    tm = TM

    def body(d_ref, a_ref, m_ref, w_ref, da_ref, dm_ref, dw_ref, acc):
        i = pl.program_id(0)

        @pl.when(i == 0)
        def _():
            acc[...] = jnp.zeros_like(acc)

        db = _bf(d_ref[...])
        dmix = _dot_nt(db, w_ref[...])
        da_ref[...] = dmix[:, 0:AW]
        dm_ref[...] = dmix[:, AW:D]
        acc[0:AW, :] = acc[0:AW, :] + _dot_tn(_bf(a_ref[...]), db)
        acc[AW:D, :] = acc[AW:D, :] + _dot_tn(_bf(m_ref[...]), db)

        @pl.when(i == S // tm - 1)
        def _():
            dw_ref[...] = _bf(acc[...])

    row = lambda wd: pl.BlockSpec((tm, wd), lambda i: (i, 0))
    return pl.pallas_call(
        body, name="out_proj_bwd", grid=(S // tm,),
        in_specs=[row(D), row(AW), row(MW), _cspec((D, D))],
        out_specs=[row(AW), row(MW), pl.BlockSpec((D, D), lambda i: (0, 0))],
        out_shape=[jax.ShapeDtypeStruct((S, AW), F32), jax.ShapeDtypeStruct((S, MW), F32),
                   jax.ShapeDtypeStruct((D, D), BF16)],
        scratch_shapes=[pltpu.VMEM((D, D), F32)],
        compiler_params=_params(1),
    )(dh1, attn, ml, w)


CHIP_FLIPS = [(0, 0), (0, 1), (1, 0), (1, 1)]


def _scatter2_phases(in_ref, out_ref, mine_v, sib_v, psum_v, loc_sems, d2d_send, d2d_recv, ici_send, ici_recv, own_sem):
    x, y, c = _place()
    chips = [((x + dx) % 2, (y + dy) % 2) for dx, dy in CHIP_FLIPS]
    nc = len(chips)

    def local(k):
        return pltpu.make_async_copy(in_ref.at[_dev_index(*chips[k], c)], mine_v.at[k], loc_sems.at[k])

    def to_sib(k):
        return pltpu.make_async_remote_copy(
            src_ref=in_ref.at[_dev_index(*chips[k], 1 - c)], dst_ref=sib_v.at[k], send_sem=d2d_send.at[k],
            recv_sem=d2d_recv.at[k], device_id=(x, y, 1 - c), device_id_type=MESH)

    def over_ici(k):
        return pltpu.make_async_remote_copy(
            src_ref=psum_v.at[k], dst_ref=out_ref.at[k], send_sem=ici_send.at[k - 1], recv_sem=ici_recv.at[k - 1],
            device_id=(*chips[k], c), device_id_type=MESH)

    def own():
        return pltpu.make_async_copy(psum_v.at[0], out_ref.at[0], own_sem)

    def start():
        for k in range(nc):
            to_sib(k).start()
            local(k).start()

    def middle():
        for k in (1, 2, 3, 0):
            local(k).wait()
            to_sib(k).wait_recv()
            psum_v[k] = _bf(mine_v[k].astype(F32) + sib_v[k].astype(F32))
            (over_ici(k) if k else own()).start()

    def finish():
        for k in range(1, nc):
            over_ici(k).wait()
        for k in range(nc):
            to_sib(k).wait_send()
        own().wait()

    return start, middle, finish


def _scatter2_scratch(shard, dtype):
    nc = len(CHIP_FLIPS)
    return ([pltpu.VMEM((nc, *shard), dtype)] * 3
            + [pltpu.SemaphoreType.DMA((nc,))] * 3 + [pltpu.SemaphoreType.DMA((nc - 1,))] * 2 + [pltpu.SemaphoreType.DMA])


def _in_proj_bwd(dparts, n_roped, rope, dh1, x, g1, w, part):
    tm = TM
    nt = S // tm
    widths = [d.shape[1] for d in dparts]
    assert sum(widths) == PW
    npar = len(dparts)

    def body(*refs):
        d_refs = refs[:npar]
        tabs = [t[...] for t in refs[npar:npar + 3]]
        dh_ref, x_ref, g_ref, w_ref, in_ref, dx_ref, dg_ref, out_ref = refs[npar + 3:npar + 11]
        rs_start, rs_middle, rs_finish = _scatter2_phases(in_ref, out_ref, *refs[npar + 11:])
        i = pl.program_id(0)
        pl.when(i == 0)(rs_start)
        pl.when(i == 1)(rs_middle)

        @pl.when(i == 0)
        def _():
            dg_ref[...] = jnp.zeros_like(dg_ref)

        du = jnp.zeros((tm, D), F32)
        off = 0
        for j, (d_ref, wd) in enumerate(zip(d_refs, widths)):
            nc = next(c for c in (768, 512) if wd % c == 0)
            for s in range(wd // nc):
                d = d_ref[:, s * nc:(s + 1) * nc]
                du = du + _dot_nt(_unrope(d, *tabs) if j < n_roped else d, w_ref[:, off + s * nc:off + (s + 1) * nc])
            off += wd
        n, rs = _rms(x_ref[...])
        dg_ref[...] = dg_ref[...] + jnp.sum(du * n, axis=0, keepdims=True)
        dx_ref[...] = dh_ref[...] + _rms_bwd(du, n, rs, g_ref[...])
        pl.when(i == nt - 1)(rs_finish)

    row = lambda wd: pl.BlockSpec((tm, wd), lambda i: (i, 0))
    shard = part.shape[1:]
    return pl.pallas_call(
        body, name="in_proj_bwd", grid=(nt,),
        in_specs=[row(wd) for wd in widths] + [row(128)] * 3 + [row(D), row(D), _cspec((1, D)), _cspec((D, PW)), ANY],
        out_specs=[row(D), pl.BlockSpec((1, D), lambda i: (0, 0)), ANY],
        out_shape=[jax.ShapeDtypeStruct((S, D), F32), jax.ShapeDtypeStruct((1, D), F32),
                   jax.ShapeDtypeStruct((len(CHIP_FLIPS), *shard), part.dtype)],
        scratch_shapes=_scatter2_scratch(shard, part.dtype),
        compiler_params=_params(1),
    )(*dparts, *rope, dh1, x, g1, w, part)


SMALL_ROWS = 96


def _small_phases(ins, out_ref, pack, rbuf, send_sems, recv_sems):
    x, y, c = _place()
    me = _dev_index(x, y, c)

    def copies():
        out = []
        for k, (dx, dy, dc) in enumerate(FLIPS):
            peer = ((x + dx) % 2, (y + dy) % 2, (c + dc) % 2)
            out.append(pltpu.make_async_remote_copy(
                src_ref=pack, dst_ref=rbuf.at[me], send_sem=send_sems.at[k], recv_sem=recv_sems.at[k],
                device_id=peer, device_id_type=MESH))
        return out

    def start():
        pack[...] = jnp.zeros_like(pack)
        for i, ref in enumerate(ins):
            pack[8 * i:8 * i + 1, 0:ref.shape[1]] = ref[...]
        rbuf[me] = pack[...]
        for cp in copies():
            cp.start()

    def finish():
        for cp in copies():
            cp.wait()
        tot = rbuf[0]
        for j in range(1, NDEV):
            tot = tot + rbuf[j]
        out_ref[...] = tot

    return start, finish


def _wgrad(name, A, Bs, a_fn, b_fn, out_shape, split=None, ts=512, small=(), rope=(), n_roped=0):
    K = A.shape[1]
    widths = [b.shape[1] for b in Bs]
    N = sum(widths)
    nb, ns, nrt = len(Bs) + len(rope), len(small), S // ts
    kc = min(K, 1024)

    def body(*refs):
        a_ref, b_refs = refs[0], refs[1:1 + len(Bs)]
        tabs = [t[...] for t in refs[1 + len(Bs):1 + nb]]
        o_ref = refs[1 + nb + ns]
        acc = refs[2 + nb + ns + bool(ns)]
        r = pl.program_id(0)
        if ns:
            sm_start, sm_finish = _small_phases(refs[1 + nb:1 + nb + ns], refs[2 + nb + ns], *refs[4 + nb + ns:])
            pl.when(r == 0)(sm_start)

        @pl.when(r == 0)
        def _():
            acc[...] = jnp.zeros_like(acc)

        bs, off = [], 0
        for i, (b_ref, w) in enumerate(zip(b_refs, widths)):
            nc = next(c for c in (1024, 768, 512) if w % c == 0)
            fn = (lambda t: _unrope(t, *tabs)) if i < n_roped else b_fn
            bs += [(off + c * nc, nc, fn(b_ref[:, c * nc:(c + 1) * nc])) for c in range(w // nc)]
            off += w
        for kk in range(K // kc):
            rows = slice(kk * kc, (kk + 1) * kc)
            at = a_fn(a_ref[:, rows]).T
            for lo, nc, b in bs:
                acc[rows, lo:lo + nc] = acc[rows, lo:lo + nc] + _dot(at, b)

        @pl.when(r == nrt - 1)
        def _():
            if split is None:
                o_ref[...] = _bf(acc[...])
            else:
                for j in range(NDEV):
                    o_ref[j] = _bf(acc[:, split * j:split * (j + 1)])

        if ns:
            pl.when(r == nrt - 1)(sm_finish)

    in_specs = ([pl.BlockSpec((ts, K), lambda r: (r, 0))] + [pl.BlockSpec((ts, w), lambda r: (r, 0)) for w in widths]
                + [pl.BlockSpec((ts, 128), lambda r: (r, 0))] * len(rope))
    out_spec = pl.BlockSpec(out_shape, lambda r: (0,) * len(out_shape))
    scratch = [pltpu.VMEM((K, N), F32)]
    if not ns:
        return pl.pallas_call(
            body, name=name, grid=(nrt,), in_specs=in_specs, out_specs=out_spec,
            out_shape=jax.ShapeDtypeStruct(out_shape, BF16), scratch_shapes=scratch, compiler_params=_params(1),
        )(A, *Bs, *rope)
    return pl.pallas_call(
        body, name=name, grid=(nrt,), in_specs=in_specs + [VM] * ns, out_specs=[out_spec, VM],
        out_shape=[jax.ShapeDtypeStruct(out_shape, BF16), jax.ShapeDtypeStruct((SMALL_ROWS, 1024), F32)],
        scratch_shapes=scratch + [pltpu.VMEM((SMALL_ROWS, 1024), F32), pltpu.VMEM((NDEV, SMALL_ROWS, 1024), F32),
                                  pltpu.SemaphoreType.DMA((7,)), pltpu.SemaphoreType.DMA((7,))],
        compiler_params=_params(1),
    )(A, *Bs, *rope, *small)


def _relu2_bf(a):
    r = jnp.maximum(a.astype(F32), 0.0)
    return _bf(r * r)


def _ident(a):
    return a


def _step(x, p, target, g1, conv_b, gate_b, gn, g_mlp, g_ple, g_fin, sh):
    (g_in, g_conv), (rc, ra, rb) = _gather_weights([sh["w_in"], sh["conv_w"]], [BF16, F32])
    conv_w = g_conv.transpose(1, 0, 2).reshape(4, 1024)
    w_in_p = _join_w_in(g_in)
    (qkv, mqk, mv, mo, gates, u1), (w_out8, w_pg8, w_ple8) = _in_proj(
        x, g1, w_in_p, rc, ra, rb, [sh["w_out"], sh["w_ple_gate"], sh["w_ple"]], [BF16] * 3)
    attn, lse, (w_up8, w_down_a) = _attn_fwd(qkv, [sh["w_up"], sh["w_down"][0:HALF]], [BF16] * 2)
    ml, cs, ns, ms, (w_down_b,) = _mlstm_fwd(mqk, mv, mo, gates, conv_w, conv_b, gate_b, gn,
                                             [sh["w_down"][HALF:2 * HALF]], [BF16])
    w_out, w_pg = w_out8.reshape(D, D), w_pg8.reshape(D, D)
    h1, u2 = _out_proj(x, attn, ml, w_out, g_mlp)
    a, h2 = _mlp_fwd(h1, u2, w_up8, w_down_a, w_down_b)
    dh2, dw_pg, dw_ple8, dg_ple, dg_fin, loss = _ple_loss(h2, p, target, w_pg, w_ple8, g_ple, g_fin)
    da, dh1, dg_mlp = _mlp_bwd(dh2, a, h1, g_mlp, w_up8, w_down_a, w_down_b)
    dw_up8 = _wgrad("wgrad_up", u2, [da], _ident, _ident, (NDEV, D, DFF // NDEV), split=DFF // NDEV)
    dw_down = _wgrad("wgrad_down", a, [dh2], _relu2_bf, _bf, (DFF, D))
    d_attn, d_ml, dw_out = _out_proj_bwd(dh1, attn, ml, w_out)
    (dm, dconv_w, dconv_b, dgn, dgate_b), (r_down,) = _mlstm_bwd(
        mqk, mv, mo, gates, conv_w, conv_b, gate_b, gn, cs, ns, ms, d_ml, [dw_down.reshape(NDEV, DFF // NDEV, D)])
    dq, dk, dv, (r_up, r_out, r_pg, r_ple) = _attn_bwd(
        qkv, attn, lse, d_attn,
        [dw_up8, dw_out.reshape(NDEV, D // NDEV, D), dw_pg.reshape(NDEV, D // NDEV, D), dw_ple8])
    dparts = [dq, dk, dv, dm]
    small = [jnp.zeros((1, D), F32), dconv_b, dgate_b, dgn, dg_mlp, dg_ple, dg_fin, loss]
    dw_in8, total = _wgrad("wgrad_in", u1, dparts, _ident, _ident, (NDEV, D, IN_W // NDEV), split=IN_W // NDEV,
                           small=small + [dconv_w[j:j + 1] for j in range(4)], rope=(rc, ra, rb), n_roped=2)
    dx, dg1, r_in = _in_proj_bwd(dparts, 2, (rc, ra, rb), dh1, x, g1, w_in_p, dw_in8)
    recv = dict(w_in=r_in, w_out=r_out, w_up=r_up, w_down=r_down, w_ple_gate=r_pg, w_ple=r_ple)
    return dx, recv, total, _allreduce_vec(dg1)


def _gather_weights(shards, dtypes):
    nw = len(shards)

    def body(*refs):
        ins, parts = refs[:nw], refs[nw:nw + 4]
        outs, tables = refs[nw + 4:2 * nw + 4], refs[2 * nw + 4:2 * nw + 7]
        start, forward, finish = _gather_phases(ins, outs, refs[2 * nw + 7:3 * nw + 7], *refs[3 * nw + 7:])
        start()
        _rope_fill(*parts, *tables)
        forward()
        finish()

    res = pl.pallas_call(
        body, name="gather_weights",
        in_specs=[VM] * (nw + 4), out_specs=[ANY] * nw + [VM] * 3,
        out_shape=_gather_shapes(shards, dtypes) + [jax.ShapeDtypeStruct((S, 128), F32)] * 3,
        scratch_shapes=_gather_scratch(shards, dtypes),
        compiler_params=_params(),
    )(*shards, *_rope_parts())
    return res[:nw], res[nw:]


def _allreduce_vec(v):
    def body(v_ref, out_ref, pack, rbuf, send_sems, recv_sems):
        start, finish = _small_phases([v_ref], out_ref, pack, rbuf, send_sems, recv_sems)
        start()
        finish()

    return pl.pallas_call(
        body, name="allreduce_last", out_shape=jax.ShapeDtypeStruct((8, 1024), F32),
        scratch_shapes=[pltpu.VMEM((8, 1024), F32), pltpu.VMEM((NDEV, 8, 1024), F32),
                        pltpu.SemaphoreType.DMA((7,)), pltpu.SemaphoreType.DMA((7,))],
        compiler_params=_params(),
    )(v)


def _adamw(name, gparts, w, m, v, tr):
    P, R, C = gparts.shape

    def body(g_ref, w_ref, m_ref, v_ref, go_ref, d_ref, mo_ref, vo_ref):
        g = g_ref[0].astype(F32)
        for j in range(1, P):
            g = g + g_ref[j].astype(F32)
        go_ref[...] = g
        d_ref[...], mo_ref[...], vo_ref[...] = _adam_update(g, w_ref[...], m_ref[...], v_ref[...])

    row = pl.BlockSpec((tr, C), lambda i: (i, 0))
    return pl.pallas_call(
        body, name=name, grid=(R // tr,),
        in_specs=[pl.BlockSpec((P, tr, C), lambda i: (0, i, 0)), row, row, row],
        out_specs=[row] * 4,
        out_shape=[jax.ShapeDtypeStruct((R, C), F32)] * 4,
        compiler_params=_params(1),
    )(gparts, w, m, v)


SMALL = ("norm_mix_g", "conv_b", "gate_b", "mlstm_norm_g", "norm_mlp_g", "norm_ple_g", "final_norm_g")


def _adam_update(g, w, m, v):
    c1 = 1.0 - ADAM_B1 ** ADAM_STEP
    c2 = 1.0 - ADAM_B2 ** ADAM_STEP
    m2 = ADAM_B1 * m + (1.0 - ADAM_B1) * g
    v2 = ADAM_B2 * v + (1.0 - ADAM_B2) * (g * g)
    return -ADAM_LR * ((m2 / c1) / (jnp.sqrt(v2 / c2) + ADAM_EPS) + ADAM_WD * w), m2, v2


def _adamw_small(total, first, ws, ms, vs):
    n = len(ws)

    def body(*refs):
        t_ref, f_ref = refs[:2]
        refs = refs[1:]
        outs = refs[1 + 3 * n:]
        for i in range(n):
            w_ref, m_ref, v_ref = refs[1 + i], refs[1 + n + i], refs[1 + 2 * n + i]
            g = (t_ref if i else f_ref)[8 * i:8 * i + 1, 0:w_ref.shape[1]]
            delta, m2, v2 = _adam_update(g, w_ref[...], m_ref[...], v_ref[...])
            for ref, val in zip(outs[4 * i:4 * i + 4], (g, delta, m2, v2)):
                ref[...] = val

    res = pl.pallas_call(
        body, name="adamw_small",
        out_shape=[jax.ShapeDtypeStruct(w.shape, F32) for w in ws for _ in range(4)],
        compiler_params=_params(),
    )(total, first, *ws, *ms, *vs)
    return [res[4 * i:4 * i + 4] for i in range(n)]


def kernel(x, p, norm_mix_g, w_in, conv_w, conv_b, gate_b, mlstm_norm_g, w_out, norm_mlp_g, w_up, w_down, norm_ple_g, w_ple_gate, w_ple, final_norm_g, loss_target, m_norm_mix_g, m_w_in, m_conv_w, m_conv_b, m_gate_b, m_mlstm_norm_g, m_w_out, m_norm_mlp_g, m_w_up, m_w_down, m_norm_ple_g, m_w_ple_gate, m_w_ple, m_final_norm_g, v_norm_mix_g, v_w_in, v_conv_w, v_conv_b, v_gate_b, v_mlstm_norm_g, v_w_out, v_norm_mlp_g, v_w_up, v_w_down, v_norm_ple_g, v_w_ple_gate, v_w_ple, v_final_norm_g):
    big_names = ("w_in", "conv_w", "w_out", "w_up", "w_down", "w_ple_gate", "w_ple")
    wts = dict(w_in=w_in, conv_w=conv_w, w_out=w_out, w_up=w_up, w_down=w_down, w_ple_gate=w_ple_gate, w_ple=w_ple)
    mom = dict(w_in=m_w_in, conv_w=m_conv_w, w_out=m_w_out, w_up=m_w_up, w_down=m_w_down, w_ple_gate=m_w_ple_gate,
               w_ple=m_w_ple)
    var = dict(w_in=v_w_in, conv_w=v_conv_w, w_out=v_w_out, w_up=v_w_up, w_down=v_w_down, w_ple_gate=v_w_ple_gate,
               w_ple=v_w_ple)
    sq = lambda a: a.reshape(a.shape[1:])
    fin = final_norm_g.reshape(1, D)
    dx, recv, total, first = _step(
        x[0], p[0, 0], loss_target[0], norm_mix_g, conv_b, jnp.pad(gate_b, ((0, 0), (0, 120))), mlstm_norm_g,
        norm_mlp_g, norm_ple_g, fin, {n: sq(wts[n]) for n in big_names})

    nrow = 8 * len(SMALL)
    me = _dev_index(*_place())
    conv_rows = total[nrow + 8:nrow + 40:8]
    recv["conv_w"] = lax.dynamic_slice_in_dim(conv_rows, me * 128, 128, axis=1).reshape(1, 4, 128)
    out = {}
    for n, tr in zip(big_names, (256, 4, 128, 256, 256, 128, 256)):
        res = _adamw("adamw_" + n, recv[n], sq(wts[n]), sq(mom[n]), sq(var[n]), tr)
        out[n] = [t.reshape(wts[n].shape) for t in res]
    sw = dict(norm_mix_g=norm_mix_g, conv_b=conv_b, gate_b=gate_b, mlstm_norm_g=mlstm_norm_g, norm_mlp_g=norm_mlp_g,
              norm_ple_g=norm_ple_g, final_norm_g=fin)
    sm = dict(norm_mix_g=m_norm_mix_g, conv_b=m_conv_b, gate_b=m_gate_b, mlstm_norm_g=m_mlstm_norm_g,
              norm_mlp_g=m_norm_mlp_g, norm_ple_g=m_norm_ple_g, final_norm_g=m_final_norm_g.reshape(1, D))
    sv = dict(norm_mix_g=v_norm_mix_g, conv_b=v_conv_b, gate_b=v_gate_b, mlstm_norm_g=v_mlstm_norm_g,
              norm_mlp_g=v_norm_mlp_g, norm_ple_g=v_norm_ple_g, final_norm_g=v_final_norm_g.reshape(1, D))
    res = _adamw_small(total, first, [sw[n] for n in SMALL], [sm[n] for n in SMALL], [sv[n] for n in SMALL])
    for n, r in zip(SMALL, res):
        out[n] = [t.reshape(final_norm_g.shape) for t in r] if n == "final_norm_g" else list(r)
    order = ("norm_mix_g", "w_in", "conv_w", "conv_b", "gate_b", "mlstm_norm_g", "w_out", "norm_mlp_g", "w_up", "w_down",
             "norm_ple_g", "w_ple_gate", "w_ple", "final_norm_g")
    loss_all = total[nrow, 0]
    return (loss_all, dx[None], *[out[n][0] for n in order], *[out[n][1] for n in order],
            *[out[n][2] for n in order], *[out[n][3] for n in order])
```

```python
import functools
import math

import jax
import jax.numpy as jnp
from jax import lax
from jax.experimental import pallas as pl
from jax.experimental.pallas import tpu as pltpu

F32, BF16 = jnp.float32, jnp.bfloat16
S = 4096
D = 1024
AW = 512
MW = 512
DFF = 4096
PLE = 256
IN_W = 3592
PW = 3840
NDEV = 8
EPS = 1e-6
NEG = -1e30
LC = 128
TB = 256
ROPE_THETA = 500000.0
VMEM_LIMIT = 56 * 1024 * 1024
HI = lax.Precision.HIGHEST

ADAM_LR, ADAM_B1, ADAM_B2, ADAM_EPS, ADAM_WD, ADAM_STEP = 0.001, 0.9, 0.999, 1e-08, 0.01, 10


def _params(n_grid=0, **kw):
    sem = dict(dimension_semantics=("arbitrary",) * n_grid) if n_grid else {}
    return pltpu.CompilerParams(vmem_limit_bytes=VMEM_LIMIT, **sem, **kw)


def _cspec(shape):
    nd = len(shape)
    return pl.BlockSpec(shape, lambda *_: (0,) * nd, pipeline_mode=pl.Buffered(1))


def _dot(a, b):
    return jnp.dot(a, b, preferred_element_type=F32)


def _dot_nt(a, b):
    return lax.dot_general(a, b, (((1,), (1,)), ((), ())), preferred_element_type=F32)


def _dot_tn(a, b):
    return lax.dot_general(a, b, (((0,), (0,)), ((), ())), preferred_element_type=F32)


def _bf(x):
    return x.astype(BF16)


def _rms(x):
    rs = lax.rsqrt(jnp.mean(x * x, axis=-1, keepdims=True) + EPS)
    return x * rs, rs


def _rms_bwd(du, n, rs, g):
    dn = du * g
    return rs * (dn - n * jnp.mean(dn * n, axis=-1, keepdims=True))


def _sigmoid(x):
    return 1.0 / (1.0 + jnp.exp(-x))


ROPE_BLK = 512


def _rope_parts():
    def cs(n, step):
        j = lax.broadcasted_iota(jnp.int32, (n, 128), 1) % 64
        pos = (lax.broadcasted_iota(jnp.int32, (n, 128), 0) * step).astype(F32)
        ang = pos * jnp.power(ROPE_THETA, -(j % 8).astype(F32) / 8.0)
        return jnp.cos(ang), jnp.sin(ang)

    return (*cs(ROPE_BLK, 1), *cs(S // ROPE_BLK, ROPE_BLK))


def _rope_fill(co_ref, so_ref, cb_ref, sb_ref, rc_ref, ra_ref, rb_ref):
    j = lax.broadcasted_iota(jnp.int32, (ROPE_BLK, 128), 1) % 64
    co, so = co_ref[...], so_ref[...]
    for t in range(S // ROPE_BLK):
        cb, sb = cb_ref[t:t + 1, :], sb_ref[t:t + 1, :]
        cos, sin = cb * co - sb * so, sb * co + cb * so
        rows = slice(t * ROPE_BLK, (t + 1) * ROPE_BLK)
        rc_ref[rows, :] = jnp.where(j < 16, cos, 1.0)
        ra_ref[rows, :] = jnp.where(j < 8, -sin, 0.0)
        rb_ref[rows, :] = jnp.where((j >= 8) & (j < 16), sin, 0.0)


def _rope(blk, c, a, b):
    return blk * c + pltpu.roll(blk, 120, 1) * a + pltpu.roll(blk, 8, 1) * b


def _rope_bwd(d, c, a, b):
    return d * c + pltpu.roll(d * a, 8, 1) + pltpu.roll(d * b, 120, 1)


def _unrope(t, c, a, b):
    return jnp.concatenate([_bf(_rope_bwd(t[:, j * 128:(j + 1) * 128].astype(F32), c, a, b))
                            for j in range(t.shape[1] // 128)], axis=1)


MESH = pl.DeviceIdType.MESH
ANY = pl.BlockSpec(memory_space=pl.ANY)
VM = pl.BlockSpec(memory_space=pltpu.VMEM)
FLIPS = [(dx, dy, dc) for dx in (0, 1) for dy in (0, 1) for dc in (0, 1)][1:]


def _place():
    return lax.axis_index("x"), lax.axis_index("y"), lax.axis_index("c")


def _dev_index(px, py, pc):
    return 4 * px + 2 * py + pc


def _gather_phases(ins, outs, bufs, send_sems=None, recv_sems=None, local_sems=None):
    nw = len(ins)
    if nw == 0:
        return (lambda: None,) * 3
    x, y, c = _place()
    me, sib = (x, y, c), (x, y, 1 - c)
    chips = [(1 - x, y), (x, 1 - y), (1 - x, 1 - y)]

    def copy(w, k, block, to, from_buf=False):
        dst = outs[w].at[_dev_index(*block)]
        return pltpu.make_async_remote_copy(
            src_ref=bufs[w] if from_buf else dst, dst_ref=dst, send_sem=send_sems.at[w, k],
            recv_sem=recv_sems.at[w, k], device_id=to, device_id_type=MESH)

    def mine(w):
        return pltpu.make_async_copy(bufs[w], outs[w].at[_dev_index(*me)], local_sems.at[w])

    def first(w):
        return [copy(w, 0, me, sib, True)] + [copy(w, 1 + j, me, (*chip, c), True) for j, chip in enumerate(chips)]

    def passed(w):
        return [copy(w, 4 + j, (*chip, c), sib) for j, chip in enumerate(chips)]

    def start():
        for w in range(nw):
            bufs[w][...] = ins[w][...].astype(bufs[w].dtype)
        for w in range(nw):
            mine(w).start()
            for cp in first(w):
                cp.start()

    def forward():
        for j, chip in enumerate(chips):
            for w in range(nw):
                copy(w, 1 + j, (*chip, c), me).wait_recv()
                passed(w)[j].start()

    def finish():
        for w in range(nw):
            copy(w, 0, sib, me).wait_recv()
        for j, chip in enumerate(chips):
            for w in range(nw):
                copy(w, 4 + j, (*chip, 1 - c), me).wait_recv()
        for w in range(nw):
            for cp in first(w) + passed(w):
                cp.wait_send()
            mine(w).wait()

    return start, forward, finish


def _gather_scratch(shards, dtypes):
    nw = len(shards)
    if nw == 0:
        return []
    return ([pltpu.VMEM(s.shape, dt) for s, dt in zip(shards, dtypes)]
            + [pltpu.SemaphoreType.DMA((nw, 7)), pltpu.SemaphoreType.DMA((nw, 7)), pltpu.SemaphoreType.DMA((nw,))])


def _gather_shapes(shards, dtypes):
    return [jax.ShapeDtypeStruct((NDEV, *s.shape), dt) for s, dt in zip(shards, dtypes)]


def _scatter_phases(ins, outs, send_sems=None, recv_sems=None, local_sems=None):
    nw = len(ins)
    if nw == 0:
        return (lambda: None,) * 2
    x, y, c = _place()
    me = _dev_index(x, y, c)

    def copies():
        out = []
        for w in range(nw):
            out.append(pltpu.make_async_copy(ins[w].at[me], outs[w].at[me], local_sems.at[w]))
            for k, (dx, dy, dc) in enumerate(FLIPS):
                peer = ((x + dx) % 2, (y + dy) % 2, (c + dc) % 2)
                out.append(pltpu.make_async_remote_copy(
                    src_ref=ins[w].at[_dev_index(*peer)], dst_ref=outs[w].at[me], send_sem=send_sems.at[w, k],
                    recv_sem=recv_sems.at[w, k], device_id=peer, device_id_type=MESH))
        return out

    def start():
        for cp in copies():
            cp.start()

    def finish():
        for cp in copies():
            cp.wait()

    return start, finish


def _scatter_scratch(nw):
    if nw == 0:
        return []
    return [pltpu.SemaphoreType.DMA((nw, 7)), pltpu.SemaphoreType.DMA((nw, 7)), pltpu.SemaphoreType.DMA((nw,))]


TM = 512


def _join_w_in(wg):
    sw = IN_W // NDEV

    def body(wg_ref, w_ref):
        for j in range(NDEV):
            w_ref[:, sw * j:sw * (j + 1)] = wg_ref[j]
        w_ref[:, IN_W:PW] = jnp.zeros((D, PW - IN_W), BF16)

    return pl.pallas_call(body, name="join_w_in", out_shape=jax.ShapeDtypeStruct((D, PW), BF16),
                          compiler_params=_params())(wg)


def _in_proj(x, g1, w, rc, ra, rb, shards, dtypes):
    tm = TM
    nw = len(shards)
    nt = S // tm

    def body(*refs):
        x_ref, g_ref, w_ref, rc_ref, ra_ref, rb_ref = refs[:6]
        ins = refs[6:6 + nw]
        qkv_ref, mqk_ref, mv_ref, mo_ref, gt_ref, u_ref = refs[6 + nw:12 + nw]
        outs = refs[12 + nw:12 + 2 * nw]
        bufs = refs[12 + 2 * nw:12 + 3 * nw]
        ag_start, ag_forward, ag_finish = _gather_phases(ins, outs, bufs, *refs[12 + 3 * nw:])
        i = pl.program_id(0)
        pl.when(i == 0)(ag_start)
        pl.when(i == nt - 2)(ag_forward)
        n, _ = _rms(x_ref[...])
        u = _bf(n * g_ref[...])
        u_ref[...] = u
        c, a, b = rc_ref[...], ra_ref[...], rb_ref[...]
        for half in range(2):
            blk = _dot(u, w_ref[:, half * 512:(half + 1) * 512])
            for t in range(4):
                lo = half * 512 + t * 128
                qkv_ref[:, lo:lo + 128] = _rope(blk[:, t * 128:(t + 1) * 128], c, a, b)
        qkv_ref[:, 1024:1536] = _dot(u, w_ref[:, 1024:1536])
        mqk_ref[:, 0:512] = _dot(u, w_ref[:, 1536:2048])
        mqk_ref[:, 512:1024] = _dot(u, w_ref[:, 2048:2560])
        mv_ref[...] = _dot(u, w_ref[:, 2560:3072])
        mo_ref[...] = _dot(u, w_ref[:, 3072:3584])
        gt_ref[...] = _dot(u, w_ref[:, 3584:3712])
        pl.when(i == nt - 1)(ag_finish)

    row = lambda wd: pl.BlockSpec((tm, wd), lambda i: (i, 0))
    res = pl.pallas_call(
        body, name="in_proj", grid=(nt,),
        in_specs=[row(D), _cspec((1, D)), _cspec((D, PW)), row(128), row(128), row(128)] + [VM] * nw,
        out_specs=[row(1536), row(1024), row(512), row(512), row(128), row(D)] + [ANY] * nw,
        out_shape=[jax.ShapeDtypeStruct((S, 1536), F32), jax.ShapeDtypeStruct((S, 1024), F32),
                   jax.ShapeDtypeStruct((S, 512), F32), jax.ShapeDtypeStruct((S, 512), F32),
                   jax.ShapeDtypeStruct((S, 128), F32), jax.ShapeDtypeStruct((S, D), BF16)]
        + _gather_shapes(shards, dtypes),
        scratch_shapes=_gather_scratch(shards, dtypes),
        compiler_params=_params(1),
    )(x, g1, w, rc, ra, rb, *shards)
    return res[:6], res[6:]


DILATIONS = (16, 4, 1)


def _attn_valid(n):
    kd = lax.broadcasted_iota(jnp.int32, (128, 256), 1) - lax.broadcasted_iota(jnp.int32, (128, 256), 0)
    off = jnp.where(n == 0, 0, 128)
    return (kd <= off) & (kd >= off - 128)


def _attn_rows(d, r, n):
    if d == 1:
        q0 = pl.multiple_of(n * 128, 128)
        k0 = pl.multiple_of(jnp.maximum(n - 1, 0) * 128, 128)
        return pl.ds(q0, 128), pl.ds(k0, 256), _attn_valid(n)
    q0 = r + n * 128 * d
    k0 = r + jnp.maximum(n - 1, 0) * 128 * d
    return pl.ds(q0, 128, stride=d), pl.ds(k0, 256, stride=d), _attn_valid(n)


ATTN_GROUP = 4
ATTN_ITERS = S // 128 // ATTN_GROUP


def _attn_group(d, i):
    nb = S // (128 * d)
    if nb == 2:
        qi = lax.broadcasted_iota(jnp.int32, (256, 256), 0) - lax.broadcasted_iota(jnp.int32, (256, 256), 1)
        whole = [pl.ds((ATTN_GROUP // 2) * i + u, 256, stride=d) for u in range(ATTN_GROUP // 2)]
        return [(rows, rows, (qi >= 0) & (qi <= 128)) for rows in whole]
    if d == 1:
        return [_attn_rows(1, 0, i + ATTN_ITERS * u) for u in range(ATTN_GROUP)]
    return [_attn_rows(d, (i // nb) * ATTN_GROUP + u, i % nb) for u in range(ATTN_GROUP)]


def _head0(shape):
    return lax.broadcasted_iota(jnp.int32, shape, 1) < 64


def _stack_heads(t):
    h0 = _head0(t.shape)
    tb = _bf(t)
    zero = jnp.zeros_like(tb)
    return jnp.concatenate([jnp.where(h0, tb, zero), jnp.where(h0, zero, tb)], axis=0)


def _attn_fwd(qkv, shards, dtypes):
    nw = len(shards)

    def body(*refs):
        q_ref, k_ref, v_ref = refs[:3]
        ins = refs[3:3 + nw]
        o_ref, lse0_ref, lse1_ref = refs[3 + nw:6 + nw]
        outs = refs[6 + nw:6 + 2 * nw]
        m0, m1, l0, l1, acc = refs[6 + 2 * nw:11 + 2 * nw]
        bufs = refs[11 + 2 * nw:11 + 3 * nw]
        ag_start, ag_forward, ag_finish = _gather_phases(ins, outs, bufs, *refs[11 + 3 * nw:])
        hp = pl.program_id(0)
        pl.when(hp == 0)(ag_start)
        pl.when(hp == 3)(ag_forward)
        stats = (m0, m1, l0, l1, acc)

        def update(blocks, first):
            loaded = [([q_ref[rq, :], k_ref[rk, :], v_ref[rk, :]], None if first else [ref[rq, :] for ref in stats])
                      for rq, rk, _ in blocks]
            results = []
            for ((q, k, v), prev), (_, _, valid) in zip(loaded, blocks):
                head0 = _head0(q.shape)
                kb, vb = _bf(k), _bf(v)
                q = q * 0.125
                m_new, l_new, acc_new = [], [], []
                for a, qa in enumerate((_bf(jnp.where(head0, q, 0.0)), _bf(jnp.where(head0, 0.0, q)))):
                    s = jnp.where(valid, _dot_nt(qa, kb), NEG)
                    mc = jnp.max(s, axis=-1, keepdims=True)
                    m_a = jnp.broadcast_to(mc, q.shape) if first else jnp.maximum(prev[a], mc)
                    p = jnp.exp(s - jnp.tile(m_a, (1, 2)))
                    l_add = jnp.sum(p, axis=-1, keepdims=True)
                    pv = _dot(_bf(p), vb)
                    if first:
                        l_a = jnp.broadcast_to(l_add, q.shape)
                    else:
                        alpha = jnp.exp(prev[a] - m_a)
                        l_a, pv = alpha * prev[2 + a] + l_add, alpha * prev[4] + pv
                    m_new.append(m_a), l_new.append(l_a), acc_new.append(pv)
                results.append((m_new[0], m_new[1], l_new[0], l_new[1], jnp.where(head0, acc_new[0], acc_new[1])))
            for (rq, _, _), res in zip(blocks, results):
                for ref, val in zip(stats, res):
                    ref[rq, :] = val

        for d in DILATIONS:
            def step(i, carry, d=d):
                update(_attn_group(d, i), d == DILATIONS[0])
                return carry

            lax.fori_loop(0, ATTN_ITERS, step, 0)

        def fin(t, carry):
            rows = pl.ds(pl.multiple_of(t * 256, 256), 256)
            h0 = lax.broadcasted_iota(jnp.int32, (256, 128), 1) < 64
            la, lb = l0[rows, :], l1[rows, :]
            o_ref[rows, :] = acc[rows, :] / jnp.where(h0, la, lb)
            lse0_ref[rows, :] = m0[rows, :] + jnp.log(la)
            lse1_ref[rows, :] = m1[rows, :] + jnp.log(lb)
            return carry

        lax.fori_loop(0, S // 256, fin, 0)
        pl.when(hp == 3)(ag_finish)

    col = lambda off: pl.BlockSpec((S, 128), lambda h, off=off: (0, off + h))
    res = pl.pallas_call(
        body, name="attn_fwd", grid=(4,),
        in_specs=[col(0), col(4), col(8)] + [VM] * nw,
        out_specs=[col(0), col(0), col(0)] + [ANY] * nw,
        out_shape=[jax.ShapeDtypeStruct((S, AW), F32)] * 3 + _gather_shapes(shards, dtypes),
        scratch_shapes=[pltpu.VMEM((S, 128), F32)] * 5 + _gather_scratch(shards, dtypes),
        compiler_params=_params(1),
    )(qkv, qkv, qkv, *shards)
    return res[0], (res[1], res[2]), res[3:]


def _attn_bwd(qkv, o, lse, do, parts):
    nw = len(parts)

    def body(*refs):
        q_ref, k_ref, v_ref, o_ref, L0, L1, do_ref = refs[:7]
        ins = refs[7:7 + nw]
        dq_out, dk_out, dv_out = refs[7 + nw:10 + nw]
        outs = refs[10 + nw:10 + 2 * nw]
        D0, D1, dq_ref, dk_ref, dv_ref = refs[10 + 2 * nw:15 + 2 * nw]
        rs_start, rs_finish = _scatter_phases(ins, outs, *refs[15 + 2 * nw:])
        hp = pl.program_id(0)
        pl.when(hp == 0)(rs_start)

        def pre(t, carry):
            rows = pl.ds(pl.multiple_of(t * 256, 256), 256)
            h0 = lax.broadcasted_iota(jnp.int32, (256, 128), 1) < 64
            dd = do_ref[rows, :] * o_ref[rows, :]
            shp = (256, 128)
            D0[rows, :] = jnp.broadcast_to(jnp.sum(jnp.where(h0, dd, 0.0), axis=-1, keepdims=True), shp)
            D1[rows, :] = jnp.broadcast_to(jnp.sum(jnp.where(h0, 0.0, dd), axis=-1, keepdims=True), shp)
            return carry

        lax.fori_loop(0, S // 256, pre, 0)

        def update(blocks, first):
            loaded = [([q_ref[rq, :], k_ref[rk, :], v_ref[rk, :], do_ref[rq, :]],
                       [L0[rq, :], L1[rq, :], D0[rq, :], D1[rq, :]],
                       [0.0] * 3 if first else [dq_ref[rq, :], dk_ref[rk, :], dv_ref[rk, :]]) for rq, rk, _ in blocks]
            results = []
            for ((q, k, v, dout), (l0v, l1v, d0v, d1v), (dq, dk, dv)), (_, _, valid) in zip(loaded, blocks):
                nq = q.shape[0]
                valid = jnp.concatenate([valid, valid], axis=0)
                q2, do2, kb, vb = _stack_heads(q), _stack_heads(dout), _bf(k), _bf(v)
                cat = lambda a, b: jnp.tile(jnp.concatenate([a, b], axis=0), (1, 2))
                s = jnp.where(valid, _dot_nt(_stack_heads(q * 0.125), kb), NEG)
                p = jnp.exp(s - cat(l0v, l1v))
                ds = _bf(p * (_dot_nt(do2, vb) - cat(d0v, d1v)) * 0.125)
                dq2 = _dot(ds, kb)
                results.append((dq + jnp.where(_head0((nq, 128)), dq2[0:nq], dq2[nq:2 * nq]),
                                dk + _dot_tn(ds, q2), dv + _dot_tn(_bf(p), do2)))
            for (rq, rk, _), (dq, dk, dv) in zip(blocks, results):
                dq_ref[rq, :] = dq
                dk_ref[rk, :] = dk
                dv_ref[rk, :] = dv

        assert S // (128 * DILATIONS[0]) == 2
        for d in DILATIONS:
            def step(i, carry, d=d):
                update(_attn_group(d, i), d == DILATIONS[0])
                return carry

            lax.fori_loop(0, ATTN_ITERS, step, 0)

        def fin(t, carry):
            rows = pl.ds(pl.multiple_of(t * 256, 256), 256)
            for src, dst in ((dq_ref, dq_out), (dk_ref, dk_out), (dv_ref, dv_out)):
                dst[rows, :] = _bf(src[rows, :])
            return carry

        lax.fori_loop(0, S // 256, fin, 0)
        pl.when(hp == 3)(rs_finish)

    col = lambda off: pl.BlockSpec((S, 128), lambda h, off=off: (0, off + h))
    res = pl.pallas_call(
        body, name="attn_bwd", grid=(4,),
        in_specs=[col(0), col(4), col(8), col(0), col(0), col(0), col(0)] + [ANY] * nw,
        out_specs=[col(0), col(0), col(0)] + [ANY] * nw,
        out_shape=[jax.ShapeDtypeStruct((S, AW), BF16)] * 3 + [jax.ShapeDtypeStruct(a.shape, a.dtype) for a in parts],
        scratch_shapes=[pltpu.VMEM((S, 128), F32)] * 5 + _scatter_scratch(nw),
        compiler_params=_params(1),
    )(qkv, qkv, qkv, o, lse[0], lse[1], do, *parts)
    return res[0], res[1], res[2], res[3:]


def _logsig(x):
    return jnp.minimum(x, 0.0) - jnp.log1p(jnp.exp(-jnp.abs(x)))


def _conv_taps(xp, n):
    return [xp[8:] if j == 3 else pltpu.roll(xp, 3 - j, 0)[8:] for j in range(4)]


def _conv_silu(xp, w_ref, b_ref, n):
    taps = _conv_taps(xp, n)
    c = b_ref[...] + sum(w_ref[j:j + 1, :] * taps[j] for j in range(4))
    sg = _sigmoid(c)
    return c, sg, taps


def _chunk_gates(G):
    assert LC == 128
    r = lax.broadcasted_iota(jnp.int32, (LC, LC), 0)
    c = lax.broadcasted_iota(jnp.int32, (LC, LC), 1)
    tril = (c <= r).astype(F32)
    triu = (c >= r).astype(F32)
    b_col = jnp.dot(tril, _logsig(G), preferred_element_type=F32, precision=HI)
    return b_col, b_col.T, G.T, tril, triu


def _colpick(X, lane):
    li = lax.broadcasted_iota(jnp.int32, X.shape, 1)
    return jnp.sum(jnp.where(li == lane, X, 0.0), axis=1, keepdims=True)


def _rowpick(XT, row):
    ri = lax.broadcasted_iota(jnp.int32, XT.shape, 0)
    return jnp.sum(jnp.where(ri == row, XT, 0.0), axis=0, keepdims=True)


def _mlstm_head(qh, kh, vh, G, b_col, b_row, g_row, h, Ch, nh, m_prev):
    bt = _colpick(b_col, 4 + h)
    i_col = _colpick(G, h)
    bs = _rowpick(b_row, 4 + h)
    i_row = _rowpick(g_row, h)
    r = lax.broadcasted_iota(jnp.int32, (LC, LC), 0)
    c = lax.broadcasted_iota(jnp.int32, (LC, LC), 1)
    log_d = jnp.where(c <= r, bt - bs + i_row, NEG)
    log_inter = bt + m_prev
    m_t = jnp.maximum(log_inter, jnp.max(log_d, axis=1, keepdims=True))
    Dm = jnp.exp(log_d - m_t)
    g = jnp.exp(log_inter - m_t)
    qb, kb, vb = _bf(qh), _bf(kh), _bf(vh)
    Am = _dot_nt(qb, kb) * Dm
    qC = _dot(qb, _bf(Ch))
    num = g * qC + _dot(_bf(Am), vb)
    qn = jnp.sum(qh * nh, axis=1, keepdims=True)
    den = g * qn + jnp.sum(Am, axis=1, keepdims=True)
    floor = jnp.exp(-m_t)
    dd = jnp.maximum(jnp.abs(den), floor)
    inv_dd = 1.0 / dd
    hh = num * inv_dd
    lane = lax.broadcasted_iota(jnp.int32, (1, LC), 1)
    blast = jnp.sum(jnp.where(lane == LC - 1, bs, 0.0), axis=1, keepdims=True)
    log_s = blast - bt + i_col
    m_new = jnp.maximum(blast + m_prev, jnp.max(log_s, axis=0, keepdims=True))
    decay = jnp.exp(blast + m_prev - m_new)
    ws = jnp.exp(log_s - m_new)
    kw = kh * ws
    C_new = decay * Ch + _dot_tn(_bf(kw), vb)
    n_new = decay * nh + jnp.sum(kw, axis=0, keepdims=True)
    return dict(Dm=Dm, g=g, Am=Am, qC=qC, qn=qn, den=den, floor=floor, inv_dd=inv_dd, h=hh, decay=decay, ws=ws, kw=kw,
                C_new=C_new, n_new=n_new, m_new=m_new, qb=qb, kb=kb, vb=vb)


def _head_out(hh, mo_h, gn_h):
    r = lax.rsqrt(jnp.mean(hh * hh, axis=-1, keepdims=True) + EPS)
    hn = hh * r
    sg = _sigmoid(mo_h)
    return sg * (hn * gn_h), hn, r, sg


def _mlstm_fwd(mqk, mv, mo, gates, conv_w, conv_b, gate_b, gn, shards, dtypes):
    nblk = S // TB
    ncb = TB // LC
    nw = len(shards)

    def body(*refs):
        x_ref, v_ref, o_ref, g_ref, w_ref, b_ref, gb_ref, gn_ref = refs[:8]
        ins = refs[8:8 + nw]
        out_ref, cs_ref, ns_ref, ms_ref = refs[8 + nw:12 + nw]
        outs = refs[12 + nw:12 + 2 * nw]
        tail, Cst, nst, mst, qs, ks = refs[12 + 2 * nw:18 + 2 * nw]
        bufs = refs[18 + 2 * nw:18 + 3 * nw]
        ag_start, ag_forward, ag_finish = _gather_phases(ins, outs, bufs, *refs[18 + 3 * nw:])
        i = pl.program_id(0)
        pl.when(i == 0)(ag_start)
        pl.when(i == nblk // 2)(ag_forward)

        @pl.when(i == 0)
        def _():
            tail[...] = jnp.zeros_like(tail)
            Cst[...] = jnp.zeros_like(Cst)
            nst[...] = jnp.zeros_like(nst)
            mst[...] = jnp.zeros_like(mst)

        x = x_ref[...]
        xp = jnp.concatenate([tail[...], x], axis=0)
        tail[...] = x[TB - 8:TB, :]
        c, sg, _ = _conv_silu(xp, w_ref, b_ref, TB)
        y = c * sg
        qs[...] = y[:, 0:MW]
        ks[...] = y[:, MW:2 * MW] * (1.0 / math.sqrt(128.0))

        for cc in range(ncb):
            rows = slice(cc * LC, (cc + 1) * LC)
            G = g_ref[rows, :] + gb_ref[...]
            b_col, b_row, g_row, _, _ = _chunk_gates(G)
            cs_ref[cc] = Cst[...]
            ns_ref[cc] = nst[...]
            ms_ref[cc] = mst[...]
            for h in range(4):
                ln = slice(h * 128, (h + 1) * 128)
                m_prev = jnp.max(mst[0:1, ln], axis=1, keepdims=True)
                f = _mlstm_head(qs[rows, ln], ks[rows, ln], v_ref[rows, ln], G, b_col, b_row, g_row, h,
                                Cst[:, ln], nst[0:1, ln], m_prev)
                out, _, _, _ = _head_out(f["h"], o_ref[rows, ln], gn_ref[:, ln])
                out_ref[rows, ln] = out
                Cst[:, ln] = f["C_new"]
                nst[0:1, ln] = f["n_new"]
                mst[0:1, ln] = jnp.broadcast_to(f["m_new"], (1, 128))
        pl.when(i == nblk - 1)(ag_finish)

    row = lambda wd: pl.BlockSpec((TB, wd), lambda i: (i, 0))
    res = pl.pallas_call(
        body, name="mlstm_fwd", grid=(nblk,),
        in_specs=[row(1024), row(MW), row(MW), row(128), _cspec((4, 1024)), _cspec((1, 1024)), _cspec((1, 128)),
                  _cspec((1, MW))] + [VM] * nw,
        out_specs=[row(MW), pl.BlockSpec((ncb, 128, MW), lambda i: (i, 0, 0)),
                   pl.BlockSpec((ncb, 8, MW), lambda i: (i, 0, 0)), pl.BlockSpec((ncb, 8, MW), lambda i: (i, 0, 0))]
        + [ANY] * nw,
        out_shape=[jax.ShapeDtypeStruct((S, MW), F32), jax.ShapeDtypeStruct((S // LC, 128, MW), F32),
                   jax.ShapeDtypeStruct((S // LC, 8, MW), F32), jax.ShapeDtypeStruct((S // LC, 8, MW), F32)]
        + _gather_shapes(shards, dtypes),
        scratch_shapes=[pltpu.VMEM((8, 1024), F32), pltpu.VMEM((128, MW), F32), pltpu.VMEM((8, MW), F32),
                        pltpu.VMEM((8, MW), F32), pltpu.VMEM((TB, MW), F32), pltpu.VMEM((TB, MW), F32)]
        + _gather_scratch(shards, dtypes),
        compiler_params=_params(1),
    )(mqk, mv, mo, gates, conv_w, conv_b, gate_b, gn, *shards)
    return res[0], res[1], res[2], res[3], res[4:]


DM_V, DM_O, DM_G, DM_W = 1024, 1536, 2048, PW - 3 * AW


def _mlstm_bwd(mqk, mv, mo, gates, conv_w, conv_b, gate_b, gn, cs, ns, ms, dout, parts):
    nblk = S // TB
    ncb = TB // LC
    kscale = 1.0 / math.sqrt(128.0)
    nw = len(parts)

    def body(*refs):
        x_ref, xprev_ref, v_ref, o_ref, g_ref, w_ref, b_ref, gb_ref, gn_ref, cs_ref, ns_ref, ms_ref, do_ref = refs[:13]
        ins = refs[13:13 + nw]
        dm_ref, dw_ref, db_ref, dgn_ref, dgb_ref = refs[13 + nw:18 + nw]
        outs = refs[18 + nw:18 + 2 * nw]
        dCst, dnst, dyhead, qs, ks, dqk = refs[18 + 2 * nw:24 + 2 * nw]
        rs_start, rs_finish = _scatter_phases(ins, outs, *refs[24 + 2 * nw:])
        i = pl.program_id(0)
        blk = nblk - 1 - i
        pl.when(i == 0)(rs_start)

        @pl.when(i == 0)
        def _():
            dCst[...] = jnp.zeros_like(dCst)
            dnst[...] = jnp.zeros_like(dnst)
            dyhead[...] = jnp.zeros_like(dyhead)
            dw_ref[...] = jnp.zeros_like(dw_ref)
            db_ref[...] = jnp.zeros_like(db_ref)
            dgn_ref[...] = jnp.zeros_like(dgn_ref)
            dgb_ref[...] = jnp.zeros_like(dgb_ref)

        x = x_ref[...]
        xprev = jnp.where(blk == 0, 0.0, xprev_ref[...])
        xp = jnp.concatenate([xprev, x], axis=0)
        c, sg, taps = _conv_silu(xp, w_ref, b_ref, TB)
        y = c * sg
        qs[...] = y[:, 0:MW]
        ks[...] = y[:, MW:2 * MW] * kscale
        lane128 = lax.broadcasted_iota(jnp.int32, (LC, 128), 1)
        rowi = lax.broadcasted_iota(jnp.int32, (LC, 1), 0)
        ones = jnp.ones((LC, 128), F32)

        for cc in reversed(range(ncb)):
            rows = slice(cc * LC, (cc + 1) * LC)
            G = g_ref[rows, :] + gb_ref[...]
            b_col, b_row, g_row, _, triu = _chunk_gates(G)
            dB = jnp.zeros((LC, 128), F32)
            dI = jnp.zeros((LC, 128), F32)
            for h in range(4):
                ln = slice(h * 128, (h + 1) * 128)
                Ch = cs_ref[cc, :, ln]
                nh = ns_ref[cc, 0:1, ln]
                m_prev = jnp.max(ms_ref[cc, 0:1, ln], axis=1, keepdims=True)
                qh, kh, vh = qs[rows, ln], ks[rows, ln], v_ref[rows, ln]
                f = _mlstm_head(qh, kh, vh, G, b_col, b_row, g_row, h, Ch, nh, m_prev)
                hh, inv_dd, den, g, Am, Dm = f["h"], f["inv_dd"], f["den"], f["g"], f["Am"], f["Dm"]
                qb, kb, vb = f["qb"], f["kb"], f["vb"]
                gn_h = gn_ref[:, ln]
                _, hn, r, sgo = _head_out(hh, o_ref[rows, ln], gn_h)
                do = do_ref[rows, ln]
                hm = hn * gn_h
                dm_ref[rows, DM_O + h * 128:DM_O + (h + 1) * 128] = _bf(do * hm * sgo * (1.0 - sgo))
                dhm = do * sgo
                dgn_ref[:, ln] = dgn_ref[:, ln] + jnp.sum(dhm * hn, axis=0, keepdims=True)
                dhn = dhm * gn_h
                dh = r * (dhn - hn * jnp.mean(dhn * hn, axis=-1, keepdims=True))
                dnum = dh * inv_dd
                ddd = -jnp.sum(dh * hh, axis=1, keepdims=True) * inv_dd
                dden = jnp.where(jnp.abs(den) >= f["floor"], ddd * jnp.sign(den), 0.0)
                dnb = _bf(dnum)
                dA = _dot_nt(dnb, vb) + dden
                dv = _dot_tn(_bf(Am), dnb)
                gd = _bf(g * dnum)
                gq = g * dden
                dq = _dot_nt(gd, _bf(Ch)) + gq * nh
                dCn = dCst[:, ln]
                dnn = dnst[0:1, ln]
                dC = f["decay"] * dCn + _dot_tn(qb, gd)
                dn = f["decay"] * dnn + jnp.sum(gq * qh, axis=0, keepdims=True)
                dg = jnp.sum(dnum * f["qC"], axis=1, keepdims=True) + dden * f["qn"]
                dS = _bf(dA * Dm)
                dq = dq + _dot(dS, kb)
                dk = _dot_tn(dS, qb)
                Gm = dA * Am
                gam = dg * g
                dCb = _bf(dCn)
                E = _dot_nt(vb, dCb) + dnn
                ws = f["ws"]
                dk = dk + ws * E
                om = jnp.sum(E * kh, axis=1, keepdims=True) * ws
                dv = dv + _dot(_bf(f["kw"]), dCb)
                ddecay = (jnp.sum(jnp.sum(dCn * Ch, axis=1, keepdims=True), axis=0, keepdims=True)
                          + jnp.sum(dnn * nh, axis=1, keepdims=True))
                delta = ddecay * f["decay"]
                rows_g = jnp.sum(Gm, axis=1, keepdims=True)
                cols_g = jnp.broadcast_to(jnp.sum(Gm, axis=0, keepdims=True), (LC, 128)).T
                last = jnp.where(rowi == LC - 1, jnp.sum(om, axis=0, keepdims=True) + delta, 0.0)
                db = rows_g + gam - om + last - cols_g
                di = cols_g + om
                dB = jnp.where(lane128 == 4 + h, db, dB)
                dI = jnp.where(lane128 == h, di, dI)
                dCst[:, ln] = dC
                dnst[0:1, ln] = dn
                dqk[rows, ln] = dq
                dqk[rows, MW + h * 128:MW + (h + 1) * 128] = dk * kscale
                dm_ref[rows, DM_V + h * 128:DM_V + (h + 1) * 128] = _bf(dv)
            dlogf = jnp.dot(triu, dB, preferred_element_type=F32, precision=HI)
            dG = dI + dlogf * _sigmoid(-G)
            dG = jnp.where(lane128 < 8, dG, 0.0)
            dm_ref[rows, DM_G:DM_G + 128] = _bf(dG)
            dm_ref[rows, DM_G + 128:DM_W] = jnp.zeros((LC, DM_W - DM_G - 128), BF16)
            dgb_ref[...] = dgb_ref[...] + jnp.sum(dG, axis=0, keepdims=True)

        dy = dqk[...] * (sg * (1.0 + c * (1.0 - sg)))
        db_ref[...] = db_ref[...] + jnp.sum(dy, axis=0, keepdims=True)
        for j in range(4):
            dw_ref[j:j + 1, :] = dw_ref[j:j + 1, :] + jnp.sum(dy * taps[j], axis=0, keepdims=True)
        dyp = jnp.concatenate([dy, dyhead[...]], axis=0)
        dx = w_ref[3:4, :] * dy
        for j in range(3):
            dx = dx + w_ref[j:j + 1, :] * pltpu.roll(dyp, TB + 8 - (3 - j), 0)[0:TB]
        dm_ref[:, 0:DM_V] = _bf(dx)
        dyhead[...] = dy[0:8, :]
        pl.when(i == nblk - 1)(rs_finish)

    rrow = lambda wd: pl.BlockSpec((TB, wd), lambda i: (nblk - 1 - i, 0))
    st = lambda r: pl.BlockSpec((ncb, r, MW), lambda i: (nblk - 1 - i, 0, 0))
    prev8 = pl.BlockSpec((8, 1024), lambda i: (jnp.maximum((nblk - 1 - i) * (TB // 8) - 1, 0), 0))
    res = pl.pallas_call(
        body, name="mlstm_bwd", grid=(nblk,),
        in_specs=[rrow(1024), prev8, rrow(MW), rrow(MW), rrow(128), _cspec((4, 1024)), _cspec((1, 1024)),
                  _cspec((1, 128)), _cspec((1, MW)), st(128), st(8), st(8), rrow(MW)] + [ANY] * nw,
        out_specs=[rrow(DM_W),
                   pl.BlockSpec((4, 1024), lambda i: (0, 0)), pl.BlockSpec((1, 1024), lambda i: (0, 0)),
                   pl.BlockSpec((1, MW), lambda i: (0, 0)), pl.BlockSpec((1, 128), lambda i: (0, 0))] + [ANY] * nw,
        out_shape=[jax.ShapeDtypeStruct((S, DM_W), BF16),
                   jax.ShapeDtypeStruct((4, 1024), F32), jax.ShapeDtypeStruct((1, 1024), F32),
                   jax.ShapeDtypeStruct((1, MW), F32), jax.ShapeDtypeStruct((1, 128), F32)]
        + [jax.ShapeDtypeStruct(a.shape, a.dtype) for a in parts],
        scratch_shapes=[pltpu.VMEM((128, MW), F32), pltpu.VMEM((8, MW), F32), pltpu.VMEM((8, 1024), F32),
                        pltpu.VMEM((TB, MW), F32), pltpu.VMEM((TB, MW), F32), pltpu.VMEM((TB, 1024), F32)]
        + _scatter_scratch(nw),
        compiler_params=_params(1),
    )(mqk, mqk, mv, mo, gates, conv_w, conv_b, gate_b, gn, cs, ns, ms, dout, *parts)
    return res[:5], res[5:]


def _out_proj(x, attn, ml, w, g):
    tm = TM

    def body(x_ref, a_ref, m_ref, w_ref, g_ref, h_ref, u_ref):
        h1 = x_ref[...] + _dot(_bf(a_ref[...]), w_ref[0:AW, :]) + _dot(_bf(m_ref[...]), w_ref[AW:D, :])
        h_ref[...] = h1
        n, _ = _rms(h1)
        u_ref[...] = _bf(n * g_ref[...])

    row = lambda wd: pl.BlockSpec((tm, wd), lambda i: (i, 0))
    return pl.pallas_call(
        body, name="out_proj", grid=(S // tm,),
        in_specs=[row(D), row(AW), row(MW), _cspec((D, D)), _cspec((1, D))],
        out_specs=[row(D), row(D)],
        out_shape=[jax.ShapeDtypeStruct((S, D), F32), jax.ShapeDtypeStruct((S, D), BF16)],
        compiler_params=_params(1),
    )(x, attn, ml, w, g)


HALF = DFF // NDEV // 2


def _mlp_fwd(h1, u2, w_up, w_down_a, w_down_b):
    tm = TM

    def body(h_ref, u_ref, wu_ref, wa_ref, wb_ref, a_ref, o_ref):
        u = u_ref[...]
        acc = h_ref[...]
        for c in range(NDEV):
            cols = slice(c * 512, (c + 1) * 512)
            a = _dot(u, wu_ref[c])
            a_ref[:, cols] = _bf(a)
            r = jnp.maximum(a, 0.0)
            r = _bf(r * r)
            acc = acc + _dot(r[:, 0:HALF], wa_ref[c]) + _dot(r[:, HALF:2 * HALF], wb_ref[c])
        o_ref[...] = acc

    row = lambda wd: pl.BlockSpec((tm, wd), lambda i: (i, 0))
    return pl.pallas_call(
        body, name="mlp_fwd", grid=(S // tm,),
        in_specs=[row(D), row(D), _cspec((NDEV, D, DFF // NDEV)), _cspec((NDEV, HALF, D)), _cspec((NDEV, HALF, D))],
        out_specs=[row(DFF), row(D)],
        out_shape=[jax.ShapeDtypeStruct((S, DFF), BF16), jax.ShapeDtypeStruct((S, D), F32)],
        compiler_params=_params(1),
    )(h1, u2, w_up, w_down_a, w_down_b)


def _ple_loss(h2, p, target, w_pg, w_ple, g_ple, g_fin):
    tm = TM

    def body(h_ref, p_ref, t_ref, wg_ref, wp_ref, gp_ref, gf_ref,
             dh_ref, dwg_ref, dwp_ref, dgp_ref, dgf_ref, loss_ref, acc_g, acc_p):
        i = pl.program_id(0)

        @pl.when(i == 0)
        def _():
            acc_g[...] = jnp.zeros_like(acc_g)
            acc_p[...] = jnp.zeros_like(acc_p)
            dgp_ref[...] = jnp.zeros_like(dgp_ref)
            dgf_ref[...] = jnp.zeros_like(dgf_ref)
            loss_ref[...] = jnp.zeros_like(loss_ref)

        h2v = h_ref[...]
        n2, rs2 = _rms(h2v)
        u3 = _bf(n2 * gp_ref[...])
        gt = _sigmoid(_dot(u3, wg_ref[...]))
        pb = _bf(p_ref[...])
        e = jnp.concatenate([_dot(pb, wp_ref[j]) for j in range(NDEV)], axis=1)
        h3 = h2v + gt * e
        n3, rs3 = _rms(h3)
        err = n3 * gf_ref[...] - t_ref[...]
        loss_ref[...] = loss_ref[...] + 0.5 / D * jnp.sum(jnp.sum(err * err, axis=1, keepdims=True), axis=0, keepdims=True)
        dy = err * (1.0 / D)
        dgf_ref[...] = dgf_ref[...] + jnp.sum(dy * n3, axis=0, keepdims=True)
        dh3 = _rms_bwd(dy, n3, rs3, gf_ref[...])
        de = _bf(dh3 * gt)
        dz = _bf(dh3 * e * gt * (1.0 - gt))
        acc_p[...] = acc_p[...] + _dot_tn(pb, de)
        acc_g[...] = acc_g[...] + _dot_tn(u3, dz)
        du3 = _dot_nt(dz, wg_ref[...])
        dgp_ref[...] = dgp_ref[...] + jnp.sum(du3 * n2, axis=0, keepdims=True)
        dh_ref[...] = dh3 + _rms_bwd(du3, n2, rs2, gp_ref[...])

        @pl.when(i == S // tm - 1)
        def _():
            dwg_ref[...] = _bf(acc_g[...])
            for j in range(NDEV):
                dwp_ref[j] = _bf(acc_p[:, j * 128:(j + 1) * 128])

    row = lambda wd: pl.BlockSpec((tm, wd), lambda i: (i, 0))
    whole = lambda shp: pl.BlockSpec(shp, lambda i: (0,) * len(shp))
    return pl.pallas_call(
        body, name="ple_loss", grid=(S // tm,),
        in_specs=[row(D), row(PLE), row(D), _cspec((D, D)), _cspec((NDEV, PLE, 128)), _cspec((1, D)), _cspec((1, D))],
        out_specs=[row(D), whole((D, D)), whole((NDEV, PLE, 128)), whole((1, D)), whole((1, D)), whole((1, 1))],
        out_shape=[jax.ShapeDtypeStruct((S, D), F32), jax.ShapeDtypeStruct((D, D), BF16),
                   jax.ShapeDtypeStruct((NDEV, PLE, 128), BF16), jax.ShapeDtypeStruct((1, D), F32),
                   jax.ShapeDtypeStruct((1, D), F32), jax.ShapeDtypeStruct((1, 1), F32)],
        scratch_shapes=[pltpu.VMEM((D, D), F32), pltpu.VMEM((PLE, D), F32)],
        compiler_params=_params(1),
    )(h2, p, target, w_pg, w_ple, g_ple, g_fin)


def _mlp_bwd(dh2, a, h1, g, w_up, w_down_a, w_down_b):
    tm = TM

    def body(d_ref, a_ref, h_ref, g_ref, wu_ref, wa_ref, wb_ref, da_ref, dh1_ref, dg_ref):
        @pl.when(pl.program_id(0) == 0)
        def _():
            dg_ref[...] = jnp.zeros_like(dg_ref)

        dh2v = d_ref[...]
        db = _bf(dh2v)
        du = jnp.zeros((tm, D), F32)
        for c in range(NDEV):
            cols = slice(c * 512, (c + 1) * 512)
            dr = jnp.concatenate([_dot_nt(db, wa_ref[c]), _dot_nt(db, wb_ref[c])], axis=1)
            da = _bf(dr * (2.0 * jnp.maximum(a_ref[:, cols], 0.0)))
            da_ref[:, cols] = da
            du = du + _dot_nt(da, wu_ref[c])
        n, rs = _rms(h_ref[...])
        dg_ref[...] = dg_ref[...] + jnp.sum(du * n, axis=0, keepdims=True)
        dh1_ref[...] = dh2v + _rms_bwd(du, n, rs, g_ref[...])

    row = lambda wd: pl.BlockSpec((tm, wd), lambda i: (i, 0))
    return pl.pallas_call(
        body, name="mlp_bwd", grid=(S // tm,),
        in_specs=[row(D), row(DFF), row(D), _cspec((1, D)), _cspec((NDEV, D, DFF // NDEV)), _cspec((NDEV, HALF, D)),
                  _cspec((NDEV, HALF, D))],
        out_specs=[row(DFF), row(D), pl.BlockSpec((1, D), lambda i: (0, 0))],
        out_shape=[jax.ShapeDtypeStruct((S, DFF), BF16), jax.ShapeDtypeStruct((S, D), F32),
                   jax.ShapeDtypeStruct((1, D), F32)],
        compiler_params=_params(1),
    )(dh2, a, h1, g, w_up, w_down_a, w_down_b)


def _out_proj_bwd(dh1, attn, ml, w):
    tm = TM

    def body(d_ref, a_ref, m_ref, w_ref, da_ref, dm_ref, dw_ref, acc):
        i = pl.program_id(0)

        @pl.when(i == 0)
        def _():
            acc[...] = jnp.zeros_like(acc)

        db = _bf(d_ref[...])
        dmix = _dot_nt(db, w_ref[...])
        da_ref[...] = dmix[:, 0:AW]
        dm_ref[...] = dmix[:, AW:D]
        acc[0:AW, :] = acc[0:AW, :] + _dot_tn(_bf(a_ref[...]), db)
        acc[AW:D, :] = acc[AW:D, :] + _dot_tn(_bf(m_ref[...]), db)

        @pl.when(i == S // tm - 1)
        def _():
            dw_ref[...] = _bf(acc[...])

    row = lambda wd: pl.BlockSpec((tm, wd), lambda i: (i, 0))
    return pl.pallas_call(
        body, name="out_proj_bwd", grid=(S // tm,),
        in_specs=[row(D), row(AW), row(MW), _cspec((D, D))],
        out_specs=[row(AW), row(MW), pl.BlockSpec((D, D), lambda i: (0, 0))],
        out_shape=[jax.ShapeDtypeStruct((S, AW), F32), jax.ShapeDtypeStruct((S, MW), F32),
                   jax.ShapeDtypeStruct((D, D), BF16)],
        scratch_shapes=[pltpu.VMEM((D, D), F32)],
        compiler_params=_params(1),
    )(dh1, attn, ml, w)


CHIP_FLIPS = [(0, 0), (0, 1), (1, 0), (1, 1)]


def _scatter2_phases(in_ref, out_ref, mine_v, sib_v, psum_v, loc_sems, d2d_send, d2d_recv, ici_send, ici_recv, own_sem):
    x, y, c = _place()
    chips = [((x + dx) % 2, (y + dy) % 2) for dx, dy in CHIP_FLIPS]
    nc = len(chips)

    def local(k):
        return pltpu.make_async_copy(in_ref.at[_dev_index(*chips[k], c)], mine_v.at[k], loc_sems.at[k])

    def to_sib(k):
        return pltpu.make_async_remote_copy(
            src_ref=in_ref.at[_dev_index(*chips[k], 1 - c)], dst_ref=sib_v.at[k], send_sem=d2d_send.at[k],
            recv_sem=d2d_recv.at[k], device_id=(x, y, 1 - c), device_id_type=MESH)

    def over_ici(k):
        return pltpu.make_async_remote_copy(
            src_ref=psum_v.at[k], dst_ref=out_ref.at[k], send_sem=ici_send.at[k - 1], recv_sem=ici_recv.at[k - 1],
            device_id=(*chips[k], c), device_id_type=MESH)

    def own():
        return pltpu.make_async_copy(psum_v.at[0], out_ref.at[0], own_sem)

    def start():
        for k in range(nc):
            to_sib(k).start()
            local(k).start()

    def middle():
        for k in (1, 2, 3, 0):
            local(k).wait()
            to_sib(k).wait_recv()
            psum_v[k] = _bf(mine_v[k].astype(F32) + sib_v[k].astype(F32))
            (over_ici(k) if k else own()).start()

    def finish():
        for k in range(1, nc):
            over_ici(k).wait()
        for k in range(nc):
            to_sib(k).wait_send()
        own().wait()

    return start, middle, finish


def _scatter2_scratch(shard, dtype):
    nc = len(CHIP_FLIPS)
    return ([pltpu.VMEM((nc, *shard), dtype)] * 3
            + [pltpu.SemaphoreType.DMA((nc,))] * 3 + [pltpu.SemaphoreType.DMA((nc - 1,))] * 2 + [pltpu.SemaphoreType.DMA])


def _in_proj_bwd(dparts, n_roped, rope, dh1, x, g1, w, part):
    tm = TM
    nt = S // tm
    widths = [d.shape[1] for d in dparts]
    assert sum(widths) == PW
    npar = len(dparts)

    def body(*refs):
        d_refs = refs[:npar]
        tabs = [t[...] for t in refs[npar:npar + 3]]
        dh_ref, x_ref, g_ref, w_ref, in_ref, dx_ref, dg_ref, out_ref = refs[npar + 3:npar + 11]
        rs_start, rs_middle, rs_finish = _scatter2_phases(in_ref, out_ref, *refs[npar + 11:])
        i = pl.program_id(0)
        pl.when(i == 0)(rs_start)
        pl.when(i == 1)(rs_middle)

        @pl.when(i == 0)
        def _():
            dg_ref[...] = jnp.zeros_like(dg_ref)

        du = jnp.zeros((tm, D), F32)
        off = 0
        for j, (d_ref, wd) in enumerate(zip(d_refs, widths)):
            nc = next(c for c in (768, 512) if wd % c == 0)
            for s in range(wd // nc):
                d = d_ref[:, s * nc:(s + 1) * nc]
                du = du + _dot_nt(_unrope(d, *tabs) if j < n_roped else d, w_ref[:, off + s * nc:off + (s + 1) * nc])
            off += wd
        n, rs = _rms(x_ref[...])
        dg_ref[...] = dg_ref[...] + jnp.sum(du * n, axis=0, keepdims=True)
        dx_ref[...] = dh_ref[...] + _rms_bwd(du, n, rs, g_ref[...])
        pl.when(i == nt - 1)(rs_finish)

    row = lambda wd: pl.BlockSpec((tm, wd), lambda i: (i, 0))
    shard = part.shape[1:]
    return pl.pallas_call(
        body, name="in_proj_bwd", grid=(nt,),
        in_specs=[row(wd) for wd in widths] + [row(128)] * 3 + [row(D), row(D), _cspec((1, D)), _cspec((D, PW)), ANY],
        out_specs=[row(D), pl.BlockSpec((1, D), lambda i: (0, 0)), ANY],
        out_shape=[jax.ShapeDtypeStruct((S, D), F32), jax.ShapeDtypeStruct((1, D), F32),
                   jax.ShapeDtypeStruct((len(CHIP_FLIPS), *shard), part.dtype)],
        scratch_shapes=_scatter2_scratch(shard, part.dtype),
        compiler_params=_params(1),
    )(*dparts, *rope, dh1, x, g1, w, part)


SMALL_ROWS = 96


def _small_phases(ins, out_ref, pack, rbuf, send_sems, recv_sems):
    x, y, c = _place()
    me = _dev_index(x, y, c)

    def copies():
        out = []
        for k, (dx, dy, dc) in enumerate(FLIPS):
            peer = ((x + dx) % 2, (y + dy) % 2, (c + dc) % 2)
            out.append(pltpu.make_async_remote_copy(
                src_ref=pack, dst_ref=rbuf.at[me], send_sem=send_sems.at[k], recv_sem=recv_sems.at[k],
                device_id=peer, device_id_type=MESH))
        return out

    def start():
        pack[...] = jnp.zeros_like(pack)
        for i, ref in enumerate(ins):
            pack[8 * i:8 * i + 1, 0:ref.shape[1]] = ref[...]
        rbuf[me] = pack[...]
        for cp in copies():
            cp.start()

    def finish():
        for cp in copies():
            cp.wait()
        tot = rbuf[0]
        for j in range(1, NDEV):
            tot = tot + rbuf[j]
        out_ref[...] = tot

    return start, finish


def _wgrad(name, A, Bs, a_fn, b_fn, out_shape, split=None, ts=512, small=(), rope=(), n_roped=0):
    K = A.shape[1]
    widths = [b.shape[1] for b in Bs]
    N = sum(widths)
    nb, ns, nrt = len(Bs) + len(rope), len(small), S // ts
    kc = min(K, 1024)

    def body(*refs):
        a_ref, b_refs = refs[0], refs[1:1 + len(Bs)]
        tabs = [t[...] for t in refs[1 + len(Bs):1 + nb]]
        o_ref = refs[1 + nb + ns]
        acc = refs[2 + nb + ns + bool(ns)]
        r = pl.program_id(0)
        if ns:
            sm_start, sm_finish = _small_phases(refs[1 + nb:1 + nb + ns], refs[2 + nb + ns], *refs[4 + nb + ns:])
            pl.when(r == 0)(sm_start)

        @pl.when(r == 0)
        def _():
            acc[...] = jnp.zeros_like(acc)

        bs, off = [], 0
        for i, (b_ref, w) in enumerate(zip(b_refs, widths)):
            nc = next(c for c in (1024, 768, 512) if w % c == 0)
            fn = (lambda t: _unrope(t, *tabs)) if i < n_roped else b_fn
            bs += [(off + c * nc, nc, fn(b_ref[:, c * nc:(c + 1) * nc])) for c in range(w // nc)]
            off += w
        for kk in range(K // kc):
            rows = slice(kk * kc, (kk + 1) * kc)
            at = a_fn(a_ref[:, rows]).T
            for lo, nc, b in bs:
                acc[rows, lo:lo + nc] = acc[rows, lo:lo + nc] + _dot(at, b)

        @pl.when(r == nrt - 1)
        def _():
            if split is None:
                o_ref[...] = _bf(acc[...])
            else:
                for j in range(NDEV):
                    o_ref[j] = _bf(acc[:, split * j:split * (j + 1)])

        if ns:
            pl.when(r == nrt - 1)(sm_finish)

    in_specs = ([pl.BlockSpec((ts, K), lambda r: (r, 0))] + [pl.BlockSpec((ts, w), lambda r: (r, 0)) for w in widths]
                + [pl.BlockSpec((ts, 128), lambda r: (r, 0))] * len(rope))
    out_spec = pl.BlockSpec(out_shape, lambda r: (0,) * len(out_shape))
    scratch = [pltpu.VMEM((K, N), F32)]
    if not ns:
        return pl.pallas_call(
            body, name=name, grid=(nrt,), in_specs=in_specs, out_specs=out_spec,
            out_shape=jax.ShapeDtypeStruct(out_shape, BF16), scratch_shapes=scratch, compiler_params=_params(1),
        )(A, *Bs, *rope)
    return pl.pallas_call(
        body, name=name, grid=(nrt,), in_specs=in_specs + [VM] * ns, out_specs=[out_spec, VM],
        out_shape=[jax.ShapeDtypeStruct(out_shape, BF16), jax.ShapeDtypeStruct((SMALL_ROWS, 1024), F32)],
        scratch_shapes=scratch + [pltpu.VMEM((SMALL_ROWS, 1024), F32), pltpu.VMEM((NDEV, SMALL_ROWS, 1024), F32),
                                  pltpu.SemaphoreType.DMA((7,)), pltpu.SemaphoreType.DMA((7,))],
        compiler_params=_params(1),
    )(A, *Bs, *rope, *small)


def _relu2_bf(a):
    r = jnp.maximum(a.astype(F32), 0.0)
    return _bf(r * r)


def _ident(a):
    return a


def _step(x, p, target, g1, conv_b, gate_b, gn, g_mlp, g_ple, g_fin, sh):
    (g_in, g_conv), (rc, ra, rb) = _gather_weights([sh["w_in"], sh["conv_w"]], [BF16, F32])
    conv_w = g_conv.transpose(1, 0, 2).reshape(4, 1024)
    w_in_p = _join_w_in(g_in)
    (qkv, mqk, mv, mo, gates, u1), (w_out8, w_pg8, w_ple8) = _in_proj(
        x, g1, w_in_p, rc, ra, rb, [sh["w_out"], sh["w_ple_gate"], sh["w_ple"]], [BF16] * 3)
    attn, lse, (w_up8, w_down_a) = _attn_fwd(qkv, [sh["w_up"], sh["w_down"][0:HALF]], [BF16] * 2)
    ml, cs, ns, ms, (w_down_b,) = _mlstm_fwd(mqk, mv, mo, gates, conv_w, conv_b, gate_b, gn,
                                             [sh["w_down"][HALF:2 * HALF]], [BF16])
    w_out, w_pg = w_out8.reshape(D, D), w_pg8.reshape(D, D)
    h1, u2 = _out_proj(x, attn, ml, w_out, g_mlp)
    a, h2 = _mlp_fwd(h1, u2, w_up8, w_down_a, w_down_b)
    dh2, dw_pg, dw_ple8, dg_ple, dg_fin, loss = _ple_loss(h2, p, target, w_pg, w_ple8, g_ple, g_fin)
    da, dh1, dg_mlp = _mlp_bwd(dh2, a, h1, g_mlp, w_up8, w_down_a, w_down_b)
    dw_up8 = _wgrad("wgrad_up", u2, [da], _ident, _ident, (NDEV, D, DFF // NDEV), split=DFF // NDEV)
    dw_down = _wgrad("wgrad_down", a, [dh2], _relu2_bf, _bf, (DFF, D))
    d_attn, d_ml, dw_out = _out_proj_bwd(dh1, attn, ml, w_out)
    (dm, dconv_w, dconv_b, dgn, dgate_b), (r_down,) = _mlstm_bwd(
        mqk, mv, mo, gates, conv_w, conv_b, gate_b, gn, cs, ns, ms, d_ml, [dw_down.reshape(NDEV, DFF // NDEV, D)])
    dq, dk, dv, (r_up, r_out, r_pg, r_ple) = _attn_bwd(
        qkv, attn, lse, d_attn,
        [dw_up8, dw_out.reshape(NDEV, D // NDEV, D), dw_pg.reshape(NDEV, D // NDEV, D), dw_ple8])
    dparts = [dq, dk, dv, dm]
    small = [jnp.zeros((1, D), F32), dconv_b, dgate_b, dgn, dg_mlp, dg_ple, dg_fin, loss]
    dw_in8, total = _wgrad("wgrad_in", u1, dparts, _ident, _ident, (NDEV, D, IN_W // NDEV), split=IN_W // NDEV,
                           small=small + [dconv_w[j:j + 1] for j in range(4)], rope=(rc, ra, rb), n_roped=2)
    dx, dg1, r_in = _in_proj_bwd(dparts, 2, (rc, ra, rb), dh1, x, g1, w_in_p, dw_in8)
    recv = dict(w_in=r_in, w_out=r_out, w_up=r_up, w_down=r_down, w_ple_gate=r_pg, w_ple=r_ple)
    return dx, recv, total, _allreduce_vec(dg1)


def _gather_weights(shards, dtypes):
    nw = len(shards)

    def body(*refs):
        ins, parts = refs[:nw], refs[nw:nw + 4]
        outs, tables = refs[nw + 4:2 * nw + 4], refs[2 * nw + 4:2 * nw + 7]
        start, forward, finish = _gather_phases(ins, outs, refs[2 * nw + 7:3 * nw + 7], *refs[3 * nw + 7:])
        start()
        _rope_fill(*parts, *tables)
        forward()
        finish()

    res = pl.pallas_call(
        body, name="gather_weights",
        in_specs=[VM] * (nw + 4), out_specs=[ANY] * nw + [VM] * 3,
        out_shape=_gather_shapes(shards, dtypes) + [jax.ShapeDtypeStruct((S, 128), F32)] * 3,
        scratch_shapes=_gather_scratch(shards, dtypes),
        compiler_params=_params(),
    )(*shards, *_rope_parts())
    return res[:nw], res[nw:]


def _allreduce_vec(v):
    def body(v_ref, out_ref, pack, rbuf, send_sems, recv_sems):
        start, finish = _small_phases([v_ref], out_ref, pack, rbuf, send_sems, recv_sems)
        start()
        finish()

    return pl.pallas_call(
        body, name="allreduce_last", out_shape=jax.ShapeDtypeStruct((8, 1024), F32),
        scratch_shapes=[pltpu.VMEM((8, 1024), F32), pltpu.VMEM((NDEV, 8, 1024), F32),
                        pltpu.SemaphoreType.DMA((7,)), pltpu.SemaphoreType.DMA((7,))],
        compiler_params=_params(),
    )(v)


ADAM_STEPS = 4


def _adamw(items):
    n = len(items)

    def body(*refs):
        for i in range(n):
            g_ref, w_ref, m_ref, v_ref = refs[4 * i:4 * i + 4]
            go_ref, d_ref, mo_ref, vo_ref = refs[4 * n + 4 * i:4 * n + 4 * i + 4]
            g = g_ref[0].astype(F32)
            for j in range(1, g_ref.shape[0]):
                g = g + g_ref[j].astype(F32)
            go_ref[...] = g
            d_ref[...], mo_ref[...], vo_ref[...] = _adam_update(g, w_ref[...], m_ref[...], v_ref[...])

    in_specs, out_specs, out_shape, args = [], [], [], []
    for gparts, w, m, v in items:
        P, R, C = gparts.shape
        if R % (8 * ADAM_STEPS) == 0:
            tr = R // ADAM_STEPS
            row, gspec = pl.BlockSpec((tr, C), lambda i: (i, 0)), pl.BlockSpec((P, tr, C), lambda i: (0, i, 0))
        else:
            row, gspec = pl.BlockSpec((R, C), lambda i: (0, 0)), pl.BlockSpec((P, R, C), lambda i: (0, 0, 0))
        in_specs += [gspec, row, row, row]
        out_specs += [row] * 4
        out_shape += [jax.ShapeDtypeStruct((R, C), F32)] * 4
        args += [gparts, w, m, v]
    res = pl.pallas_call(
        body, name="adamw", grid=(ADAM_STEPS,), in_specs=in_specs, out_specs=out_specs, out_shape=out_shape,
        compiler_params=_params(1),
    )(*args)
    return [res[4 * i:4 * i + 4] for i in range(n)]


SMALL = ("norm_mix_g", "conv_b", "gate_b", "mlstm_norm_g", "norm_mlp_g", "norm_ple_g", "final_norm_g")


def _adam_update(g, w, m, v):
    c1 = 1.0 - ADAM_B1 ** ADAM_STEP
    c2 = 1.0 - ADAM_B2 ** ADAM_STEP
    m2 = ADAM_B1 * m + (1.0 - ADAM_B1) * g
    v2 = ADAM_B2 * v + (1.0 - ADAM_B2) * (g * g)
    return -ADAM_LR * ((m2 / c1) / (jnp.sqrt(v2 / c2) + ADAM_EPS) + ADAM_WD * w), m2, v2


def _adamw_small(total, first, ws, ms, vs):
    n = len(ws)

    def body(*refs):
        t_ref, f_ref = refs[:2]
        refs = refs[1:]
        outs = refs[1 + 3 * n:]
        for i in range(n):
            w_ref, m_ref, v_ref = refs[1 + i], refs[1 + n + i], refs[1 + 2 * n + i]
            g = (t_ref if i else f_ref)[8 * i:8 * i + 1, 0:w_ref.shape[1]]
            delta, m2, v2 = _adam_update(g, w_ref[...], m_ref[...], v_ref[...])
            for ref, val in zip(outs[4 * i:4 * i + 4], (g, delta, m2, v2)):
                ref[...] = val

    res = pl.pallas_call(
        body, name="adamw_small",
        out_shape=[jax.ShapeDtypeStruct(w.shape, F32) for w in ws for _ in range(4)],
        compiler_params=_params(),
    )(total, first, *ws, *ms, *vs)
    return [res[4 * i:4 * i + 4] for i in range(n)]


def kernel(x, p, norm_mix_g, w_in, conv_w, conv_b, gate_b, mlstm_norm_g, w_out, norm_mlp_g, w_up, w_down, norm_ple_g, w_ple_gate, w_ple, final_norm_g, loss_target, m_norm_mix_g, m_w_in, m_conv_w, m_conv_b, m_gate_b, m_mlstm_norm_g, m_w_out, m_norm_mlp_g, m_w_up, m_w_down, m_norm_ple_g, m_w_ple_gate, m_w_ple, m_final_norm_g, v_norm_mix_g, v_w_in, v_conv_w, v_conv_b, v_gate_b, v_mlstm_norm_g, v_w_out, v_norm_mlp_g, v_w_up, v_w_down, v_norm_ple_g, v_w_ple_gate, v_w_ple, v_final_norm_g):
    big_names = ("w_in", "conv_w", "w_out", "w_up", "w_down", "w_ple_gate", "w_ple")
    wts = dict(w_in=w_in, conv_w=conv_w, w_out=w_out, w_up=w_up, w_down=w_down, w_ple_gate=w_ple_gate, w_ple=w_ple)
    mom = dict(w_in=m_w_in, conv_w=m_conv_w, w_out=m_w_out, w_up=m_w_up, w_down=m_w_down, w_ple_gate=m_w_ple_gate,
               w_ple=m_w_ple)
    var = dict(w_in=v_w_in, conv_w=v_conv_w, w_out=v_w_out, w_up=v_w_up, w_down=v_w_down, w_ple_gate=v_w_ple_gate,
               w_ple=v_w_ple)
    sq = lambda a: a.reshape(a.shape[1:])
    fin = final_norm_g.reshape(1, D)
    dx, recv, total, first = _step(
        x[0], p[0, 0], loss_target[0], norm_mix_g, conv_b, jnp.pad(gate_b, ((0, 0), (0, 120))), mlstm_norm_g,
        norm_mlp_g, norm_ple_g, fin, {n: sq(wts[n]) for n in big_names})

    nrow = 8 * len(SMALL)
    me = _dev_index(*_place())
    conv_rows = total[nrow + 8:nrow + 40:8]
    recv["conv_w"] = lax.dynamic_slice_in_dim(conv_rows, me * 128, 128, axis=1).reshape(1, 4, 128)
    out = {}
    for n, res in zip(big_names, _adamw([(recv[n], sq(wts[n]), sq(mom[n]), sq(var[n])) for n in big_names])):
        out[n] = [t.reshape(wts[n].shape) for t in res]
    sw = dict(norm_mix_g=norm_mix_g, conv_b=conv_b, gate_b=gate_b, mlstm_norm_g=mlstm_norm_g, norm_mlp_g=norm_mlp_g,
              norm_ple_g=norm_ple_g, final_norm_g=fin)
    sm = dict(norm_mix_g=m_norm_mix_g, conv_b=m_conv_b, gate_b=m_gate_b, mlstm_norm_g=m_mlstm_norm_g,
              norm_mlp_g=m_norm_mlp_g, norm_ple_g=m_norm_ple_g, final_norm_g=m_final_norm_g.reshape(1, D))
    sv = dict(norm_mix_g=v_norm_mix_g, conv_b=v_conv_b, gate_b=v_gate_b, mlstm_norm_g=v_mlstm_norm_g,
              norm_mlp_g=v_norm_mlp_g, norm_ple_g=v_norm_ple_g, final_norm_g=v_final_norm_g.reshape(1, D))
    res = _adamw_small(total, first, [sw[n] for n in SMALL], [sm[n] for n in SMALL], [sv[n] for n in SMALL])
    for n, r in zip(SMALL, res):
        out[n] = [t.reshape(final_norm_g.shape) for t in r] if n == "final_norm_g" else list(r)
    order = ("norm_mix_g", "w_in", "conv_w", "conv_b", "gate_b", "mlstm_norm_g", "w_out", "norm_mlp_g", "w_up", "w_down",
             "norm_ple_g", "w_ple_gate", "w_ple", "final_norm_g")
    loss_all = total[nrow, 0]
    return (loss_all, dx[None], *[out[n][0] for n in order], *[out[n][1] for n in order],
            *[out[n][2] for n in order], *[out[n][3] for n in order])
```

```python
import functools
import math

import jax
import jax.numpy as jnp
from jax import lax
from jax.experimental import pallas as pl
from jax.experimental.pallas import tpu as pltpu

F32, BF16 = jnp.float32, jnp.bfloat16
S = 4096
D = 1024
AW = 512
MW = 512
DFF = 4096
PLE = 256
IN_W = 3592
PW = 3840
NDEV = 8
EPS = 1e-6
NEG = -1e30
LC = 128
TB = 512
ROPE_THETA = 500000.0
VMEM_LIMIT = 56 * 1024 * 1024
HI = lax.Precision.HIGHEST

ADAM_LR, ADAM_B1, ADAM_B2, ADAM_EPS, ADAM_WD, ADAM_STEP = 0.001, 0.9, 0.999, 1e-08, 0.01, 10


def _params(n_grid=0, **kw):
    sem = dict(dimension_semantics=("arbitrary",) * n_grid) if n_grid else {}
    return pltpu.CompilerParams(vmem_limit_bytes=VMEM_LIMIT, **sem, **kw)


def _cspec(shape):
    nd = len(shape)
    return pl.BlockSpec(shape, lambda *_: (0,) * nd, pipeline_mode=pl.Buffered(1))


def _dot(a, b):
    return jnp.dot(a, b, preferred_element_type=F32)


def _dot_nt(a, b):
    return lax.dot_general(a, b, (((1,), (1,)), ((), ())), preferred_element_type=F32)


def _dot_tn(a, b):
    return lax.dot_general(a, b, (((0,), (0,)), ((), ())), preferred_element_type=F32)


def _bf(x):
    return x.astype(BF16)


def _rms(x):
    rs = lax.rsqrt(jnp.mean(x * x, axis=-1, keepdims=True) + EPS)
    return x * rs, rs


def _rms_bwd(du, n, rs, g):
    dn = du * g
    return rs * (dn - n * jnp.mean(dn * n, axis=-1, keepdims=True))


def _sigmoid(x):
    return 1.0 / (1.0 + jnp.exp(-x))


ROPE_BLK = 512


def _rope_parts():
    def cs(n, step):
        j = lax.broadcasted_iota(jnp.int32, (n, 128), 1) % 64
        pos = (lax.broadcasted_iota(jnp.int32, (n, 128), 0) * step).astype(F32)
        ang = pos * jnp.power(ROPE_THETA, -(j % 8).astype(F32) / 8.0)
        return jnp.cos(ang), jnp.sin(ang)

    return (*cs(ROPE_BLK, 1), *cs(S // ROPE_BLK, ROPE_BLK))


def _rope_fill(co_ref, so_ref, cb_ref, sb_ref, rc_ref, ra_ref, rb_ref):
    j = lax.broadcasted_iota(jnp.int32, (ROPE_BLK, 128), 1) % 64
    co, so = co_ref[...], so_ref[...]
    for t in range(S // ROPE_BLK):
        cb, sb = cb_ref[t:t + 1, :], sb_ref[t:t + 1, :]
        cos, sin = cb * co - sb * so, sb * co + cb * so
        rows = slice(t * ROPE_BLK, (t + 1) * ROPE_BLK)
        rc_ref[rows, :] = jnp.where(j < 16, cos, 1.0)
        ra_ref[rows, :] = jnp.where(j < 8, -sin, 0.0)
        rb_ref[rows, :] = jnp.where((j >= 8) & (j < 16), sin, 0.0)


def _rope(blk, c, a, b):
    return blk * c + pltpu.roll(blk, 120, 1) * a + pltpu.roll(blk, 8, 1) * b


def _rope_bwd(d, c, a, b):
    return d * c + pltpu.roll(d * a, 8, 1) + pltpu.roll(d * b, 120, 1)


def _unrope(t, c, a, b):
    return jnp.concatenate([_bf(_rope_bwd(t[:, j * 128:(j + 1) * 128].astype(F32), c, a, b))
                            for j in range(t.shape[1] // 128)], axis=1)


MESH = pl.DeviceIdType.MESH
ANY = pl.BlockSpec(memory_space=pl.ANY)
VM = pl.BlockSpec(memory_space=pltpu.VMEM)
FLIPS = [(dx, dy, dc) for dx in (0, 1) for dy in (0, 1) for dc in (0, 1)][1:]


def _place():
    return lax.axis_index("x"), lax.axis_index("y"), lax.axis_index("c")


def _dev_index(px, py, pc):
    return 4 * px + 2 * py + pc


def _gather_phases(ins, outs, bufs, send_sems=None, recv_sems=None, local_sems=None):
    nw = len(ins)
    if nw == 0:
        return (lambda: None,) * 3
    x, y, c = _place()
    me, sib = (x, y, c), (x, y, 1 - c)
    chips = [(1 - x, y), (x, 1 - y), (1 - x, 1 - y)]

    def copy(w, k, block, to, from_buf=False):
        dst = outs[w].at[_dev_index(*block)]
        return pltpu.make_async_remote_copy(
            src_ref=bufs[w] if from_buf else dst, dst_ref=dst, send_sem=send_sems.at[w, k],
            recv_sem=recv_sems.at[w, k], device_id=to, device_id_type=MESH)

    def mine(w):
        return pltpu.make_async_copy(bufs[w], outs[w].at[_dev_index(*me)], local_sems.at[w])

    def first(w):
        return [copy(w, 0, me, sib, True)] + [copy(w, 1 + j, me, (*chip, c), True) for j, chip in enumerate(chips)]

    def passed(w):
        return [copy(w, 4 + j, (*chip, c), sib) for j, chip in enumerate(chips)]

    def start():
        for w in range(nw):
            bufs[w][...] = ins[w][...].astype(bufs[w].dtype)
        for w in range(nw):
            mine(w).start()
            for cp in first(w):
                cp.start()

    def forward():
        for j, chip in enumerate(chips):
            for w in range(nw):
                copy(w, 1 + j, (*chip, c), me).wait_recv()
                passed(w)[j].start()

    def finish():
        for w in range(nw):
            copy(w, 0, sib, me).wait_recv()
        for j, chip in enumerate(chips):
            for w in range(nw):
                copy(w, 4 + j, (*chip, 1 - c), me).wait_recv()
        for w in range(nw):
            for cp in first(w) + passed(w):
                cp.wait_send()
            mine(w).wait()

    return start, forward, finish


def _gather_scratch(shards, dtypes):
    nw = len(shards)
    if nw == 0:
        return []
    return ([pltpu.VMEM(s.shape, dt) for s, dt in zip(shards, dtypes)]
            + [pltpu.SemaphoreType.DMA((nw, 7)), pltpu.SemaphoreType.DMA((nw, 7)), pltpu.SemaphoreType.DMA((nw,))])


def _gather_shapes(shards, dtypes):
    return [jax.ShapeDtypeStruct((NDEV, *s.shape), dt) for s, dt in zip(shards, dtypes)]


def _scatter_phases(ins, outs, send_sems=None, recv_sems=None, local_sems=None):
    nw = len(ins)
    if nw == 0:
        return (lambda: None,) * 2
    x, y, c = _place()
    me = _dev_index(x, y, c)

    def copies():
        out = []
        for w in range(nw):
            out.append(pltpu.make_async_copy(ins[w].at[me], outs[w].at[me], local_sems.at[w]))
            for k, (dx, dy, dc) in enumerate(FLIPS):
                peer = ((x + dx) % 2, (y + dy) % 2, (c + dc) % 2)
                out.append(pltpu.make_async_remote_copy(
                    src_ref=ins[w].at[_dev_index(*peer)], dst_ref=outs[w].at[me], send_sem=send_sems.at[w, k],
                    recv_sem=recv_sems.at[w, k], device_id=peer, device_id_type=MESH))
        return out

    def start():
        for cp in copies():
            cp.start()

    def finish():
        for cp in copies():
            cp.wait()

    return start, finish


def _scatter_scratch(nw):
    if nw == 0:
        return []
    return [pltpu.SemaphoreType.DMA((nw, 7)), pltpu.SemaphoreType.DMA((nw, 7)), pltpu.SemaphoreType.DMA((nw,))]


TM = 512


def _join_w_in(wg):
    sw = IN_W // NDEV

    def body(wg_ref, w_ref):
        for j in range(NDEV):
            w_ref[:, sw * j:sw * (j + 1)] = wg_ref[j]
        w_ref[:, IN_W:PW] = jnp.zeros((D, PW - IN_W), BF16)

    return pl.pallas_call(body, name="join_w_in", out_shape=jax.ShapeDtypeStruct((D, PW), BF16),
                          compiler_params=_params())(wg)


def _in_proj(x, g1, w, rc, ra, rb, shards, dtypes):
    tm = TM
    nw = len(shards)
    nt = S // tm

    def body(*refs):
        x_ref, g_ref, w_ref, rc_ref, ra_ref, rb_ref = refs[:6]
        ins = refs[6:6 + nw]
        qkv_ref, mqk_ref, mv_ref, mo_ref, gt_ref, u_ref = refs[6 + nw:12 + nw]
        outs = refs[12 + nw:12 + 2 * nw]
        bufs = refs[12 + 2 * nw:12 + 3 * nw]
        ag_start, ag_forward, ag_finish = _gather_phases(ins, outs, bufs, *refs[12 + 3 * nw:])
        i = pl.program_id(0)
        pl.when(i == 0)(ag_start)
        pl.when(i == nt - 2)(ag_forward)
        n, _ = _rms(x_ref[...])
        u = _bf(n * g_ref[...])
        u_ref[...] = u
        c, a, b = rc_ref[...], ra_ref[...], rb_ref[...]
        for half in range(2):
            blk = _dot(u, w_ref[:, half * 512:(half + 1) * 512])
            for t in range(4):
                lo = half * 512 + t * 128
                qkv_ref[:, lo:lo + 128] = _rope(blk[:, t * 128:(t + 1) * 128], c, a, b)
        qkv_ref[:, 1024:1536] = _dot(u, w_ref[:, 1024:1536])
        mqk_ref[:, 0:512] = _dot(u, w_ref[:, 1536:2048])
        mqk_ref[:, 512:1024] = _dot(u, w_ref[:, 2048:2560])
        mv_ref[...] = _dot(u, w_ref[:, 2560:3072])
        mo_ref[...] = _dot(u, w_ref[:, 3072:3584])
        gt_ref[...] = _dot(u, w_ref[:, 3584:3712])
        pl.when(i == nt - 1)(ag_finish)

    row = lambda wd: pl.BlockSpec((tm, wd), lambda i: (i, 0))
    res = pl.pallas_call(
        body, name="in_proj", grid=(nt,),
        in_specs=[row(D), _cspec((1, D)), _cspec((D, PW)), row(128), row(128), row(128)] + [VM] * nw,
        out_specs=[row(1536), row(1024), row(512), row(512), row(128), row(D)] + [ANY] * nw,
        out_shape=[jax.ShapeDtypeStruct((S, 1536), F32), jax.ShapeDtypeStruct((S, 1024), F32),
                   jax.ShapeDtypeStruct((S, 512), F32), jax.ShapeDtypeStruct((S, 512), F32),
                   jax.ShapeDtypeStruct((S, 128), F32), jax.ShapeDtypeStruct((S, D), BF16)]
        + _gather_shapes(shards, dtypes),
        scratch_shapes=_gather_scratch(shards, dtypes),
        compiler_params=_params(1),
    )(x, g1, w, rc, ra, rb, *shards)
    return res[:6], res[6:]


DILATIONS = (16, 4, 1)


def _attn_valid(n):
    kd = lax.broadcasted_iota(jnp.int32, (128, 256), 1) - lax.broadcasted_iota(jnp.int32, (128, 256), 0)
    off = jnp.where(n == 0, 0, 128)
    return (kd <= off) & (kd >= off - 128)


def _attn_rows(d, r, n):
    if d == 1:
        q0 = pl.multiple_of(n * 128, 128)
        k0 = pl.multiple_of(jnp.maximum(n - 1, 0) * 128, 128)
        return pl.ds(q0, 128), pl.ds(k0, 256), _attn_valid(n)
    q0 = r + n * 128 * d
    k0 = r + jnp.maximum(n - 1, 0) * 128 * d
    return pl.ds(q0, 128, stride=d), pl.ds(k0, 256, stride=d), _attn_valid(n)


ATTN_GROUP = 4
ATTN_ITERS = S // 128 // ATTN_GROUP


def _attn_group(d, i):
    nb = S // (128 * d)
    if nb == 2:
        qi = lax.broadcasted_iota(jnp.int32, (256, 256), 0) - lax.broadcasted_iota(jnp.int32, (256, 256), 1)
        whole = [pl.ds((ATTN_GROUP // 2) * i + u, 256, stride=d) for u in range(ATTN_GROUP // 2)]
        return [(rows, rows, (qi >= 0) & (qi <= 128)) for rows in whole]
    if d == 1:
        return [_attn_rows(1, 0, i + ATTN_ITERS * u) for u in range(ATTN_GROUP)]
    return [_attn_rows(d, (i // nb) * ATTN_GROUP + u, i % nb) for u in range(ATTN_GROUP)]


def _head0(shape):
    return lax.broadcasted_iota(jnp.int32, shape, 1) < 64


def _stack_heads(t):
    h0 = _head0(t.shape)
    tb = _bf(t)
    zero = jnp.zeros_like(tb)
    return jnp.concatenate([jnp.where(h0, tb, zero), jnp.where(h0, zero, tb)], axis=0)


def _attn_fwd(qkv, shards, dtypes):
    nw = len(shards)

    def body(*refs):
        q_ref, k_ref, v_ref = refs[:3]
        ins = refs[3:3 + nw]
        o_ref, lse0_ref, lse1_ref = refs[3 + nw:6 + nw]
        outs = refs[6 + nw:6 + 2 * nw]
        m0, m1, l0, l1, acc = refs[6 + 2 * nw:11 + 2 * nw]
        bufs = refs[11 + 2 * nw:11 + 3 * nw]
        ag_start, ag_forward, ag_finish = _gather_phases(ins, outs, bufs, *refs[11 + 3 * nw:])
        hp = pl.program_id(0)
        pl.when(hp == 0)(ag_start)
        pl.when(hp == 3)(ag_forward)
        stats = (m0, m1, l0, l1, acc)

        def update(blocks, first):
            loaded = [([q_ref[rq, :], k_ref[rk, :], v_ref[rk, :]], None if first else [ref[rq, :] for ref in stats])
                      for rq, rk, _ in blocks]
            results = []
            for ((q, k, v), prev), (_, _, valid) in zip(loaded, blocks):
                head0 = _head0(q.shape)
                kb, vb = _bf(k), _bf(v)
                q = q * 0.125
                m_new, l_new, acc_new = [], [], []
                for a, qa in enumerate((_bf(jnp.where(head0, q, 0.0)), _bf(jnp.where(head0, 0.0, q)))):
                    s = jnp.where(valid, _dot_nt(qa, kb), NEG)
                    mc = jnp.max(s, axis=-1, keepdims=True)
                    m_a = jnp.broadcast_to(mc, q.shape) if first else jnp.maximum(prev[a], mc)
                    p = jnp.exp(s - jnp.tile(m_a, (1, 2)))
                    l_add = jnp.sum(p, axis=-1, keepdims=True)
                    pv = _dot(_bf(p), vb)
                    if first:
                        l_a = jnp.broadcast_to(l_add, q.shape)
                    else:
                        alpha = jnp.exp(prev[a] - m_a)
                        l_a, pv = alpha * prev[2 + a] + l_add, alpha * prev[4] + pv
                    m_new.append(m_a), l_new.append(l_a), acc_new.append(pv)
                results.append((m_new[0], m_new[1], l_new[0], l_new[1], jnp.where(head0, acc_new[0], acc_new[1])))
            for (rq, _, _), res in zip(blocks, results):
                for ref, val in zip(stats, res):
                    ref[rq, :] = val

        for d in DILATIONS:
            def step(i, carry, d=d):
                update(_attn_group(d, i), d == DILATIONS[0])
                return carry

            lax.fori_loop(0, ATTN_ITERS, step, 0)

        def fin(t, carry):
            rows = pl.ds(pl.multiple_of(t * 256, 256), 256)
            h0 = lax.broadcasted_iota(jnp.int32, (256, 128), 1) < 64
            la, lb = l0[rows, :], l1[rows, :]
            o_ref[rows, :] = acc[rows, :] / jnp.where(h0, la, lb)
            lse0_ref[rows, :] = m0[rows, :] + jnp.log(la)
            lse1_ref[rows, :] = m1[rows, :] + jnp.log(lb)
            return carry

        lax.fori_loop(0, S // 256, fin, 0)
        pl.when(hp == 3)(ag_finish)

    col = lambda off: pl.BlockSpec((S, 128), lambda h, off=off: (0, off + h))
    res = pl.pallas_call(
        body, name="attn_fwd", grid=(4,),
        in_specs=[col(0), col(4), col(8)] + [VM] * nw,
        out_specs=[col(0), col(0), col(0)] + [ANY] * nw,
        out_shape=[jax.ShapeDtypeStruct((S, AW), F32)] * 3 + _gather_shapes(shards, dtypes),
        scratch_shapes=[pltpu.VMEM((S, 128), F32)] * 5 + _gather_scratch(shards, dtypes),
        compiler_params=_params(1),
    )(qkv, qkv, qkv, *shards)
    return res[0], (res[1], res[2]), res[3:]


def _attn_bwd(qkv, o, lse, do, parts):
    nw = len(parts)

    def body(*refs):
        q_ref, k_ref, v_ref, o_ref, L0, L1, do_ref = refs[:7]
        ins = refs[7:7 + nw]
        dq_out, dk_out, dv_out = refs[7 + nw:10 + nw]
        outs = refs[10 + nw:10 + 2 * nw]
        D0, D1, dq_ref, dk_ref, dv_ref = refs[10 + 2 * nw:15 + 2 * nw]
        rs_start, rs_finish = _scatter_phases(ins, outs, *refs[15 + 2 * nw:])
        hp = pl.program_id(0)
        pl.when(hp == 0)(rs_start)

        def pre(t, carry):
            rows = pl.ds(pl.multiple_of(t * 256, 256), 256)
            h0 = lax.broadcasted_iota(jnp.int32, (256, 128), 1) < 64
            dd = do_ref[rows, :] * o_ref[rows, :]
            shp = (256, 128)
            D0[rows, :] = jnp.broadcast_to(jnp.sum(jnp.where(h0, dd, 0.0), axis=-1, keepdims=True), shp)
            D1[rows, :] = jnp.broadcast_to(jnp.sum(jnp.where(h0, 0.0, dd), axis=-1, keepdims=True), shp)
            return carry

        lax.fori_loop(0, S // 256, pre, 0)

        def update(blocks, first):
            loaded = [([q_ref[rq, :], k_ref[rk, :], v_ref[rk, :], do_ref[rq, :]],
                       [L0[rq, :], L1[rq, :], D0[rq, :], D1[rq, :]],
                       [0.0] * 3 if first else [dq_ref[rq, :], dk_ref[rk, :], dv_ref[rk, :]]) for rq, rk, _ in blocks]
            results = []
            for ((q, k, v, dout), (l0v, l1v, d0v, d1v), (dq, dk, dv)), (_, _, valid) in zip(loaded, blocks):
                nq = q.shape[0]
                valid = jnp.concatenate([valid, valid], axis=0)
                q2, do2, kb, vb = _stack_heads(q), _stack_heads(dout), _bf(k), _bf(v)
                cat = lambda a, b: jnp.tile(jnp.concatenate([a, b], axis=0), (1, 2))
                s = jnp.where(valid, _dot_nt(_stack_heads(q * 0.125), kb), NEG)
                p = jnp.exp(s - cat(l0v, l1v))
                ds = _bf(p * (_dot_nt(do2, vb) - cat(d0v, d1v)) * 0.125)
                dq2 = _dot(ds, kb)
                results.append((dq + jnp.where(_head0((nq, 128)), dq2[0:nq], dq2[nq:2 * nq]),
                                dk + _dot_tn(ds, q2), dv + _dot_tn(_bf(p), do2)))
            for (rq, rk, _), (dq, dk, dv) in zip(blocks, results):
                dq_ref[rq, :] = dq
                dk_ref[rk, :] = dk
                dv_ref[rk, :] = dv

        assert S // (128 * DILATIONS[0]) == 2
        for d in DILATIONS:
            def step(i, carry, d=d):
                update(_attn_group(d, i), d == DILATIONS[0])
                return carry

            lax.fori_loop(0, ATTN_ITERS, step, 0)

        def fin(t, carry):
            rows = pl.ds(pl.multiple_of(t * 256, 256), 256)
            for src, dst in ((dq_ref, dq_out), (dk_ref, dk_out), (dv_ref, dv_out)):
                dst[rows, :] = _bf(src[rows, :])
            return carry

        lax.fori_loop(0, S // 256, fin, 0)
        pl.when(hp == 3)(rs_finish)

    col = lambda off: pl.BlockSpec((S, 128), lambda h, off=off: (0, off + h))
    res = pl.pallas_call(
        body, name="attn_bwd", grid=(4,),
        in_specs=[col(0), col(4), col(8), col(0), col(0), col(0), col(0)] + [ANY] * nw,
        out_specs=[col(0), col(0), col(0)] + [ANY] * nw,
        out_shape=[jax.ShapeDtypeStruct((S, AW), BF16)] * 3 + [jax.ShapeDtypeStruct(a.shape, a.dtype) for a in parts],
        scratch_shapes=[pltpu.VMEM((S, 128), F32)] * 5 + _scatter_scratch(nw),
        compiler_params=_params(1),
    )(qkv, qkv, qkv, o, lse[0], lse[1], do, *parts)
    return res[0], res[1], res[2], res[3:]


def _logsig(x):
    return jnp.minimum(x, 0.0) - jnp.log1p(jnp.exp(-jnp.abs(x)))


def _conv_taps(xp, n):
    return [xp[8:] if j == 3 else pltpu.roll(xp, 3 - j, 0)[8:] for j in range(4)]


def _conv_silu(xp, w_ref, b_ref, n):
    taps = _conv_taps(xp, n)
    c = b_ref[...] + sum(w_ref[j:j + 1, :] * taps[j] for j in range(4))
    sg = _sigmoid(c)
    return c, sg, taps


def _chunk_gates(G):
    assert LC == 128
    r = lax.broadcasted_iota(jnp.int32, (LC, LC), 0)
    c = lax.broadcasted_iota(jnp.int32, (LC, LC), 1)
    tril = (c <= r).astype(F32)
    triu = (c >= r).astype(F32)
    b_col = jnp.dot(tril, _logsig(G), preferred_element_type=F32, precision=HI)
    return b_col, b_col.T, G.T, tril, triu


def _colpick(X, lane):
    li = lax.broadcasted_iota(jnp.int32, X.shape, 1)
    return jnp.sum(jnp.where(li == lane, X, 0.0), axis=1, keepdims=True)


def _rowpick(XT, row):
    ri = lax.broadcasted_iota(jnp.int32, XT.shape, 0)
    return jnp.sum(jnp.where(ri == row, XT, 0.0), axis=0, keepdims=True)


def _mlstm_head(qh, kh, vh, G, b_col, b_row, g_row, h, Ch, nh, m_prev):
    bt = _colpick(b_col, 4 + h)
    i_col = _colpick(G, h)
    bs = _rowpick(b_row, 4 + h)
    i_row = _rowpick(g_row, h)
    r = lax.broadcasted_iota(jnp.int32, (LC, LC), 0)
    c = lax.broadcasted_iota(jnp.int32, (LC, LC), 1)
    log_d = jnp.where(c <= r, bt - bs + i_row, NEG)
    log_inter = bt + m_prev
    m_t = jnp.maximum(log_inter, jnp.max(log_d, axis=1, keepdims=True))
    Dm = jnp.exp(log_d - m_t)
    g = jnp.exp(log_inter - m_t)
    qb, kb, vb = _bf(qh), _bf(kh), _bf(vh)
    Am = _dot_nt(qb, kb) * Dm
    qC = _dot(qb, _bf(Ch))
    num = g * qC + _dot(_bf(Am), vb)
    qn = jnp.sum(qh * nh, axis=1, keepdims=True)
    den = g * qn + jnp.sum(Am, axis=1, keepdims=True)
    floor = jnp.exp(-m_t)
    dd = jnp.maximum(jnp.abs(den), floor)
    inv_dd = 1.0 / dd
    hh = num * inv_dd
    lane = lax.broadcasted_iota(jnp.int32, (1, LC), 1)
    blast = jnp.sum(jnp.where(lane == LC - 1, bs, 0.0), axis=1, keepdims=True)
    log_s = blast - bt + i_col
    m_new = jnp.maximum(blast + m_prev, jnp.max(log_s, axis=0, keepdims=True))
    decay = jnp.exp(blast + m_prev - m_new)
    ws = jnp.exp(log_s - m_new)
    kw = kh * ws
    C_new = decay * Ch + _dot_tn(_bf(kw), vb)
    n_new = decay * nh + jnp.sum(kw, axis=0, keepdims=True)
    return dict(Dm=Dm, g=g, Am=Am, qC=qC, qn=qn, den=den, floor=floor, inv_dd=inv_dd, h=hh, decay=decay, ws=ws, kw=kw,
                C_new=C_new, n_new=n_new, m_new=m_new, qb=qb, kb=kb, vb=vb)


def _head_out(hh, mo_h, gn_h):
    r = lax.rsqrt(jnp.mean(hh * hh, axis=-1, keepdims=True) + EPS)
    hn = hh * r
    sg = _sigmoid(mo_h)
    return sg * (hn * gn_h), hn, r, sg


def _mlstm_fwd(mqk, mv, mo, gates, conv_w, conv_b, gate_b, gn, shards, dtypes):
    nblk = S // TB
    ncb = TB // LC
    nw = len(shards)

    def body(*refs):
        x_ref, v_ref, o_ref, g_ref, w_ref, b_ref, gb_ref, gn_ref = refs[:8]
        ins = refs[8:8 + nw]
        out_ref, cs_ref, ns_ref, ms_ref = refs[8 + nw:12 + nw]
        outs = refs[12 + nw:12 + 2 * nw]
        tail, Cst, nst, mst, qs, ks = refs[12 + 2 * nw:18 + 2 * nw]
        bufs = refs[18 + 2 * nw:18 + 3 * nw]
        ag_start, ag_forward, ag_finish = _gather_phases(ins, outs, bufs, *refs[18 + 3 * nw:])
        i = pl.program_id(0)
        pl.when(i == 0)(ag_start)
        pl.when(i == nblk // 2)(ag_forward)

        @pl.when(i == 0)
        def _():
            tail[...] = jnp.zeros_like(tail)
            Cst[...] = jnp.zeros_like(Cst)
            nst[...] = jnp.zeros_like(nst)
            mst[...] = jnp.zeros_like(mst)

        x = x_ref[...]
        xp = jnp.concatenate([tail[...], x], axis=0)
        tail[...] = x[TB - 8:TB, :]
        c, sg, _ = _conv_silu(xp, w_ref, b_ref, TB)
        y = c * sg
        qs[...] = y[:, 0:MW]
        ks[...] = y[:, MW:2 * MW] * (1.0 / math.sqrt(128.0))

        for cc in range(ncb):
            rows = slice(cc * LC, (cc + 1) * LC)
            G = g_ref[rows, :] + gb_ref[...]
            b_col, b_row, g_row, _, _ = _chunk_gates(G)
            cs_ref[cc] = Cst[...]
            ns_ref[cc] = nst[...]
            ms_ref[cc] = mst[...]
            for h in range(4):
                ln = slice(h * 128, (h + 1) * 128)
                m_prev = jnp.max(mst[0:1, ln], axis=1, keepdims=True)
                f = _mlstm_head(qs[rows, ln], ks[rows, ln], v_ref[rows, ln], G, b_col, b_row, g_row, h,
                                Cst[:, ln], nst[0:1, ln], m_prev)
                out, _, _, _ = _head_out(f["h"], o_ref[rows, ln], gn_ref[:, ln])
                out_ref[rows, ln] = out
                Cst[:, ln] = f["C_new"]
                nst[0:1, ln] = f["n_new"]
                mst[0:1, ln] = jnp.broadcast_to(f["m_new"], (1, 128))
        pl.when(i == nblk - 1)(ag_finish)

    row = lambda wd: pl.BlockSpec((TB, wd), lambda i: (i, 0))
    res = pl.pallas_call(
        body, name="mlstm_fwd", grid=(nblk,),
        in_specs=[row(1024), row(MW), row(MW), row(128), _cspec((4, 1024)), _cspec((1, 1024)), _cspec((1, 128)),
                  _cspec((1, MW))] + [VM] * nw,
        out_specs=[row(MW), pl.BlockSpec((ncb, 128, MW), lambda i: (i, 0, 0)),
                   pl.BlockSpec((ncb, 8, MW), lambda i: (i, 0, 0)), pl.BlockSpec((ncb, 8, MW), lambda i: (i, 0, 0))]
        + [ANY] * nw,
        out_shape=[jax.ShapeDtypeStruct((S, MW), F32), jax.ShapeDtypeStruct((S // LC, 128, MW), F32),
                   jax.ShapeDtypeStruct((S // LC, 8, MW), F32), jax.ShapeDtypeStruct((S // LC, 8, MW), F32)]
        + _gather_shapes(shards, dtypes),
        scratch_shapes=[pltpu.VMEM((8, 1024), F32), pltpu.VMEM((128, MW), F32), pltpu.VMEM((8, MW), F32),
                        pltpu.VMEM((8, MW), F32), pltpu.VMEM((TB, MW), F32), pltpu.VMEM((TB, MW), F32)]
        + _gather_scratch(shards, dtypes),
        compiler_params=_params(1),
    )(mqk, mv, mo, gates, conv_w, conv_b, gate_b, gn, *shards)
    return res[0], res[1], res[2], res[3], res[4:]


DM_V, DM_O, DM_G, DM_W = 1024, 1536, 2048, PW - 3 * AW


def _mlstm_bwd(mqk, mv, mo, gates, conv_w, conv_b, gate_b, gn, cs, ns, ms, dout, parts):
    nblk = S // TB
    ncb = TB // LC
    kscale = 1.0 / math.sqrt(128.0)
    nw = len(parts)

    def body(*refs):
        x_ref, xprev_ref, v_ref, o_ref, g_ref, w_ref, b_ref, gb_ref, gn_ref, cs_ref, ns_ref, ms_ref, do_ref = refs[:13]
        ins = refs[13:13 + nw]
        dm_ref, dw_ref, db_ref, dgn_ref, dgb_ref = refs[13 + nw:18 + nw]
        outs = refs[18 + nw:18 + 2 * nw]
        dCst, dnst, dyhead, qs, ks, dqk = refs[18 + 2 * nw:24 + 2 * nw]
        rs_start, rs_finish = _scatter_phases(ins, outs, *refs[24 + 2 * nw:])
        i = pl.program_id(0)
        blk = nblk - 1 - i
        pl.when(i == 0)(rs_start)

        @pl.when(i == 0)
        def _():
            dCst[...] = jnp.zeros_like(dCst)
            dnst[...] = jnp.zeros_like(dnst)
            dyhead[...] = jnp.zeros_like(dyhead)
            dw_ref[...] = jnp.zeros_like(dw_ref)
            db_ref[...] = jnp.zeros_like(db_ref)
            dgn_ref[...] = jnp.zeros_like(dgn_ref)
            dgb_ref[...] = jnp.zeros_like(dgb_ref)

        x = x_ref[...]
        xprev = jnp.where(blk == 0, 0.0, xprev_ref[...])
        xp = jnp.concatenate([xprev, x], axis=0)
        c, sg, taps = _conv_silu(xp, w_ref, b_ref, TB)
        y = c * sg
        qs[...] = y[:, 0:MW]
        ks[...] = y[:, MW:2 * MW] * kscale
        lane128 = lax.broadcasted_iota(jnp.int32, (LC, 128), 1)
        rowi = lax.broadcasted_iota(jnp.int32, (LC, 1), 0)
        ones = jnp.ones((LC, 128), F32)

        for cc in reversed(range(ncb)):
            rows = slice(cc * LC, (cc + 1) * LC)
            G = g_ref[rows, :] + gb_ref[...]
            b_col, b_row, g_row, _, triu = _chunk_gates(G)
            dB = jnp.zeros((LC, 128), F32)
            dI = jnp.zeros((LC, 128), F32)
            for h in range(4):
                ln = slice(h * 128, (h + 1) * 128)
                Ch = cs_ref[cc, :, ln]
                nh = ns_ref[cc, 0:1, ln]
                m_prev = jnp.max(ms_ref[cc, 0:1, ln], axis=1, keepdims=True)
                qh, kh, vh = qs[rows, ln], ks[rows, ln], v_ref[rows, ln]
                f = _mlstm_head(qh, kh, vh, G, b_col, b_row, g_row, h, Ch, nh, m_prev)
                hh, inv_dd, den, g, Am, Dm = f["h"], f["inv_dd"], f["den"], f["g"], f["Am"], f["Dm"]
                qb, kb, vb = f["qb"], f["kb"], f["vb"]
                gn_h = gn_ref[:, ln]
                _, hn, r, sgo = _head_out(hh, o_ref[rows, ln], gn_h)
                do = do_ref[rows, ln]
                hm = hn * gn_h
                dm_ref[rows, DM_O + h * 128:DM_O + (h + 1) * 128] = _bf(do * hm * sgo * (1.0 - sgo))
                dhm = do * sgo
                dgn_ref[:, ln] = dgn_ref[:, ln] + jnp.sum(dhm * hn, axis=0, keepdims=True)
                dhn = dhm * gn_h
                dh = r * (dhn - hn * jnp.mean(dhn * hn, axis=-1, keepdims=True))
                dnum = dh * inv_dd
                ddd = -jnp.sum(dh * hh, axis=1, keepdims=True) * inv_dd
                dden = jnp.where(jnp.abs(den) >= f["floor"], ddd * jnp.sign(den), 0.0)
                dnb = _bf(dnum)
                dA = _dot_nt(dnb, vb) + dden
                dv = _dot_tn(_bf(Am), dnb)
                gd = _bf(g * dnum)
                gq = g * dden
                dq = _dot_nt(gd, _bf(Ch)) + gq * nh
                dCn = dCst[:, ln]
                dnn = dnst[0:1, ln]
                dC = f["decay"] * dCn + _dot_tn(qb, gd)
                dn = f["decay"] * dnn + jnp.sum(gq * qh, axis=0, keepdims=True)
                dg = jnp.sum(dnum * f["qC"], axis=1, keepdims=True) + dden * f["qn"]
                dS = _bf(dA * Dm)
                dq = dq + _dot(dS, kb)
                dk = _dot_tn(dS, qb)
                Gm = dA * Am
                gam = dg * g
                dCb = _bf(dCn)
                E = _dot_nt(vb, dCb) + dnn
                ws = f["ws"]
                dk = dk + ws * E
                om = jnp.sum(E * kh, axis=1, keepdims=True) * ws
                dv = dv + _dot(_bf(f["kw"]), dCb)
                ddecay = (jnp.sum(jnp.sum(dCn * Ch, axis=1, keepdims=True), axis=0, keepdims=True)
                          + jnp.sum(dnn * nh, axis=1, keepdims=True))
                delta = ddecay * f["decay"]
                rows_g = jnp.sum(Gm, axis=1, keepdims=True)
                cols_g = jnp.broadcast_to(jnp.sum(Gm, axis=0, keepdims=True), (LC, 128)).T
                last = jnp.where(rowi == LC - 1, jnp.sum(om, axis=0, keepdims=True) + delta, 0.0)
                db = rows_g + gam - om + last - cols_g
                di = cols_g + om
                dB = jnp.where(lane128 == 4 + h, db, dB)
                dI = jnp.where(lane128 == h, di, dI)
                dCst[:, ln] = dC
                dnst[0:1, ln] = dn
                dqk[rows, ln] = dq
                dqk[rows, MW + h * 128:MW + (h + 1) * 128] = dk * kscale
                dm_ref[rows, DM_V + h * 128:DM_V + (h + 1) * 128] = _bf(dv)
            dlogf = jnp.dot(triu, dB, preferred_element_type=F32, precision=HI)
            dG = dI + dlogf * _sigmoid(-G)
            dG = jnp.where(lane128 < 8, dG, 0.0)
            dm_ref[rows, DM_G:DM_G + 128] = _bf(dG)
            dm_ref[rows, DM_G + 128:DM_W] = jnp.zeros((LC, DM_W - DM_G - 128), BF16)
            dgb_ref[...] = dgb_ref[...] + jnp.sum(dG, axis=0, keepdims=True)

        dy = dqk[...] * (sg * (1.0 + c * (1.0 - sg)))
        db_ref[...] = db_ref[...] + jnp.sum(dy, axis=0, keepdims=True)
        for j in range(4):
            dw_ref[j:j + 1, :] = dw_ref[j:j + 1, :] + jnp.sum(dy * taps[j], axis=0, keepdims=True)
        dyp = jnp.concatenate([dy, dyhead[...]], axis=0)
        dx = w_ref[3:4, :] * dy
        for j in range(3):
            dx = dx + w_ref[j:j + 1, :] * pltpu.roll(dyp, TB + 8 - (3 - j), 0)[0:TB]
        dm_ref[:, 0:DM_V] = _bf(dx)
        dyhead[...] = dy[0:8, :]
        pl.when(i == nblk - 1)(rs_finish)

    rrow = lambda wd: pl.BlockSpec((TB, wd), lambda i: (nblk - 1 - i, 0))
    st = lambda r: pl.BlockSpec((ncb, r, MW), lambda i: (nblk - 1 - i, 0, 0))
    prev8 = pl.BlockSpec((8, 1024), lambda i: (jnp.maximum((nblk - 1 - i) * (TB // 8) - 1, 0), 0))
    res = pl.pallas_call(
        body, name="mlstm_bwd", grid=(nblk,),
        in_specs=[rrow(1024), prev8, rrow(MW), rrow(MW), rrow(128), _cspec((4, 1024)), _cspec((1, 1024)),
                  _cspec((1, 128)), _cspec((1, MW)), st(128), st(8), st(8), rrow(MW)] + [ANY] * nw,
        out_specs=[rrow(DM_W),
                   pl.BlockSpec((4, 1024), lambda i: (0, 0)), pl.BlockSpec((1, 1024), lambda i: (0, 0)),
                   pl.BlockSpec((1, MW), lambda i: (0, 0)), pl.BlockSpec((1, 128), lambda i: (0, 0))] + [ANY] * nw,
        out_shape=[jax.ShapeDtypeStruct((S, DM_W), BF16),
                   jax.ShapeDtypeStruct((4, 1024), F32), jax.ShapeDtypeStruct((1, 1024), F32),
                   jax.ShapeDtypeStruct((1, MW), F32), jax.ShapeDtypeStruct((1, 128), F32)]
        + [jax.ShapeDtypeStruct(a.shape, a.dtype) for a in parts],
        scratch_shapes=[pltpu.VMEM((128, MW), F32), pltpu.VMEM((8, MW), F32), pltpu.VMEM((8, 1024), F32),
                        pltpu.VMEM((TB, MW), F32), pltpu.VMEM((TB, MW), F32), pltpu.VMEM((TB, 1024), F32)]
        + _scatter_scratch(nw),
        compiler_params=_params(1),
    )(mqk, mqk, mv, mo, gates, conv_w, conv_b, gate_b, gn, cs, ns, ms, dout, *parts)
    return res[:5], res[5:]


def _out_proj(x, attn, ml, w, g):
    tm = TM

    def body(x_ref, a_ref, m_ref, w_ref, g_ref, h_ref, u_ref):
        h1 = x_ref[...] + _dot(_bf(a_ref[...]), w_ref[0:AW, :]) + _dot(_bf(m_ref[...]), w_ref[AW:D, :])
        h_ref[...] = h1
        n, _ = _rms(h1)
        u_ref[...] = _bf(n * g_ref[...])

    row = lambda wd: pl.BlockSpec((tm, wd), lambda i: (i, 0))
    return pl.pallas_call(
        body, name="out_proj", grid=(S // tm,),
        in_specs=[row(D), row(AW), row(MW), _cspec((D, D)), _cspec((1, D))],
        out_specs=[row(D), row(D)],
        out_shape=[jax.ShapeDtypeStruct((S, D), F32), jax.ShapeDtypeStruct((S, D), BF16)],
        compiler_params=_params(1),
    )(x, attn, ml, w, g)


HALF = DFF // NDEV // 2


def _mlp_fwd(h1, u2, w_up, w_down_a, w_down_b):
    tm = TM

    def body(h_ref, u_ref, wu_ref, wa_ref, wb_ref, a_ref, o_ref):
        u = u_ref[...]
        acc = h_ref[...]
        for c in range(NDEV):
            cols = slice(c * 512, (c + 1) * 512)
            a = _dot(u, wu_ref[c])
            a_ref[:, cols] = _bf(a)
            r = jnp.maximum(a, 0.0)
            r = _bf(r * r)
            acc = acc + _dot(r[:, 0:HALF], wa_ref[c]) + _dot(r[:, HALF:2 * HALF], wb_ref[c])
        o_ref[...] = acc

    row = lambda wd: pl.BlockSpec((tm, wd), lambda i: (i, 0))
    return pl.pallas_call(
        body, name="mlp_fwd", grid=(S // tm,),
        in_specs=[row(D), row(D), _cspec((NDEV, D, DFF // NDEV)), _cspec((NDEV, HALF, D)), _cspec((NDEV, HALF, D))],
        out_specs=[row(DFF), row(D)],
        out_shape=[jax.ShapeDtypeStruct((S, DFF), BF16), jax.ShapeDtypeStruct((S, D), F32)],
        compiler_params=_params(1),
    )(h1, u2, w_up, w_down_a, w_down_b)


def _ple_loss(h2, p, target, w_pg, w_ple, g_ple, g_fin):
    tm = TM

    def body(h_ref, p_ref, t_ref, wg_ref, wp_ref, gp_ref, gf_ref,
             dh_ref, dwg_ref, dwp_ref, dgp_ref, dgf_ref, loss_ref, acc_g, acc_p):
        i = pl.program_id(0)

        @pl.when(i == 0)
        def _():
            acc_g[...] = jnp.zeros_like(acc_g)
            acc_p[...] = jnp.zeros_like(acc_p)
            dgp_ref[...] = jnp.zeros_like(dgp_ref)
            dgf_ref[...] = jnp.zeros_like(dgf_ref)
            loss_ref[...] = jnp.zeros_like(loss_ref)

        h2v = h_ref[...]
        n2, rs2 = _rms(h2v)
        u3 = _bf(n2 * gp_ref[...])
        gt = _sigmoid(_dot(u3, wg_ref[...]))
        pb = _bf(p_ref[...])
        e = jnp.concatenate([_dot(pb, wp_ref[j]) for j in range(NDEV)], axis=1)
        h3 = h2v + gt * e
        n3, rs3 = _rms(h3)
        err = n3 * gf_ref[...] - t_ref[...]
        loss_ref[...] = loss_ref[...] + 0.5 / D * jnp.sum(jnp.sum(err * err, axis=1, keepdims=True), axis=0, keepdims=True)
        dy = err * (1.0 / D)
        dgf_ref[...] = dgf_ref[...] + jnp.sum(dy * n3, axis=0, keepdims=True)
        dh3 = _rms_bwd(dy, n3, rs3, gf_ref[...])
        de = _bf(dh3 * gt)
        dz = _bf(dh3 * e * gt * (1.0 - gt))
        acc_p[...] = acc_p[...] + _dot_tn(pb, de)
        acc_g[...] = acc_g[...] + _dot_tn(u3, dz)
        du3 = _dot_nt(dz, wg_ref[...])
        dgp_ref[...] = dgp_ref[...] + jnp.sum(du3 * n2, axis=0, keepdims=True)
        dh_ref[...] = dh3 + _rms_bwd(du3, n2, rs2, gp_ref[...])

        @pl.when(i == S // tm - 1)
        def _():
            dwg_ref[...] = _bf(acc_g[...])
            for j in range(NDEV):
                dwp_ref[j] = _bf(acc_p[:, j * 128:(j + 1) * 128])

    row = lambda wd: pl.BlockSpec((tm, wd), lambda i: (i, 0))
    whole = lambda shp: pl.BlockSpec(shp, lambda i: (0,) * len(shp))
    return pl.pallas_call(
        body, name="ple_loss", grid=(S // tm,),
        in_specs=[row(D), row(PLE), row(D), _cspec((D, D)), _cspec((NDEV, PLE, 128)), _cspec((1, D)), _cspec((1, D))],
        out_specs=[row(D), whole((D, D)), whole((NDEV, PLE, 128)), whole((1, D)), whole((1, D)), whole((1, 1))],
        out_shape=[jax.ShapeDtypeStruct((S, D), F32), jax.ShapeDtypeStruct((D, D), BF16),
                   jax.ShapeDtypeStruct((NDEV, PLE, 128), BF16), jax.ShapeDtypeStruct((1, D), F32),
                   jax.ShapeDtypeStruct((1, D), F32), jax.ShapeDtypeStruct((1, 1), F32)],
        scratch_shapes=[pltpu.VMEM((D, D), F32), pltpu.VMEM((PLE, D), F32)],
        compiler_params=_params(1),
    )(h2, p, target, w_pg, w_ple, g_ple, g_fin)


def _mlp_bwd(dh2, a, h1, g, w_up, w_down_a, w_down_b):
    tm = TM

    def body(d_ref, a_ref, h_ref, g_ref, wu_ref, wa_ref, wb_ref, da_ref, dh1_ref, dg_ref):
        @pl.when(pl.program_id(0) == 0)
        def _():
            dg_ref[...] = jnp.zeros_like(dg_ref)

        dh2v = d_ref[...]
        db = _bf(dh2v)
        du = jnp.zeros((tm, D), F32)
        for c in range(NDEV):
            cols = slice(c * 512, (c + 1) * 512)
            dr = jnp.concatenate([_dot_nt(db, wa_ref[c]), _dot_nt(db, wb_ref[c])], axis=1)
            da = _bf(dr * (2.0 * jnp.maximum(a_ref[:, cols], 0.0)))
            da_ref[:, cols] = da
            du = du + _dot_nt(da, wu_ref[c])
        n, rs = _rms(h_ref[...])
        dg_ref[...] = dg_ref[...] + jnp.sum(du * n, axis=0, keepdims=True)
        dh1_ref[...] = dh2v + _rms_bwd(du, n, rs, g_ref[...])

    row = lambda wd: pl.BlockSpec((tm, wd), lambda i: (i, 0))
    return pl.pallas_call(
        body, name="mlp_bwd", grid=(S // tm,),
        in_specs=[row(D), row(DFF), row(D), _cspec((1, D)), _cspec((NDEV, D, DFF // NDEV)), _cspec((NDEV, HALF, D)),
                  _cspec((NDEV, HALF, D))],
        out_specs=[row(DFF), row(D), pl.BlockSpec((1, D), lambda i: (0, 0))],
        out_shape=[jax.ShapeDtypeStruct((S, DFF), BF16), jax.ShapeDtypeStruct((S, D), F32),
                   jax.ShapeDtypeStruct((1, D), F32)],
        compiler_params=_params(1),
    )(dh2, a, h1, g, w_up, w_down_a, w_down_b)


def _out_proj_bwd(dh1, attn, ml, w):
    tm = TM

    def body(d_ref, a_ref, m_ref, w_ref, da_ref, dm_ref, dw_ref, acc):
        i = pl.program_id(0)

        @pl.when(i == 0)
        def _():
            acc[...] = jnp.zeros_like(acc)

        db = _bf(d_ref[...])
        dmix = _dot_nt(db, w_ref[...])
        da_ref[...] = dmix[:, 0:AW]
        dm_ref[...] = dmix[:, AW:D]
        acc[0:AW, :] = acc[0:AW, :] + _dot_tn(_bf(a_ref[...]), db)
        acc[AW:D, :] = acc[AW:D, :] + _dot_tn(_bf(m_ref[...]), db)

        @pl.when(i == S // tm - 1)
        def _():
            dw_ref[...] = _bf(acc[...])

    row = lambda wd: pl.BlockSpec((tm, wd), lambda i: (i, 0))
    return pl.pallas_call(
        body, name="out_proj_bwd", grid=(S // tm,),
        in_specs=[row(D), row(AW), row(MW), _cspec((D, D))],
        out_specs=[row(AW), row(MW), pl.BlockSpec((D, D), lambda i: (0, 0))],
        out_shape=[jax.ShapeDtypeStruct((S, AW), F32), jax.ShapeDtypeStruct((S, MW), F32),
                   jax.ShapeDtypeStruct((D, D), BF16)],
        scratch_shapes=[pltpu.VMEM((D, D), F32)],
        compiler_params=_params(1),
    )(dh1, attn, ml, w)


CHIP_FLIPS = [(0, 0), (0, 1), (1, 0), (1, 1)]


def _scatter2_phases(in_ref, out_ref, mine_v, sib_v, psum_v, loc_sems, d2d_send, d2d_recv, ici_send, ici_recv, own_sem):
    x, y, c = _place()
    chips = [((x + dx) % 2, (y + dy) % 2) for dx, dy in CHIP_FLIPS]
    nc = len(chips)

    def local(k):
        return pltpu.make_async_copy(in_ref.at[_dev_index(*chips[k], c)], mine_v.at[k], loc_sems.at[k])

    def to_sib(k):
        return pltpu.make_async_remote_copy(
            src_ref=in_ref.at[_dev_index(*chips[k], 1 - c)], dst_ref=sib_v.at[k], send_sem=d2d_send.at[k],
            recv_sem=d2d_recv.at[k], device_id=(x, y, 1 - c), device_id_type=MESH)

    def over_ici(k):
        return pltpu.make_async_remote_copy(
            src_ref=psum_v.at[k], dst_ref=out_ref.at[k], send_sem=ici_send.at[k - 1], recv_sem=ici_recv.at[k - 1],
            device_id=(*chips[k], c), device_id_type=MESH)

    def own():
        return pltpu.make_async_copy(psum_v.at[0], out_ref.at[0], own_sem)

    def start():
        for k in range(nc):
            to_sib(k).start()
            local(k).start()

    def middle():
        for k in (1, 2, 3, 0):
            local(k).wait()
            to_sib(k).wait_recv()
            psum_v[k] = _bf(mine_v[k].astype(F32) + sib_v[k].astype(F32))
            (over_ici(k) if k else own()).start()

    def finish():
        for k in range(1, nc):
            over_ici(k).wait()
        for k in range(nc):
            to_sib(k).wait_send()
        own().wait()

    return start, middle, finish


def _scatter2_scratch(shard, dtype):
    nc = len(CHIP_FLIPS)
    return ([pltpu.VMEM((nc, *shard), dtype)] * 3
            + [pltpu.SemaphoreType.DMA((nc,))] * 3 + [pltpu.SemaphoreType.DMA((nc - 1,))] * 2 + [pltpu.SemaphoreType.DMA])


def _in_proj_bwd(dparts, n_roped, rope, dh1, x, g1, w, part):
    tm = TM
    nt = S // tm
    widths = [d.shape[1] for d in dparts]
    assert sum(widths) == PW
    npar = len(dparts)

    def body(*refs):
        d_refs = refs[:npar]
        tabs = [t[...] for t in refs[npar:npar + 3]]
        dh_ref, x_ref, g_ref, w_ref, in_ref, dx_ref, dg_ref, out_ref = refs[npar + 3:npar + 11]
        rs_start, rs_middle, rs_finish = _scatter2_phases(in_ref, out_ref, *refs[npar + 11:])
        i = pl.program_id(0)
        pl.when(i == 0)(rs_start)
        pl.when(i == 1)(rs_middle)

        @pl.when(i == 0)
        def _():
            dg_ref[...] = jnp.zeros_like(dg_ref)

        du = jnp.zeros((tm, D), F32)
        off = 0
        for j, (d_ref, wd) in enumerate(zip(d_refs, widths)):
            nc = next(c for c in (768, 512) if wd % c == 0)
            for s in range(wd // nc):
                d = d_ref[:, s * nc:(s + 1) * nc]
                du = du + _dot_nt(_unrope(d, *tabs) if j < n_roped else d, w_ref[:, off + s * nc:off + (s + 1) * nc])
            off += wd
        n, rs = _rms(x_ref[...])
        dg_ref[...] = dg_ref[...] + jnp.sum(du * n, axis=0, keepdims=True)
        dx_ref[...] = dh_ref[...] + _rms_bwd(du, n, rs, g_ref[...])
        pl.when(i == nt - 1)(rs_finish)

    row = lambda wd: pl.BlockSpec((tm, wd), lambda i: (i, 0))
    shard = part.shape[1:]
    return pl.pallas_call(
        body, name="in_proj_bwd", grid=(nt,),
        in_specs=[row(wd) for wd in widths] + [row(128)] * 3 + [row(D), row(D), _cspec((1, D)), _cspec((D, PW)), ANY],
        out_specs=[row(D), pl.BlockSpec((1, D), lambda i: (0, 0)), ANY],
        out_shape=[jax.ShapeDtypeStruct((S, D), F32), jax.ShapeDtypeStruct((1, D), F32),
                   jax.ShapeDtypeStruct((len(CHIP_FLIPS), *shard), part.dtype)],
        scratch_shapes=_scatter2_scratch(shard, part.dtype),
        compiler_params=_params(1),
    )(*dparts, *rope, dh1, x, g1, w, part)


SMALL_ROWS = 96


def _small_phases(ins, out_ref, pack, rbuf, send_sems, recv_sems):
    x, y, c = _place()
    me = _dev_index(x, y, c)

    def copies():
        out = []
        for k, (dx, dy, dc) in enumerate(FLIPS):
            peer = ((x + dx) % 2, (y + dy) % 2, (c + dc) % 2)
            out.append(pltpu.make_async_remote_copy(
                src_ref=pack, dst_ref=rbuf.at[me], send_sem=send_sems.at[k], recv_sem=recv_sems.at[k],
                device_id=peer, device_id_type=MESH))
        return out

    def start():
        pack[...] = jnp.zeros_like(pack)
        for i, ref in enumerate(ins):
            pack[8 * i:8 * i + 1, 0:ref.shape[1]] = ref[...]
        rbuf[me] = pack[...]
        for cp in copies():
            cp.start()

    def finish():
        for cp in copies():
            cp.wait()
        tot = rbuf[0]
        for j in range(1, NDEV):
            tot = tot + rbuf[j]
        out_ref[...] = tot

    return start, finish


def _wgrad(name, A, Bs, a_fn, b_fn, out_shape, split=None, ts=512, small=(), rope=(), n_roped=0):
    K = A.shape[1]
    widths = [b.shape[1] for b in Bs]
    N = sum(widths)
    nb, ns, nrt = len(Bs) + len(rope), len(small), S // ts
    kc = min(K, 1024)

    def body(*refs):
        a_ref, b_refs = refs[0], refs[1:1 + len(Bs)]
        tabs = [t[...] for t in refs[1 + len(Bs):1 + nb]]
        o_ref = refs[1 + nb + ns]
        acc = refs[2 + nb + ns + bool(ns)]
        r = pl.program_id(0)
        if ns:
            sm_start, sm_finish = _small_phases(refs[1 + nb:1 + nb + ns], refs[2 + nb + ns], *refs[4 + nb + ns:])
            pl.when(r == 0)(sm_start)

        @pl.when(r == 0)
        def _():
            acc[...] = jnp.zeros_like(acc)

        bs, off = [], 0
        for i, (b_ref, w) in enumerate(zip(b_refs, widths)):
            nc = next(c for c in (1024, 768, 512) if w % c == 0)
            fn = (lambda t: _unrope(t, *tabs)) if i < n_roped else b_fn
            bs += [(off + c * nc, nc, fn(b_ref[:, c * nc:(c + 1) * nc])) for c in range(w // nc)]
            off += w
        for kk in range(K // kc):
            rows = slice(kk * kc, (kk + 1) * kc)
            at = a_fn(a_ref[:, rows]).T
            for lo, nc, b in bs:
                acc[rows, lo:lo + nc] = acc[rows, lo:lo + nc] + _dot(at, b)

        @pl.when(r == nrt - 1)
        def _():
            if split is None:
                o_ref[...] = _bf(acc[...])
            else:
                for j in range(NDEV):
                    o_ref[j] = _bf(acc[:, split * j:split * (j + 1)])

        if ns:
            pl.when(r == nrt - 1)(sm_finish)

    in_specs = ([pl.BlockSpec((ts, K), lambda r: (r, 0))] + [pl.BlockSpec((ts, w), lambda r: (r, 0)) for w in widths]
                + [pl.BlockSpec((ts, 128), lambda r: (r, 0))] * len(rope))
    out_spec = pl.BlockSpec(out_shape, lambda r: (0,) * len(out_shape))
    scratch = [pltpu.VMEM((K, N), F32)]
    if not ns:
        return pl.pallas_call(
            body, name=name, grid=(nrt,), in_specs=in_specs, out_specs=out_spec,
            out_shape=jax.ShapeDtypeStruct(out_shape, BF16), scratch_shapes=scratch, compiler_params=_params(1),
        )(A, *Bs, *rope)
    return pl.pallas_call(
        body, name=name, grid=(nrt,), in_specs=in_specs + [VM] * ns, out_specs=[out_spec, VM],
        out_shape=[jax.ShapeDtypeStruct(out_shape, BF16), jax.ShapeDtypeStruct((SMALL_ROWS, 1024), F32)],
        scratch_shapes=scratch + [pltpu.VMEM((SMALL_ROWS, 1024), F32), pltpu.VMEM((NDEV, SMALL_ROWS, 1024), F32),
                                  pltpu.SemaphoreType.DMA((7,)), pltpu.SemaphoreType.DMA((7,))],
        compiler_params=_params(1),
    )(A, *Bs, *rope, *small)


def _relu2_bf(a):
    r = jnp.maximum(a.astype(F32), 0.0)
    return _bf(r * r)


def _ident(a):
    return a


def _step(x, p, target, g1, conv_b, gate_b, gn, g_mlp, g_ple, g_fin, sh):
    (g_in, g_conv), (rc, ra, rb) = _gather_weights([sh["w_in"], sh["conv_w"]], [BF16, F32])
    conv_w = g_conv.transpose(1, 0, 2).reshape(4, 1024)
    w_in_p = _join_w_in(g_in)
    (qkv, mqk, mv, mo, gates, u1), (w_out8, w_pg8, w_ple8) = _in_proj(
        x, g1, w_in_p, rc, ra, rb, [sh["w_out"], sh["w_ple_gate"], sh["w_ple"]], [BF16] * 3)
    attn, lse, (w_up8, w_down_a) = _attn_fwd(qkv, [sh["w_up"], sh["w_down"][0:HALF]], [BF16] * 2)
    ml, cs, ns, ms, (w_down_b,) = _mlstm_fwd(mqk, mv, mo, gates, conv_w, conv_b, gate_b, gn,
                                             [sh["w_down"][HALF:2 * HALF]], [BF16])
    w_out, w_pg = w_out8.reshape(D, D), w_pg8.reshape(D, D)
    h1, u2 = _out_proj(x, attn, ml, w_out, g_mlp)
    a, h2 = _mlp_fwd(h1, u2, w_up8, w_down_a, w_down_b)
    dh2, dw_pg, dw_ple8, dg_ple, dg_fin, loss = _ple_loss(h2, p, target, w_pg, w_ple8, g_ple, g_fin)
    da, dh1, dg_mlp = _mlp_bwd(dh2, a, h1, g_mlp, w_up8, w_down_a, w_down_b)
    dw_up8 = _wgrad("wgrad_up", u2, [da], _ident, _ident, (NDEV, D, DFF // NDEV), split=DFF // NDEV)
    dw_down = _wgrad("wgrad_down", a, [dh2], _relu2_bf, _bf, (DFF, D))
    d_attn, d_ml, dw_out = _out_proj_bwd(dh1, attn, ml, w_out)
    (dm, dconv_w, dconv_b, dgn, dgate_b), (r_down,) = _mlstm_bwd(
        mqk, mv, mo, gates, conv_w, conv_b, gate_b, gn, cs, ns, ms, d_ml, [dw_down.reshape(NDEV, DFF // NDEV, D)])
    dq, dk, dv, (r_up, r_out, r_pg, r_ple) = _attn_bwd(
        qkv, attn, lse, d_attn,
        [dw_up8, dw_out.reshape(NDEV, D // NDEV, D), dw_pg.reshape(NDEV, D // NDEV, D), dw_ple8])
    dparts = [dq, dk, dv, dm]
    small = [jnp.zeros((1, D), F32), dconv_b, dgate_b, dgn, dg_mlp, dg_ple, dg_fin, loss]
    dw_in8, total = _wgrad("wgrad_in", u1, dparts, _ident, _ident, (NDEV, D, IN_W // NDEV), split=IN_W // NDEV,
                           small=small + [dconv_w[j:j + 1] for j in range(4)], rope=(rc, ra, rb), n_roped=2)
    dx, dg1, r_in = _in_proj_bwd(dparts, 2, (rc, ra, rb), dh1, x, g1, w_in_p, dw_in8)
    recv = dict(w_in=r_in, w_out=r_out, w_up=r_up, w_down=r_down, w_ple_gate=r_pg, w_ple=r_ple)
    return dx, recv, total, _allreduce_vec(dg1)


def _gather_weights(shards, dtypes):
    nw = len(shards)

    def body(*refs):
        ins, parts = refs[:nw], refs[nw:nw + 4]
        outs, tables = refs[nw + 4:2 * nw + 4], refs[2 * nw + 4:2 * nw + 7]
        start, forward, finish = _gather_phases(ins, outs, refs[2 * nw + 7:3 * nw + 7], *refs[3 * nw + 7:])
        start()
        _rope_fill(*parts, *tables)
        forward()
        finish()

    res = pl.pallas_call(
        body, name="gather_weights",
        in_specs=[VM] * (nw + 4), out_specs=[ANY] * nw + [VM] * 3,
        out_shape=_gather_shapes(shards, dtypes) + [jax.ShapeDtypeStruct((S, 128), F32)] * 3,
        scratch_shapes=_gather_scratch(shards, dtypes),
        compiler_params=_params(),
    )(*shards, *_rope_parts())
    return res[:nw], res[nw:]


def _allreduce_vec(v):
    def body(v_ref, out_ref, pack, rbuf, send_sems, recv_sems):
        start, finish = _small_phases([v_ref], out_ref, pack, rbuf, send_sems, recv_sems)
        start()
        finish()

    return pl.pallas_call(
        body, name="allreduce_last", out_shape=jax.ShapeDtypeStruct((8, 1024), F32),
        scratch_shapes=[pltpu.VMEM((8, 1024), F32), pltpu.VMEM((NDEV, 8, 1024), F32),
                        pltpu.SemaphoreType.DMA((7,)), pltpu.SemaphoreType.DMA((7,))],
        compiler_params=_params(),
    )(v)


ADAM_STEPS = 4


def _adamw(items):
    n = len(items)

    def body(*refs):
        for i in range(n):
            g_ref, w_ref, m_ref, v_ref = refs[4 * i:4 * i + 4]
            go_ref, d_ref, mo_ref, vo_ref = refs[4 * n + 4 * i:4 * n + 4 * i + 4]
            g = g_ref[0].astype(F32)
            for j in range(1, g_ref.shape[0]):
                g = g + g_ref[j].astype(F32)
            go_ref[...] = g
            d_ref[...], mo_ref[...], vo_ref[...] = _adam_update(g, w_ref[...], m_ref[...], v_ref[...])

    in_specs, out_specs, out_shape, args = [], [], [], []
    for gparts, w, m, v in items:
        P, R, C = gparts.shape
        if R % (8 * ADAM_STEPS) == 0:
            tr = R // ADAM_STEPS
            row, gspec = pl.BlockSpec((tr, C), lambda i: (i, 0)), pl.BlockSpec((P, tr, C), lambda i: (0, i, 0))
        else:
            row, gspec = pl.BlockSpec((R, C), lambda i: (0, 0)), pl.BlockSpec((P, R, C), lambda i: (0, 0, 0))
        in_specs += [gspec, row, row, row]
        out_specs += [row] * 4
        out_shape += [jax.ShapeDtypeStruct((R, C), F32)] * 4
        args += [gparts, w, m, v]
    res = pl.pallas_call(
        body, name="adamw", grid=(ADAM_STEPS,), in_specs=in_specs, out_specs=out_specs, out_shape=out_shape,
        compiler_params=_params(1),
    )(*args)
    return [res[4 * i:4 * i + 4] for i in range(n)]


SMALL = ("norm_mix_g", "conv_b", "gate_b", "mlstm_norm_g", "norm_mlp_g", "norm_ple_g", "final_norm_g")


def _adam_update(g, w, m, v):
    c1 = 1.0 - ADAM_B1 ** ADAM_STEP
    c2 = 1.0 - ADAM_B2 ** ADAM_STEP
    m2 = ADAM_B1 * m + (1.0 - ADAM_B1) * g
    v2 = ADAM_B2 * v + (1.0 - ADAM_B2) * (g * g)
    return -ADAM_LR * ((m2 / c1) / (jnp.sqrt(v2 / c2) + ADAM_EPS) + ADAM_WD * w), m2, v2


def _adamw_small(total, first, ws, ms, vs):
    n = len(ws)

    def body(*refs):
        t_ref, f_ref = refs[:2]
        refs = refs[1:]
        outs = refs[1 + 3 * n:]
        for i in range(n):
            w_ref, m_ref, v_ref = refs[1 + i], refs[1 + n + i], refs[1 + 2 * n + i]
            g = (t_ref if i else f_ref)[8 * i:8 * i + 1, 0:w_ref.shape[1]]
            delta, m2, v2 = _adam_update(g, w_ref[...], m_ref[...], v_ref[...])
            for ref, val in zip(outs[4 * i:4 * i + 4], (g, delta, m2, v2)):
                ref[...] = val

    res = pl.pallas_call(
        body, name="adamw_small",
        out_shape=[jax.ShapeDtypeStruct(w.shape, F32) for w in ws for _ in range(4)],
        compiler_params=_params(),
    )(total, first, *ws, *ms, *vs)
    return [res[4 * i:4 * i + 4] for i in range(n)]


def kernel(x, p, norm_mix_g, w_in, conv_w, conv_b, gate_b, mlstm_norm_g, w_out, norm_mlp_g, w_up, w_down, norm_ple_g, w_ple_gate, w_ple, final_norm_g, loss_target, m_norm_mix_g, m_w_in, m_conv_w, m_conv_b, m_gate_b, m_mlstm_norm_g, m_w_out, m_norm_mlp_g, m_w_up, m_w_down, m_norm_ple_g, m_w_ple_gate, m_w_ple, m_final_norm_g, v_norm_mix_g, v_w_in, v_conv_w, v_conv_b, v_gate_b, v_mlstm_norm_g, v_w_out, v_norm_mlp_g, v_w_up, v_w_down, v_norm_ple_g, v_w_ple_gate, v_w_ple, v_final_norm_g):
    big_names = ("w_in", "conv_w", "w_out", "w_up", "w_down", "w_ple_gate", "w_ple")
    wts = dict(w_in=w_in, conv_w=conv_w, w_out=w_out, w_up=w_up, w_down=w_down, w_ple_gate=w_ple_gate, w_ple=w_ple)
    mom = dict(w_in=m_w_in, conv_w=m_conv_w, w_out=m_w_out, w_up=m_w_up, w_down=m_w_down, w_ple_gate=m_w_ple_gate,
               w_ple=m_w_ple)
    var = dict(w_in=v_w_in, conv_w=v_conv_w, w_out=v_w_out, w_up=v_w_up, w_down=v_w_down, w_ple_gate=v_w_ple_gate,
               w_ple=v_w_ple)
    sq = lambda a: a.reshape(a.shape[1:])
    fin = final_norm_g.reshape(1, D)
    dx, recv, total, first = _step(
        x[0], p[0, 0], loss_target[0], norm_mix_g, conv_b, jnp.pad(gate_b, ((0, 0), (0, 120))), mlstm_norm_g,
        norm_mlp_g, norm_ple_g, fin, {n: sq(wts[n]) for n in big_names})

    nrow = 8 * len(SMALL)
    me = _dev_index(*_place())
    conv_rows = total[nrow + 8:nrow + 40:8]
    recv["conv_w"] = lax.dynamic_slice_in_dim(conv_rows, me * 128, 128, axis=1).reshape(1, 4, 128)
    out = {}
    for n, res in zip(big_names, _adamw([(recv[n], sq(wts[n]), sq(mom[n]), sq(var[n])) for n in big_names])):
        out[n] = [t.reshape(wts[n].shape) for t in res]
    sw = dict(norm_mix_g=norm_mix_g, conv_b=conv_b, gate_b=gate_b, mlstm_norm_g=mlstm_norm_g, norm_mlp_g=norm_mlp_g,
              norm_ple_g=norm_ple_g, final_norm_g=fin)
    sm = dict(norm_mix_g=m_norm_mix_g, conv_b=m_conv_b, gate_b=m_gate_b, mlstm_norm_g=m_mlstm_norm_g,
              norm_mlp_g=m_norm_mlp_g, norm_ple_g=m_norm_ple_g, final_norm_g=m_final_norm_g.reshape(1, D))
    sv = dict(norm_mix_g=v_norm_mix_g, conv_b=v_conv_b, gate_b=v_gate_b, mlstm_norm_g=v_mlstm_norm_g,
              norm_mlp_g=v_norm_mlp_g, norm_ple_g=v_norm_ple_g, final_norm_g=v_final_norm_g.reshape(1, D))
    res = _adamw_small(total, first, [sw[n] for n in SMALL], [sm[n] for n in SMALL], [sv[n] for n in SMALL])
    for n, r in zip(SMALL, res):
        out[n] = [t.reshape(final_norm_g.shape) for t in r] if n == "final_norm_g" else list(r)
    order = ("norm_mix_g", "w_in", "conv_w", "conv_b", "gate_b", "mlstm_norm_g", "w_out", "norm_mlp_g", "w_up", "w_down",
             "norm_ple_g", "w_ple_gate", "w_ple", "final_norm_g")
    loss_all = total[nrow, 0]
    return (loss_all, dx[None], *[out[n][0] for n in order], *[out[n][1] for n in order],
            *[out[n][2] for n in order], *[out[n][3] for n in order])
```

```python
import functools
import math

import jax
import jax.numpy as jnp
from jax import lax
from jax.experimental import pallas as pl
from jax.experimental.pallas import tpu as pltpu

F32, BF16 = jnp.float32, jnp.bfloat16
S = 4096
D = 1024
AW = 512
MW = 512
DFF = 4096
PLE = 256
IN_W = 3592
PW = 3840
NDEV = 8
EPS = 1e-6
NEG = -1e30
LC = 128
TB = 256
ROPE_THETA = 500000.0
VMEM_LIMIT = 56 * 1024 * 1024
HI = lax.Precision.HIGHEST

ADAM_LR, ADAM_B1, ADAM_B2, ADAM_EPS, ADAM_WD, ADAM_STEP = 0.001, 0.9, 0.999, 1e-08, 0.01, 10


def _params(n_grid=0, **kw):
    sem = dict(dimension_semantics=("arbitrary",) * n_grid) if n_grid else {}
    return pltpu.CompilerParams(vmem_limit_bytes=VMEM_LIMIT, **sem, **kw)


def _cspec(shape):
    nd = len(shape)
    return pl.BlockSpec(shape, lambda *_: (0,) * nd, pipeline_mode=pl.Buffered(1))


def _dot(a, b):
    return jnp.dot(a, b, preferred_element_type=F32)


def _dot_nt(a, b):
    return lax.dot_general(a, b, (((1,), (1,)), ((), ())), preferred_element_type=F32)


def _dot_tn(a, b):
    return lax.dot_general(a, b, (((0,), (0,)), ((), ())), preferred_element_type=F32)


def _bf(x):
    return x.astype(BF16)


def _rms(x):
    rs = lax.rsqrt(jnp.mean(x * x, axis=-1, keepdims=True) + EPS)
    return x * rs, rs


def _rms_bwd(du, n, rs, g):
    dn = du * g
    return rs * (dn - n * jnp.mean(dn * n, axis=-1, keepdims=True))


def _sigmoid(x):
    return 1.0 / (1.0 + jnp.exp(-x))


ROPE_BLK = 512


def _rope_parts():
    def cs(n, step):
        j = lax.broadcasted_iota(jnp.int32, (n, 128), 1) % 64
        pos = (lax.broadcasted_iota(jnp.int32, (n, 128), 0) * step).astype(F32)
        ang = pos * jnp.power(ROPE_THETA, -(j % 8).astype(F32) / 8.0)
        return jnp.cos(ang), jnp.sin(ang)

    return (*cs(ROPE_BLK, 1), *cs(S // ROPE_BLK, ROPE_BLK))


def _rope_fill(co_ref, so_ref, cb_ref, sb_ref, rc_ref, ra_ref, rb_ref):
    j = lax.broadcasted_iota(jnp.int32, (ROPE_BLK, 128), 1) % 64
    co, so = co_ref[...], so_ref[...]
    for t in range(S // ROPE_BLK):
        cb, sb = cb_ref[t:t + 1, :], sb_ref[t:t + 1, :]
        cos, sin = cb * co - sb * so, sb * co + cb * so
        rows = slice(t * ROPE_BLK, (t + 1) * ROPE_BLK)
        rc_ref[rows, :] = jnp.where(j < 16, cos, 1.0)
        ra_ref[rows, :] = jnp.where(j < 8, -sin, 0.0)
        rb_ref[rows, :] = jnp.where((j >= 8) & (j < 16), sin, 0.0)


def _rope(blk, c, a, b):
    return blk * c + pltpu.roll(blk, 120, 1) * a + pltpu.roll(blk, 8, 1) * b


def _rope_bwd(d, c, a, b):
    return d * c + pltpu.roll(d * a, 8, 1) + pltpu.roll(d * b, 120, 1)


def _unrope(t, c, a, b):
    return jnp.concatenate([_bf(_rope_bwd(t[:, j * 128:(j + 1) * 128].astype(F32), c, a, b))
                            for j in range(t.shape[1] // 128)], axis=1)


MESH = pl.DeviceIdType.MESH
ANY = pl.BlockSpec(memory_space=pl.ANY)
VM = pl.BlockSpec(memory_space=pltpu.VMEM)
FLIPS = [(dx, dy, dc) for dx in (0, 1) for dy in (0, 1) for dc in (0, 1)][1:]


def _place():
    return lax.axis_index("x"), lax.axis_index("y"), lax.axis_index("c")


def _dev_index(px, py, pc):
    return 4 * px + 2 * py + pc


def _gather_phases(ins, outs, bufs, send_sems=None, recv_sems=None, local_sems=None):
    nw = len(ins)
    if nw == 0:
        return (lambda: None,) * 3
    x, y, c = _place()
    me, sib = (x, y, c), (x, y, 1 - c)
    chips = [(1 - x, y), (x, 1 - y), (1 - x, 1 - y)]

    def copy(w, k, block, to, from_buf=False):
        dst = outs[w].at[_dev_index(*block)]
        return pltpu.make_async_remote_copy(
            src_ref=bufs[w] if from_buf else dst, dst_ref=dst, send_sem=send_sems.at[w, k],
            recv_sem=recv_sems.at[w, k], device_id=to, device_id_type=MESH)

    def mine(w):
        return pltpu.make_async_copy(bufs[w], outs[w].at[_dev_index(*me)], local_sems.at[w])

    def first(w):
        return [copy(w, 0, me, sib, True)] + [copy(w, 1 + j, me, (*chip, c), True) for j, chip in enumerate(chips)]

    def passed(w):
        return [copy(w, 4 + j, (*chip, c), sib) for j, chip in enumerate(chips)]

    def start():
        for w in range(nw):
            bufs[w][...] = ins[w][...].astype(bufs[w].dtype)
        for w in range(nw):
            mine(w).start()
            for cp in first(w):
                cp.start()

    def forward():
        for j, chip in enumerate(chips):
            for w in range(nw):
                copy(w, 1 + j, (*chip, c), me).wait_recv()
                passed(w)[j].start()

    def finish():
        for w in range(nw):
            copy(w, 0, sib, me).wait_recv()
        for j, chip in enumerate(chips):
            for w in range(nw):
                copy(w, 4 + j, (*chip, 1 - c), me).wait_recv()
        for w in range(nw):
            for cp in first(w) + passed(w):
                cp.wait_send()
            mine(w).wait()

    return start, forward, finish


def _gather_scratch(shards, dtypes):
    nw = len(shards)
    if nw == 0:
        return []
    return ([pltpu.VMEM(s.shape, dt) for s, dt in zip(shards, dtypes)]
            + [pltpu.SemaphoreType.DMA((nw, 7)), pltpu.SemaphoreType.DMA((nw, 7)), pltpu.SemaphoreType.DMA((nw,))])


def _gather_shapes(shards, dtypes):
    return [jax.ShapeDtypeStruct((NDEV, *s.shape), dt) for s, dt in zip(shards, dtypes)]


def _scatter_phases(ins, outs, send_sems=None, recv_sems=None, local_sems=None):
    nw = len(ins)
    if nw == 0:
        return (lambda: None,) * 2
    x, y, c = _place()
    me = _dev_index(x, y, c)

    def copies():
        out = []
        for w in range(nw):
            out.append(pltpu.make_async_copy(ins[w].at[me], outs[w].at[me], local_sems.at[w]))
            for k, (dx, dy, dc) in enumerate(FLIPS):
                peer = ((x + dx) % 2, (y + dy) % 2, (c + dc) % 2)
                out.append(pltpu.make_async_remote_copy(
                    src_ref=ins[w].at[_dev_index(*peer)], dst_ref=outs[w].at[me], send_sem=send_sems.at[w, k],
                    recv_sem=recv_sems.at[w, k], device_id=peer, device_id_type=MESH))
        return out

    def start():
        for cp in copies():
            cp.start()

    def finish():
        for cp in copies():
            cp.wait()

    return start, finish


def _scatter_scratch(nw):
    if nw == 0:
        return []
    return [pltpu.SemaphoreType.DMA((nw, 7)), pltpu.SemaphoreType.DMA((nw, 7)), pltpu.SemaphoreType.DMA((nw,))]


TM = 512


def _join_w_in(wg):
    sw = IN_W // NDEV

    def body(wg_ref, w_ref):
        for j in range(NDEV):
            w_ref[:, sw * j:sw * (j + 1)] = wg_ref[j]
        w_ref[:, IN_W:PW] = jnp.zeros((D, PW - IN_W), BF16)

    return pl.pallas_call(body, name="join_w_in", out_shape=jax.ShapeDtypeStruct((D, PW), BF16),
                          compiler_params=_params())(wg)


def _in_proj(x, g1, w, rc, ra, rb, shards, dtypes):
    tm = TM
    nw = len(shards)
    nt = S // tm

    def body(*refs):
        x_ref, g_ref, w_ref, rc_ref, ra_ref, rb_ref = refs[:6]
        ins = refs[6:6 + nw]
        qkv_ref, mqk_ref, mv_ref, mo_ref, gt_ref, u_ref = refs[6 + nw:12 + nw]
        outs = refs[12 + nw:12 + 2 * nw]
        bufs = refs[12 + 2 * nw:12 + 3 * nw]
        ag_start, ag_forward, ag_finish = _gather_phases(ins, outs, bufs, *refs[12 + 3 * nw:])
        i = pl.program_id(0)
        pl.when(i == 0)(ag_start)
        pl.when(i == nt - 2)(ag_forward)
        n, _ = _rms(x_ref[...])
        u = _bf(n * g_ref[...])
        u_ref[...] = u
        c, a, b = rc_ref[...], ra_ref[...], rb_ref[...]
        for half in range(2):
            blk = _dot(u, w_ref[:, half * 512:(half + 1) * 512])
            for t in range(4):
                lo = half * 512 + t * 128
                qkv_ref[:, lo:lo + 128] = _rope(blk[:, t * 128:(t + 1) * 128], c, a, b)
        qkv_ref[:, 1024:1536] = _dot(u, w_ref[:, 1024:1536])
        mqk_ref[:, 0:512] = _dot(u, w_ref[:, 1536:2048])
        mqk_ref[:, 512:1024] = _dot(u, w_ref[:, 2048:2560])
        mv_ref[...] = _dot(u, w_ref[:, 2560:3072])
        mo_ref[...] = _dot(u, w_ref[:, 3072:3584])
        gt_ref[...] = _dot(u, w_ref[:, 3584:3712])
        pl.when(i == nt - 1)(ag_finish)

    row = lambda wd: pl.BlockSpec((tm, wd), lambda i: (i, 0))
    res = pl.pallas_call(
        body, name="in_proj", grid=(nt,),
        in_specs=[row(D), _cspec((1, D)), _cspec((D, PW)), row(128), row(128), row(128)] + [VM] * nw,
        out_specs=[row(1536), row(1024), row(512), row(512), row(128), row(D)] + [ANY] * nw,
        out_shape=[jax.ShapeDtypeStruct((S, 1536), F32), jax.ShapeDtypeStruct((S, 1024), F32),
                   jax.ShapeDtypeStruct((S, 512), F32), jax.ShapeDtypeStruct((S, 512), F32),
                   jax.ShapeDtypeStruct((S, 128), F32), jax.ShapeDtypeStruct((S, D), BF16)]
        + _gather_shapes(shards, dtypes),
        scratch_shapes=_gather_scratch(shards, dtypes),
        compiler_params=_params(1),
    )(x, g1, w, rc, ra, rb, *shards)
    return res[:6], res[6:]


DILATIONS = (16, 4, 1)


def _attn_valid(n):
    kd = lax.broadcasted_iota(jnp.int32, (128, 256), 1) - lax.broadcasted_iota(jnp.int32, (128, 256), 0)
    off = jnp.where(n == 0, 0, 128)
    return (kd <= off) & (kd >= off - 128)


def _attn_rows(d, r, n):
    if d == 1:
        q0 = pl.multiple_of(n * 128, 128)
        k0 = pl.multiple_of(jnp.maximum(n - 1, 0) * 128, 128)
        return pl.ds(q0, 128), pl.ds(k0, 256), _attn_valid(n)
    q0 = r + n * 128 * d
    k0 = r + jnp.maximum(n - 1, 0) * 128 * d
    return pl.ds(q0, 128, stride=d), pl.ds(k0, 256, stride=d), _attn_valid(n)


ATTN_GROUP = 4
ATTN_ITERS = S // 128 // ATTN_GROUP


def _attn_group(d, i):
    nb = S // (128 * d)
    if nb == 2:
        qi = lax.broadcasted_iota(jnp.int32, (256, 256), 0) - lax.broadcasted_iota(jnp.int32, (256, 256), 1)
        whole = [pl.ds((ATTN_GROUP // 2) * i + u, 256, stride=d) for u in range(ATTN_GROUP // 2)]
        return [(rows, rows, (qi >= 0) & (qi <= 128)) for rows in whole]
    if d == 1:
        return [_attn_rows(1, 0, i + ATTN_ITERS * u) for u in range(ATTN_GROUP)]
    return [_attn_rows(d, (i // nb) * ATTN_GROUP + u, i % nb) for u in range(ATTN_GROUP)]


def _head0(shape):
    return lax.broadcasted_iota(jnp.int32, shape, 1) < 64


def _stack_heads(t):
    h0 = _head0(t.shape)
    tb = _bf(t)
    zero = jnp.zeros_like(tb)
    return jnp.concatenate([jnp.where(h0, tb, zero), jnp.where(h0, zero, tb)], axis=0)


def _attn_fwd(qkv, shards, dtypes):
    nw = len(shards)

    def body(*refs):
        q_ref, k_ref, v_ref = refs[:3]
        ins = refs[3:3 + nw]
        o_ref, lse0_ref, lse1_ref = refs[3 + nw:6 + nw]
        outs = refs[6 + nw:6 + 2 * nw]
        m0, m1, l0, l1, acc = refs[6 + 2 * nw:11 + 2 * nw]
        bufs = refs[11 + 2 * nw:11 + 3 * nw]
        ag_start, ag_forward, ag_finish = _gather_phases(ins, outs, bufs, *refs[11 + 3 * nw:])
        hp = pl.program_id(0)
        pl.when(hp == 0)(ag_start)
        pl.when(hp == 3)(ag_forward)
        stats = (m0, m1, l0, l1, acc)

        def update(blocks, first):
            loaded = [([q_ref[rq, :], k_ref[rk, :], v_ref[rk, :]], None if first else [ref[rq, :] for ref in stats])
                      for rq, rk, _ in blocks]
            results = []
            for ((q, k, v), prev), (_, _, valid) in zip(loaded, blocks):
                nq = q.shape[0]
                both = lambda a, b: jnp.concatenate([a, b], axis=0)
                kb, vb = _bf(k), _bf(v)
                s = jnp.where(both(valid, valid), _dot_nt(_stack_heads(q * 0.125), kb), NEG)
                mc = jnp.max(s, axis=-1, keepdims=True)
                m2 = jnp.broadcast_to(mc, (2 * nq, 128)) if first else jnp.maximum(both(prev[0], prev[1]), mc)
                p = jnp.exp(s - jnp.tile(m2, (1, 2)))
                l2 = jnp.sum(p, axis=-1, keepdims=True)
                acc2 = _dot(_bf(p), vb)
                if first:
                    l2 = jnp.broadcast_to(l2, (2 * nq, 128))
                else:
                    alpha = jnp.exp(both(prev[0], prev[1]) - m2)
                    l2, acc2 = alpha * both(prev[2], prev[3]) + l2, alpha * both(prev[4], prev[4]) + acc2
                results.append((m2[0:nq], m2[nq:2 * nq], l2[0:nq], l2[nq:2 * nq],
                                jnp.where(_head0((nq, 128)), acc2[0:nq], acc2[nq:2 * nq])))
            for (rq, _, _), res in zip(blocks, results):
                for ref, val in zip(stats, res):
                    ref[rq, :] = val

        for d in DILATIONS:
            def step(i, carry, d=d):
                update(_attn_group(d, i), d == DILATIONS[0])
                return carry

            lax.fori_loop(0, ATTN_ITERS, step, 0)

        def fin(t, carry):
            rows = pl.ds(pl.multiple_of(t * 256, 256), 256)
            h0 = lax.broadcasted_iota(jnp.int32, (256, 128), 1) < 64
            la, lb = l0[rows, :], l1[rows, :]
            o_ref[rows, :] = acc[rows, :] / jnp.where(h0, la, lb)
            lse0_ref[rows, :] = m0[rows, :] + jnp.log(la)
            lse1_ref[rows, :] = m1[rows, :] + jnp.log(lb)
            return carry

        lax.fori_loop(0, S // 256, fin, 0)
        pl.when(hp == 3)(ag_finish)

    col = lambda off: pl.BlockSpec((S, 128), lambda h, off=off: (0, off + h))
    res = pl.pallas_call(
        body, name="attn_fwd", grid=(4,),
        in_specs=[col(0), col(4), col(8)] + [VM] * nw,
        out_specs=[col(0), col(0), col(0)] + [ANY] * nw,
        out_shape=[jax.ShapeDtypeStruct((S, AW), F32)] * 3 + _gather_shapes(shards, dtypes),
        scratch_shapes=[pltpu.VMEM((S, 128), F32)] * 5 + _gather_scratch(shards, dtypes),
        compiler_params=_params(1),
    )(qkv, qkv, qkv, *shards)
    return res[0], (res[1], res[2]), res[3:]


def _attn_bwd(qkv, o, lse, do, parts):
    nw = len(parts)

    def body(*refs):
        q_ref, k_ref, v_ref, o_ref, L0, L1, do_ref = refs[:7]
        ins = refs[7:7 + nw]
        dq_out, dk_out, dv_out = refs[7 + nw:10 + nw]
        outs = refs[10 + nw:10 + 2 * nw]
        D0, D1, dq_ref, dk_ref, dv_ref = refs[10 + 2 * nw:15 + 2 * nw]
        rs_start, rs_finish = _scatter_phases(ins, outs, *refs[15 + 2 * nw:])
        hp = pl.program_id(0)
        pl.when(hp == 0)(rs_start)

        def pre(t, carry):
            rows = pl.ds(pl.multiple_of(t * 256, 256), 256)
            h0 = lax.broadcasted_iota(jnp.int32, (256, 128), 1) < 64
            dd = do_ref[rows, :] * o_ref[rows, :]
            shp = (256, 128)
            D0[rows, :] = jnp.broadcast_to(jnp.sum(jnp.where(h0, dd, 0.0), axis=-1, keepdims=True), shp)
            D1[rows, :] = jnp.broadcast_to(jnp.sum(jnp.where(h0, 0.0, dd), axis=-1, keepdims=True), shp)
            return carry

        lax.fori_loop(0, S // 256, pre, 0)

        def update(blocks, first):
            loaded = [([q_ref[rq, :], k_ref[rk, :], v_ref[rk, :], do_ref[rq, :]],
                       [L0[rq, :], L1[rq, :], D0[rq, :], D1[rq, :]],
                       [0.0] * 3 if first else [dq_ref[rq, :], dk_ref[rk, :], dv_ref[rk, :]]) for rq, rk, _ in blocks]
            results = []
            for ((q, k, v, dout), (l0v, l1v, d0v, d1v), (dq, dk, dv)), (_, _, valid) in zip(loaded, blocks):
                nq = q.shape[0]
                valid = jnp.concatenate([valid, valid], axis=0)
                q2, do2, kb, vb = _stack_heads(q), _stack_heads(dout), _bf(k), _bf(v)
                cat = lambda a, b: jnp.tile(jnp.concatenate([a, b], axis=0), (1, 2))
                s = jnp.where(valid, _dot_nt(_stack_heads(q * 0.125), kb), NEG)
                p = jnp.exp(s - cat(l0v, l1v))
                ds = _bf(p * (_dot_nt(do2, vb) - cat(d0v, d1v)) * 0.125)
                dq2 = _dot(ds, kb)
                results.append((dq + jnp.where(_head0((nq, 128)), dq2[0:nq], dq2[nq:2 * nq]),
                                dk + _dot_tn(ds, q2), dv + _dot_tn(_bf(p), do2)))
            for (rq, rk, _), (dq, dk, dv) in zip(blocks, results):
                dq_ref[rq, :] = dq
                dk_ref[rk, :] = dk
                dv_ref[rk, :] = dv

        assert S // (128 * DILATIONS[0]) == 2
        for d in DILATIONS:
            def step(i, carry, d=d):
                update(_attn_group(d, i), d == DILATIONS[0])
                return carry

            lax.fori_loop(0, ATTN_ITERS, step, 0)

        def fin(t, carry):
            rows = pl.ds(pl.multiple_of(t * 256, 256), 256)
            for src, dst in ((dq_ref, dq_out), (dk_ref, dk_out), (dv_ref, dv_out)):
                dst[rows, :] = _bf(src[rows, :])
            return carry

        lax.fori_loop(0, S // 256, fin, 0)
        pl.when(hp == 3)(rs_finish)

    col = lambda off: pl.BlockSpec((S, 128), lambda h, off=off: (0, off + h))
    res = pl.pallas_call(
        body, name="attn_bwd", grid=(4,),
        in_specs=[col(0), col(4), col(8), col(0), col(0), col(0), col(0)] + [ANY] * nw,
        out_specs=[col(0), col(0), col(0)] + [ANY] * nw,
        out_shape=[jax.ShapeDtypeStruct((S, AW), BF16)] * 3 + [jax.ShapeDtypeStruct(a.shape, a.dtype) for a in parts],
        scratch_shapes=[pltpu.VMEM((S, 128), F32)] * 5 + _scatter_scratch(nw),
        compiler_params=_params(1),
    )(qkv, qkv, qkv, o, lse[0], lse[1], do, *parts)
    return res[0], res[1], res[2], res[3:]


def _logsig(x):
    return jnp.minimum(x, 0.0) - jnp.log1p(jnp.exp(-jnp.abs(x)))


def _conv_taps(xp, n):
    return [xp[8:] if j == 3 else pltpu.roll(xp, 3 - j, 0)[8:] for j in range(4)]


def _conv_silu(xp, w_ref, b_ref, n):
    taps = _conv_taps(xp, n)
    c = b_ref[...] + sum(w_ref[j:j + 1, :] * taps[j] for j in range(4))
    sg = _sigmoid(c)
    return c, sg, taps


def _chunk_gates(G):
    assert LC == 128
    r = lax.broadcasted_iota(jnp.int32, (LC, LC), 0)
    c = lax.broadcasted_iota(jnp.int32, (LC, LC), 1)
    tril = (c <= r).astype(F32)
    triu = (c >= r).astype(F32)
    b_col = jnp.dot(tril, _logsig(G), preferred_element_type=F32, precision=HI)
    return b_col, b_col.T, G.T, tril, triu


def _colpick(X, lane):
    li = lax.broadcasted_iota(jnp.int32, X.shape, 1)
    return jnp.sum(jnp.where(li == lane, X, 0.0), axis=1, keepdims=True)


def _rowpick(XT, row):
    ri = lax.broadcasted_iota(jnp.int32, XT.shape, 0)
    return jnp.sum(jnp.where(ri == row, XT, 0.0), axis=0, keepdims=True)


def _mlstm_head(qh, kh, vh, G, b_col, b_row, g_row, h, Ch, nh, m_prev):
    bt = _colpick(b_col, 4 + h)
    i_col = _colpick(G, h)
    bs = _rowpick(b_row, 4 + h)
    i_row = _rowpick(g_row, h)
    r = lax.broadcasted_iota(jnp.int32, (LC, LC), 0)
    c = lax.broadcasted_iota(jnp.int32, (LC, LC), 1)
    log_d = jnp.where(c <= r, bt - bs + i_row, NEG)
    log_inter = bt + m_prev
    m_t = jnp.maximum(log_inter, jnp.max(log_d, axis=1, keepdims=True))
    Dm = jnp.exp(log_d - m_t)
    g = jnp.exp(log_inter - m_t)
    qb, kb, vb = _bf(qh), _bf(kh), _bf(vh)
    Am = _dot_nt(qb, kb) * Dm
    qC = _dot(qb, _bf(Ch))
    num = g * qC + _dot(_bf(Am), vb)
    qn = jnp.sum(qh * nh, axis=1, keepdims=True)
    den = g * qn + jnp.sum(Am, axis=1, keepdims=True)
    floor = jnp.exp(-m_t)
    dd = jnp.maximum(jnp.abs(den), floor)
    inv_dd = 1.0 / dd
    hh = num * inv_dd
    lane = lax.broadcasted_iota(jnp.int32, (1, LC), 1)
    blast = jnp.sum(jnp.where(lane == LC - 1, bs, 0.0), axis=1, keepdims=True)
    log_s = blast - bt + i_col
    m_new = jnp.maximum(blast + m_prev, jnp.max(log_s, axis=0, keepdims=True))
    decay = jnp.exp(blast + m_prev - m_new)
    ws = jnp.exp(log_s - m_new)
    kw = kh * ws
    C_new = decay * Ch + _dot_tn(_bf(kw), vb)
    n_new = decay * nh + jnp.sum(kw, axis=0, keepdims=True)
    return dict(Dm=Dm, g=g, Am=Am, qC=qC, qn=qn, den=den, floor=floor, inv_dd=inv_dd, h=hh, decay=decay, ws=ws, kw=kw,
                C_new=C_new, n_new=n_new, m_new=m_new, qb=qb, kb=kb, vb=vb)


def _head_out(hh, mo_h, gn_h):
    r = lax.rsqrt(jnp.mean(hh * hh, axis=-1, keepdims=True) + EPS)
    hn = hh * r
    sg = _sigmoid(mo_h)
    return sg * (hn * gn_h), hn, r, sg


def _mlstm_fwd(mqk, mv, mo, gates, conv_w, conv_b, gate_b, gn, shards, dtypes):
    nblk = S // TB
    ncb = TB // LC
    nw = len(shards)

    def body(*refs):
        x_ref, v_ref, o_ref, g_ref, w_ref, b_ref, gb_ref, gn_ref = refs[:8]
        ins = refs[8:8 + nw]
        out_ref, cs_ref, ns_ref, ms_ref = refs[8 + nw:12 + nw]
        outs = refs[12 + nw:12 + 2 * nw]
        tail, Cst, nst, mst, qs, ks = refs[12 + 2 * nw:18 + 2 * nw]
        bufs = refs[18 + 2 * nw:18 + 3 * nw]
        ag_start, ag_forward, ag_finish = _gather_phases(ins, outs, bufs, *refs[18 + 3 * nw:])
        i = pl.program_id(0)
        pl.when(i == 0)(ag_start)
        pl.when(i == nblk // 2)(ag_forward)

        @pl.when(i == 0)
        def _():
            tail[...] = jnp.zeros_like(tail)
            Cst[...] = jnp.zeros_like(Cst)
            nst[...] = jnp.zeros_like(nst)
            mst[...] = jnp.zeros_like(mst)

        x = x_ref[...]
        xp = jnp.concatenate([tail[...], x], axis=0)
        tail[...] = x[TB - 8:TB, :]
        c, sg, _ = _conv_silu(xp, w_ref, b_ref, TB)
        y = c * sg
        qs[...] = y[:, 0:MW]
        ks[...] = y[:, MW:2 * MW] * (1.0 / math.sqrt(128.0))

        for cc in range(ncb):
            rows = slice(cc * LC, (cc + 1) * LC)
            G = g_ref[rows, :] + gb_ref[...]
            b_col, b_row, g_row, _, _ = _chunk_gates(G)
            cs_ref[cc] = Cst[...]
            ns_ref[cc] = nst[...]
            ms_ref[cc] = mst[...]
            for h in range(4):
                ln = slice(h * 128, (h + 1) * 128)
                m_prev = jnp.max(mst[0:1, ln], axis=1, keepdims=True)
                f = _mlstm_head(qs[rows, ln], ks[rows, ln], v_ref[rows, ln], G, b_col, b_row, g_row, h,
                                Cst[:, ln], nst[0:1, ln], m_prev)
                out, _, _, _ = _head_out(f["h"], o_ref[rows, ln], gn_ref[:, ln])
                out_ref[rows, ln] = out
                Cst[:, ln] = f["C_new"]
                nst[0:1, ln] = f["n_new"]
                mst[0:1, ln] = jnp.broadcast_to(f["m_new"], (1, 128))
        pl.when(i == nblk - 1)(ag_finish)

    row = lambda wd: pl.BlockSpec((TB, wd), lambda i: (i, 0))
    res = pl.pallas_call(
        body, name="mlstm_fwd", grid=(nblk,),
        in_specs=[row(1024), row(MW), row(MW), row(128), _cspec((4, 1024)), _cspec((1, 1024)), _cspec((1, 128)),
                  _cspec((1, MW))] + [VM] * nw,
        out_specs=[row(MW), pl.BlockSpec((ncb, 128, MW), lambda i: (i, 0, 0)),
                   pl.BlockSpec((ncb, 8, MW), lambda i: (i, 0, 0)), pl.BlockSpec((ncb, 8, MW), lambda i: (i, 0, 0))]
        + [ANY] * nw,
        out_shape=[jax.ShapeDtypeStruct((S, MW), F32), jax.ShapeDtypeStruct((S // LC, 128, MW), F32),
                   jax.ShapeDtypeStruct((S // LC, 8, MW), F32), jax.ShapeDtypeStruct((S // LC, 8, MW), F32)]
        + _gather_shapes(shards, dtypes),
        scratch_shapes=[pltpu.VMEM((8, 1024), F32), pltpu.VMEM((128, MW), F32), pltpu.VMEM((8, MW), F32),
                        pltpu.VMEM((8, MW), F32), pltpu.VMEM((TB, MW), F32), pltpu.VMEM((TB, MW), F32)]
        + _gather_scratch(shards, dtypes),
        compiler_params=_params(1),
    )(mqk, mv, mo, gates, conv_w, conv_b, gate_b, gn, *shards)
    return res[0], res[1], res[2], res[3], res[4:]


DM_V, DM_O, DM_G, DM_W = 1024, 1536, 2048, PW - 3 * AW


def _mlstm_bwd(mqk, mv, mo, gates, conv_w, conv_b, gate_b, gn, cs, ns, ms, dout, parts):
    nblk = S // TB
    ncb = TB // LC
    kscale = 1.0 / math.sqrt(128.0)
    nw = len(parts)

    def body(*refs):
        x_ref, xprev_ref, v_ref, o_ref, g_ref, w_ref, b_ref, gb_ref, gn_ref, cs_ref, ns_ref, ms_ref, do_ref = refs[:13]
        ins = refs[13:13 + nw]
        dm_ref, dw_ref, db_ref, dgn_ref, dgb_ref = refs[13 + nw:18 + nw]
        outs = refs[18 + nw:18 + 2 * nw]
        dCst, dnst, dyhead, qs, ks, dqk = refs[18 + 2 * nw:24 + 2 * nw]
        rs_start, rs_finish = _scatter_phases(ins, outs, *refs[24 + 2 * nw:])
        i = pl.program_id(0)
        blk = nblk - 1 - i
        pl.when(i == 0)(rs_start)

        @pl.when(i == 0)
        def _():
            dCst[...] = jnp.zeros_like(dCst)
            dnst[...] = jnp.zeros_like(dnst)
            dyhead[...] = jnp.zeros_like(dyhead)
            dw_ref[...] = jnp.zeros_like(dw_ref)
            db_ref[...] = jnp.zeros_like(db_ref)
            dgn_ref[...] = jnp.zeros_like(dgn_ref)
            dgb_ref[...] = jnp.zeros_like(dgb_ref)

        x = x_ref[...]
        xprev = jnp.where(blk == 0, 0.0, xprev_ref[...])
        xp = jnp.concatenate([xprev, x], axis=0)
        c, sg, taps = _conv_silu(xp, w_ref, b_ref, TB)
        y = c * sg
        qs[...] = y[:, 0:MW]
        ks[...] = y[:, MW:2 * MW] * kscale
        lane128 = lax.broadcasted_iota(jnp.int32, (LC, 128), 1)
        rowi = lax.broadcasted_iota(jnp.int32, (LC, 1), 0)
        ones = jnp.ones((LC, 128), F32)

        for cc in reversed(range(ncb)):
            rows = slice(cc * LC, (cc + 1) * LC)
            G = g_ref[rows, :] + gb_ref[...]
            b_col, b_row, g_row, _, triu = _chunk_gates(G)
            dB = jnp.zeros((LC, 128), F32)
            dI = jnp.zeros((LC, 128), F32)
            for h in range(4):
                ln = slice(h * 128, (h + 1) * 128)
                Ch = cs_ref[cc, :, ln]
                nh = ns_ref[cc, 0:1, ln]
                m_prev = jnp.max(ms_ref[cc, 0:1, ln], axis=1, keepdims=True)
                qh, kh, vh = qs[rows, ln], ks[rows, ln], v_ref[rows, ln]
                f = _mlstm_head(qh, kh, vh, G, b_col, b_row, g_row, h, Ch, nh, m_prev)
                hh, inv_dd, den, g, Am, Dm = f["h"], f["inv_dd"], f["den"], f["g"], f["Am"], f["Dm"]
                qb, kb, vb = f["qb"], f["kb"], f["vb"]
                gn_h = gn_ref[:, ln]
                _, hn, r, sgo = _head_out(hh, o_ref[rows, ln], gn_h)
                do = do_ref[rows, ln]
                hm = hn * gn_h
                dm_ref[rows, DM_O + h * 128:DM_O + (h + 1) * 128] = _bf(do * hm * sgo * (1.0 - sgo))
                dhm = do * sgo
                dgn_ref[:, ln] = dgn_ref[:, ln] + jnp.sum(dhm * hn, axis=0, keepdims=True)
                dhn = dhm * gn_h
                dh = r * (dhn - hn * jnp.mean(dhn * hn, axis=-1, keepdims=True))
                dnum = dh * inv_dd
                ddd = -jnp.sum(dh * hh, axis=1, keepdims=True) * inv_dd
                dden = jnp.where(jnp.abs(den) >= f["floor"], ddd * jnp.sign(den), 0.0)
                dnb = _bf(dnum)
                dA = _dot_nt(dnb, vb) + dden
                dv = _dot_tn(_bf(Am), dnb)
                gd = _bf(g * dnum)
                gq = g * dden
                dq = _dot_nt(gd, _bf(Ch)) + gq * nh
                dCn = dCst[:, ln]
                dnn = dnst[0:1, ln]
                dC = f["decay"] * dCn + _dot_tn(qb, gd)
                dn = f["decay"] * dnn + jnp.sum(gq * qh, axis=0, keepdims=True)
                dg = jnp.sum(dnum * f["qC"], axis=1, keepdims=True) + dden * f["qn"]
                dS = _bf(dA * Dm)
                dq = dq + _dot(dS, kb)
                dk = _dot_tn(dS, qb)
                Gm = dA * Am
                gam = dg * g
                dCb = _bf(dCn)
                E = _dot_nt(vb, dCb) + dnn
                ws = f["ws"]
                dk = dk + ws * E
                om = jnp.sum(E * kh, axis=1, keepdims=True) * ws
                dv = dv + _dot(_bf(f["kw"]), dCb)
                ddecay = (jnp.sum(jnp.sum(dCn * Ch, axis=1, keepdims=True), axis=0, keepdims=True)
                          + jnp.sum(dnn * nh, axis=1, keepdims=True))
                delta = ddecay * f["decay"]
                rows_g = jnp.sum(Gm, axis=1, keepdims=True)
                cols_g = jnp.broadcast_to(jnp.sum(Gm, axis=0, keepdims=True), (LC, 128)).T
                last = jnp.where(rowi == LC - 1, jnp.sum(om, axis=0, keepdims=True) + delta, 0.0)
                db = rows_g + gam - om + last - cols_g
                di = cols_g + om
                dB = jnp.where(lane128 == 4 + h, db, dB)
                dI = jnp.where(lane128 == h, di, dI)
                dCst[:, ln] = dC
                dnst[0:1, ln] = dn
                dqk[rows, ln] = dq
                dqk[rows, MW + h * 128:MW + (h + 1) * 128] = dk * kscale
                dm_ref[rows, DM_V + h * 128:DM_V + (h + 1) * 128] = _bf(dv)
            dlogf = jnp.dot(triu, dB, preferred_element_type=F32, precision=HI)
            dG = dI + dlogf * _sigmoid(-G)
            dG = jnp.where(lane128 < 8, dG, 0.0)
            dm_ref[rows, DM_G:DM_G + 128] = _bf(dG)
            dm_ref[rows, DM_G + 128:DM_W] = jnp.zeros((LC, DM_W - DM_G - 128), BF16)
            dgb_ref[...] = dgb_ref[...] + jnp.sum(dG, axis=0, keepdims=True)

        dy = dqk[...] * (sg * (1.0 + c * (1.0 - sg)))
        db_ref[...] = db_ref[...] + jnp.sum(dy, axis=0, keepdims=True)
        for j in range(4):
            dw_ref[j:j + 1, :] = dw_ref[j:j + 1, :] + jnp.sum(dy * taps[j], axis=0, keepdims=True)
        dyp = jnp.concatenate([dy, dyhead[...]], axis=0)
        dx = w_ref[3:4, :] * dy
        for j in range(3):
            dx = dx + w_ref[j:j + 1, :] * pltpu.roll(dyp, TB + 8 - (3 - j), 0)[0:TB]
        dm_ref[:, 0:DM_V] = _bf(dx)
        dyhead[...] = dy[0:8, :]
        pl.when(i == nblk - 1)(rs_finish)

    rrow = lambda wd: pl.BlockSpec((TB, wd), lambda i: (nblk - 1 - i, 0))
    st = lambda r: pl.BlockSpec((ncb, r, MW), lambda i: (nblk - 1 - i, 0, 0))
    prev8 = pl.BlockSpec((8, 1024), lambda i: (jnp.maximum((nblk - 1 - i) * (TB // 8) - 1, 0), 0))
    res = pl.pallas_call(
        body, name="mlstm_bwd", grid=(nblk,),
        in_specs=[rrow(1024), prev8, rrow(MW), rrow(MW), rrow(128), _cspec((4, 1024)), _cspec((1, 1024)),
                  _cspec((1, 128)), _cspec((1, MW)), st(128), st(8), st(8), rrow(MW)] + [ANY] * nw,
        out_specs=[rrow(DM_W),
                   pl.BlockSpec((4, 1024), lambda i: (0, 0)), pl.BlockSpec((1, 1024), lambda i: (0, 0)),
                   pl.BlockSpec((1, MW), lambda i: (0, 0)), pl.BlockSpec((1, 128), lambda i: (0, 0))] + [ANY] * nw,
        out_shape=[jax.ShapeDtypeStruct((S, DM_W), BF16),
                   jax.ShapeDtypeStruct((4, 1024), F32), jax.ShapeDtypeStruct((1, 1024), F32),
                   jax.ShapeDtypeStruct((1, MW), F32), jax.ShapeDtypeStruct((1, 128), F32)]
        + [jax.ShapeDtypeStruct(a.shape, a.dtype) for a in parts],
        scratch_shapes=[pltpu.VMEM((128, MW), F32), pltpu.VMEM((8, MW), F32), pltpu.VMEM((8, 1024), F32),
                        pltpu.VMEM((TB, MW), F32), pltpu.VMEM((TB, MW), F32), pltpu.VMEM((TB, 1024), F32)]
        + _scatter_scratch(nw),
        compiler_params=_params(1),
    )(mqk, mqk, mv, mo, gates, conv_w, conv_b, gate_b, gn, cs, ns, ms, dout, *parts)
    return res[:5], res[5:]


def _out_proj(x, attn, ml, w, g):
    tm = TM

    def body(x_ref, a_ref, m_ref, w_ref, g_ref, h_ref, u_ref):
        h1 = x_ref[...] + _dot(_bf(a_ref[...]), w_ref[0:AW, :]) + _dot(_bf(m_ref[...]), w_ref[AW:D, :])
        h_ref[...] = h1
        n, _ = _rms(h1)
        u_ref[...] = _bf(n * g_ref[...])

    row = lambda wd: pl.BlockSpec((tm, wd), lambda i: (i, 0))
    return pl.pallas_call(
        body, name="out_proj", grid=(S // tm,),
        in_specs=[row(D), row(AW), row(MW), _cspec((D, D)), _cspec((1, D))],
        out_specs=[row(D), row(D)],
        out_shape=[jax.ShapeDtypeStruct((S, D), F32), jax.ShapeDtypeStruct((S, D), BF16)],
        compiler_params=_params(1),
    )(x, attn, ml, w, g)


HALF = DFF // NDEV // 2


def _mlp_fwd(h1, u2, w_up, w_down_a, w_down_b):
    tm = TM

    def body(h_ref, u_ref, wu_ref, wa_ref, wb_ref, a_ref, o_ref):
        u = u_ref[...]
        acc = h_ref[...]
        for c in range(NDEV):
            cols = slice(c * 512, (c + 1) * 512)
            a = _dot(u, wu_ref[c])
            a_ref[:, cols] = _bf(a)
            r = jnp.maximum(a, 0.0)
            r = _bf(r * r)
            acc = acc + _dot(r[:, 0:HALF], wa_ref[c]) + _dot(r[:, HALF:2 * HALF], wb_ref[c])
        o_ref[...] = acc

    row = lambda wd: pl.BlockSpec((tm, wd), lambda i: (i, 0))
    return pl.pallas_call(
        body, name="mlp_fwd", grid=(S // tm,),
        in_specs=[row(D), row(D), _cspec((NDEV, D, DFF // NDEV)), _cspec((NDEV, HALF, D)), _cspec((NDEV, HALF, D))],
        out_specs=[row(DFF), row(D)],
        out_shape=[jax.ShapeDtypeStruct((S, DFF), BF16), jax.ShapeDtypeStruct((S, D), F32)],
        compiler_params=_params(1),
    )(h1, u2, w_up, w_down_a, w_down_b)


def _ple_loss(h2, p, target, w_pg, w_ple, g_ple, g_fin):
    tm = TM

    def body(h_ref, p_ref, t_ref, wg_ref, wp_ref, gp_ref, gf_ref,
             dh_ref, dwg_ref, dwp_ref, dgp_ref, dgf_ref, loss_ref, acc_g, acc_p):
        i = pl.program_id(0)

        @pl.when(i == 0)
        def _():
            acc_g[...] = jnp.zeros_like(acc_g)
            acc_p[...] = jnp.zeros_like(acc_p)
            dgp_ref[...] = jnp.zeros_like(dgp_ref)
            dgf_ref[...] = jnp.zeros_like(dgf_ref)
            loss_ref[...] = jnp.zeros_like(loss_ref)

        h2v = h_ref[...]
        n2, rs2 = _rms(h2v)
        u3 = _bf(n2 * gp_ref[...])
        gt = _sigmoid(_dot(u3, wg_ref[...]))
        pb = _bf(p_ref[...])
        e = jnp.concatenate([_dot(pb, wp_ref[j]) for j in range(NDEV)], axis=1)
        h3 = h2v + gt * e
        n3, rs3 = _rms(h3)
        err = n3 * gf_ref[...] - t_ref[...]
        loss_ref[...] = loss_ref[...] + 0.5 / D * jnp.sum(jnp.sum(err * err, axis=1, keepdims=True), axis=0, keepdims=True)
        dy = err * (1.0 / D)
        dgf_ref[...] = dgf_ref[...] + jnp.sum(dy * n3, axis=0, keepdims=True)
        dh3 = _rms_bwd(dy, n3, rs3, gf_ref[...])
        de = _bf(dh3 * gt)
        dz = _bf(dh3 * e * gt * (1.0 - gt))
        acc_p[...] = acc_p[...] + _dot_tn(pb, de)
        acc_g[...] = acc_g[...] + _dot_tn(u3, dz)
        du3 = _dot_nt(dz, wg_ref[...])
        dgp_ref[...] = dgp_ref[...] + jnp.sum(du3 * n2, axis=0, keepdims=True)
        dh_ref[...] = dh3 + _rms_bwd(du3, n2, rs2, gp_ref[...])

        @pl.when(i == S // tm - 1)
        def _():
            dwg_ref[...] = _bf(acc_g[...])
            for j in range(NDEV):
                dwp_ref[j] = _bf(acc_p[:, j * 128:(j + 1) * 128])

    row = lambda wd: pl.BlockSpec((tm, wd), lambda i: (i, 0))
    whole = lambda shp: pl.BlockSpec(shp, lambda i: (0,) * len(shp))
    return pl.pallas_call(
        body, name="ple_loss", grid=(S // tm,),
        in_specs=[row(D), row(PLE), row(D), _cspec((D, D)), _cspec((NDEV, PLE, 128)), _cspec((1, D)), _cspec((1, D))],
        out_specs=[row(D), whole((D, D)), whole((NDEV, PLE, 128)), whole((1, D)), whole((1, D)), whole((1, 1))],
        out_shape=[jax.ShapeDtypeStruct((S, D), F32), jax.ShapeDtypeStruct((D, D), BF16),
                   jax.ShapeDtypeStruct((NDEV, PLE, 128), BF16), jax.ShapeDtypeStruct((1, D), F32),
                   jax.ShapeDtypeStruct((1, D), F32), jax.ShapeDtypeStruct((1, 1), F32)],
        scratch_shapes=[pltpu.VMEM((D, D), F32), pltpu.VMEM((PLE, D), F32)],
        compiler_params=_params(1),
    )(h2, p, target, w_pg, w_ple, g_ple, g_fin)


def _mlp_bwd(dh2, a, h1, g, w_up, w_down_a, w_down_b):
    tm = TM

    def body(d_ref, a_ref, h_ref, g_ref, wu_ref, wa_ref, wb_ref, da_ref, dh1_ref, dg_ref):
        @pl.when(pl.program_id(0) == 0)
        def _():
            dg_ref[...] = jnp.zeros_like(dg_ref)

        dh2v = d_ref[...]
        db = _bf(dh2v)
        du = jnp.zeros((tm, D), F32)
        for c in range(NDEV):
            cols = slice(c * 512, (c + 1) * 512)
            dr = jnp.concatenate([_dot_nt(db, wa_ref[c]), _dot_nt(db, wb_ref[c])], axis=1)
            da = _bf(dr * (2.0 * jnp.maximum(a_ref[:, cols], 0.0)))
            da_ref[:, cols] = da
            du = du + _dot_nt(da, wu_ref[c])
        n, rs = _rms(h_ref[...])
        dg_ref[...] = dg_ref[...] + jnp.sum(du * n, axis=0, keepdims=True)
        dh1_ref[...] = dh2v + _rms_bwd(du, n, rs, g_ref[...])

    row = lambda wd: pl.BlockSpec((tm, wd), lambda i: (i, 0))
    return pl.pallas_call(
        body, name="mlp_bwd", grid=(S // tm,),
        in_specs=[row(D), row(DFF), row(D), _cspec((1, D)), _cspec((NDEV, D, DFF // NDEV)), _cspec((NDEV, HALF, D)),
                  _cspec((NDEV, HALF, D))],
        out_specs=[row(DFF), row(D), pl.BlockSpec((1, D), lambda i: (0, 0))],
        out_shape=[jax.ShapeDtypeStruct((S, DFF), BF16), jax.ShapeDtypeStruct((S, D), F32),
                   jax.ShapeDtypeStruct((1, D), F32)],
        compiler_params=_params(1),
    )(dh2, a, h1, g, w_up, w_down_a, w_down_b)


def _out_proj_bwd(dh1, attn, ml, w):
    tm = TM

    def body(d_ref, a_ref, m_ref, w_ref, da_ref, dm_ref, dw_ref, acc):
        i = pl.program_id(0)

        @pl.when(i == 0)
        def _():
            acc[...] = jnp.zeros_like(acc)

        db = _bf(d_ref[...])
        dmix = _dot_nt(db, w_ref[...])
        da_ref[...] = dmix[:, 0:AW]
        dm_ref[...] = dmix[:, AW:D]
        acc[0:AW, :] = acc[0:AW, :] + _dot_tn(_bf(a_ref[...]), db)
        acc[AW:D, :] = acc[AW:D, :] + _dot_tn(_bf(m_ref[...]), db)

        @pl.when(i == S // tm - 1)
        def _():
            dw_ref[...] = _bf(acc[...])

    row = lambda wd: pl.BlockSpec((tm, wd), lambda i: (i, 0))
    return pl.pallas_call(
        body, name="out_proj_bwd", grid=(S // tm,),
        in_specs=[row(D), row(AW), row(MW), _cspec((D, D))],
        out_specs=[row(AW), row(MW), pl.BlockSpec((D, D), lambda i: (0, 0))],
        out_shape=[jax.ShapeDtypeStruct((S, AW), F32), jax.ShapeDtypeStruct((S, MW), F32),
                   jax.ShapeDtypeStruct((D, D), BF16)],
        scratch_shapes=[pltpu.VMEM((D, D), F32)],
        compiler_params=_params(1),
    )(dh1, attn, ml, w)


CHIP_FLIPS = [(0, 0), (0, 1), (1, 0), (1, 1)]


def _scatter2_phases(in_ref, out_ref, mine_v, sib_v, psum_v, loc_sems, d2d_send, d2d_recv, ici_send, ici_recv, own_sem):
    x, y, c = _place()
    chips = [((x + dx) % 2, (y + dy) % 2) for dx, dy in CHIP_FLIPS]
    nc = len(chips)

    def local(k):
        return pltpu.make_async_copy(in_ref.at[_dev_index(*chips[k], c)], mine_v.at[k], loc_sems.at[k])

    def to_sib(k):
        return pltpu.make_async_remote_copy(
            src_ref=in_ref.at[_dev_index(*chips[k], 1 - c)], dst_ref=sib_v.at[k], send_sem=d2d_send.at[k],
            recv_sem=d2d_recv.at[k], device_id=(x, y, 1 - c), device_id_type=MESH)

    def over_ici(k):
        return pltpu.make_async_remote_copy(
            src_ref=psum_v.at[k], dst_ref=out_ref.at[k], send_sem=ici_send.at[k - 1], recv_sem=ici_recv.at[k - 1],
            device_id=(*chips[k], c), device_id_type=MESH)

    def own():
        return pltpu.make_async_copy(psum_v.at[0], out_ref.at[0], own_sem)

    def start():
        for k in range(nc):
            to_sib(k).start()
            local(k).start()

    def middle():
        for k in (1, 2, 3, 0):
            local(k).wait()
            to_sib(k).wait_recv()
            psum_v[k] = _bf(mine_v[k].astype(F32) + sib_v[k].astype(F32))
            (over_ici(k) if k else own()).start()

    def finish():
        for k in range(1, nc):
            over_ici(k).wait()
        for k in range(nc):
            to_sib(k).wait_send()
        own().wait()

    return start, middle, finish


def _scatter2_scratch(shard, dtype):
    nc = len(CHIP_FLIPS)
    return ([pltpu.VMEM((nc, *shard), dtype)] * 3
            + [pltpu.SemaphoreType.DMA((nc,))] * 3 + [pltpu.SemaphoreType.DMA((nc - 1,))] * 2 + [pltpu.SemaphoreType.DMA])


def _in_proj_bwd(dparts, n_roped, rope, dh1, x, g1, w, part):
    tm = TM
    nt = S // tm
    widths = [d.shape[1] for d in dparts]
    assert sum(widths) == PW
    npar = len(dparts)

    def body(*refs):
        d_refs = refs[:npar]
        tabs = [t[...] for t in refs[npar:npar + 3]]
        dh_ref, x_ref, g_ref, w_ref, in_ref, dx_ref, dg_ref, out_ref = refs[npar + 3:npar + 11]
        rs_start, rs_middle, rs_finish = _scatter2_phases(in_ref, out_ref, *refs[npar + 11:])
        i = pl.program_id(0)
        pl.when(i == 0)(rs_start)
        pl.when(i == 1)(rs_middle)

        @pl.when(i == 0)
        def _():
            dg_ref[...] = jnp.zeros_like(dg_ref)

        du = jnp.zeros((tm, D), F32)
        off = 0
        for j, (d_ref, wd) in enumerate(zip(d_refs, widths)):
            nc = next(c for c in (768, 512) if wd % c == 0)
            for s in range(wd // nc):
                d = d_ref[:, s * nc:(s + 1) * nc]
                du = du + _dot_nt(_unrope(d, *tabs) if j < n_roped else d, w_ref[:, off + s * nc:off + (s + 1) * nc])
            off += wd
        n, rs = _rms(x_ref[...])
        dg_ref[...] = dg_ref[...] + jnp.sum(du * n, axis=0, keepdims=True)
        dx_ref[...] = dh_ref[...] + _rms_bwd(du, n, rs, g_ref[...])
        pl.when(i == nt - 1)(rs_finish)

    row = lambda wd: pl.BlockSpec((tm, wd), lambda i: (i, 0))
    shard = part.shape[1:]
    return pl.pallas_call(
        body, name="in_proj_bwd", grid=(nt,),
        in_specs=[row(wd) for wd in widths] + [row(128)] * 3 + [row(D), row(D), _cspec((1, D)), _cspec((D, PW)), ANY],
        out_specs=[row(D), pl.BlockSpec((1, D), lambda i: (0, 0)), ANY],
        out_shape=[jax.ShapeDtypeStruct((S, D), F32), jax.ShapeDtypeStruct((1, D), F32),
                   jax.ShapeDtypeStruct((len(CHIP_FLIPS), *shard), part.dtype)],
        scratch_shapes=_scatter2_scratch(shard, part.dtype),
        compiler_params=_params(1),
    )(*dparts, *rope, dh1, x, g1, w, part)


SMALL_ROWS = 96


def _small_phases(ins, out_ref, pack, rbuf, send_sems, recv_sems):
    x, y, c = _place()
    me = _dev_index(x, y, c)

    def copies():
        out = []
        for k, (dx, dy, dc) in enumerate(FLIPS):
            peer = ((x + dx) % 2, (y + dy) % 2, (c + dc) % 2)
            out.append(pltpu.make_async_remote_copy(
                src_ref=pack, dst_ref=rbuf.at[me], send_sem=send_sems.at[k], recv_sem=recv_sems.at[k],
                device_id=peer, device_id_type=MESH))
        return out

    def start():
        pack[...] = jnp.zeros_like(pack)
        for i, ref in enumerate(ins):
            pack[8 * i:8 * i + 1, 0:ref.shape[1]] = ref[...]
        rbuf[me] = pack[...]
        for cp in copies():
            cp.start()

    def finish():
        for cp in copies():
            cp.wait()
        tot = rbuf[0]
        for j in range(1, NDEV):
            tot = tot + rbuf[j]
        out_ref[...] = tot

    return start, finish


def _wgrad(name, A, Bs, a_fn, b_fn, out_shape, split=None, ts=512, small=(), rope=(), n_roped=0):
    K = A.shape[1]
    widths = [b.shape[1] for b in Bs]
    N = sum(widths)
    nb, ns, nrt = len(Bs) + len(rope), len(small), S // ts
    kc = min(K, 1024)

    def body(*refs):
        a_ref, b_refs = refs[0], refs[1:1 + len(Bs)]
        tabs = [t[...] for t in refs[1 + len(Bs):1 + nb]]
        o_ref = refs[1 + nb + ns]
        acc = refs[2 + nb + ns + bool(ns)]
        r = pl.program_id(0)
        if ns:
            sm_start, sm_finish = _small_phases(refs[1 + nb:1 + nb + ns], refs[2 + nb + ns], *refs[4 + nb + ns:])
            pl.when(r == 0)(sm_start)

        @pl.when(r == 0)
        def _():
            acc[...] = jnp.zeros_like(acc)

        bs, off = [], 0
        for i, (b_ref, w) in enumerate(zip(b_refs, widths)):
            nc = next(c for c in (1024, 768, 512) if w % c == 0)
            fn = (lambda t: _unrope(t, *tabs)) if i < n_roped else b_fn
            bs += [(off + c * nc, nc, fn(b_ref[:, c * nc:(c + 1) * nc])) for c in range(w // nc)]
            off += w
        for kk in range(K // kc):
            rows = slice(kk * kc, (kk + 1) * kc)
            at = a_fn(a_ref[:, rows]).T
            for lo, nc, b in bs:
                acc[rows, lo:lo + nc] = acc[rows, lo:lo + nc] + _dot(at, b)

        @pl.when(r == nrt - 1)
        def _():
            if split is None:
                o_ref[...] = _bf(acc[...])
            else:
                for j in range(NDEV):
                    o_ref[j] = _bf(acc[:, split * j:split * (j + 1)])

        if ns:
            pl.when(r == nrt - 1)(sm_finish)

    in_specs = ([pl.BlockSpec((ts, K), lambda r: (r, 0))] + [pl.BlockSpec((ts, w), lambda r: (r, 0)) for w in widths]
                + [pl.BlockSpec((ts, 128), lambda r: (r, 0))] * len(rope))
    out_spec = pl.BlockSpec(out_shape, lambda r: (0,) * len(out_shape))
    scratch = [pltpu.VMEM((K, N), F32)]
    if not ns:
        return pl.pallas_call(
            body, name=name, grid=(nrt,), in_specs=in_specs, out_specs=out_spec,
            out_shape=jax.ShapeDtypeStruct(out_shape, BF16), scratch_shapes=scratch, compiler_params=_params(1),
        )(A, *Bs, *rope)
    return pl.pallas_call(
        body, name=name, grid=(nrt,), in_specs=in_specs + [VM] * ns, out_specs=[out_spec, VM],
        out_shape=[jax.ShapeDtypeStruct(out_shape, BF16), jax.ShapeDtypeStruct((SMALL_ROWS, 1024), F32)],
        scratch_shapes=scratch + [pltpu.VMEM((SMALL_ROWS, 1024), F32), pltpu.VMEM((NDEV, SMALL_ROWS, 1024), F32),
                                  pltpu.SemaphoreType.DMA((7,)), pltpu.SemaphoreType.DMA((7,))],
        compiler_params=_params(1),
    )(A, *Bs, *rope, *small)


def _relu2_bf(a):
    r = jnp.maximum(a.astype(F32), 0.0)
    return _bf(r * r)


def _ident(a):
    return a


def _step(x, p, target, g1, conv_b, gate_b, gn, g_mlp, g_ple, g_fin, sh):
    (g_in, g_conv), (rc, ra, rb) = _gather_weights([sh["w_in"], sh["conv_w"]], [BF16, F32])
    conv_w = g_conv.transpose(1, 0, 2).reshape(4, 1024)
    w_in_p = _join_w_in(g_in)
    (qkv, mqk, mv, mo, gates, u1), (w_out8, w_pg8, w_ple8) = _in_proj(
        x, g1, w_in_p, rc, ra, rb, [sh["w_out"], sh["w_ple_gate"], sh["w_ple"]], [BF16] * 3)
    attn, lse, (w_up8, w_down_a) = _attn_fwd(qkv, [sh["w_up"], sh["w_down"][0:HALF]], [BF16] * 2)
    ml, cs, ns, ms, (w_down_b,) = _mlstm_fwd(mqk, mv, mo, gates, conv_w, conv_b, gate_b, gn,
                                             [sh["w_down"][HALF:2 * HALF]], [BF16])
    w_out, w_pg = w_out8.reshape(D, D), w_pg8.reshape(D, D)
    h1, u2 = _out_proj(x, attn, ml, w_out, g_mlp)
    a, h2 = _mlp_fwd(h1, u2, w_up8, w_down_a, w_down_b)
    dh2, dw_pg, dw_ple8, dg_ple, dg_fin, loss = _ple_loss(h2, p, target, w_pg, w_ple8, g_ple, g_fin)
    da, dh1, dg_mlp = _mlp_bwd(dh2, a, h1, g_mlp, w_up8, w_down_a, w_down_b)
    dw_up8 = _wgrad("wgrad_up", u2, [da], _ident, _ident, (NDEV, D, DFF // NDEV), split=DFF // NDEV)
    dw_down = _wgrad("wgrad_down", a, [dh2], _relu2_bf, _bf, (DFF, D))
    d_attn, d_ml, dw_out = _out_proj_bwd(dh1, attn, ml, w_out)
    (dm, dconv_w, dconv_b, dgn, dgate_b), (r_down,) = _mlstm_bwd(
        mqk, mv, mo, gates, conv_w, conv_b, gate_b, gn, cs, ns, ms, d_ml, [dw_down.reshape(NDEV, DFF // NDEV, D)])
    dq, dk, dv, (r_up, r_out, r_pg, r_ple) = _attn_bwd(
        qkv, attn, lse, d_attn,
        [dw_up8, dw_out.reshape(NDEV, D // NDEV, D), dw_pg.reshape(NDEV, D // NDEV, D), dw_ple8])
    dparts = [dq, dk, dv, dm]
    small = [jnp.zeros((1, D), F32), dconv_b, dgate_b, dgn, dg_mlp, dg_ple, dg_fin, loss]
    dw_in8, total = _wgrad("wgrad_in", u1, dparts, _ident, _ident, (NDEV, D, IN_W // NDEV), split=IN_W // NDEV,
                           small=small + [dconv_w[j:j + 1] for j in range(4)], rope=(rc, ra, rb), n_roped=2)
    dx, dg1, r_in = _in_proj_bwd(dparts, 2, (rc, ra, rb), dh1, x, g1, w_in_p, dw_in8)
    recv = dict(w_in=r_in, w_out=r_out, w_up=r_up, w_down=r_down, w_ple_gate=r_pg, w_ple=r_ple)
    return dx, recv, total, _allreduce_vec(dg1)


def _gather_weights(shards, dtypes):
    nw = len(shards)

    def body(*refs):
        ins, parts = refs[:nw], refs[nw:nw + 4]
        outs, tables = refs[nw + 4:2 * nw + 4], refs[2 * nw + 4:2 * nw + 7]
        start, forward, finish = _gather_phases(ins, outs, refs[2 * nw + 7:3 * nw + 7], *refs[3 * nw + 7:])
        start()
        _rope_fill(*parts, *tables)
        forward()
        finish()

    res = pl.pallas_call(
        body, name="gather_weights",
        in_specs=[VM] * (nw + 4), out_specs=[ANY] * nw + [VM] * 3,
        out_shape=_gather_shapes(shards, dtypes) + [jax.ShapeDtypeStruct((S, 128), F32)] * 3,
        scratch_shapes=_gather_scratch(shards, dtypes),
        compiler_params=_params(),
    )(*shards, *_rope_parts())
    return res[:nw], res[nw:]


def _allreduce_vec(v):
    def body(v_ref, out_ref, pack, rbuf, send_sems, recv_sems):
        start, finish = _small_phases([v_ref], out_ref, pack, rbuf, send_sems, recv_sems)
        start()
        finish()

    return pl.pallas_call(
        body, name="allreduce_last", out_shape=jax.ShapeDtypeStruct((8, 1024), F32),
        scratch_shapes=[pltpu.VMEM((8, 1024), F32), pltpu.VMEM((NDEV, 8, 1024), F32),
                        pltpu.SemaphoreType.DMA((7,)), pltpu.SemaphoreType.DMA((7,))],
        compiler_params=_params(),
    )(v)


ADAM_STEPS = 4


def _adamw(items):
    n = len(items)

    def body(*refs):
        for i in range(n):
            g_ref, w_ref, m_ref, v_ref = refs[4 * i:4 * i + 4]
            go_ref, d_ref, mo_ref, vo_ref = refs[4 * n + 4 * i:4 * n + 4 * i + 4]
            g = g_ref[0].astype(F32)
            for j in range(1, g_ref.shape[0]):
                g = g + g_ref[j].astype(F32)
            go_ref[...] = g
            d_ref[...], mo_ref[...], vo_ref[...] = _adam_update(g, w_ref[...], m_ref[...], v_ref[...])

    in_specs, out_specs, out_shape, args = [], [], [], []
    for gparts, w, m, v in items:
        P, R, C = gparts.shape
        if R % (8 * ADAM_STEPS) == 0:
            tr = R // ADAM_STEPS
            row, gspec = pl.BlockSpec((tr, C), lambda i: (i, 0)), pl.BlockSpec((P, tr, C), lambda i: (0, i, 0))
        else:
            row, gspec = pl.BlockSpec((R, C), lambda i: (0, 0)), pl.BlockSpec((P, R, C), lambda i: (0, 0, 0))
        in_specs += [gspec, row, row, row]
        out_specs += [row] * 4
        out_shape += [jax.ShapeDtypeStruct((R, C), F32)] * 4
        args += [gparts, w, m, v]
    res = pl.pallas_call(
        body, name="adamw", grid=(ADAM_STEPS,), in_specs=in_specs, out_specs=out_specs, out_shape=out_shape,
        compiler_params=_params(1),
    )(*args)
    return [res[4 * i:4 * i + 4] for i in range(n)]


SMALL = ("norm_mix_g", "conv_b", "gate_b", "mlstm_norm_g", "norm_mlp_g", "norm_ple_g", "final_norm_g")


def _adam_update(g, w, m, v):
    c1 = 1.0 - ADAM_B1 ** ADAM_STEP
    c2 = 1.0 - ADAM_B2 ** ADAM_STEP
    m2 = ADAM_B1 * m + (1.0 - ADAM_B1) * g
    v2 = ADAM_B2 * v + (1.0 - ADAM_B2) * (g * g)
    return -ADAM_LR * ((m2 / c1) / (jnp.sqrt(v2 / c2) + ADAM_EPS) + ADAM_WD * w), m2, v2


def _adamw_small(total, first, ws, ms, vs):
    n = len(ws)

    def body(*refs):
        t_ref, f_ref = refs[:2]
        refs = refs[1:]
        outs = refs[1 + 3 * n:]
        for i in range(n):
            w_ref, m_ref, v_ref = refs[1 + i], refs[1 + n + i], refs[1 + 2 * n + i]
            g = (t_ref if i else f_ref)[8 * i:8 * i + 1, 0:w_ref.shape[1]]
            delta, m2, v2 = _adam_update(g, w_ref[...], m_ref[...], v_ref[...])
            for ref, val in zip(outs[4 * i:4 * i + 4], (g, delta, m2, v2)):
                ref[...] = val

    res = pl.pallas_call(
        body, name="adamw_small",
        out_shape=[jax.ShapeDtypeStruct(w.shape, F32) for w in ws for _ in range(4)],
        compiler_params=_params(),
    )(total, first, *ws, *ms, *vs)
    return [res[4 * i:4 * i + 4] for i in range(n)]


def kernel(x, p, norm_mix_g, w_in, conv_w, conv_b, gate_b, mlstm_norm_g, w_out, norm_mlp_g, w_up, w_down, norm_ple_g, w_ple_gate, w_ple, final_norm_g, loss_target, m_norm_mix_g, m_w_in, m_conv_w, m_conv_b, m_gate_b, m_mlstm_norm_g, m_w_out, m_norm_mlp_g, m_w_up, m_w_down, m_norm_ple_g, m_w_ple_gate, m_w_ple, m_final_norm_g, v_norm_mix_g, v_w_in, v_conv_w, v_conv_b, v_gate_b, v_mlstm_norm_g, v_w_out, v_norm_mlp_g, v_w_up, v_w_down, v_norm_ple_g, v_w_ple_gate, v_w_ple, v_final_norm_g):
    big_names = ("w_in", "conv_w", "w_out", "w_up", "w_down", "w_ple_gate", "w_ple")
    wts = dict(w_in=w_in, conv_w=conv_w, w_out=w_out, w_up=w_up, w_down=w_down, w_ple_gate=w_ple_gate, w_ple=w_ple)
    mom = dict(w_in=m_w_in, conv_w=m_conv_w, w_out=m_w_out, w_up=m_w_up, w_down=m_w_down, w_ple_gate=m_w_ple_gate,
               w_ple=m_w_ple)
    var = dict(w_in=v_w_in, conv_w=v_conv_w, w_out=v_w_out, w_up=v_w_up, w_down=v_w_down, w_ple_gate=v_w_ple_gate,
               w_ple=v_w_ple)
    sq = lambda a: a.reshape(a.shape[1:])
    fin = final_norm_g.reshape(1, D)
    dx, recv, total, first = _step(
        x[0], p[0, 0], loss_target[0], norm_mix_g, conv_b, jnp.pad(gate_b, ((0, 0), (0, 120))), mlstm_norm_g,
        norm_mlp_g, norm_ple_g, fin, {n: sq(wts[n]) for n in big_names})

    nrow = 8 * len(SMALL)
    me = _dev_index(*_place())
    conv_rows = total[nrow + 8:nrow + 40:8]
    recv["conv_w"] = lax.dynamic_slice_in_dim(conv_rows, me * 128, 128, axis=1).reshape(1, 4, 128)
    out = {}
    for n, res in zip(big_names, _adamw([(recv[n], sq(wts[n]), sq(mom[n]), sq(var[n])) for n in big_names])):
        out[n] = [t.reshape(wts[n].shape) for t in res]
    sw = dict(norm_mix_g=norm_mix_g, conv_b=conv_b, gate_b=gate_b, mlstm_norm_g=mlstm_norm_g, norm_mlp_g=norm_mlp_g,
              norm_ple_g=norm_ple_g, final_norm_g=fin)
    sm = dict(norm_mix_g=m_norm_mix_g, conv_b=m_conv_b, gate_b=m_gate_b, mlstm_norm_g=m_mlstm_norm_g,
              norm_mlp_g=m_norm_mlp_g, norm_ple_g=m_norm_ple_g, final_norm_g=m_final_norm_g.reshape(1, D))
    sv = dict(norm_mix_g=v_norm_mix_g, conv_b=v_conv_b, gate_b=v_gate_b, mlstm_norm_g=v_mlstm_norm_g,
              norm_mlp_g=v_norm_mlp_g, norm_ple_g=v_norm_ple_g, final_norm_g=v_final_norm_g.reshape(1, D))
    res = _adamw_small(total, first, [sw[n] for n in SMALL], [sm[n] for n in SMALL], [sv[n] for n in SMALL])
    for n, r in zip(SMALL, res):
        out[n] = [t.reshape(final_norm_g.shape) for t in r] if n == "final_norm_g" else list(r)
    order = ("norm_mix_g", "w_in", "conv_w", "conv_b", "gate_b", "mlstm_norm_g", "w_out", "norm_mlp_g", "w_up", "w_down",
             "norm_ple_g", "w_ple_gate", "w_ple", "final_norm_g")
    loss_all = total[nrow, 0]
    return (loss_all, dx[None], *[out[n][0] for n in order], *[out[n][1] for n in order],
            *[out[n][2] for n in order], *[out[n][3] for n in order])
```

```python
import functools
import math

import jax
import jax.numpy as jnp
from jax import lax
from jax.experimental import pallas as pl
from jax.experimental.pallas import tpu as pltpu

F32, BF16 = jnp.float32, jnp.bfloat16
S = 4096
D = 1024
AW = 512
MW = 512
DFF = 4096
PLE = 256
IN_W = 3592
PW = 3840
NDEV = 8
EPS = 1e-6
NEG = -1e30
LC = 128
TB = 256
ROPE_THETA = 500000.0
VMEM_LIMIT = 56 * 1024 * 1024
HI = lax.Precision.HIGHEST

ADAM_LR, ADAM_B1, ADAM_B2, ADAM_EPS, ADAM_WD, ADAM_STEP = 0.001, 0.9, 0.999, 1e-08, 0.01, 10


def _params(n_grid=0, **kw):
    sem = dict(dimension_semantics=("arbitrary",) * n_grid) if n_grid else {}
    return pltpu.CompilerParams(vmem_limit_bytes=VMEM_LIMIT, **sem, **kw)


def _cspec(shape):
    nd = len(shape)
    return pl.BlockSpec(shape, lambda *_: (0,) * nd, pipeline_mode=pl.Buffered(1))


def _dot(a, b):
    return jnp.dot(a, b, preferred_element_type=F32)


def _dot_nt(a, b):
    return lax.dot_general(a, b, (((1,), (1,)), ((), ())), preferred_element_type=F32)


def _dot_tn(a, b):
    return lax.dot_general(a, b, (((0,), (0,)), ((), ())), preferred_element_type=F32)


def _bf(x):
    return x.astype(BF16)


def _rms(x):
    rs = lax.rsqrt(jnp.mean(x * x, axis=-1, keepdims=True) + EPS)
    return x * rs, rs


def _rms_bwd(du, n, rs, g):
    dn = du * g
    return rs * (dn - n * jnp.mean(dn * n, axis=-1, keepdims=True))


def _sigmoid(x):
    return 1.0 / (1.0 + jnp.exp(-x))


ROPE_BLK = 512


def _rope_parts():
    def cs(n, step):
        j = lax.broadcasted_iota(jnp.int32, (n, 128), 1) % 64
        pos = (lax.broadcasted_iota(jnp.int32, (n, 128), 0) * step).astype(F32)
        ang = pos * jnp.power(ROPE_THETA, -(j % 8).astype(F32) / 8.0)
        return jnp.cos(ang), jnp.sin(ang)

    return (*cs(ROPE_BLK, 1), *cs(S // ROPE_BLK, ROPE_BLK))


def _rope_fill(co_ref, so_ref, cb_ref, sb_ref, rc_ref, ra_ref, rb_ref):
    j = lax.broadcasted_iota(jnp.int32, (ROPE_BLK, 128), 1) % 64
    co, so = co_ref[...], so_ref[...]
    for t in range(S // ROPE_BLK):
        cb, sb = cb_ref[t:t + 1, :], sb_ref[t:t + 1, :]
        cos, sin = cb * co - sb * so, sb * co + cb * so
        rows = slice(t * ROPE_BLK, (t + 1) * ROPE_BLK)
        rc_ref[rows, :] = jnp.where(j < 16, cos, 1.0)
        ra_ref[rows, :] = jnp.where(j < 8, -sin, 0.0)
        rb_ref[rows, :] = jnp.where((j >= 8) & (j < 16), sin, 0.0)


def _rope(blk, c, a, b):
    return blk * c + pltpu.roll(blk, 120, 1) * a + pltpu.roll(blk, 8, 1) * b


def _rope_bwd(d, c, a, b):
    return d * c + pltpu.roll(d * a, 8, 1) + pltpu.roll(d * b, 120, 1)


def _unrope(t, c, a, b):
    return jnp.concatenate([_bf(_rope_bwd(t[:, j * 128:(j + 1) * 128].astype(F32), c, a, b))
                            for j in range(t.shape[1] // 128)], axis=1)


MESH = pl.DeviceIdType.MESH
ANY = pl.BlockSpec(memory_space=pl.ANY)
VM = pl.BlockSpec(memory_space=pltpu.VMEM)
FLIPS = [(dx, dy, dc) for dx in (0, 1) for dy in (0, 1) for dc in (0, 1)][1:]


def _place():
    return lax.axis_index("x"), lax.axis_index("y"), lax.axis_index("c")


def _dev_index(px, py, pc):
    return 4 * px + 2 * py + pc


def _gather_phases(ins, outs, bufs, send_sems=None, recv_sems=None, local_sems=None):
    nw = len(ins)
    if nw == 0:
        return (lambda: None,) * 3
    x, y, c = _place()
    me, sib = (x, y, c), (x, y, 1 - c)
    chips = [(1 - x, y), (x, 1 - y), (1 - x, 1 - y)]

    def copy(w, k, block, to, from_buf=False):
        dst = outs[w].at[_dev_index(*block)]
        return pltpu.make_async_remote_copy(
            src_ref=bufs[w] if from_buf else dst, dst_ref=dst, send_sem=send_sems.at[w, k],
            recv_sem=recv_sems.at[w, k], device_id=to, device_id_type=MESH)

    def mine(w):
        return pltpu.make_async_copy(bufs[w], outs[w].at[_dev_index(*me)], local_sems.at[w])

    def first(w):
        return [copy(w, 0, me, sib, True)] + [copy(w, 1 + j, me, (*chip, c), True) for j, chip in enumerate(chips)]

    def passed(w):
        return [copy(w, 4 + j, (*chip, c), sib) for j, chip in enumerate(chips)]

    def start():
        for w in range(nw):
            bufs[w][...] = ins[w][...].astype(bufs[w].dtype)
        for w in range(nw):
            mine(w).start()
            for cp in first(w):
                cp.start()

    def forward():
        for j, chip in enumerate(chips):
            for w in range(nw):
                copy(w, 1 + j, (*chip, c), me).wait_recv()
                passed(w)[j].start()

    def finish():
        for w in range(nw):
            copy(w, 0, sib, me).wait_recv()
        for j, chip in enumerate(chips):
            for w in range(nw):
                copy(w, 4 + j, (*chip, 1 - c), me).wait_recv()
        for w in range(nw):
            for cp in first(w) + passed(w):
                cp.wait_send()
            mine(w).wait()

    return start, forward, finish


def _gather_scratch(shards, dtypes):
    nw = len(shards)
    if nw == 0:
        return []
    return ([pltpu.VMEM(s.shape, dt) for s, dt in zip(shards, dtypes)]
            + [pltpu.SemaphoreType.DMA((nw, 7)), pltpu.SemaphoreType.DMA((nw, 7)), pltpu.SemaphoreType.DMA((nw,))])


def _gather_shapes(shards, dtypes):
    return [jax.ShapeDtypeStruct((NDEV, *s.shape), dt) for s, dt in zip(shards, dtypes)]


def _scatter_phases(ins, outs, send_sems=None, recv_sems=None, local_sems=None):
    nw = len(ins)
    if nw == 0:
        return (lambda: None,) * 2
    x, y, c = _place()
    me = _dev_index(x, y, c)

    def copies():
        out = []
        for w in range(nw):
            out.append(pltpu.make_async_copy(ins[w].at[me], outs[w].at[me], local_sems.at[w]))
            for k, (dx, dy, dc) in enumerate(FLIPS):
                peer = ((x + dx) % 2, (y + dy) % 2, (c + dc) % 2)
                out.append(pltpu.make_async_remote_copy(
                    src_ref=ins[w].at[_dev_index(*peer)], dst_ref=outs[w].at[me], send_sem=send_sems.at[w, k],
                    recv_sem=recv_sems.at[w, k], device_id=peer, device_id_type=MESH))
        return out

    def start():
        for cp in copies():
            cp.start()

    def finish():
        for cp in copies():
            cp.wait()

    return start, finish


def _scatter_scratch(nw):
    if nw == 0:
        return []
    return [pltpu.SemaphoreType.DMA((nw, 7)), pltpu.SemaphoreType.DMA((nw, 7)), pltpu.SemaphoreType.DMA((nw,))]


TM = 512


def _join_w_in(wg):
    sw = IN_W // NDEV

    def body(wg_ref, w_ref):
        for j in range(NDEV):
            w_ref[:, sw * j:sw * (j + 1)] = wg_ref[j]
        w_ref[:, IN_W:PW] = jnp.zeros((D, PW - IN_W), BF16)

    return pl.pallas_call(body, name="join_w_in", out_shape=jax.ShapeDtypeStruct((D, PW), BF16),
                          compiler_params=_params())(wg)


def _in_proj(x, g1, w, rc, ra, rb, shards, dtypes):
    tm = TM
    nw = len(shards)
    nt = S // tm

    def body(*refs):
        x_ref, g_ref, w_ref, rc_ref, ra_ref, rb_ref = refs[:6]
        ins = refs[6:6 + nw]
        qkv_ref, mqk_ref, mv_ref, mo_ref, gt_ref, u_ref = refs[6 + nw:12 + nw]
        outs = refs[12 + nw:12 + 2 * nw]
        bufs = refs[12 + 2 * nw:12 + 3 * nw]
        ag_start, ag_forward, ag_finish = _gather_phases(ins, outs, bufs, *refs[12 + 3 * nw:])
        i = pl.program_id(0)
        pl.when(i == 0)(ag_start)
        pl.when(i == nt - 2)(ag_forward)
        n, _ = _rms(x_ref[...])
        u = _bf(n * g_ref[...])
        u_ref[...] = u
        c, a, b = rc_ref[...], ra_ref[...], rb_ref[...]
        for half in range(2):
            blk = _dot(u, w_ref[:, half * 512:(half + 1) * 512])
            for t in range(4):
                lo = half * 512 + t * 128
                qkv_ref[:, lo:lo + 128] = _rope(blk[:, t * 128:(t + 1) * 128], c, a, b)
        qkv_ref[:, 1024:1536] = _dot(u, w_ref[:, 1024:1536])
        mqk_ref[:, 0:512] = _dot(u, w_ref[:, 1536:2048])
        mqk_ref[:, 512:1024] = _dot(u, w_ref[:, 2048:2560])
        mv_ref[...] = _dot(u, w_ref[:, 2560:3072])
        mo_ref[...] = _dot(u, w_ref[:, 3072:3584])
        gt_ref[...] = _dot(u, w_ref[:, 3584:3712])
        pl.when(i == nt - 1)(ag_finish)

    row = lambda wd: pl.BlockSpec((tm, wd), lambda i: (i, 0))
    res = pl.pallas_call(
        body, name="in_proj", grid=(nt,),
        in_specs=[row(D), _cspec((1, D)), _cspec((D, PW)), row(128), row(128), row(128)] + [VM] * nw,
        out_specs=[row(1536), row(1024), row(512), row(512), row(128), row(D)] + [ANY] * nw,
        out_shape=[jax.ShapeDtypeStruct((S, 1536), F32), jax.ShapeDtypeStruct((S, 1024), F32),
                   jax.ShapeDtypeStruct((S, 512), F32), jax.ShapeDtypeStruct((S, 512), F32),
                   jax.ShapeDtypeStruct((S, 128), F32), jax.ShapeDtypeStruct((S, D), BF16)]
        + _gather_shapes(shards, dtypes),
        scratch_shapes=_gather_scratch(shards, dtypes),
        compiler_params=_params(1),
    )(x, g1, w, rc, ra, rb, *shards)
    return res[:6], res[6:]


DILATIONS = (16, 4, 1)


def _attn_valid(n):
    kd = lax.broadcasted_iota(jnp.int32, (128, 256), 1) - lax.broadcasted_iota(jnp.int32, (128, 256), 0)
    off = jnp.where(n == 0, 0, 128)
    return (kd <= off) & (kd >= off - 128)


def _attn_rows(d, r, n):
    if d == 1:
        q0 = pl.multiple_of(n * 128, 128)
        k0 = pl.multiple_of(jnp.maximum(n - 1, 0) * 128, 128)
        return pl.ds(q0, 128), pl.ds(k0, 256), _attn_valid(n)
    q0 = r + n * 128 * d
    k0 = r + jnp.maximum(n - 1, 0) * 128 * d
    return pl.ds(q0, 128, stride=d), pl.ds(k0, 256, stride=d), _attn_valid(n)


ATTN_GROUP = 4
ATTN_ITERS = S // 128 // ATTN_GROUP


def _attn_group(d, i):
    nb = S // (128 * d)
    if nb == 2:
        qi = lax.broadcasted_iota(jnp.int32, (256, 256), 0) - lax.broadcasted_iota(jnp.int32, (256, 256), 1)
        whole = [pl.ds((ATTN_GROUP // 2) * i + u, 256, stride=d) for u in range(ATTN_GROUP // 2)]
        return [(rows, rows, (qi >= 0) & (qi <= 128)) for rows in whole]
    if d == 1:
        return [_attn_rows(1, 0, i + ATTN_ITERS * u) for u in range(ATTN_GROUP)]
    return [_attn_rows(d, (i // nb) * ATTN_GROUP + u, i % nb) for u in range(ATTN_GROUP)]


def _head0(shape):
    return lax.broadcasted_iota(jnp.int32, shape, 1) < 64


def _stack_heads(t):
    h0 = _head0(t.shape)
    tb = _bf(t)
    zero = jnp.zeros_like(tb)
    return jnp.concatenate([jnp.where(h0, tb, zero), jnp.where(h0, zero, tb)], axis=0)


def _attn_fwd(qkv, shards, dtypes):
    nw = len(shards)

    def body(*refs):
        q_ref, k_ref, v_ref = refs[:3]
        ins = refs[3:3 + nw]
        o_ref, lse0_ref, lse1_ref = refs[3 + nw:6 + nw]
        outs = refs[6 + nw:6 + 2 * nw]
        m0, m1, l0, l1, acc = refs[6 + 2 * nw:11 + 2 * nw]
        bufs = refs[11 + 2 * nw:11 + 3 * nw]
        ag_start, ag_forward, ag_finish = _gather_phases(ins, outs, bufs, *refs[11 + 3 * nw:])
        hp = pl.program_id(0)
        pl.when(hp == 0)(ag_start)
        pl.when(hp == 3)(ag_forward)
        stats = (m0, m1, l0, l1, acc)

        def update(blocks, first):
            loaded = [([q_ref[rq, :], k_ref[rk, :], v_ref[rk, :]], None if first else [ref[rq, :] for ref in stats])
                      for rq, rk, _ in blocks]
            results = []
            for ((q, k, v), prev), (_, _, valid) in zip(loaded, blocks):
                nq = q.shape[0]
                both = lambda a, b: jnp.concatenate([a, b], axis=0)
                kb, vb = _bf(k), _bf(v)
                s = jnp.where(both(valid, valid), _dot_nt(_stack_heads(q * 0.125), kb), NEG)
                mc = jnp.max(s, axis=-1, keepdims=True)
                m2 = jnp.broadcast_to(mc, (2 * nq, 128)) if first else jnp.maximum(both(prev[0], prev[1]), mc)
                p = jnp.exp(s - jnp.tile(m2, (1, 2)))
                l2 = jnp.sum(p, axis=-1, keepdims=True)
                acc2 = _dot(_bf(p), vb)
                if first:
                    l2 = jnp.broadcast_to(l2, (2 * nq, 128))
                else:
                    alpha = jnp.exp(both(prev[0], prev[1]) - m2)
                    l2, acc2 = alpha * both(prev[2], prev[3]) + l2, alpha * both(prev[4], prev[4]) + acc2
                results.append((m2[0:nq], m2[nq:2 * nq], l2[0:nq], l2[nq:2 * nq],
                                jnp.where(_head0((nq, 128)), acc2[0:nq], acc2[nq:2 * nq])))
            for (rq, _, _), res in zip(blocks, results):
                for ref, val in zip(stats, res):
                    ref[rq, :] = val

        for d in DILATIONS:
            def step(i, carry, d=d):
                update(_attn_group(d, i), d == DILATIONS[0])
                return carry

            lax.fori_loop(0, ATTN_ITERS, step, 0)

        def fin(t, carry):
            rows = pl.ds(pl.multiple_of(t * 256, 256), 256)
            h0 = lax.broadcasted_iota(jnp.int32, (256, 128), 1) < 64
            la, lb = l0[rows, :], l1[rows, :]
            o_ref[rows, :] = acc[rows, :] / jnp.where(h0, la, lb)
            lse0_ref[rows, :] = m0[rows, :] + jnp.log(la)
            lse1_ref[rows, :] = m1[rows, :] + jnp.log(lb)
            return carry

        lax.fori_loop(0, S // 256, fin, 0)
        pl.when(hp == 3)(ag_finish)

    col = lambda off: pl.BlockSpec((S, 128), lambda h, off=off: (0, off + h))
    res = pl.pallas_call(
        body, name="attn_fwd", grid=(4,),
        in_specs=[col(0), col(4), col(8)] + [VM] * nw,
        out_specs=[col(0), col(0), col(0)] + [ANY] * nw,
        out_shape=[jax.ShapeDtypeStruct((S, AW), F32)] * 3 + _gather_shapes(shards, dtypes),
        scratch_shapes=[pltpu.VMEM((S, 128), F32)] * 5 + _gather_scratch(shards, dtypes),
        compiler_params=_params(1),
    )(qkv, qkv, qkv, *shards)
    return res[0], (res[1], res[2]), res[3:]


def _attn_bwd(qkv, o, lse, do, parts):
    nw = len(parts)

    def body(*refs):
        q_ref, k_ref, v_ref, o_ref, L0, L1, do_ref = refs[:7]
        ins = refs[7:7 + nw]
        dq_out, dk_out, dv_out = refs[7 + nw:10 + nw]
        outs = refs[10 + nw:10 + 2 * nw]
        D0, D1, dq_ref, dk_ref, dv_ref = refs[10 + 2 * nw:15 + 2 * nw]
        rs_start, rs_finish = _scatter_phases(ins, outs, *refs[15 + 2 * nw:])
        hp = pl.program_id(0)
        pl.when(hp == 0)(rs_start)

        def pre(t, carry):
            rows = pl.ds(pl.multiple_of(t * 256, 256), 256)
            h0 = lax.broadcasted_iota(jnp.int32, (256, 128), 1) < 64
            dd = do_ref[rows, :] * o_ref[rows, :]
            shp = (256, 128)
            D0[rows, :] = jnp.broadcast_to(jnp.sum(jnp.where(h0, dd, 0.0), axis=-1, keepdims=True), shp)
            D1[rows, :] = jnp.broadcast_to(jnp.sum(jnp.where(h0, 0.0, dd), axis=-1, keepdims=True), shp)
            return carry

        lax.fori_loop(0, S // 256, pre, 0)

        def update(blocks, first):
            loaded = [([q_ref[rq, :], k_ref[rk, :], v_ref[rk, :], do_ref[rq, :]],
                       [L0[rq, :], L1[rq, :], D0[rq, :], D1[rq, :]],
                       [0.0] * 3 if first else [dq_ref[rq, :], dk_ref[rk, :], dv_ref[rk, :]]) for rq, rk, _ in blocks]
            results = []
            for ((q, k, v, dout), (l0v, l1v, d0v, d1v), (dq, dk, dv)), (_, _, valid) in zip(loaded, blocks):
                nq = q.shape[0]
                valid = jnp.concatenate([valid, valid], axis=0)
                q2, do2, kb, vb = _stack_heads(q), _stack_heads(dout), _bf(k), _bf(v)
                cat = lambda a, b: jnp.tile(jnp.concatenate([a, b], axis=0), (1, 2))
                s = jnp.where(valid, _dot_nt(_stack_heads(q * 0.125), kb), NEG)
                p = jnp.exp(s - cat(l0v, l1v))
                ds = _bf(p * (_dot_nt(do2, vb) - cat(d0v, d1v)) * 0.125)
                dq2 = _dot(ds, kb)
                results.append((dq + jnp.where(_head0((nq, 128)), dq2[0:nq], dq2[nq:2 * nq]),
                                dk + _dot_tn(ds, q2), dv + _dot_tn(_bf(p), do2)))
            for (rq, rk, _), (dq, dk, dv) in zip(blocks, results):
                dq_ref[rq, :] = dq
                dk_ref[rk, :] = dk
                dv_ref[rk, :] = dv

        assert S // (128 * DILATIONS[0]) == 2
        for d in DILATIONS:
            def step(i, carry, d=d):
                update(_attn_group(d, i), d == DILATIONS[0])
                return carry

            lax.fori_loop(0, ATTN_ITERS, step, 0)

        def fin(t, carry):
            rows = pl.ds(pl.multiple_of(t * 256, 256), 256)
            for src, dst in ((dq_ref, dq_out), (dk_ref, dk_out), (dv_ref, dv_out)):
                dst[rows, :] = _bf(src[rows, :])
            return carry

        lax.fori_loop(0, S // 256, fin, 0)
        pl.when(hp == 3)(rs_finish)

    col = lambda off: pl.BlockSpec((S, 128), lambda h, off=off: (0, off + h))
    res = pl.pallas_call(
        body, name="attn_bwd", grid=(4,),
        in_specs=[col(0), col(4), col(8), col(0), col(0), col(0), col(0)] + [ANY] * nw,
        out_specs=[col(0), col(0), col(0)] + [ANY] * nw,
        out_shape=[jax.ShapeDtypeStruct((S, AW), BF16)] * 3 + [jax.ShapeDtypeStruct(a.shape, a.dtype) for a in parts],
        scratch_shapes=[pltpu.VMEM((S, 128), F32)] * 5 + _scatter_scratch(nw),
        compiler_params=_params(1),
    )(qkv, qkv, qkv, o, lse[0], lse[1], do, *parts)
    return res[0], res[1], res[2], res[3:]


def _logsig(x):
    return jnp.minimum(x, 0.0) - jnp.log1p(jnp.exp(-jnp.abs(x)))


def _conv_taps(xp, n):
    return [xp[8:] if j == 3 else pltpu.roll(xp, 3 - j, 0)[8:] for j in range(4)]


def _conv_silu(xp, w_ref, b_ref, n):
    taps = _conv_taps(xp, n)
    c = b_ref[...] + sum(w_ref[j:j + 1, :] * taps[j] for j in range(4))
    sg = _sigmoid(c)
    return c, sg, taps


def _chunk_gates(G):
    assert LC == 128
    r = lax.broadcasted_iota(jnp.int32, (LC, LC), 0)
    c = lax.broadcasted_iota(jnp.int32, (LC, LC), 1)
    tril = (c <= r).astype(F32)
    triu = (c >= r).astype(F32)
    b_col = jnp.dot(tril, _logsig(G), preferred_element_type=F32, precision=HI)
    return b_col, b_col.T, G.T, tril, triu


def _colpick(X, lane):
    li = lax.broadcasted_iota(jnp.int32, X.shape, 1)
    return jnp.sum(jnp.where(li == lane, X, 0.0), axis=1, keepdims=True)


def _rowpick(XT, row):
    ri = lax.broadcasted_iota(jnp.int32, XT.shape, 0)
    return jnp.sum(jnp.where(ri == row, XT, 0.0), axis=0, keepdims=True)


def _mlstm_head(qh, kh, vh, G, b_col, b_row, g_row, h, Ch, nh, m_prev):
    bt = _colpick(b_col, 4 + h)
    i_col = _colpick(G, h)
    bs = _rowpick(b_row, 4 + h)
    i_row = _rowpick(g_row, h)
    r = lax.broadcasted_iota(jnp.int32, (LC, LC), 0)
    c = lax.broadcasted_iota(jnp.int32, (LC, LC), 1)
    log_d = jnp.where(c <= r, bt - bs + i_row, NEG)
    log_inter = bt + m_prev
    m_t = jnp.maximum(log_inter, jnp.max(log_d, axis=1, keepdims=True))
    Dm = jnp.exp(log_d - m_t)
    g = jnp.exp(log_inter - m_t)
    qb, kb, vb = _bf(qh), _bf(kh), _bf(vh)
    Am = _dot_nt(qb, kb) * Dm
    qC = _dot(qb, _bf(Ch))
    num = g * qC + _dot(_bf(Am), vb)
    qn = jnp.sum(qh * nh, axis=1, keepdims=True)
    den = g * qn + jnp.sum(Am, axis=1, keepdims=True)
    floor = jnp.exp(-m_t)
    dd = jnp.maximum(jnp.abs(den), floor)
    inv_dd = 1.0 / dd
    hh = num * inv_dd
    lane = lax.broadcasted_iota(jnp.int32, (1, LC), 1)
    blast = jnp.sum(jnp.where(lane == LC - 1, bs, 0.0), axis=1, keepdims=True)
    log_s = blast - bt + i_col
    m_new = jnp.maximum(blast + m_prev, jnp.max(log_s, axis=0, keepdims=True))
    decay = jnp.exp(blast + m_prev - m_new)
    ws = jnp.exp(log_s - m_new)
    kw = kh * ws
    C_new = decay * Ch + _dot_tn(_bf(kw), vb)
    n_new = decay * nh + jnp.sum(kw, axis=0, keepdims=True)
    return dict(Dm=Dm, g=g, Am=Am, qC=qC, qn=qn, den=den, floor=floor, inv_dd=inv_dd, h=hh, decay=decay, ws=ws, kw=kw,
                C_new=C_new, n_new=n_new, m_new=m_new, qb=qb, kb=kb, vb=vb)


def _head_out(hh, mo_h, gn_h):
    r = lax.rsqrt(jnp.mean(hh * hh, axis=-1, keepdims=True) + EPS)
    hn = hh * r
    sg = _sigmoid(mo_h)
    return sg * (hn * gn_h), hn, r, sg


def _mlstm_fwd(mqk, mv, mo, gates, conv_w, conv_b, gate_b, gn, shards, dtypes):
    nblk = S // TB
    ncb = TB // LC
    nw = len(shards)

    def body(*refs):
        x_ref, v_ref, o_ref, g_ref, w_ref, b_ref, gb_ref, gn_ref = refs[:8]
        ins = refs[8:8 + nw]
        out_ref, cs_ref, ns_ref, ms_ref = refs[8 + nw:12 + nw]
        outs = refs[12 + nw:12 + 2 * nw]
        tail, Cst, nst, mst, qs, ks = refs[12 + 2 * nw:18 + 2 * nw]
        bufs = refs[18 + 2 * nw:18 + 3 * nw]
        ag_start, ag_forward, ag_finish = _gather_phases(ins, outs, bufs, *refs[18 + 3 * nw:])
        i = pl.program_id(0)
        pl.when(i == 0)(ag_start)
        pl.when(i == nblk // 2)(ag_forward)

        @pl.when(i == 0)
        def _():
            tail[...] = jnp.zeros_like(tail)
            Cst[...] = jnp.zeros_like(Cst)
            nst[...] = jnp.zeros_like(nst)
            mst[...] = jnp.zeros_like(mst)

        x = x_ref[...]
        xp = jnp.concatenate([tail[...], x], axis=0)
        tail[...] = x[TB - 8:TB, :]
        c, sg, _ = _conv_silu(xp, w_ref, b_ref, TB)
        y = c * sg
        qs[...] = y[:, 0:MW]
        ks[...] = y[:, MW:2 * MW] * (1.0 / math.sqrt(128.0))

        for cc in range(ncb):
            rows = slice(cc * LC, (cc + 1) * LC)
            G = g_ref[rows, :] + gb_ref[...]
            b_col, b_row, g_row, _, _ = _chunk_gates(G)
            cs_ref[cc] = Cst[...]
            ns_ref[cc] = nst[...]
            ms_ref[cc] = mst[...]
            for h in range(4):
                ln = slice(h * 128, (h + 1) * 128)
                m_prev = jnp.max(mst[0:1, ln], axis=1, keepdims=True)
                f = _mlstm_head(qs[rows, ln], ks[rows, ln], v_ref[rows, ln], G, b_col, b_row, g_row, h,
                                Cst[:, ln], nst[0:1, ln], m_prev)
                out, _, _, _ = _head_out(f["h"], o_ref[rows, ln], gn_ref[:, ln])
                out_ref[rows, ln] = out
                Cst[:, ln] = f["C_new"]
                nst[0:1, ln] = f["n_new"]
                mst[0:1, ln] = jnp.broadcast_to(f["m_new"], (1, 128))
        pl.when(i == nblk - 1)(ag_finish)

    row = lambda wd: pl.BlockSpec((TB, wd), lambda i: (i, 0))
    res = pl.pallas_call(
        body, name="mlstm_fwd", grid=(nblk,),
        in_specs=[row(1024), row(MW), row(MW), row(128), _cspec((4, 1024)), _cspec((1, 1024)), _cspec((1, 128)),
                  _cspec((1, MW))] + [VM] * nw,
        out_specs=[row(MW), pl.BlockSpec((ncb, 128, MW), lambda i: (i, 0, 0)),
                   pl.BlockSpec((ncb, 8, MW), lambda i: (i, 0, 0)), pl.BlockSpec((ncb, 8, MW), lambda i: (i, 0, 0))]
        + [ANY] * nw,
        out_shape=[jax.ShapeDtypeStruct((S, MW), F32), jax.ShapeDtypeStruct((S // LC, 128, MW), F32),
                   jax.ShapeDtypeStruct((S // LC, 8, MW), F32), jax.ShapeDtypeStruct((S // LC, 8, MW), F32)]
        + _gather_shapes(shards, dtypes),
        scratch_shapes=[pltpu.VMEM((8, 1024), F32), pltpu.VMEM((128, MW), F32), pltpu.VMEM((8, MW), F32),
                        pltpu.VMEM((8, MW), F32), pltpu.VMEM((TB, MW), F32), pltpu.VMEM((TB, MW), F32)]
        + _gather_scratch(shards, dtypes),
        compiler_params=_params(1),
    )(mqk, mv, mo, gates, conv_w, conv_b, gate_b, gn, *shards)
    return res[0], res[1], res[2], res[3], res[4:]


DM_V, DM_O, DM_G, DM_W = 1024, 1536, 2048, PW - 3 * AW


def _mlstm_bwd(mqk, mv, mo, gates, conv_w, conv_b, gate_b, gn, cs, ns, ms, dout, parts):
    nblk = S // TB
    ncb = TB // LC
    kscale = 1.0 / math.sqrt(128.0)
    nw = len(parts)

    def body(*refs):
        x_ref, xprev_ref, v_ref, o_ref, g_ref, w_ref, b_ref, gb_ref, gn_ref, cs_ref, ns_ref, ms_ref, do_ref = refs[:13]
        ins = refs[13:13 + nw]
        dm_ref, dw_ref, db_ref, dgn_ref, dgb_ref = refs[13 + nw:18 + nw]
        outs = refs[18 + nw:18 + 2 * nw]
        dCst, dnst, dyhead, qs, ks, dqk = refs[18 + 2 * nw:24 + 2 * nw]
        rs_start, rs_finish = _scatter_phases(ins, outs, *refs[24 + 2 * nw:])
        i = pl.program_id(0)
        blk = nblk - 1 - i
        pl.when(i == 0)(rs_start)

        @pl.when(i == 0)
        def _():
            dCst[...] = jnp.zeros_like(dCst)
            dnst[...] = jnp.zeros_like(dnst)
            dyhead[...] = jnp.zeros_like(dyhead)
            dw_ref[...] = jnp.zeros_like(dw_ref)
            db_ref[...] = jnp.zeros_like(db_ref)
            dgn_ref[...] = jnp.zeros_like(dgn_ref)
            dgb_ref[...] = jnp.zeros_like(dgb_ref)

        x = x_ref[...]
        xprev = jnp.where(blk == 0, 0.0, xprev_ref[...])
        xp = jnp.concatenate([xprev, x], axis=0)
        c, sg, taps = _conv_silu(xp, w_ref, b_ref, TB)
        y = c * sg
        qs[...] = y[:, 0:MW]
        ks[...] = y[:, MW:2 * MW] * kscale
        lane128 = lax.broadcasted_iota(jnp.int32, (LC, 128), 1)
        rowi = lax.broadcasted_iota(jnp.int32, (LC, 1), 0)
        ones = jnp.ones((LC, 128), F32)

        for cc in reversed(range(ncb)):
            rows = slice(cc * LC, (cc + 1) * LC)
            G = g_ref[rows, :] + gb_ref[...]
            b_col, b_row, g_row, _, triu = _chunk_gates(G)
            dB = jnp.zeros((LC, 128), F32)
            dI = jnp.zeros((LC, 128), F32)
            for h in range(4):
                ln = slice(h * 128, (h + 1) * 128)
                Ch = cs_ref[cc, :, ln]
                nh = ns_ref[cc, 0:1, ln]
                m_prev = jnp.max(ms_ref[cc, 0:1, ln], axis=1, keepdims=True)
                qh, kh, vh = qs[rows, ln], ks[rows, ln], v_ref[rows, ln]
                f = _mlstm_head(qh, kh, vh, G, b_col, b_row, g_row, h, Ch, nh, m_prev)
                hh, inv_dd, den, g, Am, Dm = f["h"], f["inv_dd"], f["den"], f["g"], f["Am"], f["Dm"]
                qb, kb, vb = f["qb"], f["kb"], f["vb"]
                gn_h = gn_ref[:, ln]
                _, hn, r, sgo = _head_out(hh, o_ref[rows, ln], gn_h)
                do = do_ref[rows, ln]
                hm = hn * gn_h
                dm_ref[rows, DM_O + h * 128:DM_O + (h + 1) * 128] = _bf(do * hm * sgo * (1.0 - sgo))
                dhm = do * sgo
                dgn_ref[:, ln] = dgn_ref[:, ln] + jnp.sum(dhm * hn, axis=0, keepdims=True)
                dhn = dhm * gn_h
                dh = r * (dhn - hn * jnp.mean(dhn * hn, axis=-1, keepdims=True))
                dnum = dh * inv_dd
                ddd = -jnp.sum(dh * hh, axis=1, keepdims=True) * inv_dd
                dden = jnp.where(jnp.abs(den) >= f["floor"], ddd * jnp.sign(den), 0.0)
                dnb = _bf(dnum)
                dA = _dot_nt(dnb, vb) + dden
                dv = _dot_tn(_bf(Am), dnb)
                gd = _bf(g * dnum)
                gq = g * dden
                dq = _dot_nt(gd, _bf(Ch)) + gq * nh
                dCn = dCst[:, ln]
                dnn = dnst[0:1, ln]
                dC = f["decay"] * dCn + _dot_tn(qb, gd)
                dn = f["decay"] * dnn + jnp.sum(gq * qh, axis=0, keepdims=True)
                dg = jnp.sum(dnum * f["qC"], axis=1, keepdims=True) + dden * f["qn"]
                dS = _bf(dA * Dm)
                dq = dq + _dot(dS, kb)
                dk = _dot_tn(dS, qb)
                Gm = dA * Am
                gam = dg * g
                dCb = _bf(dCn)
                E = _dot_nt(vb, dCb) + dnn
                ws = f["ws"]
                dk = dk + ws * E
                om = jnp.sum(E * kh, axis=1, keepdims=True) * ws
                dv = dv + _dot(_bf(f["kw"]), dCb)
                ddecay = (jnp.sum(jnp.sum(dCn * Ch, axis=1, keepdims=True), axis=0, keepdims=True)
                          + jnp.sum(dnn * nh, axis=1, keepdims=True))
                delta = ddecay * f["decay"]
                rows_g = jnp.sum(Gm, axis=1, keepdims=True)
                cols_g = jnp.broadcast_to(jnp.sum(Gm, axis=0, keepdims=True), (LC, 128)).T
                last = jnp.where(rowi == LC - 1, jnp.sum(om, axis=0, keepdims=True) + delta, 0.0)
                db = rows_g + gam - om + last - cols_g
                di = cols_g + om
                dB = jnp.where(lane128 == 4 + h, db, dB)
                dI = jnp.where(lane128 == h, di, dI)
                dCst[:, ln] = dC
                dnst[0:1, ln] = dn
                dqk[rows, ln] = dq
                dqk[rows, MW + h * 128:MW + (h + 1) * 128] = dk * kscale
                dm_ref[rows, DM_V + h * 128:DM_V + (h + 1) * 128] = _bf(dv)
            dlogf = jnp.dot(triu, dB, preferred_element_type=F32, precision=HI)
            dG = dI + dlogf * _sigmoid(-G)
            dG = jnp.where(lane128 < 8, dG, 0.0)
            dm_ref[rows, DM_G:DM_G + 128] = _bf(dG)
            dm_ref[rows, DM_G + 128:DM_W] = jnp.zeros((LC, DM_W - DM_G - 128), BF16)
            dgb_ref[...] = dgb_ref[...] + jnp.sum(dG, axis=0, keepdims=True)

        dy = dqk[...] * (sg * (1.0 + c * (1.0 - sg)))
        db_ref[...] = db_ref[...] + jnp.sum(dy, axis=0, keepdims=True)
        for j in range(4):
            dw_ref[j:j + 1, :] = dw_ref[j:j + 1, :] + jnp.sum(dy * taps[j], axis=0, keepdims=True)
        dyp = jnp.concatenate([dy, dyhead[...]], axis=0)
        dx = w_ref[3:4, :] * dy
        for j in range(3):
            dx = dx + w_ref[j:j + 1, :] * pltpu.roll(dyp, TB + 8 - (3 - j), 0)[0:TB]
        dm_ref[:, 0:DM_V] = _bf(dx)
        dyhead[...] = dy[0:8, :]
        pl.when(i == nblk - 1)(rs_finish)

    rrow = lambda wd: pl.BlockSpec((TB, wd), lambda i: (nblk - 1 - i, 0))
    st = lambda r: pl.BlockSpec((ncb, r, MW), lambda i: (nblk - 1 - i, 0, 0))
    prev8 = pl.BlockSpec((8, 1024), lambda i: (jnp.maximum((nblk - 1 - i) * (TB // 8) - 1, 0), 0))
    res = pl.pallas_call(
        body, name="mlstm_bwd", grid=(nblk,),
        in_specs=[rrow(1024), prev8, rrow(MW), rrow(MW), rrow(128), _cspec((4, 1024)), _cspec((1, 1024)),
                  _cspec((1, 128)), _cspec((1, MW)), st(128), st(8), st(8), rrow(MW)] + [ANY] * nw,
        out_specs=[rrow(DM_W),
                   pl.BlockSpec((4, 1024), lambda i: (0, 0)), pl.BlockSpec((1, 1024), lambda i: (0, 0)),
                   pl.BlockSpec((1, MW), lambda i: (0, 0)), pl.BlockSpec((1, 128), lambda i: (0, 0))] + [ANY] * nw,
        out_shape=[jax.ShapeDtypeStruct((S, DM_W), BF16),
                   jax.ShapeDtypeStruct((4, 1024), F32), jax.ShapeDtypeStruct((1, 1024), F32),
                   jax.ShapeDtypeStruct((1, MW), F32), jax.ShapeDtypeStruct((1, 128), F32)]
        + [jax.ShapeDtypeStruct(a.shape, a.dtype) for a in parts],
        scratch_shapes=[pltpu.VMEM((128, MW), F32), pltpu.VMEM((8, MW), F32), pltpu.VMEM((8, 1024), F32),
                        pltpu.VMEM((TB, MW), F32), pltpu.VMEM((TB, MW), F32), pltpu.VMEM((TB, 1024), F32)]
        + _scatter_scratch(nw),
        compiler_params=_params(1),
    )(mqk, mqk, mv, mo, gates, conv_w, conv_b, gate_b, gn, cs, ns, ms, dout, *parts)
    return res[:5], res[5:]


def _out_proj(x, attn, ml, w, g):
    tm = TM

    def body(x_ref, a_ref, m_ref, w_ref, g_ref, h_ref, u_ref):
        h1 = x_ref[...] + _dot(_bf(a_ref[...]), w_ref[0:AW, :]) + _dot(_bf(m_ref[...]), w_ref[AW:D, :])
        h_ref[...] = h1
        n, _ = _rms(h1)
        u_ref[...] = _bf(n * g_ref[...])

    row = lambda wd: pl.BlockSpec((tm, wd), lambda i: (i, 0))
    return pl.pallas_call(
        body, name="out_proj", grid=(S // tm,),
        in_specs=[row(D), row(AW), row(MW), _cspec((D, D)), _cspec((1, D))],
        out_specs=[row(D), row(D)],
        out_shape=[jax.ShapeDtypeStruct((S, D), F32), jax.ShapeDtypeStruct((S, D), BF16)],
        compiler_params=_params(1),
    )(x, attn, ml, w, g)


HALF = DFF // NDEV // 2


def _mlp_fwd(h1, u2, w_up, w_down_a, w_down_b):
    tm = TM

    def body(h_ref, u_ref, wu_ref, wa_ref, wb_ref, a_ref, o_ref):
        u = u_ref[...]
        acc = h_ref[...]
        for c in range(NDEV):
            cols = slice(c * 512, (c + 1) * 512)
            a = _dot(u, wu_ref[c])
            a_ref[:, cols] = _bf(a)
            r = jnp.maximum(a, 0.0)
            r = _bf(r * r)
            acc = acc + _dot(r[:, 0:HALF], wa_ref[c]) + _dot(r[:, HALF:2 * HALF], wb_ref[c])
        o_ref[...] = acc

    row = lambda wd: pl.BlockSpec((tm, wd), lambda i: (i, 0))
    return pl.pallas_call(
        body, name="mlp_fwd", grid=(S // tm,),
        in_specs=[row(D), row(D), _cspec((NDEV, D, DFF // NDEV)), _cspec((NDEV, HALF, D)), _cspec((NDEV, HALF, D))],
        out_specs=[row(DFF), row(D)],
        out_shape=[jax.ShapeDtypeStruct((S, DFF), BF16), jax.ShapeDtypeStruct((S, D), F32)],
        compiler_params=_params(1),
    )(h1, u2, w_up, w_down_a, w_down_b)


def _ple_loss(h2, p, target, w_pg, w_ple, g_ple, g_fin):
    tm = TM

    def body(h_ref, p_ref, t_ref, wg_ref, wp_ref, gp_ref, gf_ref,
             dh_ref, dwg_ref, dwp_ref, dgp_ref, dgf_ref, loss_ref, acc_g, acc_p):
        i = pl.program_id(0)

        @pl.when(i == 0)
        def _():
            acc_g[...] = jnp.zeros_like(acc_g)
            acc_p[...] = jnp.zeros_like(acc_p)
            dgp_ref[...] = jnp.zeros_like(dgp_ref)
            dgf_ref[...] = jnp.zeros_like(dgf_ref)
            loss_ref[...] = jnp.zeros_like(loss_ref)

        h2v = h_ref[...]
        n2, rs2 = _rms(h2v)
        u3 = _bf(n2 * gp_ref[...])
        gt = _sigmoid(_dot(u3, wg_ref[...]))
        pb = _bf(p_ref[...])
        e = jnp.concatenate([_dot(pb, wp_ref[j]) for j in range(NDEV)], axis=1)
        h3 = h2v + gt * e
        n3, rs3 = _rms(h3)
        err = n3 * gf_ref[...] - t_ref[...]
        loss_ref[...] = loss_ref[...] + 0.5 / D * jnp.sum(jnp.sum(err * err, axis=1, keepdims=True), axis=0, keepdims=True)
        dy = err * (1.0 / D)
        dgf_ref[...] = dgf_ref[...] + jnp.sum(dy * n3, axis=0, keepdims=True)
        dh3 = _rms_bwd(dy, n3, rs3, gf_ref[...])
        de = _bf(dh3 * gt)
        dz = _bf(dh3 * e * gt * (1.0 - gt))
        acc_p[...] = acc_p[...] + _dot_tn(pb, de)
        acc_g[...] = acc_g[...] + _dot_tn(u3, dz)
        du3 = _dot_nt(dz, wg_ref[...])
        dgp_ref[...] = dgp_ref[...] + jnp.sum(du3 * n2, axis=0, keepdims=True)
        dh_ref[...] = dh3 + _rms_bwd(du3, n2, rs2, gp_ref[...])

        @pl.when(i == S // tm - 1)
        def _():
            dwg_ref[...] = _bf(acc_g[...])
            for j in range(NDEV):
                dwp_ref[j] = _bf(acc_p[:, j * 128:(j + 1) * 128])

    row = lambda wd: pl.BlockSpec((tm, wd), lambda i: (i, 0))
    whole = lambda shp: pl.BlockSpec(shp, lambda i: (0,) * len(shp))
    return pl.pallas_call(
        body, name="ple_loss", grid=(S // tm,),
        in_specs=[row(D), row(PLE), row(D), _cspec((D, D)), _cspec((NDEV, PLE, 128)), _cspec((1, D)), _cspec((1, D))],
        out_specs=[row(D), whole((D, D)), whole((NDEV, PLE, 128)), whole((1, D)), whole((1, D)), whole((1, 1))],
        out_shape=[jax.ShapeDtypeStruct((S, D), F32), jax.ShapeDtypeStruct((D, D), BF16),
                   jax.ShapeDtypeStruct((NDEV, PLE, 128), BF16), jax.ShapeDtypeStruct((1, D), F32),
                   jax.ShapeDtypeStruct((1, D), F32), jax.ShapeDtypeStruct((1, 1), F32)],
        scratch_shapes=[pltpu.VMEM((D, D), F32), pltpu.VMEM((PLE, D), F32)],
        compiler_params=_params(1),
    )(h2, p, target, w_pg, w_ple, g_ple, g_fin)


def _mlp_bwd(dh2, a, h1, g, w_up, w_down_a, w_down_b):
    tm = TM

    def body(d_ref, a_ref, h_ref, g_ref, wu_ref, wa_ref, wb_ref, da_ref, dh1_ref, dg_ref):
        @pl.when(pl.program_id(0) == 0)
        def _():
            dg_ref[...] = jnp.zeros_like(dg_ref)

        dh2v = d_ref[...]
        db = _bf(dh2v)
        du = jnp.zeros((tm, D), F32)
        for c in range(NDEV):
            cols = slice(c * 512, (c + 1) * 512)
            dr = jnp.concatenate([_dot_nt(db, wa_ref[c]), _dot_nt(db, wb_ref[c])], axis=1)
            da = _bf(dr * (2.0 * jnp.maximum(a_ref[:, cols], 0.0)))
            da_ref[:, cols] = da
            du = du + _dot_nt(da, wu_ref[c])
        n, rs = _rms(h_ref[...])
        dg_ref[...] = dg_ref[...] + jnp.sum(du * n, axis=0, keepdims=True)
        dh1_ref[...] = dh2v + _rms_bwd(du, n, rs, g_ref[...])

    row = lambda wd: pl.BlockSpec((tm, wd), lambda i: (i, 0))
    return pl.pallas_call(
        body, name="mlp_bwd", grid=(S // tm,),
        in_specs=[row(D), row(DFF), row(D), _cspec((1, D)), _cspec((NDEV, D, DFF // NDEV)), _cspec((NDEV, HALF, D)),
                  _cspec((NDEV, HALF, D))],
        out_specs=[row(DFF), row(D), pl.BlockSpec((1, D), lambda i: (0, 0))],
        out_shape=[jax.ShapeDtypeStruct((S, DFF), BF16), jax.ShapeDtypeStruct((S, D), F32),
                   jax.ShapeDtypeStruct((1, D), F32)],
        compiler_params=_params(1),
    )(dh2, a, h1, g, w_up, w_down_a, w_down_b)


def _out_proj_bwd(dh1, attn, ml, w):
    tm = TM

    def body(d_ref, a_ref, m_ref, w_ref, da_ref, dm_ref, dw_ref, acc):
        i = pl.program_id(0)

        @pl.when(i == 0)
        def _():
            acc[...] = jnp.zeros_like(acc)

        db = _bf(d_ref[...])
        dmix = _dot_nt(db, w_ref[...])
        da_ref[...] = dmix[:, 0:AW]
        dm_ref[...] = dmix[:, AW:D]
        acc[0:AW, :] = acc[0:AW, :] + _dot_tn(_bf(a_ref[...]), db)
        acc[AW:D, :] = acc[AW:D, :] + _dot_tn(_bf(m_ref[...]), db)

        @pl.when(i == S // tm - 1)
        def _():
            dw_ref[...] = _bf(acc[...])

    row = lambda wd: pl.BlockSpec((tm, wd), lambda i: (i, 0))
    return pl.pallas_call(
        body, name="out_proj_bwd", grid=(S // tm,),
        in_specs=[row(D), row(AW), row(MW), _cspec((D, D))],
        out_specs=[row(AW), row(MW), pl.BlockSpec((D, D), lambda i: (0, 0))],
        out_shape=[jax.ShapeDtypeStruct((S, AW), F32), jax.ShapeDtypeStruct((S, MW), F32),
                   jax.ShapeDtypeStruct((D, D), BF16)],
        scratch_shapes=[pltpu.VMEM((D, D), F32)],
        compiler_params=_params(1),
    )(dh1, attn, ml, w)


CHIP_FLIPS = [(0, 0), (0, 1), (1, 0), (1, 1)]


def _scatter2_phases(in_ref, out_ref, mine_v, sib_v, psum_v, loc_sems, d2d_send, d2d_recv, ici_send, ici_recv, own_sem):
    x, y, c = _place()
    chips = [((x + dx) % 2, (y + dy) % 2) for dx, dy in CHIP_FLIPS]
    nc = len(chips)

    def local(k):
        return pltpu.make_async_copy(in_ref.at[_dev_index(*chips[k], c)], mine_v.at[k], loc_sems.at[k])

    def to_sib(k):
        return pltpu.make_async_remote_copy(
            src_ref=in_ref.at[_dev_index(*chips[k], 1 - c)], dst_ref=sib_v.at[k], send_sem=d2d_send.at[k],
            recv_sem=d2d_recv.at[k], device_id=(x, y, 1 - c), device_id_type=MESH)

    def over_ici(k):
        return pltpu.make_async_remote_copy(
            src_ref=psum_v.at[k], dst_ref=out_ref.at[k], send_sem=ici_send.at[k - 1], recv_sem=ici_recv.at[k - 1],
            device_id=(*chips[k], c), device_id_type=MESH)

    def own():
        return pltpu.make_async_copy(psum_v.at[0], out_ref.at[0], own_sem)

    def start():
        for k in range(nc):
            to_sib(k).start()
            local(k).start()

    def middle():
        for k in (1, 2, 3, 0):
            local(k).wait()
            to_sib(k).wait_recv()
            psum_v[k] = _bf(mine_v[k].astype(F32) + sib_v[k].astype(F32))
            (over_ici(k) if k else own()).start()

    def finish():
        for k in range(1, nc):
            over_ici(k).wait()
        for k in range(nc):
            to_sib(k).wait_send()
        own().wait()

    return start, middle, finish


def _scatter2_scratch(shard, dtype):
    nc = len(CHIP_FLIPS)
    return ([pltpu.VMEM((nc, *shard), dtype)] * 3
            + [pltpu.SemaphoreType.DMA((nc,))] * 3 + [pltpu.SemaphoreType.DMA((nc - 1,))] * 2 + [pltpu.SemaphoreType.DMA])


def _in_proj_bwd(dparts, n_roped, rope, dh1, x, g1, w, part):
    tm = TM
    nt = S // tm
    widths = [d.shape[1] for d in dparts]
    assert sum(widths) == PW
    npar = len(dparts)

    def body(*refs):
        d_refs = refs[:npar]
        tabs = [t[...] for t in refs[npar:npar + 3]]
        dh_ref, x_ref, g_ref, w_ref, in_ref, dx_ref, dgsum_ref, out_ref = refs[npar + 3:npar + 11]
        rs_start, rs_middle, rs_finish = _scatter2_phases(in_ref, out_ref, *refs[npar + 11:npar + 20])
        dg_ref = refs[npar + 20]
        ar_start, ar_finish = _small_phases([dg_ref], dgsum_ref, *refs[npar + 21:])
        i = pl.program_id(0)
        pl.when(i == 0)(rs_start)
        pl.when(i == 1)(rs_middle)

        @pl.when(i == 0)
        def _():
            dg_ref[...] = jnp.zeros_like(dg_ref)

        du = jnp.zeros((tm, D), F32)
        off = 0
        for j, (d_ref, wd) in enumerate(zip(d_refs, widths)):
            nc = next(c for c in (768, 512) if wd % c == 0)
            for s in range(wd // nc):
                d = d_ref[:, s * nc:(s + 1) * nc]
                du = du + _dot_nt(_unrope(d, *tabs) if j < n_roped else d, w_ref[:, off + s * nc:off + (s + 1) * nc])
            off += wd
        n, rs = _rms(x_ref[...])
        dg_ref[...] = dg_ref[...] + jnp.sum(du * n, axis=0, keepdims=True)
        dx_ref[...] = dh_ref[...] + _rms_bwd(du, n, rs, g_ref[...])

        @pl.when(i == nt - 1)
        def _():
            ar_start()
            rs_finish()
            ar_finish()

    row = lambda wd: pl.BlockSpec((tm, wd), lambda i: (i, 0))
    shard = part.shape[1:]
    return pl.pallas_call(
        body, name="in_proj_bwd", grid=(nt,),
        in_specs=[row(wd) for wd in widths] + [row(128)] * 3 + [row(D), row(D), _cspec((1, D)), _cspec((D, PW)), ANY],
        out_specs=[row(D), VM, ANY],
        out_shape=[jax.ShapeDtypeStruct((S, D), F32), jax.ShapeDtypeStruct((8, 1024), F32),
                   jax.ShapeDtypeStruct((len(CHIP_FLIPS), *shard), part.dtype)],
        scratch_shapes=_scatter2_scratch(shard, part.dtype)
        + [pltpu.VMEM((1, D), F32), pltpu.VMEM((8, 1024), F32), pltpu.VMEM((NDEV, 8, 1024), F32),
           pltpu.SemaphoreType.DMA((7,)), pltpu.SemaphoreType.DMA((7,))],
        compiler_params=_params(1),
    )(*dparts, *rope, dh1, x, g1, w, part)


SMALL_ROWS = 96


def _small_phases(ins, out_ref, pack, rbuf, send_sems, recv_sems):
    x, y, c = _place()
    me = _dev_index(x, y, c)

    def copies():
        out = []
        for k, (dx, dy, dc) in enumerate(FLIPS):
            peer = ((x + dx) % 2, (y + dy) % 2, (c + dc) % 2)
            out.append(pltpu.make_async_remote_copy(
                src_ref=pack, dst_ref=rbuf.at[me], send_sem=send_sems.at[k], recv_sem=recv_sems.at[k],
                device_id=peer, device_id_type=MESH))
        return out

    def start():
        pack[...] = jnp.zeros_like(pack)
        for i, ref in enumerate(ins):
            pack[8 * i:8 * i + 1, 0:ref.shape[1]] = ref[...]
        rbuf[me] = pack[...]
        for cp in copies():
            cp.start()

    def finish():
        for cp in copies():
            cp.wait()
        tot = rbuf[0]
        for j in range(1, NDEV):
            tot = tot + rbuf[j]
        out_ref[...] = tot

    return start, finish


def _wgrad(name, A, Bs, a_fn, b_fn, out_shape, split=None, ts=512, small=(), rope=(), n_roped=0):
    K = A.shape[1]
    widths = [b.shape[1] for b in Bs]
    N = sum(widths)
    nb, ns, nrt = len(Bs) + len(rope), len(small), S // ts
    kc = min(K, 1024)

    def body(*refs):
        a_ref, b_refs = refs[0], refs[1:1 + len(Bs)]
        tabs = [t[...] for t in refs[1 + len(Bs):1 + nb]]
        o_ref = refs[1 + nb + ns]
        acc = refs[2 + nb + ns + bool(ns)]
        r = pl.program_id(0)
        if ns:
            sm_start, sm_finish = _small_phases(refs[1 + nb:1 + nb + ns], refs[2 + nb + ns], *refs[4 + nb + ns:])
            pl.when(r == 0)(sm_start)

        @pl.when(r == 0)
        def _():
            acc[...] = jnp.zeros_like(acc)

        bs, off = [], 0
        for i, (b_ref, w) in enumerate(zip(b_refs, widths)):
            nc = next(c for c in (1024, 768, 512) if w % c == 0)
            fn = (lambda t: _unrope(t, *tabs)) if i < n_roped else b_fn
            bs += [(off + c * nc, nc, fn(b_ref[:, c * nc:(c + 1) * nc])) for c in range(w // nc)]
            off += w
        for kk in range(K // kc):
            rows = slice(kk * kc, (kk + 1) * kc)
            at = a_fn(a_ref[:, rows]).T
            for lo, nc, b in bs:
                acc[rows, lo:lo + nc] = acc[rows, lo:lo + nc] + _dot(at, b)

        @pl.when(r == nrt - 1)
        def _():
            if split is None:
                o_ref[...] = _bf(acc[...])
            else:
                for j in range(NDEV):
                    o_ref[j] = _bf(acc[:, split * j:split * (j + 1)])

        if ns:
            pl.when(r == nrt - 1)(sm_finish)

    in_specs = ([pl.BlockSpec((ts, K), lambda r: (r, 0))] + [pl.BlockSpec((ts, w), lambda r: (r, 0)) for w in widths]
                + [pl.BlockSpec((ts, 128), lambda r: (r, 0))] * len(rope))
    out_spec = pl.BlockSpec(out_shape, lambda r: (0,) * len(out_shape))
    scratch = [pltpu.VMEM((K, N), F32)]
    if not ns:
        return pl.pallas_call(
            body, name=name, grid=(nrt,), in_specs=in_specs, out_specs=out_spec,
            out_shape=jax.ShapeDtypeStruct(out_shape, BF16), scratch_shapes=scratch, compiler_params=_params(1),
        )(A, *Bs, *rope)
    return pl.pallas_call(
        body, name=name, grid=(nrt,), in_specs=in_specs + [VM] * ns, out_specs=[out_spec, VM],
        out_shape=[jax.ShapeDtypeStruct(out_shape, BF16), jax.ShapeDtypeStruct((SMALL_ROWS, 1024), F32)],
        scratch_shapes=scratch + [pltpu.VMEM((SMALL_ROWS, 1024), F32), pltpu.VMEM((NDEV, SMALL_ROWS, 1024), F32),
                                  pltpu.SemaphoreType.DMA((7,)), pltpu.SemaphoreType.DMA((7,))],
        compiler_params=_params(1),
    )(A, *Bs, *rope, *small)


def _relu2_bf(a):
    r = jnp.maximum(a.astype(F32), 0.0)
    return _bf(r * r)


def _ident(a):
    return a


def _step(x, p, target, g1, conv_b, gate_b, gn, g_mlp, g_ple, g_fin, sh):
    (g_in, g_conv), (rc, ra, rb) = _gather_weights([sh["w_in"], sh["conv_w"]], [BF16, F32])
    conv_w = g_conv.transpose(1, 0, 2).reshape(4, 1024)
    w_in_p = _join_w_in(g_in)
    (qkv, mqk, mv, mo, gates, u1), (w_out8, w_pg8, w_ple8) = _in_proj(
        x, g1, w_in_p, rc, ra, rb, [sh["w_out"], sh["w_ple_gate"], sh["w_ple"]], [BF16] * 3)
    attn, lse, (w_up8, w_down_a) = _attn_fwd(qkv, [sh["w_up"], sh["w_down"][0:HALF]], [BF16] * 2)
    ml, cs, ns, ms, (w_down_b,) = _mlstm_fwd(mqk, mv, mo, gates, conv_w, conv_b, gate_b, gn,
                                             [sh["w_down"][HALF:2 * HALF]], [BF16])
    w_out, w_pg = w_out8.reshape(D, D), w_pg8.reshape(D, D)
    h1, u2 = _out_proj(x, attn, ml, w_out, g_mlp)
    a, h2 = _mlp_fwd(h1, u2, w_up8, w_down_a, w_down_b)
    dh2, dw_pg, dw_ple8, dg_ple, dg_fin, loss = _ple_loss(h2, p, target, w_pg, w_ple8, g_ple, g_fin)
    da, dh1, dg_mlp = _mlp_bwd(dh2, a, h1, g_mlp, w_up8, w_down_a, w_down_b)
    dw_up8 = _wgrad("wgrad_up", u2, [da], _ident, _ident, (NDEV, D, DFF // NDEV), split=DFF // NDEV)
    dw_down = _wgrad("wgrad_down", a, [dh2], _relu2_bf, _bf, (DFF, D))
    d_attn, d_ml, dw_out = _out_proj_bwd(dh1, attn, ml, w_out)
    (dm, dconv_w, dconv_b, dgn, dgate_b), (r_down,) = _mlstm_bwd(
        mqk, mv, mo, gates, conv_w, conv_b, gate_b, gn, cs, ns, ms, d_ml, [dw_down.reshape(NDEV, DFF // NDEV, D)])
    dq, dk, dv, (r_up, r_out, r_pg, r_ple) = _attn_bwd(
        qkv, attn, lse, d_attn,
        [dw_up8, dw_out.reshape(NDEV, D // NDEV, D), dw_pg.reshape(NDEV, D // NDEV, D), dw_ple8])
    dparts = [dq, dk, dv, dm]
    small = [jnp.zeros((1, D), F32), dconv_b, dgate_b, dgn, dg_mlp, dg_ple, dg_fin, loss]
    dw_in8, total = _wgrad("wgrad_in", u1, dparts, _ident, _ident, (NDEV, D, IN_W // NDEV), split=IN_W // NDEV,
                           small=small + [dconv_w[j:j + 1] for j in range(4)], rope=(rc, ra, rb), n_roped=2)
    dx, dg1_sum, r_in = _in_proj_bwd(dparts, 2, (rc, ra, rb), dh1, x, g1, w_in_p, dw_in8)
    recv = dict(w_in=r_in, w_out=r_out, w_up=r_up, w_down=r_down, w_ple_gate=r_pg, w_ple=r_ple)
    return dx, recv, total, dg1_sum


def _gather_weights(shards, dtypes):
    nw = len(shards)

    def body(*refs):
        ins, parts = refs[:nw], refs[nw:nw + 4]
        outs, tables = refs[nw + 4:2 * nw + 4], refs[2 * nw + 4:2 * nw + 7]
        start, forward, finish = _gather_phases(ins, outs, refs[2 * nw + 7:3 * nw + 7], *refs[3 * nw + 7:])
        start()
        _rope_fill(*parts, *tables)
        forward()
        finish()

    res = pl.pallas_call(
        body, name="gather_weights",
        in_specs=[VM] * (nw + 4), out_specs=[ANY] * nw + [VM] * 3,
        out_shape=_gather_shapes(shards, dtypes) + [jax.ShapeDtypeStruct((S, 128), F32)] * 3,
        scratch_shapes=_gather_scratch(shards, dtypes),
        compiler_params=_params(),
    )(*shards, *_rope_parts())
    return res[:nw], res[nw:]


ADAM_STEPS = 4


def _adamw(items):
    n = len(items)

    def body(*refs):
        for i in range(n):
            g_ref, w_ref, m_ref, v_ref = refs[4 * i:4 * i + 4]
            go_ref, d_ref, mo_ref, vo_ref = refs[4 * n + 4 * i:4 * n + 4 * i + 4]
            g = g_ref[0].astype(F32)
            for j in range(1, g_ref.shape[0]):
                g = g + g_ref[j].astype(F32)
            go_ref[...] = g
            d_ref[...], mo_ref[...], vo_ref[...] = _adam_update(g, w_ref[...], m_ref[...], v_ref[...])

    in_specs, out_specs, out_shape, args = [], [], [], []
    for gparts, w, m, v in items:
        P, R, C = gparts.shape
        if R % (8 * ADAM_STEPS) == 0:
            tr = R // ADAM_STEPS
            row, gspec = pl.BlockSpec((tr, C), lambda i: (i, 0)), pl.BlockSpec((P, tr, C), lambda i: (0, i, 0))
        else:
            row, gspec = pl.BlockSpec((R, C), lambda i: (0, 0)), pl.BlockSpec((P, R, C), lambda i: (0, 0, 0))
        in_specs += [gspec, row, row, row]
        out_specs += [row] * 4
        out_shape += [jax.ShapeDtypeStruct((R, C), F32)] * 4
        args += [gparts, w, m, v]
    res = pl.pallas_call(
        body, name="adamw", grid=(ADAM_STEPS,), in_specs=in_specs, out_specs=out_specs, out_shape=out_shape,
        compiler_params=_params(1),
    )(*args)
    return [res[4 * i:4 * i + 4] for i in range(n)]


SMALL = ("norm_mix_g", "conv_b", "gate_b", "mlstm_norm_g", "norm_mlp_g", "norm_ple_g", "final_norm_g")


def _adam_update(g, w, m, v):
    c1 = 1.0 - ADAM_B1 ** ADAM_STEP
    c2 = 1.0 - ADAM_B2 ** ADAM_STEP
    m2 = ADAM_B1 * m + (1.0 - ADAM_B1) * g
    v2 = ADAM_B2 * v + (1.0 - ADAM_B2) * (g * g)
    return -ADAM_LR * ((m2 / c1) / (jnp.sqrt(v2 / c2) + ADAM_EPS) + ADAM_WD * w), m2, v2


def _adamw_small(total, first, ws, ms, vs):
    n = len(ws)

    def body(*refs):
        t_ref, f_ref = refs[:2]
        refs = refs[1:]
        outs = refs[1 + 3 * n:]
        for i in range(n):
            w_ref, m_ref, v_ref = refs[1 + i], refs[1 + n + i], refs[1 + 2 * n + i]
            g = (t_ref if i else f_ref)[8 * i:8 * i + 1, 0:w_ref.shape[1]]
            delta, m2, v2 = _adam_update(g, w_ref[...], m_ref[...], v_ref[...])
            for ref, val in zip(outs[4 * i:4 * i + 4], (g, delta, m2, v2)):
                ref[...] = val

    res = pl.pallas_call(
        body, name="adamw_small",
        out_shape=[jax.ShapeDtypeStruct(w.shape, F32) for w in ws for _ in range(4)],
        compiler_params=_params(),
    )(total, first, *ws, *ms, *vs)
    return [res[4 * i:4 * i + 4] for i in range(n)]


def kernel(x, p, norm_mix_g, w_in, conv_w, conv_b, gate_b, mlstm_norm_g, w_out, norm_mlp_g, w_up, w_down, norm_ple_g, w_ple_gate, w_ple, final_norm_g, loss_target, m_norm_mix_g, m_w_in, m_conv_w, m_conv_b, m_gate_b, m_mlstm_norm_g, m_w_out, m_norm_mlp_g, m_w_up, m_w_down, m_norm_ple_g, m_w_ple_gate, m_w_ple, m_final_norm_g, v_norm_mix_g, v_w_in, v_conv_w, v_conv_b, v_gate_b, v_mlstm_norm_g, v_w_out, v_norm_mlp_g, v_w_up, v_w_down, v_norm_ple_g, v_w_ple_gate, v_w_ple, v_final_norm_g):
    big_names = ("w_in", "conv_w", "w_out", "w_up", "w_down", "w_ple_gate", "w_ple")
    wts = dict(w_in=w_in, conv_w=conv_w, w_out=w_out, w_up=w_up, w_down=w_down, w_ple_gate=w_ple_gate, w_ple=w_ple)
    mom = dict(w_in=m_w_in, conv_w=m_conv_w, w_out=m_w_out, w_up=m_w_up, w_down=m_w_down, w_ple_gate=m_w_ple_gate,
               w_ple=m_w_ple)
    var = dict(w_in=v_w_in, conv_w=v_conv_w, w_out=v_w_out, w_up=v_w_up, w_down=v_w_down, w_ple_gate=v_w_ple_gate,
               w_ple=v_w_ple)
    sq = lambda a: a.reshape(a.shape[1:])
    fin = final_norm_g.reshape(1, D)
    dx, recv, total, first = _step(
        x[0], p[0, 0], loss_target[0], norm_mix_g, conv_b, jnp.pad(gate_b, ((0, 0), (0, 120))), mlstm_norm_g,
        norm_mlp_g, norm_ple_g, fin, {n: sq(wts[n]) for n in big_names})

    nrow = 8 * len(SMALL)
    me = _dev_index(*_place())
    conv_rows = total[nrow + 8:nrow + 40:8]
    recv["conv_w"] = lax.dynamic_slice_in_dim(conv_rows, me * 128, 128, axis=1).reshape(1, 4, 128)
    out = {}
    for n, res in zip(big_names, _adamw([(recv[n], sq(wts[n]), sq(mom[n]), sq(var[n])) for n in big_names])):
        out[n] = [t.reshape(wts[n].shape) for t in res]
    sw = dict(norm_mix_g=norm_mix_g, conv_b=conv_b, gate_b=gate_b, mlstm_norm_g=mlstm_norm_g, norm_mlp_g=norm_mlp_g,
              norm_ple_g=norm_ple_g, final_norm_g=fin)
    sm = dict(norm_mix_g=m_norm_mix_g, conv_b=m_conv_b, gate_b=m_gate_b, mlstm_norm_g=m_mlstm_norm_g,
              norm_mlp_g=m_norm_mlp_g, norm_ple_g=m_norm_ple_g, final_norm_g=m_final_norm_g.reshape(1, D))
    sv = dict(norm_mix_g=v_norm_mix_g, conv_b=v_conv_b, gate_b=v_gate_b, mlstm_norm_g=v_mlstm_norm_g,
              norm_mlp_g=v_norm_mlp_g, norm_ple_g=v_norm_ple_g, final_norm_g=v_final_norm_g.reshape(1, D))
    res = _adamw_small(total, first, [sw[n] for n in SMALL], [sm[n] for n in SMALL], [sv[n] for n in SMALL])
    for n, r in zip(SMALL, res):
        out[n] = [t.reshape(final_norm_g.shape) for t in r] if n == "final_norm_g" else list(r)
    order = ("norm_mix_g", "w_in", "conv_w", "conv_b", "gate_b", "mlstm_norm_g", "w_out", "norm_mlp_g", "w_up", "w_down",
             "norm_ple_g", "w_ple_gate", "w_ple", "final_norm_g")
    loss_all = total[nrow, 0]
    return (loss_all, dx[None], *[out[n][0] for n in order], *[out[n][1] for n in order],
            *[out[n][2] for n in order], *[out[n][3] for n in order])
```

```python
import math

import jax
import jax.numpy as jnp
from jax import lax
from jax.experimental import pallas as pl
from jax.experimental.pallas import tpu as pltpu

F32, BF16 = jnp.float32, jnp.bfloat16
S = 4096
D = 1024
AW = 512
MW = 512
DFF = 4096
PLE = 256
IN_W = 3592
PW = 3840
NDEV = 8
EPS = 1e-6
NEG = -1e30
LC = 128
TB = 256
ROPE_THETA = 500000.0
VMEM_LIMIT = 56 * 1024 * 1024
HI = lax.Precision.HIGHEST

ADAM_LR, ADAM_B1, ADAM_B2, ADAM_EPS, ADAM_WD, ADAM_STEP = 0.001, 0.9, 0.999, 1e-08, 0.01, 10


def _params(n_grid=0, **kw):
    sem = dict(dimension_semantics=("arbitrary",) * n_grid) if n_grid else {}
    return pltpu.CompilerParams(vmem_limit_bytes=VMEM_LIMIT, **sem, **kw)


def _cspec(shape):
    nd = len(shape)
    return pl.BlockSpec(shape, lambda *_: (0,) * nd, pipeline_mode=pl.Buffered(1))


def _dot(a, b):
    return jnp.dot(a, b, preferred_element_type=F32)


def _dot_nt(a, b):
    return lax.dot_general(a, b, (((1,), (1,)), ((), ())), preferred_element_type=F32)


def _dot_tn(a, b):
    return lax.dot_general(a, b, (((0,), (0,)), ((), ())), preferred_element_type=F32)


def _bf(x):
    return x.astype(BF16)


def _rms(x):
    rs = lax.rsqrt(jnp.mean(x * x, axis=-1, keepdims=True) + EPS)
    return x * rs, rs


def _rms_bwd(du, n, rs, g):
    dn = du * g
    return rs * (dn - n * jnp.mean(dn * n, axis=-1, keepdims=True))


def _sigmoid(x):
    return 1.0 / (1.0 + jnp.exp(-x))


ROPE_BLK = 512


def _rope_parts():
    def cs(n, step):
        j = lax.broadcasted_iota(jnp.int32, (n, 128), 1) % 64
        pos = (lax.broadcasted_iota(jnp.int32, (n, 128), 0) * step).astype(F32)
        ang = pos * jnp.power(ROPE_THETA, -(j % 8).astype(F32) / 8.0)
        return jnp.cos(ang), jnp.sin(ang)

    return (*cs(ROPE_BLK, 1), *cs(S // ROPE_BLK, ROPE_BLK))


def _rope_fill(co_ref, so_ref, cb_ref, sb_ref, rc_ref, ra_ref, rb_ref):
    j = lax.broadcasted_iota(jnp.int32, (ROPE_BLK, 128), 1) % 64
    co, so = co_ref[...], so_ref[...]
    for t in range(S // ROPE_BLK):
        cb, sb = cb_ref[t:t + 1, :], sb_ref[t:t + 1, :]
        cos, sin = cb * co - sb * so, sb * co + cb * so
        rows = slice(t * ROPE_BLK, (t + 1) * ROPE_BLK)
        rc_ref[rows, :] = jnp.where(j < 16, cos, 1.0)
        ra_ref[rows, :] = jnp.where(j < 8, -sin, 0.0)
        rb_ref[rows, :] = jnp.where((j >= 8) & (j < 16), sin, 0.0)


def _rope(blk, c, a, b):
    return blk * c + pltpu.roll(blk, 120, 1) * a + pltpu.roll(blk, 8, 1) * b


def _rope_bwd(d, c, a, b):
    return d * c + pltpu.roll(d * a, 8, 1) + pltpu.roll(d * b, 120, 1)


def _unrope(t, c, a, b):
    return jnp.concatenate([_bf(_rope_bwd(t[:, j * 128:(j + 1) * 128].astype(F32), c, a, b))
                            for j in range(t.shape[1] // 128)], axis=1)


MESH = pl.DeviceIdType.MESH
ANY = pl.BlockSpec(memory_space=pl.ANY)
VM = pl.BlockSpec(memory_space=pltpu.VMEM)
FLIPS = [(dx, dy, dc) for dx in (0, 1) for dy in (0, 1) for dc in (0, 1)][1:]


def _place():
    return lax.axis_index("x"), lax.axis_index("y"), lax.axis_index("c")


def _dev_index(px, py, pc):
    return 4 * px + 2 * py + pc


def _gather_phases(ins, outs, bufs, send_sems=None, recv_sems=None, local_sems=None):
    nw = len(ins)
    if nw == 0:
        return (lambda: None,) * 3
    x, y, c = _place()
    me, sib = (x, y, c), (x, y, 1 - c)
    chips = [(1 - x, y), (x, 1 - y), (1 - x, 1 - y)]

    def copy(w, k, block, to, from_buf=False):
        dst = outs[w].at[_dev_index(*block)]
        return pltpu.make_async_remote_copy(
            src_ref=bufs[w] if from_buf else dst, dst_ref=dst, send_sem=send_sems.at[w, k],
            recv_sem=recv_sems.at[w, k], device_id=to, device_id_type=MESH)

    def mine(w):
        return pltpu.make_async_copy(bufs[w], outs[w].at[_dev_index(*me)], local_sems.at[w])

    def first(w):
        return [copy(w, 0, me, sib, True)] + [copy(w, 1 + j, me, (*chip, c), True) for j, chip in enumerate(chips)]

    def passed(w):
        return [copy(w, 4 + j, (*chip, c), sib) for j, chip in enumerate(chips)]

    def start():
        for w in range(nw):
            bufs[w][...] = ins[w][...].astype(bufs[w].dtype)
        for w in range(nw):
            mine(w).start()
            for cp in first(w):
                cp.start()

    def forward():
        for j, chip in enumerate(chips):
            for w in range(nw):
                copy(w, 1 + j, (*chip, c), me).wait_recv()
                passed(w)[j].start()

    def finish():
        for w in range(nw):
            copy(w, 0, sib, me).wait_recv()
        for j, chip in enumerate(chips):
            for w in range(nw):
                copy(w, 4 + j, (*chip, 1 - c), me).wait_recv()
        for w in range(nw):
            for cp in first(w) + passed(w):
                cp.wait_send()
            mine(w).wait()

    return start, forward, finish


def _gather_scratch(shards, dtypes):
    nw = len(shards)
    if nw == 0:
        return []
    return ([pltpu.VMEM(s.shape, dt) for s, dt in zip(shards, dtypes)]
            + [pltpu.SemaphoreType.DMA((nw, 7)), pltpu.SemaphoreType.DMA((nw, 7)), pltpu.SemaphoreType.DMA((nw,))])


def _gather_shapes(shards, dtypes):
    return [jax.ShapeDtypeStruct((NDEV, *s.shape), dt) for s, dt in zip(shards, dtypes)]


def _scatter_phases(ins, outs, send_sems=None, recv_sems=None, local_sems=None):
    nw = len(ins)
    if nw == 0:
        return (lambda: None,) * 2
    x, y, c = _place()
    me = _dev_index(x, y, c)

    def copies():
        out = []
        for w in range(nw):
            out.append(pltpu.make_async_copy(ins[w].at[me], outs[w].at[me], local_sems.at[w]))
            for k, (dx, dy, dc) in enumerate(FLIPS):
                peer = ((x + dx) % 2, (y + dy) % 2, (c + dc) % 2)
                out.append(pltpu.make_async_remote_copy(
                    src_ref=ins[w].at[_dev_index(*peer)], dst_ref=outs[w].at[me], send_sem=send_sems.at[w, k],
                    recv_sem=recv_sems.at[w, k], device_id=peer, device_id_type=MESH))
        return out

    def start():
        for cp in copies():
            cp.start()

    def finish():
        for cp in copies():
            cp.wait()

    return start, finish


def _scatter_scratch(nw):
    if nw == 0:
        return []
    return [pltpu.SemaphoreType.DMA((nw, 7)), pltpu.SemaphoreType.DMA((nw, 7)), pltpu.SemaphoreType.DMA((nw,))]


TM = 512


def _join_w_in(wg):
    sw = IN_W // NDEV

    def body(wg_ref, w_ref):
        for j in range(NDEV):
            w_ref[:, sw * j:sw * (j + 1)] = wg_ref[j]
        w_ref[:, IN_W:PW] = jnp.zeros((D, PW - IN_W), BF16)

    return pl.pallas_call(body, name="join_w_in", out_shape=jax.ShapeDtypeStruct((D, PW), BF16),
                          compiler_params=_params())(wg)


def _in_proj(x, g1, w, rc, ra, rb, shards, dtypes):
    tm = TM
    nw = len(shards)
    nt = S // tm

    def body(*refs):
        x_ref, g_ref, w_ref, rc_ref, ra_ref, rb_ref = refs[:6]
        ins = refs[6:6 + nw]
        qkv_ref, mqk_ref, mv_ref, mo_ref, gt_ref, u_ref = refs[6 + nw:12 + nw]
        outs = refs[12 + nw:12 + 2 * nw]
        bufs = refs[12 + 2 * nw:12 + 3 * nw]
        ag_start, ag_forward, ag_finish = _gather_phases(ins, outs, bufs, *refs[12 + 3 * nw:])
        i = pl.program_id(0)
        pl.when(i == 0)(ag_start)
        pl.when(i == nt - 2)(ag_forward)
        n, _ = _rms(x_ref[...])
        u = _bf(n * g_ref[...])
        u_ref[...] = u
        c, a, b = rc_ref[...], ra_ref[...], rb_ref[...]
        for half in range(2):
            blk = _dot(u, w_ref[:, half * 512:(half + 1) * 512])
            for t in range(4):
                lo = half * 512 + t * 128
                qkv_ref[:, lo:lo + 128] = _rope(blk[:, t * 128:(t + 1) * 128], c, a, b)
        qkv_ref[:, 1024:1536] = _dot(u, w_ref[:, 1024:1536])
        mqk_ref[:, 0:512] = _dot(u, w_ref[:, 1536:2048])
        mqk_ref[:, 512:1024] = _dot(u, w_ref[:, 2048:2560])
        mv_ref[...] = _dot(u, w_ref[:, 2560:3072])
        mo_ref[...] = _dot(u, w_ref[:, 3072:3584])
        gt_ref[...] = _dot(u, w_ref[:, 3584:3712])
        pl.when(i == nt - 1)(ag_finish)

    row = lambda wd: pl.BlockSpec((tm, wd), lambda i: (i, 0))
    res = pl.pallas_call(
        body, name="in_proj", grid=(nt,),
        in_specs=[row(D), _cspec((1, D)), _cspec((D, PW)), row(128), row(128), row(128)] + [VM] * nw,
        out_specs=[row(1536), row(1024), row(512), row(512), row(128), row(D)] + [ANY] * nw,
        out_shape=[jax.ShapeDtypeStruct((S, 1536), F32), jax.ShapeDtypeStruct((S, 1024), F32),
                   jax.ShapeDtypeStruct((S, 512), F32), jax.ShapeDtypeStruct((S, 512), F32),
                   jax.ShapeDtypeStruct((S, 128), F32), jax.ShapeDtypeStruct((S, D), BF16)]
        + _gather_shapes(shards, dtypes),
        scratch_shapes=_gather_scratch(shards, dtypes),
        compiler_params=_params(1),
    )(x, g1, w, rc, ra, rb, *shards)
    return res[:6], res[6:]


DILATIONS = (16, 4, 1)


def _attn_valid(n):
    kd = lax.broadcasted_iota(jnp.int32, (128, 256), 1) - lax.broadcasted_iota(jnp.int32, (128, 256), 0)
    off = jnp.where(n == 0, 0, 128)
    return (kd <= off) & (kd >= off - 128)


def _attn_rows(d, r, n):
    if d == 1:
        q0 = pl.multiple_of(n * 128, 128)
        k0 = pl.multiple_of(jnp.maximum(n - 1, 0) * 128, 128)
        return pl.ds(q0, 128), pl.ds(k0, 256), _attn_valid(n)
    q0 = r + n * 128 * d
    k0 = r + jnp.maximum(n - 1, 0) * 128 * d
    return pl.ds(q0, 128, stride=d), pl.ds(k0, 256, stride=d), _attn_valid(n)


ATTN_GROUP = 4
ATTN_ITERS = S // 128 // ATTN_GROUP


def _attn_group(d, i):
    nb = S // (128 * d)
    if nb == 2:
        qi = lax.broadcasted_iota(jnp.int32, (256, 256), 0) - lax.broadcasted_iota(jnp.int32, (256, 256), 1)
        whole = [pl.ds((ATTN_GROUP // 2) * i + u, 256, stride=d) for u in range(ATTN_GROUP // 2)]
        return [(rows, rows, (qi >= 0) & (qi <= 128)) for rows in whole]
    if d == 1:
        return [_attn_rows(1, 0, i + ATTN_ITERS * u) for u in range(ATTN_GROUP)]
    return [_attn_rows(d, (i // nb) * ATTN_GROUP + u, i % nb) for u in range(ATTN_GROUP)]


def _head0(shape):
    return lax.broadcasted_iota(jnp.int32, shape, 1) < 64


def _stack_heads(t):
    h0 = _head0(t.shape)
    tb = _bf(t)
    zero = jnp.zeros_like(tb)
    return jnp.concatenate([jnp.where(h0, tb, zero), jnp.where(h0, zero, tb)], axis=0)


def _attn_fwd(qkv, shards, dtypes):
    nw = len(shards)

    def body(*refs):
        q_ref, k_ref, v_ref = refs[:3]
        ins = refs[3:3 + nw]
        o_ref, lse0_ref, lse1_ref = refs[3 + nw:6 + nw]
        outs = refs[6 + nw:6 + 2 * nw]
        m0, m1, l0, l1, acc = refs[6 + 2 * nw:11 + 2 * nw]
        bufs = refs[11 + 2 * nw:11 + 3 * nw]
        ag_start, ag_forward, ag_finish = _gather_phases(ins, outs, bufs, *refs[11 + 3 * nw:])
        hp = pl.program_id(0)
        pl.when(hp == 0)(ag_start)
        pl.when(hp == 3)(ag_forward)
        stats = (m0, m1, l0, l1, acc)

        def update(blocks, first):
            loaded = [([q_ref[rq, :], k_ref[rk, :], v_ref[rk, :]], None if first else [ref[rq, :] for ref in stats])
                      for rq, rk, _ in blocks]
            both = lambda a, b: jnp.concatenate([a, b], axis=0)
            ss = [jnp.where(both(valid, valid), _dot_nt(_stack_heads(q * 0.125), _bf(k)), NEG)
                  for ((q, k, _), _), (_, _, valid) in zip(loaded, blocks)]
            mcs = [jnp.max(s, axis=-1, keepdims=True) for s in ss]
            if first:
                m2s = [jnp.broadcast_to(mc, (mc.shape[0], 128)) for mc in mcs]
            else:
                m2s = [jnp.maximum(both(prev[0], prev[1]), mc) for mc, (_, prev) in zip(mcs, loaded)]
            ps = [jnp.exp(s - jnp.tile(m2, (1, 2))) for s, m2 in zip(ss, m2s)]
            l2s = [jnp.sum(p, axis=-1, keepdims=True) for p in ps]
            acc2s = [_dot(_bf(p), _bf(v)) for p, ((_, _, v), _) in zip(ps, loaded)]
            results = []
            for m2, l2, acc2, (_, prev) in zip(m2s, l2s, acc2s, loaded):
                nq = m2.shape[0] // 2
                if first:
                    l2 = jnp.broadcast_to(l2, (2 * nq, 128))
                else:
                    alpha = jnp.exp(both(prev[0], prev[1]) - m2)
                    l2, acc2 = alpha * both(prev[2], prev[3]) + l2, alpha * both(prev[4], prev[4]) + acc2
                results.append((m2[0:nq], m2[nq:2 * nq], l2[0:nq], l2[nq:2 * nq],
                                jnp.where(_head0((nq, 128)), acc2[0:nq], acc2[nq:2 * nq])))
            for (rq, _, _), res in zip(blocks, results):
                for ref, val in zip(stats, res):
                    ref[rq, :] = val

        for d in DILATIONS:
            def step(i, carry, d=d):
                update(_attn_group(d, i), d == DILATIONS[0])
                return carry

            lax.fori_loop(0, ATTN_ITERS, step, 0)

        def fin(t, carry):
            rows = pl.ds(pl.multiple_of(t * 256, 256), 256)
            h0 = lax.broadcasted_iota(jnp.int32, (256, 128), 1) < 64
            la, lb = l0[rows, :], l1[rows, :]
            o_ref[rows, :] = acc[rows, :] / jnp.where(h0, la, lb)
            lse0_ref[rows, :] = m0[rows, :] + jnp.log(la)
            lse1_ref[rows, :] = m1[rows, :] + jnp.log(lb)
            return carry

        lax.fori_loop(0, S // 256, fin, 0)
        pl.when(hp == 3)(ag_finish)

    col = lambda off: pl.BlockSpec((S, 128), lambda h, off=off: (0, off + h))
    res = pl.pallas_call(
        body, name="attn_fwd", grid=(4,),
        in_specs=[col(0), col(4), col(8)] + [VM] * nw,
        out_specs=[col(0), col(0), col(0)] + [ANY] * nw,
        out_shape=[jax.ShapeDtypeStruct((S, AW), F32)] * 3 + _gather_shapes(shards, dtypes),
        scratch_shapes=[pltpu.VMEM((S, 128), F32)] * 5 + _gather_scratch(shards, dtypes),
        compiler_params=_params(1),
    )(qkv, qkv, qkv, *shards)
    return res[0], (res[1], res[2]), res[3:]


def _attn_bwd(qkv, o, lse, do, parts):
    nw = len(parts)

    def body(*refs):
        q_ref, k_ref, v_ref, o_ref, L0, L1, do_ref = refs[:7]
        ins = refs[7:7 + nw]
        dq_out, dk_out, dv_out = refs[7 + nw:10 + nw]
        outs = refs[10 + nw:10 + 2 * nw]
        D0, D1, dq_ref, dk_ref, dv_ref = refs[10 + 2 * nw:15 + 2 * nw]
        rs_start, rs_finish = _scatter_phases(ins, outs, *refs[15 + 2 * nw:])
        hp = pl.program_id(0)
        pl.when(hp == 0)(rs_start)

        def pre(t, carry):
            rows = pl.ds(pl.multiple_of(t * 256, 256), 256)
            h0 = lax.broadcasted_iota(jnp.int32, (256, 128), 1) < 64
            dd = do_ref[rows, :] * o_ref[rows, :]
            shp = (256, 128)
            D0[rows, :] = jnp.broadcast_to(jnp.sum(jnp.where(h0, dd, 0.0), axis=-1, keepdims=True), shp)
            D1[rows, :] = jnp.broadcast_to(jnp.sum(jnp.where(h0, 0.0, dd), axis=-1, keepdims=True), shp)
            return carry

        lax.fori_loop(0, S // 256, pre, 0)

        def update(blocks, first):
            loaded = [([q_ref[rq, :], k_ref[rk, :], v_ref[rk, :], do_ref[rq, :]],
                       [L0[rq, :], L1[rq, :], D0[rq, :], D1[rq, :]],
                       [0.0] * 3 if first else [dq_ref[rq, :], dk_ref[rk, :], dv_ref[rk, :]]) for rq, rk, _ in blocks]
            cat = lambda a, b: jnp.tile(jnp.concatenate([a, b], axis=0), (1, 2))
            ops = [(_stack_heads(q * 0.125), _stack_heads(q), _stack_heads(dout), _bf(k), _bf(v))
                   for (q, k, v, dout), _, _ in loaded]
            ss = [jnp.where(jnp.concatenate([valid, valid], axis=0), _dot_nt(qs, kb), NEG)
                  for (qs, _, _, kb, _), (_, _, valid) in zip(ops, blocks)]
            dps = [_dot_nt(do2, vb) for _, _, do2, _, vb in ops]
            ps = [jnp.exp(s - cat(st[0], st[1])) for s, (_, st, _) in zip(ss, loaded)]
            dss = [_bf(p * (dp - cat(st[2], st[3])) * 0.125) for p, dp, (_, st, _) in zip(ps, dps, loaded)]
            dq2s = [_dot(ds, kb) for ds, (_, _, _, kb, _) in zip(dss, ops)]
            dks = [_dot_tn(ds, q2) for ds, (_, q2, _, _, _) in zip(dss, ops)]
            dvs = [_dot_tn(_bf(p), do2) for p, (_, _, do2, _, _) in zip(ps, ops)]
            results = []
            for (_, _, (dq, dk, dv)), dq2, dkk, dvv in zip(loaded, dq2s, dks, dvs):
                nq = dq2.shape[0] // 2
                results.append((dq + jnp.where(_head0((nq, 128)), dq2[0:nq], dq2[nq:2 * nq]), dk + dkk, dv + dvv))
            for (rq, rk, _), (dq, dk, dv) in zip(blocks, results):
                dq_ref[rq, :] = dq
                dk_ref[rk, :] = dk
                dv_ref[rk, :] = dv

        assert S // (128 * DILATIONS[0]) == 2
        for d in DILATIONS:
            def step(i, carry, d=d):
                update(_attn_group(d, i), d == DILATIONS[0])
                return carry

            lax.fori_loop(0, ATTN_ITERS, step, 0)

        def fin(t, carry):
            rows = pl.ds(pl.multiple_of(t * 256, 256), 256)
            for src, dst in ((dq_ref, dq_out), (dk_ref, dk_out), (dv_ref, dv_out)):
                dst[rows, :] = _bf(src[rows, :])
            return carry

        lax.fori_loop(0, S // 256, fin, 0)
        pl.when(hp == 3)(rs_finish)

    col = lambda off: pl.BlockSpec((S, 128), lambda h, off=off: (0, off + h))
    res = pl.pallas_call(
        body, name="attn_bwd", grid=(4,),
        in_specs=[col(0), col(4), col(8), col(0), col(0), col(0), col(0)] + [ANY] * nw,
        out_specs=[col(0), col(0), col(0)] + [ANY] * nw,
        out_shape=[jax.ShapeDtypeStruct((S, AW), BF16)] * 3 + [jax.ShapeDtypeStruct(a.shape, a.dtype) for a in parts],
        scratch_shapes=[pltpu.VMEM((S, 128), F32)] * 5 + _scatter_scratch(nw),
        compiler_params=_params(1),
    )(qkv, qkv, qkv, o, lse[0], lse[1], do, *parts)
    return res[0], res[1], res[2], res[3:]


def _logsig(x):
    return jnp.minimum(x, 0.0) - jnp.log1p(jnp.exp(-jnp.abs(x)))


def _conv_taps(xp, n):
    return [xp[8:] if j == 3 else pltpu.roll(xp, 3 - j, 0)[8:] for j in range(4)]


def _conv_silu(xp, w_ref, b_ref, n):
    taps = _conv_taps(xp, n)
    c = b_ref[...] + sum(w_ref[j:j + 1, :] * taps[j] for j in range(4))
    sg = _sigmoid(c)
    return c, sg, taps


def _chunk_gates(G):
    assert LC == 128
    r = lax.broadcasted_iota(jnp.int32, (LC, LC), 0)
    c = lax.broadcasted_iota(jnp.int32, (LC, LC), 1)
    tril = (c <= r).astype(F32)
    triu = (c >= r).astype(F32)
    b_col = jnp.dot(tril, _logsig(G), preferred_element_type=F32, precision=HI)
    return b_col, b_col.T, G.T, tril, triu


def _colpick(X, lane):
    li = lax.broadcasted_iota(jnp.int32, X.shape, 1)
    return jnp.sum(jnp.where(li == lane, X, 0.0), axis=1, keepdims=True)


def _rowpick(XT, row):
    ri = lax.broadcasted_iota(jnp.int32, XT.shape, 0)
    return jnp.sum(jnp.where(ri == row, XT, 0.0), axis=0, keepdims=True)


def _mlstm_head(qh, kh, vh, G, b_col, b_row, g_row, h, Ch, nh, m_prev):
    bt = _colpick(b_col, 4 + h)
    i_col = _colpick(G, h)
    bs = _rowpick(b_row, 4 + h)
    i_row = _rowpick(g_row, h)
    r = lax.broadcasted_iota(jnp.int32, (LC, LC), 0)
    c = lax.broadcasted_iota(jnp.int32, (LC, LC), 1)
    log_d = jnp.where(c <= r, bt - bs + i_row, NEG)
    log_inter = bt + m_prev
    m_t = jnp.maximum(log_inter, jnp.max(log_d, axis=1, keepdims=True))
    Dm = jnp.exp(log_d - m_t)
    g = jnp.exp(log_inter - m_t)
    qb, kb, vb = _bf(qh), _bf(kh), _bf(vh)
    Am = _dot_nt(qb, kb) * Dm
    qC = _dot(qb, _bf(Ch))
    num = g * qC + _dot(_bf(Am), vb)
    qn = jnp.sum(qh * nh, axis=1, keepdims=True)
    den = g * qn + jnp.sum(Am, axis=1, keepdims=True)
    floor = jnp.exp(-m_t)
    dd = jnp.maximum(jnp.abs(den), floor)
    inv_dd = 1.0 / dd
    hh = num * inv_dd
    lane = lax.broadcasted_iota(jnp.int32, (1, LC), 1)
    blast = jnp.sum(jnp.where(lane == LC - 1, bs, 0.0), axis=1, keepdims=True)
    log_s = blast - bt + i_col
    m_new = jnp.maximum(blast + m_prev, jnp.max(log_s, axis=0, keepdims=True))
    decay = jnp.exp(blast + m_prev - m_new)
    ws = jnp.exp(log_s - m_new)
    kw = kh * ws
    C_new = decay * Ch + _dot_tn(_bf(kw), vb)
    n_new = decay * nh + jnp.sum(kw, axis=0, keepdims=True)
    return dict(Dm=Dm, g=g, Am=Am, qC=qC, qn=qn, den=den, floor=floor, inv_dd=inv_dd, h=hh, decay=decay, ws=ws, kw=kw,
                C_new=C_new, n_new=n_new, m_new=m_new, qb=qb, kb=kb, vb=vb)


def _head_out(hh, mo_h, gn_h):
    r = lax.rsqrt(jnp.mean(hh * hh, axis=-1, keepdims=True) + EPS)
    hn = hh * r
    sg = _sigmoid(mo_h)
    return sg * (hn * gn_h), hn, r, sg


def _mlstm_fwd(mqk, mv, mo, gates, conv_w, conv_b, gate_b, gn, shards, dtypes):
    nblk = S // TB
    ncb = TB // LC
    nw = len(shards)

    def body(*refs):
        x_ref, v_ref, o_ref, g_ref, w_ref, b_ref, gb_ref, gn_ref = refs[:8]
        ins = refs[8:8 + nw]
        out_ref, cs_ref, ns_ref, ms_ref = refs[8 + nw:12 + nw]
        outs = refs[12 + nw:12 + 2 * nw]
        tail, Cst, nst, mst, qs, ks = refs[12 + 2 * nw:18 + 2 * nw]
        bufs = refs[18 + 2 * nw:18 + 3 * nw]
        ag_start, ag_forward, ag_finish = _gather_phases(ins, outs, bufs, *refs[18 + 3 * nw:])
        i = pl.program_id(0)
        pl.when(i == 0)(ag_start)
        pl.when(i == nblk // 2)(ag_forward)

        @pl.when(i == 0)
        def _():
            tail[...] = jnp.zeros_like(tail)
            Cst[...] = jnp.zeros_like(Cst)
            nst[...] = jnp.zeros_like(nst)
            mst[...] = jnp.zeros_like(mst)

        x = x_ref[...]
        xp = jnp.concatenate([tail[...], x], axis=0)
        tail[...] = x[TB - 8:TB, :]
        c, sg, _ = _conv_silu(xp, w_ref, b_ref, TB)
        y = c * sg
        qs[...] = y[:, 0:MW]
        ks[...] = y[:, MW:2 * MW] * (1.0 / math.sqrt(128.0))

        for cc in range(ncb):
            rows = slice(cc * LC, (cc + 1) * LC)
            G = g_ref[rows, :] + gb_ref[...]
            b_col, b_row, g_row, _, _ = _chunk_gates(G)
            cs_ref[cc] = Cst[...]
            ns_ref[cc] = nst[...]
            ms_ref[cc] = mst[...]
            for h in range(4):
                ln = slice(h * 128, (h + 1) * 128)
                m_prev = jnp.max(mst[0:1, ln], axis=1, keepdims=True)
                f = _mlstm_head(qs[rows, ln], ks[rows, ln], v_ref[rows, ln], G, b_col, b_row, g_row, h,
                                Cst[:, ln], nst[0:1, ln], m_prev)
                out, _, _, _ = _head_out(f["h"], o_ref[rows, ln], gn_ref[:, ln])
                out_ref[rows, ln] = out
                Cst[:, ln] = f["C_new"]
                nst[0:1, ln] = f["n_new"]
                mst[0:1, ln] = jnp.broadcast_to(f["m_new"], (1, 128))
        pl.when(i == nblk - 1)(ag_finish)

    row = lambda wd: pl.BlockSpec((TB, wd), lambda i: (i, 0))
    res = pl.pallas_call(
        body, name="mlstm_fwd", grid=(nblk,),
        in_specs=[row(1024), row(MW), row(MW), row(128), _cspec((4, 1024)), _cspec((1, 1024)), _cspec((1, 128)),
                  _cspec((1, MW))] + [VM] * nw,
        out_specs=[row(MW), pl.BlockSpec((ncb, 128, MW), lambda i: (i, 0, 0)),
                   pl.BlockSpec((ncb, 8, MW), lambda i: (i, 0, 0)), pl.BlockSpec((ncb, 8, MW), lambda i: (i, 0, 0))]
        + [ANY] * nw,
        out_shape=[jax.ShapeDtypeStruct((S, MW), F32), jax.ShapeDtypeStruct((S // LC, 128, MW), F32),
                   jax.ShapeDtypeStruct((S // LC, 8, MW), F32), jax.ShapeDtypeStruct((S // LC, 8, MW), F32)]
        + _gather_shapes(shards, dtypes),
        scratch_shapes=[pltpu.VMEM((8, 1024), F32), pltpu.VMEM((128, MW), F32), pltpu.VMEM((8, MW), F32),
                        pltpu.VMEM((8, MW), F32), pltpu.VMEM((TB, MW), F32), pltpu.VMEM((TB, MW), F32)]
        + _gather_scratch(shards, dtypes),
        compiler_params=_params(1),
    )(mqk, mv, mo, gates, conv_w, conv_b, gate_b, gn, *shards)
    return res[0], res[1], res[2], res[3], res[4:]


DM_V, DM_O, DM_G, DM_W = 1024, 1536, 2048, PW - 3 * AW


def _mlstm_bwd(mqk, mv, mo, gates, conv_w, conv_b, gate_b, gn, cs, ns, ms, dout, parts):
    nblk = S // TB
    ncb = TB // LC
    kscale = 1.0 / math.sqrt(128.0)
    nw = len(parts)

    def body(*refs):
        x_ref, xprev_ref, v_ref, o_ref, g_ref, w_ref, b_ref, gb_ref, gn_ref, cs_ref, ns_ref, ms_ref, do_ref = refs[:13]
        ins = refs[13:13 + nw]
        dm_ref, dw_ref, db_ref, dgn_ref, dgb_ref = refs[13 + nw:18 + nw]
        outs = refs[18 + nw:18 + 2 * nw]
        dCst, dnst, dyhead, qs, ks, dqk = refs[18 + 2 * nw:24 + 2 * nw]
        rs_start, rs_finish = _scatter_phases(ins, outs, *refs[24 + 2 * nw:])
        i = pl.program_id(0)
        blk = nblk - 1 - i
        pl.when(i == 0)(rs_start)

        @pl.when(i == 0)
        def _():
            dCst[...] = jnp.zeros_like(dCst)
            dnst[...] = jnp.zeros_like(dnst)
            dyhead[...] = jnp.zeros_like(dyhead)
            dw_ref[...] = jnp.zeros_like(dw_ref)
            db_ref[...] = jnp.zeros_like(db_ref)
            dgn_ref[...] = jnp.zeros_like(dgn_ref)
            dgb_ref[...] = jnp.zeros_like(dgb_ref)

        x = x_ref[...]
        xprev = jnp.where(blk == 0, 0.0, xprev_ref[...])
        xp = jnp.concatenate([xprev, x], axis=0)
        c, sg, taps = _conv_silu(xp, w_ref, b_ref, TB)
        y = c * sg
        qs[...] = y[:, 0:MW]
        ks[...] = y[:, MW:2 * MW] * kscale
        lane128 = lax.broadcasted_iota(jnp.int32, (LC, 128), 1)
        rowi = lax.broadcasted_iota(jnp.int32, (LC, 1), 0)
        ones = jnp.ones((LC, 128), F32)

        for cc in reversed(range(ncb)):
            rows = slice(cc * LC, (cc + 1) * LC)
            G = g_ref[rows, :] + gb_ref[...]
            b_col, b_row, g_row, _, triu = _chunk_gates(G)
            dB = jnp.zeros((LC, 128), F32)
            dI = jnp.zeros((LC, 128), F32)
            for h in range(4):
                ln = slice(h * 128, (h + 1) * 128)
                Ch = cs_ref[cc, :, ln]
                nh = ns_ref[cc, 0:1, ln]
                m_prev = jnp.max(ms_ref[cc, 0:1, ln], axis=1, keepdims=True)
                qh, kh, vh = qs[rows, ln], ks[rows, ln], v_ref[rows, ln]
                f = _mlstm_head(qh, kh, vh, G, b_col, b_row, g_row, h, Ch, nh, m_prev)
                hh, inv_dd, den, g, Am, Dm = f["h"], f["inv_dd"], f["den"], f["g"], f["Am"], f["Dm"]
                qb, kb, vb = f["qb"], f["kb"], f["vb"]
                gn_h = gn_ref[:, ln]
                _, hn, r, sgo = _head_out(hh, o_ref[rows, ln], gn_h)
                do = do_ref[rows, ln]
                hm = hn * gn_h
                dm_ref[rows, DM_O + h * 128:DM_O + (h + 1) * 128] = _bf(do * hm * sgo * (1.0 - sgo))
                dhm = do * sgo
                dgn_ref[:, ln] = dgn_ref[:, ln] + jnp.sum(dhm * hn, axis=0, keepdims=True)
                dhn = dhm * gn_h
                dh = r * (dhn - hn * jnp.mean(dhn * hn, axis=-1, keepdims=True))
                dnum = dh * inv_dd
                ddd = -jnp.sum(dh * hh, axis=1, keepdims=True) * inv_dd
                dden = jnp.where(jnp.abs(den) >= f["floor"], ddd * jnp.sign(den), 0.0)
                dnb = _bf(dnum)
                dA = _dot_nt(dnb, vb) + dden
                dv = _dot_tn(_bf(Am), dnb)
                gd = _bf(g * dnum)
                gq = g * dden
                dq = _dot_nt(gd, _bf(Ch)) + gq * nh
                dCn = dCst[:, ln]
                dnn = dnst[0:1, ln]
                dC = f["decay"] * dCn + _dot_tn(qb, gd)
                dn = f["decay"] * dnn + jnp.sum(gq * qh, axis=0, keepdims=True)
                dg = jnp.sum(dnum * f["qC"], axis=1, keepdims=True) + dden * f["qn"]
                dS = _bf(dA * Dm)
                dq = dq + _dot(dS, kb)
                dk = _dot_tn(dS, qb)
                Gm = dA * Am
                gam = dg * g
                dCb = _bf(dCn)
                E = _dot_nt(vb, dCb) + dnn
                ws = f["ws"]
                dk = dk + ws * E
                om = jnp.sum(E * kh, axis=1, keepdims=True) * ws
                dv = dv + _dot(_bf(f["kw"]), dCb)
                ddecay = (jnp.sum(jnp.sum(dCn * Ch, axis=1, keepdims=True), axis=0, keepdims=True)
                          + jnp.sum(dnn * nh, axis=1, keepdims=True))
                delta = ddecay * f["decay"]
                rows_g = jnp.sum(Gm, axis=1, keepdims=True)
                cols_g = jnp.broadcast_to(jnp.sum(Gm, axis=0, keepdims=True), (LC, 128)).T
                last = jnp.where(rowi == LC - 1, jnp.sum(om, axis=0, keepdims=True) + delta, 0.0)
                db = rows_g + gam - om + last - cols_g
                di = cols_g + om
                dB = jnp.where(lane128 == 4 + h, db, dB)
                dI = jnp.where(lane128 == h, di, dI)
                dCst[:, ln] = dC
                dnst[0:1, ln] = dn
                dqk[rows, ln] = dq
                dqk[rows, MW + h * 128:MW + (h + 1) * 128] = dk * kscale
                dm_ref[rows, DM_V + h * 128:DM_V + (h + 1) * 128] = _bf(dv)
            dlogf = jnp.dot(triu, dB, preferred_element_type=F32, precision=HI)
            dG = dI + dlogf * _sigmoid(-G)
            dG = jnp.where(lane128 < 8, dG, 0.0)
            dm_ref[rows, DM_G:DM_G + 128] = _bf(dG)
            dm_ref[rows, DM_G + 128:DM_W] = jnp.zeros((LC, DM_W - DM_G - 128), BF16)
            dgb_ref[...] = dgb_ref[...] + jnp.sum(dG, axis=0, keepdims=True)

        dy = dqk[...] * (sg * (1.0 + c * (1.0 - sg)))
        db_ref[...] = db_ref[...] + jnp.sum(dy, axis=0, keepdims=True)
        for j in range(4):
            dw_ref[j:j + 1, :] = dw_ref[j:j + 1, :] + jnp.sum(dy * taps[j], axis=0, keepdims=True)
        dyp = jnp.concatenate([dy, dyhead[...]], axis=0)
        dx = w_ref[3:4, :] * dy
        for j in range(3):
            dx = dx + w_ref[j:j + 1, :] * pltpu.roll(dyp, TB + 8 - (3 - j), 0)[0:TB]
        dm_ref[:, 0:DM_V] = _bf(dx)
        dyhead[...] = dy[0:8, :]
        pl.when(i == nblk - 1)(rs_finish)

    rrow = lambda wd: pl.BlockSpec((TB, wd), lambda i: (nblk - 1 - i, 0))
    st = lambda r: pl.BlockSpec((ncb, r, MW), lambda i: (nblk - 1 - i, 0, 0))
    prev8 = pl.BlockSpec((8, 1024), lambda i: (jnp.maximum((nblk - 1 - i) * (TB // 8) - 1, 0), 0))
    res = pl.pallas_call(
        body, name="mlstm_bwd", grid=(nblk,),
        in_specs=[rrow(1024), prev8, rrow(MW), rrow(MW), rrow(128), _cspec((4, 1024)), _cspec((1, 1024)),
                  _cspec((1, 128)), _cspec((1, MW)), st(128), st(8), st(8), rrow(MW)] + [ANY] * nw,
        out_specs=[rrow(DM_W),
                   pl.BlockSpec((4, 1024), lambda i: (0, 0)), pl.BlockSpec((1, 1024), lambda i: (0, 0)),
                   pl.BlockSpec((1, MW), lambda i: (0, 0)), pl.BlockSpec((1, 128), lambda i: (0, 0))] + [ANY] * nw,
        out_shape=[jax.ShapeDtypeStruct((S, DM_W), BF16),
                   jax.ShapeDtypeStruct((4, 1024), F32), jax.ShapeDtypeStruct((1, 1024), F32),
                   jax.ShapeDtypeStruct((1, MW), F32), jax.ShapeDtypeStruct((1, 128), F32)]
        + [jax.ShapeDtypeStruct(a.shape, a.dtype) for a in parts],
        scratch_shapes=[pltpu.VMEM((128, MW), F32), pltpu.VMEM((8, MW), F32), pltpu.VMEM((8, 1024), F32),
                        pltpu.VMEM((TB, MW), F32), pltpu.VMEM((TB, MW), F32), pltpu.VMEM((TB, 1024), F32)]
        + _scatter_scratch(nw),
        compiler_params=_params(1),
    )(mqk, mqk, mv, mo, gates, conv_w, conv_b, gate_b, gn, cs, ns, ms, dout, *parts)
    return res[:5], res[5:]


def _out_proj(x, attn, ml, w, g):
    tm = TM

    def body(x_ref, a_ref, m_ref, w_ref, g_ref, h_ref, u_ref):
        h1 = x_ref[...] + _dot(_bf(a_ref[...]), w_ref[0:AW, :]) + _dot(_bf(m_ref[...]), w_ref[AW:D, :])
        h_ref[...] = h1
        n, _ = _rms(h1)
        u_ref[...] = _bf(n * g_ref[...])

    row = lambda wd: pl.BlockSpec((tm, wd), lambda i: (i, 0))
    return pl.pallas_call(
        body, name="out_proj", grid=(S // tm,),
        in_specs=[row(D), row(AW), row(MW), _cspec((D, D)), _cspec((1, D))],
        out_specs=[row(D), row(D)],
        out_shape=[jax.ShapeDtypeStruct((S, D), F32), jax.ShapeDtypeStruct((S, D), BF16)],
        compiler_params=_params(1),
    )(x, attn, ml, w, g)


HALF = DFF // NDEV // 2


def _mlp_fwd(h1, u2, w_up, w_down_a, w_down_b):
    tm = TM

    def body(h_ref, u_ref, wu_ref, wa_ref, wb_ref, a_ref, o_ref):
        u = u_ref[...]
        acc = h_ref[...]
        for c in range(NDEV):
            cols = slice(c * 512, (c + 1) * 512)
            a = _dot(u, wu_ref[c])
            a_ref[:, cols] = _bf(a)
            r = jnp.maximum(a, 0.0)
            r = _bf(r * r)
            acc = acc + _dot(r[:, 0:HALF], wa_ref[c]) + _dot(r[:, HALF:2 * HALF], wb_ref[c])
        o_ref[...] = acc

    row = lambda wd: pl.BlockSpec((tm, wd), lambda i: (i, 0))
    return pl.pallas_call(
        body, name="mlp_fwd", grid=(S // tm,),
        in_specs=[row(D), row(D), _cspec((NDEV, D, DFF // NDEV)), _cspec((NDEV, HALF, D)), _cspec((NDEV, HALF, D))],
        out_specs=[row(DFF), row(D)],
        out_shape=[jax.ShapeDtypeStruct((S, DFF), BF16), jax.ShapeDtypeStruct((S, D), F32)],
        compiler_params=_params(1),
    )(h1, u2, w_up, w_down_a, w_down_b)


def _ple_loss(h2, p, target, w_pg, w_ple, g_ple, g_fin):
    tm = TM

    def body(h_ref, p_ref, t_ref, wg_ref, wp_ref, gp_ref, gf_ref,
             dh_ref, dwg_ref, dwp_ref, dgp_ref, dgf_ref, loss_ref, acc_g, acc_p):
        i = pl.program_id(0)

        @pl.when(i == 0)
        def _():
            acc_g[...] = jnp.zeros_like(acc_g)
            acc_p[...] = jnp.zeros_like(acc_p)
            dgp_ref[...] = jnp.zeros_like(dgp_ref)
            dgf_ref[...] = jnp.zeros_like(dgf_ref)
            loss_ref[...] = jnp.zeros_like(loss_ref)

        h2v = h_ref[...]
        n2, rs2 = _rms(h2v)
        u3 = _bf(n2 * gp_ref[...])
        gt = _sigmoid(_dot(u3, wg_ref[...]))
        pb = _bf(p_ref[...])
        e = jnp.concatenate([_dot(pb, wp_ref[j]) for j in range(NDEV)], axis=1)
        h3 = h2v + gt * e
        n3, rs3 = _rms(h3)
        err = n3 * gf_ref[...] - t_ref[...]
        loss_ref[...] = loss_ref[...] + 0.5 / D * jnp.sum(jnp.sum(err * err, axis=1, keepdims=True), axis=0, keepdims=True)
        dy = err * (1.0 / D)
        dgf_ref[...] = dgf_ref[...] + jnp.sum(dy * n3, axis=0, keepdims=True)
        dh3 = _rms_bwd(dy, n3, rs3, gf_ref[...])
        de = _bf(dh3 * gt)
        dz = _bf(dh3 * e * gt * (1.0 - gt))
        acc_p[...] = acc_p[...] + _dot_tn(pb, de)
        acc_g[...] = acc_g[...] + _dot_tn(u3, dz)
        du3 = _dot_nt(dz, wg_ref[...])
        dgp_ref[...] = dgp_ref[...] + jnp.sum(du3 * n2, axis=0, keepdims=True)
        dh_ref[...] = dh3 + _rms_bwd(du3, n2, rs2, gp_ref[...])

        @pl.when(i == S // tm - 1)
        def _():
            dwg_ref[...] = _bf(acc_g[...])
            for j in range(NDEV):
                dwp_ref[j] = _bf(acc_p[:, j * 128:(j + 1) * 128])

    row = lambda wd: pl.BlockSpec((tm, wd), lambda i: (i, 0))
    whole = lambda shp: pl.BlockSpec(shp, lambda i: (0,) * len(shp))
    return pl.pallas_call(
        body, name="ple_loss", grid=(S // tm,),
        in_specs=[row(D), row(PLE), row(D), _cspec((D, D)), _cspec((NDEV, PLE, 128)), _cspec((1, D)), _cspec((1, D))],
        out_specs=[row(D), whole((D, D)), whole((NDEV, PLE, 128)), whole((1, D)), whole((1, D)), whole((1, 1))],
        out_shape=[jax.ShapeDtypeStruct((S, D), F32), jax.ShapeDtypeStruct((D, D), BF16),
                   jax.ShapeDtypeStruct((NDEV, PLE, 128), BF16), jax.ShapeDtypeStruct((1, D), F32),
                   jax.ShapeDtypeStruct((1, D), F32), jax.ShapeDtypeStruct((1, 1), F32)],
        scratch_shapes=[pltpu.VMEM((D, D), F32), pltpu.VMEM((PLE, D), F32)],
        compiler_params=_params(1),
    )(h2, p, target, w_pg, w_ple, g_ple, g_fin)


def _mlp_bwd(dh2, a, h1, g, w_up, w_down_a, w_down_b):
    tm = TM

    def body(d_ref, a_ref, h_ref, g_ref, wu_ref, wa_ref, wb_ref, da_ref, dh1_ref, dg_ref):
        @pl.when(pl.program_id(0) == 0)
        def _():
            dg_ref[...] = jnp.zeros_like(dg_ref)

        dh2v = d_ref[...]
        db = _bf(dh2v)
        du = jnp.zeros((tm, D), F32)
        for c in range(NDEV):
            cols = slice(c * 512, (c + 1) * 512)
            dr = jnp.concatenate([_dot_nt(db, wa_ref[c]), _dot_nt(db, wb_ref[c])], axis=1)
            da = _bf(dr * (2.0 * jnp.maximum(a_ref[:, cols], 0.0)))
            da_ref[:, cols] = da
            du = du + _dot_nt(da, wu_ref[c])
        n, rs = _rms(h_ref[...])
        dg_ref[...] = dg_ref[...] + jnp.sum(du * n, axis=0, keepdims=True)
        dh1_ref[...] = dh2v + _rms_bwd(du, n, rs, g_ref[...])

    row = lambda wd: pl.BlockSpec((tm, wd), lambda i: (i, 0))
    return pl.pallas_call(
        body, name="mlp_bwd", grid=(S // tm,),
        in_specs=[row(D), row(DFF), row(D), _cspec((1, D)), _cspec((NDEV, D, DFF // NDEV)), _cspec((NDEV, HALF, D)),
                  _cspec((NDEV, HALF, D))],
        out_specs=[row(DFF), row(D), pl.BlockSpec((1, D), lambda i: (0, 0))],
        out_shape=[jax.ShapeDtypeStruct((S, DFF), BF16), jax.ShapeDtypeStruct((S, D), F32),
                   jax.ShapeDtypeStruct((1, D), F32)],
        compiler_params=_params(1),
    )(dh2, a, h1, g, w_up, w_down_a, w_down_b)


def _out_proj_bwd(dh1, attn, ml, w):
    tm = TM

    def body(d_ref, a_ref, m_ref, w_ref, da_ref, dm_ref, dw_ref, acc):
        i = pl.program_id(0)

        @pl.when(i == 0)
        def _():
            acc[...] = jnp.zeros_like(acc)

        db = _bf(d_ref[...])
        dmix = _dot_nt(db, w_ref[...])
        da_ref[...] = dmix[:, 0:AW]
        dm_ref[...] = dmix[:, AW:D]
        acc[0:AW, :] = acc[0:AW, :] + _dot_tn(_bf(a_ref[...]), db)
        acc[AW:D, :] = acc[AW:D, :] + _dot_tn(_bf(m_ref[...]), db)

        @pl.when(i == S // tm - 1)
        def _():
            dw_ref[...] = _bf(acc[...])

    row = lambda wd: pl.BlockSpec((tm, wd), lambda i: (i, 0))
    return pl.pallas_call(
        body, name="out_proj_bwd", grid=(S // tm,),
        in_specs=[row(D), row(AW), row(MW), _cspec((D, D))],
        out_specs=[row(AW), row(MW), pl.BlockSpec((D, D), lambda i: (0, 0))],
        out_shape=[jax.ShapeDtypeStruct((S, AW), F32), jax.ShapeDtypeStruct((S, MW), F32),
                   jax.ShapeDtypeStruct((D, D), BF16)],
        scratch_shapes=[pltpu.VMEM((D, D), F32)],
        compiler_params=_params(1),
    )(dh1, attn, ml, w)


CHIP_FLIPS = [(0, 0), (0, 1), (1, 0), (1, 1)]


def _scatter2_phases(in_ref, out_ref, mine_v, sib_v, psum_v, loc_sems, d2d_send, d2d_recv, ici_send, ici_recv, own_sem):
    x, y, c = _place()
    chips = [((x + dx) % 2, (y + dy) % 2) for dx, dy in CHIP_FLIPS]
    nc = len(chips)

    def local(k):
        return pltpu.make_async_copy(in_ref.at[_dev_index(*chips[k], c)], mine_v.at[k], loc_sems.at[k])

    def to_sib(k):
        return pltpu.make_async_remote_copy(
            src_ref=in_ref.at[_dev_index(*chips[k], 1 - c)], dst_ref=sib_v.at[k], send_sem=d2d_send.at[k],
            recv_sem=d2d_recv.at[k], device_id=(x, y, 1 - c), device_id_type=MESH)

    def over_ici(k):
        return pltpu.make_async_remote_copy(
            src_ref=psum_v.at[k], dst_ref=out_ref.at[k], send_sem=ici_send.at[k - 1], recv_sem=ici_recv.at[k - 1],
            device_id=(*chips[k], c), device_id_type=MESH)

    def own():
        return pltpu.make_async_copy(psum_v.at[0], out_ref.at[0], own_sem)

    def start():
        for k in range(nc):
            to_sib(k).start()
            local(k).start()

    def middle():
        for k in (1, 2, 3, 0):
            local(k).wait()
            to_sib(k).wait_recv()
            psum_v[k] = _bf(mine_v[k].astype(F32) + sib_v[k].astype(F32))
            (over_ici(k) if k else own()).start()

    def finish():
        for k in range(1, nc):
            over_ici(k).wait()
        for k in range(nc):
            to_sib(k).wait_send()
        own().wait()

    return start, middle, finish


def _scatter2_scratch(shard, dtype):
    nc = len(CHIP_FLIPS)
    return ([pltpu.VMEM((nc, *shard), dtype)] * 3
            + [pltpu.SemaphoreType.DMA((nc,))] * 3 + [pltpu.SemaphoreType.DMA((nc - 1,))] * 2 + [pltpu.SemaphoreType.DMA])


def _in_proj_bwd(dparts, n_roped, rope, dh1, x, g1, w, part):
    tm = TM
    nt = S // tm
    widths = [d.shape[1] for d in dparts]
    assert sum(widths) == PW
    npar = len(dparts)

    def body(*refs):
        d_refs = refs[:npar]
        tabs = [t[...] for t in refs[npar:npar + 3]]
        dh_ref, x_ref, g_ref, w_ref, in_ref, dx_ref, dgsum_ref, out_ref = refs[npar + 3:npar + 11]
        rs_start, rs_middle, rs_finish = _scatter2_phases(in_ref, out_ref, *refs[npar + 11:npar + 20])
        dg_ref = refs[npar + 20]
        ar_start, ar_finish = _small_phases([dg_ref], dgsum_ref, *refs[npar + 21:])
        i = pl.program_id(0)
        pl.when(i == 0)(rs_start)
        pl.when(i == 1)(rs_middle)

        @pl.when(i == 0)
        def _():
            dg_ref[...] = jnp.zeros_like(dg_ref)

        du = jnp.zeros((tm, D), F32)
        off = 0
        for j, (d_ref, wd) in enumerate(zip(d_refs, widths)):
            nc = next(c for c in (768, 512) if wd % c == 0)
            for s in range(wd // nc):
                d = d_ref[:, s * nc:(s + 1) * nc]
                du = du + _dot_nt(_unrope(d, *tabs) if j < n_roped else d, w_ref[:, off + s * nc:off + (s + 1) * nc])
            off += wd
        n, rs = _rms(x_ref[...])
        dg_ref[...] = dg_ref[...] + jnp.sum(du * n, axis=0, keepdims=True)
        dx_ref[...] = dh_ref[...] + _rms_bwd(du, n, rs, g_ref[...])

        @pl.when(i == nt - 1)
        def _():
            ar_start()
            rs_finish()
            ar_finish()

    row = lambda wd: pl.BlockSpec((tm, wd), lambda i: (i, 0))
    shard = part.shape[1:]
    return pl.pallas_call(
        body, name="in_proj_bwd", grid=(nt,),
        in_specs=[row(wd) for wd in widths] + [row(128)] * 3 + [row(D), row(D), _cspec((1, D)), _cspec((D, PW)), ANY],
        out_specs=[row(D), VM, ANY],
        out_shape=[jax.ShapeDtypeStruct((S, D), F32), jax.ShapeDtypeStruct((8, 1024), F32),
                   jax.ShapeDtypeStruct((len(CHIP_FLIPS), *shard), part.dtype)],
        scratch_shapes=_scatter2_scratch(shard, part.dtype)
        + [pltpu.VMEM((1, D), F32), pltpu.VMEM((8, 1024), F32), pltpu.VMEM((NDEV, 8, 1024), F32),
           pltpu.SemaphoreType.DMA((7,)), pltpu.SemaphoreType.DMA((7,))],
        compiler_params=_params(1),
    )(*dparts, *rope, dh1, x, g1, w, part)


SMALL_ROWS = 96


def _small_phases(ins, out_ref, pack, rbuf, send_sems, recv_sems):
    x, y, c = _place()
    me = _dev_index(x, y, c)

    def copies():
        out = []
        for k, (dx, dy, dc) in enumerate(FLIPS):
            peer = ((x + dx) % 2, (y + dy) % 2, (c + dc) % 2)
            out.append(pltpu.make_async_remote_copy(
                src_ref=pack, dst_ref=rbuf.at[me], send_sem=send_sems.at[k], recv_sem=recv_sems.at[k],
                device_id=peer, device_id_type=MESH))
        return out

    def start():
        pack[...] = jnp.zeros_like(pack)
        for i, ref in enumerate(ins):
            pack[8 * i:8 * i + 1, 0:ref.shape[1]] = ref[...]
        rbuf[me] = pack[...]
        for cp in copies():
            cp.start()

    def finish():
        for cp in copies():
            cp.wait()
        tot = rbuf[0]
        for j in range(1, NDEV):
            tot = tot + rbuf[j]
        out_ref[...] = tot

    return start, finish


def _wgrad(name, A, Bs, a_fn, b_fn, out_shape, split=None, ts=512, small=(), rope=(), n_roped=0):
    K = A.shape[1]
    widths = [b.shape[1] for b in Bs]
    N = sum(widths)
    nb, ns, nrt = len(Bs) + len(rope), len(small), S // ts
    kc = min(K, 1024)

    def body(*refs):
        a_ref, b_refs = refs[0], refs[1:1 + len(Bs)]
        tabs = [t[...] for t in refs[1 + len(Bs):1 + nb]]
        o_ref = refs[1 + nb + ns]
        acc = refs[2 + nb + ns + bool(ns)]
        r = pl.program_id(0)
        if ns:
            sm_start, sm_finish = _small_phases(refs[1 + nb:1 + nb + ns], refs[2 + nb + ns], *refs[4 + nb + ns:])
            pl.when(r == 0)(sm_start)

        @pl.when(r == 0)
        def _():
            acc[...] = jnp.zeros_like(acc)

        bs, off = [], 0
        for i, (b_ref, w) in enumerate(zip(b_refs, widths)):
            nc = next(c for c in (1024, 768, 512) if w % c == 0)
            fn = (lambda t: _unrope(t, *tabs)) if i < n_roped else b_fn
            bs += [(off + c * nc, nc, fn(b_ref[:, c * nc:(c + 1) * nc])) for c in range(w // nc)]
            off += w
        for kk in range(K // kc):
            rows = slice(kk * kc, (kk + 1) * kc)
            at = a_fn(a_ref[:, rows]).T
            for lo, nc, b in bs:
                acc[rows, lo:lo + nc] = acc[rows, lo:lo + nc] + _dot(at, b)

        @pl.when(r == nrt - 1)
        def _():
            if split is None:
                o_ref[...] = _bf(acc[...])
            else:
                for j in range(NDEV):
                    o_ref[j] = _bf(acc[:, split * j:split * (j + 1)])

        if ns:
            pl.when(r == nrt - 1)(sm_finish)

    in_specs = ([pl.BlockSpec((ts, K), lambda r: (r, 0))] + [pl.BlockSpec((ts, w), lambda r: (r, 0)) for w in widths]
                + [pl.BlockSpec((ts, 128), lambda r: (r, 0))] * len(rope))
    out_spec = pl.BlockSpec(out_shape, lambda r: (0,) * len(out_shape))
    scratch = [pltpu.VMEM((K, N), F32)]
    if not ns:
        return pl.pallas_call(
            body, name=name, grid=(nrt,), in_specs=in_specs, out_specs=out_spec,
            out_shape=jax.ShapeDtypeStruct(out_shape, BF16), scratch_shapes=scratch, compiler_params=_params(1),
        )(A, *Bs, *rope)
    return pl.pallas_call(
        body, name=name, grid=(nrt,), in_specs=in_specs + [VM] * ns, out_specs=[out_spec, VM],
        out_shape=[jax.ShapeDtypeStruct(out_shape, BF16), jax.ShapeDtypeStruct((SMALL_ROWS, 1024), F32)],
        scratch_shapes=scratch + [pltpu.VMEM((SMALL_ROWS, 1024), F32), pltpu.VMEM((NDEV, SMALL_ROWS, 1024), F32),
                                  pltpu.SemaphoreType.DMA((7,)), pltpu.SemaphoreType.DMA((7,))],
        compiler_params=_params(1),
    )(A, *Bs, *rope, *small)


def _relu2_bf(a):
    r = jnp.maximum(a.astype(F32), 0.0)
    return _bf(r * r)


def _ident(a):
    return a


def _step(x, p, target, g1, conv_b, gate_b, gn, g_mlp, g_ple, g_fin, sh):
    (g_in, g_conv), (rc, ra, rb) = _gather_weights([sh["w_in"], sh["conv_w"]], [BF16, F32])
    conv_w = g_conv.transpose(1, 0, 2).reshape(4, 1024)
    w_in_p = _join_w_in(g_in)
    (qkv, mqk, mv, mo, gates, u1), (w_out8, w_pg8, w_ple8) = _in_proj(
        x, g1, w_in_p, rc, ra, rb, [sh["w_out"], sh["w_ple_gate"], sh["w_ple"]], [BF16] * 3)
    attn, lse, (w_up8, w_down_a) = _attn_fwd(qkv, [sh["w_up"], sh["w_down"][0:HALF]], [BF16] * 2)
    ml, cs, ns, ms, (w_down_b,) = _mlstm_fwd(mqk, mv, mo, gates, conv_w, conv_b, gate_b, gn,
                                             [sh["w_down"][HALF:2 * HALF]], [BF16])
    w_out, w_pg = w_out8.reshape(D, D), w_pg8.reshape(D, D)
    h1, u2 = _out_proj(x, attn, ml, w_out, g_mlp)
    a, h2 = _mlp_fwd(h1, u2, w_up8, w_down_a, w_down_b)
    dh2, dw_pg, dw_ple8, dg_ple, dg_fin, loss = _ple_loss(h2, p, target, w_pg, w_ple8, g_ple, g_fin)
    da, dh1, dg_mlp = _mlp_bwd(dh2, a, h1, g_mlp, w_up8, w_down_a, w_down_b)
    dw_up8 = _wgrad("wgrad_up", u2, [da], _ident, _ident, (NDEV, D, DFF // NDEV), split=DFF // NDEV)
    dw_down = _wgrad("wgrad_down", a, [dh2], _relu2_bf, _bf, (DFF, D))
    d_attn, d_ml, dw_out = _out_proj_bwd(dh1, attn, ml, w_out)
    (dm, dconv_w, dconv_b, dgn, dgate_b), (r_down,) = _mlstm_bwd(
        mqk, mv, mo, gates, conv_w, conv_b, gate_b, gn, cs, ns, ms, d_ml, [dw_down.reshape(NDEV, DFF // NDEV, D)])
    dq, dk, dv, (r_up, r_out, r_pg, r_ple) = _attn_bwd(
        qkv, attn, lse, d_attn,
        [dw_up8, dw_out.reshape(NDEV, D // NDEV, D), dw_pg.reshape(NDEV, D // NDEV, D), dw_ple8])
    dparts = [dq, dk, dv, dm]
    small = [jnp.zeros((1, D), F32), dconv_b, dgate_b, dgn, dg_mlp, dg_ple, dg_fin, loss]
    dw_in8, total = _wgrad("wgrad_in", u1, dparts, _ident, _ident, (NDEV, D, IN_W // NDEV), split=IN_W // NDEV,
                           small=small + [dconv_w[j:j + 1] for j in range(4)], rope=(rc, ra, rb), n_roped=2)
    dx, dg1_sum, r_in = _in_proj_bwd(dparts, 2, (rc, ra, rb), dh1, x, g1, w_in_p, dw_in8)
    recv = dict(w_in=r_in, w_out=r_out, w_up=r_up, w_down=r_down, w_ple_gate=r_pg, w_ple=r_ple)
    return dx, recv, total, dg1_sum


def _gather_weights(shards, dtypes):
    nw = len(shards)

    def body(*refs):
        ins, parts = refs[:nw], refs[nw:nw + 4]
        outs, tables = refs[nw + 4:2 * nw + 4], refs[2 * nw + 4:2 * nw + 7]
        start, forward, finish = _gather_phases(ins, outs, refs[2 * nw + 7:3 * nw + 7], *refs[3 * nw + 7:])
        start()
        _rope_fill(*parts, *tables)
        forward()
        finish()

    res = pl.pallas_call(
        body, name="gather_weights",
        in_specs=[VM] * (nw + 4), out_specs=[ANY] * nw + [VM] * 3,
        out_shape=_gather_shapes(shards, dtypes) + [jax.ShapeDtypeStruct((S, 128), F32)] * 3,
        scratch_shapes=_gather_scratch(shards, dtypes),
        compiler_params=_params(),
    )(*shards, *_rope_parts())
    return res[:nw], res[nw:]


ADAM_STEPS = 4


def _adamw(items):
    n = len(items)

    def body(*refs):
        for i in range(n):
            g_ref, w_ref, m_ref, v_ref = refs[4 * i:4 * i + 4]
            go_ref, d_ref, mo_ref, vo_ref = refs[4 * n + 4 * i:4 * n + 4 * i + 4]
            g = g_ref[0].astype(F32)
            for j in range(1, g_ref.shape[0]):
                g = g + g_ref[j].astype(F32)
            go_ref[...] = g
            d_ref[...], mo_ref[...], vo_ref[...] = _adam_update(g, w_ref[...], m_ref[...], v_ref[...])

    in_specs, out_specs, out_shape, args = [], [], [], []
    for gparts, w, m, v in items:
        P, R, C = gparts.shape
        if R % (8 * ADAM_STEPS) == 0:
            tr = R // ADAM_STEPS
            row, gspec = pl.BlockSpec((tr, C), lambda i: (i, 0)), pl.BlockSpec((P, tr, C), lambda i: (0, i, 0))
        else:
            row, gspec = pl.BlockSpec((R, C), lambda i: (0, 0)), pl.BlockSpec((P, R, C), lambda i: (0, 0, 0))
        in_specs += [gspec, row, row, row]
        out_specs += [row] * 4
        out_shape += [jax.ShapeDtypeStruct((R, C), F32)] * 4
        args += [gparts, w, m, v]
    res = pl.pallas_call(
        body, name="adamw", grid=(ADAM_STEPS,), in_specs=in_specs, out_specs=out_specs, out_shape=out_shape,
        compiler_params=_params(1),
    )(*args)
    return [res[4 * i:4 * i + 4] for i in range(n)]


SMALL = ("norm_mix_g", "conv_b", "gate_b", "mlstm_norm_g", "norm_mlp_g", "norm_ple_g", "final_norm_g")


def _adam_update(g, w, m, v):
    c1 = 1.0 - ADAM_B1 ** ADAM_STEP
    c2 = 1.0 - ADAM_B2 ** ADAM_STEP
    m2 = ADAM_B1 * m + (1.0 - ADAM_B1) * g
    v2 = ADAM_B2 * v + (1.0 - ADAM_B2) * (g * g)
    return -ADAM_LR * ((m2 / c1) / (jnp.sqrt(v2 / c2) + ADAM_EPS) + ADAM_WD * w), m2, v2


def _adamw_small(total, first, ws, ms, vs):
    n = len(ws)

    def body(*refs):
        t_ref, f_ref = refs[:2]
        refs = refs[1:]
        outs = refs[1 + 3 * n:]
        for i in range(n):
            w_ref, m_ref, v_ref = refs[1 + i], refs[1 + n + i], refs[1 + 2 * n + i]
            g = (t_ref if i else f_ref)[8 * i:8 * i + 1, 0:w_ref.shape[1]]
            delta, m2, v2 = _adam_update(g, w_ref[...], m_ref[...], v_ref[...])
            for ref, val in zip(outs[4 * i:4 * i + 4], (g, delta, m2, v2)):
                ref[...] = val

    res = pl.pallas_call(
        body, name="adamw_small",
        out_shape=[jax.ShapeDtypeStruct(w.shape, F32) for w in ws for _ in range(4)],
        compiler_params=_params(),
    )(total, first, *ws, *ms, *vs)
    return [res[4 * i:4 * i + 4] for i in range(n)]


def kernel(x, p, norm_mix_g, w_in, conv_w, conv_b, gate_b, mlstm_norm_g, w_out, norm_mlp_g, w_up, w_down, norm_ple_g, w_ple_gate, w_ple, final_norm_g, loss_target, m_norm_mix_g, m_w_in, m_conv_w, m_conv_b, m_gate_b, m_mlstm_norm_g, m_w_out, m_norm_mlp_g, m_w_up, m_w_down, m_norm_ple_g, m_w_ple_gate, m_w_ple, m_final_norm_g, v_norm_mix_g, v_w_in, v_conv_w, v_conv_b, v_gate_b, v_mlstm_norm_g, v_w_out, v_norm_mlp_g, v_w_up, v_w_down, v_norm_ple_g, v_w_ple_gate, v_w_ple, v_final_norm_g):
    big_names = ("w_in", "conv_w", "w_out", "w_up", "w_down", "w_ple_gate", "w_ple")
    wts = dict(w_in=w_in, conv_w=conv_w, w_out=w_out, w_up=w_up, w_down=w_down, w_ple_gate=w_ple_gate, w_ple=w_ple)
    mom = dict(w_in=m_w_in, conv_w=m_conv_w, w_out=m_w_out, w_up=m_w_up, w_down=m_w_down, w_ple_gate=m_w_ple_gate,
               w_ple=m_w_ple)
    var = dict(w_in=v_w_in, conv_w=v_conv_w, w_out=v_w_out, w_up=v_w_up, w_down=v_w_down, w_ple_gate=v_w_ple_gate,
               w_ple=v_w_ple)
    sq = lambda a: a.reshape(a.shape[1:])
    fin = final_norm_g.reshape(1, D)
    dx, recv, total, first = _step(
        x[0], p[0, 0], loss_target[0], norm_mix_g, conv_b, jnp.pad(gate_b, ((0, 0), (0, 120))), mlstm_norm_g,
        norm_mlp_g, norm_ple_g, fin, {n: sq(wts[n]) for n in big_names})

    nrow = 8 * len(SMALL)
    me = _dev_index(*_place())
    conv_rows = total[nrow + 8:nrow + 40:8]
    recv["conv_w"] = lax.dynamic_slice_in_dim(conv_rows, me * 128, 128, axis=1).reshape(1, 4, 128)
    out = {}
    for n, res in zip(big_names, _adamw([(recv[n], sq(wts[n]), sq(mom[n]), sq(var[n])) for n in big_names])):
        out[n] = [t.reshape(wts[n].shape) for t in res]
    sw = dict(norm_mix_g=norm_mix_g, conv_b=conv_b, gate_b=gate_b, mlstm_norm_g=mlstm_norm_g, norm_mlp_g=norm_mlp_g,
              norm_ple_g=norm_ple_g, final_norm_g=fin)
    sm = dict(norm_mix_g=m_norm_mix_g, conv_b=m_conv_b, gate_b=m_gate_b, mlstm_norm_g=m_mlstm_norm_g,
              norm_mlp_g=m_norm_mlp_g, norm_ple_g=m_norm_ple_g, final_norm_g=m_final_norm_g.reshape(1, D))
    sv = dict(norm_mix_g=v_norm_mix_g, conv_b=v_conv_b, gate_b=v_gate_b, mlstm_norm_g=v_mlstm_norm_g,
              norm_mlp_g=v_norm_mlp_g, norm_ple_g=v_norm_ple_g, final_norm_g=v_final_norm_g.reshape(1, D))
    res = _adamw_small(total, first, [sw[n] for n in SMALL], [sm[n] for n in SMALL], [sv[n] for n in SMALL])
    for n, r in zip(SMALL, res):
        out[n] = [t.reshape(final_norm_g.shape) for t in r] if n == "final_norm_g" else list(r)
    order = ("norm_mix_g", "w_in", "conv_w", "conv_b", "gate_b", "mlstm_norm_g", "w_out", "norm_mlp_g", "w_up", "w_down",
             "norm_ple_g", "w_ple_gate", "w_ple", "final_norm_g")
    loss_all = total[nrow, 0]
    return (loss_all, dx[None], *[out[n][0] for n in order], *[out[n][1] for n in order],
            *[out[n][2] for n in order], *[out[n][3] for n in order])
```

```python
import math

import jax
import jax.numpy as jnp
from jax import lax
from jax.experimental import pallas as pl
from jax.experimental.pallas import tpu as pltpu

F32, BF16 = jnp.float32, jnp.bfloat16
S = 4096
D = 1024
AW = 512
MW = 512
DFF = 4096
PLE = 256
IN_W = 3592
PW = 3840
NDEV = 8
EPS = 1e-6
NEG = -1e30
LC = 128
TB = 256
ROPE_THETA = 500000.0
VMEM_LIMIT = 56 * 1024 * 1024
HI = lax.Precision.HIGHEST

ADAM_LR, ADAM_B1, ADAM_B2, ADAM_EPS, ADAM_WD, ADAM_STEP = 0.001, 0.9, 0.999, 1e-08, 0.01, 10


def _params(n_grid=0, **kw):
    sem = dict(dimension_semantics=("arbitrary",) * n_grid) if n_grid else {}
    return pltpu.CompilerParams(vmem_limit_bytes=VMEM_LIMIT, **sem, **kw)


def _cspec(shape):
    nd = len(shape)
    return pl.BlockSpec(shape, lambda *_: (0,) * nd, pipeline_mode=pl.Buffered(1))


def _dot(a, b):
    return jnp.dot(a, b, preferred_element_type=F32)


def _dot_nt(a, b):
    return lax.dot_general(a, b, (((1,), (1,)), ((), ())), preferred_element_type=F32)


def _dot_tn(a, b):
    return lax.dot_general(a, b, (((0,), (0,)), ((), ())), preferred_element_type=F32)


def _bf(x):
    return x.astype(BF16)


def _rms(x):
    rs = lax.rsqrt(jnp.mean(x * x, axis=-1, keepdims=True) + EPS)
    return x * rs, rs


def _rms_bwd(du, n, rs, g):
    dn = du * g
    return rs * (dn - n * jnp.mean(dn * n, axis=-1, keepdims=True))


def _sigmoid(x):
    return 1.0 / (1.0 + jnp.exp(-x))


ROPE_BLK = 512


def _rope_parts():
    def cs(n, step):
        j = lax.broadcasted_iota(jnp.int32, (n, 128), 1) % 64
        pos = (lax.broadcasted_iota(jnp.int32, (n, 128), 0) * step).astype(F32)
        ang = pos * jnp.power(ROPE_THETA, -(j % 8).astype(F32) / 8.0)
        return jnp.cos(ang), jnp.sin(ang)

    return (*cs(ROPE_BLK, 1), *cs(S // ROPE_BLK, ROPE_BLK))


def _rope_fill(co_ref, so_ref, cb_ref, sb_ref, rc_ref, ra_ref, rb_ref):
    j = lax.broadcasted_iota(jnp.int32, (ROPE_BLK, 128), 1) % 64
    co, so = co_ref[...], so_ref[...]
    for t in range(S // ROPE_BLK):
        cb, sb = cb_ref[t:t + 1, :], sb_ref[t:t + 1, :]
        cos, sin = cb * co - sb * so, sb * co + cb * so
        rows = slice(t * ROPE_BLK, (t + 1) * ROPE_BLK)
        rc_ref[rows, :] = jnp.where(j < 16, cos, 1.0)
        ra_ref[rows, :] = jnp.where(j < 8, -sin, 0.0)
        rb_ref[rows, :] = jnp.where((j >= 8) & (j < 16), sin, 0.0)


def _rope(blk, c, a, b):
    return blk * c + pltpu.roll(blk, 120, 1) * a + pltpu.roll(blk, 8, 1) * b


def _rope_bwd(d, c, a, b):
    return d * c + pltpu.roll(d * a, 8, 1) + pltpu.roll(d * b, 120, 1)


def _unrope(t, c, a, b):
    return jnp.concatenate([_bf(_rope_bwd(t[:, j * 128:(j + 1) * 128].astype(F32), c, a, b))
                            for j in range(t.shape[1] // 128)], axis=1)


MESH = pl.DeviceIdType.MESH
ANY = pl.BlockSpec(memory_space=pl.ANY)
VM = pl.BlockSpec(memory_space=pltpu.VMEM)
FLIPS = [(dx, dy, dc) for dx in (0, 1) for dy in (0, 1) for dc in (0, 1)][1:]


def _place():
    return lax.axis_index("x"), lax.axis_index("y"), lax.axis_index("c")


def _dev_index(px, py, pc):
    return 4 * px + 2 * py + pc


def _gather_phases(ins, outs, bufs, send_sems=None, recv_sems=None, local_sems=None):
    nw = len(ins)
    if nw == 0:
        return (lambda: None,) * 3
    x, y, c = _place()
    me, sib = (x, y, c), (x, y, 1 - c)
    chips = [(1 - x, y), (x, 1 - y), (1 - x, 1 - y)]

    def copy(w, k, block, to, from_buf=False):
        dst = outs[w].at[_dev_index(*block)]
        return pltpu.make_async_remote_copy(
            src_ref=bufs[w] if from_buf else dst, dst_ref=dst, send_sem=send_sems.at[w, k],
            recv_sem=recv_sems.at[w, k], device_id=to, device_id_type=MESH)

    def mine(w):
        return pltpu.make_async_copy(bufs[w], outs[w].at[_dev_index(*me)], local_sems.at[w])

    def first(w):
        return [copy(w, 0, me, sib, True)] + [copy(w, 1 + j, me, (*chip, c), True) for j, chip in enumerate(chips)]

    def passed(w):
        return [copy(w, 4 + j, (*chip, c), sib) for j, chip in enumerate(chips)]

    def start():
        for w in range(nw):
            bufs[w][...] = ins[w][...].astype(bufs[w].dtype)
        for w in range(nw):
            mine(w).start()
            for cp in first(w):
                cp.start()

    def forward():
        for j, chip in enumerate(chips):
            for w in range(nw):
                copy(w, 1 + j, (*chip, c), me).wait_recv()
                passed(w)[j].start()

    def finish():
        for w in range(nw):
            copy(w, 0, sib, me).wait_recv()
        for j, chip in enumerate(chips):
            for w in range(nw):
                copy(w, 4 + j, (*chip, 1 - c), me).wait_recv()
        for w in range(nw):
            for cp in first(w) + passed(w):
                cp.wait_send()
            mine(w).wait()

    return start, forward, finish


def _gather_scratch(shards, dtypes):
    nw = len(shards)
    if nw == 0:
        return []
    return ([pltpu.VMEM(s.shape, dt) for s, dt in zip(shards, dtypes)]
            + [pltpu.SemaphoreType.DMA((nw, 7)), pltpu.SemaphoreType.DMA((nw, 7)), pltpu.SemaphoreType.DMA((nw,))])


def _gather_shapes(shards, dtypes):
    return [jax.ShapeDtypeStruct((NDEV, *s.shape), dt) for s, dt in zip(shards, dtypes)]


def _scatter_phases(ins, outs, send_sems=None, recv_sems=None, local_sems=None):
    nw = len(ins)
    if nw == 0:
        return (lambda: None,) * 2
    x, y, c = _place()
    me = _dev_index(x, y, c)

    def copies():
        out = []
        for w in range(nw):
            out.append(pltpu.make_async_copy(ins[w].at[me], outs[w].at[me], local_sems.at[w]))
            for k, (dx, dy, dc) in enumerate(FLIPS):
                peer = ((x + dx) % 2, (y + dy) % 2, (c + dc) % 2)
                out.append(pltpu.make_async_remote_copy(
                    src_ref=ins[w].at[_dev_index(*peer)], dst_ref=outs[w].at[me], send_sem=send_sems.at[w, k],
                    recv_sem=recv_sems.at[w, k], device_id=peer, device_id_type=MESH))
        return out

    def start():
        for cp in copies():
            cp.start()

    def finish():
        for cp in copies():
            cp.wait()

    return start, finish


def _scatter_scratch(nw):
    if nw == 0:
        return []
    return [pltpu.SemaphoreType.DMA((nw, 7)), pltpu.SemaphoreType.DMA((nw, 7)), pltpu.SemaphoreType.DMA((nw,))]


TM = 512


def _join_w_in(wg):
    sw = IN_W // NDEV

    def body(wg_ref, w_ref):
        for j in range(NDEV):
            w_ref[:, sw * j:sw * (j + 1)] = wg_ref[j]
        w_ref[:, IN_W:PW] = jnp.zeros((D, PW - IN_W), BF16)

    return pl.pallas_call(body, name="join_w_in", out_shape=jax.ShapeDtypeStruct((D, PW), BF16),
                          compiler_params=_params())(wg)


def _in_proj(x, g1, w, rc, ra, rb, shards, dtypes):
    tm = TM
    nw = len(shards)
    nt = S // tm

    def body(*refs):
        x_ref, g_ref, w_ref, rc_ref, ra_ref, rb_ref = refs[:6]
        ins = refs[6:6 + nw]
        qkv_ref, mqk_ref, mv_ref, mo_ref, gt_ref, u_ref = refs[6 + nw:12 + nw]
        outs = refs[12 + nw:12 + 2 * nw]
        bufs = refs[12 + 2 * nw:12 + 3 * nw]
        ag_start, ag_forward, ag_finish = _gather_phases(ins, outs, bufs, *refs[12 + 3 * nw:])
        i = pl.program_id(0)
        pl.when(i == 0)(ag_start)
        pl.when(i == nt - 2)(ag_forward)
        n, _ = _rms(x_ref[...])
        u = _bf(n * g_ref[...])
        u_ref[...] = u
        c, a, b = rc_ref[...], ra_ref[...], rb_ref[...]
        for half in range(2):
            blk = _dot(u, w_ref[:, half * 512:(half + 1) * 512])
            for t in range(4):
                lo = half * 512 + t * 128
                qkv_ref[:, lo:lo + 128] = _rope(blk[:, t * 128:(t + 1) * 128], c, a, b)
        qkv_ref[:, 1024:1536] = _dot(u, w_ref[:, 1024:1536])
        mqk_ref[:, 0:512] = _dot(u, w_ref[:, 1536:2048])
        mqk_ref[:, 512:1024] = _dot(u, w_ref[:, 2048:2560])
        mv_ref[...] = _dot(u, w_ref[:, 2560:3072])
        mo_ref[...] = _dot(u, w_ref[:, 3072:3584])
        gt_ref[...] = _dot(u, w_ref[:, 3584:3712])
        pl.when(i == nt - 1)(ag_finish)

    row = lambda wd: pl.BlockSpec((tm, wd), lambda i: (i, 0))
    res = pl.pallas_call(
        body, name="in_proj", grid=(nt,),
        in_specs=[row(D), _cspec((1, D)), _cspec((D, PW)), row(128), row(128), row(128)] + [VM] * nw,
        out_specs=[row(1536), row(1024), row(512), row(512), row(128), row(D)] + [ANY] * nw,
        out_shape=[jax.ShapeDtypeStruct((S, 1536), F32), jax.ShapeDtypeStruct((S, 1024), F32),
                   jax.ShapeDtypeStruct((S, 512), F32), jax.ShapeDtypeStruct((S, 512), F32),
                   jax.ShapeDtypeStruct((S, 128), F32), jax.ShapeDtypeStruct((S, D), BF16)]
        + _gather_shapes(shards, dtypes),
        scratch_shapes=_gather_scratch(shards, dtypes),
        compiler_params=_params(1),
    )(x, g1, w, rc, ra, rb, *shards)
    return res[:6], res[6:]


DILATIONS = (16, 4, 1)


def _attn_valid(n):
    kd = lax.broadcasted_iota(jnp.int32, (128, 256), 1) - lax.broadcasted_iota(jnp.int32, (128, 256), 0)
    off = jnp.where(n == 0, 0, 128)
    return (kd <= off) & (kd >= off - 128)


def _attn_rows(d, r, n):
    if d == 1:
        q0 = pl.multiple_of(n * 128, 128)
        k0 = pl.multiple_of(jnp.maximum(n - 1, 0) * 128, 128)
        return pl.ds(q0, 128), pl.ds(k0, 256), _attn_valid(n)
    q0 = r + n * 128 * d
    k0 = r + jnp.maximum(n - 1, 0) * 128 * d
    return pl.ds(q0, 128, stride=d), pl.ds(k0, 256, stride=d), _attn_valid(n)


ATTN_GROUP = 4
ATTN_ITERS = S // 128 // ATTN_GROUP


def _attn_group(d, i):
    nb = S // (128 * d)
    if nb == 2:
        qi = lax.broadcasted_iota(jnp.int32, (256, 256), 0) - lax.broadcasted_iota(jnp.int32, (256, 256), 1)
        whole = [pl.ds((ATTN_GROUP // 2) * i + u, 256, stride=d) for u in range(ATTN_GROUP // 2)]
        return [(rows, rows, (qi >= 0) & (qi <= 128)) for rows in whole]
    if d == 1:
        return [_attn_rows(1, 0, i + ATTN_ITERS * u) for u in range(ATTN_GROUP)]
    return [_attn_rows(d, (i // nb) * ATTN_GROUP + u, i % nb) for u in range(ATTN_GROUP)]


def _head0(shape):
    return lax.broadcasted_iota(jnp.int32, shape, 1) < 64


def _stack_heads(t):
    h0 = _head0(t.shape)
    tb = _bf(t)
    zero = jnp.zeros_like(tb)
    return jnp.concatenate([jnp.where(h0, tb, zero), jnp.where(h0, zero, tb)], axis=0)


def _attn_fwd(qkv, shards, dtypes):
    nw = len(shards)

    def body(*refs):
        q_ref, k_ref, v_ref = refs[:3]
        ins = refs[3:3 + nw]
        o_ref, lse0_ref, lse1_ref = refs[3 + nw:6 + nw]
        outs = refs[6 + nw:6 + 2 * nw]
        m0, m1, l0, l1, acc = refs[6 + 2 * nw:11 + 2 * nw]
        bufs = refs[11 + 2 * nw:11 + 3 * nw]
        ag_start, ag_forward, ag_finish = _gather_phases(ins, outs, bufs, *refs[11 + 3 * nw:])
        hp = pl.program_id(0)
        pl.when(hp == 0)(ag_start)
        pl.when(hp == 3)(ag_forward)
        stats = (m0, m1, l0, l1, acc)

        def update(blocks, first):
            loaded = [([q_ref[rq, :], k_ref[rk, :], v_ref[rk, :]], None if first else [ref[rq, :] for ref in stats])
                      for rq, rk, _ in blocks]
            both = lambda a, b: jnp.concatenate([a, b], axis=0)
            ss = [jnp.where(both(valid, valid), _dot_nt(_stack_heads(q * 0.125), _bf(k)), NEG)
                  for ((q, k, _), _), (_, _, valid) in zip(loaded, blocks)]
            mcs = [jnp.max(s, axis=-1, keepdims=True) for s in ss]
            if first:
                m2s = [jnp.broadcast_to(mc, (mc.shape[0], 128)) for mc in mcs]
            else:
                m2s = [jnp.maximum(both(prev[0], prev[1]), mc) for mc, (_, prev) in zip(mcs, loaded)]
            ps = [jnp.exp(s - jnp.tile(m2, (1, 2))) for s, m2 in zip(ss, m2s)]
            l2s = [jnp.sum(p, axis=-1, keepdims=True) for p in ps]
            acc2s = [_dot(_bf(p), _bf(v)) for p, ((_, _, v), _) in zip(ps, loaded)]
            results = []
            for m2, l2, acc2, (_, prev) in zip(m2s, l2s, acc2s, loaded):
                nq = m2.shape[0] // 2
                if first:
                    l2 = jnp.broadcast_to(l2, (2 * nq, 128))
                else:
                    alpha = jnp.exp(both(prev[0], prev[1]) - m2)
                    l2, acc2 = alpha * both(prev[2], prev[3]) + l2, alpha * both(prev[4], prev[4]) + acc2
                results.append((m2[0:nq], m2[nq:2 * nq], l2[0:nq], l2[nq:2 * nq],
                                jnp.where(_head0((nq, 128)), acc2[0:nq], acc2[nq:2 * nq])))
            for (rq, _, _), res in zip(blocks, results):
                for ref, val in zip(stats, res):
                    ref[rq, :] = val

        for d in DILATIONS:
            def step(i, carry, d=d):
                update(_attn_group(d, i), d == DILATIONS[0])
                return carry

            lax.fori_loop(0, ATTN_ITERS, step, 0)

        def fin(t, carry):
            rows = pl.ds(pl.multiple_of(t * 256, 256), 256)
            h0 = lax.broadcasted_iota(jnp.int32, (256, 128), 1) < 64
            la, lb = l0[rows, :], l1[rows, :]
            o_ref[rows, :] = acc[rows, :] / jnp.where(h0, la, lb)
            lse0_ref[rows, :] = m0[rows, :] + jnp.log(la)
            lse1_ref[rows, :] = m1[rows, :] + jnp.log(lb)
            return carry

        lax.fori_loop(0, S // 256, fin, 0)
        pl.when(hp == 3)(ag_finish)

    col = lambda off: pl.BlockSpec((S, 128), lambda h, off=off: (0, off + h))
    res = pl.pallas_call(
        body, name="attn_fwd", grid=(4,),
        in_specs=[col(0), col(4), col(8)] + [VM] * nw,
        out_specs=[col(0), col(0), col(0)] + [ANY] * nw,
        out_shape=[jax.ShapeDtypeStruct((S, AW), F32)] * 3 + _gather_shapes(shards, dtypes),
        scratch_shapes=[pltpu.VMEM((S, 128), F32)] * 5 + _gather_scratch(shards, dtypes),
        compiler_params=_params(1),
    )(qkv, qkv, qkv, *shards)
    return res[0], (res[1], res[2]), res[3:]


def _attn_bwd(qkv, o, lse, do, parts):
    nw = len(parts)

    def body(*refs):
        q_ref, k_ref, v_ref, o_ref, L0, L1, do_ref = refs[:7]
        ins = refs[7:7 + nw]
        dq_out, dk_out, dv_out = refs[7 + nw:10 + nw]
        outs = refs[10 + nw:10 + 2 * nw]
        D0, D1, dq_ref, dk_ref, dv_ref = refs[10 + 2 * nw:15 + 2 * nw]
        rs_start, rs_finish = _scatter_phases(ins, outs, *refs[15 + 2 * nw:])
        hp = pl.program_id(0)
        pl.when(hp == 0)(rs_start)

        def pre(t, carry):
            rows = pl.ds(pl.multiple_of(t * 256, 256), 256)
            h0 = lax.broadcasted_iota(jnp.int32, (256, 128), 1) < 64
            dd = do_ref[rows, :] * o_ref[rows, :]
            shp = (256, 128)
            D0[rows, :] = jnp.broadcast_to(jnp.sum(jnp.where(h0, dd, 0.0), axis=-1, keepdims=True), shp)
            D1[rows, :] = jnp.broadcast_to(jnp.sum(jnp.where(h0, 0.0, dd), axis=-1, keepdims=True), shp)
            return carry

        lax.fori_loop(0, S // 256, pre, 0)

        def update(blocks, first):
            loaded = [([q_ref[rq, :], k_ref[rk, :], v_ref[rk, :], do_ref[rq, :]],
                       [L0[rq, :], L1[rq, :], D0[rq, :], D1[rq, :]],
                       [0.0] * 3 if first else [dq_ref[rq, :], dk_ref[rk, :], dv_ref[rk, :]]) for rq, rk, _ in blocks]
            cat = lambda a, b: jnp.tile(jnp.concatenate([a, b], axis=0), (1, 2))
            ops = [(_stack_heads(q * 0.125), _stack_heads(q), _stack_heads(dout), _bf(k), _bf(v))
                   for (q, k, v, dout), _, _ in loaded]
            ss = [jnp.where(jnp.concatenate([valid, valid], axis=0), _dot_nt(qs, kb), NEG)
                  for (qs, _, _, kb, _), (_, _, valid) in zip(ops, blocks)]
            dps = [_dot_nt(do2, vb) for _, _, do2, _, vb in ops]
            ps = [jnp.exp(s - cat(st[0], st[1])) for s, (_, st, _) in zip(ss, loaded)]
            dss = [_bf(p * (dp - cat(st[2], st[3])) * 0.125) for p, dp, (_, st, _) in zip(ps, dps, loaded)]
            dq2s = [_dot(ds, kb) for ds, (_, _, _, kb, _) in zip(dss, ops)]
            dks = [_dot_tn(ds, q2) for ds, (_, q2, _, _, _) in zip(dss, ops)]
            dvs = [_dot_tn(_bf(p), do2) for p, (_, _, do2, _, _) in zip(ps, ops)]
            results = []
            for (_, _, (dq, dk, dv)), dq2, dkk, dvv in zip(loaded, dq2s, dks, dvs):
                nq = dq2.shape[0] // 2
                results.append((dq + jnp.where(_head0((nq, 128)), dq2[0:nq], dq2[nq:2 * nq]), dk + dkk, dv + dvv))
            for (rq, rk, _), (dq, dk, dv) in zip(blocks, results):
                dq_ref[rq, :] = dq
                dk_ref[rk, :] = dk
                dv_ref[rk, :] = dv

        assert S // (128 * DILATIONS[0]) == 2
        for d in DILATIONS:
            def step(i, carry, d=d):
                update(_attn_group(d, i), d == DILATIONS[0])
                return carry

            lax.fori_loop(0, ATTN_ITERS, step, 0)

        def fin(t, carry):
            rows = pl.ds(pl.multiple_of(t * 256, 256), 256)
            for src, dst in ((dq_ref, dq_out), (dk_ref, dk_out), (dv_ref, dv_out)):
                dst[rows, :] = _bf(src[rows, :])
            return carry

        lax.fori_loop(0, S // 256, fin, 0)
        pl.when(hp == 3)(rs_finish)

    col = lambda off: pl.BlockSpec((S, 128), lambda h, off=off: (0, off + h))
    res = pl.pallas_call(
        body, name="attn_bwd", grid=(4,),
        in_specs=[col(0), col(4), col(8), col(0), col(0), col(0), col(0)] + [ANY] * nw,
        out_specs=[col(0), col(0), col(0)] + [ANY] * nw,
        out_shape=[jax.ShapeDtypeStruct((S, AW), BF16)] * 3 + [jax.ShapeDtypeStruct(a.shape, a.dtype) for a in parts],
        scratch_shapes=[pltpu.VMEM((S, 128), F32)] * 5 + _scatter_scratch(nw),
        compiler_params=_params(1),
    )(qkv, qkv, qkv, o, lse[0], lse[1], do, *parts)
    return res[0], res[1], res[2], res[3:]


def _logsig(x):
    return jnp.minimum(x, 0.0) - jnp.log1p(jnp.exp(-jnp.abs(x)))


def _conv_taps(xp, n):
    return [xp[8:] if j == 3 else pltpu.roll(xp, 3 - j, 0)[8:] for j in range(4)]


def _conv_silu(xp, w_ref, b_ref, n):
    taps = _conv_taps(xp, n)
    c = b_ref[...] + sum(w_ref[j:j + 1, :] * taps[j] for j in range(4))
    sg = _sigmoid(c)
    return c, sg, taps


def _chunk_gates(G):
    assert LC == 128
    r = lax.broadcasted_iota(jnp.int32, (LC, LC), 0)
    c = lax.broadcasted_iota(jnp.int32, (LC, LC), 1)
    tril = (c <= r).astype(F32)
    triu = (c >= r).astype(F32)
    b_col = jnp.dot(tril, _logsig(G), preferred_element_type=F32, precision=HI)
    return b_col, b_col.T, G.T, tril, triu


def _colpick(X, lane):
    li = lax.broadcasted_iota(jnp.int32, X.shape, 1)
    return jnp.sum(jnp.where(li == lane, X, 0.0), axis=1, keepdims=True)


def _rowpick(XT, row):
    ri = lax.broadcasted_iota(jnp.int32, XT.shape, 0)
    return jnp.sum(jnp.where(ri == row, XT, 0.0), axis=0, keepdims=True)


def _mlstm_head(qh, kh, vh, G, b_col, b_row, g_row, h, Ch, nh, m_prev):
    bt = _colpick(b_col, 4 + h)
    i_col = _colpick(G, h)
    bs = _rowpick(b_row, 4 + h)
    i_row = _rowpick(g_row, h)
    r = lax.broadcasted_iota(jnp.int32, (LC, LC), 0)
    c = lax.broadcasted_iota(jnp.int32, (LC, LC), 1)
    log_d = jnp.where(c <= r, bt - bs + i_row, NEG)
    log_inter = bt + m_prev
    m_t = jnp.maximum(log_inter, jnp.max(log_d, axis=1, keepdims=True))
    Dm = jnp.exp(log_d - m_t)
    g = jnp.exp(log_inter - m_t)
    qb, kb, vb = _bf(qh), _bf(kh), _bf(vh)
    Am = _dot_nt(qb, kb) * Dm
    qC = _dot(qb, _bf(Ch))
    num = g * qC + _dot(_bf(Am), vb)
    qn = jnp.sum(qh * nh, axis=1, keepdims=True)
    den = g * qn + jnp.sum(Am, axis=1, keepdims=True)
    floor = jnp.exp(-m_t)
    dd = jnp.maximum(jnp.abs(den), floor)
    inv_dd = 1.0 / dd
    hh = num * inv_dd
    lane = lax.broadcasted_iota(jnp.int32, (1, LC), 1)
    blast = jnp.sum(jnp.where(lane == LC - 1, bs, 0.0), axis=1, keepdims=True)
    log_s = blast - bt + i_col
    m_new = jnp.maximum(blast + m_prev, jnp.max(log_s, axis=0, keepdims=True))
    decay = jnp.exp(blast + m_prev - m_new)
    ws = jnp.exp(log_s - m_new)
    kw = kh * ws
    C_new = decay * Ch + _dot_tn(_bf(kw), vb)
    n_new = decay * nh + jnp.sum(kw, axis=0, keepdims=True)
    return dict(Dm=Dm, g=g, Am=Am, qC=qC, qn=qn, den=den, floor=floor, inv_dd=inv_dd, h=hh, decay=decay, ws=ws, kw=kw,
                C_new=C_new, n_new=n_new, m_new=m_new, qb=qb, kb=kb, vb=vb)


def _head_out(hh, mo_h, gn_h):
    r = lax.rsqrt(jnp.mean(hh * hh, axis=-1, keepdims=True) + EPS)
    hn = hh * r
    sg = _sigmoid(mo_h)
    return sg * (hn * gn_h), hn, r, sg


def _mlstm_fwd(mqk, mv, mo, gates, conv_w, conv_b, gate_b, gn, shards, dtypes):
    nblk = S // TB
    ncb = TB // LC
    nw = len(shards)

    def body(*refs):
        x_ref, v_ref, o_ref, g_ref, w_ref, b_ref, gb_ref, gn_ref = refs[:8]
        ins = refs[8:8 + nw]
        out_ref, cs_ref, ns_ref, ms_ref = refs[8 + nw:12 + nw]
        outs = refs[12 + nw:12 + 2 * nw]
        tail, Cst, nst, mst, qs, ks = refs[12 + 2 * nw:18 + 2 * nw]
        bufs = refs[18 + 2 * nw:18 + 3 * nw]
        ag_start, ag_forward, ag_finish = _gather_phases(ins, outs, bufs, *refs[18 + 3 * nw:])
        i = pl.program_id(0)
        pl.when(i == 0)(ag_start)
        pl.when(i == nblk // 2)(ag_forward)

        @pl.when(i == 0)
        def _():
            tail[...] = jnp.zeros_like(tail)
            Cst[...] = jnp.zeros_like(Cst)
            nst[...] = jnp.zeros_like(nst)
            mst[...] = jnp.zeros_like(mst)

        x = x_ref[...]
        xp = jnp.concatenate([tail[...], x], axis=0)
        tail[...] = x[TB - 8:TB, :]
        c, sg, _ = _conv_silu(xp, w_ref, b_ref, TB)
        y = c * sg
        qs[...] = y[:, 0:MW]
        ks[...] = y[:, MW:2 * MW] * (1.0 / math.sqrt(128.0))

        for cc in range(ncb):
            rows = slice(cc * LC, (cc + 1) * LC)
            G = g_ref[rows, :] + gb_ref[...]
            b_col, b_row, g_row, _, _ = _chunk_gates(G)
            cs_ref[cc] = Cst[...]
            ns_ref[cc] = nst[...]
            ms_ref[cc] = mst[...]
            for h in range(4):
                ln = slice(h * 128, (h + 1) * 128)
                m_prev = jnp.max(mst[0:1, ln], axis=1, keepdims=True)
                f = _mlstm_head(qs[rows, ln], ks[rows, ln], v_ref[rows, ln], G, b_col, b_row, g_row, h,
                                Cst[:, ln], nst[0:1, ln], m_prev)
                out, _, _, _ = _head_out(f["h"], o_ref[rows, ln], gn_ref[:, ln])
                out_ref[rows, ln] = out
                Cst[:, ln] = f["C_new"]
                nst[0:1, ln] = f["n_new"]
                mst[0:1, ln] = jnp.broadcast_to(f["m_new"], (1, 128))
        pl.when(i == nblk - 1)(ag_finish)

    row = lambda wd: pl.BlockSpec((TB, wd), lambda i: (i, 0))
    res = pl.pallas_call(
        body, name="mlstm_fwd", grid=(nblk,),
        in_specs=[row(1024), row(MW), row(MW), row(128), _cspec((4, 1024)), _cspec((1, 1024)), _cspec((1, 128)),
                  _cspec((1, MW))] + [VM] * nw,
        out_specs=[row(MW), pl.BlockSpec((ncb, 128, MW), lambda i: (i, 0, 0)),
                   pl.BlockSpec((ncb, 8, MW), lambda i: (i, 0, 0)), pl.BlockSpec((ncb, 8, MW), lambda i: (i, 0, 0))]
        + [ANY] * nw,
        out_shape=[jax.ShapeDtypeStruct((S, MW), F32), jax.ShapeDtypeStruct((S // LC, 128, MW), F32),
                   jax.ShapeDtypeStruct((S // LC, 8, MW), F32), jax.ShapeDtypeStruct((S // LC, 8, MW), F32)]
        + _gather_shapes(shards, dtypes),
        scratch_shapes=[pltpu.VMEM((8, 1024), F32), pltpu.VMEM((128, MW), F32), pltpu.VMEM((8, MW), F32),
                        pltpu.VMEM((8, MW), F32), pltpu.VMEM((TB, MW), F32), pltpu.VMEM((TB, MW), F32)]
        + _gather_scratch(shards, dtypes),
        compiler_params=_params(1),
    )(mqk, mv, mo, gates, conv_w, conv_b, gate_b, gn, *shards)
    return res[0], res[1], res[2], res[3], res[4:]


DM_V, DM_O, DM_G, DM_W = 1024, 1536, 2048, PW - 3 * AW


def _mlstm_bwd(mqk, mv, mo, gates, conv_w, conv_b, gate_b, gn, cs, ns, ms, dout, parts):
    nblk = S // TB
    ncb = TB // LC
    kscale = 1.0 / math.sqrt(128.0)
    nw = len(parts)

    def body(*refs):
        x_ref, xprev_ref, v_ref, o_ref, g_ref, w_ref, b_ref, gb_ref, gn_ref, cs_ref, ns_ref, ms_ref, do_ref = refs[:13]
        ins = refs[13:13 + nw]
        dm_ref, dw_ref, db_ref, dgn_ref, dgb_ref = refs[13 + nw:18 + nw]
        outs = refs[18 + nw:18 + 2 * nw]
        dCst, dnst, dyhead, qs, ks, dqk = refs[18 + 2 * nw:24 + 2 * nw]
        rs_start, rs_finish = _scatter_phases(ins, outs, *refs[24 + 2 * nw:])
        i = pl.program_id(0)
        blk = nblk - 1 - i
        pl.when(i == 0)(rs_start)

        @pl.when(i == 0)
        def _():
            dCst[...] = jnp.zeros_like(dCst)
            dnst[...] = jnp.zeros_like(dnst)
            dyhead[...] = jnp.zeros_like(dyhead)
            dw_ref[...] = jnp.zeros_like(dw_ref)
            db_ref[...] = jnp.zeros_like(db_ref)
            dgn_ref[...] = jnp.zeros_like(dgn_ref)
            dgb_ref[...] = jnp.zeros_like(dgb_ref)

        x = x_ref[...]
        xprev = jnp.where(blk == 0, 0.0, xprev_ref[...])
        xp = jnp.concatenate([xprev, x], axis=0)
        c, sg, taps = _conv_silu(xp, w_ref, b_ref, TB)
        y = c * sg
        qs[...] = y[:, 0:MW]
        ks[...] = y[:, MW:2 * MW] * kscale
        lane128 = lax.broadcasted_iota(jnp.int32, (LC, 128), 1)
        rowi = lax.broadcasted_iota(jnp.int32, (LC, 1), 0)
        ones = jnp.ones((LC, 128), F32)

        for cc in reversed(range(ncb)):
            rows = slice(cc * LC, (cc + 1) * LC)
            G = g_ref[rows, :] + gb_ref[...]
            b_col, b_row, g_row, _, triu = _chunk_gates(G)
            dB = jnp.zeros((LC, 128), F32)
            dI = jnp.zeros((LC, 128), F32)
            for h in range(4):
                ln = slice(h * 128, (h + 1) * 128)
                Ch = cs_ref[cc, :, ln]
                nh = ns_ref[cc, 0:1, ln]
                m_prev = jnp.max(ms_ref[cc, 0:1, ln], axis=1, keepdims=True)
                qh, kh, vh = qs[rows, ln], ks[rows, ln], v_ref[rows, ln]
                f = _mlstm_head(qh, kh, vh, G, b_col, b_row, g_row, h, Ch, nh, m_prev)
                hh, inv_dd, den, g, Am, Dm = f["h"], f["inv_dd"], f["den"], f["g"], f["Am"], f["Dm"]
                qb, kb, vb = f["qb"], f["kb"], f["vb"]
                gn_h = gn_ref[:, ln]
                _, hn, r, sgo = _head_out(hh, o_ref[rows, ln], gn_h)
                do = do_ref[rows, ln]
                hm = hn * gn_h
                dm_ref[rows, DM_O + h * 128:DM_O + (h + 1) * 128] = _bf(do * hm * sgo * (1.0 - sgo))
                dhm = do * sgo
                dgn_ref[:, ln] = dgn_ref[:, ln] + jnp.sum(dhm * hn, axis=0, keepdims=True)
                dhn = dhm * gn_h
                dh = r * (dhn - hn * jnp.mean(dhn * hn, axis=-1, keepdims=True))
                dnum = dh * inv_dd
                ddd = -jnp.sum(dh * hh, axis=1, keepdims=True) * inv_dd
                dden = jnp.where(jnp.abs(den) >= f["floor"], ddd * jnp.sign(den), 0.0)
                dnb = _bf(dnum)
                dA = _dot_nt(dnb, vb) + dden
                dv = _dot_tn(_bf(Am), dnb)
                gd = _bf(g * dnum)
                gq = g * dden
                dq = _dot_nt(gd, _bf(Ch)) + gq * nh
                dCn = dCst[:, ln]
                dnn = dnst[0:1, ln]
                dC = f["decay"] * dCn + _dot_tn(qb, gd)
                dn = f["decay"] * dnn + jnp.sum(gq * qh, axis=0, keepdims=True)
                dg = jnp.sum(dnum * f["qC"], axis=1, keepdims=True) + dden * f["qn"]
                dS = _bf(dA * Dm)
                dq = dq + _dot(dS, kb)
                dk = _dot_tn(dS, qb)
                Gm = dA * Am
                gam = dg * g
                dCb = _bf(dCn)
                E = _dot_nt(vb, dCb) + dnn
                ws = f["ws"]
                dk = dk + ws * E
                om = jnp.sum(E * kh, axis=1, keepdims=True) * ws
                dv = dv + _dot(_bf(f["kw"]), dCb)
                ddecay = (jnp.sum(jnp.sum(dCn * Ch, axis=1, keepdims=True), axis=0, keepdims=True)
                          + jnp.sum(dnn * nh, axis=1, keepdims=True))
                delta = ddecay * f["decay"]
                rows_g = jnp.sum(Gm, axis=1, keepdims=True)
                cols_g = jnp.broadcast_to(jnp.sum(Gm, axis=0, keepdims=True), (LC, 128)).T
                last = jnp.where(rowi == LC - 1, jnp.sum(om, axis=0, keepdims=True) + delta, 0.0)
                db = rows_g + gam - om + last - cols_g
                di = cols_g + om
                dB = jnp.where(lane128 == 4 + h, db, dB)
                dI = jnp.where(lane128 == h, di, dI)
                dCst[:, ln] = dC
                dnst[0:1, ln] = dn
                dqk[rows, ln] = dq
                dqk[rows, MW + h * 128:MW + (h + 1) * 128] = dk * kscale
                dm_ref[rows, DM_V + h * 128:DM_V + (h + 1) * 128] = _bf(dv)
            dlogf = jnp.dot(triu, dB, preferred_element_type=F32, precision=HI)
            dG = dI + dlogf * _sigmoid(-G)
            dG = jnp.where(lane128 < 8, dG, 0.0)
            dm_ref[rows, DM_G:DM_G + 128] = _bf(dG)
            dm_ref[rows, DM_G + 128:DM_W] = jnp.zeros((LC, DM_W - DM_G - 128), BF16)
            dgb_ref[...] = dgb_ref[...] + jnp.sum(dG, axis=0, keepdims=True)

        dy = dqk[...] * (sg * (1.0 + c * (1.0 - sg)))
        db_ref[...] = db_ref[...] + jnp.sum(dy, axis=0, keepdims=True)
        for j in range(4):
            dw_ref[j:j + 1, :] = dw_ref[j:j + 1, :] + jnp.sum(dy * taps[j], axis=0, keepdims=True)
        dyp = jnp.concatenate([dy, dyhead[...]], axis=0)
        dx = w_ref[3:4, :] * dy
        for j in range(3):
            dx = dx + w_ref[j:j + 1, :] * pltpu.roll(dyp, TB + 8 - (3 - j), 0)[0:TB]
        dm_ref[:, 0:DM_V] = _bf(dx)
        dyhead[...] = dy[0:8, :]
        pl.when(i == nblk - 1)(rs_finish)

    rrow = lambda wd: pl.BlockSpec((TB, wd), lambda i: (nblk - 1 - i, 0))
    st = lambda r: pl.BlockSpec((ncb, r, MW), lambda i: (nblk - 1 - i, 0, 0))
    prev8 = pl.BlockSpec((8, 1024), lambda i: (jnp.maximum((nblk - 1 - i) * (TB // 8) - 1, 0), 0))
    res = pl.pallas_call(
        body, name="mlstm_bwd", grid=(nblk,),
        in_specs=[rrow(1024), prev8, rrow(MW), rrow(MW), rrow(128), _cspec((4, 1024)), _cspec((1, 1024)),
                  _cspec((1, 128)), _cspec((1, MW)), st(128), st(8), st(8), rrow(MW)] + [ANY] * nw,
        out_specs=[rrow(DM_W),
                   pl.BlockSpec((4, 1024), lambda i: (0, 0)), pl.BlockSpec((1, 1024), lambda i: (0, 0)),
                   pl.BlockSpec((1, MW), lambda i: (0, 0)), pl.BlockSpec((1, 128), lambda i: (0, 0))] + [ANY] * nw,
        out_shape=[jax.ShapeDtypeStruct((S, DM_W), BF16),
                   jax.ShapeDtypeStruct((4, 1024), F32), jax.ShapeDtypeStruct((1, 1024), F32),
                   jax.ShapeDtypeStruct((1, MW), F32), jax.ShapeDtypeStruct((1, 128), F32)]
        + [jax.ShapeDtypeStruct(a.shape, a.dtype) for a in parts],
        scratch_shapes=[pltpu.VMEM((128, MW), F32), pltpu.VMEM((8, MW), F32), pltpu.VMEM((8, 1024), F32),
                        pltpu.VMEM((TB, MW), F32), pltpu.VMEM((TB, MW), F32), pltpu.VMEM((TB, 1024), F32)]
        + _scatter_scratch(nw),
        compiler_params=_params(1),
    )(mqk, mqk, mv, mo, gates, conv_w, conv_b, gate_b, gn, cs, ns, ms, dout, *parts)
    return res[:5], res[5:]


def _out_proj(x, attn, ml, w, g):
    tm = TM

    def body(x_ref, a_ref, m_ref, w_ref, g_ref, h_ref, u_ref):
        h1 = x_ref[...] + _dot(_bf(a_ref[...]), w_ref[0:AW, :]) + _dot(_bf(m_ref[...]), w_ref[AW:D, :])
        h_ref[...] = h1
        n, _ = _rms(h1)
        u_ref[...] = _bf(n * g_ref[...])

    row = lambda wd: pl.BlockSpec((tm, wd), lambda i: (i, 0))
    return pl.pallas_call(
        body, name="out_proj", grid=(S // tm,),
        in_specs=[row(D), row(AW), row(MW), _cspec((D, D)), _cspec((1, D))],
        out_specs=[row(D), row(D)],
        out_shape=[jax.ShapeDtypeStruct((S, D), F32), jax.ShapeDtypeStruct((S, D), BF16)],
        compiler_params=_params(1),
    )(x, attn, ml, w, g)


HALF = DFF // NDEV // 2


def _mlp_fwd(h1, u2, w_up, w_down_a, w_down_b):
    tm = TM

    def body(h_ref, u_ref, wu_ref, wa_ref, wb_ref, a_ref, o_ref):
        u = u_ref[...]
        acc = h_ref[...]
        for c in range(NDEV):
            cols = slice(c * 512, (c + 1) * 512)
            a = _dot(u, wu_ref[c])
            a_ref[:, cols] = _bf(a)
            r = jnp.maximum(a, 0.0)
            r = _bf(r * r)
            acc = acc + _dot(r[:, 0:HALF], wa_ref[c]) + _dot(r[:, HALF:2 * HALF], wb_ref[c])
        o_ref[...] = acc

    row = lambda wd: pl.BlockSpec((tm, wd), lambda i: (i, 0))
    return pl.pallas_call(
        body, name="mlp_fwd", grid=(S // tm,),
        in_specs=[row(D), row(D), _cspec((NDEV, D, DFF // NDEV)), _cspec((NDEV, HALF, D)), _cspec((NDEV, HALF, D))],
        out_specs=[row(DFF), row(D)],
        out_shape=[jax.ShapeDtypeStruct((S, DFF), BF16), jax.ShapeDtypeStruct((S, D), F32)],
        compiler_params=_params(1),
    )(h1, u2, w_up, w_down_a, w_down_b)


def _ple_loss(h2, p, target, w_pg, w_ple, g_ple, g_fin):
    tm = TM

    def body(h_ref, p_ref, t_ref, wg_ref, wp_ref, gp_ref, gf_ref,
             dh_ref, dwg_ref, dwp_ref, dgp_ref, dgf_ref, loss_ref, acc_g, acc_p):
        i = pl.program_id(0)

        @pl.when(i == 0)
        def _():
            acc_g[...] = jnp.zeros_like(acc_g)
            acc_p[...] = jnp.zeros_like(acc_p)
            dgp_ref[...] = jnp.zeros_like(dgp_ref)
            dgf_ref[...] = jnp.zeros_like(dgf_ref)
            loss_ref[...] = jnp.zeros_like(loss_ref)

        h2v = h_ref[...]
        n2, rs2 = _rms(h2v)
        u3 = _bf(n2 * gp_ref[...])
        gt = _sigmoid(_dot(u3, wg_ref[...]))
        pb = _bf(p_ref[...])
        e = jnp.concatenate([_dot(pb, wp_ref[j]) for j in range(NDEV)], axis=1)
        h3 = h2v + gt * e
        n3, rs3 = _rms(h3)
        err = n3 * gf_ref[...] - t_ref[...]
        loss_ref[...] = loss_ref[...] + 0.5 / D * jnp.sum(jnp.sum(err * err, axis=1, keepdims=True), axis=0, keepdims=True)
        dy = err * (1.0 / D)
        dgf_ref[...] = dgf_ref[...] + jnp.sum(dy * n3, axis=0, keepdims=True)
        dh3 = _rms_bwd(dy, n3, rs3, gf_ref[...])
        de = _bf(dh3 * gt)
        dz = _bf(dh3 * e * gt * (1.0 - gt))
        acc_p[...] = acc_p[...] + _dot_tn(pb, de)
        acc_g[...] = acc_g[...] + _dot_tn(u3, dz)
        du3 = _dot_nt(dz, wg_ref[...])
        dgp_ref[...] = dgp_ref[...] + jnp.sum(du3 * n2, axis=0, keepdims=True)
        dh_ref[...] = dh3 + _rms_bwd(du3, n2, rs2, gp_ref[...])

        @pl.when(i == S // tm - 1)
        def _():
            dwg_ref[...] = _bf(acc_g[...])
            for j in range(NDEV):
                dwp_ref[j] = _bf(acc_p[:, j * 128:(j + 1) * 128])

    row = lambda wd: pl.BlockSpec((tm, wd), lambda i: (i, 0))
    whole = lambda shp: pl.BlockSpec(shp, lambda i: (0,) * len(shp))
    return pl.pallas_call(
        body, name="ple_loss", grid=(S // tm,),
        in_specs=[row(D), row(PLE), row(D), _cspec((D, D)), _cspec((NDEV, PLE, 128)), _cspec((1, D)), _cspec((1, D))],
        out_specs=[row(D), whole((D, D)), whole((NDEV, PLE, 128)), whole((1, D)), whole((1, D)), whole((1, 1))],
        out_shape=[jax.ShapeDtypeStruct((S, D), F32), jax.ShapeDtypeStruct((D, D), BF16),
                   jax.ShapeDtypeStruct((NDEV, PLE, 128), BF16), jax.ShapeDtypeStruct((1, D), F32),
                   jax.ShapeDtypeStruct((1, D), F32), jax.ShapeDtypeStruct((1, 1), F32)],
        scratch_shapes=[pltpu.VMEM((D, D), F32), pltpu.VMEM((PLE, D), F32)],
        compiler_params=_params(1),
    )(h2, p, target, w_pg, w_ple, g_ple, g_fin)


def _mlp_bwd(dh2, a, h1, g, w_up, w_down_a, w_down_b, parts):
    tm = TM
    nt = S // tm
    nw = len(parts)

    def body(*refs):
        d_ref, a_ref, h_ref, g_ref, wu_ref, wa_ref, wb_ref = refs[:7]
        ins = refs[7:7 + nw]
        da_ref, dh1_ref, dg_ref = refs[7 + nw:10 + nw]
        outs = refs[10 + nw:10 + 2 * nw]
        rs_start, rs_finish = _scatter_phases(ins, outs, *refs[10 + 2 * nw:])
        i = pl.program_id(0)
        pl.when(i == 0)(rs_start)

        @pl.when(i == 0)
        def _():
            dg_ref[...] = jnp.zeros_like(dg_ref)

        dh2v = d_ref[...]
        db = _bf(dh2v)
        du = jnp.zeros((tm, D), F32)
        for c in range(NDEV):
            cols = slice(c * 512, (c + 1) * 512)
            dr = jnp.concatenate([_dot_nt(db, wa_ref[c]), _dot_nt(db, wb_ref[c])], axis=1)
            da = _bf(dr * (2.0 * jnp.maximum(a_ref[:, cols], 0.0)))
            da_ref[:, cols] = da
            du = du + _dot_nt(da, wu_ref[c])
        n, rs = _rms(h_ref[...])
        dg_ref[...] = dg_ref[...] + jnp.sum(du * n, axis=0, keepdims=True)
        dh1_ref[...] = dh2v + _rms_bwd(du, n, rs, g_ref[...])
        pl.when(i == nt - 1)(rs_finish)

    row = lambda wd: pl.BlockSpec((tm, wd), lambda i: (i, 0))
    res = pl.pallas_call(
        body, name="mlp_bwd", grid=(nt,),
        in_specs=[row(D), row(DFF), row(D), _cspec((1, D)), _cspec((NDEV, D, DFF // NDEV)), _cspec((NDEV, HALF, D)),
                  _cspec((NDEV, HALF, D))] + [ANY] * nw,
        out_specs=[row(DFF), row(D), pl.BlockSpec((1, D), lambda i: (0, 0))] + [ANY] * nw,
        out_shape=[jax.ShapeDtypeStruct((S, DFF), BF16), jax.ShapeDtypeStruct((S, D), F32),
                   jax.ShapeDtypeStruct((1, D), F32)] + [jax.ShapeDtypeStruct(p.shape, p.dtype) for p in parts],
        scratch_shapes=_scatter_scratch(nw),
        compiler_params=_params(1),
    )(dh2, a, h1, g, w_up, w_down_a, w_down_b, *parts)
    return res[0], res[1], res[2], res[3:]


def _out_proj_bwd(dh1, attn, ml, w):
    tm = TM

    def body(d_ref, a_ref, m_ref, w_ref, da_ref, dm_ref, dw_ref, acc):
        i = pl.program_id(0)

        @pl.when(i == 0)
        def _():
            acc[...] = jnp.zeros_like(acc)

        db = _bf(d_ref[...])
        dmix = _dot_nt(db, w_ref[...])
        da_ref[...] = dmix[:, 0:AW]
        dm_ref[...] = dmix[:, AW:D]
        acc[0:AW, :] = acc[0:AW, :] + _dot_tn(_bf(a_ref[...]), db)
        acc[AW:D, :] = acc[AW:D, :] + _dot_tn(_bf(m_ref[...]), db)

        @pl.when(i == S // tm - 1)
        def _():
            dw_ref[...] = _bf(acc[...])

    row = lambda wd: pl.BlockSpec((tm, wd), lambda i: (i, 0))
    return pl.pallas_call(
        body, name="out_proj_bwd", grid=(S // tm,),
        in_specs=[row(D), row(AW), row(MW), _cspec((D, D))],
        out_specs=[row(AW), row(MW), pl.BlockSpec((D, D), lambda i: (0, 0))],
        out_shape=[jax.ShapeDtypeStruct((S, AW), F32), jax.ShapeDtypeStruct((S, MW), F32),
                   jax.ShapeDtypeStruct((D, D), BF16)],
        scratch_shapes=[pltpu.VMEM((D, D), F32)],
        compiler_params=_params(1),
    )(dh1, attn, ml, w)


CHIP_FLIPS = [(0, 0), (0, 1), (1, 0), (1, 1)]


def _scatter2_phases(in_ref, out_ref, mine_v, sib_v, psum_v, loc_sems, d2d_send, d2d_recv, ici_send, ici_recv, own_sem):
    x, y, c = _place()
    chips = [((x + dx) % 2, (y + dy) % 2) for dx, dy in CHIP_FLIPS]
    nc = len(chips)

    def local(k):
        return pltpu.make_async_copy(in_ref.at[_dev_index(*chips[k], c)], mine_v.at[k], loc_sems.at[k])

    def to_sib(k):
        return pltpu.make_async_remote_copy(
            src_ref=in_ref.at[_dev_index(*chips[k], 1 - c)], dst_ref=sib_v.at[k], send_sem=d2d_send.at[k],
            recv_sem=d2d_recv.at[k], device_id=(x, y, 1 - c), device_id_type=MESH)

    def over_ici(k):
        return pltpu.make_async_remote_copy(
            src_ref=psum_v.at[k], dst_ref=out_ref.at[k], send_sem=ici_send.at[k - 1], recv_sem=ici_recv.at[k - 1],
            device_id=(*chips[k], c), device_id_type=MESH)

    def own():
        return pltpu.make_async_copy(psum_v.at[0], out_ref.at[0], own_sem)

    def start():
        for k in range(nc):
            to_sib(k).start()
            local(k).start()

    def middle():
        for k in (1, 2, 3, 0):
            local(k).wait()
            to_sib(k).wait_recv()
            psum_v[k] = _bf(mine_v[k].astype(F32) + sib_v[k].astype(F32))
            (over_ici(k) if k else own()).start()

    def finish():
        for k in range(1, nc):
            over_ici(k).wait()
        for k in range(nc):
            to_sib(k).wait_send()
        own().wait()

    return start, middle, finish


def _scatter2_scratch(shard, dtype):
    nc = len(CHIP_FLIPS)
    return ([pltpu.VMEM((nc, *shard), dtype)] * 3
            + [pltpu.SemaphoreType.DMA((nc,))] * 3 + [pltpu.SemaphoreType.DMA((nc - 1,))] * 2 + [pltpu.SemaphoreType.DMA])


def _in_proj_bwd(dparts, n_roped, rope, dh1, x, g1, w, part):
    tm = TM
    nt = S // tm
    widths = [d.shape[1] for d in dparts]
    assert sum(widths) == PW
    npar = len(dparts)

    def body(*refs):
        d_refs = refs[:npar]
        tabs = [t[...] for t in refs[npar:npar + 3]]
        dh_ref, x_ref, g_ref, w_ref, in_ref, dx_ref, dgsum_ref, out_ref = refs[npar + 3:npar + 11]
        rs_start, rs_middle, rs_finish = _scatter2_phases(in_ref, out_ref, *refs[npar + 11:npar + 20])
        dg_ref = refs[npar + 20]
        ar_start, ar_finish = _small_phases([dg_ref], dgsum_ref, *refs[npar + 21:])
        i = pl.program_id(0)
        pl.when(i == 0)(rs_start)
        pl.when(i == 1)(rs_middle)

        @pl.when(i == 0)
        def _():
            dg_ref[...] = jnp.zeros_like(dg_ref)

        du = jnp.zeros((tm, D), F32)
        off = 0
        for j, (d_ref, wd) in enumerate(zip(d_refs, widths)):
            nc = next(c for c in (768, 512) if wd % c == 0)
            for s in range(wd // nc):
                d = d_ref[:, s * nc:(s + 1) * nc]
                du = du + _dot_nt(_unrope(d, *tabs) if j < n_roped else d, w_ref[:, off + s * nc:off + (s + 1) * nc])
            off += wd
        n, rs = _rms(x_ref[...])
        dg_ref[...] = dg_ref[...] + jnp.sum(du * n, axis=0, keepdims=True)
        dx_ref[...] = dh_ref[...] + _rms_bwd(du, n, rs, g_ref[...])

        @pl.when(i == nt - 1)
        def _():
            ar_start()
            rs_finish()
            ar_finish()

    row = lambda wd: pl.BlockSpec((tm, wd), lambda i: (i, 0))
    shard = part.shape[1:]
    return pl.pallas_call(
        body, name="in_proj_bwd", grid=(nt,),
        in_specs=[row(wd) for wd in widths] + [row(128)] * 3 + [row(D), row(D), _cspec((1, D)), _cspec((D, PW)), ANY],
        out_specs=[row(D), VM, ANY],
        out_shape=[jax.ShapeDtypeStruct((S, D), F32), jax.ShapeDtypeStruct((8, 1024), F32),
                   jax.ShapeDtypeStruct((len(CHIP_FLIPS), *shard), part.dtype)],
        scratch_shapes=_scatter2_scratch(shard, part.dtype)
        + [pltpu.VMEM((1, D), F32), pltpu.VMEM((8, 1024), F32), pltpu.VMEM((NDEV, 8, 1024), F32),
           pltpu.SemaphoreType.DMA((7,)), pltpu.SemaphoreType.DMA((7,))],
        compiler_params=_params(1),
    )(*dparts, *rope, dh1, x, g1, w, part)


SMALL_ROWS = 96


def _small_phases(ins, out_ref, pack, rbuf, send_sems, recv_sems):
    x, y, c = _place()
    me = _dev_index(x, y, c)

    def copies():
        out = []
        for k, (dx, dy, dc) in enumerate(FLIPS):
            peer = ((x + dx) % 2, (y + dy) % 2, (c + dc) % 2)
            out.append(pltpu.make_async_remote_copy(
                src_ref=pack, dst_ref=rbuf.at[me], send_sem=send_sems.at[k], recv_sem=recv_sems.at[k],
                device_id=peer, device_id_type=MESH))
        return out

    def start():
        pack[...] = jnp.zeros_like(pack)
        for i, ref in enumerate(ins):
            pack[8 * i:8 * i + 1, 0:ref.shape[1]] = ref[...]
        rbuf[me] = pack[...]
        for cp in copies():
            cp.start()

    def finish():
        for cp in copies():
            cp.wait()
        tot = rbuf[0]
        for j in range(1, NDEV):
            tot = tot + rbuf[j]
        out_ref[...] = tot

    return start, finish


def _wgrad(name, A, Bs, a_fn, b_fn, out_shape, split=None, ts=512, small=(), rope=(), n_roped=0):
    K = A.shape[1]
    widths = [b.shape[1] for b in Bs]
    N = sum(widths)
    nb, ns, nrt = len(Bs) + len(rope), len(small), S // ts
    kc = min(K, 1024)

    def body(*refs):
        a_ref, b_refs = refs[0], refs[1:1 + len(Bs)]
        tabs = [t[...] for t in refs[1 + len(Bs):1 + nb]]
        o_ref = refs[1 + nb + ns]
        acc = refs[2 + nb + ns + bool(ns)]
        r = pl.program_id(0)
        if ns:
            sm_start, sm_finish = _small_phases(refs[1 + nb:1 + nb + ns], refs[2 + nb + ns], *refs[4 + nb + ns:])
            pl.when(r == 0)(sm_start)

        @pl.when(r == 0)
        def _():
            acc[...] = jnp.zeros_like(acc)

        bs, off = [], 0
        for i, (b_ref, w) in enumerate(zip(b_refs, widths)):
            nc = next(c for c in (1024, 768, 512) if w % c == 0)
            fn = (lambda t: _unrope(t, *tabs)) if i < n_roped else b_fn
            bs += [(off + c * nc, nc, fn(b_ref[:, c * nc:(c + 1) * nc])) for c in range(w // nc)]
            off += w
        for kk in range(K // kc):
            rows = slice(kk * kc, (kk + 1) * kc)
            at = a_fn(a_ref[:, rows]).T
            for lo, nc, b in bs:
                acc[rows, lo:lo + nc] = acc[rows, lo:lo + nc] + _dot(at, b)

        @pl.when(r == nrt - 1)
        def _():
            if split is None:
                o_ref[...] = _bf(acc[...])
            else:
                for j in range(NDEV):
                    o_ref[j] = _bf(acc[:, split * j:split * (j + 1)])

        if ns:
            pl.when(r == nrt - 1)(sm_finish)

    in_specs = ([pl.BlockSpec((ts, K), lambda r: (r, 0))] + [pl.BlockSpec((ts, w), lambda r: (r, 0)) for w in widths]
                + [pl.BlockSpec((ts, 128), lambda r: (r, 0))] * len(rope))
    out_spec = pl.BlockSpec(out_shape, lambda r: (0,) * len(out_shape))
    scratch = [pltpu.VMEM((K, N), F32)]
    if not ns:
        return pl.pallas_call(
            body, name=name, grid=(nrt,), in_specs=in_specs, out_specs=out_spec,
            out_shape=jax.ShapeDtypeStruct(out_shape, BF16), scratch_shapes=scratch, compiler_params=_params(1),
        )(A, *Bs, *rope)
    return pl.pallas_call(
        body, name=name, grid=(nrt,), in_specs=in_specs + [VM] * ns, out_specs=[out_spec, VM],
        out_shape=[jax.ShapeDtypeStruct(out_shape, BF16), jax.ShapeDtypeStruct((SMALL_ROWS, 1024), F32)],
        scratch_shapes=scratch + [pltpu.VMEM((SMALL_ROWS, 1024), F32), pltpu.VMEM((NDEV, SMALL_ROWS, 1024), F32),
                                  pltpu.SemaphoreType.DMA((7,)), pltpu.SemaphoreType.DMA((7,))],
        compiler_params=_params(1),
    )(A, *Bs, *rope, *small)


def _relu2_bf(a):
    r = jnp.maximum(a.astype(F32), 0.0)
    return _bf(r * r)


def _ident(a):
    return a


def _step(x, p, target, g1, conv_b, gate_b, gn, g_mlp, g_ple, g_fin, sh):
    (g_in, g_conv), (rc, ra, rb) = _gather_weights([sh["w_in"], sh["conv_w"]], [BF16, F32])
    conv_w = g_conv.transpose(1, 0, 2).reshape(4, 1024)
    w_in_p = _join_w_in(g_in)
    (qkv, mqk, mv, mo, gates, u1), (w_out8, w_pg8, w_ple8) = _in_proj(
        x, g1, w_in_p, rc, ra, rb, [sh["w_out"], sh["w_ple_gate"], sh["w_ple"]], [BF16] * 3)
    attn, lse, (w_up8, w_down_a) = _attn_fwd(qkv, [sh["w_up"], sh["w_down"][0:HALF]], [BF16] * 2)
    ml, cs, ns, ms, (w_down_b,) = _mlstm_fwd(mqk, mv, mo, gates, conv_w, conv_b, gate_b, gn,
                                             [sh["w_down"][HALF:2 * HALF]], [BF16])
    w_out, w_pg = w_out8.reshape(D, D), w_pg8.reshape(D, D)
    h1, u2 = _out_proj(x, attn, ml, w_out, g_mlp)
    a, h2 = _mlp_fwd(h1, u2, w_up8, w_down_a, w_down_b)
    dh2, dw_pg, dw_ple8, dg_ple, dg_fin, loss = _ple_loss(h2, p, target, w_pg, w_ple8, g_ple, g_fin)
    da, dh1, dg_mlp, (r_pg, r_ple) = _mlp_bwd(dh2, a, h1, g_mlp, w_up8, w_down_a, w_down_b,
                                              [dw_pg.reshape(NDEV, D // NDEV, D), dw_ple8])
    dw_up8 = _wgrad("wgrad_up", u2, [da], _ident, _ident, (NDEV, D, DFF // NDEV), split=DFF // NDEV)
    dw_down = _wgrad("wgrad_down", a, [dh2], _relu2_bf, _bf, (DFF, D))
    d_attn, d_ml, dw_out = _out_proj_bwd(dh1, attn, ml, w_out)
    (dm, dconv_w, dconv_b, dgn, dgate_b), (r_down,) = _mlstm_bwd(
        mqk, mv, mo, gates, conv_w, conv_b, gate_b, gn, cs, ns, ms, d_ml, [dw_down.reshape(NDEV, DFF // NDEV, D)])
    dq, dk, dv, (r_up, r_out) = _attn_bwd(qkv, attn, lse, d_attn, [dw_up8, dw_out.reshape(NDEV, D // NDEV, D)])
    dparts = [dq, dk, dv, dm]
    small = [jnp.zeros((1, D), F32), dconv_b, dgate_b, dgn, dg_mlp, dg_ple, dg_fin, loss]
    dw_in8, total = _wgrad("wgrad_in", u1, dparts, _ident, _ident, (NDEV, D, IN_W // NDEV), split=IN_W // NDEV,
                           small=small + [dconv_w[j:j + 1] for j in range(4)], rope=(rc, ra, rb), n_roped=2)
    dx, dg1_sum, r_in = _in_proj_bwd(dparts, 2, (rc, ra, rb), dh1, x, g1, w_in_p, dw_in8)
    recv = dict(w_in=r_in, w_out=r_out, w_up=r_up, w_down=r_down, w_ple_gate=r_pg, w_ple=r_ple)
    return dx, recv, total, dg1_sum


def _gather_weights(shards, dtypes):
    nw = len(shards)

    def body(*refs):
        ins, parts = refs[:nw], refs[nw:nw + 4]
        outs, tables = refs[nw + 4:2 * nw + 4], refs[2 * nw + 4:2 * nw + 7]
        start, forward, finish = _gather_phases(ins, outs, refs[2 * nw + 7:3 * nw + 7], *refs[3 * nw + 7:])
        start()
        _rope_fill(*parts, *tables)
        forward()
        finish()

    res = pl.pallas_call(
        body, name="gather_weights",
        in_specs=[VM] * (nw + 4), out_specs=[ANY] * nw + [VM] * 3,
        out_shape=_gather_shapes(shards, dtypes) + [jax.ShapeDtypeStruct((S, 128), F32)] * 3,
        scratch_shapes=_gather_scratch(shards, dtypes),
        compiler_params=_params(),
    )(*shards, *_rope_parts())
    return res[:nw], res[nw:]


ADAM_STEPS = 4


def _adamw(items):
    n = len(items)

    def body(*refs):
        for i in range(n):
            g_ref, w_ref, m_ref, v_ref = refs[4 * i:4 * i + 4]
            go_ref, d_ref, mo_ref, vo_ref = refs[4 * n + 4 * i:4 * n + 4 * i + 4]
            g = g_ref[0].astype(F32)
            for j in range(1, g_ref.shape[0]):
                g = g + g_ref[j].astype(F32)
            go_ref[...] = g
            d_ref[...], mo_ref[...], vo_ref[...] = _adam_update(g, w_ref[...], m_ref[...], v_ref[...])

    in_specs, out_specs, out_shape, args = [], [], [], []
    for gparts, w, m, v in items:
        P, R, C = gparts.shape
        if R % (8 * ADAM_STEPS) == 0:
            tr = R // ADAM_STEPS
            row, gspec = pl.BlockSpec((tr, C), lambda i: (i, 0)), pl.BlockSpec((P, tr, C), lambda i: (0, i, 0))
        else:
            row, gspec = pl.BlockSpec((R, C), lambda i: (0, 0)), pl.BlockSpec((P, R, C), lambda i: (0, 0, 0))
        in_specs += [gspec, row, row, row]
        out_specs += [row] * 4
        out_shape += [jax.ShapeDtypeStruct((R, C), F32)] * 4
        args += [gparts, w, m, v]
    res = pl.pallas_call(
        body, name="adamw", grid=(ADAM_STEPS,), in_specs=in_specs, out_specs=out_specs, out_shape=out_shape,
        compiler_params=_params(1),
    )(*args)
    return [res[4 * i:4 * i + 4] for i in range(n)]


SMALL = ("norm_mix_g", "conv_b", "gate_b", "mlstm_norm_g", "norm_mlp_g", "norm_ple_g", "final_norm_g")


def _adam_update(g, w, m, v):
    c1 = 1.0 - ADAM_B1 ** ADAM_STEP
    c2 = 1.0 - ADAM_B2 ** ADAM_STEP
    m2 = ADAM_B1 * m + (1.0 - ADAM_B1) * g
    v2 = ADAM_B2 * v + (1.0 - ADAM_B2) * (g * g)
    return -ADAM_LR * ((m2 / c1) / (jnp.sqrt(v2 / c2) + ADAM_EPS) + ADAM_WD * w), m2, v2


def _adamw_small(total, first, ws, ms, vs):
    n = len(ws)

    def body(*refs):
        t_ref, f_ref = refs[:2]
        refs = refs[1:]
        outs = refs[1 + 3 * n:]
        for i in range(n):
            w_ref, m_ref, v_ref = refs[1 + i], refs[1 + n + i], refs[1 + 2 * n + i]
            g = (t_ref if i else f_ref)[8 * i:8 * i + 1, 0:w_ref.shape[1]]
            delta, m2, v2 = _adam_update(g, w_ref[...], m_ref[...], v_ref[...])
            for ref, val in zip(outs[4 * i:4 * i + 4], (g, delta, m2, v2)):
                ref[...] = val

    res = pl.pallas_call(
        body, name="adamw_small",
        out_shape=[jax.ShapeDtypeStruct(w.shape, F32) for w in ws for _ in range(4)],
        compiler_params=_params(),
    )(total, first, *ws, *ms, *vs)
    return [res[4 * i:4 * i + 4] for i in range(n)]


def kernel(x, p, norm_mix_g, w_in, conv_w, conv_b, gate_b, mlstm_norm_g, w_out, norm_mlp_g, w_up, w_down, norm_ple_g, w_ple_gate, w_ple, final_norm_g, loss_target, m_norm_mix_g, m_w_in, m_conv_w, m_conv_b, m_gate_b, m_mlstm_norm_g, m_w_out, m_norm_mlp_g, m_w_up, m_w_down, m_norm_ple_g, m_w_ple_gate, m_w_ple, m_final_norm_g, v_norm_mix_g, v_w_in, v_conv_w, v_conv_b, v_gate_b, v_mlstm_norm_g, v_w_out, v_norm_mlp_g, v_w_up, v_w_down, v_norm_ple_g, v_w_ple_gate, v_w_ple, v_final_norm_g):
    big_names = ("w_in", "conv_w", "w_out", "w_up", "w_down", "w_ple_gate", "w_ple")
    wts = dict(w_in=w_in, conv_w=conv_w, w_out=w_out, w_up=w_up, w_down=w_down, w_ple_gate=w_ple_gate, w_ple=w_ple)
    mom = dict(w_in=m_w_in, conv_w=m_conv_w, w_out=m_w_out, w_up=m_w_up, w_down=m_w_down, w_ple_gate=m_w_ple_gate,
               w_ple=m_w_ple)
    var = dict(w_in=v_w_in, conv_w=v_conv_w, w_out=v_w_out, w_up=v_w_up, w_down=v_w_down, w_ple_gate=v_w_ple_gate,
               w_ple=v_w_ple)
    sq = lambda a: a.reshape(a.shape[1:])
    fin = final_norm_g.reshape(1, D)
    dx, recv, total, first = _step(
        x[0], p[0, 0], loss_target[0], norm_mix_g, conv_b, jnp.pad(gate_b, ((0, 0), (0, 120))), mlstm_norm_g,
        norm_mlp_g, norm_ple_g, fin, {n: sq(wts[n]) for n in big_names})

    nrow = 8 * len(SMALL)
    me = _dev_index(*_place())
    conv_rows = total[nrow + 8:nrow + 40:8]
    recv["conv_w"] = lax.dynamic_slice_in_dim(conv_rows, me * 128, 128, axis=1).reshape(1, 4, 128)
    out = {}
    for n, res in zip(big_names, _adamw([(recv[n], sq(wts[n]), sq(mom[n]), sq(var[n])) for n in big_names])):
        out[n] = [t.reshape(wts[n].shape) for t in res]
    sw = dict(norm_mix_g=norm_mix_g, conv_b=conv_b, gate_b=gate_b, mlstm_norm_g=mlstm_norm_g, norm_mlp_g=norm_mlp_g,
              norm_ple_g=norm_ple_g, final_norm_g=fin)
    sm = dict(norm_mix_g=m_norm_mix_g, conv_b=m_conv_b, gate_b=m_gate_b, mlstm_norm_g=m_mlstm_norm_g,
              norm_mlp_g=m_norm_mlp_g, norm_ple_g=m_norm_ple_g, final_norm_g=m_final_norm_g.reshape(1, D))
    sv = dict(norm_mix_g=v_norm_mix_g, conv_b=v_conv_b, gate_b=v_gate_b, mlstm_norm_g=v_mlstm_norm_g,
              norm_mlp_g=v_norm_mlp_g, norm_ple_g=v_norm_ple_g, final_norm_g=v_final_norm_g.reshape(1, D))
    res = _adamw_small(total, first, [sw[n] for n in SMALL], [sm[n] for n in SMALL], [sv[n] for n in SMALL])
    for n, r in zip(SMALL, res):
        out[n] = [t.reshape(final_norm_g.shape) for t in r] if n == "final_norm_g" else list(r)
    order = ("norm_mix_g", "w_in", "conv_w", "conv_b", "gate_b", "mlstm_norm_g", "w_out", "norm_mlp_g", "w_up", "w_down",
             "norm_ple_g", "w_ple_gate", "w_ple", "final_norm_g")
    loss_all = total[nrow, 0]
    return (loss_all, dx[None], *[out[n][0] for n in order], *[out[n][1] for n in order],
            *[out[n][2] for n in order], *[out[n][3] for n in order])
```

```python
import math

import jax
import jax.numpy as jnp
from jax import lax
from jax.experimental import pallas as pl
from jax.experimental.pallas import tpu as pltpu

F32, BF16 = jnp.float32, jnp.bfloat16
S = 4096
D = 1024
AW = 512
MW = 512
DFF = 4096
PLE = 256
IN_W = 3592
PW = 3840
NDEV = 8
EPS = 1e-6
NEG = -1e30
LC = 128
TB = 256
ROPE_THETA = 500000.0
VMEM_LIMIT = 56 * 1024 * 1024
HI = lax.Precision.HIGHEST

ADAM_LR, ADAM_B1, ADAM_B2, ADAM_EPS, ADAM_WD, ADAM_STEP = 0.001, 0.9, 0.999, 1e-08, 0.01, 10


def _params(n_grid=0, **kw):
    sem = dict(dimension_semantics=("arbitrary",) * n_grid) if n_grid else {}
    return pltpu.CompilerParams(vmem_limit_bytes=VMEM_LIMIT, **sem, **kw)


def _cspec(shape):
    nd = len(shape)
    return pl.BlockSpec(shape, lambda *_: (0,) * nd, pipeline_mode=pl.Buffered(1))


def _dot(a, b):
    return jnp.dot(a, b, preferred_element_type=F32)


def _dot_nt(a, b):
    return lax.dot_general(a, b, (((1,), (1,)), ((), ())), preferred_element_type=F32)


def _dot_tn(a, b):
    return lax.dot_general(a, b, (((0,), (0,)), ((), ())), preferred_element_type=F32)


def _bf(x):
    return x.astype(BF16)


def _rms(x):
    rs = lax.rsqrt(jnp.mean(x * x, axis=-1, keepdims=True) + EPS)
    return x * rs, rs


def _rms_bwd(du, n, rs, g):
    dn = du * g
    return rs * (dn - n * jnp.mean(dn * n, axis=-1, keepdims=True))


def _sigmoid(x):
    return 1.0 / (1.0 + jnp.exp(-x))


ROPE_BLK = 512


def _rope_parts():
    def cs(n, step):
        j = lax.broadcasted_iota(jnp.int32, (n, 128), 1) % 64
        pos = (lax.broadcasted_iota(jnp.int32, (n, 128), 0) * step).astype(F32)
        ang = pos * jnp.power(ROPE_THETA, -(j % 8).astype(F32) / 8.0)
        return jnp.cos(ang), jnp.sin(ang)

    return (*cs(ROPE_BLK, 1), *cs(S // ROPE_BLK, ROPE_BLK))


def _rope_fill(co_ref, so_ref, cb_ref, sb_ref, rc_ref, ra_ref, rb_ref):
    j = lax.broadcasted_iota(jnp.int32, (ROPE_BLK, 128), 1) % 64
    co, so = co_ref[...], so_ref[...]
    for t in range(S // ROPE_BLK):
        cb, sb = cb_ref[t:t + 1, :], sb_ref[t:t + 1, :]
        cos, sin = cb * co - sb * so, sb * co + cb * so
        rows = slice(t * ROPE_BLK, (t + 1) * ROPE_BLK)
        rc_ref[rows, :] = jnp.where(j < 16, cos, 1.0)
        ra_ref[rows, :] = jnp.where(j < 8, -sin, 0.0)
        rb_ref[rows, :] = jnp.where((j >= 8) & (j < 16), sin, 0.0)


def _rope(blk, c, a, b):
    return blk * c + pltpu.roll(blk, 120, 1) * a + pltpu.roll(blk, 8, 1) * b


def _rope_bwd(d, c, a, b):
    return d * c + pltpu.roll(d * a, 8, 1) + pltpu.roll(d * b, 120, 1)


def _unrope(t, c, a, b):
    return jnp.concatenate([_bf(_rope_bwd(t[:, j * 128:(j + 1) * 128].astype(F32), c, a, b))
                            for j in range(t.shape[1] // 128)], axis=1)


MESH = pl.DeviceIdType.MESH
ANY = pl.BlockSpec(memory_space=pl.ANY)
VM = pl.BlockSpec(memory_space=pltpu.VMEM)
FLIPS = [(dx, dy, dc) for dx in (0, 1) for dy in (0, 1) for dc in (0, 1)][1:]


def _place():
    return lax.axis_index("x"), lax.axis_index("y"), lax.axis_index("c")


def _dev_index(px, py, pc):
    return 4 * px + 2 * py + pc


def _gather_phases(ins, outs, bufs, send_sems=None, recv_sems=None, local_sems=None):
    nw = len(ins)
    if nw == 0:
        return (lambda: None,) * 3
    x, y, c = _place()
    me, sib = (x, y, c), (x, y, 1 - c)
    chips = [(1 - x, y), (x, 1 - y), (1 - x, 1 - y)]

    def copy(w, k, block, to, from_buf=False):
        dst = outs[w].at[_dev_index(*block)]
        return pltpu.make_async_remote_copy(
            src_ref=bufs[w] if from_buf else dst, dst_ref=dst, send_sem=send_sems.at[w, k],
            recv_sem=recv_sems.at[w, k], device_id=to, device_id_type=MESH)

    def mine(w):
        return pltpu.make_async_copy(bufs[w], outs[w].at[_dev_index(*me)], local_sems.at[w])

    def first(w):
        return [copy(w, 0, me, sib, True)] + [copy(w, 1 + j, me, (*chip, c), True) for j, chip in enumerate(chips)]

    def passed(w):
        return [copy(w, 4 + j, (*chip, c), sib) for j, chip in enumerate(chips)]

    def start():
        for w in range(nw):
            bufs[w][...] = ins[w][...].astype(bufs[w].dtype)
        for w in range(nw):
            mine(w).start()
            for cp in first(w):
                cp.start()

    def forward():
        for j, chip in enumerate(chips):
            for w in range(nw):
                copy(w, 1 + j, (*chip, c), me).wait_recv()
                passed(w)[j].start()

    def finish():
        for w in range(nw):
            copy(w, 0, sib, me).wait_recv()
        for j, chip in enumerate(chips):
            for w in range(nw):
                copy(w, 4 + j, (*chip, 1 - c), me).wait_recv()
        for w in range(nw):
            for cp in first(w) + passed(w):
                cp.wait_send()
            mine(w).wait()

    return start, forward, finish


def _gather_scratch(shards, dtypes):
    nw = len(shards)
    if nw == 0:
        return []
    return ([pltpu.VMEM(s.shape, dt) for s, dt in zip(shards, dtypes)]
            + [pltpu.SemaphoreType.DMA((nw, 7)), pltpu.SemaphoreType.DMA((nw, 7)), pltpu.SemaphoreType.DMA((nw,))])


def _gather_shapes(shards, dtypes):
    return [jax.ShapeDtypeStruct((NDEV, *s.shape), dt) for s, dt in zip(shards, dtypes)]


def _scatter_phases(ins, outs, send_sems=None, recv_sems=None, local_sems=None):
    nw = len(ins)
    if nw == 0:
        return (lambda: None,) * 2
    x, y, c = _place()
    me = _dev_index(x, y, c)

    def copies():
        out = []
        for w in range(nw):
            out.append(pltpu.make_async_copy(ins[w].at[me], outs[w].at[me], local_sems.at[w]))
            for k, (dx, dy, dc) in enumerate(FLIPS):
                peer = ((x + dx) % 2, (y + dy) % 2, (c + dc) % 2)
                out.append(pltpu.make_async_remote_copy(
                    src_ref=ins[w].at[_dev_index(*peer)], dst_ref=outs[w].at[me], send_sem=send_sems.at[w, k],
                    recv_sem=recv_sems.at[w, k], device_id=peer, device_id_type=MESH))
        return out

    def start():
        for cp in copies():
            cp.start()

    def finish():
        for cp in copies():
            cp.wait()

    return start, finish


def _scatter_scratch(nw):
    if nw == 0:
        return []
    return [pltpu.SemaphoreType.DMA((nw, 7)), pltpu.SemaphoreType.DMA((nw, 7)), pltpu.SemaphoreType.DMA((nw,))]


TM = 512


def _join_w_in(wg):
    sw = IN_W // NDEV

    def body(wg_ref, w_ref):
        for j in range(NDEV):
            w_ref[:, sw * j:sw * (j + 1)] = wg_ref[j]
        w_ref[:, IN_W:PW] = jnp.zeros((D, PW - IN_W), BF16)

    return pl.pallas_call(body, name="join_w_in", out_shape=jax.ShapeDtypeStruct((D, PW), BF16),
                          compiler_params=_params())(wg)


def _in_proj(x, g1, w, rc, ra, rb, shards, dtypes):
    tm = TM
    nw = len(shards)
    nt = S // tm

    def body(*refs):
        x_ref, g_ref, w_ref, rc_ref, ra_ref, rb_ref = refs[:6]
        ins = refs[6:6 + nw]
        qkv_ref, mqk_ref, mv_ref, mo_ref, gt_ref, u_ref = refs[6 + nw:12 + nw]
        outs = refs[12 + nw:12 + 2 * nw]
        bufs = refs[12 + 2 * nw:12 + 3 * nw]
        ag_start, ag_forward, ag_finish = _gather_phases(ins, outs, bufs, *refs[12 + 3 * nw:])
        i = pl.program_id(0)
        pl.when(i == 0)(ag_start)
        pl.when(i == nt - 2)(ag_forward)
        n, _ = _rms(x_ref[...])
        u = _bf(n * g_ref[...])
        u_ref[...] = u
        c, a, b = rc_ref[...], ra_ref[...], rb_ref[...]
        for half in range(2):
            blk = _dot(u, w_ref[:, half * 512:(half + 1) * 512])
            for t in range(4):
                lo = half * 512 + t * 128
                qkv_ref[:, lo:lo + 128] = _rope(blk[:, t * 128:(t + 1) * 128], c, a, b)
        qkv_ref[:, 1024:1536] = _dot(u, w_ref[:, 1024:1536])
        mqk_ref[:, 0:512] = _dot(u, w_ref[:, 1536:2048])
        mqk_ref[:, 512:1024] = _dot(u, w_ref[:, 2048:2560])
        mv_ref[...] = _dot(u, w_ref[:, 2560:3072])
        mo_ref[...] = _dot(u, w_ref[:, 3072:3584])
        gt_ref[...] = _dot(u, w_ref[:, 3584:3712])
        pl.when(i == nt - 1)(ag_finish)

    row = lambda wd: pl.BlockSpec((tm, wd), lambda i: (i, 0))
    res = pl.pallas_call(
        body, name="in_proj", grid=(nt,),
        in_specs=[row(D), _cspec((1, D)), _cspec((D, PW)), row(128), row(128), row(128)] + [VM] * nw,
        out_specs=[row(1536), row(1024), row(512), row(512), row(128), row(D)] + [ANY] * nw,
        out_shape=[jax.ShapeDtypeStruct((S, 1536), F32), jax.ShapeDtypeStruct((S, 1024), F32),
                   jax.ShapeDtypeStruct((S, 512), F32), jax.ShapeDtypeStruct((S, 512), F32),
                   jax.ShapeDtypeStruct((S, 128), F32), jax.ShapeDtypeStruct((S, D), BF16)]
        + _gather_shapes(shards, dtypes),
        scratch_shapes=_gather_scratch(shards, dtypes),
        compiler_params=_params(1),
    )(x, g1, w, rc, ra, rb, *shards)
    return res[:6], res[6:]


DILATIONS = (16, 4, 1)


def _attn_valid(n):
    kd = lax.broadcasted_iota(jnp.int32, (128, 256), 1) - lax.broadcasted_iota(jnp.int32, (128, 256), 0)
    off = jnp.where(n == 0, 0, 128)
    return (kd <= off) & (kd >= off - 128)


def _attn_rows(d, r, n):
    if d == 1:
        q0 = pl.multiple_of(n * 128, 128)
        k0 = pl.multiple_of(jnp.maximum(n - 1, 0) * 128, 128)
        return pl.ds(q0, 128), pl.ds(k0, 256), _attn_valid(n)
    q0 = r + n * 128 * d
    k0 = r + jnp.maximum(n - 1, 0) * 128 * d
    return pl.ds(q0, 128, stride=d), pl.ds(k0, 256, stride=d), _attn_valid(n)


ATTN_GROUP = 4
ATTN_ITERS = S // 128 // ATTN_GROUP


def _attn_group(d, i):
    nb = S // (128 * d)
    if nb == 2:
        qi = lax.broadcasted_iota(jnp.int32, (256, 256), 0) - lax.broadcasted_iota(jnp.int32, (256, 256), 1)
        whole = [pl.ds((ATTN_GROUP // 2) * i + u, 256, stride=d) for u in range(ATTN_GROUP // 2)]
        return [(rows, rows, (qi >= 0) & (qi <= 128)) for rows in whole]
    if d == 1:
        return [_attn_rows(1, 0, i + ATTN_ITERS * u) for u in range(ATTN_GROUP)]
    return [_attn_rows(d, (i // nb) * ATTN_GROUP + u, i % nb) for u in range(ATTN_GROUP)]


def _head0(shape):
    return lax.broadcasted_iota(jnp.int32, shape, 1) < 64


def _stack_heads(t):
    h0 = _head0(t.shape)
    tb = _bf(t)
    zero = jnp.zeros_like(tb)
    return jnp.concatenate([jnp.where(h0, tb, zero), jnp.where(h0, zero, tb)], axis=0)


def _attn_fwd(qkv, shards, dtypes):
    nw = len(shards)

    def body(*refs):
        q_ref, k_ref, v_ref = refs[:3]
        ins = refs[3:3 + nw]
        o_ref, lse0_ref, lse1_ref = refs[3 + nw:6 + nw]
        outs = refs[6 + nw:6 + 2 * nw]
        m0, m1, l0, l1, acc = refs[6 + 2 * nw:11 + 2 * nw]
        bufs = refs[11 + 2 * nw:11 + 3 * nw]
        ag_start, ag_forward, ag_finish = _gather_phases(ins, outs, bufs, *refs[11 + 3 * nw:])
        hp = pl.program_id(0)
        pl.when(hp == 0)(ag_start)
        pl.when(hp == 3)(ag_forward)
        stats = (m0, m1, l0, l1, acc)

        def update(blocks, first):
            loaded = [([q_ref[rq, :], k_ref[rk, :], v_ref[rk, :]], None if first else [ref[rq, :] for ref in stats])
                      for rq, rk, _ in blocks]
            both = lambda a, b: jnp.concatenate([a, b], axis=0)
            ss = [jnp.where(both(valid, valid), _dot_nt(_stack_heads(q * 0.125), _bf(k)), NEG)
                  for ((q, k, _), _), (_, _, valid) in zip(loaded, blocks)]
            mcs = [jnp.max(s, axis=-1, keepdims=True) for s in ss]
            if first:
                m2s = [jnp.broadcast_to(mc, (mc.shape[0], 128)) for mc in mcs]
            else:
                m2s = [jnp.maximum(both(prev[0], prev[1]), mc) for mc, (_, prev) in zip(mcs, loaded)]
            ps = [jnp.exp(s - jnp.tile(m2, (1, 2))) for s, m2 in zip(ss, m2s)]
            l2s = [jnp.sum(p, axis=-1, keepdims=True) for p in ps]
            acc2s = [_dot(_bf(p), _bf(v)) for p, ((_, _, v), _) in zip(ps, loaded)]
            results = []
            for m2, l2, acc2, (_, prev) in zip(m2s, l2s, acc2s, loaded):
                nq = m2.shape[0] // 2
                if first:
                    l2 = jnp.broadcast_to(l2, (2 * nq, 128))
                else:
                    alpha = jnp.exp(both(prev[0], prev[1]) - m2)
                    l2, acc2 = alpha * both(prev[2], prev[3]) + l2, alpha * both(prev[4], prev[4]) + acc2
                results.append((m2[0:nq], m2[nq:2 * nq], l2[0:nq], l2[nq:2 * nq],
                                jnp.where(_head0((nq, 128)), acc2[0:nq], acc2[nq:2 * nq])))
            for (rq, _, _), res in zip(blocks, results):
                for ref, val in zip(stats, res):
                    ref[rq, :] = val

        for d in DILATIONS:
            def step(i, carry, d=d):
                update(_attn_group(d, i), d == DILATIONS[0])
                return carry

            lax.fori_loop(0, ATTN_ITERS, step, 0)

        def fin(t, carry):
            rows = pl.ds(pl.multiple_of(t * 256, 256), 256)
            h0 = lax.broadcasted_iota(jnp.int32, (256, 128), 1) < 64
            la, lb = l0[rows, :], l1[rows, :]
            o_ref[rows, :] = acc[rows, :] / jnp.where(h0, la, lb)
            lse0_ref[rows, :] = m0[rows, :] + jnp.log(la)
            lse1_ref[rows, :] = m1[rows, :] + jnp.log(lb)
            return carry

        lax.fori_loop(0, S // 256, fin, 0)
        pl.when(hp == 3)(ag_finish)

    col = lambda off: pl.BlockSpec((S, 128), lambda h, off=off: (0, off + h))
    res = pl.pallas_call(
        body, name="attn_fwd", grid=(4,),
        in_specs=[col(0), col(4), col(8)] + [VM] * nw,
        out_specs=[col(0), col(0), col(0)] + [ANY] * nw,
        out_shape=[jax.ShapeDtypeStruct((S, AW), F32)] * 3 + _gather_shapes(shards, dtypes),
        scratch_shapes=[pltpu.VMEM((S, 128), F32)] * 5 + _gather_scratch(shards, dtypes),
        compiler_params=_params(1),
    )(qkv, qkv, qkv, *shards)
    return res[0], (res[1], res[2]), res[3:]


def _attn_bwd(qkv, o, lse, do, parts):
    nw = len(parts)

    def body(*refs):
        q_ref, k_ref, v_ref, o_ref, L0, L1, do_ref = refs[:7]
        ins = refs[7:7 + nw]
        dq_out, dk_out, dv_out = refs[7 + nw:10 + nw]
        outs = refs[10 + nw:10 + 2 * nw]
        D0, D1, dq_ref, dk_ref, dv_ref = refs[10 + 2 * nw:15 + 2 * nw]
        rs_start, rs_finish = _scatter_phases(ins, outs, *refs[15 + 2 * nw:])
        hp = pl.program_id(0)
        pl.when(hp == 0)(rs_start)

        def pre(t, carry):
            rows = pl.ds(pl.multiple_of(t * 256, 256), 256)
            h0 = lax.broadcasted_iota(jnp.int32, (256, 128), 1) < 64
            dd = do_ref[rows, :] * o_ref[rows, :]
            shp = (256, 128)
            D0[rows, :] = jnp.broadcast_to(jnp.sum(jnp.where(h0, dd, 0.0), axis=-1, keepdims=True), shp)
            D1[rows, :] = jnp.broadcast_to(jnp.sum(jnp.where(h0, 0.0, dd), axis=-1, keepdims=True), shp)
            return carry

        lax.fori_loop(0, S // 256, pre, 0)

        def update(blocks, first):
            loaded = [([q_ref[rq, :], k_ref[rk, :], v_ref[rk, :], do_ref[rq, :]],
                       [L0[rq, :], L1[rq, :], D0[rq, :], D1[rq, :]],
                       [0.0] * 3 if first else [dq_ref[rq, :], dk_ref[rk, :], dv_ref[rk, :]]) for rq, rk, _ in blocks]
            cat = lambda a, b: jnp.tile(jnp.concatenate([a, b], axis=0), (1, 2))
            ops = [(_stack_heads(q * 0.125), _stack_heads(q), _stack_heads(dout), _bf(k), _bf(v))
                   for (q, k, v, dout), _, _ in loaded]
            ss = [jnp.where(jnp.concatenate([valid, valid], axis=0), _dot_nt(qs, kb), NEG)
                  for (qs, _, _, kb, _), (_, _, valid) in zip(ops, blocks)]
            dps = [_dot_nt(do2, vb) for _, _, do2, _, vb in ops]
            ps = [jnp.exp(s - cat(st[0], st[1])) for s, (_, st, _) in zip(ss, loaded)]
            dss = [_bf(p * (dp - cat(st[2], st[3])) * 0.125) for p, dp, (_, st, _) in zip(ps, dps, loaded)]
            dq2s = [_dot(ds, kb) for ds, (_, _, _, kb, _) in zip(dss, ops)]
            dks = [_dot_tn(ds, q2) for ds, (_, q2, _, _, _) in zip(dss, ops)]
            dvs = [_dot_tn(_bf(p), do2) for p, (_, _, do2, _, _) in zip(ps, ops)]
            results = []
            for (_, _, (dq, dk, dv)), dq2, dkk, dvv in zip(loaded, dq2s, dks, dvs):
                nq = dq2.shape[0] // 2
                results.append((dq + jnp.where(_head0((nq, 128)), dq2[0:nq], dq2[nq:2 * nq]), dk + dkk, dv + dvv))
            for (rq, rk, _), (dq, dk, dv) in zip(blocks, results):
                dq_ref[rq, :] = dq
                dk_ref[rk, :] = dk
                dv_ref[rk, :] = dv

        assert S // (128 * DILATIONS[0]) == 2
        for d in DILATIONS:
            def step(i, carry, d=d):
                update(_attn_group(d, i), d == DILATIONS[0])
                return carry

            lax.fori_loop(0, ATTN_ITERS, step, 0)

        def fin(t, carry):
            rows = pl.ds(pl.multiple_of(t * 256, 256), 256)
            for src, dst in ((dq_ref, dq_out), (dk_ref, dk_out), (dv_ref, dv_out)):
                dst[rows, :] = _bf(src[rows, :])
            return carry

        lax.fori_loop(0, S // 256, fin, 0)
        pl.when(hp == 3)(rs_finish)

    col = lambda off: pl.BlockSpec((S, 128), lambda h, off=off: (0, off + h))
    res = pl.pallas_call(
        body, name="attn_bwd", grid=(4,),
        in_specs=[col(0), col(4), col(8), col(0), col(0), col(0), col(0)] + [ANY] * nw,
        out_specs=[col(0), col(0), col(0)] + [ANY] * nw,
        out_shape=[jax.ShapeDtypeStruct((S, AW), BF16)] * 3 + [jax.ShapeDtypeStruct(a.shape, a.dtype) for a in parts],
        scratch_shapes=[pltpu.VMEM((S, 128), F32)] * 5 + _scatter_scratch(nw),
        compiler_params=_params(1),
    )(qkv, qkv, qkv, o, lse[0], lse[1], do, *parts)
    return res[0], res[1], res[2], res[3:]


def _logsig(x):
    return jnp.minimum(x, 0.0) - jnp.log1p(jnp.exp(-jnp.abs(x)))


def _conv_taps(xp, n):
    return [xp[8:] if j == 3 else pltpu.roll(xp, 3 - j, 0)[8:] for j in range(4)]


def _conv_silu(xp, w_ref, b_ref, n):
    taps = _conv_taps(xp, n)
    c = b_ref[...] + sum(w_ref[j:j + 1, :] * taps[j] for j in range(4))
    sg = _sigmoid(c)
    return c, sg, taps


def _chunk_gates(G):
    assert LC == 128
    r = lax.broadcasted_iota(jnp.int32, (LC, LC), 0)
    c = lax.broadcasted_iota(jnp.int32, (LC, LC), 1)
    tril = (c <= r).astype(F32)
    triu = (c >= r).astype(F32)
    b_col = jnp.dot(tril, _logsig(G), preferred_element_type=F32, precision=HI)
    return b_col, b_col.T, G.T, tril, triu


def _colpick(X, lane):
    li = lax.broadcasted_iota(jnp.int32, X.shape, 1)
    return jnp.sum(jnp.where(li == lane, X, 0.0), axis=1, keepdims=True)


def _rowpick(XT, row):
    ri = lax.broadcasted_iota(jnp.int32, XT.shape, 0)
    return jnp.sum(jnp.where(ri == row, XT, 0.0), axis=0, keepdims=True)


def _each(f, *lists):
    return [f(*a) for a in zip(*lists)]


def _mlstm_heads(Q, K, V, G, b_col, b_row, g_row, C, N, M):
    hs = range(len(Q))
    bt = [_colpick(b_col, 4 + h) for h in hs]
    i_col = [_colpick(G, h) for h in hs]
    bs = [_rowpick(b_row, 4 + h) for h in hs]
    i_row = [_rowpick(g_row, h) for h in hs]
    r = lax.broadcasted_iota(jnp.int32, (LC, LC), 0)
    c = lax.broadcasted_iota(jnp.int32, (LC, LC), 1)
    lane = lax.broadcasted_iota(jnp.int32, (1, LC), 1)
    qb, kb, vb = [_bf(t) for t in Q], [_bf(t) for t in K], [_bf(t) for t in V]
    S_ = _each(_dot_nt, qb, kb)
    qC = _each(lambda q, ch: _dot(q, _bf(ch)), qb, C)
    log_d = _each(lambda a, b, i: jnp.where(c <= r, a - b + i, NEG), bt, bs, i_row)
    log_inter = _each(lambda a, m: a + m, bt, M)
    m_t = _each(lambda li, ld: jnp.maximum(li, jnp.max(ld, axis=1, keepdims=True)), log_inter, log_d)
    Dm = _each(lambda ld, m: jnp.exp(ld - m), log_d, m_t)
    g = _each(lambda li, m: jnp.exp(li - m), log_inter, m_t)
    Am = _each(lambda s, d: s * d, S_, Dm)
    AV = _each(lambda a, v: _dot(_bf(a), v), Am, vb)
    num = _each(lambda gg, qc, av: gg * qc + av, g, qC, AV)
    qn = _each(lambda q, n: jnp.sum(q * n, axis=1, keepdims=True), Q, N)
    den = _each(lambda gg, x, a: gg * x + jnp.sum(a, axis=1, keepdims=True), g, qn, Am)
    floor = [jnp.exp(-m) for m in m_t]
    inv_dd = _each(lambda d, f: 1.0 / jnp.maximum(jnp.abs(d), f), den, floor)
    hh = _each(lambda n, i: n * i, num, inv_dd)
    blast = [jnp.sum(jnp.where(lane == LC - 1, b, 0.0), axis=1, keepdims=True) for b in bs]
    log_s = _each(lambda bl, a, i: bl - a + i, blast, bt, i_col)
    m_new = _each(lambda bl, m, ls: jnp.maximum(bl + m, jnp.max(ls, axis=0, keepdims=True)), blast, M, log_s)
    decay = _each(lambda bl, m, mn: jnp.exp(bl + m - mn), blast, M, m_new)
    ws = _each(lambda ls, mn: jnp.exp(ls - mn), log_s, m_new)
    kw = _each(lambda k, w: k * w, K, ws)
    KV = _each(lambda k, v: _dot_tn(_bf(k), v), kw, vb)
    C_new = _each(lambda d, ch, kv: d * ch + kv, decay, C, KV)
    n_new = _each(lambda d, n, k: d * n + jnp.sum(k, axis=0, keepdims=True), decay, N, kw)
    return dict(Dm=Dm, g=g, Am=Am, qC=qC, qn=qn, den=den, floor=floor, inv_dd=inv_dd, h=hh, decay=decay, ws=ws, kw=kw,
                C_new=C_new, n_new=n_new, m_new=m_new, qb=qb, kb=kb, vb=vb)


def _head_out(hh, mo_h, gn_h):
    r = lax.rsqrt(jnp.mean(hh * hh, axis=-1, keepdims=True) + EPS)
    hn = hh * r
    sg = _sigmoid(mo_h)
    return sg * (hn * gn_h), hn, r, sg


def _mlstm_fwd(mqk, mv, mo, gates, conv_w, conv_b, gate_b, gn, shards, dtypes):
    nblk = S // TB
    ncb = TB // LC
    nw = len(shards)

    def body(*refs):
        x_ref, v_ref, o_ref, g_ref, w_ref, b_ref, gb_ref, gn_ref = refs[:8]
        ins = refs[8:8 + nw]
        out_ref, cs_ref, ns_ref, ms_ref = refs[8 + nw:12 + nw]
        outs = refs[12 + nw:12 + 2 * nw]
        tail, Cst, nst, mst, qs, ks = refs[12 + 2 * nw:18 + 2 * nw]
        bufs = refs[18 + 2 * nw:18 + 3 * nw]
        ag_start, ag_forward, ag_finish = _gather_phases(ins, outs, bufs, *refs[18 + 3 * nw:])
        i = pl.program_id(0)
        pl.when(i == 0)(ag_start)
        pl.when(i == nblk // 2)(ag_forward)

        @pl.when(i == 0)
        def _():
            tail[...] = jnp.zeros_like(tail)
            Cst[...] = jnp.zeros_like(Cst)
            nst[...] = jnp.zeros_like(nst)
            mst[...] = jnp.zeros_like(mst)

        x = x_ref[...]
        xp = jnp.concatenate([tail[...], x], axis=0)
        tail[...] = x[TB - 8:TB, :]
        c, sg, _ = _conv_silu(xp, w_ref, b_ref, TB)
        y = c * sg
        qs[...] = y[:, 0:MW]
        ks[...] = y[:, MW:2 * MW] * (1.0 / math.sqrt(128.0))

        for cc in range(ncb):
            rows = slice(cc * LC, (cc + 1) * LC)
            G = g_ref[rows, :] + gb_ref[...]
            b_col, b_row, g_row, _, _ = _chunk_gates(G)
            cs_ref[cc] = Cst[...]
            ns_ref[cc] = nst[...]
            ms_ref[cc] = mst[...]
            lns = [slice(h * 128, (h + 1) * 128) for h in range(4)]
            f = _mlstm_heads([qs[rows, ln] for ln in lns], [ks[rows, ln] for ln in lns], [v_ref[rows, ln] for ln in lns],
                             G, b_col, b_row, g_row, [Cst[:, ln] for ln in lns], [nst[0:1, ln] for ln in lns],
                             [jnp.max(mst[0:1, ln], axis=1, keepdims=True) for ln in lns])
            outs = [_head_out(hh, o_ref[rows, ln], gn_ref[:, ln])[0] for hh, ln in zip(f["h"], lns)]
            for h, ln in enumerate(lns):
                out_ref[rows, ln] = outs[h]
                Cst[:, ln] = f["C_new"][h]
                nst[0:1, ln] = f["n_new"][h]
                mst[0:1, ln] = jnp.broadcast_to(f["m_new"][h], (1, 128))
        pl.when(i == nblk - 1)(ag_finish)

    row = lambda wd: pl.BlockSpec((TB, wd), lambda i: (i, 0))
    res = pl.pallas_call(
        body, name="mlstm_fwd", grid=(nblk,),
        in_specs=[row(1024), row(MW), row(MW), row(128), _cspec((4, 1024)), _cspec((1, 1024)), _cspec((1, 128)),
                  _cspec((1, MW))] + [VM] * nw,
        out_specs=[row(MW), pl.BlockSpec((ncb, 128, MW), lambda i: (i, 0, 0)),
                   pl.BlockSpec((ncb, 8, MW), lambda i: (i, 0, 0)), pl.BlockSpec((ncb, 8, MW), lambda i: (i, 0, 0))]
        + [ANY] * nw,
        out_shape=[jax.ShapeDtypeStruct((S, MW), F32), jax.ShapeDtypeStruct((S // LC, 128, MW), F32),
                   jax.ShapeDtypeStruct((S // LC, 8, MW), F32), jax.ShapeDtypeStruct((S // LC, 8, MW), F32)]
        + _gather_shapes(shards, dtypes),
        scratch_shapes=[pltpu.VMEM((8, 1024), F32), pltpu.VMEM((128, MW), F32), pltpu.VMEM((8, MW), F32),
                        pltpu.VMEM((8, MW), F32), pltpu.VMEM((TB, MW), F32), pltpu.VMEM((TB, MW), F32)]
        + _gather_scratch(shards, dtypes),
        compiler_params=_params(1),
    )(mqk, mv, mo, gates, conv_w, conv_b, gate_b, gn, *shards)
    return res[0], res[1], res[2], res[3], res[4:]


DM_V, DM_O, DM_G, DM_W = 1024, 1536, 2048, PW - 3 * AW


def _mlstm_bwd(mqk, mv, mo, gates, conv_w, conv_b, gate_b, gn, cs, ns, ms, dout, parts):
    nblk = S // TB
    ncb = TB // LC
    kscale = 1.0 / math.sqrt(128.0)
    nw = len(parts)

    def body(*refs):
        x_ref, xprev_ref, v_ref, o_ref, g_ref, w_ref, b_ref, gb_ref, gn_ref, cs_ref, ns_ref, ms_ref, do_ref = refs[:13]
        ins = refs[13:13 + nw]
        dm_ref, dw_ref, db_ref, dgn_ref, dgb_ref = refs[13 + nw:18 + nw]
        outs = refs[18 + nw:18 + 2 * nw]
        dCst, dnst, dyhead, qs, ks, dqk = refs[18 + 2 * nw:24 + 2 * nw]
        rs_start, rs_finish = _scatter_phases(ins, outs, *refs[24 + 2 * nw:])
        i = pl.program_id(0)
        blk = nblk - 1 - i
        pl.when(i == 0)(rs_start)

        @pl.when(i == 0)
        def _():
            dCst[...] = jnp.zeros_like(dCst)
            dnst[...] = jnp.zeros_like(dnst)
            dyhead[...] = jnp.zeros_like(dyhead)
            dw_ref[...] = jnp.zeros_like(dw_ref)
            db_ref[...] = jnp.zeros_like(db_ref)
            dgn_ref[...] = jnp.zeros_like(dgn_ref)
            dgb_ref[...] = jnp.zeros_like(dgb_ref)

        x = x_ref[...]
        xprev = jnp.where(blk == 0, 0.0, xprev_ref[...])
        xp = jnp.concatenate([xprev, x], axis=0)
        c, sg, taps = _conv_silu(xp, w_ref, b_ref, TB)
        y = c * sg
        qs[...] = y[:, 0:MW]
        ks[...] = y[:, MW:2 * MW] * kscale
        lane128 = lax.broadcasted_iota(jnp.int32, (LC, 128), 1)
        rowi = lax.broadcasted_iota(jnp.int32, (LC, 1), 0)

        for cc in reversed(range(ncb)):
            rows = slice(cc * LC, (cc + 1) * LC)
            G = g_ref[rows, :] + gb_ref[...]
            b_col, b_row, g_row, _, triu = _chunk_gates(G)
            lns = [slice(h * 128, (h + 1) * 128) for h in range(4)]
            C = [cs_ref[cc, :, ln] for ln in lns]
            N = [ns_ref[cc, 0:1, ln] for ln in lns]
            Q, Kk = [qs[rows, ln] for ln in lns], [ks[rows, ln] for ln in lns]
            dCn, dnn = [dCst[:, ln] for ln in lns], [dnst[0:1, ln] for ln in lns]
            gns, dos, mos = [gn_ref[:, ln] for ln in lns], [do_ref[rows, ln] for ln in lns], [o_ref[rows, ln] for ln in lns]
            f = _mlstm_heads(Q, Kk, [v_ref[rows, ln] for ln in lns], G, b_col, b_row, g_row, C, N,
                             [jnp.max(ms_ref[cc, 0:1, ln], axis=1, keepdims=True) for ln in lns])
            hh, inv_dd, den, g, Am, Dm = f["h"], f["inv_dd"], f["den"], f["g"], f["Am"], f["Dm"]
            qb, kb, vb, ws, decay = f["qb"], f["kb"], f["vb"], f["ws"], f["decay"]
            ho = _each(_head_out, hh, mos, gns)
            hn, r, sgo = [t[1] for t in ho], [t[2] for t in ho], [t[3] for t in ho]
            dmo = _each(lambda d, n, gn_h, s: _bf(d * (n * gn_h) * s * (1.0 - s)), dos, hn, gns, sgo)
            dhm = _each(lambda d, s: d * s, dos, sgo)
            dgn = _each(lambda d, n: jnp.sum(d * n, axis=0, keepdims=True), dhm, hn)
            dhn = _each(lambda d, gn_h: d * gn_h, dhm, gns)
            dh = _each(lambda rr, d, n: rr * (d - n * jnp.mean(d * n, axis=-1, keepdims=True)), r, dhn, hn)
            dnum = _each(lambda d, i: d * i, dh, inv_dd)
            ddd = _each(lambda d, x, i: -jnp.sum(d * x, axis=1, keepdims=True) * i, dh, hh, inv_dd)
            dden = _each(lambda dn_, fl, d: jnp.where(jnp.abs(dn_) >= fl, d * jnp.sign(dn_), 0.0), den, f["floor"], ddd)
            dnb = [_bf(t) for t in dnum]
            gd = _each(lambda gg, d: _bf(gg * d), g, dnum)
            gq = _each(lambda gg, d: gg * d, g, dden)
            dCb = [_bf(t) for t in dCn]
            dA = _each(lambda d, v, dd_: _dot_nt(d, v) + dd_, dnb, vb, dden)
            dv1 = _each(lambda a, d: _dot_tn(_bf(a), d), Am, dnb)
            dq1 = _each(lambda d, ch: _dot_nt(d, _bf(ch)), gd, C)
            dC1 = _each(_dot_tn, qb, gd)
            E = _each(lambda v, d, n: _dot_nt(v, d) + n, vb, dCb, dnn)
            dv2 = _each(lambda k, d: _dot(_bf(k), d), f["kw"], dCb)
            dS = _each(lambda a, d: _bf(a * d), dA, Dm)
            dq2 = _each(_dot, dS, kb)
            dk1 = _each(_dot_tn, dS, qb)
            dq = _each(lambda a, x, n, b: a + x * n + b, dq1, gq, N, dq2)
            dC = _each(lambda d, x, y: d * x + y, decay, dCn, dC1)
            dn = _each(lambda d, x, y, q: d * x + jnp.sum(y * q, axis=0, keepdims=True), decay, dnn, gq, Q)
            dg = _each(lambda d, qc, dd_, x: jnp.sum(d * qc, axis=1, keepdims=True) + dd_ * x, dnum, f["qC"], dden, f["qn"])
            Gm = _each(lambda a, b: a * b, dA, Am)
            gam = _each(lambda a, b: a * b, dg, g)
            dk = _each(lambda a, w, e: (a + w * e) * kscale, dk1, ws, E)
            om = _each(lambda e, k, w: jnp.sum(e * k, axis=1, keepdims=True) * w, E, Kk, ws)
            dv = _each(lambda a, b: _bf(a + b), dv1, dv2)
            ddecay = _each(lambda d, ch, dn_, n: jnp.sum(jnp.sum(d * ch, axis=1, keepdims=True), axis=0, keepdims=True)
                           + jnp.sum(dn_ * n, axis=1, keepdims=True), dCn, C, dnn, N)
            rows_g = [jnp.sum(t, axis=1, keepdims=True) for t in Gm]
            cols_g = [jnp.broadcast_to(jnp.sum(t, axis=0, keepdims=True), (LC, 128)).T for t in Gm]
            last = _each(lambda o, dd_, d: jnp.where(rowi == LC - 1, jnp.sum(o, axis=0, keepdims=True) + dd_ * d, 0.0),
                         om, ddecay, decay)
            db = _each(lambda a, b, o, l, cg: a + b - o + l - cg, rows_g, gam, om, last, cols_g)
            di = _each(lambda cg, o: cg + o, cols_g, om)
            dB = jnp.zeros((LC, 128), F32)
            dI = jnp.zeros((LC, 128), F32)
            for h, ln in enumerate(lns):
                dB = jnp.where(lane128 == 4 + h, db[h], dB)
                dI = jnp.where(lane128 == h, di[h], dI)
                dgn_ref[:, ln] = dgn_ref[:, ln] + dgn[h]
                dCst[:, ln] = dC[h]
                dnst[0:1, ln] = dn[h]
                dqk[rows, ln] = dq[h]
                dqk[rows, MW + h * 128:MW + (h + 1) * 128] = dk[h]
                dm_ref[rows, DM_O + h * 128:DM_O + (h + 1) * 128] = dmo[h]
                dm_ref[rows, DM_V + h * 128:DM_V + (h + 1) * 128] = dv[h]
            dlogf = jnp.dot(triu, dB, preferred_element_type=F32, precision=HI)
            dG = dI + dlogf * _sigmoid(-G)
            dG = jnp.where(lane128 < 8, dG, 0.0)
            dm_ref[rows, DM_G:DM_G + 128] = _bf(dG)
            dm_ref[rows, DM_G + 128:DM_W] = jnp.zeros((LC, DM_W - DM_G - 128), BF16)
            dgb_ref[...] = dgb_ref[...] + jnp.sum(dG, axis=0, keepdims=True)

        dy = dqk[...] * (sg * (1.0 + c * (1.0 - sg)))
        db_ref[...] = db_ref[...] + jnp.sum(dy, axis=0, keepdims=True)
        for j in range(4):
            dw_ref[j:j + 1, :] = dw_ref[j:j + 1, :] + jnp.sum(dy * taps[j], axis=0, keepdims=True)
        dyp = jnp.concatenate([dy, dyhead[...]], axis=0)
        dx = w_ref[3:4, :] * dy
        for j in range(3):
            dx = dx + w_ref[j:j + 1, :] * pltpu.roll(dyp, TB + 8 - (3 - j), 0)[0:TB]
        dm_ref[:, 0:DM_V] = _bf(dx)
        dyhead[...] = dy[0:8, :]
        pl.when(i == nblk - 1)(rs_finish)

    rrow = lambda wd: pl.BlockSpec((TB, wd), lambda i: (nblk - 1 - i, 0))
    st = lambda r: pl.BlockSpec((ncb, r, MW), lambda i: (nblk - 1 - i, 0, 0))
    prev8 = pl.BlockSpec((8, 1024), lambda i: (jnp.maximum((nblk - 1 - i) * (TB // 8) - 1, 0), 0))
    res = pl.pallas_call(
        body, name="mlstm_bwd", grid=(nblk,),
        in_specs=[rrow(1024), prev8, rrow(MW), rrow(MW), rrow(128), _cspec((4, 1024)), _cspec((1, 1024)),
                  _cspec((1, 128)), _cspec((1, MW)), st(128), st(8), st(8), rrow(MW)] + [ANY] * nw,
        out_specs=[rrow(DM_W),
                   pl.BlockSpec((4, 1024), lambda i: (0, 0)), pl.BlockSpec((1, 1024), lambda i: (0, 0)),
                   pl.BlockSpec((1, MW), lambda i: (0, 0)), pl.BlockSpec((1, 128), lambda i: (0, 0))] + [ANY] * nw,
        out_shape=[jax.ShapeDtypeStruct((S, DM_W), BF16),
                   jax.ShapeDtypeStruct((4, 1024), F32), jax.ShapeDtypeStruct((1, 1024), F32),
                   jax.ShapeDtypeStruct((1, MW), F32), jax.ShapeDtypeStruct((1, 128), F32)]
        + [jax.ShapeDtypeStruct(a.shape, a.dtype) for a in parts],
        scratch_shapes=[pltpu.VMEM((128, MW), F32), pltpu.VMEM((8, MW), F32), pltpu.VMEM((8, 1024), F32),
                        pltpu.VMEM((TB, MW), F32), pltpu.VMEM((TB, MW), F32), pltpu.VMEM((TB, 1024), F32)]
        + _scatter_scratch(nw),
        compiler_params=_params(1),
    )(mqk, mqk, mv, mo, gates, conv_w, conv_b, gate_b, gn, cs, ns, ms, dout, *parts)
    return res[:5], res[5:]


def _out_proj(x, attn, ml, w, g):
    tm = TM

    def body(x_ref, a_ref, m_ref, w_ref, g_ref, h_ref, u_ref):
        h1 = x_ref[...] + _dot(_bf(a_ref[...]), w_ref[0:AW, :]) + _dot(_bf(m_ref[...]), w_ref[AW:D, :])
        h_ref[...] = h1
        n, _ = _rms(h1)
        u_ref[...] = _bf(n * g_ref[...])

    row = lambda wd: pl.BlockSpec((tm, wd), lambda i: (i, 0))
    return pl.pallas_call(
        body, name="out_proj", grid=(S // tm,),
        in_specs=[row(D), row(AW), row(MW), _cspec((D, D)), _cspec((1, D))],
        out_specs=[row(D), row(D)],
        out_shape=[jax.ShapeDtypeStruct((S, D), F32), jax.ShapeDtypeStruct((S, D), BF16)],
        compiler_params=_params(1),
    )(x, attn, ml, w, g)


HALF = DFF // NDEV // 2


def _mlp_fwd(h1, u2, w_up, w_down_a, w_down_b):
    tm = TM

    def body(h_ref, u_ref, wu_ref, wa_ref, wb_ref, a_ref, o_ref):
        u = u_ref[...]
        acc = h_ref[...]
        for c in range(NDEV):
            cols = slice(c * 512, (c + 1) * 512)
            a = _dot(u, wu_ref[c])
            a_ref[:, cols] = _bf(a)
            r = jnp.maximum(a, 0.0)
            r = _bf(r * r)
            acc = acc + _dot(r[:, 0:HALF], wa_ref[c]) + _dot(r[:, HALF:2 * HALF], wb_ref[c])
        o_ref[...] = acc

    row = lambda wd: pl.BlockSpec((tm, wd), lambda i: (i, 0))
    return pl.pallas_call(
        body, name="mlp_fwd", grid=(S // tm,),
        in_specs=[row(D), row(D), _cspec((NDEV, D, DFF // NDEV)), _cspec((NDEV, HALF, D)), _cspec((NDEV, HALF, D))],
        out_specs=[row(DFF), row(D)],
        out_shape=[jax.ShapeDtypeStruct((S, DFF), BF16), jax.ShapeDtypeStruct((S, D), F32)],
        compiler_params=_params(1),
    )(h1, u2, w_up, w_down_a, w_down_b)


def _ple_loss(h2, p, target, w_pg, w_ple, g_ple, g_fin):
    tm = TM

    def body(h_ref, p_ref, t_ref, wg_ref, wp_ref, gp_ref, gf_ref,
             dh_ref, dwg_ref, dwp_ref, dgp_ref, dgf_ref, loss_ref, acc_g, acc_p):
        i = pl.program_id(0)

        @pl.when(i == 0)
        def _():
            acc_g[...] = jnp.zeros_like(acc_g)
            acc_p[...] = jnp.zeros_like(acc_p)
            dgp_ref[...] = jnp.zeros_like(dgp_ref)
            dgf_ref[...] = jnp.zeros_like(dgf_ref)
            loss_ref[...] = jnp.zeros_like(loss_ref)

        h2v = h_ref[...]
        n2, rs2 = _rms(h2v)
        u3 = _bf(n2 * gp_ref[...])
        gt = _sigmoid(_dot(u3, wg_ref[...]))
        pb = _bf(p_ref[...])
        e = jnp.concatenate([_dot(pb, wp_ref[j]) for j in range(NDEV)], axis=1)
        h3 = h2v + gt * e
        n3, rs3 = _rms(h3)
        err = n3 * gf_ref[...] - t_ref[...]
        loss_ref[...] = loss_ref[...] + 0.5 / D * jnp.sum(jnp.sum(err * err, axis=1, keepdims=True), axis=0, keepdims=True)
        dy = err * (1.0 / D)
        dgf_ref[...] = dgf_ref[...] + jnp.sum(dy * n3, axis=0, keepdims=True)
        dh3 = _rms_bwd(dy, n3, rs3, gf_ref[...])
        de = _bf(dh3 * gt)
        dz = _bf(dh3 * e * gt * (1.0 - gt))
        acc_p[...] = acc_p[...] + _dot_tn(pb, de)
        acc_g[...] = acc_g[...] + _dot_tn(u3, dz)
        du3 = _dot_nt(dz, wg_ref[...])
        dgp_ref[...] = dgp_ref[...] + jnp.sum(du3 * n2, axis=0, keepdims=True)
        dh_ref[...] = dh3 + _rms_bwd(du3, n2, rs2, gp_ref[...])

        @pl.when(i == S // tm - 1)
        def _():
            dwg_ref[...] = _bf(acc_g[...])
            for j in range(NDEV):
                dwp_ref[j] = _bf(acc_p[:, j * 128:(j + 1) * 128])

    row = lambda wd: pl.BlockSpec((tm, wd), lambda i: (i, 0))
    whole = lambda shp: pl.BlockSpec(shp, lambda i: (0,) * len(shp))
    return pl.pallas_call(
        body, name="ple_loss", grid=(S // tm,),
        in_specs=[row(D), row(PLE), row(D), _cspec((D, D)), _cspec((NDEV, PLE, 128)), _cspec((1, D)), _cspec((1, D))],
        out_specs=[row(D), whole((D, D)), whole((NDEV, PLE, 128)), whole((1, D)), whole((1, D)), whole((1, 1))],
        out_shape=[jax.ShapeDtypeStruct((S, D), F32), jax.ShapeDtypeStruct((D, D), BF16),
                   jax.ShapeDtypeStruct((NDEV, PLE, 128), BF16), jax.ShapeDtypeStruct((1, D), F32),
                   jax.ShapeDtypeStruct((1, D), F32), jax.ShapeDtypeStruct((1, 1), F32)],
        scratch_shapes=[pltpu.VMEM((D, D), F32), pltpu.VMEM((PLE, D), F32)],
        compiler_params=_params(1),
    )(h2, p, target, w_pg, w_ple, g_ple, g_fin)


def _mlp_bwd(dh2, a, h1, g, w_up, w_down_a, w_down_b, parts):
    tm = TM
    nt = S // tm
    nw = len(parts)

    def body(*refs):
        d_ref, a_ref, h_ref, g_ref, wu_ref, wa_ref, wb_ref = refs[:7]
        ins = refs[7:7 + nw]
        da_ref, dh1_ref, dg_ref = refs[7 + nw:10 + nw]
        outs = refs[10 + nw:10 + 2 * nw]
        rs_start, rs_finish = _scatter_phases(ins, outs, *refs[10 + 2 * nw:])
        i = pl.program_id(0)
        pl.when(i == 0)(rs_start)

        @pl.when(i == 0)
        def _():
            dg_ref[...] = jnp.zeros_like(dg_ref)

        dh2v = d_ref[...]
        db = _bf(dh2v)
        du = jnp.zeros((tm, D), F32)
        for c in range(NDEV):
            cols = slice(c * 512, (c + 1) * 512)
            dr = jnp.concatenate([_dot_nt(db, wa_ref[c]), _dot_nt(db, wb_ref[c])], axis=1)
            da = _bf(dr * (2.0 * jnp.maximum(a_ref[:, cols], 0.0)))
            da_ref[:, cols] = da
            du = du + _dot_nt(da, wu_ref[c])
        n, rs = _rms(h_ref[...])
        dg_ref[...] = dg_ref[...] + jnp.sum(du * n, axis=0, keepdims=True)
        dh1_ref[...] = dh2v + _rms_bwd(du, n, rs, g_ref[...])
        pl.when(i == nt - 1)(rs_finish)

    row = lambda wd: pl.BlockSpec((tm, wd), lambda i: (i, 0))
    res = pl.pallas_call(
        body, name="mlp_bwd", grid=(nt,),
        in_specs=[row(D), row(DFF), row(D), _cspec((1, D)), _cspec((NDEV, D, DFF // NDEV)), _cspec((NDEV, HALF, D)),
                  _cspec((NDEV, HALF, D))] + [ANY] * nw,
        out_specs=[row(DFF), row(D), pl.BlockSpec((1, D), lambda i: (0, 0))] + [ANY] * nw,
        out_shape=[jax.ShapeDtypeStruct((S, DFF), BF16), jax.ShapeDtypeStruct((S, D), F32),
                   jax.ShapeDtypeStruct((1, D), F32)] + [jax.ShapeDtypeStruct(p.shape, p.dtype) for p in parts],
        scratch_shapes=_scatter_scratch(nw),
        compiler_params=_params(1),
    )(dh2, a, h1, g, w_up, w_down_a, w_down_b, *parts)
    return res[0], res[1], res[2], res[3:]


def _out_proj_bwd(dh1, attn, ml, w):
    tm = TM

    def body(d_ref, a_ref, m_ref, w_ref, da_ref, dm_ref, dw_ref, acc):
        i = pl.program_id(0)

        @pl.when(i == 0)
        def _():
            acc[...] = jnp.zeros_like(acc)

        db = _bf(d_ref[...])
        dmix = _dot_nt(db, w_ref[...])
        da_ref[...] = dmix[:, 0:AW]
        dm_ref[...] = dmix[:, AW:D]
        acc[0:AW, :] = acc[0:AW, :] + _dot_tn(_bf(a_ref[...]), db)
        acc[AW:D, :] = acc[AW:D, :] + _dot_tn(_bf(m_ref[...]), db)

        @pl.when(i == S // tm - 1)
        def _():
            dw_ref[...] = _bf(acc[...])

    row = lambda wd: pl.BlockSpec((tm, wd), lambda i: (i, 0))
    return pl.pallas_call(
        body, name="out_proj_bwd", grid=(S // tm,),
        in_specs=[row(D), row(AW), row(MW), _cspec((D, D))],
        out_specs=[row(AW), row(MW), pl.BlockSpec((D, D), lambda i: (0, 0))],
        out_shape=[jax.ShapeDtypeStruct((S, AW), F32), jax.ShapeDtypeStruct((S, MW), F32),
                   jax.ShapeDtypeStruct((D, D), BF16)],
        scratch_shapes=[pltpu.VMEM((D, D), F32)],
        compiler_params=_params(1),
    )(dh1, attn, ml, w)


CHIP_FLIPS = [(0, 0), (0, 1), (1, 0), (1, 1)]


def _scatter2_phases(in_ref, out_ref, mine_v, sib_v, psum_v, loc_sems, d2d_send, d2d_recv, ici_send, ici_recv, own_sem):
    x, y, c = _place()
    chips = [((x + dx) % 2, (y + dy) % 2) for dx, dy in CHIP_FLIPS]
    nc = len(chips)

    def local(k):
        return pltpu.make_async_copy(in_ref.at[_dev_index(*chips[k], c)], mine_v.at[k], loc_sems.at[k])

    def to_sib(k):
        return pltpu.make_async_remote_copy(
            src_ref=in_ref.at[_dev_index(*chips[k], 1 - c)], dst_ref=sib_v.at[k], send_sem=d2d_send.at[k],
            recv_sem=d2d_recv.at[k], device_id=(x, y, 1 - c), device_id_type=MESH)

    def over_ici(k):
        return pltpu.make_async_remote_copy(
            src_ref=psum_v.at[k], dst_ref=out_ref.at[k], send_sem=ici_send.at[k - 1], recv_sem=ici_recv.at[k - 1],
            device_id=(*chips[k], c), device_id_type=MESH)

    def own():
        return pltpu.make_async_copy(psum_v.at[0], out_ref.at[0], own_sem)

    def start():
        for k in range(nc):
            to_sib(k).start()
            local(k).start()

    def middle():
        for k in (1, 2, 3, 0):
            local(k).wait()
            to_sib(k).wait_recv()
            psum_v[k] = _bf(mine_v[k].astype(F32) + sib_v[k].astype(F32))
            (over_ici(k) if k else own()).start()

    def finish():
        for k in range(1, nc):
            over_ici(k).wait()
        for k in range(nc):
            to_sib(k).wait_send()
        own().wait()

    return start, middle, finish


def _scatter2_scratch(shard, dtype):
    nc = len(CHIP_FLIPS)
    return ([pltpu.VMEM((nc, *shard), dtype)] * 3
            + [pltpu.SemaphoreType.DMA((nc,))] * 3 + [pltpu.SemaphoreType.DMA((nc - 1,))] * 2 + [pltpu.SemaphoreType.DMA])


def _in_proj_bwd(dparts, n_roped, rope, dh1, x, g1, w, part):
    tm = TM
    nt = S // tm
    widths = [d.shape[1] for d in dparts]
    assert sum(widths) == PW
    npar = len(dparts)

    def body(*refs):
        d_refs = refs[:npar]
        tabs = [t[...] for t in refs[npar:npar + 3]]
        dh_ref, x_ref, g_ref, w_ref, in_ref, dx_ref, dgsum_ref, out_ref = refs[npar + 3:npar + 11]
        rs_start, rs_middle, rs_finish = _scatter2_phases(in_ref, out_ref, *refs[npar + 11:npar + 20])
        dg_ref = refs[npar + 20]
        ar_start, ar_finish = _small_phases([dg_ref], dgsum_ref, *refs[npar + 21:])
        i = pl.program_id(0)
        pl.when(i == 0)(rs_start)
        pl.when(i == 1)(rs_middle)

        @pl.when(i == 0)
        def _():
            dg_ref[...] = jnp.zeros_like(dg_ref)

        du = jnp.zeros((tm, D), F32)
        off = 0
        for j, (d_ref, wd) in enumerate(zip(d_refs, widths)):
            nc = next(c for c in (768, 512) if wd % c == 0)
            for s in range(wd // nc):
                d = d_ref[:, s * nc:(s + 1) * nc]
                du = du + _dot_nt(_unrope(d, *tabs) if j < n_roped else d, w_ref[:, off + s * nc:off + (s + 1) * nc])
            off += wd
        n, rs = _rms(x_ref[...])
        dg_ref[...] = dg_ref[...] + jnp.sum(du * n, axis=0, keepdims=True)
        dx_ref[...] = dh_ref[...] + _rms_bwd(du, n, rs, g_ref[...])

        @pl.when(i == nt - 1)
        def _():
            ar_start()
            rs_finish()
            ar_finish()

    row = lambda wd: pl.BlockSpec((tm, wd), lambda i: (i, 0))
    shard = part.shape[1:]
    return pl.pallas_call(
        body, name="in_proj_bwd", grid=(nt,),
        in_specs=[row(wd) for wd in widths] + [row(128)] * 3 + [row(D), row(D), _cspec((1, D)), _cspec((D, PW)), ANY],
        out_specs=[row(D), VM, ANY],
        out_shape=[jax.ShapeDtypeStruct((S, D), F32), jax.ShapeDtypeStruct((8, 1024), F32),
                   jax.ShapeDtypeStruct((len(CHIP_FLIPS), *shard), part.dtype)],
        scratch_shapes=_scatter2_scratch(shard, part.dtype)
        + [pltpu.VMEM((1, D), F32), pltpu.VMEM((8, 1024), F32), pltpu.VMEM((NDEV, 8, 1024), F32),
           pltpu.SemaphoreType.DMA((7,)), pltpu.SemaphoreType.DMA((7,))],
        compiler_params=_params(1),
    )(*dparts, *rope, dh1, x, g1, w, part)


SMALL_ROWS = 96


def _small_phases(ins, out_ref, pack, rbuf, send_sems, recv_sems):
    x, y, c = _place()
    me = _dev_index(x, y, c)

    def copies():
        out = []
        for k, (dx, dy, dc) in enumerate(FLIPS):
            peer = ((x + dx) % 2, (y + dy) % 2, (c + dc) % 2)
            out.append(pltpu.make_async_remote_copy(
                src_ref=pack, dst_ref=rbuf.at[me], send_sem=send_sems.at[k], recv_sem=recv_sems.at[k],
                device_id=peer, device_id_type=MESH))
        return out

    def start():
        pack[...] = jnp.zeros_like(pack)
        for i, ref in enumerate(ins):
            pack[8 * i:8 * i + 1, 0:ref.shape[1]] = ref[...]
        rbuf[me] = pack[...]
        for cp in copies():
            cp.start()

    def finish():
        for cp in copies():
            cp.wait()
        tot = rbuf[0]
        for j in range(1, NDEV):
            tot = tot + rbuf[j]
        out_ref[...] = tot

    return start, finish


def _wgrad(name, A, Bs, a_fn, b_fn, out_shape, split=None, ts=512, small=(), rope=(), n_roped=0):
    K = A.shape[1]
    widths = [b.shape[1] for b in Bs]
    N = sum(widths)
    nb, ns, nrt = len(Bs) + len(rope), len(small), S // ts
    kc = min(K, 1024)

    def body(*refs):
        a_ref, b_refs = refs[0], refs[1:1 + len(Bs)]
        tabs = [t[...] for t in refs[1 + len(Bs):1 + nb]]
        o_ref = refs[1 + nb + ns]
        acc = refs[2 + nb + ns + bool(ns)]
        r = pl.program_id(0)
        if ns:
            sm_start, sm_finish = _small_phases(refs[1 + nb:1 + nb + ns], refs[2 + nb + ns], *refs[4 + nb + ns:])
            pl.when(r == 0)(sm_start)

        @pl.when(r == 0)
        def _():
            acc[...] = jnp.zeros_like(acc)

        bs, off = [], 0
        for i, (b_ref, w) in enumerate(zip(b_refs, widths)):
            nc = next(c for c in (1024, 768, 512) if w % c == 0)
            fn = (lambda t: _unrope(t, *tabs)) if i < n_roped else b_fn
            bs += [(off + c * nc, nc, fn(b_ref[:, c * nc:(c + 1) * nc])) for c in range(w // nc)]
            off += w
        for kk in range(K // kc):
            rows = slice(kk * kc, (kk + 1) * kc)
            at = a_fn(a_ref[:, rows]).T
            for lo, nc, b in bs:
                acc[rows, lo:lo + nc] = acc[rows, lo:lo + nc] + _dot(at, b)

        @pl.when(r == nrt - 1)
        def _():
            if split is None:
                o_ref[...] = _bf(acc[...])
            else:
                for j in range(NDEV):
                    o_ref[j] = _bf(acc[:, split * j:split * (j + 1)])

        if ns:
            pl.when(r == nrt - 1)(sm_finish)

    in_specs = ([pl.BlockSpec((ts, K), lambda r: (r, 0))] + [pl.BlockSpec((ts, w), lambda r: (r, 0)) for w in widths]
                + [pl.BlockSpec((ts, 128), lambda r: (r, 0))] * len(rope))
    out_spec = pl.BlockSpec(out_shape, lambda r: (0,) * len(out_shape))
    scratch = [pltpu.VMEM((K, N), F32)]
    if not ns:
        return pl.pallas_call(
            body, name=name, grid=(nrt,), in_specs=in_specs, out_specs=out_spec,
            out_shape=jax.ShapeDtypeStruct(out_shape, BF16), scratch_shapes=scratch, compiler_params=_params(1),
        )(A, *Bs, *rope)
    return pl.pallas_call(
        body, name=name, grid=(nrt,), in_specs=in_specs + [VM] * ns, out_specs=[out_spec, VM],
        out_shape=[jax.ShapeDtypeStruct(out_shape, BF16), jax.ShapeDtypeStruct((SMALL_ROWS, 1024), F32)],
        scratch_shapes=scratch + [pltpu.VMEM((SMALL_ROWS, 1024), F32), pltpu.VMEM((NDEV, SMALL_ROWS, 1024), F32),
                                  pltpu.SemaphoreType.DMA((7,)), pltpu.SemaphoreType.DMA((7,))],
        compiler_params=_params(1),
    )(A, *Bs, *rope, *small)


def _relu2_bf(a):
    r = jnp.maximum(a.astype(F32), 0.0)
    return _bf(r * r)


def _ident(a):
    return a


def _step(x, p, target, g1, conv_b, gate_b, gn, g_mlp, g_ple, g_fin, sh):
    (g_in, g_conv), (rc, ra, rb) = _gather_weights([sh["w_in"], sh["conv_w"]], [BF16, F32])
    conv_w = g_conv.transpose(1, 0, 2).reshape(4, 1024)
    w_in_p = _join_w_in(g_in)
    (qkv, mqk, mv, mo, gates, u1), (w_out8, w_pg8, w_ple8) = _in_proj(
        x, g1, w_in_p, rc, ra, rb, [sh["w_out"], sh["w_ple_gate"], sh["w_ple"]], [BF16] * 3)
    attn, lse, (w_up8, w_down_a) = _attn_fwd(qkv, [sh["w_up"], sh["w_down"][0:HALF]], [BF16] * 2)
    ml, cs, ns, ms, (w_down_b,) = _mlstm_fwd(mqk, mv, mo, gates, conv_w, conv_b, gate_b, gn,
                                             [sh["w_down"][HALF:2 * HALF]], [BF16])
    w_out, w_pg = w_out8.reshape(D, D), w_pg8.reshape(D, D)
    h1, u2 = _out_proj(x, attn, ml, w_out, g_mlp)
    a, h2 = _mlp_fwd(h1, u2, w_up8, w_down_a, w_down_b)
    dh2, dw_pg, dw_ple8, dg_ple, dg_fin, loss = _ple_loss(h2, p, target, w_pg, w_ple8, g_ple, g_fin)
    da, dh1, dg_mlp, (r_pg, r_ple) = _mlp_bwd(dh2, a, h1, g_mlp, w_up8, w_down_a, w_down_b,
                                              [dw_pg.reshape(NDEV, D // NDEV, D), dw_ple8])
    dw_up8 = _wgrad("wgrad_up", u2, [da], _ident, _ident, (NDEV, D, DFF // NDEV), split=DFF // NDEV)
    dw_down = _wgrad("wgrad_down", a, [dh2], _relu2_bf, _bf, (DFF, D))
    d_attn, d_ml, dw_out = _out_proj_bwd(dh1, attn, ml, w_out)
    (dm, dconv_w, dconv_b, dgn, dgate_b), (r_down,) = _mlstm_bwd(
        mqk, mv, mo, gates, conv_w, conv_b, gate_b, gn, cs, ns, ms, d_ml, [dw_down.reshape(NDEV, DFF // NDEV, D)])
    dq, dk, dv, (r_up, r_out) = _attn_bwd(qkv, attn, lse, d_attn, [dw_up8, dw_out.reshape(NDEV, D // NDEV, D)])
    dparts = [dq, dk, dv, dm]
    small = [jnp.zeros((1, D), F32), dconv_b, dgate_b, dgn, dg_mlp, dg_ple, dg_fin, loss]
    dw_in8, total = _wgrad("wgrad_in", u1, dparts, _ident, _ident, (NDEV, D, IN_W // NDEV), split=IN_W // NDEV,
                           small=small + [dconv_w[j:j + 1] for j in range(4)], rope=(rc, ra, rb), n_roped=2)
    dx, dg1_sum, r_in = _in_proj_bwd(dparts, 2, (rc, ra, rb), dh1, x, g1, w_in_p, dw_in8)
    recv = dict(w_in=r_in, w_out=r_out, w_up=r_up, w_down=r_down, w_ple_gate=r_pg, w_ple=r_ple)
    return dx, recv, total, dg1_sum


def _gather_weights(shards, dtypes):
    nw = len(shards)

    def body(*refs):
        ins, parts = refs[:nw], refs[nw:nw + 4]
        outs, tables = refs[nw + 4:2 * nw + 4], refs[2 * nw + 4:2 * nw + 7]
        start, forward, finish = _gather_phases(ins, outs, refs[2 * nw + 7:3 * nw + 7], *refs[3 * nw + 7:])
        start()
        _rope_fill(*parts, *tables)
        forward()
        finish()

    res = pl.pallas_call(
        body, name="gather_weights",
        in_specs=[VM] * (nw + 4), out_specs=[ANY] * nw + [VM] * 3,
        out_shape=_gather_shapes(shards, dtypes) + [jax.ShapeDtypeStruct((S, 128), F32)] * 3,
        scratch_shapes=_gather_scratch(shards, dtypes),
        compiler_params=_params(),
    )(*shards, *_rope_parts())
    return res[:nw], res[nw:]


ADAM_STEPS = 4


def _adamw(items):
    n = len(items)

    def body(*refs):
        for i in range(n):
            g_ref, w_ref, m_ref, v_ref = refs[4 * i:4 * i + 4]
            go_ref, d_ref, mo_ref, vo_ref = refs[4 * n + 4 * i:4 * n + 4 * i + 4]
            g = g_ref[0].astype(F32)
            for j in range(1, g_ref.shape[0]):
                g = g + g_ref[j].astype(F32)
            go_ref[...] = g
            d_ref[...], mo_ref[...], vo_ref[...] = _adam_update(g, w_ref[...], m_ref[...], v_ref[...])

    in_specs, out_specs, out_shape, args = [], [], [], []
    for gparts, w, m, v in items:
        P, R, C = gparts.shape
        if R % (8 * ADAM_STEPS) == 0:
            tr = R // ADAM_STEPS
            row, gspec = pl.BlockSpec((tr, C), lambda i: (i, 0)), pl.BlockSpec((P, tr, C), lambda i: (0, i, 0))
        else:
            row, gspec = pl.BlockSpec((R, C), lambda i: (0, 0)), pl.BlockSpec((P, R, C), lambda i: (0, 0, 0))
        in_specs += [gspec, row, row, row]
        out_specs += [row] * 4
        out_shape += [jax.ShapeDtypeStruct((R, C), F32)] * 4
        args += [gparts, w, m, v]
    res = pl.pallas_call(
        body, name="adamw", grid=(ADAM_STEPS,), in_specs=in_specs, out_specs=out_specs, out_shape=out_shape,
        compiler_params=_params(1),
    )(*args)
    return [res[4 * i:4 * i + 4] for i in range(n)]


SMALL = ("norm_mix_g", "conv_b", "gate_b", "mlstm_norm_g", "norm_mlp_g", "norm_ple_g", "final_norm_g")


def _adam_update(g, w, m, v):
    c1 = 1.0 - ADAM_B1 ** ADAM_STEP
    c2 = 1.0 - ADAM_B2 ** ADAM_STEP
    m2 = ADAM_B1 * m + (1.0 - ADAM_B1) * g
    v2 = ADAM_B2 * v + (1.0 - ADAM_B2) * (g * g)
    return -ADAM_LR * ((m2 / c1) / (jnp.sqrt(v2 / c2) + ADAM_EPS) + ADAM_WD * w), m2, v2


def _adamw_small(total, first, ws, ms, vs):
    n = len(ws)

    def body(*refs):
        t_ref, f_ref = refs[:2]
        refs = refs[1:]
        outs = refs[1 + 3 * n:]
        for i in range(n):
            w_ref, m_ref, v_ref = refs[1 + i], refs[1 + n + i], refs[1 + 2 * n + i]
            g = (t_ref if i else f_ref)[8 * i:8 * i + 1, 0:w_ref.shape[1]]
            delta, m2, v2 = _adam_update(g, w_ref[...], m_ref[...], v_ref[...])
            for ref, val in zip(outs[4 * i:4 * i + 4], (g, delta, m2, v2)):
                ref[...] = val

    res = pl.pallas_call(
        body, name="adamw_small",
        out_shape=[jax.ShapeDtypeStruct(w.shape, F32) for w in ws for _ in range(4)],
        compiler_params=_params(),
    )(total, first, *ws, *ms, *vs)
    return [res[4 * i:4 * i + 4] for i in range(n)]


def kernel(x, p, norm_mix_g, w_in, conv_w, conv_b, gate_b, mlstm_norm_g, w_out, norm_mlp_g, w_up, w_down, norm_ple_g, w_ple_gate, w_ple, final_norm_g, loss_target, m_norm_mix_g, m_w_in, m_conv_w, m_conv_b, m_gate_b, m_mlstm_norm_g, m_w_out, m_norm_mlp_g, m_w_up, m_w_down, m_norm_ple_g, m_w_ple_gate, m_w_ple, m_final_norm_g, v_norm_mix_g, v_w_in, v_conv_w, v_conv_b, v_gate_b, v_mlstm_norm_g, v_w_out, v_norm_mlp_g, v_w_up, v_w_down, v_norm_ple_g, v_w_ple_gate, v_w_ple, v_final_norm_g):
    big_names = ("w_in", "conv_w", "w_out", "w_up", "w_down", "w_ple_gate", "w_ple")
    wts = dict(w_in=w_in, conv_w=conv_w, w_out=w_out, w_up=w_up, w_down=w_down, w_ple_gate=w_ple_gate, w_ple=w_ple)
    mom = dict(w_in=m_w_in, conv_w=m_conv_w, w_out=m_w_out, w_up=m_w_up, w_down=m_w_down, w_ple_gate=m_w_ple_gate,
               w_ple=m_w_ple)
    var = dict(w_in=v_w_in, conv_w=v_conv_w, w_out=v_w_out, w_up=v_w_up, w_down=v_w_down, w_ple_gate=v_w_ple_gate,
               w_ple=v_w_ple)
    sq = lambda a: a.reshape(a.shape[1:])
    fin = final_norm_g.reshape(1, D)
    dx, recv, total, first = _step(
        x[0], p[0, 0], loss_target[0], norm_mix_g, conv_b, jnp.pad(gate_b, ((0, 0), (0, 120))), mlstm_norm_g,
        norm_mlp_g, norm_ple_g, fin, {n: sq(wts[n]) for n in big_names})

    nrow = 8 * len(SMALL)
    me = _dev_index(*_place())
    conv_rows = total[nrow + 8:nrow + 40:8]
    recv["conv_w"] = lax.dynamic_slice_in_dim(conv_rows, me * 128, 128, axis=1).reshape(1, 4, 128)
    out = {}
    for n, res in zip(big_names, _adamw([(recv[n], sq(wts[n]), sq(mom[n]), sq(var[n])) for n in big_names])):
        out[n] = [t.reshape(wts[n].shape) for t in res]
    sw = dict(norm_mix_g=norm_mix_g, conv_b=conv_b, gate_b=gate_b, mlstm_norm_g=mlstm_norm_g, norm_mlp_g=norm_mlp_g,
              norm_ple_g=norm_ple_g, final_norm_g=fin)
    sm = dict(norm_mix_g=m_norm_mix_g, conv_b=m_conv_b, gate_b=m_gate_b, mlstm_norm_g=m_mlstm_norm_g,
              norm_mlp_g=m_norm_mlp_g, norm_ple_g=m_norm_ple_g, final_norm_g=m_final_norm_g.reshape(1, D))
    sv = dict(norm_mix_g=v_norm_mix_g, conv_b=v_conv_b, gate_b=v_gate_b, mlstm_norm_g=v_mlstm_norm_g,
              norm_mlp_g=v_norm_mlp_g, norm_ple_g=v_norm_ple_g, final_norm_g=v_final_norm_g.reshape(1, D))
    res = _adamw_small(total, first, [sw[n] for n in SMALL], [sm[n] for n in SMALL], [sv[n] for n in SMALL])
    for n, r in zip(SMALL, res):
        out[n] = [t.reshape(final_norm_g.shape) for t in r] if n == "final_norm_g" else list(r)
    order = ("norm_mix_g", "w_in", "conv_w", "conv_b", "gate_b", "mlstm_norm_g", "w_out", "norm_mlp_g", "w_up", "w_down",
             "norm_ple_g", "w_ple_gate", "w_ple", "final_norm_g")
    loss_all = total[nrow, 0]
    return (loss_all, dx[None], *[out[n][0] for n in order], *[out[n][1] for n in order],
            *[out[n][2] for n in order], *[out[n][3] for n in order])
```

```python
import math

import jax
import jax.numpy as jnp
from jax import lax
from jax.experimental import pallas as pl
from jax.experimental.pallas import tpu as pltpu

F32, BF16 = jnp.float32, jnp.bfloat16
S = 4096
D = 1024
AW = 512
MW = 512
DFF = 4096
PLE = 256
IN_W = 3592
PW = 3840
NDEV = 8
EPS = 1e-6
NEG = -1e30
LC = 128
TB = 256
ROPE_THETA = 500000.0
VMEM_LIMIT = 56 * 1024 * 1024
HI = lax.Precision.HIGHEST

ADAM_LR, ADAM_B1, ADAM_B2, ADAM_EPS, ADAM_WD, ADAM_STEP = 0.001, 0.9, 0.999, 1e-08, 0.01, 10


def _params(n_grid=0, **kw):
    sem = dict(dimension_semantics=("arbitrary",) * n_grid) if n_grid else {}
    return pltpu.CompilerParams(vmem_limit_bytes=VMEM_LIMIT, **sem, **kw)


def _cspec(shape):
    nd = len(shape)
    return pl.BlockSpec(shape, lambda *_: (0,) * nd, pipeline_mode=pl.Buffered(1))


def _dot(a, b):
    return jnp.dot(a, b, preferred_element_type=F32)


def _dot_nt(a, b):
    return lax.dot_general(a, b, (((1,), (1,)), ((), ())), preferred_element_type=F32)


def _dot_tn(a, b):
    return lax.dot_general(a, b, (((0,), (0,)), ((), ())), preferred_element_type=F32)


def _bf(x):
    return x.astype(BF16)


def _rms(x):
    rs = lax.rsqrt(jnp.mean(x * x, axis=-1, keepdims=True) + EPS)
    return x * rs, rs


def _rms_bwd(du, n, rs, g):
    dn = du * g
    return rs * (dn - n * jnp.mean(dn * n, axis=-1, keepdims=True))


def _sigmoid(x):
    return 1.0 / (1.0 + jnp.exp(-x))


ROPE_BLK = 512


def _rope_parts():
    def cs(n, step):
        j = lax.broadcasted_iota(jnp.int32, (n, 128), 1) % 64
        pos = (lax.broadcasted_iota(jnp.int32, (n, 128), 0) * step).astype(F32)
        ang = pos * jnp.power(ROPE_THETA, -(j % 8).astype(F32) / 8.0)
        return jnp.cos(ang), jnp.sin(ang)

    return (*cs(ROPE_BLK, 1), *cs(S // ROPE_BLK, ROPE_BLK))


def _rope_fill(co_ref, so_ref, cb_ref, sb_ref, rc_ref, ra_ref, rb_ref):
    j = lax.broadcasted_iota(jnp.int32, (ROPE_BLK, 128), 1) % 64
    co, so = co_ref[...], so_ref[...]
    for t in range(S // ROPE_BLK):
        cb, sb = cb_ref[t:t + 1, :], sb_ref[t:t + 1, :]
        cos, sin = cb * co - sb * so, sb * co + cb * so
        rows = slice(t * ROPE_BLK, (t + 1) * ROPE_BLK)
        rc_ref[rows, :] = jnp.where(j < 16, cos, 1.0)
        ra_ref[rows, :] = jnp.where(j < 8, -sin, 0.0)
        rb_ref[rows, :] = jnp.where((j >= 8) & (j < 16), sin, 0.0)


def _rope(blk, c, a, b):
    return blk * c + pltpu.roll(blk, 120, 1) * a + pltpu.roll(blk, 8, 1) * b


def _rope_bwd(d, c, a, b):
    return d * c + pltpu.roll(d * a, 8, 1) + pltpu.roll(d * b, 120, 1)


def _unrope(t, c, a, b):
    return jnp.concatenate([_bf(_rope_bwd(t[:, j * 128:(j + 1) * 128].astype(F32), c, a, b))
                            for j in range(t.shape[1] // 128)], axis=1)


MESH = pl.DeviceIdType.MESH
ANY = pl.BlockSpec(memory_space=pl.ANY)
VM = pl.BlockSpec(memory_space=pltpu.VMEM)
FLIPS = [(dx, dy, dc) for dx in (0, 1) for dy in (0, 1) for dc in (0, 1)][1:]


def _place():
    return lax.axis_index("x"), lax.axis_index("y"), lax.axis_index("c")


def _dev_index(px, py, pc):
    return 4 * px + 2 * py + pc


def _gather_phases(ins, outs, bufs, send_sems=None, recv_sems=None, local_sems=None):
    nw = len(ins)
    if nw == 0:
        return (lambda: None,) * 3
    x, y, c = _place()
    me, sib = (x, y, c), (x, y, 1 - c)
    chips = [(1 - x, y), (x, 1 - y), (1 - x, 1 - y)]

    def copy(w, k, block, to, from_buf=False):
        dst = outs[w].at[_dev_index(*block)]
        return pltpu.make_async_remote_copy(
            src_ref=bufs[w] if from_buf else dst, dst_ref=dst, send_sem=send_sems.at[w, k],
            recv_sem=recv_sems.at[w, k], device_id=to, device_id_type=MESH)

    def mine(w):
        return pltpu.make_async_copy(bufs[w], outs[w].at[_dev_index(*me)], local_sems.at[w])

    def first(w):
        return [copy(w, 0, me, sib, True)] + [copy(w, 1 + j, me, (*chip, c), True) for j, chip in enumerate(chips)]

    def passed(w):
        return [copy(w, 4 + j, (*chip, c), sib) for j, chip in enumerate(chips)]

    def start():
        for w in range(nw):
            bufs[w][...] = ins[w][...].astype(bufs[w].dtype)
        for w in range(nw):
            mine(w).start()
            for cp in first(w):
                cp.start()

    def forward():
        for j, chip in enumerate(chips):
            for w in range(nw):
                copy(w, 1 + j, (*chip, c), me).wait_recv()
                passed(w)[j].start()

    def finish():
        for w in range(nw):
            copy(w, 0, sib, me).wait_recv()
        for j, chip in enumerate(chips):
            for w in range(nw):
                copy(w, 4 + j, (*chip, 1 - c), me).wait_recv()
        for w in range(nw):
            for cp in first(w) + passed(w):
                cp.wait_send()
            mine(w).wait()

    return start, forward, finish


def _gather_scratch(shards, dtypes):
    nw = len(shards)
    if nw == 0:
        return []
    return ([pltpu.VMEM(s.shape, dt) for s, dt in zip(shards, dtypes)]
            + [pltpu.SemaphoreType.DMA((nw, 7)), pltpu.SemaphoreType.DMA((nw, 7)), pltpu.SemaphoreType.DMA((nw,))])


def _gather_shapes(shards, dtypes):
    return [jax.ShapeDtypeStruct((NDEV, *s.shape), dt) for s, dt in zip(shards, dtypes)]


def _scatter_phases(ins, outs, send_sems=None, recv_sems=None, local_sems=None):
    nw = len(ins)
    if nw == 0:
        return (lambda: None,) * 2
    x, y, c = _place()
    me = _dev_index(x, y, c)

    def copies():
        out = []
        for w in range(nw):
            out.append(pltpu.make_async_copy(ins[w].at[me], outs[w].at[me], local_sems.at[w]))
            for k, (dx, dy, dc) in enumerate(FLIPS):
                peer = ((x + dx) % 2, (y + dy) % 2, (c + dc) % 2)
                out.append(pltpu.make_async_remote_copy(
                    src_ref=ins[w].at[_dev_index(*peer)], dst_ref=outs[w].at[me], send_sem=send_sems.at[w, k],
                    recv_sem=recv_sems.at[w, k], device_id=peer, device_id_type=MESH))
        return out

    def start():
        for cp in copies():
            cp.start()

    def finish():
        for cp in copies():
            cp.wait()

    return start, finish


def _scatter_scratch(nw):
    if nw == 0:
        return []
    return [pltpu.SemaphoreType.DMA((nw, 7)), pltpu.SemaphoreType.DMA((nw, 7)), pltpu.SemaphoreType.DMA((nw,))]


TM = 512


def _join_w_in(wg):
    sw = IN_W // NDEV

    def body(wg_ref, w_ref):
        for j in range(NDEV):
            w_ref[:, sw * j:sw * (j + 1)] = wg_ref[j]
        w_ref[:, IN_W:PW] = jnp.zeros((D, PW - IN_W), BF16)

    return pl.pallas_call(body, name="join_w_in", out_shape=jax.ShapeDtypeStruct((D, PW), BF16),
                          compiler_params=_params())(wg)


def _in_proj(x, g1, w, rc, ra, rb, shards, dtypes):
    tm = TM
    nw = len(shards)
    nt = S // tm

    def body(*refs):
        x_ref, g_ref, w_ref, rc_ref, ra_ref, rb_ref = refs[:6]
        ins = refs[6:6 + nw]
        qkv_ref, mqk_ref, mv_ref, mo_ref, gt_ref, u_ref = refs[6 + nw:12 + nw]
        outs = refs[12 + nw:12 + 2 * nw]
        bufs = refs[12 + 2 * nw:12 + 3 * nw]
        ag_start, ag_forward, ag_finish = _gather_phases(ins, outs, bufs, *refs[12 + 3 * nw:])
        i = pl.program_id(0)
        pl.when(i == 0)(ag_start)
        pl.when(i == nt - 2)(ag_forward)
        n, _ = _rms(x_ref[...])
        u = _bf(n * g_ref[...])
        u_ref[...] = u
        c, a, b = rc_ref[...], ra_ref[...], rb_ref[...]
        for half in range(2):
            blk = _dot(u, w_ref[:, half * 512:(half + 1) * 512])
            for t in range(4):
                lo = half * 512 + t * 128
                qkv_ref[:, lo:lo + 128] = _rope(blk[:, t * 128:(t + 1) * 128], c, a, b)
        qkv_ref[:, 1024:1536] = _dot(u, w_ref[:, 1024:1536])
        mqk_ref[:, 0:512] = _dot(u, w_ref[:, 1536:2048])
        mqk_ref[:, 512:1024] = _dot(u, w_ref[:, 2048:2560])
        mv_ref[...] = _dot(u, w_ref[:, 2560:3072])
        mo_ref[...] = _dot(u, w_ref[:, 3072:3584])
        gt_ref[...] = _dot(u, w_ref[:, 3584:3712])
        pl.when(i == nt - 1)(ag_finish)

    row = lambda wd: pl.BlockSpec((tm, wd), lambda i: (i, 0))
    res = pl.pallas_call(
        body, name="in_proj", grid=(nt,),
        in_specs=[row(D), _cspec((1, D)), _cspec((D, PW)), row(128), row(128), row(128)] + [VM] * nw,
        out_specs=[row(1536), row(1024), row(512), row(512), row(128), row(D)] + [ANY] * nw,
        out_shape=[jax.ShapeDtypeStruct((S, 1536), F32), jax.ShapeDtypeStruct((S, 1024), F32),
                   jax.ShapeDtypeStruct((S, 512), F32), jax.ShapeDtypeStruct((S, 512), F32),
                   jax.ShapeDtypeStruct((S, 128), F32), jax.ShapeDtypeStruct((S, D), BF16)]
        + _gather_shapes(shards, dtypes),
        scratch_shapes=_gather_scratch(shards, dtypes),
        compiler_params=_params(1),
    )(x, g1, w, rc, ra, rb, *shards)
    return res[:6], res[6:]


DILATIONS = (16, 4, 1)


def _attn_valid(n):
    kd = lax.broadcasted_iota(jnp.int32, (128, 256), 1) - lax.broadcasted_iota(jnp.int32, (128, 256), 0)
    off = jnp.where(n == 0, 0, 128)
    return (kd <= off) & (kd >= off - 128)


def _attn_rows(d, r, n):
    if d == 1:
        q0 = pl.multiple_of(n * 128, 128)
        k0 = pl.multiple_of(jnp.maximum(n - 1, 0) * 128, 128)
        return pl.ds(q0, 128), pl.ds(k0, 256), _attn_valid(n)
    q0 = r + n * 128 * d
    k0 = r + jnp.maximum(n - 1, 0) * 128 * d
    return pl.ds(q0, 128, stride=d), pl.ds(k0, 256, stride=d), _attn_valid(n)


ATTN_GROUP = 4
ATTN_ITERS = S // 128 // ATTN_GROUP


def _attn_group(d, i):
    nb = S // (128 * d)
    if nb == 2:
        qi = lax.broadcasted_iota(jnp.int32, (256, 256), 0) - lax.broadcasted_iota(jnp.int32, (256, 256), 1)
        whole = [pl.ds((ATTN_GROUP // 2) * i + u, 256, stride=d) for u in range(ATTN_GROUP // 2)]
        return [(rows, rows, (qi >= 0) & (qi <= 128)) for rows in whole]
    if d == 1:
        return [_attn_rows(1, 0, i + ATTN_ITERS * u) for u in range(ATTN_GROUP)]
    return [_attn_rows(d, (i // nb) * ATTN_GROUP + u, i % nb) for u in range(ATTN_GROUP)]


def _head0(shape):
    return lax.broadcasted_iota(jnp.int32, shape, 1) < 64


def _stack_heads(t):
    h0 = _head0(t.shape)
    tb = _bf(t)
    zero = jnp.zeros_like(tb)
    return jnp.concatenate([jnp.where(h0, tb, zero), jnp.where(h0, zero, tb)], axis=0)


def _attn_fwd(qkv, shards, dtypes):
    nw = len(shards)

    def body(*refs):
        q_ref, k_ref, v_ref = refs[:3]
        ins = refs[3:3 + nw]
        o_ref, lse0_ref, lse1_ref = refs[3 + nw:6 + nw]
        outs = refs[6 + nw:6 + 2 * nw]
        m0, m1, l0, l1, acc = refs[6 + 2 * nw:11 + 2 * nw]
        bufs = refs[11 + 2 * nw:11 + 3 * nw]
        ag_start, ag_forward, ag_finish = _gather_phases(ins, outs, bufs, *refs[11 + 3 * nw:])
        hp = pl.program_id(0)
        pl.when(hp == 0)(ag_start)
        pl.when(hp == 3)(ag_forward)
        stats = (m0, m1, l0, l1, acc)

        def update(blocks, first):
            loaded = [([q_ref[rq, :], k_ref[rk, :], v_ref[rk, :]], None if first else [ref[rq, :] for ref in stats])
                      for rq, rk, _ in blocks]
            both = lambda a, b: jnp.concatenate([a, b], axis=0)
            ss = [jnp.where(both(valid, valid), _dot_nt(_stack_heads(q * 0.125), _bf(k)), NEG)
                  for ((q, k, _), _), (_, _, valid) in zip(loaded, blocks)]
            mcs = [jnp.max(s, axis=-1, keepdims=True) for s in ss]
            if first:
                m2s = [jnp.broadcast_to(mc, (mc.shape[0], 128)) for mc in mcs]
            else:
                m2s = [jnp.maximum(both(prev[0], prev[1]), mc) for mc, (_, prev) in zip(mcs, loaded)]
            ps = [jnp.exp(s - jnp.tile(m2, (1, 2))) for s, m2 in zip(ss, m2s)]
            l2s = [jnp.sum(p, axis=-1, keepdims=True) for p in ps]
            acc2s = [_dot(_bf(p), _bf(v)) for p, ((_, _, v), _) in zip(ps, loaded)]
            results = []
            for m2, l2, acc2, (_, prev) in zip(m2s, l2s, acc2s, loaded):
                nq = m2.shape[0] // 2
                if first:
                    l2 = jnp.broadcast_to(l2, (2 * nq, 128))
                else:
                    alpha = jnp.exp(both(prev[0], prev[1]) - m2)
                    l2, acc2 = alpha * both(prev[2], prev[3]) + l2, alpha * both(prev[4], prev[4]) + acc2
                results.append((m2[0:nq], m2[nq:2 * nq], l2[0:nq], l2[nq:2 * nq],
                                jnp.where(_head0((nq, 128)), acc2[0:nq], acc2[nq:2 * nq])))
            for (rq, _, _), res in zip(blocks, results):
                for ref, val in zip(stats, res):
                    ref[rq, :] = val

        for d in DILATIONS:
            def step(i, carry, d=d):
                update(_attn_group(d, i), d == DILATIONS[0])
                return carry

            lax.fori_loop(0, ATTN_ITERS, step, 0)

        def fin(t, carry):
            rows = pl.ds(pl.multiple_of(t * 256, 256), 256)
            h0 = lax.broadcasted_iota(jnp.int32, (256, 128), 1) < 64
            la, lb = l0[rows, :], l1[rows, :]
            o_ref[rows, :] = acc[rows, :] / jnp.where(h0, la, lb)
            lse0_ref[rows, :] = m0[rows, :] + jnp.log(la)
            lse1_ref[rows, :] = m1[rows, :] + jnp.log(lb)
            return carry

        lax.fori_loop(0, S // 256, fin, 0)
        pl.when(hp == 3)(ag_finish)

    col = lambda off: pl.BlockSpec((S, 128), lambda h, off=off: (0, off + h))
    res = pl.pallas_call(
        body, name="attn_fwd", grid=(4,),
        in_specs=[col(0), col(4), col(8)] + [VM] * nw,
        out_specs=[col(0), col(0), col(0)] + [ANY] * nw,
        out_shape=[jax.ShapeDtypeStruct((S, AW), F32)] * 3 + _gather_shapes(shards, dtypes),
        scratch_shapes=[pltpu.VMEM((S, 128), F32)] * 5 + _gather_scratch(shards, dtypes),
        compiler_params=_params(1),
    )(qkv, qkv, qkv, *shards)
    return res[0], (res[1], res[2]), res[3:]


def _attn_bwd(qkv, o, lse, do, parts):
    nw = len(parts)

    def body(*refs):
        q_ref, k_ref, v_ref, o_ref, L0, L1, do_ref = refs[:7]
        ins = refs[7:7 + nw]
        dq_out, dk_out, dv_out = refs[7 + nw:10 + nw]
        outs = refs[10 + nw:10 + 2 * nw]
        D0, D1, dq_ref, dk_ref, dv_ref = refs[10 + 2 * nw:15 + 2 * nw]
        rs_start, rs_finish = _scatter_phases(ins, outs, *refs[15 + 2 * nw:])
        hp = pl.program_id(0)
        pl.when(hp == 0)(rs_start)

        def pre(t, carry):
            rows = pl.ds(pl.multiple_of(t * 256, 256), 256)
            h0 = lax.broadcasted_iota(jnp.int32, (256, 128), 1) < 64
            dd = do_ref[rows, :] * o_ref[rows, :]
            shp = (256, 128)
            D0[rows, :] = jnp.broadcast_to(jnp.sum(jnp.where(h0, dd, 0.0), axis=-1, keepdims=True), shp)
            D1[rows, :] = jnp.broadcast_to(jnp.sum(jnp.where(h0, 0.0, dd), axis=-1, keepdims=True), shp)
            return carry

        lax.fori_loop(0, S // 256, pre, 0)

        def update(blocks, first):
            loaded = [([q_ref[rq, :], k_ref[rk, :], v_ref[rk, :], do_ref[rq, :]],
                       [L0[rq, :], L1[rq, :], D0[rq, :], D1[rq, :]],
                       [0.0] * 3 if first else [dq_ref[rq, :], dk_ref[rk, :], dv_ref[rk, :]]) for rq, rk, _ in blocks]
            cat = lambda a, b: jnp.tile(jnp.concatenate([a, b], axis=0), (1, 2))
            ops = [(_stack_heads(q * 0.125), _stack_heads(q), _stack_heads(dout), _bf(k), _bf(v))
                   for (q, k, v, dout), _, _ in loaded]
            ss = [jnp.where(jnp.concatenate([valid, valid], axis=0), _dot_nt(qs, kb), NEG)
                  for (qs, _, _, kb, _), (_, _, valid) in zip(ops, blocks)]
            dps = [_dot_nt(do2, vb) for _, _, do2, _, vb in ops]
            ps = [jnp.exp(s - cat(st[0], st[1])) for s, (_, st, _) in zip(ss, loaded)]
            dss = [_bf(p * (dp - cat(st[2], st[3])) * 0.125) for p, dp, (_, st, _) in zip(ps, dps, loaded)]
            dq2s = [_dot(ds, kb) for ds, (_, _, _, kb, _) in zip(dss, ops)]
            dks = [_dot_tn(ds, q2) for ds, (_, q2, _, _, _) in zip(dss, ops)]
            dvs = [_dot_tn(_bf(p), do2) for p, (_, _, do2, _, _) in zip(ps, ops)]
            results = []
            for (_, _, (dq, dk, dv)), dq2, dkk, dvv in zip(loaded, dq2s, dks, dvs):
                nq = dq2.shape[0] // 2
                results.append((dq + jnp.where(_head0((nq, 128)), dq2[0:nq], dq2[nq:2 * nq]), dk + dkk, dv + dvv))
            for (rq, rk, _), (dq, dk, dv) in zip(blocks, results):
                dq_ref[rq, :] = dq
                dk_ref[rk, :] = dk
                dv_ref[rk, :] = dv

        assert S // (128 * DILATIONS[0]) == 2
        for d in DILATIONS:
            def step(i, carry, d=d):
                update(_attn_group(d, i), d == DILATIONS[0])
                return carry

            lax.fori_loop(0, ATTN_ITERS, step, 0)

        def fin(t, carry):
            rows = pl.ds(pl.multiple_of(t * 256, 256), 256)
            for src, dst in ((dq_ref, dq_out), (dk_ref, dk_out), (dv_ref, dv_out)):
                dst[rows, :] = _bf(src[rows, :])
            return carry

        lax.fori_loop(0, S // 256, fin, 0)
        pl.when(hp == 3)(rs_finish)

    col = lambda off: pl.BlockSpec((S, 128), lambda h, off=off: (0, off + h))
    res = pl.pallas_call(
        body, name="attn_bwd", grid=(4,),
        in_specs=[col(0), col(4), col(8), col(0), col(0), col(0), col(0)] + [ANY] * nw,
        out_specs=[col(0), col(0), col(0)] + [ANY] * nw,
        out_shape=[jax.ShapeDtypeStruct((S, AW), BF16)] * 3 + [jax.ShapeDtypeStruct(a.shape, a.dtype) for a in parts],
        scratch_shapes=[pltpu.VMEM((S, 128), F32)] * 5 + _scatter_scratch(nw),
        compiler_params=_params(1),
    )(qkv, qkv, qkv, o, lse[0], lse[1], do, *parts)
    return res[0], res[1], res[2], res[3:]


def _logsig(x):
    return jnp.minimum(x, 0.0) - jnp.log1p(jnp.exp(-jnp.abs(x)))


def _conv_taps(xp, n):
    return [xp[8:] if j == 3 else pltpu.roll(xp, 3 - j, 0)[8:] for j in range(4)]


def _conv_silu(xp, w_ref, b_ref, n):
    taps = _conv_taps(xp, n)
    c = b_ref[...] + sum(w_ref[j:j + 1, :] * taps[j] for j in range(4))
    sg = _sigmoid(c)
    return c, sg, taps


def _chunk_gates(G):
    assert LC == 128
    r = lax.broadcasted_iota(jnp.int32, (LC, LC), 0)
    c = lax.broadcasted_iota(jnp.int32, (LC, LC), 1)
    tril = (c <= r).astype(F32)
    triu = (c >= r).astype(F32)
    b_col = jnp.dot(tril, _logsig(G), preferred_element_type=F32, precision=HI)
    return b_col, b_col.T, G.T, tril, triu


def _colpick(X, lane):
    li = lax.broadcasted_iota(jnp.int32, X.shape, 1)
    return jnp.sum(jnp.where(li == lane, X, 0.0), axis=1, keepdims=True)


def _rowpick(XT, row):
    ri = lax.broadcasted_iota(jnp.int32, XT.shape, 0)
    return jnp.sum(jnp.where(ri == row, XT, 0.0), axis=0, keepdims=True)


def _each(f, *lists):
    return [f(*a) for a in zip(*lists)]


def _mlstm_heads(Q, K, V, G, b_col, b_row, g_row, C, N, M):
    hs = range(len(Q))
    bt = [_colpick(b_col, 4 + h) for h in hs]
    i_col = [_colpick(G, h) for h in hs]
    bs = [_rowpick(b_row, 4 + h) for h in hs]
    i_row = [_rowpick(g_row, h) for h in hs]
    r = lax.broadcasted_iota(jnp.int32, (LC, LC), 0)
    c = lax.broadcasted_iota(jnp.int32, (LC, LC), 1)
    lane = lax.broadcasted_iota(jnp.int32, (1, LC), 1)
    qb, kb, vb = [_bf(t) for t in Q], [_bf(t) for t in K], [_bf(t) for t in V]
    S_ = _each(_dot_nt, qb, kb)
    qC = _each(lambda q, ch: _dot(q, _bf(ch)), qb, C)
    log_d = _each(lambda a, b, i: jnp.where(c <= r, a - b + i, NEG), bt, bs, i_row)
    log_inter = _each(lambda a, m: a + m, bt, M)
    m_t = _each(lambda li, ld: jnp.maximum(li, jnp.max(ld, axis=1, keepdims=True)), log_inter, log_d)
    Dm = _each(lambda ld, m: jnp.exp(ld - m), log_d, m_t)
    g = _each(lambda li, m: jnp.exp(li - m), log_inter, m_t)
    Am = _each(lambda s, d: s * d, S_, Dm)
    AV = _each(lambda a, v: _dot(_bf(a), v), Am, vb)
    num = _each(lambda gg, qc, av: gg * qc + av, g, qC, AV)
    qn = _each(lambda q, n: jnp.sum(q * n, axis=1, keepdims=True), Q, N)
    den = _each(lambda gg, x, a: gg * x + jnp.sum(a, axis=1, keepdims=True), g, qn, Am)
    floor = [jnp.exp(-m) for m in m_t]
    inv_dd = _each(lambda d, f: 1.0 / jnp.maximum(jnp.abs(d), f), den, floor)
    hh = _each(lambda n, i: n * i, num, inv_dd)
    blast = [jnp.sum(jnp.where(lane == LC - 1, b, 0.0), axis=1, keepdims=True) for b in bs]
    log_s = _each(lambda bl, a, i: bl - a + i, blast, bt, i_col)
    m_new = _each(lambda bl, m, ls: jnp.maximum(bl + m, jnp.max(ls, axis=0, keepdims=True)), blast, M, log_s)
    decay = _each(lambda bl, m, mn: jnp.exp(bl + m - mn), blast, M, m_new)
    ws = _each(lambda ls, mn: jnp.exp(ls - mn), log_s, m_new)
    kw = _each(lambda k, w: k * w, K, ws)
    KV = _each(lambda k, v: _dot_tn(_bf(k), v), kw, vb)
    C_new = _each(lambda d, ch, kv: d * ch + kv, decay, C, KV)
    n_new = _each(lambda d, n, k: d * n + jnp.sum(k, axis=0, keepdims=True), decay, N, kw)
    return dict(Dm=Dm, g=g, Am=Am, qC=qC, qn=qn, den=den, floor=floor, inv_dd=inv_dd, h=hh, decay=decay, ws=ws, kw=kw,
                C_new=C_new, n_new=n_new, m_new=m_new, qb=qb, kb=kb, vb=vb)


def _head_out(hh, mo_h, gn_h):
    r = lax.rsqrt(jnp.mean(hh * hh, axis=-1, keepdims=True) + EPS)
    hn = hh * r
    sg = _sigmoid(mo_h)
    return sg * (hn * gn_h), hn, r, sg


def _mlstm_fwd(mqk, mv, mo, gates, conv_w, conv_b, gate_b, gn, shards, dtypes):
    nblk = S // TB
    ncb = TB // LC
    nw = len(shards)

    def body(*refs):
        x_ref, v_ref, o_ref, g_ref, w_ref, b_ref, gb_ref, gn_ref = refs[:8]
        ins = refs[8:8 + nw]
        out_ref, cs_ref, ns_ref, ms_ref = refs[8 + nw:12 + nw]
        outs = refs[12 + nw:12 + 2 * nw]
        tail, Cst, nst, mst, qs, ks = refs[12 + 2 * nw:18 + 2 * nw]
        bufs = refs[18 + 2 * nw:18 + 3 * nw]
        ag_start, ag_forward, ag_finish = _gather_phases(ins, outs, bufs, *refs[18 + 3 * nw:])
        i = pl.program_id(0)
        pl.when(i == 0)(ag_start)
        pl.when(i == nblk // 2)(ag_forward)

        @pl.when(i == 0)
        def _():
            tail[...] = jnp.zeros_like(tail)
            Cst[...] = jnp.zeros_like(Cst)
            nst[...] = jnp.zeros_like(nst)
            mst[...] = jnp.zeros_like(mst)

        x = x_ref[...]
        xp = jnp.concatenate([tail[...], x], axis=0)
        tail[...] = x[TB - 8:TB, :]
        c, sg, _ = _conv_silu(xp, w_ref, b_ref, TB)
        y = c * sg
        qs[...] = y[:, 0:MW]
        ks[...] = y[:, MW:2 * MW] * (1.0 / math.sqrt(128.0))

        for cc in range(ncb):
            rows = slice(cc * LC, (cc + 1) * LC)
            G = g_ref[rows, :] + gb_ref[...]
            b_col, b_row, g_row, _, _ = _chunk_gates(G)
            cs_ref[cc] = Cst[...]
            ns_ref[cc] = nst[...]
            ms_ref[cc] = mst[...]
            lns = [slice(h * 128, (h + 1) * 128) for h in range(4)]
            f = _mlstm_heads([qs[rows, ln] for ln in lns], [ks[rows, ln] for ln in lns], [v_ref[rows, ln] for ln in lns],
                             G, b_col, b_row, g_row, [Cst[:, ln] for ln in lns], [nst[0:1, ln] for ln in lns],
                             [jnp.max(mst[0:1, ln], axis=1, keepdims=True) for ln in lns])
            outs = [_head_out(hh, o_ref[rows, ln], gn_ref[:, ln])[0] for hh, ln in zip(f["h"], lns)]
            for h, ln in enumerate(lns):
                out_ref[rows, ln] = outs[h]
                Cst[:, ln] = f["C_new"][h]
                nst[0:1, ln] = f["n_new"][h]
                mst[0:1, ln] = jnp.broadcast_to(f["m_new"][h], (1, 128))
        pl.when(i == nblk - 1)(ag_finish)

    row = lambda wd: pl.BlockSpec((TB, wd), lambda i: (i, 0))
    res = pl.pallas_call(
        body, name="mlstm_fwd", grid=(nblk,),
        in_specs=[row(1024), row(MW), row(MW), row(128), _cspec((4, 1024)), _cspec((1, 1024)), _cspec((1, 128)),
                  _cspec((1, MW))] + [VM] * nw,
        out_specs=[row(MW), pl.BlockSpec((ncb, 128, MW), lambda i: (i, 0, 0)),
                   pl.BlockSpec((ncb, 8, MW), lambda i: (i, 0, 0)), pl.BlockSpec((ncb, 8, MW), lambda i: (i, 0, 0))]
        + [ANY] * nw,
        out_shape=[jax.ShapeDtypeStruct((S, MW), F32), jax.ShapeDtypeStruct((S // LC, 128, MW), F32),
                   jax.ShapeDtypeStruct((S // LC, 8, MW), F32), jax.ShapeDtypeStruct((S // LC, 8, MW), F32)]
        + _gather_shapes(shards, dtypes),
        scratch_shapes=[pltpu.VMEM((8, 1024), F32), pltpu.VMEM((128, MW), F32), pltpu.VMEM((8, MW), F32),
                        pltpu.VMEM((8, MW), F32), pltpu.VMEM((TB, MW), F32), pltpu.VMEM((TB, MW), F32)]
        + _gather_scratch(shards, dtypes),
        compiler_params=_params(1),
    )(mqk, mv, mo, gates, conv_w, conv_b, gate_b, gn, *shards)
    return res[0], res[1], res[2], res[3], res[4:]


DM_V, DM_O, DM_G, DM_W = 1024, 1536, 2048, PW - 3 * AW


def _mlstm_bwd(mqk, mv, mo, gates, conv_w, conv_b, gate_b, gn, cs, ns, ms, dout, parts):
    nblk = S // TB
    ncb = TB // LC
    kscale = 1.0 / math.sqrt(128.0)
    nw = len(parts)

    def body(*refs):
        x_ref, xprev_ref, v_ref, o_ref, g_ref, w_ref, b_ref, gb_ref, gn_ref, cs_ref, ns_ref, ms_ref, do_ref = refs[:13]
        ins = refs[13:13 + nw]
        dm_ref, dw_ref, db_ref, dgn_ref, dgb_ref = refs[13 + nw:18 + nw]
        outs = refs[18 + nw:18 + 2 * nw]
        dCst, dnst, dyhead, qs, ks, dqk = refs[18 + 2 * nw:24 + 2 * nw]
        rs_start, rs_finish = _scatter_phases(ins, outs, *refs[24 + 2 * nw:])
        i = pl.program_id(0)
        blk = nblk - 1 - i
        pl.when(i == 0)(rs_start)

        @pl.when(i == 0)
        def _():
            dCst[...] = jnp.zeros_like(dCst)
            dnst[...] = jnp.zeros_like(dnst)
            dyhead[...] = jnp.zeros_like(dyhead)
            dw_ref[...] = jnp.zeros_like(dw_ref)
            db_ref[...] = jnp.zeros_like(db_ref)
            dgn_ref[...] = jnp.zeros_like(dgn_ref)
            dgb_ref[...] = jnp.zeros_like(dgb_ref)

        x = x_ref[...]
        xprev = jnp.where(blk == 0, 0.0, xprev_ref[...])
        xp = jnp.concatenate([xprev, x], axis=0)
        c, sg, taps = _conv_silu(xp, w_ref, b_ref, TB)
        y = c * sg
        qs[...] = y[:, 0:MW]
        ks[...] = y[:, MW:2 * MW] * kscale
        lane128 = lax.broadcasted_iota(jnp.int32, (LC, 128), 1)
        rowi = lax.broadcasted_iota(jnp.int32, (LC, 1), 0)

        for cc in reversed(range(ncb)):
            rows = slice(cc * LC, (cc + 1) * LC)
            G = g_ref[rows, :] + gb_ref[...]
            b_col, b_row, g_row, _, triu = _chunk_gates(G)
            lns = [slice(h * 128, (h + 1) * 128) for h in range(4)]
            C = [cs_ref[cc, :, ln] for ln in lns]
            N = [ns_ref[cc, 0:1, ln] for ln in lns]
            Q, Kk = [qs[rows, ln] for ln in lns], [ks[rows, ln] for ln in lns]
            dCn, dnn = [dCst[:, ln] for ln in lns], [dnst[0:1, ln] for ln in lns]
            gns, dos, mos = [gn_ref[:, ln] for ln in lns], [do_ref[rows, ln] for ln in lns], [o_ref[rows, ln] for ln in lns]
            f = _mlstm_heads(Q, Kk, [v_ref[rows, ln] for ln in lns], G, b_col, b_row, g_row, C, N,
                             [jnp.max(ms_ref[cc, 0:1, ln], axis=1, keepdims=True) for ln in lns])
            hh, inv_dd, den, g, Am, Dm = f["h"], f["inv_dd"], f["den"], f["g"], f["Am"], f["Dm"]
            qb, kb, vb, ws, decay = f["qb"], f["kb"], f["vb"], f["ws"], f["decay"]
            ho = _each(_head_out, hh, mos, gns)
            hn, r, sgo = [t[1] for t in ho], [t[2] for t in ho], [t[3] for t in ho]
            dmo = _each(lambda d, n, gn_h, s: _bf(d * (n * gn_h) * s * (1.0 - s)), dos, hn, gns, sgo)
            dhm = _each(lambda d, s: d * s, dos, sgo)
            dgn = _each(lambda d, n: jnp.sum(d * n, axis=0, keepdims=True), dhm, hn)
            dhn = _each(lambda d, gn_h: d * gn_h, dhm, gns)
            dh = _each(lambda rr, d, n: rr * (d - n * jnp.mean(d * n, axis=-1, keepdims=True)), r, dhn, hn)
            dnum = _each(lambda d, i: d * i, dh, inv_dd)
            ddd = _each(lambda d, x, i: -jnp.sum(d * x, axis=1, keepdims=True) * i, dh, hh, inv_dd)
            dden = _each(lambda dn_, fl, d: jnp.where(jnp.abs(dn_) >= fl, d * jnp.sign(dn_), 0.0), den, f["floor"], ddd)
            dnb = [_bf(t) for t in dnum]
            gd = _each(lambda gg, d: _bf(gg * d), g, dnum)
            gq = _each(lambda gg, d: gg * d, g, dden)
            dCb = [_bf(t) for t in dCn]
            dA = _each(lambda d, v, dd_: _dot_nt(d, v) + dd_, dnb, vb, dden)
            dv1 = _each(lambda a, d: _dot_tn(_bf(a), d), Am, dnb)
            dq1 = _each(lambda d, ch: _dot_nt(d, _bf(ch)), gd, C)
            dC1 = _each(_dot_tn, qb, gd)
            E = _each(lambda v, d, n: _dot_nt(v, d) + n, vb, dCb, dnn)
            dv2 = _each(lambda k, d: _dot(_bf(k), d), f["kw"], dCb)
            dS = _each(lambda a, d: _bf(a * d), dA, Dm)
            dq2 = _each(_dot, dS, kb)
            dk1 = _each(_dot_tn, dS, qb)
            dq = _each(lambda a, x, n, b: a + x * n + b, dq1, gq, N, dq2)
            dC = _each(lambda d, x, y: d * x + y, decay, dCn, dC1)
            dn = _each(lambda d, x, y, q: d * x + jnp.sum(y * q, axis=0, keepdims=True), decay, dnn, gq, Q)
            dg = _each(lambda d, qc, dd_, x: jnp.sum(d * qc, axis=1, keepdims=True) + dd_ * x, dnum, f["qC"], dden, f["qn"])
            Gm = _each(lambda a, b: a * b, dA, Am)
            gam = _each(lambda a, b: a * b, dg, g)
            dk = _each(lambda a, w, e: (a + w * e) * kscale, dk1, ws, E)
            om = _each(lambda e, k, w: jnp.sum(e * k, axis=1, keepdims=True) * w, E, Kk, ws)
            dv = _each(lambda a, b: _bf(a + b), dv1, dv2)
            ddecay = _each(lambda d, ch, dn_, n: jnp.sum(jnp.sum(d * ch, axis=1, keepdims=True), axis=0, keepdims=True)
                           + jnp.sum(dn_ * n, axis=1, keepdims=True), dCn, C, dnn, N)
            rows_g = [jnp.sum(t, axis=1, keepdims=True) for t in Gm]
            cols_g = [jnp.broadcast_to(jnp.sum(t, axis=0, keepdims=True), (LC, 128)).T for t in Gm]
            last = _each(lambda o, dd_, d: jnp.where(rowi == LC - 1, jnp.sum(o, axis=0, keepdims=True) + dd_ * d, 0.0),
                         om, ddecay, decay)
            db = _each(lambda a, b, o, l, cg: a + b - o + l - cg, rows_g, gam, om, last, cols_g)
            di = _each(lambda cg, o: cg + o, cols_g, om)
            dB = jnp.zeros((LC, 128), F32)
            dI = jnp.zeros((LC, 128), F32)
            for h, ln in enumerate(lns):
                dB = jnp.where(lane128 == 4 + h, db[h], dB)
                dI = jnp.where(lane128 == h, di[h], dI)
                dgn_ref[:, ln] = dgn_ref[:, ln] + dgn[h]
                dCst[:, ln] = dC[h]
                dnst[0:1, ln] = dn[h]
                dqk[rows, ln] = dq[h]
                dqk[rows, MW + h * 128:MW + (h + 1) * 128] = dk[h]
                dm_ref[rows, DM_O + h * 128:DM_O + (h + 1) * 128] = dmo[h]
                dm_ref[rows, DM_V + h * 128:DM_V + (h + 1) * 128] = dv[h]
            dlogf = jnp.dot(triu, dB, preferred_element_type=F32, precision=HI)
            dG = dI + dlogf * _sigmoid(-G)
            dG = jnp.where(lane128 < 8, dG, 0.0)
            dm_ref[rows, DM_G:DM_G + 128] = _bf(dG)
            dm_ref[rows, DM_G + 128:DM_W] = jnp.zeros((LC, DM_W - DM_G - 128), BF16)
            dgb_ref[...] = dgb_ref[...] + jnp.sum(dG, axis=0, keepdims=True)

        dy = dqk[...] * (sg * (1.0 + c * (1.0 - sg)))
        db_ref[...] = db_ref[...] + jnp.sum(dy, axis=0, keepdims=True)
        for j in range(4):
            dw_ref[j:j + 1, :] = dw_ref[j:j + 1, :] + jnp.sum(dy * taps[j], axis=0, keepdims=True)
        dyp = jnp.concatenate([dy, dyhead[...]], axis=0)
        dx = w_ref[3:4, :] * dy
        for j in range(3):
            dx = dx + w_ref[j:j + 1, :] * pltpu.roll(dyp, TB + 8 - (3 - j), 0)[0:TB]
        dm_ref[:, 0:DM_V] = _bf(dx)
        dyhead[...] = dy[0:8, :]
        pl.when(i == nblk - 1)(rs_finish)

    rrow = lambda wd: pl.BlockSpec((TB, wd), lambda i: (nblk - 1 - i, 0))
    st = lambda r: pl.BlockSpec((ncb, r, MW), lambda i: (nblk - 1 - i, 0, 0))
    prev8 = pl.BlockSpec((8, 1024), lambda i: (jnp.maximum((nblk - 1 - i) * (TB // 8) - 1, 0), 0))
    res = pl.pallas_call(
        body, name="mlstm_bwd", grid=(nblk,),
        in_specs=[rrow(1024), prev8, rrow(MW), rrow(MW), rrow(128), _cspec((4, 1024)), _cspec((1, 1024)),
                  _cspec((1, 128)), _cspec((1, MW)), st(128), st(8), st(8), rrow(MW)] + [ANY] * nw,
        out_specs=[rrow(DM_W),
                   pl.BlockSpec((4, 1024), lambda i: (0, 0)), pl.BlockSpec((1, 1024), lambda i: (0, 0)),
                   pl.BlockSpec((1, MW), lambda i: (0, 0)), pl.BlockSpec((1, 128), lambda i: (0, 0))] + [ANY] * nw,
        out_shape=[jax.ShapeDtypeStruct((S, DM_W), BF16),
                   jax.ShapeDtypeStruct((4, 1024), F32), jax.ShapeDtypeStruct((1, 1024), F32),
                   jax.ShapeDtypeStruct((1, MW), F32), jax.ShapeDtypeStruct((1, 128), F32)]
        + [jax.ShapeDtypeStruct(a.shape, a.dtype) for a in parts],
        scratch_shapes=[pltpu.VMEM((128, MW), F32), pltpu.VMEM((8, MW), F32), pltpu.VMEM((8, 1024), F32),
                        pltpu.VMEM((TB, MW), F32), pltpu.VMEM((TB, MW), F32), pltpu.VMEM((TB, 1024), F32)]
        + _scatter_scratch(nw),
        compiler_params=_params(1),
    )(mqk, mqk, mv, mo, gates, conv_w, conv_b, gate_b, gn, cs, ns, ms, dout, *parts)
    return res[:5], res[5:]


def _out_proj(x, attn, ml, w, g):
    tm = TM

    def body(x_ref, a_ref, m_ref, w_ref, g_ref, h_ref, u_ref):
        h1 = x_ref[...] + _dot(_bf(a_ref[...]), w_ref[0:AW, :]) + _dot(_bf(m_ref[...]), w_ref[AW:D, :])
        h_ref[...] = h1
        n, _ = _rms(h1)
        u_ref[...] = _bf(n * g_ref[...])

    row = lambda wd: pl.BlockSpec((tm, wd), lambda i: (i, 0))
    return pl.pallas_call(
        body, name="out_proj", grid=(S // tm,),
        in_specs=[row(D), row(AW), row(MW), _cspec((D, D)), _cspec((1, D))],
        out_specs=[row(D), row(D)],
        out_shape=[jax.ShapeDtypeStruct((S, D), F32), jax.ShapeDtypeStruct((S, D), BF16)],
        compiler_params=_params(1),
    )(x, attn, ml, w, g)


HALF = DFF // NDEV // 2


def _mlp_fwd(h1, u2, w_up, w_down_a, w_down_b):
    tm = TM

    def body(h_ref, u_ref, wu_ref, wa_ref, wb_ref, a_ref, o_ref):
        u = u_ref[...]
        acc = h_ref[...]
        for c in range(NDEV):
            cols = slice(c * 512, (c + 1) * 512)
            a = _dot(u, wu_ref[c])
            a_ref[:, cols] = _bf(a)
            r = jnp.maximum(a, 0.0)
            r = _bf(r * r)
            acc = acc + _dot(r[:, 0:HALF], wa_ref[c]) + _dot(r[:, HALF:2 * HALF], wb_ref[c])
        o_ref[...] = acc

    row = lambda wd: pl.BlockSpec((tm, wd), lambda i: (i, 0))
    return pl.pallas_call(
        body, name="mlp_fwd", grid=(S // tm,),
        in_specs=[row(D), row(D), _cspec((NDEV, D, DFF // NDEV)), _cspec((NDEV, HALF, D)), _cspec((NDEV, HALF, D))],
        out_specs=[row(DFF), row(D)],
        out_shape=[jax.ShapeDtypeStruct((S, DFF), BF16), jax.ShapeDtypeStruct((S, D), F32)],
        compiler_params=_params(1),
    )(h1, u2, w_up, w_down_a, w_down_b)


def _ple_loss(h2, p, target, w_pg, w_ple, g_ple, g_fin):
    tm = TM

    def body(h_ref, p_ref, t_ref, wg_ref, wp_ref, gp_ref, gf_ref,
             dh_ref, dwg_ref, dwp_ref, dgp_ref, dgf_ref, loss_ref, acc_g, acc_p):
        i = pl.program_id(0)

        @pl.when(i == 0)
        def _():
            acc_g[...] = jnp.zeros_like(acc_g)
            acc_p[...] = jnp.zeros_like(acc_p)
            dgp_ref[...] = jnp.zeros_like(dgp_ref)
            dgf_ref[...] = jnp.zeros_like(dgf_ref)
            loss_ref[...] = jnp.zeros_like(loss_ref)

        h2v = h_ref[...]
        n2, rs2 = _rms(h2v)
        u3 = _bf(n2 * gp_ref[...])
        gt = _sigmoid(_dot(u3, wg_ref[...]))
        pb = _bf(p_ref[...])
        e = jnp.concatenate([_dot(pb, wp_ref[j]) for j in range(NDEV)], axis=1)
        h3 = h2v + gt * e
        n3, rs3 = _rms(h3)
        err = n3 * gf_ref[...] - t_ref[...]
        loss_ref[...] = loss_ref[...] + 0.5 / D * jnp.sum(jnp.sum(err * err, axis=1, keepdims=True), axis=0, keepdims=True)
        dy = err * (1.0 / D)
        dgf_ref[...] = dgf_ref[...] + jnp.sum(dy * n3, axis=0, keepdims=True)
        dh3 = _rms_bwd(dy, n3, rs3, gf_ref[...])
        de = _bf(dh3 * gt)
        dz = _bf(dh3 * e * gt * (1.0 - gt))
        acc_p[...] = acc_p[...] + _dot_tn(pb, de)
        acc_g[...] = acc_g[...] + _dot_tn(u3, dz)
        du3 = _dot_nt(dz, wg_ref[...])
        dgp_ref[...] = dgp_ref[...] + jnp.sum(du3 * n2, axis=0, keepdims=True)
        dh_ref[...] = dh3 + _rms_bwd(du3, n2, rs2, gp_ref[...])

        @pl.when(i == S // tm - 1)
        def _():
            dwg_ref[...] = _bf(acc_g[...])
            for j in range(NDEV):
                dwp_ref[j] = _bf(acc_p[:, j * 128:(j + 1) * 128])

    row = lambda wd: pl.BlockSpec((tm, wd), lambda i: (i, 0))
    whole = lambda shp: pl.BlockSpec(shp, lambda i: (0,) * len(shp))
    return pl.pallas_call(
        body, name="ple_loss", grid=(S // tm,),
        in_specs=[row(D), row(PLE), row(D), _cspec((D, D)), _cspec((NDEV, PLE, 128)), _cspec((1, D)), _cspec((1, D))],
        out_specs=[row(D), whole((D, D)), whole((NDEV, PLE, 128)), whole((1, D)), whole((1, D)), whole((1, 1))],
        out_shape=[jax.ShapeDtypeStruct((S, D), F32), jax.ShapeDtypeStruct((D, D), BF16),
                   jax.ShapeDtypeStruct((NDEV, PLE, 128), BF16), jax.ShapeDtypeStruct((1, D), F32),
                   jax.ShapeDtypeStruct((1, D), F32), jax.ShapeDtypeStruct((1, 1), F32)],
        scratch_shapes=[pltpu.VMEM((D, D), F32), pltpu.VMEM((PLE, D), F32)],
        compiler_params=_params(1),
    )(h2, p, target, w_pg, w_ple, g_ple, g_fin)


def _mlp_bwd(dh2, a, h1, g, w_up, w_down_a, w_down_b, parts):
    tm = TM
    nt = S // tm
    nw = len(parts)

    def body(*refs):
        d_ref, a_ref, h_ref, g_ref, wu_ref, wa_ref, wb_ref = refs[:7]
        ins = refs[7:7 + nw]
        da_ref, dh1_ref, dg_ref = refs[7 + nw:10 + nw]
        outs = refs[10 + nw:10 + 2 * nw]
        rs_start, rs_finish = _scatter_phases(ins, outs, *refs[10 + 2 * nw:])
        i = pl.program_id(0)
        pl.when(i == 0)(rs_start)

        @pl.when(i == 0)
        def _():
            dg_ref[...] = jnp.zeros_like(dg_ref)

        dh2v = d_ref[...]
        db = _bf(dh2v)
        du = jnp.zeros((tm, D), F32)
        for c in range(NDEV):
            cols = slice(c * 512, (c + 1) * 512)
            dr = jnp.concatenate([_dot_nt(db, wa_ref[c]), _dot_nt(db, wb_ref[c])], axis=1)
            da = _bf(dr * (2.0 * jnp.maximum(a_ref[:, cols], 0.0)))
            da_ref[:, cols] = da
            du = du + _dot_nt(da, wu_ref[c])
        n, rs = _rms(h_ref[...])
        dg_ref[...] = dg_ref[...] + jnp.sum(du * n, axis=0, keepdims=True)
        dh1_ref[...] = dh2v + _rms_bwd(du, n, rs, g_ref[...])
        pl.when(i == nt - 1)(rs_finish)

    row = lambda wd: pl.BlockSpec((tm, wd), lambda i: (i, 0))
    res = pl.pallas_call(
        body, name="mlp_bwd", grid=(nt,),
        in_specs=[row(D), row(DFF), row(D), _cspec((1, D)), _cspec((NDEV, D, DFF // NDEV)), _cspec((NDEV, HALF, D)),
                  _cspec((NDEV, HALF, D))] + [ANY] * nw,
        out_specs=[row(DFF), row(D), pl.BlockSpec((1, D), lambda i: (0, 0))] + [ANY] * nw,
        out_shape=[jax.ShapeDtypeStruct((S, DFF), BF16), jax.ShapeDtypeStruct((S, D), F32),
                   jax.ShapeDtypeStruct((1, D), F32)] + [jax.ShapeDtypeStruct(p.shape, p.dtype) for p in parts],
        scratch_shapes=_scatter_scratch(nw),
        compiler_params=_params(1),
    )(dh2, a, h1, g, w_up, w_down_a, w_down_b, *parts)
    return res[0], res[1], res[2], res[3:]


def _out_proj_bwd(dh1, attn, ml, w):
    tm = TM

    def body(d_ref, a_ref, m_ref, w_ref, da_ref, dm_ref, dw_ref, acc):
        i = pl.program_id(0)

        @pl.when(i == 0)
        def _():
            acc[...] = jnp.zeros_like(acc)

        db = _bf(d_ref[...])
        dmix = _dot_nt(db, w_ref[...])
        da_ref[...] = dmix[:, 0:AW]
        dm_ref[...] = dmix[:, AW:D]
        acc[0:AW, :] = acc[0:AW, :] + _dot_tn(_bf(a_ref[...]), db)
        acc[AW:D, :] = acc[AW:D, :] + _dot_tn(_bf(m_ref[...]), db)

        @pl.when(i == S // tm - 1)
        def _():
            dw_ref[...] = _bf(acc[...])

    row = lambda wd: pl.BlockSpec((tm, wd), lambda i: (i, 0))
    return pl.pallas_call(
        body, name="out_proj_bwd", grid=(S // tm,),
        in_specs=[row(D), row(AW), row(MW), _cspec((D, D))],
        out_specs=[row(AW), row(MW), pl.BlockSpec((D, D), lambda i: (0, 0))],
        out_shape=[jax.ShapeDtypeStruct((S, AW), F32), jax.ShapeDtypeStruct((S, MW), F32),
                   jax.ShapeDtypeStruct((D, D), BF16)],
        scratch_shapes=[pltpu.VMEM((D, D), F32)],
        compiler_params=_params(1),
    )(dh1, attn, ml, w)


CHIP_FLIPS = [(0, 0), (0, 1), (1, 0), (1, 1)]


def _scatter2_phases(in_ref, out_ref, mine_v, sib_v, psum_v, loc_sems, d2d_send, d2d_recv, ici_send, ici_recv, own_sem):
    x, y, c = _place()
    chips = [((x + dx) % 2, (y + dy) % 2) for dx, dy in CHIP_FLIPS]
    nc = len(chips)

    def local(k):
        return pltpu.make_async_copy(in_ref.at[_dev_index(*chips[k], c)], mine_v.at[k], loc_sems.at[k])

    def to_sib(k):
        return pltpu.make_async_remote_copy(
            src_ref=in_ref.at[_dev_index(*chips[k], 1 - c)], dst_ref=sib_v.at[k], send_sem=d2d_send.at[k],
            recv_sem=d2d_recv.at[k], device_id=(x, y, 1 - c), device_id_type=MESH)

    def over_ici(k):
        return pltpu.make_async_remote_copy(
            src_ref=psum_v.at[k], dst_ref=out_ref.at[k], send_sem=ici_send.at[k - 1], recv_sem=ici_recv.at[k - 1],
            device_id=(*chips[k], c), device_id_type=MESH)

    def own():
        return pltpu.make_async_copy(psum_v.at[0], out_ref.at[0], own_sem)

    def start():
        for k in range(nc):
            to_sib(k).start()
            local(k).start()

    def middle():
        for k in (1, 2, 3, 0):
            local(k).wait()
            to_sib(k).wait_recv()
            psum_v[k] = _bf(mine_v[k].astype(F32) + sib_v[k].astype(F32))
            (over_ici(k) if k else own()).start()

    def finish():
        for k in range(1, nc):
            over_ici(k).wait()
        for k in range(nc):
            to_sib(k).wait_send()
        own().wait()

    return start, middle, finish


def _scatter2_scratch(shard, dtype):
    nc = len(CHIP_FLIPS)
    return ([pltpu.VMEM((nc, *shard), dtype)] * 3
            + [pltpu.SemaphoreType.DMA((nc,))] * 3 + [pltpu.SemaphoreType.DMA((nc - 1,))] * 2 + [pltpu.SemaphoreType.DMA])


def _in_proj_bwd(dparts, n_roped, rope, dh1, x, g1, w, part):
    tm = TM
    nt = S // tm
    widths = [d.shape[1] for d in dparts]
    assert sum(widths) == PW
    npar = len(dparts)

    def body(*refs):
        d_refs = refs[:npar]
        tabs = [t[...] for t in refs[npar:npar + 3]]
        dh_ref, x_ref, g_ref, w_ref, in_ref, dx_ref, dgsum_ref, out_ref = refs[npar + 3:npar + 11]
        rs_start, rs_middle, rs_finish = _scatter2_phases(in_ref, out_ref, *refs[npar + 11:npar + 20])
        dg_ref = refs[npar + 20]
        ar_start, ar_finish = _small_phases([dg_ref], dgsum_ref, *refs[npar + 21:])
        i = pl.program_id(0)
        pl.when(i == 0)(rs_start)
        pl.when(i == 1)(rs_middle)

        @pl.when(i == 0)
        def _():
            dg_ref[...] = jnp.zeros_like(dg_ref)

        du = jnp.zeros((tm, D), F32)
        off = 0
        for j, (d_ref, wd) in enumerate(zip(d_refs, widths)):
            nc = next(c for c in (768, 512) if wd % c == 0)
            for s in range(wd // nc):
                d = d_ref[:, s * nc:(s + 1) * nc]
                du = du + _dot_nt(_unrope(d, *tabs) if j < n_roped else d, w_ref[:, off + s * nc:off + (s + 1) * nc])
            off += wd
        n, rs = _rms(x_ref[...])
        dg_ref[...] = dg_ref[...] + jnp.sum(du * n, axis=0, keepdims=True)
        dx_ref[...] = dh_ref[...] + _rms_bwd(du, n, rs, g_ref[...])

        @pl.when(i == nt - 1)
        def _():
            ar_start()
            rs_finish()
            ar_finish()

    row = lambda wd: pl.BlockSpec((tm, wd), lambda i: (i, 0))
    shard = part.shape[1:]
    return pl.pallas_call(
        body, name="in_proj_bwd", grid=(nt,),
        in_specs=[row(wd) for wd in widths] + [row(128)] * 3 + [row(D), row(D), _cspec((1, D)), _cspec((D, PW)), ANY],
        out_specs=[row(D), VM, ANY],
        out_shape=[jax.ShapeDtypeStruct((S, D), F32), jax.ShapeDtypeStruct((8, 1024), F32),
                   jax.ShapeDtypeStruct((len(CHIP_FLIPS), *shard), part.dtype)],
        scratch_shapes=_scatter2_scratch(shard, part.dtype)
        + [pltpu.VMEM((1, D), F32), pltpu.VMEM((8, 1024), F32), pltpu.VMEM((NDEV, 8, 1024), F32),
           pltpu.SemaphoreType.DMA((7,)), pltpu.SemaphoreType.DMA((7,))],
        compiler_params=_params(1),
    )(*dparts, *rope, dh1, x, g1, w, part)


SMALL_ROWS = 96


def _small_phases(ins, out_ref, pack, rbuf, send_sems, recv_sems):
    x, y, c = _place()
    me = _dev_index(x, y, c)

    def copies():
        out = []
        for k, (dx, dy, dc) in enumerate(FLIPS):
            peer = ((x + dx) % 2, (y + dy) % 2, (c + dc) % 2)
            out.append(pltpu.make_async_remote_copy(
                src_ref=pack, dst_ref=rbuf.at[me], send_sem=send_sems.at[k], recv_sem=recv_sems.at[k],
                device_id=peer, device_id_type=MESH))
        return out

    def start():
        pack[...] = jnp.zeros_like(pack)
        for i, ref in enumerate(ins):
            pack[8 * i:8 * i + 1, 0:ref.shape[1]] = ref[...]
        rbuf[me] = pack[...]
        for cp in copies():
            cp.start()

    def finish():
        for cp in copies():
            cp.wait()
        tot = rbuf[0]
        for j in range(1, NDEV):
            tot = tot + rbuf[j]
        out_ref[...] = tot

    return start, finish


def _wgrad(name, A, Bs, a_fn, b_fn, out_shape, split=None, ts=512, small=(), rope=(), n_roped=0, parts=()):
    K = A.shape[1]
    widths = [b.shape[1] for b in Bs]
    N = sum(widths)
    nb, ns, nw, nrt = len(Bs) + len(rope), len(small), len(parts), S // ts
    kc = min(K, 1024)
    n_out = 1 + bool(ns) + nw

    def body(*refs):
        a_ref, b_refs = refs[0], refs[1:1 + len(Bs)]
        tabs = [t[...] for t in refs[1 + len(Bs):1 + nb]]
        n_in = 1 + nb + ns + nw
        o_ref = refs[n_in]
        acc = refs[n_in + n_out]
        r = pl.program_id(0)
        if ns:
            sm_start, sm_finish = _small_phases(refs[1 + nb:1 + nb + ns], refs[n_in + 1], *refs[n_in + n_out + 1:n_in + n_out + 5])
            pl.when(r == 0)(sm_start)
        if nw:
            rs_start, rs_finish = _scatter_phases(refs[1 + nb + ns:n_in], refs[n_in + 1 + bool(ns):n_in + n_out],
                                                  *refs[n_in + n_out + 1 + 4 * bool(ns):])
            pl.when(r == 0)(rs_start)

        @pl.when(r == 0)
        def _():
            acc[...] = jnp.zeros_like(acc)

        bs, off = [], 0
        for i, (b_ref, w) in enumerate(zip(b_refs, widths)):
            nc = next(c for c in (1024, 768, 512) if w % c == 0)
            fn = (lambda t: _unrope(t, *tabs)) if i < n_roped else b_fn
            bs += [(off + c * nc, nc, fn(b_ref[:, c * nc:(c + 1) * nc])) for c in range(w // nc)]
            off += w
        for kk in range(K // kc):
            rows = slice(kk * kc, (kk + 1) * kc)
            at = a_fn(a_ref[:, rows]).T
            for lo, nc, b in bs:
                acc[rows, lo:lo + nc] = acc[rows, lo:lo + nc] + _dot(at, b)

        @pl.when(r == nrt - 1)
        def _():
            if split is None:
                o_ref[...] = _bf(acc[...])
            else:
                for j in range(NDEV):
                    o_ref[j] = _bf(acc[:, split * j:split * (j + 1)])

        if ns:
            pl.when(r == nrt - 1)(sm_finish)
        if nw:
            pl.when(r == nrt - 1)(rs_finish)

    in_specs = ([pl.BlockSpec((ts, K), lambda r: (r, 0))] + [pl.BlockSpec((ts, w), lambda r: (r, 0)) for w in widths]
                + [pl.BlockSpec((ts, 128), lambda r: (r, 0))] * len(rope) + [VM] * ns + [ANY] * nw)
    out_specs = [pl.BlockSpec(out_shape, lambda r: (0,) * len(out_shape))] + [VM] * bool(ns) + [ANY] * nw
    out_shapes = ([jax.ShapeDtypeStruct(out_shape, BF16)] + [jax.ShapeDtypeStruct((SMALL_ROWS, 1024), F32)] * bool(ns)
                  + [jax.ShapeDtypeStruct(p.shape, p.dtype) for p in parts])
    scratch = [pltpu.VMEM((K, N), F32)]
    if ns:
        scratch += [pltpu.VMEM((SMALL_ROWS, 1024), F32), pltpu.VMEM((NDEV, SMALL_ROWS, 1024), F32),
                    pltpu.SemaphoreType.DMA((7,)), pltpu.SemaphoreType.DMA((7,))]
    res = pl.pallas_call(
        body, name=name, grid=(nrt,), in_specs=in_specs, out_specs=out_specs, out_shape=out_shapes,
        scratch_shapes=scratch + _scatter_scratch(nw), compiler_params=_params(1),
    )(A, *Bs, *rope, *small, *parts)
    if n_out == 1:
        return res[0]
    return (*res[:1 + bool(ns)], tuple(res[1 + bool(ns):])) if nw else tuple(res)


def _relu2_bf(a):
    r = jnp.maximum(a.astype(F32), 0.0)
    return _bf(r * r)


def _ident(a):
    return a


def _step(x, p, target, g1, conv_b, gate_b, gn, g_mlp, g_ple, g_fin, sh):
    (g_in, g_conv), (rc, ra, rb) = _gather_weights([sh["w_in"], sh["conv_w"]], [BF16, F32])
    conv_w = g_conv.transpose(1, 0, 2).reshape(4, 1024)
    w_in_p = _join_w_in(g_in)
    (qkv, mqk, mv, mo, gates, u1), (w_out8, w_pg8, w_ple8) = _in_proj(
        x, g1, w_in_p, rc, ra, rb, [sh["w_out"], sh["w_ple_gate"], sh["w_ple"]], [BF16] * 3)
    attn, lse, (w_up8, w_down_a) = _attn_fwd(qkv, [sh["w_up"], sh["w_down"][0:HALF]], [BF16] * 2)
    ml, cs, ns, ms, (w_down_b,) = _mlstm_fwd(mqk, mv, mo, gates, conv_w, conv_b, gate_b, gn,
                                             [sh["w_down"][HALF:2 * HALF]], [BF16])
    w_out, w_pg = w_out8.reshape(D, D), w_pg8.reshape(D, D)
    h1, u2 = _out_proj(x, attn, ml, w_out, g_mlp)
    a, h2 = _mlp_fwd(h1, u2, w_up8, w_down_a, w_down_b)
    dh2, dw_pg, dw_ple8, dg_ple, dg_fin, loss = _ple_loss(h2, p, target, w_pg, w_ple8, g_ple, g_fin)
    da, dh1, dg_mlp, (r_pg, r_ple) = _mlp_bwd(dh2, a, h1, g_mlp, w_up8, w_down_a, w_down_b,
                                              [dw_pg.reshape(NDEV, D // NDEV, D), dw_ple8])
    dw_up8 = _wgrad("wgrad_up", u2, [da], _ident, _ident, (NDEV, D, DFF // NDEV), split=DFF // NDEV)
    dw_down = _wgrad("wgrad_down", a, [dh2], _relu2_bf, _bf, (DFF, D))
    d_attn, d_ml, dw_out = _out_proj_bwd(dh1, attn, ml, w_out)
    (dm, dconv_w, dconv_b, dgn, dgate_b), (r_down,) = _mlstm_bwd(
        mqk, mv, mo, gates, conv_w, conv_b, gate_b, gn, cs, ns, ms, d_ml, [dw_down.reshape(NDEV, DFF // NDEV, D)])
    dq, dk, dv, (r_up,) = _attn_bwd(qkv, attn, lse, d_attn, [dw_up8])
    dparts = [dq, dk, dv, dm]
    small = [jnp.zeros((1, D), F32), dconv_b, dgate_b, dgn, dg_mlp, dg_ple, dg_fin, loss]
    dw_in8, total, (r_out,) = _wgrad(
        "wgrad_in", u1, dparts, _ident, _ident, (NDEV, D, IN_W // NDEV), split=IN_W // NDEV,
        small=small + [dconv_w[j:j + 1] for j in range(4)], rope=(rc, ra, rb), n_roped=2,
        parts=[dw_out.reshape(NDEV, D // NDEV, D)])
    dx, dg1_sum, r_in = _in_proj_bwd(dparts, 2, (rc, ra, rb), dh1, x, g1, w_in_p, dw_in8)
    recv = dict(w_in=r_in, w_out=r_out, w_up=r_up, w_down=r_down, w_ple_gate=r_pg, w_ple=r_ple)
    return dx, recv, total, dg1_sum


def _gather_weights(shards, dtypes):
    nw = len(shards)

    def body(*refs):
        ins, parts = refs[:nw], refs[nw:nw + 4]
        outs, tables = refs[nw + 4:2 * nw + 4], refs[2 * nw + 4:2 * nw + 7]
        start, forward, finish = _gather_phases(ins, outs, refs[2 * nw + 7:3 * nw + 7], *refs[3 * nw + 7:])
        start()
        _rope_fill(*parts, *tables)
        forward()
        finish()

    res = pl.pallas_call(
        body, name="gather_weights",
        in_specs=[VM] * (nw + 4), out_specs=[ANY] * nw + [VM] * 3,
        out_shape=_gather_shapes(shards, dtypes) + [jax.ShapeDtypeStruct((S, 128), F32)] * 3,
        scratch_shapes=_gather_scratch(shards, dtypes),
        compiler_params=_params(),
    )(*shards, *_rope_parts())
    return res[:nw], res[nw:]


ADAM_STEPS = 4


def _adamw(items):
    n = len(items)

    def body(*refs):
        for i in range(n):
            g_ref, w_ref, m_ref, v_ref = refs[4 * i:4 * i + 4]
            go_ref, d_ref, mo_ref, vo_ref = refs[4 * n + 4 * i:4 * n + 4 * i + 4]
            g = g_ref[0].astype(F32)
            for j in range(1, g_ref.shape[0]):
                g = g + g_ref[j].astype(F32)
            go_ref[...] = g
            d_ref[...], mo_ref[...], vo_ref[...] = _adam_update(g, w_ref[...], m_ref[...], v_ref[...])

    in_specs, out_specs, out_shape, args = [], [], [], []
    for gparts, w, m, v in items:
        P, R, C = gparts.shape
        if R % (8 * ADAM_STEPS) == 0:
            tr = R // ADAM_STEPS
            row, gspec = pl.BlockSpec((tr, C), lambda i: (i, 0)), pl.BlockSpec((P, tr, C), lambda i: (0, i, 0))
        else:
            row, gspec = pl.BlockSpec((R, C), lambda i: (0, 0)), pl.BlockSpec((P, R, C), lambda i: (0, 0, 0))
        in_specs += [gspec, row, row, row]
        out_specs += [row] * 4
        out_shape += [jax.ShapeDtypeStruct((R, C), F32)] * 4
        args += [gparts, w, m, v]
    res = pl.pallas_call(
        body, name="adamw", grid=(ADAM_STEPS,), in_specs=in_specs, out_specs=out_specs, out_shape=out_shape,
        compiler_params=_params(1),
    )(*args)
    return [res[4 * i:4 * i + 4] for i in range(n)]


SMALL = ("norm_mix_g", "conv_b", "gate_b", "mlstm_norm_g", "norm_mlp_g", "norm_ple_g", "final_norm_g")


def _adam_update(g, w, m, v):
    c1 = 1.0 - ADAM_B1 ** ADAM_STEP
    c2 = 1.0 - ADAM_B2 ** ADAM_STEP
    m2 = ADAM_B1 * m + (1.0 - ADAM_B1) * g
    v2 = ADAM_B2 * v + (1.0 - ADAM_B2) * (g * g)
    return -ADAM_LR * ((m2 / c1) / (jnp.sqrt(v2 / c2) + ADAM_EPS) + ADAM_WD * w), m2, v2


def _adamw_small(total, first, ws, ms, vs):
    n = len(ws)

    def body(*refs):
        t_ref, f_ref = refs[:2]
        refs = refs[1:]
        outs = refs[1 + 3 * n:]
        for i in range(n):
            w_ref, m_ref, v_ref = refs[1 + i], refs[1 + n + i], refs[1 + 2 * n + i]
            g = (t_ref if i else f_ref)[8 * i:8 * i + 1, 0:w_ref.shape[1]]
            delta, m2, v2 = _adam_update(g, w_ref[...], m_ref[...], v_ref[...])
            for ref, val in zip(outs[4 * i:4 * i + 4], (g, delta, m2, v2)):
                ref[...] = val

    res = pl.pallas_call(
        body, name="adamw_small",
        out_shape=[jax.ShapeDtypeStruct(w.shape, F32) for w in ws for _ in range(4)],
        compiler_params=_params(),
    )(total, first, *ws, *ms, *vs)
    return [res[4 * i:4 * i + 4] for i in range(n)]


def kernel(x, p, norm_mix_g, w_in, conv_w, conv_b, gate_b, mlstm_norm_g, w_out, norm_mlp_g, w_up, w_down, norm_ple_g, w_ple_gate, w_ple, final_norm_g, loss_target, m_norm_mix_g, m_w_in, m_conv_w, m_conv_b, m_gate_b, m_mlstm_norm_g, m_w_out, m_norm_mlp_g, m_w_up, m_w_down, m_norm_ple_g, m_w_ple_gate, m_w_ple, m_final_norm_g, v_norm_mix_g, v_w_in, v_conv_w, v_conv_b, v_gate_b, v_mlstm_norm_g, v_w_out, v_norm_mlp_g, v_w_up, v_w_down, v_norm_ple_g, v_w_ple_gate, v_w_ple, v_final_norm_g):
    big_names = ("w_in", "conv_w", "w_out", "w_up", "w_down", "w_ple_gate", "w_ple")
    wts = dict(w_in=w_in, conv_w=conv_w, w_out=w_out, w_up=w_up, w_down=w_down, w_ple_gate=w_ple_gate, w_ple=w_ple)
    mom = dict(w_in=m_w_in, conv_w=m_conv_w, w_out=m_w_out, w_up=m_w_up, w_down=m_w_down, w_ple_gate=m_w_ple_gate,
               w_ple=m_w_ple)
    var = dict(w_in=v_w_in, conv_w=v_conv_w, w_out=v_w_out, w_up=v_w_up, w_down=v_w_down, w_ple_gate=v_w_ple_gate,
               w_ple=v_w_ple)
    sq = lambda a: a.reshape(a.shape[1:])
    fin = final_norm_g.reshape(1, D)
    dx, recv, total, first = _step(
        x[0], p[0, 0], loss_target[0], norm_mix_g, conv_b, jnp.pad(gate_b, ((0, 0), (0, 120))), mlstm_norm_g,
        norm_mlp_g, norm_ple_g, fin, {n: sq(wts[n]) for n in big_names})

    nrow = 8 * len(SMALL)
    me = _dev_index(*_place())
    conv_rows = total[nrow + 8:nrow + 40:8]
    recv["conv_w"] = lax.dynamic_slice_in_dim(conv_rows, me * 128, 128, axis=1).reshape(1, 4, 128)
    out = {}
    for n, res in zip(big_names, _adamw([(recv[n], sq(wts[n]), sq(mom[n]), sq(var[n])) for n in big_names])):
        out[n] = [t.reshape(wts[n].shape) for t in res]
    sw = dict(norm_mix_g=norm_mix_g, conv_b=conv_b, gate_b=gate_b, mlstm_norm_g=mlstm_norm_g, norm_mlp_g=norm_mlp_g,
              norm_ple_g=norm_ple_g, final_norm_g=fin)
    sm = dict(norm_mix_g=m_norm_mix_g, conv_b=m_conv_b, gate_b=m_gate_b, mlstm_norm_g=m_mlstm_norm_g,
              norm_mlp_g=m_norm_mlp_g, norm_ple_g=m_norm_ple_g, final_norm_g=m_final_norm_g.reshape(1, D))
    sv = dict(norm_mix_g=v_norm_mix_g, conv_b=v_conv_b, gate_b=v_gate_b, mlstm_norm_g=v_mlstm_norm_g,
              norm_mlp_g=v_norm_mlp_g, norm_ple_g=v_norm_ple_g, final_norm_g=v_final_norm_g.reshape(1, D))
    res = _adamw_small(total, first, [sw[n] for n in SMALL], [sm[n] for n in SMALL], [sv[n] for n in SMALL])
    for n, r in zip(SMALL, res):
        out[n] = [t.reshape(final_norm_g.shape) for t in r] if n == "final_norm_g" else list(r)
    order = ("norm_mix_g", "w_in", "conv_w", "conv_b", "gate_b", "mlstm_norm_g", "w_out", "norm_mlp_g", "w_up", "w_down",
             "norm_ple_g", "w_ple_gate", "w_ple", "final_norm_g")
    loss_all = total[nrow, 0]
    return (loss_all, dx[None], *[out[n][0] for n in order], *[out[n][1] for n in order],
            *[out[n][2] for n in order], *[out[n][3] for n in order])
```

```python
import math

import jax
import jax.numpy as jnp
from jax import lax
from jax.experimental import pallas as pl
from jax.experimental.pallas import tpu as pltpu

F32, BF16 = jnp.float32, jnp.bfloat16
S = 4096
D = 1024
AW = 512
MW = 512
DFF = 4096
PLE = 256
IN_W = 3592
PW = 3840
NDEV = 8
EPS = 1e-6
NEG = -1e30
LC = 128
TB = 256
ROPE_THETA = 500000.0
VMEM_LIMIT = 56 * 1024 * 1024
HI = lax.Precision.HIGHEST

ADAM_LR, ADAM_B1, ADAM_B2, ADAM_EPS, ADAM_WD, ADAM_STEP = 0.001, 0.9, 0.999, 1e-08, 0.01, 10


def _params(n_grid=0, **kw):
    sem = dict(dimension_semantics=("arbitrary",) * n_grid) if n_grid else {}
    return pltpu.CompilerParams(vmem_limit_bytes=VMEM_LIMIT, **sem, **kw)


def _cspec(shape):
    nd = len(shape)
    return pl.BlockSpec(shape, lambda *_: (0,) * nd, pipeline_mode=pl.Buffered(1))


def _dot(a, b):
    return jnp.dot(a, b, preferred_element_type=F32)


def _dot_nt(a, b):
    return lax.dot_general(a, b, (((1,), (1,)), ((), ())), preferred_element_type=F32)


def _dot_tn(a, b):
    return lax.dot_general(a, b, (((0,), (0,)), ((), ())), preferred_element_type=F32)


def _bf(x):
    return x.astype(BF16)


def _rms(x):
    rs = lax.rsqrt(jnp.mean(x * x, axis=-1, keepdims=True) + EPS)
    return x * rs, rs


def _rms_bwd(du, n, rs, g):
    dn = du * g
    return rs * (dn - n * jnp.mean(dn * n, axis=-1, keepdims=True))


def _sigmoid(x):
    return 1.0 / (1.0 + jnp.exp(-x))


ROPE_BLK = 512


def _rope_parts():
    def cs(n, step):
        j = lax.broadcasted_iota(jnp.int32, (n, 128), 1) % 64
        pos = (lax.broadcasted_iota(jnp.int32, (n, 128), 0) * step).astype(F32)
        ang = pos * jnp.power(ROPE_THETA, -(j % 8).astype(F32) / 8.0)
        return jnp.cos(ang), jnp.sin(ang)

    return (*cs(ROPE_BLK, 1), *cs(S // ROPE_BLK, ROPE_BLK))


def _rope_fill(co_ref, so_ref, cb_ref, sb_ref, rc_ref, ra_ref, rb_ref):
    j = lax.broadcasted_iota(jnp.int32, (ROPE_BLK, 128), 1) % 64
    co, so = co_ref[...], so_ref[...]
    for t in range(S // ROPE_BLK):
        cb, sb = cb_ref[t:t + 1, :], sb_ref[t:t + 1, :]
        cos, sin = cb * co - sb * so, sb * co + cb * so
        rows = slice(t * ROPE_BLK, (t + 1) * ROPE_BLK)
        rc_ref[rows, :] = jnp.where(j < 16, cos, 1.0)
        ra_ref[rows, :] = jnp.where(j < 8, -sin, 0.0)
        rb_ref[rows, :] = jnp.where((j >= 8) & (j < 16), sin, 0.0)


def _rope(blk, c, a, b):
    return blk * c + pltpu.roll(blk, 120, 1) * a + pltpu.roll(blk, 8, 1) * b


def _rope_bwd(d, c, a, b):
    return d * c + pltpu.roll(d * a, 8, 1) + pltpu.roll(d * b, 120, 1)


def _unrope(t, c, a, b):
    return jnp.concatenate([_bf(_rope_bwd(t[:, j * 128:(j + 1) * 128].astype(F32), c, a, b))
                            for j in range(t.shape[1] // 128)], axis=1)


MESH = pl.DeviceIdType.MESH
ANY = pl.BlockSpec(memory_space=pl.ANY)
VM = pl.BlockSpec(memory_space=pltpu.VMEM)
FLIPS = [(dx, dy, dc) for dx in (0, 1) for dy in (0, 1) for dc in (0, 1)][1:]


def _place():
    return lax.axis_index("x"), lax.axis_index("y"), lax.axis_index("c")


def _dev_index(px, py, pc):
    return 4 * px + 2 * py + pc


def _gather_phases(ins, outs, bufs, send_sems=None, recv_sems=None, local_sems=None):
    nw = len(ins)
    if nw == 0:
        return (lambda: None,) * 3
    x, y, c = _place()
    me, sib = (x, y, c), (x, y, 1 - c)
    chips = [(1 - x, y), (x, 1 - y), (1 - x, 1 - y)]

    def copy(w, k, block, to, from_buf=False):
        dst = outs[w].at[_dev_index(*block)]
        return pltpu.make_async_remote_copy(
            src_ref=bufs[w] if from_buf else dst, dst_ref=dst, send_sem=send_sems.at[w, k],
            recv_sem=recv_sems.at[w, k], device_id=to, device_id_type=MESH)

    def mine(w):
        return pltpu.make_async_copy(bufs[w], outs[w].at[_dev_index(*me)], local_sems.at[w])

    def first(w):
        return [copy(w, 0, me, sib, True)] + [copy(w, 1 + j, me, (*chip, c), True) for j, chip in enumerate(chips)]

    def passed(w):
        return [copy(w, 4 + j, (*chip, c), sib) for j, chip in enumerate(chips)]

    def start():
        for w in range(nw):
            bufs[w][...] = ins[w][...].astype(bufs[w].dtype)
        for w in range(nw):
            mine(w).start()
            for cp in first(w):
                cp.start()

    def forward():
        for j, chip in enumerate(chips):
            for w in range(nw):
                copy(w, 1 + j, (*chip, c), me).wait_recv()
                passed(w)[j].start()

    def finish():
        for w in range(nw):
            copy(w, 0, sib, me).wait_recv()
        for j, chip in enumerate(chips):
            for w in range(nw):
                copy(w, 4 + j, (*chip, 1 - c), me).wait_recv()
        for w in range(nw):
            for cp in first(w) + passed(w):
                cp.wait_send()
            mine(w).wait()

    return start, forward, finish


def _gather_scratch(shards, dtypes):
    nw = len(shards)
    if nw == 0:
        return []
    return ([pltpu.VMEM(s.shape, dt) for s, dt in zip(shards, dtypes)]
            + [pltpu.SemaphoreType.DMA((nw, 7)), pltpu.SemaphoreType.DMA((nw, 7)), pltpu.SemaphoreType.DMA((nw,))])


def _gather_shapes(shards, dtypes):
    return [jax.ShapeDtypeStruct((NDEV, *s.shape), dt) for s, dt in zip(shards, dtypes)]


def _scatter_phases(ins, outs, send_sems=None, recv_sems=None, local_sems=None):
    nw = len(ins)
    if nw == 0:
        return (lambda: None,) * 2
    x, y, c = _place()
    me = _dev_index(x, y, c)

    def copies():
        out = []
        for w in range(nw):
            out.append(pltpu.make_async_copy(ins[w].at[me], outs[w].at[me], local_sems.at[w]))
            for k, (dx, dy, dc) in enumerate(FLIPS):
                peer = ((x + dx) % 2, (y + dy) % 2, (c + dc) % 2)
                out.append(pltpu.make_async_remote_copy(
                    src_ref=ins[w].at[_dev_index(*peer)], dst_ref=outs[w].at[me], send_sem=send_sems.at[w, k],
                    recv_sem=recv_sems.at[w, k], device_id=peer, device_id_type=MESH))
        return out

    def start():
        for cp in copies():
            cp.start()

    def finish():
        for cp in copies():
            cp.wait()

    return start, finish


def _scatter_scratch(nw):
    if nw == 0:
        return []
    return [pltpu.SemaphoreType.DMA((nw, 7)), pltpu.SemaphoreType.DMA((nw, 7)), pltpu.SemaphoreType.DMA((nw,))]


TM = 512


def _join_w_in(wg):
    sw = IN_W // NDEV

    def body(wg_ref, w_ref):
        for j in range(NDEV):
            w_ref[:, sw * j:sw * (j + 1)] = wg_ref[j]
        w_ref[:, IN_W:PW] = jnp.zeros((D, PW - IN_W), BF16)

    return pl.pallas_call(body, name="join_w_in", out_shape=jax.ShapeDtypeStruct((D, PW), BF16),
                          compiler_params=_params())(wg)


def _in_proj(x, g1, w, rc, ra, rb, shards, dtypes):
    tm = TM
    nw = len(shards)
    nt = S // tm

    def body(*refs):
        x_ref, g_ref, w_ref, rc_ref, ra_ref, rb_ref = refs[:6]
        ins = refs[6:6 + nw]
        qkv_ref, mqk_ref, mv_ref, mo_ref, gt_ref, u_ref = refs[6 + nw:12 + nw]
        outs = refs[12 + nw:12 + 2 * nw]
        bufs = refs[12 + 2 * nw:12 + 3 * nw]
        ag_start, ag_forward, ag_finish = _gather_phases(ins, outs, bufs, *refs[12 + 3 * nw:])
        i = pl.program_id(0)
        pl.when(i == 0)(ag_start)
        pl.when(i == nt - 2)(ag_forward)
        n, _ = _rms(x_ref[...])
        u = _bf(n * g_ref[...])
        u_ref[...] = u
        c, a, b = rc_ref[...], ra_ref[...], rb_ref[...]
        for half in range(2):
            blk = _dot(u, w_ref[:, half * 512:(half + 1) * 512])
            for t in range(4):
                lo = half * 512 + t * 128
                qkv_ref[:, lo:lo + 128] = _rope(blk[:, t * 128:(t + 1) * 128], c, a, b)
        qkv_ref[:, 1024:1536] = _dot(u, w_ref[:, 1024:1536])
        mqk_ref[:, 0:512] = _dot(u, w_ref[:, 1536:2048])
        mqk_ref[:, 512:1024] = _dot(u, w_ref[:, 2048:2560])
        mv_ref[...] = _dot(u, w_ref[:, 2560:3072])
        mo_ref[...] = _dot(u, w_ref[:, 3072:3584])
        gt_ref[...] = _dot(u, w_ref[:, 3584:3712])
        pl.when(i == nt - 1)(ag_finish)

    row = lambda wd: pl.BlockSpec((tm, wd), lambda i: (i, 0))
    res = pl.pallas_call(
        body, name="in_proj", grid=(nt,),
        in_specs=[row(D), _cspec((1, D)), _cspec((D, PW)), row(128), row(128), row(128)] + [VM] * nw,
        out_specs=[row(1536), row(1024), row(512), row(512), row(128), row(D)] + [ANY] * nw,
        out_shape=[jax.ShapeDtypeStruct((S, 1536), F32), jax.ShapeDtypeStruct((S, 1024), F32),
                   jax.ShapeDtypeStruct((S, 512), F32), jax.ShapeDtypeStruct((S, 512), F32),
                   jax.ShapeDtypeStruct((S, 128), F32), jax.ShapeDtypeStruct((S, D), BF16)]
        + _gather_shapes(shards, dtypes),
        scratch_shapes=_gather_scratch(shards, dtypes),
        compiler_params=_params(1),
    )(x, g1, w, rc, ra, rb, *shards)
    return res[:6], res[6:]


DILATIONS = (16, 4, 1)


def _attn_valid(n):
    kd = lax.broadcasted_iota(jnp.int32, (128, 256), 1) - lax.broadcasted_iota(jnp.int32, (128, 256), 0)
    off = jnp.where(n == 0, 0, 128)
    return (kd <= off) & (kd >= off - 128)


def _attn_rows(d, r, n):
    if d == 1:
        q0 = pl.multiple_of(n * 128, 128)
        k0 = pl.multiple_of(jnp.maximum(n - 1, 0) * 128, 128)
        return pl.ds(q0, 128), pl.ds(k0, 256), _attn_valid(n)
    q0 = r + n * 128 * d
    k0 = r + jnp.maximum(n - 1, 0) * 128 * d
    return pl.ds(q0, 128, stride=d), pl.ds(k0, 256, stride=d), _attn_valid(n)


ATTN_GROUP = 4
ATTN_ITERS = S // 128 // ATTN_GROUP


def _attn_group(d, i):
    nb = S // (128 * d)
    if nb == 2:
        qi = lax.broadcasted_iota(jnp.int32, (256, 256), 0) - lax.broadcasted_iota(jnp.int32, (256, 256), 1)
        whole = [pl.ds((ATTN_GROUP // 2) * i + u, 256, stride=d) for u in range(ATTN_GROUP // 2)]
        return [(rows, rows, (qi >= 0) & (qi <= 128)) for rows in whole]
    if d == 1:
        return [_attn_rows(1, 0, i + ATTN_ITERS * u) for u in range(ATTN_GROUP)]
    return [_attn_rows(d, (i // nb) * ATTN_GROUP + u, i % nb) for u in range(ATTN_GROUP)]


def _head0(shape):
    return lax.broadcasted_iota(jnp.int32, shape, 1) < 64


def _stack_heads(t):
    h0 = _head0(t.shape)
    tb = _bf(t)
    zero = jnp.zeros_like(tb)
    return jnp.concatenate([jnp.where(h0, tb, zero), jnp.where(h0, zero, tb)], axis=0)


def _attn_fwd(qkv, shards, dtypes):
    nw = len(shards)

    def body(*refs):
        q_ref, k_ref, v_ref = refs[:3]
        ins = refs[3:3 + nw]
        o_ref, lse0_ref, lse1_ref = refs[3 + nw:6 + nw]
        outs = refs[6 + nw:6 + 2 * nw]
        m0, m1, l0, l1, acc = refs[6 + 2 * nw:11 + 2 * nw]
        bufs = refs[11 + 2 * nw:11 + 3 * nw]
        ag_start, ag_forward, ag_finish = _gather_phases(ins, outs, bufs, *refs[11 + 3 * nw:])
        hp = pl.program_id(0)
        pl.when(hp == 0)(ag_start)
        pl.when(hp == 3)(ag_forward)
        stats = (m0, m1, l0, l1, acc)

        def update(blocks, first):
            loaded = [([q_ref[rq, :], k_ref[rk, :], v_ref[rk, :]], None if first else [ref[rq, :] for ref in stats])
                      for rq, rk, _ in blocks]
            both = lambda a, b: jnp.concatenate([a, b], axis=0)
            ss = [jnp.where(both(valid, valid), _dot_nt(_stack_heads(q * 0.125), _bf(k)), NEG)
                  for ((q, k, _), _), (_, _, valid) in zip(loaded, blocks)]
            mcs = [jnp.max(s, axis=-1, keepdims=True) for s in ss]
            if first:
                m2s = [jnp.broadcast_to(mc, (mc.shape[0], 128)) for mc in mcs]
            else:
                m2s = [jnp.maximum(both(prev[0], prev[1]), mc) for mc, (_, prev) in zip(mcs, loaded)]
            ps = [jnp.exp(s - jnp.tile(m2, (1, 2))) for s, m2 in zip(ss, m2s)]
            l2s = [jnp.sum(p, axis=-1, keepdims=True) for p in ps]
            acc2s = [_dot(_bf(p), _bf(v)) for p, ((_, _, v), _) in zip(ps, loaded)]
            results = []
            for m2, l2, acc2, (_, prev) in zip(m2s, l2s, acc2s, loaded):
                nq = m2.shape[0] // 2
                if first:
                    l2 = jnp.broadcast_to(l2, (2 * nq, 128))
                else:
                    alpha = jnp.exp(both(prev[0], prev[1]) - m2)
                    l2, acc2 = alpha * both(prev[2], prev[3]) + l2, alpha * both(prev[4], prev[4]) + acc2
                results.append((m2[0:nq], m2[nq:2 * nq], l2[0:nq], l2[nq:2 * nq],
                                jnp.where(_head0((nq, 128)), acc2[0:nq], acc2[nq:2 * nq])))
            for (rq, _, _), res in zip(blocks, results):
                for ref, val in zip(stats, res):
                    ref[rq, :] = val

        for d in DILATIONS:
            def step(i, carry, d=d):
                update(_attn_group(d, i), d == DILATIONS[0])
                return carry

            lax.fori_loop(0, ATTN_ITERS, step, 0)

        def fin(t, carry):
            rows = pl.ds(pl.multiple_of(t * 256, 256), 256)
            h0 = lax.broadcasted_iota(jnp.int32, (256, 128), 1) < 64
            la, lb = l0[rows, :], l1[rows, :]
            o_ref[rows, :] = acc[rows, :] / jnp.where(h0, la, lb)
            lse0_ref[rows, :] = m0[rows, :] + jnp.log(la)
            lse1_ref[rows, :] = m1[rows, :] + jnp.log(lb)
            return carry

        lax.fori_loop(0, S // 256, fin, 0)
        pl.when(hp == 3)(ag_finish)

    col = lambda off: pl.BlockSpec((S, 128), lambda h, off=off: (0, off + h))
    res = pl.pallas_call(
        body, name="attn_fwd", grid=(4,),
        in_specs=[col(0), col(4), col(8)] + [VM] * nw,
        out_specs=[col(0), col(0), col(0)] + [ANY] * nw,
        out_shape=[jax.ShapeDtypeStruct((S, AW), F32)] * 3 + _gather_shapes(shards, dtypes),
        scratch_shapes=[pltpu.VMEM((S, 128), F32)] * 5 + _gather_scratch(shards, dtypes),
        compiler_params=_params(1),
    )(qkv, qkv, qkv, *shards)
    return res[0], (res[1], res[2]), res[3:]


def _attn_bwd(qkv, o, lse, do, parts):
    nw = len(parts)

    def body(*refs):
        q_ref, k_ref, v_ref, o_ref, L0, L1, do_ref = refs[:7]
        ins = refs[7:7 + nw]
        dq_out, dk_out, dv_out = refs[7 + nw:10 + nw]
        outs = refs[10 + nw:10 + 2 * nw]
        D0, D1, dq_ref, dk_ref, dv_ref = refs[10 + 2 * nw:15 + 2 * nw]
        rs_start, rs_finish = _scatter_phases(ins, outs, *refs[15 + 2 * nw:])
        hp = pl.program_id(0)
        pl.when(hp == 0)(rs_start)

        def pre(t, carry):
            rows = pl.ds(pl.multiple_of(t * 256, 256), 256)
            h0 = lax.broadcasted_iota(jnp.int32, (256, 128), 1) < 64
            dd = do_ref[rows, :] * o_ref[rows, :]
            shp = (256, 128)
            D0[rows, :] = jnp.broadcast_to(jnp.sum(jnp.where(h0, dd, 0.0), axis=-1, keepdims=True), shp)
            D1[rows, :] = jnp.broadcast_to(jnp.sum(jnp.where(h0, 0.0, dd), axis=-1, keepdims=True), shp)
            return carry

        lax.fori_loop(0, S // 256, pre, 0)

        def update(blocks, first):
            loaded = [([q_ref[rq, :], k_ref[rk, :], v_ref[rk, :], do_ref[rq, :]],
                       [L0[rq, :], L1[rq, :], D0[rq, :], D1[rq, :]],
                       [0.0] * 3 if first else [dq_ref[rq, :], dk_ref[rk, :], dv_ref[rk, :]]) for rq, rk, _ in blocks]
            cat = lambda a, b: jnp.tile(jnp.concatenate([a, b], axis=0), (1, 2))
            ops = [(_stack_heads(q * 0.125), _stack_heads(q), _stack_heads(dout), _bf(k), _bf(v))
                   for (q, k, v, dout), _, _ in loaded]
            ss = [jnp.where(jnp.concatenate([valid, valid], axis=0), _dot_nt(qs, kb), NEG)
                  for (qs, _, _, kb, _), (_, _, valid) in zip(ops, blocks)]
            dps = [_dot_nt(do2, vb) for _, _, do2, _, vb in ops]
            ps = [jnp.exp(s - cat(st[0], st[1])) for s, (_, st, _) in zip(ss, loaded)]
            dss = [_bf(p * (dp - cat(st[2], st[3])) * 0.125) for p, dp, (_, st, _) in zip(ps, dps, loaded)]
            dq2s = [_dot(ds, kb) for ds, (_, _, _, kb, _) in zip(dss, ops)]
            dks = [_dot_tn(ds, q2) for ds, (_, q2, _, _, _) in zip(dss, ops)]
            dvs = [_dot_tn(_bf(p), do2) for p, (_, _, do2, _, _) in zip(ps, ops)]
            results = []
            for (_, _, (dq, dk, dv)), dq2, dkk, dvv in zip(loaded, dq2s, dks, dvs):
                nq = dq2.shape[0] // 2
                results.append((dq + jnp.where(_head0((nq, 128)), dq2[0:nq], dq2[nq:2 * nq]), dk + dkk, dv + dvv))
            for (rq, rk, _), (dq, dk, dv) in zip(blocks, results):
                dq_ref[rq, :] = dq
                dk_ref[rk, :] = dk
                dv_ref[rk, :] = dv

        assert S // (128 * DILATIONS[0]) == 2
        for d in DILATIONS:
            def step(i, carry, d=d):
                update(_attn_group(d, i), d == DILATIONS[0])
                return carry

            lax.fori_loop(0, ATTN_ITERS, step, 0)

        def fin(t, carry):
            rows = pl.ds(pl.multiple_of(t * 256, 256), 256)
            for src, dst in ((dq_ref, dq_out), (dk_ref, dk_out), (dv_ref, dv_out)):
                dst[rows, :] = _bf(src[rows, :])
            return carry

        lax.fori_loop(0, S // 256, fin, 0)
        pl.when(hp == 3)(rs_finish)

    col = lambda off: pl.BlockSpec((S, 128), lambda h, off=off: (0, off + h))
    res = pl.pallas_call(
        body, name="attn_bwd", grid=(4,),
        in_specs=[col(0), col(4), col(8), col(0), col(0), col(0), col(0)] + [ANY] * nw,
        out_specs=[col(0), col(0), col(0)] + [ANY] * nw,
        out_shape=[jax.ShapeDtypeStruct((S, AW), BF16)] * 3 + [jax.ShapeDtypeStruct(a.shape, a.dtype) for a in parts],
        scratch_shapes=[pltpu.VMEM((S, 128), F32)] * 5 + _scatter_scratch(nw),
        compiler_params=_params(1),
    )(qkv, qkv, qkv, o, lse[0], lse[1], do, *parts)
    return res[0], res[1], res[2], res[3:]


def _logsig(x):
    return jnp.minimum(x, 0.0) - jnp.log1p(jnp.exp(-jnp.abs(x)))


def _conv_taps(xp, n):
    return [xp[8:] if j == 3 else pltpu.roll(xp, 3 - j, 0)[8:] for j in range(4)]


def _conv_silu(xp, w_ref, b_ref, n):
    taps = _conv_taps(xp, n)
    c = b_ref[...] + sum(w_ref[j:j + 1, :] * taps[j] for j in range(4))
    sg = _sigmoid(c)
    return c, sg, taps


def _chunk_gates(G):
    assert LC == 128
    r = lax.broadcasted_iota(jnp.int32, (LC, LC), 0)
    c = lax.broadcasted_iota(jnp.int32, (LC, LC), 1)
    tril = (c <= r).astype(F32)
    triu = (c >= r).astype(F32)
    b_col = jnp.dot(tril, _logsig(G), preferred_element_type=F32, precision=HI)
    return b_col, b_col.T, G.T, tril, triu


def _colpick(X, lane):
    li = lax.broadcasted_iota(jnp.int32, X.shape, 1)
    return jnp.sum(jnp.where(li == lane, X, 0.0), axis=1, keepdims=True)


def _rowpick(XT, row):
    ri = lax.broadcasted_iota(jnp.int32, XT.shape, 0)
    return jnp.sum(jnp.where(ri == row, XT, 0.0), axis=0, keepdims=True)


def _each(f, *lists):
    return [f(*a) for a in zip(*lists)]


def _mlstm_heads(Q, K, V, G, b_col, b_row, g_row, C, N, M):
    hs = range(len(Q))
    bt = [_colpick(b_col, 4 + h) for h in hs]
    i_col = [_colpick(G, h) for h in hs]
    bs = [_rowpick(b_row, 4 + h) for h in hs]
    i_row = [_rowpick(g_row, h) for h in hs]
    r = lax.broadcasted_iota(jnp.int32, (LC, LC), 0)
    c = lax.broadcasted_iota(jnp.int32, (LC, LC), 1)
    lane = lax.broadcasted_iota(jnp.int32, (1, LC), 1)
    qb, kb, vb = [_bf(t) for t in Q], [_bf(t) for t in K], [_bf(t) for t in V]
    S_ = _each(_dot_nt, qb, kb)
    qC = _each(lambda q, ch: _dot(q, _bf(ch)), qb, C)
    log_d = _each(lambda a, b, i: jnp.where(c <= r, a - b + i, NEG), bt, bs, i_row)
    log_inter = _each(lambda a, m: a + m, bt, M)
    m_t = _each(lambda li, ld: jnp.maximum(li, jnp.max(ld, axis=1, keepdims=True)), log_inter, log_d)
    Dm = _each(lambda ld, m: jnp.exp(ld - m), log_d, m_t)
    g = _each(lambda li, m: jnp.exp(li - m), log_inter, m_t)
    Am = _each(lambda s, d: s * d, S_, Dm)
    AV = _each(lambda a, v: _dot(_bf(a), v), Am, vb)
    num = _each(lambda gg, qc, av: gg * qc + av, g, qC, AV)
    qn = _each(lambda q, n: jnp.sum(q * n, axis=1, keepdims=True), Q, N)
    den = _each(lambda gg, x, a: gg * x + jnp.sum(a, axis=1, keepdims=True), g, qn, Am)
    floor = [jnp.exp(-m) for m in m_t]
    inv_dd = _each(lambda d, f: 1.0 / jnp.maximum(jnp.abs(d), f), den, floor)
    hh = _each(lambda n, i: n * i, num, inv_dd)
    blast = [jnp.sum(jnp.where(lane == LC - 1, b, 0.0), axis=1, keepdims=True) for b in bs]
    log_s = _each(lambda bl, a, i: bl - a + i, blast, bt, i_col)
    m_new = _each(lambda bl, m, ls: jnp.maximum(bl + m, jnp.max(ls, axis=0, keepdims=True)), blast, M, log_s)
    decay = _each(lambda bl, m, mn: jnp.exp(bl + m - mn), blast, M, m_new)
    ws = _each(lambda ls, mn: jnp.exp(ls - mn), log_s, m_new)
    kw = _each(lambda k, w: k * w, K, ws)
    KV = _each(lambda k, v: _dot_tn(_bf(k), v), kw, vb)
    C_new = _each(lambda d, ch, kv: d * ch + kv, decay, C, KV)
    n_new = _each(lambda d, n, k: d * n + jnp.sum(k, axis=0, keepdims=True), decay, N, kw)
    return dict(Dm=Dm, g=g, Am=Am, qC=qC, qn=qn, den=den, floor=floor, inv_dd=inv_dd, h=hh, decay=decay, ws=ws, kw=kw,
                C_new=C_new, n_new=n_new, m_new=m_new, qb=qb, kb=kb, vb=vb)


def _head_out(hh, mo_h, gn_h):
    r = lax.rsqrt(jnp.mean(hh * hh, axis=-1, keepdims=True) + EPS)
    hn = hh * r
    sg = _sigmoid(mo_h)
    return sg * (hn * gn_h), hn, r, sg


def _mlstm_fwd(mqk, mv, mo, gates, conv_w, conv_b, gate_b, gn, shards, dtypes):
    nblk = S // TB
    ncb = TB // LC
    nw = len(shards)

    def body(*refs):
        x_ref, v_ref, o_ref, g_ref, w_ref, b_ref, gb_ref, gn_ref = refs[:8]
        ins = refs[8:8 + nw]
        out_ref, cs_ref, ns_ref, ms_ref = refs[8 + nw:12 + nw]
        outs = refs[12 + nw:12 + 2 * nw]
        tail, Cst, nst, mst, qs, ks = refs[12 + 2 * nw:18 + 2 * nw]
        bufs = refs[18 + 2 * nw:18 + 3 * nw]
        ag_start, ag_forward, ag_finish = _gather_phases(ins, outs, bufs, *refs[18 + 3 * nw:])
        i = pl.program_id(0)
        pl.when(i == 0)(ag_start)
        pl.when(i == nblk // 2)(ag_forward)

        @pl.when(i == 0)
        def _():
            tail[...] = jnp.zeros_like(tail)
            Cst[...] = jnp.zeros_like(Cst)
            nst[...] = jnp.zeros_like(nst)
            mst[...] = jnp.zeros_like(mst)

        x = x_ref[...]
        xp = jnp.concatenate([tail[...], x], axis=0)
        tail[...] = x[TB - 8:TB, :]
        c, sg, _ = _conv_silu(xp, w_ref, b_ref, TB)
        y = c * sg
        qs[...] = y[:, 0:MW]
        ks[...] = y[:, MW:2 * MW] * (1.0 / math.sqrt(128.0))

        for cc in range(ncb):
            rows = slice(cc * LC, (cc + 1) * LC)
            G = g_ref[rows, :] + gb_ref[...]
            b_col, b_row, g_row, _, _ = _chunk_gates(G)
            cs_ref[cc] = Cst[...]
            ns_ref[cc] = nst[...]
            ms_ref[cc] = mst[...]
            lns = [slice(h * 128, (h + 1) * 128) for h in range(4)]
            f = _mlstm_heads([qs[rows, ln] for ln in lns], [ks[rows, ln] for ln in lns], [v_ref[rows, ln] for ln in lns],
                             G, b_col, b_row, g_row, [Cst[:, ln] for ln in lns], [nst[0:1, ln] for ln in lns],
                             [jnp.max(mst[0:1, ln], axis=1, keepdims=True) for ln in lns])
            outs = [_head_out(hh, o_ref[rows, ln], gn_ref[:, ln])[0] for hh, ln in zip(f["h"], lns)]
            for h, ln in enumerate(lns):
                out_ref[rows, ln] = outs[h]
                Cst[:, ln] = f["C_new"][h]
                nst[0:1, ln] = f["n_new"][h]
                mst[0:1, ln] = jnp.broadcast_to(f["m_new"][h], (1, 128))
        pl.when(i == nblk - 1)(ag_finish)

    row = lambda wd: pl.BlockSpec((TB, wd), lambda i: (i, 0))
    res = pl.pallas_call(
        body, name="mlstm_fwd", grid=(nblk,),
        in_specs=[row(1024), row(MW), row(MW), row(128), _cspec((4, 1024)), _cspec((1, 1024)), _cspec((1, 128)),
                  _cspec((1, MW))] + [VM] * nw,
        out_specs=[row(MW), pl.BlockSpec((ncb, 128, MW), lambda i: (i, 0, 0)),
                   pl.BlockSpec((ncb, 8, MW), lambda i: (i, 0, 0)), pl.BlockSpec((ncb, 8, MW), lambda i: (i, 0, 0))]
        + [ANY] * nw,
        out_shape=[jax.ShapeDtypeStruct((S, MW), F32), jax.ShapeDtypeStruct((S // LC, 128, MW), F32),
                   jax.ShapeDtypeStruct((S // LC, 8, MW), F32), jax.ShapeDtypeStruct((S // LC, 8, MW), F32)]
        + _gather_shapes(shards, dtypes),
        scratch_shapes=[pltpu.VMEM((8, 1024), F32), pltpu.VMEM((128, MW), F32), pltpu.VMEM((8, MW), F32),
                        pltpu.VMEM((8, MW), F32), pltpu.VMEM((TB, MW), F32), pltpu.VMEM((TB, MW), F32)]
        + _gather_scratch(shards, dtypes),
        compiler_params=_params(1),
    )(mqk, mv, mo, gates, conv_w, conv_b, gate_b, gn, *shards)
    return res[0], res[1], res[2], res[3], res[4:]


DM_V, DM_O, DM_G, DM_W = 1024, 1536, 2048, PW - 3 * AW


def _mlstm_bwd(mqk, mv, mo, gates, conv_w, conv_b, gate_b, gn, cs, ns, ms, dout, parts):
    nblk = S // TB
    ncb = TB // LC
    kscale = 1.0 / math.sqrt(128.0)
    nw = len(parts)

    def body(*refs):
        x_ref, xprev_ref, v_ref, o_ref, g_ref, w_ref, b_ref, gb_ref, gn_ref, cs_ref, ns_ref, ms_ref, do_ref = refs[:13]
        ins = refs[13:13 + nw]
        dm_ref, dw_ref, db_ref, dgn_ref, dgb_ref = refs[13 + nw:18 + nw]
        outs = refs[18 + nw:18 + 2 * nw]
        dCst, dnst, dyhead, qs, ks, dqk = refs[18 + 2 * nw:24 + 2 * nw]
        rs_start, rs_finish = _scatter_phases(ins, outs, *refs[24 + 2 * nw:])
        i = pl.program_id(0)
        blk = nblk - 1 - i
        pl.when(i == 0)(rs_start)

        @pl.when(i == 0)
        def _():
            dCst[...] = jnp.zeros_like(dCst)
            dnst[...] = jnp.zeros_like(dnst)
            dyhead[...] = jnp.zeros_like(dyhead)
            dw_ref[...] = jnp.zeros_like(dw_ref)
            db_ref[...] = jnp.zeros_like(db_ref)
            dgn_ref[...] = jnp.zeros_like(dgn_ref)
            dgb_ref[...] = jnp.zeros_like(dgb_ref)

        x = x_ref[...]
        xprev = jnp.where(blk == 0, 0.0, xprev_ref[...])
        xp = jnp.concatenate([xprev, x], axis=0)
        c, sg, taps = _conv_silu(xp, w_ref, b_ref, TB)
        y = c * sg
        qs[...] = y[:, 0:MW]
        ks[...] = y[:, MW:2 * MW] * kscale
        lane128 = lax.broadcasted_iota(jnp.int32, (LC, 128), 1)
        rowi = lax.broadcasted_iota(jnp.int32, (LC, 1), 0)

        for cc in reversed(range(ncb)):
            rows = slice(cc * LC, (cc + 1) * LC)
            G = g_ref[rows, :] + gb_ref[...]
            b_col, b_row, g_row, _, triu = _chunk_gates(G)
            lns = [slice(h * 128, (h + 1) * 128) for h in range(4)]
            C = [cs_ref[cc, :, ln] for ln in lns]
            N = [ns_ref[cc, 0:1, ln] for ln in lns]
            Q, Kk = [qs[rows, ln] for ln in lns], [ks[rows, ln] for ln in lns]
            dCn, dnn = [dCst[:, ln] for ln in lns], [dnst[0:1, ln] for ln in lns]
            gns, dos, mos = [gn_ref[:, ln] for ln in lns], [do_ref[rows, ln] for ln in lns], [o_ref[rows, ln] for ln in lns]
            f = _mlstm_heads(Q, Kk, [v_ref[rows, ln] for ln in lns], G, b_col, b_row, g_row, C, N,
                             [jnp.max(ms_ref[cc, 0:1, ln], axis=1, keepdims=True) for ln in lns])
            hh, inv_dd, den, g, Am, Dm = f["h"], f["inv_dd"], f["den"], f["g"], f["Am"], f["Dm"]
            qb, kb, vb, ws, decay = f["qb"], f["kb"], f["vb"], f["ws"], f["decay"]
            ho = _each(_head_out, hh, mos, gns)
            hn, r, sgo = [t[1] for t in ho], [t[2] for t in ho], [t[3] for t in ho]
            dmo = _each(lambda d, n, gn_h, s: _bf(d * (n * gn_h) * s * (1.0 - s)), dos, hn, gns, sgo)
            dhm = _each(lambda d, s: d * s, dos, sgo)
            dgn = _each(lambda d, n: jnp.sum(d * n, axis=0, keepdims=True), dhm, hn)
            dhn = _each(lambda d, gn_h: d * gn_h, dhm, gns)
            dh = _each(lambda rr, d, n: rr * (d - n * jnp.mean(d * n, axis=-1, keepdims=True)), r, dhn, hn)
            dnum = _each(lambda d, i: d * i, dh, inv_dd)
            ddd = _each(lambda d, x, i: -jnp.sum(d * x, axis=1, keepdims=True) * i, dh, hh, inv_dd)
            dden = _each(lambda dn_, fl, d: jnp.where(jnp.abs(dn_) >= fl, d * jnp.sign(dn_), 0.0), den, f["floor"], ddd)
            dnb = [_bf(t) for t in dnum]
            gd = _each(lambda gg, d: _bf(gg * d), g, dnum)
            gq = _each(lambda gg, d: gg * d, g, dden)
            dCb = [_bf(t) for t in dCn]
            dA = _each(lambda d, v, dd_: _dot_nt(d, v) + dd_, dnb, vb, dden)
            dv1 = _each(lambda a, d: _dot_tn(_bf(a), d), Am, dnb)
            dq1 = _each(lambda d, ch: _dot_nt(d, _bf(ch)), gd, C)
            dC1 = _each(_dot_tn, qb, gd)
            E = _each(lambda v, d, n: _dot_nt(v, d) + n, vb, dCb, dnn)
            dv2 = _each(lambda k, d: _dot(_bf(k), d), f["kw"], dCb)
            dS = _each(lambda a, d: _bf(a * d), dA, Dm)
            dq2 = _each(_dot, dS, kb)
            dk1 = _each(_dot_tn, dS, qb)
            dq = _each(lambda a, x, n, b: a + x * n + b, dq1, gq, N, dq2)
            dC = _each(lambda d, x, y: d * x + y, decay, dCn, dC1)
            dn = _each(lambda d, x, y, q: d * x + jnp.sum(y * q, axis=0, keepdims=True), decay, dnn, gq, Q)
            dg = _each(lambda d, qc, dd_, x: jnp.sum(d * qc, axis=1, keepdims=True) + dd_ * x, dnum, f["qC"], dden, f["qn"])
            Gm = _each(lambda a, b: a * b, dA, Am)
            gam = _each(lambda a, b: a * b, dg, g)
            dk = _each(lambda a, w, e: (a + w * e) * kscale, dk1, ws, E)
            om = _each(lambda e, k, w: jnp.sum(e * k, axis=1, keepdims=True) * w, E, Kk, ws)
            dv = _each(lambda a, b: _bf(a + b), dv1, dv2)
            ddecay = _each(lambda d, ch, dn_, n: jnp.sum(jnp.sum(d * ch, axis=1, keepdims=True), axis=0, keepdims=True)
                           + jnp.sum(dn_ * n, axis=1, keepdims=True), dCn, C, dnn, N)
            rows_g = [jnp.sum(t, axis=1, keepdims=True) for t in Gm]
            cols_g = [jnp.broadcast_to(jnp.sum(t, axis=0, keepdims=True), (LC, 128)).T for t in Gm]
            last = _each(lambda o, dd_, d: jnp.where(rowi == LC - 1, jnp.sum(o, axis=0, keepdims=True) + dd_ * d, 0.0),
                         om, ddecay, decay)
            db = _each(lambda a, b, o, l, cg: a + b - o + l - cg, rows_g, gam, om, last, cols_g)
            di = _each(lambda cg, o: cg + o, cols_g, om)
            dB = jnp.zeros((LC, 128), F32)
            dI = jnp.zeros((LC, 128), F32)
            for h, ln in enumerate(lns):
                dB = jnp.where(lane128 == 4 + h, db[h], dB)
                dI = jnp.where(lane128 == h, di[h], dI)
                dgn_ref[:, ln] = dgn_ref[:, ln] + dgn[h]
                dCst[:, ln] = dC[h]
                dnst[0:1, ln] = dn[h]
                dqk[rows, ln] = dq[h]
                dqk[rows, MW + h * 128:MW + (h + 1) * 128] = dk[h]
                dm_ref[rows, DM_O + h * 128:DM_O + (h + 1) * 128] = dmo[h]
                dm_ref[rows, DM_V + h * 128:DM_V + (h + 1) * 128] = dv[h]
            dlogf = jnp.dot(triu, dB, preferred_element_type=F32, precision=HI)
            dG = dI + dlogf * _sigmoid(-G)
            dG = jnp.where(lane128 < 8, dG, 0.0)
            dm_ref[rows, DM_G:DM_G + 128] = _bf(dG)
            dm_ref[rows, DM_G + 128:DM_W] = jnp.zeros((LC, DM_W - DM_G - 128), BF16)
            dgb_ref[...] = dgb_ref[...] + jnp.sum(dG, axis=0, keepdims=True)

        dy = dqk[...] * (sg * (1.0 + c * (1.0 - sg)))
        db_ref[...] = db_ref[...] + jnp.sum(dy, axis=0, keepdims=True)
        for j in range(4):
            dw_ref[j:j + 1, :] = dw_ref[j:j + 1, :] + jnp.sum(dy * taps[j], axis=0, keepdims=True)
        dyp = jnp.concatenate([dy, dyhead[...]], axis=0)
        dx = w_ref[3:4, :] * dy
        for j in range(3):
            dx = dx + w_ref[j:j + 1, :] * pltpu.roll(dyp, TB + 8 - (3 - j), 0)[0:TB]
        dm_ref[:, 0:DM_V] = _bf(dx)
        dyhead[...] = dy[0:8, :]
        pl.when(i == nblk - 1)(rs_finish)

    rrow = lambda wd: pl.BlockSpec((TB, wd), lambda i: (nblk - 1 - i, 0))
    st = lambda r: pl.BlockSpec((ncb, r, MW), lambda i: (nblk - 1 - i, 0, 0))
    prev8 = pl.BlockSpec((8, 1024), lambda i: (jnp.maximum((nblk - 1 - i) * (TB // 8) - 1, 0), 0))
    res = pl.pallas_call(
        body, name="mlstm_bwd", grid=(nblk,),
        in_specs=[rrow(1024), prev8, rrow(MW), rrow(MW), rrow(128), _cspec((4, 1024)), _cspec((1, 1024)),
                  _cspec((1, 128)), _cspec((1, MW)), st(128), st(8), st(8), rrow(MW)] + [ANY] * nw,
        out_specs=[rrow(DM_W),
                   pl.BlockSpec((4, 1024), lambda i: (0, 0)), pl.BlockSpec((1, 1024), lambda i: (0, 0)),
                   pl.BlockSpec((1, MW), lambda i: (0, 0)), pl.BlockSpec((1, 128), lambda i: (0, 0))] + [ANY] * nw,
        out_shape=[jax.ShapeDtypeStruct((S, DM_W), BF16),
                   jax.ShapeDtypeStruct((4, 1024), F32), jax.ShapeDtypeStruct((1, 1024), F32),
                   jax.ShapeDtypeStruct((1, MW), F32), jax.ShapeDtypeStruct((1, 128), F32)]
        + [jax.ShapeDtypeStruct(a.shape, a.dtype) for a in parts],
        scratch_shapes=[pltpu.VMEM((128, MW), F32), pltpu.VMEM((8, MW), F32), pltpu.VMEM((8, 1024), F32),
                        pltpu.VMEM((TB, MW), F32), pltpu.VMEM((TB, MW), F32), pltpu.VMEM((TB, 1024), F32)]
        + _scatter_scratch(nw),
        compiler_params=_params(1),
    )(mqk, mqk, mv, mo, gates, conv_w, conv_b, gate_b, gn, cs, ns, ms, dout, *parts)
    return res[:5], res[5:]


def _out_proj(x, attn, ml, w, g):
    tm = TM

    def body(x_ref, a_ref, m_ref, w_ref, g_ref, h_ref, u_ref):
        h1 = x_ref[...] + _dot(_bf(a_ref[...]), w_ref[0:AW, :]) + _dot(_bf(m_ref[...]), w_ref[AW:D, :])
        h_ref[...] = h1
        n, _ = _rms(h1)
        u_ref[...] = _bf(n * g_ref[...])

    row = lambda wd: pl.BlockSpec((tm, wd), lambda i: (i, 0))
    return pl.pallas_call(
        body, name="out_proj", grid=(S // tm,),
        in_specs=[row(D), row(AW), row(MW), _cspec((D, D)), _cspec((1, D))],
        out_specs=[row(D), row(D)],
        out_shape=[jax.ShapeDtypeStruct((S, D), F32), jax.ShapeDtypeStruct((S, D), BF16)],
        compiler_params=_params(1),
    )(x, attn, ml, w, g)


HALF = DFF // NDEV // 2


def _mlp_fwd(h1, u2, w_up, w_down_a, w_down_b, shards, dtypes):
    tm = TM
    nt = S // tm
    nw = len(shards)

    def body(*refs):
        h_ref, u_ref, wu_ref, wa_ref, wb_ref = refs[:5]
        ins = refs[5:5 + nw]
        a_ref, o_ref = refs[5 + nw:7 + nw]
        outs = refs[7 + nw:7 + 2 * nw]
        bufs = refs[7 + 2 * nw:7 + 3 * nw]
        ag_start, ag_forward, ag_finish = _gather_phases(ins, outs, bufs, *refs[7 + 3 * nw:])
        i = pl.program_id(0)
        pl.when(i == 0)(ag_start)
        pl.when(i == nt - 2)(ag_forward)
        u = u_ref[...]
        acc = h_ref[...]
        for c in range(NDEV):
            cols = slice(c * 512, (c + 1) * 512)
            a = _dot(u, wu_ref[c])
            a_ref[:, cols] = _bf(a)
            r = jnp.maximum(a, 0.0)
            r = _bf(r * r)
            acc = acc + _dot(r[:, 0:HALF], wa_ref[c]) + _dot(r[:, HALF:2 * HALF], wb_ref[c])
        o_ref[...] = acc
        pl.when(i == nt - 1)(ag_finish)

    row = lambda wd: pl.BlockSpec((tm, wd), lambda i: (i, 0))
    res = pl.pallas_call(
        body, name="mlp_fwd", grid=(nt,),
        in_specs=[row(D), row(D), _cspec((NDEV, D, DFF // NDEV)), _cspec((NDEV, HALF, D)), _cspec((NDEV, HALF, D))]
        + [VM] * nw,
        out_specs=[row(DFF), row(D)] + [ANY] * nw,
        out_shape=[jax.ShapeDtypeStruct((S, DFF), BF16), jax.ShapeDtypeStruct((S, D), F32)]
        + _gather_shapes(shards, dtypes),
        scratch_shapes=_gather_scratch(shards, dtypes),
        compiler_params=_params(1),
    )(h1, u2, w_up, w_down_a, w_down_b, *shards)
    return res[0], res[1], res[2:]


def _ple_loss(h2, p, target, w_pg, w_ple, g_ple, g_fin):
    tm = TM

    def body(h_ref, p_ref, t_ref, wg_ref, wp_ref, gp_ref, gf_ref,
             dh_ref, dwg_ref, dwp_ref, dgp_ref, dgf_ref, loss_ref, acc_g, acc_p):
        i = pl.program_id(0)

        @pl.when(i == 0)
        def _():
            acc_g[...] = jnp.zeros_like(acc_g)
            acc_p[...] = jnp.zeros_like(acc_p)
            dgp_ref[...] = jnp.zeros_like(dgp_ref)
            dgf_ref[...] = jnp.zeros_like(dgf_ref)
            loss_ref[...] = jnp.zeros_like(loss_ref)

        h2v = h_ref[...]
        n2, rs2 = _rms(h2v)
        u3 = _bf(n2 * gp_ref[...])
        gt = _sigmoid(_dot(u3, wg_ref[...]))
        pb = _bf(p_ref[...])
        e = jnp.concatenate([_dot(pb, wp_ref[j]) for j in range(NDEV)], axis=1)
        h3 = h2v + gt * e
        n3, rs3 = _rms(h3)
        err = n3 * gf_ref[...] - t_ref[...]
        loss_ref[...] = loss_ref[...] + 0.5 / D * jnp.sum(jnp.sum(err * err, axis=1, keepdims=True), axis=0, keepdims=True)
        dy = err * (1.0 / D)
        dgf_ref[...] = dgf_ref[...] + jnp.sum(dy * n3, axis=0, keepdims=True)
        dh3 = _rms_bwd(dy, n3, rs3, gf_ref[...])
        de = _bf(dh3 * gt)
        dz = _bf(dh3 * e * gt * (1.0 - gt))
        acc_p[...] = acc_p[...] + _dot_tn(pb, de)
        acc_g[...] = acc_g[...] + _dot_tn(u3, dz)
        du3 = _dot_nt(dz, wg_ref[...])
        dgp_ref[...] = dgp_ref[...] + jnp.sum(du3 * n2, axis=0, keepdims=True)
        dh_ref[...] = dh3 + _rms_bwd(du3, n2, rs2, gp_ref[...])

        @pl.when(i == S // tm - 1)
        def _():
            dwg_ref[...] = _bf(acc_g[...])
            for j in range(NDEV):
                dwp_ref[j] = _bf(acc_p[:, j * 128:(j + 1) * 128])

    row = lambda wd: pl.BlockSpec((tm, wd), lambda i: (i, 0))
    whole = lambda shp: pl.BlockSpec(shp, lambda i: (0,) * len(shp))
    return pl.pallas_call(
        body, name="ple_loss", grid=(S // tm,),
        in_specs=[row(D), row(PLE), row(D), _cspec((D, D)), _cspec((NDEV, PLE, 128)), _cspec((1, D)), _cspec((1, D))],
        out_specs=[row(D), whole((D, D)), whole((NDEV, PLE, 128)), whole((1, D)), whole((1, D)), whole((1, 1))],
        out_shape=[jax.ShapeDtypeStruct((S, D), F32), jax.ShapeDtypeStruct((D, D), BF16),
                   jax.ShapeDtypeStruct((NDEV, PLE, 128), BF16), jax.ShapeDtypeStruct((1, D), F32),
                   jax.ShapeDtypeStruct((1, D), F32), jax.ShapeDtypeStruct((1, 1), F32)],
        scratch_shapes=[pltpu.VMEM((D, D), F32), pltpu.VMEM((PLE, D), F32)],
        compiler_params=_params(1),
    )(h2, p, target, w_pg, w_ple, g_ple, g_fin)


def _mlp_bwd(dh2, a, h1, g, w_up, w_down_a, w_down_b, parts):
    tm = TM
    nt = S // tm
    nw = len(parts)

    def body(*refs):
        d_ref, a_ref, h_ref, g_ref, wu_ref, wa_ref, wb_ref = refs[:7]
        ins = refs[7:7 + nw]
        da_ref, dh1_ref, dg_ref = refs[7 + nw:10 + nw]
        outs = refs[10 + nw:10 + 2 * nw]
        rs_start, rs_finish = _scatter_phases(ins, outs, *refs[10 + 2 * nw:])
        i = pl.program_id(0)
        pl.when(i == 0)(rs_start)

        @pl.when(i == 0)
        def _():
            dg_ref[...] = jnp.zeros_like(dg_ref)

        dh2v = d_ref[...]
        db = _bf(dh2v)
        du = jnp.zeros((tm, D), F32)
        for c in range(NDEV):
            cols = slice(c * 512, (c + 1) * 512)
            dr = jnp.concatenate([_dot_nt(db, wa_ref[c]), _dot_nt(db, wb_ref[c])], axis=1)
            da = _bf(dr * (2.0 * jnp.maximum(a_ref[:, cols], 0.0)))
            da_ref[:, cols] = da
            du = du + _dot_nt(da, wu_ref[c])
        n, rs = _rms(h_ref[...])
        dg_ref[...] = dg_ref[...] + jnp.sum(du * n, axis=0, keepdims=True)
        dh1_ref[...] = dh2v + _rms_bwd(du, n, rs, g_ref[...])
        pl.when(i == nt - 1)(rs_finish)

    row = lambda wd: pl.BlockSpec((tm, wd), lambda i: (i, 0))
    res = pl.pallas_call(
        body, name="mlp_bwd", grid=(nt,),
        in_specs=[row(D), row(DFF), row(D), _cspec((1, D)), _cspec((NDEV, D, DFF // NDEV)), _cspec((NDEV, HALF, D)),
                  _cspec((NDEV, HALF, D))] + [ANY] * nw,
        out_specs=[row(DFF), row(D), pl.BlockSpec((1, D), lambda i: (0, 0))] + [ANY] * nw,
        out_shape=[jax.ShapeDtypeStruct((S, DFF), BF16), jax.ShapeDtypeStruct((S, D), F32),
                   jax.ShapeDtypeStruct((1, D), F32)] + [jax.ShapeDtypeStruct(p.shape, p.dtype) for p in parts],
        scratch_shapes=_scatter_scratch(nw),
        compiler_params=_params(1),
    )(dh2, a, h1, g, w_up, w_down_a, w_down_b, *parts)
    return res[0], res[1], res[2], res[3:]


def _out_proj_bwd(dh1, attn, ml, w):
    tm = TM

    def body(d_ref, a_ref, m_ref, w_ref, da_ref, dm_ref, dw_ref, acc):
        i = pl.program_id(0)

        @pl.when(i == 0)
        def _():
            acc[...] = jnp.zeros_like(acc)

        db = _bf(d_ref[...])
        dmix = _dot_nt(db, w_ref[...])
        da_ref[...] = dmix[:, 0:AW]
        dm_ref[...] = dmix[:, AW:D]
        acc[0:AW, :] = acc[0:AW, :] + _dot_tn(_bf(a_ref[...]), db)
        acc[AW:D, :] = acc[AW:D, :] + _dot_tn(_bf(m_ref[...]), db)

        @pl.when(i == S // tm - 1)
        def _():
            dw_ref[...] = _bf(acc[...])

    row = lambda wd: pl.BlockSpec((tm, wd), lambda i: (i, 0))
    return pl.pallas_call(
        body, name="out_proj_bwd", grid=(S // tm,),
        in_specs=[row(D), row(AW), row(MW), _cspec((D, D))],
        out_specs=[row(AW), row(MW), pl.BlockSpec((D, D), lambda i: (0, 0))],
        out_shape=[jax.ShapeDtypeStruct((S, AW), F32), jax.ShapeDtypeStruct((S, MW), F32),
                   jax.ShapeDtypeStruct((D, D), BF16)],
        scratch_shapes=[pltpu.VMEM((D, D), F32)],
        compiler_params=_params(1),
    )(dh1, attn, ml, w)


CHIP_FLIPS = [(0, 0), (0, 1), (1, 0), (1, 1)]


def _scatter2_phases(in_ref, out_ref, mine_v, sib_v, psum_v, loc_sems, d2d_send, d2d_recv, ici_send, ici_recv, own_sem):
    x, y, c = _place()
    chips = [((x + dx) % 2, (y + dy) % 2) for dx, dy in CHIP_FLIPS]
    nc = len(chips)

    def local(k):
        return pltpu.make_async_copy(in_ref.at[_dev_index(*chips[k], c)], mine_v.at[k], loc_sems.at[k])

    def to_sib(k):
        return pltpu.make_async_remote_copy(
            src_ref=in_ref.at[_dev_index(*chips[k], 1 - c)], dst_ref=sib_v.at[k], send_sem=d2d_send.at[k],
            recv_sem=d2d_recv.at[k], device_id=(x, y, 1 - c), device_id_type=MESH)

    def over_ici(k):
        return pltpu.make_async_remote_copy(
            src_ref=psum_v.at[k], dst_ref=out_ref.at[k], send_sem=ici_send.at[k - 1], recv_sem=ici_recv.at[k - 1],
            device_id=(*chips[k], c), device_id_type=MESH)

    def own():
        return pltpu.make_async_copy(psum_v.at[0], out_ref.at[0], own_sem)

    def start():
        for k in range(nc):
            to_sib(k).start()
            local(k).start()

    def middle():
        for k in (1, 2, 3, 0):
            local(k).wait()
            to_sib(k).wait_recv()
            psum_v[k] = _bf(mine_v[k].astype(F32) + sib_v[k].astype(F32))
            (over_ici(k) if k else own()).start()

    def finish():
        for k in range(1, nc):
            over_ici(k).wait()
        for k in range(nc):
            to_sib(k).wait_send()
        own().wait()

    return start, middle, finish


def _scatter2_scratch(shard, dtype):
    nc = len(CHIP_FLIPS)
    return ([pltpu.VMEM((nc, *shard), dtype)] * 3
            + [pltpu.SemaphoreType.DMA((nc,))] * 3 + [pltpu.SemaphoreType.DMA((nc - 1,))] * 2 + [pltpu.SemaphoreType.DMA])


def _in_proj_bwd(dparts, n_roped, rope, dh1, x, g1, w, part):
    tm = TM
    nt = S // tm
    widths = [d.shape[1] for d in dparts]
    assert sum(widths) == PW
    npar = len(dparts)

    def body(*refs):
        d_refs = refs[:npar]
        tabs = [t[...] for t in refs[npar:npar + 3]]
        dh_ref, x_ref, g_ref, w_ref, in_ref, dx_ref, dgsum_ref, out_ref = refs[npar + 3:npar + 11]
        rs_start, rs_middle, rs_finish = _scatter2_phases(in_ref, out_ref, *refs[npar + 11:npar + 20])
        dg_ref = refs[npar + 20]
        ar_start, ar_finish = _small_phases([dg_ref], dgsum_ref, *refs[npar + 21:])
        i = pl.program_id(0)
        pl.when(i == 0)(rs_start)
        pl.when(i == 1)(rs_middle)

        @pl.when(i == 0)
        def _():
            dg_ref[...] = jnp.zeros_like(dg_ref)

        du = jnp.zeros((tm, D), F32)
        off = 0
        for j, (d_ref, wd) in enumerate(zip(d_refs, widths)):
            nc = next(c for c in (768, 512) if wd % c == 0)
            for s in range(wd // nc):
                d = d_ref[:, s * nc:(s + 1) * nc]
                du = du + _dot_nt(_unrope(d, *tabs) if j < n_roped else d, w_ref[:, off + s * nc:off + (s + 1) * nc])
            off += wd
        n, rs = _rms(x_ref[...])
        dg_ref[...] = dg_ref[...] + jnp.sum(du * n, axis=0, keepdims=True)
        dx_ref[...] = dh_ref[...] + _rms_bwd(du, n, rs, g_ref[...])

        @pl.when(i == nt - 1)
        def _():
            ar_start()
            rs_finish()
            ar_finish()

    row = lambda wd: pl.BlockSpec((tm, wd), lambda i: (i, 0))
    shard = part.shape[1:]
    return pl.pallas_call(
        body, name="in_proj_bwd", grid=(nt,),
        in_specs=[row(wd) for wd in widths] + [row(128)] * 3 + [row(D), row(D), _cspec((1, D)), _cspec((D, PW)), ANY],
        out_specs=[row(D), VM, ANY],
        out_shape=[jax.ShapeDtypeStruct((S, D), F32), jax.ShapeDtypeStruct((8, 1024), F32),
                   jax.ShapeDtypeStruct((len(CHIP_FLIPS), *shard), part.dtype)],
        scratch_shapes=_scatter2_scratch(shard, part.dtype)
        + [pltpu.VMEM((1, D), F32), pltpu.VMEM((8, 1024), F32), pltpu.VMEM((NDEV, 8, 1024), F32),
           pltpu.SemaphoreType.DMA((7,)), pltpu.SemaphoreType.DMA((7,))],
        compiler_params=_params(1),
    )(*dparts, *rope, dh1, x, g1, w, part)


SMALL_ROWS = 96


def _small_phases(ins, out_ref, pack, rbuf, send_sems, recv_sems):
    x, y, c = _place()
    me = _dev_index(x, y, c)

    def copies():
        out = []
        for k, (dx, dy, dc) in enumerate(FLIPS):
            peer = ((x + dx) % 2, (y + dy) % 2, (c + dc) % 2)
            out.append(pltpu.make_async_remote_copy(
                src_ref=pack, dst_ref=rbuf.at[me], send_sem=send_sems.at[k], recv_sem=recv_sems.at[k],
                device_id=peer, device_id_type=MESH))
        return out

    def start():
        pack[...] = jnp.zeros_like(pack)
        for i, ref in enumerate(ins):
            pack[8 * i:8 * i + 1, 0:ref.shape[1]] = ref[...]
        rbuf[me] = pack[...]
        for cp in copies():
            cp.start()

    def finish():
        for cp in copies():
            cp.wait()
        tot = rbuf[0]
        for j in range(1, NDEV):
            tot = tot + rbuf[j]
        out_ref[...] = tot

    return start, finish


def _wgrad(name, A, Bs, a_fn, b_fn, out_shape, split=None, ts=512, small=(), rope=(), n_roped=0):
    K = A.shape[1]
    widths = [b.shape[1] for b in Bs]
    N = sum(widths)
    nb, ns, nrt = len(Bs) + len(rope), len(small), S // ts
    kc = min(K, 1024)

    def body(*refs):
        a_ref, b_refs = refs[0], refs[1:1 + len(Bs)]
        tabs = [t[...] for t in refs[1 + len(Bs):1 + nb]]
        o_ref = refs[1 + nb + ns]
        acc = refs[2 + nb + ns + bool(ns)]
        r = pl.program_id(0)
        if ns:
            sm_start, sm_finish = _small_phases(refs[1 + nb:1 + nb + ns], refs[2 + nb + ns], *refs[4 + nb + ns:])
            pl.when(r == 0)(sm_start)

        @pl.when(r == 0)
        def _():
            acc[...] = jnp.zeros_like(acc)

        bs, off = [], 0
        for i, (b_ref, w) in enumerate(zip(b_refs, widths)):
            nc = next(c for c in (1024, 768, 512) if w % c == 0)
            fn = (lambda t: _unrope(t, *tabs)) if i < n_roped else b_fn
            bs += [(off + c * nc, nc, fn(b_ref[:, c * nc:(c + 1) * nc])) for c in range(w // nc)]
            off += w
        for kk in range(K // kc):
            rows = slice(kk * kc, (kk + 1) * kc)
            at = a_fn(a_ref[:, rows]).T
            for lo, nc, b in bs:
                acc[rows, lo:lo + nc] = acc[rows, lo:lo + nc] + _dot(at, b)

        @pl.when(r == nrt - 1)
        def _():
            if split is None:
                o_ref[...] = _bf(acc[...])
            else:
                for j in range(NDEV):
                    o_ref[j] = _bf(acc[:, split * j:split * (j + 1)])

        if ns:
            pl.when(r == nrt - 1)(sm_finish)

    in_specs = ([pl.BlockSpec((ts, K), lambda r: (r, 0))] + [pl.BlockSpec((ts, w), lambda r: (r, 0)) for w in widths]
                + [pl.BlockSpec((ts, 128), lambda r: (r, 0))] * len(rope))
    out_spec = pl.BlockSpec(out_shape, lambda r: (0,) * len(out_shape))
    scratch = [pltpu.VMEM((K, N), F32)]
    if not ns:
        return pl.pallas_call(
            body, name=name, grid=(nrt,), in_specs=in_specs, out_specs=out_spec,
            out_shape=jax.ShapeDtypeStruct(out_shape, BF16), scratch_shapes=scratch, compiler_params=_params(1),
        )(A, *Bs, *rope)
    return pl.pallas_call(
        body, name=name, grid=(nrt,), in_specs=in_specs + [VM] * ns, out_specs=[out_spec, VM],
        out_shape=[jax.ShapeDtypeStruct(out_shape, BF16), jax.ShapeDtypeStruct((SMALL_ROWS, 1024), F32)],
        scratch_shapes=scratch + [pltpu.VMEM((SMALL_ROWS, 1024), F32), pltpu.VMEM((NDEV, SMALL_ROWS, 1024), F32),
                                  pltpu.SemaphoreType.DMA((7,)), pltpu.SemaphoreType.DMA((7,))],
        compiler_params=_params(1),
    )(A, *Bs, *rope, *small)


def _relu2_bf(a):
    r = jnp.maximum(a.astype(F32), 0.0)
    return _bf(r * r)


def _ident(a):
    return a


def _step(x, p, target, g1, conv_b, gate_b, gn, g_mlp, g_ple, g_fin, sh):
    (g_in, g_conv), (rc, ra, rb) = _gather_weights([sh["w_in"], sh["conv_w"]], [BF16, F32])
    conv_w = g_conv.transpose(1, 0, 2).reshape(4, 1024)
    w_in_p = _join_w_in(g_in)
    (qkv, mqk, mv, mo, gates, u1), (w_down_a,) = _in_proj(x, g1, w_in_p, rc, ra, rb, [sh["w_down"][0:HALF]], [BF16])
    attn, lse, (w_up8, w_out8) = _attn_fwd(qkv, [sh["w_up"], sh["w_out"]], [BF16] * 2)
    ml, cs, ns, ms, (w_down_b,) = _mlstm_fwd(mqk, mv, mo, gates, conv_w, conv_b, gate_b, gn,
                                             [sh["w_down"][HALF:2 * HALF]], [BF16])
    w_out = w_out8.reshape(D, D)
    h1, u2 = _out_proj(x, attn, ml, w_out, g_mlp)
    a, h2, (w_pg8, w_ple8) = _mlp_fwd(h1, u2, w_up8, w_down_a, w_down_b, [sh["w_ple_gate"], sh["w_ple"]], [BF16] * 2)
    w_pg = w_pg8.reshape(D, D)
    dh2, dw_pg, dw_ple8, dg_ple, dg_fin, loss = _ple_loss(h2, p, target, w_pg, w_ple8, g_ple, g_fin)
    da, dh1, dg_mlp, (r_pg, r_ple) = _mlp_bwd(dh2, a, h1, g_mlp, w_up8, w_down_a, w_down_b,
                                              [dw_pg.reshape(NDEV, D // NDEV, D), dw_ple8])
    dw_up8 = _wgrad("wgrad_up", u2, [da], _ident, _ident, (NDEV, D, DFF // NDEV), split=DFF // NDEV)
    dw_down = _wgrad("wgrad_down", a, [dh2], _relu2_bf, _bf, (DFF, D))
    d_attn, d_ml, dw_out = _out_proj_bwd(dh1, attn, ml, w_out)
    (dm, dconv_w, dconv_b, dgn, dgate_b), (r_down,) = _mlstm_bwd(
        mqk, mv, mo, gates, conv_w, conv_b, gate_b, gn, cs, ns, ms, d_ml, [dw_down.reshape(NDEV, DFF // NDEV, D)])
    dq, dk, dv, (r_up, r_out) = _attn_bwd(qkv, attn, lse, d_attn, [dw_up8, dw_out.reshape(NDEV, D // NDEV, D)])
    dparts = [dq, dk, dv, dm]
    small = [jnp.zeros((1, D), F32), dconv_b, dgate_b, dgn, dg_mlp, dg_ple, dg_fin, loss]
    dw_in8, total = _wgrad("wgrad_in", u1, dparts, _ident, _ident, (NDEV, D, IN_W // NDEV), split=IN_W // NDEV,
                           small=small + [dconv_w[j:j + 1] for j in range(4)], rope=(rc, ra, rb), n_roped=2)
    dx, dg1_sum, r_in = _in_proj_bwd(dparts, 2, (rc, ra, rb), dh1, x, g1, w_in_p, dw_in8)
    recv = dict(w_in=r_in, w_out=r_out, w_up=r_up, w_down=r_down, w_ple_gate=r_pg, w_ple=r_ple)
    return dx, recv, total, dg1_sum


def _gather_weights(shards, dtypes):
    nw = len(shards)

    def body(*refs):
        ins, parts = refs[:nw], refs[nw:nw + 4]
        outs, tables = refs[nw + 4:2 * nw + 4], refs[2 * nw + 4:2 * nw + 7]
        start, forward, finish = _gather_phases(ins, outs, refs[2 * nw + 7:3 * nw + 7], *refs[3 * nw + 7:])
        start()
        _rope_fill(*parts, *tables)
        forward()
        finish()

    res = pl.pallas_call(
        body, name="gather_weights",
        in_specs=[VM] * (nw + 4), out_specs=[ANY] * nw + [VM] * 3,
        out_shape=_gather_shapes(shards, dtypes) + [jax.ShapeDtypeStruct((S, 128), F32)] * 3,
        scratch_shapes=_gather_scratch(shards, dtypes),
        compiler_params=_params(),
    )(*shards, *_rope_parts())
    return res[:nw], res[nw:]


ADAM_STEPS = 4


def _adamw(items):
    n = len(items)

    def body(*refs):
        for i in range(n):
            g_ref, w_ref, m_ref, v_ref = refs[4 * i:4 * i + 4]
            go_ref, d_ref, mo_ref, vo_ref = refs[4 * n + 4 * i:4 * n + 4 * i + 4]
            g = g_ref[0].astype(F32)
            for j in range(1, g_ref.shape[0]):
                g = g + g_ref[j].astype(F32)
            go_ref[...] = g
            d_ref[...], mo_ref[...], vo_ref[...] = _adam_update(g, w_ref[...], m_ref[...], v_ref[...])

    in_specs, out_specs, out_shape, args = [], [], [], []
    for gparts, w, m, v in items:
        P, R, C = gparts.shape
        if R % (8 * ADAM_STEPS) == 0:
            tr = R // ADAM_STEPS
            row, gspec = pl.BlockSpec((tr, C), lambda i: (i, 0)), pl.BlockSpec((P, tr, C), lambda i: (0, i, 0))
        else:
            row, gspec = pl.BlockSpec((R, C), lambda i: (0, 0)), pl.BlockSpec((P, R, C), lambda i: (0, 0, 0))
        in_specs += [gspec, row, row, row]
        out_specs += [row] * 4
        out_shape += [jax.ShapeDtypeStruct((R, C), F32)] * 4
        args += [gparts, w, m, v]
    res = pl.pallas_call(
        body, name="adamw", grid=(ADAM_STEPS,), in_specs=in_specs, out_specs=out_specs, out_shape=out_shape,
        compiler_params=_params(1),
    )(*args)
    return [res[4 * i:4 * i + 4] for i in range(n)]


SMALL = ("norm_mix_g", "conv_b", "gate_b", "mlstm_norm_g", "norm_mlp_g", "norm_ple_g", "final_norm_g")


def _adam_update(g, w, m, v):
    c1 = 1.0 - ADAM_B1 ** ADAM_STEP
    c2 = 1.0 - ADAM_B2 ** ADAM_STEP
    m2 = ADAM_B1 * m + (1.0 - ADAM_B1) * g
    v2 = ADAM_B2 * v + (1.0 - ADAM_B2) * (g * g)
    return -ADAM_LR * ((m2 / c1) / (jnp.sqrt(v2 / c2) + ADAM_EPS) + ADAM_WD * w), m2, v2


def _adamw_small(total, first, ws, ms, vs):
    n = len(ws)

    def body(*refs):
        t_ref, f_ref = refs[:2]
        refs = refs[1:]
        outs = refs[1 + 3 * n:]
        for i in range(n):
            w_ref, m_ref, v_ref = refs[1 + i], refs[1 + n + i], refs[1 + 2 * n + i]
            g = (t_ref if i else f_ref)[8 * i:8 * i + 1, 0:w_ref.shape[1]]
            delta, m2, v2 = _adam_update(g, w_ref[...], m_ref[...], v_ref[...])
            for ref, val in zip(outs[4 * i:4 * i + 4], (g, delta, m2, v2)):
                ref[...] = val

    res = pl.pallas_call(
        body, name="adamw_small",
        out_shape=[jax.ShapeDtypeStruct(w.shape, F32) for w in ws for _ in range(4)],
        compiler_params=_params(),
    )(total, first, *ws, *ms, *vs)
    return [res[4 * i:4 * i + 4] for i in range(n)]


def kernel(x, p, norm_mix_g, w_in, conv_w, conv_b, gate_b, mlstm_norm_g, w_out, norm_mlp_g, w_up, w_down, norm_ple_g, w_ple_gate, w_ple, final_norm_g, loss_target, m_norm_mix_g, m_w_in, m_conv_w, m_conv_b, m_gate_b, m_mlstm_norm_g, m_w_out, m_norm_mlp_g, m_w_up, m_w_down, m_norm_ple_g, m_w_ple_gate, m_w_ple, m_final_norm_g, v_norm_mix_g, v_w_in, v_conv_w, v_conv_b, v_gate_b, v_mlstm_norm_g, v_w_out, v_norm_mlp_g, v_w_up, v_w_down, v_norm_ple_g, v_w_ple_gate, v_w_ple, v_final_norm_g):
    big_names = ("w_in", "conv_w", "w_out", "w_up", "w_down", "w_ple_gate", "w_ple")
    wts = dict(w_in=w_in, conv_w=conv_w, w_out=w_out, w_up=w_up, w_down=w_down, w_ple_gate=w_ple_gate, w_ple=w_ple)
    mom = dict(w_in=m_w_in, conv_w=m_conv_w, w_out=m_w_out, w_up=m_w_up, w_down=m_w_down, w_ple_gate=m_w_ple_gate,
               w_ple=m_w_ple)
    var = dict(w_in=v_w_in, conv_w=v_conv_w, w_out=v_w_out, w_up=v_w_up, w_down=v_w_down, w_ple_gate=v_w_ple_gate,
               w_ple=v_w_ple)
    sq = lambda a: a.reshape(a.shape[1:])
    fin = final_norm_g.reshape(1, D)
    dx, recv, total, first = _step(
        x[0], p[0, 0], loss_target[0], norm_mix_g, conv_b, jnp.pad(gate_b, ((0, 0), (0, 120))), mlstm_norm_g,
        norm_mlp_g, norm_ple_g, fin, {n: sq(wts[n]) for n in big_names})

    nrow = 8 * len(SMALL)
    me = _dev_index(*_place())
    conv_rows = total[nrow + 8:nrow + 40:8]
    recv["conv_w"] = lax.dynamic_slice_in_dim(conv_rows, me * 128, 128, axis=1).reshape(1, 4, 128)
    out = {}
    for n, res in zip(big_names, _adamw([(recv[n], sq(wts[n]), sq(mom[n]), sq(var[n])) for n in big_names])):
        out[n] = [t.reshape(wts[n].shape) for t in res]
    sw = dict(norm_mix_g=norm_mix_g, conv_b=conv_b, gate_b=gate_b, mlstm_norm_g=mlstm_norm_g, norm_mlp_g=norm_mlp_g,
              norm_ple_g=norm_ple_g, final_norm_g=fin)
    sm = dict(norm_mix_g=m_norm_mix_g, conv_b=m_conv_b, gate_b=m_gate_b, mlstm_norm_g=m_mlstm_norm_g,
              norm_mlp_g=m_norm_mlp_g, norm_ple_g=m_norm_ple_g, final_norm_g=m_final_norm_g.reshape(1, D))
    sv = dict(norm_mix_g=v_norm_mix_g, conv_b=v_conv_b, gate_b=v_gate_b, mlstm_norm_g=v_mlstm_norm_g,
              norm_mlp_g=v_norm_mlp_g, norm_ple_g=v_norm_ple_g, final_norm_g=v_final_norm_g.reshape(1, D))
    res = _adamw_small(total, first, [sw[n] for n in SMALL], [sm[n] for n in SMALL], [sv[n] for n in SMALL])
    for n, r in zip(SMALL, res):
        out[n] = [t.reshape(final_norm_g.shape) for t in r] if n == "final_norm_g" else list(r)
    order = ("norm_mix_g", "w_in", "conv_w", "conv_b", "gate_b", "mlstm_norm_g", "w_out", "norm_mlp_g", "w_up", "w_down",
             "norm_ple_g", "w_ple_gate", "w_ple", "final_norm_g")
    loss_all = total[nrow, 0]
    return (loss_all, dx[None], *[out[n][0] for n in order], *[out[n][1] for n in order],
            *[out[n][2] for n in order], *[out[n][3] for n in order])
```

```python
import math

import jax
import jax.numpy as jnp
from jax import lax
from jax.experimental import pallas as pl
from jax.experimental.pallas import tpu as pltpu

F32, BF16 = jnp.float32, jnp.bfloat16
S = 4096
D = 1024
AW = 512
MW = 512
DFF = 4096
PLE = 256
IN_W = 3592
PW = 3840
NDEV = 8
EPS = 1e-6
NEG = -1e30
LC = 128
TB = 256
ROPE_THETA = 500000.0
VMEM_LIMIT = 56 * 1024 * 1024
HI = lax.Precision.HIGHEST

ADAM_LR, ADAM_B1, ADAM_B2, ADAM_EPS, ADAM_WD, ADAM_STEP = 0.001, 0.9, 0.999, 1e-08, 0.01, 10


def _params(n_grid=0, **kw):
    sem = dict(dimension_semantics=("arbitrary",) * n_grid) if n_grid else {}
    return pltpu.CompilerParams(vmem_limit_bytes=VMEM_LIMIT, **sem, **kw)


def _cspec(shape):
    nd = len(shape)
    return pl.BlockSpec(shape, lambda *_: (0,) * nd, pipeline_mode=pl.Buffered(1))


def _dot(a, b):
    return jnp.dot(a, b, preferred_element_type=F32)


def _dot_nt(a, b):
    return lax.dot_general(a, b, (((1,), (1,)), ((), ())), preferred_element_type=F32)


def _dot_tn(a, b):
    return lax.dot_general(a, b, (((0,), (0,)), ((), ())), preferred_element_type=F32)


def _bf(x):
    return x.astype(BF16)


def _rms(x):
    rs = lax.rsqrt(jnp.mean(x * x, axis=-1, keepdims=True) + EPS)
    return x * rs, rs


def _rms_bwd(du, n, rs, g):
    dn = du * g
    return rs * (dn - n * jnp.mean(dn * n, axis=-1, keepdims=True))


def _sigmoid(x):
    return 1.0 / (1.0 + jnp.exp(-x))


ROPE_BLK = 512


def _rope_parts():
    def cs(n, step):
        j = lax.broadcasted_iota(jnp.int32, (n, 128), 1) % 64
        pos = (lax.broadcasted_iota(jnp.int32, (n, 128), 0) * step).astype(F32)
        ang = pos * jnp.power(ROPE_THETA, -(j % 8).astype(F32) / 8.0)
        return jnp.cos(ang), jnp.sin(ang)

    return (*cs(ROPE_BLK, 1), *cs(S // ROPE_BLK, ROPE_BLK))


def _rope_fill(co_ref, so_ref, cb_ref, sb_ref, rc_ref, ra_ref, rb_ref):
    j = lax.broadcasted_iota(jnp.int32, (ROPE_BLK, 128), 1) % 64
    co, so = co_ref[...], so_ref[...]
    for t in range(S // ROPE_BLK):
        cb, sb = cb_ref[t:t + 1, :], sb_ref[t:t + 1, :]
        cos, sin = cb * co - sb * so, sb * co + cb * so
        rows = slice(t * ROPE_BLK, (t + 1) * ROPE_BLK)
        rc_ref[rows, :] = jnp.where(j < 16, cos, 1.0)
        ra_ref[rows, :] = jnp.where(j < 8, -sin, 0.0)
        rb_ref[rows, :] = jnp.where((j >= 8) & (j < 16), sin, 0.0)


def _rope(blk, c, a, b):
    return blk * c + pltpu.roll(blk, 120, 1) * a + pltpu.roll(blk, 8, 1) * b


def _rope_bwd(d, c, a, b):
    return d * c + pltpu.roll(d * a, 8, 1) + pltpu.roll(d * b, 120, 1)


def _unrope(t, c, a, b):
    return jnp.concatenate([_bf(_rope_bwd(t[:, j * 128:(j + 1) * 128].astype(F32), c, a, b))
                            for j in range(t.shape[1] // 128)], axis=1)


MESH = pl.DeviceIdType.MESH
ANY = pl.BlockSpec(memory_space=pl.ANY)
VM = pl.BlockSpec(memory_space=pltpu.VMEM)
FLIPS = [(dx, dy, dc) for dx in (0, 1) for dy in (0, 1) for dc in (0, 1)][1:]


def _place():
    return lax.axis_index("x"), lax.axis_index("y"), lax.axis_index("c")


def _dev_index(px, py, pc):
    return 4 * px + 2 * py + pc


def _gather_phases(ins, outs, bufs, send_sems=None, recv_sems=None, local_sems=None):
    nw = len(ins)
    if nw == 0:
        return (lambda: None,) * 3
    x, y, c = _place()
    me, sib = (x, y, c), (x, y, 1 - c)
    chips = [(1 - x, y), (x, 1 - y), (1 - x, 1 - y)]

    def copy(w, k, block, to, from_buf=False):
        dst = outs[w].at[_dev_index(*block)]
        return pltpu.make_async_remote_copy(
            src_ref=bufs[w] if from_buf else dst, dst_ref=dst, send_sem=send_sems.at[w, k],
            recv_sem=recv_sems.at[w, k], device_id=to, device_id_type=MESH)

    def mine(w):
        return pltpu.make_async_copy(bufs[w], outs[w].at[_dev_index(*me)], local_sems.at[w])

    def first(w):
        return [copy(w, 0, me, sib, True)] + [copy(w, 1 + j, me, (*chip, c), True) for j, chip in enumerate(chips)]

    def passed(w):
        return [copy(w, 4 + j, (*chip, c), sib) for j, chip in enumerate(chips)]

    def start():
        for w in range(nw):
            bufs[w][...] = ins[w][...].astype(bufs[w].dtype)
        for w in range(nw):
            mine(w).start()
            for cp in first(w):
                cp.start()

    def forward():
        for j, chip in enumerate(chips):
            for w in range(nw):
                copy(w, 1 + j, (*chip, c), me).wait_recv()
                passed(w)[j].start()

    def finish():
        for w in range(nw):
            copy(w, 0, sib, me).wait_recv()
        for j, chip in enumerate(chips):
            for w in range(nw):
                copy(w, 4 + j, (*chip, 1 - c), me).wait_recv()
        for w in range(nw):
            for cp in first(w) + passed(w):
                cp.wait_send()
            mine(w).wait()

    return start, forward, finish


def _gather_scratch(shards, dtypes):
    nw = len(shards)
    if nw == 0:
        return []
    return ([pltpu.VMEM(s.shape, dt) for s, dt in zip(shards, dtypes)]
            + [pltpu.SemaphoreType.DMA((nw, 7)), pltpu.SemaphoreType.DMA((nw, 7)), pltpu.SemaphoreType.DMA((nw,))])


def _gather_shapes(shards, dtypes):
    return [jax.ShapeDtypeStruct((NDEV, *s.shape), dt) for s, dt in zip(shards, dtypes)]


def _scatter_phases(ins, outs, send_sems=None, recv_sems=None, local_sems=None):
    nw = len(ins)
    if nw == 0:
        return (lambda: None,) * 2
    x, y, c = _place()
    me = _dev_index(x, y, c)

    def copies():
        out = []
        for w in range(nw):
            out.append(pltpu.make_async_copy(ins[w].at[me], outs[w].at[me], local_sems.at[w]))
            for k, (dx, dy, dc) in enumerate(FLIPS):
                peer = ((x + dx) % 2, (y + dy) % 2, (c + dc) % 2)
                out.append(pltpu.make_async_remote_copy(
                    src_ref=ins[w].at[_dev_index(*peer)], dst_ref=outs[w].at[me], send_sem=send_sems.at[w, k],
                    recv_sem=recv_sems.at[w, k], device_id=peer, device_id_type=MESH))
        return out

    def start():
        for cp in copies():
            cp.start()

    def finish():
        for cp in copies():
            cp.wait()

    return start, finish


def _scatter_scratch(nw):
    if nw == 0:
        return []
    return [pltpu.SemaphoreType.DMA((nw, 7)), pltpu.SemaphoreType.DMA((nw, 7)), pltpu.SemaphoreType.DMA((nw,))]


TM = 512


def _join_w_in(wg):
    sw = IN_W // NDEV

    def body(wg_ref, w_ref):
        for j in range(NDEV):
            w_ref[:, sw * j:sw * (j + 1)] = wg_ref[j]
        w_ref[:, IN_W:PW] = jnp.zeros((D, PW - IN_W), BF16)

    return pl.pallas_call(body, name="join_w_in", out_shape=jax.ShapeDtypeStruct((D, PW), BF16),
                          compiler_params=_params())(wg)


def _in_proj(x, g1, w, rc, ra, rb, shards, dtypes):
    tm = TM
    nw = len(shards)
    nt = S // tm

    def body(*refs):
        x_ref, g_ref, w_ref, rc_ref, ra_ref, rb_ref = refs[:6]
        ins = refs[6:6 + nw]
        qkv_ref, mqk_ref, mv_ref, mo_ref, gt_ref, u_ref = refs[6 + nw:12 + nw]
        outs = refs[12 + nw:12 + 2 * nw]
        bufs = refs[12 + 2 * nw:12 + 3 * nw]
        ag_start, ag_forward, ag_finish = _gather_phases(ins, outs, bufs, *refs[12 + 3 * nw:])
        i = pl.program_id(0)
        pl.when(i == 0)(ag_start)
        pl.when(i == nt - 2)(ag_forward)
        n, _ = _rms(x_ref[...])
        u = _bf(n * g_ref[...])
        u_ref[...] = u
        c, a, b = rc_ref[...], ra_ref[...], rb_ref[...]
        for half in range(2):
            blk = _dot(u, w_ref[:, half * 512:(half + 1) * 512])
            for t in range(4):
                lo = half * 512 + t * 128
                qkv_ref[:, lo:lo + 128] = _rope(blk[:, t * 128:(t + 1) * 128], c, a, b)
        qkv_ref[:, 1024:1536] = _dot(u, w_ref[:, 1024:1536])
        mqk_ref[:, 0:512] = _dot(u, w_ref[:, 1536:2048])
        mqk_ref[:, 512:1024] = _dot(u, w_ref[:, 2048:2560])
        mv_ref[...] = _dot(u, w_ref[:, 2560:3072])
        mo_ref[...] = _dot(u, w_ref[:, 3072:3584])
        gt_ref[...] = _dot(u, w_ref[:, 3584:3712])
        pl.when(i == nt - 1)(ag_finish)

    row = lambda wd: pl.BlockSpec((tm, wd), lambda i: (i, 0))
    res = pl.pallas_call(
        body, name="in_proj", grid=(nt,),
        in_specs=[row(D), _cspec((1, D)), _cspec((D, PW)), row(128), row(128), row(128)] + [VM] * nw,
        out_specs=[row(1536), row(1024), row(512), row(512), row(128), row(D)] + [ANY] * nw,
        out_shape=[jax.ShapeDtypeStruct((S, 1536), F32), jax.ShapeDtypeStruct((S, 1024), F32),
                   jax.ShapeDtypeStruct((S, 512), F32), jax.ShapeDtypeStruct((S, 512), F32),
                   jax.ShapeDtypeStruct((S, 128), F32), jax.ShapeDtypeStruct((S, D), BF16)]
        + _gather_shapes(shards, dtypes),
        scratch_shapes=_gather_scratch(shards, dtypes),
        compiler_params=_params(1),
    )(x, g1, w, rc, ra, rb, *shards)
    return res[:6], res[6:]


DILATIONS = (16, 4, 1)


def _attn_valid(n):
    kd = lax.broadcasted_iota(jnp.int32, (128, 256), 1) - lax.broadcasted_iota(jnp.int32, (128, 256), 0)
    off = jnp.where(n == 0, 0, 128)
    return (kd <= off) & (kd >= off - 128)


def _attn_rows(d, r, n):
    if d == 1:
        q0 = pl.multiple_of(n * 128, 128)
        k0 = pl.multiple_of(jnp.maximum(n - 1, 0) * 128, 128)
        return pl.ds(q0, 128), pl.ds(k0, 256), _attn_valid(n)
    q0 = r + n * 128 * d
    k0 = r + jnp.maximum(n - 1, 0) * 128 * d
    return pl.ds(q0, 128, stride=d), pl.ds(k0, 256, stride=d), _attn_valid(n)


ATTN_GROUP = 4
ATTN_ITERS = S // 128 // ATTN_GROUP


def _attn_group(d, i):
    nb = S // (128 * d)
    if nb == 2:
        qi = lax.broadcasted_iota(jnp.int32, (256, 256), 0) - lax.broadcasted_iota(jnp.int32, (256, 256), 1)
        whole = [pl.ds((ATTN_GROUP // 2) * i + u, 256, stride=d) for u in range(ATTN_GROUP // 2)]
        return [(rows, rows, (qi >= 0) & (qi <= 128)) for rows in whole]
    if d == 1:
        return [_attn_rows(1, 0, i + ATTN_ITERS * u) for u in range(ATTN_GROUP)]
    return [_attn_rows(d, (i // nb) * ATTN_GROUP + u, i % nb) for u in range(ATTN_GROUP)]


def _head0(shape):
    return lax.broadcasted_iota(jnp.int32, shape, 1) < 64


def _stack_heads(t):
    h0 = _head0(t.shape)
    tb = _bf(t)
    zero = jnp.zeros_like(tb)
    return jnp.concatenate([jnp.where(h0, tb, zero), jnp.where(h0, zero, tb)], axis=0)


def _attn_fwd(qkv, shards, dtypes):
    nw = len(shards)

    def body(*refs):
        q_ref, k_ref, v_ref = refs[:3]
        ins = refs[3:3 + nw]
        o_ref, lse0_ref, lse1_ref = refs[3 + nw:6 + nw]
        outs = refs[6 + nw:6 + 2 * nw]
        m0, m1, l0, l1, acc = refs[6 + 2 * nw:11 + 2 * nw]
        bufs = refs[11 + 2 * nw:11 + 3 * nw]
        ag_start, ag_forward, ag_finish = _gather_phases(ins, outs, bufs, *refs[11 + 3 * nw:])
        hp = pl.program_id(0)
        pl.when(hp == 0)(ag_start)
        pl.when(hp == 3)(ag_forward)
        stats = (m0, m1, l0, l1, acc)

        def update(blocks, first):
            loaded = [([q_ref[rq, :], k_ref[rk, :], v_ref[rk, :]], None if first else [ref[rq, :] for ref in stats])
                      for rq, rk, _ in blocks]
            both = lambda a, b: jnp.concatenate([a, b], axis=0)
            ss = [jnp.where(both(valid, valid), _dot_nt(_stack_heads(q * 0.125), _bf(k)), NEG)
                  for ((q, k, _), _), (_, _, valid) in zip(loaded, blocks)]
            mcs = [jnp.max(s, axis=-1, keepdims=True) for s in ss]
            if first:
                m2s = [jnp.broadcast_to(mc, (mc.shape[0], 128)) for mc in mcs]
            else:
                m2s = [jnp.maximum(both(prev[0], prev[1]), mc) for mc, (_, prev) in zip(mcs, loaded)]
            ps = [jnp.exp(s - jnp.tile(m2, (1, 2))) for s, m2 in zip(ss, m2s)]
            l2s = [jnp.sum(p, axis=-1, keepdims=True) for p in ps]
            acc2s = [_dot(_bf(p), _bf(v)) for p, ((_, _, v), _) in zip(ps, loaded)]
            results = []
            for m2, l2, acc2, (_, prev) in zip(m2s, l2s, acc2s, loaded):
                nq = m2.shape[0] // 2
                if first:
                    l2 = jnp.broadcast_to(l2, (2 * nq, 128))
                else:
                    alpha = jnp.exp(both(prev[0], prev[1]) - m2)
                    l2, acc2 = alpha * both(prev[2], prev[3]) + l2, alpha * both(prev[4], prev[4]) + acc2
                results.append((m2[0:nq], m2[nq:2 * nq], l2[0:nq], l2[nq:2 * nq],
                                jnp.where(_head0((nq, 128)), acc2[0:nq], acc2[nq:2 * nq])))
            for (rq, _, _), res in zip(blocks, results):
                for ref, val in zip(stats, res):
                    ref[rq, :] = val

        for d in DILATIONS:
            def step(i, carry, d=d):
                update(_attn_group(d, i), d == DILATIONS[0])
                return carry

            lax.fori_loop(0, ATTN_ITERS, step, 0)

        def fin(t, carry):
            rows = pl.ds(pl.multiple_of(t * 256, 256), 256)
            h0 = lax.broadcasted_iota(jnp.int32, (256, 128), 1) < 64
            la, lb = l0[rows, :], l1[rows, :]
            o_ref[rows, :] = acc[rows, :] / jnp.where(h0, la, lb)
            lse0_ref[rows, :] = m0[rows, :] + jnp.log(la)
            lse1_ref[rows, :] = m1[rows, :] + jnp.log(lb)
            return carry

        lax.fori_loop(0, S // 256, fin, 0)
        pl.when(hp == 3)(ag_finish)

    col = lambda off: pl.BlockSpec((S, 128), lambda h, off=off: (0, off + h))
    res = pl.pallas_call(
        body, name="attn_fwd", grid=(4,),
        in_specs=[col(0), col(4), col(8)] + [VM] * nw,
        out_specs=[col(0), col(0), col(0)] + [ANY] * nw,
        out_shape=[jax.ShapeDtypeStruct((S, AW), F32)] * 3 + _gather_shapes(shards, dtypes),
        scratch_shapes=[pltpu.VMEM((S, 128), F32)] * 5 + _gather_scratch(shards, dtypes),
        compiler_params=_params(1),
    )(qkv, qkv, qkv, *shards)
    return res[0], (res[1], res[2]), res[3:]


def _attn_bwd(qkv, o, lse, do, parts):
    nw = len(parts)

    def body(*refs):
        q_ref, k_ref, v_ref, o_ref, L0, L1, do_ref = refs[:7]
        ins = refs[7:7 + nw]
        dq_out, dk_out, dv_out = refs[7 + nw:10 + nw]
        outs = refs[10 + nw:10 + 2 * nw]
        D0, D1, dq_ref, dk_ref, dv_ref = refs[10 + 2 * nw:15 + 2 * nw]
        rs_start, rs_finish = _scatter_phases(ins, outs, *refs[15 + 2 * nw:])
        hp = pl.program_id(0)
        pl.when(hp == 0)(rs_start)

        def pre(t, carry):
            rows = pl.ds(pl.multiple_of(t * 256, 256), 256)
            h0 = lax.broadcasted_iota(jnp.int32, (256, 128), 1) < 64
            dd = do_ref[rows, :] * o_ref[rows, :]
            shp = (256, 128)
            D0[rows, :] = jnp.broadcast_to(jnp.sum(jnp.where(h0, dd, 0.0), axis=-1, keepdims=True), shp)
            D1[rows, :] = jnp.broadcast_to(jnp.sum(jnp.where(h0, 0.0, dd), axis=-1, keepdims=True), shp)
            return carry

        lax.fori_loop(0, S // 256, pre, 0)

        def update(blocks, first):
            loaded = [([q_ref[rq, :], k_ref[rk, :], v_ref[rk, :], do_ref[rq, :]],
                       [L0[rq, :], L1[rq, :], D0[rq, :], D1[rq, :]],
                       [0.0] * 3 if first else [dq_ref[rq, :], dk_ref[rk, :], dv_ref[rk, :]]) for rq, rk, _ in blocks]
            cat = lambda a, b: jnp.tile(jnp.concatenate([a, b], axis=0), (1, 2))
            ops = [(_stack_heads(q * 0.125), _stack_heads(q), _stack_heads(dout), _bf(k), _bf(v))
                   for (q, k, v, dout), _, _ in loaded]
            ss = [jnp.where(jnp.concatenate([valid, valid], axis=0), _dot_nt(qs, kb), NEG)
                  for (qs, _, _, kb, _), (_, _, valid) in zip(ops, blocks)]
            dps = [_dot_nt(do2, vb) for _, _, do2, _, vb in ops]
            ps = [jnp.exp(s - cat(st[0], st[1])) for s, (_, st, _) in zip(ss, loaded)]
            dss = [_bf(p * (dp - cat(st[2], st[3])) * 0.125) for p, dp, (_, st, _) in zip(ps, dps, loaded)]
            dq2s = [_dot(ds, kb) for ds, (_, _, _, kb, _) in zip(dss, ops)]
            dks = [_dot_tn(ds, q2) for ds, (_, q2, _, _, _) in zip(dss, ops)]
            dvs = [_dot_tn(_bf(p), do2) for p, (_, _, do2, _, _) in zip(ps, ops)]
            results = []
            for (_, _, (dq, dk, dv)), dq2, dkk, dvv in zip(loaded, dq2s, dks, dvs):
                nq = dq2.shape[0] // 2
                results.append((dq + jnp.where(_head0((nq, 128)), dq2[0:nq], dq2[nq:2 * nq]), dk + dkk, dv + dvv))
            for (rq, rk, _), (dq, dk, dv) in zip(blocks, results):
                dq_ref[rq, :] = dq
                dk_ref[rk, :] = dk
                dv_ref[rk, :] = dv

        assert S // (128 * DILATIONS[0]) == 2
        for d in DILATIONS:
            def step(i, carry, d=d):
                update(_attn_group(d, i), d == DILATIONS[0])
                return carry

            lax.fori_loop(0, ATTN_ITERS, step, 0)

        def fin(t, carry):
            rows = pl.ds(pl.multiple_of(t * 256, 256), 256)
            for src, dst in ((dq_ref, dq_out), (dk_ref, dk_out), (dv_ref, dv_out)):
                dst[rows, :] = _bf(src[rows, :])
            return carry

        lax.fori_loop(0, S // 256, fin, 0)
        pl.when(hp == 3)(rs_finish)

    col = lambda off: pl.BlockSpec((S, 128), lambda h, off=off: (0, off + h))
    res = pl.pallas_call(
        body, name="attn_bwd", grid=(4,),
        in_specs=[col(0), col(4), col(8), col(0), col(0), col(0), col(0)] + [ANY] * nw,
        out_specs=[col(0), col(0), col(0)] + [ANY] * nw,
        out_shape=[jax.ShapeDtypeStruct((S, AW), BF16)] * 3 + [jax.ShapeDtypeStruct(a.shape, a.dtype) for a in parts],
        scratch_shapes=[pltpu.VMEM((S, 128), F32)] * 5 + _scatter_scratch(nw),
        compiler_params=_params(1),
    )(qkv, qkv, qkv, o, lse[0], lse[1], do, *parts)
    return res[0], res[1], res[2], res[3:]


def _logsig(x):
    return jnp.minimum(x, 0.0) - jnp.log1p(jnp.exp(-jnp.abs(x)))


def _conv_taps(xp, n):
    return [xp[8:] if j == 3 else pltpu.roll(xp, 3 - j, 0)[8:] for j in range(4)]


def _conv_silu(xp, w_ref, b_ref, n):
    taps = _conv_taps(xp, n)
    c = b_ref[...] + sum(w_ref[j:j + 1, :] * taps[j] for j in range(4))
    sg = _sigmoid(c)
    return c, sg, taps


def _chunk_gates(G):
    assert LC == 128
    r = lax.broadcasted_iota(jnp.int32, (LC, LC), 0)
    c = lax.broadcasted_iota(jnp.int32, (LC, LC), 1)
    tril = (c <= r).astype(F32)
    triu = (c >= r).astype(F32)
    b_col = jnp.dot(tril, _logsig(G), preferred_element_type=F32, precision=HI)
    return b_col, b_col.T, G.T, tril, triu


def _colpick(X, lane):
    li = lax.broadcasted_iota(jnp.int32, X.shape, 1)
    return jnp.sum(jnp.where(li == lane, X, 0.0), axis=1, keepdims=True)


def _rowpick(XT, row):
    ri = lax.broadcasted_iota(jnp.int32, XT.shape, 0)
    return jnp.sum(jnp.where(ri == row, XT, 0.0), axis=0, keepdims=True)


def _each(f, *lists):
    return [f(*a) for a in zip(*lists)]


def _mlstm_heads(Q, K, V, G, b_col, b_row, g_row, C, N, M):
    hs = range(len(Q))
    bt = [_colpick(b_col, 4 + h) for h in hs]
    i_col = [_colpick(G, h) for h in hs]
    bs = [_rowpick(b_row, 4 + h) for h in hs]
    i_row = [_rowpick(g_row, h) for h in hs]
    r = lax.broadcasted_iota(jnp.int32, (LC, LC), 0)
    c = lax.broadcasted_iota(jnp.int32, (LC, LC), 1)
    lane = lax.broadcasted_iota(jnp.int32, (1, LC), 1)
    qb, kb, vb = [_bf(t) for t in Q], [_bf(t) for t in K], [_bf(t) for t in V]
    S_ = _each(_dot_nt, qb, kb)
    qC = _each(lambda q, ch: _dot(q, _bf(ch)), qb, C)
    log_d = _each(lambda a, b, i: jnp.where(c <= r, a - b + i, NEG), bt, bs, i_row)
    log_inter = _each(lambda a, m: a + m, bt, M)
    m_t = _each(lambda li, ld: jnp.maximum(li, jnp.max(ld, axis=1, keepdims=True)), log_inter, log_d)
    Dm = _each(lambda ld, m: jnp.exp(ld - m), log_d, m_t)
    g = _each(lambda li, m: jnp.exp(li - m), log_inter, m_t)
    Am = _each(lambda s, d: s * d, S_, Dm)
    AV = _each(lambda a, v: _dot(_bf(a), v), Am, vb)
    num = _each(lambda gg, qc, av: gg * qc + av, g, qC, AV)
    qn = _each(lambda q, n: jnp.sum(q * n, axis=1, keepdims=True), Q, N)
    den = _each(lambda gg, x, a: gg * x + jnp.sum(a, axis=1, keepdims=True), g, qn, Am)
    floor = [jnp.exp(-m) for m in m_t]
    inv_dd = _each(lambda d, f: 1.0 / jnp.maximum(jnp.abs(d), f), den, floor)
    hh = _each(lambda n, i: n * i, num, inv_dd)
    blast = [jnp.sum(jnp.where(lane == LC - 1, b, 0.0), axis=1, keepdims=True) for b in bs]
    log_s = _each(lambda bl, a, i: bl - a + i, blast, bt, i_col)
    m_new = _each(lambda bl, m, ls: jnp.maximum(bl + m, jnp.max(ls, axis=0, keepdims=True)), blast, M, log_s)
    decay = _each(lambda bl, m, mn: jnp.exp(bl + m - mn), blast, M, m_new)
    ws = _each(lambda ls, mn: jnp.exp(ls - mn), log_s, m_new)
    kw = _each(lambda k, w: k * w, K, ws)
    KV = _each(lambda k, v: _dot_tn(_bf(k), v), kw, vb)
    C_new = _each(lambda d, ch, kv: d * ch + kv, decay, C, KV)
    n_new = _each(lambda d, n, k: d * n + jnp.sum(k, axis=0, keepdims=True), decay, N, kw)
    return dict(Dm=Dm, g=g, Am=Am, qC=qC, qn=qn, den=den, floor=floor, inv_dd=inv_dd, h=hh, decay=decay, ws=ws, kw=kw,
                C_new=C_new, n_new=n_new, m_new=m_new, qb=qb, kb=kb, vb=vb)


def _head_out(hh, mo_h, gn_h):
    r = lax.rsqrt(jnp.mean(hh * hh, axis=-1, keepdims=True) + EPS)
    hn = hh * r
    sg = _sigmoid(mo_h)
    return sg * (hn * gn_h), hn, r, sg


def _mlstm_fwd(mqk, mv, mo, gates, conv_w, conv_b, gate_b, gn, shards, dtypes):
    nblk = S // TB
    ncb = TB // LC
    nw = len(shards)

    def body(*refs):
        x_ref, v_ref, o_ref, g_ref, w_ref, b_ref, gb_ref, gn_ref = refs[:8]
        ins = refs[8:8 + nw]
        out_ref, cs_ref, ns_ref, ms_ref = refs[8 + nw:12 + nw]
        outs = refs[12 + nw:12 + 2 * nw]
        tail, Cst, nst, mst, qs, ks = refs[12 + 2 * nw:18 + 2 * nw]
        bufs = refs[18 + 2 * nw:18 + 3 * nw]
        ag_start, ag_forward, ag_finish = _gather_phases(ins, outs, bufs, *refs[18 + 3 * nw:])
        i = pl.program_id(0)
        pl.when(i == 0)(ag_start)
        pl.when(i == nblk // 2)(ag_forward)

        @pl.when(i == 0)
        def _():
            tail[...] = jnp.zeros_like(tail)
            Cst[...] = jnp.zeros_like(Cst)
            nst[...] = jnp.zeros_like(nst)
            mst[...] = jnp.zeros_like(mst)

        x = x_ref[...]
        xp = jnp.concatenate([tail[...], x], axis=0)
        tail[...] = x[TB - 8:TB, :]
        c, sg, _ = _conv_silu(xp, w_ref, b_ref, TB)
        y = c * sg
        qs[...] = y[:, 0:MW]
        ks[...] = y[:, MW:2 * MW] * (1.0 / math.sqrt(128.0))

        for cc in range(ncb):
            rows = slice(cc * LC, (cc + 1) * LC)
            G = g_ref[rows, :] + gb_ref[...]
            b_col, b_row, g_row, _, _ = _chunk_gates(G)
            cs_ref[cc] = Cst[...]
            ns_ref[cc] = nst[...]
            ms_ref[cc] = mst[...]
            lns = [slice(h * 128, (h + 1) * 128) for h in range(4)]
            f = _mlstm_heads([qs[rows, ln] for ln in lns], [ks[rows, ln] for ln in lns], [v_ref[rows, ln] for ln in lns],
                             G, b_col, b_row, g_row, [Cst[:, ln] for ln in lns], [nst[0:1, ln] for ln in lns],
                             [jnp.max(mst[0:1, ln], axis=1, keepdims=True) for ln in lns])
            outs = [_head_out(hh, o_ref[rows, ln], gn_ref[:, ln])[0] for hh, ln in zip(f["h"], lns)]
            for h, ln in enumerate(lns):
                out_ref[rows, ln] = outs[h]
                Cst[:, ln] = f["C_new"][h]
                nst[0:1, ln] = f["n_new"][h]
                mst[0:1, ln] = jnp.broadcast_to(f["m_new"][h], (1, 128))
        pl.when(i == nblk - 1)(ag_finish)

    row = lambda wd: pl.BlockSpec((TB, wd), lambda i: (i, 0))
    res = pl.pallas_call(
        body, name="mlstm_fwd", grid=(nblk,),
        in_specs=[row(1024), row(MW), row(MW), row(128), _cspec((4, 1024)), _cspec((1, 1024)), _cspec((1, 128)),
                  _cspec((1, MW))] + [VM] * nw,
        out_specs=[row(MW), pl.BlockSpec((ncb, 128, MW), lambda i: (i, 0, 0)),
                   pl.BlockSpec((ncb, 8, MW), lambda i: (i, 0, 0)), pl.BlockSpec((ncb, 8, MW), lambda i: (i, 0, 0))]
        + [ANY] * nw,
        out_shape=[jax.ShapeDtypeStruct((S, MW), F32), jax.ShapeDtypeStruct((S // LC, 128, MW), F32),
                   jax.ShapeDtypeStruct((S // LC, 8, MW), F32), jax.ShapeDtypeStruct((S // LC, 8, MW), F32)]
        + _gather_shapes(shards, dtypes),
        scratch_shapes=[pltpu.VMEM((8, 1024), F32), pltpu.VMEM((128, MW), F32), pltpu.VMEM((8, MW), F32),
                        pltpu.VMEM((8, MW), F32), pltpu.VMEM((TB, MW), F32), pltpu.VMEM((TB, MW), F32)]
        + _gather_scratch(shards, dtypes),
        compiler_params=_params(1),
    )(mqk, mv, mo, gates, conv_w, conv_b, gate_b, gn, *shards)
    return res[0], res[1], res[2], res[3], res[4:]


DM_V, DM_O, DM_G, DM_W = 1024, 1536, 2048, PW - 3 * AW


def _mlstm_bwd(mqk, mv, mo, gates, conv_w, conv_b, gate_b, gn, cs, ns, ms, dout, parts):
    assert len(parts) == 1
    nblk = S // TB
    ncb = TB // LC
    kscale = 1.0 / math.sqrt(128.0)
    nw = len(parts)

    def body(*refs):
        x_ref, xprev_ref, v_ref, o_ref, g_ref, w_ref, b_ref, gb_ref, gn_ref, cs_ref, ns_ref, ms_ref, do_ref = refs[:13]
        ins = refs[13:13 + nw]
        dm_ref, dw_ref, db_ref, dgn_ref, dgb_ref = refs[13 + nw:18 + nw]
        outs = refs[18 + nw:18 + 2 * nw]
        dCst, dnst, dyhead, qs, ks, dqk = refs[18 + 2 * nw:24 + 2 * nw]
        rs_start, rs_middle, rs_finish = _scatter2_phases(ins[0], outs[0], *refs[24 + 2 * nw:])
        i = pl.program_id(0)
        blk = nblk - 1 - i
        pl.when(i == 0)(rs_start)
        pl.when(i == 2)(rs_middle)

        @pl.when(i == 0)
        def _():
            dCst[...] = jnp.zeros_like(dCst)
            dnst[...] = jnp.zeros_like(dnst)
            dyhead[...] = jnp.zeros_like(dyhead)
            dw_ref[...] = jnp.zeros_like(dw_ref)
            db_ref[...] = jnp.zeros_like(db_ref)
            dgn_ref[...] = jnp.zeros_like(dgn_ref)
            dgb_ref[...] = jnp.zeros_like(dgb_ref)

        x = x_ref[...]
        xprev = jnp.where(blk == 0, 0.0, xprev_ref[...])
        xp = jnp.concatenate([xprev, x], axis=0)
        c, sg, taps = _conv_silu(xp, w_ref, b_ref, TB)
        y = c * sg
        qs[...] = y[:, 0:MW]
        ks[...] = y[:, MW:2 * MW] * kscale
        lane128 = lax.broadcasted_iota(jnp.int32, (LC, 128), 1)
        rowi = lax.broadcasted_iota(jnp.int32, (LC, 1), 0)

        for cc in reversed(range(ncb)):
            rows = slice(cc * LC, (cc + 1) * LC)
            G = g_ref[rows, :] + gb_ref[...]
            b_col, b_row, g_row, _, triu = _chunk_gates(G)
            lns = [slice(h * 128, (h + 1) * 128) for h in range(4)]
            C = [cs_ref[cc, :, ln] for ln in lns]
            N = [ns_ref[cc, 0:1, ln] for ln in lns]
            Q, Kk = [qs[rows, ln] for ln in lns], [ks[rows, ln] for ln in lns]
            dCn, dnn = [dCst[:, ln] for ln in lns], [dnst[0:1, ln] for ln in lns]
            gns, dos, mos = [gn_ref[:, ln] for ln in lns], [do_ref[rows, ln] for ln in lns], [o_ref[rows, ln] for ln in lns]
            f = _mlstm_heads(Q, Kk, [v_ref[rows, ln] for ln in lns], G, b_col, b_row, g_row, C, N,
                             [jnp.max(ms_ref[cc, 0:1, ln], axis=1, keepdims=True) for ln in lns])
            hh, inv_dd, den, g, Am, Dm = f["h"], f["inv_dd"], f["den"], f["g"], f["Am"], f["Dm"]
            qb, kb, vb, ws, decay = f["qb"], f["kb"], f["vb"], f["ws"], f["decay"]
            ho = _each(_head_out, hh, mos, gns)
            hn, r, sgo = [t[1] for t in ho], [t[2] for t in ho], [t[3] for t in ho]
            dmo = _each(lambda d, n, gn_h, s: _bf(d * (n * gn_h) * s * (1.0 - s)), dos, hn, gns, sgo)
            dhm = _each(lambda d, s: d * s, dos, sgo)
            dgn = _each(lambda d, n: jnp.sum(d * n, axis=0, keepdims=True), dhm, hn)
            dhn = _each(lambda d, gn_h: d * gn_h, dhm, gns)
            dh = _each(lambda rr, d, n: rr * (d - n * jnp.mean(d * n, axis=-1, keepdims=True)), r, dhn, hn)
            dnum = _each(lambda d, i: d * i, dh, inv_dd)
            ddd = _each(lambda d, x, i: -jnp.sum(d * x, axis=1, keepdims=True) * i, dh, hh, inv_dd)
            dden = _each(lambda dn_, fl, d: jnp.where(jnp.abs(dn_) >= fl, d * jnp.sign(dn_), 0.0), den, f["floor"], ddd)
            dnb = [_bf(t) for t in dnum]
            gd = _each(lambda gg, d: _bf(gg * d), g, dnum)
            gq = _each(lambda gg, d: gg * d, g, dden)
            dCb = [_bf(t) for t in dCn]
            dA = _each(lambda d, v, dd_: _dot_nt(d, v) + dd_, dnb, vb, dden)
            dv1 = _each(lambda a, d: _dot_tn(_bf(a), d), Am, dnb)
            dq1 = _each(lambda d, ch: _dot_nt(d, _bf(ch)), gd, C)
            dC1 = _each(_dot_tn, qb, gd)
            E = _each(lambda v, d, n: _dot_nt(v, d) + n, vb, dCb, dnn)
            dv2 = _each(lambda k, d: _dot(_bf(k), d), f["kw"], dCb)
            dS = _each(lambda a, d: _bf(a * d), dA, Dm)
            dq2 = _each(_dot, dS, kb)
            dk1 = _each(_dot_tn, dS, qb)
            dq = _each(lambda a, x, n, b: a + x * n + b, dq1, gq, N, dq2)
            dC = _each(lambda d, x, y: d * x + y, decay, dCn, dC1)
            dn = _each(lambda d, x, y, q: d * x + jnp.sum(y * q, axis=0, keepdims=True), decay, dnn, gq, Q)
            dg = _each(lambda d, qc, dd_, x: jnp.sum(d * qc, axis=1, keepdims=True) + dd_ * x, dnum, f["qC"], dden, f["qn"])
            Gm = _each(lambda a, b: a * b, dA, Am)
            gam = _each(lambda a, b: a * b, dg, g)
            dk = _each(lambda a, w, e: (a + w * e) * kscale, dk1, ws, E)
            om = _each(lambda e, k, w: jnp.sum(e * k, axis=1, keepdims=True) * w, E, Kk, ws)
            dv = _each(lambda a, b: _bf(a + b), dv1, dv2)
            ddecay = _each(lambda d, ch, dn_, n: jnp.sum(jnp.sum(d * ch, axis=1, keepdims=True), axis=0, keepdims=True)
                           + jnp.sum(dn_ * n, axis=1, keepdims=True), dCn, C, dnn, N)
            rows_g = [jnp.sum(t, axis=1, keepdims=True) for t in Gm]
            cols_g = [jnp.broadcast_to(jnp.sum(t, axis=0, keepdims=True), (LC, 128)).T for t in Gm]
            last = _each(lambda o, dd_, d: jnp.where(rowi == LC - 1, jnp.sum(o, axis=0, keepdims=True) + dd_ * d, 0.0),
                         om, ddecay, decay)
            db = _each(lambda a, b, o, l, cg: a + b - o + l - cg, rows_g, gam, om, last, cols_g)
            di = _each(lambda cg, o: cg + o, cols_g, om)
            dB = jnp.zeros((LC, 128), F32)
            dI = jnp.zeros((LC, 128), F32)
            for h, ln in enumerate(lns):
                dB = jnp.where(lane128 == 4 + h, db[h], dB)
                dI = jnp.where(lane128 == h, di[h], dI)
                dgn_ref[:, ln] = dgn_ref[:, ln] + dgn[h]
                dCst[:, ln] = dC[h]
                dnst[0:1, ln] = dn[h]
                dqk[rows, ln] = dq[h]
                dqk[rows, MW + h * 128:MW + (h + 1) * 128] = dk[h]
                dm_ref[rows, DM_O + h * 128:DM_O + (h + 1) * 128] = dmo[h]
                dm_ref[rows, DM_V + h * 128:DM_V + (h + 1) * 128] = dv[h]
            dlogf = jnp.dot(triu, dB, preferred_element_type=F32, precision=HI)
            dG = dI + dlogf * _sigmoid(-G)
            dG = jnp.where(lane128 < 8, dG, 0.0)
            dm_ref[rows, DM_G:DM_G + 128] = _bf(dG)
            dm_ref[rows, DM_G + 128:DM_W] = jnp.zeros((LC, DM_W - DM_G - 128), BF16)
            dgb_ref[...] = dgb_ref[...] + jnp.sum(dG, axis=0, keepdims=True)

        dy = dqk[...] * (sg * (1.0 + c * (1.0 - sg)))
        db_ref[...] = db_ref[...] + jnp.sum(dy, axis=0, keepdims=True)
        for j in range(4):
            dw_ref[j:j + 1, :] = dw_ref[j:j + 1, :] + jnp.sum(dy * taps[j], axis=0, keepdims=True)
        dyp = jnp.concatenate([dy, dyhead[...]], axis=0)
        dx = w_ref[3:4, :] * dy
        for j in range(3):
            dx = dx + w_ref[j:j + 1, :] * pltpu.roll(dyp, TB + 8 - (3 - j), 0)[0:TB]
        dm_ref[:, 0:DM_V] = _bf(dx)
        dyhead[...] = dy[0:8, :]
        pl.when(i == nblk - 1)(rs_finish)

    rrow = lambda wd: pl.BlockSpec((TB, wd), lambda i: (nblk - 1 - i, 0))
    st = lambda r: pl.BlockSpec((ncb, r, MW), lambda i: (nblk - 1 - i, 0, 0))
    prev8 = pl.BlockSpec((8, 1024), lambda i: (jnp.maximum((nblk - 1 - i) * (TB // 8) - 1, 0), 0))
    res = pl.pallas_call(
        body, name="mlstm_bwd", grid=(nblk,),
        in_specs=[rrow(1024), prev8, rrow(MW), rrow(MW), rrow(128), _cspec((4, 1024)), _cspec((1, 1024)),
                  _cspec((1, 128)), _cspec((1, MW)), st(128), st(8), st(8), rrow(MW)] + [ANY] * nw,
        out_specs=[rrow(DM_W),
                   pl.BlockSpec((4, 1024), lambda i: (0, 0)), pl.BlockSpec((1, 1024), lambda i: (0, 0)),
                   pl.BlockSpec((1, MW), lambda i: (0, 0)), pl.BlockSpec((1, 128), lambda i: (0, 0))] + [ANY] * nw,
        out_shape=[jax.ShapeDtypeStruct((S, DM_W), BF16),
                   jax.ShapeDtypeStruct((4, 1024), F32), jax.ShapeDtypeStruct((1, 1024), F32),
                   jax.ShapeDtypeStruct((1, MW), F32), jax.ShapeDtypeStruct((1, 128), F32)]
        + [jax.ShapeDtypeStruct((len(CHIP_FLIPS), *a.shape[1:]), a.dtype) for a in parts],
        scratch_shapes=[pltpu.VMEM((128, MW), F32), pltpu.VMEM((8, MW), F32), pltpu.VMEM((8, 1024), F32),
                        pltpu.VMEM((TB, MW), F32), pltpu.VMEM((TB, MW), F32), pltpu.VMEM((TB, 1024), F32)]
        + _scatter2_scratch(parts[0].shape[1:], parts[0].dtype),
        compiler_params=_params(1),
    )(mqk, mqk, mv, mo, gates, conv_w, conv_b, gate_b, gn, cs, ns, ms, dout, *parts)
    return res[:5], res[5:]


def _out_proj(x, attn, ml, w, g):
    tm = TM

    def body(x_ref, a_ref, m_ref, w_ref, g_ref, h_ref, u_ref):
        h1 = x_ref[...] + _dot(_bf(a_ref[...]), w_ref[0:AW, :]) + _dot(_bf(m_ref[...]), w_ref[AW:D, :])
        h_ref[...] = h1
        n, _ = _rms(h1)
        u_ref[...] = _bf(n * g_ref[...])

    row = lambda wd: pl.BlockSpec((tm, wd), lambda i: (i, 0))
    return pl.pallas_call(
        body, name="out_proj", grid=(S // tm,),
        in_specs=[row(D), row(AW), row(MW), _cspec((D, D)), _cspec((1, D))],
        out_specs=[row(D), row(D)],
        out_shape=[jax.ShapeDtypeStruct((S, D), F32), jax.ShapeDtypeStruct((S, D), BF16)],
        compiler_params=_params(1),
    )(x, attn, ml, w, g)


HALF = DFF // NDEV // 2


def _mlp_fwd(h1, u2, w_up, w_down_a, w_down_b, shards, dtypes):
    tm = TM
    nt = S // tm
    nw = len(shards)

    def body(*refs):
        h_ref, u_ref, wu_ref, wa_ref, wb_ref = refs[:5]
        ins = refs[5:5 + nw]
        a_ref, o_ref = refs[5 + nw:7 + nw]
        outs = refs[7 + nw:7 + 2 * nw]
        bufs = refs[7 + 2 * nw:7 + 3 * nw]
        ag_start, ag_forward, ag_finish = _gather_phases(ins, outs, bufs, *refs[7 + 3 * nw:])
        i = pl.program_id(0)
        pl.when(i == 0)(ag_start)
        pl.when(i == nt - 2)(ag_forward)
        u = u_ref[...]
        acc = h_ref[...]
        for c in range(NDEV):
            cols = slice(c * 512, (c + 1) * 512)
            a = _dot(u, wu_ref[c])
            a_ref[:, cols] = _bf(a)
            r = jnp.maximum(a, 0.0)
            r = _bf(r * r)
            acc = acc + _dot(r[:, 0:HALF], wa_ref[c]) + _dot(r[:, HALF:2 * HALF], wb_ref[c])
        o_ref[...] = acc
        pl.when(i == nt - 1)(ag_finish)

    row = lambda wd: pl.BlockSpec((tm, wd), lambda i: (i, 0))
    res = pl.pallas_call(
        body, name="mlp_fwd", grid=(nt,),
        in_specs=[row(D), row(D), _cspec((NDEV, D, DFF // NDEV)), _cspec((NDEV, HALF, D)), _cspec((NDEV, HALF, D))]
        + [VM] * nw,
        out_specs=[row(DFF), row(D)] + [ANY] * nw,
        out_shape=[jax.ShapeDtypeStruct((S, DFF), BF16), jax.ShapeDtypeStruct((S, D), F32)]
        + _gather_shapes(shards, dtypes),
        scratch_shapes=_gather_scratch(shards, dtypes),
        compiler_params=_params(1),
    )(h1, u2, w_up, w_down_a, w_down_b, *shards)
    return res[0], res[1], res[2:]


def _ple_loss(h2, p, target, w_pg, w_ple, g_ple, g_fin):
    tm = TM

    def body(h_ref, p_ref, t_ref, wg_ref, wp_ref, gp_ref, gf_ref,
             dh_ref, dwg_ref, dwp_ref, dgp_ref, dgf_ref, loss_ref, acc_g, acc_p):
        i = pl.program_id(0)

        @pl.when(i == 0)
        def _():
            acc_g[...] = jnp.zeros_like(acc_g)
            acc_p[...] = jnp.zeros_like(acc_p)
            dgp_ref[...] = jnp.zeros_like(dgp_ref)
            dgf_ref[...] = jnp.zeros_like(dgf_ref)
            loss_ref[...] = jnp.zeros_like(loss_ref)

        h2v = h_ref[...]
        n2, rs2 = _rms(h2v)
        u3 = _bf(n2 * gp_ref[...])
        gt = _sigmoid(_dot(u3, wg_ref[...]))
        pb = _bf(p_ref[...])
        e = jnp.concatenate([_dot(pb, wp_ref[j]) for j in range(NDEV)], axis=1)
        h3 = h2v + gt * e
        n3, rs3 = _rms(h3)
        err = n3 * gf_ref[...] - t_ref[...]
        loss_ref[...] = loss_ref[...] + 0.5 / D * jnp.sum(jnp.sum(err * err, axis=1, keepdims=True), axis=0, keepdims=True)
        dy = err * (1.0 / D)
        dgf_ref[...] = dgf_ref[...] + jnp.sum(dy * n3, axis=0, keepdims=True)
        dh3 = _rms_bwd(dy, n3, rs3, gf_ref[...])
        de = _bf(dh3 * gt)
        dz = _bf(dh3 * e * gt * (1.0 - gt))
        acc_p[...] = acc_p[...] + _dot_tn(pb, de)
        acc_g[...] = acc_g[...] + _dot_tn(u3, dz)
        du3 = _dot_nt(dz, wg_ref[...])
        dgp_ref[...] = dgp_ref[...] + jnp.sum(du3 * n2, axis=0, keepdims=True)
        dh_ref[...] = dh3 + _rms_bwd(du3, n2, rs2, gp_ref[...])

        @pl.when(i == S // tm - 1)
        def _():
            dwg_ref[...] = _bf(acc_g[...])
            for j in range(NDEV):
                dwp_ref[j] = _bf(acc_p[:, j * 128:(j + 1) * 128])

    row = lambda wd: pl.BlockSpec((tm, wd), lambda i: (i, 0))
    whole = lambda shp: pl.BlockSpec(shp, lambda i: (0,) * len(shp))
    return pl.pallas_call(
        body, name="ple_loss", grid=(S // tm,),
        in_specs=[row(D), row(PLE), row(D), _cspec((D, D)), _cspec((NDEV, PLE, 128)), _cspec((1, D)), _cspec((1, D))],
        out_specs=[row(D), whole((D, D)), whole((NDEV, PLE, 128)), whole((1, D)), whole((1, D)), whole((1, 1))],
        out_shape=[jax.ShapeDtypeStruct((S, D), F32), jax.ShapeDtypeStruct((D, D), BF16),
                   jax.ShapeDtypeStruct((NDEV, PLE, 128), BF16), jax.ShapeDtypeStruct((1, D), F32),
                   jax.ShapeDtypeStruct((1, D), F32), jax.ShapeDtypeStruct((1, 1), F32)],
        scratch_shapes=[pltpu.VMEM((D, D), F32), pltpu.VMEM((PLE, D), F32)],
        compiler_params=_params(1),
    )(h2, p, target, w_pg, w_ple, g_ple, g_fin)


def _mlp_bwd(dh2, a, h1, g, w_up, w_down_a, w_down_b, parts):
    tm = TM
    nt = S // tm
    nw = len(parts)

    def body(*refs):
        d_ref, a_ref, h_ref, g_ref, wu_ref, wa_ref, wb_ref = refs[:7]
        ins = refs[7:7 + nw]
        da_ref, dh1_ref, dg_ref = refs[7 + nw:10 + nw]
        outs = refs[10 + nw:10 + 2 * nw]
        rs_start, rs_finish = _scatter_phases(ins, outs, *refs[10 + 2 * nw:])
        i = pl.program_id(0)
        pl.when(i == 0)(rs_start)

        @pl.when(i == 0)
        def _():
            dg_ref[...] = jnp.zeros_like(dg_ref)

        dh2v = d_ref[...]
        db = _bf(dh2v)
        du = jnp.zeros((tm, D), F32)
        for c in range(NDEV):
            cols = slice(c * 512, (c + 1) * 512)
            dr = jnp.concatenate([_dot_nt(db, wa_ref[c]), _dot_nt(db, wb_ref[c])], axis=1)
            da = _bf(dr * (2.0 * jnp.maximum(a_ref[:, cols], 0.0)))
            da_ref[:, cols] = da
            du = du + _dot_nt(da, wu_ref[c])
        n, rs = _rms(h_ref[...])
        dg_ref[...] = dg_ref[...] + jnp.sum(du * n, axis=0, keepdims=True)
        dh1_ref[...] = dh2v + _rms_bwd(du, n, rs, g_ref[...])
        pl.when(i == nt - 1)(rs_finish)

    row = lambda wd: pl.BlockSpec((tm, wd), lambda i: (i, 0))
    res = pl.pallas_call(
        body, name="mlp_bwd", grid=(nt,),
        in_specs=[row(D), row(DFF), row(D), _cspec((1, D)), _cspec((NDEV, D, DFF // NDEV)), _cspec((NDEV, HALF, D)),
                  _cspec((NDEV, HALF, D))] + [ANY] * nw,
        out_specs=[row(DFF), row(D), pl.BlockSpec((1, D), lambda i: (0, 0))] + [ANY] * nw,
        out_shape=[jax.ShapeDtypeStruct((S, DFF), BF16), jax.ShapeDtypeStruct((S, D), F32),
                   jax.ShapeDtypeStruct((1, D), F32)] + [jax.ShapeDtypeStruct(p.shape, p.dtype) for p in parts],
        scratch_shapes=_scatter_scratch(nw),
        compiler_params=_params(1),
    )(dh2, a, h1, g, w_up, w_down_a, w_down_b, *parts)
    return res[0], res[1], res[2], res[3:]


def _out_proj_bwd(dh1, attn, ml, w):
    tm = TM

    def body(d_ref, a_ref, m_ref, w_ref, da_ref, dm_ref, dw_ref, acc):
        i = pl.program_id(0)

        @pl.when(i == 0)
        def _():
            acc[...] = jnp.zeros_like(acc)

        db = _bf(d_ref[...])
        dmix = _dot_nt(db, w_ref[...])
        da_ref[...] = dmix[:, 0:AW]
        dm_ref[...] = dmix[:, AW:D]
        acc[0:AW, :] = acc[0:AW, :] + _dot_tn(_bf(a_ref[...]), db)
        acc[AW:D, :] = acc[AW:D, :] + _dot_tn(_bf(m_ref[...]), db)

        @pl.when(i == S // tm - 1)
        def _():
            dw_ref[...] = _bf(acc[...])

    row = lambda wd: pl.BlockSpec((tm, wd), lambda i: (i, 0))
    return pl.pallas_call(
        body, name="out_proj_bwd", grid=(S // tm,),
        in_specs=[row(D), row(AW), row(MW), _cspec((D, D))],
        out_specs=[row(AW), row(MW), pl.BlockSpec((D, D), lambda i: (0, 0))],
        out_shape=[jax.ShapeDtypeStruct((S, AW), F32), jax.ShapeDtypeStruct((S, MW), F32),
                   jax.ShapeDtypeStruct((D, D), BF16)],
        scratch_shapes=[pltpu.VMEM((D, D), F32)],
        compiler_params=_params(1),
    )(dh1, attn, ml, w)


CHIP_FLIPS = [(0, 0), (0, 1), (1, 0), (1, 1)]


def _scatter2_phases(in_ref, out_ref, mine_v, sib_v, psum_v, loc_sems, d2d_send, d2d_recv, ici_send, ici_recv, own_sem):
    x, y, c = _place()
    chips = [((x + dx) % 2, (y + dy) % 2) for dx, dy in CHIP_FLIPS]
    nc = len(chips)

    def local(k):
        return pltpu.make_async_copy(in_ref.at[_dev_index(*chips[k], c)], mine_v.at[k], loc_sems.at[k])

    def to_sib(k):
        return pltpu.make_async_remote_copy(
            src_ref=in_ref.at[_dev_index(*chips[k], 1 - c)], dst_ref=sib_v.at[k], send_sem=d2d_send.at[k],
            recv_sem=d2d_recv.at[k], device_id=(x, y, 1 - c), device_id_type=MESH)

    def over_ici(k):
        return pltpu.make_async_remote_copy(
            src_ref=psum_v.at[k], dst_ref=out_ref.at[k], send_sem=ici_send.at[k - 1], recv_sem=ici_recv.at[k - 1],
            device_id=(*chips[k], c), device_id_type=MESH)

    def own():
        return pltpu.make_async_copy(psum_v.at[0], out_ref.at[0], own_sem)

    def start():
        for k in range(nc):
            to_sib(k).start()
            local(k).start()

    def middle():
        for k in (1, 2, 3, 0):
            local(k).wait()
            to_sib(k).wait_recv()
            psum_v[k] = _bf(mine_v[k].astype(F32) + sib_v[k].astype(F32))
            (over_ici(k) if k else own()).start()

    def finish():
        for k in range(1, nc):
            over_ici(k).wait()
        for k in range(nc):
            to_sib(k).wait_send()
        own().wait()

    return start, middle, finish


def _scatter2_scratch(shard, dtype):
    nc = len(CHIP_FLIPS)
    return ([pltpu.VMEM((nc, *shard), dtype)] * 3
            + [pltpu.SemaphoreType.DMA((nc,))] * 3 + [pltpu.SemaphoreType.DMA((nc - 1,))] * 2 + [pltpu.SemaphoreType.DMA])


def _in_proj_bwd(dparts, n_roped, rope, dh1, x, g1, w, part):
    tm = TM
    nt = S // tm
    widths = [d.shape[1] for d in dparts]
    assert sum(widths) == PW
    npar = len(dparts)

    def body(*refs):
        d_refs = refs[:npar]
        tabs = [t[...] for t in refs[npar:npar + 3]]
        dh_ref, x_ref, g_ref, w_ref, in_ref, dx_ref, dgsum_ref, out_ref = refs[npar + 3:npar + 11]
        rs_start, rs_middle, rs_finish = _scatter2_phases(in_ref, out_ref, *refs[npar + 11:npar + 20])
        dg_ref = refs[npar + 20]
        ar_start, ar_finish = _small_phases([dg_ref], dgsum_ref, *refs[npar + 21:])
        i = pl.program_id(0)
        pl.when(i == 0)(rs_start)
        pl.when(i == 1)(rs_middle)

        @pl.when(i == 0)
        def _():
            dg_ref[...] = jnp.zeros_like(dg_ref)

        du = jnp.zeros((tm, D), F32)
        off = 0
        for j, (d_ref, wd) in enumerate(zip(d_refs, widths)):
            nc = next(c for c in (768, 512) if wd % c == 0)
            for s in range(wd // nc):
                d = d_ref[:, s * nc:(s + 1) * nc]
                du = du + _dot_nt(_unrope(d, *tabs) if j < n_roped else d, w_ref[:, off + s * nc:off + (s + 1) * nc])
            off += wd
        n, rs = _rms(x_ref[...])
        dg_ref[...] = dg_ref[...] + jnp.sum(du * n, axis=0, keepdims=True)
        dx_ref[...] = dh_ref[...] + _rms_bwd(du, n, rs, g_ref[...])

        @pl.when(i == nt - 1)
        def _():
            ar_start()
            rs_finish()
            ar_finish()

    row = lambda wd: pl.BlockSpec((tm, wd), lambda i: (i, 0))
    shard = part.shape[1:]
    return pl.pallas_call(
        body, name="in_proj_bwd", grid=(nt,),
        in_specs=[row(wd) for wd in widths] + [row(128)] * 3 + [row(D), row(D), _cspec((1, D)), _cspec((D, PW)), ANY],
        out_specs=[row(D), VM, ANY],
        out_shape=[jax.ShapeDtypeStruct((S, D), F32), jax.ShapeDtypeStruct((8, 1024), F32),
                   jax.ShapeDtypeStruct((len(CHIP_FLIPS), *shard), part.dtype)],
        scratch_shapes=_scatter2_scratch(shard, part.dtype)
        + [pltpu.VMEM((1, D), F32), pltpu.VMEM((8, 1024), F32), pltpu.VMEM((NDEV, 8, 1024), F32),
           pltpu.SemaphoreType.DMA((7,)), pltpu.SemaphoreType.DMA((7,))],
        compiler_params=_params(1),
    )(*dparts, *rope, dh1, x, g1, w, part)


SMALL_ROWS = 96


def _small_phases(ins, out_ref, pack, rbuf, send_sems, recv_sems):
    x, y, c = _place()
    me = _dev_index(x, y, c)

    def copies():
        out = []
        for k, (dx, dy, dc) in enumerate(FLIPS):
            peer = ((x + dx) % 2, (y + dy) % 2, (c + dc) % 2)
            out.append(pltpu.make_async_remote_copy(
                src_ref=pack, dst_ref=rbuf.at[me], send_sem=send_sems.at[k], recv_sem=recv_sems.at[k],
                device_id=peer, device_id_type=MESH))
        return out

    def start():
        pack[...] = jnp.zeros_like(pack)
        for i, ref in enumerate(ins):
            pack[8 * i:8 * i + 1, 0:ref.shape[1]] = ref[...]
        rbuf[me] = pack[...]
        for cp in copies():
            cp.start()

    def finish():
        for cp in copies():
            cp.wait()
        tot = rbuf[0]
        for j in range(1, NDEV):
            tot = tot + rbuf[j]
        out_ref[...] = tot

    return start, finish


def _wgrad(name, A, Bs, a_fn, b_fn, out_shape, split=None, ts=512, small=(), rope=(), n_roped=0):
    K = A.shape[1]
    widths = [b.shape[1] for b in Bs]
    N = sum(widths)
    nb, ns, nrt = len(Bs) + len(rope), len(small), S // ts
    kc = min(K, 1024)

    def body(*refs):
        a_ref, b_refs = refs[0], refs[1:1 + len(Bs)]
        tabs = [t[...] for t in refs[1 + len(Bs):1 + nb]]
        o_ref = refs[1 + nb + ns]
        acc = refs[2 + nb + ns + bool(ns)]
        r = pl.program_id(0)
        if ns:
            sm_start, sm_finish = _small_phases(refs[1 + nb:1 + nb + ns], refs[2 + nb + ns], *refs[4 + nb + ns:])
            pl.when(r == 0)(sm_start)

        @pl.when(r == 0)
        def _():
            acc[...] = jnp.zeros_like(acc)

        bs, off = [], 0
        for i, (b_ref, w) in enumerate(zip(b_refs, widths)):
            nc = next(c for c in (1024, 768, 512) if w % c == 0)
            fn = (lambda t: _unrope(t, *tabs)) if i < n_roped else b_fn
            bs += [(off + c * nc, nc, fn(b_ref[:, c * nc:(c + 1) * nc])) for c in range(w // nc)]
            off += w
        for kk in range(K // kc):
            rows = slice(kk * kc, (kk + 1) * kc)
            at = a_fn(a_ref[:, rows]).T
            for lo, nc, b in bs:
                acc[rows, lo:lo + nc] = acc[rows, lo:lo + nc] + _dot(at, b)

        @pl.when(r == nrt - 1)
        def _():
            if split is None:
                o_ref[...] = _bf(acc[...])
            else:
                for j in range(NDEV):
                    o_ref[j] = _bf(acc[:, split * j:split * (j + 1)])

        if ns:
            pl.when(r == nrt - 1)(sm_finish)

    in_specs = ([pl.BlockSpec((ts, K), lambda r: (r, 0))] + [pl.BlockSpec((ts, w), lambda r: (r, 0)) for w in widths]
                + [pl.BlockSpec((ts, 128), lambda r: (r, 0))] * len(rope))
    out_spec = pl.BlockSpec(out_shape, lambda r: (0,) * len(out_shape))
    scratch = [pltpu.VMEM((K, N), F32)]
    if not ns:
        return pl.pallas_call(
            body, name=name, grid=(nrt,), in_specs=in_specs, out_specs=out_spec,
            out_shape=jax.ShapeDtypeStruct(out_shape, BF16), scratch_shapes=scratch, compiler_params=_params(1),
        )(A, *Bs, *rope)
    return pl.pallas_call(
        body, name=name, grid=(nrt,), in_specs=in_specs + [VM] * ns, out_specs=[out_spec, VM],
        out_shape=[jax.ShapeDtypeStruct(out_shape, BF16), jax.ShapeDtypeStruct((SMALL_ROWS, 1024), F32)],
        scratch_shapes=scratch + [pltpu.VMEM((SMALL_ROWS, 1024), F32), pltpu.VMEM((NDEV, SMALL_ROWS, 1024), F32),
                                  pltpu.SemaphoreType.DMA((7,)), pltpu.SemaphoreType.DMA((7,))],
        compiler_params=_params(1),
    )(A, *Bs, *rope, *small)


def _relu2_bf(a):
    r = jnp.maximum(a.astype(F32), 0.0)
    return _bf(r * r)


def _ident(a):
    return a


def _step(x, p, target, g1, conv_b, gate_b, gn, g_mlp, g_ple, g_fin, sh):
    (g_in, g_conv), (rc, ra, rb) = _gather_weights([sh["w_in"], sh["conv_w"]], [BF16, F32])
    conv_w = g_conv.transpose(1, 0, 2).reshape(4, 1024)
    w_in_p = _join_w_in(g_in)
    (qkv, mqk, mv, mo, gates, u1), (w_down_a,) = _in_proj(x, g1, w_in_p, rc, ra, rb, [sh["w_down"][0:HALF]], [BF16])
    attn, lse, (w_up8, w_out8) = _attn_fwd(qkv, [sh["w_up"], sh["w_out"]], [BF16] * 2)
    ml, cs, ns, ms, (w_down_b,) = _mlstm_fwd(mqk, mv, mo, gates, conv_w, conv_b, gate_b, gn,
                                             [sh["w_down"][HALF:2 * HALF]], [BF16])
    w_out = w_out8.reshape(D, D)
    h1, u2 = _out_proj(x, attn, ml, w_out, g_mlp)
    a, h2, (w_pg8, w_ple8) = _mlp_fwd(h1, u2, w_up8, w_down_a, w_down_b, [sh["w_ple_gate"], sh["w_ple"]], [BF16] * 2)
    w_pg = w_pg8.reshape(D, D)
    dh2, dw_pg, dw_ple8, dg_ple, dg_fin, loss = _ple_loss(h2, p, target, w_pg, w_ple8, g_ple, g_fin)
    da, dh1, dg_mlp, (r_pg, r_ple) = _mlp_bwd(dh2, a, h1, g_mlp, w_up8, w_down_a, w_down_b,
                                              [dw_pg.reshape(NDEV, D // NDEV, D), dw_ple8])
    dw_up8 = _wgrad("wgrad_up", u2, [da], _ident, _ident, (NDEV, D, DFF // NDEV), split=DFF // NDEV)
    dw_down = _wgrad("wgrad_down", a, [dh2], _relu2_bf, _bf, (DFF, D))
    d_attn, d_ml, dw_out = _out_proj_bwd(dh1, attn, ml, w_out)
    (dm, dconv_w, dconv_b, dgn, dgate_b), (r_down,) = _mlstm_bwd(
        mqk, mv, mo, gates, conv_w, conv_b, gate_b, gn, cs, ns, ms, d_ml, [dw_down.reshape(NDEV, DFF // NDEV, D)])
    dq, dk, dv, (r_up, r_out) = _attn_bwd(qkv, attn, lse, d_attn, [dw_up8, dw_out.reshape(NDEV, D // NDEV, D)])
    dparts = [dq, dk, dv, dm]
    small = [jnp.zeros((1, D), F32), dconv_b, dgate_b, dgn, dg_mlp, dg_ple, dg_fin, loss]
    dw_in8, total = _wgrad("wgrad_in", u1, dparts, _ident, _ident, (NDEV, D, IN_W // NDEV), split=IN_W // NDEV,
                           small=small + [dconv_w[j:j + 1] for j in range(4)], rope=(rc, ra, rb), n_roped=2)
    dx, dg1_sum, r_in = _in_proj_bwd(dparts, 2, (rc, ra, rb), dh1, x, g1, w_in_p, dw_in8)
    recv = dict(w_in=r_in, w_out=r_out, w_up=r_up, w_down=r_down, w_ple_gate=r_pg, w_ple=r_ple)
    return dx, recv, total, dg1_sum


def _gather_weights(shards, dtypes):
    nw = len(shards)

    def body(*refs):
        ins, parts = refs[:nw], refs[nw:nw + 4]
        outs, tables = refs[nw + 4:2 * nw + 4], refs[2 * nw + 4:2 * nw + 7]
        start, forward, finish = _gather_phases(ins, outs, refs[2 * nw + 7:3 * nw + 7], *refs[3 * nw + 7:])
        start()
        _rope_fill(*parts, *tables)
        forward()
        finish()

    res = pl.pallas_call(
        body, name="gather_weights",
        in_specs=[VM] * (nw + 4), out_specs=[ANY] * nw + [VM] * 3,
        out_shape=_gather_shapes(shards, dtypes) + [jax.ShapeDtypeStruct((S, 128), F32)] * 3,
        scratch_shapes=_gather_scratch(shards, dtypes),
        compiler_params=_params(),
    )(*shards, *_rope_parts())
    return res[:nw], res[nw:]


ADAM_STEPS = 4


def _adamw(items):
    n = len(items)

    def body(*refs):
        for i in range(n):
            g_ref, w_ref, m_ref, v_ref = refs[4 * i:4 * i + 4]
            go_ref, d_ref, mo_ref, vo_ref = refs[4 * n + 4 * i:4 * n + 4 * i + 4]
            g = g_ref[0].astype(F32)
            for j in range(1, g_ref.shape[0]):
                g = g + g_ref[j].astype(F32)
            go_ref[...] = g
            d_ref[...], mo_ref[...], vo_ref[...] = _adam_update(g, w_ref[...], m_ref[...], v_ref[...])

    in_specs, out_specs, out_shape, args = [], [], [], []
    for gparts, w, m, v in items:
        P, R, C = gparts.shape
        if R % (8 * ADAM_STEPS) == 0:
            tr = R // ADAM_STEPS
            row, gspec = pl.BlockSpec((tr, C), lambda i: (i, 0)), pl.BlockSpec((P, tr, C), lambda i: (0, i, 0))
        else:
            row, gspec = pl.BlockSpec((R, C), lambda i: (0, 0)), pl.BlockSpec((P, R, C), lambda i: (0, 0, 0))
        in_specs += [gspec, row, row, row]
        out_specs += [row] * 4
        out_shape += [jax.ShapeDtypeStruct((R, C), F32)] * 4
        args += [gparts, w, m, v]
    res = pl.pallas_call(
        body, name="adamw", grid=(ADAM_STEPS,), in_specs=in_specs, out_specs=out_specs, out_shape=out_shape,
        compiler_params=_params(1),
    )(*args)
    return [res[4 * i:4 * i + 4] for i in range(n)]


SMALL = ("norm_mix_g", "conv_b", "gate_b", "mlstm_norm_g", "norm_mlp_g", "norm_ple_g", "final_norm_g")


def _adam_update(g, w, m, v):
    c1 = 1.0 - ADAM_B1 ** ADAM_STEP
    c2 = 1.0 - ADAM_B2 ** ADAM_STEP
    m2 = ADAM_B1 * m + (1.0 - ADAM_B1) * g
    v2 = ADAM_B2 * v + (1.0 - ADAM_B2) * (g * g)
    return -ADAM_LR * ((m2 / c1) / (jnp.sqrt(v2 / c2) + ADAM_EPS) + ADAM_WD * w), m2, v2


def _adamw_small(total, first, ws, ms, vs):
    n = len(ws)

    def body(*refs):
        t_ref, f_ref = refs[:2]
        refs = refs[1:]
        outs = refs[1 + 3 * n:]
        for i in range(n):
            w_ref, m_ref, v_ref = refs[1 + i], refs[1 + n + i], refs[1 + 2 * n + i]
            g = (t_ref if i else f_ref)[8 * i:8 * i + 1, 0:w_ref.shape[1]]
            delta, m2, v2 = _adam_update(g, w_ref[...], m_ref[...], v_ref[...])
            for ref, val in zip(outs[4 * i:4 * i + 4], (g, delta, m2, v2)):
                ref[...] = val

    res = pl.pallas_call(
        body, name="adamw_small",
        out_shape=[jax.ShapeDtypeStruct(w.shape, F32) for w in ws for _ in range(4)],
        compiler_params=_params(),
    )(total, first, *ws, *ms, *vs)
    return [res[4 * i:4 * i + 4] for i in range(n)]


def kernel(x, p, norm_mix_g, w_in, conv_w, conv_b, gate_b, mlstm_norm_g, w_out, norm_mlp_g, w_up, w_down, norm_ple_g, w_ple_gate, w_ple, final_norm_g, loss_target, m_norm_mix_g, m_w_in, m_conv_w, m_conv_b, m_gate_b, m_mlstm_norm_g, m_w_out, m_norm_mlp_g, m_w_up, m_w_down, m_norm_ple_g, m_w_ple_gate, m_w_ple, m_final_norm_g, v_norm_mix_g, v_w_in, v_conv_w, v_conv_b, v_gate_b, v_mlstm_norm_g, v_w_out, v_norm_mlp_g, v_w_up, v_w_down, v_norm_ple_g, v_w_ple_gate, v_w_ple, v_final_norm_g):
    big_names = ("w_in", "conv_w", "w_out", "w_up", "w_down", "w_ple_gate", "w_ple")
    wts = dict(w_in=w_in, conv_w=conv_w, w_out=w_out, w_up=w_up, w_down=w_down, w_ple_gate=w_ple_gate, w_ple=w_ple)
    mom = dict(w_in=m_w_in, conv_w=m_conv_w, w_out=m_w_out, w_up=m_w_up, w_down=m_w_down, w_ple_gate=m_w_ple_gate,
               w_ple=m_w_ple)
    var = dict(w_in=v_w_in, conv_w=v_conv_w, w_out=v_w_out, w_up=v_w_up, w_down=v_w_down, w_ple_gate=v_w_ple_gate,
               w_ple=v_w_ple)
    sq = lambda a: a.reshape(a.shape[1:])
    fin = final_norm_g.reshape(1, D)
    dx, recv, total, first = _step(
        x[0], p[0, 0], loss_target[0], norm_mix_g, conv_b, jnp.pad(gate_b, ((0, 0), (0, 120))), mlstm_norm_g,
        norm_mlp_g, norm_ple_g, fin, {n: sq(wts[n]) for n in big_names})

    nrow = 8 * len(SMALL)
    me = _dev_index(*_place())
    conv_rows = total[nrow + 8:nrow + 40:8]
    recv["conv_w"] = lax.dynamic_slice_in_dim(conv_rows, me * 128, 128, axis=1).reshape(1, 4, 128)
    out = {}
    for n, res in zip(big_names, _adamw([(recv[n], sq(wts[n]), sq(mom[n]), sq(var[n])) for n in big_names])):
        out[n] = [t.reshape(wts[n].shape) for t in res]
    sw = dict(norm_mix_g=norm_mix_g, conv_b=conv_b, gate_b=gate_b, mlstm_norm_g=mlstm_norm_g, norm_mlp_g=norm_mlp_g,
              norm_ple_g=norm_ple_g, final_norm_g=fin)
    sm = dict(norm_mix_g=m_norm_mix_g, conv_b=m_conv_b, gate_b=m_gate_b, mlstm_norm_g=m_mlstm_norm_g,
              norm_mlp_g=m_norm_mlp_g, norm_ple_g=m_norm_ple_g, final_norm_g=m_final_norm_g.reshape(1, D))
    sv = dict(norm_mix_g=v_norm_mix_g, conv_b=v_conv_b, gate_b=v_gate_b, mlstm_norm_g=v_mlstm_norm_g,
              norm_mlp_g=v_norm_mlp_g, norm_ple_g=v_norm_ple_g, final_norm_g=v_final_norm_g.reshape(1, D))
    res = _adamw_small(total, first, [sw[n] for n in SMALL], [sm[n] for n in SMALL], [sv[n] for n in SMALL])
    for n, r in zip(SMALL, res):
        out[n] = [t.reshape(final_norm_g.shape) for t in r] if n == "final_norm_g" else list(r)
    order = ("norm_mix_g", "w_in", "conv_w", "conv_b", "gate_b", "mlstm_norm_g", "w_out", "norm_mlp_g", "w_up", "w_down",
             "norm_ple_g", "w_ple_gate", "w_ple", "final_norm_g")
    loss_all = total[nrow, 0]
    return (loss_all, dx[None], *[out[n][0] for n in order], *[out[n][1] for n in order],
            *[out[n][2] for n in order], *[out[n][3] for n in order])
```

```python
import math

import jax
import jax.numpy as jnp
from jax import lax
from jax.experimental import pallas as pl
from jax.experimental.pallas import tpu as pltpu

F32, BF16 = jnp.float32, jnp.bfloat16
S = 4096
D = 1024
AW = 512
MW = 512
DFF = 4096
PLE = 256
IN_W = 3592
PW = 3840
NDEV = 8
EPS = 1e-6
NEG = -1e30
LC = 128
TB = 256
ROPE_THETA = 500000.0
VMEM_LIMIT = 56 * 1024 * 1024
HI = lax.Precision.HIGHEST

ADAM_LR, ADAM_B1, ADAM_B2, ADAM_EPS, ADAM_WD, ADAM_STEP = 0.001, 0.9, 0.999, 1e-08, 0.01, 10


def _params(n_grid=0, **kw):
    sem = dict(dimension_semantics=("arbitrary",) * n_grid) if n_grid else {}
    return pltpu.CompilerParams(vmem_limit_bytes=VMEM_LIMIT, **sem, **kw)


def _cspec(shape):
    nd = len(shape)
    return pl.BlockSpec(shape, lambda *_: (0,) * nd, pipeline_mode=pl.Buffered(1))


def _dot(a, b):
    return jnp.dot(a, b, preferred_element_type=F32)


def _dot_nt(a, b):
    return lax.dot_general(a, b, (((1,), (1,)), ((), ())), preferred_element_type=F32)


def _dot_tn(a, b):
    return lax.dot_general(a, b, (((0,), (0,)), ((), ())), preferred_element_type=F32)


def _bf(x):
    return x.astype(BF16)


def _rms(x):
    rs = lax.rsqrt(jnp.mean(x * x, axis=-1, keepdims=True) + EPS)
    return x * rs, rs


def _rms_bwd(du, n, rs, g):
    dn = du * g
    return rs * (dn - n * jnp.mean(dn * n, axis=-1, keepdims=True))


def _sigmoid(x):
    return 1.0 / (1.0 + jnp.exp(-x))


ROPE_BLK = 512


def _rope_parts():
    def cs(n, step):
        j = lax.broadcasted_iota(jnp.int32, (n, 128), 1) % 64
        pos = (lax.broadcasted_iota(jnp.int32, (n, 128), 0) * step).astype(F32)
        ang = pos * jnp.power(ROPE_THETA, -(j % 8).astype(F32) / 8.0)
        return jnp.cos(ang), jnp.sin(ang)

    return (*cs(ROPE_BLK, 1), *cs(S // ROPE_BLK, ROPE_BLK))


def _rope_fill(co_ref, so_ref, cb_ref, sb_ref, rc_ref, ra_ref, rb_ref):
    j = lax.broadcasted_iota(jnp.int32, (ROPE_BLK, 128), 1) % 64
    co, so = co_ref[...], so_ref[...]
    for t in range(S // ROPE_BLK):
        cb, sb = cb_ref[t:t + 1, :], sb_ref[t:t + 1, :]
        cos, sin = cb * co - sb * so, sb * co + cb * so
        rows = slice(t * ROPE_BLK, (t + 1) * ROPE_BLK)
        rc_ref[rows, :] = jnp.where(j < 16, cos, 1.0)
        ra_ref[rows, :] = jnp.where(j < 8, -sin, 0.0)
        rb_ref[rows, :] = jnp.where((j >= 8) & (j < 16), sin, 0.0)


def _rope(blk, c, a, b):
    return blk * c + pltpu.roll(blk, 120, 1) * a + pltpu.roll(blk, 8, 1) * b


def _rope_bwd(d, c, a, b):
    return d * c + pltpu.roll(d * a, 8, 1) + pltpu.roll(d * b, 120, 1)


def _unrope(t, c, a, b):
    return jnp.concatenate([_bf(_rope_bwd(t[:, j * 128:(j + 1) * 128].astype(F32), c, a, b))
                            for j in range(t.shape[1] // 128)], axis=1)


MESH = pl.DeviceIdType.MESH
ANY = pl.BlockSpec(memory_space=pl.ANY)
VM = pl.BlockSpec(memory_space=pltpu.VMEM)
FLIPS = [(dx, dy, dc) for dx in (0, 1) for dy in (0, 1) for dc in (0, 1)][1:]


def _place():
    return lax.axis_index("x"), lax.axis_index("y"), lax.axis_index("c")


def _dev_index(px, py, pc):
    return 4 * px + 2 * py + pc


def _gather_phases(ins, outs, bufs, send_sems=None, recv_sems=None, local_sems=None):
    nw = len(ins)
    if nw == 0:
        return (lambda: None,) * 3
    x, y, c = _place()
    me, sib = (x, y, c), (x, y, 1 - c)
    chips = [(1 - x, y), (x, 1 - y), (1 - x, 1 - y)]

    def copy(w, k, block, to, from_buf=False):
        dst = outs[w].at[_dev_index(*block)]
        return pltpu.make_async_remote_copy(
            src_ref=bufs[w] if from_buf else dst, dst_ref=dst, send_sem=send_sems.at[w, k],
            recv_sem=recv_sems.at[w, k], device_id=to, device_id_type=MESH)

    def mine(w):
        return pltpu.make_async_copy(bufs[w], outs[w].at[_dev_index(*me)], local_sems.at[w])

    def first(w):
        return [copy(w, 0, me, sib, True)] + [copy(w, 1 + j, me, (*chip, c), True) for j, chip in enumerate(chips)]

    def passed(w):
        return [copy(w, 4 + j, (*chip, c), sib) for j, chip in enumerate(chips)]

    def start():
        for w in range(nw):
            bufs[w][...] = ins[w][...].astype(bufs[w].dtype)
        for w in range(nw):
            mine(w).start()
            for cp in first(w):
                cp.start()

    def forward():
        for j, chip in enumerate(chips):
            for w in range(nw):
                copy(w, 1 + j, (*chip, c), me).wait_recv()
                passed(w)[j].start()

    def finish():
        for w in range(nw):
            copy(w, 0, sib, me).wait_recv()
        for j, chip in enumerate(chips):
            for w in range(nw):
                copy(w, 4 + j, (*chip, 1 - c), me).wait_recv()
        for w in range(nw):
            for cp in first(w) + passed(w):
                cp.wait_send()
            mine(w).wait()

    return start, forward, finish


def _gather_scratch(shards, dtypes):
    nw = len(shards)
    if nw == 0:
        return []
    return ([pltpu.VMEM(s.shape, dt) for s, dt in zip(shards, dtypes)]
            + [pltpu.SemaphoreType.DMA((nw, 7)), pltpu.SemaphoreType.DMA((nw, 7)), pltpu.SemaphoreType.DMA((nw,))])


def _gather_shapes(shards, dtypes):
    return [jax.ShapeDtypeStruct((NDEV, *s.shape), dt) for s, dt in zip(shards, dtypes)]


def _scatter_phases(ins, outs, send_sems=None, recv_sems=None, local_sems=None):
    nw = len(ins)
    if nw == 0:
        return (lambda: None,) * 2
    x, y, c = _place()
    me = _dev_index(x, y, c)

    def copies():
        out = []
        for w in range(nw):
            out.append(pltpu.make_async_copy(ins[w].at[me], outs[w].at[me], local_sems.at[w]))
            for k, (dx, dy, dc) in enumerate(FLIPS):
                peer = ((x + dx) % 2, (y + dy) % 2, (c + dc) % 2)
                out.append(pltpu.make_async_remote_copy(
                    src_ref=ins[w].at[_dev_index(*peer)], dst_ref=outs[w].at[me], send_sem=send_sems.at[w, k],
                    recv_sem=recv_sems.at[w, k], device_id=peer, device_id_type=MESH))
        return out

    def start():
        for cp in copies():
            cp.start()

    def finish():
        for cp in copies():
            cp.wait()

    return start, finish


def _scatter_scratch(nw):
    if nw == 0:
        return []
    return [pltpu.SemaphoreType.DMA((nw, 7)), pltpu.SemaphoreType.DMA((nw, 7)), pltpu.SemaphoreType.DMA((nw,))]


TM = 512


def _join_w_in(wg):
    sw = IN_W // NDEV

    def body(wg_ref, w_ref):
        for j in range(NDEV):
            w_ref[:, sw * j:sw * (j + 1)] = wg_ref[j]
        w_ref[:, IN_W:PW] = jnp.zeros((D, PW - IN_W), BF16)

    return pl.pallas_call(body, name="join_w_in", out_shape=jax.ShapeDtypeStruct((D, PW), BF16),
                          compiler_params=_params())(wg)


def _in_proj(x, g1, w, rc, ra, rb, shards, dtypes):
    tm = TM
    nw = len(shards)
    nt = S // tm

    def body(*refs):
        x_ref, g_ref, w_ref, rc_ref, ra_ref, rb_ref = refs[:6]
        ins = refs[6:6 + nw]
        qkv_ref, mqk_ref, mv_ref, mo_ref, gt_ref, u_ref = refs[6 + nw:12 + nw]
        outs = refs[12 + nw:12 + 2 * nw]
        bufs = refs[12 + 2 * nw:12 + 3 * nw]
        ag_start, ag_forward, ag_finish = _gather_phases(ins, outs, bufs, *refs[12 + 3 * nw:])
        i = pl.program_id(0)
        pl.when(i == 0)(ag_start)
        pl.when(i == nt - 2)(ag_forward)
        n, _ = _rms(x_ref[...])
        u = _bf(n * g_ref[...])
        u_ref[...] = u
        c, a, b = rc_ref[...], ra_ref[...], rb_ref[...]
        for half in range(2):
            blk = _dot(u, w_ref[:, half * 512:(half + 1) * 512])
            for t in range(4):
                lo = half * 512 + t * 128
                qkv_ref[:, lo:lo + 128] = _rope(blk[:, t * 128:(t + 1) * 128], c, a, b)
        qkv_ref[:, 1024:1536] = _dot(u, w_ref[:, 1024:1536])
        mqk_ref[:, 0:512] = _dot(u, w_ref[:, 1536:2048])
        mqk_ref[:, 512:1024] = _dot(u, w_ref[:, 2048:2560])
        mv_ref[...] = _dot(u, w_ref[:, 2560:3072])
        mo_ref[...] = _dot(u, w_ref[:, 3072:3584])
        gt_ref[...] = _dot(u, w_ref[:, 3584:3712])
        pl.when(i == nt - 1)(ag_finish)

    row = lambda wd: pl.BlockSpec((tm, wd), lambda i: (i, 0))
    res = pl.pallas_call(
        body, name="in_proj", grid=(nt,),
        in_specs=[row(D), _cspec((1, D)), _cspec((D, PW)), row(128), row(128), row(128)] + [VM] * nw,
        out_specs=[row(1536), row(1024), row(512), row(512), row(128), row(D)] + [ANY] * nw,
        out_shape=[jax.ShapeDtypeStruct((S, 1536), F32), jax.ShapeDtypeStruct((S, 1024), F32),
                   jax.ShapeDtypeStruct((S, 512), F32), jax.ShapeDtypeStruct((S, 512), F32),
                   jax.ShapeDtypeStruct((S, 128), F32), jax.ShapeDtypeStruct((S, D), BF16)]
        + _gather_shapes(shards, dtypes),
        scratch_shapes=_gather_scratch(shards, dtypes),
        compiler_params=_params(1),
    )(x, g1, w, rc, ra, rb, *shards)
    return res[:6], res[6:]


DILATIONS = (16, 4, 1)


def _attn_valid(n):
    kd = lax.broadcasted_iota(jnp.int32, (128, 256), 1) - lax.broadcasted_iota(jnp.int32, (128, 256), 0)
    off = jnp.where(n == 0, 0, 128)
    return (kd <= off) & (kd >= off - 128)


def _attn_rows(d, r, n):
    if d == 1:
        q0 = pl.multiple_of(n * 128, 128)
        k0 = pl.multiple_of(jnp.maximum(n - 1, 0) * 128, 128)
        return pl.ds(q0, 128), pl.ds(k0, 256), _attn_valid(n)
    q0 = r + n * 128 * d
    k0 = r + jnp.maximum(n - 1, 0) * 128 * d
    return pl.ds(q0, 128, stride=d), pl.ds(k0, 256, stride=d), _attn_valid(n)


ATTN_GROUP = 4
ATTN_ITERS = S // 128 // ATTN_GROUP


def _attn_group(d, i):
    nb = S // (128 * d)
    if nb == 2:
        qi = lax.broadcasted_iota(jnp.int32, (256, 256), 0) - lax.broadcasted_iota(jnp.int32, (256, 256), 1)
        whole = [pl.ds((ATTN_GROUP // 2) * i + u, 256, stride=d) for u in range(ATTN_GROUP // 2)]
        return [(rows, rows, (qi >= 0) & (qi <= 128)) for rows in whole]
    if d == 1:
        return [_attn_rows(1, 0, i + ATTN_ITERS * u) for u in range(ATTN_GROUP)]
    return [_attn_rows(d, (i // nb) * ATTN_GROUP + u, i % nb) for u in range(ATTN_GROUP)]


def _head0(shape):
    return lax.broadcasted_iota(jnp.int32, shape, 1) < 64


def _stack_heads(t):
    h0 = _head0(t.shape)
    tb = _bf(t)
    zero = jnp.zeros_like(tb)
    return jnp.concatenate([jnp.where(h0, tb, zero), jnp.where(h0, zero, tb)], axis=0)


def _attn_fwd(qkv, shards, dtypes):
    nw = len(shards)

    def body(*refs):
        q_ref, k_ref, v_ref = refs[:3]
        ins = refs[3:3 + nw]
        o_ref, lse0_ref, lse1_ref = refs[3 + nw:6 + nw]
        outs = refs[6 + nw:6 + 2 * nw]
        m0, m1, l0, l1, acc = refs[6 + 2 * nw:11 + 2 * nw]
        bufs = refs[11 + 2 * nw:11 + 3 * nw]
        ag_start, ag_forward, ag_finish = _gather_phases(ins, outs, bufs, *refs[11 + 3 * nw:])
        hp = pl.program_id(0)
        pl.when(hp == 0)(ag_start)
        pl.when(hp == 3)(ag_forward)
        stats = (m0, m1, l0, l1, acc)

        def update(blocks, first):
            loaded = [([q_ref[rq, :], k_ref[rk, :], v_ref[rk, :]], None if first else [ref[rq, :] for ref in stats])
                      for rq, rk, _ in blocks]
            both = lambda a, b: jnp.concatenate([a, b], axis=0)
            ss = [jnp.where(both(valid, valid), _dot_nt(_stack_heads(q * 0.125), _bf(k)), NEG)
                  for ((q, k, _), _), (_, _, valid) in zip(loaded, blocks)]
            mcs = [jnp.max(s, axis=-1, keepdims=True) for s in ss]
            if first:
                m2s = [jnp.broadcast_to(mc, (mc.shape[0], 128)) for mc in mcs]
            else:
                m2s = [jnp.maximum(both(prev[0], prev[1]), mc) for mc, (_, prev) in zip(mcs, loaded)]
            ps = [jnp.exp(s - jnp.tile(m2, (1, 2))) for s, m2 in zip(ss, m2s)]
            l2s = [jnp.sum(p, axis=-1, keepdims=True) for p in ps]
            acc2s = [_dot(_bf(p), _bf(v)) for p, ((_, _, v), _) in zip(ps, loaded)]
            results = []
            for m2, l2, acc2, (_, prev) in zip(m2s, l2s, acc2s, loaded):
                nq = m2.shape[0] // 2
                if first:
                    l2 = jnp.broadcast_to(l2, (2 * nq, 128))
                else:
                    alpha = jnp.exp(both(prev[0], prev[1]) - m2)
                    l2, acc2 = alpha * both(prev[2], prev[3]) + l2, alpha * both(prev[4], prev[4]) + acc2
                results.append((m2[0:nq], m2[nq:2 * nq], l2[0:nq], l2[nq:2 * nq],
                                jnp.where(_head0((nq, 128)), acc2[0:nq], acc2[nq:2 * nq])))
            for (rq, _, _), res in zip(blocks, results):
                for ref, val in zip(stats, res):
                    ref[rq, :] = val

        for d in DILATIONS:
            def step(i, carry, d=d):
                update(_attn_group(d, i), d == DILATIONS[0])
                return carry

            lax.fori_loop(0, ATTN_ITERS, step, 0)

        def fin(t, carry):
            rows = pl.ds(pl.multiple_of(t * 256, 256), 256)
            h0 = lax.broadcasted_iota(jnp.int32, (256, 128), 1) < 64
            la, lb = l0[rows, :], l1[rows, :]
            o_ref[rows, :] = acc[rows, :] / jnp.where(h0, la, lb)
            lse0_ref[rows, :] = m0[rows, :] + jnp.log(la)
            lse1_ref[rows, :] = m1[rows, :] + jnp.log(lb)
            return carry

        lax.fori_loop(0, S // 256, fin, 0)
        pl.when(hp == 3)(ag_finish)

    col = lambda off: pl.BlockSpec((S, 128), lambda h, off=off: (0, off + h))
    res = pl.pallas_call(
        body, name="attn_fwd", grid=(4,),
        in_specs=[col(0), col(4), col(8)] + [VM] * nw,
        out_specs=[col(0), col(0), col(0)] + [ANY] * nw,
        out_shape=[jax.ShapeDtypeStruct((S, AW), F32)] * 3 + _gather_shapes(shards, dtypes),
        scratch_shapes=[pltpu.VMEM((S, 128), F32)] * 5 + _gather_scratch(shards, dtypes),
        compiler_params=_params(1),
    )(qkv, qkv, qkv, *shards)
    return res[0], (res[1], res[2]), res[3:]


def _attn_bwd(qkv, o, lse, do, parts):
    nw = len(parts)

    def body(*refs):
        q_ref, k_ref, v_ref, o_ref, L0, L1, do_ref = refs[:7]
        ins = refs[7:7 + nw]
        dq_out, dk_out, dv_out = refs[7 + nw:10 + nw]
        outs = refs[10 + nw:10 + 2 * nw]
        D0, D1, dq_ref, dk_ref, dv_ref = refs[10 + 2 * nw:15 + 2 * nw]
        rs_start, rs_finish = _scatter_phases(ins, outs, *refs[15 + 2 * nw:])
        hp = pl.program_id(0)
        pl.when(hp == 0)(rs_start)

        def pre(t, carry):
            rows = pl.ds(pl.multiple_of(t * 256, 256), 256)
            h0 = lax.broadcasted_iota(jnp.int32, (256, 128), 1) < 64
            dd = do_ref[rows, :] * o_ref[rows, :]
            shp = (256, 128)
            D0[rows, :] = jnp.broadcast_to(jnp.sum(jnp.where(h0, dd, 0.0), axis=-1, keepdims=True), shp)
            D1[rows, :] = jnp.broadcast_to(jnp.sum(jnp.where(h0, 0.0, dd), axis=-1, keepdims=True), shp)
            return carry

        lax.fori_loop(0, S // 256, pre, 0)

        def update(blocks, first):
            loaded = [([q_ref[rq, :], k_ref[rk, :], v_ref[rk, :], do_ref[rq, :]],
                       [L0[rq, :], L1[rq, :], D0[rq, :], D1[rq, :]],
                       [0.0] * 3 if first else [dq_ref[rq, :], dk_ref[rk, :], dv_ref[rk, :]]) for rq, rk, _ in blocks]
            cat = lambda a, b: jnp.tile(jnp.concatenate([a, b], axis=0), (1, 2))
            ops = [(_stack_heads(q * 0.125), _stack_heads(q), _stack_heads(dout), _bf(k), _bf(v))
                   for (q, k, v, dout), _, _ in loaded]
            ss = [jnp.where(jnp.concatenate([valid, valid], axis=0), _dot_nt(qs, kb), NEG)
                  for (qs, _, _, kb, _), (_, _, valid) in zip(ops, blocks)]
            dps = [_dot_nt(do2, vb) for _, _, do2, _, vb in ops]
            ps = [jnp.exp(s - cat(st[0], st[1])) for s, (_, st, _) in zip(ss, loaded)]
            dss = [_bf(p * (dp - cat(st[2], st[3])) * 0.125) for p, dp, (_, st, _) in zip(ps, dps, loaded)]
            dq2s = [_dot(ds, kb) for ds, (_, _, _, kb, _) in zip(dss, ops)]
            dks = [_dot_tn(ds, q2) for ds, (_, q2, _, _, _) in zip(dss, ops)]
            dvs = [_dot_tn(_bf(p), do2) for p, (_, _, do2, _, _) in zip(ps, ops)]
            results = []
            for (_, _, (dq, dk, dv)), dq2, dkk, dvv in zip(loaded, dq2s, dks, dvs):
                nq = dq2.shape[0] // 2
                results.append((dq + jnp.where(_head0((nq, 128)), dq2[0:nq], dq2[nq:2 * nq]), dk + dkk, dv + dvv))
            for (rq, rk, _), (dq, dk, dv) in zip(blocks, results):
                dq_ref[rq, :] = dq
                dk_ref[rk, :] = dk
                dv_ref[rk, :] = dv

        assert S // (128 * DILATIONS[0]) == 2
        for d in DILATIONS:
            def step(i, carry, d=d):
                update(_attn_group(d, i), d == DILATIONS[0])
                return carry

            lax.fori_loop(0, ATTN_ITERS, step, 0)

        def fin(t, carry):
            rows = pl.ds(pl.multiple_of(t * 256, 256), 256)
            for src, dst in ((dq_ref, dq_out), (dk_ref, dk_out), (dv_ref, dv_out)):
                dst[rows, :] = _bf(src[rows, :])
            return carry

        lax.fori_loop(0, S // 256, fin, 0)
        pl.when(hp == 3)(rs_finish)

    col = lambda off: pl.BlockSpec((S, 128), lambda h, off=off: (0, off + h))
    res = pl.pallas_call(
        body, name="attn_bwd", grid=(4,),
        in_specs=[col(0), col(4), col(8), col(0), col(0), col(0), col(0)] + [ANY] * nw,
        out_specs=[col(0), col(0), col(0)] + [ANY] * nw,
        out_shape=[jax.ShapeDtypeStruct((S, AW), BF16)] * 3 + [jax.ShapeDtypeStruct(a.shape, a.dtype) for a in parts],
        scratch_shapes=[pltpu.VMEM((S, 128), F32)] * 5 + _scatter_scratch(nw),
        compiler_params=_params(1),
    )(qkv, qkv, qkv, o, lse[0], lse[1], do, *parts)
    return res[0], res[1], res[2], res[3:]


def _logsig(x):
    return jnp.minimum(x, 0.0) - jnp.log1p(jnp.exp(-jnp.abs(x)))


def _conv_taps(xp, n):
    return [xp[8:] if j == 3 else pltpu.roll(xp, 3 - j, 0)[8:] for j in range(4)]


def _conv_silu(xp, w_ref, b_ref, n):
    taps = _conv_taps(xp, n)
    c = b_ref[...] + sum(w_ref[j:j + 1, :] * taps[j] for j in range(4))
    sg = _sigmoid(c)
    return c, sg, taps


def _chunk_gates(G):
    assert LC == 128
    r = lax.broadcasted_iota(jnp.int32, (LC, LC), 0)
    c = lax.broadcasted_iota(jnp.int32, (LC, LC), 1)
    tril = (c <= r).astype(F32)
    triu = (c >= r).astype(F32)
    b_col = jnp.dot(tril, _logsig(G), preferred_element_type=F32, precision=HI)
    return b_col, b_col.T, G.T, tril, triu


def _colpick(X, lane):
    li = lax.broadcasted_iota(jnp.int32, X.shape, 1)
    return jnp.sum(jnp.where(li == lane, X, 0.0), axis=1, keepdims=True)


def _rowpick(XT, row):
    ri = lax.broadcasted_iota(jnp.int32, XT.shape, 0)
    return jnp.sum(jnp.where(ri == row, XT, 0.0), axis=0, keepdims=True)


def _each(f, *lists):
    return [f(*a) for a in zip(*lists)]


def _mlstm_heads(Q, K, V, G, b_col, b_row, g_row, C, N, M):
    hs = range(len(Q))
    bt = [_colpick(b_col, 4 + h) for h in hs]
    i_col = [_colpick(G, h) for h in hs]
    bs = [_rowpick(b_row, 4 + h) for h in hs]
    i_row = [_rowpick(g_row, h) for h in hs]
    r = lax.broadcasted_iota(jnp.int32, (LC, LC), 0)
    c = lax.broadcasted_iota(jnp.int32, (LC, LC), 1)
    lane = lax.broadcasted_iota(jnp.int32, (1, LC), 1)
    qb, kb, vb = [_bf(t) for t in Q], [_bf(t) for t in K], [_bf(t) for t in V]
    S_ = _each(_dot_nt, qb, kb)
    qC = _each(lambda q, ch: _dot(q, _bf(ch)), qb, C)
    log_d = _each(lambda a, b, i: jnp.where(c <= r, a - b + i, NEG), bt, bs, i_row)
    log_inter = _each(lambda a, m: a + m, bt, M)
    m_t = _each(lambda li, ld: jnp.maximum(li, jnp.max(ld, axis=1, keepdims=True)), log_inter, log_d)
    Dm = _each(lambda ld, m: jnp.exp(ld - m), log_d, m_t)
    g = _each(lambda li, m: jnp.exp(li - m), log_inter, m_t)
    Am = _each(lambda s, d: s * d, S_, Dm)
    AV = _each(lambda a, v: _dot(_bf(a), v), Am, vb)
    num = _each(lambda gg, qc, av: gg * qc + av, g, qC, AV)
    qn = _each(lambda q, n: jnp.sum(q * n, axis=1, keepdims=True), Q, N)
    den = _each(lambda gg, x, a: gg * x + jnp.sum(a, axis=1, keepdims=True), g, qn, Am)
    floor = [jnp.exp(-m) for m in m_t]
    inv_dd = _each(lambda d, f: 1.0 / jnp.maximum(jnp.abs(d), f), den, floor)
    hh = _each(lambda n, i: n * i, num, inv_dd)
    blast = [jnp.sum(jnp.where(lane == LC - 1, b, 0.0), axis=1, keepdims=True) for b in bs]
    log_s = _each(lambda bl, a, i: bl - a + i, blast, bt, i_col)
    m_new = _each(lambda bl, m, ls: jnp.maximum(bl + m, jnp.max(ls, axis=0, keepdims=True)), blast, M, log_s)
    decay = _each(lambda bl, m, mn: jnp.exp(bl + m - mn), blast, M, m_new)
    ws = _each(lambda ls, mn: jnp.exp(ls - mn), log_s, m_new)
    kw = _each(lambda k, w: k * w, K, ws)
    KV = _each(lambda k, v: _dot_tn(_bf(k), v), kw, vb)
    C_new = _each(lambda d, ch, kv: d * ch + kv, decay, C, KV)
    n_new = _each(lambda d, n, k: d * n + jnp.sum(k, axis=0, keepdims=True), decay, N, kw)
    return dict(Dm=Dm, g=g, Am=Am, qC=qC, qn=qn, den=den, floor=floor, inv_dd=inv_dd, h=hh, decay=decay, ws=ws, kw=kw,
                C_new=C_new, n_new=n_new, m_new=m_new, qb=qb, kb=kb, vb=vb)


def _head_out(hh, mo_h, gn_h):
    r = lax.rsqrt(jnp.mean(hh * hh, axis=-1, keepdims=True) + EPS)
    hn = hh * r
    sg = _sigmoid(mo_h)
    return sg * (hn * gn_h), hn, r, sg


def _mlstm_fwd(mqk, mv, mo, gates, conv_w, conv_b, gate_b, gn, shards, dtypes):
    nblk = S // TB
    ncb = TB // LC
    nw = len(shards)

    def body(*refs):
        x_ref, v_ref, o_ref, g_ref, w_ref, b_ref, gb_ref, gn_ref = refs[:8]
        ins = refs[8:8 + nw]
        out_ref, cs_ref, ns_ref, ms_ref = refs[8 + nw:12 + nw]
        outs = refs[12 + nw:12 + 2 * nw]
        tail, Cst, nst, mst, qs, ks = refs[12 + 2 * nw:18 + 2 * nw]
        bufs = refs[18 + 2 * nw:18 + 3 * nw]
        ag_start, ag_forward, ag_finish = _gather_phases(ins, outs, bufs, *refs[18 + 3 * nw:])
        i = pl.program_id(0)
        pl.when(i == 0)(ag_start)
        pl.when(i == nblk // 2)(ag_forward)

        @pl.when(i == 0)
        def _():
            tail[...] = jnp.zeros_like(tail)
            Cst[...] = jnp.zeros_like(Cst)
            nst[...] = jnp.zeros_like(nst)
            mst[...] = jnp.zeros_like(mst)

        x = x_ref[...]
        xp = jnp.concatenate([tail[...], x], axis=0)
        tail[...] = x[TB - 8:TB, :]
        c, sg, _ = _conv_silu(xp, w_ref, b_ref, TB)
        y = c * sg
        qs[...] = y[:, 0:MW]
        ks[...] = y[:, MW:2 * MW] * (1.0 / math.sqrt(128.0))

        for cc in range(ncb):
            rows = slice(cc * LC, (cc + 1) * LC)
            G = g_ref[rows, :] + gb_ref[...]
            b_col, b_row, g_row, _, _ = _chunk_gates(G)
            cs_ref[cc] = Cst[...]
            ns_ref[cc] = nst[...]
            ms_ref[cc] = mst[...]
            lns = [slice(h * 128, (h + 1) * 128) for h in range(4)]
            f = _mlstm_heads([qs[rows, ln] for ln in lns], [ks[rows, ln] for ln in lns], [v_ref[rows, ln] for ln in lns],
                             G, b_col, b_row, g_row, [Cst[:, ln] for ln in lns], [nst[0:1, ln] for ln in lns],
                             [jnp.max(mst[0:1, ln], axis=1, keepdims=True) for ln in lns])
            outs = [_head_out(hh, o_ref[rows, ln], gn_ref[:, ln])[0] for hh, ln in zip(f["h"], lns)]
            for h, ln in enumerate(lns):
                out_ref[rows, ln] = outs[h]
                Cst[:, ln] = f["C_new"][h]
                nst[0:1, ln] = f["n_new"][h]
                mst[0:1, ln] = jnp.broadcast_to(f["m_new"][h], (1, 128))
        pl.when(i == nblk - 1)(ag_finish)

    row = lambda wd: pl.BlockSpec((TB, wd), lambda i: (i, 0))
    res = pl.pallas_call(
        body, name="mlstm_fwd", grid=(nblk,),
        in_specs=[row(1024), row(MW), row(MW), row(128), _cspec((4, 1024)), _cspec((1, 1024)), _cspec((1, 128)),
                  _cspec((1, MW))] + [VM] * nw,
        out_specs=[row(MW), pl.BlockSpec((ncb, 128, MW), lambda i: (i, 0, 0)),
                   pl.BlockSpec((ncb, 8, MW), lambda i: (i, 0, 0)), pl.BlockSpec((ncb, 8, MW), lambda i: (i, 0, 0))]
        + [ANY] * nw,
        out_shape=[jax.ShapeDtypeStruct((S, MW), F32), jax.ShapeDtypeStruct((S // LC, 128, MW), F32),
                   jax.ShapeDtypeStruct((S // LC, 8, MW), F32), jax.ShapeDtypeStruct((S // LC, 8, MW), F32)]
        + _gather_shapes(shards, dtypes),
        scratch_shapes=[pltpu.VMEM((8, 1024), F32), pltpu.VMEM((128, MW), F32), pltpu.VMEM((8, MW), F32),
                        pltpu.VMEM((8, MW), F32), pltpu.VMEM((TB, MW), F32), pltpu.VMEM((TB, MW), F32)]
        + _gather_scratch(shards, dtypes),
        compiler_params=_params(1),
    )(mqk, mv, mo, gates, conv_w, conv_b, gate_b, gn, *shards)
    return res[0], res[1], res[2], res[3], res[4:]


DM_V, DM_O, DM_G, DM_W = 1024, 1536, 2048, PW - 3 * AW


def _mlstm_bwd(mqk, mv, mo, gates, conv_w, conv_b, gate_b, gn, cs, ns, ms, dout, parts):
    assert len(parts) == 1
    nblk = S // TB
    ncb = TB // LC
    kscale = 1.0 / math.sqrt(128.0)
    nw = len(parts)

    def body(*refs):
        x_ref, xprev_ref, v_ref, o_ref, g_ref, w_ref, b_ref, gb_ref, gn_ref, cs_ref, ns_ref, ms_ref, do_ref = refs[:13]
        ins = refs[13:13 + nw]
        dm_ref, dw_ref, db_ref, dgn_ref, dgb_ref = refs[13 + nw:18 + nw]
        outs = refs[18 + nw:18 + 2 * nw]
        dCst, dnst, dyhead, qs, ks, dqk = refs[18 + 2 * nw:24 + 2 * nw]
        rs_start, rs_middle, rs_finish = _scatter2_phases(ins[0], outs[0], *refs[24 + 2 * nw:])
        i = pl.program_id(0)
        blk = nblk - 1 - i
        pl.when(i == 0)(rs_start)
        pl.when(i == 2)(rs_middle)

        @pl.when(i == 0)
        def _():
            dCst[...] = jnp.zeros_like(dCst)
            dnst[...] = jnp.zeros_like(dnst)
            dyhead[...] = jnp.zeros_like(dyhead)
            dw_ref[...] = jnp.zeros_like(dw_ref)
            db_ref[...] = jnp.zeros_like(db_ref)
            dgn_ref[...] = jnp.zeros_like(dgn_ref)
            dgb_ref[...] = jnp.zeros_like(dgb_ref)

        x = x_ref[...]
        xprev = jnp.where(blk == 0, 0.0, xprev_ref[...])
        xp = jnp.concatenate([xprev, x], axis=0)
        c, sg, taps = _conv_silu(xp, w_ref, b_ref, TB)
        y = c * sg
        qs[...] = y[:, 0:MW]
        ks[...] = y[:, MW:2 * MW] * kscale
        lane128 = lax.broadcasted_iota(jnp.int32, (LC, 128), 1)
        rowi = lax.broadcasted_iota(jnp.int32, (LC, 1), 0)

        for cc in reversed(range(ncb)):
            rows = slice(cc * LC, (cc + 1) * LC)
            G = g_ref[rows, :] + gb_ref[...]
            b_col, b_row, g_row, _, triu = _chunk_gates(G)
            lns = [slice(h * 128, (h + 1) * 128) for h in range(4)]
            C = [cs_ref[cc, :, ln] for ln in lns]
            N = [ns_ref[cc, 0:1, ln] for ln in lns]
            Q, Kk = [qs[rows, ln] for ln in lns], [ks[rows, ln] for ln in lns]
            dCn, dnn = [dCst[:, ln] for ln in lns], [dnst[0:1, ln] for ln in lns]
            gns, dos, mos = [gn_ref[:, ln] for ln in lns], [do_ref[rows, ln] for ln in lns], [o_ref[rows, ln] for ln in lns]
            f = _mlstm_heads(Q, Kk, [v_ref[rows, ln] for ln in lns], G, b_col, b_row, g_row, C, N,
                             [jnp.max(ms_ref[cc, 0:1, ln], axis=1, keepdims=True) for ln in lns])
            hh, inv_dd, den, g, Am, Dm = f["h"], f["inv_dd"], f["den"], f["g"], f["Am"], f["Dm"]
            qb, kb, vb, ws, decay = f["qb"], f["kb"], f["vb"], f["ws"], f["decay"]
            ho = _each(_head_out, hh, mos, gns)
            hn, r, sgo = [t[1] for t in ho], [t[2] for t in ho], [t[3] for t in ho]
            dmo = _each(lambda d, n, gn_h, s: _bf(d * (n * gn_h) * s * (1.0 - s)), dos, hn, gns, sgo)
            dhm = _each(lambda d, s: d * s, dos, sgo)
            dgn = _each(lambda d, n: jnp.sum(d * n, axis=0, keepdims=True), dhm, hn)
            dhn = _each(lambda d, gn_h: d * gn_h, dhm, gns)
            dh = _each(lambda rr, d, n: rr * (d - n * jnp.mean(d * n, axis=-1, keepdims=True)), r, dhn, hn)
            dnum = _each(lambda d, i: d * i, dh, inv_dd)
            ddd = _each(lambda d, x, i: -jnp.sum(d * x, axis=1, keepdims=True) * i, dh, hh, inv_dd)
            dden = _each(lambda dn_, fl, d: jnp.where(jnp.abs(dn_) >= fl, d * jnp.sign(dn_), 0.0), den, f["floor"], ddd)
            dnb = [_bf(t) for t in dnum]
            gd = _each(lambda gg, d: _bf(gg * d), g, dnum)
            gq = _each(lambda gg, d: gg * d, g, dden)
            dCb = [_bf(t) for t in dCn]
            dA = _each(lambda d, v, dd_: _dot_nt(d, v) + dd_, dnb, vb, dden)
            dv1 = _each(lambda a, d: _dot_tn(_bf(a), d), Am, dnb)
            dq1 = _each(lambda d, ch: _dot_nt(d, _bf(ch)), gd, C)
            dC1 = _each(_dot_tn, qb, gd)
            E = _each(lambda v, d, n: _dot_nt(v, d) + n, vb, dCb, dnn)
            dv2 = _each(lambda k, d: _dot(_bf(k), d), f["kw"], dCb)
            dS = _each(lambda a, d: _bf(a * d), dA, Dm)
            dq2 = _each(_dot, dS, kb)
            dk1 = _each(_dot_tn, dS, qb)
            dq = _each(lambda a, x, n, b: a + x * n + b, dq1, gq, N, dq2)
            dC = _each(lambda d, x, y: d * x + y, decay, dCn, dC1)
            dn = _each(lambda d, x, y, q: d * x + jnp.sum(y * q, axis=0, keepdims=True), decay, dnn, gq, Q)
            dg = _each(lambda d, qc, dd_, x: jnp.sum(d * qc, axis=1, keepdims=True) + dd_ * x, dnum, f["qC"], dden, f["qn"])
            Gm = _each(lambda a, b: a * b, dA, Am)
            gam = _each(lambda a, b: a * b, dg, g)
            dk = _each(lambda a, w, e: (a + w * e) * kscale, dk1, ws, E)
            om = _each(lambda e, k, w: jnp.sum(e * k, axis=1, keepdims=True) * w, E, Kk, ws)
            dv = _each(lambda a, b: _bf(a + b), dv1, dv2)
            ddecay = _each(lambda d, ch, dn_, n: jnp.sum(jnp.sum(d * ch, axis=1, keepdims=True), axis=0, keepdims=True)
                           + jnp.sum(dn_ * n, axis=1, keepdims=True), dCn, C, dnn, N)
            rows_g = [jnp.sum(t, axis=1, keepdims=True) for t in Gm]
            cols_g = [jnp.broadcast_to(jnp.sum(t, axis=0, keepdims=True), (LC, 128)).T for t in Gm]
            last = _each(lambda o, dd_, d: jnp.where(rowi == LC - 1, jnp.sum(o, axis=0, keepdims=True) + dd_ * d, 0.0),
                         om, ddecay, decay)
            db = _each(lambda a, b, o, l, cg: a + b - o + l - cg, rows_g, gam, om, last, cols_g)
            di = _each(lambda cg, o: cg + o, cols_g, om)
            dB = jnp.zeros((LC, 128), F32)
            dI = jnp.zeros((LC, 128), F32)
            for h, ln in enumerate(lns):
                dB = jnp.where(lane128 == 4 + h, db[h], dB)
                dI = jnp.where(lane128 == h, di[h], dI)
                dgn_ref[:, ln] = dgn_ref[:, ln] + dgn[h]
                dCst[:, ln] = dC[h]
                dnst[0:1, ln] = dn[h]
                dqk[rows, ln] = dq[h]
                dqk[rows, MW + h * 128:MW + (h + 1) * 128] = dk[h]
                dm_ref[rows, DM_O + h * 128:DM_O + (h + 1) * 128] = dmo[h]
                dm_ref[rows, DM_V + h * 128:DM_V + (h + 1) * 128] = dv[h]
            dlogf = jnp.dot(triu, dB, preferred_element_type=F32, precision=HI)
            dG = dI + dlogf * _sigmoid(-G)
            dG = jnp.where(lane128 < 8, dG, 0.0)
            dm_ref[rows, DM_G:DM_G + 128] = _bf(dG)
            dm_ref[rows, DM_G + 128:DM_W] = jnp.zeros((LC, DM_W - DM_G - 128), BF16)
            dgb_ref[...] = dgb_ref[...] + jnp.sum(dG, axis=0, keepdims=True)

        dy = dqk[...] * (sg * (1.0 + c * (1.0 - sg)))
        db_ref[...] = db_ref[...] + jnp.sum(dy, axis=0, keepdims=True)
        for j in range(4):
            dw_ref[j:j + 1, :] = dw_ref[j:j + 1, :] + jnp.sum(dy * taps[j], axis=0, keepdims=True)
        dyp = jnp.concatenate([dy, dyhead[...]], axis=0)
        dx = w_ref[3:4, :] * dy
        for j in range(3):
            dx = dx + w_ref[j:j + 1, :] * pltpu.roll(dyp, TB + 8 - (3 - j), 0)[0:TB]
        dm_ref[:, 0:DM_V] = _bf(dx)
        dyhead[...] = dy[0:8, :]
        pl.when(i == nblk - 1)(rs_finish)

    rrow = lambda wd: pl.BlockSpec((TB, wd), lambda i: (nblk - 1 - i, 0))
    st = lambda r: pl.BlockSpec((ncb, r, MW), lambda i: (nblk - 1 - i, 0, 0))
    prev8 = pl.BlockSpec((8, 1024), lambda i: (jnp.maximum((nblk - 1 - i) * (TB // 8) - 1, 0), 0))
    res = pl.pallas_call(
        body, name="mlstm_bwd", grid=(nblk,),
        in_specs=[rrow(1024), prev8, rrow(MW), rrow(MW), rrow(128), _cspec((4, 1024)), _cspec((1, 1024)),
                  _cspec((1, 128)), _cspec((1, MW)), st(128), st(8), st(8), rrow(MW)] + [ANY] * nw,
        out_specs=[rrow(DM_W),
                   pl.BlockSpec((4, 1024), lambda i: (0, 0)), pl.BlockSpec((1, 1024), lambda i: (0, 0)),
                   pl.BlockSpec((1, MW), lambda i: (0, 0)), pl.BlockSpec((1, 128), lambda i: (0, 0))] + [ANY] * nw,
        out_shape=[jax.ShapeDtypeStruct((S, DM_W), BF16),
                   jax.ShapeDtypeStruct((4, 1024), F32), jax.ShapeDtypeStruct((1, 1024), F32),
                   jax.ShapeDtypeStruct((1, MW), F32), jax.ShapeDtypeStruct((1, 128), F32)]
        + [jax.ShapeDtypeStruct((len(CHIP_FLIPS), *a.shape[1:]), a.dtype) for a in parts],
        scratch_shapes=[pltpu.VMEM((128, MW), F32), pltpu.VMEM((8, MW), F32), pltpu.VMEM((8, 1024), F32),
                        pltpu.VMEM((TB, MW), F32), pltpu.VMEM((TB, MW), F32), pltpu.VMEM((TB, 1024), F32)]
        + _scatter2_scratch(parts[0].shape[1:], parts[0].dtype),
        compiler_params=_params(1),
    )(mqk, mqk, mv, mo, gates, conv_w, conv_b, gate_b, gn, cs, ns, ms, dout, *parts)
    return res[:5], res[5:]


def _out_proj(x, attn, ml, w, g):
    tm = TM

    def body(x_ref, a_ref, m_ref, w_ref, g_ref, h_ref, u_ref):
        h1 = x_ref[...] + _dot(_bf(a_ref[...]), w_ref[0:AW, :]) + _dot(_bf(m_ref[...]), w_ref[AW:D, :])
        h_ref[...] = h1
        n, _ = _rms(h1)
        u_ref[...] = _bf(n * g_ref[...])

    row = lambda wd: pl.BlockSpec((tm, wd), lambda i: (i, 0))
    return pl.pallas_call(
        body, name="out_proj", grid=(S // tm,),
        in_specs=[row(D), row(AW), row(MW), _cspec((D, D)), _cspec((1, D))],
        out_specs=[row(D), row(D)],
        out_shape=[jax.ShapeDtypeStruct((S, D), F32), jax.ShapeDtypeStruct((S, D), BF16)],
        compiler_params=_params(1),
    )(x, attn, ml, w, g)


HALF = DFF // NDEV // 2


def _mlp_fwd(h1, u2, w_up, w_down_a, w_down_b, shards, dtypes):
    tm = TM
    nt = S // tm
    nw = len(shards)

    def body(*refs):
        h_ref, u_ref, wu_ref, wa_ref, wb_ref = refs[:5]
        ins = refs[5:5 + nw]
        a_ref, o_ref = refs[5 + nw:7 + nw]
        outs = refs[7 + nw:7 + 2 * nw]
        bufs = refs[7 + 2 * nw:7 + 3 * nw]
        ag_start, ag_forward, ag_finish = _gather_phases(ins, outs, bufs, *refs[7 + 3 * nw:])
        i = pl.program_id(0)
        pl.when(i == 0)(ag_start)
        pl.when(i == nt - 2)(ag_forward)
        u = u_ref[...]
        acc = h_ref[...]
        for c in range(NDEV):
            cols = slice(c * 512, (c + 1) * 512)
            a = _dot(u, wu_ref[c])
            a_ref[:, cols] = _bf(a)
            r = jnp.maximum(a, 0.0)
            r = _bf(r * r)
            acc = acc + _dot(r[:, 0:HALF], wa_ref[c]) + _dot(r[:, HALF:2 * HALF], wb_ref[c])
        o_ref[...] = acc
        pl.when(i == nt - 1)(ag_finish)

    row = lambda wd: pl.BlockSpec((tm, wd), lambda i: (i, 0))
    res = pl.pallas_call(
        body, name="mlp_fwd", grid=(nt,),
        in_specs=[row(D), row(D), _cspec((NDEV, D, DFF // NDEV)), _cspec((NDEV, HALF, D)), _cspec((NDEV, HALF, D))]
        + [VM] * nw,
        out_specs=[row(DFF), row(D)] + [ANY] * nw,
        out_shape=[jax.ShapeDtypeStruct((S, DFF), BF16), jax.ShapeDtypeStruct((S, D), F32)]
        + _gather_shapes(shards, dtypes),
        scratch_shapes=_gather_scratch(shards, dtypes),
        compiler_params=_params(1),
    )(h1, u2, w_up, w_down_a, w_down_b, *shards)
    return res[0], res[1], res[2:]


def _ple_loss(h2, p, target, w_pg, w_ple, g_ple, g_fin):
    tm = TM

    def body(h_ref, p_ref, t_ref, wg_ref, wp_ref, gp_ref, gf_ref,
             dh_ref, dwg_ref, dwp_ref, dgp_ref, dgf_ref, loss_ref, acc_g, acc_p):
        i = pl.program_id(0)

        @pl.when(i == 0)
        def _():
            acc_g[...] = jnp.zeros_like(acc_g)
            acc_p[...] = jnp.zeros_like(acc_p)
            dgp_ref[...] = jnp.zeros_like(dgp_ref)
            dgf_ref[...] = jnp.zeros_like(dgf_ref)
            loss_ref[...] = jnp.zeros_like(loss_ref)

        h2v = h_ref[...]
        n2, rs2 = _rms(h2v)
        u3 = _bf(n2 * gp_ref[...])
        gt = _sigmoid(_dot(u3, wg_ref[...]))
        pb = _bf(p_ref[...])
        e = jnp.concatenate([_dot(pb, wp_ref[j]) for j in range(NDEV)], axis=1)
        h3 = h2v + gt * e
        n3, rs3 = _rms(h3)
        err = n3 * gf_ref[...] - t_ref[...]
        loss_ref[...] = loss_ref[...] + 0.5 / D * jnp.sum(jnp.sum(err * err, axis=1, keepdims=True), axis=0, keepdims=True)
        dy = err * (1.0 / D)
        dgf_ref[...] = dgf_ref[...] + jnp.sum(dy * n3, axis=0, keepdims=True)
        dh3 = _rms_bwd(dy, n3, rs3, gf_ref[...])
        de = _bf(dh3 * gt)
        dz = _bf(dh3 * e * gt * (1.0 - gt))
        acc_p[...] = acc_p[...] + _dot_tn(pb, de)
        acc_g[...] = acc_g[...] + _dot_tn(u3, dz)
        du3 = _dot_nt(dz, wg_ref[...])
        dgp_ref[...] = dgp_ref[...] + jnp.sum(du3 * n2, axis=0, keepdims=True)
        dh_ref[...] = dh3 + _rms_bwd(du3, n2, rs2, gp_ref[...])

        @pl.when(i == S // tm - 1)
        def _():
            dwg_ref[...] = _bf(acc_g[...])
            for j in range(NDEV):
                dwp_ref[j] = _bf(acc_p[:, j * 128:(j + 1) * 128])

    row = lambda wd: pl.BlockSpec((tm, wd), lambda i: (i, 0))
    whole = lambda shp: pl.BlockSpec(shp, lambda i: (0,) * len(shp))
    return pl.pallas_call(
        body, name="ple_loss", grid=(S // tm,),
        in_specs=[row(D), row(PLE), row(D), _cspec((D, D)), _cspec((NDEV, PLE, 128)), _cspec((1, D)), _cspec((1, D))],
        out_specs=[row(D), whole((D, D)), whole((NDEV, PLE, 128)), whole((1, D)), whole((1, D)), whole((1, 1))],
        out_shape=[jax.ShapeDtypeStruct((S, D), F32), jax.ShapeDtypeStruct((D, D), BF16),
                   jax.ShapeDtypeStruct((NDEV, PLE, 128), BF16), jax.ShapeDtypeStruct((1, D), F32),
                   jax.ShapeDtypeStruct((1, D), F32), jax.ShapeDtypeStruct((1, 1), F32)],
        scratch_shapes=[pltpu.VMEM((D, D), F32), pltpu.VMEM((PLE, D), F32)],
        compiler_params=_params(1),
    )(h2, p, target, w_pg, w_ple, g_ple, g_fin)


def _mlp_bwd(dh2, a, h1, g, w_up, w_down_a, w_down_b, parts):
    tm = TM
    nt = S // tm
    nw = len(parts)

    def body(*refs):
        d_ref, a_ref, h_ref, g_ref, wu_ref, wa_ref, wb_ref = refs[:7]
        ins = refs[7:7 + nw]
        da_ref, dh1_ref, dg_ref = refs[7 + nw:10 + nw]
        outs = refs[10 + nw:10 + 2 * nw]
        rs_start, rs_finish = _scatter_phases(ins, outs, *refs[10 + 2 * nw:])
        i = pl.program_id(0)
        pl.when(i == 0)(rs_start)

        @pl.when(i == 0)
        def _():
            dg_ref[...] = jnp.zeros_like(dg_ref)

        dh2v = d_ref[...]
        db = _bf(dh2v)
        du = jnp.zeros((tm, D), F32)
        for c in range(NDEV):
            cols = slice(c * 512, (c + 1) * 512)
            dr = jnp.concatenate([_dot_nt(db, wa_ref[c]), _dot_nt(db, wb_ref[c])], axis=1)
            da = _bf(dr * (2.0 * jnp.maximum(a_ref[:, cols], 0.0)))
            da_ref[:, cols] = da
            du = du + _dot_nt(da, wu_ref[c])
        n, rs = _rms(h_ref[...])
        dg_ref[...] = dg_ref[...] + jnp.sum(du * n, axis=0, keepdims=True)
        dh1_ref[...] = dh2v + _rms_bwd(du, n, rs, g_ref[...])
        pl.when(i == nt - 1)(rs_finish)

    row = lambda wd: pl.BlockSpec((tm, wd), lambda i: (i, 0))
    res = pl.pallas_call(
        body, name="mlp_bwd", grid=(nt,),
        in_specs=[row(D), row(DFF), row(D), _cspec((1, D)), _cspec((NDEV, D, DFF // NDEV)), _cspec((NDEV, HALF, D)),
                  _cspec((NDEV, HALF, D))] + [ANY] * nw,
        out_specs=[row(DFF), row(D), pl.BlockSpec((1, D), lambda i: (0, 0))] + [ANY] * nw,
        out_shape=[jax.ShapeDtypeStruct((S, DFF), BF16), jax.ShapeDtypeStruct((S, D), F32),
                   jax.ShapeDtypeStruct((1, D), F32)] + [jax.ShapeDtypeStruct(p.shape, p.dtype) for p in parts],
        scratch_shapes=_scatter_scratch(nw),
        compiler_params=_params(1),
    )(dh2, a, h1, g, w_up, w_down_a, w_down_b, *parts)
    return res[0], res[1], res[2], res[3:]


def _out_proj_bwd(dh1, attn, ml, w):
    tm = TM

    def body(d_ref, a_ref, m_ref, w_ref, da_ref, dm_ref, dw_ref, acc):
        i = pl.program_id(0)

        @pl.when(i == 0)
        def _():
            acc[...] = jnp.zeros_like(acc)

        db = _bf(d_ref[...])
        dmix = _dot_nt(db, w_ref[...])
        da_ref[...] = dmix[:, 0:AW]
        dm_ref[...] = dmix[:, AW:D]
        acc[0:AW, :] = acc[0:AW, :] + _dot_tn(_bf(a_ref[...]), db)
        acc[AW:D, :] = acc[AW:D, :] + _dot_tn(_bf(m_ref[...]), db)

        @pl.when(i == S // tm - 1)
        def _():
            dw_ref[...] = _bf(acc[...])

    row = lambda wd: pl.BlockSpec((tm, wd), lambda i: (i, 0))
    return pl.pallas_call(
        body, name="out_proj_bwd", grid=(S // tm,),
        in_specs=[row(D), row(AW), row(MW), _cspec((D, D))],
        out_specs=[row(AW), row(MW), pl.BlockSpec((D, D), lambda i: (0, 0))],
        out_shape=[jax.ShapeDtypeStruct((S, AW), F32), jax.ShapeDtypeStruct((S, MW), F32),
                   jax.ShapeDtypeStruct((D, D), BF16)],
        scratch_shapes=[pltpu.VMEM((D, D), F32)],
        compiler_params=_params(1),
    )(dh1, attn, ml, w)


CHIP_FLIPS = [(0, 0), (0, 1), (1, 0), (1, 1)]


def _scatter2_phases(in_ref, out_ref, mine_v, sib_v, psum_v, loc_sems, d2d_send, d2d_recv, ici_send, ici_recv, own_sem):
    x, y, c = _place()
    chips = [((x + dx) % 2, (y + dy) % 2) for dx, dy in CHIP_FLIPS]
    nc = len(chips)

    def local(k):
        return pltpu.make_async_copy(in_ref.at[_dev_index(*chips[k], c)], mine_v.at[k], loc_sems.at[k])

    def to_sib(k):
        return pltpu.make_async_remote_copy(
            src_ref=in_ref.at[_dev_index(*chips[k], 1 - c)], dst_ref=sib_v.at[k], send_sem=d2d_send.at[k],
            recv_sem=d2d_recv.at[k], device_id=(x, y, 1 - c), device_id_type=MESH)

    def over_ici(k):
        return pltpu.make_async_remote_copy(
            src_ref=psum_v.at[k], dst_ref=out_ref.at[k], send_sem=ici_send.at[k - 1], recv_sem=ici_recv.at[k - 1],
            device_id=(*chips[k], c), device_id_type=MESH)

    def own():
        return pltpu.make_async_copy(psum_v.at[0], out_ref.at[0], own_sem)

    def start():
        for k in range(nc):
            to_sib(k).start()
            local(k).start()

    def middle():
        for k in (1, 2, 3, 0):
            local(k).wait()
            to_sib(k).wait_recv()
            psum_v[k] = _bf(mine_v[k].astype(F32) + sib_v[k].astype(F32))
            (over_ici(k) if k else own()).start()

    def finish():
        for k in range(1, nc):
            over_ici(k).wait()
        for k in range(nc):
            to_sib(k).wait_send()
        own().wait()

    return start, middle, finish


def _scatter2_scratch(shard, dtype):
    nc = len(CHIP_FLIPS)
    return ([pltpu.VMEM((nc, *shard), dtype)] * 3
            + [pltpu.SemaphoreType.DMA((nc,))] * 3 + [pltpu.SemaphoreType.DMA((nc - 1,))] * 2 + [pltpu.SemaphoreType.DMA])


def _in_proj_bwd(dparts, n_roped, rope, dh1, x, g1, w, part):
    tm = TM
    nt = S // tm
    widths = [d.shape[1] for d in dparts]
    assert sum(widths) == PW
    npar = len(dparts)

    def body(*refs):
        d_refs = refs[:npar]
        tabs = [t[...] for t in refs[npar:npar + 3]]
        dh_ref, x_ref, g_ref, w_ref, in_ref, dx_ref, dgsum_ref, out_ref = refs[npar + 3:npar + 11]
        rs_start, rs_middle, rs_finish = _scatter2_phases(in_ref, out_ref, *refs[npar + 11:npar + 20])
        dg_ref = refs[npar + 20]
        ar_start, ar_finish = _small_phases([dg_ref], dgsum_ref, *refs[npar + 21:])
        i = pl.program_id(0)
        pl.when(i == 0)(rs_start)
        pl.when(i == 1)(rs_middle)

        @pl.when(i == 0)
        def _():
            dg_ref[...] = jnp.zeros_like(dg_ref)

        du = jnp.zeros((tm, D), F32)
        off = 0
        for j, (d_ref, wd) in enumerate(zip(d_refs, widths)):
            nc = next(c for c in (768, 512) if wd % c == 0)
            for s in range(wd // nc):
                d = d_ref[:, s * nc:(s + 1) * nc]
                du = du + _dot_nt(_unrope(d, *tabs) if j < n_roped else d, w_ref[:, off + s * nc:off + (s + 1) * nc])
            off += wd
        n, rs = _rms(x_ref[...])
        dg_ref[...] = dg_ref[...] + jnp.sum(du * n, axis=0, keepdims=True)
        dx_ref[...] = dh_ref[...] + _rms_bwd(du, n, rs, g_ref[...])

        @pl.when(i == nt - 1)
        def _():
            ar_start()
            rs_finish()
            ar_finish()

    row = lambda wd: pl.BlockSpec((tm, wd), lambda i: (i, 0))
    shard = part.shape[1:]
    return pl.pallas_call(
        body, name="in_proj_bwd", grid=(nt,),
        in_specs=[row(wd) for wd in widths] + [row(128)] * 3 + [row(D), row(D), _cspec((1, D)), _cspec((D, PW)), ANY],
        out_specs=[row(D), VM, ANY],
        out_shape=[jax.ShapeDtypeStruct((S, D), F32), jax.ShapeDtypeStruct((8, 1024), F32),
                   jax.ShapeDtypeStruct((len(CHIP_FLIPS), *shard), part.dtype)],
        scratch_shapes=_scatter2_scratch(shard, part.dtype)
        + [pltpu.VMEM((1, D), F32), pltpu.VMEM((8, 1024), F32), pltpu.VMEM((NDEV, 8, 1024), F32),
           pltpu.SemaphoreType.DMA((7,)), pltpu.SemaphoreType.DMA((7,))],
        compiler_params=_params(1),
    )(*dparts, *rope, dh1, x, g1, w, part)


SMALL_ROWS = 96


def _small_phases(ins, out_ref, pack, rbuf, send_sems, recv_sems):
    x, y, c = _place()
    me = _dev_index(x, y, c)

    def copies():
        out = []
        for k, (dx, dy, dc) in enumerate(FLIPS):
            peer = ((x + dx) % 2, (y + dy) % 2, (c + dc) % 2)
            out.append(pltpu.make_async_remote_copy(
                src_ref=pack, dst_ref=rbuf.at[me], send_sem=send_sems.at[k], recv_sem=recv_sems.at[k],
                device_id=peer, device_id_type=MESH))
        return out

    def start():
        pack[...] = jnp.zeros_like(pack)
        for i, ref in enumerate(ins):
            pack[8 * i:8 * i + 1, 0:ref.shape[1]] = ref[...]
        rbuf[me] = pack[...]
        for cp in copies():
            cp.start()

    def finish():
        for cp in copies():
            cp.wait()
        tot = rbuf[0]
        for j in range(1, NDEV):
            tot = tot + rbuf[j]
        out_ref[...] = tot

    return start, finish


def _wgrad(name, A, Bs, a_fn, b_fn, out_shape, split=None, ts=512, small=(), rope=(), n_roped=0):
    K = A.shape[1]
    widths = [b.shape[1] for b in Bs]
    N = sum(widths)
    nb, ns, nrt = len(Bs) + len(rope), len(small), S // ts
    kc = min(K, 1024)

    def body(*refs):
        a_ref, b_refs = refs[0], refs[1:1 + len(Bs)]
        tabs = [t[...] for t in refs[1 + len(Bs):1 + nb]]
        o_ref = refs[1 + nb + ns]
        acc = refs[2 + nb + ns + bool(ns)]
        r = pl.program_id(0)
        if ns:
            sm_start, sm_finish = _small_phases(refs[1 + nb:1 + nb + ns], refs[2 + nb + ns], *refs[4 + nb + ns:])
            pl.when(r == 0)(sm_start)

        @pl.when(r == 0)
        def _():
            acc[...] = jnp.zeros_like(acc)

        bs, off = [], 0
        for i, (b_ref, w) in enumerate(zip(b_refs, widths)):
            nc = next(c for c in (1024, 768, 512) if w % c == 0)
            fn = (lambda t: _unrope(t, *tabs)) if i < n_roped else b_fn
            bs += [(off + c * nc, nc, fn(b_ref[:, c * nc:(c + 1) * nc])) for c in range(w // nc)]
            off += w
        for kk in range(K // kc):
            rows = slice(kk * kc, (kk + 1) * kc)
            at = a_fn(a_ref[:, rows]).T
            for lo, nc, b in bs:
                acc[rows, lo:lo + nc] = acc[rows, lo:lo + nc] + _dot(at, b)

        @pl.when(r == nrt - 1)
        def _():
            if split is None:
                o_ref[...] = _bf(acc[...])
            else:
                for j in range(NDEV):
                    o_ref[j] = _bf(acc[:, split * j:split * (j + 1)])

        if ns:
            pl.when(r == nrt - 1)(sm_finish)

    in_specs = ([pl.BlockSpec((ts, K), lambda r: (r, 0))] + [pl.BlockSpec((ts, w), lambda r: (r, 0)) for w in widths]
                + [pl.BlockSpec((ts, 128), lambda r: (r, 0))] * len(rope))
    out_spec = pl.BlockSpec(out_shape, lambda r: (0,) * len(out_shape))
    scratch = [pltpu.VMEM((K, N), F32)]
    if not ns:
        return pl.pallas_call(
            body, name=name, grid=(nrt,), in_specs=in_specs, out_specs=out_spec,
            out_shape=jax.ShapeDtypeStruct(out_shape, BF16), scratch_shapes=scratch, compiler_params=_params(1),
        )(A, *Bs, *rope)
    return pl.pallas_call(
        body, name=name, grid=(nrt,), in_specs=in_specs + [VM] * ns, out_specs=[out_spec, VM],
        out_shape=[jax.ShapeDtypeStruct(out_shape, BF16), jax.ShapeDtypeStruct((SMALL_ROWS, 1024), F32)],
        scratch_shapes=scratch + [pltpu.VMEM((SMALL_ROWS, 1024), F32), pltpu.VMEM((NDEV, SMALL_ROWS, 1024), F32),
                                  pltpu.SemaphoreType.DMA((7,)), pltpu.SemaphoreType.DMA((7,))],
        compiler_params=_params(1),
    )(A, *Bs, *rope, *small)


def _relu2_bf(a):
    r = jnp.maximum(a.astype(F32), 0.0)
    return _bf(r * r)


def _ident(a):
    return a


def _step(x, p, target, g1, conv_b, gate_b, gn, g_mlp, g_ple, g_fin, sh):
    (g_in, g_conv), (rc, ra, rb) = _gather_weights([sh["w_in"], sh["conv_w"]], [BF16, F32])
    conv_w = g_conv.transpose(1, 0, 2).reshape(4, 1024)
    w_in_p = _join_w_in(g_in)
    (qkv, mqk, mv, mo, gates, u1), (w_down_a,) = _in_proj(x, g1, w_in_p, rc, ra, rb, [sh["w_down"][0:HALF]], [BF16])
    attn, lse, (w_up8, w_out8) = _attn_fwd(qkv, [sh["w_up"], sh["w_out"]], [BF16] * 2)
    ml, cs, ns, ms, (w_down_b,) = _mlstm_fwd(mqk, mv, mo, gates, conv_w, conv_b, gate_b, gn,
                                             [sh["w_down"][HALF:2 * HALF]], [BF16])
    w_out = w_out8.reshape(D, D)
    h1, u2 = _out_proj(x, attn, ml, w_out, g_mlp)
    a, h2, (w_pg8, w_ple8) = _mlp_fwd(h1, u2, w_up8, w_down_a, w_down_b, [sh["w_ple_gate"], sh["w_ple"]], [BF16] * 2)
    w_pg = w_pg8.reshape(D, D)
    dh2, dw_pg, dw_ple8, dg_ple, dg_fin, loss = _ple_loss(h2, p, target, w_pg, w_ple8, g_ple, g_fin)
    da, dh1, dg_mlp, (r_pg, r_ple) = _mlp_bwd(dh2, a, h1, g_mlp, w_up8, w_down_a, w_down_b,
                                              [dw_pg.reshape(NDEV, D // NDEV, D), dw_ple8])
    dw_up8 = _wgrad("wgrad_up", u2, [da], _ident, _ident, (NDEV, D, DFF // NDEV), split=DFF // NDEV)
    dw_down = _wgrad("wgrad_down", a, [dh2], _relu2_bf, _bf, (DFF, D))
    d_attn, d_ml, dw_out = _out_proj_bwd(dh1, attn, ml, w_out)
    (dm, dconv_w, dconv_b, dgn, dgate_b), (r_down,) = _mlstm_bwd(
        mqk, mv, mo, gates, conv_w, conv_b, gate_b, gn, cs, ns, ms, d_ml, [dw_down.reshape(NDEV, DFF // NDEV, D)])
    dq, dk, dv, (r_up, r_out) = _attn_bwd(qkv, attn, lse, d_attn, [dw_up8, dw_out.reshape(NDEV, D // NDEV, D)])
    dparts = [dq, dk, dv, dm]
    small = [jnp.zeros((1, D), F32), dconv_b, dgate_b, dgn, dg_mlp, dg_ple, dg_fin, loss]
    dw_in8, total = _wgrad("wgrad_in", u1, dparts, _ident, _ident, (NDEV, D, IN_W // NDEV), split=IN_W // NDEV,
                           small=small + [dconv_w[j:j + 1] for j in range(4)], rope=(rc, ra, rb), n_roped=2)
    dx, dg1_sum, r_in = _in_proj_bwd(dparts, 2, (rc, ra, rb), dh1, x, g1, w_in_p, dw_in8)
    recv = dict(w_in=r_in, w_out=r_out, w_up=r_up, w_down=r_down, w_ple_gate=r_pg, w_ple=r_ple)
    return dx, recv, total, dg1_sum


def _gather_weights(shards, dtypes):
    nw = len(shards)

    def body(*refs):
        ins, parts = refs[:nw], refs[nw:nw + 4]
        outs, tables = refs[nw + 4:2 * nw + 4], refs[2 * nw + 4:2 * nw + 7]
        start, forward, finish = _gather_phases(ins, outs, refs[2 * nw + 7:3 * nw + 7], *refs[3 * nw + 7:])
        start()
        _rope_fill(*parts, *tables)
        forward()
        finish()

    res = pl.pallas_call(
        body, name="gather_weights",
        in_specs=[VM] * (nw + 4), out_specs=[ANY] * nw + [VM] * 3,
        out_shape=_gather_shapes(shards, dtypes) + [jax.ShapeDtypeStruct((S, 128), F32)] * 3,
        scratch_shapes=_gather_scratch(shards, dtypes),
        compiler_params=_params(),
    )(*shards, *_rope_parts())
    return res[:nw], res[nw:]


ADAM_STEPS = 8


def _adamw(items):
    n = len(items)

    def body(*refs):
        for i in range(n):
            g_ref, w_ref, m_ref, v_ref = refs[4 * i:4 * i + 4]
            go_ref, d_ref, mo_ref, vo_ref = refs[4 * n + 4 * i:4 * n + 4 * i + 4]
            g = g_ref[0].astype(F32)
            for j in range(1, g_ref.shape[0]):
                g = g + g_ref[j].astype(F32)
            go_ref[...] = g
            d_ref[...], mo_ref[...], vo_ref[...] = _adam_update(g, w_ref[...], m_ref[...], v_ref[...])

    in_specs, out_specs, out_shape, args = [], [], [], []
    for gparts, w, m, v in items:
        P, R, C = gparts.shape
        if R % (8 * ADAM_STEPS) == 0:
            tr = R // ADAM_STEPS
            row, gspec = pl.BlockSpec((tr, C), lambda i: (i, 0)), pl.BlockSpec((P, tr, C), lambda i: (0, i, 0))
        else:
            row, gspec = pl.BlockSpec((R, C), lambda i: (0, 0)), pl.BlockSpec((P, R, C), lambda i: (0, 0, 0))
        in_specs += [gspec, row, row, row]
        out_specs += [row] * 4
        out_shape += [jax.ShapeDtypeStruct((R, C), F32)] * 4
        args += [gparts, w, m, v]
    res = pl.pallas_call(
        body, name="adamw", grid=(ADAM_STEPS,), in_specs=in_specs, out_specs=out_specs, out_shape=out_shape,
        compiler_params=_params(1),
    )(*args)
    return [res[4 * i:4 * i + 4] for i in range(n)]


SMALL = ("norm_mix_g", "conv_b", "gate_b", "mlstm_norm_g", "norm_mlp_g", "norm_ple_g", "final_norm_g")


def _adam_update(g, w, m, v):
    c1 = 1.0 - ADAM_B1 ** ADAM_STEP
    c2 = 1.0 - ADAM_B2 ** ADAM_STEP
    m2 = ADAM_B1 * m + (1.0 - ADAM_B1) * g
    v2 = ADAM_B2 * v + (1.0 - ADAM_B2) * (g * g)
    return -ADAM_LR * ((m2 / c1) / (jnp.sqrt(v2 / c2) + ADAM_EPS) + ADAM_WD * w), m2, v2


def _adamw_small(total, first, ws, ms, vs):
    n = len(ws)

    def body(*refs):
        t_ref, f_ref = refs[:2]
        refs = refs[1:]
        outs = refs[1 + 3 * n:]
        for i in range(n):
            w_ref, m_ref, v_ref = refs[1 + i], refs[1 + n + i], refs[1 + 2 * n + i]
            g = (t_ref if i else f_ref)[8 * i:8 * i + 1, 0:w_ref.shape[1]]
            delta, m2, v2 = _adam_update(g, w_ref[...], m_ref[...], v_ref[...])
            for ref, val in zip(outs[4 * i:4 * i + 4], (g, delta, m2, v2)):
                ref[...] = val

    res = pl.pallas_call(
        body, name="adamw_small",
        out_shape=[jax.ShapeDtypeStruct(w.shape, F32) for w in ws for _ in range(4)],
        compiler_params=_params(),
    )(total, first, *ws, *ms, *vs)
    return [res[4 * i:4 * i + 4] for i in range(n)]


def kernel(x, p, norm_mix_g, w_in, conv_w, conv_b, gate_b, mlstm_norm_g, w_out, norm_mlp_g, w_up, w_down, norm_ple_g, w_ple_gate, w_ple, final_norm_g, loss_target, m_norm_mix_g, m_w_in, m_conv_w, m_conv_b, m_gate_b, m_mlstm_norm_g, m_w_out, m_norm_mlp_g, m_w_up, m_w_down, m_norm_ple_g, m_w_ple_gate, m_w_ple, m_final_norm_g, v_norm_mix_g, v_w_in, v_conv_w, v_conv_b, v_gate_b, v_mlstm_norm_g, v_w_out, v_norm_mlp_g, v_w_up, v_w_down, v_norm_ple_g, v_w_ple_gate, v_w_ple, v_final_norm_g):
    big_names = ("w_in", "conv_w", "w_out", "w_up", "w_down", "w_ple_gate", "w_ple")
    wts = dict(w_in=w_in, conv_w=conv_w, w_out=w_out, w_up=w_up, w_down=w_down, w_ple_gate=w_ple_gate, w_ple=w_ple)
    mom = dict(w_in=m_w_in, conv_w=m_conv_w, w_out=m_w_out, w_up=m_w_up, w_down=m_w_down, w_ple_gate=m_w_ple_gate,
               w_ple=m_w_ple)
    var = dict(w_in=v_w_in, conv_w=v_conv_w, w_out=v_w_out, w_up=v_w_up, w_down=v_w_down, w_ple_gate=v_w_ple_gate,
               w_ple=v_w_ple)
    sq = lambda a: a.reshape(a.shape[1:])
    fin = final_norm_g.reshape(1, D)
    dx, recv, total, first = _step(
        x[0], p[0, 0], loss_target[0], norm_mix_g, conv_b, jnp.pad(gate_b, ((0, 0), (0, 120))), mlstm_norm_g,
        norm_mlp_g, norm_ple_g, fin, {n: sq(wts[n]) for n in big_names})

    nrow = 8 * len(SMALL)
    me = _dev_index(*_place())
    conv_rows = total[nrow + 8:nrow + 40:8]
    recv["conv_w"] = lax.dynamic_slice_in_dim(conv_rows, me * 128, 128, axis=1).reshape(1, 4, 128)
    out = {}
    for n, res in zip(big_names, _adamw([(recv[n], sq(wts[n]), sq(mom[n]), sq(var[n])) for n in big_names])):
        out[n] = [t.reshape(wts[n].shape) for t in res]
    sw = dict(norm_mix_g=norm_mix_g, conv_b=conv_b, gate_b=gate_b, mlstm_norm_g=mlstm_norm_g, norm_mlp_g=norm_mlp_g,
              norm_ple_g=norm_ple_g, final_norm_g=fin)
    sm = dict(norm_mix_g=m_norm_mix_g, conv_b=m_conv_b, gate_b=m_gate_b, mlstm_norm_g=m_mlstm_norm_g,
              norm_mlp_g=m_norm_mlp_g, norm_ple_g=m_norm_ple_g, final_norm_g=m_final_norm_g.reshape(1, D))
    sv = dict(norm_mix_g=v_norm_mix_g, conv_b=v_conv_b, gate_b=v_gate_b, mlstm_norm_g=v_mlstm_norm_g,
              norm_mlp_g=v_norm_mlp_g, norm_ple_g=v_norm_ple_g, final_norm_g=v_final_norm_g.reshape(1, D))
    res = _adamw_small(total, first, [sw[n] for n in SMALL], [sm[n] for n in SMALL], [sv[n] for n in SMALL])
    for n, r in zip(SMALL, res):
        out[n] = [t.reshape(final_norm_g.shape) for t in r] if n == "final_norm_g" else list(r)
    order = ("norm_mix_g", "w_in", "conv_w", "conv_b", "gate_b", "mlstm_norm_g", "w_out", "norm_mlp_g", "w_up", "w_down",
             "norm_ple_g", "w_ple_gate", "w_ple", "final_norm_g")
    loss_all = total[nrow, 0]
    return (loss_all, dx[None], *[out[n][0] for n in order], *[out[n][1] for n in order],
            *[out[n][2] for n in order], *[out[n][3] for n in order])
```

```python
import math

import jax
import jax.numpy as jnp
from jax import lax
from jax.experimental import pallas as pl
from jax.experimental.pallas import tpu as pltpu

F32, BF16 = jnp.float32, jnp.bfloat16
S = 4096
D = 1024
AW = 512
MW = 512
DFF = 4096
PLE = 256
IN_W = 3592
PW = 3840
NDEV = 8
EPS = 1e-6
NEG = -1e30
LC = 128
TB = 256
ROPE_THETA = 500000.0
VMEM_LIMIT = 56 * 1024 * 1024
HI = lax.Precision.HIGHEST

ADAM_LR, ADAM_B1, ADAM_B2, ADAM_EPS, ADAM_WD, ADAM_STEP = 0.001, 0.9, 0.999, 1e-08, 0.01, 10


def _params(n_grid=0, **kw):
    sem = dict(dimension_semantics=("arbitrary",) * n_grid) if n_grid else {}
    return pltpu.CompilerParams(vmem_limit_bytes=VMEM_LIMIT, **sem, **kw)


def _cspec(shape):
    nd = len(shape)
    return pl.BlockSpec(shape, lambda *_: (0,) * nd, pipeline_mode=pl.Buffered(1))


def _dot(a, b):
    return jnp.dot(a, b, preferred_element_type=F32)


def _dot_nt(a, b):
    return lax.dot_general(a, b, (((1,), (1,)), ((), ())), preferred_element_type=F32)


def _dot_tn(a, b):
    return lax.dot_general(a, b, (((0,), (0,)), ((), ())), preferred_element_type=F32)


def _bf(x):
    return x.astype(BF16)


def _rms(x):
    rs = lax.rsqrt(jnp.mean(x * x, axis=-1, keepdims=True) + EPS)
    return x * rs, rs


def _rms_bwd(du, n, rs, g):
    dn = du * g
    return rs * (dn - n * jnp.mean(dn * n, axis=-1, keepdims=True))


def _sigmoid(x):
    return 1.0 / (1.0 + jnp.exp(-x))


ROPE_BLK = 512


def _rope_parts():
    def cs(n, step):
        j = lax.broadcasted_iota(jnp.int32, (n, 128), 1) % 64
        pos = (lax.broadcasted_iota(jnp.int32, (n, 128), 0) * step).astype(F32)
        ang = pos * jnp.power(ROPE_THETA, -(j % 8).astype(F32) / 8.0)
        return jnp.cos(ang), jnp.sin(ang)

    return (*cs(ROPE_BLK, 1), *cs(S // ROPE_BLK, ROPE_BLK))


def _rope_fill(co_ref, so_ref, cb_ref, sb_ref, rc_ref, ra_ref, rb_ref):
    j = lax.broadcasted_iota(jnp.int32, (ROPE_BLK, 128), 1) % 64
    co, so = co_ref[...], so_ref[...]
    for t in range(S // ROPE_BLK):
        cb, sb = cb_ref[t:t + 1, :], sb_ref[t:t + 1, :]
        cos, sin = cb * co - sb * so, sb * co + cb * so
        rows = slice(t * ROPE_BLK, (t + 1) * ROPE_BLK)
        rc_ref[rows, :] = jnp.where(j < 16, cos, 1.0)
        ra_ref[rows, :] = jnp.where(j < 8, -sin, 0.0)
        rb_ref[rows, :] = jnp.where((j >= 8) & (j < 16), sin, 0.0)


def _rope(blk, c, a, b):
    return blk * c + pltpu.roll(blk, 120, 1) * a + pltpu.roll(blk, 8, 1) * b


def _rope_bwd(d, c, a, b):
    return d * c + pltpu.roll(d * a, 8, 1) + pltpu.roll(d * b, 120, 1)


def _unrope(t, c, a, b):
    return jnp.concatenate([_bf(_rope_bwd(t[:, j * 128:(j + 1) * 128].astype(F32), c, a, b))
                            for j in range(t.shape[1] // 128)], axis=1)


MESH = pl.DeviceIdType.MESH
ANY = pl.BlockSpec(memory_space=pl.ANY)
VM = pl.BlockSpec(memory_space=pltpu.VMEM)
FLIPS = [(dx, dy, dc) for dx in (0, 1) for dy in (0, 1) for dc in (0, 1)][1:]


def _place():
    return lax.axis_index("x"), lax.axis_index("y"), lax.axis_index("c")


def _dev_index(px, py, pc):
    return 4 * px + 2 * py + pc


def _gather_phases(ins, outs, bufs, send_sems=None, recv_sems=None, local_sems=None):
    nw = len(ins)
    if nw == 0:
        return (lambda: None,) * 3
    x, y, c = _place()
    me, sib = (x, y, c), (x, y, 1 - c)
    chips = [(1 - x, y), (x, 1 - y), (1 - x, 1 - y)]

    def copy(w, k, block, to, from_buf=False):
        dst = outs[w].at[_dev_index(*block)]
        return pltpu.make_async_remote_copy(
            src_ref=bufs[w] if from_buf else dst, dst_ref=dst, send_sem=send_sems.at[w, k],
            recv_sem=recv_sems.at[w, k], device_id=to, device_id_type=MESH)

    def mine(w):
        return pltpu.make_async_copy(bufs[w], outs[w].at[_dev_index(*me)], local_sems.at[w])

    def first(w):
        return [copy(w, 0, me, sib, True)] + [copy(w, 1 + j, me, (*chip, c), True) for j, chip in enumerate(chips)]

    def passed(w):
        return [copy(w, 4 + j, (*chip, c), sib) for j, chip in enumerate(chips)]

    def start():
        for w in range(nw):
            bufs[w][...] = ins[w][...].astype(bufs[w].dtype)
        for w in range(nw):
            mine(w).start()
            for cp in first(w):
                cp.start()

    def forward():
        for j, chip in enumerate(chips):
            for w in range(nw):
                copy(w, 1 + j, (*chip, c), me).wait_recv()
                passed(w)[j].start()

    def finish():
        for w in range(nw):
            copy(w, 0, sib, me).wait_recv()
        for j, chip in enumerate(chips):
            for w in range(nw):
                copy(w, 4 + j, (*chip, 1 - c), me).wait_recv()
        for w in range(nw):
            for cp in first(w) + passed(w):
                cp.wait_send()
            mine(w).wait()

    return start, forward, finish


def _gather_scratch(shards, dtypes):
    nw = len(shards)
    if nw == 0:
        return []
    return ([pltpu.VMEM(s.shape, dt) for s, dt in zip(shards, dtypes)]
            + [pltpu.SemaphoreType.DMA((nw, 7)), pltpu.SemaphoreType.DMA((nw, 7)), pltpu.SemaphoreType.DMA((nw,))])


def _gather_shapes(shards, dtypes):
    return [jax.ShapeDtypeStruct((NDEV, *s.shape), dt) for s, dt in zip(shards, dtypes)]


def _scatter_phases(ins, outs, send_sems=None, recv_sems=None, local_sems=None):
    nw = len(ins)
    if nw == 0:
        return (lambda: None,) * 2
    x, y, c = _place()
    me = _dev_index(x, y, c)

    def copies():
        out = []
        for w in range(nw):
            out.append(pltpu.make_async_copy(ins[w].at[me], outs[w].at[me], local_sems.at[w]))
            for k, (dx, dy, dc) in enumerate(FLIPS):
                peer = ((x + dx) % 2, (y + dy) % 2, (c + dc) % 2)
                out.append(pltpu.make_async_remote_copy(
                    src_ref=ins[w].at[_dev_index(*peer)], dst_ref=outs[w].at[me], send_sem=send_sems.at[w, k],
                    recv_sem=recv_sems.at[w, k], device_id=peer, device_id_type=MESH))
        return out

    def start():
        for cp in copies():
            cp.start()

    def finish():
        for cp in copies():
            cp.wait()

    return start, finish


def _scatter_scratch(nw):
    if nw == 0:
        return []
    return [pltpu.SemaphoreType.DMA((nw, 7)), pltpu.SemaphoreType.DMA((nw, 7)), pltpu.SemaphoreType.DMA((nw,))]


TM = 512


def _join_w_in(wg):
    sw = IN_W // NDEV

    def body(wg_ref, w_ref):
        for j in range(NDEV):
            w_ref[:, sw * j:sw * (j + 1)] = wg_ref[j]
        w_ref[:, IN_W:PW] = jnp.zeros((D, PW - IN_W), BF16)

    return pl.pallas_call(body, name="join_w_in", out_shape=jax.ShapeDtypeStruct((D, PW), BF16),
                          compiler_params=_params())(wg)


def _in_proj(x, g1, w, rc, ra, rb, shards, dtypes):
    tm = TM
    nw = len(shards)
    nt = S // tm

    def body(*refs):
        x_ref, g_ref, w_ref, rc_ref, ra_ref, rb_ref = refs[:6]
        ins = refs[6:6 + nw]
        qkv_ref, mqk_ref, mv_ref, mo_ref, gt_ref, u_ref = refs[6 + nw:12 + nw]
        outs = refs[12 + nw:12 + 2 * nw]
        bufs = refs[12 + 2 * nw:12 + 3 * nw]
        ag_start, ag_forward, ag_finish = _gather_phases(ins, outs, bufs, *refs[12 + 3 * nw:])
        i = pl.program_id(0)
        pl.when(i == 0)(ag_start)
        pl.when(i == nt - 1)(ag_forward)
        n, _ = _rms(x_ref[...])
        u = _bf(n * g_ref[...])
        u_ref[...] = u
        c, a, b = rc_ref[...], ra_ref[...], rb_ref[...]
        for half in range(2):
            blk = _dot(u, w_ref[:, half * 512:(half + 1) * 512])
            for t in range(4):
                lo = half * 512 + t * 128
                qkv_ref[:, lo:lo + 128] = _rope(blk[:, t * 128:(t + 1) * 128], c, a, b)
        qkv_ref[:, 1024:1536] = _dot(u, w_ref[:, 1024:1536])
        mqk_ref[:, 0:512] = _dot(u, w_ref[:, 1536:2048])
        mqk_ref[:, 512:1024] = _dot(u, w_ref[:, 2048:2560])
        mv_ref[...] = _dot(u, w_ref[:, 2560:3072])
        mo_ref[...] = _dot(u, w_ref[:, 3072:3584])
        gt_ref[...] = _dot(u, w_ref[:, 3584:3712])
        pl.when(i == nt - 1)(ag_finish)

    row = lambda wd: pl.BlockSpec((tm, wd), lambda i: (i, 0))
    res = pl.pallas_call(
        body, name="in_proj", grid=(nt,),
        in_specs=[row(D), _cspec((1, D)), _cspec((D, PW)), row(128), row(128), row(128)] + [VM] * nw,
        out_specs=[row(1536), row(1024), row(512), row(512), row(128), row(D)] + [ANY] * nw,
        out_shape=[jax.ShapeDtypeStruct((S, 1536), F32), jax.ShapeDtypeStruct((S, 1024), F32),
                   jax.ShapeDtypeStruct((S, 512), F32), jax.ShapeDtypeStruct((S, 512), F32),
                   jax.ShapeDtypeStruct((S, 128), F32), jax.ShapeDtypeStruct((S, D), BF16)]
        + _gather_shapes(shards, dtypes),
        scratch_shapes=_gather_scratch(shards, dtypes),
        compiler_params=_params(1),
    )(x, g1, w, rc, ra, rb, *shards)
    return res[:6], res[6:]


DILATIONS = (16, 4, 1)


def _attn_valid(n):
    kd = lax.broadcasted_iota(jnp.int32, (128, 256), 1) - lax.broadcasted_iota(jnp.int32, (128, 256), 0)
    off = jnp.where(n == 0, 0, 128)
    return (kd <= off) & (kd >= off - 128)


def _attn_rows(d, r, n):
    if d == 1:
        q0 = pl.multiple_of(n * 128, 128)
        k0 = pl.multiple_of(jnp.maximum(n - 1, 0) * 128, 128)
        return pl.ds(q0, 128), pl.ds(k0, 256), _attn_valid(n)
    q0 = r + n * 128 * d
    k0 = r + jnp.maximum(n - 1, 0) * 128 * d
    return pl.ds(q0, 128, stride=d), pl.ds(k0, 256, stride=d), _attn_valid(n)


ATTN_GROUP = 4
ATTN_ITERS = S // 128 // ATTN_GROUP


def _attn_group(d, i):
    nb = S // (128 * d)
    if nb == 2:
        qi = lax.broadcasted_iota(jnp.int32, (256, 256), 0) - lax.broadcasted_iota(jnp.int32, (256, 256), 1)
        whole = [pl.ds((ATTN_GROUP // 2) * i + u, 256, stride=d) for u in range(ATTN_GROUP // 2)]
        return [(rows, rows, (qi >= 0) & (qi <= 128)) for rows in whole]
    if d == 1:
        return [_attn_rows(1, 0, i + ATTN_ITERS * u) for u in range(ATTN_GROUP)]
    return [_attn_rows(d, (i // nb) * ATTN_GROUP + u, i % nb) for u in range(ATTN_GROUP)]


def _head0(shape):
    return lax.broadcasted_iota(jnp.int32, shape, 1) < 64


def _stack_heads(t):
    h0 = _head0(t.shape)
    tb = _bf(t)
    zero = jnp.zeros_like(tb)
    return jnp.concatenate([jnp.where(h0, tb, zero), jnp.where(h0, zero, tb)], axis=0)


def _attn_fwd(qkv, shards, dtypes):
    nw = len(shards)

    def body(*refs):
        q_ref, k_ref, v_ref = refs[:3]
        ins = refs[3:3 + nw]
        o_ref, lse0_ref, lse1_ref = refs[3 + nw:6 + nw]
        outs = refs[6 + nw:6 + 2 * nw]
        m0, m1, l0, l1, acc = refs[6 + 2 * nw:11 + 2 * nw]
        bufs = refs[11 + 2 * nw:11 + 3 * nw]
        ag_start, ag_forward, ag_finish = _gather_phases(ins, outs, bufs, *refs[11 + 3 * nw:])
        hp = pl.program_id(0)
        pl.when(hp == 0)(ag_start)
        pl.when(hp == 3)(ag_forward)
        stats = (m0, m1, l0, l1, acc)

        def update(blocks, first):
            loaded = [([q_ref[rq, :], k_ref[rk, :], v_ref[rk, :]], None if first else [ref[rq, :] for ref in stats])
                      for rq, rk, _ in blocks]
            both = lambda a, b: jnp.concatenate([a, b], axis=0)
            ss = [jnp.where(both(valid, valid), _dot_nt(_stack_heads(q * 0.125), _bf(k)), NEG)
                  for ((q, k, _), _), (_, _, valid) in zip(loaded, blocks)]
            mcs = [jnp.max(s, axis=-1, keepdims=True) for s in ss]
            if first:
                m2s = [jnp.broadcast_to(mc, (mc.shape[0], 128)) for mc in mcs]
            else:
                m2s = [jnp.maximum(both(prev[0], prev[1]), mc) for mc, (_, prev) in zip(mcs, loaded)]
            ps = [jnp.exp(s - jnp.tile(m2, (1, 2))) for s, m2 in zip(ss, m2s)]
            l2s = [jnp.sum(p, axis=-1, keepdims=True) for p in ps]
            acc2s = [_dot(_bf(p), _bf(v)) for p, ((_, _, v), _) in zip(ps, loaded)]
            results = []
            for m2, l2, acc2, (_, prev) in zip(m2s, l2s, acc2s, loaded):
                nq = m2.shape[0] // 2
                if first:
                    l2 = jnp.broadcast_to(l2, (2 * nq, 128))
                else:
                    alpha = jnp.exp(both(prev[0], prev[1]) - m2)
                    l2, acc2 = alpha * both(prev[2], prev[3]) + l2, alpha * both(prev[4], prev[4]) + acc2
                results.append((m2[0:nq], m2[nq:2 * nq], l2[0:nq], l2[nq:2 * nq],
                                jnp.where(_head0((nq, 128)), acc2[0:nq], acc2[nq:2 * nq])))
            for (rq, _, _), res in zip(blocks, results):
                for ref, val in zip(stats, res):
                    ref[rq, :] = val

        for d in DILATIONS:
            def step(i, carry, d=d):
                update(_attn_group(d, i), d == DILATIONS[0])
                return carry

            lax.fori_loop(0, ATTN_ITERS, step, 0)

        def fin(t, carry):
            rows = pl.ds(pl.multiple_of(t * 256, 256), 256)
            h0 = lax.broadcasted_iota(jnp.int32, (256, 128), 1) < 64
            la, lb = l0[rows, :], l1[rows, :]
            o_ref[rows, :] = acc[rows, :] / jnp.where(h0, la, lb)
            lse0_ref[rows, :] = m0[rows, :] + jnp.log(la)
            lse1_ref[rows, :] = m1[rows, :] + jnp.log(lb)
            return carry

        lax.fori_loop(0, S // 256, fin, 0)
        pl.when(hp == 3)(ag_finish)

    col = lambda off: pl.BlockSpec((S, 128), lambda h, off=off: (0, off + h))
    res = pl.pallas_call(
        body, name="attn_fwd", grid=(4,),
        in_specs=[col(0), col(4), col(8)] + [VM] * nw,
        out_specs=[col(0), col(0), col(0)] + [ANY] * nw,
        out_shape=[jax.ShapeDtypeStruct((S, AW), F32)] * 3 + _gather_shapes(shards, dtypes),
        scratch_shapes=[pltpu.VMEM((S, 128), F32)] * 5 + _gather_scratch(shards, dtypes),
        compiler_params=_params(1),
    )(qkv, qkv, qkv, *shards)
    return res[0], (res[1], res[2]), res[3:]


def _attn_bwd(qkv, o, lse, do, parts):
    nw = len(parts)

    def body(*refs):
        q_ref, k_ref, v_ref, o_ref, L0, L1, do_ref = refs[:7]
        ins = refs[7:7 + nw]
        dq_out, dk_out, dv_out = refs[7 + nw:10 + nw]
        outs = refs[10 + nw:10 + 2 * nw]
        D0, D1, dq_ref, dk_ref, dv_ref = refs[10 + 2 * nw:15 + 2 * nw]
        rs_start, rs_finish = _scatter_phases(ins, outs, *refs[15 + 2 * nw:])
        hp = pl.program_id(0)
        pl.when(hp == 0)(rs_start)

        def pre(t, carry):
            rows = pl.ds(pl.multiple_of(t * 256, 256), 256)
            h0 = lax.broadcasted_iota(jnp.int32, (256, 128), 1) < 64
            dd = do_ref[rows, :] * o_ref[rows, :]
            shp = (256, 128)
            D0[rows, :] = jnp.broadcast_to(jnp.sum(jnp.where(h0, dd, 0.0), axis=-1, keepdims=True), shp)
            D1[rows, :] = jnp.broadcast_to(jnp.sum(jnp.where(h0, 0.0, dd), axis=-1, keepdims=True), shp)
            return carry

        lax.fori_loop(0, S // 256, pre, 0)

        def update(blocks, first):
            loaded = [([q_ref[rq, :], k_ref[rk, :], v_ref[rk, :], do_ref[rq, :]],
                       [L0[rq, :], L1[rq, :], D0[rq, :], D1[rq, :]],
                       [0.0] * 3 if first else [dq_ref[rq, :], dk_ref[rk, :], dv_ref[rk, :]]) for rq, rk, _ in blocks]
            cat = lambda a, b: jnp.tile(jnp.concatenate([a, b], axis=0), (1, 2))
            ops = [(_stack_heads(q * 0.125), _stack_heads(q), _stack_heads(dout), _bf(k), _bf(v))
                   for (q, k, v, dout), _, _ in loaded]
            ss = [jnp.where(jnp.concatenate([valid, valid], axis=0), _dot_nt(qs, kb), NEG)
                  for (qs, _, _, kb, _), (_, _, valid) in zip(ops, blocks)]
            dps = [_dot_nt(do2, vb) for _, _, do2, _, vb in ops]
            ps = [jnp.exp(s - cat(st[0], st[1])) for s, (_, st, _) in zip(ss, loaded)]
            dss = [_bf(p * (dp - cat(st[2], st[3])) * 0.125) for p, dp, (_, st, _) in zip(ps, dps, loaded)]
            dq2s = [_dot(ds, kb) for ds, (_, _, _, kb, _) in zip(dss, ops)]
            dks = [_dot_tn(ds, q2) for ds, (_, q2, _, _, _) in zip(dss, ops)]
            dvs = [_dot_tn(_bf(p), do2) for p, (_, _, do2, _, _) in zip(ps, ops)]
            results = []
            for (_, _, (dq, dk, dv)), dq2, dkk, dvv in zip(loaded, dq2s, dks, dvs):
                nq = dq2.shape[0] // 2
                results.append((dq + jnp.where(_head0((nq, 128)), dq2[0:nq], dq2[nq:2 * nq]), dk + dkk, dv + dvv))
            for (rq, rk, _), (dq, dk, dv) in zip(blocks, results):
                dq_ref[rq, :] = dq
                dk_ref[rk, :] = dk
                dv_ref[rk, :] = dv

        assert S // (128 * DILATIONS[0]) == 2
        for d in DILATIONS:
            def step(i, carry, d=d):
                update(_attn_group(d, i), d == DILATIONS[0])
                return carry

            lax.fori_loop(0, ATTN_ITERS, step, 0)

        def fin(t, carry):
            rows = pl.ds(pl.multiple_of(t * 256, 256), 256)
            for src, dst in ((dq_ref, dq_out), (dk_ref, dk_out), (dv_ref, dv_out)):
                dst[rows, :] = _bf(src[rows, :])
            return carry

        lax.fori_loop(0, S // 256, fin, 0)
        pl.when(hp == 3)(rs_finish)

    col = lambda off: pl.BlockSpec((S, 128), lambda h, off=off: (0, off + h))
    res = pl.pallas_call(
        body, name="attn_bwd", grid=(4,),
        in_specs=[col(0), col(4), col(8), col(0), col(0), col(0), col(0)] + [ANY] * nw,
        out_specs=[col(0), col(0), col(0)] + [ANY] * nw,
        out_shape=[jax.ShapeDtypeStruct((S, AW), BF16)] * 3 + [jax.ShapeDtypeStruct(a.shape, a.dtype) for a in parts],
        scratch_shapes=[pltpu.VMEM((S, 128), F32)] * 5 + _scatter_scratch(nw),
        compiler_params=_params(1),
    )(qkv, qkv, qkv, o, lse[0], lse[1], do, *parts)
    return res[0], res[1], res[2], res[3:]


def _logsig(x):
    return jnp.minimum(x, 0.0) - jnp.log1p(jnp.exp(-jnp.abs(x)))


def _conv_taps(xp, n):
    return [xp[8:] if j == 3 else pltpu.roll(xp, 3 - j, 0)[8:] for j in range(4)]


def _conv_silu(xp, w_ref, b_ref, n):
    taps = _conv_taps(xp, n)
    c = b_ref[...] + sum(w_ref[j:j + 1, :] * taps[j] for j in range(4))
    sg = _sigmoid(c)
    return c, sg, taps


def _chunk_gates(G):
    assert LC == 128
    r = lax.broadcasted_iota(jnp.int32, (LC, LC), 0)
    c = lax.broadcasted_iota(jnp.int32, (LC, LC), 1)
    tril = (c <= r).astype(F32)
    triu = (c >= r).astype(F32)
    b_col = jnp.dot(tril, _logsig(G), preferred_element_type=F32, precision=HI)
    return b_col, b_col.T, G.T, tril, triu


def _colpick(X, lane):
    li = lax.broadcasted_iota(jnp.int32, X.shape, 1)
    return jnp.sum(jnp.where(li == lane, X, 0.0), axis=1, keepdims=True)


def _rowpick(XT, row):
    ri = lax.broadcasted_iota(jnp.int32, XT.shape, 0)
    return jnp.sum(jnp.where(ri == row, XT, 0.0), axis=0, keepdims=True)


def _each(f, *lists):
    return [f(*a) for a in zip(*lists)]


def _mlstm_heads(Q, K, V, G, b_col, b_row, g_row, C, N, M):
    hs = range(len(Q))
    bt = [_colpick(b_col, 4 + h) for h in hs]
    i_col = [_colpick(G, h) for h in hs]
    bs = [_rowpick(b_row, 4 + h) for h in hs]
    i_row = [_rowpick(g_row, h) for h in hs]
    r = lax.broadcasted_iota(jnp.int32, (LC, LC), 0)
    c = lax.broadcasted_iota(jnp.int32, (LC, LC), 1)
    lane = lax.broadcasted_iota(jnp.int32, (1, LC), 1)
    qb, kb, vb = [_bf(t) for t in Q], [_bf(t) for t in K], [_bf(t) for t in V]
    S_ = _each(_dot_nt, qb, kb)
    qC = _each(lambda q, ch: _dot(q, _bf(ch)), qb, C)
    log_d = _each(lambda a, b, i: jnp.where(c <= r, a - b + i, NEG), bt, bs, i_row)
    log_inter = _each(lambda a, m: a + m, bt, M)
    m_t = _each(lambda li, ld: jnp.maximum(li, jnp.max(ld, axis=1, keepdims=True)), log_inter, log_d)
    Dm = _each(lambda ld, m: jnp.exp(ld - m), log_d, m_t)
    g = _each(lambda li, m: jnp.exp(li - m), log_inter, m_t)
    Am = _each(lambda s, d: s * d, S_, Dm)
    AV = _each(lambda a, v: _dot(_bf(a), v), Am, vb)
    num = _each(lambda gg, qc, av: gg * qc + av, g, qC, AV)
    qn = _each(lambda q, n: jnp.sum(q * n, axis=1, keepdims=True), Q, N)
    den = _each(lambda gg, x, a: gg * x + jnp.sum(a, axis=1, keepdims=True), g, qn, Am)
    floor = [jnp.exp(-m) for m in m_t]
    inv_dd = _each(lambda d, f: 1.0 / jnp.maximum(jnp.abs(d), f), den, floor)
    hh = _each(lambda n, i: n * i, num, inv_dd)
    blast = [jnp.sum(jnp.where(lane == LC - 1, b, 0.0), axis=1, keepdims=True) for b in bs]
    log_s = _each(lambda bl, a, i: bl - a + i, blast, bt, i_col)
    m_new = _each(lambda bl, m, ls: jnp.maximum(bl + m, jnp.max(ls, axis=0, keepdims=True)), blast, M, log_s)
    decay = _each(lambda bl, m, mn: jnp.exp(bl + m - mn), blast, M, m_new)
    ws = _each(lambda ls, mn: jnp.exp(ls - mn), log_s, m_new)
    kw = _each(lambda k, w: k * w, K, ws)
    KV = _each(lambda k, v: _dot_tn(_bf(k), v), kw, vb)
    C_new = _each(lambda d, ch, kv: d * ch + kv, decay, C, KV)
    n_new = _each(lambda d, n, k: d * n + jnp.sum(k, axis=0, keepdims=True), decay, N, kw)
    return dict(Dm=Dm, g=g, Am=Am, qC=qC, qn=qn, den=den, floor=floor, inv_dd=inv_dd, h=hh, decay=decay, ws=ws, kw=kw,
                C_new=C_new, n_new=n_new, m_new=m_new, qb=qb, kb=kb, vb=vb)


def _head_out(hh, mo_h, gn_h):
    r = lax.rsqrt(jnp.mean(hh * hh, axis=-1, keepdims=True) + EPS)
    hn = hh * r
    sg = _sigmoid(mo_h)
    return sg * (hn * gn_h), hn, r, sg


def _mlstm_fwd(mqk, mv, mo, gates, conv_w, conv_b, gate_b, gn, shards, dtypes):
    nblk = S // TB
    ncb = TB // LC
    nw = len(shards)

    def body(*refs):
        x_ref, v_ref, o_ref, g_ref, w_ref, b_ref, gb_ref, gn_ref = refs[:8]
        ins = refs[8:8 + nw]
        out_ref, cs_ref, ns_ref, ms_ref = refs[8 + nw:12 + nw]
        outs = refs[12 + nw:12 + 2 * nw]
        tail, Cst, nst, mst, qs, ks = refs[12 + 2 * nw:18 + 2 * nw]
        bufs = refs[18 + 2 * nw:18 + 3 * nw]
        ag_start, ag_forward, ag_finish = _gather_phases(ins, outs, bufs, *refs[18 + 3 * nw:])
        i = pl.program_id(0)
        pl.when(i == 0)(ag_start)
        pl.when(i == nblk - 3)(ag_forward)

        @pl.when(i == 0)
        def _():
            tail[...] = jnp.zeros_like(tail)
            Cst[...] = jnp.zeros_like(Cst)
            nst[...] = jnp.zeros_like(nst)
            mst[...] = jnp.zeros_like(mst)

        x = x_ref[...]
        xp = jnp.concatenate([tail[...], x], axis=0)
        tail[...] = x[TB - 8:TB, :]
        c, sg, _ = _conv_silu(xp, w_ref, b_ref, TB)
        y = c * sg
        qs[...] = y[:, 0:MW]
        ks[...] = y[:, MW:2 * MW] * (1.0 / math.sqrt(128.0))

        for cc in range(ncb):
            rows = slice(cc * LC, (cc + 1) * LC)
            G = g_ref[rows, :] + gb_ref[...]
            b_col, b_row, g_row, _, _ = _chunk_gates(G)
            cs_ref[cc] = Cst[...]
            ns_ref[cc] = nst[...]
            ms_ref[cc] = mst[...]
            lns = [slice(h * 128, (h + 1) * 128) for h in range(4)]
            f = _mlstm_heads([qs[rows, ln] for ln in lns], [ks[rows, ln] for ln in lns], [v_ref[rows, ln] for ln in lns],
                             G, b_col, b_row, g_row, [Cst[:, ln] for ln in lns], [nst[0:1, ln] for ln in lns],
                             [jnp.max(mst[0:1, ln], axis=1, keepdims=True) for ln in lns])
            outs = [_head_out(hh, o_ref[rows, ln], gn_ref[:, ln])[0] for hh, ln in zip(f["h"], lns)]
            for h, ln in enumerate(lns):
                out_ref[rows, ln] = outs[h]
                Cst[:, ln] = f["C_new"][h]
                nst[0:1, ln] = f["n_new"][h]
                mst[0:1, ln] = jnp.broadcast_to(f["m_new"][h], (1, 128))
        pl.when(i == nblk - 1)(ag_finish)

    row = lambda wd: pl.BlockSpec((TB, wd), lambda i: (i, 0))
    res = pl.pallas_call(
        body, name="mlstm_fwd", grid=(nblk,),
        in_specs=[row(1024), row(MW), row(MW), row(128), _cspec((4, 1024)), _cspec((1, 1024)), _cspec((1, 128)),
                  _cspec((1, MW))] + [VM] * nw,
        out_specs=[row(MW), pl.BlockSpec((ncb, 128, MW), lambda i: (i, 0, 0)),
                   pl.BlockSpec((ncb, 8, MW), lambda i: (i, 0, 0)), pl.BlockSpec((ncb, 8, MW), lambda i: (i, 0, 0))]
        + [ANY] * nw,
        out_shape=[jax.ShapeDtypeStruct((S, MW), F32), jax.ShapeDtypeStruct((S // LC, 128, MW), F32),
                   jax.ShapeDtypeStruct((S // LC, 8, MW), F32), jax.ShapeDtypeStruct((S // LC, 8, MW), F32)]
        + _gather_shapes(shards, dtypes),
        scratch_shapes=[pltpu.VMEM((8, 1024), F32), pltpu.VMEM((128, MW), F32), pltpu.VMEM((8, MW), F32),
                        pltpu.VMEM((8, MW), F32), pltpu.VMEM((TB, MW), F32), pltpu.VMEM((TB, MW), F32)]
        + _gather_scratch(shards, dtypes),
        compiler_params=_params(1),
    )(mqk, mv, mo, gates, conv_w, conv_b, gate_b, gn, *shards)
    return res[0], res[1], res[2], res[3], res[4:]


DM_V, DM_O, DM_G, DM_W = 1024, 1536, 2048, PW - 3 * AW


def _mlstm_bwd(mqk, mv, mo, gates, conv_w, conv_b, gate_b, gn, cs, ns, ms, dout, parts):
    assert len(parts) == 1
    nblk = S // TB
    ncb = TB // LC
    kscale = 1.0 / math.sqrt(128.0)
    nw = len(parts)

    def body(*refs):
        x_ref, xprev_ref, v_ref, o_ref, g_ref, w_ref, b_ref, gb_ref, gn_ref, cs_ref, ns_ref, ms_ref, do_ref = refs[:13]
        ins = refs[13:13 + nw]
        dm_ref, dw_ref, db_ref, dgn_ref, dgb_ref = refs[13 + nw:18 + nw]
        outs = refs[18 + nw:18 + 2 * nw]
        dCst, dnst, dyhead, qs, ks, dqk = refs[18 + 2 * nw:24 + 2 * nw]
        rs_start, rs_middle, rs_finish = _scatter2_phases(ins[0], outs[0], *refs[24 + 2 * nw:])
        i = pl.program_id(0)
        blk = nblk - 1 - i
        pl.when(i == 0)(rs_start)
        pl.when(i == 2)(rs_middle)

        @pl.when(i == 0)
        def _():
            dCst[...] = jnp.zeros_like(dCst)
            dnst[...] = jnp.zeros_like(dnst)
            dyhead[...] = jnp.zeros_like(dyhead)
            dw_ref[...] = jnp.zeros_like(dw_ref)
            db_ref[...] = jnp.zeros_like(db_ref)
            dgn_ref[...] = jnp.zeros_like(dgn_ref)
            dgb_ref[...] = jnp.zeros_like(dgb_ref)

        x = x_ref[...]
        xprev = jnp.where(blk == 0, 0.0, xprev_ref[...])
        xp = jnp.concatenate([xprev, x], axis=0)
        c, sg, taps = _conv_silu(xp, w_ref, b_ref, TB)
        y = c * sg
        qs[...] = y[:, 0:MW]
        ks[...] = y[:, MW:2 * MW] * kscale
        lane128 = lax.broadcasted_iota(jnp.int32, (LC, 128), 1)
        rowi = lax.broadcasted_iota(jnp.int32, (LC, 1), 0)

        for cc in reversed(range(ncb)):
            rows = slice(cc * LC, (cc + 1) * LC)
            G = g_ref[rows, :] + gb_ref[...]
            b_col, b_row, g_row, _, triu = _chunk_gates(G)
            lns = [slice(h * 128, (h + 1) * 128) for h in range(4)]
            C = [cs_ref[cc, :, ln] for ln in lns]
            N = [ns_ref[cc, 0:1, ln] for ln in lns]
            Q, Kk = [qs[rows, ln] for ln in lns], [ks[rows, ln] for ln in lns]
            dCn, dnn = [dCst[:, ln] for ln in lns], [dnst[0:1, ln] for ln in lns]
            gns, dos, mos = [gn_ref[:, ln] for ln in lns], [do_ref[rows, ln] for ln in lns], [o_ref[rows, ln] for ln in lns]
            f = _mlstm_heads(Q, Kk, [v_ref[rows, ln] for ln in lns], G, b_col, b_row, g_row, C, N,
                             [jnp.max(ms_ref[cc, 0:1, ln], axis=1, keepdims=True) for ln in lns])
            hh, inv_dd, den, g, Am, Dm = f["h"], f["inv_dd"], f["den"], f["g"], f["Am"], f["Dm"]
            qb, kb, vb, ws, decay = f["qb"], f["kb"], f["vb"], f["ws"], f["decay"]
            ho = _each(_head_out, hh, mos, gns)
            hn, r, sgo = [t[1] for t in ho], [t[2] for t in ho], [t[3] for t in ho]
            dmo = _each(lambda d, n, gn_h, s: _bf(d * (n * gn_h) * s * (1.0 - s)), dos, hn, gns, sgo)
            dhm = _each(lambda d, s: d * s, dos, sgo)
            dgn = _each(lambda d, n: jnp.sum(d * n, axis=0, keepdims=True), dhm, hn)
            dhn = _each(lambda d, gn_h: d * gn_h, dhm, gns)
            dh = _each(lambda rr, d, n: rr * (d - n * jnp.mean(d * n, axis=-1, keepdims=True)), r, dhn, hn)
            dnum = _each(lambda d, i: d * i, dh, inv_dd)
            ddd = _each(lambda d, x, i: -jnp.sum(d * x, axis=1, keepdims=True) * i, dh, hh, inv_dd)
            dden = _each(lambda dn_, fl, d: jnp.where(jnp.abs(dn_) >= fl, d * jnp.sign(dn_), 0.0), den, f["floor"], ddd)
            dnb = [_bf(t) for t in dnum]
            gd = _each(lambda gg, d: _bf(gg * d), g, dnum)
            gq = _each(lambda gg, d: gg * d, g, dden)
            dCb = [_bf(t) for t in dCn]
            dA = _each(lambda d, v, dd_: _dot_nt(d, v) + dd_, dnb, vb, dden)
            dv1 = _each(lambda a, d: _dot_tn(_bf(a), d), Am, dnb)
            dq1 = _each(lambda d, ch: _dot_nt(d, _bf(ch)), gd, C)
            dC1 = _each(_dot_tn, qb, gd)
            E = _each(lambda v, d, n: _dot_nt(v, d) + n, vb, dCb, dnn)
            dv2 = _each(lambda k, d: _dot(_bf(k), d), f["kw"], dCb)
            dS = _each(lambda a, d: _bf(a * d), dA, Dm)
            dq2 = _each(_dot, dS, kb)
            dk1 = _each(_dot_tn, dS, qb)
            dq = _each(lambda a, x, n, b: a + x * n + b, dq1, gq, N, dq2)
            dC = _each(lambda d, x, y: d * x + y, decay, dCn, dC1)
            dn = _each(lambda d, x, y, q: d * x + jnp.sum(y * q, axis=0, keepdims=True), decay, dnn, gq, Q)
            dg = _each(lambda d, qc, dd_, x: jnp.sum(d * qc, axis=1, keepdims=True) + dd_ * x, dnum, f["qC"], dden, f["qn"])
            Gm = _each(lambda a, b: a * b, dA, Am)
            gam = _each(lambda a, b: a * b, dg, g)
            dk = _each(lambda a, w, e: (a + w * e) * kscale, dk1, ws, E)
            om = _each(lambda e, k, w: jnp.sum(e * k, axis=1, keepdims=True) * w, E, Kk, ws)
            dv = _each(lambda a, b: _bf(a + b), dv1, dv2)
            ddecay = _each(lambda d, ch, dn_, n: jnp.sum(jnp.sum(d * ch, axis=1, keepdims=True), axis=0, keepdims=True)
                           + jnp.sum(dn_ * n, axis=1, keepdims=True), dCn, C, dnn, N)
            rows_g = [jnp.sum(t, axis=1, keepdims=True) for t in Gm]
            cols_g = [jnp.broadcast_to(jnp.sum(t, axis=0, keepdims=True), (LC, 128)).T for t in Gm]
            last = _each(lambda o, dd_, d: jnp.where(rowi == LC - 1, jnp.sum(o, axis=0, keepdims=True) + dd_ * d, 0.0),
                         om, ddecay, decay)
            db = _each(lambda a, b, o, l, cg: a + b - o + l - cg, rows_g, gam, om, last, cols_g)
            di = _each(lambda cg, o: cg + o, cols_g, om)
            dB = jnp.zeros((LC, 128), F32)
            dI = jnp.zeros((LC, 128), F32)
            for h, ln in enumerate(lns):
                dB = jnp.where(lane128 == 4 + h, db[h], dB)
                dI = jnp.where(lane128 == h, di[h], dI)
                dgn_ref[:, ln] = dgn_ref[:, ln] + dgn[h]
                dCst[:, ln] = dC[h]
                dnst[0:1, ln] = dn[h]
                dqk[rows, ln] = dq[h]
                dqk[rows, MW + h * 128:MW + (h + 1) * 128] = dk[h]
                dm_ref[rows, DM_O + h * 128:DM_O + (h + 1) * 128] = dmo[h]
                dm_ref[rows, DM_V + h * 128:DM_V + (h + 1) * 128] = dv[h]
            dlogf = jnp.dot(triu, dB, preferred_element_type=F32, precision=HI)
            dG = dI + dlogf * _sigmoid(-G)
            dG = jnp.where(lane128 < 8, dG, 0.0)
            dm_ref[rows, DM_G:DM_G + 128] = _bf(dG)
            dm_ref[rows, DM_G + 128:DM_W] = jnp.zeros((LC, DM_W - DM_G - 128), BF16)
            dgb_ref[...] = dgb_ref[...] + jnp.sum(dG, axis=0, keepdims=True)

        dy = dqk[...] * (sg * (1.0 + c * (1.0 - sg)))
        db_ref[...] = db_ref[...] + jnp.sum(dy, axis=0, keepdims=True)
        for j in range(4):
            dw_ref[j:j + 1, :] = dw_ref[j:j + 1, :] + jnp.sum(dy * taps[j], axis=0, keepdims=True)
        dyp = jnp.concatenate([dy, dyhead[...]], axis=0)
        dx = w_ref[3:4, :] * dy
        for j in range(3):
            dx = dx + w_ref[j:j + 1, :] * pltpu.roll(dyp, TB + 8 - (3 - j), 0)[0:TB]
        dm_ref[:, 0:DM_V] = _bf(dx)
        dyhead[...] = dy[0:8, :]
        pl.when(i == nblk - 1)(rs_finish)

    rrow = lambda wd: pl.BlockSpec((TB, wd), lambda i: (nblk - 1 - i, 0))
    st = lambda r: pl.BlockSpec((ncb, r, MW), lambda i: (nblk - 1 - i, 0, 0))
    prev8 = pl.BlockSpec((8, 1024), lambda i: (jnp.maximum((nblk - 1 - i) * (TB // 8) - 1, 0), 0))
    res = pl.pallas_call(
        body, name="mlstm_bwd", grid=(nblk,),
        in_specs=[rrow(1024), prev8, rrow(MW), rrow(MW), rrow(128), _cspec((4, 1024)), _cspec((1, 1024)),
                  _cspec((1, 128)), _cspec((1, MW)), st(128), st(8), st(8), rrow(MW)] + [ANY] * nw,
        out_specs=[rrow(DM_W),
                   pl.BlockSpec((4, 1024), lambda i: (0, 0)), pl.BlockSpec((1, 1024), lambda i: (0, 0)),
                   pl.BlockSpec((1, MW), lambda i: (0, 0)), pl.BlockSpec((1, 128), lambda i: (0, 0))] + [ANY] * nw,
        out_shape=[jax.ShapeDtypeStruct((S, DM_W), BF16),
                   jax.ShapeDtypeStruct((4, 1024), F32), jax.ShapeDtypeStruct((1, 1024), F32),
                   jax.ShapeDtypeStruct((1, MW), F32), jax.ShapeDtypeStruct((1, 128), F32)]
        + [jax.ShapeDtypeStruct((len(CHIP_FLIPS), *a.shape[1:]), a.dtype) for a in parts],
        scratch_shapes=[pltpu.VMEM((128, MW), F32), pltpu.VMEM((8, MW), F32), pltpu.VMEM((8, 1024), F32),
                        pltpu.VMEM((TB, MW), F32), pltpu.VMEM((TB, MW), F32), pltpu.VMEM((TB, 1024), F32)]
        + _scatter2_scratch(parts[0].shape[1:], parts[0].dtype),
        compiler_params=_params(1),
    )(mqk, mqk, mv, mo, gates, conv_w, conv_b, gate_b, gn, cs, ns, ms, dout, *parts)
    return res[:5], res[5:]


def _out_proj(x, attn, ml, w, g):
    tm = TM

    def body(x_ref, a_ref, m_ref, w_ref, g_ref, h_ref, u_ref):
        h1 = x_ref[...] + _dot(_bf(a_ref[...]), w_ref[0:AW, :]) + _dot(_bf(m_ref[...]), w_ref[AW:D, :])
        h_ref[...] = h1
        n, _ = _rms(h1)
        u_ref[...] = _bf(n * g_ref[...])

    row = lambda wd: pl.BlockSpec((tm, wd), lambda i: (i, 0))
    return pl.pallas_call(
        body, name="out_proj", grid=(S // tm,),
        in_specs=[row(D), row(AW), row(MW), _cspec((D, D)), _cspec((1, D))],
        out_specs=[row(D), row(D)],
        out_shape=[jax.ShapeDtypeStruct((S, D), F32), jax.ShapeDtypeStruct((S, D), BF16)],
        compiler_params=_params(1),
    )(x, attn, ml, w, g)


HALF = DFF // NDEV // 2


def _mlp_fwd(h1, u2, w_up, w_down_a, w_down_b, shards, dtypes):
    tm = TM
    nt = S // tm
    nw = len(shards)

    def body(*refs):
        h_ref, u_ref, wu_ref, wa_ref, wb_ref = refs[:5]
        ins = refs[5:5 + nw]
        a_ref, o_ref = refs[5 + nw:7 + nw]
        outs = refs[7 + nw:7 + 2 * nw]
        bufs = refs[7 + 2 * nw:7 + 3 * nw]
        ag_start, ag_forward, ag_finish = _gather_phases(ins, outs, bufs, *refs[7 + 3 * nw:])
        i = pl.program_id(0)
        pl.when(i == 0)(ag_start)
        pl.when(i == nt - 2)(ag_forward)
        u = u_ref[...]
        acc = h_ref[...]
        for c in range(NDEV):
            cols = slice(c * 512, (c + 1) * 512)
            a = _dot(u, wu_ref[c])
            a_ref[:, cols] = _bf(a)
            r = jnp.maximum(a, 0.0)
            r = _bf(r * r)
            acc = acc + _dot(r[:, 0:HALF], wa_ref[c]) + _dot(r[:, HALF:2 * HALF], wb_ref[c])
        o_ref[...] = acc
        pl.when(i == nt - 1)(ag_finish)

    row = lambda wd: pl.BlockSpec((tm, wd), lambda i: (i, 0))
    res = pl.pallas_call(
        body, name="mlp_fwd", grid=(nt,),
        in_specs=[row(D), row(D), _cspec((NDEV, D, DFF // NDEV)), _cspec((NDEV, HALF, D)), _cspec((NDEV, HALF, D))]
        + [VM] * nw,
        out_specs=[row(DFF), row(D)] + [ANY] * nw,
        out_shape=[jax.ShapeDtypeStruct((S, DFF), BF16), jax.ShapeDtypeStruct((S, D), F32)]
        + _gather_shapes(shards, dtypes),
        scratch_shapes=_gather_scratch(shards, dtypes),
        compiler_params=_params(1),
    )(h1, u2, w_up, w_down_a, w_down_b, *shards)
    return res[0], res[1], res[2:]


def _ple_loss(h2, p, target, w_pg, w_ple, g_ple, g_fin):
    tm = TM

    def body(h_ref, p_ref, t_ref, wg_ref, wp_ref, gp_ref, gf_ref,
             dh_ref, dwg_ref, dwp_ref, dgp_ref, dgf_ref, loss_ref, acc_g, acc_p):
        i = pl.program_id(0)

        @pl.when(i == 0)
        def _():
            acc_g[...] = jnp.zeros_like(acc_g)
            acc_p[...] = jnp.zeros_like(acc_p)
            dgp_ref[...] = jnp.zeros_like(dgp_ref)
            dgf_ref[...] = jnp.zeros_like(dgf_ref)
            loss_ref[...] = jnp.zeros_like(loss_ref)

        h2v = h_ref[...]
        n2, rs2 = _rms(h2v)
        u3 = _bf(n2 * gp_ref[...])
        gt = _sigmoid(_dot(u3, wg_ref[...]))
        pb = _bf(p_ref[...])
        e = jnp.concatenate([_dot(pb, wp_ref[j]) for j in range(NDEV)], axis=1)
        h3 = h2v + gt * e
        n3, rs3 = _rms(h3)
        err = n3 * gf_ref[...] - t_ref[...]
        loss_ref[...] = loss_ref[...] + 0.5 / D * jnp.sum(jnp.sum(err * err, axis=1, keepdims=True), axis=0, keepdims=True)
        dy = err * (1.0 / D)
        dgf_ref[...] = dgf_ref[...] + jnp.sum(dy * n3, axis=0, keepdims=True)
        dh3 = _rms_bwd(dy, n3, rs3, gf_ref[...])
        de = _bf(dh3 * gt)
        dz = _bf(dh3 * e * gt * (1.0 - gt))
        acc_p[...] = acc_p[...] + _dot_tn(pb, de)
        acc_g[...] = acc_g[...] + _dot_tn(u3, dz)
        du3 = _dot_nt(dz, wg_ref[...])
        dgp_ref[...] = dgp_ref[...] + jnp.sum(du3 * n2, axis=0, keepdims=True)
        dh_ref[...] = dh3 + _rms_bwd(du3, n2, rs2, gp_ref[...])

        @pl.when(i == S // tm - 1)
        def _():
            dwg_ref[...] = _bf(acc_g[...])
            for j in range(NDEV):
                dwp_ref[j] = _bf(acc_p[:, j * 128:(j + 1) * 128])

    row = lambda wd: pl.BlockSpec((tm, wd), lambda i: (i, 0))
    whole = lambda shp: pl.BlockSpec(shp, lambda i: (0,) * len(shp))
    return pl.pallas_call(
        body, name="ple_loss", grid=(S // tm,),
        in_specs=[row(D), row(PLE), row(D), _cspec((D, D)), _cspec((NDEV, PLE, 128)), _cspec((1, D)), _cspec((1, D))],
        out_specs=[row(D), whole((D, D)), whole((NDEV, PLE, 128)), whole((1, D)), whole((1, D)), whole((1, 1))],
        out_shape=[jax.ShapeDtypeStruct((S, D), F32), jax.ShapeDtypeStruct((D, D), BF16),
                   jax.ShapeDtypeStruct((NDEV, PLE, 128), BF16), jax.ShapeDtypeStruct((1, D), F32),
                   jax.ShapeDtypeStruct((1, D), F32), jax.ShapeDtypeStruct((1, 1), F32)],
        scratch_shapes=[pltpu.VMEM((D, D), F32), pltpu.VMEM((PLE, D), F32)],
        compiler_params=_params(1),
    )(h2, p, target, w_pg, w_ple, g_ple, g_fin)


def _mlp_bwd(dh2, a, h1, g, w_up, w_down_a, w_down_b, parts):
    tm = TM
    nt = S // tm
    nw = len(parts)

    def body(*refs):
        d_ref, a_ref, h_ref, g_ref, wu_ref, wa_ref, wb_ref = refs[:7]
        ins = refs[7:7 + nw]
        da_ref, dh1_ref, dg_ref = refs[7 + nw:10 + nw]
        outs = refs[10 + nw:10 + 2 * nw]
        rs_start, rs_finish = _scatter_phases(ins, outs, *refs[10 + 2 * nw:])
        i = pl.program_id(0)
        pl.when(i == 0)(rs_start)

        @pl.when(i == 0)
        def _():
            dg_ref[...] = jnp.zeros_like(dg_ref)

        dh2v = d_ref[...]
        db = _bf(dh2v)
        du = jnp.zeros((tm, D), F32)
        for c in range(NDEV):
            cols = slice(c * 512, (c + 1) * 512)
            dr = jnp.concatenate([_dot_nt(db, wa_ref[c]), _dot_nt(db, wb_ref[c])], axis=1)
            da = _bf(dr * (2.0 * jnp.maximum(a_ref[:, cols], 0.0)))
            da_ref[:, cols] = da
            du = du + _dot_nt(da, wu_ref[c])
        n, rs = _rms(h_ref[...])
        dg_ref[...] = dg_ref[...] + jnp.sum(du * n, axis=0, keepdims=True)
        dh1_ref[...] = dh2v + _rms_bwd(du, n, rs, g_ref[...])
        pl.when(i == nt - 1)(rs_finish)

    row = lambda wd: pl.BlockSpec((tm, wd), lambda i: (i, 0))
    res = pl.pallas_call(
        body, name="mlp_bwd", grid=(nt,),
        in_specs=[row(D), row(DFF), row(D), _cspec((1, D)), _cspec((NDEV, D, DFF // NDEV)), _cspec((NDEV, HALF, D)),
                  _cspec((NDEV, HALF, D))] + [ANY] * nw,
        out_specs=[row(DFF), row(D), pl.BlockSpec((1, D), lambda i: (0, 0))] + [ANY] * nw,
        out_shape=[jax.ShapeDtypeStruct((S, DFF), BF16), jax.ShapeDtypeStruct((S, D), F32),
                   jax.ShapeDtypeStruct((1, D), F32)] + [jax.ShapeDtypeStruct(p.shape, p.dtype) for p in parts],
        scratch_shapes=_scatter_scratch(nw),
        compiler_params=_params(1),
    )(dh2, a, h1, g, w_up, w_down_a, w_down_b, *parts)
    return res[0], res[1], res[2], res[3:]


def _out_proj_bwd(dh1, attn, ml, w):
    tm = TM

    def body(d_ref, a_ref, m_ref, w_ref, da_ref, dm_ref, dw_ref, acc):
        i = pl.program_id(0)

        @pl.when(i == 0)
        def _():
            acc[...] = jnp.zeros_like(acc)

        db = _bf(d_ref[...])
        dmix = _dot_nt(db, w_ref[...])
        da_ref[...] = dmix[:, 0:AW]
        dm_ref[...] = dmix[:, AW:D]
        acc[0:AW, :] = acc[0:AW, :] + _dot_tn(_bf(a_ref[...]), db)
        acc[AW:D, :] = acc[AW:D, :] + _dot_tn(_bf(m_ref[...]), db)

        @pl.when(i == S // tm - 1)
        def _():
            dw_ref[...] = _bf(acc[...])

    row = lambda wd: pl.BlockSpec((tm, wd), lambda i: (i, 0))
    return pl.pallas_call(
        body, name="out_proj_bwd", grid=(S // tm,),
        in_specs=[row(D), row(AW), row(MW), _cspec((D, D))],
        out_specs=[row(AW), row(MW), pl.BlockSpec((D, D), lambda i: (0, 0))],
        out_shape=[jax.ShapeDtypeStruct((S, AW), F32), jax.ShapeDtypeStruct((S, MW), F32),
                   jax.ShapeDtypeStruct((D, D), BF16)],
        scratch_shapes=[pltpu.VMEM((D, D), F32)],
        compiler_params=_params(1),
    )(dh1, attn, ml, w)


CHIP_FLIPS = [(0, 0), (0, 1), (1, 0), (1, 1)]


def _scatter2_phases(in_ref, out_ref, mine_v, sib_v, psum_v, loc_sems, d2d_send, d2d_recv, ici_send, ici_recv, own_sem):
    x, y, c = _place()
    chips = [((x + dx) % 2, (y + dy) % 2) for dx, dy in CHIP_FLIPS]
    nc = len(chips)

    def local(k):
        return pltpu.make_async_copy(in_ref.at[_dev_index(*chips[k], c)], mine_v.at[k], loc_sems.at[k])

    def to_sib(k):
        return pltpu.make_async_remote_copy(
            src_ref=in_ref.at[_dev_index(*chips[k], 1 - c)], dst_ref=sib_v.at[k], send_sem=d2d_send.at[k],
            recv_sem=d2d_recv.at[k], device_id=(x, y, 1 - c), device_id_type=MESH)

    def over_ici(k):
        return pltpu.make_async_remote_copy(
            src_ref=psum_v.at[k], dst_ref=out_ref.at[k], send_sem=ici_send.at[k - 1], recv_sem=ici_recv.at[k - 1],
            device_id=(*chips[k], c), device_id_type=MESH)

    def own():
        return pltpu.make_async_copy(psum_v.at[0], out_ref.at[0], own_sem)

    def start():
        for k in range(nc):
            to_sib(k).start()
            local(k).start()

    def middle():
        for k in (1, 2, 3, 0):
            local(k).wait()
            to_sib(k).wait_recv()
            psum_v[k] = _bf(mine_v[k].astype(F32) + sib_v[k].astype(F32))
            (over_ici(k) if k else own()).start()

    def finish():
        for k in range(1, nc):
            over_ici(k).wait()
        for k in range(nc):
            to_sib(k).wait_send()
        own().wait()

    return start, middle, finish


def _scatter2_scratch(shard, dtype):
    nc = len(CHIP_FLIPS)
    return ([pltpu.VMEM((nc, *shard), dtype)] * 3
            + [pltpu.SemaphoreType.DMA((nc,))] * 3 + [pltpu.SemaphoreType.DMA((nc - 1,))] * 2 + [pltpu.SemaphoreType.DMA])


def _in_proj_bwd(dparts, n_roped, rope, dh1, x, g1, w, part):
    tm = TM
    nt = S // tm
    widths = [d.shape[1] for d in dparts]
    assert sum(widths) == PW
    npar = len(dparts)

    def body(*refs):
        d_refs = refs[:npar]
        tabs = [t[...] for t in refs[npar:npar + 3]]
        dh_ref, x_ref, g_ref, w_ref, in_ref, dx_ref, dgsum_ref, out_ref = refs[npar + 3:npar + 11]
        rs_start, rs_middle, rs_finish = _scatter2_phases(in_ref, out_ref, *refs[npar + 11:npar + 20])
        dg_ref = refs[npar + 20]
        ar_start, ar_finish = _small_phases([dg_ref], dgsum_ref, *refs[npar + 21:])
        i = pl.program_id(0)
        pl.when(i == 0)(rs_start)
        pl.when(i == 1)(rs_middle)

        @pl.when(i == 0)
        def _():
            dg_ref[...] = jnp.zeros_like(dg_ref)

        du = jnp.zeros((tm, D), F32)
        off = 0
        for j, (d_ref, wd) in enumerate(zip(d_refs, widths)):
            nc = next(c for c in (768, 512) if wd % c == 0)
            for s in range(wd // nc):
                d = d_ref[:, s * nc:(s + 1) * nc]
                du = du + _dot_nt(_unrope(d, *tabs) if j < n_roped else d, w_ref[:, off + s * nc:off + (s + 1) * nc])
            off += wd
        n, rs = _rms(x_ref[...])
        dg_ref[...] = dg_ref[...] + jnp.sum(du * n, axis=0, keepdims=True)
        dx_ref[...] = dh_ref[...] + _rms_bwd(du, n, rs, g_ref[...])

        @pl.when(i == nt - 1)
        def _():
            ar_start()
            rs_finish()
            ar_finish()

    row = lambda wd: pl.BlockSpec((tm, wd), lambda i: (i, 0))
    shard = part.shape[1:]
    return pl.pallas_call(
        body, name="in_proj_bwd", grid=(nt,),
        in_specs=[row(wd) for wd in widths] + [row(128)] * 3 + [row(D), row(D), _cspec((1, D)), _cspec((D, PW)), ANY],
        out_specs=[row(D), VM, ANY],
        out_shape=[jax.ShapeDtypeStruct((S, D), F32), jax.ShapeDtypeStruct((8, 1024), F32),
                   jax.ShapeDtypeStruct((len(CHIP_FLIPS), *shard), part.dtype)],
        scratch_shapes=_scatter2_scratch(shard, part.dtype)
        + [pltpu.VMEM((1, D), F32), pltpu.VMEM((8, 1024), F32), pltpu.VMEM((NDEV, 8, 1024), F32),
           pltpu.SemaphoreType.DMA((7,)), pltpu.SemaphoreType.DMA((7,))],
        compiler_params=_params(1),
    )(*dparts, *rope, dh1, x, g1, w, part)


SMALL_ROWS = 96


def _small_phases(ins, out_ref, pack, rbuf, send_sems, recv_sems):
    x, y, c = _place()
    me = _dev_index(x, y, c)

    def copies():
        out = []
        for k, (dx, dy, dc) in enumerate(FLIPS):
            peer = ((x + dx) % 2, (y + dy) % 2, (c + dc) % 2)
            out.append(pltpu.make_async_remote_copy(
                src_ref=pack, dst_ref=rbuf.at[me], send_sem=send_sems.at[k], recv_sem=recv_sems.at[k],
                device_id=peer, device_id_type=MESH))
        return out

    def start():
        pack[...] = jnp.zeros_like(pack)
        for i, ref in enumerate(ins):
            pack[8 * i:8 * i + 1, 0:ref.shape[1]] = ref[...]
        rbuf[me] = pack[...]
        for cp in copies():
            cp.start()

    def finish():
        for cp in copies():
            cp.wait()
        tot = rbuf[0]
        for j in range(1, NDEV):
            tot = tot + rbuf[j]
        out_ref[...] = tot

    return start, finish


def _wgrad(name, A, Bs, a_fn, b_fn, out_shape, split=None, ts=512, small=(), rope=(), n_roped=0):
    K = A.shape[1]
    widths = [b.shape[1] for b in Bs]
    N = sum(widths)
    nb, ns, nrt = len(Bs) + len(rope), len(small), S // ts
    kc = min(K, 1024)

    def body(*refs):
        a_ref, b_refs = refs[0], refs[1:1 + len(Bs)]
        tabs = [t[...] for t in refs[1 + len(Bs):1 + nb]]
        o_ref = refs[1 + nb + ns]
        acc = refs[2 + nb + ns + bool(ns)]
        r = pl.program_id(0)
        if ns:
            sm_start, sm_finish = _small_phases(refs[1 + nb:1 + nb + ns], refs[2 + nb + ns], *refs[4 + nb + ns:])
            pl.when(r == 0)(sm_start)

        @pl.when(r == 0)
        def _():
            acc[...] = jnp.zeros_like(acc)

        bs, off = [], 0
        for i, (b_ref, w) in enumerate(zip(b_refs, widths)):
            nc = next(c for c in (1024, 768, 512) if w % c == 0)
            fn = (lambda t: _unrope(t, *tabs)) if i < n_roped else b_fn
            bs += [(off + c * nc, nc, fn(b_ref[:, c * nc:(c + 1) * nc])) for c in range(w // nc)]
            off += w
        for kk in range(K // kc):
            rows = slice(kk * kc, (kk + 1) * kc)
            at = a_fn(a_ref[:, rows]).T
            for lo, nc, b in bs:
                acc[rows, lo:lo + nc] = acc[rows, lo:lo + nc] + _dot(at, b)

        @pl.when(r == nrt - 1)
        def _():
            if split is None:
                o_ref[...] = _bf(acc[...])
            else:
                for j in range(NDEV):
                    o_ref[j] = _bf(acc[:, split * j:split * (j + 1)])

        if ns:
            pl.when(r == nrt - 1)(sm_finish)

    in_specs = ([pl.BlockSpec((ts, K), lambda r: (r, 0))] + [pl.BlockSpec((ts, w), lambda r: (r, 0)) for w in widths]
                + [pl.BlockSpec((ts, 128), lambda r: (r, 0))] * len(rope))
    out_spec = pl.BlockSpec(out_shape, lambda r: (0,) * len(out_shape))
    scratch = [pltpu.VMEM((K, N), F32)]
    if not ns:
        return pl.pallas_call(
            body, name=name, grid=(nrt,), in_specs=in_specs, out_specs=out_spec,
            out_shape=jax.ShapeDtypeStruct(out_shape, BF16), scratch_shapes=scratch, compiler_params=_params(1),
        )(A, *Bs, *rope)
    return pl.pallas_call(
        body, name=name, grid=(nrt,), in_specs=in_specs + [VM] * ns, out_specs=[out_spec, VM],
        out_shape=[jax.ShapeDtypeStruct(out_shape, BF16), jax.ShapeDtypeStruct((SMALL_ROWS, 1024), F32)],
        scratch_shapes=scratch + [pltpu.VMEM((SMALL_ROWS, 1024), F32), pltpu.VMEM((NDEV, SMALL_ROWS, 1024), F32),
                                  pltpu.SemaphoreType.DMA((7,)), pltpu.SemaphoreType.DMA((7,))],
        compiler_params=_params(1),
    )(A, *Bs, *rope, *small)


def _relu2_bf(a):
    r = jnp.maximum(a.astype(F32), 0.0)
    return _bf(r * r)


def _ident(a):
    return a


def _step(x, p, target, g1, conv_b, gate_b, gn, g_mlp, g_ple, g_fin, sh):
    (g_in, g_conv), (rc, ra, rb) = _gather_weights([sh["w_in"], sh["conv_w"]], [BF16, F32])
    conv_w = g_conv.transpose(1, 0, 2).reshape(4, 1024)
    w_in_p = _join_w_in(g_in)
    (qkv, mqk, mv, mo, gates, u1), (w_down_a,) = _in_proj(x, g1, w_in_p, rc, ra, rb, [sh["w_down"][0:HALF]], [BF16])
    attn, lse, (w_up8, w_out8) = _attn_fwd(qkv, [sh["w_up"], sh["w_out"]], [BF16] * 2)
    ml, cs, ns, ms, (w_down_b,) = _mlstm_fwd(mqk, mv, mo, gates, conv_w, conv_b, gate_b, gn,
                                             [sh["w_down"][HALF:2 * HALF]], [BF16])
    w_out = w_out8.reshape(D, D)
    h1, u2 = _out_proj(x, attn, ml, w_out, g_mlp)
    a, h2, (w_pg8, w_ple8) = _mlp_fwd(h1, u2, w_up8, w_down_a, w_down_b, [sh["w_ple_gate"], sh["w_ple"]], [BF16] * 2)
    w_pg = w_pg8.reshape(D, D)
    dh2, dw_pg, dw_ple8, dg_ple, dg_fin, loss = _ple_loss(h2, p, target, w_pg, w_ple8, g_ple, g_fin)
    da, dh1, dg_mlp, (r_pg, r_ple) = _mlp_bwd(dh2, a, h1, g_mlp, w_up8, w_down_a, w_down_b,
                                              [dw_pg.reshape(NDEV, D // NDEV, D), dw_ple8])
    dw_up8 = _wgrad("wgrad_up", u2, [da], _ident, _ident, (NDEV, D, DFF // NDEV), split=DFF // NDEV)
    dw_down = _wgrad("wgrad_down", a, [dh2], _relu2_bf, _bf, (DFF, D))
    d_attn, d_ml, dw_out = _out_proj_bwd(dh1, attn, ml, w_out)
    (dm, dconv_w, dconv_b, dgn, dgate_b), (r_down,) = _mlstm_bwd(
        mqk, mv, mo, gates, conv_w, conv_b, gate_b, gn, cs, ns, ms, d_ml, [dw_down.reshape(NDEV, DFF // NDEV, D)])
    dq, dk, dv, (r_up, r_out) = _attn_bwd(qkv, attn, lse, d_attn, [dw_up8, dw_out.reshape(NDEV, D // NDEV, D)])
    dparts = [dq, dk, dv, dm]
    small = [jnp.zeros((1, D), F32), dconv_b, dgate_b, dgn, dg_mlp, dg_ple, dg_fin, loss]
    dw_in8, total = _wgrad("wgrad_in", u1, dparts, _ident, _ident, (NDEV, D, IN_W // NDEV), split=IN_W // NDEV,
                           small=small + [dconv_w[j:j + 1] for j in range(4)], rope=(rc, ra, rb), n_roped=2)
    dx, dg1_sum, r_in = _in_proj_bwd(dparts, 2, (rc, ra, rb), dh1, x, g1, w_in_p, dw_in8)
    recv = dict(w_in=r_in, w_out=r_out, w_up=r_up, w_down=r_down, w_ple_gate=r_pg, w_ple=r_ple)
    return dx, recv, total, dg1_sum


def _gather_weights(shards, dtypes):
    nw = len(shards)

    def body(*refs):
        ins, parts = refs[:nw], refs[nw:nw + 4]
        outs, tables = refs[nw + 4:2 * nw + 4], refs[2 * nw + 4:2 * nw + 7]
        start, forward, finish = _gather_phases(ins, outs, refs[2 * nw + 7:3 * nw + 7], *refs[3 * nw + 7:])
        start()
        _rope_fill(*parts, *tables)
        forward()
        finish()

    res = pl.pallas_call(
        body, name="gather_weights",
        in_specs=[VM] * (nw + 4), out_specs=[ANY] * nw + [VM] * 3,
        out_shape=_gather_shapes(shards, dtypes) + [jax.ShapeDtypeStruct((S, 128), F32)] * 3,
        scratch_shapes=_gather_scratch(shards, dtypes),
        compiler_params=_params(),
    )(*shards, *_rope_parts())
    return res[:nw], res[nw:]


ADAM_STEPS = 8


def _adamw(items):
    n = len(items)

    def body(*refs):
        for i in range(n):
            g_ref, w_ref, m_ref, v_ref = refs[4 * i:4 * i + 4]
            go_ref, d_ref, mo_ref, vo_ref = refs[4 * n + 4 * i:4 * n + 4 * i + 4]
            g = g_ref[0].astype(F32)
            for j in range(1, g_ref.shape[0]):
                g = g + g_ref[j].astype(F32)
            go_ref[...] = g
            d_ref[...], mo_ref[...], vo_ref[...] = _adam_update(g, w_ref[...], m_ref[...], v_ref[...])

    in_specs, out_specs, out_shape, args = [], [], [], []
    for gparts, w, m, v in items:
        P, R, C = gparts.shape
        if R % (8 * ADAM_STEPS) == 0:
            tr = R // ADAM_STEPS
            row, gspec = pl.BlockSpec((tr, C), lambda i: (i, 0)), pl.BlockSpec((P, tr, C), lambda i: (0, i, 0))
        else:
            row, gspec = pl.BlockSpec((R, C), lambda i: (0, 0)), pl.BlockSpec((P, R, C), lambda i: (0, 0, 0))
        in_specs += [gspec, row, row, row]
        out_specs += [row] * 4
        out_shape += [jax.ShapeDtypeStruct((R, C), F32)] * 4
        args += [gparts, w, m, v]
    res = pl.pallas_call(
        body, name="adamw", grid=(ADAM_STEPS,), in_specs=in_specs, out_specs=out_specs, out_shape=out_shape,
        compiler_params=_params(1),
    )(*args)
    return [res[4 * i:4 * i + 4] for i in range(n)]


SMALL = ("norm_mix_g", "conv_b", "gate_b", "mlstm_norm_g", "norm_mlp_g", "norm_ple_g", "final_norm_g")


def _adam_update(g, w, m, v):
    c1 = 1.0 - ADAM_B1 ** ADAM_STEP
    c2 = 1.0 - ADAM_B2 ** ADAM_STEP
    m2 = ADAM_B1 * m + (1.0 - ADAM_B1) * g
    v2 = ADAM_B2 * v + (1.0 - ADAM_B2) * (g * g)
    return -ADAM_LR * ((m2 / c1) / (jnp.sqrt(v2 / c2) + ADAM_EPS) + ADAM_WD * w), m2, v2


def _adamw_small(total, first, ws, ms, vs):
    n = len(ws)

    def body(*refs):
        t_ref, f_ref = refs[:2]
        refs = refs[1:]
        outs = refs[1 + 3 * n:]
        for i in range(n):
            w_ref, m_ref, v_ref = refs[1 + i], refs[1 + n + i], refs[1 + 2 * n + i]
            g = (t_ref if i else f_ref)[8 * i:8 * i + 1, 0:w_ref.shape[1]]
            delta, m2, v2 = _adam_update(g, w_ref[...], m_ref[...], v_ref[...])
            for ref, val in zip(outs[4 * i:4 * i + 4], (g, delta, m2, v2)):
                ref[...] = val

    res = pl.pallas_call(
        body, name="adamw_small",
        out_shape=[jax.ShapeDtypeStruct(w.shape, F32) for w in ws for _ in range(4)],
        compiler_params=_params(),
    )(total, first, *ws, *ms, *vs)
    return [res[4 * i:4 * i + 4] for i in range(n)]


def kernel(x, p, norm_mix_g, w_in, conv_w, conv_b, gate_b, mlstm_norm_g, w_out, norm_mlp_g, w_up, w_down, norm_ple_g, w_ple_gate, w_ple, final_norm_g, loss_target, m_norm_mix_g, m_w_in, m_conv_w, m_conv_b, m_gate_b, m_mlstm_norm_g, m_w_out, m_norm_mlp_g, m_w_up, m_w_down, m_norm_ple_g, m_w_ple_gate, m_w_ple, m_final_norm_g, v_norm_mix_g, v_w_in, v_conv_w, v_conv_b, v_gate_b, v_mlstm_norm_g, v_w_out, v_norm_mlp_g, v_w_up, v_w_down, v_norm_ple_g, v_w_ple_gate, v_w_ple, v_final_norm_g):
    big_names = ("w_in", "conv_w", "w_out", "w_up", "w_down", "w_ple_gate", "w_ple")
    wts = dict(w_in=w_in, conv_w=conv_w, w_out=w_out, w_up=w_up, w_down=w_down, w_ple_gate=w_ple_gate, w_ple=w_ple)
    mom = dict(w_in=m_w_in, conv_w=m_conv_w, w_out=m_w_out, w_up=m_w_up, w_down=m_w_down, w_ple_gate=m_w_ple_gate,
               w_ple=m_w_ple)
    var = dict(w_in=v_w_in, conv_w=v_conv_w, w_out=v_w_out, w_up=v_w_up, w_down=v_w_down, w_ple_gate=v_w_ple_gate,
               w_ple=v_w_ple)
    sq = lambda a: a.reshape(a.shape[1:])
    fin = final_norm_g.reshape(1, D)
    dx, recv, total, first = _step(
        x[0], p[0, 0], loss_target[0], norm_mix_g, conv_b, jnp.pad(gate_b, ((0, 0), (0, 120))), mlstm_norm_g,
        norm_mlp_g, norm_ple_g, fin, {n: sq(wts[n]) for n in big_names})

    nrow = 8 * len(SMALL)
    me = _dev_index(*_place())
    conv_rows = total[nrow + 8:nrow + 40:8]
    recv["conv_w"] = lax.dynamic_slice_in_dim(conv_rows, me * 128, 128, axis=1).reshape(1, 4, 128)
    out = {}
    for n, res in zip(big_names, _adamw([(recv[n], sq(wts[n]), sq(mom[n]), sq(var[n])) for n in big_names])):
        out[n] = [t.reshape(wts[n].shape) for t in res]
    sw = dict(norm_mix_g=norm_mix_g, conv_b=conv_b, gate_b=gate_b, mlstm_norm_g=mlstm_norm_g, norm_mlp_g=norm_mlp_g,
              norm_ple_g=norm_ple_g, final_norm_g=fin)
    sm = dict(norm_mix_g=m_norm_mix_g, conv_b=m_conv_b, gate_b=m_gate_b, mlstm_norm_g=m_mlstm_norm_g,
              norm_mlp_g=m_norm_mlp_g, norm_ple_g=m_norm_ple_g, final_norm_g=m_final_norm_g.reshape(1, D))
    sv = dict(norm_mix_g=v_norm_mix_g, conv_b=v_conv_b, gate_b=v_gate_b, mlstm_norm_g=v_mlstm_norm_g,
              norm_mlp_g=v_norm_mlp_g, norm_ple_g=v_norm_ple_g, final_norm_g=v_final_norm_g.reshape(1, D))
    res = _adamw_small(total, first, [sw[n] for n in SMALL], [sm[n] for n in SMALL], [sv[n] for n in SMALL])
    for n, r in zip(SMALL, res):
        out[n] = [t.reshape(final_norm_g.shape) for t in r] if n == "final_norm_g" else list(r)
    order = ("norm_mix_g", "w_in", "conv_w", "conv_b", "gate_b", "mlstm_norm_g", "w_out", "norm_mlp_g", "w_up", "w_down",
             "norm_ple_g", "w_ple_gate", "w_ple", "final_norm_g")
    loss_all = total[nrow, 0]
    return (loss_all, dx[None], *[out[n][0] for n in order], *[out[n][1] for n in order],
            *[out[n][2] for n in order], *[out[n][3] for n in order])
```

```python
import math

import jax
import jax.numpy as jnp
from jax import lax
from jax.experimental import pallas as pl
from jax.experimental.pallas import tpu as pltpu

F32, BF16 = jnp.float32, jnp.bfloat16
S = 4096
D = 1024
AW = 512
MW = 512
DFF = 4096
PLE = 256
IN_W = 3592
PW = 3840
NDEV = 8
EPS = 1e-6
NEG = -1e30
LC = 128
TB = 256
ROPE_THETA = 500000.0
VMEM_LIMIT = 56 * 1024 * 1024
HI = lax.Precision.HIGHEST

ADAM_LR, ADAM_B1, ADAM_B2, ADAM_EPS, ADAM_WD, ADAM_STEP = 0.001, 0.9, 0.999, 1e-08, 0.01, 10


def _params(n_grid=0, **kw):
    sem = dict(dimension_semantics=("arbitrary",) * n_grid) if n_grid else {}
    return pltpu.CompilerParams(vmem_limit_bytes=VMEM_LIMIT, **sem, **kw)


def _cspec(shape):
    nd = len(shape)
    return pl.BlockSpec(shape, lambda *_: (0,) * nd, pipeline_mode=pl.Buffered(1))


def _dot(a, b):
    return jnp.dot(a, b, preferred_element_type=F32)


def _dot_nt(a, b):
    return lax.dot_general(a, b, (((1,), (1,)), ((), ())), preferred_element_type=F32)


def _dot_tn(a, b):
    return lax.dot_general(a, b, (((0,), (0,)), ((), ())), preferred_element_type=F32)


def _bf(x):
    return x.astype(BF16)


def _rms(x):
    rs = lax.rsqrt(jnp.mean(x * x, axis=-1, keepdims=True) + EPS)
    return x * rs, rs


def _rms_bwd(du, n, rs, g):
    dn = du * g
    return rs * (dn - n * jnp.mean(dn * n, axis=-1, keepdims=True))


def _sigmoid(x):
    return 1.0 / (1.0 + jnp.exp(-x))


ROPE_BLK = 512


def _rope_parts():
    def cs(n, step):
        j = lax.broadcasted_iota(jnp.int32, (n, 128), 1) % 64
        pos = (lax.broadcasted_iota(jnp.int32, (n, 128), 0) * step).astype(F32)
        ang = pos * jnp.power(ROPE_THETA, -(j % 8).astype(F32) / 8.0)
        return jnp.cos(ang), jnp.sin(ang)

    return (*cs(ROPE_BLK, 1), *cs(S // ROPE_BLK, ROPE_BLK))


def _rope_fill(co_ref, so_ref, cb_ref, sb_ref, rc_ref, ra_ref, rb_ref):
    j = lax.broadcasted_iota(jnp.int32, (ROPE_BLK, 128), 1) % 64
    co, so = co_ref[...], so_ref[...]
    for t in range(S // ROPE_BLK):
        cb, sb = cb_ref[t:t + 1, :], sb_ref[t:t + 1, :]
        cos, sin = cb * co - sb * so, sb * co + cb * so
        rows = slice(t * ROPE_BLK, (t + 1) * ROPE_BLK)
        rc_ref[rows, :] = jnp.where(j < 16, cos, 1.0)
        ra_ref[rows, :] = jnp.where(j < 8, -sin, 0.0)
        rb_ref[rows, :] = jnp.where((j >= 8) & (j < 16), sin, 0.0)


def _rope(blk, c, a, b):
    return blk * c + pltpu.roll(blk, 120, 1) * a + pltpu.roll(blk, 8, 1) * b


def _rope_bwd(d, c, a, b):
    return d * c + pltpu.roll(d * a, 8, 1) + pltpu.roll(d * b, 120, 1)


def _unrope(t, c, a, b):
    return jnp.concatenate([_bf(_rope_bwd(t[:, j * 128:(j + 1) * 128].astype(F32), c, a, b))
                            for j in range(t.shape[1] // 128)], axis=1)


MESH = pl.DeviceIdType.MESH
ANY = pl.BlockSpec(memory_space=pl.ANY)
VM = pl.BlockSpec(memory_space=pltpu.VMEM)
FLIPS = [(dx, dy, dc) for dx in (0, 1) for dy in (0, 1) for dc in (0, 1)][1:]


def _place():
    return lax.axis_index("x"), lax.axis_index("y"), lax.axis_index("c")


def _dev_index(px, py, pc):
    return 4 * px + 2 * py + pc


def _gather_phases(ins, outs, bufs, send_sems=None, recv_sems=None, local_sems=None):
    nw = len(ins)
    if nw == 0:
        return (lambda: None,) * 3
    x, y, c = _place()
    me, sib = (x, y, c), (x, y, 1 - c)
    chips = [(1 - x, y), (x, 1 - y), (1 - x, 1 - y)]

    def copy(w, k, block, to, from_buf=False):
        dst = outs[w].at[_dev_index(*block)]
        return pltpu.make_async_remote_copy(
            src_ref=bufs[w] if from_buf else dst, dst_ref=dst, send_sem=send_sems.at[w, k],
            recv_sem=recv_sems.at[w, k], device_id=to, device_id_type=MESH)

    def mine(w):
        return pltpu.make_async_copy(bufs[w], outs[w].at[_dev_index(*me)], local_sems.at[w])

    def first(w):
        return [copy(w, 0, me, sib, True)] + [copy(w, 1 + j, me, (*chip, c), True) for j, chip in enumerate(chips)]

    def passed(w):
        return [copy(w, 4 + j, (*chip, c), sib) for j, chip in enumerate(chips)]

    def start():
        for w in range(nw):
            bufs[w][...] = ins[w][...].astype(bufs[w].dtype)
        for w in range(nw):
            mine(w).start()
            for cp in first(w):
                cp.start()

    def forward():
        for j, chip in enumerate(chips):
            for w in range(nw):
                copy(w, 1 + j, (*chip, c), me).wait_recv()
                passed(w)[j].start()

    def finish():
        for w in range(nw):
            copy(w, 0, sib, me).wait_recv()
        for j, chip in enumerate(chips):
            for w in range(nw):
                copy(w, 4 + j, (*chip, 1 - c), me).wait_recv()
        for w in range(nw):
            for cp in first(w) + passed(w):
                cp.wait_send()
            mine(w).wait()

    return start, forward, finish


def _gather_scratch(shards, dtypes):
    nw = len(shards)
    if nw == 0:
        return []
    return ([pltpu.VMEM(s.shape, dt) for s, dt in zip(shards, dtypes)]
            + [pltpu.SemaphoreType.DMA((nw, 7)), pltpu.SemaphoreType.DMA((nw, 7)), pltpu.SemaphoreType.DMA((nw,))])


def _gather_shapes(shards, dtypes):
    return [jax.ShapeDtypeStruct((NDEV, *s.shape), dt) for s, dt in zip(shards, dtypes)]


def _scatter_phases(ins, outs, send_sems=None, recv_sems=None, local_sems=None):
    nw = len(ins)
    if nw == 0:
        return (lambda: None,) * 2
    x, y, c = _place()
    me = _dev_index(x, y, c)

    def copies():
        out = []
        for w in range(nw):
            out.append(pltpu.make_async_copy(ins[w].at[me], outs[w].at[me], local_sems.at[w]))
            for k, (dx, dy, dc) in enumerate(FLIPS):
                peer = ((x + dx) % 2, (y + dy) % 2, (c + dc) % 2)
                out.append(pltpu.make_async_remote_copy(
                    src_ref=ins[w].at[_dev_index(*peer)], dst_ref=outs[w].at[me], send_sem=send_sems.at[w, k],
                    recv_sem=recv_sems.at[w, k], device_id=peer, device_id_type=MESH))
        return out

    def start():
        for cp in copies():
            cp.start()

    def finish():
        for cp in copies():
            cp.wait()

    return start, finish


def _scatter_scratch(nw):
    if nw == 0:
        return []
    return [pltpu.SemaphoreType.DMA((nw, 7)), pltpu.SemaphoreType.DMA((nw, 7)), pltpu.SemaphoreType.DMA((nw,))]


TM = 512


def _join_w_in(wg):
    sw = IN_W // NDEV

    def body(wg_ref, w_ref):
        for j in range(NDEV):
            w_ref[:, sw * j:sw * (j + 1)] = wg_ref[j]
        w_ref[:, IN_W:PW] = jnp.zeros((D, PW - IN_W), BF16)

    return pl.pallas_call(body, name="join_w_in", out_shape=jax.ShapeDtypeStruct((D, PW), BF16),
                          compiler_params=_params())(wg)


def _in_proj(x, g1, w, rc, ra, rb, shards, dtypes):
    tm = TM
    nw = len(shards)
    nt = S // tm

    def body(*refs):
        x_ref, g_ref, w_ref, rc_ref, ra_ref, rb_ref = refs[:6]
        ins = refs[6:6 + nw]
        qkv_ref, mqk_ref, mv_ref, mo_ref, gt_ref, u_ref = refs[6 + nw:12 + nw]
        outs = refs[12 + nw:12 + 2 * nw]
        bufs = refs[12 + 2 * nw:12 + 3 * nw]
        ag_start, ag_forward, ag_finish = _gather_phases(ins, outs, bufs, *refs[12 + 3 * nw:])
        i = pl.program_id(0)
        pl.when(i == 0)(ag_start)
        pl.when(i == nt - 1)(ag_forward)
        n, _ = _rms(x_ref[...])
        u = _bf(n * g_ref[...])
        u_ref[...] = u
        c, a, b = rc_ref[...], ra_ref[...], rb_ref[...]
        for half in range(2):
            blk = _dot(u, w_ref[:, half * 512:(half + 1) * 512])
            for t in range(4):
                lo = half * 512 + t * 128
                qkv_ref[:, lo:lo + 128] = _rope(blk[:, t * 128:(t + 1) * 128], c, a, b)
        qkv_ref[:, 1024:1536] = _dot(u, w_ref[:, 1024:1536])
        mqk_ref[:, 0:512] = _dot(u, w_ref[:, 1536:2048])
        mqk_ref[:, 512:1024] = _dot(u, w_ref[:, 2048:2560])
        mv_ref[...] = _dot(u, w_ref[:, 2560:3072])
        mo_ref[...] = _dot(u, w_ref[:, 3072:3584])
        gt_ref[...] = _dot(u, w_ref[:, 3584:3712])
        pl.when(i == nt - 1)(ag_finish)

    row = lambda wd: pl.BlockSpec((tm, wd), lambda i: (i, 0))
    res = pl.pallas_call(
        body, name="in_proj", grid=(nt,),
        in_specs=[row(D), _cspec((1, D)), _cspec((D, PW)), row(128), row(128), row(128)] + [VM] * nw,
        out_specs=[row(1536), row(1024), row(512), row(512), row(128), row(D)] + [ANY] * nw,
        out_shape=[jax.ShapeDtypeStruct((S, 1536), F32), jax.ShapeDtypeStruct((S, 1024), F32),
                   jax.ShapeDtypeStruct((S, 512), F32), jax.ShapeDtypeStruct((S, 512), F32),
                   jax.ShapeDtypeStruct((S, 128), F32), jax.ShapeDtypeStruct((S, D), BF16)]
        + _gather_shapes(shards, dtypes),
        scratch_shapes=_gather_scratch(shards, dtypes),
        compiler_params=_params(1),
    )(x, g1, w, rc, ra, rb, *shards)
    return res[:6], res[6:]


DILATIONS = (16, 4, 1)


def _attn_valid(n):
    kd = lax.broadcasted_iota(jnp.int32, (128, 256), 1) - lax.broadcasted_iota(jnp.int32, (128, 256), 0)
    off = jnp.where(n == 0, 0, 128)
    return (kd <= off) & (kd >= off - 128)


def _attn_rows(d, r, n):
    if d == 1:
        q0 = pl.multiple_of(n * 128, 128)
        k0 = pl.multiple_of(jnp.maximum(n - 1, 0) * 128, 128)
        return pl.ds(q0, 128), pl.ds(k0, 256), _attn_valid(n)
    q0 = r + n * 128 * d
    k0 = r + jnp.maximum(n - 1, 0) * 128 * d
    return pl.ds(q0, 128, stride=d), pl.ds(k0, 256, stride=d), _attn_valid(n)


ATTN_GROUP = 4
ATTN_ITERS = S // 128 // ATTN_GROUP


def _attn_group(d, i):
    nb = S // (128 * d)
    if nb == 2:
        qi = lax.broadcasted_iota(jnp.int32, (256, 256), 0) - lax.broadcasted_iota(jnp.int32, (256, 256), 1)
        whole = [pl.ds((ATTN_GROUP // 2) * i + u, 256, stride=d) for u in range(ATTN_GROUP // 2)]
        return [(rows, rows, (qi >= 0) & (qi <= 128)) for rows in whole]
    if d == 1:
        return [_attn_rows(1, 0, i + ATTN_ITERS * u) for u in range(ATTN_GROUP)]
    return [_attn_rows(d, (i // nb) * ATTN_GROUP + u, i % nb) for u in range(ATTN_GROUP)]


def _head0(shape):
    return lax.broadcasted_iota(jnp.int32, shape, 1) < 64


def _stack_heads(t):
    h0 = _head0(t.shape)
    tb = _bf(t)
    zero = jnp.zeros_like(tb)
    return jnp.concatenate([jnp.where(h0, tb, zero), jnp.where(h0, zero, tb)], axis=0)


def _attn_fwd(qkv, shards, dtypes):
    nw = len(shards)

    def body(*refs):
        q_ref, k_ref, v_ref = refs[:3]
        ins = refs[3:3 + nw]
        o_ref, lse0_ref, lse1_ref = refs[3 + nw:6 + nw]
        outs = refs[6 + nw:6 + 2 * nw]
        m0, m1, l0, l1, acc = refs[6 + 2 * nw:11 + 2 * nw]
        bufs = refs[11 + 2 * nw:11 + 3 * nw]
        ag_start, ag_forward, ag_finish = _gather_phases(ins, outs, bufs, *refs[11 + 3 * nw:])
        hp = pl.program_id(0)
        pl.when(hp == 0)(ag_start)
        stats = (m0, m1, l0, l1, acc)

        def update(blocks, first):
            loaded = [([q_ref[rq, :], k_ref[rk, :], v_ref[rk, :]], None if first else [ref[rq, :] for ref in stats])
                      for rq, rk, _ in blocks]
            both = lambda a, b: jnp.concatenate([a, b], axis=0)
            ss = [jnp.where(both(valid, valid), _dot_nt(_stack_heads(q * 0.125), _bf(k)), NEG)
                  for ((q, k, _), _), (_, _, valid) in zip(loaded, blocks)]
            mcs = [jnp.max(s, axis=-1, keepdims=True) for s in ss]
            if first:
                m2s = [jnp.broadcast_to(mc, (mc.shape[0], 128)) for mc in mcs]
            else:
                m2s = [jnp.maximum(both(prev[0], prev[1]), mc) for mc, (_, prev) in zip(mcs, loaded)]
            ps = [jnp.exp(s - jnp.tile(m2, (1, 2))) for s, m2 in zip(ss, m2s)]
            l2s = [jnp.sum(p, axis=-1, keepdims=True) for p in ps]
            acc2s = [_dot(_bf(p), _bf(v)) for p, ((_, _, v), _) in zip(ps, loaded)]
            results = []
            for m2, l2, acc2, (_, prev) in zip(m2s, l2s, acc2s, loaded):
                nq = m2.shape[0] // 2
                if first:
                    l2 = jnp.broadcast_to(l2, (2 * nq, 128))
                else:
                    alpha = jnp.exp(both(prev[0], prev[1]) - m2)
                    l2, acc2 = alpha * both(prev[2], prev[3]) + l2, alpha * both(prev[4], prev[4]) + acc2
                results.append((m2[0:nq], m2[nq:2 * nq], l2[0:nq], l2[nq:2 * nq],
                                jnp.where(_head0((nq, 128)), acc2[0:nq], acc2[nq:2 * nq])))
            for (rq, _, _), res in zip(blocks, results):
                for ref, val in zip(stats, res):
                    ref[rq, :] = val

        for d in DILATIONS:
            def step(i, carry, d=d):
                update(_attn_group(d, i), d == DILATIONS[0])
                return carry

            lax.fori_loop(0, ATTN_ITERS, step, 0)
            if d == DILATIONS[1]:
                pl.when(hp == 3)(ag_forward)

        def fin(t, carry):
            rows = pl.ds(pl.multiple_of(t * 256, 256), 256)
            h0 = lax.broadcasted_iota(jnp.int32, (256, 128), 1) < 64
            la, lb = l0[rows, :], l1[rows, :]
            o_ref[rows, :] = acc[rows, :] / jnp.where(h0, la, lb)
            lse0_ref[rows, :] = m0[rows, :] + jnp.log(la)
            lse1_ref[rows, :] = m1[rows, :] + jnp.log(lb)
            return carry

        lax.fori_loop(0, S // 256, fin, 0)
        pl.when(hp == 3)(ag_finish)

    col = lambda off: pl.BlockSpec((S, 128), lambda h, off=off: (0, off + h))
    res = pl.pallas_call(
        body, name="attn_fwd", grid=(4,),
        in_specs=[col(0), col(4), col(8)] + [VM] * nw,
        out_specs=[col(0), col(0), col(0)] + [ANY] * nw,
        out_shape=[jax.ShapeDtypeStruct((S, AW), F32)] * 3 + _gather_shapes(shards, dtypes),
        scratch_shapes=[pltpu.VMEM((S, 128), F32)] * 5 + _gather_scratch(shards, dtypes),
        compiler_params=_params(1),
    )(qkv, qkv, qkv, *shards)
    return res[0], (res[1], res[2]), res[3:]


def _attn_bwd(qkv, o, lse, do, parts):
    nw = len(parts)

    def body(*refs):
        q_ref, k_ref, v_ref, o_ref, L0, L1, do_ref = refs[:7]
        ins = refs[7:7 + nw]
        dq_out, dk_out, dv_out = refs[7 + nw:10 + nw]
        outs = refs[10 + nw:10 + 2 * nw]
        D0, D1, dq_ref, dk_ref, dv_ref = refs[10 + 2 * nw:15 + 2 * nw]
        rs_start, rs_finish = _scatter_phases(ins, outs, *refs[15 + 2 * nw:])
        hp = pl.program_id(0)
        pl.when(hp == 0)(rs_start)

        def pre(t, carry):
            rows = pl.ds(pl.multiple_of(t * 256, 256), 256)
            h0 = lax.broadcasted_iota(jnp.int32, (256, 128), 1) < 64
            dd = do_ref[rows, :] * o_ref[rows, :]
            shp = (256, 128)
            D0[rows, :] = jnp.broadcast_to(jnp.sum(jnp.where(h0, dd, 0.0), axis=-1, keepdims=True), shp)
            D1[rows, :] = jnp.broadcast_to(jnp.sum(jnp.where(h0, 0.0, dd), axis=-1, keepdims=True), shp)
            return carry

        lax.fori_loop(0, S // 256, pre, 0)

        def update(blocks, first):
            loaded = [([q_ref[rq, :], k_ref[rk, :], v_ref[rk, :], do_ref[rq, :]],
                       [L0[rq, :], L1[rq, :], D0[rq, :], D1[rq, :]],
                       [0.0] * 3 if first else [dq_ref[rq, :], dk_ref[rk, :], dv_ref[rk, :]]) for rq, rk, _ in blocks]
            cat = lambda a, b: jnp.tile(jnp.concatenate([a, b], axis=0), (1, 2))
            ops = [(_stack_heads(q * 0.125), _stack_heads(q), _stack_heads(dout), _bf(k), _bf(v))
                   for (q, k, v, dout), _, _ in loaded]
            ss = [jnp.where(jnp.concatenate([valid, valid], axis=0), _dot_nt(qs, kb), NEG)
                  for (qs, _, _, kb, _), (_, _, valid) in zip(ops, blocks)]
            dps = [_dot_nt(do2, vb) for _, _, do2, _, vb in ops]
            ps = [jnp.exp(s - cat(st[0], st[1])) for s, (_, st, _) in zip(ss, loaded)]
            dss = [_bf(p * (dp - cat(st[2], st[3])) * 0.125) for p, dp, (_, st, _) in zip(ps, dps, loaded)]
            dq2s = [_dot(ds, kb) for ds, (_, _, _, kb, _) in zip(dss, ops)]
            dks = [_dot_tn(ds, q2) for ds, (_, q2, _, _, _) in zip(dss, ops)]
            dvs = [_dot_tn(_bf(p), do2) for p, (_, _, do2, _, _) in zip(ps, ops)]
            results = []
            for (_, _, (dq, dk, dv)), dq2, dkk, dvv in zip(loaded, dq2s, dks, dvs):
                nq = dq2.shape[0] // 2
                results.append((dq + jnp.where(_head0((nq, 128)), dq2[0:nq], dq2[nq:2 * nq]), dk + dkk, dv + dvv))
            for (rq, rk, _), (dq, dk, dv) in zip(blocks, results):
                dq_ref[rq, :] = dq
                dk_ref[rk, :] = dk
                dv_ref[rk, :] = dv

        assert S // (128 * DILATIONS[0]) == 2
        for d in DILATIONS:
            def step(i, carry, d=d):
                update(_attn_group(d, i), d == DILATIONS[0])
                return carry

            lax.fori_loop(0, ATTN_ITERS, step, 0)

        def fin(t, carry):
            rows = pl.ds(pl.multiple_of(t * 256, 256), 256)
            for src, dst in ((dq_ref, dq_out), (dk_ref, dk_out), (dv_ref, dv_out)):
                dst[rows, :] = _bf(src[rows, :])
            return carry

        lax.fori_loop(0, S // 256, fin, 0)
        pl.when(hp == 3)(rs_finish)

    col = lambda off: pl.BlockSpec((S, 128), lambda h, off=off: (0, off + h))
    res = pl.pallas_call(
        body, name="attn_bwd", grid=(4,),
        in_specs=[col(0), col(4), col(8), col(0), col(0), col(0), col(0)] + [ANY] * nw,
        out_specs=[col(0), col(0), col(0)] + [ANY] * nw,
        out_shape=[jax.ShapeDtypeStruct((S, AW), BF16)] * 3 + [jax.ShapeDtypeStruct(a.shape, a.dtype) for a in parts],
        scratch_shapes=[pltpu.VMEM((S, 128), F32)] * 5 + _scatter_scratch(nw),
        compiler_params=_params(1),
    )(qkv, qkv, qkv, o, lse[0], lse[1], do, *parts)
    return res[0], res[1], res[2], res[3:]


def _logsig(x):
    return jnp.minimum(x, 0.0) - jnp.log1p(jnp.exp(-jnp.abs(x)))


def _conv_taps(xp, n):
    return [xp[8:] if j == 3 else pltpu.roll(xp, 3 - j, 0)[8:] for j in range(4)]


def _conv_silu(xp, w_ref, b_ref, n):
    taps = _conv_taps(xp, n)
    c = b_ref[...] + sum(w_ref[j:j + 1, :] * taps[j] for j in range(4))
    sg = _sigmoid(c)
    return c, sg, taps


def _chunk_gates(G):
    assert LC == 128
    r = lax.broadcasted_iota(jnp.int32, (LC, LC), 0)
    c = lax.broadcasted_iota(jnp.int32, (LC, LC), 1)
    tril = (c <= r).astype(F32)
    triu = (c >= r).astype(F32)
    b_col = jnp.dot(tril, _logsig(G), preferred_element_type=F32, precision=HI)
    return b_col, b_col.T, G.T, tril, triu


def _colpick(X, lane):
    li = lax.broadcasted_iota(jnp.int32, X.shape, 1)
    return jnp.sum(jnp.where(li == lane, X, 0.0), axis=1, keepdims=True)


def _rowpick(XT, row):
    ri = lax.broadcasted_iota(jnp.int32, XT.shape, 0)
    return jnp.sum(jnp.where(ri == row, XT, 0.0), axis=0, keepdims=True)


def _each(f, *lists):
    return [f(*a) for a in zip(*lists)]


def _mlstm_heads(Q, K, V, G, b_col, b_row, g_row, C, N, M):
    hs = range(len(Q))
    bt = [_colpick(b_col, 4 + h) for h in hs]
    i_col = [_colpick(G, h) for h in hs]
    bs = [_rowpick(b_row, 4 + h) for h in hs]
    i_row = [_rowpick(g_row, h) for h in hs]
    r = lax.broadcasted_iota(jnp.int32, (LC, LC), 0)
    c = lax.broadcasted_iota(jnp.int32, (LC, LC), 1)
    lane = lax.broadcasted_iota(jnp.int32, (1, LC), 1)
    qb, kb, vb = [_bf(t) for t in Q], [_bf(t) for t in K], [_bf(t) for t in V]
    S_ = _each(_dot_nt, qb, kb)
    qC = _each(lambda q, ch: _dot(q, _bf(ch)), qb, C)
    log_d = _each(lambda a, b, i: jnp.where(c <= r, a - b + i, NEG), bt, bs, i_row)
    log_inter = _each(lambda a, m: a + m, bt, M)
    m_t = _each(lambda li, ld: jnp.maximum(li, jnp.max(ld, axis=1, keepdims=True)), log_inter, log_d)
    Dm = _each(lambda ld, m: jnp.exp(ld - m), log_d, m_t)
    g = _each(lambda li, m: jnp.exp(li - m), log_inter, m_t)
    Am = _each(lambda s, d: s * d, S_, Dm)
    AV = _each(lambda a, v: _dot(_bf(a), v), Am, vb)
    num = _each(lambda gg, qc, av: gg * qc + av, g, qC, AV)
    qn = _each(lambda q, n: jnp.sum(q * n, axis=1, keepdims=True), Q, N)
    den = _each(lambda gg, x, a: gg * x + jnp.sum(a, axis=1, keepdims=True), g, qn, Am)
    floor = [jnp.exp(-m) for m in m_t]
    inv_dd = _each(lambda d, f: 1.0 / jnp.maximum(jnp.abs(d), f), den, floor)
    hh = _each(lambda n, i: n * i, num, inv_dd)
    blast = [jnp.sum(jnp.where(lane == LC - 1, b, 0.0), axis=1, keepdims=True) for b in bs]
    log_s = _each(lambda bl, a, i: bl - a + i, blast, bt, i_col)
    m_new = _each(lambda bl, m, ls: jnp.maximum(bl + m, jnp.max(ls, axis=0, keepdims=True)), blast, M, log_s)
    decay = _each(lambda bl, m, mn: jnp.exp(bl + m - mn), blast, M, m_new)
    ws = _each(lambda ls, mn: jnp.exp(ls - mn), log_s, m_new)
    kw = _each(lambda k, w: k * w, K, ws)
    KV = _each(lambda k, v: _dot_tn(_bf(k), v), kw, vb)
    C_new = _each(lambda d, ch, kv: d * ch + kv, decay, C, KV)
    n_new = _each(lambda d, n, k: d * n + jnp.sum(k, axis=0, keepdims=True), decay, N, kw)
    return dict(Dm=Dm, g=g, Am=Am, qC=qC, qn=qn, den=den, floor=floor, inv_dd=inv_dd, h=hh, decay=decay, ws=ws, kw=kw,
                C_new=C_new, n_new=n_new, m_new=m_new, qb=qb, kb=kb, vb=vb)


def _head_out(hh, mo_h, gn_h):
    r = lax.rsqrt(jnp.mean(hh * hh, axis=-1, keepdims=True) + EPS)
    hn = hh * r
    sg = _sigmoid(mo_h)
    return sg * (hn * gn_h), hn, r, sg


def _mlstm_fwd(mqk, mv, mo, gates, conv_w, conv_b, gate_b, gn, shards, dtypes):
    nblk = S // TB
    ncb = TB // LC
    nw = len(shards)

    def body(*refs):
        x_ref, v_ref, o_ref, g_ref, w_ref, b_ref, gb_ref, gn_ref = refs[:8]
        ins = refs[8:8 + nw]
        out_ref, cs_ref, ns_ref, ms_ref = refs[8 + nw:12 + nw]
        outs = refs[12 + nw:12 + 2 * nw]
        tail, Cst, nst, mst, qs, ks = refs[12 + 2 * nw:18 + 2 * nw]
        bufs = refs[18 + 2 * nw:18 + 3 * nw]
        ag_start, ag_forward, ag_finish = _gather_phases(ins, outs, bufs, *refs[18 + 3 * nw:])
        i = pl.program_id(0)
        pl.when(i == 0)(ag_start)
        pl.when(i == nblk - 3)(ag_forward)

        @pl.when(i == 0)
        def _():
            tail[...] = jnp.zeros_like(tail)
            Cst[...] = jnp.zeros_like(Cst)
            nst[...] = jnp.zeros_like(nst)
            mst[...] = jnp.zeros_like(mst)

        x = x_ref[...]
        xp = jnp.concatenate([tail[...], x], axis=0)
        tail[...] = x[TB - 8:TB, :]
        c, sg, _ = _conv_silu(xp, w_ref, b_ref, TB)
        y = c * sg
        qs[...] = y[:, 0:MW]
        ks[...] = y[:, MW:2 * MW] * (1.0 / math.sqrt(128.0))

        for cc in range(ncb):
            rows = slice(cc * LC, (cc + 1) * LC)
            G = g_ref[rows, :] + gb_ref[...]
            b_col, b_row, g_row, _, _ = _chunk_gates(G)
            cs_ref[cc] = Cst[...]
            ns_ref[cc] = nst[...]
            ms_ref[cc] = mst[...]
            lns = [slice(h * 128, (h + 1) * 128) for h in range(4)]
            f = _mlstm_heads([qs[rows, ln] for ln in lns], [ks[rows, ln] for ln in lns], [v_ref[rows, ln] for ln in lns],
                             G, b_col, b_row, g_row, [Cst[:, ln] for ln in lns], [nst[0:1, ln] for ln in lns],
                             [jnp.max(mst[0:1, ln], axis=1, keepdims=True) for ln in lns])
            outs = [_head_out(hh, o_ref[rows, ln], gn_ref[:, ln])[0] for hh, ln in zip(f["h"], lns)]
            for h, ln in enumerate(lns):
                out_ref[rows, ln] = outs[h]
                Cst[:, ln] = f["C_new"][h]
                nst[0:1, ln] = f["n_new"][h]
                mst[0:1, ln] = jnp.broadcast_to(f["m_new"][h], (1, 128))
        pl.when(i == nblk - 1)(ag_finish)

    row = lambda wd: pl.BlockSpec((TB, wd), lambda i: (i, 0))
    res = pl.pallas_call(
        body, name="mlstm_fwd", grid=(nblk,),
        in_specs=[row(1024), row(MW), row(MW), row(128), _cspec((4, 1024)), _cspec((1, 1024)), _cspec((1, 128)),
                  _cspec((1, MW))] + [VM] * nw,
        out_specs=[row(MW), pl.BlockSpec((ncb, 128, MW), lambda i: (i, 0, 0)),
                   pl.BlockSpec((ncb, 8, MW), lambda i: (i, 0, 0)), pl.BlockSpec((ncb, 8, MW), lambda i: (i, 0, 0))]
        + [ANY] * nw,
        out_shape=[jax.ShapeDtypeStruct((S, MW), F32), jax.ShapeDtypeStruct((S // LC, 128, MW), F32),
                   jax.ShapeDtypeStruct((S // LC, 8, MW), F32), jax.ShapeDtypeStruct((S // LC, 8, MW), F32)]
        + _gather_shapes(shards, dtypes),
        scratch_shapes=[pltpu.VMEM((8, 1024), F32), pltpu.VMEM((128, MW), F32), pltpu.VMEM((8, MW), F32),
                        pltpu.VMEM((8, MW), F32), pltpu.VMEM((TB, MW), F32), pltpu.VMEM((TB, MW), F32)]
        + _gather_scratch(shards, dtypes),
        compiler_params=_params(1),
    )(mqk, mv, mo, gates, conv_w, conv_b, gate_b, gn, *shards)
    return res[0], res[1], res[2], res[3], res[4:]


DM_V, DM_O, DM_G, DM_W = 1024, 1536, 2048, PW - 3 * AW


def _mlstm_bwd(mqk, mv, mo, gates, conv_w, conv_b, gate_b, gn, cs, ns, ms, dout, parts):
    assert len(parts) == 1
    nblk = S // TB
    ncb = TB // LC
    kscale = 1.0 / math.sqrt(128.0)
    nw = len(parts)

    def body(*refs):
        x_ref, xprev_ref, v_ref, o_ref, g_ref, w_ref, b_ref, gb_ref, gn_ref, cs_ref, ns_ref, ms_ref, do_ref = refs[:13]
        ins = refs[13:13 + nw]
        dm_ref, dw_ref, db_ref, dgn_ref, dgb_ref = refs[13 + nw:18 + nw]
        outs = refs[18 + nw:18 + 2 * nw]
        dCst, dnst, dyhead, qs, ks, dqk = refs[18 + 2 * nw:24 + 2 * nw]
        rs_start, rs_middle, rs_finish = _scatter2_phases(ins[0], outs[0], *refs[24 + 2 * nw:])
        i = pl.program_id(0)
        blk = nblk - 1 - i
        pl.when(i == 0)(rs_start)
        pl.when(i == 2)(rs_middle)

        @pl.when(i == 0)
        def _():
            dCst[...] = jnp.zeros_like(dCst)
            dnst[...] = jnp.zeros_like(dnst)
            dyhead[...] = jnp.zeros_like(dyhead)
            dw_ref[...] = jnp.zeros_like(dw_ref)
            db_ref[...] = jnp.zeros_like(db_ref)
            dgn_ref[...] = jnp.zeros_like(dgn_ref)
            dgb_ref[...] = jnp.zeros_like(dgb_ref)

        x = x_ref[...]
        xprev = jnp.where(blk == 0, 0.0, xprev_ref[...])
        xp = jnp.concatenate([xprev, x], axis=0)
        c, sg, taps = _conv_silu(xp, w_ref, b_ref, TB)
        y = c * sg
        qs[...] = y[:, 0:MW]
        ks[...] = y[:, MW:2 * MW] * kscale
        lane128 = lax.broadcasted_iota(jnp.int32, (LC, 128), 1)
        rowi = lax.broadcasted_iota(jnp.int32, (LC, 1), 0)

        for cc in reversed(range(ncb)):
            rows = slice(cc * LC, (cc + 1) * LC)
            G = g_ref[rows, :] + gb_ref[...]
            b_col, b_row, g_row, _, triu = _chunk_gates(G)
            lns = [slice(h * 128, (h + 1) * 128) for h in range(4)]
            C = [cs_ref[cc, :, ln] for ln in lns]
            N = [ns_ref[cc, 0:1, ln] for ln in lns]
            Q, Kk = [qs[rows, ln] for ln in lns], [ks[rows, ln] for ln in lns]
            dCn, dnn = [dCst[:, ln] for ln in lns], [dnst[0:1, ln] for ln in lns]
            gns, dos, mos = [gn_ref[:, ln] for ln in lns], [do_ref[rows, ln] for ln in lns], [o_ref[rows, ln] for ln in lns]
            f = _mlstm_heads(Q, Kk, [v_ref[rows, ln] for ln in lns], G, b_col, b_row, g_row, C, N,
                             [jnp.max(ms_ref[cc, 0:1, ln], axis=1, keepdims=True) for ln in lns])
            hh, inv_dd, den, g, Am, Dm = f["h"], f["inv_dd"], f["den"], f["g"], f["Am"], f["Dm"]
            qb, kb, vb, ws, decay = f["qb"], f["kb"], f["vb"], f["ws"], f["decay"]
            ho = _each(_head_out, hh, mos, gns)
            hn, r, sgo = [t[1] for t in ho], [t[2] for t in ho], [t[3] for t in ho]
            dmo = _each(lambda d, n, gn_h, s: _bf(d * (n * gn_h) * s * (1.0 - s)), dos, hn, gns, sgo)
            dhm = _each(lambda d, s: d * s, dos, sgo)
            dgn = _each(lambda d, n: jnp.sum(d * n, axis=0, keepdims=True), dhm, hn)
            dhn = _each(lambda d, gn_h: d * gn_h, dhm, gns)
            dh = _each(lambda rr, d, n: rr * (d - n * jnp.mean(d * n, axis=-1, keepdims=True)), r, dhn, hn)
            dnum = _each(lambda d, i: d * i, dh, inv_dd)
            ddd = _each(lambda d, x, i: -jnp.sum(d * x, axis=1, keepdims=True) * i, dh, hh, inv_dd)
            dden = _each(lambda dn_, fl, d: jnp.where(jnp.abs(dn_) >= fl, d * jnp.sign(dn_), 0.0), den, f["floor"], ddd)
            dnb = [_bf(t) for t in dnum]
            gd = _each(lambda gg, d: _bf(gg * d), g, dnum)
            gq = _each(lambda gg, d: gg * d, g, dden)
            dCb = [_bf(t) for t in dCn]
            dA = _each(lambda d, v, dd_: _dot_nt(d, v) + dd_, dnb, vb, dden)
            dv1 = _each(lambda a, d: _dot_tn(_bf(a), d), Am, dnb)
            dq1 = _each(lambda d, ch: _dot_nt(d, _bf(ch)), gd, C)
            dC1 = _each(_dot_tn, qb, gd)
            E = _each(lambda v, d, n: _dot_nt(v, d) + n, vb, dCb, dnn)
            dv2 = _each(lambda k, d: _dot(_bf(k), d), f["kw"], dCb)
            dS = _each(lambda a, d: _bf(a * d), dA, Dm)
            dq2 = _each(_dot, dS, kb)
            dk1 = _each(_dot_tn, dS, qb)
            dq = _each(lambda a, x, n, b: a + x * n + b, dq1, gq, N, dq2)
            dC = _each(lambda d, x, y: d * x + y, decay, dCn, dC1)
            dn = _each(lambda d, x, y, q: d * x + jnp.sum(y * q, axis=0, keepdims=True), decay, dnn, gq, Q)
            dg = _each(lambda d, qc, dd_, x: jnp.sum(d * qc, axis=1, keepdims=True) + dd_ * x, dnum, f["qC"], dden, f["qn"])
            Gm = _each(lambda a, b: a * b, dA, Am)
            gam = _each(lambda a, b: a * b, dg, g)
            dk = _each(lambda a, w, e: (a + w * e) * kscale, dk1, ws, E)
            om = _each(lambda e, k, w: jnp.sum(e * k, axis=1, keepdims=True) * w, E, Kk, ws)
            dv = _each(lambda a, b: _bf(a + b), dv1, dv2)
            ddecay = _each(lambda d, ch, dn_, n: jnp.sum(jnp.sum(d * ch, axis=1, keepdims=True), axis=0, keepdims=True)
                           + jnp.sum(dn_ * n, axis=1, keepdims=True), dCn, C, dnn, N)
            rows_g = [jnp.sum(t, axis=1, keepdims=True) for t in Gm]
            cols_g = [jnp.broadcast_to(jnp.sum(t, axis=0, keepdims=True), (LC, 128)).T for t in Gm]
            last = _each(lambda o, dd_, d: jnp.where(rowi == LC - 1, jnp.sum(o, axis=0, keepdims=True) + dd_ * d, 0.0),
                         om, ddecay, decay)
            db = _each(lambda a, b, o, l, cg: a + b - o + l - cg, rows_g, gam, om, last, cols_g)
            di = _each(lambda cg, o: cg + o, cols_g, om)
            dB = jnp.zeros((LC, 128), F32)
            dI = jnp.zeros((LC, 128), F32)
            for h, ln in enumerate(lns):
                dB = jnp.where(lane128 == 4 + h, db[h], dB)
                dI = jnp.where(lane128 == h, di[h], dI)
                dgn_ref[:, ln] = dgn_ref[:, ln] + dgn[h]
                dCst[:, ln] = dC[h]
                dnst[0:1, ln] = dn[h]
                dqk[rows, ln] = dq[h]
                dqk[rows, MW + h * 128:MW + (h + 1) * 128] = dk[h]
                dm_ref[rows, DM_O + h * 128:DM_O + (h + 1) * 128] = dmo[h]
                dm_ref[rows, DM_V + h * 128:DM_V + (h + 1) * 128] = dv[h]
            dlogf = jnp.dot(triu, dB, preferred_element_type=F32, precision=HI)
            dG = dI + dlogf * _sigmoid(-G)
            dG = jnp.where(lane128 < 8, dG, 0.0)
            dm_ref[rows, DM_G:DM_G + 128] = _bf(dG)
            dm_ref[rows, DM_G + 128:DM_W] = jnp.zeros((LC, DM_W - DM_G - 128), BF16)
            dgb_ref[...] = dgb_ref[...] + jnp.sum(dG, axis=0, keepdims=True)

        dy = dqk[...] * (sg * (1.0 + c * (1.0 - sg)))
        db_ref[...] = db_ref[...] + jnp.sum(dy, axis=0, keepdims=True)
        for j in range(4):
            dw_ref[j:j + 1, :] = dw_ref[j:j + 1, :] + jnp.sum(dy * taps[j], axis=0, keepdims=True)
        dyp = jnp.concatenate([dy, dyhead[...]], axis=0)
        dx = w_ref[3:4, :] * dy
        for j in range(3):
            dx = dx + w_ref[j:j + 1, :] * pltpu.roll(dyp, TB + 8 - (3 - j), 0)[0:TB]
        dm_ref[:, 0:DM_V] = _bf(dx)
        dyhead[...] = dy[0:8, :]
        pl.when(i == nblk - 1)(rs_finish)

    rrow = lambda wd: pl.BlockSpec((TB, wd), lambda i: (nblk - 1 - i, 0))
    st = lambda r: pl.BlockSpec((ncb, r, MW), lambda i: (nblk - 1 - i, 0, 0))
    prev8 = pl.BlockSpec((8, 1024), lambda i: (jnp.maximum((nblk - 1 - i) * (TB // 8) - 1, 0), 0))
    res = pl.pallas_call(
        body, name="mlstm_bwd", grid=(nblk,),
        in_specs=[rrow(1024), prev8, rrow(MW), rrow(MW), rrow(128), _cspec((4, 1024)), _cspec((1, 1024)),
                  _cspec((1, 128)), _cspec((1, MW)), st(128), st(8), st(8), rrow(MW)] + [ANY] * nw,
        out_specs=[rrow(DM_W),
                   pl.BlockSpec((4, 1024), lambda i: (0, 0)), pl.BlockSpec((1, 1024), lambda i: (0, 0)),
                   pl.BlockSpec((1, MW), lambda i: (0, 0)), pl.BlockSpec((1, 128), lambda i: (0, 0))] + [ANY] * nw,
        out_shape=[jax.ShapeDtypeStruct((S, DM_W), BF16),
                   jax.ShapeDtypeStruct((4, 1024), F32), jax.ShapeDtypeStruct((1, 1024), F32),
                   jax.ShapeDtypeStruct((1, MW), F32), jax.ShapeDtypeStruct((1, 128), F32)]
        + [jax.ShapeDtypeStruct((len(CHIP_FLIPS), *a.shape[1:]), a.dtype) for a in parts],
        scratch_shapes=[pltpu.VMEM((128, MW), F32), pltpu.VMEM((8, MW), F32), pltpu.VMEM((8, 1024), F32),
                        pltpu.VMEM((TB, MW), F32), pltpu.VMEM((TB, MW), F32), pltpu.VMEM((TB, 1024), F32)]
        + _scatter2_scratch(parts[0].shape[1:], parts[0].dtype),
        compiler_params=_params(1),
    )(mqk, mqk, mv, mo, gates, conv_w, conv_b, gate_b, gn, cs, ns, ms, dout, *parts)
    return res[:5], res[5:]


def _out_proj(x, attn, ml, w, g):
    tm = TM

    def body(x_ref, a_ref, m_ref, w_ref, g_ref, h_ref, u_ref):
        h1 = x_ref[...] + _dot(_bf(a_ref[...]), w_ref[0:AW, :]) + _dot(_bf(m_ref[...]), w_ref[AW:D, :])
        h_ref[...] = h1
        n, _ = _rms(h1)
        u_ref[...] = _bf(n * g_ref[...])

    row = lambda wd: pl.BlockSpec((tm, wd), lambda i: (i, 0))
    return pl.pallas_call(
        body, name="out_proj", grid=(S // tm,),
        in_specs=[row(D), row(AW), row(MW), _cspec((D, D)), _cspec((1, D))],
        out_specs=[row(D), row(D)],
        out_shape=[jax.ShapeDtypeStruct((S, D), F32), jax.ShapeDtypeStruct((S, D), BF16)],
        compiler_params=_params(1),
    )(x, attn, ml, w, g)


HALF = DFF // NDEV // 2


def _mlp_fwd(h1, u2, w_up, w_down_a, w_down_b, shards, dtypes):
    tm = TM
    nt = S // tm
    nw = len(shards)

    def body(*refs):
        h_ref, u_ref, wu_ref, wa_ref, wb_ref = refs[:5]
        ins = refs[5:5 + nw]
        a_ref, o_ref = refs[5 + nw:7 + nw]
        outs = refs[7 + nw:7 + 2 * nw]
        bufs = refs[7 + 2 * nw:7 + 3 * nw]
        ag_start, ag_forward, ag_finish = _gather_phases(ins, outs, bufs, *refs[7 + 3 * nw:])
        i = pl.program_id(0)
        pl.when(i == 0)(ag_start)
        pl.when(i == nt - 2)(ag_forward)
        u = u_ref[...]
        acc = h_ref[...]
        for c in range(NDEV):
            cols = slice(c * 512, (c + 1) * 512)
            a = _dot(u, wu_ref[c])
            a_ref[:, cols] = _bf(a)
            r = jnp.maximum(a, 0.0)
            r = _bf(r * r)
            acc = acc + _dot(r[:, 0:HALF], wa_ref[c]) + _dot(r[:, HALF:2 * HALF], wb_ref[c])
        o_ref[...] = acc
        pl.when(i == nt - 1)(ag_finish)

    row = lambda wd: pl.BlockSpec((tm, wd), lambda i: (i, 0))
    res = pl.pallas_call(
        body, name="mlp_fwd", grid=(nt,),
        in_specs=[row(D), row(D), _cspec((NDEV, D, DFF // NDEV)), _cspec((NDEV, HALF, D)), _cspec((NDEV, HALF, D))]
        + [VM] * nw,
        out_specs=[row(DFF), row(D)] + [ANY] * nw,
        out_shape=[jax.ShapeDtypeStruct((S, DFF), BF16), jax.ShapeDtypeStruct((S, D), F32)]
        + _gather_shapes(shards, dtypes),
        scratch_shapes=_gather_scratch(shards, dtypes),
        compiler_params=_params(1),
    )(h1, u2, w_up, w_down_a, w_down_b, *shards)
    return res[0], res[1], res[2:]


def _ple_loss(h2, p, target, w_pg, w_ple, g_ple, g_fin):
    tm = TM

    def body(h_ref, p_ref, t_ref, wg_ref, wp_ref, gp_ref, gf_ref,
             dh_ref, dwg_ref, dwp_ref, dgp_ref, dgf_ref, loss_ref, acc_g, acc_p):
        i = pl.program_id(0)

        @pl.when(i == 0)
        def _():
            acc_g[...] = jnp.zeros_like(acc_g)
            acc_p[...] = jnp.zeros_like(acc_p)
            dgp_ref[...] = jnp.zeros_like(dgp_ref)
            dgf_ref[...] = jnp.zeros_like(dgf_ref)
            loss_ref[...] = jnp.zeros_like(loss_ref)

        h2v = h_ref[...]
        n2, rs2 = _rms(h2v)
        u3 = _bf(n2 * gp_ref[...])
        gt = _sigmoid(_dot(u3, wg_ref[...]))
        pb = _bf(p_ref[...])
        e = jnp.concatenate([_dot(pb, wp_ref[j]) for j in range(NDEV)], axis=1)
        h3 = h2v + gt * e
        n3, rs3 = _rms(h3)
        err = n3 * gf_ref[...] - t_ref[...]
        loss_ref[...] = loss_ref[...] + 0.5 / D * jnp.sum(jnp.sum(err * err, axis=1, keepdims=True), axis=0, keepdims=True)
        dy = err * (1.0 / D)
        dgf_ref[...] = dgf_ref[...] + jnp.sum(dy * n3, axis=0, keepdims=True)
        dh3 = _rms_bwd(dy, n3, rs3, gf_ref[...])
        de = _bf(dh3 * gt)
        dz = _bf(dh3 * e * gt * (1.0 - gt))
        acc_p[...] = acc_p[...] + _dot_tn(pb, de)
        acc_g[...] = acc_g[...] + _dot_tn(u3, dz)
        du3 = _dot_nt(dz, wg_ref[...])
        dgp_ref[...] = dgp_ref[...] + jnp.sum(du3 * n2, axis=0, keepdims=True)
        dh_ref[...] = dh3 + _rms_bwd(du3, n2, rs2, gp_ref[...])

        @pl.when(i == S // tm - 1)
        def _():
            dwg_ref[...] = _bf(acc_g[...])
            for j in range(NDEV):
                dwp_ref[j] = _bf(acc_p[:, j * 128:(j + 1) * 128])

    row = lambda wd: pl.BlockSpec((tm, wd), lambda i: (i, 0))
    whole = lambda shp: pl.BlockSpec(shp, lambda i: (0,) * len(shp))
    return pl.pallas_call(
        body, name="ple_loss", grid=(S // tm,),
        in_specs=[row(D), row(PLE), row(D), _cspec((D, D)), _cspec((NDEV, PLE, 128)), _cspec((1, D)), _cspec((1, D))],
        out_specs=[row(D), whole((D, D)), whole((NDEV, PLE, 128)), whole((1, D)), whole((1, D)), whole((1, 1))],
        out_shape=[jax.ShapeDtypeStruct((S, D), F32), jax.ShapeDtypeStruct((D, D), BF16),
                   jax.ShapeDtypeStruct((NDEV, PLE, 128), BF16), jax.ShapeDtypeStruct((1, D), F32),
                   jax.ShapeDtypeStruct((1, D), F32), jax.ShapeDtypeStruct((1, 1), F32)],
        scratch_shapes=[pltpu.VMEM((D, D), F32), pltpu.VMEM((PLE, D), F32)],
        compiler_params=_params(1),
    )(h2, p, target, w_pg, w_ple, g_ple, g_fin)


def _mlp_bwd(dh2, a, h1, g, w_up, w_down_a, w_down_b, parts):
    tm = TM
    nt = S // tm
    nw = len(parts)

    def body(*refs):
        d_ref, a_ref, h_ref, g_ref, wu_ref, wa_ref, wb_ref = refs[:7]
        ins = refs[7:7 + nw]
        da_ref, dh1_ref, dg_ref = refs[7 + nw:10 + nw]
        outs = refs[10 + nw:10 + 2 * nw]
        rs_start, rs_finish = _scatter_phases(ins, outs, *refs[10 + 2 * nw:])
        i = pl.program_id(0)
        pl.when(i == 0)(rs_start)

        @pl.when(i == 0)
        def _():
            dg_ref[...] = jnp.zeros_like(dg_ref)

        dh2v = d_ref[...]
        db = _bf(dh2v)
        du = jnp.zeros((tm, D), F32)
        for c in range(NDEV):
            cols = slice(c * 512, (c + 1) * 512)
            dr = jnp.concatenate([_dot_nt(db, wa_ref[c]), _dot_nt(db, wb_ref[c])], axis=1)
            da = _bf(dr * (2.0 * jnp.maximum(a_ref[:, cols], 0.0)))
            da_ref[:, cols] = da
            du = du + _dot_nt(da, wu_ref[c])
        n, rs = _rms(h_ref[...])
        dg_ref[...] = dg_ref[...] + jnp.sum(du * n, axis=0, keepdims=True)
        dh1_ref[...] = dh2v + _rms_bwd(du, n, rs, g_ref[...])
        pl.when(i == nt - 1)(rs_finish)

    row = lambda wd: pl.BlockSpec((tm, wd), lambda i: (i, 0))
    res = pl.pallas_call(
        body, name="mlp_bwd", grid=(nt,),
        in_specs=[row(D), row(DFF), row(D), _cspec((1, D)), _cspec((NDEV, D, DFF // NDEV)), _cspec((NDEV, HALF, D)),
                  _cspec((NDEV, HALF, D))] + [ANY] * nw,
        out_specs=[row(DFF), row(D), pl.BlockSpec((1, D), lambda i: (0, 0))] + [ANY] * nw,
        out_shape=[jax.ShapeDtypeStruct((S, DFF), BF16), jax.ShapeDtypeStruct((S, D), F32),
                   jax.ShapeDtypeStruct((1, D), F32)] + [jax.ShapeDtypeStruct(p.shape, p.dtype) for p in parts],
        scratch_shapes=_scatter_scratch(nw),
        compiler_params=_params(1),
    )(dh2, a, h1, g, w_up, w_down_a, w_down_b, *parts)
    return res[0], res[1], res[2], res[3:]


def _out_proj_bwd(dh1, attn, ml, w):
    tm = TM

    def body(d_ref, a_ref, m_ref, w_ref, da_ref, dm_ref, dw_ref, acc):
        i = pl.program_id(0)

        @pl.when(i == 0)
        def _():
            acc[...] = jnp.zeros_like(acc)

        db = _bf(d_ref[...])
        dmix = _dot_nt(db, w_ref[...])
        da_ref[...] = dmix[:, 0:AW]
        dm_ref[...] = dmix[:, AW:D]
        acc[0:AW, :] = acc[0:AW, :] + _dot_tn(_bf(a_ref[...]), db)
        acc[AW:D, :] = acc[AW:D, :] + _dot_tn(_bf(m_ref[...]), db)

        @pl.when(i == S // tm - 1)
        def _():
            dw_ref[...] = _bf(acc[...])

    row = lambda wd: pl.BlockSpec((tm, wd), lambda i: (i, 0))
    return pl.pallas_call(
        body, name="out_proj_bwd", grid=(S // tm,),
        in_specs=[row(D), row(AW), row(MW), _cspec((D, D))],
        out_specs=[row(AW), row(MW), pl.BlockSpec((D, D), lambda i: (0, 0))],
        out_shape=[jax.ShapeDtypeStruct((S, AW), F32), jax.ShapeDtypeStruct((S, MW), F32),
                   jax.ShapeDtypeStruct((D, D), BF16)],
        scratch_shapes=[pltpu.VMEM((D, D), F32)],
        compiler_params=_params(1),
    )(dh1, attn, ml, w)


CHIP_FLIPS = [(0, 0), (0, 1), (1, 0), (1, 1)]


def _scatter2_phases(in_ref, out_ref, mine_v, sib_v, psum_v, loc_sems, d2d_send, d2d_recv, ici_send, ici_recv, own_sem):
    x, y, c = _place()
    chips = [((x + dx) % 2, (y + dy) % 2) for dx, dy in CHIP_FLIPS]
    nc = len(chips)

    def local(k):
        return pltpu.make_async_copy(in_ref.at[_dev_index(*chips[k], c)], mine_v.at[k], loc_sems.at[k])

    def to_sib(k):
        return pltpu.make_async_remote_copy(
            src_ref=in_ref.at[_dev_index(*chips[k], 1 - c)], dst_ref=sib_v.at[k], send_sem=d2d_send.at[k],
            recv_sem=d2d_recv.at[k], device_id=(x, y, 1 - c), device_id_type=MESH)

    def over_ici(k):
        return pltpu.make_async_remote_copy(
            src_ref=psum_v.at[k], dst_ref=out_ref.at[k], send_sem=ici_send.at[k - 1], recv_sem=ici_recv.at[k - 1],
            device_id=(*chips[k], c), device_id_type=MESH)

    def own():
        return pltpu.make_async_copy(psum_v.at[0], out_ref.at[0], own_sem)

    def start():
        for k in range(nc):
            to_sib(k).start()
            local(k).start()

    def middle():
        for k in (1, 2, 3, 0):
            local(k).wait()
            to_sib(k).wait_recv()
            psum_v[k] = _bf(mine_v[k].astype(F32) + sib_v[k].astype(F32))
            (over_ici(k) if k else own()).start()

    def finish():
        for k in range(1, nc):
            over_ici(k).wait()
        for k in range(nc):
            to_sib(k).wait_send()
        own().wait()

    return start, middle, finish


def _scatter2_scratch(shard, dtype):
    nc = len(CHIP_FLIPS)
    return ([pltpu.VMEM((nc, *shard), dtype)] * 3
            + [pltpu.SemaphoreType.DMA((nc,))] * 3 + [pltpu.SemaphoreType.DMA((nc - 1,))] * 2 + [pltpu.SemaphoreType.DMA])


def _in_proj_bwd(dparts, n_roped, rope, dh1, x, g1, w, part):
    tm = TM
    nt = S // tm
    widths = [d.shape[1] for d in dparts]
    assert sum(widths) == PW
    npar = len(dparts)

    def body(*refs):
        d_refs = refs[:npar]
        tabs = [t[...] for t in refs[npar:npar + 3]]
        dh_ref, x_ref, g_ref, w_ref, in_ref, dx_ref, dgsum_ref, out_ref = refs[npar + 3:npar + 11]
        rs_start, rs_middle, rs_finish = _scatter2_phases(in_ref, out_ref, *refs[npar + 11:npar + 20])
        dg_ref = refs[npar + 20]
        ar_start, ar_finish = _small_phases([dg_ref], dgsum_ref, *refs[npar + 21:])
        i = pl.program_id(0)
        pl.when(i == 0)(rs_start)
        pl.when(i == 1)(rs_middle)

        @pl.when(i == 0)
        def _():
            dg_ref[...] = jnp.zeros_like(dg_ref)

        du = jnp.zeros((tm, D), F32)
        off = 0
        for j, (d_ref, wd) in enumerate(zip(d_refs, widths)):
            nc = next(c for c in (768, 512) if wd % c == 0)
            for s in range(wd // nc):
                d = d_ref[:, s * nc:(s + 1) * nc]
                du = du + _dot_nt(_unrope(d, *tabs) if j < n_roped else d, w_ref[:, off + s * nc:off + (s + 1) * nc])
            off += wd
        n, rs = _rms(x_ref[...])
        dg_ref[...] = dg_ref[...] + jnp.sum(du * n, axis=0, keepdims=True)
        dx_ref[...] = dh_ref[...] + _rms_bwd(du, n, rs, g_ref[...])

        @pl.when(i == nt - 1)
        def _():
            ar_start()
            rs_finish()
            ar_finish()

    row = lambda wd: pl.BlockSpec((tm, wd), lambda i: (i, 0))
    shard = part.shape[1:]
    return pl.pallas_call(
        body, name="in_proj_bwd", grid=(nt,),
        in_specs=[row(wd) for wd in widths] + [row(128)] * 3 + [row(D), row(D), _cspec((1, D)), _cspec((D, PW)), ANY],
        out_specs=[row(D), VM, ANY],
        out_shape=[jax.ShapeDtypeStruct((S, D), F32), jax.ShapeDtypeStruct((8, 1024), F32),
                   jax.ShapeDtypeStruct((len(CHIP_FLIPS), *shard), part.dtype)],
        scratch_shapes=_scatter2_scratch(shard, part.dtype)
        + [pltpu.VMEM((1, D), F32), pltpu.VMEM((8, 1024), F32), pltpu.VMEM((NDEV, 8, 1024), F32),
           pltpu.SemaphoreType.DMA((7,)), pltpu.SemaphoreType.DMA((7,))],
        compiler_params=_params(1),
    )(*dparts, *rope, dh1, x, g1, w, part)


SMALL_ROWS = 96


def _small_phases(ins, out_ref, pack, rbuf, send_sems, recv_sems):
    x, y, c = _place()
    me = _dev_index(x, y, c)

    def copies():
        out = []
        for k, (dx, dy, dc) in enumerate(FLIPS):
            peer = ((x + dx) % 2, (y + dy) % 2, (c + dc) % 2)
            out.append(pltpu.make_async_remote_copy(
                src_ref=pack, dst_ref=rbuf.at[me], send_sem=send_sems.at[k], recv_sem=recv_sems.at[k],
                device_id=peer, device_id_type=MESH))
        return out

    def start():
        pack[...] = jnp.zeros_like(pack)
        for i, ref in enumerate(ins):
            pack[8 * i:8 * i + 1, 0:ref.shape[1]] = ref[...]
        rbuf[me] = pack[...]
        for cp in copies():
            cp.start()

    def finish():
        for cp in copies():
            cp.wait()
        tot = rbuf[0]
        for j in range(1, NDEV):
            tot = tot + rbuf[j]
        out_ref[...] = tot

    return start, finish


def _wgrad(name, A, Bs, a_fn, b_fn, out_shape, split=None, ts=512, small=(), rope=(), n_roped=0):
    K = A.shape[1]
    widths = [b.shape[1] for b in Bs]
    N = sum(widths)
    nb, ns, nrt = len(Bs) + len(rope), len(small), S // ts
    kc = min(K, 1024)

    def body(*refs):
        a_ref, b_refs = refs[0], refs[1:1 + len(Bs)]
        tabs = [t[...] for t in refs[1 + len(Bs):1 + nb]]
        o_ref = refs[1 + nb + ns]
        acc = refs[2 + nb + ns + bool(ns)]
        r = pl.program_id(0)
        if ns:
            sm_start, sm_finish = _small_phases(refs[1 + nb:1 + nb + ns], refs[2 + nb + ns], *refs[4 + nb + ns:])
            pl.when(r == 0)(sm_start)

        @pl.when(r == 0)
        def _():
            acc[...] = jnp.zeros_like(acc)

        bs, off = [], 0
        for i, (b_ref, w) in enumerate(zip(b_refs, widths)):
            nc = next(c for c in (1024, 768, 512) if w % c == 0)
            fn = (lambda t: _unrope(t, *tabs)) if i < n_roped else b_fn
            bs += [(off + c * nc, nc, fn(b_ref[:, c * nc:(c + 1) * nc])) for c in range(w // nc)]
            off += w
        for kk in range(K // kc):
            rows = slice(kk * kc, (kk + 1) * kc)
            at = a_fn(a_ref[:, rows]).T
            for lo, nc, b in bs:
                acc[rows, lo:lo + nc] = acc[rows, lo:lo + nc] + _dot(at, b)

        @pl.when(r == nrt - 1)
        def _():
            if split is None:
                o_ref[...] = _bf(acc[...])
            else:
                for j in range(NDEV):
                    o_ref[j] = _bf(acc[:, split * j:split * (j + 1)])

        if ns:
            pl.when(r == nrt - 1)(sm_finish)

    in_specs = ([pl.BlockSpec((ts, K), lambda r: (r, 0))] + [pl.BlockSpec((ts, w), lambda r: (r, 0)) for w in widths]
                + [pl.BlockSpec((ts, 128), lambda r: (r, 0))] * len(rope))
    out_spec = pl.BlockSpec(out_shape, lambda r: (0,) * len(out_shape))
    scratch = [pltpu.VMEM((K, N), F32)]
    if not ns:
        return pl.pallas_call(
            body, name=name, grid=(nrt,), in_specs=in_specs, out_specs=out_spec,
            out_shape=jax.ShapeDtypeStruct(out_shape, BF16), scratch_shapes=scratch, compiler_params=_params(1),
        )(A, *Bs, *rope)
    return pl.pallas_call(
        body, name=name, grid=(nrt,), in_specs=in_specs + [VM] * ns, out_specs=[out_spec, VM],
        out_shape=[jax.ShapeDtypeStruct(out_shape, BF16), jax.ShapeDtypeStruct((SMALL_ROWS, 1024), F32)],
        scratch_shapes=scratch + [pltpu.VMEM((SMALL_ROWS, 1024), F32), pltpu.VMEM((NDEV, SMALL_ROWS, 1024), F32),
                                  pltpu.SemaphoreType.DMA((7,)), pltpu.SemaphoreType.DMA((7,))],
        compiler_params=_params(1),
    )(A, *Bs, *rope, *small)


def _relu2_bf(a):
    r = jnp.maximum(a.astype(F32), 0.0)
    return _bf(r * r)


def _ident(a):
    return a


def _step(x, p, target, g1, conv_b, gate_b, gn, g_mlp, g_ple, g_fin, sh):
    (g_in, g_conv), (rc, ra, rb) = _gather_weights([sh["w_in"], sh["conv_w"]], [BF16, F32])
    conv_w = g_conv.transpose(1, 0, 2).reshape(4, 1024)
    w_in_p = _join_w_in(g_in)
    (qkv, mqk, mv, mo, gates, u1), (w_down_a,) = _in_proj(x, g1, w_in_p, rc, ra, rb, [sh["w_down"][0:HALF]], [BF16])
    attn, lse, (w_up8, w_out8) = _attn_fwd(qkv, [sh["w_up"], sh["w_out"]], [BF16] * 2)
    ml, cs, ns, ms, (w_down_b,) = _mlstm_fwd(mqk, mv, mo, gates, conv_w, conv_b, gate_b, gn,
                                             [sh["w_down"][HALF:2 * HALF]], [BF16])
    w_out = w_out8.reshape(D, D)
    h1, u2 = _out_proj(x, attn, ml, w_out, g_mlp)
    a, h2, (w_pg8, w_ple8) = _mlp_fwd(h1, u2, w_up8, w_down_a, w_down_b, [sh["w_ple_gate"], sh["w_ple"]], [BF16] * 2)
    w_pg = w_pg8.reshape(D, D)
    dh2, dw_pg, dw_ple8, dg_ple, dg_fin, loss = _ple_loss(h2, p, target, w_pg, w_ple8, g_ple, g_fin)
    da, dh1, dg_mlp, (r_pg, r_ple) = _mlp_bwd(dh2, a, h1, g_mlp, w_up8, w_down_a, w_down_b,
                                              [dw_pg.reshape(NDEV, D // NDEV, D), dw_ple8])
    dw_up8 = _wgrad("wgrad_up", u2, [da], _ident, _ident, (NDEV, D, DFF // NDEV), split=DFF // NDEV)
    dw_down = _wgrad("wgrad_down", a, [dh2], _relu2_bf, _bf, (DFF, D))
    d_attn, d_ml, dw_out = _out_proj_bwd(dh1, attn, ml, w_out)
    (dm, dconv_w, dconv_b, dgn, dgate_b), (r_down,) = _mlstm_bwd(
        mqk, mv, mo, gates, conv_w, conv_b, gate_b, gn, cs, ns, ms, d_ml, [dw_down.reshape(NDEV, DFF // NDEV, D)])
    dq, dk, dv, (r_up, r_out) = _attn_bwd(qkv, attn, lse, d_attn, [dw_up8, dw_out.reshape(NDEV, D // NDEV, D)])
    dparts = [dq, dk, dv, dm]
    small = [jnp.zeros((1, D), F32), dconv_b, dgate_b, dgn, dg_mlp, dg_ple, dg_fin, loss]
    dw_in8, total = _wgrad("wgrad_in", u1, dparts, _ident, _ident, (NDEV, D, IN_W // NDEV), split=IN_W // NDEV,
                           small=small + [dconv_w[j:j + 1] for j in range(4)], rope=(rc, ra, rb), n_roped=2)
    dx, dg1_sum, r_in = _in_proj_bwd(dparts, 2, (rc, ra, rb), dh1, x, g1, w_in_p, dw_in8)
    recv = dict(w_in=r_in, w_out=r_out, w_up=r_up, w_down=r_down, w_ple_gate=r_pg, w_ple=r_ple)
    return dx, recv, total, dg1_sum


def _gather_weights(shards, dtypes):
    nw = len(shards)

    def body(*refs):
        ins, parts = refs[:nw], refs[nw:nw + 4]
        outs, tables = refs[nw + 4:2 * nw + 4], refs[2 * nw + 4:2 * nw + 7]
        start, forward, finish = _gather_phases(ins, outs, refs[2 * nw + 7:3 * nw + 7], *refs[3 * nw + 7:])
        start()
        _rope_fill(*parts, *tables)
        forward()
        finish()

    res = pl.pallas_call(
        body, name="gather_weights",
        in_specs=[VM] * (nw + 4), out_specs=[ANY] * nw + [VM] * 3,
        out_shape=_gather_shapes(shards, dtypes) + [jax.ShapeDtypeStruct((S, 128), F32)] * 3,
        scratch_shapes=_gather_scratch(shards, dtypes),
        compiler_params=_params(),
    )(*shards, *_rope_parts())
    return res[:nw], res[nw:]


ADAM_STEPS = 8


def _adamw(items):
    n = len(items)

    def body(*refs):
        for i in range(n):
            g_ref, w_ref, m_ref, v_ref = refs[4 * i:4 * i + 4]
            go_ref, d_ref, mo_ref, vo_ref = refs[4 * n + 4 * i:4 * n + 4 * i + 4]
            g = g_ref[0].astype(F32)
            for j in range(1, g_ref.shape[0]):
                g = g + g_ref[j].astype(F32)
            go_ref[...] = g
            d_ref[...], mo_ref[...], vo_ref[...] = _adam_update(g, w_ref[...], m_ref[...], v_ref[...])

    in_specs, out_specs, out_shape, args = [], [], [], []
    for gparts, w, m, v in items:
        P, R, C = gparts.shape
        if R % (8 * ADAM_STEPS) == 0:
            tr = R // ADAM_STEPS
            row, gspec = pl.BlockSpec((tr, C), lambda i: (i, 0)), pl.BlockSpec((P, tr, C), lambda i: (0, i, 0))
        else:
            row, gspec = pl.BlockSpec((R, C), lambda i: (0, 0)), pl.BlockSpec((P, R, C), lambda i: (0, 0, 0))
        in_specs += [gspec, row, row, row]
        out_specs += [row] * 4
        out_shape += [jax.ShapeDtypeStruct((R, C), F32)] * 4
        args += [gparts, w, m, v]
    res = pl.pallas_call(
        body, name="adamw", grid=(ADAM_STEPS,), in_specs=in_specs, out_specs=out_specs, out_shape=out_shape,
        compiler_params=_params(1),
    )(*args)
    return [res[4 * i:4 * i + 4] for i in range(n)]


SMALL = ("norm_mix_g", "conv_b", "gate_b", "mlstm_norm_g", "norm_mlp_g", "norm_ple_g", "final_norm_g")


def _adam_update(g, w, m, v):
    c1 = 1.0 - ADAM_B1 ** ADAM_STEP
    c2 = 1.0 - ADAM_B2 ** ADAM_STEP
    m2 = ADAM_B1 * m + (1.0 - ADAM_B1) * g
    v2 = ADAM_B2 * v + (1.0 - ADAM_B2) * (g * g)
    return -ADAM_LR * ((m2 / c1) / (jnp.sqrt(v2 / c2) + ADAM_EPS) + ADAM_WD * w), m2, v2


def _adamw_small(total, first, ws, ms, vs):
    n = len(ws)

    def body(*refs):
        t_ref, f_ref = refs[:2]
        refs = refs[1:]
        outs = refs[1 + 3 * n:]
        for i in range(n):
            w_ref, m_ref, v_ref = refs[1 + i], refs[1 + n + i], refs[1 + 2 * n + i]
            g = (t_ref if i else f_ref)[8 * i:8 * i + 1, 0:w_ref.shape[1]]
            delta, m2, v2 = _adam_update(g, w_ref[...], m_ref[...], v_ref[...])
            for ref, val in zip(outs[4 * i:4 * i + 4], (g, delta, m2, v2)):
                ref[...] = val

    res = pl.pallas_call(
        body, name="adamw_small",
        out_shape=[jax.ShapeDtypeStruct(w.shape, F32) for w in ws for _ in range(4)],
        compiler_params=_params(),
    )(total, first, *ws, *ms, *vs)
    return [res[4 * i:4 * i + 4] for i in range(n)]


def kernel(x, p, norm_mix_g, w_in, conv_w, conv_b, gate_b, mlstm_norm_g, w_out, norm_mlp_g, w_up, w_down, norm_ple_g, w_ple_gate, w_ple, final_norm_g, loss_target, m_norm_mix_g, m_w_in, m_conv_w, m_conv_b, m_gate_b, m_mlstm_norm_g, m_w_out, m_norm_mlp_g, m_w_up, m_w_down, m_norm_ple_g, m_w_ple_gate, m_w_ple, m_final_norm_g, v_norm_mix_g, v_w_in, v_conv_w, v_conv_b, v_gate_b, v_mlstm_norm_g, v_w_out, v_norm_mlp_g, v_w_up, v_w_down, v_norm_ple_g, v_w_ple_gate, v_w_ple, v_final_norm_g):
    big_names = ("w_in", "conv_w", "w_out", "w_up", "w_down", "w_ple_gate", "w_ple")
    wts = dict(w_in=w_in, conv_w=conv_w, w_out=w_out, w_up=w_up, w_down=w_down, w_ple_gate=w_ple_gate, w_ple=w_ple)
    mom = dict(w_in=m_w_in, conv_w=m_conv_w, w_out=m_w_out, w_up=m_w_up, w_down=m_w_down, w_ple_gate=m_w_ple_gate,
               w_ple=m_w_ple)
    var = dict(w_in=v_w_in, conv_w=v_conv_w, w_out=v_w_out, w_up=v_w_up, w_down=v_w_down, w_ple_gate=v_w_ple_gate,
               w_ple=v_w_ple)
    sq = lambda a: a.reshape(a.shape[1:])
    fin = final_norm_g.reshape(1, D)
    dx, recv, total, first = _step(
        x[0], p[0, 0], loss_target[0], norm_mix_g, conv_b, jnp.pad(gate_b, ((0, 0), (0, 120))), mlstm_norm_g,
        norm_mlp_g, norm_ple_g, fin, {n: sq(wts[n]) for n in big_names})

    nrow = 8 * len(SMALL)
    me = _dev_index(*_place())
    conv_rows = total[nrow + 8:nrow + 40:8]
    recv["conv_w"] = lax.dynamic_slice_in_dim(conv_rows, me * 128, 128, axis=1).reshape(1, 4, 128)
    out = {}
    for n, res in zip(big_names, _adamw([(recv[n], sq(wts[n]), sq(mom[n]), sq(var[n])) for n in big_names])):
        out[n] = [t.reshape(wts[n].shape) for t in res]
    sw = dict(norm_mix_g=norm_mix_g, conv_b=conv_b, gate_b=gate_b, mlstm_norm_g=mlstm_norm_g, norm_mlp_g=norm_mlp_g,
              norm_ple_g=norm_ple_g, final_norm_g=fin)
    sm = dict(norm_mix_g=m_norm_mix_g, conv_b=m_conv_b, gate_b=m_gate_b, mlstm_norm_g=m_mlstm_norm_g,
              norm_mlp_g=m_norm_mlp_g, norm_ple_g=m_norm_ple_g, final_norm_g=m_final_norm_g.reshape(1, D))
    sv = dict(norm_mix_g=v_norm_mix_g, conv_b=v_conv_b, gate_b=v_gate_b, mlstm_norm_g=v_mlstm_norm_g,
              norm_mlp_g=v_norm_mlp_g, norm_ple_g=v_norm_ple_g, final_norm_g=v_final_norm_g.reshape(1, D))
    res = _adamw_small(total, first, [sw[n] for n in SMALL], [sm[n] for n in SMALL], [sv[n] for n in SMALL])
    for n, r in zip(SMALL, res):
        out[n] = [t.reshape(final_norm_g.shape) for t in r] if n == "final_norm_g" else list(r)
    order = ("norm_mix_g", "w_in", "conv_w", "conv_b", "gate_b", "mlstm_norm_g", "w_out", "norm_mlp_g", "w_up", "w_down",
             "norm_ple_g", "w_ple_gate", "w_ple", "final_norm_g")
    loss_all = total[nrow, 0]
    return (loss_all, dx[None], *[out[n][0] for n in order], *[out[n][1] for n in order],
            *[out[n][2] for n in order], *[out[n][3] for n in order])
```

```python
import math

import jax
import jax.numpy as jnp
from jax import lax
from jax.experimental import pallas as pl
from jax.experimental.pallas import tpu as pltpu

F32, BF16 = jnp.float32, jnp.bfloat16
S = 4096
D = 1024
AW = 512
MW = 512
DFF = 4096
PLE = 256
IN_W = 3592
PW = 3840
NDEV = 8
EPS = 1e-6
NEG = -1e30
LC = 128
TB = 256
ROPE_THETA = 500000.0
VMEM_LIMIT = 56 * 1024 * 1024
HI = lax.Precision.HIGHEST

ADAM_LR, ADAM_B1, ADAM_B2, ADAM_EPS, ADAM_WD, ADAM_STEP = 0.001, 0.9, 0.999, 1e-08, 0.01, 10


def _params(n_grid=0, **kw):
    sem = dict(dimension_semantics=("arbitrary",) * n_grid) if n_grid else {}
    return pltpu.CompilerParams(vmem_limit_bytes=VMEM_LIMIT, **sem, **kw)


def _cspec(shape):
    nd = len(shape)
    return pl.BlockSpec(shape, lambda *_: (0,) * nd, pipeline_mode=pl.Buffered(1))


def _dot(a, b):
    return jnp.dot(a, b, preferred_element_type=F32)


def _dot_nt(a, b):
    return lax.dot_general(a, b, (((1,), (1,)), ((), ())), preferred_element_type=F32)


def _dot_tn(a, b):
    return lax.dot_general(a, b, (((0,), (0,)), ((), ())), preferred_element_type=F32)


def _bf(x):
    return x.astype(BF16)


def _rms(x):
    rs = lax.rsqrt(jnp.mean(x * x, axis=-1, keepdims=True) + EPS)
    return x * rs, rs


def _rms_bwd(du, n, rs, g):
    dn = du * g
    return rs * (dn - n * jnp.mean(dn * n, axis=-1, keepdims=True))


def _sigmoid(x):
    return 1.0 / (1.0 + jnp.exp(-x))


ROPE_BLK = 512


def _rope_parts():
    def cs(n, step):
        j = lax.broadcasted_iota(jnp.int32, (n, 128), 1) % 64
        pos = (lax.broadcasted_iota(jnp.int32, (n, 128), 0) * step).astype(F32)
        ang = pos * jnp.power(ROPE_THETA, -(j % 8).astype(F32) / 8.0)
        return jnp.cos(ang), jnp.sin(ang)

    return (*cs(ROPE_BLK, 1), *cs(S // ROPE_BLK, ROPE_BLK))


def _rope_fill(co_ref, so_ref, cb_ref, sb_ref, rc_ref, ra_ref, rb_ref):
    j = lax.broadcasted_iota(jnp.int32, (ROPE_BLK, 128), 1) % 64
    co, so = co_ref[...], so_ref[...]
    for t in range(S // ROPE_BLK):
        cb, sb = cb_ref[t:t + 1, :], sb_ref[t:t + 1, :]
        cos, sin = cb * co - sb * so, sb * co + cb * so
        rows = slice(t * ROPE_BLK, (t + 1) * ROPE_BLK)
        rc_ref[rows, :] = jnp.where(j < 16, cos, 1.0)
        ra_ref[rows, :] = jnp.where(j < 8, -sin, 0.0)
        rb_ref[rows, :] = jnp.where((j >= 8) & (j < 16), sin, 0.0)


def _rope(blk, c, a, b):
    return blk * c + pltpu.roll(blk, 120, 1) * a + pltpu.roll(blk, 8, 1) * b


def _rope_bwd(d, c, a, b):
    return d * c + pltpu.roll(d * a, 8, 1) + pltpu.roll(d * b, 120, 1)


def _unrope(t, c, a, b):
    return jnp.concatenate([_bf(_rope_bwd(t[:, j * 128:(j + 1) * 128].astype(F32), c, a, b))
                            for j in range(t.shape[1] // 128)], axis=1)


MESH = pl.DeviceIdType.MESH
ANY = pl.BlockSpec(memory_space=pl.ANY)
VM = pl.BlockSpec(memory_space=pltpu.VMEM)
FLIPS = [(dx, dy, dc) for dx in (0, 1) for dy in (0, 1) for dc in (0, 1)][1:]


def _place():
    return lax.axis_index("x"), lax.axis_index("y"), lax.axis_index("c")


def _dev_index(px, py, pc):
    return 4 * px + 2 * py + pc


def _gather_phases(ins, outs, bufs, send_sems=None, recv_sems=None, local_sems=None):
    nw = len(ins)
    if nw == 0:
        return (lambda: None,) * 3
    x, y, c = _place()
    me, sib = (x, y, c), (x, y, 1 - c)
    chips = [(1 - x, y), (x, 1 - y), (1 - x, 1 - y)]

    def copy(w, k, block, to, from_buf=False):
        dst = outs[w].at[_dev_index(*block)]
        return pltpu.make_async_remote_copy(
            src_ref=bufs[w] if from_buf else dst, dst_ref=dst, send_sem=send_sems.at[w, k],
            recv_sem=recv_sems.at[w, k], device_id=to, device_id_type=MESH)

    def mine(w):
        return pltpu.make_async_copy(bufs[w], outs[w].at[_dev_index(*me)], local_sems.at[w])

    def first(w):
        return [copy(w, 0, me, sib, True)] + [copy(w, 1 + j, me, (*chip, c), True) for j, chip in enumerate(chips)]

    def passed(w):
        return [copy(w, 4 + j, (*chip, c), sib) for j, chip in enumerate(chips)]

    def start():
        for w in range(nw):
            bufs[w][...] = ins[w][...].astype(bufs[w].dtype)
        for w in range(nw):
            mine(w).start()
            for cp in first(w):
                cp.start()

    def forward():
        for j, chip in enumerate(chips):
            for w in range(nw):
                copy(w, 1 + j, (*chip, c), me).wait_recv()
                passed(w)[j].start()

    def finish():
        for w in range(nw):
            copy(w, 0, sib, me).wait_recv()
        for j, chip in enumerate(chips):
            for w in range(nw):
                copy(w, 4 + j, (*chip, 1 - c), me).wait_recv()
        for w in range(nw):
            for cp in first(w) + passed(w):
                cp.wait_send()
            mine(w).wait()

    return start, forward, finish


def _gather_scratch(shards, dtypes):
    nw = len(shards)
    if nw == 0:
        return []
    return ([pltpu.VMEM(s.shape, dt) for s, dt in zip(shards, dtypes)]
            + [pltpu.SemaphoreType.DMA((nw, 7)), pltpu.SemaphoreType.DMA((nw, 7)), pltpu.SemaphoreType.DMA((nw,))])


def _gather_shapes(shards, dtypes):
    return [jax.ShapeDtypeStruct((NDEV, *s.shape), dt) for s, dt in zip(shards, dtypes)]


def _scatter_phases(ins, outs, send_sems=None, recv_sems=None, local_sems=None):
    nw = len(ins)
    if nw == 0:
        return (lambda: None,) * 2
    x, y, c = _place()
    me = _dev_index(x, y, c)

    def copies():
        out = []
        for w in range(nw):
            out.append(pltpu.make_async_copy(ins[w].at[me], outs[w].at[me], local_sems.at[w]))
            for k, (dx, dy, dc) in enumerate(FLIPS):
                peer = ((x + dx) % 2, (y + dy) % 2, (c + dc) % 2)
                out.append(pltpu.make_async_remote_copy(
                    src_ref=ins[w].at[_dev_index(*peer)], dst_ref=outs[w].at[me], send_sem=send_sems.at[w, k],
                    recv_sem=recv_sems.at[w, k], device_id=peer, device_id_type=MESH))
        return out

    def start():
        for cp in copies():
            cp.start()

    def finish():
        for cp in copies():
            cp.wait()

    return start, finish


def _scatter_scratch(nw):
    if nw == 0:
        return []
    return [pltpu.SemaphoreType.DMA((nw, 7)), pltpu.SemaphoreType.DMA((nw, 7)), pltpu.SemaphoreType.DMA((nw,))]


TM = 512


def _join_w_in(wg):
    sw = IN_W // NDEV

    def body(wg_ref, w_ref):
        for j in range(NDEV):
            w_ref[:, sw * j:sw * (j + 1)] = wg_ref[j]
        w_ref[:, IN_W:PW] = jnp.zeros((D, PW - IN_W), BF16)

    return pl.pallas_call(body, name="join_w_in", out_shape=jax.ShapeDtypeStruct((D, PW), BF16),
                          compiler_params=_params())(wg)


def _in_proj(x, g1, w, rc, ra, rb, shards, dtypes):
    tm = TM
    nw = len(shards)
    nt = S // tm

    def body(*refs):
        x_ref, g_ref, w_ref, rc_ref, ra_ref, rb_ref = refs[:6]
        ins = refs[6:6 + nw]
        qkv_ref, mqk_ref, mv_ref, mo_ref, gt_ref, u_ref = refs[6 + nw:12 + nw]
        outs = refs[12 + nw:12 + 2 * nw]
        bufs = refs[12 + 2 * nw:12 + 3 * nw]
        ag_start, ag_forward, ag_finish = _gather_phases(ins, outs, bufs, *refs[12 + 3 * nw:])
        i = pl.program_id(0)
        pl.when(i == 0)(ag_start)
        pl.when(i == nt - 1)(ag_forward)
        n, _ = _rms(x_ref[...])
        u = _bf(n * g_ref[...])
        u_ref[...] = u
        c, a, b = rc_ref[...], ra_ref[...], rb_ref[...]
        for half in range(2):
            blk = _dot(u, w_ref[:, half * 512:(half + 1) * 512])
            for t in range(4):
                lo = half * 512 + t * 128
                qkv_ref[:, lo:lo + 128] = _rope(blk[:, t * 128:(t + 1) * 128], c, a, b)
        qkv_ref[:, 1024:1536] = _dot(u, w_ref[:, 1024:1536])
        mqk_ref[:, 0:512] = _dot(u, w_ref[:, 1536:2048])
        mqk_ref[:, 512:1024] = _dot(u, w_ref[:, 2048:2560])
        mv_ref[...] = _dot(u, w_ref[:, 2560:3072])
        mo_ref[...] = _dot(u, w_ref[:, 3072:3584])
        gt_ref[...] = _dot(u, w_ref[:, 3584:3712])
        pl.when(i == nt - 1)(ag_finish)

    row = lambda wd: pl.BlockSpec((tm, wd), lambda i: (i, 0))
    res = pl.pallas_call(
        body, name="in_proj", grid=(nt,),
        in_specs=[row(D), _cspec((1, D)), _cspec((D, PW)), row(128), row(128), row(128)] + [VM] * nw,
        out_specs=[row(1536), row(1024), row(512), row(512), row(128), row(D)] + [ANY] * nw,
        out_shape=[jax.ShapeDtypeStruct((S, 1536), F32), jax.ShapeDtypeStruct((S, 1024), F32),
                   jax.ShapeDtypeStruct((S, 512), F32), jax.ShapeDtypeStruct((S, 512), F32),
                   jax.ShapeDtypeStruct((S, 128), F32), jax.ShapeDtypeStruct((S, D), BF16)]
        + _gather_shapes(shards, dtypes),
        scratch_shapes=_gather_scratch(shards, dtypes),
        compiler_params=_params(1),
    )(x, g1, w, rc, ra, rb, *shards)
    return res[:6], res[6:]


DILATIONS = (16, 4, 1)


def _attn_valid(n):
    kd = lax.broadcasted_iota(jnp.int32, (128, 256), 1) - lax.broadcasted_iota(jnp.int32, (128, 256), 0)
    off = jnp.where(n == 0, 0, 128)
    return (kd <= off) & (kd >= off - 128)


def _attn_rows(d, r, n):
    if d == 1:
        q0 = pl.multiple_of(n * 128, 128)
        k0 = pl.multiple_of(jnp.maximum(n - 1, 0) * 128, 128)
        return pl.ds(q0, 128), pl.ds(k0, 256), _attn_valid(n)
    q0 = r + n * 128 * d
    k0 = r + jnp.maximum(n - 1, 0) * 128 * d
    return pl.ds(q0, 128, stride=d), pl.ds(k0, 256, stride=d), _attn_valid(n)


ATTN_GROUP = 4
ATTN_ITERS = S // 128 // ATTN_GROUP


def _attn_group(d, i):
    nb = S // (128 * d)
    if nb == 2:
        qi = lax.broadcasted_iota(jnp.int32, (256, 256), 0) - lax.broadcasted_iota(jnp.int32, (256, 256), 1)
        whole = [pl.ds((ATTN_GROUP // 2) * i + u, 256, stride=d) for u in range(ATTN_GROUP // 2)]
        return [(rows, rows, (qi >= 0) & (qi <= 128)) for rows in whole]
    if d == 1:
        return [_attn_rows(1, 0, i + ATTN_ITERS * u) for u in range(ATTN_GROUP)]
    return [_attn_rows(d, (i // nb) * ATTN_GROUP + u, i % nb) for u in range(ATTN_GROUP)]


def _head0(shape):
    return lax.broadcasted_iota(jnp.int32, shape, 1) < 64


def _stack_heads(t):
    h0 = _head0(t.shape)
    tb = _bf(t)
    zero = jnp.zeros_like(tb)
    return jnp.concatenate([jnp.where(h0, tb, zero), jnp.where(h0, zero, tb)], axis=0)


def _attn_fwd(qkv, shards, dtypes):
    nw = len(shards)

    def body(*refs):
        q_ref, k_ref, v_ref = refs[:3]
        ins = refs[3:3 + nw]
        o_ref, lse0_ref, lse1_ref = refs[3 + nw:6 + nw]
        outs = refs[6 + nw:6 + 2 * nw]
        m0, m1, l0, l1, acc = refs[6 + 2 * nw:11 + 2 * nw]
        bufs = refs[11 + 2 * nw:11 + 3 * nw]
        ag_start, ag_forward, ag_finish = _gather_phases(ins, outs, bufs, *refs[11 + 3 * nw:])
        hp = pl.program_id(0)
        pl.when(hp == 0)(ag_start)
        pl.when(hp == 3)(ag_forward)
        stats = (m0, m1, l0, l1, acc)

        def update(blocks, first):
            loaded = [([q_ref[rq, :], k_ref[rk, :], v_ref[rk, :]], None if first else [ref[rq, :] for ref in stats])
                      for rq, rk, _ in blocks]
            both = lambda a, b: jnp.concatenate([a, b], axis=0)
            ss = [jnp.where(both(valid, valid), _dot_nt(_stack_heads(q * 0.125), _bf(k)), NEG)
                  for ((q, k, _), _), (_, _, valid) in zip(loaded, blocks)]
            mcs = [jnp.max(s, axis=-1, keepdims=True) for s in ss]
            if first:
                m2s = [jnp.broadcast_to(mc, (mc.shape[0], 128)) for mc in mcs]
            else:
                m2s = [jnp.maximum(both(prev[0], prev[1]), mc) for mc, (_, prev) in zip(mcs, loaded)]
            ps = [jnp.exp(s - jnp.tile(m2, (1, 2))) for s, m2 in zip(ss, m2s)]
            l2s = [jnp.sum(p, axis=-1, keepdims=True) for p in ps]
            acc2s = [_dot(_bf(p), _bf(v)) for p, ((_, _, v), _) in zip(ps, loaded)]
            results = []
            for m2, l2, acc2, (_, prev) in zip(m2s, l2s, acc2s, loaded):
                nq = m2.shape[0] // 2
                if first:
                    l2 = jnp.broadcast_to(l2, (2 * nq, 128))
                else:
                    alpha = jnp.exp(both(prev[0], prev[1]) - m2)
                    l2, acc2 = alpha * both(prev[2], prev[3]) + l2, alpha * both(prev[4], prev[4]) + acc2
                results.append((m2[0:nq], m2[nq:2 * nq], l2[0:nq], l2[nq:2 * nq],
                                jnp.where(_head0((nq, 128)), acc2[0:nq], acc2[nq:2 * nq])))
            for (rq, _, _), res in zip(blocks, results):
                for ref, val in zip(stats, res):
                    ref[rq, :] = val

        for d in DILATIONS:
            def step(i, carry, d=d):
                update(_attn_group(d, i), d == DILATIONS[0])
                return carry

            lax.fori_loop(0, ATTN_ITERS, step, 0)

        def fin(t, carry):
            rows = pl.ds(pl.multiple_of(t * 256, 256), 256)
            h0 = lax.broadcasted_iota(jnp.int32, (256, 128), 1) < 64
            la, lb = l0[rows, :], l1[rows, :]
            o_ref[rows, :] = acc[rows, :] / jnp.where(h0, la, lb)
            lse0_ref[rows, :] = m0[rows, :] + jnp.log(la)
            lse1_ref[rows, :] = m1[rows, :] + jnp.log(lb)
            return carry

        lax.fori_loop(0, S // 256, fin, 0)
        pl.when(hp == 3)(ag_finish)

    col = lambda off: pl.BlockSpec((S, 128), lambda h, off=off: (0, off + h))
    res = pl.pallas_call(
        body, name="attn_fwd", grid=(4,),
        in_specs=[col(0), col(4), col(8)] + [VM] * nw,
        out_specs=[col(0), col(0), col(0)] + [ANY] * nw,
        out_shape=[jax.ShapeDtypeStruct((S, AW), F32)] * 3 + _gather_shapes(shards, dtypes),
        scratch_shapes=[pltpu.VMEM((S, 128), F32)] * 5 + _gather_scratch(shards, dtypes),
        compiler_params=_params(1),
    )(qkv, qkv, qkv, *shards)
    return res[0], (res[1], res[2]), res[3:]


def _attn_bwd(qkv, o, lse, do, parts):
    nw = len(parts)

    def body(*refs):
        q_ref, k_ref, v_ref, o_ref, L0, L1, do_ref = refs[:7]
        ins = refs[7:7 + nw]
        dq_out, dk_out, dv_out = refs[7 + nw:10 + nw]
        outs = refs[10 + nw:10 + 2 * nw]
        D0, D1, dq_ref, dk_ref, dv_ref = refs[10 + 2 * nw:15 + 2 * nw]
        rs_start, rs_finish = _scatter_phases(ins, outs, *refs[15 + 2 * nw:])
        hp = pl.program_id(0)
        pl.when(hp == 0)(rs_start)

        def pre(t, carry):
            rows = pl.ds(pl.multiple_of(t * 256, 256), 256)
            h0 = lax.broadcasted_iota(jnp.int32, (256, 128), 1) < 64
            dd = do_ref[rows, :] * o_ref[rows, :]
            shp = (256, 128)
            D0[rows, :] = jnp.broadcast_to(jnp.sum(jnp.where(h0, dd, 0.0), axis=-1, keepdims=True), shp)
            D1[rows, :] = jnp.broadcast_to(jnp.sum(jnp.where(h0, 0.0, dd), axis=-1, keepdims=True), shp)
            return carry

        lax.fori_loop(0, S // 256, pre, 0)

        def update(blocks, first):
            loaded = [([q_ref[rq, :], k_ref[rk, :], v_ref[rk, :], do_ref[rq, :]],
                       [L0[rq, :], L1[rq, :], D0[rq, :], D1[rq, :]],
                       [0.0] * 3 if first else [dq_ref[rq, :], dk_ref[rk, :], dv_ref[rk, :]]) for rq, rk, _ in blocks]
            cat = lambda a, b: jnp.tile(jnp.concatenate([a, b], axis=0), (1, 2))
            ops = [(_stack_heads(q * 0.125), _stack_heads(q), _stack_heads(dout), _bf(k), _bf(v))
                   for (q, k, v, dout), _, _ in loaded]
            ss = [jnp.where(jnp.concatenate([valid, valid], axis=0), _dot_nt(qs, kb), NEG)
                  for (qs, _, _, kb, _), (_, _, valid) in zip(ops, blocks)]
            dps = [_dot_nt(do2, vb) for _, _, do2, _, vb in ops]
            ps = [jnp.exp(s - cat(st[0], st[1])) for s, (_, st, _) in zip(ss, loaded)]
            dss = [_bf(p * (dp - cat(st[2], st[3])) * 0.125) for p, dp, (_, st, _) in zip(ps, dps, loaded)]
            dq2s = [_dot(ds, kb) for ds, (_, _, _, kb, _) in zip(dss, ops)]
            dks = [_dot_tn(ds, q2) for ds, (_, q2, _, _, _) in zip(dss, ops)]
            dvs = [_dot_tn(_bf(p), do2) for p, (_, _, do2, _, _) in zip(ps, ops)]
            results = []
            for (_, _, (dq, dk, dv)), dq2, dkk, dvv in zip(loaded, dq2s, dks, dvs):
                nq = dq2.shape[0] // 2
                results.append((dq + jnp.where(_head0((nq, 128)), dq2[0:nq], dq2[nq:2 * nq]), dk + dkk, dv + dvv))
            for (rq, rk, _), (dq, dk, dv) in zip(blocks, results):
                dq_ref[rq, :] = dq
                dk_ref[rk, :] = dk
                dv_ref[rk, :] = dv

        assert S // (128 * DILATIONS[0]) == 2
        for d in DILATIONS:
            def step(i, carry, d=d):
                update(_attn_group(d, i), d == DILATIONS[0])
                return carry

            lax.fori_loop(0, ATTN_ITERS, step, 0)

        def fin(t, carry):
            rows = pl.ds(pl.multiple_of(t * 256, 256), 256)
            for src, dst in ((dq_ref, dq_out), (dk_ref, dk_out), (dv_ref, dv_out)):
                dst[rows, :] = _bf(src[rows, :])
            return carry

        lax.fori_loop(0, S // 256, fin, 0)
        pl.when(hp == 3)(rs_finish)

    col = lambda off: pl.BlockSpec((S, 128), lambda h, off=off: (0, off + h))
    res = pl.pallas_call(
        body, name="attn_bwd", grid=(4,),
        in_specs=[col(0), col(4), col(8), col(0), col(0), col(0), col(0)] + [ANY] * nw,
        out_specs=[col(0), col(0), col(0)] + [ANY] * nw,
        out_shape=[jax.ShapeDtypeStruct((S, AW), BF16)] * 3 + [jax.ShapeDtypeStruct(a.shape, a.dtype) for a in parts],
        scratch_shapes=[pltpu.VMEM((S, 128), F32)] * 5 + _scatter_scratch(nw),
        compiler_params=_params(1),
    )(qkv, qkv, qkv, o, lse[0], lse[1], do, *parts)
    return res[0], res[1], res[2], res[3:]


def _logsig(x):
    return jnp.minimum(x, 0.0) - jnp.log1p(jnp.exp(-jnp.abs(x)))


def _conv_taps(xp, n):
    return [xp[8:] if j == 3 else pltpu.roll(xp, 3 - j, 0)[8:] for j in range(4)]


def _conv_silu(xp, w_ref, b_ref, n):
    taps = _conv_taps(xp, n)
    c = b_ref[...] + sum(w_ref[j:j + 1, :] * taps[j] for j in range(4))
    sg = _sigmoid(c)
    return c, sg, taps


def _chunk_gates(G):
    assert LC == 128
    r = lax.broadcasted_iota(jnp.int32, (LC, LC), 0)
    c = lax.broadcasted_iota(jnp.int32, (LC, LC), 1)
    tril = (c <= r).astype(F32)
    triu = (c >= r).astype(F32)
    b_col = jnp.dot(tril, _logsig(G), preferred_element_type=F32, precision=HI)
    return b_col, b_col.T, G.T, tril, triu


def _colpick(X, lane):
    li = lax.broadcasted_iota(jnp.int32, X.shape, 1)
    return jnp.sum(jnp.where(li == lane, X, 0.0), axis=1, keepdims=True)


def _rowpick(XT, row):
    ri = lax.broadcasted_iota(jnp.int32, XT.shape, 0)
    return jnp.sum(jnp.where(ri == row, XT, 0.0), axis=0, keepdims=True)


def _each(f, *lists):
    return [f(*a) for a in zip(*lists)]


def _mlstm_heads(Q, K, V, G, b_col, b_row, g_row, C, N, M):
    hs = range(len(Q))
    bt = [_colpick(b_col, 4 + h) for h in hs]
    i_col = [_colpick(G, h) for h in hs]
    bs = [_rowpick(b_row, 4 + h) for h in hs]
    i_row = [_rowpick(g_row, h) for h in hs]
    r = lax.broadcasted_iota(jnp.int32, (LC, LC), 0)
    c = lax.broadcasted_iota(jnp.int32, (LC, LC), 1)
    lane = lax.broadcasted_iota(jnp.int32, (1, LC), 1)
    qb, kb, vb = [_bf(t) for t in Q], [_bf(t) for t in K], [_bf(t) for t in V]
    S_ = _each(_dot_nt, qb, kb)
    qC = _each(lambda q, ch: _dot(q, _bf(ch)), qb, C)
    log_d = _each(lambda a, b, i: jnp.where(c <= r, a - b + i, NEG), bt, bs, i_row)
    log_inter = _each(lambda a, m: a + m, bt, M)
    m_t = _each(lambda li, ld: jnp.maximum(li, jnp.max(ld, axis=1, keepdims=True)), log_inter, log_d)
    Dm = _each(lambda ld, m: jnp.exp(ld - m), log_d, m_t)
    g = _each(lambda li, m: jnp.exp(li - m), log_inter, m_t)
    Am = _each(lambda s, d: s * d, S_, Dm)
    AV = _each(lambda a, v: _dot(_bf(a), v), Am, vb)
    num = _each(lambda gg, qc, av: gg * qc + av, g, qC, AV)
    qn = _each(lambda q, n: jnp.sum(q * n, axis=1, keepdims=True), Q, N)
    den = _each(lambda gg, x, a: gg * x + jnp.sum(a, axis=1, keepdims=True), g, qn, Am)
    floor = [jnp.exp(-m) for m in m_t]
    inv_dd = _each(lambda d, f: 1.0 / jnp.maximum(jnp.abs(d), f), den, floor)
    hh = _each(lambda n, i: n * i, num, inv_dd)
    blast = [jnp.sum(jnp.where(lane == LC - 1, b, 0.0), axis=1, keepdims=True) for b in bs]
    log_s = _each(lambda bl, a, i: bl - a + i, blast, bt, i_col)
    m_new = _each(lambda bl, m, ls: jnp.maximum(bl + m, jnp.max(ls, axis=0, keepdims=True)), blast, M, log_s)
    decay = _each(lambda bl, m, mn: jnp.exp(bl + m - mn), blast, M, m_new)
    ws = _each(lambda ls, mn: jnp.exp(ls - mn), log_s, m_new)
    kw = _each(lambda k, w: k * w, K, ws)
    KV = _each(lambda k, v: _dot_tn(_bf(k), v), kw, vb)
    C_new = _each(lambda d, ch, kv: d * ch + kv, decay, C, KV)
    n_new = _each(lambda d, n, k: d * n + jnp.sum(k, axis=0, keepdims=True), decay, N, kw)
    return dict(Dm=Dm, g=g, Am=Am, qC=qC, qn=qn, den=den, floor=floor, inv_dd=inv_dd, h=hh, decay=decay, ws=ws, kw=kw,
                C_new=C_new, n_new=n_new, m_new=m_new, qb=qb, kb=kb, vb=vb)


def _head_out(hh, mo_h, gn_h):
    r = lax.rsqrt(jnp.mean(hh * hh, axis=-1, keepdims=True) + EPS)
    hn = hh * r
    sg = _sigmoid(mo_h)
    return sg * (hn * gn_h), hn, r, sg


def _mlstm_fwd(mqk, mv, mo, gates, conv_w, conv_b, gate_b, gn, shards, dtypes):
    nblk = S // TB
    ncb = TB // LC
    nw = len(shards)

    def body(*refs):
        x_ref, v_ref, o_ref, g_ref, w_ref, b_ref, gb_ref, gn_ref = refs[:8]
        ins = refs[8:8 + nw]
        out_ref, cs_ref, ns_ref, ms_ref = refs[8 + nw:12 + nw]
        outs = refs[12 + nw:12 + 2 * nw]
        tail, Cst, nst, mst, qs, ks = refs[12 + 2 * nw:18 + 2 * nw]
        bufs = refs[18 + 2 * nw:18 + 3 * nw]
        ag_start, ag_forward, ag_finish = _gather_phases(ins, outs, bufs, *refs[18 + 3 * nw:])
        i = pl.program_id(0)
        pl.when(i == 0)(ag_start)
        pl.when(i == nblk - 3)(ag_forward)

        @pl.when(i == 0)
        def _():
            tail[...] = jnp.zeros_like(tail)
            Cst[...] = jnp.zeros_like(Cst)
            nst[...] = jnp.zeros_like(nst)
            mst[...] = jnp.zeros_like(mst)

        x = x_ref[...]
        xp = jnp.concatenate([tail[...], x], axis=0)
        tail[...] = x[TB - 8:TB, :]
        c, sg, _ = _conv_silu(xp, w_ref, b_ref, TB)
        y = c * sg
        qs[...] = y[:, 0:MW]
        ks[...] = y[:, MW:2 * MW] * (1.0 / math.sqrt(128.0))

        for cc in range(ncb):
            rows = slice(cc * LC, (cc + 1) * LC)
            G = g_ref[rows, :] + gb_ref[...]
            b_col, b_row, g_row, _, _ = _chunk_gates(G)
            cs_ref[cc] = Cst[...]
            ns_ref[cc] = nst[...]
            ms_ref[cc] = mst[...]
            lns = [slice(h * 128, (h + 1) * 128) for h in range(4)]
            f = _mlstm_heads([qs[rows, ln] for ln in lns], [ks[rows, ln] for ln in lns], [v_ref[rows, ln] for ln in lns],
                             G, b_col, b_row, g_row, [Cst[:, ln] for ln in lns], [nst[0:1, ln] for ln in lns],
                             [jnp.max(mst[0:1, ln], axis=1, keepdims=True) for ln in lns])
            outs = [_head_out(hh, o_ref[rows, ln], gn_ref[:, ln])[0] for hh, ln in zip(f["h"], lns)]
            for h, ln in enumerate(lns):
                out_ref[rows, ln] = outs[h]
                Cst[:, ln] = f["C_new"][h]
                nst[0:1, ln] = f["n_new"][h]
                mst[0:1, ln] = jnp.broadcast_to(f["m_new"][h], (1, 128))
        pl.when(i == nblk - 1)(ag_finish)

    row = lambda wd: pl.BlockSpec((TB, wd), lambda i: (i, 0))
    res = pl.pallas_call(
        body, name="mlstm_fwd", grid=(nblk,),
        in_specs=[row(1024), row(MW), row(MW), row(128), _cspec((4, 1024)), _cspec((1, 1024)), _cspec((1, 128)),
                  _cspec((1, MW))] + [VM] * nw,
        out_specs=[row(MW), pl.BlockSpec((ncb, 128, MW), lambda i: (i, 0, 0)),
                   pl.BlockSpec((ncb, 8, MW), lambda i: (i, 0, 0)), pl.BlockSpec((ncb, 8, MW), lambda i: (i, 0, 0))]
        + [ANY] * nw,
        out_shape=[jax.ShapeDtypeStruct((S, MW), F32), jax.ShapeDtypeStruct((S // LC, 128, MW), F32),
                   jax.ShapeDtypeStruct((S // LC, 8, MW), F32), jax.ShapeDtypeStruct((S // LC, 8, MW), F32)]
        + _gather_shapes(shards, dtypes),
        scratch_shapes=[pltpu.VMEM((8, 1024), F32), pltpu.VMEM((128, MW), F32), pltpu.VMEM((8, MW), F32),
                        pltpu.VMEM((8, MW), F32), pltpu.VMEM((TB, MW), F32), pltpu.VMEM((TB, MW), F32)]
        + _gather_scratch(shards, dtypes),
        compiler_params=_params(1),
    )(mqk, mv, mo, gates, conv_w, conv_b, gate_b, gn, *shards)
    return res[0], res[1], res[2], res[3], res[4:]


DM_V, DM_O, DM_G, DM_W = 1024, 1536, 2048, PW - 3 * AW


def _mlstm_bwd(mqk, mv, mo, gates, conv_w, conv_b, gate_b, gn, cs, ns, ms, dout, parts):
    assert len(parts) == 1
    nblk = S // TB
    ncb = TB // LC
    kscale = 1.0 / math.sqrt(128.0)
    nw = len(parts)

    def body(*refs):
        x_ref, xprev_ref, v_ref, o_ref, g_ref, w_ref, b_ref, gb_ref, gn_ref, cs_ref, ns_ref, ms_ref, do_ref = refs[:13]
        ins = refs[13:13 + nw]
        dm_ref, dw_ref, db_ref, dgn_ref, dgb_ref = refs[13 + nw:18 + nw]
        outs = refs[18 + nw:18 + 2 * nw]
        dCst, dnst, dyhead, qs, ks, dqk = refs[18 + 2 * nw:24 + 2 * nw]
        rs_start, rs_middle, rs_finish = _scatter2_phases(ins[0], outs[0], *refs[24 + 2 * nw:])
        i = pl.program_id(0)
        blk = nblk - 1 - i
        pl.when(i == 0)(rs_start)
        pl.when(i == 2)(rs_middle)

        @pl.when(i == 0)
        def _():
            dCst[...] = jnp.zeros_like(dCst)
            dnst[...] = jnp.zeros_like(dnst)
            dyhead[...] = jnp.zeros_like(dyhead)
            dw_ref[...] = jnp.zeros_like(dw_ref)
            db_ref[...] = jnp.zeros_like(db_ref)
            dgn_ref[...] = jnp.zeros_like(dgn_ref)
            dgb_ref[...] = jnp.zeros_like(dgb_ref)

        x = x_ref[...]
        xprev = jnp.where(blk == 0, 0.0, xprev_ref[...])
        xp = jnp.concatenate([xprev, x], axis=0)
        c, sg, taps = _conv_silu(xp, w_ref, b_ref, TB)
        y = c * sg
        qs[...] = y[:, 0:MW]
        ks[...] = y[:, MW:2 * MW] * kscale
        lane128 = lax.broadcasted_iota(jnp.int32, (LC, 128), 1)
        rowi = lax.broadcasted_iota(jnp.int32, (LC, 1), 0)

        for cc in reversed(range(ncb)):
            rows = slice(cc * LC, (cc + 1) * LC)
            G = g_ref[rows, :] + gb_ref[...]
            b_col, b_row, g_row, _, triu = _chunk_gates(G)
            lns = [slice(h * 128, (h + 1) * 128) for h in range(4)]
            C = [cs_ref[cc, :, ln] for ln in lns]
            N = [ns_ref[cc, 0:1, ln] for ln in lns]
            Q, Kk = [qs[rows, ln] for ln in lns], [ks[rows, ln] for ln in lns]
            dCn, dnn = [dCst[:, ln] for ln in lns], [dnst[0:1, ln] for ln in lns]
            gns, dos, mos = [gn_ref[:, ln] for ln in lns], [do_ref[rows, ln] for ln in lns], [o_ref[rows, ln] for ln in lns]
            f = _mlstm_heads(Q, Kk, [v_ref[rows, ln] for ln in lns], G, b_col, b_row, g_row, C, N,
                             [jnp.max(ms_ref[cc, 0:1, ln], axis=1, keepdims=True) for ln in lns])
            hh, inv_dd, den, g, Am, Dm = f["h"], f["inv_dd"], f["den"], f["g"], f["Am"], f["Dm"]
            qb, kb, vb, ws, decay = f["qb"], f["kb"], f["vb"], f["ws"], f["decay"]
            ho = _each(_head_out, hh, mos, gns)
            hn, r, sgo = [t[1] for t in ho], [t[2] for t in ho], [t[3] for t in ho]
            dmo = _each(lambda d, n, gn_h, s: _bf(d * (n * gn_h) * s * (1.0 - s)), dos, hn, gns, sgo)
            dhm = _each(lambda d, s: d * s, dos, sgo)
            dgn = _each(lambda d, n: jnp.sum(d * n, axis=0, keepdims=True), dhm, hn)
            dhn = _each(lambda d, gn_h: d * gn_h, dhm, gns)
            dh = _each(lambda rr, d, n: rr * (d - n * jnp.mean(d * n, axis=-1, keepdims=True)), r, dhn, hn)
            dnum = _each(lambda d, i: d * i, dh, inv_dd)
            ddd = _each(lambda d, x, i: -jnp.sum(d * x, axis=1, keepdims=True) * i, dh, hh, inv_dd)
            dden = _each(lambda dn_, fl, d: jnp.where(jnp.abs(dn_) >= fl, d * jnp.sign(dn_), 0.0), den, f["floor"], ddd)
            dnb = [_bf(t) for t in dnum]
            gd = _each(lambda gg, d: _bf(gg * d), g, dnum)
            gq = _each(lambda gg, d: gg * d, g, dden)
            dCb = [_bf(t) for t in dCn]
            dA = _each(lambda d, v, dd_: _dot_nt(d, v) + dd_, dnb, vb, dden)
            dv1 = _each(lambda a, d: _dot_tn(_bf(a), d), Am, dnb)
            dq1 = _each(lambda d, ch: _dot_nt(d, _bf(ch)), gd, C)
            dC1 = _each(_dot_tn, qb, gd)
            E = _each(lambda v, d, n: _dot_nt(v, d) + n, vb, dCb, dnn)
            dv2 = _each(lambda k, d: _dot(_bf(k), d), f["kw"], dCb)
            dS = _each(lambda a, d: _bf(a * d), dA, Dm)
            dq2 = _each(_dot, dS, kb)
            dk1 = _each(_dot_tn, dS, qb)
            dq = _each(lambda a, x, n, b: a + x * n + b, dq1, gq, N, dq2)
            dC = _each(lambda d, x, y: d * x + y, decay, dCn, dC1)
            dn = _each(lambda d, x, y, q: d * x + jnp.sum(y * q, axis=0, keepdims=True), decay, dnn, gq, Q)
            dg = _each(lambda d, qc, dd_, x: jnp.sum(d * qc, axis=1, keepdims=True) + dd_ * x, dnum, f["qC"], dden, f["qn"])
            Gm = _each(lambda a, b: a * b, dA, Am)
            gam = _each(lambda a, b: a * b, dg, g)
            dk = _each(lambda a, w, e: (a + w * e) * kscale, dk1, ws, E)
            om = _each(lambda e, k, w: jnp.sum(e * k, axis=1, keepdims=True) * w, E, Kk, ws)
            dv = _each(lambda a, b: _bf(a + b), dv1, dv2)
            ddecay = _each(lambda d, ch, dn_, n: jnp.sum(jnp.sum(d * ch, axis=1, keepdims=True), axis=0, keepdims=True)
                           + jnp.sum(dn_ * n, axis=1, keepdims=True), dCn, C, dnn, N)
            rows_g = [jnp.sum(t, axis=1, keepdims=True) for t in Gm]
            cols_g = [jnp.broadcast_to(jnp.sum(t, axis=0, keepdims=True), (LC, 128)).T for t in Gm]
            last = _each(lambda o, dd_, d: jnp.where(rowi == LC - 1, jnp.sum(o, axis=0, keepdims=True) + dd_ * d, 0.0),
                         om, ddecay, decay)
            db = _each(lambda a, b, o, l, cg: a + b - o + l - cg, rows_g, gam, om, last, cols_g)
            di = _each(lambda cg, o: cg + o, cols_g, om)
            dB = jnp.zeros((LC, 128), F32)
            dI = jnp.zeros((LC, 128), F32)
            for h, ln in enumerate(lns):
                dB = jnp.where(lane128 == 4 + h, db[h], dB)
                dI = jnp.where(lane128 == h, di[h], dI)
                dgn_ref[:, ln] = dgn_ref[:, ln] + dgn[h]
                dCst[:, ln] = dC[h]
                dnst[0:1, ln] = dn[h]
                dqk[rows, ln] = dq[h]
                dqk[rows, MW + h * 128:MW + (h + 1) * 128] = dk[h]
                dm_ref[rows, DM_O + h * 128:DM_O + (h + 1) * 128] = dmo[h]
                dm_ref[rows, DM_V + h * 128:DM_V + (h + 1) * 128] = dv[h]
            dlogf = jnp.dot(triu, dB, preferred_element_type=F32, precision=HI)
            dG = dI + dlogf * _sigmoid(-G)
            dG = jnp.where(lane128 < 8, dG, 0.0)
            dm_ref[rows, DM_G:DM_G + 128] = _bf(dG)
            dm_ref[rows, DM_G + 128:DM_W] = jnp.zeros((LC, DM_W - DM_G - 128), BF16)
            dgb_ref[...] = dgb_ref[...] + jnp.sum(dG, axis=0, keepdims=True)

        dy = dqk[...] * (sg * (1.0 + c * (1.0 - sg)))
        db_ref[...] = db_ref[...] + jnp.sum(dy, axis=0, keepdims=True)
        for j in range(4):
            dw_ref[j:j + 1, :] = dw_ref[j:j + 1, :] + jnp.sum(dy * taps[j], axis=0, keepdims=True)
        dyp = jnp.concatenate([dy, dyhead[...]], axis=0)
        dx = w_ref[3:4, :] * dy
        for j in range(3):
            dx = dx + w_ref[j:j + 1, :] * pltpu.roll(dyp, TB + 8 - (3 - j), 0)[0:TB]
        dm_ref[:, 0:DM_V] = _bf(dx)
        dyhead[...] = dy[0:8, :]
        pl.when(i == nblk - 1)(rs_finish)

    rrow = lambda wd: pl.BlockSpec((TB, wd), lambda i: (nblk - 1 - i, 0))
    st = lambda r: pl.BlockSpec((ncb, r, MW), lambda i: (nblk - 1 - i, 0, 0))
    prev8 = pl.BlockSpec((8, 1024), lambda i: (jnp.maximum((nblk - 1 - i) * (TB // 8) - 1, 0), 0))
    res = pl.pallas_call(
        body, name="mlstm_bwd", grid=(nblk,),
        in_specs=[rrow(1024), prev8, rrow(MW), rrow(MW), rrow(128), _cspec((4, 1024)), _cspec((1, 1024)),
                  _cspec((1, 128)), _cspec((1, MW)), st(128), st(8), st(8), rrow(MW)] + [ANY] * nw,
        out_specs=[rrow(DM_W),
                   pl.BlockSpec((4, 1024), lambda i: (0, 0)), pl.BlockSpec((1, 1024), lambda i: (0, 0)),
                   pl.BlockSpec((1, MW), lambda i: (0, 0)), pl.BlockSpec((1, 128), lambda i: (0, 0))] + [ANY] * nw,
        out_shape=[jax.ShapeDtypeStruct((S, DM_W), BF16),
                   jax.ShapeDtypeStruct((4, 1024), F32), jax.ShapeDtypeStruct((1, 1024), F32),
                   jax.ShapeDtypeStruct((1, MW), F32), jax.ShapeDtypeStruct((1, 128), F32)]
        + [jax.ShapeDtypeStruct((len(CHIP_FLIPS), *a.shape[1:]), a.dtype) for a in parts],
        scratch_shapes=[pltpu.VMEM((128, MW), F32), pltpu.VMEM((8, MW), F32), pltpu.VMEM((8, 1024), F32),
                        pltpu.VMEM((TB, MW), F32), pltpu.VMEM((TB, MW), F32), pltpu.VMEM((TB, 1024), F32)]
        + _scatter2_scratch(parts[0].shape[1:], parts[0].dtype),
        compiler_params=_params(1),
    )(mqk, mqk, mv, mo, gates, conv_w, conv_b, gate_b, gn, cs, ns, ms, dout, *parts)
    return res[:5], res[5:]


def _out_proj(x, attn, ml, w, g):
    tm = TM

    def body(x_ref, a_ref, m_ref, w_ref, g_ref, h_ref, u_ref):
        h1 = x_ref[...] + _dot(_bf(a_ref[...]), w_ref[0:AW, :]) + _dot(_bf(m_ref[...]), w_ref[AW:D, :])
        h_ref[...] = h1
        n, _ = _rms(h1)
        u_ref[...] = _bf(n * g_ref[...])

    row = lambda wd: pl.BlockSpec((tm, wd), lambda i: (i, 0))
    return pl.pallas_call(
        body, name="out_proj", grid=(S // tm,),
        in_specs=[row(D), row(AW), row(MW), _cspec((D, D)), _cspec((1, D))],
        out_specs=[row(D), row(D)],
        out_shape=[jax.ShapeDtypeStruct((S, D), F32), jax.ShapeDtypeStruct((S, D), BF16)],
        compiler_params=_params(1),
    )(x, attn, ml, w, g)


HALF = DFF // NDEV // 2


def _mlp_fwd(h1, u2, w_up, w_down_a, w_down_b, shards, dtypes):
    tm = TM
    nt = S // tm
    nw = len(shards)

    def body(*refs):
        h_ref, u_ref, wu_ref, wa_ref, wb_ref = refs[:5]
        ins = refs[5:5 + nw]
        a_ref, o_ref = refs[5 + nw:7 + nw]
        outs = refs[7 + nw:7 + 2 * nw]
        bufs = refs[7 + 2 * nw:7 + 3 * nw]
        ag_start, ag_forward, ag_finish = _gather_phases(ins, outs, bufs, *refs[7 + 3 * nw:])
        i = pl.program_id(0)
        pl.when(i == 0)(ag_start)
        pl.when(i == nt - 2)(ag_forward)
        u = u_ref[...]
        acc = h_ref[...]
        for c in range(NDEV):
            cols = slice(c * 512, (c + 1) * 512)
            a = _dot(u, wu_ref[c])
            a_ref[:, cols] = _bf(a)
            r = jnp.maximum(a, 0.0)
            r = _bf(r * r)
            acc = acc + _dot(r[:, 0:HALF], wa_ref[c]) + _dot(r[:, HALF:2 * HALF], wb_ref[c])
        o_ref[...] = acc
        pl.when(i == nt - 1)(ag_finish)

    row = lambda wd: pl.BlockSpec((tm, wd), lambda i: (i, 0))
    res = pl.pallas_call(
        body, name="mlp_fwd", grid=(nt,),
        in_specs=[row(D), row(D), _cspec((NDEV, D, DFF // NDEV)), _cspec((NDEV, HALF, D)), _cspec((NDEV, HALF, D))]
        + [VM] * nw,
        out_specs=[row(DFF), row(D)] + [ANY] * nw,
        out_shape=[jax.ShapeDtypeStruct((S, DFF), BF16), jax.ShapeDtypeStruct((S, D), F32)]
        + _gather_shapes(shards, dtypes),
        scratch_shapes=_gather_scratch(shards, dtypes),
        compiler_params=_params(1),
    )(h1, u2, w_up, w_down_a, w_down_b, *shards)
    return res[0], res[1], res[2:]


def _ple_loss(h2, p, target, w_pg, w_ple, g_ple, g_fin):
    tm = TM

    def body(h_ref, p_ref, t_ref, wg_ref, wp_ref, gp_ref, gf_ref,
             dh_ref, dwg_ref, dwp_ref, dgp_ref, dgf_ref, loss_ref, acc_g, acc_p):
        i = pl.program_id(0)

        @pl.when(i == 0)
        def _():
            acc_g[...] = jnp.zeros_like(acc_g)
            acc_p[...] = jnp.zeros_like(acc_p)
            dgp_ref[...] = jnp.zeros_like(dgp_ref)
            dgf_ref[...] = jnp.zeros_like(dgf_ref)
            loss_ref[...] = jnp.zeros_like(loss_ref)

        h2v = h_ref[...]
        n2, rs2 = _rms(h2v)
        u3 = _bf(n2 * gp_ref[...])
        gt = _sigmoid(_dot(u3, wg_ref[...]))
        pb = _bf(p_ref[...])
        e = jnp.concatenate([_dot(pb, wp_ref[j]) for j in range(NDEV)], axis=1)
        h3 = h2v + gt * e
        n3, rs3 = _rms(h3)
        err = n3 * gf_ref[...] - t_ref[...]
        loss_ref[...] = loss_ref[...] + 0.5 / D * jnp.sum(jnp.sum(err * err, axis=1, keepdims=True), axis=0, keepdims=True)
        dy = err * (1.0 / D)
        dgf_ref[...] = dgf_ref[...] + jnp.sum(dy * n3, axis=0, keepdims=True)
        dh3 = _rms_bwd(dy, n3, rs3, gf_ref[...])
        de = _bf(dh3 * gt)
        dz = _bf(dh3 * e * gt * (1.0 - gt))
        acc_p[...] = acc_p[...] + _dot_tn(pb, de)
        acc_g[...] = acc_g[...] + _dot_tn(u3, dz)
        du3 = _dot_nt(dz, wg_ref[...])
        dgp_ref[...] = dgp_ref[...] + jnp.sum(du3 * n2, axis=0, keepdims=True)
        dh_ref[...] = dh3 + _rms_bwd(du3, n2, rs2, gp_ref[...])

        @pl.when(i == S // tm - 1)
        def _():
            dwg_ref[...] = _bf(acc_g[...])
            for j in range(NDEV):
                dwp_ref[j] = _bf(acc_p[:, j * 128:(j + 1) * 128])

    row = lambda wd: pl.BlockSpec((tm, wd), lambda i: (i, 0))
    whole = lambda shp: pl.BlockSpec(shp, lambda i: (0,) * len(shp))
    return pl.pallas_call(
        body, name="ple_loss", grid=(S // tm,),
        in_specs=[row(D), row(PLE), row(D), _cspec((D, D)), _cspec((NDEV, PLE, 128)), _cspec((1, D)), _cspec((1, D))],
        out_specs=[row(D), whole((D, D)), whole((NDEV, PLE, 128)), whole((1, D)), whole((1, D)), whole((1, 1))],
        out_shape=[jax.ShapeDtypeStruct((S, D), F32), jax.ShapeDtypeStruct((D, D), BF16),
                   jax.ShapeDtypeStruct((NDEV, PLE, 128), BF16), jax.ShapeDtypeStruct((1, D), F32),
                   jax.ShapeDtypeStruct((1, D), F32), jax.ShapeDtypeStruct((1, 1), F32)],
        scratch_shapes=[pltpu.VMEM((D, D), F32), pltpu.VMEM((PLE, D), F32)],
        compiler_params=_params(1),
    )(h2, p, target, w_pg, w_ple, g_ple, g_fin)


def _mlp_bwd(dh2, a, h1, g, w_up, w_down_a, w_down_b, parts):
    tm = TM
    nt = S // tm
    nw = len(parts)

    def body(*refs):
        d_ref, a_ref, h_ref, g_ref, wu_ref, wa_ref, wb_ref = refs[:7]
        ins = refs[7:7 + nw]
        da_ref, dh1_ref, dg_ref = refs[7 + nw:10 + nw]
        outs = refs[10 + nw:10 + 2 * nw]
        rs_start, rs_finish = _scatter_phases(ins, outs, *refs[10 + 2 * nw:])
        i = pl.program_id(0)
        pl.when(i == 0)(rs_start)

        @pl.when(i == 0)
        def _():
            dg_ref[...] = jnp.zeros_like(dg_ref)

        dh2v = d_ref[...]
        db = _bf(dh2v)
        du = jnp.zeros((tm, D), F32)
        for c in range(NDEV):
            cols = slice(c * 512, (c + 1) * 512)
            dr = jnp.concatenate([_dot_nt(db, wa_ref[c]), _dot_nt(db, wb_ref[c])], axis=1)
            da = _bf(dr * (2.0 * jnp.maximum(a_ref[:, cols], 0.0)))
            da_ref[:, cols] = da
            du = du + _dot_nt(da, wu_ref[c])
        n, rs = _rms(h_ref[...])
        dg_ref[...] = dg_ref[...] + jnp.sum(du * n, axis=0, keepdims=True)
        dh1_ref[...] = dh2v + _rms_bwd(du, n, rs, g_ref[...])
        pl.when(i == nt - 1)(rs_finish)

    row = lambda wd: pl.BlockSpec((tm, wd), lambda i: (i, 0))
    res = pl.pallas_call(
        body, name="mlp_bwd", grid=(nt,),
        in_specs=[row(D), row(DFF), row(D), _cspec((1, D)), _cspec((NDEV, D, DFF // NDEV)), _cspec((NDEV, HALF, D)),
                  _cspec((NDEV, HALF, D))] + [ANY] * nw,
        out_specs=[row(DFF), row(D), pl.BlockSpec((1, D), lambda i: (0, 0))] + [ANY] * nw,
        out_shape=[jax.ShapeDtypeStruct((S, DFF), BF16), jax.ShapeDtypeStruct((S, D), F32),
                   jax.ShapeDtypeStruct((1, D), F32)] + [jax.ShapeDtypeStruct(p.shape, p.dtype) for p in parts],
        scratch_shapes=_scatter_scratch(nw),
        compiler_params=_params(1),
    )(dh2, a, h1, g, w_up, w_down_a, w_down_b, *parts)
    return res[0], res[1], res[2], res[3:]


def _out_proj_bwd(dh1, attn, ml, w):
    tm = TM

    def body(d_ref, a_ref, m_ref, w_ref, da_ref, dm_ref, dw_ref, acc):
        i = pl.program_id(0)

        @pl.when(i == 0)
        def _():
            acc[...] = jnp.zeros_like(acc)

        db = _bf(d_ref[...])
        dmix = _dot_nt(db, w_ref[...])
        da_ref[...] = dmix[:, 0:AW]
        dm_ref[...] = dmix[:, AW:D]
        acc[0:AW, :] = acc[0:AW, :] + _dot_tn(_bf(a_ref[...]), db)
        acc[AW:D, :] = acc[AW:D, :] + _dot_tn(_bf(m_ref[...]), db)

        @pl.when(i == S // tm - 1)
        def _():
            dw_ref[...] = _bf(acc[...])

    row = lambda wd: pl.BlockSpec((tm, wd), lambda i: (i, 0))
    return pl.pallas_call(
        body, name="out_proj_bwd", grid=(S // tm,),
        in_specs=[row(D), row(AW), row(MW), _cspec((D, D))],
        out_specs=[row(AW), row(MW), pl.BlockSpec((D, D), lambda i: (0, 0))],
        out_shape=[jax.ShapeDtypeStruct((S, AW), F32), jax.ShapeDtypeStruct((S, MW), F32),
                   jax.ShapeDtypeStruct((D, D), BF16)],
        scratch_shapes=[pltpu.VMEM((D, D), F32)],
        compiler_params=_params(1),
    )(dh1, attn, ml, w)


CHIP_FLIPS = [(0, 0), (0, 1), (1, 0), (1, 1)]


def _scatter2_phases(in_ref, out_ref, mine_v, sib_v, psum_v, loc_sems, d2d_send, d2d_recv, ici_send, ici_recv, own_sem):
    x, y, c = _place()
    chips = [((x + dx) % 2, (y + dy) % 2) for dx, dy in CHIP_FLIPS]
    nc = len(chips)

    def local(k):
        return pltpu.make_async_copy(in_ref.at[_dev_index(*chips[k], c)], mine_v.at[k], loc_sems.at[k])

    def to_sib(k):
        return pltpu.make_async_remote_copy(
            src_ref=in_ref.at[_dev_index(*chips[k], 1 - c)], dst_ref=sib_v.at[k], send_sem=d2d_send.at[k],
            recv_sem=d2d_recv.at[k], device_id=(x, y, 1 - c), device_id_type=MESH)

    def over_ici(k):
        return pltpu.make_async_remote_copy(
            src_ref=psum_v.at[k], dst_ref=out_ref.at[k], send_sem=ici_send.at[k - 1], recv_sem=ici_recv.at[k - 1],
            device_id=(*chips[k], c), device_id_type=MESH)

    def own():
        return pltpu.make_async_copy(psum_v.at[0], out_ref.at[0], own_sem)

    def start():
        for k in range(nc):
            to_sib(k).start()
            local(k).start()

    def middle():
        for k in (3, 1, 2, 0):
            local(k).wait()
            to_sib(k).wait_recv()
            psum_v[k] = _bf(mine_v[k].astype(F32) + sib_v[k].astype(F32))
            (over_ici(k) if k else own()).start()

    def finish():
        for k in range(1, nc):
            over_ici(k).wait()
        for k in range(nc):
            to_sib(k).wait_send()
        own().wait()

    return start, middle, finish


def _scatter2_scratch(shard, dtype):
    nc = len(CHIP_FLIPS)
    return ([pltpu.VMEM((nc, *shard), dtype)] * 3
            + [pltpu.SemaphoreType.DMA((nc,))] * 3 + [pltpu.SemaphoreType.DMA((nc - 1,))] * 2 + [pltpu.SemaphoreType.DMA])


def _in_proj_bwd(dparts, n_roped, rope, dh1, x, g1, w, part):
    tm = TM
    nt = S // tm
    widths = [d.shape[1] for d in dparts]
    assert sum(widths) == PW
    npar = len(dparts)

    def body(*refs):
        d_refs = refs[:npar]
        tabs = [t[...] for t in refs[npar:npar + 3]]
        dh_ref, x_ref, g_ref, w_ref, in_ref, dx_ref, dgsum_ref, out_ref = refs[npar + 3:npar + 11]
        rs_start, rs_middle, rs_finish = _scatter2_phases(in_ref, out_ref, *refs[npar + 11:npar + 20])
        dg_ref = refs[npar + 20]
        ar_start, ar_finish = _small_phases([dg_ref], dgsum_ref, *refs[npar + 21:])
        i = pl.program_id(0)
        pl.when(i == 0)(rs_start)
        pl.when(i == 1)(rs_middle)

        @pl.when(i == 0)
        def _():
            dg_ref[...] = jnp.zeros_like(dg_ref)

        du = jnp.zeros((tm, D), F32)
        off = 0
        for j, (d_ref, wd) in enumerate(zip(d_refs, widths)):
            nc = next(c for c in (768, 512) if wd % c == 0)
            for s in range(wd // nc):
                d = d_ref[:, s * nc:(s + 1) * nc]
                du = du + _dot_nt(_unrope(d, *tabs) if j < n_roped else d, w_ref[:, off + s * nc:off + (s + 1) * nc])
            off += wd
        n, rs = _rms(x_ref[...])
        dg_ref[...] = dg_ref[...] + jnp.sum(du * n, axis=0, keepdims=True)
        dx_ref[...] = dh_ref[...] + _rms_bwd(du, n, rs, g_ref[...])

        @pl.when(i == nt - 1)
        def _():
            ar_start()
            rs_finish()
            ar_finish()

    row = lambda wd: pl.BlockSpec((tm, wd), lambda i: (i, 0))
    shard = part.shape[1:]
    return pl.pallas_call(
        body, name="in_proj_bwd", grid=(nt,),
        in_specs=[row(wd) for wd in widths] + [row(128)] * 3 + [row(D), row(D), _cspec((1, D)), _cspec((D, PW)), ANY],
        out_specs=[row(D), VM, ANY],
        out_shape=[jax.ShapeDtypeStruct((S, D), F32), jax.ShapeDtypeStruct((8, 1024), F32),
                   jax.ShapeDtypeStruct((len(CHIP_FLIPS), *shard), part.dtype)],
        scratch_shapes=_scatter2_scratch(shard, part.dtype)
        + [pltpu.VMEM((1, D), F32), pltpu.VMEM((8, 1024), F32), pltpu.VMEM((NDEV, 8, 1024), F32),
           pltpu.SemaphoreType.DMA((7,)), pltpu.SemaphoreType.DMA((7,))],
        compiler_params=_params(1),
    )(*dparts, *rope, dh1, x, g1, w, part)


SMALL_ROWS = 96


def _small_phases(ins, out_ref, pack, rbuf, send_sems, recv_sems):
    x, y, c = _place()
    me = _dev_index(x, y, c)

    def copies():
        out = []
        for k, (dx, dy, dc) in enumerate(FLIPS):
            peer = ((x + dx) % 2, (y + dy) % 2, (c + dc) % 2)
            out.append(pltpu.make_async_remote_copy(
                src_ref=pack, dst_ref=rbuf.at[me], send_sem=send_sems.at[k], recv_sem=recv_sems.at[k],
                device_id=peer, device_id_type=MESH))
        return out

    def start():
        pack[...] = jnp.zeros_like(pack)
        for i, ref in enumerate(ins):
            pack[8 * i:8 * i + 1, 0:ref.shape[1]] = ref[...]
        rbuf[me] = pack[...]
        for cp in copies():
            cp.start()

    def finish():
        for cp in copies():
            cp.wait()
        tot = rbuf[0]
        for j in range(1, NDEV):
            tot = tot + rbuf[j]
        out_ref[...] = tot

    return start, finish


def _wgrad(name, A, Bs, a_fn, b_fn, out_shape, split=None, ts=512, small=(), rope=(), n_roped=0):
    K = A.shape[1]
    widths = [b.shape[1] for b in Bs]
    N = sum(widths)
    nb, ns, nrt = len(Bs) + len(rope), len(small), S // ts
    kc = min(K, 1024)

    def body(*refs):
        a_ref, b_refs = refs[0], refs[1:1 + len(Bs)]
        tabs = [t[...] for t in refs[1 + len(Bs):1 + nb]]
        o_ref = refs[1 + nb + ns]
        acc = refs[2 + nb + ns + bool(ns)]
        r = pl.program_id(0)
        if ns:
            sm_start, sm_finish = _small_phases(refs[1 + nb:1 + nb + ns], refs[2 + nb + ns], *refs[4 + nb + ns:])
            pl.when(r == 0)(sm_start)

        @pl.when(r == 0)
        def _():
            acc[...] = jnp.zeros_like(acc)

        bs, off = [], 0
        for i, (b_ref, w) in enumerate(zip(b_refs, widths)):
            nc = next(c for c in (1024, 768, 512) if w % c == 0)
            fn = (lambda t: _unrope(t, *tabs)) if i < n_roped else b_fn
            bs += [(off + c * nc, nc, fn(b_ref[:, c * nc:(c + 1) * nc])) for c in range(w // nc)]
            off += w
        for kk in range(K // kc):
            rows = slice(kk * kc, (kk + 1) * kc)
            at = a_fn(a_ref[:, rows]).T
            for lo, nc, b in bs:
                acc[rows, lo:lo + nc] = acc[rows, lo:lo + nc] + _dot(at, b)

        @pl.when(r == nrt - 1)
        def _():
            if split is None:
                o_ref[...] = _bf(acc[...])
            else:
                for j in range(NDEV):
                    o_ref[j] = _bf(acc[:, split * j:split * (j + 1)])

        if ns:
            pl.when(r == nrt - 1)(sm_finish)

    in_specs = ([pl.BlockSpec((ts, K), lambda r: (r, 0))] + [pl.BlockSpec((ts, w), lambda r: (r, 0)) for w in widths]
                + [pl.BlockSpec((ts, 128), lambda r: (r, 0))] * len(rope))
    out_spec = pl.BlockSpec(out_shape, lambda r: (0,) * len(out_shape))
    scratch = [pltpu.VMEM((K, N), F32)]
    if not ns:
        return pl.pallas_call(
            body, name=name, grid=(nrt,), in_specs=in_specs, out_specs=out_spec,
            out_shape=jax.ShapeDtypeStruct(out_shape, BF16), scratch_shapes=scratch, compiler_params=_params(1),
        )(A, *Bs, *rope)
    return pl.pallas_call(
        body, name=name, grid=(nrt,), in_specs=in_specs + [VM] * ns, out_specs=[out_spec, VM],
        out_shape=[jax.ShapeDtypeStruct(out_shape, BF16), jax.ShapeDtypeStruct((SMALL_ROWS, 1024), F32)],
        scratch_shapes=scratch + [pltpu.VMEM((SMALL_ROWS, 1024), F32), pltpu.VMEM((NDEV, SMALL_ROWS, 1024), F32),
                                  pltpu.SemaphoreType.DMA((7,)), pltpu.SemaphoreType.DMA((7,))],
        compiler_params=_params(1),
    )(A, *Bs, *rope, *small)


def _relu2_bf(a):
    r = jnp.maximum(a.astype(F32), 0.0)
    return _bf(r * r)


def _ident(a):
    return a


def _step(x, p, target, g1, conv_b, gate_b, gn, g_mlp, g_ple, g_fin, sh):
    (g_in, g_conv), (rc, ra, rb) = _gather_weights([sh["w_in"], sh["conv_w"]], [BF16, F32])
    conv_w = g_conv.transpose(1, 0, 2).reshape(4, 1024)
    w_in_p = _join_w_in(g_in)
    (qkv, mqk, mv, mo, gates, u1), (w_down_a,) = _in_proj(x, g1, w_in_p, rc, ra, rb, [sh["w_down"][0:HALF]], [BF16])
    attn, lse, (w_up8, w_out8) = _attn_fwd(qkv, [sh["w_up"], sh["w_out"]], [BF16] * 2)
    ml, cs, ns, ms, (w_down_b,) = _mlstm_fwd(mqk, mv, mo, gates, conv_w, conv_b, gate_b, gn,
                                             [sh["w_down"][HALF:2 * HALF]], [BF16])
    w_out = w_out8.reshape(D, D)
    h1, u2 = _out_proj(x, attn, ml, w_out, g_mlp)
    a, h2, (w_pg8, w_ple8) = _mlp_fwd(h1, u2, w_up8, w_down_a, w_down_b, [sh["w_ple_gate"], sh["w_ple"]], [BF16] * 2)
    w_pg = w_pg8.reshape(D, D)
    dh2, dw_pg, dw_ple8, dg_ple, dg_fin, loss = _ple_loss(h2, p, target, w_pg, w_ple8, g_ple, g_fin)
    da, dh1, dg_mlp, (r_pg, r_ple) = _mlp_bwd(dh2, a, h1, g_mlp, w_up8, w_down_a, w_down_b,
                                              [dw_pg.reshape(NDEV, D // NDEV, D), dw_ple8])
    dw_up8 = _wgrad("wgrad_up", u2, [da], _ident, _ident, (NDEV, D, DFF // NDEV), split=DFF // NDEV)
    dw_down = _wgrad("wgrad_down", a, [dh2], _relu2_bf, _bf, (DFF, D))
    d_attn, d_ml, dw_out = _out_proj_bwd(dh1, attn, ml, w_out)
    (dm, dconv_w, dconv_b, dgn, dgate_b), (r_down,) = _mlstm_bwd(
        mqk, mv, mo, gates, conv_w, conv_b, gate_b, gn, cs, ns, ms, d_ml, [dw_down.reshape(NDEV, DFF // NDEV, D)])
    dq, dk, dv, (r_up, r_out) = _attn_bwd(qkv, attn, lse, d_attn, [dw_up8, dw_out.reshape(NDEV, D // NDEV, D)])
    dparts = [dq, dk, dv, dm]
    small = [jnp.zeros((1, D), F32), dconv_b, dgate_b, dgn, dg_mlp, dg_ple, dg_fin, loss]
    dw_in8, total = _wgrad("wgrad_in", u1, dparts, _ident, _ident, (NDEV, D, IN_W // NDEV), split=IN_W // NDEV,
                           small=small + [dconv_w[j:j + 1] for j in range(4)], rope=(rc, ra, rb), n_roped=2)
    dx, dg1_sum, r_in = _in_proj_bwd(dparts, 2, (rc, ra, rb), dh1, x, g1, w_in_p, dw_in8)
    recv = dict(w_in=r_in, w_out=r_out, w_up=r_up, w_down=r_down, w_ple_gate=r_pg, w_ple=r_ple)
    return dx, recv, total, dg1_sum


def _gather_weights(shards, dtypes):
    nw = len(shards)

    def body(*refs):
        ins, parts = refs[:nw], refs[nw:nw + 4]
        outs, tables = refs[nw + 4:2 * nw + 4], refs[2 * nw + 4:2 * nw + 7]
        start, forward, finish = _gather_phases(ins, outs, refs[2 * nw + 7:3 * nw + 7], *refs[3 * nw + 7:])
        start()
        _rope_fill(*parts, *tables)
        forward()
        finish()

    res = pl.pallas_call(
        body, name="gather_weights",
        in_specs=[VM] * (nw + 4), out_specs=[ANY] * nw + [VM] * 3,
        out_shape=_gather_shapes(shards, dtypes) + [jax.ShapeDtypeStruct((S, 128), F32)] * 3,
        scratch_shapes=_gather_scratch(shards, dtypes),
        compiler_params=_params(),
    )(*shards, *_rope_parts())
    return res[:nw], res[nw:]


ADAM_STEPS = 8


def _adamw(items):
    n = len(items)

    def body(*refs):
        for i in range(n):
            g_ref, w_ref, m_ref, v_ref = refs[4 * i:4 * i + 4]
            go_ref, d_ref, mo_ref, vo_ref = refs[4 * n + 4 * i:4 * n + 4 * i + 4]
            g = g_ref[0].astype(F32)
            for j in range(1, g_ref.shape[0]):
                g = g + g_ref[j].astype(F32)
            go_ref[...] = g
            d_ref[...], mo_ref[...], vo_ref[...] = _adam_update(g, w_ref[...], m_ref[...], v_ref[...])

    in_specs, out_specs, out_shape, args = [], [], [], []
    for gparts, w, m, v in items:
        P, R, C = gparts.shape
        if R % (8 * ADAM_STEPS) == 0:
            tr = R // ADAM_STEPS
            row, gspec = pl.BlockSpec((tr, C), lambda i: (i, 0)), pl.BlockSpec((P, tr, C), lambda i: (0, i, 0))
        else:
            row, gspec = pl.BlockSpec((R, C), lambda i: (0, 0)), pl.BlockSpec((P, R, C), lambda i: (0, 0, 0))
        in_specs += [gspec, row, row, row]
        out_specs += [row] * 4
        out_shape += [jax.ShapeDtypeStruct((R, C), F32)] * 4
        args += [gparts, w, m, v]
    res = pl.pallas_call(
        body, name="adamw", grid=(ADAM_STEPS,), in_specs=in_specs, out_specs=out_specs, out_shape=out_shape,
        compiler_params=_params(1),
    )(*args)
    return [res[4 * i:4 * i + 4] for i in range(n)]


SMALL = ("norm_mix_g", "conv_b", "gate_b", "mlstm_norm_g", "norm_mlp_g", "norm_ple_g", "final_norm_g")


def _adam_update(g, w, m, v):
    c1 = 1.0 - ADAM_B1 ** ADAM_STEP
    c2 = 1.0 - ADAM_B2 ** ADAM_STEP
    m2 = ADAM_B1 * m + (1.0 - ADAM_B1) * g
    v2 = ADAM_B2 * v + (1.0 - ADAM_B2) * (g * g)
    return -ADAM_LR * ((m2 / c1) / (jnp.sqrt(v2 / c2) + ADAM_EPS) + ADAM_WD * w), m2, v2


def _adamw_small(total, first, ws, ms, vs):
    n = len(ws)

    def body(*refs):
        t_ref, f_ref = refs[:2]
        refs = refs[1:]
        outs = refs[1 + 3 * n:]
        for i in range(n):
            w_ref, m_ref, v_ref = refs[1 + i], refs[1 + n + i], refs[1 + 2 * n + i]
            g = (t_ref if i else f_ref)[8 * i:8 * i + 1, 0:w_ref.shape[1]]
            delta, m2, v2 = _adam_update(g, w_ref[...], m_ref[...], v_ref[...])
            for ref, val in zip(outs[4 * i:4 * i + 4], (g, delta, m2, v2)):
                ref[...] = val

    res = pl.pallas_call(
        body, name="adamw_small",
        out_shape=[jax.ShapeDtypeStruct(w.shape, F32) for w in ws for _ in range(4)],
        compiler_params=_params(),
    )(total, first, *ws, *ms, *vs)
    return [res[4 * i:4 * i + 4] for i in range(n)]


def kernel(x, p, norm_mix_g, w_in, conv_w, conv_b, gate_b, mlstm_norm_g, w_out, norm_mlp_g, w_up, w_down, norm_ple_g, w_ple_gate, w_ple, final_norm_g, loss_target, m_norm_mix_g, m_w_in, m_conv_w, m_conv_b, m_gate_b, m_mlstm_norm_g, m_w_out, m_norm_mlp_g, m_w_up, m_w_down, m_norm_ple_g, m_w_ple_gate, m_w_ple, m_final_norm_g, v_norm_mix_g, v_w_in, v_conv_w, v_conv_b, v_gate_b, v_mlstm_norm_g, v_w_out, v_norm_mlp_g, v_w_up, v_w_down, v_norm_ple_g, v_w_ple_gate, v_w_ple, v_final_norm_g):
    big_names = ("w_in", "conv_w", "w_out", "w_up", "w_down", "w_ple_gate", "w_ple")
    wts = dict(w_in=w_in, conv_w=conv_w, w_out=w_out, w_up=w_up, w_down=w_down, w_ple_gate=w_ple_gate, w_ple=w_ple)
    mom = dict(w_in=m_w_in, conv_w=m_conv_w, w_out=m_w_out, w_up=m_w_up, w_down=m_w_down, w_ple_gate=m_w_ple_gate,
               w_ple=m_w_ple)
    var = dict(w_in=v_w_in, conv_w=v_conv_w, w_out=v_w_out, w_up=v_w_up, w_down=v_w_down, w_ple_gate=v_w_ple_gate,
               w_ple=v_w_ple)
    sq = lambda a: a.reshape(a.shape[1:])
    fin = final_norm_g.reshape(1, D)
    dx, recv, total, first = _step(
        x[0], p[0, 0], loss_target[0], norm_mix_g, conv_b, jnp.pad(gate_b, ((0, 0), (0, 120))), mlstm_norm_g,
        norm_mlp_g, norm_ple_g, fin, {n: sq(wts[n]) for n in big_names})

    nrow = 8 * len(SMALL)
    me = _dev_index(*_place())
    conv_rows = total[nrow + 8:nrow + 40:8]
    recv["conv_w"] = lax.dynamic_slice_in_dim(conv_rows, me * 128, 128, axis=1).reshape(1, 4, 128)
    out = {}
    for n, res in zip(big_names, _adamw([(recv[n], sq(wts[n]), sq(mom[n]), sq(var[n])) for n in big_names])):
        out[n] = [t.reshape(wts[n].shape) for t in res]
    sw = dict(norm_mix_g=norm_mix_g, conv_b=conv_b, gate_b=gate_b, mlstm_norm_g=mlstm_norm_g, norm_mlp_g=norm_mlp_g,
              norm_ple_g=norm_ple_g, final_norm_g=fin)
    sm = dict(norm_mix_g=m_norm_mix_g, conv_b=m_conv_b, gate_b=m_gate_b, mlstm_norm_g=m_mlstm_norm_g,
              norm_mlp_g=m_norm_mlp_g, norm_ple_g=m_norm_ple_g, final_norm_g=m_final_norm_g.reshape(1, D))
    sv = dict(norm_mix_g=v_norm_mix_g, conv_b=v_conv_b, gate_b=v_gate_b, mlstm_norm_g=v_mlstm_norm_g,
              norm_mlp_g=v_norm_mlp_g, norm_ple_g=v_norm_ple_g, final_norm_g=v_final_norm_g.reshape(1, D))
    res = _adamw_small(total, first, [sw[n] for n in SMALL], [sm[n] for n in SMALL], [sv[n] for n in SMALL])
    for n, r in zip(SMALL, res):
        out[n] = [t.reshape(final_norm_g.shape) for t in r] if n == "final_norm_g" else list(r)
    order = ("norm_mix_g", "w_in", "conv_w", "conv_b", "gate_b", "mlstm_norm_g", "w_out", "norm_mlp_g", "w_up", "w_down",
             "norm_ple_g", "w_ple_gate", "w_ple", "final_norm_g")
    loss_all = total[nrow, 0]
    return (loss_all, dx[None], *[out[n][0] for n in order], *[out[n][1] for n in order],
            *[out[n][2] for n in order], *[out[n][3] for n in order])
```
